```python
import math
import jax, jax.numpy as jnp
from jax import lax
import numpy as np

D_MODEL = 2048
BATCH = 8
SEQ = 8192
DEPTH = 2

D_MIX = D_MODEL
BR = D_MIX // 4
CONV_A_WIDTH = 3
ATT_HEADS = 8
ATT_HEAD_DIM = BR // ATT_HEADS
DILATIONS = ((128, 1), (512, 4), (2048, 16))
BLK = 128
REL_BUCKETS = 32
REL_MAX_DIST = 2048
LRU_HEADS = 8
LRU_HEAD_DIM = BR // LRU_HEADS
CONV_C_WIDTH = 4
LRU_C = 8.0
S5_CH = 16
S5_GROUPS = BR // S5_CH
S5_STATE = 64
N_IN = 4 * BR + 4 * BR + 2 * BR + 2 * BR
ALPHA = (2 * DEPTH) ** 0.25
BETA = (8 * DEPTH) ** -0.25
LN_EPS = 1e-5

kernel_name = "hybrid_parallel_conv_dilattn_rglru_s5"


def layer_norm(x, g, b):
    xf = x.astype(jnp.float32)
    mu = jnp.mean(xf, axis=-1, keepdims=True)
    var = jnp.mean(jnp.square(xf - mu), axis=-1, keepdims=True)
    return ((xf - mu) * lax.rsqrt(var + LN_EPS)).astype(x.dtype) * g + b


def causal_dwconv(x, w):
    K = w.shape[0]
    S = x.shape[1]
    xp = jnp.pad(x, ((0, 0), (K - 1, 0), (0, 0)))
    y = xp[:, :S] * w[0]
    for j in range(1, K):
        y = y + xp[:, j:j + S] * w[j]
    return y


def t5_bucket(dist):
    max_exact = REL_BUCKETS // 2
    nf = jnp.maximum(dist, 1).astype(jnp.float32)
    large = max_exact + (jnp.log(nf / max_exact) / math.log(REL_MAX_DIST / max_exact)
                         * (REL_BUCKETS - max_exact)).astype(jnp.int32)
    large = jnp.minimum(large, REL_BUCKETS - 1)
    return jnp.where(dist < max_exact, dist, large)


def dilated_group(q, k, v, rel_bias, window, dil):
    Bsz, S, H, hd = q.shape
    span = window // dil
    assert span <= BLK
    unit = dil * BLK
    S_pad = -(-S // unit) * unit
    nb = S_pad // unit
    pad = ((0, 0), (0, S_pad - S), (0, 0), (0, 0))

    def split(t):
        return jnp.pad(t, pad).reshape(Bsz, nb, BLK, dil, H, hd)

    def with_prev(t):
        prev = jnp.pad(t, ((0, 0), (1, 0), (0, 0), (0, 0), (0, 0), (0, 0)))[:, :-1]
        return jnp.concatenate([prev, t], axis=2)

    qb = split(q)
    kk = with_prev(split(k))
    vv = with_prev(split(v))
    s = jnp.einsum('bnirhd,bnjrhd->bnrhij', qb, kk).astype(jnp.float32)

    i = jnp.arange(BLK)[:, None]
    j = jnp.arange(2 * BLK)[None, :]
    delta = i + BLK - j
    bucket = t5_bucket(jnp.clip(delta, 0, span) * dil)
    bias = jnp.transpose(rel_bias[bucket], (2, 0, 1)).astype(jnp.float32)
    valid = (delta >= 0) & (delta <= span)
    has_prev = (jnp.arange(nb)[:, None, None] > 0) | (j >= BLK)[None]
    mask = valid[None] & has_prev

    s = jnp.where(mask[None, :, None, None], s + bias, -1e30)
    m = jnp.max(s, axis=-1, keepdims=True)
    p = jnp.exp(s - m)
    l = jnp.sum(p, axis=-1, keepdims=True)
    o = jnp.einsum('bnrhij,bnjrhd->bnirhd', (p / l).astype(v.dtype), vv)
    lse = (m + jnp.log(l))[..., 0]
    lse = jnp.transpose(lse, (0, 1, 4, 2, 3)).reshape(Bsz, S_pad, H)[:, :S]
    o = o.reshape(Bsz, S_pad, H, hd)[:, :S]
    return o, lse


def dilated_attention(q, k, v, rel_bias):
    outs, lses = [], []
    for window, dil in DILATIONS:
        o, lse = dilated_group(q, k, v, rel_bias, window, dil)
        outs.append(o)
        lses.append(lse)
    w = jax.nn.softmax(jnp.stack(lses, axis=0), axis=0)
    return jnp.einsum('gbsh,gbshd->bshd', w.astype(q.dtype), jnp.stack(outs, axis=0))


def linear_combine(e1, e2):
    a1, b1 = e1
    a2, b2 = e2
    return a1 * a2, a2 * b1 + b2


def complex_combine(e1, e2):
    ar1, ai1, br1, bi1 = e1
    ar2, ai2, br2, bi2 = e2
    return (ar2 * ar1 - ai2 * ai1,
            ar2 * ai1 + ai2 * ar1,
            ar2 * br1 - ai2 * bi1 + br2,
            ar2 * bi1 + ai2 * br1 + bi2)


def rglru_branch(xb, conv_w, conv_b, wa, ba, wx, bx, lam):
    Bsz, S, _ = xb.shape
    xc = causal_dwconv(xb, conv_w) + conv_b
    xh = xc.reshape(Bsz, S, LRU_HEADS, LRU_HEAD_DIM)
    r = jax.nn.sigmoid(jnp.einsum('bshi,hij->bshj', xh, wa).reshape(Bsz, S, BR) + ba)
    ig = jax.nn.sigmoid(jnp.einsum('bshi,hij->bshj', xh, wx).reshape(Bsz, S, BR) + bx)
    log_a = -LRU_C * r * jax.nn.softplus(-lam)
    a = jnp.exp(log_a)
    mult = jnp.sqrt(-jnp.expm1(2.0 * log_a))
    _, h = lax.associative_scan(linear_combine, (a, mult * ig * xc), axis=1)
    return h


def s5_branch(u, lam_re, lam_im, log_dt, b_re, b_im, c_re, c_im, d_skip, w_glu, b_glu):
    Bsz, S, _ = u.shape
    ug = u.reshape(Bsz, S, S5_GROUPS, S5_CH)
    dt = jnp.exp(log_dt)[:, None]
    mag = jnp.exp(lam_re * dt)
    ab_re = mag * jnp.cos(lam_im * dt)
    ab_im = mag * jnp.sin(lam_im * dt)
    den = lam_re * lam_re + lam_im * lam_im
    f_re = ((ab_re - 1.0) * lam_re + ab_im * lam_im) / den
    f_im = (ab_im * lam_re - (ab_re - 1.0) * lam_im) / den
    bb_re = f_re[..., None] * b_re - f_im[..., None] * b_im
    bb_im = f_re[..., None] * b_im + f_im[..., None] * b_re
    bu_re = jnp.einsum('bsgc,gpc->bsgp', ug, bb_re)
    bu_im = jnp.einsum('bsgc,gpc->bsgp', ug, bb_im)
    a_re = jnp.broadcast_to(ab_re, bu_re.shape)
    a_im = jnp.broadcast_to(ab_im, bu_im.shape)
    _, _, x_re, x_im = lax.associative_scan(complex_combine, (a_re, a_im, bu_re, bu_im), axis=1)
    y = jnp.einsum('gcp,bsgp->bsgc', c_re, x_re) - jnp.einsum('gcp,bsgp->bsgc', c_im, x_im)
    y = y.reshape(Bsz, S, BR) + d_skip * u
    y = jax.nn.gelu(y)
    return y * jax.nn.sigmoid(y @ w_glu + b_glu)


def _fwd_setup_inputs(seed: int = 0) -> dict:
    key = jax.random.key(seed)
    ks = jax.random.split(key, 32)

    def nrm(k, shape, scale):
        return scale * jax.random.normal(k, shape, jnp.float32)

    HD = LRU_HEAD_DIM
    a_c = jax.random.uniform(ks[13], (DEPTH, BR), jnp.float32, minval=0.9, maxval=0.999)
    a0 = a_c ** (1.0 / LRU_C)
    lru_lambda = jnp.log(a0) - jnp.log1p(-a0)
    n_idx = jnp.arange(S5_STATE, dtype=jnp.float32)
    return {
        "x": nrm(ks[0], (BATCH, SEQ, D_MODEL), 1.0),
        "c": nrm(ks[1], (BATCH, D_MODEL), 1.0),
        "rel_bias": nrm(ks[2], (REL_BUCKETS, ATT_HEADS), 0.5),
        "w_ada": nrm(ks[3], (DEPTH, D_MODEL, 3 * D_MODEL), 0.1 * D_MODEL ** -0.5),
        "b_ada": nrm(ks[4], (DEPTH, 3 * D_MODEL), 0.01),
        "w_in": nrm(ks[5], (DEPTH, D_MODEL, N_IN), D_MODEL ** -0.5),
        "conv_a": nrm(ks[6], (DEPTH, CONV_A_WIDTH, BR), CONV_A_WIDTH ** -0.5),
        "conv_c": nrm(ks[7], (DEPTH, CONV_C_WIDTH, BR), CONV_C_WIDTH ** -0.5),
        "conv_c_b": nrm(ks[8], (DEPTH, BR), 0.01),
        "lru_wa": nrm(ks[9], (DEPTH, LRU_HEADS, HD, HD), HD ** -0.5),
        "lru_ba": nrm(ks[10], (DEPTH, BR), 0.01),
        "lru_wx": nrm(ks[11], (DEPTH, LRU_HEADS, HD, HD), HD ** -0.5),
        "lru_bx": nrm(ks[12], (DEPTH, BR), 0.01),
        "lru_lambda": lru_lambda,
        "s5_lam_re": -0.5 + nrm(ks[14], (DEPTH, S5_GROUPS, S5_STATE), 0.01),
        "s5_lam_im": jnp.pi * n_idx + nrm(ks[15], (DEPTH, S5_GROUPS, S5_STATE), 0.01),
        "s5_log_dt": jax.random.uniform(ks[16], (DEPTH, S5_GROUPS), jnp.float32,
                                        minval=math.log(1e-3), maxval=math.log(1e-1)),
        "s5_b_re": nrm(ks[17], (DEPTH, S5_GROUPS, S5_STATE, S5_CH), (2 * S5_CH) ** -0.5),
        "s5_b_im": nrm(ks[18], (DEPTH, S5_GROUPS, S5_STATE, S5_CH), (2 * S5_CH) ** -0.5),
        "s5_c_re": nrm(ks[19], (DEPTH, S5_GROUPS, S5_CH, S5_STATE), (2 * S5_STATE) ** -0.5),
        "s5_c_im": nrm(ks[20], (DEPTH, S5_GROUPS, S5_CH, S5_STATE), (2 * S5_STATE) ** -0.5),
        "s5_d": nrm(ks[21], (DEPTH, BR), 1.0),
        "s5_w_glu": nrm(ks[22], (DEPTH, BR, BR), BR ** -0.5),
        "s5_b_glu": nrm(ks[23], (DEPTH, BR), 0.01),
        "w_out": nrm(ks[24], (DEPTH, D_MIX, D_MODEL), BETA * D_MIX ** -0.5),
        "ln_g": 1.0 + nrm(ks[25], (DEPTH, D_MODEL), 0.01),
        "ln_b": nrm(ks[26], (DEPTH, D_MODEL), 0.01),
    }


def _fwd_reference(x, c, rel_bias, w_ada, b_ada, w_in, conv_a, conv_c, conv_c_b, lru_wa, lru_ba,
              lru_wx, lru_bx, lru_lambda, s5_lam_re, s5_lam_im, s5_log_dt, s5_b_re, s5_b_im,
              s5_c_re, s5_c_im, s5_d, s5_w_glu, s5_b_glu, w_out, ln_g, ln_b):
    Bsz, S, _ = x.shape
    cond = jax.nn.silu(c)
    for l in range(DEPTH):
        ada = cond @ w_ada[l] + b_ada[l]
        shift, scale, gate = jnp.split(ada, 3, axis=-1)
        h = x * (1.0 + scale[:, None]) + shift[:, None]
        proj = h @ w_in[l]
        pa, pb, pc, pd = jnp.split(proj, [4 * BR, 8 * BR, 10 * BR], axis=-1)

        a_b, a_c, a_x, a_g = jnp.split(pa, 4, axis=-1)
        y_a = a_b * causal_dwconv(a_c * a_x, conv_a[l]) * jax.nn.silu(a_g)

        q, k, v, b_g = jnp.split(pb, 4, axis=-1)
        q = q.reshape(Bsz, S, ATT_HEADS, ATT_HEAD_DIM) * (ATT_HEAD_DIM ** -0.5)
        k = k.reshape(Bsz, S, ATT_HEADS, ATT_HEAD_DIM)
        v = v.reshape(Bsz, S, ATT_HEADS, ATT_HEAD_DIM)
        y_b = dilated_attention(q, k, v, rel_bias).reshape(Bsz, S, BR) * jax.nn.silu(b_g)

        c_x, c_g = jnp.split(pc, 2, axis=-1)
        y_c = rglru_branch(c_x, conv_c[l], conv_c_b[l], lru_wa[l], lru_ba[l], lru_wx[l],
                           lru_bx[l], lru_lambda[l]) * jax.nn.silu(c_g)

        d_u, d_g = jnp.split(pd, 2, axis=-1)
        y_d = s5_branch(d_u, s5_lam_re[l], s5_lam_im[l], s5_log_dt[l], s5_b_re[l], s5_b_im[l],
                        s5_c_re[l], s5_c_im[l], s5_d[l], s5_w_glu[l], s5_b_glu[l]) * jax.nn.silu(d_g)

        y = jnp.concatenate([y_a, y_b, y_c, y_d], axis=-1) @ w_out[l]
        x = layer_norm(ALPHA * x + (1.0 + gate[:, None]) * y, ln_g[l], ln_b[l])
    return x


import jax as _jax
import jax.numpy as _jnp

TWIN_FORMAT = 'train_step'
FWD_PARAMS = ['x', 'c', 'rel_bias', 'w_ada', 'b_ada', 'w_in', 'conv_a', 'conv_c', 'conv_c_b', 'lru_wa', 'lru_ba', 'lru_wx', 'lru_bx', 'lru_lambda', 's5_lam_re', 's5_lam_im', 's5_log_dt', 's5_b_re', 's5_b_im', 's5_c_re', 's5_c_im', 's5_d', 's5_w_glu', 's5_b_glu', 'w_out', 'ln_g', 'ln_b']
TWIN_WEIGHTS = ['rel_bias', 'w_ada', 'b_ada', 'w_in', 'conv_a', 'conv_c', 'conv_c_b', 'lru_wa', 'lru_ba', 'lru_wx', 'lru_bx', 'lru_lambda', 's5_lam_re', 's5_lam_im', 's5_log_dt', 's5_b_re', 's5_b_im', 's5_c_re', 's5_c_im', 's5_d', 's5_w_glu', 's5_b_glu', 'w_out', 'ln_g', 'ln_b']
TWIN_DIFF_INPUT = 'x'
TWIN_INPUTS = ['x', 'c', 'rel_bias', 'w_ada', 'b_ada', 'w_in', 'conv_a', 'conv_c', 'conv_c_b', 'lru_wa', 'lru_ba', 'lru_wx', 'lru_bx', 'lru_lambda', 's5_lam_re', 's5_lam_im', 's5_log_dt', 's5_b_re', 's5_b_im', 's5_c_re', 's5_c_im', 's5_d', 's5_w_glu', 's5_b_glu', 'w_out', 'ln_g', 'ln_b', 'loss_target', 'm_rel_bias', 'm_w_ada', 'm_b_ada', 'm_w_in', 'm_conv_a', 'm_conv_c', 'm_conv_c_b', 'm_lru_wa', 'm_lru_ba', 'm_lru_wx', 'm_lru_bx', 'm_lru_lambda', 'm_s5_lam_re', 'm_s5_lam_im', 'm_s5_log_dt', 'm_s5_b_re', 'm_s5_b_im', 'm_s5_c_re', 'm_s5_c_im', 'm_s5_d', 'm_s5_w_glu', 'm_s5_b_glu', 'm_w_out', 'm_ln_g', 'm_ln_b', 'v_rel_bias', 'v_w_ada', 'v_b_ada', 'v_w_in', 'v_conv_a', 'v_conv_c', 'v_conv_c_b', 'v_lru_wa', 'v_lru_ba', 'v_lru_wx', 'v_lru_bx', 'v_lru_lambda', 'v_s5_lam_re', 'v_s5_lam_im', 'v_s5_log_dt', 'v_s5_b_re', 'v_s5_b_im', 'v_s5_c_re', 'v_s5_c_im', 'v_s5_d', 'v_s5_w_glu', 'v_s5_b_glu', 'v_w_out', 'v_ln_g', 'v_ln_b']
TWIN_OUTPUTS = ['loss', 'grad_x', 'grad_rel_bias', 'grad_w_ada', 'grad_b_ada', 'grad_w_in', 'grad_conv_a', 'grad_conv_c', 'grad_conv_c_b', 'grad_lru_wa', 'grad_lru_ba', 'grad_lru_wx', 'grad_lru_bx', 'grad_lru_lambda', 'grad_s5_lam_re', 'grad_s5_lam_im', 'grad_s5_log_dt', 'grad_s5_b_re', 'grad_s5_b_im', 'grad_s5_c_re', 'grad_s5_c_im', 'grad_s5_d', 'grad_s5_w_glu', 'grad_s5_b_glu', 'grad_w_out', 'grad_ln_g', 'grad_ln_b', 'delta_rel_bias', 'delta_w_ada', 'delta_b_ada', 'delta_w_in', 'delta_conv_a', 'delta_conv_c', 'delta_conv_c_b', 'delta_lru_wa', 'delta_lru_ba', 'delta_lru_wx', 'delta_lru_bx', 'delta_lru_lambda', 'delta_s5_lam_re', 'delta_s5_lam_im', 'delta_s5_log_dt', 'delta_s5_b_re', 'delta_s5_b_im', 'delta_s5_c_re', 'delta_s5_c_im', 'delta_s5_d', 'delta_s5_w_glu', 'delta_s5_b_glu', 'delta_w_out', 'delta_ln_g', 'delta_ln_b', 'new_m_rel_bias', 'new_m_w_ada', 'new_m_b_ada', 'new_m_w_in', 'new_m_conv_a', 'new_m_conv_c', 'new_m_conv_c_b', 'new_m_lru_wa', 'new_m_lru_ba', 'new_m_lru_wx', 'new_m_lru_bx', 'new_m_lru_lambda', 'new_m_s5_lam_re', 'new_m_s5_lam_im', 'new_m_s5_log_dt', 'new_m_s5_b_re', 'new_m_s5_b_im', 'new_m_s5_c_re', 'new_m_s5_c_im', 'new_m_s5_d', 'new_m_s5_w_glu', 'new_m_s5_b_glu', 'new_m_w_out', 'new_m_ln_g', 'new_m_ln_b', 'new_v_rel_bias', 'new_v_w_ada', 'new_v_b_ada', 'new_v_w_in', 'new_v_conv_a', 'new_v_conv_c', 'new_v_conv_c_b', 'new_v_lru_wa', 'new_v_lru_ba', 'new_v_lru_wx', 'new_v_lru_bx', 'new_v_lru_lambda', 'new_v_s5_lam_re', 'new_v_s5_lam_im', 'new_v_s5_log_dt', 'new_v_s5_b_re', 'new_v_s5_b_im', 'new_v_s5_c_re', 'new_v_s5_c_im', 'new_v_s5_d', 'new_v_s5_w_glu', 'new_v_s5_b_glu', 'new_v_w_out', 'new_v_ln_g', 'new_v_ln_b']
TWIN_LEAF_KINDS = {'loss': 'loss', 'grad_x': 'grad_x', 'grad_rel_bias': 'grad_w', 'grad_w_ada': 'grad_w', 'grad_b_ada': 'grad_w', 'grad_w_in': 'grad_w', 'grad_conv_a': 'grad_w', 'grad_conv_c': 'grad_w', 'grad_conv_c_b': 'grad_w', 'grad_lru_wa': 'grad_w', 'grad_lru_ba': 'grad_w', 'grad_lru_wx': 'grad_w', 'grad_lru_bx': 'grad_w', 'grad_lru_lambda': 'grad_w', 'grad_s5_lam_re': 'grad_w', 'grad_s5_lam_im': 'grad_w', 'grad_s5_log_dt': 'grad_w', 'grad_s5_b_re': 'grad_w', 'grad_s5_b_im': 'grad_w', 'grad_s5_c_re': 'grad_w', 'grad_s5_c_im': 'grad_w', 'grad_s5_d': 'grad_w', 'grad_s5_w_glu': 'grad_w', 'grad_s5_b_glu': 'grad_w', 'grad_w_out': 'grad_w', 'grad_ln_g': 'grad_w', 'grad_ln_b': 'grad_w', 'delta_rel_bias': 'delta_w', 'delta_w_ada': 'delta_w', 'delta_b_ada': 'delta_w', 'delta_w_in': 'delta_w', 'delta_conv_a': 'delta_w', 'delta_conv_c': 'delta_w', 'delta_conv_c_b': 'delta_w', 'delta_lru_wa': 'delta_w', 'delta_lru_ba': 'delta_w', 'delta_lru_wx': 'delta_w', 'delta_lru_bx': 'delta_w', 'delta_lru_lambda': 'delta_w', 'delta_s5_lam_re': 'delta_w', 'delta_s5_lam_im': 'delta_w', 'delta_s5_log_dt': 'delta_w', 'delta_s5_b_re': 'delta_w', 'delta_s5_b_im': 'delta_w', 'delta_s5_c_re': 'delta_w', 'delta_s5_c_im': 'delta_w', 'delta_s5_d': 'delta_w', 'delta_s5_w_glu': 'delta_w', 'delta_s5_b_glu': 'delta_w', 'delta_w_out': 'delta_w', 'delta_ln_g': 'delta_w', 'delta_ln_b': 'delta_w', 'new_m_rel_bias': 'new_m', 'new_m_w_ada': 'new_m', 'new_m_b_ada': 'new_m', 'new_m_w_in': 'new_m', 'new_m_conv_a': 'new_m', 'new_m_conv_c': 'new_m', 'new_m_conv_c_b': 'new_m', 'new_m_lru_wa': 'new_m', 'new_m_lru_ba': 'new_m', 'new_m_lru_wx': 'new_m', 'new_m_lru_bx': 'new_m', 'new_m_lru_lambda': 'new_m', 'new_m_s5_lam_re': 'new_m', 'new_m_s5_lam_im': 'new_m', 'new_m_s5_log_dt': 'new_m', 'new_m_s5_b_re': 'new_m', 'new_m_s5_b_im': 'new_m', 'new_m_s5_c_re': 'new_m', 'new_m_s5_c_im': 'new_m', 'new_m_s5_d': 'new_m', 'new_m_s5_w_glu': 'new_m', 'new_m_s5_b_glu': 'new_m', 'new_m_w_out': 'new_m', 'new_m_ln_g': 'new_m', 'new_m_ln_b': 'new_m', 'new_v_rel_bias': 'new_v', 'new_v_w_ada': 'new_v', 'new_v_b_ada': 'new_v', 'new_v_w_in': 'new_v', 'new_v_conv_a': 'new_v', 'new_v_conv_c': 'new_v', 'new_v_conv_c_b': 'new_v', 'new_v_lru_wa': 'new_v', 'new_v_lru_ba': 'new_v', 'new_v_lru_wx': 'new_v', 'new_v_lru_bx': 'new_v', 'new_v_lru_lambda': 'new_v', 'new_v_s5_lam_re': 'new_v', 'new_v_s5_lam_im': 'new_v', 'new_v_s5_log_dt': 'new_v', 'new_v_s5_b_re': 'new_v', 'new_v_s5_b_im': 'new_v', 'new_v_s5_c_re': 'new_v', 'new_v_s5_c_im': 'new_v', 'new_v_s5_d': 'new_v', 'new_v_s5_w_glu': 'new_v', 'new_v_s5_b_glu': 'new_v', 'new_v_w_out': 'new_v', 'new_v_ln_g': 'new_v', 'new_v_ln_b': 'new_v'}


def _forward(args):
    return _fwd_reference(*[args[k] for k in FWD_PARAMS])


def _output_shape():
    def fwd():
        inp = _fwd_setup_inputs(0)
        return _fwd_reference(*[inp[k] for k in FWD_PARAMS])
    out = _jax.eval_shape(fwd)
    return out.shape, out.dtype

N_MICROBATCH = 1
ADAM_LR = 0.001
ADAM_B1 = 0.9
ADAM_B2 = 0.999
ADAM_EPS = 1e-08
ADAM_WD = 0.01
ADAM_STEP = 10
PER_EXAMPLE_BATCH_AXIS = {'x': 0, 'c': 0, 'loss_target': 0}
SHARED_INPUTS = []
_WEIGHT_DTYPES = {'rel_bias': _jnp.float32, 'w_ada': _jnp.float32, 'b_ada': _jnp.float32, 'w_in': _jnp.float32, 'conv_a': _jnp.float32, 'conv_c': _jnp.float32, 'conv_c_b': _jnp.float32, 'lru_wa': _jnp.float32, 'lru_ba': _jnp.float32, 'lru_wx': _jnp.float32, 'lru_bx': _jnp.float32, 'lru_lambda': _jnp.float32, 's5_lam_re': _jnp.float32, 's5_lam_im': _jnp.float32, 's5_log_dt': _jnp.float32, 's5_b_re': _jnp.float32, 's5_b_im': _jnp.float32, 's5_c_re': _jnp.float32, 's5_c_im': _jnp.float32, 's5_d': _jnp.float32, 's5_w_glu': _jnp.float32, 's5_b_glu': _jnp.float32, 'w_out': _jnp.float32, 'ln_g': _jnp.float32, 'ln_b': _jnp.float32}
MOMENT_SCALE = {'rel_bias': 8.693832e-03, 'w_ada': 3.808781e-02, 'b_ada': 6.736344e-02, 'w_in': 1.836760e-02, 'conv_a': 2.698432e-02, 'conv_c': 2.574919e-02, 'conv_c_b': 2.603644e-01, 'lru_wa': 1.065101e-02, 'lru_ba': 6.817195e-03, 'lru_wx': 1.941799e-02, 'lru_bx': 9.008001e-03, 'lru_lambda': 1.283553e-02, 's5_lam_re': 4.890123e-04, 's5_lam_im': 5.565730e-04, 's5_log_dt': 3.535234e-01, 's5_b_re': 3.421474e-04, 's5_b_im': 3.426844e-04, 's5_c_re': 6.910302e-04, 's5_c_im': 6.796573e-04, 's5_d': 1.006699e-02, 's5_w_glu': 2.840522e-03, 's5_b_glu': 4.210620e-03, 'w_out': 3.738553e-02, 'ln_g': 2.260534e+01, 'ln_b': 5.124808e-01}


def _to_microbatches(a, axis):
    t = _jnp.moveaxis(a, axis, 0)
    t = t.reshape((N_MICROBATCH, t.shape[0] // N_MICROBATCH) + t.shape[1:])
    return _jnp.moveaxis(t, 1, axis + 1)


def setup_inputs(seed: int = 0) -> dict:
    inp = _fwd_setup_inputs(seed)
    key = _jax.random.fold_in(_jax.random.key(seed), 7919)
    shape, _ = _output_shape()
    out = dict(inp)
    out["loss_target"] = _jax.random.normal(_jax.random.fold_in(key, 0), shape, _jnp.float32)
    for i, name in enumerate(TWIN_WEIGHTS):
        w = inp[name].astype(_jnp.float32)
        if MOMENT_SCALE is None:
            s = _jnp.sqrt(_jnp.mean(_jnp.square(w)) + 1e-30)
        else:
            s = MOMENT_SCALE[name]
        km, kv = _jax.random.split(_jax.random.fold_in(key, i + 1))
        out[name] = w
        out["m_" + name] = s * _jax.random.normal(km, w.shape, _jnp.float32)
        out["v_" + name] = (s * s) * _jax.random.uniform(kv, w.shape, _jnp.float32, 0.5, 1.5)
    if N_MICROBATCH > 1:
        for name, axis in PER_EXAMPLE_BATCH_AXIS.items():
            out[name] = _to_microbatches(out[name], axis)
    return {'x': out['x'], 'c': out['c'], 'rel_bias': out['rel_bias'], 'w_ada': out['w_ada'], 'b_ada': out['b_ada'], 'w_in': out['w_in'], 'conv_a': out['conv_a'], 'conv_c': out['conv_c'], 'conv_c_b': out['conv_c_b'], 'lru_wa': out['lru_wa'], 'lru_ba': out['lru_ba'], 'lru_wx': out['lru_wx'], 'lru_bx': out['lru_bx'], 'lru_lambda': out['lru_lambda'], 's5_lam_re': out['s5_lam_re'], 's5_lam_im': out['s5_lam_im'], 's5_log_dt': out['s5_log_dt'], 's5_b_re': out['s5_b_re'], 's5_b_im': out['s5_b_im'], 's5_c_re': out['s5_c_re'], 's5_c_im': out['s5_c_im'], 's5_d': out['s5_d'], 's5_w_glu': out['s5_w_glu'], 's5_b_glu': out['s5_b_glu'], 'w_out': out['w_out'], 'ln_g': out['ln_g'], 'ln_b': out['ln_b'], 'loss_target': out['loss_target'], 'm_rel_bias': out['m_rel_bias'], 'm_w_ada': out['m_w_ada'], 'm_b_ada': out['m_b_ada'], 'm_w_in': out['m_w_in'], 'm_conv_a': out['m_conv_a'], 'm_conv_c': out['m_conv_c'], 'm_conv_c_b': out['m_conv_c_b'], 'm_lru_wa': out['m_lru_wa'], 'm_lru_ba': out['m_lru_ba'], 'm_lru_wx': out['m_lru_wx'], 'm_lru_bx': out['m_lru_bx'], 'm_lru_lambda': out['m_lru_lambda'], 'm_s5_lam_re': out['m_s5_lam_re'], 'm_s5_lam_im': out['m_s5_lam_im'], 'm_s5_log_dt': out['m_s5_log_dt'], 'm_s5_b_re': out['m_s5_b_re'], 'm_s5_b_im': out['m_s5_b_im'], 'm_s5_c_re': out['m_s5_c_re'], 'm_s5_c_im': out['m_s5_c_im'], 'm_s5_d': out['m_s5_d'], 'm_s5_w_glu': out['m_s5_w_glu'], 'm_s5_b_glu': out['m_s5_b_glu'], 'm_w_out': out['m_w_out'], 'm_ln_g': out['m_ln_g'], 'm_ln_b': out['m_ln_b'], 'v_rel_bias': out['v_rel_bias'], 'v_w_ada': out['v_w_ada'], 'v_b_ada': out['v_b_ada'], 'v_w_in': out['v_w_in'], 'v_conv_a': out['v_conv_a'], 'v_conv_c': out['v_conv_c'], 'v_conv_c_b': out['v_conv_c_b'], 'v_lru_wa': out['v_lru_wa'], 'v_lru_ba': out['v_lru_ba'], 'v_lru_wx': out['v_lru_wx'], 'v_lru_bx': out['v_lru_bx'], 'v_lru_lambda': out['v_lru_lambda'], 'v_s5_lam_re': out['v_s5_lam_re'], 'v_s5_lam_im': out['v_s5_lam_im'], 'v_s5_log_dt': out['v_s5_log_dt'], 'v_s5_b_re': out['v_s5_b_re'], 'v_s5_b_im': out['v_s5_b_im'], 'v_s5_c_re': out['v_s5_c_re'], 'v_s5_c_im': out['v_s5_c_im'], 'v_s5_d': out['v_s5_d'], 'v_s5_w_glu': out['v_s5_w_glu'], 'v_s5_b_glu': out['v_s5_b_glu'], 'v_w_out': out['v_w_out'], 'v_ln_g': out['v_ln_g'], 'v_ln_b': out['v_ln_b']}


def _loss(weights, diff, rest, loss_target):
    with _jax.named_scope("forward"):
        args = {**rest, TWIN_DIFF_INPUT: diff, **{k: w.astype(_WEIGHT_DTYPES[k]) for k, w in weights.items()}}
        y = _forward(args)
    with _jax.named_scope("loss_head"):
        err = _jnp.square(y.astype(_jnp.float32) - loss_target)
        return 0.5 * _jnp.sum(_jnp.mean(err, axis=-1)) if err.ndim else 0.5 * err


def _adamw(w, g, m, v):
    m = ADAM_B1 * m + (1.0 - ADAM_B1) * g
    v = ADAM_B2 * v + (1.0 - ADAM_B2) * _jnp.square(g)
    m_hat = m / (1.0 - ADAM_B1 ** ADAM_STEP)
    v_hat = v / (1.0 - ADAM_B2 ** ADAM_STEP)
    delta = -ADAM_LR * (m_hat / (_jnp.sqrt(v_hat) + ADAM_EPS) + ADAM_WD * w)
    return delta, m, v


def reference(x, c, rel_bias, w_ada, b_ada, w_in, conv_a, conv_c, conv_c_b, lru_wa, lru_ba, lru_wx, lru_bx, lru_lambda, s5_lam_re, s5_lam_im, s5_log_dt, s5_b_re, s5_b_im, s5_c_re, s5_c_im, s5_d, s5_w_glu, s5_b_glu, w_out, ln_g, ln_b, loss_target, m_rel_bias, m_w_ada, m_b_ada, m_w_in, m_conv_a, m_conv_c, m_conv_c_b, m_lru_wa, m_lru_ba, m_lru_wx, m_lru_bx, m_lru_lambda, m_s5_lam_re, m_s5_lam_im, m_s5_log_dt, m_s5_b_re, m_s5_b_im, m_s5_c_re, m_s5_c_im, m_s5_d, m_s5_w_glu, m_s5_b_glu, m_w_out, m_ln_g, m_ln_b, v_rel_bias, v_w_ada, v_b_ada, v_w_in, v_conv_a, v_conv_c, v_conv_c_b, v_lru_wa, v_lru_ba, v_lru_wx, v_lru_bx, v_lru_lambda, v_s5_lam_re, v_s5_lam_im, v_s5_log_dt, v_s5_b_re, v_s5_b_im, v_s5_c_re, v_s5_c_im, v_s5_d, v_s5_w_glu, v_s5_b_glu, v_w_out, v_ln_g, v_ln_b):
    given = dict(x=x, c=c, rel_bias=rel_bias, w_ada=w_ada, b_ada=b_ada, w_in=w_in, conv_a=conv_a, conv_c=conv_c, conv_c_b=conv_c_b, lru_wa=lru_wa, lru_ba=lru_ba, lru_wx=lru_wx, lru_bx=lru_bx, lru_lambda=lru_lambda, s5_lam_re=s5_lam_re, s5_lam_im=s5_lam_im, s5_log_dt=s5_log_dt, s5_b_re=s5_b_re, s5_b_im=s5_b_im, s5_c_re=s5_c_re, s5_c_im=s5_c_im, s5_d=s5_d, s5_w_glu=s5_w_glu, s5_b_glu=s5_b_glu, w_out=w_out, ln_g=ln_g, ln_b=ln_b, loss_target=loss_target, m_rel_bias=m_rel_bias, m_w_ada=m_w_ada, m_b_ada=m_b_ada, m_w_in=m_w_in, m_conv_a=m_conv_a, m_conv_c=m_conv_c, m_conv_c_b=m_conv_c_b, m_lru_wa=m_lru_wa, m_lru_ba=m_lru_ba, m_lru_wx=m_lru_wx, m_lru_bx=m_lru_bx, m_lru_lambda=m_lru_lambda, m_s5_lam_re=m_s5_lam_re, m_s5_lam_im=m_s5_lam_im, m_s5_log_dt=m_s5_log_dt, m_s5_b_re=m_s5_b_re, m_s5_b_im=m_s5_b_im, m_s5_c_re=m_s5_c_re, m_s5_c_im=m_s5_c_im, m_s5_d=m_s5_d, m_s5_w_glu=m_s5_w_glu, m_s5_b_glu=m_s5_b_glu, m_w_out=m_w_out, m_ln_g=m_ln_g, m_ln_b=m_ln_b, v_rel_bias=v_rel_bias, v_w_ada=v_w_ada, v_b_ada=v_b_ada, v_w_in=v_w_in, v_conv_a=v_conv_a, v_conv_c=v_conv_c, v_conv_c_b=v_conv_c_b, v_lru_wa=v_lru_wa, v_lru_ba=v_lru_ba, v_lru_wx=v_lru_wx, v_lru_bx=v_lru_bx, v_lru_lambda=v_lru_lambda, v_s5_lam_re=v_s5_lam_re, v_s5_lam_im=v_s5_lam_im, v_s5_log_dt=v_s5_log_dt, v_s5_b_re=v_s5_b_re, v_s5_b_im=v_s5_b_im, v_s5_c_re=v_s5_c_re, v_s5_c_im=v_s5_c_im, v_s5_d=v_s5_d, v_s5_w_glu=v_s5_w_glu, v_s5_b_glu=v_s5_b_glu, v_w_out=v_w_out, v_ln_g=v_ln_g, v_ln_b=v_ln_b)
    weights = {n: given[n] for n in TWIN_WEIGHTS}
    shared = {n: given[n] for n in SHARED_INPUTS}
    per_example = {n: given[n] for n in ['x', 'c']}
    grad_fn = _jax.value_and_grad(_loss, argnums=(0, 1))

    def one_microbatch(ex, loss_target):
        ex = dict(ex)
        diff = ex.pop(TWIN_DIFF_INPUT)
        return grad_fn(weights, diff, {**shared, **ex}, loss_target)

    if N_MICROBATCH == 1:
        loss, (grad_w, grad_x) = one_microbatch(per_example, given["loss_target"])
    else:
        def body(carry, xs):
            loss_sum, grad_sum = carry
            l_k, (gw_k, gx_k) = one_microbatch(xs[0], xs[1])
            with _jax.named_scope("update"):
                return (loss_sum + l_k, _jax.tree.map(_jnp.add, grad_sum, gw_k)), gx_k

        init = (_jnp.zeros((), _jnp.float32), _jax.tree.map(_jnp.zeros_like, weights))
        (loss, grad_w), grad_x = _jax.lax.scan(body, init, (per_example, given["loss_target"]))
    with _jax.named_scope("update"):
        delta_w, new_m, new_v = {}, {}, {}
        for n in TWIN_WEIGHTS:
            delta_w[n], new_m[n], new_v[n] = _adamw(weights[n], grad_w[n], given["m_" + n], given["v_" + n])
    return (loss, grad_x, *[grad_w[n] for n in TWIN_WEIGHTS], *[delta_w[n] for n in TWIN_WEIGHTS],
            *[new_m[n] for n in TWIN_WEIGHTS], *[new_v[n] for n in TWIN_WEIGHTS])
```

```python
import functools
import math

import jax
import jax.numpy as jnp
from jax import lax
from jax.experimental import pallas as pl
from jax.experimental.pallas import tpu as pltpu

F32 = jnp.float32
MXU_DTYPE = jnp.bfloat16
WIRE_DTYPE = jnp.bfloat16
SDS = jax.ShapeDtypeStruct
MESH = pl.DeviceIdType.MESH
ANY = pl.BlockSpec(memory_space=pl.ANY)
VMEM_LIMIT = 48 * 1024 * 1024

D_MODEL = 2048
DEPTH = 2
BR = 512
ATT_HEADS = 8
HEAD_DIM = 64
DILATIONS = ((128, 1), (512, 4), (2048, 16))
BLK = 128
REL_BUCKETS = 32
REL_MAX_DIST = 2048
LRU_HEADS = 8
LRU_C = 8.0
S5_CH = 16
S5_GROUPS = 32
S5_STATE = 64
S5_N = S5_GROUPS * S5_STATE
N_IN = 12 * BR
ALPHA = (2 * DEPTH) ** 0.25
LN_EPS = 1e-5
NEG = -1e30
ADAM_LR, ADAM_B1, ADAM_B2, ADAM_EPS, ADAM_WD, ADAM_STEP = 0.001, 0.9, 0.999, 1e-08, 0.01, 10
CB_AB, CB_AC, CB_AX, CB_AG, CB_Q, CB_K, CB_V, CB_BG, CB_CX, CB_CG, CB_DU, CB_DG = range(12)
N_CHIPS = 4
N_DEV = 8


def _params(n_axes=0):
    kw = {"dimension_semantics": ("arbitrary",) * n_axes} if n_axes else {}
    return pltpu.CompilerParams(vmem_limit_bytes=VMEM_LIMIT, **kw)


def _rows(tb, w, cb=0):
    return pl.BlockSpec((tb, w), lambda i: (i, cb))


def _prev8(tb, w, cb=0):
    return pl.BlockSpec((8, w), lambda i: (jnp.maximum(i * (tb // 8) - 1, 0), cb))


def _next8(tb, w, n_rows, cb=0):
    return pl.BlockSpec((8, w), lambda i: (jnp.minimum((i + 1) * (tb // 8), n_rows // 8 - 1), cb))


def _const(shape):
    return pl.BlockSpec(shape, lambda *_: (0,) * len(shape))


def _silu(x):
    return x * jax.nn.sigmoid(x)


def _dsilu(x):
    s = jax.nn.sigmoid(x)
    return s * (1.0 + x * (1.0 - s))


def _shift_down(cur, prev8, j):
    rolled = pltpu.roll(cur, j, 0)
    row = lax.broadcasted_iota(jnp.int32, (8, cur.shape[1]), 0)
    first = jnp.where(row < j, pltpu.roll(prev8, j, 0), rolled[0:8])
    return jnp.concatenate([first, rolled[8:]], axis=0)


def _shift_up(cur, next8, j):
    t = cur.shape[0]
    rolled = pltpu.roll(cur, t - j, 0)
    row = lax.broadcasted_iota(jnp.int32, (8, cur.shape[1]), 0)
    last = jnp.where(row >= 8 - j, pltpu.roll(next8, 8 - j, 0), rolled[t - 8:t])
    return jnp.concatenate([rolled[:t - 8], last], axis=0)


def _colsum(x):
    return jnp.sum(x, axis=0, keepdims=True)


def _init_acc(*refs):
    @pl.when(pl.program_id(0) == 0)
    def _():
        for r in refs:
            r[...] = jnp.zeros_like(r)


def _mm(a, b, *, name, ta=False, tb=False, out_dtype=F32, tm=512, tn=512, tk=512, a_col0=0, a_ncols=None, bias=None):
    a_ncols = a.shape[1] - a_col0 if a_ncols is None else a_ncols
    m, k = (a_ncols, a.shape[0]) if ta else (a.shape[0], a_ncols)
    n = b.shape[0] if tb else b.shape[1]
    assert k == (b.shape[1] if tb else b.shape[0]), (name, a.shape, b.shape)
    tm, tn, tk = min(tm, m), min(tn, n), min(tk, k)
    nk = k // tk
    a_off = a_col0 // (tm if ta else tk)
    assert m % tm == 0 and n % tn == 0 and k % tk == 0 and a_col0 % (tm if ta else tk) == 0, (name, m, n, k)

    def body(*refs):
        if bias is None:
            a_ref, b_ref, o_ref, acc = refs
        else:
            a_ref, b_ref, bias_ref, o_ref, acc = refs
        kk = pl.program_id(2)

        @pl.when(kk == 0)
        def _():
            acc[...] = jnp.zeros_like(acc)

        dims = (((0 if ta else 1,), (1 if tb else 0,)), ((), ()))
        acc[...] += lax.dot_general(a_ref[...].astype(MXU_DTYPE), b_ref[...].astype(MXU_DTYPE), dims,
                                    preferred_element_type=F32)

        @pl.when(kk == nk - 1)
        def _():
            r = acc[...]
            if bias is not None:
                r = r + bias_ref[...]
            o_ref[...] = r.astype(out_dtype)

    a_spec = (pl.BlockSpec((tk, tm), lambda i, j, kk: (kk, i + a_off)) if ta
              else pl.BlockSpec((tm, tk), lambda i, j, kk: (i, kk + a_off)))
    b_spec = (pl.BlockSpec((tn, tk), lambda i, j, kk: (j, kk)) if tb
              else pl.BlockSpec((tk, tn), lambda i, j, kk: (kk, j)))
    in_specs, args = [a_spec, b_spec], [a, b]
    if bias is not None:
        in_specs.append(pl.BlockSpec((1, tn), lambda i, j, kk: (0, j)))
        args.append(bias)
    return pl.pallas_call(
        body, name=name, out_shape=SDS((m, n), out_dtype), grid=(m // tm, n // tn, nk), in_specs=in_specs,
        out_specs=pl.BlockSpec((tm, tn), lambda i, j, kk: (i, j)), scratch_shapes=[pltpu.VMEM((tm, tn), F32)],
        compiler_params=_params(3))(*args)


def _silu_rows(c_all):
    def body(c_ref, o_ref):
        o_ref[...] = _silu(c_ref[...])
    return pl.pallas_call(body, name="cond_silu", out_shape=SDS(c_all.shape, F32))(c_all)


def _modulate(x, scale, shift, tb):
    s, d = x.shape

    def body(x_ref, sc_ref, sh_ref, o_ref):
        o_ref[...] = (x_ref[...] * (1.0 + sc_ref[...]) + sh_ref[...]).astype(MXU_DTYPE)

    return pl.pallas_call(body, name="modulate", out_shape=SDS((s, d), MXU_DTYPE), grid=(s // tb,),
                          in_specs=[_rows(tb, d), _const((1, d)), _const((1, d))], out_specs=_rows(tb, d),
                          compiler_params=_params(1))(x, scale, shift)


def _out_ln(ycat, w_out, x, gate, ln_g, ln_b, tb):
    s, d = x.shape

    def body(yc_ref, w_ref, x_ref, gt_ref, g_ref, b_ref, xn_ref, xh_ref, y_ref, rs_ref):
        y = jnp.dot(yc_ref[...], w_ref[...], preferred_element_type=F32)
        res = ALPHA * x_ref[...] + (1.0 + gt_ref[...]) * y
        mu = jnp.mean(res, axis=-1, keepdims=True)
        cen = res - mu
        var = jnp.mean(cen * cen, axis=-1, keepdims=True)
        rstd = lax.rsqrt(var + LN_EPS)
        xhat = cen * rstd
        xn_ref[...] = xhat * g_ref[...] + b_ref[...]
        xh_ref[...] = xhat
        y_ref[...] = y
        rs_ref[...] = rstd

    big = SDS((s, d), F32)
    return pl.pallas_call(
        body, name="out_proj_ln", out_shape=(big, big, big, SDS((s, 1), F32)), grid=(s // tb,),
        in_specs=[_rows(tb, d), _const((d, d)), _rows(tb, d), _const((1, d)), _const((1, d)), _const((1, d))],
        out_specs=(_rows(tb, d), _rows(tb, d), _rows(tb, d), _rows(tb, 1)), compiler_params=_params(1),
    )(ycat, w_out, x, gate, ln_g, ln_b)


def _ln_bwd(dxn, xhat, y, rstd, ln_g, gate, tb):
    s, d = dxn.shape

    def body(dxn_ref, xh_ref, y_ref, rs_ref, g_ref, gt_ref, dy_ref, dxa_ref, dg_ref, db_ref, dgt_ref):
        _init_acc(dg_ref, db_ref, dgt_ref)
        dxn_t, xh = dxn_ref[...], xh_ref[...]
        dxh = dxn_t * g_ref[...]
        dres = rs_ref[...] * (dxh - jnp.mean(dxh, axis=-1, keepdims=True)
                              - xh * jnp.mean(dxh * xh, axis=-1, keepdims=True))
        dy_ref[...] = ((1.0 + gt_ref[...]) * dres).astype(MXU_DTYPE)
        dxa_ref[...] = ALPHA * dres
        dg_ref[...] += _colsum(dxn_t * xh)
        db_ref[...] += _colsum(dxn_t)
        dgt_ref[...] += _colsum(dres * y_ref[...])

    vec = SDS((1, d), F32)
    return pl.pallas_call(
        body, name="ln_bwd", out_shape=(SDS((s, d), MXU_DTYPE), SDS((s, d), F32), vec, vec, vec), grid=(s // tb,),
        in_specs=[_rows(tb, d), _rows(tb, d), _rows(tb, d), _rows(tb, 1), _const((1, d)), _const((1, d))],
        out_specs=(_rows(tb, d), _rows(tb, d), _const((1, d)), _const((1, d)), _const((1, d))),
        compiler_params=_params(1))(dxn, xhat, y, rstd, ln_g, gate)


def _mod_bwd(dh, dxa, x, scale, tb):
    s, d = dh.shape

    def body(dh_ref, dxa_ref, x_ref, sc_ref, dx_ref, dsh_ref, dsc_ref):
        _init_acc(dsh_ref, dsc_ref)
        dh_t = dh_ref[...]
        dx_ref[...] = dxa_ref[...] + dh_t * (1.0 + sc_ref[...])
        dsh_ref[...] += _colsum(dh_t)
        dsc_ref[...] += _colsum(dh_t * x_ref[...])

    vec = SDS((1, d), F32)
    return pl.pallas_call(
        body, name="mod_bwd", out_shape=(SDS((s, d), F32), vec, vec), grid=(s // tb,),
        in_specs=[_rows(tb, d), _rows(tb, d), _rows(tb, d), _const((1, d))],
        out_specs=(_rows(tb, d), _const((1, d)), _const((1, d))), compiler_params=_params(1))(dh, dxa, x, scale)


def _loss_head(y, target, tb):
    s, d = y.shape

    def body(y_ref, t_ref, l_ref, dy_ref):
        _init_acc(l_ref)
        err = y_ref[...] - t_ref[...]
        l_ref[...] += (0.5 / d) * jnp.sum(err * err, keepdims=True)
        dy_ref[...] = err * (1.0 / d)

    return pl.pallas_call(body, name="loss_head", out_shape=(SDS((1, 1), F32), SDS((s, d), F32)), grid=(s // tb,),
                          in_specs=[_rows(tb, d), _rows(tb, d)], out_specs=(_const((1, 1)), _rows(tb, d)),
                          compiler_params=_params(1))(y, target)


def _conv_taps(u, up, w_ref, width):
    out = w_ref[width - 1:width, :] * u
    for j in range(width - 2, -1, -1):
        out = out + w_ref[j:j + 1, :] * _shift_down(u, up, width - 1 - j)
    return out


def _conv_taps_t(g, gn, w_ref, width):
    out = w_ref[width - 1:width, :] * g
    for j in range(width - 2, -1, -1):
        out = out + w_ref[j:j + 1, :] * _shift_up(g, gn, width - 1 - j)
    return out


def _conv_wgrad(dw_ref, g, u, up, width):
    dw_ref[width - 1:width, :] += _colsum(g * u)
    for j in range(width - 1):
        dw_ref[j:j + 1, :] += _colsum(g * _shift_down(u, up, width - 1 - j))


def _branch_a_fwd(proj, conv_w, tb):
    s = proj.shape[0]

    def body(ab, ac, ax, ag, acp, axp, w_ref, o_ref):
        has_prev = (pl.program_id(0) > 0).astype(F32)
        u = ac[...] * ax[...]
        up = acp[...] * axp[...] * has_prev
        o_ref[...] = (ab[...] * _conv_taps(u, up, w_ref, 3) * _silu(ag[...])).astype(MXU_DTYPE)

    return pl.pallas_call(
        body, name="branch_a_fwd", out_shape=SDS((s, BR), MXU_DTYPE), grid=(s // tb,),
        in_specs=[_rows(tb, BR, CB_AB), _rows(tb, BR, CB_AC), _rows(tb, BR, CB_AX), _rows(tb, BR, CB_AG),
                  _prev8(tb, BR, CB_AC), _prev8(tb, BR, CB_AX), _const((8, BR))],
        out_specs=_rows(tb, BR), compiler_params=_params(1))(proj, proj, proj, proj, proj, proj, conv_w)


def _branch_a_bwd(dycat, proj, conv_w, tb):
    s = proj.shape[0]

    def body(dy, dyn, ab, abn, ag, agn, ac, acp, ax, axp, w_ref, o_ref, dw_ref):
        _init_acc(dw_ref)
        i = pl.program_id(0)
        has_prev = (i > 0).astype(F32)
        has_next = (i < pl.num_programs(0) - 1).astype(F32)
        u = ac[...] * ax[...]
        up = acp[...] * axp[...] * has_prev
        v = _conv_taps(u, up, w_ref, 3)
        sg = _silu(ag[...])
        dv = dy[...] * ab[...] * sg
        dvn = dyn[...] * abn[...] * _silu(agn[...]) * has_next
        du = _conv_taps_t(dv, dvn, w_ref, 3)
        o_ref[:, 0:BR] = (dy[...] * v * sg).astype(MXU_DTYPE)
        o_ref[:, BR:2 * BR] = (du * ax[...]).astype(MXU_DTYPE)
        o_ref[:, 2 * BR:3 * BR] = (du * ac[...]).astype(MXU_DTYPE)
        o_ref[:, 3 * BR:4 * BR] = (dy[...] * ab[...] * v * _dsilu(ag[...])).astype(MXU_DTYPE)
        _conv_wgrad(dw_ref, dv, u, up, 3)

    return pl.pallas_call(
        body, name="branch_a_bwd", out_shape=(SDS((s, 4 * BR), MXU_DTYPE), SDS((8, BR), F32)), grid=(s // tb,),
        in_specs=[_rows(tb, BR, 0), _next8(tb, BR, s, 0),
                  _rows(tb, BR, CB_AB), _next8(tb, BR, s, CB_AB), _rows(tb, BR, CB_AG), _next8(tb, BR, s, CB_AG),
                  _rows(tb, BR, CB_AC), _prev8(tb, BR, CB_AC), _rows(tb, BR, CB_AX), _prev8(tb, BR, CB_AX),
                  _const((8, BR))],
        out_specs=(_rows(tb, 4 * BR), _const((8, BR))), compiler_params=_params(1),
    )(dycat, dycat, proj, proj, proj, proj, proj, proj, proj, proj, conv_w)


def _t5_bucket(dist):
    max_exact = REL_BUCKETS // 2
    nf = jnp.maximum(dist, 1).astype(F32)
    large = max_exact + (jnp.log(nf / max_exact) / math.log(REL_MAX_DIST / max_exact)
                         * (REL_BUCKETS - max_exact)).astype(jnp.int32)
    large = jnp.minimum(large, REL_BUCKETS - 1)
    return jnp.where(dist < max_exact, dist, large)


def _bucket_maps():
    maps = []
    i = jnp.arange(BLK)[:, None]
    j = jnp.arange(2 * BLK)[None, :]
    delta = i + BLK - j
    for window, dil in DILATIONS:
        span = window // dil
        bucket = _t5_bucket(jnp.clip(delta, 0, span) * dil)
        maps.append(jnp.where((delta >= 0) & (delta <= span), bucket, -1))
    return jnp.stack(maps).astype(jnp.int32)


def _bias_tables(rel_bias, buckets):
    tab = jnp.transpose(rel_bias[jnp.maximum(buckets, 0)], (0, 3, 1, 2))
    return jnp.where((buckets >= 0)[:, None], tab, NEG).astype(F32)


def _head_masks():
    lane = lax.broadcasted_iota(jnp.int32, (1, 2 * HEAD_DIM), 1)
    return [(lane < HEAD_DIM).astype(F32), (lane >= HEAD_DIM).astype(F32)]


def _rows_of(r, dil):
    return pl.ds(r, BLK, stride=dil) if dil > 1 else pl.ds(0, BLK)


def _attn_fwd(proj, bias, dil):
    s = proj.shape[0]
    unit = dil * BLK
    nb = s // unit
    w = 2 * HEAD_DIM
    q0, k0, v0 = (cb * (BR // w) for cb in (CB_Q, CB_K, CB_V))

    def body(q_ref, kc_ref, kp_ref, vc_ref, vp_ref, bias_ref, o_ref, lse_ref):
        n = pl.program_id(1)
        col = lax.broadcasted_iota(jnp.int32, (1, 2 * BLK), 1)
        no_prev = jnp.where((n == 0) & (col < BLK), NEG, 0.0)
        masks = _head_masks()

        def per_r(r, carry):
            rows = _rows_of(r, dil)
            q = q_ref[rows, :] * (HEAD_DIM ** -0.5)
            k = jnp.concatenate([kp_ref[rows, :], kc_ref[rows, :]], axis=0).astype(MXU_DTYPE)
            v = jnp.concatenate([vp_ref[rows, :], vc_ref[rows, :]], axis=0).astype(MXU_DTYPE)
            o_acc = jnp.zeros((BLK, w), F32)
            lse_acc = jnp.zeros((BLK, w), F32)
            for h in range(2):
                qh = (q * masks[h]).astype(MXU_DTYPE)
                sc = lax.dot_general(qh, k, (((1,), (1,)), ((), ())), preferred_element_type=F32)
                sc = sc + bias_ref[h] + no_prev
                mx = jnp.max(sc, axis=-1, keepdims=True)
                p = jnp.exp(sc - mx)
                l = jnp.sum(p, axis=-1, keepdims=True)
                oh = jnp.dot((p / l).astype(MXU_DTYPE), v, preferred_element_type=F32)
                o_acc = o_acc + oh * masks[h]
                lse_acc = lse_acc + (mx + jnp.log(l)) * masks[h]
            o_ref[rows, :] = o_acc
            lse_ref[rows, :] = lse_acc
            return carry

        lax.fori_loop(0, dil, per_r, 0)

    cur = lambda c0: pl.BlockSpec((unit, w), lambda hp, n: (n, c0 + hp))
    prev = lambda c0: pl.BlockSpec((unit, w), lambda hp, n: (jnp.maximum(n - 1, 0), c0 + hp))
    out = pl.BlockSpec((unit, w), lambda hp, n: (n, hp))
    return pl.pallas_call(
        body, name=f"attn_fwd_d{dil}", out_shape=(SDS((s, BR), F32), SDS((s, BR), F32)), grid=(BR // w, nb),
        in_specs=[cur(q0), cur(k0), prev(k0), cur(v0), prev(v0),
                  pl.BlockSpec((2, BLK, 2 * BLK), lambda hp, n: (hp, 0, 0))],
        out_specs=(out, out), compiler_params=_params(2))(proj, proj, proj, proj, proj, bias)


def _softmax3(l0, l1, l2):
    mx = jnp.maximum(jnp.maximum(l0, l1), l2)
    e0, e1, e2 = jnp.exp(l0 - mx), jnp.exp(l1 - mx), jnp.exp(l2 - mx)
    inv = 1.0 / (e0 + e1 + e2)
    return e0 * inv, e1 * inv, e2 * inv


def _attn_combine(os_, lses, proj, tb):
    s = proj.shape[0]

    def body(o0, o1, o2, l0, l1, l2, bg, y_ref):
        w0, w1, w2 = _softmax3(l0[...], l1[...], l2[...])
        attn = w0 * o0[...] + w1 * o1[...] + w2 * o2[...]
        y_ref[...] = (attn * _silu(bg[...])).astype(MXU_DTYPE)

    return pl.pallas_call(
        body, name="attn_combine", out_shape=SDS((s, BR), MXU_DTYPE), grid=(s // tb,),
        in_specs=[_rows(tb, BR)] * 6 + [_rows(tb, BR, CB_BG)], out_specs=_rows(tb, BR),
        compiler_params=_params(1))(*os_, *lses, proj)


def _attn_bwd_pre(dycat, os_, lses, proj, head_ones, tb):
    s = proj.shape[0]

    def body(dy, o0, o1, o2, l0, l1, l2, bg, e_ref, dbg_ref, do0, do1, do2, dm0, dm1, dm2):
        w0, w1, w2 = _softmax3(l0[...], l1[...], l2[...])
        attn = w0 * o0[...] + w1 * o1[...] + w2 * o2[...]
        dattn = dy[...] * _silu(bg[...])
        dbg_ref[...] = dy[...] * attn * _dsilu(bg[...])
        prod = dattn * attn
        hi = prod.astype(MXU_DTYPE)
        lo = (prod - hi.astype(F32)).astype(MXU_DTYPE)
        tot = (jnp.dot(hi, e_ref[...], preferred_element_type=F32)
               + jnp.dot(lo, e_ref[...], preferred_element_type=F32))
        for wg, do_ref, dm_ref in ((w0, do0, dm0), (w1, do1, dm1), (w2, do2, dm2)):
            do_ref[...] = wg * dattn
            dm_ref[...] = wg * tot

    big = SDS((s, BR), F32)
    return pl.pallas_call(
        body, name="attn_bwd_pre", out_shape=(big,) * 7, grid=(s // tb,),
        in_specs=[_rows(tb, BR, 1)] + [_rows(tb, BR)] * 6 + [_rows(tb, BR, CB_BG), _const((BR, BR))],
        out_specs=(_rows(tb, BR),) * 7, compiler_params=_params(1))(dycat, *os_, *lses, proj, head_ones)


def _attn_bwd(proj, do, lse, dm, bias, dil):
    s = proj.shape[0]
    unit = dil * BLK
    nb = s // unit
    w = 2 * HEAD_DIM
    q0, k0, v0 = (cb * (BR // w) for cb in (CB_Q, CB_K, CB_V))

    def body(q_ref, kc_ref, kp_ref, vc_ref, vp_ref, do_ref, lse_ref, dm_ref, bias_ref,
             dq_ref, dk_ref, dv_ref, dbias_ref, carry_k, carry_v):
        n = pl.program_id(1)
        col = lax.broadcasted_iota(jnp.int32, (1, 2 * BLK), 1)
        no_prev = jnp.where((n == 0) & (col < BLK), NEG, 0.0)
        masks = _head_masks()

        @pl.when(n == 0)
        def _():
            dbias_ref[...] = jnp.zeros_like(dbias_ref)

        @pl.when(n < nb)
        def _():
            def per_r(r, carry):
                rows = _rows_of(r, dil)
                q = q_ref[rows, :] * (HEAD_DIM ** -0.5)
                k = jnp.concatenate([kp_ref[rows, :], kc_ref[rows, :]], axis=0).astype(MXU_DTYPE)
                v = jnp.concatenate([vp_ref[rows, :], vc_ref[rows, :]], axis=0).astype(MXU_DTYPE)
                do_t, lse_t, dm_t = do_ref[rows, :], lse_ref[rows, :], dm_ref[rows, :]
                dq_acc = jnp.zeros((BLK, w), F32)
                dk_acc = jnp.zeros((2 * BLK, w), F32)
                dv_acc = jnp.zeros((2 * BLK, w), F32)
                for h in range(2):
                    qh = (q * masks[h]).astype(MXU_DTYPE)
                    doh = (do_t * masks[h]).astype(MXU_DTYPE)
                    c0 = h * HEAD_DIM
                    sc = lax.dot_general(qh, k, (((1,), (1,)), ((), ())), preferred_element_type=F32)
                    p = jnp.exp(sc + bias_ref[h] + no_prev - lse_t[:, c0:c0 + 1])
                    dp = lax.dot_general(doh, v, (((1,), (1,)), ((), ())), preferred_element_type=F32)
                    ds = p * (dp - dm_t[:, c0:c0 + 1])
                    dbias_ref[h] += ds
                    dsb, pb = ds.astype(MXU_DTYPE), p.astype(MXU_DTYPE)
                    dq_acc = dq_acc + jnp.dot(dsb, k, preferred_element_type=F32) * masks[h]
                    dk_acc = dk_acc + lax.dot_general(dsb, qh, (((0,), (0,)), ((), ())), preferred_element_type=F32)
                    dv_acc = dv_acc + lax.dot_general(pb, doh, (((0,), (0,)), ((), ())), preferred_element_type=F32)
                dq_ref[rows, :] = dq_acc * (HEAD_DIM ** -0.5)

                @pl.when(n > 0)
                def _():
                    dk_ref[rows, :] = carry_k[rows, :] + dk_acc[0:BLK]
                    dv_ref[rows, :] = carry_v[rows, :] + dv_acc[0:BLK]

                carry_k[rows, :] = dk_acc[BLK:2 * BLK]
                carry_v[rows, :] = dv_acc[BLK:2 * BLK]
                return carry

            lax.fori_loop(0, dil, per_r, 0)

        @pl.when(n == nb)
        def _():
            dk_ref[...] = carry_k[...]
            dv_ref[...] = carry_v[...]

    qn = lambda n: jnp.minimum(n, nb - 1)
    cur = lambda c0: pl.BlockSpec((unit, w), lambda hp, n: (qn(n), c0 + hp))
    prev = lambda c0: pl.BlockSpec((unit, w), lambda hp, n: (jnp.maximum(qn(n) - 1, 0), c0 + hp))
    row = pl.BlockSpec((unit, w), lambda hp, n: (qn(n), hp))
    late = pl.BlockSpec((unit, w), lambda hp, n: (jnp.maximum(n - 1, 0), hp))
    tab = pl.BlockSpec((2, BLK, 2 * BLK), lambda hp, n: (hp, 0, 0))
    big = SDS((s, BR), F32)
    return pl.pallas_call(
        body, name=f"attn_bwd_d{dil}", out_shape=(big, big, big, SDS((ATT_HEADS, BLK, 2 * BLK), F32)),
        grid=(BR // w, nb + 1),
        in_specs=[cur(q0), cur(k0), prev(k0), cur(v0), prev(v0), row, row, row, tab],
        out_specs=(row, late, late, tab),
        scratch_shapes=[pltpu.VMEM((unit, w), F32), pltpu.VMEM((unit, w), F32)],
        compiler_params=_params(2))(proj, proj, proj, proj, proj, do, lse, dm, bias)


def _rel_bias_grad(dbias, buckets):
    def body(db_ref, bk_ref, o_ref):
        row = lax.broadcasted_iota(jnp.int32, (REL_BUCKETS, 128), 0)
        lane = lax.broadcasted_iota(jnp.int32, (REL_BUCKETS, 128), 1)

        def per_bucket(b, acc):
            for g in range(len(DILATIONS)):
                hit = bk_ref[g] == b
                for h in range(ATT_HEADS):
                    both = db_ref[0, g, h] + db_ref[1, g, h]
                    val = jnp.sum(jnp.where(hit, both, 0.0), keepdims=True)
                    acc = acc + jnp.where((row == b) & (lane == h), val, 0.0)
            return acc

        o_ref[...] = lax.fori_loop(0, REL_BUCKETS, per_bucket, jnp.zeros((REL_BUCKETS, 128), F32))

    assert dbias.shape[0] == DEPTH == 2
    return pl.pallas_call(body, name="rel_bias_grad", out_shape=SDS((REL_BUCKETS, 128), F32),
                          compiler_params=_params())(dbias, buckets)


def _scan_real(a, b, *, reverse, tb, name):
    s, ch = a.shape
    nt = s // tb
    order = range(7, -1, -1) if reverse else range(8)

    def body(a_ref, b_ref, o_ref, carry):
        @pl.when(pl.program_id(0) == 0)
        def _():
            carry[...] = jnp.zeros_like(carry)

        def group(gi, h):
            r0 = pl.multiple_of((tb // 8 - 1 - gi if reverse else gi) * 8, 8)
            a8, b8 = a_ref[pl.ds(r0, 8), :], b_ref[pl.ds(r0, 8), :]
            rows = [None] * 8
            for k in order:
                if reverse:
                    rows[k] = b8[k:k + 1] + h
                    h = a8[k:k + 1] * rows[k]
                else:
                    h = a8[k:k + 1] * h + b8[k:k + 1]
                    rows[k] = h
            o_ref[pl.ds(r0, 8), :] = jnp.concatenate(rows, axis=0)
            return h

        carry[...] = lax.fori_loop(0, tb // 8, group, carry[...])

    spec = pl.BlockSpec((tb, ch), (lambda i: (nt - 1 - i, 0)) if reverse else (lambda i: (i, 0)))
    return pl.pallas_call(body, name=name, out_shape=SDS((s, ch), F32), grid=(nt,), in_specs=[spec, spec],
                          out_specs=spec, scratch_shapes=[pltpu.VMEM((1, ch), F32)],
                          compiler_params=_params(1))(a, b)


def _scan_cplx(b, a_row, *, reverse, tb, name):
    s, ch2 = b.shape
    ch = ch2 // 2
    nt = s // tb
    order = range(7, -1, -1) if reverse else range(8)

    def body(a_ref, b_ref, o_ref, carry):
        @pl.when(pl.program_id(0) == 0)
        def _():
            carry[...] = jnp.zeros_like(carry)

        ar = a_ref[:, 0:ch]
        ai = -a_ref[:, ch:ch2] if reverse else a_ref[:, ch:ch2]

        def group(gi, x):
            xr, xi = x
            r0 = pl.multiple_of((tb // 8 - 1 - gi if reverse else gi) * 8, 8)
            br8, bi8 = b_ref[pl.ds(r0, 8), 0:ch], b_ref[pl.ds(r0, 8), ch:ch2]
            rr, ri = [None] * 8, [None] * 8
            for k in order:
                nr = ar * xr - ai * xi + br8[k:k + 1]
                ni = ar * xi + ai * xr + bi8[k:k + 1]
                xr, xi = nr, ni
                rr[k], ri[k] = xr, xi
            o_ref[pl.ds(r0, 8), 0:ch] = jnp.concatenate(rr, axis=0)
            o_ref[pl.ds(r0, 8), ch:ch2] = jnp.concatenate(ri, axis=0)
            return xr, xi

        xr, xi = lax.fori_loop(0, tb // 8, group, (carry[:, 0:ch], carry[:, ch:ch2]))
        carry[:, 0:ch] = xr
        carry[:, ch:ch2] = xi

    spec = pl.BlockSpec((tb, ch2), (lambda i: (nt - 1 - i, 0)) if reverse else (lambda i: (i, 0)))
    return pl.pallas_call(body, name=name, out_shape=SDS((s, ch2), F32), grid=(nt,),
                          in_specs=[_const((1, ch2)), spec], out_specs=spec,
                          scratch_shapes=[pltpu.VMEM((1, ch2), F32)], compiler_params=_params(1))(a_row, b)


def _neg_expm1(z):
    series = -z * (1.0 + z * (0.5 + z * (1.0 / 6 + z * (1.0 / 24 + z * (1.0 / 120)))))
    return jnp.where(z > -0.05, series, 1.0 - jnp.exp(z))


def _lru_gate(xc, pre_r, pre_i, lam):
    log_a = -LRU_C * jax.nn.sigmoid(pre_r) * jax.nn.softplus(-lam)
    return jnp.exp(log_a), jnp.sqrt(_neg_expm1(2.0 * log_a)) * jax.nn.sigmoid(pre_i) * xc


def _lru_gates_fwd(proj, conv_w, conv_b, w_cat, b_cat, lam, tb):
    s = proj.shape[0]

    def body(cx, cxp, w_ref, cb_ref, wc_ref, bc_ref, lam_ref, a_ref, b_ref):
        has_prev = (pl.program_id(0) > 0).astype(F32)
        xc = _conv_taps(cx[...], cxp[...] * has_prev, w_ref, 4) + cb_ref[...]
        pre = jnp.dot(xc.astype(MXU_DTYPE), wc_ref[...], preferred_element_type=F32) + bc_ref[...]
        a_ref[...], b_ref[...] = _lru_gate(xc, pre[:, 0:BR], pre[:, BR:2 * BR], lam_ref[...])

    big = SDS((s, BR), F32)
    return pl.pallas_call(
        body, name="lru_gates_fwd", out_shape=(big, big), grid=(s // tb,),
        in_specs=[_rows(tb, BR, CB_CX), _prev8(tb, BR, CB_CX), _const((8, BR)), _const((1, BR)),
                  _const((BR, 2 * BR)), _const((1, 2 * BR)), _const((1, BR))],
        out_specs=(_rows(tb, BR), _rows(tb, BR)), compiler_params=_params(1),
    )(proj, proj, conv_w, conv_b, w_cat, b_cat, lam)


def _gate_out(h, proj, cb, tb, name):
    s = proj.shape[0]

    def body(h_ref, g_ref, o_ref):
        o_ref[...] = (h_ref[...] * _silu(g_ref[...])).astype(MXU_DTYPE)

    return pl.pallas_call(body, name=name, out_shape=SDS((s, BR), MXU_DTYPE), grid=(s // tb,),
                          in_specs=[_rows(tb, BR), _rows(tb, BR, cb)], out_specs=_rows(tb, BR),
                          compiler_params=_params(1))(h, proj)


def _gate_out_bwd(dycat, dy_cb, h, proj, cb, tb, name):
    s = proj.shape[0]

    def body(dy, h_ref, g_ref, dh_ref, dg_ref):
        dh_ref[...] = dy[...] * _silu(g_ref[...])
        dg_ref[...] = dy[...] * h_ref[...] * _dsilu(g_ref[...])

    big = SDS((s, BR), F32)
    return pl.pallas_call(body, name=name, out_shape=(big, big), grid=(s // tb,),
                          in_specs=[_rows(tb, BR, dy_cb), _rows(tb, BR), _rows(tb, BR, cb)],
                          out_specs=(_rows(tb, BR), _rows(tb, BR)), compiler_params=_params(1))(dycat, h, proj)


def _lru_gates_bwd(proj, lmb, h, conv_w, conv_b, w_cat, b_cat, lam, tb):
    s = proj.shape[0]

    def body(cx, cxp, l_ref, h_ref, hp_ref, w_ref, cb_ref, wc_ref, bc_ref, lam_ref,
             dxc_ref, dpre_ref, xc_ref, dbc_ref, dlam_ref):
        _init_acc(dbc_ref, dlam_ref)
        has_prev = (pl.program_id(0) > 0).astype(F32)
        xc = _conv_taps(cx[...], cxp[...] * has_prev, w_ref, 4) + cb_ref[...]
        xcb = xc.astype(MXU_DTYPE)
        pre = jnp.dot(xcb, wc_ref[...], preferred_element_type=F32) + bc_ref[...]
        _, vjp = jax.vjp(_lru_gate, xc, pre[:, 0:BR], pre[:, BR:2 * BR], lam_ref[...])
        lm = l_ref[...]
        dxc, dpr, dpi, dlam = vjp((lm * _shift_down(h_ref[...], hp_ref[...] * has_prev, 1), lm))
        dpre = jnp.concatenate([dpr, dpi], axis=1)
        dpreb = dpre.astype(MXU_DTYPE)
        dxc_ref[...] = dxc + lax.dot_general(dpreb, wc_ref[...], (((1,), (1,)), ((), ())),
                                             preferred_element_type=F32)
        dpre_ref[...] = dpreb
        xc_ref[...] = xcb
        dbc_ref[...] += _colsum(dpre)
        dlam_ref[...] += dlam

    return pl.pallas_call(
        body, name="lru_gates_bwd",
        out_shape=(SDS((s, BR), F32), SDS((s, 2 * BR), MXU_DTYPE), SDS((s, BR), MXU_DTYPE),
                   SDS((1, 2 * BR), F32), SDS((1, BR), F32)),
        grid=(s // tb,),
        in_specs=[_rows(tb, BR, CB_CX), _prev8(tb, BR, CB_CX), _rows(tb, BR), _rows(tb, BR), _prev8(tb, BR),
                  _const((8, BR)), _const((1, BR)), _const((BR, 2 * BR)), _const((1, 2 * BR)), _const((1, BR))],
        out_specs=(_rows(tb, BR), _rows(tb, 2 * BR), _rows(tb, BR), _const((1, 2 * BR)), _const((1, BR))),
        compiler_params=_params(1))(proj, proj, lmb, h, h, conv_w, conv_b, w_cat, b_cat, lam)


def _conv_c_bwd(dxc, proj, conv_w, tb):
    s = proj.shape[0]

    def body(g, gn, cx, cxp, w_ref, dcx_ref, dw_ref, db_ref):
        _init_acc(dw_ref, db_ref)
        i = pl.program_id(0)
        has_prev = (i > 0).astype(F32)
        has_next = (i < pl.num_programs(0) - 1).astype(F32)
        gt = g[...]
        dcx_ref[...] = _conv_taps_t(gt, gn[...] * has_next, w_ref, 4)
        _conv_wgrad(dw_ref, gt, cx[...], cxp[...] * has_prev, 4)
        db_ref[...] += _colsum(gt)

    return pl.pallas_call(
        body, name="conv_c_bwd", out_shape=(SDS((s, BR), F32), SDS((8, BR), F32), SDS((1, BR), F32)),
        grid=(s // tb,),
        in_specs=[_rows(tb, BR), _next8(tb, BR, s), _rows(tb, BR, CB_CX), _prev8(tb, BR, CB_CX), _const((8, BR))],
        out_specs=(_rows(tb, BR), _const((8, BR)), _const((1, BR))), compiler_params=_params(1),
    )(dxc, dxc, proj, proj, conv_w)


def _s5_disc(lam_re, lam_im, log_dt):
    dt = jnp.exp(log_dt)
    mag = jnp.exp(lam_re * dt)
    ab_re = mag * jnp.cos(lam_im * dt)
    ab_im = mag * jnp.sin(lam_im * dt)
    den = lam_re * lam_re + lam_im * lam_im
    f_re = ((ab_re - 1.0) * lam_re + ab_im * lam_im) / den
    f_im = (ab_im * lam_re - (ab_re - 1.0) * lam_im) / den
    return ab_re, ab_im, f_re, f_im


def _s5_bbar(f_re, f_im, b_re, b_im):
    return f_re * b_re - f_im * b_im, f_re * b_im + f_im * b_re


def _s5_disc_fwd(lam_re, lam_im, log_dt):
    def body(lr, li, ld, o0, o1, o2, o3):
        o0[...], o1[...], o2[...], o3[...] = _s5_disc(lr[...], li[...], ld[...])
    return pl.pallas_call(body, name="s5_disc_fwd", out_shape=(SDS(lam_re.shape, F32),) * 4)(lam_re, lam_im, log_dt)


def _s5_disc_bwd(lam_re, lam_im, log_dt, cts):
    def body(lr, li, ld, c0, c1, c2, c3, o0, o1, o2):
        _, vjp = jax.vjp(_s5_disc, lr[...], li[...], ld[...])
        o0[...], o1[...], o2[...] = vjp((c0[...], c1[...], c2[...], c3[...]))
    return pl.pallas_call(body, name="s5_disc_bwd", out_shape=(SDS(lam_re.shape, F32), SDS(lam_re.shape, F32),
                                                                SDS(log_dt.shape, F32)))(lam_re, lam_im, log_dt, *cts)


def _s5_bbar_fwd(f_re, f_im, b_re, b_im):
    def body(fr, fi, br, bi, o0, o1):
        o0[...], o1[...] = _s5_bbar(fr[...], fi[...], br[...], bi[...])
    return pl.pallas_call(body, name="s5_bbar_fwd", out_shape=(SDS(b_re.shape, F32),) * 2)(f_re, f_im, b_re, b_im)


def _s5_bbar_bwd(f_re, f_im, b_re, b_im, d_re, d_im):
    def body(fr, fi, br, bi, dr, di, o0, o1, o2, o3):
        _, vjp = jax.vjp(_s5_bbar, fr[...], fi[...], br[...], bi[...])
        o0[...], o1[...], o2[...], o3[...] = vjp((dr[...], di[...]))
    col, mat = SDS(f_re.shape, F32), SDS(b_re.shape, F32)
    return pl.pallas_call(body, name="s5_bbar_bwd", out_shape=(col, col, mat, mat))(f_re, f_im, b_re, b_im, d_re, d_im)


def _s5_tail_fwd(ylin, proj, d_skip, w_glu, b_glu, tb):
    s = proj.shape[0]

    def body(yl, u, dg, dk, w_ref, b_ref, o_ref):
        g = jax.nn.gelu(yl[...] + dk[...] * u[...])
        t = jnp.dot(g.astype(MXU_DTYPE), w_ref[...], preferred_element_type=F32) + b_ref[...]
        o_ref[...] = (g * jax.nn.sigmoid(t) * _silu(dg[...])).astype(MXU_DTYPE)

    return pl.pallas_call(
        body, name="s5_tail_fwd", out_shape=SDS((s, BR), MXU_DTYPE), grid=(s // tb,),
        in_specs=[_rows(tb, BR), _rows(tb, BR, CB_DU), _rows(tb, BR, CB_DG), _const((1, BR)), _const((BR, BR)),
                  _const((1, BR))],
        out_specs=_rows(tb, BR), compiler_params=_params(1))(ylin, proj, proj, d_skip, w_glu, b_glu)


def _s5_tail_bwd(dycat, ylin, proj, d_skip, w_glu, b_glu, tb):
    s = proj.shape[0]

    def body(dy, yl, u, dg, dk, w_ref, b_ref, dyl_ref, dus_ref, ddg_ref, g_ref, dt_ref, ddk_ref, dbg_ref):
        _init_acc(ddk_ref, dbg_ref)
        g, gelu_vjp = jax.vjp(jax.nn.gelu, yl[...] + dk[...] * u[...])
        gb = g.astype(MXU_DTYPE)
        sg = jax.nn.sigmoid(jnp.dot(gb, w_ref[...], preferred_element_type=F32) + b_ref[...])
        dz = dy[...] * _silu(dg[...])
        ddg_ref[...] = dy[...] * g * sg * _dsilu(dg[...])
        dt = dz * g * sg * (1.0 - sg)
        dtb = dt.astype(MXU_DTYPE)
        dgel = dz * sg + lax.dot_general(dtb, w_ref[...], (((1,), (1,)), ((), ())), preferred_element_type=F32)
        dyv, = gelu_vjp(dgel)
        dyl_ref[...] = dyv
        dus_ref[...] = dyv * dk[...]
        g_ref[...] = gb
        dt_ref[...] = dtb
        ddk_ref[...] += _colsum(dyv * u[...])
        dbg_ref[...] += _colsum(dt)

    big, half, vec = SDS((s, BR), F32), SDS((s, BR), MXU_DTYPE), SDS((1, BR), F32)
    return pl.pallas_call(
        body, name="s5_tail_bwd", out_shape=(big, big, big, half, half, vec, vec), grid=(s // tb,),
        in_specs=[_rows(tb, BR, 3), _rows(tb, BR), _rows(tb, BR, CB_DU), _rows(tb, BR, CB_DG), _const((1, BR)),
                  _const((BR, BR)), _const((1, BR))],
        out_specs=(_rows(tb, BR),) * 5 + (_const((1, BR)), _const((1, BR))), compiler_params=_params(1),
    )(dycat, ylin, proj, proj, d_skip, w_glu, b_glu)


def _s5_da(lmb, x, tb):
    s, ch2 = x.shape
    ch = ch2 // 2

    def body(l_ref, x_ref, xp_ref, o_ref):
        _init_acc(o_ref)
        has_prev = (pl.program_id(0) > 0).astype(F32)
        xprev = _shift_down(x_ref[...], xp_ref[...] * has_prev, 1)
        lr, li, xr, xi = l_ref[:, 0:ch], l_ref[:, ch:ch2], xprev[:, 0:ch], xprev[:, ch:ch2]
        o_ref[:, 0:ch] += _colsum(lr * xr + li * xi)
        o_ref[:, ch:ch2] += _colsum(li * xr - lr * xi)

    return pl.pallas_call(body, name="s5_da", out_shape=SDS((1, ch2), F32), grid=(s // tb,),
                          in_specs=[_rows(tb, ch2), _rows(tb, ch2), _prev8(tb, ch2)], out_specs=_const((1, ch2)),
                          compiler_params=_params(1))(lmb, x, x)


def _assemble_dproj(da, dqkv, dbg, dcx, dcg, du, dus, ddg, tb):
    s = da.shape[0]

    def body(da_ref, q0, q1, q2, k0, k1, k2, v0, v1, v2, dbg_ref, dcx_ref, dcg_ref, du_ref, dus_ref, ddg_ref, o_ref):
        o_ref[:, 0:4 * BR] = da_ref[...]
        for j, parts in enumerate(((q0, q1, q2), (k0, k1, k2), (v0, v1, v2))):
            o_ref[:, (4 + j) * BR:(5 + j) * BR] = (parts[0][...] + parts[1][...] + parts[2][...]).astype(MXU_DTYPE)
        o_ref[:, 7 * BR:8 * BR] = dbg_ref[...].astype(MXU_DTYPE)
        o_ref[:, 8 * BR:9 * BR] = dcx_ref[...].astype(MXU_DTYPE)
        o_ref[:, 9 * BR:10 * BR] = dcg_ref[...].astype(MXU_DTYPE)
        o_ref[:, 10 * BR:11 * BR] = (du_ref[...] + dus_ref[...]).astype(MXU_DTYPE)
        o_ref[:, 11 * BR:12 * BR] = ddg_ref[...].astype(MXU_DTYPE)

    flat = [t for grp in dqkv for t in grp]
    return pl.pallas_call(
        body, name="assemble_dproj", out_shape=SDS((s, N_IN), MXU_DTYPE), grid=(s // tb,),
        in_specs=[_rows(tb, 4 * BR)] + [_rows(tb, BR)] * 15, out_specs=_rows(tb, N_IN),
        compiler_params=_params(1))(da, *flat, dbg, dcx, dcg, du, dus, ddg)


def _sum_leading(x, tr, name):
    n, r, c = x.shape
    tr = min(tr, r)
    assert r % tr == 0, (name, r, tr)

    def body(*refs):
        acc = refs[0][...].astype(F32)
        for ref in refs[1:n]:
            acc = acc + ref[...].astype(F32)
        refs[n][...] = acc

    specs = [pl.BlockSpec((None, tr, c), functools.partial(lambda i, k: (k, i, 0), k=k)) for k in range(n)]
    return pl.pallas_call(body, name=name, out_shape=SDS((r, c), F32), grid=(r // tr,), in_specs=specs,
                          out_specs=pl.BlockSpec((tr, c), lambda i: (i, 0)), compiler_params=_params(1))(*([x] * n))


def _adamw(w, g_parts, m, v, tr, name):
    r, c = w.shape
    tr = min(tr, r)
    n = len(g_parts)
    assert r % tr == 0, (name, r, tr)

    def body(*refs):
        w_ref, m_ref, v_ref = refs[0], refs[1 + n], refs[2 + n]
        g_ref, d_ref, nm_ref, nv_ref = refs[3 + n:]
        g = refs[1][...]
        for ref in refs[2:1 + n]:
            g = g + ref[...]
        mm = ADAM_B1 * m_ref[...] + (1.0 - ADAM_B1) * g
        vv = ADAM_B2 * v_ref[...] + (1.0 - ADAM_B2) * jnp.square(g)
        m_hat = mm / (1.0 - ADAM_B1 ** ADAM_STEP)
        v_hat = vv / (1.0 - ADAM_B2 ** ADAM_STEP)
        g_ref[...] = g
        d_ref[...] = -ADAM_LR * (m_hat / (jnp.sqrt(v_hat) + ADAM_EPS) + ADAM_WD * w_ref[...])
        nm_ref[...] = mm
        nv_ref[...] = vv

    spec = pl.BlockSpec((tr, c), lambda i: (i, 0))
    return pl.pallas_call(body, name=name, out_shape=(SDS((r, c), F32),) * 4, grid=(r // tr,),
                          in_specs=[spec] * (3 + n), out_specs=(spec,) * 4,
                          compiler_params=_params(1))(w, *g_parts, m, v)


def _allgather8(block, name):
    m_per, n = block.shape

    def body(x_ref, out_ref, send_sems, recv_sems, local_sem):
        x, y, c = lax.axis_index("x"), lax.axis_index("y"), lax.axis_index("c")
        me, sibling = (x, y, c), (x, y, 1 - c)
        chips = [(1 - x, y), (x, 1 - y), (1 - x, 1 - y)]

        def rows(px, py, pc):
            return out_ref.at[pl.ds((4 * px + 2 * py + pc) * m_per, m_per), :]

        def copy(k, blk, to, src=None):
            return pltpu.make_async_remote_copy(
                src_ref=rows(*blk) if src is None else src, dst_ref=rows(*blk), send_sem=send_sems.at[k],
                recv_sem=recv_sems.at[k], device_id=to, device_id_type=MESH)

        mine = pltpu.make_async_copy(x_ref, rows(*me), local_sem)
        mine.start()
        first = [copy(0, me, sibling, src=x_ref)]
        first += [copy(1 + j, me, (*chip, c), src=x_ref) for j, chip in enumerate(chips)]
        for cp in first:
            cp.start()
        passed = [copy(4 + j, (*chip, c), sibling) for j, chip in enumerate(chips)]
        for j, chip in enumerate(chips):
            copy(1 + j, (*chip, c), me).wait_recv()
            passed[j].start()
        copy(0, sibling, me).wait_recv()
        for j, chip in enumerate(chips):
            copy(4 + j, (*chip, 1 - c), me).wait_recv()
        for cp in first + passed:
            cp.wait_send()
        mine.wait()

    return pl.pallas_call(
        body, name=name, out_shape=SDS((N_DEV * m_per, n), block.dtype),
        in_specs=[pl.BlockSpec(memory_space=pltpu.VMEM)], out_specs=pl.BlockSpec(memory_space=pltpu.VMEM),
        scratch_shapes=[pltpu.SemaphoreType.DMA((7,)), pltpu.SemaphoreType.DMA((7,)), pltpu.SemaphoreType.DMA],
        compiler_params=_params())(block)


def _chip_exchange(items, out_shapes, name):
    n, n_out = len(items), len(out_shapes)

    def body(*refs):
        ins, outs = refs[:n], refs[n:n + n_out]
        send_sems, recv_sems, local_sems = refs[n + n_out:]
        c = lax.axis_index("c")
        chip = 2 * lax.axis_index("x") + lax.axis_index("y")

        def remote(a, src, dst, to, from_):
            return pltpu.make_async_remote_copy(
                src_ref=src, dst_ref=dst, send_sem=send_sems.at[a * N_CHIPS + to],
                recv_sem=recv_sems.at[a * N_CHIPS + from_], device_id=(to // 2, to % 2, c), device_id_type=MESH)

        for m in range(N_CHIPS):
            @pl.when(chip == m)
            def _():
                others = [j for j in range(N_CHIPS) if j != m]
                local, sends = [], []
                for a, (_, oi, src_of, dst_of) in enumerate(items):
                    local.append(pltpu.make_async_copy(src_of(ins[a], m), dst_of(outs[oi], m), local_sems.at[a]))
                    local[-1].start()
                    for j in others:
                        sends.append(remote(a, src_of(ins[a], j), dst_of(outs[oi], m), j, m))
                        sends[-1].start()
                for a, (_, oi, src_of, dst_of) in enumerate(items):
                    for j in others:
                        remote(a, src_of(ins[a], m), dst_of(outs[oi], j), j, j).wait_recv()
                for cp in sends:
                    cp.wait_send()
                for cp in local:
                    cp.wait()

    return pl.pallas_call(
        body, name=name, out_shape=tuple(out_shapes), in_specs=[ANY] * n, out_specs=(ANY,) * n_out,
        scratch_shapes=[pltpu.SemaphoreType.DMA((n * N_CHIPS,)), pltpu.SemaphoreType.DMA((n * N_CHIPS,)),
                        pltpu.SemaphoreType.DMA((n,))],
        compiler_params=_params())(*[it[0] for it in items])


def _sibling_swap(arrays, name):
    n = len(arrays)

    def body(*refs):
        ins, outs = refs[:n], refs[n:2 * n]
        send_sems, recv_sems = refs[2 * n:]
        peer = (lax.axis_index("x"), lax.axis_index("y"), 1 - lax.axis_index("c"))
        cps = [pltpu.make_async_remote_copy(src_ref=ins[a], dst_ref=outs[a], send_sem=send_sems.at[a],
                                            recv_sem=recv_sems.at[a], device_id=peer, device_id_type=MESH)
               for a in range(n)]
        for cp in cps:
            cp.start()
        for cp in cps:
            cp.wait()

    return pl.pallas_call(
        body, name=name, out_shape=tuple(SDS(a.shape, a.dtype) for a in arrays), in_specs=[ANY] * n,
        out_specs=(ANY,) * n, scratch_shapes=[pltpu.SemaphoreType.DMA((n,)), pltpu.SemaphoreType.DMA((n,))],
        compiler_params=_params())(*arrays)


def _block_diag(w):
    h, n, m = w.shape
    eye = jnp.eye(h, dtype=w.dtype)
    return (w[:, :, None, :] * eye[:, None, :, None]).reshape(h * n, h * m)


def _diag_blocks(d, h):
    n, m = d.shape[0] // h, d.shape[1] // h
    idx = jnp.arange(h)
    return d.reshape(h, n, h, m)[idx, :, idx, :]


def _tiles(s):
    return dict(tb=min(512, s), tln=min(256, s), tscan=min(256, s))


def _layer_weights(p, l):
    pad8 = lambda w: jnp.pad(w, ((0, 8 - w.shape[0]), (0, 0)))
    return dict(
        conv_a=pad8(p["conv_a"][l]), conv_c=pad8(p["conv_c"][l]), conv_c_b=p["conv_c_b"][l][None],
        w_cat=jnp.concatenate([_block_diag(p["lru_wa"][l]), _block_diag(p["lru_wx"][l])], axis=1).astype(MXU_DTYPE),
        b_cat=jnp.concatenate([p["lru_ba"][l], p["lru_bx"][l]])[None], lam=p["lru_lambda"][l][None],
        lam_re=p["s5_lam_re"][l], lam_im=p["s5_lam_im"][l], log_dt=p["s5_log_dt"][l][:, None],
        b_re=p["s5_b_re"][l].reshape(S5_N, S5_CH), b_im=p["s5_b_im"][l].reshape(S5_N, S5_CH),
        c_re=p["s5_c_re"][l], c_im=p["s5_c_im"][l], d_skip=p["s5_d"][l][None], b_glu=p["s5_b_glu"][l][None],
        ln_g=p["ln_g"][l][None], ln_b=p["ln_b"][l][None])


def _s5_matrices(lw):
    ab_re, ab_im, f_re, f_im = _s5_disc_fwd(lw["lam_re"], lw["lam_im"], lw["log_dt"])
    f_re, f_im = f_re.reshape(S5_N, 1), f_im.reshape(S5_N, 1)
    bb_re, bb_im = _s5_bbar_fwd(f_re, f_im, lw["b_re"], lw["b_im"])
    to_bd = lambda bb: _block_diag(jnp.swapaxes(bb.reshape(S5_GROUPS, S5_STATE, S5_CH), 1, 2))
    bmat = jnp.concatenate([to_bd(bb_re), to_bd(bb_im)], axis=1).astype(MXU_DTYPE)
    cmat_t = jnp.concatenate([_block_diag(lw["c_re"]), -_block_diag(lw["c_im"])], axis=1).astype(MXU_DTYPE)
    a_row = jnp.concatenate([ab_re.reshape(1, S5_N), ab_im.reshape(1, S5_N)], axis=1)
    return dict(f_re=f_re, f_im=f_im, bmat=bmat, bmat_t=bmat.T, cmat_t=cmat_t, cmat=cmat_t.T, a_row=a_row)


def _layer_fwd(x, ada, w_in, w_out, w_glu, lw, s5m, bias_tabs):
    s = x.shape[0]
    t = _tiles(s)
    tb = t["tb"]
    shift, scale, gate = ada
    h = _modulate(x, scale, shift, tb)
    proj = _mm(h, w_in, name="in_proj", tk=D_MODEL)
    y_a = _branch_a_fwd(proj, lw["conv_a"], tb)
    os_, lses = [], []
    for g, (_, dil) in enumerate(DILATIONS):
        o, lse = _attn_fwd(proj, bias_tabs[g], dil)
        os_.append(o)
        lses.append(lse)
    y_b = _attn_combine(os_, lses, proj, tb)
    lru_a, lru_b = _lru_gates_fwd(proj, lw["conv_c"], lw["conv_c_b"], lw["w_cat"], lw["b_cat"], lw["lam"], tb)
    lru_h = _scan_real(lru_a, lru_b, reverse=False, tb=tb, name="lru_scan")
    y_c = _gate_out(lru_h, proj, CB_CG, tb, "lru_out")
    bu = _mm(proj, s5m["bmat"], name="s5_bu", a_col0=CB_DU * BR, a_ncols=BR, tn=1024)
    s5_x = _scan_cplx(bu, s5m["a_row"], reverse=False, tb=t["tscan"], name="s5_scan")
    ylin = _mm(s5_x, s5m["cmat"], name="s5_cx", tk=1024)
    y_d = _s5_tail_fwd(ylin, proj, lw["d_skip"], w_glu, lw["b_glu"], tb)
    ycat = jnp.concatenate([y_a, y_b, y_c, y_d], axis=1)
    x_next, xhat, y, rstd = _out_ln(ycat, w_out, x, gate, lw["ln_g"], lw["ln_b"], t["tln"])
    saved = dict(x=x, h=h, proj=proj, os=os_, lses=lses, lru_a=lru_a, lru_h=lru_h, s5_x=s5_x, ylin=ylin, ycat=ycat,
                 xhat=xhat, y=y, rstd=rstd)
    return x_next, saved


def _layer_bwd(dxn, sv, ada, w_in, w_out, w_glu, lw, s5m, bias_tabs, head_ones):
    s = dxn.shape[0]
    t = _tiles(s)
    tb = t["tb"]
    shift, scale, gate = ada
    proj = sv["proj"]
    g = {}
    dyb, dxa, g["ln_g"], g["ln_b"], dgate = _ln_bwd(dxn, sv["xhat"], sv["y"], sv["rstd"], lw["ln_g"], gate, t["tln"])
    g["w_out"] = _mm(sv["ycat"], dyb, name="dw_out", ta=True, out_dtype=WIRE_DTYPE, tn=1024)
    dycat = _mm(dyb, w_out, name="dycat", tb=True, tk=D_MODEL)
    da, dconv_a = _branch_a_bwd(dycat, proj, lw["conv_a"], tb)
    g["conv_a"] = dconv_a[0:3]
    pre = _attn_bwd_pre(dycat, sv["os"], sv["lses"], proj, head_ones, tb)
    dbg, dos, dms = pre[0], pre[1:4], pre[4:7]
    dqkv, dbias = [], []
    for gi, (_, dil) in enumerate(DILATIONS):
        dq, dk, dv, dbi = _attn_bwd(proj, dos[gi], sv["lses"][gi], dms[gi], bias_tabs[gi], dil)
        dqkv.append((dq, dk, dv))
        dbias.append(dbi)
    dqkv = list(zip(*dqkv))
    dh, dcg = _gate_out_bwd(dycat, 2, sv["lru_h"], proj, CB_CG, tb, "lru_out_bwd")
    lmb = _scan_real(sv["lru_a"], dh, reverse=True, tb=tb, name="lru_scan_bwd")
    dxc, dpre, xcb, dbcat, dlam = _lru_gates_bwd(proj, lmb, sv["lru_h"], lw["conv_c"], lw["conv_c_b"], lw["w_cat"],
                                                  lw["b_cat"], lw["lam"], tb)
    dwcat = _mm(xcb, dpre, name="dw_lru", ta=True, tn=1024)
    g["lru_wa"] = _diag_blocks(dwcat[:, 0:BR], LRU_HEADS)
    g["lru_wx"] = _diag_blocks(dwcat[:, BR:2 * BR], LRU_HEADS)
    g["lru_ba"], g["lru_bx"], g["lru_lambda"] = dbcat[0, 0:BR], dbcat[0, BR:2 * BR], dlam[0]
    dcx, dconv_c, dccb = _conv_c_bwd(dxc, proj, lw["conv_c"], tb)
    g["conv_c"], g["conv_c_b"] = dconv_c[0:4], dccb[0]
    dyl, dus, ddg, gb, dtb, ddk, dbglu = _s5_tail_bwd(dycat, sv["ylin"], proj, lw["d_skip"], w_glu, lw["b_glu"], tb)
    g["s5_d"], g["s5_b_glu"] = ddk[0], dbglu[0]
    g["s5_w_glu"] = _mm(gb, dtb, name="dw_glu", ta=True, out_dtype=WIRE_DTYPE)
    dxd = _mm(dyl, s5m["cmat_t"], name="s5_dx", tk=BR, tn=1024)
    s5_l = _scan_cplx(dxd, s5m["a_row"], reverse=True, tb=t["tscan"], name="s5_scan_bwd")
    dab = _s5_da(s5_l, sv["s5_x"], t["tscan"])
    dbmat = _mm(proj, s5_l, name="dw_s5_b", ta=True, a_col0=CB_DU * BR, a_ncols=BR, tn=1024)
    dcmat_t = _mm(dyl, sv["s5_x"], name="dw_s5_c", ta=True, tn=1024)
    du = _mm(s5_l, s5m["bmat_t"], name="s5_du", tk=1024)
    from_bd = lambda dm: jnp.swapaxes(_diag_blocks(dm, S5_GROUPS), 1, 2).reshape(S5_N, S5_CH)
    df_re, df_im, db_re, db_im = _s5_bbar_bwd(s5m["f_re"], s5m["f_im"], lw["b_re"], lw["b_im"],
                                              from_bd(dbmat[:, 0:S5_N]), from_bd(dbmat[:, S5_N:]))
    shp = (S5_GROUPS, S5_STATE)
    g["s5_lam_re"], g["s5_lam_im"], dlog_dt = _s5_disc_bwd(
        lw["lam_re"], lw["lam_im"], lw["log_dt"],
        (dab[:, 0:S5_N].reshape(shp), dab[:, S5_N:].reshape(shp), df_re.reshape(shp), df_im.reshape(shp)))
    g["s5_log_dt"] = dlog_dt[:, 0]
    g["s5_b_re"] = db_re.reshape(S5_GROUPS, S5_STATE, S5_CH)
    g["s5_b_im"] = db_im.reshape(S5_GROUPS, S5_STATE, S5_CH)
    g["s5_c_re"] = _diag_blocks(dcmat_t[:, 0:S5_N], S5_GROUPS)
    g["s5_c_im"] = -_diag_blocks(dcmat_t[:, S5_N:], S5_GROUPS)
    dproj = _assemble_dproj(da, dqkv, dbg, dcx, dcg, du, dus, ddg, tb)
    g["w_in"] = _mm(sv["h"], dproj, name="dw_in", ta=True, out_dtype=WIRE_DTYPE, tn=1536)
    dhm = _mm(dproj, w_in, name="dh", tb=True, tk=1536)
    dx, dshift, dscale = _mod_bwd(dhm, dxa, sv["x"], scale, tb)
    g["ada"] = jnp.concatenate([dshift[0], dscale[0], dgate[0]])
    return dx, g, dbias


SMALL = ("rel_bias", "conv_a", "conv_c", "conv_c_b", "lru_wa", "lru_ba", "lru_wx", "lru_bx", "lru_lambda",
         "s5_lam_re", "s5_lam_im", "s5_log_dt", "s5_b_re", "s5_b_im", "s5_c_re", "s5_c_im", "s5_d", "s5_b_glu",
         "ln_g", "ln_b")
PER_LAYER_SMALL = SMALL[1:]


def _local_step(x, target, ada_rows, w_in, w_out, w_glu, p):
    s = x.shape[0]
    buckets = _bucket_maps()
    bias_tabs = _bias_tables(p["rel_bias"], buckets)
    head_ones = _block_diag(jnp.ones((ATT_HEADS, HEAD_DIM, HEAD_DIM), MXU_DTYPE))
    lws = [_layer_weights(p, l) for l in range(DEPTH)]
    s5ms = [_s5_matrices(lw) for lw in lws]
    adas = [tuple(ada_rows[l, k * D_MODEL:(k + 1) * D_MODEL][None] for k in range(3)) for l in range(DEPTH)]
    saved = []
    for l in range(DEPTH):
        x, sv = _layer_fwd(x, adas[l], w_in[l], w_out[l], w_glu[l], lws[l], s5ms[l], bias_tabs)
        saved.append(sv)
    loss, dx = _loss_head(x, target, _tiles(s)["tb"])
    grads = [None] * DEPTH
    dbias_sum = []
    for l in reversed(range(DEPTH)):
        dx, grads[l], dbias = _layer_bwd(dx, saved[l], adas[l], w_in[l], w_out[l], w_glu[l], lws[l], s5ms[l],
                                         bias_tabs, head_ones)
        dbias_sum.append(jnp.stack(dbias))
    drel = _rel_bias_grad(jnp.stack(dbias_sum), buckets)[:, 0:ATT_HEADS]
    small = {n: jnp.stack([grads[l][n] for l in range(DEPTH)]) for n in PER_LAYER_SMALL + ("ada",)}
    small["rel_bias"] = drel
    big = {n: [grads[l][n] for l in range(DEPTH)] for n in ("w_in", "w_out", "s5_w_glu")}
    return loss, dx, big, small


PACK_ROWS = 256


def _pack(parts):
    flat = jnp.concatenate([t.reshape(-1).astype(F32) for t in parts])
    n = flat.shape[0]
    rows = -(-n // (PACK_ROWS * 128)) * PACK_ROWS
    return jnp.pad(flat, (0, rows * 128 - n)).reshape(rows, 128)


def _unpack(packed, shapes):
    flat = packed.reshape(packed.shape[:-2] + (-1,))
    out, off = [], 0
    for shp in shapes:
        size = math.prod(shp)
        out.append(flat[..., off:off + size].reshape(flat.shape[:-1] + tuple(shp)))
        off += size
    return out


def _take_cols(t, chip, width):
    return lax.dynamic_slice_in_dim(t, chip * width, width, axis=t.ndim - 1)


def kernel(x, c, rel_bias, w_ada, b_ada, w_in, conv_a, conv_c, conv_c_b, lru_wa, lru_ba, lru_wx, lru_bx, lru_lambda, s5_lam_re, s5_lam_im, s5_log_dt, s5_b_re, s5_b_im, s5_c_re, s5_c_im, s5_d, s5_w_glu, s5_b_glu, w_out, ln_g, ln_b, loss_target, m_rel_bias, m_w_ada, m_b_ada, m_w_in, m_conv_a, m_conv_c, m_conv_c_b, m_lru_wa, m_lru_ba, m_lru_wx, m_lru_bx, m_lru_lambda, m_s5_lam_re, m_s5_lam_im, m_s5_log_dt, m_s5_b_re, m_s5_b_im, m_s5_c_re, m_s5_c_im, m_s5_d, m_s5_w_glu, m_s5_b_glu, m_w_out, m_ln_g, m_ln_b, v_rel_bias, v_w_ada, v_b_ada, v_w_in, v_conv_a, v_conv_c, v_conv_c_b, v_lru_wa, v_lru_ba, v_lru_wx, v_lru_bx, v_lru_lambda, v_s5_lam_re, v_s5_lam_im, v_s5_log_dt, v_s5_b_re, v_s5_b_im, v_s5_c_re, v_s5_c_im, v_s5_d, v_s5_w_glu, v_s5_b_glu, v_w_out, v_ln_g, v_ln_b):
    args = dict(locals())
    names = ("rel_bias", "w_ada", "b_ada", "w_in", "conv_a", "conv_c", "conv_c_b", "lru_wa", "lru_ba", "lru_wx",
             "lru_bx", "lru_lambda", "s5_lam_re", "s5_lam_im", "s5_log_dt", "s5_b_re", "s5_b_im", "s5_c_re", "s5_c_im",
             "s5_d", "s5_w_glu", "s5_b_glu", "w_out", "ln_g", "ln_b")
    w = {n: args[n] for n in names}
    mom = {n: args["m_" + n] for n in names}
    var = {n: args["v_" + n] for n in names}
    chip = 2 * lax.axis_index("x") + lax.axis_index("y")
    me = 2 * chip + lax.axis_index("c")
    ada_w = 3 * D_MODEL // N_CHIPS
    in_w = N_IN // N_CHIPS
    out_r = D_MODEL // N_CHIPS
    glu_r = BR // N_CHIPS
    conv_w = BR // N_CHIPS

    cols = lambda width: (lambda ref, j: ref.at[:, :, pl.ds(j * width, width)])
    rows = lambda height: (lambda ref, j: ref.at[:, pl.ds(j * height, height), :])
    whole = lambda ref, j: ref
    gather = [(w["w_in"].astype(WIRE_DTYPE), 0, whole, cols(in_w)),
              (w["w_out"].astype(WIRE_DTYPE), 1, whole, rows(out_r)),
              (w["s5_w_glu"].astype(WIRE_DTYPE), 2, whole, rows(glu_r))]
    full_shapes = [SDS((DEPTH, D_MODEL, N_IN), WIRE_DTYPE), SDS((DEPTH, D_MODEL, D_MODEL), WIRE_DTYPE),
                   SDS((DEPTH, BR, BR), WIRE_DTYPE)]
    w_in_f, w_out_f, w_glu_f = _chip_exchange(gather, full_shapes, "gather_weights")

    taps = jnp.concatenate([w["conv_a"].reshape(DEPTH * 3, conv_w), w["conv_c"].reshape(DEPTH * 4, conv_w)])
    first = jnp.concatenate([c, jnp.pad(taps, ((0, 1), (0, D_MODEL - conv_w)))])
    got = _allgather8(first, "gather_c_taps").reshape(N_CHIPS, 2, 16, D_MODEL)
    c_all = got[:, :, 0].reshape(N_DEV, D_MODEL)
    taps_all = jnp.transpose(got[:, 0, 1:1 + DEPTH * 7, 0:conv_w], (1, 0, 2)).reshape(DEPTH * 7, BR)
    conv_a_f = taps_all[0:DEPTH * 3].reshape(DEPTH, 3, BR)
    conv_c_f = taps_all[DEPTH * 3:].reshape(DEPTH, 4, BR)

    cond_all = _silu_rows(c_all)
    ada_part = jnp.stack([_mm(cond_all, w["w_ada"][l], name="ada_fwd", tk=D_MODEL, tn=512,
                              bias=_take_cols(w["b_ada"][l][None], chip, ada_w)) for l in range(DEPTH)])
    ada_all = _allgather8(ada_part.reshape(DEPTH * N_DEV, ada_w), "gather_ada")
    ada_all = ada_all.reshape(N_CHIPS, 2, DEPTH, N_DEV, ada_w)[:, 0]
    ada_rows = lax.dynamic_index_in_dim(ada_all, me, axis=2, keepdims=False)
    ada_rows = jnp.transpose(ada_rows, (1, 0, 2)).reshape(DEPTH, 3 * D_MODEL)

    p = dict(w)
    p["conv_a"], p["conv_c"] = conv_a_f, conv_c_f
    loss, dx, big, small = _local_step(x[0], loss_target[0], ada_rows, w_in_f, w_out_f, w_glu_f, p)

    to_slot = lambda l: (lambda ref, j: ref.at[j, l])
    scatter, recv_shapes = [], []
    for oi, (name, src_of, shard) in enumerate((
            ("w_in", lambda ref, j: ref.at[:, pl.ds(j * in_w, in_w)], (D_MODEL, in_w)),
            ("w_out", lambda ref, j: ref.at[pl.ds(j * out_r, out_r), :], (out_r, D_MODEL)),
            ("s5_w_glu", lambda ref, j: ref.at[pl.ds(j * glu_r, glu_r), :], (glu_r, BR)))):
        for l in range(DEPTH):
            scatter.append((big[name][l], oi, src_of, to_slot(l)))
        recv_shapes.append(SDS((N_CHIPS, DEPTH) + shard, WIRE_DTYPE))
    recv = _chip_exchange(scatter, recv_shapes, "scatter_grads")
    sums = [_sum_leading(r.reshape(N_CHIPS, -1, r.shape[-1]), 256, "sum_chips") for r in recv]
    others = _sibling_swap(sums, "swap_cores")
    out = {}
    for name, mine, other in zip(("w_in", "w_out", "s5_w_glu"), sums, others):
        shp = w[name].shape
        flat = lambda t: t.reshape(-1, shp[-1])
        res = _adamw(flat(w[name]), [mine, other], flat(mom[name]), flat(var[name]), 128, "adamw_big")
        out[name] = [t.reshape(shp) for t in res]

    small_names = SMALL + ("ada",)
    small["loss"] = loss
    order = small_names + ("loss",)
    shapes = [small[n].shape for n in order]
    gathered = _allgather8(_pack([small[n] for n in order]), "gather_small")
    gathered = gathered.reshape(N_DEV, -1, 128)
    total = dict(zip(order, _unpack(_sum_leading(gathered, PACK_ROWS, "sum_devices"), shapes)))
    d_ada_all = _unpack(gathered, shapes)[order.index("ada")]
    g_small = {n: total[n] for n in SMALL}
    g_small["conv_a"] = _take_cols(total["conv_a"], chip, conv_w)
    g_small["conv_c"] = _take_cols(total["conv_c"], chip, conv_w)
    g_small["b_ada"] = total["ada"]
    g_w_ada = jnp.stack([_mm(cond_all, _take_cols(d_ada_all[:, l], chip, ada_w), name="dw_ada", ta=True, tn=ada_w)
                         for l in range(DEPTH)])
    upd_names = SMALL + ("b_ada",)
    upd_shapes = [w[n].shape for n in upd_names]
    res = _adamw(_pack([w[n] for n in upd_names]), [_pack([g_small[n] for n in upd_names])],
                 _pack([mom[n] for n in upd_names]), _pack([var[n] for n in upd_names]), PACK_ROWS, "adamw_small")
    for k, t in enumerate(res):
        for n, val in zip(upd_names, _unpack(t, upd_shapes)):
            out.setdefault(n, [None] * 4)[k] = val
    shp = w["w_ada"].shape
    flat = lambda t: t.reshape(-1, shp[-1])
    out["w_ada"] = [t.reshape(shp) for t in _adamw(flat(w["w_ada"]), [flat(g_w_ada)], flat(mom["w_ada"]),
                                                  flat(var["w_ada"]), 128, "adamw_ada")]
    return (total["loss"].reshape(()), dx[None]) + tuple(out[n][k] for k in range(4) for n in names)
```

```python
import functools
import math

import jax
import jax.numpy as jnp
from jax import lax
from jax.experimental import pallas as pl
from jax.experimental.pallas import tpu as pltpu

F32 = jnp.float32
MXU_DTYPE = jnp.bfloat16
WIRE_DTYPE = jnp.bfloat16
SDS = jax.ShapeDtypeStruct
MESH = pl.DeviceIdType.MESH
ANY = pl.BlockSpec(memory_space=pl.ANY)
VMEM_LIMIT = 48 * 1024 * 1024

D_MODEL = 2048
DEPTH = 2
BR = 512
ATT_HEADS = 8
HEAD_DIM = 64
DILATIONS = ((128, 1), (512, 4), (2048, 16))
BLK = 128
REL_BUCKETS = 32
REL_MAX_DIST = 2048
LRU_HEADS = 8
LRU_C = 8.0
S5_CH = 16
S5_GROUPS = 32
S5_STATE = 64
S5_N = S5_GROUPS * S5_STATE
N_IN = 12 * BR
ALPHA = (2 * DEPTH) ** 0.25
LN_EPS = 1e-5
NEG = -1e30
ADAM_LR, ADAM_B1, ADAM_B2, ADAM_EPS, ADAM_WD, ADAM_STEP = 0.001, 0.9, 0.999, 1e-08, 0.01, 10
CB_AB, CB_AC, CB_AX, CB_AG, CB_Q, CB_K, CB_V, CB_BG, CB_CX, CB_CG, CB_DU, CB_DG = range(12)
N_CHIPS = 4
N_DEV = 8


def _params(n_axes=0):
    kw = {"dimension_semantics": ("arbitrary",) * n_axes} if n_axes else {}
    return pltpu.CompilerParams(vmem_limit_bytes=VMEM_LIMIT, **kw)


def _rows(tb, w, cb=0):
    return pl.BlockSpec((tb, w), lambda i: (i, cb))


def _prev8(tb, w, cb=0):
    return pl.BlockSpec((8, w), lambda i: (jnp.maximum(i * (tb // 8) - 1, 0), cb))


def _next8(tb, w, n_rows, cb=0):
    return pl.BlockSpec((8, w), lambda i: (jnp.minimum((i + 1) * (tb // 8), n_rows // 8 - 1), cb))


def _const(shape):
    return pl.BlockSpec(shape, lambda *_: (0,) * len(shape))


def _silu(x):
    return x * jax.nn.sigmoid(x)


def _dsilu(x):
    s = jax.nn.sigmoid(x)
    return s * (1.0 + x * (1.0 - s))


def _shift_down(cur, prev8, j):
    rolled = pltpu.roll(cur, j, 0)
    row = lax.broadcasted_iota(jnp.int32, (8, cur.shape[1]), 0)
    first = jnp.where(row < j, pltpu.roll(prev8, j, 0), rolled[0:8])
    return jnp.concatenate([first, rolled[8:]], axis=0)


def _shift_up(cur, next8, j):
    t = cur.shape[0]
    rolled = pltpu.roll(cur, t - j, 0)
    row = lax.broadcasted_iota(jnp.int32, (8, cur.shape[1]), 0)
    last = jnp.where(row >= 8 - j, pltpu.roll(next8, 8 - j, 0), rolled[t - 8:t])
    return jnp.concatenate([rolled[:t - 8], last], axis=0)


def _colsum(x):
    return jnp.sum(x, axis=0, keepdims=True)


def _init_acc(*refs):
    @pl.when(pl.program_id(0) == 0)
    def _():
        for r in refs:
            r[...] = jnp.zeros_like(r)


def _mm(a, b, *, name, ta=False, tb=False, out_dtype=F32, tm=512, tn=512, tk=512, a_col0=0, a_ncols=None, bias=None):
    a_ncols = a.shape[1] - a_col0 if a_ncols is None else a_ncols
    m, k = (a_ncols, a.shape[0]) if ta else (a.shape[0], a_ncols)
    n = b.shape[0] if tb else b.shape[1]
    assert k == (b.shape[1] if tb else b.shape[0]), (name, a.shape, b.shape)
    tm, tn, tk = min(tm, m), min(tn, n), min(tk, k)
    nk = k // tk
    a_off = a_col0 // (tm if ta else tk)
    assert m % tm == 0 and n % tn == 0 and k % tk == 0 and a_col0 % (tm if ta else tk) == 0, (name, m, n, k)

    def body(*refs):
        if bias is None:
            a_ref, b_ref, o_ref, acc = refs
        else:
            a_ref, b_ref, bias_ref, o_ref, acc = refs
        kk = pl.program_id(2)

        @pl.when(kk == 0)
        def _():
            acc[...] = jnp.zeros_like(acc)

        dims = (((0 if ta else 1,), (1 if tb else 0,)), ((), ()))
        acc[...] += lax.dot_general(a_ref[...].astype(MXU_DTYPE), b_ref[...].astype(MXU_DTYPE), dims,
                                    preferred_element_type=F32)

        @pl.when(kk == nk - 1)
        def _():
            r = acc[...]
            if bias is not None:
                r = r + bias_ref[...]
            o_ref[...] = r.astype(out_dtype)

    a_spec = (pl.BlockSpec((tk, tm), lambda i, j, kk: (kk, i + a_off)) if ta
              else pl.BlockSpec((tm, tk), lambda i, j, kk: (i, kk + a_off)))
    b_spec = (pl.BlockSpec((tn, tk), lambda i, j, kk: (j, kk)) if tb
              else pl.BlockSpec((tk, tn), lambda i, j, kk: (kk, j)))
    in_specs, args = [a_spec, b_spec], [a, b]
    if bias is not None:
        in_specs.append(pl.BlockSpec((1, tn), lambda i, j, kk: (0, j)))
        args.append(bias)
    return pl.pallas_call(
        body, name=name, out_shape=SDS((m, n), out_dtype), grid=(m // tm, n // tn, nk), in_specs=in_specs,
        out_specs=pl.BlockSpec((tm, tn), lambda i, j, kk: (i, j)), scratch_shapes=[pltpu.VMEM((tm, tn), F32)],
        compiler_params=_params(3))(*args)


def _silu_rows(c_all):
    def body(c_ref, o_ref):
        o_ref[...] = _silu(c_ref[...])
    return pl.pallas_call(body, name="cond_silu", out_shape=SDS(c_all.shape, F32))(c_all)


def _modulate(x, scale, shift, tb):
    s, d = x.shape

    def body(x_ref, sc_ref, sh_ref, o_ref):
        o_ref[...] = (x_ref[...] * (1.0 + sc_ref[...]) + sh_ref[...]).astype(MXU_DTYPE)

    return pl.pallas_call(body, name="modulate", out_shape=SDS((s, d), MXU_DTYPE), grid=(s // tb,),
                          in_specs=[_rows(tb, d), _const((1, d)), _const((1, d))], out_specs=_rows(tb, d),
                          compiler_params=_params(1))(x, scale, shift)


def _out_ln(ycat, w_out, x, gate, ln_g, ln_b, tb):
    s, d = x.shape

    def body(yc_ref, w_ref, x_ref, gt_ref, g_ref, b_ref, xn_ref, xh_ref, y_ref, rs_ref):
        y = jnp.dot(yc_ref[...], w_ref[...], preferred_element_type=F32)
        res = ALPHA * x_ref[...] + (1.0 + gt_ref[...]) * y
        mu = jnp.mean(res, axis=-1, keepdims=True)
        cen = res - mu
        var = jnp.mean(cen * cen, axis=-1, keepdims=True)
        rstd = lax.rsqrt(var + LN_EPS)
        xhat = cen * rstd
        xn_ref[...] = xhat * g_ref[...] + b_ref[...]
        xh_ref[...] = xhat
        y_ref[...] = y
        rs_ref[...] = rstd

    big = SDS((s, d), F32)
    return pl.pallas_call(
        body, name="out_proj_ln", out_shape=(big, big, big, SDS((s, 1), F32)), grid=(s // tb,),
        in_specs=[_rows(tb, d), _const((d, d)), _rows(tb, d), _const((1, d)), _const((1, d)), _const((1, d))],
        out_specs=(_rows(tb, d), _rows(tb, d), _rows(tb, d), _rows(tb, 1)), compiler_params=_params(1),
    )(ycat, w_out, x, gate, ln_g, ln_b)


def _ln_bwd(dxn, xhat, y, rstd, ln_g, gate, tb):
    s, d = dxn.shape

    def body(dxn_ref, xh_ref, y_ref, rs_ref, g_ref, gt_ref, dy_ref, dxa_ref, dg_ref, db_ref, dgt_ref):
        _init_acc(dg_ref, db_ref, dgt_ref)
        dxn_t, xh = dxn_ref[...], xh_ref[...]
        dxh = dxn_t * g_ref[...]
        dres = rs_ref[...] * (dxh - jnp.mean(dxh, axis=-1, keepdims=True)
                              - xh * jnp.mean(dxh * xh, axis=-1, keepdims=True))
        dy_ref[...] = ((1.0 + gt_ref[...]) * dres).astype(MXU_DTYPE)
        dxa_ref[...] = ALPHA * dres
        dg_ref[...] += _colsum(dxn_t * xh)
        db_ref[...] += _colsum(dxn_t)
        dgt_ref[...] += _colsum(dres * y_ref[...])

    vec = SDS((1, d), F32)
    return pl.pallas_call(
        body, name="ln_bwd", out_shape=(SDS((s, d), MXU_DTYPE), SDS((s, d), F32), vec, vec, vec), grid=(s // tb,),
        in_specs=[_rows(tb, d), _rows(tb, d), _rows(tb, d), _rows(tb, 1), _const((1, d)), _const((1, d))],
        out_specs=(_rows(tb, d), _rows(tb, d), _const((1, d)), _const((1, d)), _const((1, d))),
        compiler_params=_params(1))(dxn, xhat, y, rstd, ln_g, gate)


def _mod_bwd(dh, dxa, x, scale, tb):
    s, d = dh.shape

    def body(dh_ref, dxa_ref, x_ref, sc_ref, dx_ref, dsh_ref, dsc_ref):
        _init_acc(dsh_ref, dsc_ref)
        dh_t = dh_ref[...]
        dx_ref[...] = dxa_ref[...] + dh_t * (1.0 + sc_ref[...])
        dsh_ref[...] += _colsum(dh_t)
        dsc_ref[...] += _colsum(dh_t * x_ref[...])

    vec = SDS((1, d), F32)
    return pl.pallas_call(
        body, name="mod_bwd", out_shape=(SDS((s, d), F32), vec, vec), grid=(s // tb,),
        in_specs=[_rows(tb, d), _rows(tb, d), _rows(tb, d), _const((1, d))],
        out_specs=(_rows(tb, d), _const((1, d)), _const((1, d))), compiler_params=_params(1))(dh, dxa, x, scale)


def _loss_head(y, target, tb):
    s, d = y.shape

    def body(y_ref, t_ref, l_ref, dy_ref):
        _init_acc(l_ref)
        err = y_ref[...] - t_ref[...]
        l_ref[...] += (0.5 / d) * jnp.sum(err * err, keepdims=True)
        dy_ref[...] = err * (1.0 / d)

    return pl.pallas_call(body, name="loss_head", out_shape=(SDS((1, 1), F32), SDS((s, d), F32)), grid=(s // tb,),
                          in_specs=[_rows(tb, d), _rows(tb, d)], out_specs=(_const((1, 1)), _rows(tb, d)),
                          compiler_params=_params(1))(y, target)


def _conv_taps(u, up, w_ref, width):
    out = w_ref[width - 1:width, :] * u
    for j in range(width - 2, -1, -1):
        out = out + w_ref[j:j + 1, :] * _shift_down(u, up, width - 1 - j)
    return out


def _conv_taps_t(g, gn, w_ref, width):
    out = w_ref[width - 1:width, :] * g
    for j in range(width - 2, -1, -1):
        out = out + w_ref[j:j + 1, :] * _shift_up(g, gn, width - 1 - j)
    return out


def _conv_wgrad(dw_ref, g, u, up, width):
    dw_ref[width - 1:width, :] += _colsum(g * u)
    for j in range(width - 1):
        dw_ref[j:j + 1, :] += _colsum(g * _shift_down(u, up, width - 1 - j))


def _branch_a_fwd(proj, conv_w, tb):
    s = proj.shape[0]

    def body(ab, ac, ax, ag, acp, axp, w_ref, o_ref):
        has_prev = (pl.program_id(0) > 0).astype(F32)
        u = ac[...] * ax[...]
        up = acp[...] * axp[...] * has_prev
        o_ref[...] = (ab[...] * _conv_taps(u, up, w_ref, 3) * _silu(ag[...])).astype(MXU_DTYPE)

    return pl.pallas_call(
        body, name="branch_a_fwd", out_shape=SDS((s, BR), MXU_DTYPE), grid=(s // tb,),
        in_specs=[_rows(tb, BR, CB_AB), _rows(tb, BR, CB_AC), _rows(tb, BR, CB_AX), _rows(tb, BR, CB_AG),
                  _prev8(tb, BR, CB_AC), _prev8(tb, BR, CB_AX), _const((8, BR))],
        out_specs=_rows(tb, BR), compiler_params=_params(1))(proj, proj, proj, proj, proj, proj, conv_w)


def _branch_a_bwd(dycat, proj, conv_w, tb):
    s = proj.shape[0]

    def body(dy, dyn, ab, abn, ag, agn, ac, acp, ax, axp, w_ref, o_ref, dw_ref):
        _init_acc(dw_ref)
        i = pl.program_id(0)
        has_prev = (i > 0).astype(F32)
        has_next = (i < pl.num_programs(0) - 1).astype(F32)
        u = ac[...] * ax[...]
        up = acp[...] * axp[...] * has_prev
        v = _conv_taps(u, up, w_ref, 3)
        sg = _silu(ag[...])
        dv = dy[...] * ab[...] * sg
        dvn = dyn[...] * abn[...] * _silu(agn[...]) * has_next
        du = _conv_taps_t(dv, dvn, w_ref, 3)
        o_ref[:, 0:BR] = (dy[...] * v * sg).astype(MXU_DTYPE)
        o_ref[:, BR:2 * BR] = (du * ax[...]).astype(MXU_DTYPE)
        o_ref[:, 2 * BR:3 * BR] = (du * ac[...]).astype(MXU_DTYPE)
        o_ref[:, 3 * BR:4 * BR] = (dy[...] * ab[...] * v * _dsilu(ag[...])).astype(MXU_DTYPE)
        _conv_wgrad(dw_ref, dv, u, up, 3)

    return pl.pallas_call(
        body, name="branch_a_bwd", out_shape=(SDS((s, 4 * BR), MXU_DTYPE), SDS((8, BR), F32)), grid=(s // tb,),
        in_specs=[_rows(tb, BR, 0), _next8(tb, BR, s, 0),
                  _rows(tb, BR, CB_AB), _next8(tb, BR, s, CB_AB), _rows(tb, BR, CB_AG), _next8(tb, BR, s, CB_AG),
                  _rows(tb, BR, CB_AC), _prev8(tb, BR, CB_AC), _rows(tb, BR, CB_AX), _prev8(tb, BR, CB_AX),
                  _const((8, BR))],
        out_specs=(_rows(tb, 4 * BR), _const((8, BR))), compiler_params=_params(1),
    )(dycat, dycat, proj, proj, proj, proj, proj, proj, proj, proj, conv_w)


def _t5_bucket(dist):
    max_exact = REL_BUCKETS // 2
    nf = jnp.maximum(dist, 1).astype(F32)
    large = max_exact + (jnp.log(nf / max_exact) / math.log(REL_MAX_DIST / max_exact)
                         * (REL_BUCKETS - max_exact)).astype(jnp.int32)
    large = jnp.minimum(large, REL_BUCKETS - 1)
    return jnp.where(dist < max_exact, dist, large)


def _bucket_maps():
    maps = []
    i = jnp.arange(BLK)[:, None]
    j = jnp.arange(2 * BLK)[None, :]
    delta = i + BLK - j
    for window, dil in DILATIONS:
        span = window // dil
        bucket = _t5_bucket(jnp.clip(delta, 0, span) * dil)
        maps.append(jnp.where((delta >= 0) & (delta <= span), bucket, -1))
    return jnp.stack(maps).astype(jnp.int32)


def _bias_tables(rel_bias, buckets):
    tab = jnp.transpose(rel_bias[jnp.maximum(buckets, 0)], (0, 3, 1, 2))
    return jnp.where((buckets >= 0)[:, None], tab, NEG).astype(F32)


def _head_masks():
    lane = lax.broadcasted_iota(jnp.int32, (1, 2 * HEAD_DIM), 1)
    return [(lane < HEAD_DIM).astype(F32), (lane >= HEAD_DIM).astype(F32)]


def _strided(base, size, dil):
    return pl.ds(base, size, stride=dil) if dil > 1 else pl.ds(pl.multiple_of(base, BLK), size)


def _attn_groups(s, dil):
    return max(1, min(1024, s) // (dil * BLK)) if dil == 1 else max(1, min(2048, s) // (dil * BLK))


def _attn_fwd(proj, bias, dil):
    s = proj.shape[0]
    grp = _attn_groups(s, dil)
    u1 = dil * BLK
    unit = grp * u1
    nb = s // unit
    w = 2 * HEAD_DIM
    q0, k0, v0 = (cb * (BR // w) for cb in (CB_Q, CB_K, CB_V))

    def body(q_ref, kc_ref, kp_ref, vc_ref, vp_ref, bias_ref, o_ref, lse_ref, kbuf, vbuf):
        n = pl.program_id(1)
        col = lax.broadcasted_iota(jnp.int32, (1, 2 * BLK), 1)
        masks = _head_masks()
        kbuf[0:u1, :] = kp_ref[...]
        kbuf[u1:, :] = kc_ref[...]
        vbuf[0:u1, :] = vp_ref[...]
        vbuf[u1:, :] = vc_ref[...]

        def per_r(t, carry):
            j = t // dil
            base = j * u1 + t % dil
            rows = _strided(base, BLK, dil)
            no_prev = jnp.where((n == 0) & (j == 0) & (col < BLK), NEG, 0.0)
            q = q_ref[rows, :] * (HEAD_DIM ** -0.5)
            k = kbuf[_strided(base, 2 * BLK, dil), :].astype(MXU_DTYPE)
            v = vbuf[_strided(base, 2 * BLK, dil), :].astype(MXU_DTYPE)
            o_acc = jnp.zeros((BLK, w), F32)
            lse_acc = jnp.zeros((BLK, w), F32)
            for h in range(2):
                qh = (q * masks[h]).astype(MXU_DTYPE)
                sc = lax.dot_general(qh, k, (((1,), (1,)), ((), ())), preferred_element_type=F32)
                sc = sc + bias_ref[h] + no_prev
                mx = jnp.max(sc, axis=-1, keepdims=True)
                p = jnp.exp(sc - mx)
                l = jnp.sum(p, axis=-1, keepdims=True)
                oh = jnp.dot((p / l).astype(MXU_DTYPE), v, preferred_element_type=F32)
                o_acc = o_acc + oh * masks[h]
                lse_acc = lse_acc + (mx + jnp.log(l)) * masks[h]
            o_ref[rows, :] = o_acc
            lse_ref[rows, :] = lse_acc
            return carry

        lax.fori_loop(0, grp * dil, per_r, 0, unroll=2)

    cur = lambda c0: pl.BlockSpec((unit, w), lambda hp, n: (n, c0 + hp))
    prev = lambda c0: pl.BlockSpec((u1, w), lambda hp, n: (jnp.maximum(n * grp - 1, 0), c0 + hp))
    out = pl.BlockSpec((unit, w), lambda hp, n: (n, hp))
    return pl.pallas_call(
        body, name=f"attn_fwd_d{dil}", out_shape=(SDS((s, BR), F32), SDS((s, BR), F32)), grid=(BR // w, nb),
        in_specs=[cur(q0), cur(k0), prev(k0), cur(v0), prev(v0),
                  pl.BlockSpec((2, BLK, 2 * BLK), lambda hp, n: (hp, 0, 0))],
        out_specs=(out, out),
        scratch_shapes=[pltpu.VMEM((unit + u1, w), F32), pltpu.VMEM((unit + u1, w), F32)],
        compiler_params=_params(2))(proj, proj, proj, proj, proj, bias)


def _softmax3(l0, l1, l2):
    mx = jnp.maximum(jnp.maximum(l0, l1), l2)
    e0, e1, e2 = jnp.exp(l0 - mx), jnp.exp(l1 - mx), jnp.exp(l2 - mx)
    inv = 1.0 / (e0 + e1 + e2)
    return e0 * inv, e1 * inv, e2 * inv


def _attn_combine(os_, lses, proj, tb):
    s = proj.shape[0]

    def body(o0, o1, o2, l0, l1, l2, bg, y_ref):
        w0, w1, w2 = _softmax3(l0[...], l1[...], l2[...])
        attn = w0 * o0[...] + w1 * o1[...] + w2 * o2[...]
        y_ref[...] = (attn * _silu(bg[...])).astype(MXU_DTYPE)

    return pl.pallas_call(
        body, name="attn_combine", out_shape=SDS((s, BR), MXU_DTYPE), grid=(s // tb,),
        in_specs=[_rows(tb, BR)] * 6 + [_rows(tb, BR, CB_BG)], out_specs=_rows(tb, BR),
        compiler_params=_params(1))(*os_, *lses, proj)


def _attn_bwd_pre(dycat, os_, lses, proj, head_ones, tb):
    s = proj.shape[0]

    def body(dy, o0, o1, o2, l0, l1, l2, bg, e_ref, dbg_ref, do0, do1, do2, dm0, dm1, dm2):
        w0, w1, w2 = _softmax3(l0[...], l1[...], l2[...])
        attn = w0 * o0[...] + w1 * o1[...] + w2 * o2[...]
        dattn = dy[...] * _silu(bg[...])
        dbg_ref[...] = dy[...] * attn * _dsilu(bg[...])
        prod = dattn * attn
        hi = prod.astype(MXU_DTYPE)
        lo = (prod - hi.astype(F32)).astype(MXU_DTYPE)
        tot = (jnp.dot(hi, e_ref[...], preferred_element_type=F32)
               + jnp.dot(lo, e_ref[...], preferred_element_type=F32))
        for wg, do_ref, dm_ref in ((w0, do0, dm0), (w1, do1, dm1), (w2, do2, dm2)):
            do_ref[...] = wg * dattn
            dm_ref[...] = wg * tot

    big = SDS((s, BR), F32)
    return pl.pallas_call(
        body, name="attn_bwd_pre", out_shape=(big,) * 7, grid=(s // tb,),
        in_specs=[_rows(tb, BR, 1)] + [_rows(tb, BR)] * 6 + [_rows(tb, BR, CB_BG), _const((BR, BR))],
        out_specs=(_rows(tb, BR),) * 7, compiler_params=_params(1))(dycat, *os_, *lses, proj, head_ones)


def _attn_bwd(proj, do, lse, dm, bias, dil):
    s = proj.shape[0]
    grp = _attn_groups(s, dil)
    u1 = dil * BLK
    unit = grp * u1
    nb = s // unit
    w = 2 * HEAD_DIM
    q0, k0, v0 = (cb * (BR // w) for cb in (CB_Q, CB_K, CB_V))

    def body(q_ref, kc_ref, kp_ref, vc_ref, vp_ref, do_ref, lse_ref, dm_ref, bias_ref,
             dq_ref, dk_ref, dv_ref, dbias_ref, kbuf, vbuf, stage_k, stage_v):
        n = pl.program_id(1)
        col = lax.broadcasted_iota(jnp.int32, (1, 2 * BLK), 1)
        masks = _head_masks()

        @pl.when(n == 0)
        def _():
            dbias_ref[...] = jnp.zeros_like(dbias_ref)
            stage_k[...] = jnp.zeros_like(stage_k)
            stage_v[...] = jnp.zeros_like(stage_v)

        for out_ref, stage in ((dk_ref, stage_k), (dv_ref, stage_v)):
            if grp > 1:
                out_ref[0:unit - u1, :] = stage[u1:unit, :]
            stage[0:u1, :] = stage[unit:unit + u1, :]

        @pl.when(n < nb)
        def _():
            kbuf[0:u1, :] = kp_ref[...]
            kbuf[u1:, :] = kc_ref[...]
            vbuf[0:u1, :] = vp_ref[...]
            vbuf[u1:, :] = vc_ref[...]

            def per_r(t, carry):
                j = t // dil
                base = j * u1 + t % dil
                rows = _strided(base, BLK, dil)
                rows_hi = _strided(base + u1, BLK, dil)
                no_prev = jnp.where((n == 0) & (j == 0) & (col < BLK), NEG, 0.0)
                q = q_ref[rows, :] * (HEAD_DIM ** -0.5)
                k = kbuf[_strided(base, 2 * BLK, dil), :].astype(MXU_DTYPE)
                v = vbuf[_strided(base, 2 * BLK, dil), :].astype(MXU_DTYPE)
                do_t, lse_t, dm_t = do_ref[rows, :], lse_ref[rows, :], dm_ref[rows, :]
                dq_acc = jnp.zeros((BLK, w), F32)
                dk_acc = jnp.zeros((2 * BLK, w), F32)
                dv_acc = jnp.zeros((2 * BLK, w), F32)
                for h in range(2):
                    qh = (q * masks[h]).astype(MXU_DTYPE)
                    doh = (do_t * masks[h]).astype(MXU_DTYPE)
                    c0 = h * HEAD_DIM
                    sc = lax.dot_general(qh, k, (((1,), (1,)), ((), ())), preferred_element_type=F32)
                    p = jnp.exp(sc + bias_ref[h] + no_prev - lse_t[:, c0:c0 + 1])
                    dp = lax.dot_general(doh, v, (((1,), (1,)), ((), ())), preferred_element_type=F32)
                    ds = p * (dp - dm_t[:, c0:c0 + 1])
                    dbias_ref[h] += ds
                    dsb, pb = ds.astype(MXU_DTYPE), p.astype(MXU_DTYPE)
                    dq_acc = dq_acc + jnp.dot(dsb, k, preferred_element_type=F32) * masks[h]
                    dk_acc = dk_acc + lax.dot_general(dsb, qh, (((0,), (0,)), ((), ())), preferred_element_type=F32)
                    dv_acc = dv_acc + lax.dot_general(pb, doh, (((0,), (0,)), ((), ())), preferred_element_type=F32)
                dq_ref[rows, :] = dq_acc * (HEAD_DIM ** -0.5)
                stage_k[rows, :] = stage_k[rows, :] + dk_acc[0:BLK]
                stage_v[rows, :] = stage_v[rows, :] + dv_acc[0:BLK]
                stage_k[rows_hi, :] = dk_acc[BLK:2 * BLK]
                stage_v[rows_hi, :] = dv_acc[BLK:2 * BLK]
                return carry

            lax.fori_loop(0, grp * dil, per_r, 0, unroll=2)

        dk_ref[unit - u1:unit, :] = stage_k[0:u1, :]
        dv_ref[unit - u1:unit, :] = stage_v[0:u1, :]

    qn = lambda n: jnp.minimum(n, nb - 1)
    cur = lambda c0: pl.BlockSpec((unit, w), lambda hp, n: (qn(n), c0 + hp))
    prev = lambda c0: pl.BlockSpec((u1, w), lambda hp, n: (jnp.maximum(qn(n) * grp - 1, 0), c0 + hp))
    row = pl.BlockSpec((unit, w), lambda hp, n: (qn(n), hp))
    late = pl.BlockSpec((unit, w), lambda hp, n: (jnp.maximum(n - 1, 0), hp))
    tab = pl.BlockSpec((2, BLK, 2 * BLK), lambda hp, n: (hp, 0, 0))
    big = SDS((s, BR), F32)
    return pl.pallas_call(
        body, name=f"attn_bwd_d{dil}", out_shape=(big, big, big, SDS((ATT_HEADS, BLK, 2 * BLK), F32)),
        grid=(BR // w, nb + 1),
        in_specs=[cur(q0), cur(k0), prev(k0), cur(v0), prev(v0), row, row, row, tab],
        out_specs=(row, late, late, tab),
        scratch_shapes=[pltpu.VMEM((unit + u1, w), F32)] * 4,
        compiler_params=_params(2))(proj, proj, proj, proj, proj, do, lse, dm, bias)


def _rel_bias_grad(dbias, buckets):
    def body(db_ref, bk_ref, o_ref):
        row = lax.broadcasted_iota(jnp.int32, (REL_BUCKETS, 128), 0)
        lane = lax.broadcasted_iota(jnp.int32, (REL_BUCKETS, 128), 1)

        def per_bucket(b, acc):
            for g in range(len(DILATIONS)):
                hit = bk_ref[g] == b
                for h in range(ATT_HEADS):
                    both = db_ref[0, g, h] + db_ref[1, g, h]
                    val = jnp.sum(jnp.where(hit, both, 0.0), keepdims=True)
                    acc = acc + jnp.where((row == b) & (lane == h), val, 0.0)
            return acc

        o_ref[...] = lax.fori_loop(0, REL_BUCKETS, per_bucket, jnp.zeros((REL_BUCKETS, 128), F32))

    assert dbias.shape[0] == DEPTH == 2
    return pl.pallas_call(body, name="rel_bias_grad", out_shape=SDS((REL_BUCKETS, 128), F32),
                          compiler_params=_params())(dbias, buckets)


def _scan_real(a, b, *, reverse, tb, name):
    s, ch = a.shape
    nt = s // tb
    order = range(7, -1, -1) if reverse else range(8)

    def body(a_ref, b_ref, o_ref, carry):
        @pl.when(pl.program_id(0) == 0)
        def _():
            carry[...] = jnp.zeros_like(carry)

        def group(gi, h):
            r0 = pl.multiple_of((tb // 8 - 1 - gi if reverse else gi) * 8, 8)
            a8, b8 = a_ref[pl.ds(r0, 8), :], b_ref[pl.ds(r0, 8), :]
            rows = [None] * 8
            for k in order:
                if reverse:
                    rows[k] = b8[k:k + 1] + h
                    h = a8[k:k + 1] * rows[k]
                else:
                    h = a8[k:k + 1] * h + b8[k:k + 1]
                    rows[k] = h
            o_ref[pl.ds(r0, 8), :] = jnp.concatenate(rows, axis=0)
            return h

        carry[...] = lax.fori_loop(0, tb // 8, group, carry[...])

    spec = pl.BlockSpec((tb, ch), (lambda i: (nt - 1 - i, 0)) if reverse else (lambda i: (i, 0)))
    return pl.pallas_call(body, name=name, out_shape=SDS((s, ch), F32), grid=(nt,), in_specs=[spec, spec],
                          out_specs=spec, scratch_shapes=[pltpu.VMEM((1, ch), F32)],
                          compiler_params=_params(1))(a, b)


def _scan_cplx(b, a_row, *, reverse, tb, name):
    s, ch2 = b.shape
    ch = ch2 // 2
    nt = s // tb
    order = range(7, -1, -1) if reverse else range(8)

    def body(a_ref, b_ref, o_ref, carry):
        @pl.when(pl.program_id(0) == 0)
        def _():
            carry[...] = jnp.zeros_like(carry)

        ar = a_ref[:, 0:ch]
        ai = -a_ref[:, ch:ch2] if reverse else a_ref[:, ch:ch2]

        def group(gi, x):
            xr, xi = x
            r0 = pl.multiple_of((tb // 8 - 1 - gi if reverse else gi) * 8, 8)
            br8, bi8 = b_ref[pl.ds(r0, 8), 0:ch], b_ref[pl.ds(r0, 8), ch:ch2]
            rr, ri = [None] * 8, [None] * 8
            for k in order:
                nr = ar * xr - ai * xi + br8[k:k + 1]
                ni = ar * xi + ai * xr + bi8[k:k + 1]
                xr, xi = nr, ni
                rr[k], ri[k] = xr, xi
            o_ref[pl.ds(r0, 8), 0:ch] = jnp.concatenate(rr, axis=0)
            o_ref[pl.ds(r0, 8), ch:ch2] = jnp.concatenate(ri, axis=0)
            return xr, xi

        xr, xi = lax.fori_loop(0, tb // 8, group, (carry[:, 0:ch], carry[:, ch:ch2]))
        carry[:, 0:ch] = xr
        carry[:, ch:ch2] = xi

    spec = pl.BlockSpec((tb, ch2), (lambda i: (nt - 1 - i, 0)) if reverse else (lambda i: (i, 0)))
    return pl.pallas_call(body, name=name, out_shape=SDS((s, ch2), F32), grid=(nt,),
                          in_specs=[_const((1, ch2)), spec], out_specs=spec,
                          scratch_shapes=[pltpu.VMEM((1, ch2), F32)], compiler_params=_params(1))(a_row, b)


def _neg_expm1(z):
    series = -z * (1.0 + z * (0.5 + z * (1.0 / 6 + z * (1.0 / 24 + z * (1.0 / 120)))))
    return jnp.where(z > -0.05, series, 1.0 - jnp.exp(z))


def _lru_gate(xc, pre_r, pre_i, lam):
    log_a = -LRU_C * jax.nn.sigmoid(pre_r) * jax.nn.softplus(-lam)
    return jnp.exp(log_a), jnp.sqrt(_neg_expm1(2.0 * log_a)) * jax.nn.sigmoid(pre_i) * xc


def _lru_gates_fwd(proj, conv_w, conv_b, w_cat, b_cat, lam, tb):
    s = proj.shape[0]

    def body(cx, cxp, w_ref, cb_ref, wc_ref, bc_ref, lam_ref, a_ref, b_ref):
        has_prev = (pl.program_id(0) > 0).astype(F32)
        xc = _conv_taps(cx[...], cxp[...] * has_prev, w_ref, 4) + cb_ref[...]
        pre = jnp.dot(xc.astype(MXU_DTYPE), wc_ref[...], preferred_element_type=F32) + bc_ref[...]
        a_ref[...], b_ref[...] = _lru_gate(xc, pre[:, 0:BR], pre[:, BR:2 * BR], lam_ref[...])

    big = SDS((s, BR), F32)
    return pl.pallas_call(
        body, name="lru_gates_fwd", out_shape=(big, big), grid=(s // tb,),
        in_specs=[_rows(tb, BR, CB_CX), _prev8(tb, BR, CB_CX), _const((8, BR)), _const((1, BR)),
                  _const((BR, 2 * BR)), _const((1, 2 * BR)), _const((1, BR))],
        out_specs=(_rows(tb, BR), _rows(tb, BR)), compiler_params=_params(1),
    )(proj, proj, conv_w, conv_b, w_cat, b_cat, lam)


def _gate_out(h, proj, cb, tb, name):
    s = proj.shape[0]

    def body(h_ref, g_ref, o_ref):
        o_ref[...] = (h_ref[...] * _silu(g_ref[...])).astype(MXU_DTYPE)

    return pl.pallas_call(body, name=name, out_shape=SDS((s, BR), MXU_DTYPE), grid=(s // tb,),
                          in_specs=[_rows(tb, BR), _rows(tb, BR, cb)], out_specs=_rows(tb, BR),
                          compiler_params=_params(1))(h, proj)


def _gate_out_bwd(dycat, dy_cb, h, proj, cb, tb, name):
    s = proj.shape[0]

    def body(dy, h_ref, g_ref, dh_ref, dg_ref):
        dh_ref[...] = dy[...] * _silu(g_ref[...])
        dg_ref[...] = dy[...] * h_ref[...] * _dsilu(g_ref[...])

    big = SDS((s, BR), F32)
    return pl.pallas_call(body, name=name, out_shape=(big, big), grid=(s // tb,),
                          in_specs=[_rows(tb, BR, dy_cb), _rows(tb, BR), _rows(tb, BR, cb)],
                          out_specs=(_rows(tb, BR), _rows(tb, BR)), compiler_params=_params(1))(dycat, h, proj)


def _lru_gates_bwd(proj, lmb, h, conv_w, conv_b, w_cat, b_cat, lam, tb):
    s = proj.shape[0]

    def body(cx, cxp, l_ref, h_ref, hp_ref, w_ref, cb_ref, wc_ref, bc_ref, lam_ref,
             dxc_ref, dpre_ref, xc_ref, dbc_ref, dlam_ref):
        _init_acc(dbc_ref, dlam_ref)
        has_prev = (pl.program_id(0) > 0).astype(F32)
        xc = _conv_taps(cx[...], cxp[...] * has_prev, w_ref, 4) + cb_ref[...]
        xcb = xc.astype(MXU_DTYPE)
        pre = jnp.dot(xcb, wc_ref[...], preferred_element_type=F32) + bc_ref[...]
        _, vjp = jax.vjp(_lru_gate, xc, pre[:, 0:BR], pre[:, BR:2 * BR], lam_ref[...])
        lm = l_ref[...]
        dxc, dpr, dpi, dlam = vjp((lm * _shift_down(h_ref[...], hp_ref[...] * has_prev, 1), lm))
        dpre = jnp.concatenate([dpr, dpi], axis=1)
        dpreb = dpre.astype(MXU_DTYPE)
        dxc_ref[...] = dxc + lax.dot_general(dpreb, wc_ref[...], (((1,), (1,)), ((), ())),
                                             preferred_element_type=F32)
        dpre_ref[...] = dpreb
        xc_ref[...] = xcb
        dbc_ref[...] += _colsum(dpre)
        dlam_ref[...] += dlam

    return pl.pallas_call(
        body, name="lru_gates_bwd",
        out_shape=(SDS((s, BR), F32), SDS((s, 2 * BR), MXU_DTYPE), SDS((s, BR), MXU_DTYPE),
                   SDS((1, 2 * BR), F32), SDS((1, BR), F32)),
        grid=(s // tb,),
        in_specs=[_rows(tb, BR, CB_CX), _prev8(tb, BR, CB_CX), _rows(tb, BR), _rows(tb, BR), _prev8(tb, BR),
                  _const((8, BR)), _const((1, BR)), _const((BR, 2 * BR)), _const((1, 2 * BR)), _const((1, BR))],
        out_specs=(_rows(tb, BR), _rows(tb, 2 * BR), _rows(tb, BR), _const((1, 2 * BR)), _const((1, BR))),
        compiler_params=_params(1))(proj, proj, lmb, h, h, conv_w, conv_b, w_cat, b_cat, lam)


def _conv_c_bwd(dxc, proj, conv_w, tb):
    s = proj.shape[0]

    def body(g, gn, cx, cxp, w_ref, dcx_ref, dw_ref, db_ref):
        _init_acc(dw_ref, db_ref)
        i = pl.program_id(0)
        has_prev = (i > 0).astype(F32)
        has_next = (i < pl.num_programs(0) - 1).astype(F32)
        gt = g[...]
        dcx_ref[...] = _conv_taps_t(gt, gn[...] * has_next, w_ref, 4)
        _conv_wgrad(dw_ref, gt, cx[...], cxp[...] * has_prev, 4)
        db_ref[...] += _colsum(gt)

    return pl.pallas_call(
        body, name="conv_c_bwd", out_shape=(SDS((s, BR), F32), SDS((8, BR), F32), SDS((1, BR), F32)),
        grid=(s // tb,),
        in_specs=[_rows(tb, BR), _next8(tb, BR, s), _rows(tb, BR, CB_CX), _prev8(tb, BR, CB_CX), _const((8, BR))],
        out_specs=(_rows(tb, BR), _const((8, BR)), _const((1, BR))), compiler_params=_params(1),
    )(dxc, dxc, proj, proj, conv_w)


def _s5_disc(lam_re, lam_im, log_dt):
    dt = jnp.exp(log_dt)
    mag = jnp.exp(lam_re * dt)
    ab_re = mag * jnp.cos(lam_im * dt)
    ab_im = mag * jnp.sin(lam_im * dt)
    den = lam_re * lam_re + lam_im * lam_im
    f_re = ((ab_re - 1.0) * lam_re + ab_im * lam_im) / den
    f_im = (ab_im * lam_re - (ab_re - 1.0) * lam_im) / den
    return ab_re, ab_im, f_re, f_im


def _s5_bbar(f_re, f_im, b_re, b_im):
    return f_re * b_re - f_im * b_im, f_re * b_im + f_im * b_re


def _s5_disc_fwd(lam_re, lam_im, log_dt):
    def body(lr, li, ld, o0, o1, o2, o3):
        o0[...], o1[...], o2[...], o3[...] = _s5_disc(lr[...], li[...], ld[...])
    return pl.pallas_call(body, name="s5_disc_fwd", out_shape=(SDS(lam_re.shape, F32),) * 4)(lam_re, lam_im, log_dt)


def _s5_disc_bwd(lam_re, lam_im, log_dt, cts):
    def body(lr, li, ld, c0, c1, c2, c3, o0, o1, o2):
        _, vjp = jax.vjp(_s5_disc, lr[...], li[...], ld[...])
        o0[...], o1[...], o2[...] = vjp((c0[...], c1[...], c2[...], c3[...]))
    return pl.pallas_call(body, name="s5_disc_bwd", out_shape=(SDS(lam_re.shape, F32), SDS(lam_re.shape, F32),
                                                                SDS(log_dt.shape, F32)))(lam_re, lam_im, log_dt, *cts)


def _s5_bbar_fwd(f_re, f_im, b_re, b_im):
    def body(fr, fi, br, bi, o0, o1):
        o0[...], o1[...] = _s5_bbar(fr[...], fi[...], br[...], bi[...])
    return pl.pallas_call(body, name="s5_bbar_fwd", out_shape=(SDS(b_re.shape, F32),) * 2)(f_re, f_im, b_re, b_im)


def _s5_bbar_bwd(f_re, f_im, b_re, b_im, d_re, d_im):
    def body(fr, fi, br, bi, dr, di, o0, o1, o2, o3):
        _, vjp = jax.vjp(_s5_bbar, fr[...], fi[...], br[...], bi[...])
        o0[...], o1[...], o2[...], o3[...] = vjp((dr[...], di[...]))
    col, mat = SDS(f_re.shape, F32), SDS(b_re.shape, F32)
    return pl.pallas_call(body, name="s5_bbar_bwd", out_shape=(col, col, mat, mat))(f_re, f_im, b_re, b_im, d_re, d_im)


def _s5_tail_fwd(ylin, proj, d_skip, w_glu, b_glu, tb):
    s = proj.shape[0]

    def body(yl, u, dg, dk, w_ref, b_ref, o_ref):
        g = jax.nn.gelu(yl[...] + dk[...] * u[...])
        t = jnp.dot(g.astype(MXU_DTYPE), w_ref[...], preferred_element_type=F32) + b_ref[...]
        o_ref[...] = (g * jax.nn.sigmoid(t) * _silu(dg[...])).astype(MXU_DTYPE)

    return pl.pallas_call(
        body, name="s5_tail_fwd", out_shape=SDS((s, BR), MXU_DTYPE), grid=(s // tb,),
        in_specs=[_rows(tb, BR), _rows(tb, BR, CB_DU), _rows(tb, BR, CB_DG), _const((1, BR)), _const((BR, BR)),
                  _const((1, BR))],
        out_specs=_rows(tb, BR), compiler_params=_params(1))(ylin, proj, proj, d_skip, w_glu, b_glu)


def _s5_tail_bwd(dycat, ylin, proj, d_skip, w_glu, b_glu, tb):
    s = proj.shape[0]

    def body(dy, yl, u, dg, dk, w_ref, b_ref, dyl_ref, dus_ref, ddg_ref, g_ref, dt_ref, ddk_ref, dbg_ref):
        _init_acc(ddk_ref, dbg_ref)
        g, gelu_vjp = jax.vjp(jax.nn.gelu, yl[...] + dk[...] * u[...])
        gb = g.astype(MXU_DTYPE)
        sg = jax.nn.sigmoid(jnp.dot(gb, w_ref[...], preferred_element_type=F32) + b_ref[...])
        dz = dy[...] * _silu(dg[...])
        ddg_ref[...] = dy[...] * g * sg * _dsilu(dg[...])
        dt = dz * g * sg * (1.0 - sg)
        dtb = dt.astype(MXU_DTYPE)
        dgel = dz * sg + lax.dot_general(dtb, w_ref[...], (((1,), (1,)), ((), ())), preferred_element_type=F32)
        dyv, = gelu_vjp(dgel)
        dyl_ref[...] = dyv
        dus_ref[...] = dyv * dk[...]
        g_ref[...] = gb
        dt_ref[...] = dtb
        ddk_ref[...] += _colsum(dyv * u[...])
        dbg_ref[...] += _colsum(dt)

    big, half, vec = SDS((s, BR), F32), SDS((s, BR), MXU_DTYPE), SDS((1, BR), F32)
    return pl.pallas_call(
        body, name="s5_tail_bwd", out_shape=(big, big, big, half, half, vec, vec), grid=(s // tb,),
        in_specs=[_rows(tb, BR, 3), _rows(tb, BR), _rows(tb, BR, CB_DU), _rows(tb, BR, CB_DG), _const((1, BR)),
                  _const((BR, BR)), _const((1, BR))],
        out_specs=(_rows(tb, BR),) * 5 + (_const((1, BR)), _const((1, BR))), compiler_params=_params(1),
    )(dycat, ylin, proj, proj, d_skip, w_glu, b_glu)


def _s5_da(lmb, x, tb):
    s, ch2 = x.shape
    ch = ch2 // 2

    def body(l_ref, x_ref, xp_ref, o_ref):
        _init_acc(o_ref)
        has_prev = (pl.program_id(0) > 0).astype(F32)
        xprev = _shift_down(x_ref[...], xp_ref[...] * has_prev, 1)
        lr, li, xr, xi = l_ref[:, 0:ch], l_ref[:, ch:ch2], xprev[:, 0:ch], xprev[:, ch:ch2]
        o_ref[:, 0:ch] += _colsum(lr * xr + li * xi)
        o_ref[:, ch:ch2] += _colsum(li * xr - lr * xi)

    return pl.pallas_call(body, name="s5_da", out_shape=SDS((1, ch2), F32), grid=(s // tb,),
                          in_specs=[_rows(tb, ch2), _rows(tb, ch2), _prev8(tb, ch2)], out_specs=_const((1, ch2)),
                          compiler_params=_params(1))(lmb, x, x)


def _assemble_dproj(da, dqkv, dbg, dcx, dcg, du, dus, ddg, tb):
    s = da.shape[0]

    def body(da_ref, q0, q1, q2, k0, k1, k2, v0, v1, v2, dbg_ref, dcx_ref, dcg_ref, du_ref, dus_ref, ddg_ref, o_ref):
        o_ref[:, 0:4 * BR] = da_ref[...]
        for j, parts in enumerate(((q0, q1, q2), (k0, k1, k2), (v0, v1, v2))):
            o_ref[:, (4 + j) * BR:(5 + j) * BR] = (parts[0][...] + parts[1][...] + parts[2][...]).astype(MXU_DTYPE)
        o_ref[:, 7 * BR:8 * BR] = dbg_ref[...].astype(MXU_DTYPE)
        o_ref[:, 8 * BR:9 * BR] = dcx_ref[...].astype(MXU_DTYPE)
        o_ref[:, 9 * BR:10 * BR] = dcg_ref[...].astype(MXU_DTYPE)
        o_ref[:, 10 * BR:11 * BR] = (du_ref[...] + dus_ref[...]).astype(MXU_DTYPE)
        o_ref[:, 11 * BR:12 * BR] = ddg_ref[...].astype(MXU_DTYPE)

    flat = [t for grp in dqkv for t in grp]
    return pl.pallas_call(
        body, name="assemble_dproj", out_shape=SDS((s, N_IN), MXU_DTYPE), grid=(s // tb,),
        in_specs=[_rows(tb, 4 * BR)] + [_rows(tb, BR)] * 15, out_specs=_rows(tb, N_IN),
        compiler_params=_params(1))(da, *flat, dbg, dcx, dcg, du, dus, ddg)


def _sum_leading(x, tr, name):
    n, r, c = x.shape
    tr = min(tr, r)
    assert r % tr == 0, (name, r, tr)

    def body(*refs):
        acc = refs[0][...].astype(F32)
        for ref in refs[1:n]:
            acc = acc + ref[...].astype(F32)
        refs[n][...] = acc

    specs = [pl.BlockSpec((None, tr, c), functools.partial(lambda i, k: (k, i, 0), k=k)) for k in range(n)]
    return pl.pallas_call(body, name=name, out_shape=SDS((r, c), F32), grid=(r // tr,), in_specs=specs,
                          out_specs=pl.BlockSpec((tr, c), lambda i: (i, 0)), compiler_params=_params(1))(*([x] * n))


def _adamw(w, g_parts, m, v, tr, name):
    r, c = w.shape
    tr = min(tr, r)
    n = len(g_parts)
    assert r % tr == 0, (name, r, tr)

    def body(*refs):
        w_ref, m_ref, v_ref = refs[0], refs[1 + n], refs[2 + n]
        g_ref, d_ref, nm_ref, nv_ref = refs[3 + n:]
        g = refs[1][...]
        for ref in refs[2:1 + n]:
            g = g + ref[...]
        mm = ADAM_B1 * m_ref[...] + (1.0 - ADAM_B1) * g
        vv = ADAM_B2 * v_ref[...] + (1.0 - ADAM_B2) * jnp.square(g)
        m_hat = mm / (1.0 - ADAM_B1 ** ADAM_STEP)
        v_hat = vv / (1.0 - ADAM_B2 ** ADAM_STEP)
        g_ref[...] = g
        d_ref[...] = -ADAM_LR * (m_hat / (jnp.sqrt(v_hat) + ADAM_EPS) + ADAM_WD * w_ref[...])
        nm_ref[...] = mm
        nv_ref[...] = vv

    spec = pl.BlockSpec((tr, c), lambda i: (i, 0))
    return pl.pallas_call(body, name=name, out_shape=(SDS((r, c), F32),) * 4, grid=(r // tr,),
                          in_specs=[spec] * (3 + n), out_specs=(spec,) * 4,
                          compiler_params=_params(1))(w, *g_parts, m, v)


def _allgather8(block, name):
    m_per, n = block.shape

    def body(x_ref, out_ref, send_sems, recv_sems, local_sem):
        x, y, c = lax.axis_index("x"), lax.axis_index("y"), lax.axis_index("c")
        me, sibling = (x, y, c), (x, y, 1 - c)
        chips = [(1 - x, y), (x, 1 - y), (1 - x, 1 - y)]

        def rows(px, py, pc):
            return out_ref.at[pl.ds((4 * px + 2 * py + pc) * m_per, m_per), :]

        def copy(k, blk, to, src=None):
            return pltpu.make_async_remote_copy(
                src_ref=rows(*blk) if src is None else src, dst_ref=rows(*blk), send_sem=send_sems.at[k],
                recv_sem=recv_sems.at[k], device_id=to, device_id_type=MESH)

        mine = pltpu.make_async_copy(x_ref, rows(*me), local_sem)
        mine.start()
        first = [copy(0, me, sibling, src=x_ref)]
        first += [copy(1 + j, me, (*chip, c), src=x_ref) for j, chip in enumerate(chips)]
        for cp in first:
            cp.start()
        passed = [copy(4 + j, (*chip, c), sibling) for j, chip in enumerate(chips)]
        for j, chip in enumerate(chips):
            copy(1 + j, (*chip, c), me).wait_recv()
            passed[j].start()
        copy(0, sibling, me).wait_recv()
        for j, chip in enumerate(chips):
            copy(4 + j, (*chip, 1 - c), me).wait_recv()
        for cp in first + passed:
            cp.wait_send()
        mine.wait()

    return pl.pallas_call(
        body, name=name, out_shape=SDS((N_DEV * m_per, n), block.dtype),
        in_specs=[pl.BlockSpec(memory_space=pltpu.VMEM)], out_specs=pl.BlockSpec(memory_space=pltpu.VMEM),
        scratch_shapes=[pltpu.SemaphoreType.DMA((7,)), pltpu.SemaphoreType.DMA((7,)), pltpu.SemaphoreType.DMA],
        compiler_params=_params())(block)


def _chip_exchange(items, out_shapes, name):
    n, n_out = len(items), len(out_shapes)

    def body(*refs):
        ins, outs = refs[:n], refs[n:n + n_out]
        send_sems, recv_sems, local_sems = refs[n + n_out:]
        c = lax.axis_index("c")
        chip = 2 * lax.axis_index("x") + lax.axis_index("y")

        def remote(a, src, dst, to, from_):
            return pltpu.make_async_remote_copy(
                src_ref=src, dst_ref=dst, send_sem=send_sems.at[a * N_CHIPS + to],
                recv_sem=recv_sems.at[a * N_CHIPS + from_], device_id=(to // 2, to % 2, c), device_id_type=MESH)

        for m in range(N_CHIPS):
            @pl.when(chip == m)
            def _():
                others = [j for j in range(N_CHIPS) if j != m]
                local, sends = [], []
                for a, (_, oi, src_of, dst_of) in enumerate(items):
                    local.append(pltpu.make_async_copy(src_of(ins[a], m), dst_of(outs[oi], m), local_sems.at[a]))
                    local[-1].start()
                    for j in others:
                        sends.append(remote(a, src_of(ins[a], j), dst_of(outs[oi], m), j, m))
                        sends[-1].start()
                for a, (_, oi, src_of, dst_of) in enumerate(items):
                    for j in others:
                        remote(a, src_of(ins[a], m), dst_of(outs[oi], j), j, j).wait_recv()
                for cp in sends:
                    cp.wait_send()
                for cp in local:
                    cp.wait()

    return pl.pallas_call(
        body, name=name, out_shape=tuple(out_shapes), in_specs=[ANY] * n, out_specs=(ANY,) * n_out,
        scratch_shapes=[pltpu.SemaphoreType.DMA((n * N_CHIPS,)), pltpu.SemaphoreType.DMA((n * N_CHIPS,)),
                        pltpu.SemaphoreType.DMA((n,))],
        compiler_params=_params())(*[it[0] for it in items])


def _sibling_swap(arrays, name):
    n = len(arrays)

    def body(*refs):
        ins, outs = refs[:n], refs[n:2 * n]
        send_sems, recv_sems = refs[2 * n:]
        peer = (lax.axis_index("x"), lax.axis_index("y"), 1 - lax.axis_index("c"))
        cps = [pltpu.make_async_remote_copy(src_ref=ins[a], dst_ref=outs[a], send_sem=send_sems.at[a],
                                            recv_sem=recv_sems.at[a], device_id=peer, device_id_type=MESH)
               for a in range(n)]
        for cp in cps:
            cp.start()
        for cp in cps:
            cp.wait()

    return pl.pallas_call(
        body, name=name, out_shape=tuple(SDS(a.shape, a.dtype) for a in arrays), in_specs=[ANY] * n,
        out_specs=(ANY,) * n, scratch_shapes=[pltpu.SemaphoreType.DMA((n,)), pltpu.SemaphoreType.DMA((n,))],
        compiler_params=_params())(*arrays)


def _block_diag(w):
    h, n, m = w.shape
    eye = jnp.eye(h, dtype=w.dtype)
    return (w[:, :, None, :] * eye[:, None, :, None]).reshape(h * n, h * m)


def _diag_blocks(d, h, col0=0, ncols=None):
    ncols = d.shape[1] - col0 if ncols is None else ncols
    n, m = d.shape[0] // h, ncols // h
    lanes = 128
    assert m <= lanes and lanes % m == 0 and col0 % lanes == 0

    def body(d_ref, o_ref):
        for g in range(h):
            c = col0 + g * m
            chunk = d_ref[g * n:(g + 1) * n, c // lanes * lanes:c // lanes * lanes + lanes]
            o_ref[g * n:(g + 1) * n, :] = chunk[:, c % lanes:c % lanes + m]

    out = pl.pallas_call(body, name="diag_blocks", out_shape=SDS((h * n, m), d.dtype), compiler_params=_params())(d)
    return out.reshape(h, n, m)


def _tiles(s):
    return dict(tb=min(512, s), tln=min(256, s), tscan=min(256, s))


def _layer_weights(p, l):
    pad8 = lambda w: jnp.pad(w, ((0, 8 - w.shape[0]), (0, 0)))
    return dict(
        conv_a=pad8(p["conv_a"][l]), conv_c=pad8(p["conv_c"][l]), conv_c_b=p["conv_c_b"][l][None],
        w_cat=jnp.concatenate([_block_diag(p["lru_wa"][l]), _block_diag(p["lru_wx"][l])], axis=1).astype(MXU_DTYPE),
        b_cat=jnp.concatenate([p["lru_ba"][l], p["lru_bx"][l]])[None], lam=p["lru_lambda"][l][None],
        lam_re=p["s5_lam_re"][l], lam_im=p["s5_lam_im"][l], log_dt=p["s5_log_dt"][l][:, None],
        b_re=p["s5_b_re"][l].reshape(S5_N, S5_CH), b_im=p["s5_b_im"][l].reshape(S5_N, S5_CH),
        c_re=p["s5_c_re"][l], c_im=p["s5_c_im"][l], d_skip=p["s5_d"][l][None], b_glu=p["s5_b_glu"][l][None],
        ln_g=p["ln_g"][l][None], ln_b=p["ln_b"][l][None])


def _s5_matrices(lw):
    ab_re, ab_im, f_re, f_im = _s5_disc_fwd(lw["lam_re"], lw["lam_im"], lw["log_dt"])
    f_re, f_im = f_re.reshape(S5_N, 1), f_im.reshape(S5_N, 1)
    bb_re, bb_im = _s5_bbar_fwd(f_re, f_im, lw["b_re"], lw["b_im"])
    to_bd = lambda bb: _block_diag(jnp.swapaxes(bb.reshape(S5_GROUPS, S5_STATE, S5_CH), 1, 2))
    bmat = jnp.concatenate([to_bd(bb_re), to_bd(bb_im)], axis=1).astype(MXU_DTYPE)
    cmat_t = jnp.concatenate([_block_diag(lw["c_re"]), -_block_diag(lw["c_im"])], axis=1).astype(MXU_DTYPE)
    a_row = jnp.concatenate([ab_re.reshape(1, S5_N), ab_im.reshape(1, S5_N)], axis=1)
    return dict(f_re=f_re, f_im=f_im, bmat=bmat, bmat_t=bmat.T, cmat_t=cmat_t, cmat=cmat_t.T, a_row=a_row)


def _layer_fwd(x, ada, w_in, w_out, w_glu, lw, s5m, bias_tabs):
    s = x.shape[0]
    t = _tiles(s)
    tb = t["tb"]
    shift, scale, gate = ada
    h = _modulate(x, scale, shift, tb)
    proj = _mm(h, w_in, name="in_proj", tk=D_MODEL)
    y_a = _branch_a_fwd(proj, lw["conv_a"], tb)
    os_, lses = [], []
    for g, (_, dil) in enumerate(DILATIONS):
        o, lse = _attn_fwd(proj, bias_tabs[g], dil)
        os_.append(o)
        lses.append(lse)
    y_b = _attn_combine(os_, lses, proj, tb)
    lru_a, lru_b = _lru_gates_fwd(proj, lw["conv_c"], lw["conv_c_b"], lw["w_cat"], lw["b_cat"], lw["lam"], tb)
    lru_h = _scan_real(lru_a, lru_b, reverse=False, tb=tb, name="lru_scan")
    y_c = _gate_out(lru_h, proj, CB_CG, tb, "lru_out")
    bu = _mm(proj, s5m["bmat"], name="s5_bu", a_col0=CB_DU * BR, a_ncols=BR, tn=1024)
    s5_x = _scan_cplx(bu, s5m["a_row"], reverse=False, tb=t["tscan"], name="s5_scan")
    ylin = _mm(s5_x, s5m["cmat"], name="s5_cx", tk=1024)
    y_d = _s5_tail_fwd(ylin, proj, lw["d_skip"], w_glu, lw["b_glu"], tb)
    ycat = jnp.concatenate([y_a, y_b, y_c, y_d], axis=1)
    x_next, xhat, y, rstd = _out_ln(ycat, w_out, x, gate, lw["ln_g"], lw["ln_b"], t["tln"])
    saved = dict(x=x, h=h, proj=proj, os=os_, lses=lses, lru_a=lru_a, lru_h=lru_h, s5_x=s5_x, ylin=ylin, ycat=ycat,
                 xhat=xhat, y=y, rstd=rstd)
    return x_next, saved


def _layer_bwd(dxn, sv, ada, w_in, w_out, w_glu, lw, s5m, bias_tabs, head_ones):
    s = dxn.shape[0]
    t = _tiles(s)
    tb = t["tb"]
    shift, scale, gate = ada
    proj = sv["proj"]
    g = {}
    dyb, dxa, g["ln_g"], g["ln_b"], dgate = _ln_bwd(dxn, sv["xhat"], sv["y"], sv["rstd"], lw["ln_g"], gate, t["tln"])
    g["w_out"] = _mm(sv["ycat"], dyb, name="dw_out", ta=True, out_dtype=WIRE_DTYPE, tn=1024)
    dycat = _mm(dyb, w_out, name="dycat", tb=True, tk=D_MODEL)
    da, dconv_a = _branch_a_bwd(dycat, proj, lw["conv_a"], tb)
    g["conv_a"] = dconv_a[0:3]
    pre = _attn_bwd_pre(dycat, sv["os"], sv["lses"], proj, head_ones, tb)
    dbg, dos, dms = pre[0], pre[1:4], pre[4:7]
    dqkv, dbias = [], []
    for gi, (_, dil) in enumerate(DILATIONS):
        dq, dk, dv, dbi = _attn_bwd(proj, dos[gi], sv["lses"][gi], dms[gi], bias_tabs[gi], dil)
        dqkv.append((dq, dk, dv))
        dbias.append(dbi)
    dqkv = list(zip(*dqkv))
    dh, dcg = _gate_out_bwd(dycat, 2, sv["lru_h"], proj, CB_CG, tb, "lru_out_bwd")
    lmb = _scan_real(sv["lru_a"], dh, reverse=True, tb=tb, name="lru_scan_bwd")
    dxc, dpre, xcb, dbcat, dlam = _lru_gates_bwd(proj, lmb, sv["lru_h"], lw["conv_c"], lw["conv_c_b"], lw["w_cat"],
                                                  lw["b_cat"], lw["lam"], tb)
    dwcat = _mm(xcb, dpre, name="dw_lru", ta=True, tn=1024)
    g["lru_wa"] = _diag_blocks(dwcat, LRU_HEADS, 0, BR)
    g["lru_wx"] = _diag_blocks(dwcat, LRU_HEADS, BR, BR)
    g["lru_ba"], g["lru_bx"], g["lru_lambda"] = dbcat[0, 0:BR], dbcat[0, BR:2 * BR], dlam[0]
    dcx, dconv_c, dccb = _conv_c_bwd(dxc, proj, lw["conv_c"], tb)
    g["conv_c"], g["conv_c_b"] = dconv_c[0:4], dccb[0]
    dyl, dus, ddg, gb, dtb, ddk, dbglu = _s5_tail_bwd(dycat, sv["ylin"], proj, lw["d_skip"], w_glu, lw["b_glu"], tb)
    g["s5_d"], g["s5_b_glu"] = ddk[0], dbglu[0]
    g["s5_w_glu"] = _mm(gb, dtb, name="dw_glu", ta=True, out_dtype=WIRE_DTYPE)
    dxd = _mm(dyl, s5m["cmat_t"], name="s5_dx", tk=BR, tn=1024)
    s5_l = _scan_cplx(dxd, s5m["a_row"], reverse=True, tb=t["tscan"], name="s5_scan_bwd")
    dab = _s5_da(s5_l, sv["s5_x"], t["tscan"])
    dbmat = _mm(proj, s5_l, name="dw_s5_b", ta=True, a_col0=CB_DU * BR, a_ncols=BR, tn=1024)
    dcmat_t = _mm(dyl, sv["s5_x"], name="dw_s5_c", ta=True, tn=1024)
    du = _mm(s5_l, s5m["bmat_t"], name="s5_du", tk=1024)
    from_bd = lambda col0: jnp.swapaxes(_diag_blocks(dbmat, S5_GROUPS, col0, S5_N), 1, 2).reshape(S5_N, S5_CH)
    df_re, df_im, db_re, db_im = _s5_bbar_bwd(s5m["f_re"], s5m["f_im"], lw["b_re"], lw["b_im"],
                                              from_bd(0), from_bd(S5_N))
    shp = (S5_GROUPS, S5_STATE)
    g["s5_lam_re"], g["s5_lam_im"], dlog_dt = _s5_disc_bwd(
        lw["lam_re"], lw["lam_im"], lw["log_dt"],
        (dab[:, 0:S5_N].reshape(shp), dab[:, S5_N:].reshape(shp), df_re.reshape(shp), df_im.reshape(shp)))
    g["s5_log_dt"] = dlog_dt[:, 0]
    g["s5_b_re"] = db_re.reshape(S5_GROUPS, S5_STATE, S5_CH)
    g["s5_b_im"] = db_im.reshape(S5_GROUPS, S5_STATE, S5_CH)
    g["s5_c_re"] = _diag_blocks(dcmat_t, S5_GROUPS, 0, S5_N)
    g["s5_c_im"] = -_diag_blocks(dcmat_t, S5_GROUPS, S5_N, S5_N)
    dproj = _assemble_dproj(da, dqkv, dbg, dcx, dcg, du, dus, ddg, tb)
    g["w_in"] = _mm(sv["h"], dproj, name="dw_in", ta=True, out_dtype=WIRE_DTYPE, tn=1536)
    dhm = _mm(dproj, w_in, name="dh", tb=True, tk=1536)
    dx, dshift, dscale = _mod_bwd(dhm, dxa, sv["x"], scale, tb)
    g["ada"] = jnp.concatenate([dshift[0], dscale[0], dgate[0]])
    return dx, g, dbias


SMALL = ("rel_bias", "conv_a", "conv_c", "conv_c_b", "lru_wa", "lru_ba", "lru_wx", "lru_bx", "lru_lambda",
         "s5_lam_re", "s5_lam_im", "s5_log_dt", "s5_b_re", "s5_b_im", "s5_c_re", "s5_c_im", "s5_d", "s5_b_glu",
         "ln_g", "ln_b")
PER_LAYER_SMALL = SMALL[1:]


def _local_step(x, target, ada_rows, w_in, w_out, w_glu, p):
    s = x.shape[0]
    buckets = _bucket_maps()
    bias_tabs = _bias_tables(p["rel_bias"], buckets)
    head_ones = _block_diag(jnp.ones((ATT_HEADS, HEAD_DIM, HEAD_DIM), MXU_DTYPE))
    lws = [_layer_weights(p, l) for l in range(DEPTH)]
    s5ms = [_s5_matrices(lw) for lw in lws]
    adas = [tuple(ada_rows[l, k * D_MODEL:(k + 1) * D_MODEL][None] for k in range(3)) for l in range(DEPTH)]
    saved = []
    for l in range(DEPTH):
        x, sv = _layer_fwd(x, adas[l], w_in[l], w_out[l], w_glu[l], lws[l], s5ms[l], bias_tabs)
        saved.append(sv)
    loss, dx = _loss_head(x, target, _tiles(s)["tb"])
    grads = [None] * DEPTH
    dbias_sum = []
    for l in reversed(range(DEPTH)):
        dx, grads[l], dbias = _layer_bwd(dx, saved[l], adas[l], w_in[l], w_out[l], w_glu[l], lws[l], s5ms[l],
                                         bias_tabs, head_ones)
        dbias_sum.append(jnp.stack(dbias))
    drel = _rel_bias_grad(jnp.stack(dbias_sum), buckets)[:, 0:ATT_HEADS]
    small = {n: jnp.stack([grads[l][n] for l in range(DEPTH)]) for n in PER_LAYER_SMALL + ("ada",)}
    small["rel_bias"] = drel
    big = {n: [grads[l][n] for l in range(DEPTH)] for n in ("w_in", "w_out", "s5_w_glu")}
    return loss, dx, big, small


PACK_ROWS = 256


def _pack(parts):
    flat = jnp.concatenate([t.reshape(-1).astype(F32) for t in parts])
    n = flat.shape[0]
    rows = -(-n // (PACK_ROWS * 128)) * PACK_ROWS
    return jnp.pad(flat, (0, rows * 128 - n)).reshape(rows, 128)


def _unpack(packed, shapes):
    flat = packed.reshape(packed.shape[:-2] + (-1,))
    out, off = [], 0
    for shp in shapes:
        size = math.prod(shp)
        out.append(flat[..., off:off + size].reshape(flat.shape[:-1] + tuple(shp)))
        off += size
    return out


def _take_cols(t, chip, width):
    return lax.dynamic_slice_in_dim(t, chip * width, width, axis=t.ndim - 1)


def kernel(x, c, rel_bias, w_ada, b_ada, w_in, conv_a, conv_c, conv_c_b, lru_wa, lru_ba, lru_wx, lru_bx, lru_lambda, s5_lam_re, s5_lam_im, s5_log_dt, s5_b_re, s5_b_im, s5_c_re, s5_c_im, s5_d, s5_w_glu, s5_b_glu, w_out, ln_g, ln_b, loss_target, m_rel_bias, m_w_ada, m_b_ada, m_w_in, m_conv_a, m_conv_c, m_conv_c_b, m_lru_wa, m_lru_ba, m_lru_wx, m_lru_bx, m_lru_lambda, m_s5_lam_re, m_s5_lam_im, m_s5_log_dt, m_s5_b_re, m_s5_b_im, m_s5_c_re, m_s5_c_im, m_s5_d, m_s5_w_glu, m_s5_b_glu, m_w_out, m_ln_g, m_ln_b, v_rel_bias, v_w_ada, v_b_ada, v_w_in, v_conv_a, v_conv_c, v_conv_c_b, v_lru_wa, v_lru_ba, v_lru_wx, v_lru_bx, v_lru_lambda, v_s5_lam_re, v_s5_lam_im, v_s5_log_dt, v_s5_b_re, v_s5_b_im, v_s5_c_re, v_s5_c_im, v_s5_d, v_s5_w_glu, v_s5_b_glu, v_w_out, v_ln_g, v_ln_b):
    args = dict(locals())
    names = ("rel_bias", "w_ada", "b_ada", "w_in", "conv_a", "conv_c", "conv_c_b", "lru_wa", "lru_ba", "lru_wx",
             "lru_bx", "lru_lambda", "s5_lam_re", "s5_lam_im", "s5_log_dt", "s5_b_re", "s5_b_im", "s5_c_re", "s5_c_im",
             "s5_d", "s5_w_glu", "s5_b_glu", "w_out", "ln_g", "ln_b")
    w = {n: args[n] for n in names}
    mom = {n: args["m_" + n] for n in names}
    var = {n: args["v_" + n] for n in names}
    chip = 2 * lax.axis_index("x") + lax.axis_index("y")
    me = 2 * chip + lax.axis_index("c")
    ada_w = 3 * D_MODEL // N_CHIPS
    in_w = N_IN // N_CHIPS
    out_r = D_MODEL // N_CHIPS
    glu_r = BR // N_CHIPS
    conv_w = BR // N_CHIPS

    cols = lambda width: (lambda ref, j: ref.at[:, :, pl.ds(j * width, width)])
    rows = lambda height: (lambda ref, j: ref.at[:, pl.ds(j * height, height), :])
    whole = lambda ref, j: ref
    gather = [(w["w_in"].astype(WIRE_DTYPE), 0, whole, cols(in_w)),
              (w["w_out"].astype(WIRE_DTYPE), 1, whole, rows(out_r)),
              (w["s5_w_glu"].astype(WIRE_DTYPE), 2, whole, rows(glu_r))]
    full_shapes = [SDS((DEPTH, D_MODEL, N_IN), WIRE_DTYPE), SDS((DEPTH, D_MODEL, D_MODEL), WIRE_DTYPE),
                   SDS((DEPTH, BR, BR), WIRE_DTYPE)]
    w_in_f, w_out_f, w_glu_f = _chip_exchange(gather, full_shapes, "gather_weights")

    taps = jnp.concatenate([w["conv_a"].reshape(DEPTH * 3, conv_w), w["conv_c"].reshape(DEPTH * 4, conv_w)])
    first = jnp.concatenate([c, jnp.pad(taps, ((0, 1), (0, D_MODEL - conv_w)))])
    got = _allgather8(first, "gather_c_taps").reshape(N_CHIPS, 2, 16, D_MODEL)
    c_all = got[:, :, 0].reshape(N_DEV, D_MODEL)
    taps_all = jnp.transpose(got[:, 0, 1:1 + DEPTH * 7, 0:conv_w], (1, 0, 2)).reshape(DEPTH * 7, BR)
    conv_a_f = taps_all[0:DEPTH * 3].reshape(DEPTH, 3, BR)
    conv_c_f = taps_all[DEPTH * 3:].reshape(DEPTH, 4, BR)

    cond_all = _silu_rows(c_all)
    ada_part = jnp.stack([_mm(cond_all, w["w_ada"][l], name="ada_fwd", tk=D_MODEL, tn=512,
                              bias=_take_cols(w["b_ada"][l][None], chip, ada_w)) for l in range(DEPTH)])
    ada_all = _allgather8(ada_part.reshape(DEPTH * N_DEV, ada_w), "gather_ada")
    ada_all = ada_all.reshape(N_CHIPS, 2, DEPTH, N_DEV, ada_w)[:, 0]
    ada_rows = lax.dynamic_index_in_dim(ada_all, me, axis=2, keepdims=False)
    ada_rows = jnp.transpose(ada_rows, (1, 0, 2)).reshape(DEPTH, 3 * D_MODEL)

    p = dict(w)
    p["conv_a"], p["conv_c"] = conv_a_f, conv_c_f
    loss, dx, big, small = _local_step(x[0], loss_target[0], ada_rows, w_in_f, w_out_f, w_glu_f, p)

    to_slot = lambda l: (lambda ref, j: ref.at[j, l])
    scatter, recv_shapes = [], []
    for oi, (name, src_of, shard) in enumerate((
            ("w_in", lambda ref, j: ref.at[:, pl.ds(j * in_w, in_w)], (D_MODEL, in_w)),
            ("w_out", lambda ref, j: ref.at[pl.ds(j * out_r, out_r), :], (out_r, D_MODEL)),
            ("s5_w_glu", lambda ref, j: ref.at[pl.ds(j * glu_r, glu_r), :], (glu_r, BR)))):
        for l in range(DEPTH):
            scatter.append((big[name][l], oi, src_of, to_slot(l)))
        recv_shapes.append(SDS((N_CHIPS, DEPTH) + shard, WIRE_DTYPE))
    recv = _chip_exchange(scatter, recv_shapes, "scatter_grads")
    sums = [_sum_leading(r.reshape(N_CHIPS, -1, r.shape[-1]), 256, "sum_chips") for r in recv]
    others = _sibling_swap(sums, "swap_cores")
    out = {}
    for name, mine, other in zip(("w_in", "w_out", "s5_w_glu"), sums, others):
        shp = w[name].shape
        flat = lambda t: t.reshape(-1, shp[-1])
        res = _adamw(flat(w[name]), [mine, other], flat(mom[name]), flat(var[name]), 128, "adamw_big")
        out[name] = [t.reshape(shp) for t in res]

    small_names = SMALL + ("ada",)
    small["loss"] = loss
    order = small_names + ("loss",)
    shapes = [small[n].shape for n in order]
    gathered = _allgather8(_pack([small[n] for n in order]), "gather_small")
    gathered = gathered.reshape(N_DEV, -1, 128)
    total = dict(zip(order, _unpack(_sum_leading(gathered, PACK_ROWS, "sum_devices"), shapes)))
    d_ada_all = _unpack(gathered, shapes)[order.index("ada")]
    g_small = {n: total[n] for n in SMALL}
    g_small["conv_a"] = _take_cols(total["conv_a"], chip, conv_w)
    g_small["conv_c"] = _take_cols(total["conv_c"], chip, conv_w)
    g_small["b_ada"] = total["ada"]
    g_w_ada = jnp.stack([_mm(cond_all, _take_cols(d_ada_all[:, l], chip, ada_w), name="dw_ada", ta=True, tn=ada_w)
                         for l in range(DEPTH)])
    upd_names = SMALL + ("b_ada",)
    upd_shapes = [w[n].shape for n in upd_names]
    res = _adamw(_pack([w[n] for n in upd_names]), [_pack([g_small[n] for n in upd_names])],
                 _pack([mom[n] for n in upd_names]), _pack([var[n] for n in upd_names]), PACK_ROWS, "adamw_small")
    for k, t in enumerate(res):
        for n, val in zip(upd_names, _unpack(t, upd_shapes)):
            out.setdefault(n, [None] * 4)[k] = val
    shp = w["w_ada"].shape
    flat = lambda t: t.reshape(-1, shp[-1])
    out["w_ada"] = [t.reshape(shp) for t in _adamw(flat(w["w_ada"]), [flat(g_w_ada)], flat(mom["w_ada"]),
                                                  flat(var["w_ada"]), 128, "adamw_ada")]
    return (total["loss"].reshape(()), dx[None]) + tuple(out[n][k] for k in range(4) for n in names)
```

```python
import functools
import math

import jax
import jax.numpy as jnp
from jax import lax
from jax.experimental import pallas as pl
from jax.experimental.pallas import tpu as pltpu

F32 = jnp.float32
MXU_DTYPE = jnp.bfloat16
WIRE_DTYPE = jnp.bfloat16
SDS = jax.ShapeDtypeStruct
MESH = pl.DeviceIdType.MESH
ANY = pl.BlockSpec(memory_space=pl.ANY)
VMEM_LIMIT = 48 * 1024 * 1024

D_MODEL = 2048
DEPTH = 2
BR = 512
ATT_HEADS = 8
HEAD_DIM = 64
DILATIONS = ((128, 1), (512, 4), (2048, 16))
BLK = 128
REL_BUCKETS = 32
REL_MAX_DIST = 2048
LRU_HEADS = 8
LRU_C = 8.0
S5_CH = 16
S5_GROUPS = 32
S5_STATE = 64
S5_N = S5_GROUPS * S5_STATE
N_IN = 12 * BR
ALPHA = (2 * DEPTH) ** 0.25
LN_EPS = 1e-5
NEG = -1e30
ADAM_LR, ADAM_B1, ADAM_B2, ADAM_EPS, ADAM_WD, ADAM_STEP = 0.001, 0.9, 0.999, 1e-08, 0.01, 10
CB_AB, CB_AC, CB_AX, CB_AG, CB_Q, CB_K, CB_V, CB_BG, CB_CX, CB_CG, CB_DU, CB_DG = range(12)
N_CHIPS = 4
N_DEV = 8


def _params(n_axes=0):
    kw = {"dimension_semantics": ("arbitrary",) * n_axes} if n_axes else {}
    return pltpu.CompilerParams(vmem_limit_bytes=VMEM_LIMIT, **kw)


def _rows(tb, w, cb=0):
    return pl.BlockSpec((tb, w), lambda i: (i, cb))


def _prev8(tb, w, cb=0):
    return pl.BlockSpec((8, w), lambda i: (jnp.maximum(i * (tb // 8) - 1, 0), cb))


def _next8(tb, w, n_rows, cb=0):
    return pl.BlockSpec((8, w), lambda i: (jnp.minimum((i + 1) * (tb // 8), n_rows // 8 - 1), cb))


def _const(shape):
    return pl.BlockSpec(shape, lambda *_: (0,) * len(shape))


def _silu(x):
    return x * jax.nn.sigmoid(x)


def _dsilu(x):
    s = jax.nn.sigmoid(x)
    return s * (1.0 + x * (1.0 - s))


def _shift_down(cur, prev8, j):
    rolled = pltpu.roll(cur, j, 0)
    row = lax.broadcasted_iota(jnp.int32, (8, cur.shape[1]), 0)
    first = jnp.where(row < j, pltpu.roll(prev8, j, 0), rolled[0:8])
    return jnp.concatenate([first, rolled[8:]], axis=0)


def _shift_up(cur, next8, j):
    t = cur.shape[0]
    rolled = pltpu.roll(cur, t - j, 0)
    row = lax.broadcasted_iota(jnp.int32, (8, cur.shape[1]), 0)
    last = jnp.where(row >= 8 - j, pltpu.roll(next8, 8 - j, 0), rolled[t - 8:t])
    return jnp.concatenate([rolled[:t - 8], last], axis=0)


def _colsum(x):
    return jnp.sum(x, axis=0, keepdims=True)


def _init_acc(*refs):
    @pl.when(pl.program_id(0) == 0)
    def _():
        for r in refs:
            r[...] = jnp.zeros_like(r)


def _mm(a, b, *, name, ta=False, tb=False, out_dtype=F32, tm=512, tn=512, tk=512, a_col0=0, a_ncols=None, bias=None,
        carry=None):
    a_ncols = a.shape[1] - a_col0 if a_ncols is None else a_ncols
    m, k = (a_ncols, a.shape[0]) if ta else (a.shape[0], a_ncols)
    n = b.shape[0] if tb else b.shape[1]
    assert k == (b.shape[1] if tb else b.shape[0]), (name, a.shape, b.shape)
    tm, tn, tk = min(tm, m), min(tn, n), min(tk, k)
    nk = k // tk
    a_off = a_col0 // (tm if ta else tk)
    assert m % tm == 0 and n % tn == 0 and k % tk == 0 and a_col0 % (tm if ta else tk) == 0, (name, m, n, k)
    n_mine = 2 if bias is None else 3
    grid = (m // tm, n // tn, nk)

    def body(*refs):
        if carry is not None:
            x_ins = refs[n_mine:n_mine + carry.n_in]
            x_outs = refs[n_mine + carry.n_in + 1:n_mine + carry.n_in + 1 + carry.n_out]
            x_sems = refs[-3:]
            refs = refs[:n_mine] + (refs[n_mine + carry.n_in], refs[-4])
            at = [pl.program_id(d) for d in range(3)]
            first = (at[0] == 0) & (at[1] == 0) & (at[2] == 0)
            last = (at[0] == grid[0] - 1) & (at[1] == grid[1] - 1) & (at[2] == grid[2] - 1)
            pl.when(first)(lambda: carry.start(x_ins, x_outs, x_sems))
        if bias is None:
            a_ref, b_ref, o_ref, acc = refs
        else:
            a_ref, b_ref, bias_ref, o_ref, acc = refs
        kk = pl.program_id(2)

        @pl.when(kk == 0)
        def _():
            acc[...] = jnp.zeros_like(acc)

        dims = (((0 if ta else 1,), (1 if tb else 0,)), ((), ()))
        acc[...] += lax.dot_general(a_ref[...].astype(MXU_DTYPE), b_ref[...].astype(MXU_DTYPE), dims,
                                    preferred_element_type=F32)

        @pl.when(kk == nk - 1)
        def _():
            r = acc[...]
            if bias is not None:
                r = r + bias_ref[...]
            o_ref[...] = r.astype(out_dtype)

        if carry is not None:
            pl.when(last)(lambda: carry.wait(x_ins, x_outs, x_sems))

    a_spec = (pl.BlockSpec((tk, tm), lambda i, j, kk: (kk, i + a_off)) if ta
              else pl.BlockSpec((tm, tk), lambda i, j, kk: (i, kk + a_off)))
    b_spec = (pl.BlockSpec((tn, tk), lambda i, j, kk: (j, kk)) if tb
              else pl.BlockSpec((tk, tn), lambda i, j, kk: (kk, j)))
    in_specs, args = [a_spec, b_spec], [a, b]
    if bias is not None:
        in_specs.append(pl.BlockSpec((1, tn), lambda i, j, kk: (0, j)))
        args.append(bias)
    out_shape, out_specs = SDS((m, n), out_dtype), pl.BlockSpec((tm, tn), lambda i, j, kk: (i, j))
    scratch = [pltpu.VMEM((tm, tn), F32)]
    if carry is not None:
        in_specs, args = in_specs + [ANY] * carry.n_in, args + carry.arrays
        out_shape, out_specs = (out_shape,) + carry.out_shapes, (out_specs,) + (ANY,) * carry.n_out
        scratch = scratch + carry.scratch
    return pl.pallas_call(body, name=name, out_shape=out_shape, grid=grid, in_specs=in_specs, out_specs=out_specs,
                          scratch_shapes=scratch, compiler_params=_params(3))(*args)


def _silu_rows(c_all):
    def body(c_ref, o_ref):
        o_ref[...] = _silu(c_ref[...])
    return pl.pallas_call(body, name="cond_silu", out_shape=SDS(c_all.shape, F32))(c_all)


def _modulate(x, scale, shift, tb):
    s, d = x.shape

    def body(x_ref, sc_ref, sh_ref, o_ref):
        o_ref[...] = (x_ref[...] * (1.0 + sc_ref[...]) + sh_ref[...]).astype(MXU_DTYPE)

    return pl.pallas_call(body, name="modulate", out_shape=SDS((s, d), MXU_DTYPE), grid=(s // tb,),
                          in_specs=[_rows(tb, d), _const((1, d)), _const((1, d))], out_specs=_rows(tb, d),
                          compiler_params=_params(1))(x, scale, shift)


def _out_ln(ycat, w_out, x, gate, ln_g, ln_b, tb):
    s, d = x.shape

    def body(yc_ref, w_ref, x_ref, gt_ref, g_ref, b_ref, xn_ref, xh_ref, y_ref, rs_ref):
        y = jnp.dot(yc_ref[...], w_ref[...], preferred_element_type=F32)
        res = ALPHA * x_ref[...] + (1.0 + gt_ref[...]) * y
        mu = jnp.mean(res, axis=-1, keepdims=True)
        cen = res - mu
        var = jnp.mean(cen * cen, axis=-1, keepdims=True)
        rstd = lax.rsqrt(var + LN_EPS)
        xhat = cen * rstd
        xn_ref[...] = xhat * g_ref[...] + b_ref[...]
        xh_ref[...] = xhat
        y_ref[...] = y
        rs_ref[...] = rstd

    big = SDS((s, d), F32)
    return pl.pallas_call(
        body, name="out_proj_ln", out_shape=(big, big, big, SDS((s, 1), F32)), grid=(s // tb,),
        in_specs=[_rows(tb, d), _const((d, d)), _rows(tb, d), _const((1, d)), _const((1, d)), _const((1, d))],
        out_specs=(_rows(tb, d), _rows(tb, d), _rows(tb, d), _rows(tb, 1)), compiler_params=_params(1),
    )(ycat, w_out, x, gate, ln_g, ln_b)


def _ln_bwd(dxn, xhat, y, rstd, ln_g, gate, tb):
    s, d = dxn.shape

    def body(dxn_ref, xh_ref, y_ref, rs_ref, g_ref, gt_ref, dy_ref, dxa_ref, dg_ref, db_ref, dgt_ref):
        _init_acc(dg_ref, db_ref, dgt_ref)
        dxn_t, xh = dxn_ref[...], xh_ref[...]
        dxh = dxn_t * g_ref[...]
        dres = rs_ref[...] * (dxh - jnp.mean(dxh, axis=-1, keepdims=True)
                              - xh * jnp.mean(dxh * xh, axis=-1, keepdims=True))
        dy_ref[...] = ((1.0 + gt_ref[...]) * dres).astype(MXU_DTYPE)
        dxa_ref[...] = ALPHA * dres
        dg_ref[...] += _colsum(dxn_t * xh)
        db_ref[...] += _colsum(dxn_t)
        dgt_ref[...] += _colsum(dres * y_ref[...])

    vec = SDS((1, d), F32)
    return pl.pallas_call(
        body, name="ln_bwd", out_shape=(SDS((s, d), MXU_DTYPE), SDS((s, d), F32), vec, vec, vec), grid=(s // tb,),
        in_specs=[_rows(tb, d), _rows(tb, d), _rows(tb, d), _rows(tb, 1), _const((1, d)), _const((1, d))],
        out_specs=(_rows(tb, d), _rows(tb, d), _const((1, d)), _const((1, d)), _const((1, d))),
        compiler_params=_params(1))(dxn, xhat, y, rstd, ln_g, gate)


def _mod_bwd(dh, dxa, x, scale, tb):
    s, d = dh.shape

    def body(dh_ref, dxa_ref, x_ref, sc_ref, dx_ref, dsh_ref, dsc_ref):
        _init_acc(dsh_ref, dsc_ref)
        dh_t = dh_ref[...]
        dx_ref[...] = dxa_ref[...] + dh_t * (1.0 + sc_ref[...])
        dsh_ref[...] += _colsum(dh_t)
        dsc_ref[...] += _colsum(dh_t * x_ref[...])

    vec = SDS((1, d), F32)
    return pl.pallas_call(
        body, name="mod_bwd", out_shape=(SDS((s, d), F32), vec, vec), grid=(s // tb,),
        in_specs=[_rows(tb, d), _rows(tb, d), _rows(tb, d), _const((1, d))],
        out_specs=(_rows(tb, d), _const((1, d)), _const((1, d))), compiler_params=_params(1))(dh, dxa, x, scale)


def _loss_head(y, target, tb):
    s, d = y.shape

    def body(y_ref, t_ref, l_ref, dy_ref):
        _init_acc(l_ref)
        err = y_ref[...] - t_ref[...]
        l_ref[...] += (0.5 / d) * jnp.sum(err * err, keepdims=True)
        dy_ref[...] = err * (1.0 / d)

    return pl.pallas_call(body, name="loss_head", out_shape=(SDS((1, 1), F32), SDS((s, d), F32)), grid=(s // tb,),
                          in_specs=[_rows(tb, d), _rows(tb, d)], out_specs=(_const((1, 1)), _rows(tb, d)),
                          compiler_params=_params(1))(y, target)


def _conv_taps(u, up, w_ref, width):
    out = w_ref[width - 1:width, :] * u
    for j in range(width - 2, -1, -1):
        out = out + w_ref[j:j + 1, :] * _shift_down(u, up, width - 1 - j)
    return out


def _conv_taps_t(g, gn, w_ref, width):
    out = w_ref[width - 1:width, :] * g
    for j in range(width - 2, -1, -1):
        out = out + w_ref[j:j + 1, :] * _shift_up(g, gn, width - 1 - j)
    return out


def _conv_wgrad(dw_ref, g, u, up, width):
    dw_ref[width - 1:width, :] += _colsum(g * u)
    for j in range(width - 1):
        dw_ref[j:j + 1, :] += _colsum(g * _shift_down(u, up, width - 1 - j))


def _branch_a_fwd(proj, conv_w, tb):
    s = proj.shape[0]

    def body(ab, ac, ax, ag, acp, axp, w_ref, o_ref):
        has_prev = (pl.program_id(0) > 0).astype(F32)
        u = ac[...] * ax[...]
        up = acp[...] * axp[...] * has_prev
        o_ref[...] = (ab[...] * _conv_taps(u, up, w_ref, 3) * _silu(ag[...])).astype(MXU_DTYPE)

    return pl.pallas_call(
        body, name="branch_a_fwd", out_shape=SDS((s, BR), MXU_DTYPE), grid=(s // tb,),
        in_specs=[_rows(tb, BR, CB_AB), _rows(tb, BR, CB_AC), _rows(tb, BR, CB_AX), _rows(tb, BR, CB_AG),
                  _prev8(tb, BR, CB_AC), _prev8(tb, BR, CB_AX), _const((8, BR))],
        out_specs=_rows(tb, BR), compiler_params=_params(1))(proj, proj, proj, proj, proj, proj, conv_w)


def _branch_a_bwd(dycat, proj, conv_w, tb):
    s = proj.shape[0]

    def body(dy, dyn, ab, abn, ag, agn, ac, acp, ax, axp, w_ref, o_ref, dw_ref):
        _init_acc(dw_ref)
        i = pl.program_id(0)
        has_prev = (i > 0).astype(F32)
        has_next = (i < pl.num_programs(0) - 1).astype(F32)
        u = ac[...] * ax[...]
        up = acp[...] * axp[...] * has_prev
        v = _conv_taps(u, up, w_ref, 3)
        sg = _silu(ag[...])
        dv = dy[...] * ab[...] * sg
        dvn = dyn[...] * abn[...] * _silu(agn[...]) * has_next
        du = _conv_taps_t(dv, dvn, w_ref, 3)
        o_ref[:, 0:BR] = (dy[...] * v * sg).astype(MXU_DTYPE)
        o_ref[:, BR:2 * BR] = (du * ax[...]).astype(MXU_DTYPE)
        o_ref[:, 2 * BR:3 * BR] = (du * ac[...]).astype(MXU_DTYPE)
        o_ref[:, 3 * BR:4 * BR] = (dy[...] * ab[...] * v * _dsilu(ag[...])).astype(MXU_DTYPE)
        _conv_wgrad(dw_ref, dv, u, up, 3)

    return pl.pallas_call(
        body, name="branch_a_bwd", out_shape=(SDS((s, 4 * BR), MXU_DTYPE), SDS((8, BR), F32)), grid=(s // tb,),
        in_specs=[_rows(tb, BR, 0), _next8(tb, BR, s, 0),
                  _rows(tb, BR, CB_AB), _next8(tb, BR, s, CB_AB), _rows(tb, BR, CB_AG), _next8(tb, BR, s, CB_AG),
                  _rows(tb, BR, CB_AC), _prev8(tb, BR, CB_AC), _rows(tb, BR, CB_AX), _prev8(tb, BR, CB_AX),
                  _const((8, BR))],
        out_specs=(_rows(tb, 4 * BR), _const((8, BR))), compiler_params=_params(1),
    )(dycat, dycat, proj, proj, proj, proj, proj, proj, proj, proj, conv_w)


def _t5_bucket(dist):
    max_exact = REL_BUCKETS // 2
    nf = jnp.maximum(dist, 1).astype(F32)
    large = max_exact + (jnp.log(nf / max_exact) / math.log(REL_MAX_DIST / max_exact)
                         * (REL_BUCKETS - max_exact)).astype(jnp.int32)
    large = jnp.minimum(large, REL_BUCKETS - 1)
    return jnp.where(dist < max_exact, dist, large)


def _bucket_maps():
    maps = []
    i = jnp.arange(BLK)[:, None]
    j = jnp.arange(2 * BLK)[None, :]
    delta = i + BLK - j
    for window, dil in DILATIONS:
        span = window // dil
        bucket = _t5_bucket(jnp.clip(delta, 0, span) * dil)
        maps.append(jnp.where((delta >= 0) & (delta <= span), bucket, -1))
    return jnp.stack(maps).astype(jnp.int32)


def _bias_tables(rel_bias, buckets):
    n_pat = len(DILATIONS)

    def body(rb_ref, bk_ref, o_ref):
        for g in range(n_pat):
            bk = bk_ref[g]
            for h in range(ATT_HEADS):
                def per_bucket(b, acc):
                    return jnp.where(bk == b, rb_ref[b, h], acc)
                o_ref[g, h] = lax.fori_loop(0, REL_BUCKETS, per_bucket, jnp.full((BLK, 2 * BLK), NEG, F32))

    return pl.pallas_call(
        body, name="bias_tables", out_shape=SDS((n_pat, ATT_HEADS, BLK, 2 * BLK), F32),
        in_specs=[pl.BlockSpec(memory_space=pltpu.SMEM), pl.BlockSpec(memory_space=pltpu.VMEM)],
        compiler_params=_params())(rel_bias, buckets)


def _head_masks():
    lane = lax.broadcasted_iota(jnp.int32, (1, 2 * HEAD_DIM), 1)
    return [(lane < HEAD_DIM).astype(F32), (lane >= HEAD_DIM).astype(F32)]


def _strided(base, size, dil):
    return pl.ds(base, size, stride=dil) if dil > 1 else pl.ds(pl.multiple_of(base, BLK), size)


def _attn_groups(s, dil):
    return max(1, min(1024, s) // (dil * BLK)) if dil == 1 else max(1, min(2048, s) // (dil * BLK))


def _attn_fwd(proj, bias, dil):
    s = proj.shape[0]
    grp = _attn_groups(s, dil)
    u1 = dil * BLK
    unit = grp * u1
    nb = s // unit
    w = 2 * HEAD_DIM
    q0, k0, v0 = (cb * (BR // w) for cb in (CB_Q, CB_K, CB_V))

    def body(q_ref, kc_ref, kp_ref, vc_ref, vp_ref, bias_ref, o_ref, lse_ref, kbuf, vbuf):
        n = pl.program_id(1)
        col = lax.broadcasted_iota(jnp.int32, (1, 2 * BLK), 1)
        masks = _head_masks()
        kbuf[0:u1, :] = kp_ref[...]
        kbuf[u1:, :] = kc_ref[...]
        vbuf[0:u1, :] = vp_ref[...]
        vbuf[u1:, :] = vc_ref[...]

        def per_r(t, carry):
            j = t // dil
            base = j * u1 + t % dil
            rows = _strided(base, BLK, dil)
            no_prev = jnp.where((n == 0) & (j == 0) & (col < BLK), NEG, 0.0)
            q = q_ref[rows, :] * (HEAD_DIM ** -0.5)
            k = kbuf[_strided(base, 2 * BLK, dil), :].astype(MXU_DTYPE)
            v = vbuf[_strided(base, 2 * BLK, dil), :].astype(MXU_DTYPE)
            o_acc = jnp.zeros((BLK, w), F32)
            lse_acc = jnp.zeros((BLK, w), F32)
            for h in range(2):
                qh = (q * masks[h]).astype(MXU_DTYPE)
                sc = lax.dot_general(qh, k, (((1,), (1,)), ((), ())), preferred_element_type=F32)
                sc = sc + bias_ref[h] + no_prev
                mx = jnp.max(sc, axis=-1, keepdims=True)
                p = jnp.exp(sc - mx)
                l = jnp.sum(p, axis=-1, keepdims=True)
                oh = jnp.dot((p / l).astype(MXU_DTYPE), v, preferred_element_type=F32)
                o_acc = o_acc + oh * masks[h]
                lse_acc = lse_acc + (mx + jnp.log(l)) * masks[h]
            o_ref[rows, :] = o_acc
            lse_ref[rows, :] = lse_acc
            return carry

        lax.fori_loop(0, grp * dil, per_r, 0, unroll=2)

    cur = lambda c0: pl.BlockSpec((unit, w), lambda hp, n: (n, c0 + hp))
    prev = lambda c0: pl.BlockSpec((u1, w), lambda hp, n: (jnp.maximum(n * grp - 1, 0), c0 + hp))
    out = pl.BlockSpec((unit, w), lambda hp, n: (n, hp))
    return pl.pallas_call(
        body, name=f"attn_fwd_d{dil}", out_shape=(SDS((s, BR), F32), SDS((s, BR), F32)), grid=(BR // w, nb),
        in_specs=[cur(q0), cur(k0), prev(k0), cur(v0), prev(v0),
                  pl.BlockSpec((2, BLK, 2 * BLK), lambda hp, n: (hp, 0, 0))],
        out_specs=(out, out),
        scratch_shapes=[pltpu.VMEM((unit + u1, w), F32), pltpu.VMEM((unit + u1, w), F32)],
        compiler_params=_params(2))(proj, proj, proj, proj, proj, bias)


def _softmax3(l0, l1, l2):
    mx = jnp.maximum(jnp.maximum(l0, l1), l2)
    e0, e1, e2 = jnp.exp(l0 - mx), jnp.exp(l1 - mx), jnp.exp(l2 - mx)
    inv = 1.0 / (e0 + e1 + e2)
    return e0 * inv, e1 * inv, e2 * inv


def _attn_combine(os_, lses, proj, tb):
    s = proj.shape[0]

    def body(o0, o1, o2, l0, l1, l2, bg, y_ref):
        w0, w1, w2 = _softmax3(l0[...], l1[...], l2[...])
        attn = w0 * o0[...] + w1 * o1[...] + w2 * o2[...]
        y_ref[...] = (attn * _silu(bg[...])).astype(MXU_DTYPE)

    return pl.pallas_call(
        body, name="attn_combine", out_shape=SDS((s, BR), MXU_DTYPE), grid=(s // tb,),
        in_specs=[_rows(tb, BR)] * 6 + [_rows(tb, BR, CB_BG)], out_specs=_rows(tb, BR),
        compiler_params=_params(1))(*os_, *lses, proj)


def _attn_bwd_pre(dycat, os_, lses, proj, head_ones, tb):
    s = proj.shape[0]

    def body(dy, o0, o1, o2, l0, l1, l2, bg, e_ref, dbg_ref, do0, do1, do2, dm0, dm1, dm2):
        w0, w1, w2 = _softmax3(l0[...], l1[...], l2[...])
        attn = w0 * o0[...] + w1 * o1[...] + w2 * o2[...]
        dattn = dy[...] * _silu(bg[...])
        dbg_ref[...] = dy[...] * attn * _dsilu(bg[...])
        prod = dattn * attn
        hi = prod.astype(MXU_DTYPE)
        lo = (prod - hi.astype(F32)).astype(MXU_DTYPE)
        tot = (jnp.dot(hi, e_ref[...], preferred_element_type=F32)
               + jnp.dot(lo, e_ref[...], preferred_element_type=F32))
        for wg, do_ref, dm_ref in ((w0, do0, dm0), (w1, do1, dm1), (w2, do2, dm2)):
            do_ref[...] = wg * dattn
            dm_ref[...] = wg * tot

    big = SDS((s, BR), F32)
    return pl.pallas_call(
        body, name="attn_bwd_pre", out_shape=(big,) * 7, grid=(s // tb,),
        in_specs=[_rows(tb, BR, 1)] + [_rows(tb, BR)] * 6 + [_rows(tb, BR, CB_BG), _const((BR, BR))],
        out_specs=(_rows(tb, BR),) * 7, compiler_params=_params(1))(dycat, *os_, *lses, proj, head_ones)


def _attn_bwd(proj, do, lse, dm, bias, dil):
    s = proj.shape[0]
    grp = _attn_groups(s, dil)
    u1 = dil * BLK
    unit = grp * u1
    nb = s // unit
    w = 2 * HEAD_DIM
    q0, k0, v0 = (cb * (BR // w) for cb in (CB_Q, CB_K, CB_V))

    def body(q_ref, kc_ref, kp_ref, vc_ref, vp_ref, do_ref, lse_ref, dm_ref, bias_ref,
             dq_ref, dk_ref, dv_ref, dbias_ref, kbuf, vbuf, stage_k, stage_v):
        n = pl.program_id(1)
        col = lax.broadcasted_iota(jnp.int32, (1, 2 * BLK), 1)
        masks = _head_masks()

        @pl.when(n == 0)
        def _():
            dbias_ref[...] = jnp.zeros_like(dbias_ref)
            stage_k[...] = jnp.zeros_like(stage_k)
            stage_v[...] = jnp.zeros_like(stage_v)

        for out_ref, stage in ((dk_ref, stage_k), (dv_ref, stage_v)):
            if grp > 1:
                out_ref[0:unit - u1, :] = stage[u1:unit, :]
            stage[0:u1, :] = stage[unit:unit + u1, :]

        @pl.when(n < nb)
        def _():
            kbuf[0:u1, :] = kp_ref[...]
            kbuf[u1:, :] = kc_ref[...]
            vbuf[0:u1, :] = vp_ref[...]
            vbuf[u1:, :] = vc_ref[...]

            def per_r(t, carry):
                j = t // dil
                base = j * u1 + t % dil
                rows = _strided(base, BLK, dil)
                rows_hi = _strided(base + u1, BLK, dil)
                no_prev = jnp.where((n == 0) & (j == 0) & (col < BLK), NEG, 0.0)
                q = q_ref[rows, :] * (HEAD_DIM ** -0.5)
                k = kbuf[_strided(base, 2 * BLK, dil), :].astype(MXU_DTYPE)
                v = vbuf[_strided(base, 2 * BLK, dil), :].astype(MXU_DTYPE)
                do_t, lse_t, dm_t = do_ref[rows, :], lse_ref[rows, :], dm_ref[rows, :]
                dq_acc = jnp.zeros((BLK, w), F32)
                dk_acc = jnp.zeros((2 * BLK, w), F32)
                dv_acc = jnp.zeros((2 * BLK, w), F32)
                for h in range(2):
                    qh = (q * masks[h]).astype(MXU_DTYPE)
                    doh = (do_t * masks[h]).astype(MXU_DTYPE)
                    c0 = h * HEAD_DIM
                    sc = lax.dot_general(qh, k, (((1,), (1,)), ((), ())), preferred_element_type=F32)
                    p = jnp.exp(sc + bias_ref[h] + no_prev - lse_t[:, c0:c0 + 1])
                    dp = lax.dot_general(doh, v, (((1,), (1,)), ((), ())), preferred_element_type=F32)
                    ds = p * (dp - dm_t[:, c0:c0 + 1])
                    dbias_ref[h] += ds
                    dsb, pb = ds.astype(MXU_DTYPE), p.astype(MXU_DTYPE)
                    dq_acc = dq_acc + jnp.dot(dsb, k, preferred_element_type=F32) * masks[h]
                    dk_acc = dk_acc + lax.dot_general(dsb, qh, (((0,), (0,)), ((), ())), preferred_element_type=F32)
                    dv_acc = dv_acc + lax.dot_general(pb, doh, (((0,), (0,)), ((), ())), preferred_element_type=F32)
                dq_ref[rows, :] = dq_acc * (HEAD_DIM ** -0.5)
                stage_k[rows, :] = stage_k[rows, :] + dk_acc[0:BLK]
                stage_v[rows, :] = stage_v[rows, :] + dv_acc[0:BLK]
                stage_k[rows_hi, :] = dk_acc[BLK:2 * BLK]
                stage_v[rows_hi, :] = dv_acc[BLK:2 * BLK]
                return carry

            lax.fori_loop(0, grp * dil, per_r, 0, unroll=2)

        dk_ref[unit - u1:unit, :] = stage_k[0:u1, :]
        dv_ref[unit - u1:unit, :] = stage_v[0:u1, :]

    qn = lambda n: jnp.minimum(n, nb - 1)
    cur = lambda c0: pl.BlockSpec((unit, w), lambda hp, n: (qn(n), c0 + hp))
    prev = lambda c0: pl.BlockSpec((u1, w), lambda hp, n: (jnp.maximum(qn(n) * grp - 1, 0), c0 + hp))
    row = pl.BlockSpec((unit, w), lambda hp, n: (qn(n), hp))
    late = pl.BlockSpec((unit, w), lambda hp, n: (jnp.maximum(n - 1, 0), hp))
    tab = pl.BlockSpec((2, BLK, 2 * BLK), lambda hp, n: (hp, 0, 0))
    big = SDS((s, BR), F32)
    return pl.pallas_call(
        body, name=f"attn_bwd_d{dil}", out_shape=(big, big, big, SDS((ATT_HEADS, BLK, 2 * BLK), F32)),
        grid=(BR // w, nb + 1),
        in_specs=[cur(q0), cur(k0), prev(k0), cur(v0), prev(v0), row, row, row, tab],
        out_specs=(row, late, late, tab),
        scratch_shapes=[pltpu.VMEM((unit + u1, w), F32)] * 4,
        compiler_params=_params(2))(proj, proj, proj, proj, proj, do, lse, dm, bias)


def _rel_bias_grad(dbias, buckets):
    def body(db_ref, bk_ref, o_ref):
        row = lax.broadcasted_iota(jnp.int32, (REL_BUCKETS, 128), 0)
        lane = lax.broadcasted_iota(jnp.int32, (REL_BUCKETS, 128), 1)

        def per_bucket(b, acc):
            for g in range(len(DILATIONS)):
                hit = bk_ref[g] == b
                for h in range(ATT_HEADS):
                    both = db_ref[0, g, h] + db_ref[1, g, h]
                    val = jnp.sum(jnp.where(hit, both, 0.0), keepdims=True)
                    acc = acc + jnp.where((row == b) & (lane == h), val, 0.0)
            return acc

        o_ref[...] = lax.fori_loop(0, REL_BUCKETS, per_bucket, jnp.zeros((REL_BUCKETS, 128), F32))

    assert dbias.shape[0] == DEPTH == 2
    return pl.pallas_call(body, name="rel_bias_grad", out_shape=SDS((REL_BUCKETS, 128), F32),
                          compiler_params=_params())(dbias, buckets)


def _scan_real(a, b, *, reverse, tb, name):
    s, ch = a.shape
    nt = s // tb
    order = range(7, -1, -1) if reverse else range(8)

    def body(a_ref, b_ref, o_ref, carry):
        @pl.when(pl.program_id(0) == 0)
        def _():
            carry[...] = jnp.zeros_like(carry)

        def group(gi, h):
            r0 = pl.multiple_of((tb // 8 - 1 - gi if reverse else gi) * 8, 8)
            a8, b8 = a_ref[pl.ds(r0, 8), :], b_ref[pl.ds(r0, 8), :]
            rows = [None] * 8
            for k in order:
                if reverse:
                    rows[k] = b8[k:k + 1] + h
                    h = a8[k:k + 1] * rows[k]
                else:
                    h = a8[k:k + 1] * h + b8[k:k + 1]
                    rows[k] = h
            o_ref[pl.ds(r0, 8), :] = jnp.concatenate(rows, axis=0)
            return h

        carry[...] = lax.fori_loop(0, tb // 8, group, carry[...])

    spec = pl.BlockSpec((tb, ch), (lambda i: (nt - 1 - i, 0)) if reverse else (lambda i: (i, 0)))
    return pl.pallas_call(body, name=name, out_shape=SDS((s, ch), F32), grid=(nt,), in_specs=[spec, spec],
                          out_specs=spec, scratch_shapes=[pltpu.VMEM((1, ch), F32)],
                          compiler_params=_params(1))(a, b)


def _scan_cplx(b, a_row, *, reverse, tb, name):
    s, ch2 = b.shape
    ch = ch2 // 2
    nt = s // tb
    order = range(7, -1, -1) if reverse else range(8)

    def body(a_ref, b_ref, o_ref, carry):
        @pl.when(pl.program_id(0) == 0)
        def _():
            carry[...] = jnp.zeros_like(carry)

        ar = a_ref[:, 0:ch]
        ai = -a_ref[:, ch:ch2] if reverse else a_ref[:, ch:ch2]

        def group(gi, x):
            xr, xi = x
            r0 = pl.multiple_of((tb // 8 - 1 - gi if reverse else gi) * 8, 8)
            br8, bi8 = b_ref[pl.ds(r0, 8), 0:ch], b_ref[pl.ds(r0, 8), ch:ch2]
            rr, ri = [None] * 8, [None] * 8
            for k in order:
                nr = ar * xr - ai * xi + br8[k:k + 1]
                ni = ar * xi + ai * xr + bi8[k:k + 1]
                xr, xi = nr, ni
                rr[k], ri[k] = xr, xi
            o_ref[pl.ds(r0, 8), 0:ch] = jnp.concatenate(rr, axis=0)
            o_ref[pl.ds(r0, 8), ch:ch2] = jnp.concatenate(ri, axis=0)
            return xr, xi

        xr, xi = lax.fori_loop(0, tb // 8, group, (carry[:, 0:ch], carry[:, ch:ch2]))
        carry[:, 0:ch] = xr
        carry[:, ch:ch2] = xi

    spec = pl.BlockSpec((tb, ch2), (lambda i: (nt - 1 - i, 0)) if reverse else (lambda i: (i, 0)))
    return pl.pallas_call(body, name=name, out_shape=SDS((s, ch2), F32), grid=(nt,),
                          in_specs=[_const((1, ch2)), spec], out_specs=spec,
                          scratch_shapes=[pltpu.VMEM((1, ch2), F32)], compiler_params=_params(1))(a_row, b)


def _neg_expm1(z):
    series = -z * (1.0 + z * (0.5 + z * (1.0 / 6 + z * (1.0 / 24 + z * (1.0 / 120)))))
    return jnp.where(z > -0.05, series, 1.0 - jnp.exp(z))


def _lru_gate(xc, pre_r, pre_i, lam):
    log_a = -LRU_C * jax.nn.sigmoid(pre_r) * jax.nn.softplus(-lam)
    return jnp.exp(log_a), jnp.sqrt(_neg_expm1(2.0 * log_a)) * jax.nn.sigmoid(pre_i) * xc


def _lru_gates_fwd(proj, conv_w, conv_b, w_cat, b_cat, lam, tb):
    s = proj.shape[0]

    def body(cx, cxp, w_ref, cb_ref, wc_ref, bc_ref, lam_ref, a_ref, b_ref):
        has_prev = (pl.program_id(0) > 0).astype(F32)
        xc = _conv_taps(cx[...], cxp[...] * has_prev, w_ref, 4) + cb_ref[...]
        pre = jnp.dot(xc.astype(MXU_DTYPE), wc_ref[...], preferred_element_type=F32) + bc_ref[...]
        a_ref[...], b_ref[...] = _lru_gate(xc, pre[:, 0:BR], pre[:, BR:2 * BR], lam_ref[...])

    big = SDS((s, BR), F32)
    return pl.pallas_call(
        body, name="lru_gates_fwd", out_shape=(big, big), grid=(s // tb,),
        in_specs=[_rows(tb, BR, CB_CX), _prev8(tb, BR, CB_CX), _const((8, BR)), _const((1, BR)),
                  _const((BR, 2 * BR)), _const((1, 2 * BR)), _const((1, BR))],
        out_specs=(_rows(tb, BR), _rows(tb, BR)), compiler_params=_params(1),
    )(proj, proj, conv_w, conv_b, w_cat, b_cat, lam)


def _gate_out(h, proj, cb, tb, name):
    s = proj.shape[0]

    def body(h_ref, g_ref, o_ref):
        o_ref[...] = (h_ref[...] * _silu(g_ref[...])).astype(MXU_DTYPE)

    return pl.pallas_call(body, name=name, out_shape=SDS((s, BR), MXU_DTYPE), grid=(s // tb,),
                          in_specs=[_rows(tb, BR), _rows(tb, BR, cb)], out_specs=_rows(tb, BR),
                          compiler_params=_params(1))(h, proj)


def _gate_out_bwd(dycat, dy_cb, h, proj, cb, tb, name):
    s = proj.shape[0]

    def body(dy, h_ref, g_ref, dh_ref, dg_ref):
        dh_ref[...] = dy[...] * _silu(g_ref[...])
        dg_ref[...] = dy[...] * h_ref[...] * _dsilu(g_ref[...])

    big = SDS((s, BR), F32)
    return pl.pallas_call(body, name=name, out_shape=(big, big), grid=(s // tb,),
                          in_specs=[_rows(tb, BR, dy_cb), _rows(tb, BR), _rows(tb, BR, cb)],
                          out_specs=(_rows(tb, BR), _rows(tb, BR)), compiler_params=_params(1))(dycat, h, proj)


def _lru_gates_bwd(proj, lmb, h, conv_w, conv_b, w_cat, b_cat, lam, tb):
    s = proj.shape[0]

    def body(cx, cxp, l_ref, h_ref, hp_ref, w_ref, cb_ref, wc_ref, bc_ref, lam_ref,
             dxc_ref, dpre_ref, xc_ref, dbc_ref, dlam_ref):
        _init_acc(dbc_ref, dlam_ref)
        has_prev = (pl.program_id(0) > 0).astype(F32)
        xc = _conv_taps(cx[...], cxp[...] * has_prev, w_ref, 4) + cb_ref[...]
        xcb = xc.astype(MXU_DTYPE)
        pre = jnp.dot(xcb, wc_ref[...], preferred_element_type=F32) + bc_ref[...]
        _, vjp = jax.vjp(_lru_gate, xc, pre[:, 0:BR], pre[:, BR:2 * BR], lam_ref[...])
        lm = l_ref[...]
        dxc, dpr, dpi, dlam = vjp((lm * _shift_down(h_ref[...], hp_ref[...] * has_prev, 1), lm))
        dpre = jnp.concatenate([dpr, dpi], axis=1)
        dpreb = dpre.astype(MXU_DTYPE)
        dxc_ref[...] = dxc + lax.dot_general(dpreb, wc_ref[...], (((1,), (1,)), ((), ())),
                                             preferred_element_type=F32)
        dpre_ref[...] = dpreb
        xc_ref[...] = xcb
        dbc_ref[...] += _colsum(dpre)
        dlam_ref[...] += dlam

    return pl.pallas_call(
        body, name="lru_gates_bwd",
        out_shape=(SDS((s, BR), F32), SDS((s, 2 * BR), MXU_DTYPE), SDS((s, BR), MXU_DTYPE),
                   SDS((1, 2 * BR), F32), SDS((1, BR), F32)),
        grid=(s // tb,),
        in_specs=[_rows(tb, BR, CB_CX), _prev8(tb, BR, CB_CX), _rows(tb, BR), _rows(tb, BR), _prev8(tb, BR),
                  _const((8, BR)), _const((1, BR)), _const((BR, 2 * BR)), _const((1, 2 * BR)), _const((1, BR))],
        out_specs=(_rows(tb, BR), _rows(tb, 2 * BR), _rows(tb, BR), _const((1, 2 * BR)), _const((1, BR))),
        compiler_params=_params(1))(proj, proj, lmb, h, h, conv_w, conv_b, w_cat, b_cat, lam)


def _conv_c_bwd(dxc, proj, conv_w, tb):
    s = proj.shape[0]

    def body(g, gn, cx, cxp, w_ref, dcx_ref, dw_ref, db_ref):
        _init_acc(dw_ref, db_ref)
        i = pl.program_id(0)
        has_prev = (i > 0).astype(F32)
        has_next = (i < pl.num_programs(0) - 1).astype(F32)
        gt = g[...]
        dcx_ref[...] = _conv_taps_t(gt, gn[...] * has_next, w_ref, 4)
        _conv_wgrad(dw_ref, gt, cx[...], cxp[...] * has_prev, 4)
        db_ref[...] += _colsum(gt)

    return pl.pallas_call(
        body, name="conv_c_bwd", out_shape=(SDS((s, BR), F32), SDS((8, BR), F32), SDS((1, BR), F32)),
        grid=(s // tb,),
        in_specs=[_rows(tb, BR), _next8(tb, BR, s), _rows(tb, BR, CB_CX), _prev8(tb, BR, CB_CX), _const((8, BR))],
        out_specs=(_rows(tb, BR), _const((8, BR)), _const((1, BR))), compiler_params=_params(1),
    )(dxc, dxc, proj, proj, conv_w)


def _s5_disc(lam_re, lam_im, log_dt):
    dt = jnp.exp(log_dt)
    mag = jnp.exp(lam_re * dt)
    ab_re = mag * jnp.cos(lam_im * dt)
    ab_im = mag * jnp.sin(lam_im * dt)
    den = lam_re * lam_re + lam_im * lam_im
    f_re = ((ab_re - 1.0) * lam_re + ab_im * lam_im) / den
    f_im = (ab_im * lam_re - (ab_re - 1.0) * lam_im) / den
    return ab_re, ab_im, f_re, f_im


def _s5_bbar(f_re, f_im, b_re, b_im):
    return f_re * b_re - f_im * b_im, f_re * b_im + f_im * b_re


def _s5_disc_fwd(lam_re, lam_im, log_dt):
    def body(lr, li, ld, o0, o1, o2, o3):
        o0[...], o1[...], o2[...], o3[...] = _s5_disc(lr[...], li[...], ld[...])
    return pl.pallas_call(body, name="s5_disc_fwd", out_shape=(SDS(lam_re.shape, F32),) * 4)(lam_re, lam_im, log_dt)


def _s5_disc_bwd(lam_re, lam_im, log_dt, cts):
    def body(lr, li, ld, c0, c1, c2, c3, o0, o1, o2):
        _, vjp = jax.vjp(_s5_disc, lr[...], li[...], ld[...])
        o0[...], o1[...], o2[...] = vjp((c0[...], c1[...], c2[...], c3[...]))
    return pl.pallas_call(body, name="s5_disc_bwd", out_shape=(SDS(lam_re.shape, F32), SDS(lam_re.shape, F32),
                                                                SDS(log_dt.shape, F32)))(lam_re, lam_im, log_dt, *cts)


def _s5_bbar_fwd(f_re, f_im, b_re, b_im):
    def body(fr, fi, br, bi, o0, o1):
        o0[...], o1[...] = _s5_bbar(fr[...], fi[...], br[...], bi[...])
    return pl.pallas_call(body, name="s5_bbar_fwd", out_shape=(SDS(b_re.shape, F32),) * 2)(f_re, f_im, b_re, b_im)


def _s5_bbar_bwd(f_re, f_im, b_re, b_im, d_re, d_im):
    def body(fr, fi, br, bi, dr, di, o0, o1, o2, o3):
        _, vjp = jax.vjp(_s5_bbar, fr[...], fi[...], br[...], bi[...])
        o0[...], o1[...], o2[...], o3[...] = vjp((dr[...], di[...]))
    col, mat = SDS(f_re.shape, F32), SDS(b_re.shape, F32)
    return pl.pallas_call(body, name="s5_bbar_bwd", out_shape=(col, col, mat, mat))(f_re, f_im, b_re, b_im, d_re, d_im)


def _s5_tail_fwd(ylin, proj, d_skip, w_glu, b_glu, tb):
    s = proj.shape[0]

    def body(yl, u, dg, dk, w_ref, b_ref, o_ref):
        g = jax.nn.gelu(yl[...] + dk[...] * u[...])
        t = jnp.dot(g.astype(MXU_DTYPE), w_ref[...], preferred_element_type=F32) + b_ref[...]
        o_ref[...] = (g * jax.nn.sigmoid(t) * _silu(dg[...])).astype(MXU_DTYPE)

    return pl.pallas_call(
        body, name="s5_tail_fwd", out_shape=SDS((s, BR), MXU_DTYPE), grid=(s // tb,),
        in_specs=[_rows(tb, BR), _rows(tb, BR, CB_DU), _rows(tb, BR, CB_DG), _const((1, BR)), _const((BR, BR)),
                  _const((1, BR))],
        out_specs=_rows(tb, BR), compiler_params=_params(1))(ylin, proj, proj, d_skip, w_glu, b_glu)


def _s5_tail_bwd(dycat, ylin, proj, d_skip, w_glu, b_glu, tb):
    s = proj.shape[0]

    def body(dy, yl, u, dg, dk, w_ref, b_ref, dyl_ref, dus_ref, ddg_ref, g_ref, dt_ref, ddk_ref, dbg_ref):
        _init_acc(ddk_ref, dbg_ref)
        g, gelu_vjp = jax.vjp(jax.nn.gelu, yl[...] + dk[...] * u[...])
        gb = g.astype(MXU_DTYPE)
        sg = jax.nn.sigmoid(jnp.dot(gb, w_ref[...], preferred_element_type=F32) + b_ref[...])
        dz = dy[...] * _silu(dg[...])
        ddg_ref[...] = dy[...] * g * sg * _dsilu(dg[...])
        dt = dz * g * sg * (1.0 - sg)
        dtb = dt.astype(MXU_DTYPE)
        dgel = dz * sg + lax.dot_general(dtb, w_ref[...], (((1,), (1,)), ((), ())), preferred_element_type=F32)
        dyv, = gelu_vjp(dgel)
        dyl_ref[...] = dyv
        dus_ref[...] = dyv * dk[...]
        g_ref[...] = gb
        dt_ref[...] = dtb
        ddk_ref[...] += _colsum(dyv * u[...])
        dbg_ref[...] += _colsum(dt)

    big, half, vec = SDS((s, BR), F32), SDS((s, BR), MXU_DTYPE), SDS((1, BR), F32)
    return pl.pallas_call(
        body, name="s5_tail_bwd", out_shape=(big, big, big, half, half, vec, vec), grid=(s // tb,),
        in_specs=[_rows(tb, BR, 3), _rows(tb, BR), _rows(tb, BR, CB_DU), _rows(tb, BR, CB_DG), _const((1, BR)),
                  _const((BR, BR)), _const((1, BR))],
        out_specs=(_rows(tb, BR),) * 5 + (_const((1, BR)), _const((1, BR))), compiler_params=_params(1),
    )(dycat, ylin, proj, proj, d_skip, w_glu, b_glu)


def _s5_da(lmb, x, tb):
    s, ch2 = x.shape
    ch = ch2 // 2

    def body(l_ref, x_ref, xp_ref, o_ref):
        _init_acc(o_ref)
        has_prev = (pl.program_id(0) > 0).astype(F32)
        xprev = _shift_down(x_ref[...], xp_ref[...] * has_prev, 1)
        lr, li, xr, xi = l_ref[:, 0:ch], l_ref[:, ch:ch2], xprev[:, 0:ch], xprev[:, ch:ch2]
        o_ref[:, 0:ch] += _colsum(lr * xr + li * xi)
        o_ref[:, ch:ch2] += _colsum(li * xr - lr * xi)

    return pl.pallas_call(body, name="s5_da", out_shape=SDS((1, ch2), F32), grid=(s // tb,),
                          in_specs=[_rows(tb, ch2), _rows(tb, ch2), _prev8(tb, ch2)], out_specs=_const((1, ch2)),
                          compiler_params=_params(1))(lmb, x, x)


def _assemble_dproj(da, dqkv, dbg, dcx, dcg, du, dus, ddg, tb):
    s = da.shape[0]

    def body(da_ref, q0, q1, q2, k0, k1, k2, v0, v1, v2, dbg_ref, dcx_ref, dcg_ref, du_ref, dus_ref, ddg_ref, o_ref):
        o_ref[:, 0:4 * BR] = da_ref[...]
        for j, parts in enumerate(((q0, q1, q2), (k0, k1, k2), (v0, v1, v2))):
            o_ref[:, (4 + j) * BR:(5 + j) * BR] = (parts[0][...] + parts[1][...] + parts[2][...]).astype(MXU_DTYPE)
        o_ref[:, 7 * BR:8 * BR] = dbg_ref[...].astype(MXU_DTYPE)
        o_ref[:, 8 * BR:9 * BR] = dcx_ref[...].astype(MXU_DTYPE)
        o_ref[:, 9 * BR:10 * BR] = dcg_ref[...].astype(MXU_DTYPE)
        o_ref[:, 10 * BR:11 * BR] = (du_ref[...] + dus_ref[...]).astype(MXU_DTYPE)
        o_ref[:, 11 * BR:12 * BR] = ddg_ref[...].astype(MXU_DTYPE)

    flat = [t for grp in dqkv for t in grp]
    return pl.pallas_call(
        body, name="assemble_dproj", out_shape=SDS((s, N_IN), MXU_DTYPE), grid=(s // tb,),
        in_specs=[_rows(tb, 4 * BR)] + [_rows(tb, BR)] * 15, out_specs=_rows(tb, N_IN),
        compiler_params=_params(1))(da, *flat, dbg, dcx, dcg, du, dus, ddg)


def _sum_leading(xs, tr, name):
    n, r, c = xs[0].shape
    nl = len(xs)
    tr = min(tr, r)
    nr = r // tr
    assert r % tr == 0, (name, r, tr)

    def body(*refs):
        i = pl.program_id(0)
        for l in range(nl):
            @pl.when((i >= l * nr) & (i < (l + 1) * nr))
            def _():
                acc = refs[l * n][...].astype(F32)
                for ref in refs[l * n + 1:(l + 1) * n]:
                    acc = acc + ref[...].astype(F32)
                refs[nl * n][...] = acc

    specs = [pl.BlockSpec((None, tr, c), functools.partial(lambda i, k, l: (k, jnp.clip(i - l * nr, 0, nr - 1), 0), k=k, l=l))
             for l in range(nl) for k in range(n)]
    return pl.pallas_call(body, name=name, out_shape=SDS((nl * r, c), F32), grid=(nl * nr,), in_specs=specs,
                          out_specs=pl.BlockSpec((tr, c), lambda i: (i, 0)),
                          compiler_params=_params(1))(*[x for x in xs for _ in range(n)])


def _adamw(w, g_parts, m, v, tr, name):
    r, c = w.shape
    tr = min(tr, r)
    n = len(g_parts)
    assert r % tr == 0, (name, r, tr)

    def body(*refs):
        w_ref, m_ref, v_ref = refs[0], refs[1 + n], refs[2 + n]
        g_ref, d_ref, nm_ref, nv_ref = refs[3 + n:]
        g = refs[1][...]
        for ref in refs[2:1 + n]:
            g = g + ref[...]
        mm = ADAM_B1 * m_ref[...] + (1.0 - ADAM_B1) * g
        vv = ADAM_B2 * v_ref[...] + (1.0 - ADAM_B2) * jnp.square(g)
        m_hat = mm / (1.0 - ADAM_B1 ** ADAM_STEP)
        v_hat = vv / (1.0 - ADAM_B2 ** ADAM_STEP)
        g_ref[...] = g
        d_ref[...] = -ADAM_LR * (m_hat / (jnp.sqrt(v_hat) + ADAM_EPS) + ADAM_WD * w_ref[...])
        nm_ref[...] = mm
        nv_ref[...] = vv

    spec = pl.BlockSpec((tr, c), lambda i: (i, 0))
    return pl.pallas_call(body, name=name, out_shape=(SDS((r, c), F32),) * 4, grid=(r // tr,),
                          in_specs=[spec] * (3 + n), out_specs=(spec,) * 4,
                          compiler_params=_params(1))(w, *g_parts, m, v)


def _allgather8(block, name):
    m_per, n = block.shape

    def body(x_ref, out_ref, send_sems, recv_sems, local_sem):
        x, y, c = lax.axis_index("x"), lax.axis_index("y"), lax.axis_index("c")
        me, sibling = (x, y, c), (x, y, 1 - c)
        chips = [(1 - x, y), (x, 1 - y), (1 - x, 1 - y)]

        def rows(px, py, pc):
            return out_ref.at[pl.ds((4 * px + 2 * py + pc) * m_per, m_per), :]

        def copy(k, blk, to, src=None):
            return pltpu.make_async_remote_copy(
                src_ref=rows(*blk) if src is None else src, dst_ref=rows(*blk), send_sem=send_sems.at[k],
                recv_sem=recv_sems.at[k], device_id=to, device_id_type=MESH)

        mine = pltpu.make_async_copy(x_ref, rows(*me), local_sem)
        mine.start()
        first = [copy(0, me, sibling, src=x_ref)]
        first += [copy(1 + j, me, (*chip, c), src=x_ref) for j, chip in enumerate(chips)]
        for cp in first:
            cp.start()
        passed = [copy(4 + j, (*chip, c), sibling) for j, chip in enumerate(chips)]
        for j, chip in enumerate(chips):
            copy(1 + j, (*chip, c), me).wait_recv()
            passed[j].start()
        copy(0, sibling, me).wait_recv()
        for j, chip in enumerate(chips):
            copy(4 + j, (*chip, 1 - c), me).wait_recv()
        for cp in first + passed:
            cp.wait_send()
        mine.wait()

    return pl.pallas_call(
        body, name=name, out_shape=SDS((N_DEV * m_per, n), block.dtype),
        in_specs=[pl.BlockSpec(memory_space=pltpu.VMEM)], out_specs=pl.BlockSpec(memory_space=pltpu.VMEM),
        scratch_shapes=[pltpu.SemaphoreType.DMA((7,)), pltpu.SemaphoreType.DMA((7,)), pltpu.SemaphoreType.DMA],
        compiler_params=_params())(block)


class _Exchange:
    def __init__(self, items, out_shapes):
        self.items, self.out_shapes = list(items), tuple(out_shapes)
        self.arrays = [it[0] for it in self.items]
        n = len(self.items)
        self.n_in, self.n_out = n, len(self.out_shapes)
        self.scratch = [pltpu.SemaphoreType.DMA((n * N_CHIPS,)), pltpu.SemaphoreType.DMA((n * N_CHIPS,)),
                        pltpu.SemaphoreType.DMA((n,))]

    def _copies(self, ins, outs, sems, m):
        send_sems, recv_sems, local_sems = sems
        c = lax.axis_index("c")
        others = [j for j in range(N_CHIPS) if j != m]

        def remote(a, src, dst, to, from_):
            return pltpu.make_async_remote_copy(
                src_ref=src, dst_ref=dst, send_sem=send_sems.at[a * N_CHIPS + to],
                recv_sem=recv_sems.at[a * N_CHIPS + from_], device_id=(to // 2, to % 2, c), device_id_type=MESH)

        local, sends, recvs = [], [], []
        for a, (_, oi, src_of, dst_of) in enumerate(self.items):
            local.append(pltpu.make_async_copy(src_of(ins[a], m), dst_of(outs[oi], m), local_sems.at[a]))
            for j in others:
                sends.append(remote(a, src_of(ins[a], j), dst_of(outs[oi], m), j, m))
                recvs.append(remote(a, src_of(ins[a], m), dst_of(outs[oi], j), j, j))
        return local, sends, recvs

    def _on_my_chip(self, fn):
        chip = 2 * lax.axis_index("x") + lax.axis_index("y")
        for m in range(N_CHIPS):
            pl.when(chip == m)(functools.partial(fn, m))

    def start(self, ins, outs, sems):
        def go(m):
            local, sends, _ = self._copies(ins, outs, sems, m)
            for cp in local + sends:
                cp.start()
        self._on_my_chip(go)

    def wait(self, ins, outs, sems):
        def go(m):
            local, sends, recvs = self._copies(ins, outs, sems, m)
            for cp in recvs:
                cp.wait_recv()
            for cp in sends:
                cp.wait_send()
            for cp in local:
                cp.wait()
        self._on_my_chip(go)


def _chip_exchange(items, out_shapes, name):
    ex = _Exchange(items, out_shapes)

    def body(*refs):
        ins, outs, sems = refs[:ex.n_in], refs[ex.n_in:ex.n_in + ex.n_out], refs[ex.n_in + ex.n_out:]
        ex.start(ins, outs, sems)
        ex.wait(ins, outs, sems)

    return pl.pallas_call(
        body, name=name, out_shape=ex.out_shapes, in_specs=[ANY] * ex.n_in, out_specs=(ANY,) * ex.n_out,
        scratch_shapes=ex.scratch, compiler_params=_params())(*ex.arrays)


def _sibling_swap(arrays, name):
    n = len(arrays)

    def body(*refs):
        ins, outs = refs[:n], refs[n:2 * n]
        send_sems, recv_sems = refs[2 * n:]
        peer = (lax.axis_index("x"), lax.axis_index("y"), 1 - lax.axis_index("c"))
        cps = [pltpu.make_async_remote_copy(src_ref=ins[a], dst_ref=outs[a], send_sem=send_sems.at[a],
                                            recv_sem=recv_sems.at[a], device_id=peer, device_id_type=MESH)
               for a in range(n)]
        for cp in cps:
            cp.start()
        for cp in cps:
            cp.wait()

    return pl.pallas_call(
        body, name=name, out_shape=tuple(SDS(a.shape, a.dtype) for a in arrays), in_specs=[ANY] * n,
        out_specs=(ANY,) * n, scratch_shapes=[pltpu.SemaphoreType.DMA((n,)), pltpu.SemaphoreType.DMA((n,))],
        compiler_params=_params())(*arrays)


def _block_diag(w):
    h, n, m = w.shape
    eye = jnp.eye(h, dtype=w.dtype)
    return (w[:, :, None, :] * eye[:, None, :, None]).reshape(h * n, h * m)


def _diag_blocks(d, h, col0=0, ncols=None):
    ncols = d.shape[1] - col0 if ncols is None else ncols
    n, m = d.shape[0] // h, ncols // h
    lanes = 128
    assert m <= lanes and lanes % m == 0 and col0 % lanes == 0

    def body(d_ref, o_ref):
        for g in range(h):
            c = col0 + g * m
            chunk = d_ref[g * n:(g + 1) * n, c // lanes * lanes:c // lanes * lanes + lanes]
            o_ref[g * n:(g + 1) * n, :] = chunk[:, c % lanes:c % lanes + m]

    out = pl.pallas_call(body, name="diag_blocks", out_shape=SDS((h * n, m), d.dtype), compiler_params=_params())(d)
    return out.reshape(h, n, m)


def _tiles(s):
    return dict(tb=min(512, s), tln=min(256, s), tscan=min(256, s))


def _layer_weights(p, l):
    pad8 = lambda w: jnp.pad(w, ((0, 8 - w.shape[0]), (0, 0)))
    return dict(
        conv_a=pad8(p["conv_a"][l]), conv_c=pad8(p["conv_c"][l]), conv_c_b=p["conv_c_b"][l][None],
        w_cat=jnp.concatenate([_block_diag(p["lru_wa"][l]), _block_diag(p["lru_wx"][l])], axis=1).astype(MXU_DTYPE),
        b_cat=jnp.concatenate([p["lru_ba"][l], p["lru_bx"][l]])[None], lam=p["lru_lambda"][l][None],
        lam_re=p["s5_lam_re"][l], lam_im=p["s5_lam_im"][l], log_dt=p["s5_log_dt"][l][:, None],
        b_re=p["s5_b_re"][l].reshape(S5_N, S5_CH), b_im=p["s5_b_im"][l].reshape(S5_N, S5_CH),
        c_re=p["s5_c_re"][l], c_im=p["s5_c_im"][l], d_skip=p["s5_d"][l][None], b_glu=p["s5_b_glu"][l][None],
        ln_g=p["ln_g"][l][None], ln_b=p["ln_b"][l][None])


def _s5_matrices(lw):
    ab_re, ab_im, f_re, f_im = _s5_disc_fwd(lw["lam_re"], lw["lam_im"], lw["log_dt"])
    f_re, f_im = f_re.reshape(S5_N, 1), f_im.reshape(S5_N, 1)
    bb_re, bb_im = _s5_bbar_fwd(f_re, f_im, lw["b_re"], lw["b_im"])
    to_bd = lambda bb: _block_diag(jnp.swapaxes(bb.reshape(S5_GROUPS, S5_STATE, S5_CH), 1, 2))
    bmat = jnp.concatenate([to_bd(bb_re), to_bd(bb_im)], axis=1).astype(MXU_DTYPE)
    cmat_t = jnp.concatenate([_block_diag(lw["c_re"]), -_block_diag(lw["c_im"])], axis=1).astype(MXU_DTYPE)
    a_row = jnp.concatenate([ab_re.reshape(1, S5_N), ab_im.reshape(1, S5_N)], axis=1)
    return dict(f_re=f_re, f_im=f_im, bmat=bmat, bmat_t=bmat.T, cmat_t=cmat_t, cmat=cmat_t.T, a_row=a_row)


def _mm_hooked(hook, *args, **kw):
    if hook is None:
        return _mm(*args, **kw)
    out = _mm(*args, carry=hook[0], **kw)
    hook[1](out[1:])
    return out[0]


def _layer_fwd(x, ada, w_in, get_rest, lw, s5m, bias_tabs, hooks=None):
    s = x.shape[0]
    t = _tiles(s)
    tb = t["tb"]
    shift, scale, gate = ada
    hooks = hooks or {}
    h = _modulate(x, scale, shift, tb)
    proj = _mm_hooked(hooks.get("in_proj"), h, w_in, name="in_proj", tm=1024, tn=1024, tk=D_MODEL)
    w_out, w_glu = get_rest()
    y_a = _branch_a_fwd(proj, lw["conv_a"], tb)
    os_, lses = [], []
    for g, (_, dil) in enumerate(DILATIONS):
        o, lse = _attn_fwd(proj, bias_tabs[g], dil)
        os_.append(o)
        lses.append(lse)
    y_b = _attn_combine(os_, lses, proj, tb)
    lru_a, lru_b = _lru_gates_fwd(proj, lw["conv_c"], lw["conv_c_b"], lw["w_cat"], lw["b_cat"], lw["lam"], tb)
    lru_h = _scan_real(lru_a, lru_b, reverse=False, tb=tb, name="lru_scan")
    y_c = _gate_out(lru_h, proj, CB_CG, tb, "lru_out")
    bu = _mm(proj, s5m["bmat"], name="s5_bu", a_col0=CB_DU * BR, a_ncols=BR, tn=1024)
    s5_x = _scan_cplx(bu, s5m["a_row"], reverse=False, tb=t["tscan"], name="s5_scan")
    ylin = _mm(s5_x, s5m["cmat"], name="s5_cx", tk=1024)
    y_d = _s5_tail_fwd(ylin, proj, lw["d_skip"], w_glu, lw["b_glu"], tb)
    ycat = jnp.concatenate([y_a, y_b, y_c, y_d], axis=1)
    x_next, xhat, y, rstd = _out_ln(ycat, w_out, x, gate, lw["ln_g"], lw["ln_b"], t["tln"])
    saved = dict(x=x, h=h, proj=proj, os=os_, lses=lses, lru_a=lru_a, lru_h=lru_h, s5_x=s5_x, ylin=ylin, ycat=ycat,
                 xhat=xhat, y=y, rstd=rstd)
    return x_next, saved


def _layer_bwd(dxn, sv, ada, w_in, w_out, w_glu, lw, s5m, bias_tabs, head_ones, hooks=None):
    s = dxn.shape[0]
    t = _tiles(s)
    tb = t["tb"]
    shift, scale, gate = ada
    proj = sv["proj"]
    g = {}
    hook = lambda name: hooks[name](g) if hooks and name in hooks else None
    dyb, dxa, g["ln_g"], g["ln_b"], dgate = _ln_bwd(dxn, sv["xhat"], sv["y"], sv["rstd"], lw["ln_g"], gate, t["tln"])
    g["w_out"] = _mm_hooked(hook("dw_out"), sv["ycat"], dyb, name="dw_out", ta=True, out_dtype=WIRE_DTYPE,
                            tm=1024, tn=1024, tk=1024)
    dycat = _mm(dyb, w_out, name="dycat", tb=True, tm=1024, tn=1024, tk=D_MODEL)
    da, dconv_a = _branch_a_bwd(dycat, proj, lw["conv_a"], tb)
    g["conv_a"] = dconv_a[0:3]
    pre = _attn_bwd_pre(dycat, sv["os"], sv["lses"], proj, head_ones, tb)
    dbg, dos, dms = pre[0], pre[1:4], pre[4:7]
    dqkv, dbias = [], []
    for gi, (_, dil) in enumerate(DILATIONS):
        dq, dk, dv, dbi = _attn_bwd(proj, dos[gi], sv["lses"][gi], dms[gi], bias_tabs[gi], dil)
        dqkv.append((dq, dk, dv))
        dbias.append(dbi)
    dqkv = list(zip(*dqkv))
    dh, dcg = _gate_out_bwd(dycat, 2, sv["lru_h"], proj, CB_CG, tb, "lru_out_bwd")
    lmb = _scan_real(sv["lru_a"], dh, reverse=True, tb=tb, name="lru_scan_bwd")
    dxc, dpre, xcb, dbcat, dlam = _lru_gates_bwd(proj, lmb, sv["lru_h"], lw["conv_c"], lw["conv_c_b"], lw["w_cat"],
                                                  lw["b_cat"], lw["lam"], tb)
    dwcat = _mm(xcb, dpre, name="dw_lru", ta=True, tn=1024)
    g["lru_wa"] = _diag_blocks(dwcat, LRU_HEADS, 0, BR)
    g["lru_wx"] = _diag_blocks(dwcat, LRU_HEADS, BR, BR)
    g["lru_ba"], g["lru_bx"], g["lru_lambda"] = dbcat[0, 0:BR], dbcat[0, BR:2 * BR], dlam[0]
    dcx, dconv_c, dccb = _conv_c_bwd(dxc, proj, lw["conv_c"], tb)
    g["conv_c"], g["conv_c_b"] = dconv_c[0:4], dccb[0]
    dyl, dus, ddg, gb, dtb, ddk, dbglu = _s5_tail_bwd(dycat, sv["ylin"], proj, lw["d_skip"], w_glu, lw["b_glu"], tb)
    g["s5_d"], g["s5_b_glu"] = ddk[0], dbglu[0]
    g["s5_w_glu"] = _mm(gb, dtb, name="dw_glu", ta=True, out_dtype=WIRE_DTYPE)
    dxd = _mm(dyl, s5m["cmat_t"], name="s5_dx", tk=BR, tn=1024)
    s5_l = _scan_cplx(dxd, s5m["a_row"], reverse=True, tb=t["tscan"], name="s5_scan_bwd")
    dab = _s5_da(s5_l, sv["s5_x"], t["tscan"])
    dbmat = _mm(proj, s5_l, name="dw_s5_b", ta=True, a_col0=CB_DU * BR, a_ncols=BR, tn=1024)
    dcmat_t = _mm(dyl, sv["s5_x"], name="dw_s5_c", ta=True, tn=1024)
    du = _mm(s5_l, s5m["bmat_t"], name="s5_du", tk=1024)
    from_bd = lambda col0: jnp.swapaxes(_diag_blocks(dbmat, S5_GROUPS, col0, S5_N), 1, 2).reshape(S5_N, S5_CH)
    df_re, df_im, db_re, db_im = _s5_bbar_bwd(s5m["f_re"], s5m["f_im"], lw["b_re"], lw["b_im"],
                                              from_bd(0), from_bd(S5_N))
    shp = (S5_GROUPS, S5_STATE)
    g["s5_lam_re"], g["s5_lam_im"], dlog_dt = _s5_disc_bwd(
        lw["lam_re"], lw["lam_im"], lw["log_dt"],
        (dab[:, 0:S5_N].reshape(shp), dab[:, S5_N:].reshape(shp), df_re.reshape(shp), df_im.reshape(shp)))
    g["s5_log_dt"] = dlog_dt[:, 0]
    g["s5_b_re"] = db_re.reshape(S5_GROUPS, S5_STATE, S5_CH)
    g["s5_b_im"] = db_im.reshape(S5_GROUPS, S5_STATE, S5_CH)
    g["s5_c_re"] = _diag_blocks(dcmat_t, S5_GROUPS, 0, S5_N)
    g["s5_c_im"] = -_diag_blocks(dcmat_t, S5_GROUPS, S5_N, S5_N)
    dproj = _assemble_dproj(da, dqkv, dbg, dcx, dcg, du, dus, ddg, tb)
    g["w_in"] = _mm_hooked(hook("dw_in"), sv["h"], dproj, name="dw_in", ta=True, out_dtype=WIRE_DTYPE,
                           tm=1024, tn=1536, tk=1024)
    dhm = _mm_hooked(hook("dh"), dproj, w_in, name="dh", tb=True, tm=1024, tn=1024, tk=1536)
    dx, dshift, dscale = _mod_bwd(dhm, dxa, sv["x"], scale, tb)
    g["ada"] = jnp.concatenate([dshift[0], dscale[0], dgate[0]])
    return dx, g, dbias


SMALL = ("rel_bias", "conv_a", "conv_c", "conv_c_b", "lru_wa", "lru_ba", "lru_wx", "lru_bx", "lru_lambda",
         "s5_lam_re", "s5_lam_im", "s5_log_dt", "s5_b_re", "s5_b_im", "s5_c_re", "s5_c_im", "s5_d", "s5_b_glu",
         "ln_g", "ln_b")
PER_LAYER_SMALL = SMALL[1:]


def _local_step(x, target, ada_rows, w_in, w_out, w_glu, p, comm=None):
    if comm is None:
        get_w_in = lambda l: w_in[l]
        get_rest = lambda l: (w_out[l], w_glu[l])
        fwd_hooks = bwd_hooks = lambda *_: None
    else:
        get_w_in, get_rest, fwd_hooks, bwd_hooks = comm.w_in, comm.rest, comm.fwd_hooks, comm.bwd_hooks
    s = x.shape[0]
    buckets = _bucket_maps()
    bias_tabs = _bias_tables(p["rel_bias"], buckets)
    head_ones = _block_diag(jnp.ones((ATT_HEADS, HEAD_DIM, HEAD_DIM), MXU_DTYPE))
    lws = [_layer_weights(p, l) for l in range(DEPTH)]
    s5ms = [_s5_matrices(lw) for lw in lws]
    adas = [tuple(ada_rows[l, k * D_MODEL:(k + 1) * D_MODEL][None] for k in range(3)) for l in range(DEPTH)]
    saved = []
    for l in range(DEPTH):
        x, sv = _layer_fwd(x, adas[l], get_w_in(l), functools.partial(get_rest, l), lws[l], s5ms[l], bias_tabs,
                           fwd_hooks(l))
        saved.append(sv)
    loss, dx = _loss_head(x, target, _tiles(s)["tb"])
    grads = [None] * DEPTH
    dbias_sum = []
    for l in reversed(range(DEPTH)):
        dx, grads[l], dbias = _layer_bwd(dx, saved[l], adas[l], get_w_in(l), *get_rest(l), lws[l], s5ms[l],
                                         bias_tabs, head_ones, bwd_hooks(l, grads))
        dbias_sum.append(jnp.stack(dbias))
    drel = _rel_bias_grad(jnp.stack(dbias_sum), buckets)[:, 0:ATT_HEADS]
    small = {n: jnp.stack([grads[l][n] for l in range(DEPTH)]) for n in PER_LAYER_SMALL + ("ada",)}
    small["rel_bias"] = drel
    big = {n: [grads[l][n] for l in range(DEPTH)] for n in ("w_in", "w_out", "s5_w_glu")}
    return loss, dx, big, small


PACK_ROWS = 256


def _pack(parts):
    flat = jnp.concatenate([t.reshape(-1).astype(F32) for t in parts])
    n = flat.shape[0]
    rows = -(-n // (PACK_ROWS * 128)) * PACK_ROWS
    return jnp.pad(flat, (0, rows * 128 - n)).reshape(rows, 128)


def _unpack(packed, shapes):
    flat = packed.reshape(packed.shape[:-2] + (-1,))
    out, off = [], 0
    for shp in shapes:
        size = math.prod(shp)
        out.append(flat[..., off:off + size].reshape(flat.shape[:-1] + tuple(shp)))
        off += size
    return out


def _take_cols(t, chip, width):
    return lax.dynamic_slice_in_dim(t, chip * width, width, axis=t.ndim - 1)


class _Comm:
    IN_W, OUT_R, GLU_R = N_IN // N_CHIPS, D_MODEL // N_CHIPS, BR // N_CHIPS

    def __init__(self, w_in_b, w_out_b, w_glu_b):
        assert DEPTH == 2
        self.shards = (w_in_b, w_out_b, w_glu_b)
        in_w = self.IN_W
        self.w_in_full = {0: _chip_exchange(
            [(w_in_b, 0, lambda ref, j: ref.at[0], lambda ref, j: ref.at[:, pl.ds(j * in_w, in_w)])],
            [SDS((D_MODEL, N_IN), WIRE_DTYPE)], "gather_w_in0")[0]}
        self.w_out_full = self.w_glu_full = None
        self.recv = {}

    def w_in(self, l):
        return self.w_in_full[l]

    def rest(self, l):
        return self.w_out_full[l], self.w_glu_full[l]

    def fwd_hooks(self, l):
        if l != 0:
            return None
        w_in_b, w_out_b, w_glu_b = self.shards
        in_w, out_r, glu_r = self.IN_W, self.OUT_R, self.GLU_R
        whole = lambda ref, j: ref
        items = [(w_out_b, 0, whole, lambda ref, j: ref.at[:, pl.ds(j * out_r, out_r), :]),
                 (w_glu_b, 1, whole, lambda ref, j: ref.at[:, pl.ds(j * glu_r, glu_r), :]),
                 (w_in_b, 2, lambda ref, j: ref.at[1], lambda ref, j: ref.at[:, pl.ds(j * in_w, in_w)])]
        shapes = [SDS((DEPTH, D_MODEL, D_MODEL), WIRE_DTYPE), SDS((DEPTH, BR, BR), WIRE_DTYPE),
                  SDS((D_MODEL, N_IN), WIRE_DTYPE)]

        def done(got):
            self.w_out_full, self.w_glu_full, self.w_in_full[1] = got

        return {"in_proj": (_Exchange(items, shapes), done)}

    def _scatter(self, parts):
        in_w, out_r, glu_r = self.IN_W, self.OUT_R, self.GLU_R
        cut = {"w_in": (lambda ref, j: ref.at[:, pl.ds(j * in_w, in_w)], (D_MODEL, in_w)),
               "w_out": (lambda ref, j: ref.at[pl.ds(j * out_r, out_r), :], (out_r, D_MODEL)),
               "s5_w_glu": (lambda ref, j: ref.at[pl.ds(j * glu_r, glu_r), :], (glu_r, BR))}
        items = [(arr, oi, cut[name][0], lambda ref, j: ref.at[j]) for oi, (name, _, arr) in enumerate(parts)]
        shapes = [SDS((N_CHIPS,) + cut[name][1], WIRE_DTYPE) for name, _, _ in parts]

        def done(got):
            for (name, l, _), arr in zip(parts, got):
                self.recv[name, l] = arr

        return _Exchange(items, shapes), done

    def bwd_hooks(self, l, grads):
        if l != 0:
            return None
        g1 = grads[1]
        return {"dw_out": lambda g: self._scatter([("w_out", 1, g1["w_out"]), ("s5_w_glu", 1, g1["s5_w_glu"])]),
                "dw_in": lambda g: self._scatter([("w_in", 1, g1["w_in"])]),
                "dh": lambda g: self._scatter([("w_in", 0, g["w_in"]), ("w_out", 0, g["w_out"]),
                                               ("s5_w_glu", 0, g["s5_w_glu"])])}


def kernel(x, c, rel_bias, w_ada, b_ada, w_in, conv_a, conv_c, conv_c_b, lru_wa, lru_ba, lru_wx, lru_bx, lru_lambda, s5_lam_re, s5_lam_im, s5_log_dt, s5_b_re, s5_b_im, s5_c_re, s5_c_im, s5_d, s5_w_glu, s5_b_glu, w_out, ln_g, ln_b, loss_target, m_rel_bias, m_w_ada, m_b_ada, m_w_in, m_conv_a, m_conv_c, m_conv_c_b, m_lru_wa, m_lru_ba, m_lru_wx, m_lru_bx, m_lru_lambda, m_s5_lam_re, m_s5_lam_im, m_s5_log_dt, m_s5_b_re, m_s5_b_im, m_s5_c_re, m_s5_c_im, m_s5_d, m_s5_w_glu, m_s5_b_glu, m_w_out, m_ln_g, m_ln_b, v_rel_bias, v_w_ada, v_b_ada, v_w_in, v_conv_a, v_conv_c, v_conv_c_b, v_lru_wa, v_lru_ba, v_lru_wx, v_lru_bx, v_lru_lambda, v_s5_lam_re, v_s5_lam_im, v_s5_log_dt, v_s5_b_re, v_s5_b_im, v_s5_c_re, v_s5_c_im, v_s5_d, v_s5_w_glu, v_s5_b_glu, v_w_out, v_ln_g, v_ln_b):
    args = dict(locals())
    names = ("rel_bias", "w_ada", "b_ada", "w_in", "conv_a", "conv_c", "conv_c_b", "lru_wa", "lru_ba", "lru_wx",
             "lru_bx", "lru_lambda", "s5_lam_re", "s5_lam_im", "s5_log_dt", "s5_b_re", "s5_b_im", "s5_c_re", "s5_c_im",
             "s5_d", "s5_w_glu", "s5_b_glu", "w_out", "ln_g", "ln_b")
    w = {n: args[n] for n in names}
    mom = {n: args["m_" + n] for n in names}
    var = {n: args["v_" + n] for n in names}
    chip = 2 * lax.axis_index("x") + lax.axis_index("y")
    me = 2 * chip + lax.axis_index("c")
    ada_w = 3 * D_MODEL // N_CHIPS
    in_w = N_IN // N_CHIPS
    out_r = D_MODEL // N_CHIPS
    glu_r = BR // N_CHIPS
    conv_w = BR // N_CHIPS

    comm = _Comm(w["w_in"].astype(WIRE_DTYPE), w["w_out"].astype(WIRE_DTYPE), w["s5_w_glu"].astype(WIRE_DTYPE))

    taps = jnp.concatenate([w["conv_a"].reshape(DEPTH * 3, conv_w), w["conv_c"].reshape(DEPTH * 4, conv_w)])
    first = jnp.concatenate([c, jnp.pad(taps, ((0, 1), (0, D_MODEL - conv_w)))])
    got = _allgather8(first, "gather_c_taps").reshape(N_CHIPS, 2, 16, D_MODEL)
    c_all = got[:, :, 0].reshape(N_DEV, D_MODEL)
    taps_all = jnp.transpose(got[:, 0, 1:1 + DEPTH * 7, 0:conv_w], (1, 0, 2)).reshape(DEPTH * 7, BR)
    conv_a_f = taps_all[0:DEPTH * 3].reshape(DEPTH, 3, BR)
    conv_c_f = taps_all[DEPTH * 3:].reshape(DEPTH, 4, BR)

    cond_all = _silu_rows(c_all)
    ada_part = jnp.stack([_mm(cond_all, w["w_ada"][l], name="ada_fwd", tk=D_MODEL, tn=512,
                              bias=_take_cols(w["b_ada"][l][None], chip, ada_w)) for l in range(DEPTH)])
    ada_all = _allgather8(ada_part.reshape(DEPTH * N_DEV, ada_w), "gather_ada")
    ada_all = ada_all.reshape(N_CHIPS, 2, DEPTH, N_DEV, ada_w)[:, 0]
    ada_rows = lax.dynamic_index_in_dim(ada_all, me, axis=2, keepdims=False)
    ada_rows = jnp.transpose(ada_rows, (1, 0, 2)).reshape(DEPTH, 3 * D_MODEL)

    p = dict(w)
    p["conv_a"], p["conv_c"] = conv_a_f, conv_c_f
    loss, dx, _, small = _local_step(x[0], loss_target[0], ada_rows, None, None, None, p, comm)

    sums = [_sum_leading([comm.recv[name, l] for l in range(DEPTH)], 256, "sum_chips")
            for name in ("w_in", "w_out", "s5_w_glu")]
    others = _sibling_swap(sums, "swap_cores")
    out = {}
    for name, mine, other in zip(("w_in", "w_out", "s5_w_glu"), sums, others):
        shp = w[name].shape
        flat = lambda t: t.reshape(-1, shp[-1])
        res = _adamw(flat(w[name]), [mine, other], flat(mom[name]), flat(var[name]), 128, "adamw_big")
        out[name] = [t.reshape(shp) for t in res]

    small_names = SMALL + ("ada",)
    small["loss"] = loss
    order = small_names + ("loss",)
    shapes = [small[n].shape for n in order]
    gathered = _allgather8(_pack([small[n] for n in order]), "gather_small")
    gathered = gathered.reshape(N_DEV, -1, 128)
    total = dict(zip(order, _unpack(_sum_leading([gathered], PACK_ROWS, "sum_devices"), shapes)))
    d_ada_all = _unpack(gathered, shapes)[order.index("ada")]
    g_small = {n: total[n] for n in SMALL}
    g_small["conv_a"] = _take_cols(total["conv_a"], chip, conv_w)
    g_small["conv_c"] = _take_cols(total["conv_c"], chip, conv_w)
    g_small["b_ada"] = total["ada"]
    g_w_ada = jnp.stack([_mm(cond_all, _take_cols(d_ada_all[:, l], chip, ada_w), name="dw_ada", ta=True, tn=ada_w)
                         for l in range(DEPTH)])
    upd_names = SMALL + ("b_ada",)
    upd_shapes = [w[n].shape for n in upd_names]
    res = _adamw(_pack([w[n] for n in upd_names]), [_pack([g_small[n] for n in upd_names])],
                 _pack([mom[n] for n in upd_names]), _pack([var[n] for n in upd_names]), PACK_ROWS, "adamw_small")
    for k, t in enumerate(res):
        for n, val in zip(upd_names, _unpack(t, upd_shapes)):
            out.setdefault(n, [None] * 4)[k] = val
    shp = w["w_ada"].shape
    flat = lambda t: t.reshape(-1, shp[-1])
    out["w_ada"] = [t.reshape(shp) for t in _adamw(flat(w["w_ada"]), [flat(g_w_ada)], flat(mom["w_ada"]),
                                                  flat(var["w_ada"]), 128, "adamw_ada")]
    return (total["loss"].reshape(()), dx[None]) + tuple(out[n][k] for k in range(4) for n in names)
```

```python
import functools
import math

import jax
import jax.numpy as jnp
from jax import lax
from jax.experimental import pallas as pl
from jax.experimental.pallas import tpu as pltpu

F32 = jnp.float32
MXU_DTYPE = jnp.bfloat16
WIRE_DTYPE = jnp.bfloat16
SDS = jax.ShapeDtypeStruct
MESH = pl.DeviceIdType.MESH
ANY = pl.BlockSpec(memory_space=pl.ANY)
VMEM_LIMIT = 48 * 1024 * 1024

D_MODEL = 2048
DEPTH = 2
BR = 512
ATT_HEADS = 8
HEAD_DIM = 64
DILATIONS = ((128, 1), (512, 4), (2048, 16))
BLK = 128
REL_BUCKETS = 32
REL_MAX_DIST = 2048
LRU_HEADS = 8
LRU_C = 8.0
S5_CH = 16
S5_GROUPS = 32
S5_STATE = 64
S5_N = S5_GROUPS * S5_STATE
N_IN = 12 * BR
ALPHA = (2 * DEPTH) ** 0.25
LN_EPS = 1e-5
NEG = -1e30
ADAM_LR, ADAM_B1, ADAM_B2, ADAM_EPS, ADAM_WD, ADAM_STEP = 0.001, 0.9, 0.999, 1e-08, 0.01, 10
CB_AB, CB_AC, CB_AX, CB_AG, CB_Q, CB_K, CB_V, CB_BG, CB_CX, CB_CG, CB_DU, CB_DG = range(12)
N_CHIPS = 4
N_DEV = 8


def _params(n_axes=0):
    kw = {"dimension_semantics": ("arbitrary",) * n_axes} if n_axes else {}
    return pltpu.CompilerParams(vmem_limit_bytes=VMEM_LIMIT, **kw)


def _rows(tb, w, cb=0):
    return pl.BlockSpec((tb, w), lambda i: (i, cb))


def _prev8(tb, w, cb=0):
    return pl.BlockSpec((8, w), lambda i: (jnp.maximum(i * (tb // 8) - 1, 0), cb))


def _next8(tb, w, n_rows, cb=0):
    return pl.BlockSpec((8, w), lambda i: (jnp.minimum((i + 1) * (tb // 8), n_rows // 8 - 1), cb))


def _const(shape):
    return pl.BlockSpec(shape, lambda *_: (0,) * len(shape))


def _silu(x):
    return x * jax.nn.sigmoid(x)


def _dsilu(x):
    s = jax.nn.sigmoid(x)
    return s * (1.0 + x * (1.0 - s))


def _shift_down(cur, prev8, j):
    rolled = pltpu.roll(cur, j, 0)
    row = lax.broadcasted_iota(jnp.int32, (8, cur.shape[1]), 0)
    first = jnp.where(row < j, pltpu.roll(prev8, j, 0), rolled[0:8])
    return jnp.concatenate([first, rolled[8:]], axis=0)


def _shift_up(cur, next8, j):
    t = cur.shape[0]
    rolled = pltpu.roll(cur, t - j, 0)
    row = lax.broadcasted_iota(jnp.int32, (8, cur.shape[1]), 0)
    last = jnp.where(row >= 8 - j, pltpu.roll(next8, 8 - j, 0), rolled[t - 8:t])
    return jnp.concatenate([rolled[:t - 8], last], axis=0)


def _colsum(x):
    return jnp.sum(x, axis=0, keepdims=True)


def _init_acc(*refs):
    @pl.when(pl.program_id(0) == 0)
    def _():
        for r in refs:
            r[...] = jnp.zeros_like(r)


def _call(body, *, name, out_shape, grid, in_specs, out_specs, scratch_shapes, args, carry=None):
    out_shape, out_specs, in_specs = tuple(out_shape), tuple(out_specs), list(in_specs)
    scratch_shapes = list(scratch_shapes)
    if carry is None:
        return pl.pallas_call(body, name=name, out_shape=out_shape, grid=grid, in_specs=in_specs, out_specs=out_specs,
                              scratch_shapes=scratch_shapes, compiler_params=_params(len(grid)))(*args)
    n_in, n_out, n_scr = len(in_specs), len(out_shape), len(scratch_shapes)

    def wrapped(*refs):
        ins, refs = refs[:n_in], refs[n_in:]
        x_ins, refs = refs[:carry.n_in], refs[carry.n_in:]
        outs, refs = refs[:n_out], refs[n_out:]
        x_outs, refs = refs[:carry.n_out], refs[carry.n_out:]
        scr, x_sems = refs[:n_scr], refs[n_scr:]
        at = [pl.program_id(d) for d in range(len(grid))]
        first = functools.reduce(lambda p, q: p & q, [i == 0 for i in at])
        last = functools.reduce(lambda p, q: p & q, [i == g - 1 for i, g in zip(at, grid)])
        pl.when(first)(lambda: carry.start(x_ins, x_outs, x_sems))
        body(*ins, *outs, *scr)
        pl.when(last)(lambda: carry.wait(x_ins, x_outs, x_sems))

    return pl.pallas_call(
        wrapped, name=name, out_shape=out_shape + carry.out_shapes, grid=grid, in_specs=in_specs + [ANY] * carry.n_in,
        out_specs=out_specs + (ANY,) * carry.n_out, scratch_shapes=scratch_shapes + carry.scratch,
        compiler_params=_params(len(grid)))(*args, *carry.arrays)


def _mm(a, b, *, name, ta=False, tb=False, out_dtype=F32, tm=512, tn=512, tk=512, a_col0=0, a_ncols=None, bias=None,
        carry=None):
    a_ncols = a.shape[1] - a_col0 if a_ncols is None else a_ncols
    m, k = (a_ncols, a.shape[0]) if ta else (a.shape[0], a_ncols)
    n = b.shape[0] if tb else b.shape[1]
    assert k == (b.shape[1] if tb else b.shape[0]), (name, a.shape, b.shape)
    tm, tn, tk = min(tm, m), min(tn, n), min(tk, k)
    nk = k // tk
    a_off = a_col0 // (tm if ta else tk)
    assert m % tm == 0 and n % tn == 0 and k % tk == 0 and a_col0 % (tm if ta else tk) == 0, (name, m, n, k)

    def body(*refs):
        if bias is None:
            a_ref, b_ref, o_ref, acc = refs
        else:
            a_ref, b_ref, bias_ref, o_ref, acc = refs
        kk = pl.program_id(2)

        @pl.when(kk == 0)
        def _():
            acc[...] = jnp.zeros_like(acc)

        dims = (((0 if ta else 1,), (1 if tb else 0,)), ((), ()))
        acc[...] += lax.dot_general(a_ref[...].astype(MXU_DTYPE), b_ref[...].astype(MXU_DTYPE), dims,
                                    preferred_element_type=F32)

        @pl.when(kk == nk - 1)
        def _():
            r = acc[...]
            if bias is not None:
                r = r + bias_ref[...]
            o_ref[...] = r.astype(out_dtype)

    a_spec = (pl.BlockSpec((tk, tm), lambda i, j, kk: (kk, i + a_off)) if ta
              else pl.BlockSpec((tm, tk), lambda i, j, kk: (i, kk + a_off)))
    b_spec = (pl.BlockSpec((tn, tk), lambda i, j, kk: (j, kk)) if tb
              else pl.BlockSpec((tk, tn), lambda i, j, kk: (kk, j)))
    in_specs, args = [a_spec, b_spec], [a, b]
    if bias is not None:
        in_specs.append(pl.BlockSpec((1, tn), lambda i, j, kk: (0, j)))
        args.append(bias)
    out = _call(body, name=name, out_shape=[SDS((m, n), out_dtype)], grid=(m // tm, n // tn, nk), in_specs=in_specs,
                out_specs=[pl.BlockSpec((tm, tn), lambda i, j, kk: (i, j))],
                scratch_shapes=[pltpu.VMEM((tm, tn), F32)], args=args, carry=carry)
    return out[0] if carry is None else out


def _silu_rows(c_all):
    def body(c_ref, o_ref):
        o_ref[...] = _silu(c_ref[...])
    return pl.pallas_call(body, name="cond_silu", out_shape=SDS(c_all.shape, F32))(c_all)


def _modulate(x, scale, shift, tb):
    s, d = x.shape

    def body(x_ref, sc_ref, sh_ref, o_ref):
        o_ref[...] = (x_ref[...] * (1.0 + sc_ref[...]) + sh_ref[...]).astype(MXU_DTYPE)

    return pl.pallas_call(body, name="modulate", out_shape=SDS((s, d), MXU_DTYPE), grid=(s // tb,),
                          in_specs=[_rows(tb, d), _const((1, d)), _const((1, d))], out_specs=_rows(tb, d),
                          compiler_params=_params(1))(x, scale, shift)


def _out_ln(ycat, w_out, x, gate, ln_g, ln_b, tb):
    s, d = x.shape

    def body(yc_ref, w_ref, x_ref, gt_ref, g_ref, b_ref, xn_ref, xh_ref, y_ref, rs_ref):
        y = jnp.dot(yc_ref[...], w_ref[...], preferred_element_type=F32)
        res = ALPHA * x_ref[...] + (1.0 + gt_ref[...]) * y
        mu = jnp.mean(res, axis=-1, keepdims=True)
        cen = res - mu
        var = jnp.mean(cen * cen, axis=-1, keepdims=True)
        rstd = lax.rsqrt(var + LN_EPS)
        xhat = cen * rstd
        xn_ref[...] = xhat * g_ref[...] + b_ref[...]
        xh_ref[...] = xhat
        y_ref[...] = y
        rs_ref[...] = rstd

    big = SDS((s, d), F32)
    return pl.pallas_call(
        body, name="out_proj_ln", out_shape=(big, big, big, SDS((s, 1), F32)), grid=(s // tb,),
        in_specs=[_rows(tb, d), _const((d, d)), _rows(tb, d), _const((1, d)), _const((1, d)), _const((1, d))],
        out_specs=(_rows(tb, d), _rows(tb, d), _rows(tb, d), _rows(tb, 1)), compiler_params=_params(1),
    )(ycat, w_out, x, gate, ln_g, ln_b)


def _ln_bwd(dxn, xhat, y, rstd, ln_g, gate, tb):
    s, d = dxn.shape

    def body(dxn_ref, xh_ref, y_ref, rs_ref, g_ref, gt_ref, dy_ref, dxa_ref, dg_ref, db_ref, dgt_ref):
        _init_acc(dg_ref, db_ref, dgt_ref)
        dxn_t, xh = dxn_ref[...], xh_ref[...]
        dxh = dxn_t * g_ref[...]
        dres = rs_ref[...] * (dxh - jnp.mean(dxh, axis=-1, keepdims=True)
                              - xh * jnp.mean(dxh * xh, axis=-1, keepdims=True))
        dy_ref[...] = ((1.0 + gt_ref[...]) * dres).astype(MXU_DTYPE)
        dxa_ref[...] = ALPHA * dres
        dg_ref[...] += _colsum(dxn_t * xh)
        db_ref[...] += _colsum(dxn_t)
        dgt_ref[...] += _colsum(dres * y_ref[...])

    vec = SDS((1, d), F32)
    return pl.pallas_call(
        body, name="ln_bwd", out_shape=(SDS((s, d), MXU_DTYPE), SDS((s, d), F32), vec, vec, vec), grid=(s // tb,),
        in_specs=[_rows(tb, d), _rows(tb, d), _rows(tb, d), _rows(tb, 1), _const((1, d)), _const((1, d))],
        out_specs=(_rows(tb, d), _rows(tb, d), _const((1, d)), _const((1, d)), _const((1, d))),
        compiler_params=_params(1))(dxn, xhat, y, rstd, ln_g, gate)


def _mod_bwd(dh, dxa, x, scale, tb):
    s, d = dh.shape

    def body(dh_ref, dxa_ref, x_ref, sc_ref, dx_ref, dsh_ref, dsc_ref):
        _init_acc(dsh_ref, dsc_ref)
        dh_t = dh_ref[...]
        dx_ref[...] = dxa_ref[...] + dh_t * (1.0 + sc_ref[...])
        dsh_ref[...] += _colsum(dh_t)
        dsc_ref[...] += _colsum(dh_t * x_ref[...])

    vec = SDS((1, d), F32)
    return pl.pallas_call(
        body, name="mod_bwd", out_shape=(SDS((s, d), F32), vec, vec), grid=(s // tb,),
        in_specs=[_rows(tb, d), _rows(tb, d), _rows(tb, d), _const((1, d))],
        out_specs=(_rows(tb, d), _const((1, d)), _const((1, d))), compiler_params=_params(1))(dh, dxa, x, scale)


def _loss_head(y, target, tb):
    s, d = y.shape

    def body(y_ref, t_ref, l_ref, dy_ref):
        _init_acc(l_ref)
        err = y_ref[...] - t_ref[...]
        l_ref[...] += (0.5 / d) * jnp.sum(err * err, keepdims=True)
        dy_ref[...] = err * (1.0 / d)

    return pl.pallas_call(body, name="loss_head", out_shape=(SDS((1, 1), F32), SDS((s, d), F32)), grid=(s // tb,),
                          in_specs=[_rows(tb, d), _rows(tb, d)], out_specs=(_const((1, 1)), _rows(tb, d)),
                          compiler_params=_params(1))(y, target)


def _conv_taps(u, up, w_ref, width):
    out = w_ref[width - 1:width, :] * u
    for j in range(width - 2, -1, -1):
        out = out + w_ref[j:j + 1, :] * _shift_down(u, up, width - 1 - j)
    return out


def _conv_taps_t(g, gn, w_ref, width):
    out = w_ref[width - 1:width, :] * g
    for j in range(width - 2, -1, -1):
        out = out + w_ref[j:j + 1, :] * _shift_up(g, gn, width - 1 - j)
    return out


def _conv_wgrad(dw_ref, g, u, up, width):
    dw_ref[width - 1:width, :] += _colsum(g * u)
    for j in range(width - 1):
        dw_ref[j:j + 1, :] += _colsum(g * _shift_down(u, up, width - 1 - j))


def _branch_a_fwd(proj, conv_w, tb):
    s = proj.shape[0]

    def body(ab, ac, ax, ag, acp, axp, w_ref, o_ref):
        has_prev = (pl.program_id(0) > 0).astype(F32)
        u = ac[...] * ax[...]
        up = acp[...] * axp[...] * has_prev
        o_ref[...] = (ab[...] * _conv_taps(u, up, w_ref, 3) * _silu(ag[...])).astype(MXU_DTYPE)

    return pl.pallas_call(
        body, name="branch_a_fwd", out_shape=SDS((s, BR), MXU_DTYPE), grid=(s // tb,),
        in_specs=[_rows(tb, BR, CB_AB), _rows(tb, BR, CB_AC), _rows(tb, BR, CB_AX), _rows(tb, BR, CB_AG),
                  _prev8(tb, BR, CB_AC), _prev8(tb, BR, CB_AX), _const((8, BR))],
        out_specs=_rows(tb, BR), compiler_params=_params(1))(proj, proj, proj, proj, proj, proj, conv_w)


def _branch_a_bwd(dycat, proj, conv_w, tb):
    s = proj.shape[0]

    def body(dy, dyn, ab, abn, ag, agn, ac, acp, ax, axp, w_ref, o_ref, dw_ref):
        _init_acc(dw_ref)
        i = pl.program_id(0)
        has_prev = (i > 0).astype(F32)
        has_next = (i < pl.num_programs(0) - 1).astype(F32)
        u = ac[...] * ax[...]
        up = acp[...] * axp[...] * has_prev
        v = _conv_taps(u, up, w_ref, 3)
        sg = _silu(ag[...])
        dv = dy[...] * ab[...] * sg
        dvn = dyn[...] * abn[...] * _silu(agn[...]) * has_next
        du = _conv_taps_t(dv, dvn, w_ref, 3)
        o_ref[:, 0:BR] = (dy[...] * v * sg).astype(MXU_DTYPE)
        o_ref[:, BR:2 * BR] = (du * ax[...]).astype(MXU_DTYPE)
        o_ref[:, 2 * BR:3 * BR] = (du * ac[...]).astype(MXU_DTYPE)
        o_ref[:, 3 * BR:4 * BR] = (dy[...] * ab[...] * v * _dsilu(ag[...])).astype(MXU_DTYPE)
        _conv_wgrad(dw_ref, dv, u, up, 3)

    return pl.pallas_call(
        body, name="branch_a_bwd", out_shape=(SDS((s, 4 * BR), MXU_DTYPE), SDS((8, BR), F32)), grid=(s // tb,),
        in_specs=[_rows(tb, BR, 0), _next8(tb, BR, s, 0),
                  _rows(tb, BR, CB_AB), _next8(tb, BR, s, CB_AB), _rows(tb, BR, CB_AG), _next8(tb, BR, s, CB_AG),
                  _rows(tb, BR, CB_AC), _prev8(tb, BR, CB_AC), _rows(tb, BR, CB_AX), _prev8(tb, BR, CB_AX),
                  _const((8, BR))],
        out_specs=(_rows(tb, 4 * BR), _const((8, BR))), compiler_params=_params(1),
    )(dycat, dycat, proj, proj, proj, proj, proj, proj, proj, proj, conv_w)


def _t5_bucket(dist):
    max_exact = REL_BUCKETS // 2
    nf = jnp.maximum(dist, 1).astype(F32)
    large = max_exact + (jnp.log(nf / max_exact) / math.log(REL_MAX_DIST / max_exact)
                         * (REL_BUCKETS - max_exact)).astype(jnp.int32)
    large = jnp.minimum(large, REL_BUCKETS - 1)
    return jnp.where(dist < max_exact, dist, large)


def _bucket_maps():
    maps = []
    i = jnp.arange(BLK)[:, None]
    j = jnp.arange(2 * BLK)[None, :]
    delta = i + BLK - j
    for window, dil in DILATIONS:
        span = window // dil
        bucket = _t5_bucket(jnp.clip(delta, 0, span) * dil)
        maps.append(jnp.where((delta >= 0) & (delta <= span), bucket, -1))
    return jnp.stack(maps).astype(jnp.int32)


def _bias_tables(rel_bias, buckets):
    n_pat = len(DILATIONS)

    def body(rb_ref, bk_ref, o_ref):
        for g in range(n_pat):
            bk = bk_ref[g]
            for h in range(ATT_HEADS):
                def per_bucket(b, acc):
                    return jnp.where(bk == b, rb_ref[b, h], acc)
                o_ref[g, h] = lax.fori_loop(0, REL_BUCKETS, per_bucket, jnp.full((BLK, 2 * BLK), NEG, F32))

    return pl.pallas_call(
        body, name="bias_tables", out_shape=SDS((n_pat, ATT_HEADS, BLK, 2 * BLK), F32),
        in_specs=[pl.BlockSpec(memory_space=pltpu.SMEM), pl.BlockSpec(memory_space=pltpu.VMEM)],
        compiler_params=_params())(rel_bias, buckets)


def _head_masks():
    lane = lax.broadcasted_iota(jnp.int32, (1, 2 * HEAD_DIM), 1)
    return [(lane < HEAD_DIM).astype(F32), (lane >= HEAD_DIM).astype(F32)]


def _strided(base, size, dil):
    return pl.ds(base, size, stride=dil) if dil > 1 else pl.ds(pl.multiple_of(base, BLK), size)


def _attn_groups(s, dil):
    return max(1, min(1024, s) // (dil * BLK)) if dil == 1 else max(1, min(2048, s) // (dil * BLK))


def _attn_fwd(proj, bias, dil):
    s = proj.shape[0]
    grp = _attn_groups(s, dil)
    u1 = dil * BLK
    unit = grp * u1
    nb = s // unit
    w = 2 * HEAD_DIM
    q0, k0, v0 = (cb * (BR // w) for cb in (CB_Q, CB_K, CB_V))

    def body(q_ref, kc_ref, kp_ref, vc_ref, vp_ref, bias_ref, o_ref, lse_ref, kbuf, vbuf):
        n = pl.program_id(1)
        col = lax.broadcasted_iota(jnp.int32, (1, 2 * BLK), 1)
        masks = _head_masks()
        kbuf[0:u1, :] = kp_ref[...]
        kbuf[u1:, :] = kc_ref[...]
        vbuf[0:u1, :] = vp_ref[...]
        vbuf[u1:, :] = vc_ref[...]

        def per_r(t, carry):
            j = t // dil
            base = j * u1 + t % dil
            rows = _strided(base, BLK, dil)
            no_prev = jnp.where((n == 0) & (j == 0) & (col < BLK), NEG, 0.0)
            q = q_ref[rows, :] * (HEAD_DIM ** -0.5)
            k = kbuf[_strided(base, 2 * BLK, dil), :].astype(MXU_DTYPE)
            v = vbuf[_strided(base, 2 * BLK, dil), :].astype(MXU_DTYPE)
            o_acc = jnp.zeros((BLK, w), F32)
            lse_acc = jnp.zeros((BLK, w), F32)
            for h in range(2):
                qh = (q * masks[h]).astype(MXU_DTYPE)
                sc = lax.dot_general(qh, k, (((1,), (1,)), ((), ())), preferred_element_type=F32)
                sc = sc + bias_ref[h] + no_prev
                mx = jnp.max(sc, axis=-1, keepdims=True)
                p = jnp.exp(sc - mx)
                l = jnp.sum(p, axis=-1, keepdims=True)
                oh = jnp.dot((p / l).astype(MXU_DTYPE), v, preferred_element_type=F32)
                o_acc = o_acc + oh * masks[h]
                lse_acc = lse_acc + (mx + jnp.log(l)) * masks[h]
            o_ref[rows, :] = o_acc
            lse_ref[rows, :] = lse_acc
            return carry

        lax.fori_loop(0, grp * dil, per_r, 0, unroll=2)

    cur = lambda c0: pl.BlockSpec((unit, w), lambda hp, n: (n, c0 + hp))
    prev = lambda c0: pl.BlockSpec((u1, w), lambda hp, n: (jnp.maximum(n * grp - 1, 0), c0 + hp))
    out = pl.BlockSpec((unit, w), lambda hp, n: (n, hp))
    return pl.pallas_call(
        body, name=f"attn_fwd_d{dil}", out_shape=(SDS((s, BR), F32), SDS((s, BR), F32)), grid=(BR // w, nb),
        in_specs=[cur(q0), cur(k0), prev(k0), cur(v0), prev(v0),
                  pl.BlockSpec((2, BLK, 2 * BLK), lambda hp, n: (hp, 0, 0))],
        out_specs=(out, out),
        scratch_shapes=[pltpu.VMEM((unit + u1, w), F32), pltpu.VMEM((unit + u1, w), F32)],
        compiler_params=_params(2))(proj, proj, proj, proj, proj, bias)


def _softmax3(l0, l1, l2):
    mx = jnp.maximum(jnp.maximum(l0, l1), l2)
    e0, e1, e2 = jnp.exp(l0 - mx), jnp.exp(l1 - mx), jnp.exp(l2 - mx)
    inv = 1.0 / (e0 + e1 + e2)
    return e0 * inv, e1 * inv, e2 * inv


def _attn_combine(os_, lses, proj, tb):
    s = proj.shape[0]

    def body(o0, o1, o2, l0, l1, l2, bg, y_ref):
        w0, w1, w2 = _softmax3(l0[...], l1[...], l2[...])
        attn = w0 * o0[...] + w1 * o1[...] + w2 * o2[...]
        y_ref[...] = (attn * _silu(bg[...])).astype(MXU_DTYPE)

    return pl.pallas_call(
        body, name="attn_combine", out_shape=SDS((s, BR), MXU_DTYPE), grid=(s // tb,),
        in_specs=[_rows(tb, BR)] * 6 + [_rows(tb, BR, CB_BG)], out_specs=_rows(tb, BR),
        compiler_params=_params(1))(*os_, *lses, proj)


def _attn_bwd_pre(dycat, os_, lses, proj, head_ones, tb):
    s = proj.shape[0]

    def body(dy, o0, o1, o2, l0, l1, l2, bg, e_ref, dbg_ref, do0, do1, do2, dm0, dm1, dm2):
        w0, w1, w2 = _softmax3(l0[...], l1[...], l2[...])
        attn = w0 * o0[...] + w1 * o1[...] + w2 * o2[...]
        dattn = dy[...] * _silu(bg[...])
        dbg_ref[...] = dy[...] * attn * _dsilu(bg[...])
        prod = dattn * attn
        hi = prod.astype(MXU_DTYPE)
        lo = (prod - hi.astype(F32)).astype(MXU_DTYPE)
        tot = (jnp.dot(hi, e_ref[...], preferred_element_type=F32)
               + jnp.dot(lo, e_ref[...], preferred_element_type=F32))
        for wg, do_ref, dm_ref in ((w0, do0, dm0), (w1, do1, dm1), (w2, do2, dm2)):
            do_ref[...] = wg * dattn
            dm_ref[...] = wg * tot

    big = SDS((s, BR), F32)
    return pl.pallas_call(
        body, name="attn_bwd_pre", out_shape=(big,) * 7, grid=(s // tb,),
        in_specs=[_rows(tb, BR, 1)] + [_rows(tb, BR)] * 6 + [_rows(tb, BR, CB_BG), _const((BR, BR))],
        out_specs=(_rows(tb, BR),) * 7, compiler_params=_params(1))(dycat, *os_, *lses, proj, head_ones)


def _attn_bwd(proj, do, lse, dm, bias, dil, carry=None):
    s = proj.shape[0]
    grp = _attn_groups(s, dil)
    u1 = dil * BLK
    unit = grp * u1
    nb = s // unit
    w = 2 * HEAD_DIM
    q0, k0, v0 = (cb * (BR // w) for cb in (CB_Q, CB_K, CB_V))

    def body(q_ref, kc_ref, kp_ref, vc_ref, vp_ref, do_ref, lse_ref, dm_ref, bias_ref,
             dq_ref, dk_ref, dv_ref, dbias_ref, kbuf, vbuf, stage_k, stage_v):
        n = pl.program_id(1)
        col = lax.broadcasted_iota(jnp.int32, (1, 2 * BLK), 1)
        masks = _head_masks()

        @pl.when(n == 0)
        def _():
            dbias_ref[...] = jnp.zeros_like(dbias_ref)
            stage_k[...] = jnp.zeros_like(stage_k)
            stage_v[...] = jnp.zeros_like(stage_v)

        for out_ref, stage in ((dk_ref, stage_k), (dv_ref, stage_v)):
            if grp > 1:
                out_ref[0:unit - u1, :] = stage[u1:unit, :]
            stage[0:u1, :] = stage[unit:unit + u1, :]

        @pl.when(n < nb)
        def _():
            kbuf[0:u1, :] = kp_ref[...]
            kbuf[u1:, :] = kc_ref[...]
            vbuf[0:u1, :] = vp_ref[...]
            vbuf[u1:, :] = vc_ref[...]

            def per_r(t, carry):
                j = t // dil
                base = j * u1 + t % dil
                rows = _strided(base, BLK, dil)
                rows_hi = _strided(base + u1, BLK, dil)
                no_prev = jnp.where((n == 0) & (j == 0) & (col < BLK), NEG, 0.0)
                q = q_ref[rows, :] * (HEAD_DIM ** -0.5)
                k = kbuf[_strided(base, 2 * BLK, dil), :].astype(MXU_DTYPE)
                v = vbuf[_strided(base, 2 * BLK, dil), :].astype(MXU_DTYPE)
                do_t, lse_t, dm_t = do_ref[rows, :], lse_ref[rows, :], dm_ref[rows, :]
                dq_acc = jnp.zeros((BLK, w), F32)
                dk_acc = jnp.zeros((2 * BLK, w), F32)
                dv_acc = jnp.zeros((2 * BLK, w), F32)
                for h in range(2):
                    qh = (q * masks[h]).astype(MXU_DTYPE)
                    doh = (do_t * masks[h]).astype(MXU_DTYPE)
                    c0 = h * HEAD_DIM
                    sc = lax.dot_general(qh, k, (((1,), (1,)), ((), ())), preferred_element_type=F32)
                    p = jnp.exp(sc + bias_ref[h] + no_prev - lse_t[:, c0:c0 + 1])
                    dp = lax.dot_general(doh, v, (((1,), (1,)), ((), ())), preferred_element_type=F32)
                    ds = p * (dp - dm_t[:, c0:c0 + 1])
                    dbias_ref[h] += ds
                    dsb, pb = ds.astype(MXU_DTYPE), p.astype(MXU_DTYPE)
                    dq_acc = dq_acc + jnp.dot(dsb, k, preferred_element_type=F32) * masks[h]
                    dk_acc = dk_acc + lax.dot_general(dsb, qh, (((0,), (0,)), ((), ())), preferred_element_type=F32)
                    dv_acc = dv_acc + lax.dot_general(pb, doh, (((0,), (0,)), ((), ())), preferred_element_type=F32)
                dq_ref[rows, :] = dq_acc * (HEAD_DIM ** -0.5)
                stage_k[rows, :] = stage_k[rows, :] + dk_acc[0:BLK]
                stage_v[rows, :] = stage_v[rows, :] + dv_acc[0:BLK]
                stage_k[rows_hi, :] = dk_acc[BLK:2 * BLK]
                stage_v[rows_hi, :] = dv_acc[BLK:2 * BLK]
                return carry

            lax.fori_loop(0, grp * dil, per_r, 0, unroll=2)

        dk_ref[unit - u1:unit, :] = stage_k[0:u1, :]
        dv_ref[unit - u1:unit, :] = stage_v[0:u1, :]

    qn = lambda n: jnp.minimum(n, nb - 1)
    cur = lambda c0: pl.BlockSpec((unit, w), lambda hp, n: (qn(n), c0 + hp))
    prev = lambda c0: pl.BlockSpec((u1, w), lambda hp, n: (jnp.maximum(qn(n) * grp - 1, 0), c0 + hp))
    row = pl.BlockSpec((unit, w), lambda hp, n: (qn(n), hp))
    late = pl.BlockSpec((unit, w), lambda hp, n: (jnp.maximum(n - 1, 0), hp))
    tab = pl.BlockSpec((2, BLK, 2 * BLK), lambda hp, n: (hp, 0, 0))
    big = SDS((s, BR), F32)
    return _call(
        body, name=f"attn_bwd_d{dil}", out_shape=(big, big, big, SDS((ATT_HEADS, BLK, 2 * BLK), F32)),
        grid=(BR // w, nb + 1),
        in_specs=[cur(q0), cur(k0), prev(k0), cur(v0), prev(v0), row, row, row, tab],
        out_specs=(row, late, late, tab),
        scratch_shapes=[pltpu.VMEM((unit + u1, w), F32)] * 4,
        args=(proj, proj, proj, proj, proj, do, lse, dm, bias), carry=carry)


def _rel_bias_grad(dbias, buckets):
    def body(db_ref, bk_ref, o_ref):
        row = lax.broadcasted_iota(jnp.int32, (REL_BUCKETS, 128), 0)
        lane = lax.broadcasted_iota(jnp.int32, (REL_BUCKETS, 128), 1)

        def per_bucket(b, acc):
            for g in range(len(DILATIONS)):
                hit = bk_ref[g] == b
                for h in range(ATT_HEADS):
                    both = db_ref[0, g, h] + db_ref[1, g, h]
                    val = jnp.sum(jnp.where(hit, both, 0.0), keepdims=True)
                    acc = acc + jnp.where((row == b) & (lane == h), val, 0.0)
            return acc

        o_ref[...] = lax.fori_loop(0, REL_BUCKETS, per_bucket, jnp.zeros((REL_BUCKETS, 128), F32))

    assert dbias.shape[0] == DEPTH == 2
    return pl.pallas_call(body, name="rel_bias_grad", out_shape=SDS((REL_BUCKETS, 128), F32),
                          compiler_params=_params())(dbias, buckets)


def _scan_real(a, b, *, reverse, tb, name):
    s, ch = a.shape
    nt = s // tb
    order = range(7, -1, -1) if reverse else range(8)

    def body(a_ref, b_ref, o_ref, carry):
        @pl.when(pl.program_id(0) == 0)
        def _():
            carry[...] = jnp.zeros_like(carry)

        def group(gi, h):
            r0 = pl.multiple_of((tb // 8 - 1 - gi if reverse else gi) * 8, 8)
            a8, b8 = a_ref[pl.ds(r0, 8), :], b_ref[pl.ds(r0, 8), :]
            rows = [None] * 8
            for k in order:
                if reverse:
                    rows[k] = b8[k:k + 1] + h
                    h = a8[k:k + 1] * rows[k]
                else:
                    h = a8[k:k + 1] * h + b8[k:k + 1]
                    rows[k] = h
            o_ref[pl.ds(r0, 8), :] = jnp.concatenate(rows, axis=0)
            return h

        carry[...] = lax.fori_loop(0, tb // 8, group, carry[...])

    spec = pl.BlockSpec((tb, ch), (lambda i: (nt - 1 - i, 0)) if reverse else (lambda i: (i, 0)))
    return pl.pallas_call(body, name=name, out_shape=SDS((s, ch), F32), grid=(nt,), in_specs=[spec, spec],
                          out_specs=spec, scratch_shapes=[pltpu.VMEM((1, ch), F32)],
                          compiler_params=_params(1))(a, b)


def _scan_cplx(b, a_row, *, reverse, tb, name):
    s, ch2 = b.shape
    ch = ch2 // 2
    nt = s // tb
    order = range(7, -1, -1) if reverse else range(8)

    def body(a_ref, b_ref, o_ref, carry):
        @pl.when(pl.program_id(0) == 0)
        def _():
            carry[...] = jnp.zeros_like(carry)

        ar = a_ref[:, 0:ch]
        ai = -a_ref[:, ch:ch2] if reverse else a_ref[:, ch:ch2]

        def group(gi, x):
            xr, xi = x
            r0 = pl.multiple_of((tb // 8 - 1 - gi if reverse else gi) * 8, 8)
            br8, bi8 = b_ref[pl.ds(r0, 8), 0:ch], b_ref[pl.ds(r0, 8), ch:ch2]
            rr, ri = [None] * 8, [None] * 8
            for k in order:
                nr = ar * xr - ai * xi + br8[k:k + 1]
                ni = ar * xi + ai * xr + bi8[k:k + 1]
                xr, xi = nr, ni
                rr[k], ri[k] = xr, xi
            o_ref[pl.ds(r0, 8), 0:ch] = jnp.concatenate(rr, axis=0)
            o_ref[pl.ds(r0, 8), ch:ch2] = jnp.concatenate(ri, axis=0)
            return xr, xi

        xr, xi = lax.fori_loop(0, tb // 8, group, (carry[:, 0:ch], carry[:, ch:ch2]))
        carry[:, 0:ch] = xr
        carry[:, ch:ch2] = xi

    spec = pl.BlockSpec((tb, ch2), (lambda i: (nt - 1 - i, 0)) if reverse else (lambda i: (i, 0)))
    return pl.pallas_call(body, name=name, out_shape=SDS((s, ch2), F32), grid=(nt,),
                          in_specs=[_const((1, ch2)), spec], out_specs=spec,
                          scratch_shapes=[pltpu.VMEM((1, ch2), F32)], compiler_params=_params(1))(a_row, b)


def _neg_expm1(z):
    series = -z * (1.0 + z * (0.5 + z * (1.0 / 6 + z * (1.0 / 24 + z * (1.0 / 120)))))
    return jnp.where(z > -0.05, series, 1.0 - jnp.exp(z))


def _lru_gate(xc, pre_r, pre_i, lam):
    log_a = -LRU_C * jax.nn.sigmoid(pre_r) * jax.nn.softplus(-lam)
    return jnp.exp(log_a), jnp.sqrt(_neg_expm1(2.0 * log_a)) * jax.nn.sigmoid(pre_i) * xc


def _lru_gates_fwd(proj, conv_w, conv_b, w_cat, b_cat, lam, tb):
    s = proj.shape[0]

    def body(cx, cxp, w_ref, cb_ref, wc_ref, bc_ref, lam_ref, a_ref, b_ref):
        has_prev = (pl.program_id(0) > 0).astype(F32)
        xc = _conv_taps(cx[...], cxp[...] * has_prev, w_ref, 4) + cb_ref[...]
        pre = jnp.dot(xc.astype(MXU_DTYPE), wc_ref[...], preferred_element_type=F32) + bc_ref[...]
        a_ref[...], b_ref[...] = _lru_gate(xc, pre[:, 0:BR], pre[:, BR:2 * BR], lam_ref[...])

    big = SDS((s, BR), F32)
    return pl.pallas_call(
        body, name="lru_gates_fwd", out_shape=(big, big), grid=(s // tb,),
        in_specs=[_rows(tb, BR, CB_CX), _prev8(tb, BR, CB_CX), _const((8, BR)), _const((1, BR)),
                  _const((BR, 2 * BR)), _const((1, 2 * BR)), _const((1, BR))],
        out_specs=(_rows(tb, BR), _rows(tb, BR)), compiler_params=_params(1),
    )(proj, proj, conv_w, conv_b, w_cat, b_cat, lam)


def _gate_out(h, proj, cb, tb, name):
    s = proj.shape[0]

    def body(h_ref, g_ref, o_ref):
        o_ref[...] = (h_ref[...] * _silu(g_ref[...])).astype(MXU_DTYPE)

    return pl.pallas_call(body, name=name, out_shape=SDS((s, BR), MXU_DTYPE), grid=(s // tb,),
                          in_specs=[_rows(tb, BR), _rows(tb, BR, cb)], out_specs=_rows(tb, BR),
                          compiler_params=_params(1))(h, proj)


def _gate_out_bwd(dycat, dy_cb, h, proj, cb, tb, name):
    s = proj.shape[0]

    def body(dy, h_ref, g_ref, dh_ref, dg_ref):
        dh_ref[...] = dy[...] * _silu(g_ref[...])
        dg_ref[...] = dy[...] * h_ref[...] * _dsilu(g_ref[...])

    big = SDS((s, BR), F32)
    return pl.pallas_call(body, name=name, out_shape=(big, big), grid=(s // tb,),
                          in_specs=[_rows(tb, BR, dy_cb), _rows(tb, BR), _rows(tb, BR, cb)],
                          out_specs=(_rows(tb, BR), _rows(tb, BR)), compiler_params=_params(1))(dycat, h, proj)


def _lru_gates_bwd(proj, lmb, h, conv_w, conv_b, w_cat, b_cat, lam, tb):
    s = proj.shape[0]

    def body(cx, cxp, l_ref, h_ref, hp_ref, w_ref, cb_ref, wc_ref, bc_ref, lam_ref,
             dxc_ref, dpre_ref, xc_ref, dbc_ref, dlam_ref):
        _init_acc(dbc_ref, dlam_ref)
        has_prev = (pl.program_id(0) > 0).astype(F32)
        xc = _conv_taps(cx[...], cxp[...] * has_prev, w_ref, 4) + cb_ref[...]
        xcb = xc.astype(MXU_DTYPE)
        pre = jnp.dot(xcb, wc_ref[...], preferred_element_type=F32) + bc_ref[...]
        _, vjp = jax.vjp(_lru_gate, xc, pre[:, 0:BR], pre[:, BR:2 * BR], lam_ref[...])
        lm = l_ref[...]
        dxc, dpr, dpi, dlam = vjp((lm * _shift_down(h_ref[...], hp_ref[...] * has_prev, 1), lm))
        dpre = jnp.concatenate([dpr, dpi], axis=1)
        dpreb = dpre.astype(MXU_DTYPE)
        dxc_ref[...] = dxc + lax.dot_general(dpreb, wc_ref[...], (((1,), (1,)), ((), ())),
                                             preferred_element_type=F32)
        dpre_ref[...] = dpreb
        xc_ref[...] = xcb
        dbc_ref[...] += _colsum(dpre)
        dlam_ref[...] += dlam

    return pl.pallas_call(
        body, name="lru_gates_bwd",
        out_shape=(SDS((s, BR), F32), SDS((s, 2 * BR), MXU_DTYPE), SDS((s, BR), MXU_DTYPE),
                   SDS((1, 2 * BR), F32), SDS((1, BR), F32)),
        grid=(s // tb,),
        in_specs=[_rows(tb, BR, CB_CX), _prev8(tb, BR, CB_CX), _rows(tb, BR), _rows(tb, BR), _prev8(tb, BR),
                  _const((8, BR)), _const((1, BR)), _const((BR, 2 * BR)), _const((1, 2 * BR)), _const((1, BR))],
        out_specs=(_rows(tb, BR), _rows(tb, 2 * BR), _rows(tb, BR), _const((1, 2 * BR)), _const((1, BR))),
        compiler_params=_params(1))(proj, proj, lmb, h, h, conv_w, conv_b, w_cat, b_cat, lam)


def _conv_c_bwd(dxc, proj, conv_w, tb):
    s = proj.shape[0]

    def body(g, gn, cx, cxp, w_ref, dcx_ref, dw_ref, db_ref):
        _init_acc(dw_ref, db_ref)
        i = pl.program_id(0)
        has_prev = (i > 0).astype(F32)
        has_next = (i < pl.num_programs(0) - 1).astype(F32)
        gt = g[...]
        dcx_ref[...] = _conv_taps_t(gt, gn[...] * has_next, w_ref, 4)
        _conv_wgrad(dw_ref, gt, cx[...], cxp[...] * has_prev, 4)
        db_ref[...] += _colsum(gt)

    return pl.pallas_call(
        body, name="conv_c_bwd", out_shape=(SDS((s, BR), F32), SDS((8, BR), F32), SDS((1, BR), F32)),
        grid=(s // tb,),
        in_specs=[_rows(tb, BR), _next8(tb, BR, s), _rows(tb, BR, CB_CX), _prev8(tb, BR, CB_CX), _const((8, BR))],
        out_specs=(_rows(tb, BR), _const((8, BR)), _const((1, BR))), compiler_params=_params(1),
    )(dxc, dxc, proj, proj, conv_w)


def _s5_disc(lam_re, lam_im, log_dt):
    dt = jnp.exp(log_dt)
    mag = jnp.exp(lam_re * dt)
    ab_re = mag * jnp.cos(lam_im * dt)
    ab_im = mag * jnp.sin(lam_im * dt)
    den = lam_re * lam_re + lam_im * lam_im
    f_re = ((ab_re - 1.0) * lam_re + ab_im * lam_im) / den
    f_im = (ab_im * lam_re - (ab_re - 1.0) * lam_im) / den
    return ab_re, ab_im, f_re, f_im


def _s5_bbar(f_re, f_im, b_re, b_im):
    return f_re * b_re - f_im * b_im, f_re * b_im + f_im * b_re


def _s5_disc_fwd(lam_re, lam_im, log_dt):
    def body(lr, li, ld, o0, o1, o2, o3):
        o0[...], o1[...], o2[...], o3[...] = _s5_disc(lr[...], li[...], ld[...])
    return pl.pallas_call(body, name="s5_disc_fwd", out_shape=(SDS(lam_re.shape, F32),) * 4)(lam_re, lam_im, log_dt)


def _s5_disc_bwd(lam_re, lam_im, log_dt, cts):
    def body(lr, li, ld, c0, c1, c2, c3, o0, o1, o2):
        _, vjp = jax.vjp(_s5_disc, lr[...], li[...], ld[...])
        o0[...], o1[...], o2[...] = vjp((c0[...], c1[...], c2[...], c3[...]))
    return pl.pallas_call(body, name="s5_disc_bwd", out_shape=(SDS(lam_re.shape, F32), SDS(lam_re.shape, F32),
                                                                SDS(log_dt.shape, F32)))(lam_re, lam_im, log_dt, *cts)


def _s5_bbar_fwd(f_re, f_im, b_re, b_im):
    def body(fr, fi, br, bi, o0, o1):
        o0[...], o1[...] = _s5_bbar(fr[...], fi[...], br[...], bi[...])
    return pl.pallas_call(body, name="s5_bbar_fwd", out_shape=(SDS(b_re.shape, F32),) * 2)(f_re, f_im, b_re, b_im)


def _s5_bbar_bwd(f_re, f_im, b_re, b_im, d_re, d_im):
    def body(fr, fi, br, bi, dr, di, o0, o1, o2, o3):
        _, vjp = jax.vjp(_s5_bbar, fr[...], fi[...], br[...], bi[...])
        o0[...], o1[...], o2[...], o3[...] = vjp((dr[...], di[...]))
    col, mat = SDS(f_re.shape, F32), SDS(b_re.shape, F32)
    return pl.pallas_call(body, name="s5_bbar_bwd", out_shape=(col, col, mat, mat))(f_re, f_im, b_re, b_im, d_re, d_im)


def _s5_tail_fwd(ylin, proj, d_skip, w_glu, b_glu, tb):
    s = proj.shape[0]

    def body(yl, u, dg, dk, w_ref, b_ref, o_ref):
        g = jax.nn.gelu(yl[...] + dk[...] * u[...])
        t = jnp.dot(g.astype(MXU_DTYPE), w_ref[...], preferred_element_type=F32) + b_ref[...]
        o_ref[...] = (g * jax.nn.sigmoid(t) * _silu(dg[...])).astype(MXU_DTYPE)

    return pl.pallas_call(
        body, name="s5_tail_fwd", out_shape=SDS((s, BR), MXU_DTYPE), grid=(s // tb,),
        in_specs=[_rows(tb, BR), _rows(tb, BR, CB_DU), _rows(tb, BR, CB_DG), _const((1, BR)), _const((BR, BR)),
                  _const((1, BR))],
        out_specs=_rows(tb, BR), compiler_params=_params(1))(ylin, proj, proj, d_skip, w_glu, b_glu)


def _s5_tail_bwd(dycat, ylin, proj, d_skip, w_glu, b_glu, tb):
    s = proj.shape[0]

    def body(dy, yl, u, dg, dk, w_ref, b_ref, dyl_ref, dus_ref, ddg_ref, g_ref, dt_ref, ddk_ref, dbg_ref):
        _init_acc(ddk_ref, dbg_ref)
        g, gelu_vjp = jax.vjp(jax.nn.gelu, yl[...] + dk[...] * u[...])
        gb = g.astype(MXU_DTYPE)
        sg = jax.nn.sigmoid(jnp.dot(gb, w_ref[...], preferred_element_type=F32) + b_ref[...])
        dz = dy[...] * _silu(dg[...])
        ddg_ref[...] = dy[...] * g * sg * _dsilu(dg[...])
        dt = dz * g * sg * (1.0 - sg)
        dtb = dt.astype(MXU_DTYPE)
        dgel = dz * sg + lax.dot_general(dtb, w_ref[...], (((1,), (1,)), ((), ())), preferred_element_type=F32)
        dyv, = gelu_vjp(dgel)
        dyl_ref[...] = dyv
        dus_ref[...] = dyv * dk[...]
        g_ref[...] = gb
        dt_ref[...] = dtb
        ddk_ref[...] += _colsum(dyv * u[...])
        dbg_ref[...] += _colsum(dt)

    big, half, vec = SDS((s, BR), F32), SDS((s, BR), MXU_DTYPE), SDS((1, BR), F32)
    return pl.pallas_call(
        body, name="s5_tail_bwd", out_shape=(big, big, big, half, half, vec, vec), grid=(s // tb,),
        in_specs=[_rows(tb, BR, 3), _rows(tb, BR), _rows(tb, BR, CB_DU), _rows(tb, BR, CB_DG), _const((1, BR)),
                  _const((BR, BR)), _const((1, BR))],
        out_specs=(_rows(tb, BR),) * 5 + (_const((1, BR)), _const((1, BR))), compiler_params=_params(1),
    )(dycat, ylin, proj, proj, d_skip, w_glu, b_glu)


def _s5_da(lmb, x, tb):
    s, ch2 = x.shape
    ch = ch2 // 2

    def body(l_ref, x_ref, xp_ref, o_ref):
        _init_acc(o_ref)
        has_prev = (pl.program_id(0) > 0).astype(F32)
        xprev = _shift_down(x_ref[...], xp_ref[...] * has_prev, 1)
        lr, li, xr, xi = l_ref[:, 0:ch], l_ref[:, ch:ch2], xprev[:, 0:ch], xprev[:, ch:ch2]
        o_ref[:, 0:ch] += _colsum(lr * xr + li * xi)
        o_ref[:, ch:ch2] += _colsum(li * xr - lr * xi)

    return pl.pallas_call(body, name="s5_da", out_shape=SDS((1, ch2), F32), grid=(s // tb,),
                          in_specs=[_rows(tb, ch2), _rows(tb, ch2), _prev8(tb, ch2)], out_specs=_const((1, ch2)),
                          compiler_params=_params(1))(lmb, x, x)


def _assemble_dproj(da, dqkv, dbg, dcx, dcg, du, dus, ddg, tb):
    s = da.shape[0]

    def body(da_ref, q0, q1, q2, k0, k1, k2, v0, v1, v2, dbg_ref, dcx_ref, dcg_ref, du_ref, dus_ref, ddg_ref, o_ref):
        o_ref[:, 0:4 * BR] = da_ref[...]
        for j, parts in enumerate(((q0, q1, q2), (k0, k1, k2), (v0, v1, v2))):
            o_ref[:, (4 + j) * BR:(5 + j) * BR] = (parts[0][...] + parts[1][...] + parts[2][...]).astype(MXU_DTYPE)
        o_ref[:, 7 * BR:8 * BR] = dbg_ref[...].astype(MXU_DTYPE)
        o_ref[:, 8 * BR:9 * BR] = dcx_ref[...].astype(MXU_DTYPE)
        o_ref[:, 9 * BR:10 * BR] = dcg_ref[...].astype(MXU_DTYPE)
        o_ref[:, 10 * BR:11 * BR] = (du_ref[...] + dus_ref[...]).astype(MXU_DTYPE)
        o_ref[:, 11 * BR:12 * BR] = ddg_ref[...].astype(MXU_DTYPE)

    flat = [t for grp in dqkv for t in grp]
    return pl.pallas_call(
        body, name="assemble_dproj", out_shape=SDS((s, N_IN), MXU_DTYPE), grid=(s // tb,),
        in_specs=[_rows(tb, 4 * BR)] + [_rows(tb, BR)] * 15, out_specs=_rows(tb, N_IN),
        compiler_params=_params(1))(da, *flat, dbg, dcx, dcg, du, dus, ddg)


def _sum_leading(xs, tr, name):
    n, r, c = xs[0].shape
    nl = len(xs)
    tr = min(tr, r)
    nr = r // tr
    assert r % tr == 0, (name, r, tr)

    def body(*refs):
        i = pl.program_id(0)
        for l in range(nl):
            @pl.when((i >= l * nr) & (i < (l + 1) * nr))
            def _():
                acc = refs[l * n][...].astype(F32)
                for ref in refs[l * n + 1:(l + 1) * n]:
                    acc = acc + ref[...].astype(F32)
                refs[nl * n][...] = acc

    specs = [pl.BlockSpec((None, tr, c), functools.partial(lambda i, k, l: (k, jnp.clip(i - l * nr, 0, nr - 1), 0), k=k, l=l))
             for l in range(nl) for k in range(n)]
    return pl.pallas_call(body, name=name, out_shape=SDS((nl * r, c), F32), grid=(nl * nr,), in_specs=specs,
                          out_specs=pl.BlockSpec((tr, c), lambda i: (i, 0)),
                          compiler_params=_params(1))(*[x for x in xs for _ in range(n)])


def _adamw(w, g_parts, m, v, tr, name):
    r, c = w.shape
    tr = min(tr, r)
    n = len(g_parts)
    assert r % tr == 0, (name, r, tr)

    def body(*refs):
        w_ref, m_ref, v_ref = refs[0], refs[1 + n], refs[2 + n]
        g_ref, d_ref, nm_ref, nv_ref = refs[3 + n:]
        g = refs[1][...]
        for ref in refs[2:1 + n]:
            g = g + ref[...]
        mm = ADAM_B1 * m_ref[...] + (1.0 - ADAM_B1) * g
        vv = ADAM_B2 * v_ref[...] + (1.0 - ADAM_B2) * jnp.square(g)
        m_hat = mm / (1.0 - ADAM_B1 ** ADAM_STEP)
        v_hat = vv / (1.0 - ADAM_B2 ** ADAM_STEP)
        g_ref[...] = g
        d_ref[...] = -ADAM_LR * (m_hat / (jnp.sqrt(v_hat) + ADAM_EPS) + ADAM_WD * w_ref[...])
        nm_ref[...] = mm
        nv_ref[...] = vv

    spec = pl.BlockSpec((tr, c), lambda i: (i, 0))
    return pl.pallas_call(body, name=name, out_shape=(SDS((r, c), F32),) * 4, grid=(r // tr,),
                          in_specs=[spec] * (3 + n), out_specs=(spec,) * 4,
                          compiler_params=_params(1))(w, *g_parts, m, v)


def _allgather8(block, name):
    m_per, n = block.shape

    def body(x_ref, out_ref, send_sems, recv_sems, local_sem):
        x, y, c = lax.axis_index("x"), lax.axis_index("y"), lax.axis_index("c")
        me, sibling = (x, y, c), (x, y, 1 - c)
        chips = [(1 - x, y), (x, 1 - y), (1 - x, 1 - y)]

        def rows(px, py, pc):
            return out_ref.at[pl.ds((4 * px + 2 * py + pc) * m_per, m_per), :]

        def copy(k, blk, to, src=None):
            return pltpu.make_async_remote_copy(
                src_ref=rows(*blk) if src is None else src, dst_ref=rows(*blk), send_sem=send_sems.at[k],
                recv_sem=recv_sems.at[k], device_id=to, device_id_type=MESH)

        mine = pltpu.make_async_copy(x_ref, rows(*me), local_sem)
        mine.start()
        first = [copy(0, me, sibling, src=x_ref)]
        first += [copy(1 + j, me, (*chip, c), src=x_ref) for j, chip in enumerate(chips)]
        for cp in first:
            cp.start()
        passed = [copy(4 + j, (*chip, c), sibling) for j, chip in enumerate(chips)]
        for j, chip in enumerate(chips):
            copy(1 + j, (*chip, c), me).wait_recv()
            passed[j].start()
        copy(0, sibling, me).wait_recv()
        for j, chip in enumerate(chips):
            copy(4 + j, (*chip, 1 - c), me).wait_recv()
        for cp in first + passed:
            cp.wait_send()
        mine.wait()

    return pl.pallas_call(
        body, name=name, out_shape=SDS((N_DEV * m_per, n), block.dtype),
        in_specs=[pl.BlockSpec(memory_space=pltpu.VMEM)], out_specs=pl.BlockSpec(memory_space=pltpu.VMEM),
        scratch_shapes=[pltpu.SemaphoreType.DMA((7,)), pltpu.SemaphoreType.DMA((7,)), pltpu.SemaphoreType.DMA],
        compiler_params=_params())(block)


class _Exchange:
    def __init__(self, items, out_shapes):
        self.items, self.out_shapes = list(items), tuple(out_shapes)
        self.arrays = [it[0] for it in self.items]
        n = len(self.items)
        self.n_in, self.n_out = n, len(self.out_shapes)
        self.scratch = [pltpu.SemaphoreType.DMA((n * N_CHIPS,)), pltpu.SemaphoreType.DMA((n * N_CHIPS,)),
                        pltpu.SemaphoreType.DMA((n,))]

    def _copies(self, ins, outs, sems, m):
        send_sems, recv_sems, local_sems = sems
        c = lax.axis_index("c")
        others = [j for j in range(N_CHIPS) if j != m]

        def remote(a, src, dst, to, from_):
            return pltpu.make_async_remote_copy(
                src_ref=src, dst_ref=dst, send_sem=send_sems.at[a * N_CHIPS + to],
                recv_sem=recv_sems.at[a * N_CHIPS + from_], device_id=(to // 2, to % 2, c), device_id_type=MESH)

        local, sends, recvs = [], [], []
        for a, (_, oi, src_of, dst_of) in enumerate(self.items):
            local.append(pltpu.make_async_copy(src_of(ins[a], m), dst_of(outs[oi], m), local_sems.at[a]))
            for j in others:
                sends.append(remote(a, src_of(ins[a], j), dst_of(outs[oi], m), j, m))
                recvs.append(remote(a, src_of(ins[a], m), dst_of(outs[oi], j), j, j))
        return local, sends, recvs

    def _on_my_chip(self, fn):
        chip = 2 * lax.axis_index("x") + lax.axis_index("y")
        for m in range(N_CHIPS):
            pl.when(chip == m)(functools.partial(fn, m))

    def start(self, ins, outs, sems):
        def go(m):
            local, sends, _ = self._copies(ins, outs, sems, m)
            for cp in local + sends:
                cp.start()
        self._on_my_chip(go)

    def wait(self, ins, outs, sems):
        def go(m):
            local, sends, recvs = self._copies(ins, outs, sems, m)
            for cp in recvs:
                cp.wait_recv()
            for cp in sends:
                cp.wait_send()
            for cp in local:
                cp.wait()
        self._on_my_chip(go)


def _half_rows(ref, cc):
    h = ref.shape[-2] // 2
    return ref.at[(slice(None),) * (len(ref.shape) - 2) + (pl.ds(cc * h, h), slice(None))]


class _Gather:
    def __init__(self, items, out_shapes):
        self.items, self.out_shapes = list(items), tuple(out_shapes)
        self.arrays = [it[0] for it in self.items]
        n = len(self.items)
        self.n_in, self.n_out = n, len(self.out_shapes)
        self.scratch = [pltpu.SemaphoreType.DMA((n * N_CHIPS,)) for _ in range(4)] + [pltpu.SemaphoreType.DMA((n,))]

    def _copies(self, ins, outs, sems, m, cc):
        ici_send, ici_recv, d2d_send, d2d_recv, local_sems = sems
        others = [j for j in range(N_CHIPS) if j != m]
        local, sends, arrivals, passed_on, from_sibling = [], [], [], [], []
        for a, (_, oi, src_of, dst_of) in enumerate(self.items):
            src, out = src_of(ins[a]), outs[oi]
            local.append(pltpu.make_async_copy(src, dst_of(out, m), local_sems.at[a]))
            for j in others:
                k = a * N_CHIPS + j
                mine_there = _half_rows(dst_of(out, m), cc)
                theirs_here = _half_rows(dst_of(out, j), cc)
                sends.append(pltpu.make_async_remote_copy(
                    src_ref=_half_rows(src, cc), dst_ref=mine_there, send_sem=ici_send.at[k],
                    recv_sem=ici_recv.at[a * N_CHIPS + m], device_id=(j // 2, j % 2, cc), device_id_type=MESH))
                arrivals.append(pltpu.make_async_remote_copy(
                    src_ref=_half_rows(src, cc), dst_ref=theirs_here, send_sem=ici_send.at[k], recv_sem=ici_recv.at[k],
                    device_id=(j // 2, j % 2, cc), device_id_type=MESH))
                passed_on.append(pltpu.make_async_remote_copy(
                    src_ref=theirs_here, dst_ref=theirs_here, send_sem=d2d_send.at[k], recv_sem=d2d_recv.at[k],
                    device_id=(m // 2, m % 2, 1 - cc), device_id_type=MESH))
                other_half = _half_rows(dst_of(out, j), 1 - cc)
                from_sibling.append(pltpu.make_async_remote_copy(
                    src_ref=other_half, dst_ref=other_half, send_sem=d2d_send.at[k], recv_sem=d2d_recv.at[k],
                    device_id=(m // 2, m % 2, 1 - cc), device_id_type=MESH))
        return local, sends, arrivals, passed_on, from_sibling

    def _on_my_core(self, fn):
        chip = 2 * lax.axis_index("x") + lax.axis_index("y")
        c = lax.axis_index("c")
        for m in range(N_CHIPS):
            for cc in range(2):
                pl.when((chip == m) & (c == cc))(functools.partial(fn, m, cc))

    def start(self, ins, outs, sems):
        def go(m, cc):
            local, sends, _, _, _ = self._copies(ins, outs, sems, m, cc)
            for cp in local + sends:
                cp.start()
        self._on_my_core(go)

    def wait(self, ins, outs, sems):
        def go(m, cc):
            local, sends, arrivals, passed_on, from_sibling = self._copies(ins, outs, sems, m, cc)
            for arrived, onward in zip(arrivals, passed_on):
                arrived.wait_recv()
                onward.start()
            for cp in from_sibling:
                cp.wait_recv()
            for cp in sends + passed_on:
                cp.wait_send()
            for cp in local:
                cp.wait()
        self._on_my_core(go)


def _run_exchange(ex, name):
    def body(*refs):
        ins, outs, sems = refs[:ex.n_in], refs[ex.n_in:ex.n_in + ex.n_out], refs[ex.n_in + ex.n_out:]
        ex.start(ins, outs, sems)
        ex.wait(ins, outs, sems)

    return pl.pallas_call(
        body, name=name, out_shape=ex.out_shapes, in_specs=[ANY] * ex.n_in, out_specs=(ANY,) * ex.n_out,
        scratch_shapes=ex.scratch, compiler_params=_params())(*ex.arrays)


def _sibling_swap(arrays, name):
    n = len(arrays)

    def body(*refs):
        ins, outs = refs[:n], refs[n:2 * n]
        send_sems, recv_sems = refs[2 * n:]
        peer = (lax.axis_index("x"), lax.axis_index("y"), 1 - lax.axis_index("c"))
        cps = [pltpu.make_async_remote_copy(src_ref=ins[a], dst_ref=outs[a], send_sem=send_sems.at[a],
                                            recv_sem=recv_sems.at[a], device_id=peer, device_id_type=MESH)
               for a in range(n)]
        for cp in cps:
            cp.start()
        for cp in cps:
            cp.wait()

    return pl.pallas_call(
        body, name=name, out_shape=tuple(SDS(a.shape, a.dtype) for a in arrays), in_specs=[ANY] * n,
        out_specs=(ANY,) * n, scratch_shapes=[pltpu.SemaphoreType.DMA((n,)), pltpu.SemaphoreType.DMA((n,))],
        compiler_params=_params())(*arrays)


def _block_diag(w):
    h, n, m = w.shape
    eye = jnp.eye(h, dtype=w.dtype)
    return (w[:, :, None, :] * eye[:, None, :, None]).reshape(h * n, h * m)


def _diag_blocks(d, h, col0=0, ncols=None):
    ncols = d.shape[1] - col0 if ncols is None else ncols
    n, m = d.shape[0] // h, ncols // h
    lanes = 128
    assert m <= lanes and lanes % m == 0 and col0 % lanes == 0

    def body(d_ref, o_ref):
        for g in range(h):
            c = col0 + g * m
            chunk = d_ref[g * n:(g + 1) * n, c // lanes * lanes:c // lanes * lanes + lanes]
            o_ref[g * n:(g + 1) * n, :] = chunk[:, c % lanes:c % lanes + m]

    out = pl.pallas_call(body, name="diag_blocks", out_shape=SDS((h * n, m), d.dtype), compiler_params=_params())(d)
    return out.reshape(h, n, m)


def _tiles(s):
    return dict(tb=min(512, s), tln=min(256, s), tscan=min(256, s))


def _layer_weights(p, l):
    pad8 = lambda w: jnp.pad(w, ((0, 8 - w.shape[0]), (0, 0)))
    return dict(
        conv_a=pad8(p["conv_a"][l]), conv_c=pad8(p["conv_c"][l]), conv_c_b=p["conv_c_b"][l][None],
        w_cat=jnp.concatenate([_block_diag(p["lru_wa"][l]), _block_diag(p["lru_wx"][l])], axis=1).astype(MXU_DTYPE),
        b_cat=jnp.concatenate([p["lru_ba"][l], p["lru_bx"][l]])[None], lam=p["lru_lambda"][l][None],
        lam_re=p["s5_lam_re"][l], lam_im=p["s5_lam_im"][l], log_dt=p["s5_log_dt"][l][:, None],
        b_re=p["s5_b_re"][l].reshape(S5_N, S5_CH), b_im=p["s5_b_im"][l].reshape(S5_N, S5_CH),
        c_re=p["s5_c_re"][l], c_im=p["s5_c_im"][l], d_skip=p["s5_d"][l][None], b_glu=p["s5_b_glu"][l][None],
        ln_g=p["ln_g"][l][None], ln_b=p["ln_b"][l][None])


def _s5_matrices(lw):
    ab_re, ab_im, f_re, f_im = _s5_disc_fwd(lw["lam_re"], lw["lam_im"], lw["log_dt"])
    f_re, f_im = f_re.reshape(S5_N, 1), f_im.reshape(S5_N, 1)
    bb_re, bb_im = _s5_bbar_fwd(f_re, f_im, lw["b_re"], lw["b_im"])
    to_bd = lambda bb: _block_diag(jnp.swapaxes(bb.reshape(S5_GROUPS, S5_STATE, S5_CH), 1, 2))
    bmat = jnp.concatenate([to_bd(bb_re), to_bd(bb_im)], axis=1).astype(MXU_DTYPE)
    cmat_t = jnp.concatenate([_block_diag(lw["c_re"]), -_block_diag(lw["c_im"])], axis=1).astype(MXU_DTYPE)
    a_row = jnp.concatenate([ab_re.reshape(1, S5_N), ab_im.reshape(1, S5_N)], axis=1)
    return dict(f_re=f_re, f_im=f_im, bmat=bmat, bmat_t=bmat.T, cmat_t=cmat_t, cmat=cmat_t.T, a_row=a_row)


def _mm_hooked(hook, *args, **kw):
    if hook is None:
        return _mm(*args, **kw)
    out = _mm(*args, carry=hook[0], **kw)
    hook[1](out[1:])
    return out[0]


def _layer_fwd(x, ada, w_in, get_rest, lw, s5m, bias_tabs, hooks=None):
    s = x.shape[0]
    t = _tiles(s)
    tb = t["tb"]
    shift, scale, gate = ada
    hooks = hooks or {}
    h = _modulate(x, scale, shift, tb)
    proj = _mm_hooked(hooks.get("in_proj"), h, w_in, name="in_proj", tm=1024, tn=1024, tk=D_MODEL)
    w_out, w_glu = get_rest()
    y_a = _branch_a_fwd(proj, lw["conv_a"], tb)
    os_, lses = [], []
    for g, (_, dil) in enumerate(DILATIONS):
        o, lse = _attn_fwd(proj, bias_tabs[g], dil)
        os_.append(o)
        lses.append(lse)
    y_b = _attn_combine(os_, lses, proj, tb)
    lru_a, lru_b = _lru_gates_fwd(proj, lw["conv_c"], lw["conv_c_b"], lw["w_cat"], lw["b_cat"], lw["lam"], tb)
    lru_h = _scan_real(lru_a, lru_b, reverse=False, tb=tb, name="lru_scan")
    y_c = _gate_out(lru_h, proj, CB_CG, tb, "lru_out")
    bu = _mm(proj, s5m["bmat"], name="s5_bu", a_col0=CB_DU * BR, a_ncols=BR, tn=1024)
    s5_x = _scan_cplx(bu, s5m["a_row"], reverse=False, tb=t["tscan"], name="s5_scan")
    ylin = _mm(s5_x, s5m["cmat"], name="s5_cx", tk=1024)
    y_d = _s5_tail_fwd(ylin, proj, lw["d_skip"], w_glu, lw["b_glu"], tb)
    ycat = jnp.concatenate([y_a, y_b, y_c, y_d], axis=1)
    x_next, xhat, y, rstd = _out_ln(ycat, w_out, x, gate, lw["ln_g"], lw["ln_b"], t["tln"])
    saved = dict(x=x, h=h, proj=proj, os=os_, lses=lses, lru_a=lru_a, lru_h=lru_h, s5_x=s5_x, ylin=ylin, ycat=ycat,
                 xhat=xhat, y=y, rstd=rstd)
    return x_next, saved


def _layer_bwd(dxn, sv, ada, w_in, w_out, w_glu, lw, s5m, bias_tabs, head_ones, hooks=None):
    s = dxn.shape[0]
    t = _tiles(s)
    tb = t["tb"]
    shift, scale, gate = ada
    proj = sv["proj"]
    g = {}
    hook = lambda name: hooks[name](g) if hooks and name in hooks else None
    dyb, dxa, g["ln_g"], g["ln_b"], dgate = _ln_bwd(dxn, sv["xhat"], sv["y"], sv["rstd"], lw["ln_g"], gate, t["tln"])
    g["w_out"] = _mm_hooked(hook("dw_out"), sv["ycat"], dyb, name="dw_out", ta=True, out_dtype=WIRE_DTYPE,
                            tm=1024, tn=1024, tk=1024)
    dycat = _mm(dyb, w_out, name="dycat", tb=True, tm=1024, tn=1024, tk=D_MODEL)
    da, dconv_a = _branch_a_bwd(dycat, proj, lw["conv_a"], tb)
    g["conv_a"] = dconv_a[0:3]
    pre = _attn_bwd_pre(dycat, sv["os"], sv["lses"], proj, head_ones, tb)
    dbg, dos, dms = pre[0], pre[1:4], pre[4:7]
    dqkv, dbias = [], []
    for gi, (_, dil) in enumerate(DILATIONS):
        hk = hook(f"attn_bwd_d{dil}")
        dq, dk, dv, dbi, *got = _attn_bwd(proj, dos[gi], sv["lses"][gi], dms[gi], bias_tabs[gi], dil,
                                          carry=hk and hk[0])
        if hk:
            hk[1](got)
        dqkv.append((dq, dk, dv))
        dbias.append(dbi)
    dqkv = list(zip(*dqkv))
    dh, dcg = _gate_out_bwd(dycat, 2, sv["lru_h"], proj, CB_CG, tb, "lru_out_bwd")
    lmb = _scan_real(sv["lru_a"], dh, reverse=True, tb=tb, name="lru_scan_bwd")
    dxc, dpre, xcb, dbcat, dlam = _lru_gates_bwd(proj, lmb, sv["lru_h"], lw["conv_c"], lw["conv_c_b"], lw["w_cat"],
                                                  lw["b_cat"], lw["lam"], tb)
    dwcat = _mm(xcb, dpre, name="dw_lru", ta=True, tn=1024)
    g["lru_wa"] = _diag_blocks(dwcat, LRU_HEADS, 0, BR)
    g["lru_wx"] = _diag_blocks(dwcat, LRU_HEADS, BR, BR)
    g["lru_ba"], g["lru_bx"], g["lru_lambda"] = dbcat[0, 0:BR], dbcat[0, BR:2 * BR], dlam[0]
    dcx, dconv_c, dccb = _conv_c_bwd(dxc, proj, lw["conv_c"], tb)
    g["conv_c"], g["conv_c_b"] = dconv_c[0:4], dccb[0]
    dyl, dus, ddg, gb, dtb, ddk, dbglu = _s5_tail_bwd(dycat, sv["ylin"], proj, lw["d_skip"], w_glu, lw["b_glu"], tb)
    g["s5_d"], g["s5_b_glu"] = ddk[0], dbglu[0]
    g["s5_w_glu"] = _mm(gb, dtb, name="dw_glu", ta=True, out_dtype=WIRE_DTYPE)
    dxd = _mm(dyl, s5m["cmat_t"], name="s5_dx", tk=BR, tn=1024)
    s5_l = _scan_cplx(dxd, s5m["a_row"], reverse=True, tb=t["tscan"], name="s5_scan_bwd")
    dab = _s5_da(s5_l, sv["s5_x"], t["tscan"])
    dbmat = _mm(proj, s5_l, name="dw_s5_b", ta=True, a_col0=CB_DU * BR, a_ncols=BR, tn=1024)
    dcmat_t = _mm(dyl, sv["s5_x"], name="dw_s5_c", ta=True, tn=1024)
    du = _mm(s5_l, s5m["bmat_t"], name="s5_du", tk=1024)
    from_bd = lambda col0: jnp.swapaxes(_diag_blocks(dbmat, S5_GROUPS, col0, S5_N), 1, 2).reshape(S5_N, S5_CH)
    df_re, df_im, db_re, db_im = _s5_bbar_bwd(s5m["f_re"], s5m["f_im"], lw["b_re"], lw["b_im"],
                                              from_bd(0), from_bd(S5_N))
    shp = (S5_GROUPS, S5_STATE)
    g["s5_lam_re"], g["s5_lam_im"], dlog_dt = _s5_disc_bwd(
        lw["lam_re"], lw["lam_im"], lw["log_dt"],
        (dab[:, 0:S5_N].reshape(shp), dab[:, S5_N:].reshape(shp), df_re.reshape(shp), df_im.reshape(shp)))
    g["s5_log_dt"] = dlog_dt[:, 0]
    g["s5_b_re"] = db_re.reshape(S5_GROUPS, S5_STATE, S5_CH)
    g["s5_b_im"] = db_im.reshape(S5_GROUPS, S5_STATE, S5_CH)
    g["s5_c_re"] = _diag_blocks(dcmat_t, S5_GROUPS, 0, S5_N)
    g["s5_c_im"] = -_diag_blocks(dcmat_t, S5_GROUPS, S5_N, S5_N)
    dproj = _assemble_dproj(da, dqkv, dbg, dcx, dcg, du, dus, ddg, tb)
    g["w_in"] = _mm_hooked(hook("dw_in"), sv["h"], dproj, name="dw_in", ta=True, out_dtype=WIRE_DTYPE,
                           tm=1024, tn=1536, tk=1024)
    dhm = _mm_hooked(hook("dh"), dproj, w_in, name="dh", tb=True, tm=1024, tn=1024, tk=1536)
    dx, dshift, dscale = _mod_bwd(dhm, dxa, sv["x"], scale, tb)
    g["ada"] = jnp.concatenate([dshift[0], dscale[0], dgate[0]])
    return dx, g, dbias


SMALL = ("rel_bias", "conv_a", "conv_c", "conv_c_b", "lru_wa", "lru_ba", "lru_wx", "lru_bx", "lru_lambda",
         "s5_lam_re", "s5_lam_im", "s5_log_dt", "s5_b_re", "s5_b_im", "s5_c_re", "s5_c_im", "s5_d", "s5_b_glu",
         "ln_g", "ln_b")
PER_LAYER_SMALL = SMALL[1:]


def _local_step(x, target, ada_rows, w_in, w_out, w_glu, p, comm=None):
    if comm is None:
        get_w_in = lambda l: w_in[l]
        get_rest = lambda l: (w_out[l], w_glu[l])
        fwd_hooks = bwd_hooks = lambda *_: None
    else:
        get_w_in, get_rest, fwd_hooks, bwd_hooks = comm.w_in, comm.rest, comm.fwd_hooks, comm.bwd_hooks
    s = x.shape[0]
    buckets = _bucket_maps()
    bias_tabs = _bias_tables(p["rel_bias"], buckets)
    head_ones = _block_diag(jnp.ones((ATT_HEADS, HEAD_DIM, HEAD_DIM), MXU_DTYPE))
    lws = [_layer_weights(p, l) for l in range(DEPTH)]
    s5ms = [_s5_matrices(lw) for lw in lws]
    adas = [tuple(ada_rows[l, k * D_MODEL:(k + 1) * D_MODEL][None] for k in range(3)) for l in range(DEPTH)]
    saved = []
    for l in range(DEPTH):
        x, sv = _layer_fwd(x, adas[l], get_w_in(l), functools.partial(get_rest, l), lws[l], s5ms[l], bias_tabs,
                           fwd_hooks(l))
        saved.append(sv)
    loss, dx = _loss_head(x, target, _tiles(s)["tb"])
    grads = [None] * DEPTH
    dbias_sum = []
    for l in reversed(range(DEPTH)):
        dx, grads[l], dbias = _layer_bwd(dx, saved[l], adas[l], get_w_in(l), *get_rest(l), lws[l], s5ms[l],
                                         bias_tabs, head_ones, bwd_hooks(l, grads))
        dbias_sum.append(jnp.stack(dbias))
    drel = _rel_bias_grad(jnp.stack(dbias_sum), buckets)[:, 0:ATT_HEADS]
    small = {n: jnp.stack([grads[l][n] for l in range(DEPTH)]) for n in PER_LAYER_SMALL + ("ada",)}
    small["rel_bias"] = drel
    big = {n: [grads[l][n] for l in range(DEPTH)] for n in ("w_in", "w_out", "s5_w_glu")}
    return loss, dx, big, small


PACK_ROWS = 256


def _pack(parts):
    flat = jnp.concatenate([t.reshape(-1).astype(F32) for t in parts])
    n = flat.shape[0]
    rows = -(-n // (PACK_ROWS * 128)) * PACK_ROWS
    return jnp.pad(flat, (0, rows * 128 - n)).reshape(rows, 128)


def _unpack(packed, shapes):
    flat = packed.reshape(packed.shape[:-2] + (-1,))
    out, off = [], 0
    for shp in shapes:
        size = math.prod(shp)
        out.append(flat[..., off:off + size].reshape(flat.shape[:-1] + tuple(shp)))
        off += size
    return out


def _take_cols(t, chip, width):
    return lax.dynamic_slice_in_dim(t, chip * width, width, axis=t.ndim - 1)


class _Comm:
    IN_W, OUT_R, GLU_R = N_IN // N_CHIPS, D_MODEL // N_CHIPS, BR // N_CHIPS

    def __init__(self, w_in_b, w_out_b, w_glu_b):
        assert DEPTH == 2
        self.shards = (w_in_b, w_out_b, w_glu_b)
        in_w = self.IN_W
        self.w_in_full = {0: _run_exchange(_Gather(
            [(w_in_b, 0, lambda ref: ref.at[0], lambda ref, j: ref.at[:, pl.ds(j * in_w, in_w)])],
            [SDS((D_MODEL, N_IN), WIRE_DTYPE)]), "gather_w_in0")[0]}
        self.w_out_full = self.w_glu_full = None
        self.recv = {}

    def w_in(self, l):
        return self.w_in_full[l]

    def rest(self, l):
        return self.w_out_full[l], self.w_glu_full[l]

    def fwd_hooks(self, l):
        if l != 0:
            return None
        w_in_b, w_out_b, w_glu_b = self.shards
        in_w, out_r, glu_r = self.IN_W, self.OUT_R, self.GLU_R
        whole = lambda ref: ref
        items = [(w_out_b, 0, whole, lambda ref, j: ref.at[:, pl.ds(j * out_r, out_r), :]),
                 (w_glu_b, 1, whole, lambda ref, j: ref.at[:, pl.ds(j * glu_r, glu_r), :]),
                 (w_in_b, 2, lambda ref: ref.at[1], lambda ref, j: ref.at[:, pl.ds(j * in_w, in_w)])]
        shapes = [SDS((DEPTH, D_MODEL, D_MODEL), WIRE_DTYPE), SDS((DEPTH, BR, BR), WIRE_DTYPE),
                  SDS((D_MODEL, N_IN), WIRE_DTYPE)]

        def done(got):
            self.w_out_full, self.w_glu_full, self.w_in_full[1] = got

        return {"in_proj": (_Gather(items, shapes), done)}

    def _scatter(self, parts):
        in_w, out_r, glu_r = self.IN_W, self.OUT_R, self.GLU_R
        cut = {"w_in": (lambda ref, j: ref.at[:, pl.ds(j * in_w, in_w)], (D_MODEL, in_w)),
               "w_out": (lambda ref, j: ref.at[pl.ds(j * out_r, out_r), :], (out_r, D_MODEL)),
               "s5_w_glu": (lambda ref, j: ref.at[pl.ds(j * glu_r, glu_r), :], (glu_r, BR))}
        items = [(arr, oi, cut[name][0], lambda ref, j: ref.at[j]) for oi, (name, _, arr) in enumerate(parts)]
        shapes = [SDS((N_CHIPS,) + cut[name][1], WIRE_DTYPE) for name, _, _ in parts]

        def done(got):
            for (name, l, _), arr in zip(parts, got):
                self.recv[name, l] = arr

        return _Exchange(items, shapes), done

    def bwd_hooks(self, l, grads):
        if l != 0:
            return None
        g1 = grads[1]
        return {"dw_out": lambda g: self._scatter([("w_out", 1, g1["w_out"]), ("s5_w_glu", 1, g1["s5_w_glu"])]),
                "attn_bwd_d16": lambda g: self._scatter([("w_in", 1, g1["w_in"])]),
                "dw_in": lambda g: self._scatter([("w_out", 0, g["w_out"]), ("s5_w_glu", 0, g["s5_w_glu"])]),
                "dh": lambda g: self._scatter([("w_in", 0, g["w_in"])])}


def kernel(x, c, rel_bias, w_ada, b_ada, w_in, conv_a, conv_c, conv_c_b, lru_wa, lru_ba, lru_wx, lru_bx, lru_lambda, s5_lam_re, s5_lam_im, s5_log_dt, s5_b_re, s5_b_im, s5_c_re, s5_c_im, s5_d, s5_w_glu, s5_b_glu, w_out, ln_g, ln_b, loss_target, m_rel_bias, m_w_ada, m_b_ada, m_w_in, m_conv_a, m_conv_c, m_conv_c_b, m_lru_wa, m_lru_ba, m_lru_wx, m_lru_bx, m_lru_lambda, m_s5_lam_re, m_s5_lam_im, m_s5_log_dt, m_s5_b_re, m_s5_b_im, m_s5_c_re, m_s5_c_im, m_s5_d, m_s5_w_glu, m_s5_b_glu, m_w_out, m_ln_g, m_ln_b, v_rel_bias, v_w_ada, v_b_ada, v_w_in, v_conv_a, v_conv_c, v_conv_c_b, v_lru_wa, v_lru_ba, v_lru_wx, v_lru_bx, v_lru_lambda, v_s5_lam_re, v_s5_lam_im, v_s5_log_dt, v_s5_b_re, v_s5_b_im, v_s5_c_re, v_s5_c_im, v_s5_d, v_s5_w_glu, v_s5_b_glu, v_w_out, v_ln_g, v_ln_b):
    args = dict(locals())
    names = ("rel_bias", "w_ada", "b_ada", "w_in", "conv_a", "conv_c", "conv_c_b", "lru_wa", "lru_ba", "lru_wx",
             "lru_bx", "lru_lambda", "s5_lam_re", "s5_lam_im", "s5_log_dt", "s5_b_re", "s5_b_im", "s5_c_re", "s5_c_im",
             "s5_d", "s5_w_glu", "s5_b_glu", "w_out", "ln_g", "ln_b")
    w = {n: args[n] for n in names}
    mom = {n: args["m_" + n] for n in names}
    var = {n: args["v_" + n] for n in names}
    chip = 2 * lax.axis_index("x") + lax.axis_index("y")
    me = 2 * chip + lax.axis_index("c")
    ada_w = 3 * D_MODEL // N_CHIPS
    in_w = N_IN // N_CHIPS
    out_r = D_MODEL // N_CHIPS
    glu_r = BR // N_CHIPS
    conv_w = BR // N_CHIPS

    comm = _Comm(w["w_in"].astype(WIRE_DTYPE), w["w_out"].astype(WIRE_DTYPE), w["s5_w_glu"].astype(WIRE_DTYPE))

    taps = jnp.concatenate([w["conv_a"].reshape(DEPTH * 3, conv_w), w["conv_c"].reshape(DEPTH * 4, conv_w)])
    first = jnp.concatenate([c, jnp.pad(taps, ((0, 1), (0, D_MODEL - conv_w)))])
    got = _allgather8(first, "gather_c_taps").reshape(N_CHIPS, 2, 16, D_MODEL)
    c_all = got[:, :, 0].reshape(N_DEV, D_MODEL)
    taps_all = jnp.transpose(got[:, 0, 1:1 + DEPTH * 7, 0:conv_w], (1, 0, 2)).reshape(DEPTH * 7, BR)
    conv_a_f = taps_all[0:DEPTH * 3].reshape(DEPTH, 3, BR)
    conv_c_f = taps_all[DEPTH * 3:].reshape(DEPTH, 4, BR)

    cond_all = _silu_rows(c_all)
    ada_part = jnp.stack([_mm(cond_all, w["w_ada"][l], name="ada_fwd", tk=D_MODEL, tn=512,
                              bias=_take_cols(w["b_ada"][l][None], chip, ada_w)) for l in range(DEPTH)])
    ada_all = _allgather8(ada_part.reshape(DEPTH * N_DEV, ada_w), "gather_ada")
    ada_all = ada_all.reshape(N_CHIPS, 2, DEPTH, N_DEV, ada_w)[:, 0]
    ada_rows = lax.dynamic_index_in_dim(ada_all, me, axis=2, keepdims=False)
    ada_rows = jnp.transpose(ada_rows, (1, 0, 2)).reshape(DEPTH, 3 * D_MODEL)

    p = dict(w)
    p["conv_a"], p["conv_c"] = conv_a_f, conv_c_f
    loss, dx, _, small = _local_step(x[0], loss_target[0], ada_rows, None, None, None, p, comm)

    sums = [_sum_leading([comm.recv[name, l] for l in range(DEPTH)], 256, "sum_chips")
            for name in ("w_in", "w_out", "s5_w_glu")]
    others = _sibling_swap(sums, "swap_cores")
    out = {}
    for name, mine, other in zip(("w_in", "w_out", "s5_w_glu"), sums, others):
        shp = w[name].shape
        flat = lambda t: t.reshape(-1, shp[-1])
        res = _adamw(flat(w[name]), [mine, other], flat(mom[name]), flat(var[name]), 128, "adamw_big")
        out[name] = [t.reshape(shp) for t in res]

    small_names = SMALL + ("ada",)
    small["loss"] = loss
    order = small_names + ("loss",)
    shapes = [small[n].shape for n in order]
    gathered = _allgather8(_pack([small[n] for n in order]), "gather_small")
    gathered = gathered.reshape(N_DEV, -1, 128)
    total = dict(zip(order, _unpack(_sum_leading([gathered], PACK_ROWS, "sum_devices"), shapes)))
    d_ada_all = _unpack(gathered, shapes)[order.index("ada")]
    g_small = {n: total[n] for n in SMALL}
    g_small["conv_a"] = _take_cols(total["conv_a"], chip, conv_w)
    g_small["conv_c"] = _take_cols(total["conv_c"], chip, conv_w)
    g_small["b_ada"] = total["ada"]
    g_w_ada = jnp.stack([_mm(cond_all, _take_cols(d_ada_all[:, l], chip, ada_w), name="dw_ada", ta=True, tn=ada_w)
                         for l in range(DEPTH)])
    upd_names = SMALL + ("b_ada",)
    upd_shapes = [w[n].shape for n in upd_names]
    res = _adamw(_pack([w[n] for n in upd_names]), [_pack([g_small[n] for n in upd_names])],
                 _pack([mom[n] for n in upd_names]), _pack([var[n] for n in upd_names]), PACK_ROWS, "adamw_small")
    for k, t in enumerate(res):
        for n, val in zip(upd_names, _unpack(t, upd_shapes)):
            out.setdefault(n, [None] * 4)[k] = val
    shp = w["w_ada"].shape
    flat = lambda t: t.reshape(-1, shp[-1])
    out["w_ada"] = [t.reshape(shp) for t in _adamw(flat(w["w_ada"]), [flat(g_w_ada)], flat(mom["w_ada"]),
                                                  flat(var["w_ada"]), 128, "adamw_ada")]
    return (total["loss"].reshape(()), dx[None]) + tuple(out[n][k] for k in range(4) for n in names)
```

```python
import functools
import math

import jax
import jax.numpy as jnp
from jax import lax
from jax.experimental import pallas as pl
from jax.experimental.pallas import tpu as pltpu

F32 = jnp.float32
MXU_DTYPE = jnp.bfloat16
WIRE_DTYPE = jnp.bfloat16
SDS = jax.ShapeDtypeStruct
MESH = pl.DeviceIdType.MESH
ANY = pl.BlockSpec(memory_space=pl.ANY)
VMEM_LIMIT = 48 * 1024 * 1024

D_MODEL = 2048
DEPTH = 2
BR = 512
ATT_HEADS = 8
HEAD_DIM = 64
DILATIONS = ((128, 1), (512, 4), (2048, 16))
BLK = 128
REL_BUCKETS = 32
REL_MAX_DIST = 2048
LRU_HEADS = 8
LRU_C = 8.0
S5_CH = 16
S5_GROUPS = 32
S5_STATE = 64
S5_N = S5_GROUPS * S5_STATE
N_IN = 12 * BR
ALPHA = (2 * DEPTH) ** 0.25
LN_EPS = 1e-5
NEG = -1e30
ADAM_LR, ADAM_B1, ADAM_B2, ADAM_EPS, ADAM_WD, ADAM_STEP = 0.001, 0.9, 0.999, 1e-08, 0.01, 10
CB_AB, CB_AC, CB_AX, CB_AG, CB_Q, CB_K, CB_V, CB_BG, CB_CX, CB_CG, CB_DU, CB_DG = range(12)
N_CHIPS = 4
N_DEV = 8


def _params(n_axes=0):
    kw = {"dimension_semantics": ("arbitrary",) * n_axes} if n_axes else {}
    return pltpu.CompilerParams(vmem_limit_bytes=VMEM_LIMIT, **kw)


def _rows(tb, w, cb=0):
    return pl.BlockSpec((tb, w), lambda i: (i, cb))


def _prev8(tb, w, cb=0):
    return pl.BlockSpec((8, w), lambda i: (jnp.maximum(i * (tb // 8) - 1, 0), cb))


def _next8(tb, w, n_rows, cb=0):
    return pl.BlockSpec((8, w), lambda i: (jnp.minimum((i + 1) * (tb // 8), n_rows // 8 - 1), cb))


def _const(shape):
    return pl.BlockSpec(shape, lambda *_: (0,) * len(shape))


def _silu(x):
    return x * jax.nn.sigmoid(x)


def _dsilu(x):
    s = jax.nn.sigmoid(x)
    return s * (1.0 + x * (1.0 - s))


def _shift_down(cur, prev8, j):
    rolled = pltpu.roll(cur, j, 0)
    row = lax.broadcasted_iota(jnp.int32, (8, cur.shape[1]), 0)
    first = jnp.where(row < j, pltpu.roll(prev8, j, 0), rolled[0:8])
    return jnp.concatenate([first, rolled[8:]], axis=0)


def _shift_up(cur, next8, j):
    t = cur.shape[0]
    rolled = pltpu.roll(cur, t - j, 0)
    row = lax.broadcasted_iota(jnp.int32, (8, cur.shape[1]), 0)
    last = jnp.where(row >= 8 - j, pltpu.roll(next8, 8 - j, 0), rolled[t - 8:t])
    return jnp.concatenate([rolled[:t - 8], last], axis=0)


def _colsum(x):
    return jnp.sum(x, axis=0, keepdims=True)


def _init_acc(*refs):
    @pl.when(pl.program_id(0) == 0)
    def _():
        for r in refs:
            r[...] = jnp.zeros_like(r)


def _call(body, *, name, out_shape, grid, in_specs, out_specs, scratch_shapes, args, carry=None):
    out_shape, out_specs, in_specs = tuple(out_shape), tuple(out_specs), list(in_specs)
    scratch_shapes = list(scratch_shapes)
    if carry is None:
        return pl.pallas_call(body, name=name, out_shape=out_shape, grid=grid, in_specs=in_specs, out_specs=out_specs,
                              scratch_shapes=scratch_shapes, compiler_params=_params(len(grid)))(*args)
    n_in, n_out, n_scr = len(in_specs), len(out_shape), len(scratch_shapes)

    def wrapped(*refs):
        ins, refs = refs[:n_in], refs[n_in:]
        x_ins, refs = refs[:carry.n_in], refs[carry.n_in:]
        outs, refs = refs[:n_out], refs[n_out:]
        x_outs, refs = refs[:carry.n_out], refs[carry.n_out:]
        scr, x_sems = refs[:n_scr], refs[n_scr:]
        at = [pl.program_id(d) for d in range(len(grid))]
        first = functools.reduce(lambda p, q: p & q, [i == 0 for i in at])
        last = functools.reduce(lambda p, q: p & q, [i == g - 1 for i, g in zip(at, grid)])
        pl.when(first)(lambda: carry.start(x_ins, x_outs, x_sems))
        body(*ins, *outs, *scr)
        pl.when(last)(lambda: carry.wait(x_ins, x_outs, x_sems))

    return pl.pallas_call(
        wrapped, name=name, out_shape=out_shape + carry.out_shapes, grid=grid, in_specs=in_specs + [ANY] * carry.n_in,
        out_specs=out_specs + (ANY,) * carry.n_out, scratch_shapes=scratch_shapes + carry.scratch,
        compiler_params=_params(len(grid)))(*args, *carry.arrays)


def _mm(a, b, *, name, ta=False, tb=False, out_dtype=F32, tm=512, tn=512, tk=512, a_col0=0, a_ncols=None, bias=None,
        carry=None):
    a_ncols = a.shape[1] - a_col0 if a_ncols is None else a_ncols
    m, k = (a_ncols, a.shape[0]) if ta else (a.shape[0], a_ncols)
    n = b.shape[0] if tb else b.shape[1]
    assert k == (b.shape[1] if tb else b.shape[0]), (name, a.shape, b.shape)
    tm, tn, tk = min(tm, m), min(tn, n), min(tk, k)
    nk = k // tk
    a_off = a_col0 // (tm if ta else tk)
    assert m % tm == 0 and n % tn == 0 and k % tk == 0 and a_col0 % (tm if ta else tk) == 0, (name, m, n, k)

    def body(*refs):
        if bias is None:
            a_ref, b_ref, o_ref, acc = refs
        else:
            a_ref, b_ref, bias_ref, o_ref, acc = refs
        kk = pl.program_id(2)

        @pl.when(kk == 0)
        def _():
            acc[...] = jnp.zeros_like(acc)

        dims = (((0 if ta else 1,), (1 if tb else 0,)), ((), ()))
        acc[...] += lax.dot_general(a_ref[...].astype(MXU_DTYPE), b_ref[...].astype(MXU_DTYPE), dims,
                                    preferred_element_type=F32)

        @pl.when(kk == nk - 1)
        def _():
            r = acc[...]
            if bias is not None:
                r = r + bias_ref[...]
            o_ref[...] = r.astype(out_dtype)

    a_spec = (pl.BlockSpec((tk, tm), lambda i, j, kk: (kk, i + a_off)) if ta
              else pl.BlockSpec((tm, tk), lambda i, j, kk: (i, kk + a_off)))
    b_spec = (pl.BlockSpec((tn, tk), lambda i, j, kk: (j, kk)) if tb
              else pl.BlockSpec((tk, tn), lambda i, j, kk: (kk, j)))
    in_specs, args = [a_spec, b_spec], [a, b]
    if bias is not None:
        in_specs.append(pl.BlockSpec((1, tn), lambda i, j, kk: (0, j)))
        args.append(bias)
    out = _call(body, name=name, out_shape=[SDS((m, n), out_dtype)], grid=(m // tm, n // tn, nk), in_specs=in_specs,
                out_specs=[pl.BlockSpec((tm, tn), lambda i, j, kk: (i, j))],
                scratch_shapes=[pltpu.VMEM((tm, tn), F32)], args=args, carry=carry)
    return out[0] if carry is None else out


def _silu_rows(c_all):
    def body(c_ref, o_ref):
        o_ref[...] = _silu(c_ref[...])
    return pl.pallas_call(body, name="cond_silu", out_shape=SDS(c_all.shape, F32))(c_all)


def _modulate(x, scale, shift, tb):
    s, d = x.shape

    def body(x_ref, sc_ref, sh_ref, o_ref):
        o_ref[...] = (x_ref[...] * (1.0 + sc_ref[...]) + sh_ref[...]).astype(MXU_DTYPE)

    return pl.pallas_call(body, name="modulate", out_shape=SDS((s, d), MXU_DTYPE), grid=(s // tb,),
                          in_specs=[_rows(tb, d), _const((1, d)), _const((1, d))], out_specs=_rows(tb, d),
                          compiler_params=_params(1))(x, scale, shift)


def _out_ln(ycat, w_out, x, gate, ln_g, ln_b, tb):
    s, d = x.shape

    def body(yc_ref, w_ref, x_ref, gt_ref, g_ref, b_ref, xn_ref, xh_ref, y_ref, rs_ref):
        y = jnp.dot(yc_ref[...], w_ref[...], preferred_element_type=F32)
        res = ALPHA * x_ref[...] + (1.0 + gt_ref[...]) * y
        mu = jnp.mean(res, axis=-1, keepdims=True)
        cen = res - mu
        var = jnp.mean(cen * cen, axis=-1, keepdims=True)
        rstd = lax.rsqrt(var + LN_EPS)
        xhat = cen * rstd
        xn_ref[...] = xhat * g_ref[...] + b_ref[...]
        xh_ref[...] = xhat
        y_ref[...] = y
        rs_ref[...] = rstd

    big = SDS((s, d), F32)
    return pl.pallas_call(
        body, name="out_proj_ln", out_shape=(big, big, big, SDS((s, 1), F32)), grid=(s // tb,),
        in_specs=[_rows(tb, d), _const((d, d)), _rows(tb, d), _const((1, d)), _const((1, d)), _const((1, d))],
        out_specs=(_rows(tb, d), _rows(tb, d), _rows(tb, d), _rows(tb, 1)), compiler_params=_params(1),
    )(ycat, w_out, x, gate, ln_g, ln_b)


def _ln_bwd(dxn, xhat, y, rstd, ln_g, gate, tb):
    s, d = dxn.shape

    def body(dxn_ref, xh_ref, y_ref, rs_ref, g_ref, gt_ref, dy_ref, dxa_ref, dg_ref, db_ref, dgt_ref):
        _init_acc(dg_ref, db_ref, dgt_ref)
        dxn_t, xh = dxn_ref[...], xh_ref[...]
        dxh = dxn_t * g_ref[...]
        dres = rs_ref[...] * (dxh - jnp.mean(dxh, axis=-1, keepdims=True)
                              - xh * jnp.mean(dxh * xh, axis=-1, keepdims=True))
        dy_ref[...] = ((1.0 + gt_ref[...]) * dres).astype(MXU_DTYPE)
        dxa_ref[...] = ALPHA * dres
        dg_ref[...] += _colsum(dxn_t * xh)
        db_ref[...] += _colsum(dxn_t)
        dgt_ref[...] += _colsum(dres * y_ref[...])

    vec = SDS((1, d), F32)
    return pl.pallas_call(
        body, name="ln_bwd", out_shape=(SDS((s, d), MXU_DTYPE), SDS((s, d), F32), vec, vec, vec), grid=(s // tb,),
        in_specs=[_rows(tb, d), _rows(tb, d), _rows(tb, d), _rows(tb, 1), _const((1, d)), _const((1, d))],
        out_specs=(_rows(tb, d), _rows(tb, d), _const((1, d)), _const((1, d)), _const((1, d))),
        compiler_params=_params(1))(dxn, xhat, y, rstd, ln_g, gate)


def _mod_bwd(dh, dxa, x, scale, tb):
    s, d = dh.shape

    def body(dh_ref, dxa_ref, x_ref, sc_ref, dx_ref, dsh_ref, dsc_ref):
        _init_acc(dsh_ref, dsc_ref)
        dh_t = dh_ref[...]
        dx_ref[...] = dxa_ref[...] + dh_t * (1.0 + sc_ref[...])
        dsh_ref[...] += _colsum(dh_t)
        dsc_ref[...] += _colsum(dh_t * x_ref[...])

    vec = SDS((1, d), F32)
    return pl.pallas_call(
        body, name="mod_bwd", out_shape=(SDS((s, d), F32), vec, vec), grid=(s // tb,),
        in_specs=[_rows(tb, d), _rows(tb, d), _rows(tb, d), _const((1, d))],
        out_specs=(_rows(tb, d), _const((1, d)), _const((1, d))), compiler_params=_params(1))(dh, dxa, x, scale)


def _loss_head(y, target, tb):
    s, d = y.shape

    def body(y_ref, t_ref, l_ref, dy_ref):
        _init_acc(l_ref)
        err = y_ref[...] - t_ref[...]
        l_ref[...] += (0.5 / d) * jnp.sum(err * err, keepdims=True)
        dy_ref[...] = err * (1.0 / d)

    return pl.pallas_call(body, name="loss_head", out_shape=(SDS((1, 1), F32), SDS((s, d), F32)), grid=(s // tb,),
                          in_specs=[_rows(tb, d), _rows(tb, d)], out_specs=(_const((1, 1)), _rows(tb, d)),
                          compiler_params=_params(1))(y, target)


def _conv_taps(u, up, w_ref, width):
    out = w_ref[width - 1:width, :] * u
    for j in range(width - 2, -1, -1):
        out = out + w_ref[j:j + 1, :] * _shift_down(u, up, width - 1 - j)
    return out


def _conv_taps_t(g, gn, w_ref, width):
    out = w_ref[width - 1:width, :] * g
    for j in range(width - 2, -1, -1):
        out = out + w_ref[j:j + 1, :] * _shift_up(g, gn, width - 1 - j)
    return out


def _conv_wgrad(dw_ref, g, u, up, width):
    dw_ref[width - 1:width, :] += _colsum(g * u)
    for j in range(width - 1):
        dw_ref[j:j + 1, :] += _colsum(g * _shift_down(u, up, width - 1 - j))


def _branch_a_fwd(proj, conv_w, tb):
    s = proj.shape[0]

    def body(ab, ac, ax, ag, acp, axp, w_ref, o_ref):
        has_prev = (pl.program_id(0) > 0).astype(F32)
        u = ac[...] * ax[...]
        up = acp[...] * axp[...] * has_prev
        o_ref[...] = (ab[...] * _conv_taps(u, up, w_ref, 3) * _silu(ag[...])).astype(MXU_DTYPE)

    return pl.pallas_call(
        body, name="branch_a_fwd", out_shape=SDS((s, BR), MXU_DTYPE), grid=(s // tb,),
        in_specs=[_rows(tb, BR, CB_AB), _rows(tb, BR, CB_AC), _rows(tb, BR, CB_AX), _rows(tb, BR, CB_AG),
                  _prev8(tb, BR, CB_AC), _prev8(tb, BR, CB_AX), _const((8, BR))],
        out_specs=_rows(tb, BR), compiler_params=_params(1))(proj, proj, proj, proj, proj, proj, conv_w)


def _branch_a_bwd(dycat, proj, conv_w, tb):
    s = proj.shape[0]

    def body(dy, dyn, ab, abn, ag, agn, ac, acp, ax, axp, w_ref, o_ref, dw_ref):
        _init_acc(dw_ref)
        i = pl.program_id(0)
        has_prev = (i > 0).astype(F32)
        has_next = (i < pl.num_programs(0) - 1).astype(F32)
        u = ac[...] * ax[...]
        up = acp[...] * axp[...] * has_prev
        v = _conv_taps(u, up, w_ref, 3)
        sg = _silu(ag[...])
        dv = dy[...] * ab[...] * sg
        dvn = dyn[...] * abn[...] * _silu(agn[...]) * has_next
        du = _conv_taps_t(dv, dvn, w_ref, 3)
        o_ref[:, 0:BR] = (dy[...] * v * sg).astype(MXU_DTYPE)
        o_ref[:, BR:2 * BR] = (du * ax[...]).astype(MXU_DTYPE)
        o_ref[:, 2 * BR:3 * BR] = (du * ac[...]).astype(MXU_DTYPE)
        o_ref[:, 3 * BR:4 * BR] = (dy[...] * ab[...] * v * _dsilu(ag[...])).astype(MXU_DTYPE)
        _conv_wgrad(dw_ref, dv, u, up, 3)

    return pl.pallas_call(
        body, name="branch_a_bwd", out_shape=(SDS((s, 4 * BR), MXU_DTYPE), SDS((8, BR), F32)), grid=(s // tb,),
        in_specs=[_rows(tb, BR, 0), _next8(tb, BR, s, 0),
                  _rows(tb, BR, CB_AB), _next8(tb, BR, s, CB_AB), _rows(tb, BR, CB_AG), _next8(tb, BR, s, CB_AG),
                  _rows(tb, BR, CB_AC), _prev8(tb, BR, CB_AC), _rows(tb, BR, CB_AX), _prev8(tb, BR, CB_AX),
                  _const((8, BR))],
        out_specs=(_rows(tb, 4 * BR), _const((8, BR))), compiler_params=_params(1),
    )(dycat, dycat, proj, proj, proj, proj, proj, proj, proj, proj, conv_w)


def _t5_bucket(dist):
    max_exact = REL_BUCKETS // 2
    nf = jnp.maximum(dist, 1).astype(F32)
    large = max_exact + (jnp.log(nf / max_exact) / math.log(REL_MAX_DIST / max_exact)
                         * (REL_BUCKETS - max_exact)).astype(jnp.int32)
    large = jnp.minimum(large, REL_BUCKETS - 1)
    return jnp.where(dist < max_exact, dist, large)


def _bucket_maps():
    maps = []
    i = jnp.arange(BLK)[:, None]
    j = jnp.arange(2 * BLK)[None, :]
    delta = i + BLK - j
    for window, dil in DILATIONS:
        span = window // dil
        bucket = _t5_bucket(jnp.clip(delta, 0, span) * dil)
        maps.append(jnp.where((delta >= 0) & (delta <= span), bucket, -1))
    return jnp.stack(maps).astype(jnp.int32)


def _bias_tables(rel_bias, buckets):
    n_pat = len(DILATIONS)

    def body(rb_ref, bk_ref, o_ref):
        for g in range(n_pat):
            bk = bk_ref[g]
            for h in range(ATT_HEADS):
                def per_bucket(b, acc):
                    return jnp.where(bk == b, rb_ref[b, h], acc)
                o_ref[g, h] = lax.fori_loop(0, REL_BUCKETS, per_bucket, jnp.full((BLK, 2 * BLK), NEG, F32))

    return pl.pallas_call(
        body, name="bias_tables", out_shape=SDS((n_pat, ATT_HEADS, BLK, 2 * BLK), F32),
        in_specs=[pl.BlockSpec(memory_space=pltpu.SMEM), pl.BlockSpec(memory_space=pltpu.VMEM)],
        compiler_params=_params())(rel_bias, buckets)


def _head_masks():
    lane = lax.broadcasted_iota(jnp.int32, (1, 2 * HEAD_DIM), 1)
    return [(lane < HEAD_DIM).astype(F32), (lane >= HEAD_DIM).astype(F32)]


def _strided(base, size, dil):
    return pl.ds(base, size, stride=dil) if dil > 1 else pl.ds(pl.multiple_of(base, BLK), size)


def _attn_groups(s, dil):
    return max(1, min(1024, s) // (dil * BLK)) if dil == 1 else max(1, min(2048, s) // (dil * BLK))


def _attn_fwd(proj, bias, dil):
    s = proj.shape[0]
    grp = _attn_groups(s, dil)
    u1 = dil * BLK
    unit = grp * u1
    nb = s // unit
    w = 2 * HEAD_DIM
    q0, k0, v0 = (cb * (BR // w) for cb in (CB_Q, CB_K, CB_V))

    def body(q_ref, kc_ref, kp_ref, vc_ref, vp_ref, bias_ref, o_ref, lse_ref, kbuf, vbuf):
        n = pl.program_id(1)
        col = lax.broadcasted_iota(jnp.int32, (1, 2 * BLK), 1)
        masks = _head_masks()
        kbuf[0:u1, :] = kp_ref[...]
        kbuf[u1:, :] = kc_ref[...]
        vbuf[0:u1, :] = vp_ref[...]
        vbuf[u1:, :] = vc_ref[...]

        def per_r(t, carry):
            j = t // dil
            base = j * u1 + t % dil
            rows = _strided(base, BLK, dil)
            no_prev = jnp.where((n == 0) & (j == 0) & (col < BLK), NEG, 0.0)
            q = q_ref[rows, :] * (HEAD_DIM ** -0.5)
            k = kbuf[_strided(base, 2 * BLK, dil), :].astype(MXU_DTYPE)
            v = vbuf[_strided(base, 2 * BLK, dil), :].astype(MXU_DTYPE)
            o_acc = jnp.zeros((BLK, w), F32)
            lse_acc = jnp.zeros((BLK, w), F32)
            for h in range(2):
                qh = (q * masks[h]).astype(MXU_DTYPE)
                sc = lax.dot_general(qh, k, (((1,), (1,)), ((), ())), preferred_element_type=F32)
                sc = sc + bias_ref[h] + no_prev
                mx = jnp.max(sc, axis=-1, keepdims=True)
                p = jnp.exp(sc - mx)
                l = jnp.sum(p, axis=-1, keepdims=True)
                oh = jnp.dot((p / l).astype(MXU_DTYPE), v, preferred_element_type=F32)
                o_acc = o_acc + oh * masks[h]
                lse_acc = lse_acc + (mx + jnp.log(l)) * masks[h]
            o_ref[rows, :] = o_acc
            lse_ref[rows, :] = lse_acc
            return carry

        lax.fori_loop(0, grp * dil, per_r, 0, unroll=2)

    cur = lambda c0: pl.BlockSpec((unit, w), lambda hp, n: (n, c0 + hp))
    prev = lambda c0: pl.BlockSpec((u1, w), lambda hp, n: (jnp.maximum(n * grp - 1, 0), c0 + hp))
    out = pl.BlockSpec((unit, w), lambda hp, n: (n, hp))
    return pl.pallas_call(
        body, name=f"attn_fwd_d{dil}", out_shape=(SDS((s, BR), F32), SDS((s, BR), F32)), grid=(BR // w, nb),
        in_specs=[cur(q0), cur(k0), prev(k0), cur(v0), prev(v0),
                  pl.BlockSpec((2, BLK, 2 * BLK), lambda hp, n: (hp, 0, 0))],
        out_specs=(out, out),
        scratch_shapes=[pltpu.VMEM((unit + u1, w), F32), pltpu.VMEM((unit + u1, w), F32)],
        compiler_params=_params(2))(proj, proj, proj, proj, proj, bias)


def _softmax3(l0, l1, l2):
    mx = jnp.maximum(jnp.maximum(l0, l1), l2)
    e0, e1, e2 = jnp.exp(l0 - mx), jnp.exp(l1 - mx), jnp.exp(l2 - mx)
    inv = 1.0 / (e0 + e1 + e2)
    return e0 * inv, e1 * inv, e2 * inv


def _attn_combine(os_, lses, proj, tb):
    s = proj.shape[0]

    def body(o0, o1, o2, l0, l1, l2, bg, y_ref):
        w0, w1, w2 = _softmax3(l0[...], l1[...], l2[...])
        attn = w0 * o0[...] + w1 * o1[...] + w2 * o2[...]
        y_ref[...] = (attn * _silu(bg[...])).astype(MXU_DTYPE)

    return pl.pallas_call(
        body, name="attn_combine", out_shape=SDS((s, BR), MXU_DTYPE), grid=(s // tb,),
        in_specs=[_rows(tb, BR)] * 6 + [_rows(tb, BR, CB_BG)], out_specs=_rows(tb, BR),
        compiler_params=_params(1))(*os_, *lses, proj)


def _attn_bwd_pre(dycat, os_, lses, proj, head_ones, tb):
    s = proj.shape[0]

    def body(dy, o0, o1, o2, l0, l1, l2, bg, e_ref, dbg_ref, do0, do1, do2, dm0, dm1, dm2):
        w0, w1, w2 = _softmax3(l0[...], l1[...], l2[...])
        attn = w0 * o0[...] + w1 * o1[...] + w2 * o2[...]
        dattn = dy[...] * _silu(bg[...])
        dbg_ref[...] = dy[...] * attn * _dsilu(bg[...])
        prod = dattn * attn
        hi = prod.astype(MXU_DTYPE)
        lo = (prod - hi.astype(F32)).astype(MXU_DTYPE)
        tot = (jnp.dot(hi, e_ref[...], preferred_element_type=F32)
               + jnp.dot(lo, e_ref[...], preferred_element_type=F32))
        for wg, do_ref, dm_ref in ((w0, do0, dm0), (w1, do1, dm1), (w2, do2, dm2)):
            do_ref[...] = wg * dattn
            dm_ref[...] = wg * tot

    big = SDS((s, BR), F32)
    return pl.pallas_call(
        body, name="attn_bwd_pre", out_shape=(big,) * 7, grid=(s // tb,),
        in_specs=[_rows(tb, BR, 1)] + [_rows(tb, BR)] * 6 + [_rows(tb, BR, CB_BG), _const((BR, BR))],
        out_specs=(_rows(tb, BR),) * 7, compiler_params=_params(1))(dycat, *os_, *lses, proj, head_ones)


def _attn_bwd(proj, do, lse, dm, bias, dil, carry=None):
    s = proj.shape[0]
    grp = _attn_groups(s, dil)
    u1 = dil * BLK
    unit = grp * u1
    nb = s // unit
    w = 2 * HEAD_DIM
    q0, k0, v0 = (cb * (BR // w) for cb in (CB_Q, CB_K, CB_V))

    def body(q_ref, kc_ref, kp_ref, vc_ref, vp_ref, do_ref, lse_ref, dm_ref, bias_ref,
             dq_ref, dk_ref, dv_ref, dbias_ref, kbuf, vbuf, stage_k, stage_v):
        n = pl.program_id(1)
        col = lax.broadcasted_iota(jnp.int32, (1, 2 * BLK), 1)
        masks = _head_masks()

        @pl.when(n == 0)
        def _():
            dbias_ref[...] = jnp.zeros_like(dbias_ref)
            stage_k[...] = jnp.zeros_like(stage_k)
            stage_v[...] = jnp.zeros_like(stage_v)

        for out_ref, stage in ((dk_ref, stage_k), (dv_ref, stage_v)):
            if grp > 1:
                out_ref[0:unit - u1, :] = stage[u1:unit, :]
            stage[0:u1, :] = stage[unit:unit + u1, :]

        @pl.when(n < nb)
        def _():
            kbuf[0:u1, :] = kp_ref[...]
            kbuf[u1:, :] = kc_ref[...]
            vbuf[0:u1, :] = vp_ref[...]
            vbuf[u1:, :] = vc_ref[...]

            def per_r(t, carry):
                j = t // dil
                base = j * u1 + t % dil
                rows = _strided(base, BLK, dil)
                rows_hi = _strided(base + u1, BLK, dil)
                no_prev = jnp.where((n == 0) & (j == 0) & (col < BLK), NEG, 0.0)
                q = q_ref[rows, :] * (HEAD_DIM ** -0.5)
                k = kbuf[_strided(base, 2 * BLK, dil), :].astype(MXU_DTYPE)
                v = vbuf[_strided(base, 2 * BLK, dil), :].astype(MXU_DTYPE)
                do_t, lse_t, dm_t = do_ref[rows, :], lse_ref[rows, :], dm_ref[rows, :]
                dq_acc = jnp.zeros((BLK, w), F32)
                dk_acc = jnp.zeros((2 * BLK, w), F32)
                dv_acc = jnp.zeros((2 * BLK, w), F32)
                for h in range(2):
                    qh = (q * masks[h]).astype(MXU_DTYPE)
                    doh = (do_t * masks[h]).astype(MXU_DTYPE)
                    c0 = h * HEAD_DIM
                    sc = lax.dot_general(qh, k, (((1,), (1,)), ((), ())), preferred_element_type=F32)
                    p = jnp.exp(sc + bias_ref[h] + no_prev - lse_t[:, c0:c0 + 1])
                    dp = lax.dot_general(doh, v, (((1,), (1,)), ((), ())), preferred_element_type=F32)
                    ds = p * (dp - dm_t[:, c0:c0 + 1])
                    dbias_ref[h] += ds
                    dsb, pb = ds.astype(MXU_DTYPE), p.astype(MXU_DTYPE)
                    dq_acc = dq_acc + jnp.dot(dsb, k, preferred_element_type=F32) * masks[h]
                    dk_acc = dk_acc + lax.dot_general(dsb, qh, (((0,), (0,)), ((), ())), preferred_element_type=F32)
                    dv_acc = dv_acc + lax.dot_general(pb, doh, (((0,), (0,)), ((), ())), preferred_element_type=F32)
                dq_ref[rows, :] = dq_acc * (HEAD_DIM ** -0.5)
                stage_k[rows, :] = stage_k[rows, :] + dk_acc[0:BLK]
                stage_v[rows, :] = stage_v[rows, :] + dv_acc[0:BLK]
                stage_k[rows_hi, :] = dk_acc[BLK:2 * BLK]
                stage_v[rows_hi, :] = dv_acc[BLK:2 * BLK]
                return carry

            lax.fori_loop(0, grp * dil, per_r, 0, unroll=2)

        dk_ref[unit - u1:unit, :] = stage_k[0:u1, :]
        dv_ref[unit - u1:unit, :] = stage_v[0:u1, :]

    qn = lambda n: jnp.minimum(n, nb - 1)
    cur = lambda c0: pl.BlockSpec((unit, w), lambda hp, n: (qn(n), c0 + hp))
    prev = lambda c0: pl.BlockSpec((u1, w), lambda hp, n: (jnp.maximum(qn(n) * grp - 1, 0), c0 + hp))
    row = pl.BlockSpec((unit, w), lambda hp, n: (qn(n), hp))
    late = pl.BlockSpec((unit, w), lambda hp, n: (jnp.maximum(n - 1, 0), hp))
    tab = pl.BlockSpec((2, BLK, 2 * BLK), lambda hp, n: (hp, 0, 0))
    big = SDS((s, BR), F32)
    return _call(
        body, name=f"attn_bwd_d{dil}", out_shape=(big, big, big, SDS((ATT_HEADS, BLK, 2 * BLK), F32)),
        grid=(BR // w, nb + 1),
        in_specs=[cur(q0), cur(k0), prev(k0), cur(v0), prev(v0), row, row, row, tab],
        out_specs=(row, late, late, tab),
        scratch_shapes=[pltpu.VMEM((unit + u1, w), F32)] * 4,
        args=(proj, proj, proj, proj, proj, do, lse, dm, bias), carry=carry)


def _rel_bias_grad(dbias, buckets):
    def body(db_ref, bk_ref, o_ref):
        row = lax.broadcasted_iota(jnp.int32, (REL_BUCKETS, 128), 0)
        lane = lax.broadcasted_iota(jnp.int32, (REL_BUCKETS, 128), 1)

        def per_bucket(b, acc):
            for g in range(len(DILATIONS)):
                hit = bk_ref[g] == b
                for h in range(ATT_HEADS):
                    both = db_ref[0, g, h] + db_ref[1, g, h]
                    val = jnp.sum(jnp.where(hit, both, 0.0), keepdims=True)
                    acc = acc + jnp.where((row == b) & (lane == h), val, 0.0)
            return acc

        o_ref[...] = lax.fori_loop(0, REL_BUCKETS, per_bucket, jnp.zeros((REL_BUCKETS, 128), F32))

    assert dbias.shape[0] == DEPTH == 2
    return pl.pallas_call(body, name="rel_bias_grad", out_shape=SDS((REL_BUCKETS, 128), F32),
                          compiler_params=_params())(dbias, buckets)


def _scan_real(a, b, *, reverse, tb, name):
    s, ch = a.shape
    nt = s // tb
    order = range(7, -1, -1) if reverse else range(8)

    def body(a_ref, b_ref, o_ref, carry):
        @pl.when(pl.program_id(0) == 0)
        def _():
            carry[...] = jnp.zeros_like(carry)

        def group(gi, h):
            r0 = pl.multiple_of((tb // 8 - 1 - gi if reverse else gi) * 8, 8)
            a8, b8 = a_ref[pl.ds(r0, 8), :], b_ref[pl.ds(r0, 8), :]
            rows = [None] * 8
            for k in order:
                if reverse:
                    rows[k] = b8[k:k + 1] + h
                    h = a8[k:k + 1] * rows[k]
                else:
                    h = a8[k:k + 1] * h + b8[k:k + 1]
                    rows[k] = h
            o_ref[pl.ds(r0, 8), :] = jnp.concatenate(rows, axis=0)
            return h

        carry[...] = lax.fori_loop(0, tb // 8, group, carry[...])

    spec = pl.BlockSpec((tb, ch), (lambda i: (nt - 1 - i, 0)) if reverse else (lambda i: (i, 0)))
    return pl.pallas_call(body, name=name, out_shape=SDS((s, ch), F32), grid=(nt,), in_specs=[spec, spec],
                          out_specs=spec, scratch_shapes=[pltpu.VMEM((1, ch), F32)],
                          compiler_params=_params(1))(a, b)


def _scan_cplx(b, a_row, *, reverse, tb, name):
    s, ch2 = b.shape
    ch = ch2 // 2
    nt = s // tb
    order = range(7, -1, -1) if reverse else range(8)

    def body(a_ref, b_ref, o_ref, carry):
        @pl.when(pl.program_id(0) == 0)
        def _():
            carry[...] = jnp.zeros_like(carry)

        ar = a_ref[:, 0:ch]
        ai = -a_ref[:, ch:ch2] if reverse else a_ref[:, ch:ch2]

        def group(gi, x):
            xr, xi = x
            r0 = pl.multiple_of((tb // 8 - 1 - gi if reverse else gi) * 8, 8)
            br8, bi8 = b_ref[pl.ds(r0, 8), 0:ch], b_ref[pl.ds(r0, 8), ch:ch2]
            rr, ri = [None] * 8, [None] * 8
            for k in order:
                nr = ar * xr - ai * xi + br8[k:k + 1]
                ni = ar * xi + ai * xr + bi8[k:k + 1]
                xr, xi = nr, ni
                rr[k], ri[k] = xr, xi
            o_ref[pl.ds(r0, 8), 0:ch] = jnp.concatenate(rr, axis=0)
            o_ref[pl.ds(r0, 8), ch:ch2] = jnp.concatenate(ri, axis=0)
            return xr, xi

        xr, xi = lax.fori_loop(0, tb // 8, group, (carry[:, 0:ch], carry[:, ch:ch2]))
        carry[:, 0:ch] = xr
        carry[:, ch:ch2] = xi

    spec = pl.BlockSpec((tb, ch2), (lambda i: (nt - 1 - i, 0)) if reverse else (lambda i: (i, 0)))
    return pl.pallas_call(body, name=name, out_shape=SDS((s, ch2), F32), grid=(nt,),
                          in_specs=[_const((1, ch2)), spec], out_specs=spec,
                          scratch_shapes=[pltpu.VMEM((1, ch2), F32)], compiler_params=_params(1))(a_row, b)


def _neg_expm1(z):
    series = -z * (1.0 + z * (0.5 + z * (1.0 / 6 + z * (1.0 / 24 + z * (1.0 / 120)))))
    return jnp.where(z > -0.05, series, 1.0 - jnp.exp(z))


def _lru_gate(xc, pre_r, pre_i, lam):
    log_a = -LRU_C * jax.nn.sigmoid(pre_r) * jax.nn.softplus(-lam)
    return jnp.exp(log_a), jnp.sqrt(_neg_expm1(2.0 * log_a)) * jax.nn.sigmoid(pre_i) * xc


def _lru_gates_fwd(proj, conv_w, conv_b, w_cat, b_cat, lam, tb):
    s = proj.shape[0]

    def body(cx, cxp, w_ref, cb_ref, wc_ref, bc_ref, lam_ref, a_ref, b_ref):
        has_prev = (pl.program_id(0) > 0).astype(F32)
        xc = _conv_taps(cx[...], cxp[...] * has_prev, w_ref, 4) + cb_ref[...]
        pre = jnp.dot(xc.astype(MXU_DTYPE), wc_ref[...], preferred_element_type=F32) + bc_ref[...]
        a_ref[...], b_ref[...] = _lru_gate(xc, pre[:, 0:BR], pre[:, BR:2 * BR], lam_ref[...])

    big = SDS((s, BR), F32)
    return pl.pallas_call(
        body, name="lru_gates_fwd", out_shape=(big, big), grid=(s // tb,),
        in_specs=[_rows(tb, BR, CB_CX), _prev8(tb, BR, CB_CX), _const((8, BR)), _const((1, BR)),
                  _const((BR, 2 * BR)), _const((1, 2 * BR)), _const((1, BR))],
        out_specs=(_rows(tb, BR), _rows(tb, BR)), compiler_params=_params(1),
    )(proj, proj, conv_w, conv_b, w_cat, b_cat, lam)


def _gate_out(h, proj, cb, tb, name):
    s = proj.shape[0]

    def body(h_ref, g_ref, o_ref):
        o_ref[...] = (h_ref[...] * _silu(g_ref[...])).astype(MXU_DTYPE)

    return pl.pallas_call(body, name=name, out_shape=SDS((s, BR), MXU_DTYPE), grid=(s // tb,),
                          in_specs=[_rows(tb, BR), _rows(tb, BR, cb)], out_specs=_rows(tb, BR),
                          compiler_params=_params(1))(h, proj)


def _gate_out_bwd(dycat, dy_cb, h, proj, cb, tb, name):
    s = proj.shape[0]

    def body(dy, h_ref, g_ref, dh_ref, dg_ref):
        dh_ref[...] = dy[...] * _silu(g_ref[...])
        dg_ref[...] = dy[...] * h_ref[...] * _dsilu(g_ref[...])

    big = SDS((s, BR), F32)
    return pl.pallas_call(body, name=name, out_shape=(big, big), grid=(s // tb,),
                          in_specs=[_rows(tb, BR, dy_cb), _rows(tb, BR), _rows(tb, BR, cb)],
                          out_specs=(_rows(tb, BR), _rows(tb, BR)), compiler_params=_params(1))(dycat, h, proj)


def _lru_gates_bwd(proj, lmb, h, conv_w, conv_b, w_cat, b_cat, lam, tb):
    s = proj.shape[0]

    def body(cx, cxp, l_ref, h_ref, hp_ref, w_ref, cb_ref, wc_ref, bc_ref, lam_ref,
             dxc_ref, dpre_ref, xc_ref, dbc_ref, dlam_ref):
        _init_acc(dbc_ref, dlam_ref)
        has_prev = (pl.program_id(0) > 0).astype(F32)
        xc = _conv_taps(cx[...], cxp[...] * has_prev, w_ref, 4) + cb_ref[...]
        xcb = xc.astype(MXU_DTYPE)
        pre = jnp.dot(xcb, wc_ref[...], preferred_element_type=F32) + bc_ref[...]
        _, vjp = jax.vjp(_lru_gate, xc, pre[:, 0:BR], pre[:, BR:2 * BR], lam_ref[...])
        lm = l_ref[...]
        dxc, dpr, dpi, dlam = vjp((lm * _shift_down(h_ref[...], hp_ref[...] * has_prev, 1), lm))
        dpre = jnp.concatenate([dpr, dpi], axis=1)
        dpreb = dpre.astype(MXU_DTYPE)
        dxc_ref[...] = dxc + lax.dot_general(dpreb, wc_ref[...], (((1,), (1,)), ((), ())),
                                             preferred_element_type=F32)
        dpre_ref[...] = dpreb
        xc_ref[...] = xcb
        dbc_ref[...] += _colsum(dpre)
        dlam_ref[...] += dlam

    return pl.pallas_call(
        body, name="lru_gates_bwd",
        out_shape=(SDS((s, BR), F32), SDS((s, 2 * BR), MXU_DTYPE), SDS((s, BR), MXU_DTYPE),
                   SDS((1, 2 * BR), F32), SDS((1, BR), F32)),
        grid=(s // tb,),
        in_specs=[_rows(tb, BR, CB_CX), _prev8(tb, BR, CB_CX), _rows(tb, BR), _rows(tb, BR), _prev8(tb, BR),
                  _const((8, BR)), _const((1, BR)), _const((BR, 2 * BR)), _const((1, 2 * BR)), _const((1, BR))],
        out_specs=(_rows(tb, BR), _rows(tb, 2 * BR), _rows(tb, BR), _const((1, 2 * BR)), _const((1, BR))),
        compiler_params=_params(1))(proj, proj, lmb, h, h, conv_w, conv_b, w_cat, b_cat, lam)


def _conv_c_bwd(dxc, proj, conv_w, tb):
    s = proj.shape[0]

    def body(g, gn, cx, cxp, w_ref, dcx_ref, dw_ref, db_ref):
        _init_acc(dw_ref, db_ref)
        i = pl.program_id(0)
        has_prev = (i > 0).astype(F32)
        has_next = (i < pl.num_programs(0) - 1).astype(F32)
        gt = g[...]
        dcx_ref[...] = _conv_taps_t(gt, gn[...] * has_next, w_ref, 4)
        _conv_wgrad(dw_ref, gt, cx[...], cxp[...] * has_prev, 4)
        db_ref[...] += _colsum(gt)

    return pl.pallas_call(
        body, name="conv_c_bwd", out_shape=(SDS((s, BR), F32), SDS((8, BR), F32), SDS((1, BR), F32)),
        grid=(s // tb,),
        in_specs=[_rows(tb, BR), _next8(tb, BR, s), _rows(tb, BR, CB_CX), _prev8(tb, BR, CB_CX), _const((8, BR))],
        out_specs=(_rows(tb, BR), _const((8, BR)), _const((1, BR))), compiler_params=_params(1),
    )(dxc, dxc, proj, proj, conv_w)


def _s5_disc(lam_re, lam_im, log_dt):
    dt = jnp.exp(log_dt)
    mag = jnp.exp(lam_re * dt)
    ab_re = mag * jnp.cos(lam_im * dt)
    ab_im = mag * jnp.sin(lam_im * dt)
    den = lam_re * lam_re + lam_im * lam_im
    f_re = ((ab_re - 1.0) * lam_re + ab_im * lam_im) / den
    f_im = (ab_im * lam_re - (ab_re - 1.0) * lam_im) / den
    return ab_re, ab_im, f_re, f_im


def _s5_bbar(f_re, f_im, b_re, b_im):
    return f_re * b_re - f_im * b_im, f_re * b_im + f_im * b_re


def _s5_disc_fwd(lam_re, lam_im, log_dt):
    def body(lr, li, ld, o0, o1, o2, o3):
        o0[...], o1[...], o2[...], o3[...] = _s5_disc(lr[...], li[...], ld[...])
    return pl.pallas_call(body, name="s5_disc_fwd", out_shape=(SDS(lam_re.shape, F32),) * 4)(lam_re, lam_im, log_dt)


def _s5_disc_bwd(lam_re, lam_im, log_dt, cts):
    def body(lr, li, ld, c0, c1, c2, c3, o0, o1, o2):
        _, vjp = jax.vjp(_s5_disc, lr[...], li[...], ld[...])
        o0[...], o1[...], o2[...] = vjp((c0[...], c1[...], c2[...], c3[...]))
    return pl.pallas_call(body, name="s5_disc_bwd", out_shape=(SDS(lam_re.shape, F32), SDS(lam_re.shape, F32),
                                                                SDS(log_dt.shape, F32)))(lam_re, lam_im, log_dt, *cts)


def _s5_bbar_fwd(f_re, f_im, b_re, b_im):
    def body(fr, fi, br, bi, o0, o1):
        o0[...], o1[...] = _s5_bbar(fr[...], fi[...], br[...], bi[...])
    return pl.pallas_call(body, name="s5_bbar_fwd", out_shape=(SDS(b_re.shape, F32),) * 2)(f_re, f_im, b_re, b_im)


def _s5_bbar_bwd(f_re, f_im, b_re, b_im, d_re, d_im):
    def body(fr, fi, br, bi, dr, di, o0, o1, o2, o3):
        _, vjp = jax.vjp(_s5_bbar, fr[...], fi[...], br[...], bi[...])
        o0[...], o1[...], o2[...], o3[...] = vjp((dr[...], di[...]))
    col, mat = SDS(f_re.shape, F32), SDS(b_re.shape, F32)
    return pl.pallas_call(body, name="s5_bbar_bwd", out_shape=(col, col, mat, mat))(f_re, f_im, b_re, b_im, d_re, d_im)


def _s5_tail_fwd(ylin, proj, d_skip, w_glu, b_glu, tb):
    s = proj.shape[0]

    def body(yl, u, dg, dk, w_ref, b_ref, o_ref):
        g = jax.nn.gelu(yl[...] + dk[...] * u[...])
        t = jnp.dot(g.astype(MXU_DTYPE), w_ref[...], preferred_element_type=F32) + b_ref[...]
        o_ref[...] = (g * jax.nn.sigmoid(t) * _silu(dg[...])).astype(MXU_DTYPE)

    return pl.pallas_call(
        body, name="s5_tail_fwd", out_shape=SDS((s, BR), MXU_DTYPE), grid=(s // tb,),
        in_specs=[_rows(tb, BR), _rows(tb, BR, CB_DU), _rows(tb, BR, CB_DG), _const((1, BR)), _const((BR, BR)),
                  _const((1, BR))],
        out_specs=_rows(tb, BR), compiler_params=_params(1))(ylin, proj, proj, d_skip, w_glu, b_glu)


def _s5_tail_bwd(dycat, ylin, proj, d_skip, w_glu, b_glu, tb):
    s = proj.shape[0]

    def body(dy, yl, u, dg, dk, w_ref, b_ref, dyl_ref, dus_ref, ddg_ref, g_ref, dt_ref, ddk_ref, dbg_ref):
        _init_acc(ddk_ref, dbg_ref)
        g, gelu_vjp = jax.vjp(jax.nn.gelu, yl[...] + dk[...] * u[...])
        gb = g.astype(MXU_DTYPE)
        sg = jax.nn.sigmoid(jnp.dot(gb, w_ref[...], preferred_element_type=F32) + b_ref[...])
        dz = dy[...] * _silu(dg[...])
        ddg_ref[...] = dy[...] * g * sg * _dsilu(dg[...])
        dt = dz * g * sg * (1.0 - sg)
        dtb = dt.astype(MXU_DTYPE)
        dgel = dz * sg + lax.dot_general(dtb, w_ref[...], (((1,), (1,)), ((), ())), preferred_element_type=F32)
        dyv, = gelu_vjp(dgel)
        dyl_ref[...] = dyv
        dus_ref[...] = dyv * dk[...]
        g_ref[...] = gb
        dt_ref[...] = dtb
        ddk_ref[...] += _colsum(dyv * u[...])
        dbg_ref[...] += _colsum(dt)

    big, half, vec = SDS((s, BR), F32), SDS((s, BR), MXU_DTYPE), SDS((1, BR), F32)
    return pl.pallas_call(
        body, name="s5_tail_bwd", out_shape=(big, big, big, half, half, vec, vec), grid=(s // tb,),
        in_specs=[_rows(tb, BR, 3), _rows(tb, BR), _rows(tb, BR, CB_DU), _rows(tb, BR, CB_DG), _const((1, BR)),
                  _const((BR, BR)), _const((1, BR))],
        out_specs=(_rows(tb, BR),) * 5 + (_const((1, BR)), _const((1, BR))), compiler_params=_params(1),
    )(dycat, ylin, proj, proj, d_skip, w_glu, b_glu)


def _s5_da(lmb, x, tb):
    s, ch2 = x.shape
    ch = ch2 // 2

    def body(l_ref, x_ref, xp_ref, o_ref):
        _init_acc(o_ref)
        has_prev = (pl.program_id(0) > 0).astype(F32)
        xprev = _shift_down(x_ref[...], xp_ref[...] * has_prev, 1)
        lr, li, xr, xi = l_ref[:, 0:ch], l_ref[:, ch:ch2], xprev[:, 0:ch], xprev[:, ch:ch2]
        o_ref[:, 0:ch] += _colsum(lr * xr + li * xi)
        o_ref[:, ch:ch2] += _colsum(li * xr - lr * xi)

    return pl.pallas_call(body, name="s5_da", out_shape=SDS((1, ch2), F32), grid=(s // tb,),
                          in_specs=[_rows(tb, ch2), _rows(tb, ch2), _prev8(tb, ch2)], out_specs=_const((1, ch2)),
                          compiler_params=_params(1))(lmb, x, x)


def _assemble_dproj(da, dqkv, dbg, dcx, dcg, du, dus, ddg, tb):
    s = da.shape[0]

    def body(da_ref, q0, q1, q2, k0, k1, k2, v0, v1, v2, dbg_ref, dcx_ref, dcg_ref, du_ref, dus_ref, ddg_ref, o_ref):
        o_ref[:, 0:4 * BR] = da_ref[...]
        for j, parts in enumerate(((q0, q1, q2), (k0, k1, k2), (v0, v1, v2))):
            o_ref[:, (4 + j) * BR:(5 + j) * BR] = (parts[0][...] + parts[1][...] + parts[2][...]).astype(MXU_DTYPE)
        o_ref[:, 7 * BR:8 * BR] = dbg_ref[...].astype(MXU_DTYPE)
        o_ref[:, 8 * BR:9 * BR] = dcx_ref[...].astype(MXU_DTYPE)
        o_ref[:, 9 * BR:10 * BR] = dcg_ref[...].astype(MXU_DTYPE)
        o_ref[:, 10 * BR:11 * BR] = (du_ref[...] + dus_ref[...]).astype(MXU_DTYPE)
        o_ref[:, 11 * BR:12 * BR] = ddg_ref[...].astype(MXU_DTYPE)

    flat = [t for grp in dqkv for t in grp]
    return pl.pallas_call(
        body, name="assemble_dproj", out_shape=SDS((s, N_IN), MXU_DTYPE), grid=(s // tb,),
        in_specs=[_rows(tb, 4 * BR)] + [_rows(tb, BR)] * 15, out_specs=_rows(tb, N_IN),
        compiler_params=_params(1))(da, *flat, dbg, dcx, dcg, du, dus, ddg)


def _sum_leading(xs, tr, name):
    n, r, c = xs[0].shape
    nl = len(xs)
    tr = min(tr, r)
    nr = r // tr
    assert r % tr == 0, (name, r, tr)

    def body(*refs):
        i = pl.program_id(0)
        for l in range(nl):
            @pl.when((i >= l * nr) & (i < (l + 1) * nr))
            def _():
                acc = refs[l * n][...].astype(F32)
                for ref in refs[l * n + 1:(l + 1) * n]:
                    acc = acc + ref[...].astype(F32)
                refs[nl * n][...] = acc

    specs = [pl.BlockSpec((None, tr, c), functools.partial(lambda i, k, l: (k, jnp.clip(i - l * nr, 0, nr - 1), 0), k=k, l=l))
             for l in range(nl) for k in range(n)]
    return pl.pallas_call(body, name=name, out_shape=SDS((nl * r, c), F32), grid=(nl * nr,), in_specs=specs,
                          out_specs=pl.BlockSpec((tr, c), lambda i: (i, 0)),
                          compiler_params=_params(1))(*[x for x in xs for _ in range(n)])


def _adamw(w, g_parts, m, v, tr, name):
    r, c = w.shape
    tr = min(tr, r)
    n = len(g_parts)
    assert r % tr == 0, (name, r, tr)

    def body(*refs):
        w_ref, m_ref, v_ref = refs[0], refs[1 + n], refs[2 + n]
        g_ref, d_ref, nm_ref, nv_ref = refs[3 + n:]
        g = refs[1][...]
        for ref in refs[2:1 + n]:
            g = g + ref[...]
        mm = ADAM_B1 * m_ref[...] + (1.0 - ADAM_B1) * g
        vv = ADAM_B2 * v_ref[...] + (1.0 - ADAM_B2) * jnp.square(g)
        m_hat = mm / (1.0 - ADAM_B1 ** ADAM_STEP)
        v_hat = vv / (1.0 - ADAM_B2 ** ADAM_STEP)
        g_ref[...] = g
        d_ref[...] = -ADAM_LR * (m_hat / (jnp.sqrt(v_hat) + ADAM_EPS) + ADAM_WD * w_ref[...])
        nm_ref[...] = mm
        nv_ref[...] = vv

    spec = pl.BlockSpec((tr, c), lambda i: (i, 0))
    return pl.pallas_call(body, name=name, out_shape=(SDS((r, c), F32),) * 4, grid=(r // tr,),
                          in_specs=[spec] * (3 + n), out_specs=(spec,) * 4,
                          compiler_params=_params(1))(w, *g_parts, m, v)


def _allgather8(block, name):
    m_per, n = block.shape

    def body(x_ref, out_ref, send_sems, recv_sems, local_sem):
        x, y, c = lax.axis_index("x"), lax.axis_index("y"), lax.axis_index("c")
        me, sibling = (x, y, c), (x, y, 1 - c)
        chips = [(1 - x, y), (x, 1 - y), (1 - x, 1 - y)]

        def rows(px, py, pc):
            return out_ref.at[pl.ds((4 * px + 2 * py + pc) * m_per, m_per), :]

        def copy(k, blk, to, src=None):
            return pltpu.make_async_remote_copy(
                src_ref=rows(*blk) if src is None else src, dst_ref=rows(*blk), send_sem=send_sems.at[k],
                recv_sem=recv_sems.at[k], device_id=to, device_id_type=MESH)

        mine = pltpu.make_async_copy(x_ref, rows(*me), local_sem)
        mine.start()
        first = [copy(0, me, sibling, src=x_ref)]
        first += [copy(1 + j, me, (*chip, c), src=x_ref) for j, chip in enumerate(chips)]
        for cp in first:
            cp.start()
        passed = [copy(4 + j, (*chip, c), sibling) for j, chip in enumerate(chips)]
        for j, chip in enumerate(chips):
            copy(1 + j, (*chip, c), me).wait_recv()
            passed[j].start()
        copy(0, sibling, me).wait_recv()
        for j, chip in enumerate(chips):
            copy(4 + j, (*chip, 1 - c), me).wait_recv()
        for cp in first + passed:
            cp.wait_send()
        mine.wait()

    return pl.pallas_call(
        body, name=name, out_shape=SDS((N_DEV * m_per, n), block.dtype),
        in_specs=[pl.BlockSpec(memory_space=pltpu.VMEM)], out_specs=pl.BlockSpec(memory_space=pltpu.VMEM),
        scratch_shapes=[pltpu.SemaphoreType.DMA((7,)), pltpu.SemaphoreType.DMA((7,)), pltpu.SemaphoreType.DMA],
        compiler_params=_params())(block)


class _Exchange:
    def __init__(self, items, out_shapes):
        self.items, self.out_shapes = list(items), tuple(out_shapes)
        self.arrays = [it[0] for it in self.items]
        n = len(self.items)
        self.n_in, self.n_out = n, len(self.out_shapes)
        self.scratch = [pltpu.SemaphoreType.DMA((n * N_CHIPS,)), pltpu.SemaphoreType.DMA((n * N_CHIPS,)),
                        pltpu.SemaphoreType.DMA((n,))]

    def _copies(self, ins, outs, sems, m):
        send_sems, recv_sems, local_sems = sems
        c = lax.axis_index("c")
        others = [j for j in range(N_CHIPS) if j != m]

        def remote(a, src, dst, to, from_):
            return pltpu.make_async_remote_copy(
                src_ref=src, dst_ref=dst, send_sem=send_sems.at[a * N_CHIPS + to],
                recv_sem=recv_sems.at[a * N_CHIPS + from_], device_id=(to // 2, to % 2, c), device_id_type=MESH)

        local, sends, recvs = [], [], []
        for a, (_, oi, src_of, dst_of) in enumerate(self.items):
            local.append(pltpu.make_async_copy(src_of(ins[a], m), dst_of(outs[oi], m), local_sems.at[a]))
            for j in others:
                sends.append(remote(a, src_of(ins[a], j), dst_of(outs[oi], m), j, m))
                recvs.append(remote(a, src_of(ins[a], m), dst_of(outs[oi], j), j, j))
        return local, sends, recvs

    def _on_my_chip(self, fn):
        chip = 2 * lax.axis_index("x") + lax.axis_index("y")
        for m in range(N_CHIPS):
            pl.when(chip == m)(functools.partial(fn, m))

    def start(self, ins, outs, sems):
        def go(m):
            local, sends, _ = self._copies(ins, outs, sems, m)
            for cp in local + sends:
                cp.start()
        self._on_my_chip(go)

    def wait(self, ins, outs, sems):
        def go(m):
            local, sends, recvs = self._copies(ins, outs, sems, m)
            for cp in recvs:
                cp.wait_recv()
            for cp in sends:
                cp.wait_send()
            for cp in local:
                cp.wait()
        self._on_my_chip(go)


def _half_rows(ref, cc):
    h = ref.shape[-2] // 2
    return ref.at[(slice(None),) * (len(ref.shape) - 2) + (pl.ds(cc * h, h), slice(None))]


class _Gather:
    def __init__(self, items, out_shapes):
        self.items, self.out_shapes = list(items), tuple(out_shapes)
        self.arrays = [it[0] for it in self.items]
        n = len(self.items)
        self.n_in, self.n_out = n, len(self.out_shapes)
        self.scratch = [pltpu.SemaphoreType.DMA((n * N_CHIPS,)) for _ in range(4)] + [pltpu.SemaphoreType.DMA((n,))]

    def _copies(self, ins, outs, sems, m, cc):
        ici_send, ici_recv, d2d_send, d2d_recv, local_sems = sems
        others = [j for j in range(N_CHIPS) if j != m]
        local, sends, arrivals, passed_on, from_sibling = [], [], [], [], []
        for a, (_, oi, src_of, dst_of) in enumerate(self.items):
            src, out = src_of(ins[a]), outs[oi]
            local.append(pltpu.make_async_copy(src, dst_of(out, m), local_sems.at[a]))
            for j in others:
                k = a * N_CHIPS + j
                mine_there = _half_rows(dst_of(out, m), cc)
                theirs_here = _half_rows(dst_of(out, j), cc)
                sends.append(pltpu.make_async_remote_copy(
                    src_ref=_half_rows(src, cc), dst_ref=mine_there, send_sem=ici_send.at[k],
                    recv_sem=ici_recv.at[a * N_CHIPS + m], device_id=(j // 2, j % 2, cc), device_id_type=MESH))
                arrivals.append(pltpu.make_async_remote_copy(
                    src_ref=_half_rows(src, cc), dst_ref=theirs_here, send_sem=ici_send.at[k], recv_sem=ici_recv.at[k],
                    device_id=(j // 2, j % 2, cc), device_id_type=MESH))
                passed_on.append(pltpu.make_async_remote_copy(
                    src_ref=theirs_here, dst_ref=theirs_here, send_sem=d2d_send.at[k], recv_sem=d2d_recv.at[k],
                    device_id=(m // 2, m % 2, 1 - cc), device_id_type=MESH))
                other_half = _half_rows(dst_of(out, j), 1 - cc)
                from_sibling.append(pltpu.make_async_remote_copy(
                    src_ref=other_half, dst_ref=other_half, send_sem=d2d_send.at[k], recv_sem=d2d_recv.at[k],
                    device_id=(m // 2, m % 2, 1 - cc), device_id_type=MESH))
        return local, sends, arrivals, passed_on, from_sibling

    def _on_my_core(self, fn):
        chip = 2 * lax.axis_index("x") + lax.axis_index("y")
        c = lax.axis_index("c")
        for m in range(N_CHIPS):
            for cc in range(2):
                pl.when((chip == m) & (c == cc))(functools.partial(fn, m, cc))

    def start(self, ins, outs, sems):
        def go(m, cc):
            local, sends, _, _, _ = self._copies(ins, outs, sems, m, cc)
            for cp in local + sends:
                cp.start()
        self._on_my_core(go)

    def wait(self, ins, outs, sems):
        def go(m, cc):
            local, sends, arrivals, passed_on, from_sibling = self._copies(ins, outs, sems, m, cc)
            for arrived, onward in zip(arrivals, passed_on):
                arrived.wait_recv()
                onward.start()
            for cp in from_sibling:
                cp.wait_recv()
            for cp in sends + passed_on:
                cp.wait_send()
            for cp in local:
                cp.wait()
        self._on_my_core(go)


def _run_exchange(ex, name):
    def body(*refs):
        ins, outs, sems = refs[:ex.n_in], refs[ex.n_in:ex.n_in + ex.n_out], refs[ex.n_in + ex.n_out:]
        ex.start(ins, outs, sems)
        ex.wait(ins, outs, sems)

    return pl.pallas_call(
        body, name=name, out_shape=ex.out_shapes, in_specs=[ANY] * ex.n_in, out_specs=(ANY,) * ex.n_out,
        scratch_shapes=ex.scratch, compiler_params=_params())(*ex.arrays)


def _sibling_swap(arrays, name):
    n = len(arrays)

    def body(*refs):
        ins, outs = refs[:n], refs[n:2 * n]
        send_sems, recv_sems = refs[2 * n:]
        peer = (lax.axis_index("x"), lax.axis_index("y"), 1 - lax.axis_index("c"))
        cps = [pltpu.make_async_remote_copy(src_ref=ins[a], dst_ref=outs[a], send_sem=send_sems.at[a],
                                            recv_sem=recv_sems.at[a], device_id=peer, device_id_type=MESH)
               for a in range(n)]
        for cp in cps:
            cp.start()
        for cp in cps:
            cp.wait()

    return pl.pallas_call(
        body, name=name, out_shape=tuple(SDS(a.shape, a.dtype) for a in arrays), in_specs=[ANY] * n,
        out_specs=(ANY,) * n, scratch_shapes=[pltpu.SemaphoreType.DMA((n,)), pltpu.SemaphoreType.DMA((n,))],
        compiler_params=_params())(*arrays)


def _block_diag(w):
    h, n, m = w.shape
    eye = jnp.eye(h, dtype=w.dtype)
    return (w[:, :, None, :] * eye[:, None, :, None]).reshape(h * n, h * m)


def _diag_blocks(d, h, col0=0, ncols=None, stacked=1):
    ncols = d.shape[1] - col0 if ncols is None else ncols
    n, m = d.shape[0] // (h * stacked), ncols // h
    lanes = 128
    assert m <= lanes and lanes % m == 0 and col0 % lanes == 0

    def body(d_ref, o_ref):
        for gi in range(h * stacked):
            c = col0 + (gi % h) * m
            chunk = d_ref[gi * n:(gi + 1) * n, c // lanes * lanes:c // lanes * lanes + lanes]
            o_ref[gi * n:(gi + 1) * n, :] = chunk[:, c % lanes:c % lanes + m]

    out = pl.pallas_call(body, name="diag_blocks", out_shape=SDS((stacked * h * n, m), d.dtype),
                         compiler_params=_params())(d)
    return out.reshape(stacked * h, n, m)


S5_CHUNKS = 4
S5_PER = S5_GROUPS // S5_CHUNKS
CH_W = S5_PER * S5_CH
ST_W = S5_PER * S5_STATE


def _bd_stack(mats):
    _, _, n, m = mats.shape
    eye = jnp.eye(S5_PER, dtype=mats.dtype)
    t = mats.reshape(2, S5_CHUNKS, S5_PER, n, m)
    bd = t[:, :, :, :, None, :] * eye[None, None, :, None, :, None]
    return bd.reshape(2 * S5_CHUNKS, S5_PER * n, S5_PER * m).astype(MXU_DTYPE)


def _bd_expand(a, a_col0, w8, name):
    s = a.shape[0]
    tm = min(1024, s)
    c0 = a_col0 // CH_W

    def body(a_ref, w_ref, o_ref):
        o_ref[...] = jnp.dot(a_ref[...].astype(MXU_DTYPE), w_ref[...], preferred_element_type=F32)

    return pl.pallas_call(
        body, name=name, out_shape=SDS((s, 2 * S5_N), F32), grid=(s // tm, 2 * S5_CHUNKS),
        in_specs=[pl.BlockSpec((tm, CH_W), lambda i, b: (i, c0 + b % S5_CHUNKS)),
                  pl.BlockSpec((None, CH_W, ST_W), lambda i, b: (b, 0, 0))],
        out_specs=pl.BlockSpec((tm, ST_W), lambda i, b: (i, b)), compiler_params=_params(2))(a, w8)


def _bd_reduce(x, w8, name):
    s = x.shape[0]
    tm = min(1024, s)

    def body(x_ref, w_ref, o_ref, acc):
        p = pl.program_id(2)

        @pl.when(p == 0)
        def _():
            acc[...] = jnp.zeros_like(acc)

        acc[...] += jnp.dot(x_ref[...].astype(MXU_DTYPE), w_ref[...], preferred_element_type=F32)

        @pl.when(p == 1)
        def _():
            o_ref[...] = acc[...]

    return pl.pallas_call(
        body, name=name, out_shape=SDS((s, BR), F32), grid=(s // tm, S5_CHUNKS, 2),
        in_specs=[pl.BlockSpec((tm, ST_W), lambda i, q, p: (i, p * S5_CHUNKS + q)),
                  pl.BlockSpec((None, ST_W, CH_W), lambda i, q, p: (p * S5_CHUNKS + q, 0, 0))],
        out_specs=pl.BlockSpec((tm, CH_W), lambda i, q, p: (i, q)),
        scratch_shapes=[pltpu.VMEM((tm, CH_W), F32)], compiler_params=_params(3))(x, w8)


def _bd_wgrad(a, a_col0, x, name):
    s = a.shape[0]
    tk = min(1024, s)
    nk = s // tk
    c0 = a_col0 // CH_W

    def body(a_ref, x_ref, o_ref, acc):
        k = pl.program_id(1)

        @pl.when(k == 0)
        def _():
            acc[...] = jnp.zeros_like(acc)

        acc[...] += lax.dot_general(a_ref[...].astype(MXU_DTYPE), x_ref[...].astype(MXU_DTYPE),
                                    (((0,), (0,)), ((), ())), preferred_element_type=F32)

        @pl.when(k == nk - 1)
        def _():
            o_ref[...] = acc[...]

    return pl.pallas_call(
        body, name=name, out_shape=SDS((2 * S5_CHUNKS * CH_W, ST_W), F32), grid=(2 * S5_CHUNKS, nk),
        in_specs=[pl.BlockSpec((tk, CH_W), lambda b, k: (k, c0 + b % S5_CHUNKS)),
                  pl.BlockSpec((tk, ST_W), lambda b, k: (k, b))],
        out_specs=pl.BlockSpec((CH_W, ST_W), lambda b, k: (b, 0)),
        scratch_shapes=[pltpu.VMEM((CH_W, ST_W), F32)], compiler_params=_params(2))(a, x)


def _tiles(s):
    return dict(tb=min(512, s), tln=min(256, s), tscan=min(256, s))


def _layer_weights(p, l):
    pad8 = lambda w: jnp.pad(w, ((0, 8 - w.shape[0]), (0, 0)))
    return dict(
        conv_a=pad8(p["conv_a"][l]), conv_c=pad8(p["conv_c"][l]), conv_c_b=p["conv_c_b"][l][None],
        w_cat=jnp.concatenate([_block_diag(p["lru_wa"][l]), _block_diag(p["lru_wx"][l])], axis=1).astype(MXU_DTYPE),
        b_cat=jnp.concatenate([p["lru_ba"][l], p["lru_bx"][l]])[None], lam=p["lru_lambda"][l][None],
        lam_re=p["s5_lam_re"][l], lam_im=p["s5_lam_im"][l], log_dt=p["s5_log_dt"][l][:, None],
        b_re=p["s5_b_re"][l].reshape(S5_N, S5_CH), b_im=p["s5_b_im"][l].reshape(S5_N, S5_CH),
        c_re=p["s5_c_re"][l], c_im=p["s5_c_im"][l], d_skip=p["s5_d"][l][None], b_glu=p["s5_b_glu"][l][None],
        ln_g=p["ln_g"][l][None], ln_b=p["ln_b"][l][None])


def _s5_matrices(lw):
    ab_re, ab_im, f_re, f_im = _s5_disc_fwd(lw["lam_re"], lw["lam_im"], lw["log_dt"])
    f_re, f_im = f_re.reshape(S5_N, 1), f_im.reshape(S5_N, 1)
    bb_re, bb_im = _s5_bbar_fwd(f_re, f_im, lw["b_re"], lw["b_im"])
    bb = jnp.stack([bb_re, bb_im]).reshape(2, S5_GROUPS, S5_STATE, S5_CH)
    cc = jnp.stack([lw["c_re"], -lw["c_im"]])
    a_row = jnp.concatenate([ab_re.reshape(1, S5_N), ab_im.reshape(1, S5_N)], axis=1)
    return dict(f_re=f_re, f_im=f_im, a_row=a_row, w_bu=_bd_stack(jnp.swapaxes(bb, 2, 3)), w_du=_bd_stack(bb),
                w_cx=_bd_stack(jnp.swapaxes(cc, 2, 3)), w_dx=_bd_stack(cc))


def _mm_hooked(hook, *args, **kw):
    if hook is None:
        return _mm(*args, **kw)
    out = _mm(*args, carry=hook[0], **kw)
    hook[1](out[1:])
    return out[0]


def _layer_fwd(x, ada, w_in, get_rest, lw, s5m, bias_tabs, hooks=None):
    s = x.shape[0]
    t = _tiles(s)
    tb = t["tb"]
    shift, scale, gate = ada
    hooks = hooks or {}
    h = _modulate(x, scale, shift, tb)
    proj = _mm_hooked(hooks.get("in_proj"), h, w_in, name="in_proj", tm=1024, tn=1024, tk=D_MODEL)
    w_out, w_glu = get_rest()
    y_a = _branch_a_fwd(proj, lw["conv_a"], tb)
    os_, lses = [], []
    for g, (_, dil) in enumerate(DILATIONS):
        o, lse = _attn_fwd(proj, bias_tabs[g], dil)
        os_.append(o)
        lses.append(lse)
    y_b = _attn_combine(os_, lses, proj, tb)
    lru_a, lru_b = _lru_gates_fwd(proj, lw["conv_c"], lw["conv_c_b"], lw["w_cat"], lw["b_cat"], lw["lam"], tb)
    lru_h = _scan_real(lru_a, lru_b, reverse=False, tb=tb, name="lru_scan")
    y_c = _gate_out(lru_h, proj, CB_CG, tb, "lru_out")
    bu = _bd_expand(proj, CB_DU * BR, s5m["w_bu"], "s5_bu")
    s5_x = _scan_cplx(bu, s5m["a_row"], reverse=False, tb=t["tscan"], name="s5_scan")
    ylin = _bd_reduce(s5_x, s5m["w_cx"], "s5_cx")
    y_d = _s5_tail_fwd(ylin, proj, lw["d_skip"], w_glu, lw["b_glu"], tb)
    ycat = jnp.concatenate([y_a, y_b, y_c, y_d], axis=1)
    x_next, xhat, y, rstd = _out_ln(ycat, w_out, x, gate, lw["ln_g"], lw["ln_b"], t["tln"])
    saved = dict(x=x, h=h, proj=proj, os=os_, lses=lses, lru_a=lru_a, lru_h=lru_h, s5_x=s5_x, ylin=ylin, ycat=ycat,
                 xhat=xhat, y=y, rstd=rstd)
    return x_next, saved


def _layer_bwd(dxn, sv, ada, w_in, w_out, w_glu, lw, s5m, bias_tabs, head_ones, hooks=None):
    s = dxn.shape[0]
    t = _tiles(s)
    tb = t["tb"]
    shift, scale, gate = ada
    proj = sv["proj"]
    g = {}
    hook = lambda name: hooks[name](g) if hooks and name in hooks else None
    dyb, dxa, g["ln_g"], g["ln_b"], dgate = _ln_bwd(dxn, sv["xhat"], sv["y"], sv["rstd"], lw["ln_g"], gate, t["tln"])
    g["w_out"] = _mm_hooked(hook("dw_out"), sv["ycat"], dyb, name="dw_out", ta=True, out_dtype=WIRE_DTYPE,
                            tm=1024, tn=1024, tk=1024)
    dycat = _mm(dyb, w_out, name="dycat", tb=True, tm=1024, tn=1024, tk=D_MODEL)
    da, dconv_a = _branch_a_bwd(dycat, proj, lw["conv_a"], tb)
    g["conv_a"] = dconv_a[0:3]
    pre = _attn_bwd_pre(dycat, sv["os"], sv["lses"], proj, head_ones, tb)
    dbg, dos, dms = pre[0], pre[1:4], pre[4:7]
    dqkv, dbias = [], []
    for gi, (_, dil) in enumerate(DILATIONS):
        hk = hook(f"attn_bwd_d{dil}")
        dq, dk, dv, dbi, *got = _attn_bwd(proj, dos[gi], sv["lses"][gi], dms[gi], bias_tabs[gi], dil,
                                          carry=hk and hk[0])
        if hk:
            hk[1](got)
        dqkv.append((dq, dk, dv))
        dbias.append(dbi)
    dqkv = list(zip(*dqkv))
    dh, dcg = _gate_out_bwd(dycat, 2, sv["lru_h"], proj, CB_CG, tb, "lru_out_bwd")
    lmb = _scan_real(sv["lru_a"], dh, reverse=True, tb=tb, name="lru_scan_bwd")
    dxc, dpre, xcb, dbcat, dlam = _lru_gates_bwd(proj, lmb, sv["lru_h"], lw["conv_c"], lw["conv_c_b"], lw["w_cat"],
                                                  lw["b_cat"], lw["lam"], tb)
    dwcat = _mm(xcb, dpre, name="dw_lru", ta=True, tn=1024)
    g["lru_wa"] = _diag_blocks(dwcat, LRU_HEADS, 0, BR)
    g["lru_wx"] = _diag_blocks(dwcat, LRU_HEADS, BR, BR)
    g["lru_ba"], g["lru_bx"], g["lru_lambda"] = dbcat[0, 0:BR], dbcat[0, BR:2 * BR], dlam[0]
    dcx, dconv_c, dccb = _conv_c_bwd(dxc, proj, lw["conv_c"], tb)
    g["conv_c"], g["conv_c_b"] = dconv_c[0:4], dccb[0]
    dyl, dus, ddg, gb, dtb, ddk, dbglu = _s5_tail_bwd(dycat, sv["ylin"], proj, lw["d_skip"], w_glu, lw["b_glu"], tb)
    g["s5_d"], g["s5_b_glu"] = ddk[0], dbglu[0]
    g["s5_w_glu"] = _mm(gb, dtb, name="dw_glu", ta=True, out_dtype=WIRE_DTYPE)
    dxd = _bd_expand(dyl, 0, s5m["w_dx"], "s5_dx")
    s5_l = _scan_cplx(dxd, s5m["a_row"], reverse=True, tb=t["tscan"], name="s5_scan_bwd")
    dab = _s5_da(s5_l, sv["s5_x"], t["tscan"])
    du = _bd_reduce(s5_l, s5m["w_du"], "s5_du")
    per_group = lambda d8: _diag_blocks(d8, S5_PER, stacked=2 * S5_CHUNKS).reshape(2, S5_GROUPS, S5_CH, S5_STATE)
    dbb = per_group(_bd_wgrad(proj, CB_DU * BR, s5_l, "dw_s5_b"))
    dcc = per_group(_bd_wgrad(dyl, 0, sv["s5_x"], "dw_s5_c"))
    from_bd = lambda half: jnp.swapaxes(dbb[half], 1, 2).reshape(S5_N, S5_CH)
    df_re, df_im, db_re, db_im = _s5_bbar_bwd(s5m["f_re"], s5m["f_im"], lw["b_re"], lw["b_im"],
                                              from_bd(0), from_bd(1))
    shp = (S5_GROUPS, S5_STATE)
    g["s5_lam_re"], g["s5_lam_im"], dlog_dt = _s5_disc_bwd(
        lw["lam_re"], lw["lam_im"], lw["log_dt"],
        (dab[:, 0:S5_N].reshape(shp), dab[:, S5_N:].reshape(shp), df_re.reshape(shp), df_im.reshape(shp)))
    g["s5_log_dt"] = dlog_dt[:, 0]
    g["s5_b_re"] = db_re.reshape(S5_GROUPS, S5_STATE, S5_CH)
    g["s5_b_im"] = db_im.reshape(S5_GROUPS, S5_STATE, S5_CH)
    g["s5_c_re"], g["s5_c_im"] = dcc[0], -dcc[1]
    dproj = _assemble_dproj(da, dqkv, dbg, dcx, dcg, du, dus, ddg, tb)
    g["w_in"] = _mm_hooked(hook("dw_in"), sv["h"], dproj, name="dw_in", ta=True, out_dtype=WIRE_DTYPE,
                           tm=1024, tn=1536, tk=1024)
    dhm = _mm_hooked(hook("dh"), dproj, w_in, name="dh", tb=True, tm=1024, tn=1024, tk=1536)
    dx, dshift, dscale = _mod_bwd(dhm, dxa, sv["x"], scale, tb)
    g["ada"] = jnp.concatenate([dshift[0], dscale[0], dgate[0]])
    return dx, g, dbias


SMALL = ("rel_bias", "conv_a", "conv_c", "conv_c_b", "lru_wa", "lru_ba", "lru_wx", "lru_bx", "lru_lambda",
         "s5_lam_re", "s5_lam_im", "s5_log_dt", "s5_b_re", "s5_b_im", "s5_c_re", "s5_c_im", "s5_d", "s5_b_glu",
         "ln_g", "ln_b")
PER_LAYER_SMALL = SMALL[1:]


def _local_step(x, target, ada_rows, w_in, w_out, w_glu, p, comm=None):
    if comm is None:
        get_w_in = lambda l: w_in[l]
        get_rest = lambda l: (w_out[l], w_glu[l])
        fwd_hooks = bwd_hooks = lambda *_: None
    else:
        get_w_in, get_rest, fwd_hooks, bwd_hooks = comm.w_in, comm.rest, comm.fwd_hooks, comm.bwd_hooks
    s = x.shape[0]
    buckets = _bucket_maps()
    bias_tabs = _bias_tables(p["rel_bias"], buckets)
    head_ones = _block_diag(jnp.ones((ATT_HEADS, HEAD_DIM, HEAD_DIM), MXU_DTYPE))
    lws = [_layer_weights(p, l) for l in range(DEPTH)]
    s5ms = [_s5_matrices(lw) for lw in lws]
    adas = [tuple(ada_rows[l, k * D_MODEL:(k + 1) * D_MODEL][None] for k in range(3)) for l in range(DEPTH)]
    saved = []
    for l in range(DEPTH):
        x, sv = _layer_fwd(x, adas[l], get_w_in(l), functools.partial(get_rest, l), lws[l], s5ms[l], bias_tabs,
                           fwd_hooks(l))
        saved.append(sv)
    loss, dx = _loss_head(x, target, _tiles(s)["tb"])
    grads = [None] * DEPTH
    dbias_sum = []
    for l in reversed(range(DEPTH)):
        dx, grads[l], dbias = _layer_bwd(dx, saved[l], adas[l], get_w_in(l), *get_rest(l), lws[l], s5ms[l],
                                         bias_tabs, head_ones, bwd_hooks(l, grads))
        dbias_sum.append(jnp.stack(dbias))
    drel = _rel_bias_grad(jnp.stack(dbias_sum), buckets)[:, 0:ATT_HEADS]
    small = {n: jnp.stack([grads[l][n] for l in range(DEPTH)]) for n in PER_LAYER_SMALL + ("ada",)}
    small["rel_bias"] = drel
    big = {n: [grads[l][n] for l in range(DEPTH)] for n in ("w_in", "w_out", "s5_w_glu")}
    return loss, dx, big, small


PACK_ROWS = 256


def _pack(parts):
    flat = jnp.concatenate([t.reshape(-1).astype(F32) for t in parts])
    n = flat.shape[0]
    rows = -(-n // (PACK_ROWS * 128)) * PACK_ROWS
    return jnp.pad(flat, (0, rows * 128 - n)).reshape(rows, 128)


def _unpack(packed, shapes):
    flat = packed.reshape(packed.shape[:-2] + (-1,))
    out, off = [], 0
    for shp in shapes:
        size = math.prod(shp)
        out.append(flat[..., off:off + size].reshape(flat.shape[:-1] + tuple(shp)))
        off += size
    return out


def _take_cols(t, chip, width):
    return lax.dynamic_slice_in_dim(t, chip * width, width, axis=t.ndim - 1)


class _Comm:
    IN_W, OUT_R, GLU_R = N_IN // N_CHIPS, D_MODEL // N_CHIPS, BR // N_CHIPS

    def __init__(self, w_in_b, w_out_b, w_glu_b):
        assert DEPTH == 2
        self.shards = (w_in_b, w_out_b, w_glu_b)
        in_w = self.IN_W
        self.w_in_full = {0: _run_exchange(_Gather(
            [(w_in_b, 0, lambda ref: ref.at[0], lambda ref, j: ref.at[:, pl.ds(j * in_w, in_w)])],
            [SDS((D_MODEL, N_IN), WIRE_DTYPE)]), "gather_w_in0")[0]}
        self.w_out_full = self.w_glu_full = None
        self.recv = {}

    def w_in(self, l):
        return self.w_in_full[l]

    def rest(self, l):
        return self.w_out_full[l], self.w_glu_full[l]

    def fwd_hooks(self, l):
        if l != 0:
            return None
        w_in_b, w_out_b, w_glu_b = self.shards
        in_w, out_r, glu_r = self.IN_W, self.OUT_R, self.GLU_R
        whole = lambda ref: ref
        items = [(w_out_b, 0, whole, lambda ref, j: ref.at[:, pl.ds(j * out_r, out_r), :]),
                 (w_glu_b, 1, whole, lambda ref, j: ref.at[:, pl.ds(j * glu_r, glu_r), :]),
                 (w_in_b, 2, lambda ref: ref.at[1], lambda ref, j: ref.at[:, pl.ds(j * in_w, in_w)])]
        shapes = [SDS((DEPTH, D_MODEL, D_MODEL), WIRE_DTYPE), SDS((DEPTH, BR, BR), WIRE_DTYPE),
                  SDS((D_MODEL, N_IN), WIRE_DTYPE)]

        def done(got):
            self.w_out_full, self.w_glu_full, self.w_in_full[1] = got

        return {"in_proj": (_Gather(items, shapes), done)}

    def _scatter(self, parts):
        in_w, out_r, glu_r = self.IN_W, self.OUT_R, self.GLU_R
        cut = {"w_in": (lambda ref, j: ref.at[:, pl.ds(j * in_w, in_w)], (D_MODEL, in_w)),
               "w_out": (lambda ref, j: ref.at[pl.ds(j * out_r, out_r), :], (out_r, D_MODEL)),
               "s5_w_glu": (lambda ref, j: ref.at[pl.ds(j * glu_r, glu_r), :], (glu_r, BR))}
        items = [(arr, oi, cut[name][0], lambda ref, j: ref.at[j]) for oi, (name, _, arr) in enumerate(parts)]
        shapes = [SDS((N_CHIPS,) + cut[name][1], WIRE_DTYPE) for name, _, _ in parts]

        def done(got):
            for (name, l, _), arr in zip(parts, got):
                self.recv[name, l] = arr

        return _Exchange(items, shapes), done

    def bwd_hooks(self, l, grads):
        if l != 0:
            return None
        g1 = grads[1]
        return {"dw_out": lambda g: self._scatter([("w_out", 1, g1["w_out"]), ("s5_w_glu", 1, g1["s5_w_glu"])]),
                "attn_bwd_d16": lambda g: self._scatter([("w_in", 1, g1["w_in"])]),
                "dw_in": lambda g: self._scatter([("w_out", 0, g["w_out"]), ("s5_w_glu", 0, g["s5_w_glu"])]),
                "dh": lambda g: self._scatter([("w_in", 0, g["w_in"])])}


def kernel(x, c, rel_bias, w_ada, b_ada, w_in, conv_a, conv_c, conv_c_b, lru_wa, lru_ba, lru_wx, lru_bx, lru_lambda, s5_lam_re, s5_lam_im, s5_log_dt, s5_b_re, s5_b_im, s5_c_re, s5_c_im, s5_d, s5_w_glu, s5_b_glu, w_out, ln_g, ln_b, loss_target, m_rel_bias, m_w_ada, m_b_ada, m_w_in, m_conv_a, m_conv_c, m_conv_c_b, m_lru_wa, m_lru_ba, m_lru_wx, m_lru_bx, m_lru_lambda, m_s5_lam_re, m_s5_lam_im, m_s5_log_dt, m_s5_b_re, m_s5_b_im, m_s5_c_re, m_s5_c_im, m_s5_d, m_s5_w_glu, m_s5_b_glu, m_w_out, m_ln_g, m_ln_b, v_rel_bias, v_w_ada, v_b_ada, v_w_in, v_conv_a, v_conv_c, v_conv_c_b, v_lru_wa, v_lru_ba, v_lru_wx, v_lru_bx, v_lru_lambda, v_s5_lam_re, v_s5_lam_im, v_s5_log_dt, v_s5_b_re, v_s5_b_im, v_s5_c_re, v_s5_c_im, v_s5_d, v_s5_w_glu, v_s5_b_glu, v_w_out, v_ln_g, v_ln_b):
    args = dict(locals())
    names = ("rel_bias", "w_ada", "b_ada", "w_in", "conv_a", "conv_c", "conv_c_b", "lru_wa", "lru_ba", "lru_wx",
             "lru_bx", "lru_lambda", "s5_lam_re", "s5_lam_im", "s5_log_dt", "s5_b_re", "s5_b_im", "s5_c_re", "s5_c_im",
             "s5_d", "s5_w_glu", "s5_b_glu", "w_out", "ln_g", "ln_b")
    w = {n: args[n] for n in names}
    mom = {n: args["m_" + n] for n in names}
    var = {n: args["v_" + n] for n in names}
    chip = 2 * lax.axis_index("x") + lax.axis_index("y")
    me = 2 * chip + lax.axis_index("c")
    ada_w = 3 * D_MODEL // N_CHIPS
    in_w = N_IN // N_CHIPS
    out_r = D_MODEL // N_CHIPS
    glu_r = BR // N_CHIPS
    conv_w = BR // N_CHIPS

    comm = _Comm(w["w_in"].astype(WIRE_DTYPE), w["w_out"].astype(WIRE_DTYPE), w["s5_w_glu"].astype(WIRE_DTYPE))

    taps = jnp.concatenate([w["conv_a"].reshape(DEPTH * 3, conv_w), w["conv_c"].reshape(DEPTH * 4, conv_w)])
    first = jnp.concatenate([c, jnp.pad(taps, ((0, 1), (0, D_MODEL - conv_w)))])
    got = _allgather8(first, "gather_c_taps").reshape(N_CHIPS, 2, 16, D_MODEL)
    c_all = got[:, :, 0].reshape(N_DEV, D_MODEL)
    taps_all = jnp.transpose(got[:, 0, 1:1 + DEPTH * 7, 0:conv_w], (1, 0, 2)).reshape(DEPTH * 7, BR)
    conv_a_f = taps_all[0:DEPTH * 3].reshape(DEPTH, 3, BR)
    conv_c_f = taps_all[DEPTH * 3:].reshape(DEPTH, 4, BR)

    cond_all = _silu_rows(c_all)
    ada_part = jnp.stack([_mm(cond_all, w["w_ada"][l], name="ada_fwd", tk=D_MODEL, tn=512,
                              bias=_take_cols(w["b_ada"][l][None], chip, ada_w)) for l in range(DEPTH)])
    ada_all = _allgather8(ada_part.reshape(DEPTH * N_DEV, ada_w), "gather_ada")
    ada_all = ada_all.reshape(N_CHIPS, 2, DEPTH, N_DEV, ada_w)[:, 0]
    ada_rows = lax.dynamic_index_in_dim(ada_all, me, axis=2, keepdims=False)
    ada_rows = jnp.transpose(ada_rows, (1, 0, 2)).reshape(DEPTH, 3 * D_MODEL)

    p = dict(w)
    p["conv_a"], p["conv_c"] = conv_a_f, conv_c_f
    loss, dx, _, small = _local_step(x[0], loss_target[0], ada_rows, None, None, None, p, comm)

    sums = [_sum_leading([comm.recv[name, l] for l in range(DEPTH)], 256, "sum_chips")
            for name in ("w_in", "w_out", "s5_w_glu")]
    others = _sibling_swap(sums, "swap_cores")
    out = {}
    for name, mine, other in zip(("w_in", "w_out", "s5_w_glu"), sums, others):
        shp = w[name].shape
        flat = lambda t: t.reshape(-1, shp[-1])
        res = _adamw(flat(w[name]), [mine, other], flat(mom[name]), flat(var[name]), 128, "adamw_big")
        out[name] = [t.reshape(shp) for t in res]

    small_names = SMALL + ("ada",)
    small["loss"] = loss
    order = small_names + ("loss",)
    shapes = [small[n].shape for n in order]
    gathered = _allgather8(_pack([small[n] for n in order]), "gather_small")
    gathered = gathered.reshape(N_DEV, -1, 128)
    total = dict(zip(order, _unpack(_sum_leading([gathered], PACK_ROWS, "sum_devices"), shapes)))
    d_ada_all = _unpack(gathered, shapes)[order.index("ada")]
    g_small = {n: total[n] for n in SMALL}
    g_small["conv_a"] = _take_cols(total["conv_a"], chip, conv_w)
    g_small["conv_c"] = _take_cols(total["conv_c"], chip, conv_w)
    g_small["b_ada"] = total["ada"]
    g_w_ada = jnp.stack([_mm(cond_all, _take_cols(d_ada_all[:, l], chip, ada_w), name="dw_ada", ta=True, tn=ada_w)
                         for l in range(DEPTH)])
    upd_names = SMALL + ("b_ada",)
    upd_shapes = [w[n].shape for n in upd_names]
    res = _adamw(_pack([w[n] for n in upd_names]), [_pack([g_small[n] for n in upd_names])],
                 _pack([mom[n] for n in upd_names]), _pack([var[n] for n in upd_names]), PACK_ROWS, "adamw_small")
    for k, t in enumerate(res):
        for n, val in zip(upd_names, _unpack(t, upd_shapes)):
            out.setdefault(n, [None] * 4)[k] = val
    shp = w["w_ada"].shape
    flat = lambda t: t.reshape(-1, shp[-1])
    out["w_ada"] = [t.reshape(shp) for t in _adamw(flat(w["w_ada"]), [flat(g_w_ada)], flat(mom["w_ada"]),
                                                  flat(var["w_ada"]), 128, "adamw_ada")]
    return (total["loss"].reshape(()), dx[None]) + tuple(out[n][k] for k in range(4) for n in names)
```

```python
import functools
import math

import jax
import jax.numpy as jnp
from jax import lax
from jax.experimental import pallas as pl
from jax.experimental.pallas import tpu as pltpu

F32 = jnp.float32
MXU_DTYPE = jnp.bfloat16
WIRE_DTYPE = jnp.bfloat16
SDS = jax.ShapeDtypeStruct
MESH = pl.DeviceIdType.MESH
ANY = pl.BlockSpec(memory_space=pl.ANY)
VMEM_LIMIT = 48 * 1024 * 1024

D_MODEL = 2048
DEPTH = 2
BR = 512
ATT_HEADS = 8
HEAD_DIM = 64
DILATIONS = ((128, 1), (512, 4), (2048, 16))
BLK = 128
REL_BUCKETS = 32
REL_MAX_DIST = 2048
LRU_HEADS = 8
LRU_C = 8.0
S5_CH = 16
S5_GROUPS = 32
S5_STATE = 64
S5_N = S5_GROUPS * S5_STATE
N_IN = 12 * BR
ALPHA = (2 * DEPTH) ** 0.25
LN_EPS = 1e-5
NEG = -1e30
ADAM_LR, ADAM_B1, ADAM_B2, ADAM_EPS, ADAM_WD, ADAM_STEP = 0.001, 0.9, 0.999, 1e-08, 0.01, 10
CB_AB, CB_AC, CB_AX, CB_AG, CB_Q, CB_K, CB_V, CB_BG, CB_CX, CB_CG, CB_DU, CB_DG = range(12)
N_CHIPS = 4
N_DEV = 8


def _params(n_axes=0):
    kw = {"dimension_semantics": ("arbitrary",) * n_axes} if n_axes else {}
    return pltpu.CompilerParams(vmem_limit_bytes=VMEM_LIMIT, **kw)


def _rows(tb, w, cb=0):
    return pl.BlockSpec((tb, w), lambda i: (i, cb))


def _prev8(tb, w, cb=0):
    return pl.BlockSpec((8, w), lambda i: (jnp.maximum(i * (tb // 8) - 1, 0), cb))


def _next8(tb, w, n_rows, cb=0):
    return pl.BlockSpec((8, w), lambda i: (jnp.minimum((i + 1) * (tb // 8), n_rows // 8 - 1), cb))


def _const(shape):
    return pl.BlockSpec(shape, lambda *_: (0,) * len(shape))


def _silu(x):
    return x * jax.nn.sigmoid(x)


def _dsilu(x):
    s = jax.nn.sigmoid(x)
    return s * (1.0 + x * (1.0 - s))


def _shift_down(cur, prev8, j):
    rolled = pltpu.roll(cur, j, 0)
    row = lax.broadcasted_iota(jnp.int32, (8, cur.shape[1]), 0)
    first = jnp.where(row < j, pltpu.roll(prev8, j, 0), rolled[0:8])
    return jnp.concatenate([first, rolled[8:]], axis=0)


def _shift_up(cur, next8, j):
    t = cur.shape[0]
    rolled = pltpu.roll(cur, t - j, 0)
    row = lax.broadcasted_iota(jnp.int32, (8, cur.shape[1]), 0)
    last = jnp.where(row >= 8 - j, pltpu.roll(next8, 8 - j, 0), rolled[t - 8:t])
    return jnp.concatenate([rolled[:t - 8], last], axis=0)


def _colsum(x):
    return jnp.sum(x, axis=0, keepdims=True)


def _init_acc(*refs):
    @pl.when(pl.program_id(0) == 0)
    def _():
        for r in refs:
            r[...] = jnp.zeros_like(r)


def _call(body, *, name, out_shape, grid, in_specs, out_specs, scratch_shapes, args, carry=None):
    out_shape, out_specs, in_specs = tuple(out_shape), tuple(out_specs), list(in_specs)
    scratch_shapes = list(scratch_shapes)
    if carry is None:
        return pl.pallas_call(body, name=name, out_shape=out_shape, grid=grid, in_specs=in_specs, out_specs=out_specs,
                              scratch_shapes=scratch_shapes, compiler_params=_params(len(grid)))(*args)
    n_in, n_out, n_scr = len(in_specs), len(out_shape), len(scratch_shapes)

    def wrapped(*refs):
        ins, refs = refs[:n_in], refs[n_in:]
        x_ins, refs = refs[:carry.n_in], refs[carry.n_in:]
        outs, refs = refs[:n_out], refs[n_out:]
        x_outs, refs = refs[:carry.n_out], refs[carry.n_out:]
        scr, x_sems = refs[:n_scr], refs[n_scr:]
        at = [pl.program_id(d) for d in range(len(grid))]
        first = functools.reduce(lambda p, q: p & q, [i == 0 for i in at])
        last = functools.reduce(lambda p, q: p & q, [i == g - 1 for i, g in zip(at, grid)])
        pl.when(first)(lambda: carry.start(x_ins, x_outs, x_sems))
        body(*ins, *outs, *scr)
        pl.when(last)(lambda: carry.wait(x_ins, x_outs, x_sems))

    return pl.pallas_call(
        wrapped, name=name, out_shape=out_shape + carry.out_shapes, grid=grid, in_specs=in_specs + [ANY] * carry.n_in,
        out_specs=out_specs + (ANY,) * carry.n_out, scratch_shapes=scratch_shapes + carry.scratch,
        compiler_params=_params(len(grid)))(*args, *carry.arrays)


def _mm(a, b, *, name, ta=False, tb=False, out_dtype=F32, tm=512, tn=512, tk=512, a_col0=0, a_ncols=None, bias=None,
        carry=None):
    a_ncols = a.shape[1] - a_col0 if a_ncols is None else a_ncols
    m, k = (a_ncols, a.shape[0]) if ta else (a.shape[0], a_ncols)
    n = b.shape[0] if tb else b.shape[1]
    assert k == (b.shape[1] if tb else b.shape[0]), (name, a.shape, b.shape)
    tm, tn, tk = min(tm, m), min(tn, n), min(tk, k)
    nk = k // tk
    a_off = a_col0 // (tm if ta else tk)
    assert m % tm == 0 and n % tn == 0 and k % tk == 0 and a_col0 % (tm if ta else tk) == 0, (name, m, n, k)

    def body(*refs):
        if bias is None:
            a_ref, b_ref, o_ref, acc = refs
        else:
            a_ref, b_ref, bias_ref, o_ref, acc = refs
        kk = pl.program_id(2)

        @pl.when(kk == 0)
        def _():
            acc[...] = jnp.zeros_like(acc)

        dims = (((0 if ta else 1,), (1 if tb else 0,)), ((), ()))
        acc[...] += lax.dot_general(a_ref[...].astype(MXU_DTYPE), b_ref[...].astype(MXU_DTYPE), dims,
                                    preferred_element_type=F32)

        @pl.when(kk == nk - 1)
        def _():
            r = acc[...]
            if bias is not None:
                r = r + bias_ref[...]
            o_ref[...] = r.astype(out_dtype)

    a_spec = (pl.BlockSpec((tk, tm), lambda i, j, kk: (kk, i + a_off)) if ta
              else pl.BlockSpec((tm, tk), lambda i, j, kk: (i, kk + a_off)))
    b_spec = (pl.BlockSpec((tn, tk), lambda i, j, kk: (j, kk)) if tb
              else pl.BlockSpec((tk, tn), lambda i, j, kk: (kk, j)))
    in_specs, args = [a_spec, b_spec], [a, b]
    if bias is not None:
        in_specs.append(pl.BlockSpec((1, tn), lambda i, j, kk: (0, j)))
        args.append(bias)
    out = _call(body, name=name, out_shape=[SDS((m, n), out_dtype)], grid=(m // tm, n // tn, nk), in_specs=in_specs,
                out_specs=[pl.BlockSpec((tm, tn), lambda i, j, kk: (i, j))],
                scratch_shapes=[pltpu.VMEM((tm, tn), F32)], args=args, carry=carry)
    return out[0] if carry is None else out


def _silu_rows(c_all):
    def body(c_ref, o_ref):
        o_ref[...] = _silu(c_ref[...])
    return pl.pallas_call(body, name="cond_silu", out_shape=SDS(c_all.shape, F32))(c_all)


def _modulate(x, scale, shift, tb):
    s, d = x.shape

    def body(x_ref, sc_ref, sh_ref, o_ref):
        o_ref[...] = (x_ref[...] * (1.0 + sc_ref[...]) + sh_ref[...]).astype(MXU_DTYPE)

    return pl.pallas_call(body, name="modulate", out_shape=SDS((s, d), MXU_DTYPE), grid=(s // tb,),
                          in_specs=[_rows(tb, d), _const((1, d)), _const((1, d))], out_specs=_rows(tb, d),
                          compiler_params=_params(1))(x, scale, shift)


def _out_ln(ycat, w_out, x, gate, ln_g, ln_b, tb):
    s, d = x.shape

    def body(yc_ref, w_ref, x_ref, gt_ref, g_ref, b_ref, xn_ref, xh_ref, y_ref, rs_ref):
        y = jnp.dot(yc_ref[...], w_ref[...], preferred_element_type=F32)
        res = ALPHA * x_ref[...] + (1.0 + gt_ref[...]) * y
        mu = jnp.mean(res, axis=-1, keepdims=True)
        cen = res - mu
        var = jnp.mean(cen * cen, axis=-1, keepdims=True)
        rstd = lax.rsqrt(var + LN_EPS)
        xhat = cen * rstd
        xn_ref[...] = xhat * g_ref[...] + b_ref[...]
        xh_ref[...] = xhat
        y_ref[...] = y
        rs_ref[...] = rstd

    big = SDS((s, d), F32)
    return pl.pallas_call(
        body, name="out_proj_ln", out_shape=(big, big, big, SDS((s, 1), F32)), grid=(s // tb,),
        in_specs=[_rows(tb, d), _const((d, d)), _rows(tb, d), _const((1, d)), _const((1, d)), _const((1, d))],
        out_specs=(_rows(tb, d), _rows(tb, d), _rows(tb, d), _rows(tb, 1)), compiler_params=_params(1),
    )(ycat, w_out, x, gate, ln_g, ln_b)


def _ln_bwd(dxn, xhat, y, rstd, ln_g, gate, tb):
    s, d = dxn.shape

    def body(dxn_ref, xh_ref, y_ref, rs_ref, g_ref, gt_ref, dy_ref, dxa_ref, dg_ref, db_ref, dgt_ref):
        _init_acc(dg_ref, db_ref, dgt_ref)
        dxn_t, xh = dxn_ref[...], xh_ref[...]
        dxh = dxn_t * g_ref[...]
        dres = rs_ref[...] * (dxh - jnp.mean(dxh, axis=-1, keepdims=True)
                              - xh * jnp.mean(dxh * xh, axis=-1, keepdims=True))
        dy_ref[...] = ((1.0 + gt_ref[...]) * dres).astype(MXU_DTYPE)
        dxa_ref[...] = ALPHA * dres
        dg_ref[...] += _colsum(dxn_t * xh)
        db_ref[...] += _colsum(dxn_t)
        dgt_ref[...] += _colsum(dres * y_ref[...])

    vec = SDS((1, d), F32)
    return pl.pallas_call(
        body, name="ln_bwd", out_shape=(SDS((s, d), MXU_DTYPE), SDS((s, d), F32), vec, vec, vec), grid=(s // tb,),
        in_specs=[_rows(tb, d), _rows(tb, d), _rows(tb, d), _rows(tb, 1), _const((1, d)), _const((1, d))],
        out_specs=(_rows(tb, d), _rows(tb, d), _const((1, d)), _const((1, d)), _const((1, d))),
        compiler_params=_params(1))(dxn, xhat, y, rstd, ln_g, gate)


def _mod_bwd(dh, dxa, x, scale, tb):
    s, d = dh.shape

    def body(dh_ref, dxa_ref, x_ref, sc_ref, dx_ref, dsh_ref, dsc_ref):
        _init_acc(dsh_ref, dsc_ref)
        dh_t = dh_ref[...]
        dx_ref[...] = dxa_ref[...] + dh_t * (1.0 + sc_ref[...])
        dsh_ref[...] += _colsum(dh_t)
        dsc_ref[...] += _colsum(dh_t * x_ref[...])

    vec = SDS((1, d), F32)
    return pl.pallas_call(
        body, name="mod_bwd", out_shape=(SDS((s, d), F32), vec, vec), grid=(s // tb,),
        in_specs=[_rows(tb, d), _rows(tb, d), _rows(tb, d), _const((1, d))],
        out_specs=(_rows(tb, d), _const((1, d)), _const((1, d))), compiler_params=_params(1))(dh, dxa, x, scale)


def _loss_head(y, target, tb):
    s, d = y.shape

    def body(y_ref, t_ref, l_ref, dy_ref):
        _init_acc(l_ref)
        err = y_ref[...] - t_ref[...]
        l_ref[...] += (0.5 / d) * jnp.sum(err * err, keepdims=True)
        dy_ref[...] = err * (1.0 / d)

    return pl.pallas_call(body, name="loss_head", out_shape=(SDS((1, 1), F32), SDS((s, d), F32)), grid=(s // tb,),
                          in_specs=[_rows(tb, d), _rows(tb, d)], out_specs=(_const((1, 1)), _rows(tb, d)),
                          compiler_params=_params(1))(y, target)


def _conv_taps(u, up, w_ref, width):
    out = w_ref[width - 1:width, :] * u
    for j in range(width - 2, -1, -1):
        out = out + w_ref[j:j + 1, :] * _shift_down(u, up, width - 1 - j)
    return out


def _conv_taps_t(g, gn, w_ref, width):
    out = w_ref[width - 1:width, :] * g
    for j in range(width - 2, -1, -1):
        out = out + w_ref[j:j + 1, :] * _shift_up(g, gn, width - 1 - j)
    return out


def _conv_wgrad(dw_ref, g, u, up, width):
    dw_ref[width - 1:width, :] += _colsum(g * u)
    for j in range(width - 1):
        dw_ref[j:j + 1, :] += _colsum(g * _shift_down(u, up, width - 1 - j))


def _branch_a_fwd(proj, conv_w, tb):
    s = proj.shape[0]

    def body(ab, ac, ax, ag, acp, axp, w_ref, o_ref):
        has_prev = (pl.program_id(0) > 0).astype(F32)
        u = ac[...] * ax[...]
        up = acp[...] * axp[...] * has_prev
        o_ref[...] = (ab[...] * _conv_taps(u, up, w_ref, 3) * _silu(ag[...])).astype(MXU_DTYPE)

    return pl.pallas_call(
        body, name="branch_a_fwd", out_shape=SDS((s, BR), MXU_DTYPE), grid=(s // tb,),
        in_specs=[_rows(tb, BR, CB_AB), _rows(tb, BR, CB_AC), _rows(tb, BR, CB_AX), _rows(tb, BR, CB_AG),
                  _prev8(tb, BR, CB_AC), _prev8(tb, BR, CB_AX), _const((8, BR))],
        out_specs=_rows(tb, BR), compiler_params=_params(1))(proj, proj, proj, proj, proj, proj, conv_w)


def _branch_a_bwd(dycat, proj, conv_w, tb):
    s = proj.shape[0]

    def body(dy, dyn, ab, abn, ag, agn, ac, acp, ax, axp, w_ref, o_ref, dw_ref):
        _init_acc(dw_ref)
        i = pl.program_id(0)
        has_prev = (i > 0).astype(F32)
        has_next = (i < pl.num_programs(0) - 1).astype(F32)
        u = ac[...] * ax[...]
        up = acp[...] * axp[...] * has_prev
        v = _conv_taps(u, up, w_ref, 3)
        sg = _silu(ag[...])
        dv = dy[...] * ab[...] * sg
        dvn = dyn[...] * abn[...] * _silu(agn[...]) * has_next
        du = _conv_taps_t(dv, dvn, w_ref, 3)
        o_ref[:, 0:BR] = (dy[...] * v * sg).astype(MXU_DTYPE)
        o_ref[:, BR:2 * BR] = (du * ax[...]).astype(MXU_DTYPE)
        o_ref[:, 2 * BR:3 * BR] = (du * ac[...]).astype(MXU_DTYPE)
        o_ref[:, 3 * BR:4 * BR] = (dy[...] * ab[...] * v * _dsilu(ag[...])).astype(MXU_DTYPE)
        _conv_wgrad(dw_ref, dv, u, up, 3)

    return pl.pallas_call(
        body, name="branch_a_bwd", out_shape=(SDS((s, 4 * BR), MXU_DTYPE), SDS((8, BR), F32)), grid=(s // tb,),
        in_specs=[_rows(tb, BR, 0), _next8(tb, BR, s, 0),
                  _rows(tb, BR, CB_AB), _next8(tb, BR, s, CB_AB), _rows(tb, BR, CB_AG), _next8(tb, BR, s, CB_AG),
                  _rows(tb, BR, CB_AC), _prev8(tb, BR, CB_AC), _rows(tb, BR, CB_AX), _prev8(tb, BR, CB_AX),
                  _const((8, BR))],
        out_specs=(_rows(tb, 4 * BR), _const((8, BR))), compiler_params=_params(1),
    )(dycat, dycat, proj, proj, proj, proj, proj, proj, proj, proj, conv_w)


def _t5_bucket(dist):
    max_exact = REL_BUCKETS // 2
    nf = jnp.maximum(dist, 1).astype(F32)
    large = max_exact + (jnp.log(nf / max_exact) / math.log(REL_MAX_DIST / max_exact)
                         * (REL_BUCKETS - max_exact)).astype(jnp.int32)
    large = jnp.minimum(large, REL_BUCKETS - 1)
    return jnp.where(dist < max_exact, dist, large)


def _bucket_maps():
    maps = []
    i = jnp.arange(BLK)[:, None]
    j = jnp.arange(2 * BLK)[None, :]
    delta = i + BLK - j
    for window, dil in DILATIONS:
        span = window // dil
        bucket = _t5_bucket(jnp.clip(delta, 0, span) * dil)
        maps.append(jnp.where((delta >= 0) & (delta <= span), bucket, -1))
    return jnp.stack(maps).astype(jnp.int32)


def _bias_tables(rel_bias, buckets):
    n_pat = len(DILATIONS)

    def body(rb_ref, bk_ref, o_ref):
        for g in range(n_pat):
            bk = bk_ref[g]
            for h in range(ATT_HEADS):
                def per_bucket(b, acc):
                    return jnp.where(bk == b, rb_ref[b, h], acc)
                o_ref[g, h] = lax.fori_loop(0, REL_BUCKETS, per_bucket, jnp.full((BLK, 2 * BLK), NEG, F32))

    return pl.pallas_call(
        body, name="bias_tables", out_shape=SDS((n_pat, ATT_HEADS, BLK, 2 * BLK), F32),
        in_specs=[pl.BlockSpec(memory_space=pltpu.SMEM), pl.BlockSpec(memory_space=pltpu.VMEM)],
        compiler_params=_params())(rel_bias, buckets)


def _head_masks():
    lane = lax.broadcasted_iota(jnp.int32, (1, 2 * HEAD_DIM), 1)
    return [(lane < HEAD_DIM).astype(F32), (lane >= HEAD_DIM).astype(F32)]


def _strided(base, size, dil):
    return pl.ds(base, size, stride=dil) if dil > 1 else pl.ds(pl.multiple_of(base, BLK), size)


def _attn_groups(s, dil):
    return max(1, min(1024, s) // (dil * BLK)) if dil == 1 else max(1, min(2048, s) // (dil * BLK))


def _attn_fwd(proj, bias, dil):
    s = proj.shape[0]
    grp = _attn_groups(s, dil)
    u1 = dil * BLK
    unit = grp * u1
    nb = s // unit
    w = 2 * HEAD_DIM
    q0, k0, v0 = (cb * (BR // w) for cb in (CB_Q, CB_K, CB_V))

    def body(q_ref, kc_ref, kp_ref, vc_ref, vp_ref, bias_ref, o_ref, lse_ref, kbuf, vbuf):
        n = pl.program_id(1)
        col = lax.broadcasted_iota(jnp.int32, (1, 2 * BLK), 1)
        masks = _head_masks()
        kbuf[0:u1, :] = kp_ref[...]
        kbuf[u1:, :] = kc_ref[...]
        vbuf[0:u1, :] = vp_ref[...]
        vbuf[u1:, :] = vc_ref[...]

        def per_r(t, carry):
            j = t // dil
            base = j * u1 + t % dil
            rows = _strided(base, BLK, dil)
            no_prev = jnp.where((n == 0) & (j == 0) & (col < BLK), NEG, 0.0)
            q = q_ref[rows, :] * (HEAD_DIM ** -0.5)
            k = kbuf[_strided(base, 2 * BLK, dil), :].astype(MXU_DTYPE)
            v = vbuf[_strided(base, 2 * BLK, dil), :].astype(MXU_DTYPE)
            q2 = jnp.concatenate([q * masks[0], q * masks[1]], axis=0).astype(MXU_DTYPE)
            sc = lax.dot_general(q2, k, (((1,), (1,)), ((), ())), preferred_element_type=F32)
            sc = sc + jnp.concatenate([bias_ref[0], bias_ref[1]], axis=0) + no_prev
            mx = jnp.max(sc, axis=-1, keepdims=True)
            p = jnp.exp(sc - mx)
            l = jnp.sum(p, axis=-1, keepdims=True)
            o2 = jnp.dot((p / l).astype(MXU_DTYPE), v, preferred_element_type=F32)
            lse2 = mx + jnp.log(l)
            o_ref[rows, :] = o2[0:BLK] * masks[0] + o2[BLK:2 * BLK] * masks[1]
            lse_ref[rows, :] = lse2[0:BLK] * masks[0] + lse2[BLK:2 * BLK] * masks[1]
            return carry

        lax.fori_loop(0, grp * dil, per_r, 0, unroll=8)

    cur = lambda c0: pl.BlockSpec((unit, w), lambda hp, n: (n, c0 + hp))
    prev = lambda c0: pl.BlockSpec((u1, w), lambda hp, n: (jnp.maximum(n * grp - 1, 0), c0 + hp))
    out = pl.BlockSpec((unit, w), lambda hp, n: (n, hp))
    return pl.pallas_call(
        body, name=f"attn_fwd_d{dil}", out_shape=(SDS((s, BR), F32), SDS((s, BR), F32)), grid=(BR // w, nb),
        in_specs=[cur(q0), cur(k0), prev(k0), cur(v0), prev(v0),
                  pl.BlockSpec((2, BLK, 2 * BLK), lambda hp, n: (hp, 0, 0))],
        out_specs=(out, out),
        scratch_shapes=[pltpu.VMEM((unit + u1, w), F32), pltpu.VMEM((unit + u1, w), F32)],
        compiler_params=_params(2))(proj, proj, proj, proj, proj, bias)


def _softmax3(l0, l1, l2):
    mx = jnp.maximum(jnp.maximum(l0, l1), l2)
    e0, e1, e2 = jnp.exp(l0 - mx), jnp.exp(l1 - mx), jnp.exp(l2 - mx)
    inv = 1.0 / (e0 + e1 + e2)
    return e0 * inv, e1 * inv, e2 * inv


def _attn_combine(os_, lses, proj, tb):
    s = proj.shape[0]

    def body(o0, o1, o2, l0, l1, l2, bg, y_ref):
        w0, w1, w2 = _softmax3(l0[...], l1[...], l2[...])
        attn = w0 * o0[...] + w1 * o1[...] + w2 * o2[...]
        y_ref[...] = (attn * _silu(bg[...])).astype(MXU_DTYPE)

    return pl.pallas_call(
        body, name="attn_combine", out_shape=SDS((s, BR), MXU_DTYPE), grid=(s // tb,),
        in_specs=[_rows(tb, BR)] * 6 + [_rows(tb, BR, CB_BG)], out_specs=_rows(tb, BR),
        compiler_params=_params(1))(*os_, *lses, proj)


def _attn_bwd_pre(dycat, os_, lses, proj, head_ones, tb):
    s = proj.shape[0]

    def body(dy, o0, o1, o2, l0, l1, l2, bg, e_ref, dbg_ref, do0, do1, do2, dm0, dm1, dm2):
        w0, w1, w2 = _softmax3(l0[...], l1[...], l2[...])
        attn = w0 * o0[...] + w1 * o1[...] + w2 * o2[...]
        dattn = dy[...] * _silu(bg[...])
        dbg_ref[...] = dy[...] * attn * _dsilu(bg[...])
        prod = dattn * attn
        hi = prod.astype(MXU_DTYPE)
        lo = (prod - hi.astype(F32)).astype(MXU_DTYPE)
        tot = (jnp.dot(hi, e_ref[...], preferred_element_type=F32)
               + jnp.dot(lo, e_ref[...], preferred_element_type=F32))
        for wg, do_ref, dm_ref in ((w0, do0, dm0), (w1, do1, dm1), (w2, do2, dm2)):
            do_ref[...] = wg * dattn
            dm_ref[...] = wg * tot

    big = SDS((s, BR), F32)
    return pl.pallas_call(
        body, name="attn_bwd_pre", out_shape=(big,) * 7, grid=(s // tb,),
        in_specs=[_rows(tb, BR, 1)] + [_rows(tb, BR)] * 6 + [_rows(tb, BR, CB_BG), _const((BR, BR))],
        out_specs=(_rows(tb, BR),) * 7, compiler_params=_params(1))(dycat, *os_, *lses, proj, head_ones)


def _attn_bwd(proj, do, lse, dm, bias, dil, carry=None):
    s = proj.shape[0]
    grp = _attn_groups(s, dil)
    u1 = dil * BLK
    unit = grp * u1
    nb = s // unit
    w = 2 * HEAD_DIM
    q0, k0, v0 = (cb * (BR // w) for cb in (CB_Q, CB_K, CB_V))

    def body(q_ref, kc_ref, kp_ref, vc_ref, vp_ref, do_ref, lse_ref, dm_ref, bias_ref,
             dq_ref, dk_ref, dv_ref, dbias_ref, kbuf, vbuf, stage_k, stage_v):
        n = pl.program_id(1)
        col = lax.broadcasted_iota(jnp.int32, (1, 2 * BLK), 1)
        masks = _head_masks()

        @pl.when(n == 0)
        def _():
            dbias_ref[...] = jnp.zeros_like(dbias_ref)
            stage_k[...] = jnp.zeros_like(stage_k)
            stage_v[...] = jnp.zeros_like(stage_v)

        for out_ref, stage in ((dk_ref, stage_k), (dv_ref, stage_v)):
            if grp > 1:
                out_ref[0:unit - u1, :] = stage[u1:unit, :]
            stage[0:u1, :] = stage[unit:unit + u1, :]

        @pl.when(n < nb)
        def _():
            kbuf[0:u1, :] = kp_ref[...]
            kbuf[u1:, :] = kc_ref[...]
            vbuf[0:u1, :] = vp_ref[...]
            vbuf[u1:, :] = vc_ref[...]

            def per_r(t, carry):
                j = t // dil
                base = j * u1 + t % dil
                rows = _strided(base, BLK, dil)
                rows_hi = _strided(base + u1, BLK, dil)
                no_prev = jnp.where((n == 0) & (j == 0) & (col < BLK), NEG, 0.0)
                q = q_ref[rows, :] * (HEAD_DIM ** -0.5)
                k = kbuf[_strided(base, 2 * BLK, dil), :].astype(MXU_DTYPE)
                v = vbuf[_strided(base, 2 * BLK, dil), :].astype(MXU_DTYPE)
                do_t, lse_t, dm_t = do_ref[rows, :], lse_ref[rows, :], dm_ref[rows, :]
                stack = lambda t: jnp.concatenate([t * masks[0], t * masks[1]], axis=0).astype(MXU_DTYPE)
                per_head = lambda t: jnp.concatenate([t[:, 0:1], t[:, HEAD_DIM:HEAD_DIM + 1]], axis=0)
                q2, do2 = stack(q), stack(do_t)
                sc = lax.dot_general(q2, k, (((1,), (1,)), ((), ())), preferred_element_type=F32)
                p = jnp.exp(sc + jnp.concatenate([bias_ref[0], bias_ref[1]], axis=0) + no_prev - per_head(lse_t))
                dp = lax.dot_general(do2, v, (((1,), (1,)), ((), ())), preferred_element_type=F32)
                ds = p * (dp - per_head(dm_t))
                dbias_ref[0] += ds[0:BLK]
                dbias_ref[1] += ds[BLK:2 * BLK]
                dsb, pb = ds.astype(MXU_DTYPE), p.astype(MXU_DTYPE)
                dq2 = jnp.dot(dsb, k, preferred_element_type=F32)
                dk_acc = lax.dot_general(dsb, q2, (((0,), (0,)), ((), ())), preferred_element_type=F32)
                dv_acc = lax.dot_general(pb, do2, (((0,), (0,)), ((), ())), preferred_element_type=F32)
                dq_ref[rows, :] = (dq2[0:BLK] * masks[0] + dq2[BLK:2 * BLK] * masks[1]) * (HEAD_DIM ** -0.5)
                stage_k[rows, :] = stage_k[rows, :] + dk_acc[0:BLK]
                stage_v[rows, :] = stage_v[rows, :] + dv_acc[0:BLK]
                stage_k[rows_hi, :] = dk_acc[BLK:2 * BLK]
                stage_v[rows_hi, :] = dv_acc[BLK:2 * BLK]
                return carry

            lax.fori_loop(0, grp * dil, per_r, 0, unroll=8)

        dk_ref[unit - u1:unit, :] = stage_k[0:u1, :]
        dv_ref[unit - u1:unit, :] = stage_v[0:u1, :]

    qn = lambda n: jnp.minimum(n, nb - 1)
    cur = lambda c0: pl.BlockSpec((unit, w), lambda hp, n: (qn(n), c0 + hp))
    prev = lambda c0: pl.BlockSpec((u1, w), lambda hp, n: (jnp.maximum(qn(n) * grp - 1, 0), c0 + hp))
    row = pl.BlockSpec((unit, w), lambda hp, n: (qn(n), hp))
    late = pl.BlockSpec((unit, w), lambda hp, n: (jnp.maximum(n - 1, 0), hp))
    tab = pl.BlockSpec((2, BLK, 2 * BLK), lambda hp, n: (hp, 0, 0))
    big = SDS((s, BR), F32)
    return _call(
        body, name=f"attn_bwd_d{dil}", out_shape=(big, big, big, SDS((ATT_HEADS, BLK, 2 * BLK), F32)),
        grid=(BR // w, nb + 1),
        in_specs=[cur(q0), cur(k0), prev(k0), cur(v0), prev(v0), row, row, row, tab],
        out_specs=(row, late, late, tab),
        scratch_shapes=[pltpu.VMEM((unit + u1, w), F32)] * 4,
        args=(proj, proj, proj, proj, proj, do, lse, dm, bias), carry=carry)


def _rel_bias_grad(dbias, buckets):
    def body(db_ref, bk_ref, o_ref):
        row = lax.broadcasted_iota(jnp.int32, (REL_BUCKETS, 128), 0)
        lane = lax.broadcasted_iota(jnp.int32, (REL_BUCKETS, 128), 1)

        def per_bucket(b, acc):
            for g in range(len(DILATIONS)):
                hit = bk_ref[g] == b
                for h in range(ATT_HEADS):
                    both = db_ref[0, g, h] + db_ref[1, g, h]
                    val = jnp.sum(jnp.where(hit, both, 0.0), keepdims=True)
                    acc = acc + jnp.where((row == b) & (lane == h), val, 0.0)
            return acc

        o_ref[...] = lax.fori_loop(0, REL_BUCKETS, per_bucket, jnp.zeros((REL_BUCKETS, 128), F32))

    assert dbias.shape[0] == DEPTH == 2
    return pl.pallas_call(body, name="rel_bias_grad", out_shape=SDS((REL_BUCKETS, 128), F32),
                          compiler_params=_params())(dbias, buckets)


def _scan_real(a, b, *, reverse, tb, name):
    s, ch = a.shape
    nt = s // tb
    order = range(7, -1, -1) if reverse else range(8)

    def body(a_ref, b_ref, o_ref, carry):
        @pl.when(pl.program_id(0) == 0)
        def _():
            carry[...] = jnp.zeros_like(carry)

        def group(gi, h):
            r0 = pl.multiple_of((tb // 8 - 1 - gi if reverse else gi) * 8, 8)
            a8, b8 = a_ref[pl.ds(r0, 8), :], b_ref[pl.ds(r0, 8), :]
            rows = [None] * 8
            for k in order:
                if reverse:
                    rows[k] = b8[k:k + 1] + h
                    h = a8[k:k + 1] * rows[k]
                else:
                    h = a8[k:k + 1] * h + b8[k:k + 1]
                    rows[k] = h
            o_ref[pl.ds(r0, 8), :] = jnp.concatenate(rows, axis=0)
            return h

        carry[...] = lax.fori_loop(0, tb // 8, group, carry[...])

    spec = pl.BlockSpec((tb, ch), (lambda i: (nt - 1 - i, 0)) if reverse else (lambda i: (i, 0)))
    return pl.pallas_call(body, name=name, out_shape=SDS((s, ch), F32), grid=(nt,), in_specs=[spec, spec],
                          out_specs=spec, scratch_shapes=[pltpu.VMEM((1, ch), F32)],
                          compiler_params=_params(1))(a, b)


def _scan_cplx(b, a_row, *, reverse, tb, name):
    s, ch2 = b.shape
    ch = ch2 // 2
    nt = s // tb
    order = range(7, -1, -1) if reverse else range(8)

    def body(a_ref, b_ref, o_ref, carry):
        @pl.when(pl.program_id(0) == 0)
        def _():
            carry[...] = jnp.zeros_like(carry)

        ar = a_ref[:, 0:ch]
        ai = -a_ref[:, ch:ch2] if reverse else a_ref[:, ch:ch2]

        def group(gi, x):
            xr, xi = x
            r0 = pl.multiple_of((tb // 8 - 1 - gi if reverse else gi) * 8, 8)
            br8, bi8 = b_ref[pl.ds(r0, 8), 0:ch], b_ref[pl.ds(r0, 8), ch:ch2]
            rr, ri = [None] * 8, [None] * 8
            for k in order:
                nr = ar * xr - ai * xi + br8[k:k + 1]
                ni = ar * xi + ai * xr + bi8[k:k + 1]
                xr, xi = nr, ni
                rr[k], ri[k] = xr, xi
            o_ref[pl.ds(r0, 8), 0:ch] = jnp.concatenate(rr, axis=0)
            o_ref[pl.ds(r0, 8), ch:ch2] = jnp.concatenate(ri, axis=0)
            return xr, xi

        xr, xi = lax.fori_loop(0, tb // 8, group, (carry[:, 0:ch], carry[:, ch:ch2]))
        carry[:, 0:ch] = xr
        carry[:, ch:ch2] = xi

    spec = pl.BlockSpec((tb, ch2), (lambda i: (nt - 1 - i, 0)) if reverse else (lambda i: (i, 0)))
    return pl.pallas_call(body, name=name, out_shape=SDS((s, ch2), F32), grid=(nt,),
                          in_specs=[_const((1, ch2)), spec], out_specs=spec,
                          scratch_shapes=[pltpu.VMEM((1, ch2), F32)], compiler_params=_params(1))(a_row, b)


def _neg_expm1(z):
    series = -z * (1.0 + z * (0.5 + z * (1.0 / 6 + z * (1.0 / 24 + z * (1.0 / 120)))))
    return jnp.where(z > -0.05, series, 1.0 - jnp.exp(z))


def _lru_gate(xc, pre_r, pre_i, lam):
    log_a = -LRU_C * jax.nn.sigmoid(pre_r) * jax.nn.softplus(-lam)
    return jnp.exp(log_a), jnp.sqrt(_neg_expm1(2.0 * log_a)) * jax.nn.sigmoid(pre_i) * xc


def _lru_gates_fwd(proj, conv_w, conv_b, w_cat, b_cat, lam, tb):
    s = proj.shape[0]

    def body(cx, cxp, w_ref, cb_ref, wc_ref, bc_ref, lam_ref, a_ref, b_ref):
        has_prev = (pl.program_id(0) > 0).astype(F32)
        xc = _conv_taps(cx[...], cxp[...] * has_prev, w_ref, 4) + cb_ref[...]
        pre = jnp.dot(xc.astype(MXU_DTYPE), wc_ref[...], preferred_element_type=F32) + bc_ref[...]
        a_ref[...], b_ref[...] = _lru_gate(xc, pre[:, 0:BR], pre[:, BR:2 * BR], lam_ref[...])

    big = SDS((s, BR), F32)
    return pl.pallas_call(
        body, name="lru_gates_fwd", out_shape=(big, big), grid=(s // tb,),
        in_specs=[_rows(tb, BR, CB_CX), _prev8(tb, BR, CB_CX), _const((8, BR)), _const((1, BR)),
                  _const((BR, 2 * BR)), _const((1, 2 * BR)), _const((1, BR))],
        out_specs=(_rows(tb, BR), _rows(tb, BR)), compiler_params=_params(1),
    )(proj, proj, conv_w, conv_b, w_cat, b_cat, lam)


def _gate_out(h, proj, cb, tb, name):
    s = proj.shape[0]

    def body(h_ref, g_ref, o_ref):
        o_ref[...] = (h_ref[...] * _silu(g_ref[...])).astype(MXU_DTYPE)

    return pl.pallas_call(body, name=name, out_shape=SDS((s, BR), MXU_DTYPE), grid=(s // tb,),
                          in_specs=[_rows(tb, BR), _rows(tb, BR, cb)], out_specs=_rows(tb, BR),
                          compiler_params=_params(1))(h, proj)


def _gate_out_bwd(dycat, dy_cb, h, proj, cb, tb, name):
    s = proj.shape[0]

    def body(dy, h_ref, g_ref, dh_ref, dg_ref):
        dh_ref[...] = dy[...] * _silu(g_ref[...])
        dg_ref[...] = dy[...] * h_ref[...] * _dsilu(g_ref[...])

    big = SDS((s, BR), F32)
    return pl.pallas_call(body, name=name, out_shape=(big, big), grid=(s // tb,),
                          in_specs=[_rows(tb, BR, dy_cb), _rows(tb, BR), _rows(tb, BR, cb)],
                          out_specs=(_rows(tb, BR), _rows(tb, BR)), compiler_params=_params(1))(dycat, h, proj)


def _lru_gates_bwd(proj, lmb, h, conv_w, conv_b, w_cat, b_cat, lam, tb):
    s = proj.shape[0]

    def body(cx, cxp, l_ref, h_ref, hp_ref, w_ref, cb_ref, wc_ref, bc_ref, lam_ref,
             dxc_ref, dpre_ref, xc_ref, dbc_ref, dlam_ref):
        _init_acc(dbc_ref, dlam_ref)
        has_prev = (pl.program_id(0) > 0).astype(F32)
        xc = _conv_taps(cx[...], cxp[...] * has_prev, w_ref, 4) + cb_ref[...]
        xcb = xc.astype(MXU_DTYPE)
        pre = jnp.dot(xcb, wc_ref[...], preferred_element_type=F32) + bc_ref[...]
        _, vjp = jax.vjp(_lru_gate, xc, pre[:, 0:BR], pre[:, BR:2 * BR], lam_ref[...])
        lm = l_ref[...]
        dxc, dpr, dpi, dlam = vjp((lm * _shift_down(h_ref[...], hp_ref[...] * has_prev, 1), lm))
        dpre = jnp.concatenate([dpr, dpi], axis=1)
        dpreb = dpre.astype(MXU_DTYPE)
        dxc_ref[...] = dxc + lax.dot_general(dpreb, wc_ref[...], (((1,), (1,)), ((), ())),
                                             preferred_element_type=F32)
        dpre_ref[...] = dpreb
        xc_ref[...] = xcb
        dbc_ref[...] += _colsum(dpre)
        dlam_ref[...] += dlam

    return pl.pallas_call(
        body, name="lru_gates_bwd",
        out_shape=(SDS((s, BR), F32), SDS((s, 2 * BR), MXU_DTYPE), SDS((s, BR), MXU_DTYPE),
                   SDS((1, 2 * BR), F32), SDS((1, BR), F32)),
        grid=(s // tb,),
        in_specs=[_rows(tb, BR, CB_CX), _prev8(tb, BR, CB_CX), _rows(tb, BR), _rows(tb, BR), _prev8(tb, BR),
                  _const((8, BR)), _const((1, BR)), _const((BR, 2 * BR)), _const((1, 2 * BR)), _const((1, BR))],
        out_specs=(_rows(tb, BR), _rows(tb, 2 * BR), _rows(tb, BR), _const((1, 2 * BR)), _const((1, BR))),
        compiler_params=_params(1))(proj, proj, lmb, h, h, conv_w, conv_b, w_cat, b_cat, lam)


def _conv_c_bwd(dxc, proj, conv_w, tb):
    s = proj.shape[0]

    def body(g, gn, cx, cxp, w_ref, dcx_ref, dw_ref, db_ref):
        _init_acc(dw_ref, db_ref)
        i = pl.program_id(0)
        has_prev = (i > 0).astype(F32)
        has_next = (i < pl.num_programs(0) - 1).astype(F32)
        gt = g[...]
        dcx_ref[...] = _conv_taps_t(gt, gn[...] * has_next, w_ref, 4)
        _conv_wgrad(dw_ref, gt, cx[...], cxp[...] * has_prev, 4)
        db_ref[...] += _colsum(gt)

    return pl.pallas_call(
        body, name="conv_c_bwd", out_shape=(SDS((s, BR), F32), SDS((8, BR), F32), SDS((1, BR), F32)),
        grid=(s // tb,),
        in_specs=[_rows(tb, BR), _next8(tb, BR, s), _rows(tb, BR, CB_CX), _prev8(tb, BR, CB_CX), _const((8, BR))],
        out_specs=(_rows(tb, BR), _const((8, BR)), _const((1, BR))), compiler_params=_params(1),
    )(dxc, dxc, proj, proj, conv_w)


def _s5_disc(lam_re, lam_im, log_dt):
    dt = jnp.exp(log_dt)
    mag = jnp.exp(lam_re * dt)
    ab_re = mag * jnp.cos(lam_im * dt)
    ab_im = mag * jnp.sin(lam_im * dt)
    den = lam_re * lam_re + lam_im * lam_im
    f_re = ((ab_re - 1.0) * lam_re + ab_im * lam_im) / den
    f_im = (ab_im * lam_re - (ab_re - 1.0) * lam_im) / den
    return ab_re, ab_im, f_re, f_im


def _s5_bbar(f_re, f_im, b_re, b_im):
    return f_re * b_re - f_im * b_im, f_re * b_im + f_im * b_re


def _s5_disc_fwd(lam_re, lam_im, log_dt):
    def body(lr, li, ld, o0, o1, o2, o3):
        o0[...], o1[...], o2[...], o3[...] = _s5_disc(lr[...], li[...], ld[...])
    return pl.pallas_call(body, name="s5_disc_fwd", out_shape=(SDS(lam_re.shape, F32),) * 4)(lam_re, lam_im, log_dt)


def _s5_disc_bwd(lam_re, lam_im, log_dt, cts):
    def body(lr, li, ld, c0, c1, c2, c3, o0, o1, o2):
        _, vjp = jax.vjp(_s5_disc, lr[...], li[...], ld[...])
        o0[...], o1[...], o2[...] = vjp((c0[...], c1[...], c2[...], c3[...]))
    return pl.pallas_call(body, name="s5_disc_bwd", out_shape=(SDS(lam_re.shape, F32), SDS(lam_re.shape, F32),
                                                                SDS(log_dt.shape, F32)))(lam_re, lam_im, log_dt, *cts)


def _s5_bbar_fwd(f_re, f_im, b_re, b_im):
    def body(fr, fi, br, bi, o0, o1):
        o0[...], o1[...] = _s5_bbar(fr[...], fi[...], br[...], bi[...])
    return pl.pallas_call(body, name="s5_bbar_fwd", out_shape=(SDS(b_re.shape, F32),) * 2)(f_re, f_im, b_re, b_im)


def _s5_bbar_bwd(f_re, f_im, b_re, b_im, d_re, d_im):
    def body(fr, fi, br, bi, dr, di, o0, o1, o2, o3):
        _, vjp = jax.vjp(_s5_bbar, fr[...], fi[...], br[...], bi[...])
        o0[...], o1[...], o2[...], o3[...] = vjp((dr[...], di[...]))
    col, mat = SDS(f_re.shape, F32), SDS(b_re.shape, F32)
    return pl.pallas_call(body, name="s5_bbar_bwd", out_shape=(col, col, mat, mat))(f_re, f_im, b_re, b_im, d_re, d_im)


def _s5_tail_fwd(ylin, proj, d_skip, w_glu, b_glu, tb):
    s = proj.shape[0]

    def body(yl, u, dg, dk, w_ref, b_ref, o_ref):
        g = jax.nn.gelu(yl[...] + dk[...] * u[...])
        t = jnp.dot(g.astype(MXU_DTYPE), w_ref[...], preferred_element_type=F32) + b_ref[...]
        o_ref[...] = (g * jax.nn.sigmoid(t) * _silu(dg[...])).astype(MXU_DTYPE)

    return pl.pallas_call(
        body, name="s5_tail_fwd", out_shape=SDS((s, BR), MXU_DTYPE), grid=(s // tb,),
        in_specs=[_rows(tb, BR), _rows(tb, BR, CB_DU), _rows(tb, BR, CB_DG), _const((1, BR)), _const((BR, BR)),
                  _const((1, BR))],
        out_specs=_rows(tb, BR), compiler_params=_params(1))(ylin, proj, proj, d_skip, w_glu, b_glu)


def _s5_tail_bwd(dycat, ylin, proj, d_skip, w_glu, b_glu, tb):
    s = proj.shape[0]

    def body(dy, yl, u, dg, dk, w_ref, b_ref, dyl_ref, dus_ref, ddg_ref, g_ref, dt_ref, ddk_ref, dbg_ref):
        _init_acc(ddk_ref, dbg_ref)
        g, gelu_vjp = jax.vjp(jax.nn.gelu, yl[...] + dk[...] * u[...])
        gb = g.astype(MXU_DTYPE)
        sg = jax.nn.sigmoid(jnp.dot(gb, w_ref[...], preferred_element_type=F32) + b_ref[...])
        dz = dy[...] * _silu(dg[...])
        ddg_ref[...] = dy[...] * g * sg * _dsilu(dg[...])
        dt = dz * g * sg * (1.0 - sg)
        dtb = dt.astype(MXU_DTYPE)
        dgel = dz * sg + lax.dot_general(dtb, w_ref[...], (((1,), (1,)), ((), ())), preferred_element_type=F32)
        dyv, = gelu_vjp(dgel)
        dyl_ref[...] = dyv
        dus_ref[...] = dyv * dk[...]
        g_ref[...] = gb
        dt_ref[...] = dtb
        ddk_ref[...] += _colsum(dyv * u[...])
        dbg_ref[...] += _colsum(dt)

    big, half, vec = SDS((s, BR), F32), SDS((s, BR), MXU_DTYPE), SDS((1, BR), F32)
    return pl.pallas_call(
        body, name="s5_tail_bwd", out_shape=(big, big, big, half, half, vec, vec), grid=(s // tb,),
        in_specs=[_rows(tb, BR, 3), _rows(tb, BR), _rows(tb, BR, CB_DU), _rows(tb, BR, CB_DG), _const((1, BR)),
                  _const((BR, BR)), _const((1, BR))],
        out_specs=(_rows(tb, BR),) * 5 + (_const((1, BR)), _const((1, BR))), compiler_params=_params(1),
    )(dycat, ylin, proj, proj, d_skip, w_glu, b_glu)


def _s5_da(lmb, x, tb):
    s, ch2 = x.shape
    ch = ch2 // 2

    def body(l_ref, x_ref, xp_ref, o_ref):
        _init_acc(o_ref)
        has_prev = (pl.program_id(0) > 0).astype(F32)
        xprev = _shift_down(x_ref[...], xp_ref[...] * has_prev, 1)
        lr, li, xr, xi = l_ref[:, 0:ch], l_ref[:, ch:ch2], xprev[:, 0:ch], xprev[:, ch:ch2]
        o_ref[:, 0:ch] += _colsum(lr * xr + li * xi)
        o_ref[:, ch:ch2] += _colsum(li * xr - lr * xi)

    return pl.pallas_call(body, name="s5_da", out_shape=SDS((1, ch2), F32), grid=(s // tb,),
                          in_specs=[_rows(tb, ch2), _rows(tb, ch2), _prev8(tb, ch2)], out_specs=_const((1, ch2)),
                          compiler_params=_params(1))(lmb, x, x)


def _assemble_dproj(da, dqkv, dbg, dcx, dcg, du, dus, ddg, tb):
    s = da.shape[0]

    def body(da_ref, q0, q1, q2, k0, k1, k2, v0, v1, v2, dbg_ref, dcx_ref, dcg_ref, du_ref, dus_ref, ddg_ref, o_ref):
        o_ref[:, 0:4 * BR] = da_ref[...]
        for j, parts in enumerate(((q0, q1, q2), (k0, k1, k2), (v0, v1, v2))):
            o_ref[:, (4 + j) * BR:(5 + j) * BR] = (parts[0][...] + parts[1][...] + parts[2][...]).astype(MXU_DTYPE)
        o_ref[:, 7 * BR:8 * BR] = dbg_ref[...].astype(MXU_DTYPE)
        o_ref[:, 8 * BR:9 * BR] = dcx_ref[...].astype(MXU_DTYPE)
        o_ref[:, 9 * BR:10 * BR] = dcg_ref[...].astype(MXU_DTYPE)
        o_ref[:, 10 * BR:11 * BR] = (du_ref[...] + dus_ref[...]).astype(MXU_DTYPE)
        o_ref[:, 11 * BR:12 * BR] = ddg_ref[...].astype(MXU_DTYPE)

    flat = [t for grp in dqkv for t in grp]
    return pl.pallas_call(
        body, name="assemble_dproj", out_shape=SDS((s, N_IN), MXU_DTYPE), grid=(s // tb,),
        in_specs=[_rows(tb, 4 * BR)] + [_rows(tb, BR)] * 15, out_specs=_rows(tb, N_IN),
        compiler_params=_params(1))(da, *flat, dbg, dcx, dcg, du, dus, ddg)


def _sum_leading(xs, tr, name):
    n, r, c = xs[0].shape
    nl = len(xs)
    tr = min(tr, r)
    nr = r // tr
    assert r % tr == 0, (name, r, tr)

    def body(*refs):
        i = pl.program_id(0)
        for l in range(nl):
            @pl.when((i >= l * nr) & (i < (l + 1) * nr))
            def _():
                acc = refs[l * n][...].astype(F32)
                for ref in refs[l * n + 1:(l + 1) * n]:
                    acc = acc + ref[...].astype(F32)
                refs[nl * n][...] = acc

    specs = [pl.BlockSpec((None, tr, c), functools.partial(lambda i, k, l: (k, jnp.clip(i - l * nr, 0, nr - 1), 0), k=k, l=l))
             for l in range(nl) for k in range(n)]
    return pl.pallas_call(body, name=name, out_shape=SDS((nl * r, c), F32), grid=(nl * nr,), in_specs=specs,
                          out_specs=pl.BlockSpec((tr, c), lambda i: (i, 0)),
                          compiler_params=_params(1))(*[x for x in xs for _ in range(n)])


def _adamw(w, g_parts, m, v, tr, name):
    r, c = w.shape
    tr = min(tr, r)
    n = len(g_parts)
    assert r % tr == 0, (name, r, tr)

    def body(*refs):
        w_ref, m_ref, v_ref = refs[0], refs[1 + n], refs[2 + n]
        g_ref, d_ref, nm_ref, nv_ref = refs[3 + n:]
        g = refs[1][...]
        for ref in refs[2:1 + n]:
            g = g + ref[...]
        mm = ADAM_B1 * m_ref[...] + (1.0 - ADAM_B1) * g
        vv = ADAM_B2 * v_ref[...] + (1.0 - ADAM_B2) * jnp.square(g)
        m_hat = mm / (1.0 - ADAM_B1 ** ADAM_STEP)
        v_hat = vv / (1.0 - ADAM_B2 ** ADAM_STEP)
        g_ref[...] = g
        d_ref[...] = -ADAM_LR * (m_hat / (jnp.sqrt(v_hat) + ADAM_EPS) + ADAM_WD * w_ref[...])
        nm_ref[...] = mm
        nv_ref[...] = vv

    spec = pl.BlockSpec((tr, c), lambda i: (i, 0))
    return pl.pallas_call(body, name=name, out_shape=(SDS((r, c), F32),) * 4, grid=(r // tr,),
                          in_specs=[spec] * (3 + n), out_specs=(spec,) * 4,
                          compiler_params=_params(1))(w, *g_parts, m, v)


def _allgather8(block, name):
    m_per, n = block.shape

    def body(x_ref, out_ref, send_sems, recv_sems, local_sem):
        x, y, c = lax.axis_index("x"), lax.axis_index("y"), lax.axis_index("c")
        me, sibling = (x, y, c), (x, y, 1 - c)
        chips = [(1 - x, y), (x, 1 - y), (1 - x, 1 - y)]

        def rows(px, py, pc):
            return out_ref.at[pl.ds((4 * px + 2 * py + pc) * m_per, m_per), :]

        def copy(k, blk, to, src=None):
            return pltpu.make_async_remote_copy(
                src_ref=rows(*blk) if src is None else src, dst_ref=rows(*blk), send_sem=send_sems.at[k],
                recv_sem=recv_sems.at[k], device_id=to, device_id_type=MESH)

        mine = pltpu.make_async_copy(x_ref, rows(*me), local_sem)
        mine.start()
        first = [copy(0, me, sibling, src=x_ref)]
        first += [copy(1 + j, me, (*chip, c), src=x_ref) for j, chip in enumerate(chips)]
        for cp in first:
            cp.start()
        passed = [copy(4 + j, (*chip, c), sibling) for j, chip in enumerate(chips)]
        for j, chip in enumerate(chips):
            copy(1 + j, (*chip, c), me).wait_recv()
            passed[j].start()
        copy(0, sibling, me).wait_recv()
        for j, chip in enumerate(chips):
            copy(4 + j, (*chip, 1 - c), me).wait_recv()
        for cp in first + passed:
            cp.wait_send()
        mine.wait()

    return pl.pallas_call(
        body, name=name, out_shape=SDS((N_DEV * m_per, n), block.dtype),
        in_specs=[pl.BlockSpec(memory_space=pltpu.VMEM)], out_specs=pl.BlockSpec(memory_space=pltpu.VMEM),
        scratch_shapes=[pltpu.SemaphoreType.DMA((7,)), pltpu.SemaphoreType.DMA((7,)), pltpu.SemaphoreType.DMA],
        compiler_params=_params())(block)


class _Exchange:
    def __init__(self, items, out_shapes):
        self.items, self.out_shapes = list(items), tuple(out_shapes)
        self.arrays = [it[0] for it in self.items]
        n = len(self.items)
        self.n_in, self.n_out = n, len(self.out_shapes)
        self.scratch = [pltpu.SemaphoreType.DMA((n * N_CHIPS,)), pltpu.SemaphoreType.DMA((n * N_CHIPS,)),
                        pltpu.SemaphoreType.DMA((n,))]

    def _copies(self, ins, outs, sems, m):
        send_sems, recv_sems, local_sems = sems
        c = lax.axis_index("c")
        others = [j for j in range(N_CHIPS) if j != m]

        def remote(a, src, dst, to, from_):
            return pltpu.make_async_remote_copy(
                src_ref=src, dst_ref=dst, send_sem=send_sems.at[a * N_CHIPS + to],
                recv_sem=recv_sems.at[a * N_CHIPS + from_], device_id=(to // 2, to % 2, c), device_id_type=MESH)

        local, sends, recvs = [], [], []
        for a, (_, oi, src_of, dst_of) in enumerate(self.items):
            local.append(pltpu.make_async_copy(src_of(ins[a], m), dst_of(outs[oi], m), local_sems.at[a]))
            for j in others:
                sends.append(remote(a, src_of(ins[a], j), dst_of(outs[oi], m), j, m))
                recvs.append(remote(a, src_of(ins[a], m), dst_of(outs[oi], j), j, j))
        return local, sends, recvs

    def _on_my_chip(self, fn):
        chip = 2 * lax.axis_index("x") + lax.axis_index("y")
        for m in range(N_CHIPS):
            pl.when(chip == m)(functools.partial(fn, m))

    def start(self, ins, outs, sems):
        def go(m):
            local, sends, _ = self._copies(ins, outs, sems, m)
            for cp in local + sends:
                cp.start()
        self._on_my_chip(go)

    def wait(self, ins, outs, sems):
        def go(m):
            local, sends, recvs = self._copies(ins, outs, sems, m)
            for cp in recvs:
                cp.wait_recv()
            for cp in sends:
                cp.wait_send()
            for cp in local:
                cp.wait()
        self._on_my_chip(go)


def _half_rows(ref, cc):
    h = ref.shape[-2] // 2
    return ref.at[(slice(None),) * (len(ref.shape) - 2) + (pl.ds(cc * h, h), slice(None))]


class _Gather:
    def __init__(self, items, out_shapes):
        self.items, self.out_shapes = list(items), tuple(out_shapes)
        self.arrays = [it[0] for it in self.items]
        n = len(self.items)
        self.n_in, self.n_out = n, len(self.out_shapes)
        self.scratch = [pltpu.SemaphoreType.DMA((n * N_CHIPS,)) for _ in range(4)] + [pltpu.SemaphoreType.DMA((n,))]

    def _copies(self, ins, outs, sems, m, cc):
        ici_send, ici_recv, d2d_send, d2d_recv, local_sems = sems
        others = [j for j in range(N_CHIPS) if j != m]
        local, sends, arrivals, passed_on, from_sibling = [], [], [], [], []
        for a, (_, oi, src_of, dst_of) in enumerate(self.items):
            src, out = src_of(ins[a]), outs[oi]
            local.append(pltpu.make_async_copy(src, dst_of(out, m), local_sems.at[a]))
            for j in others:
                k = a * N_CHIPS + j
                mine_there = _half_rows(dst_of(out, m), cc)
                theirs_here = _half_rows(dst_of(out, j), cc)
                sends.append(pltpu.make_async_remote_copy(
                    src_ref=_half_rows(src, cc), dst_ref=mine_there, send_sem=ici_send.at[k],
                    recv_sem=ici_recv.at[a * N_CHIPS + m], device_id=(j // 2, j % 2, cc), device_id_type=MESH))
                arrivals.append(pltpu.make_async_remote_copy(
                    src_ref=_half_rows(src, cc), dst_ref=theirs_here, send_sem=ici_send.at[k], recv_sem=ici_recv.at[k],
                    device_id=(j // 2, j % 2, cc), device_id_type=MESH))
                passed_on.append(pltpu.make_async_remote_copy(
                    src_ref=theirs_here, dst_ref=theirs_here, send_sem=d2d_send.at[k], recv_sem=d2d_recv.at[k],
                    device_id=(m // 2, m % 2, 1 - cc), device_id_type=MESH))
                other_half = _half_rows(dst_of(out, j), 1 - cc)
                from_sibling.append(pltpu.make_async_remote_copy(
                    src_ref=other_half, dst_ref=other_half, send_sem=d2d_send.at[k], recv_sem=d2d_recv.at[k],
                    device_id=(m // 2, m % 2, 1 - cc), device_id_type=MESH))
        return local, sends, arrivals, passed_on, from_sibling

    def _on_my_core(self, fn):
        chip = 2 * lax.axis_index("x") + lax.axis_index("y")
        c = lax.axis_index("c")
        for m in range(N_CHIPS):
            for cc in range(2):
                pl.when((chip == m) & (c == cc))(functools.partial(fn, m, cc))

    def start(self, ins, outs, sems):
        def go(m, cc):
            local, sends, _, _, _ = self._copies(ins, outs, sems, m, cc)
            for cp in local + sends:
                cp.start()
        self._on_my_core(go)

    def wait(self, ins, outs, sems):
        def go(m, cc):
            local, sends, arrivals, passed_on, from_sibling = self._copies(ins, outs, sems, m, cc)
            for arrived, onward in zip(arrivals, passed_on):
                arrived.wait_recv()
                onward.start()
            for cp in from_sibling:
                cp.wait_recv()
            for cp in sends + passed_on:
                cp.wait_send()
            for cp in local:
                cp.wait()
        self._on_my_core(go)


def _run_exchange(ex, name):
    def body(*refs):
        ins, outs, sems = refs[:ex.n_in], refs[ex.n_in:ex.n_in + ex.n_out], refs[ex.n_in + ex.n_out:]
        ex.start(ins, outs, sems)
        ex.wait(ins, outs, sems)

    return pl.pallas_call(
        body, name=name, out_shape=ex.out_shapes, in_specs=[ANY] * ex.n_in, out_specs=(ANY,) * ex.n_out,
        scratch_shapes=ex.scratch, compiler_params=_params())(*ex.arrays)


def _sibling_swap(arrays, name):
    n = len(arrays)

    def body(*refs):
        ins, outs = refs[:n], refs[n:2 * n]
        send_sems, recv_sems = refs[2 * n:]
        peer = (lax.axis_index("x"), lax.axis_index("y"), 1 - lax.axis_index("c"))
        cps = [pltpu.make_async_remote_copy(src_ref=ins[a], dst_ref=outs[a], send_sem=send_sems.at[a],
                                            recv_sem=recv_sems.at[a], device_id=peer, device_id_type=MESH)
               for a in range(n)]
        for cp in cps:
            cp.start()
        for cp in cps:
            cp.wait()

    return pl.pallas_call(
        body, name=name, out_shape=tuple(SDS(a.shape, a.dtype) for a in arrays), in_specs=[ANY] * n,
        out_specs=(ANY,) * n, scratch_shapes=[pltpu.SemaphoreType.DMA((n,)), pltpu.SemaphoreType.DMA((n,))],
        compiler_params=_params())(*arrays)


def _block_diag(w):
    h, n, m = w.shape
    eye = jnp.eye(h, dtype=w.dtype)
    return (w[:, :, None, :] * eye[:, None, :, None]).reshape(h * n, h * m)


def _diag_blocks(d, h, col0=0, ncols=None, stacked=1):
    ncols = d.shape[1] - col0 if ncols is None else ncols
    n, m = d.shape[0] // (h * stacked), ncols // h
    lanes = 128
    assert m <= lanes and lanes % m == 0 and col0 % lanes == 0

    def body(d_ref, o_ref):
        for gi in range(h * stacked):
            c = col0 + (gi % h) * m
            chunk = d_ref[gi * n:(gi + 1) * n, c // lanes * lanes:c // lanes * lanes + lanes]
            o_ref[gi * n:(gi + 1) * n, :] = chunk[:, c % lanes:c % lanes + m]

    out = pl.pallas_call(body, name="diag_blocks", out_shape=SDS((stacked * h * n, m), d.dtype),
                         compiler_params=_params())(d)
    return out.reshape(stacked * h, n, m)


S5_CHUNKS = 4
S5_PER = S5_GROUPS // S5_CHUNKS
CH_W = S5_PER * S5_CH
ST_W = S5_PER * S5_STATE


def _bd_stack(mats):
    _, _, n, m = mats.shape
    eye = jnp.eye(S5_PER, dtype=mats.dtype)
    t = mats.reshape(2, S5_CHUNKS, S5_PER, n, m)
    bd = t[:, :, :, :, None, :] * eye[None, None, :, None, :, None]
    return bd.reshape(2 * S5_CHUNKS, S5_PER * n, S5_PER * m).astype(MXU_DTYPE)


def _bd_expand(a, a_col0, w8, name):
    s = a.shape[0]
    tm = min(1024, s)
    c0 = a_col0 // CH_W

    def body(a_ref, w_ref, o_ref):
        o_ref[...] = jnp.dot(a_ref[...].astype(MXU_DTYPE), w_ref[...], preferred_element_type=F32)

    return pl.pallas_call(
        body, name=name, out_shape=SDS((s, 2 * S5_N), F32), grid=(s // tm, 2 * S5_CHUNKS),
        in_specs=[pl.BlockSpec((tm, CH_W), lambda i, b: (i, c0 + b % S5_CHUNKS)),
                  pl.BlockSpec((None, CH_W, ST_W), lambda i, b: (b, 0, 0))],
        out_specs=pl.BlockSpec((tm, ST_W), lambda i, b: (i, b)), compiler_params=_params(2))(a, w8)


def _bd_reduce(x, w8, name):
    s = x.shape[0]
    tm = min(1024, s)

    def body(x_ref, w_ref, o_ref, acc):
        p = pl.program_id(2)

        @pl.when(p == 0)
        def _():
            acc[...] = jnp.zeros_like(acc)

        acc[...] += jnp.dot(x_ref[...].astype(MXU_DTYPE), w_ref[...], preferred_element_type=F32)

        @pl.when(p == 1)
        def _():
            o_ref[...] = acc[...]

    return pl.pallas_call(
        body, name=name, out_shape=SDS((s, BR), F32), grid=(s // tm, S5_CHUNKS, 2),
        in_specs=[pl.BlockSpec((tm, ST_W), lambda i, q, p: (i, p * S5_CHUNKS + q)),
                  pl.BlockSpec((None, ST_W, CH_W), lambda i, q, p: (p * S5_CHUNKS + q, 0, 0))],
        out_specs=pl.BlockSpec((tm, CH_W), lambda i, q, p: (i, q)),
        scratch_shapes=[pltpu.VMEM((tm, CH_W), F32)], compiler_params=_params(3))(x, w8)


def _bd_wgrad(a, a_col0, x, name):
    s = a.shape[0]
    tk = min(1024, s)
    nk = s // tk
    c0 = a_col0 // CH_W

    def body(a_ref, x_ref, o_ref, acc):
        k = pl.program_id(1)

        @pl.when(k == 0)
        def _():
            acc[...] = jnp.zeros_like(acc)

        acc[...] += lax.dot_general(a_ref[...].astype(MXU_DTYPE), x_ref[...].astype(MXU_DTYPE),
                                    (((0,), (0,)), ((), ())), preferred_element_type=F32)

        @pl.when(k == nk - 1)
        def _():
            o_ref[...] = acc[...]

    return pl.pallas_call(
        body, name=name, out_shape=SDS((2 * S5_CHUNKS * CH_W, ST_W), F32), grid=(2 * S5_CHUNKS, nk),
        in_specs=[pl.BlockSpec((tk, CH_W), lambda b, k: (k, c0 + b % S5_CHUNKS)),
                  pl.BlockSpec((tk, ST_W), lambda b, k: (k, b))],
        out_specs=pl.BlockSpec((CH_W, ST_W), lambda b, k: (b, 0)),
        scratch_shapes=[pltpu.VMEM((CH_W, ST_W), F32)], compiler_params=_params(2))(a, x)


def _tiles(s):
    return dict(tb=min(512, s), tln=min(256, s), tscan=min(256, s))


def _layer_weights(p, l):
    pad8 = lambda w: jnp.pad(w, ((0, 8 - w.shape[0]), (0, 0)))
    return dict(
        conv_a=pad8(p["conv_a"][l]), conv_c=pad8(p["conv_c"][l]), conv_c_b=p["conv_c_b"][l][None],
        w_cat=jnp.concatenate([_block_diag(p["lru_wa"][l]), _block_diag(p["lru_wx"][l])], axis=1).astype(MXU_DTYPE),
        b_cat=jnp.concatenate([p["lru_ba"][l], p["lru_bx"][l]])[None], lam=p["lru_lambda"][l][None],
        lam_re=p["s5_lam_re"][l], lam_im=p["s5_lam_im"][l], log_dt=p["s5_log_dt"][l][:, None],
        b_re=p["s5_b_re"][l].reshape(S5_N, S5_CH), b_im=p["s5_b_im"][l].reshape(S5_N, S5_CH),
        c_re=p["s5_c_re"][l], c_im=p["s5_c_im"][l], d_skip=p["s5_d"][l][None], b_glu=p["s5_b_glu"][l][None],
        ln_g=p["ln_g"][l][None], ln_b=p["ln_b"][l][None])


def _s5_matrices(lw):
    ab_re, ab_im, f_re, f_im = _s5_disc_fwd(lw["lam_re"], lw["lam_im"], lw["log_dt"])
    f_re, f_im = f_re.reshape(S5_N, 1), f_im.reshape(S5_N, 1)
    bb_re, bb_im = _s5_bbar_fwd(f_re, f_im, lw["b_re"], lw["b_im"])
    bb = jnp.stack([bb_re, bb_im]).reshape(2, S5_GROUPS, S5_STATE, S5_CH)
    cc = jnp.stack([lw["c_re"], -lw["c_im"]])
    a_row = jnp.concatenate([ab_re.reshape(1, S5_N), ab_im.reshape(1, S5_N)], axis=1)
    return dict(f_re=f_re, f_im=f_im, a_row=a_row, w_bu=_bd_stack(jnp.swapaxes(bb, 2, 3)), w_du=_bd_stack(bb),
                w_cx=_bd_stack(jnp.swapaxes(cc, 2, 3)), w_dx=_bd_stack(cc))


def _mm_hooked(hook, *args, **kw):
    if hook is None:
        return _mm(*args, **kw)
    out = _mm(*args, carry=hook[0], **kw)
    hook[1](out[1:])
    return out[0]


def _layer_fwd(x, ada, w_in, get_rest, lw, s5m, bias_tabs, hooks=None):
    s = x.shape[0]
    t = _tiles(s)
    tb = t["tb"]
    shift, scale, gate = ada
    hooks = hooks or {}
    h = _modulate(x, scale, shift, tb)
    proj = _mm_hooked(hooks.get("in_proj"), h, w_in, name="in_proj", tm=1024, tn=1024, tk=D_MODEL)
    w_out, w_glu = get_rest()
    y_a = _branch_a_fwd(proj, lw["conv_a"], tb)
    os_, lses = [], []
    for g, (_, dil) in enumerate(DILATIONS):
        o, lse = _attn_fwd(proj, bias_tabs[g], dil)
        os_.append(o)
        lses.append(lse)
    y_b = _attn_combine(os_, lses, proj, tb)
    lru_a, lru_b = _lru_gates_fwd(proj, lw["conv_c"], lw["conv_c_b"], lw["w_cat"], lw["b_cat"], lw["lam"], tb)
    lru_h = _scan_real(lru_a, lru_b, reverse=False, tb=tb, name="lru_scan")
    y_c = _gate_out(lru_h, proj, CB_CG, tb, "lru_out")
    bu = _bd_expand(proj, CB_DU * BR, s5m["w_bu"], "s5_bu")
    s5_x = _scan_cplx(bu, s5m["a_row"], reverse=False, tb=t["tscan"], name="s5_scan")
    ylin = _bd_reduce(s5_x, s5m["w_cx"], "s5_cx")
    y_d = _s5_tail_fwd(ylin, proj, lw["d_skip"], w_glu, lw["b_glu"], tb)
    ycat = jnp.concatenate([y_a, y_b, y_c, y_d], axis=1)
    x_next, xhat, y, rstd = _out_ln(ycat, w_out, x, gate, lw["ln_g"], lw["ln_b"], t["tln"])
    saved = dict(x=x, h=h, proj=proj, os=os_, lses=lses, lru_a=lru_a, lru_h=lru_h, s5_x=s5_x, ylin=ylin, ycat=ycat,
                 xhat=xhat, y=y, rstd=rstd)
    return x_next, saved


def _layer_bwd(dxn, sv, ada, w_in, w_out, w_glu, lw, s5m, bias_tabs, head_ones, hooks=None):
    s = dxn.shape[0]
    t = _tiles(s)
    tb = t["tb"]
    shift, scale, gate = ada
    proj = sv["proj"]
    g = {}
    hook = lambda name: hooks[name](g) if hooks and name in hooks else None
    dyb, dxa, g["ln_g"], g["ln_b"], dgate = _ln_bwd(dxn, sv["xhat"], sv["y"], sv["rstd"], lw["ln_g"], gate, t["tln"])
    g["w_out"] = _mm_hooked(hook("dw_out"), sv["ycat"], dyb, name="dw_out", ta=True, out_dtype=WIRE_DTYPE,
                            tm=1024, tn=1024, tk=1024)
    dycat = _mm(dyb, w_out, name="dycat", tb=True, tm=1024, tn=1024, tk=D_MODEL)
    da, dconv_a = _branch_a_bwd(dycat, proj, lw["conv_a"], tb)
    g["conv_a"] = dconv_a[0:3]
    pre = _attn_bwd_pre(dycat, sv["os"], sv["lses"], proj, head_ones, tb)
    dbg, dos, dms = pre[0], pre[1:4], pre[4:7]
    dqkv, dbias = [], []
    for gi, (_, dil) in enumerate(DILATIONS):
        hk = hook(f"attn_bwd_d{dil}")
        dq, dk, dv, dbi, *got = _attn_bwd(proj, dos[gi], sv["lses"][gi], dms[gi], bias_tabs[gi], dil,
                                          carry=hk and hk[0])
        if hk:
            hk[1](got)
        dqkv.append((dq, dk, dv))
        dbias.append(dbi)
    dqkv = list(zip(*dqkv))
    dh, dcg = _gate_out_bwd(dycat, 2, sv["lru_h"], proj, CB_CG, tb, "lru_out_bwd")
    lmb = _scan_real(sv["lru_a"], dh, reverse=True, tb=tb, name="lru_scan_bwd")
    dxc, dpre, xcb, dbcat, dlam = _lru_gates_bwd(proj, lmb, sv["lru_h"], lw["conv_c"], lw["conv_c_b"], lw["w_cat"],
                                                  lw["b_cat"], lw["lam"], tb)
    dwcat = _mm(xcb, dpre, name="dw_lru", ta=True, tn=1024)
    g["lru_wa"] = _diag_blocks(dwcat, LRU_HEADS, 0, BR)
    g["lru_wx"] = _diag_blocks(dwcat, LRU_HEADS, BR, BR)
    g["lru_ba"], g["lru_bx"], g["lru_lambda"] = dbcat[0, 0:BR], dbcat[0, BR:2 * BR], dlam[0]
    dcx, dconv_c, dccb = _conv_c_bwd(dxc, proj, lw["conv_c"], tb)
    g["conv_c"], g["conv_c_b"] = dconv_c[0:4], dccb[0]
    dyl, dus, ddg, gb, dtb, ddk, dbglu = _s5_tail_bwd(dycat, sv["ylin"], proj, lw["d_skip"], w_glu, lw["b_glu"], tb)
    g["s5_d"], g["s5_b_glu"] = ddk[0], dbglu[0]
    g["s5_w_glu"] = _mm(gb, dtb, name="dw_glu", ta=True, out_dtype=WIRE_DTYPE)
    dxd = _bd_expand(dyl, 0, s5m["w_dx"], "s5_dx")
    s5_l = _scan_cplx(dxd, s5m["a_row"], reverse=True, tb=t["tscan"], name="s5_scan_bwd")
    dab = _s5_da(s5_l, sv["s5_x"], t["tscan"])
    du = _bd_reduce(s5_l, s5m["w_du"], "s5_du")
    per_group = lambda d8: _diag_blocks(d8, S5_PER, stacked=2 * S5_CHUNKS).reshape(2, S5_GROUPS, S5_CH, S5_STATE)
    dbb = per_group(_bd_wgrad(proj, CB_DU * BR, s5_l, "dw_s5_b"))
    dcc = per_group(_bd_wgrad(dyl, 0, sv["s5_x"], "dw_s5_c"))
    from_bd = lambda half: jnp.swapaxes(dbb[half], 1, 2).reshape(S5_N, S5_CH)
    df_re, df_im, db_re, db_im = _s5_bbar_bwd(s5m["f_re"], s5m["f_im"], lw["b_re"], lw["b_im"],
                                              from_bd(0), from_bd(1))
    shp = (S5_GROUPS, S5_STATE)
    g["s5_lam_re"], g["s5_lam_im"], dlog_dt = _s5_disc_bwd(
        lw["lam_re"], lw["lam_im"], lw["log_dt"],
        (dab[:, 0:S5_N].reshape(shp), dab[:, S5_N:].reshape(shp), df_re.reshape(shp), df_im.reshape(shp)))
    g["s5_log_dt"] = dlog_dt[:, 0]
    g["s5_b_re"] = db_re.reshape(S5_GROUPS, S5_STATE, S5_CH)
    g["s5_b_im"] = db_im.reshape(S5_GROUPS, S5_STATE, S5_CH)
    g["s5_c_re"], g["s5_c_im"] = dcc[0], -dcc[1]
    dproj = _assemble_dproj(da, dqkv, dbg, dcx, dcg, du, dus, ddg, tb)
    g["w_in"] = _mm_hooked(hook("dw_in"), sv["h"], dproj, name="dw_in", ta=True, out_dtype=WIRE_DTYPE,
                           tm=1024, tn=1536, tk=1024)
    dhm = _mm_hooked(hook("dh"), dproj, w_in, name="dh", tb=True, tm=1024, tn=1024, tk=1536)
    dx, dshift, dscale = _mod_bwd(dhm, dxa, sv["x"], scale, tb)
    g["ada"] = jnp.concatenate([dshift[0], dscale[0], dgate[0]])
    return dx, g, dbias


SMALL = ("rel_bias", "conv_a", "conv_c", "conv_c_b", "lru_wa", "lru_ba", "lru_wx", "lru_bx", "lru_lambda",
         "s5_lam_re", "s5_lam_im", "s5_log_dt", "s5_b_re", "s5_b_im", "s5_c_re", "s5_c_im", "s5_d", "s5_b_glu",
         "ln_g", "ln_b")
PER_LAYER_SMALL = SMALL[1:]


def _local_step(x, target, ada_rows, w_in, w_out, w_glu, p, comm=None):
    if comm is None:
        get_w_in = lambda l: w_in[l]
        get_rest = lambda l: (w_out[l], w_glu[l])
        fwd_hooks = bwd_hooks = lambda *_: None
    else:
        get_w_in, get_rest, fwd_hooks, bwd_hooks = comm.w_in, comm.rest, comm.fwd_hooks, comm.bwd_hooks
    s = x.shape[0]
    buckets = _bucket_maps()
    bias_tabs = _bias_tables(p["rel_bias"], buckets)
    head_ones = _block_diag(jnp.ones((ATT_HEADS, HEAD_DIM, HEAD_DIM), MXU_DTYPE))
    lws = [_layer_weights(p, l) for l in range(DEPTH)]
    s5ms = [_s5_matrices(lw) for lw in lws]
    adas = [tuple(ada_rows[l, k * D_MODEL:(k + 1) * D_MODEL][None] for k in range(3)) for l in range(DEPTH)]
    saved = []
    for l in range(DEPTH):
        x, sv = _layer_fwd(x, adas[l], get_w_in(l), functools.partial(get_rest, l), lws[l], s5ms[l], bias_tabs,
                           fwd_hooks(l))
        saved.append(sv)
    loss, dx = _loss_head(x, target, _tiles(s)["tb"])
    grads = [None] * DEPTH
    dbias_sum = []
    for l in reversed(range(DEPTH)):
        dx, grads[l], dbias = _layer_bwd(dx, saved[l], adas[l], get_w_in(l), *get_rest(l), lws[l], s5ms[l],
                                         bias_tabs, head_ones, bwd_hooks(l, grads))
        dbias_sum.append(jnp.stack(dbias))
    drel = _rel_bias_grad(jnp.stack(dbias_sum), buckets)[:, 0:ATT_HEADS]
    small = {n: jnp.stack([grads[l][n] for l in range(DEPTH)]) for n in PER_LAYER_SMALL + ("ada",)}
    small["rel_bias"] = drel
    big = {n: [grads[l][n] for l in range(DEPTH)] for n in ("w_in", "w_out", "s5_w_glu")}
    return loss, dx, big, small


PACK_ROWS = 256


def _pack(parts):
    flat = jnp.concatenate([t.reshape(-1).astype(F32) for t in parts])
    n = flat.shape[0]
    rows = -(-n // (PACK_ROWS * 128)) * PACK_ROWS
    return jnp.pad(flat, (0, rows * 128 - n)).reshape(rows, 128)


def _unpack(packed, shapes):
    flat = packed.reshape(packed.shape[:-2] + (-1,))
    out, off = [], 0
    for shp in shapes:
        size = math.prod(shp)
        out.append(flat[..., off:off + size].reshape(flat.shape[:-1] + tuple(shp)))
        off += size
    return out


def _take_cols(t, chip, width):
    return lax.dynamic_slice_in_dim(t, chip * width, width, axis=t.ndim - 1)


class _Comm:
    IN_W, OUT_R, GLU_R = N_IN // N_CHIPS, D_MODEL // N_CHIPS, BR // N_CHIPS

    def __init__(self, w_in_b, w_out_b, w_glu_b):
        assert DEPTH == 2
        self.shards = (w_in_b, w_out_b, w_glu_b)
        in_w = self.IN_W
        self.w_in_full = {0: _run_exchange(_Gather(
            [(w_in_b, 0, lambda ref: ref.at[0], lambda ref, j: ref.at[:, pl.ds(j * in_w, in_w)])],
            [SDS((D_MODEL, N_IN), WIRE_DTYPE)]), "gather_w_in0")[0]}
        self.w_out_full = self.w_glu_full = None
        self.recv = {}

    def w_in(self, l):
        return self.w_in_full[l]

    def rest(self, l):
        return self.w_out_full[l], self.w_glu_full[l]

    def fwd_hooks(self, l):
        if l != 0:
            return None
        w_in_b, w_out_b, w_glu_b = self.shards
        in_w, out_r, glu_r = self.IN_W, self.OUT_R, self.GLU_R
        whole = lambda ref: ref
        items = [(w_out_b, 0, whole, lambda ref, j: ref.at[:, pl.ds(j * out_r, out_r), :]),
                 (w_glu_b, 1, whole, lambda ref, j: ref.at[:, pl.ds(j * glu_r, glu_r), :]),
                 (w_in_b, 2, lambda ref: ref.at[1], lambda ref, j: ref.at[:, pl.ds(j * in_w, in_w)])]
        shapes = [SDS((DEPTH, D_MODEL, D_MODEL), WIRE_DTYPE), SDS((DEPTH, BR, BR), WIRE_DTYPE),
                  SDS((D_MODEL, N_IN), WIRE_DTYPE)]

        def done(got):
            self.w_out_full, self.w_glu_full, self.w_in_full[1] = got

        return {"in_proj": (_Gather(items, shapes), done)}

    def _scatter(self, parts):
        in_w, out_r, glu_r = self.IN_W, self.OUT_R, self.GLU_R
        cut = {"w_in": (lambda ref, j: ref.at[:, pl.ds(j * in_w, in_w)], (D_MODEL, in_w)),
               "w_out": (lambda ref, j: ref.at[pl.ds(j * out_r, out_r), :], (out_r, D_MODEL)),
               "s5_w_glu": (lambda ref, j: ref.at[pl.ds(j * glu_r, glu_r), :], (glu_r, BR))}
        items = [(arr, oi, cut[name][0], lambda ref, j: ref.at[j]) for oi, (name, _, arr) in enumerate(parts)]
        shapes = [SDS((N_CHIPS,) + cut[name][1], WIRE_DTYPE) for name, _, _ in parts]

        def done(got):
            for (name, l, _), arr in zip(parts, got):
                self.recv[name, l] = arr

        return _Exchange(items, shapes), done

    def bwd_hooks(self, l, grads):
        if l != 0:
            return None
        g1 = grads[1]
        return {"dw_out": lambda g: self._scatter([("w_out", 1, g1["w_out"]), ("s5_w_glu", 1, g1["s5_w_glu"])]),
                "attn_bwd_d16": lambda g: self._scatter([("w_in", 1, g1["w_in"])]),
                "dw_in": lambda g: self._scatter([("w_out", 0, g["w_out"]), ("s5_w_glu", 0, g["s5_w_glu"])]),
                "dh": lambda g: self._scatter([("w_in", 0, g["w_in"])])}


def kernel(x, c, rel_bias, w_ada, b_ada, w_in, conv_a, conv_c, conv_c_b, lru_wa, lru_ba, lru_wx, lru_bx, lru_lambda, s5_lam_re, s5_lam_im, s5_log_dt, s5_b_re, s5_b_im, s5_c_re, s5_c_im, s5_d, s5_w_glu, s5_b_glu, w_out, ln_g, ln_b, loss_target, m_rel_bias, m_w_ada, m_b_ada, m_w_in, m_conv_a, m_conv_c, m_conv_c_b, m_lru_wa, m_lru_ba, m_lru_wx, m_lru_bx, m_lru_lambda, m_s5_lam_re, m_s5_lam_im, m_s5_log_dt, m_s5_b_re, m_s5_b_im, m_s5_c_re, m_s5_c_im, m_s5_d, m_s5_w_glu, m_s5_b_glu, m_w_out, m_ln_g, m_ln_b, v_rel_bias, v_w_ada, v_b_ada, v_w_in, v_conv_a, v_conv_c, v_conv_c_b, v_lru_wa, v_lru_ba, v_lru_wx, v_lru_bx, v_lru_lambda, v_s5_lam_re, v_s5_lam_im, v_s5_log_dt, v_s5_b_re, v_s5_b_im, v_s5_c_re, v_s5_c_im, v_s5_d, v_s5_w_glu, v_s5_b_glu, v_w_out, v_ln_g, v_ln_b):
    args = dict(locals())
    names = ("rel_bias", "w_ada", "b_ada", "w_in", "conv_a", "conv_c", "conv_c_b", "lru_wa", "lru_ba", "lru_wx",
             "lru_bx", "lru_lambda", "s5_lam_re", "s5_lam_im", "s5_log_dt", "s5_b_re", "s5_b_im", "s5_c_re", "s5_c_im",
             "s5_d", "s5_w_glu", "s5_b_glu", "w_out", "ln_g", "ln_b")
    w = {n: args[n] for n in names}
    mom = {n: args["m_" + n] for n in names}
    var = {n: args["v_" + n] for n in names}
    chip = 2 * lax.axis_index("x") + lax.axis_index("y")
    me = 2 * chip + lax.axis_index("c")
    ada_w = 3 * D_MODEL // N_CHIPS
    in_w = N_IN // N_CHIPS
    out_r = D_MODEL // N_CHIPS
    glu_r = BR // N_CHIPS
    conv_w = BR // N_CHIPS

    comm = _Comm(w["w_in"].astype(WIRE_DTYPE), w["w_out"].astype(WIRE_DTYPE), w["s5_w_glu"].astype(WIRE_DTYPE))

    taps = jnp.concatenate([w["conv_a"].reshape(DEPTH * 3, conv_w), w["conv_c"].reshape(DEPTH * 4, conv_w)])
    first = jnp.concatenate([c, jnp.pad(taps, ((0, 1), (0, D_MODEL - conv_w)))])
    got = _allgather8(first, "gather_c_taps").reshape(N_CHIPS, 2, 16, D_MODEL)
    c_all = got[:, :, 0].reshape(N_DEV, D_MODEL)
    taps_all = jnp.transpose(got[:, 0, 1:1 + DEPTH * 7, 0:conv_w], (1, 0, 2)).reshape(DEPTH * 7, BR)
    conv_a_f = taps_all[0:DEPTH * 3].reshape(DEPTH, 3, BR)
    conv_c_f = taps_all[DEPTH * 3:].reshape(DEPTH, 4, BR)

    cond_all = _silu_rows(c_all)
    ada_part = jnp.stack([_mm(cond_all, w["w_ada"][l], name="ada_fwd", tk=D_MODEL, tn=512,
                              bias=_take_cols(w["b_ada"][l][None], chip, ada_w)) for l in range(DEPTH)])
    ada_all = _allgather8(ada_part.reshape(DEPTH * N_DEV, ada_w), "gather_ada")
    ada_all = ada_all.reshape(N_CHIPS, 2, DEPTH, N_DEV, ada_w)[:, 0]
    ada_rows = lax.dynamic_index_in_dim(ada_all, me, axis=2, keepdims=False)
    ada_rows = jnp.transpose(ada_rows, (1, 0, 2)).reshape(DEPTH, 3 * D_MODEL)

    p = dict(w)
    p["conv_a"], p["conv_c"] = conv_a_f, conv_c_f
    loss, dx, _, small = _local_step(x[0], loss_target[0], ada_rows, None, None, None, p, comm)

    sums = [_sum_leading([comm.recv[name, l] for l in range(DEPTH)], 256, "sum_chips")
            for name in ("w_in", "w_out", "s5_w_glu")]
    others = _sibling_swap(sums, "swap_cores")
    out = {}
    for name, mine, other in zip(("w_in", "w_out", "s5_w_glu"), sums, others):
        shp = w[name].shape
        flat = lambda t: t.reshape(-1, shp[-1])
        res = _adamw(flat(w[name]), [mine, other], flat(mom[name]), flat(var[name]), 128, "adamw_big")
        out[name] = [t.reshape(shp) for t in res]

    small_names = SMALL + ("ada",)
    small["loss"] = loss
    order = small_names + ("loss",)
    shapes = [small[n].shape for n in order]
    gathered = _allgather8(_pack([small[n] for n in order]), "gather_small")
    gathered = gathered.reshape(N_DEV, -1, 128)
    total = dict(zip(order, _unpack(_sum_leading([gathered], PACK_ROWS, "sum_devices"), shapes)))
    d_ada_all = _unpack(gathered, shapes)[order.index("ada")]
    g_small = {n: total[n] for n in SMALL}
    g_small["conv_a"] = _take_cols(total["conv_a"], chip, conv_w)
    g_small["conv_c"] = _take_cols(total["conv_c"], chip, conv_w)
    g_small["b_ada"] = total["ada"]
    g_w_ada = jnp.stack([_mm(cond_all, _take_cols(d_ada_all[:, l], chip, ada_w), name="dw_ada", ta=True, tn=ada_w)
                         for l in range(DEPTH)])
    upd_names = SMALL + ("b_ada",)
    upd_shapes = [w[n].shape for n in upd_names]
    res = _adamw(_pack([w[n] for n in upd_names]), [_pack([g_small[n] for n in upd_names])],
                 _pack([mom[n] for n in upd_names]), _pack([var[n] for n in upd_names]), PACK_ROWS, "adamw_small")
    for k, t in enumerate(res):
        for n, val in zip(upd_names, _unpack(t, upd_shapes)):
            out.setdefault(n, [None] * 4)[k] = val
    shp = w["w_ada"].shape
    flat = lambda t: t.reshape(-1, shp[-1])
    out["w_ada"] = [t.reshape(shp) for t in _adamw(flat(w["w_ada"]), [flat(g_w_ada)], flat(mom["w_ada"]),
                                                  flat(var["w_ada"]), 128, "adamw_ada")]
    return (total["loss"].reshape(()), dx[None]) + tuple(out[n][k] for k in range(4) for n in names)
```

```python
import functools
import math

import jax
import jax.numpy as jnp
from jax import lax
from jax.experimental import pallas as pl
from jax.experimental.pallas import tpu as pltpu

F32 = jnp.float32
MXU_DTYPE = jnp.bfloat16
WIRE_DTYPE = jnp.bfloat16
SDS = jax.ShapeDtypeStruct
MESH = pl.DeviceIdType.MESH
ANY = pl.BlockSpec(memory_space=pl.ANY)
VMEM_LIMIT = 48 * 1024 * 1024

D_MODEL = 2048
DEPTH = 2
BR = 512
ATT_HEADS = 8
HEAD_DIM = 64
DILATIONS = ((128, 1), (512, 4), (2048, 16))
BLK = 128
REL_BUCKETS = 32
REL_MAX_DIST = 2048
LRU_HEADS = 8
LRU_C = 8.0
S5_CH = 16
S5_GROUPS = 32
S5_STATE = 64
S5_N = S5_GROUPS * S5_STATE
N_IN = 12 * BR
ALPHA = (2 * DEPTH) ** 0.25
LN_EPS = 1e-5
NEG = -1e30
ADAM_LR, ADAM_B1, ADAM_B2, ADAM_EPS, ADAM_WD, ADAM_STEP = 0.001, 0.9, 0.999, 1e-08, 0.01, 10
CB_AB, CB_AC, CB_AX, CB_AG, CB_Q, CB_K, CB_V, CB_BG, CB_CX, CB_CG, CB_DU, CB_DG = range(12)
N_CHIPS = 4
N_DEV = 8


def _params(n_axes=0):
    kw = {"dimension_semantics": ("arbitrary",) * n_axes} if n_axes else {}
    return pltpu.CompilerParams(vmem_limit_bytes=VMEM_LIMIT, **kw)


def _rows(tb, w, cb=0):
    return pl.BlockSpec((tb, w), lambda i: (i, cb))


def _prev8(tb, w, cb=0):
    return pl.BlockSpec((8, w), lambda i: (jnp.maximum(i * (tb // 8) - 1, 0), cb))


def _next8(tb, w, n_rows, cb=0):
    return pl.BlockSpec((8, w), lambda i: (jnp.minimum((i + 1) * (tb // 8), n_rows // 8 - 1), cb))


def _const(shape):
    return pl.BlockSpec(shape, lambda *_: (0,) * len(shape))


def _silu(x):
    return x * jax.nn.sigmoid(x)


def _dsilu(x):
    s = jax.nn.sigmoid(x)
    return s * (1.0 + x * (1.0 - s))


def _shift_down(cur, prev8, j):
    rolled = pltpu.roll(cur, j, 0)
    row = lax.broadcasted_iota(jnp.int32, (8, cur.shape[1]), 0)
    first = jnp.where(row < j, pltpu.roll(prev8, j, 0), rolled[0:8])
    return jnp.concatenate([first, rolled[8:]], axis=0)


def _shift_up(cur, next8, j):
    t = cur.shape[0]
    rolled = pltpu.roll(cur, t - j, 0)
    row = lax.broadcasted_iota(jnp.int32, (8, cur.shape[1]), 0)
    last = jnp.where(row >= 8 - j, pltpu.roll(next8, 8 - j, 0), rolled[t - 8:t])
    return jnp.concatenate([rolled[:t - 8], last], axis=0)


def _colsum(x):
    return jnp.sum(x, axis=0, keepdims=True)


def _init_acc(*refs):
    @pl.when(pl.program_id(0) == 0)
    def _():
        for r in refs:
            r[...] = jnp.zeros_like(r)


def _call(body, *, name, out_shape, grid, in_specs, out_specs, scratch_shapes, args, carry=None):
    out_shape, out_specs, in_specs = tuple(out_shape), tuple(out_specs), list(in_specs)
    scratch_shapes = list(scratch_shapes)
    if carry is None:
        return pl.pallas_call(body, name=name, out_shape=out_shape, grid=grid, in_specs=in_specs, out_specs=out_specs,
                              scratch_shapes=scratch_shapes, compiler_params=_params(len(grid)))(*args)
    n_in, n_out, n_scr = len(in_specs), len(out_shape), len(scratch_shapes)

    def wrapped(*refs):
        ins, refs = refs[:n_in], refs[n_in:]
        x_ins, refs = refs[:carry.n_in], refs[carry.n_in:]
        outs, refs = refs[:n_out], refs[n_out:]
        x_outs, refs = refs[:carry.n_out], refs[carry.n_out:]
        scr, x_sems = refs[:n_scr], refs[n_scr:]
        at = [pl.program_id(d) for d in range(len(grid))]
        first = functools.reduce(lambda p, q: p & q, [i == 0 for i in at])
        last = functools.reduce(lambda p, q: p & q, [i == g - 1 for i, g in zip(at, grid)])
        pl.when(first)(lambda: carry.start(x_ins, x_outs, x_sems))
        body(*ins, *outs, *scr)
        pl.when(last)(lambda: carry.wait(x_ins, x_outs, x_sems))

    return pl.pallas_call(
        wrapped, name=name, out_shape=out_shape + carry.out_shapes, grid=grid, in_specs=in_specs + [ANY] * carry.n_in,
        out_specs=out_specs + (ANY,) * carry.n_out, scratch_shapes=scratch_shapes + carry.scratch,
        compiler_params=_params(len(grid)))(*args, *carry.arrays)


def _mm(a, b, *, name, ta=False, tb=False, out_dtype=F32, tm=512, tn=512, tk=512, a_col0=0, a_ncols=None, bias=None,
        carry=None):
    a_ncols = a.shape[1] - a_col0 if a_ncols is None else a_ncols
    m, k = (a_ncols, a.shape[0]) if ta else (a.shape[0], a_ncols)
    n = b.shape[0] if tb else b.shape[1]
    assert k == (b.shape[1] if tb else b.shape[0]), (name, a.shape, b.shape)
    tm, tn, tk = min(tm, m), min(tn, n), min(tk, k)
    nk = k // tk
    a_off = a_col0 // (tm if ta else tk)
    assert m % tm == 0 and n % tn == 0 and k % tk == 0 and a_col0 % (tm if ta else tk) == 0, (name, m, n, k)

    def body(*refs):
        if bias is None:
            a_ref, b_ref, o_ref, acc = refs
        else:
            a_ref, b_ref, bias_ref, o_ref, acc = refs
        kk = pl.program_id(2)

        @pl.when(kk == 0)
        def _():
            acc[...] = jnp.zeros_like(acc)

        dims = (((0 if ta else 1,), (1 if tb else 0,)), ((), ()))
        acc[...] += lax.dot_general(a_ref[...].astype(MXU_DTYPE), b_ref[...].astype(MXU_DTYPE), dims,
                                    preferred_element_type=F32)

        @pl.when(kk == nk - 1)
        def _():
            r = acc[...]
            if bias is not None:
                r = r + bias_ref[...]
            o_ref[...] = r.astype(out_dtype)

    a_spec = (pl.BlockSpec((tk, tm), lambda i, j, kk: (kk, i + a_off)) if ta
              else pl.BlockSpec((tm, tk), lambda i, j, kk: (i, kk + a_off)))
    b_spec = (pl.BlockSpec((tn, tk), lambda i, j, kk: (j, kk)) if tb
              else pl.BlockSpec((tk, tn), lambda i, j, kk: (kk, j)))
    in_specs, args = [a_spec, b_spec], [a, b]
    if bias is not None:
        in_specs.append(pl.BlockSpec((1, tn), lambda i, j, kk: (0, j)))
        args.append(bias)
    out = _call(body, name=name, out_shape=[SDS((m, n), out_dtype)], grid=(m // tm, n // tn, nk), in_specs=in_specs,
                out_specs=[pl.BlockSpec((tm, tn), lambda i, j, kk: (i, j))],
                scratch_shapes=[pltpu.VMEM((tm, tn), F32)], args=args, carry=carry)
    return out[0] if carry is None else out


def _silu_rows(c_all):
    def body(c_ref, o_ref):
        o_ref[...] = _silu(c_ref[...])
    return pl.pallas_call(body, name="cond_silu", out_shape=SDS(c_all.shape, F32))(c_all)


def _modulate(x, scale, shift, tb):
    s, d = x.shape

    def body(x_ref, sc_ref, sh_ref, o_ref):
        o_ref[...] = (x_ref[...] * (1.0 + sc_ref[...]) + sh_ref[...]).astype(MXU_DTYPE)

    return pl.pallas_call(body, name="modulate", out_shape=SDS((s, d), MXU_DTYPE), grid=(s // tb,),
                          in_specs=[_rows(tb, d), _const((1, d)), _const((1, d))], out_specs=_rows(tb, d),
                          compiler_params=_params(1))(x, scale, shift)


def _out_ln(ycat, w_out, x, gate, ln_g, ln_b, tb):
    s, d = x.shape

    def body(yc_ref, w_ref, x_ref, gt_ref, g_ref, b_ref, xn_ref, xh_ref, y_ref, rs_ref):
        y = jnp.dot(yc_ref[...], w_ref[...], preferred_element_type=F32)
        res = ALPHA * x_ref[...] + (1.0 + gt_ref[...]) * y
        mu = jnp.mean(res, axis=-1, keepdims=True)
        cen = res - mu
        var = jnp.mean(cen * cen, axis=-1, keepdims=True)
        rstd = lax.rsqrt(var + LN_EPS)
        xhat = cen * rstd
        xn_ref[...] = xhat * g_ref[...] + b_ref[...]
        xh_ref[...] = xhat
        y_ref[...] = y
        rs_ref[...] = rstd

    big = SDS((s, d), F32)
    return pl.pallas_call(
        body, name="out_proj_ln", out_shape=(big, big, big, SDS((s, 1), F32)), grid=(s // tb,),
        in_specs=[_rows(tb, d), _const((d, d)), _rows(tb, d), _const((1, d)), _const((1, d)), _const((1, d))],
        out_specs=(_rows(tb, d), _rows(tb, d), _rows(tb, d), _rows(tb, 1)), compiler_params=_params(1),
    )(ycat, w_out, x, gate, ln_g, ln_b)


def _ln_bwd(dxn, xhat, y, rstd, ln_g, gate, tb):
    s, d = dxn.shape

    def body(dxn_ref, xh_ref, y_ref, rs_ref, g_ref, gt_ref, dy_ref, dxa_ref, dg_ref, db_ref, dgt_ref):
        _init_acc(dg_ref, db_ref, dgt_ref)
        dxn_t, xh = dxn_ref[...], xh_ref[...]
        dxh = dxn_t * g_ref[...]
        dres = rs_ref[...] * (dxh - jnp.mean(dxh, axis=-1, keepdims=True)
                              - xh * jnp.mean(dxh * xh, axis=-1, keepdims=True))
        dy_ref[...] = ((1.0 + gt_ref[...]) * dres).astype(MXU_DTYPE)
        dxa_ref[...] = ALPHA * dres
        dg_ref[...] += _colsum(dxn_t * xh)
        db_ref[...] += _colsum(dxn_t)
        dgt_ref[...] += _colsum(dres * y_ref[...])

    vec = SDS((1, d), F32)
    return pl.pallas_call(
        body, name="ln_bwd", out_shape=(SDS((s, d), MXU_DTYPE), SDS((s, d), F32), vec, vec, vec), grid=(s // tb,),
        in_specs=[_rows(tb, d), _rows(tb, d), _rows(tb, d), _rows(tb, 1), _const((1, d)), _const((1, d))],
        out_specs=(_rows(tb, d), _rows(tb, d), _const((1, d)), _const((1, d)), _const((1, d))),
        compiler_params=_params(1))(dxn, xhat, y, rstd, ln_g, gate)


def _mod_bwd(dh, dxa, x, scale, tb):
    s, d = dh.shape

    def body(dh_ref, dxa_ref, x_ref, sc_ref, dx_ref, dsh_ref, dsc_ref):
        _init_acc(dsh_ref, dsc_ref)
        dh_t = dh_ref[...]
        dx_ref[...] = dxa_ref[...] + dh_t * (1.0 + sc_ref[...])
        dsh_ref[...] += _colsum(dh_t)
        dsc_ref[...] += _colsum(dh_t * x_ref[...])

    vec = SDS((1, d), F32)
    return pl.pallas_call(
        body, name="mod_bwd", out_shape=(SDS((s, d), F32), vec, vec), grid=(s // tb,),
        in_specs=[_rows(tb, d), _rows(tb, d), _rows(tb, d), _const((1, d))],
        out_specs=(_rows(tb, d), _const((1, d)), _const((1, d))), compiler_params=_params(1))(dh, dxa, x, scale)


def _loss_head(y, target, tb):
    s, d = y.shape

    def body(y_ref, t_ref, l_ref, dy_ref):
        _init_acc(l_ref)
        err = y_ref[...] - t_ref[...]
        l_ref[...] += (0.5 / d) * jnp.sum(err * err, keepdims=True)
        dy_ref[...] = err * (1.0 / d)

    return pl.pallas_call(body, name="loss_head", out_shape=(SDS((1, 1), F32), SDS((s, d), F32)), grid=(s // tb,),
                          in_specs=[_rows(tb, d), _rows(tb, d)], out_specs=(_const((1, 1)), _rows(tb, d)),
                          compiler_params=_params(1))(y, target)


def _conv_taps(u, up, w_ref, width):
    out = w_ref[width - 1:width, :] * u
    for j in range(width - 2, -1, -1):
        out = out + w_ref[j:j + 1, :] * _shift_down(u, up, width - 1 - j)
    return out


def _conv_taps_t(g, gn, w_ref, width):
    out = w_ref[width - 1:width, :] * g
    for j in range(width - 2, -1, -1):
        out = out + w_ref[j:j + 1, :] * _shift_up(g, gn, width - 1 - j)
    return out


def _conv_wgrad(dw_ref, g, u, up, width):
    dw_ref[width - 1:width, :] += _colsum(g * u)
    for j in range(width - 1):
        dw_ref[j:j + 1, :] += _colsum(g * _shift_down(u, up, width - 1 - j))


def _branch_a_fwd(proj, conv_w, tb):
    s = proj.shape[0]

    def body(ab, ac, ax, ag, acp, axp, w_ref, o_ref):
        has_prev = (pl.program_id(0) > 0).astype(F32)
        u = ac[...] * ax[...]
        up = acp[...] * axp[...] * has_prev
        o_ref[...] = (ab[...] * _conv_taps(u, up, w_ref, 3) * _silu(ag[...])).astype(MXU_DTYPE)

    return pl.pallas_call(
        body, name="branch_a_fwd", out_shape=SDS((s, BR), MXU_DTYPE), grid=(s // tb,),
        in_specs=[_rows(tb, BR, CB_AB), _rows(tb, BR, CB_AC), _rows(tb, BR, CB_AX), _rows(tb, BR, CB_AG),
                  _prev8(tb, BR, CB_AC), _prev8(tb, BR, CB_AX), _const((8, BR))],
        out_specs=_rows(tb, BR), compiler_params=_params(1))(proj, proj, proj, proj, proj, proj, conv_w)


def _branch_a_bwd(dycat, proj, conv_w, tb):
    s = proj.shape[0]

    def body(dy, dyn, ab, abn, ag, agn, ac, acp, ax, axp, w_ref, o_ref, dw_ref):
        _init_acc(dw_ref)
        i = pl.program_id(0)
        has_prev = (i > 0).astype(F32)
        has_next = (i < pl.num_programs(0) - 1).astype(F32)
        u = ac[...] * ax[...]
        up = acp[...] * axp[...] * has_prev
        v = _conv_taps(u, up, w_ref, 3)
        sg = _silu(ag[...])
        dv = dy[...] * ab[...] * sg
        dvn = dyn[...] * abn[...] * _silu(agn[...]) * has_next
        du = _conv_taps_t(dv, dvn, w_ref, 3)
        o_ref[:, 0:BR] = (dy[...] * v * sg).astype(MXU_DTYPE)
        o_ref[:, BR:2 * BR] = (du * ax[...]).astype(MXU_DTYPE)
        o_ref[:, 2 * BR:3 * BR] = (du * ac[...]).astype(MXU_DTYPE)
        o_ref[:, 3 * BR:4 * BR] = (dy[...] * ab[...] * v * _dsilu(ag[...])).astype(MXU_DTYPE)
        _conv_wgrad(dw_ref, dv, u, up, 3)

    return pl.pallas_call(
        body, name="branch_a_bwd", out_shape=(SDS((s, 4 * BR), MXU_DTYPE), SDS((8, BR), F32)), grid=(s // tb,),
        in_specs=[_rows(tb, BR, 0), _next8(tb, BR, s, 0),
                  _rows(tb, BR, CB_AB), _next8(tb, BR, s, CB_AB), _rows(tb, BR, CB_AG), _next8(tb, BR, s, CB_AG),
                  _rows(tb, BR, CB_AC), _prev8(tb, BR, CB_AC), _rows(tb, BR, CB_AX), _prev8(tb, BR, CB_AX),
                  _const((8, BR))],
        out_specs=(_rows(tb, 4 * BR), _const((8, BR))), compiler_params=_params(1),
    )(dycat, dycat, proj, proj, proj, proj, proj, proj, proj, proj, conv_w)


def _t5_bucket(dist):
    max_exact = REL_BUCKETS // 2
    nf = jnp.maximum(dist, 1).astype(F32)
    large = max_exact + (jnp.log(nf / max_exact) / math.log(REL_MAX_DIST / max_exact)
                         * (REL_BUCKETS - max_exact)).astype(jnp.int32)
    large = jnp.minimum(large, REL_BUCKETS - 1)
    return jnp.where(dist < max_exact, dist, large)


def _bucket_maps():
    maps = []
    i = jnp.arange(BLK)[:, None]
    j = jnp.arange(2 * BLK)[None, :]
    delta = i + BLK - j
    for window, dil in DILATIONS:
        span = window // dil
        bucket = _t5_bucket(jnp.clip(delta, 0, span) * dil)
        maps.append(jnp.where((delta >= 0) & (delta <= span), bucket, -1))
    return jnp.stack(maps).astype(jnp.int32)


def _bias_tables(rel_bias, buckets):
    n_pat = len(DILATIONS)

    def body(rb_ref, bk_ref, o_ref):
        for g in range(n_pat):
            bk = bk_ref[g]
            for h in range(ATT_HEADS):
                def per_bucket(b, acc):
                    return jnp.where(bk == b, rb_ref[b, h], acc)
                o_ref[g, h] = lax.fori_loop(0, REL_BUCKETS, per_bucket, jnp.full((BLK, 2 * BLK), NEG, F32))

    return pl.pallas_call(
        body, name="bias_tables", out_shape=SDS((n_pat, ATT_HEADS, BLK, 2 * BLK), F32),
        in_specs=[pl.BlockSpec(memory_space=pltpu.SMEM), pl.BlockSpec(memory_space=pltpu.VMEM)],
        compiler_params=_params())(rel_bias, buckets)


def _head_masks():
    lane = lax.broadcasted_iota(jnp.int32, (1, 2 * HEAD_DIM), 1)
    return [(lane < HEAD_DIM).astype(F32), (lane >= HEAD_DIM).astype(F32)]


def _strided(base, size, dil):
    return pl.ds(base, size, stride=dil) if dil > 1 else pl.ds(pl.multiple_of(base, BLK), size)


def _attn_groups(s, dil):
    return max(1, min(1024, s) // (dil * BLK)) if dil == 1 else max(1, min(2048, s) // (dil * BLK))


def _attn_fwd(proj, bias, dil):
    s = proj.shape[0]
    grp = _attn_groups(s, dil)
    u1 = dil * BLK
    unit = grp * u1
    nb = s // unit
    w = 2 * HEAD_DIM
    q0, k0, v0 = (cb * (BR // w) for cb in (CB_Q, CB_K, CB_V))

    def body(q_ref, kc_ref, kp_ref, vc_ref, vp_ref, bias_ref, o_ref, lse_ref, kbuf, vbuf):
        n = pl.program_id(1)
        col = lax.broadcasted_iota(jnp.int32, (1, 2 * BLK), 1)
        masks = _head_masks()
        kbuf[0:u1, :] = kp_ref[...]
        kbuf[u1:, :] = kc_ref[...]
        vbuf[0:u1, :] = vp_ref[...]
        vbuf[u1:, :] = vc_ref[...]

        def per_r(t, carry):
            j = t // dil
            base = j * u1 + t % dil
            rows = _strided(base, BLK, dil)
            no_prev = jnp.where((n == 0) & (j == 0) & (col < BLK), NEG, 0.0)
            q = q_ref[rows, :] * (HEAD_DIM ** -0.5)
            k = kbuf[_strided(base, 2 * BLK, dil), :].astype(MXU_DTYPE)
            v = vbuf[_strided(base, 2 * BLK, dil), :].astype(MXU_DTYPE)
            q2 = jnp.concatenate([q * masks[0], q * masks[1]], axis=0).astype(MXU_DTYPE)
            sc = lax.dot_general(q2, k, (((1,), (1,)), ((), ())), preferred_element_type=F32)
            sc = sc + jnp.concatenate([bias_ref[0], bias_ref[1]], axis=0) + no_prev
            mx = jnp.max(sc, axis=-1, keepdims=True)
            p = jnp.exp(sc - mx)
            l = jnp.sum(p, axis=-1, keepdims=True)
            o2 = jnp.dot((p / l).astype(MXU_DTYPE), v, preferred_element_type=F32)
            lse2 = mx + jnp.log(l)
            o_ref[rows, :] = o2[0:BLK] * masks[0] + o2[BLK:2 * BLK] * masks[1]
            lse_ref[rows, :] = lse2[0:BLK] * masks[0] + lse2[BLK:2 * BLK] * masks[1]
            return carry

        lax.fori_loop(0, grp * dil, per_r, 0, unroll=8)

    cur = lambda c0: pl.BlockSpec((unit, w), lambda hp, n: (n, c0 + hp))
    prev = lambda c0: pl.BlockSpec((u1, w), lambda hp, n: (jnp.maximum(n * grp - 1, 0), c0 + hp))
    out = pl.BlockSpec((unit, w), lambda hp, n: (n, hp))
    return pl.pallas_call(
        body, name=f"attn_fwd_d{dil}", out_shape=(SDS((s, BR), F32), SDS((s, BR), F32)), grid=(BR // w, nb),
        in_specs=[cur(q0), cur(k0), prev(k0), cur(v0), prev(v0),
                  pl.BlockSpec((2, BLK, 2 * BLK), lambda hp, n: (hp, 0, 0))],
        out_specs=(out, out),
        scratch_shapes=[pltpu.VMEM((unit + u1, w), F32), pltpu.VMEM((unit + u1, w), F32)],
        compiler_params=_params(2))(proj, proj, proj, proj, proj, bias)


def _softmax3(l0, l1, l2):
    mx = jnp.maximum(jnp.maximum(l0, l1), l2)
    e0, e1, e2 = jnp.exp(l0 - mx), jnp.exp(l1 - mx), jnp.exp(l2 - mx)
    inv = 1.0 / (e0 + e1 + e2)
    return e0 * inv, e1 * inv, e2 * inv


def _attn_combine(os_, lses, proj, tb):
    s = proj.shape[0]

    def body(o0, o1, o2, l0, l1, l2, bg, y_ref):
        w0, w1, w2 = _softmax3(l0[...], l1[...], l2[...])
        attn = w0 * o0[...] + w1 * o1[...] + w2 * o2[...]
        y_ref[...] = (attn * _silu(bg[...])).astype(MXU_DTYPE)

    return pl.pallas_call(
        body, name="attn_combine", out_shape=SDS((s, BR), MXU_DTYPE), grid=(s // tb,),
        in_specs=[_rows(tb, BR)] * 6 + [_rows(tb, BR, CB_BG)], out_specs=_rows(tb, BR),
        compiler_params=_params(1))(*os_, *lses, proj)


def _attn_bwd_pre(dycat, os_, lses, proj, head_ones, tb):
    s = proj.shape[0]

    def body(dy, o0, o1, o2, l0, l1, l2, bg, e_ref, dbg_ref, do0, do1, do2, dm0, dm1, dm2):
        w0, w1, w2 = _softmax3(l0[...], l1[...], l2[...])
        attn = w0 * o0[...] + w1 * o1[...] + w2 * o2[...]
        dattn = dy[...] * _silu(bg[...])
        dbg_ref[...] = dy[...] * attn * _dsilu(bg[...])
        prod = dattn * attn
        hi = prod.astype(MXU_DTYPE)
        lo = (prod - hi.astype(F32)).astype(MXU_DTYPE)
        tot = (jnp.dot(hi, e_ref[...], preferred_element_type=F32)
               + jnp.dot(lo, e_ref[...], preferred_element_type=F32))
        for wg, do_ref, dm_ref in ((w0, do0, dm0), (w1, do1, dm1), (w2, do2, dm2)):
            do_ref[...] = wg * dattn
            dm_ref[...] = wg * tot

    big = SDS((s, BR), F32)
    return pl.pallas_call(
        body, name="attn_bwd_pre", out_shape=(big,) * 7, grid=(s // tb,),
        in_specs=[_rows(tb, BR, 1)] + [_rows(tb, BR)] * 6 + [_rows(tb, BR, CB_BG), _const((BR, BR))],
        out_specs=(_rows(tb, BR),) * 7, compiler_params=_params(1))(dycat, *os_, *lses, proj, head_ones)


def _attn_bwd(proj, do, lse, dm, bias, dil, carry=None):
    s = proj.shape[0]
    grp = _attn_groups(s, dil)
    u1 = dil * BLK
    unit = grp * u1
    nb = s // unit
    w = 2 * HEAD_DIM
    q0, k0, v0 = (cb * (BR // w) for cb in (CB_Q, CB_K, CB_V))

    def body(q_ref, kc_ref, kp_ref, vc_ref, vp_ref, do_ref, lse_ref, dm_ref, bias_ref,
             dq_ref, dk_ref, dv_ref, dbias_ref, kbuf, vbuf, stage_k, stage_v):
        n = pl.program_id(1)
        col = lax.broadcasted_iota(jnp.int32, (1, 2 * BLK), 1)
        masks = _head_masks()

        @pl.when(n == 0)
        def _():
            dbias_ref[...] = jnp.zeros_like(dbias_ref)
            stage_k[...] = jnp.zeros_like(stage_k)
            stage_v[...] = jnp.zeros_like(stage_v)

        for out_ref, stage in ((dk_ref, stage_k), (dv_ref, stage_v)):
            if grp > 1:
                out_ref[0:unit - u1, :] = stage[u1:unit, :]
            stage[0:u1, :] = stage[unit:unit + u1, :]

        @pl.when(n < nb)
        def _():
            kbuf[0:u1, :] = kp_ref[...]
            kbuf[u1:, :] = kc_ref[...]
            vbuf[0:u1, :] = vp_ref[...]
            vbuf[u1:, :] = vc_ref[...]

            def per_r(t, carry):
                j = t // dil
                base = j * u1 + t % dil
                rows = _strided(base, BLK, dil)
                rows_hi = _strided(base + u1, BLK, dil)
                no_prev = jnp.where((n == 0) & (j == 0) & (col < BLK), NEG, 0.0)
                q = q_ref[rows, :] * (HEAD_DIM ** -0.5)
                k = kbuf[_strided(base, 2 * BLK, dil), :].astype(MXU_DTYPE)
                v = vbuf[_strided(base, 2 * BLK, dil), :].astype(MXU_DTYPE)
                do_t, lse_t, dm_t = do_ref[rows, :], lse_ref[rows, :], dm_ref[rows, :]
                stack = lambda t: jnp.concatenate([t * masks[0], t * masks[1]], axis=0).astype(MXU_DTYPE)
                per_head = lambda t: jnp.concatenate([t[:, 0:1], t[:, HEAD_DIM:HEAD_DIM + 1]], axis=0)
                q2, do2 = stack(q), stack(do_t)
                sc = lax.dot_general(q2, k, (((1,), (1,)), ((), ())), preferred_element_type=F32)
                p = jnp.exp(sc + jnp.concatenate([bias_ref[0], bias_ref[1]], axis=0) + no_prev - per_head(lse_t))
                dp = lax.dot_general(do2, v, (((1,), (1,)), ((), ())), preferred_element_type=F32)
                ds = p * (dp - per_head(dm_t))
                dbias_ref[0] += ds[0:BLK]
                dbias_ref[1] += ds[BLK:2 * BLK]
                dsb, pb = ds.astype(MXU_DTYPE), p.astype(MXU_DTYPE)
                dq2 = jnp.dot(dsb, k, preferred_element_type=F32)
                dk_acc = lax.dot_general(dsb, q2, (((0,), (0,)), ((), ())), preferred_element_type=F32)
                dv_acc = lax.dot_general(pb, do2, (((0,), (0,)), ((), ())), preferred_element_type=F32)
                dq_ref[rows, :] = (dq2[0:BLK] * masks[0] + dq2[BLK:2 * BLK] * masks[1]) * (HEAD_DIM ** -0.5)
                stage_k[rows, :] = stage_k[rows, :] + dk_acc[0:BLK]
                stage_v[rows, :] = stage_v[rows, :] + dv_acc[0:BLK]
                stage_k[rows_hi, :] = dk_acc[BLK:2 * BLK]
                stage_v[rows_hi, :] = dv_acc[BLK:2 * BLK]
                return carry

            lax.fori_loop(0, grp * dil, per_r, 0, unroll=8)

        dk_ref[unit - u1:unit, :] = stage_k[0:u1, :]
        dv_ref[unit - u1:unit, :] = stage_v[0:u1, :]

    qn = lambda n: jnp.minimum(n, nb - 1)
    cur = lambda c0: pl.BlockSpec((unit, w), lambda hp, n: (qn(n), c0 + hp))
    prev = lambda c0: pl.BlockSpec((u1, w), lambda hp, n: (jnp.maximum(qn(n) * grp - 1, 0), c0 + hp))
    row = pl.BlockSpec((unit, w), lambda hp, n: (qn(n), hp))
    late = pl.BlockSpec((unit, w), lambda hp, n: (jnp.maximum(n - 1, 0), hp))
    tab = pl.BlockSpec((2, BLK, 2 * BLK), lambda hp, n: (hp, 0, 0))
    big = SDS((s, BR), F32)
    return _call(
        body, name=f"attn_bwd_d{dil}", out_shape=(big, big, big, SDS((ATT_HEADS, BLK, 2 * BLK), F32)),
        grid=(BR // w, nb + 1),
        in_specs=[cur(q0), cur(k0), prev(k0), cur(v0), prev(v0), row, row, row, tab],
        out_specs=(row, late, late, tab),
        scratch_shapes=[pltpu.VMEM((unit + u1, w), F32)] * 4,
        args=(proj, proj, proj, proj, proj, do, lse, dm, bias), carry=carry)


def _rel_bias_grad(dbias, buckets):
    def body(db_ref, bk_ref, o_ref):
        row = lax.broadcasted_iota(jnp.int32, (REL_BUCKETS, 128), 0)
        lane = lax.broadcasted_iota(jnp.int32, (REL_BUCKETS, 128), 1)

        def per_bucket(b, acc):
            for g in range(len(DILATIONS)):
                hit = bk_ref[g] == b
                for h in range(ATT_HEADS):
                    both = db_ref[0, g, h] + db_ref[1, g, h]
                    val = jnp.sum(jnp.where(hit, both, 0.0), keepdims=True)
                    acc = acc + jnp.where((row == b) & (lane == h), val, 0.0)
            return acc

        o_ref[...] = lax.fori_loop(0, REL_BUCKETS, per_bucket, jnp.zeros((REL_BUCKETS, 128), F32))

    assert dbias.shape[0] == DEPTH == 2
    return pl.pallas_call(body, name="rel_bias_grad", out_shape=SDS((REL_BUCKETS, 128), F32),
                          compiler_params=_params())(dbias, buckets)


def _scan_real(a, b, *, reverse, tb, name):
    s, ch = a.shape
    nt = s // tb
    order = range(7, -1, -1) if reverse else range(8)

    def body(a_ref, b_ref, o_ref, carry):
        @pl.when(pl.program_id(0) == 0)
        def _():
            carry[...] = jnp.zeros_like(carry)

        def group(gi, h):
            r0 = pl.multiple_of((tb // 8 - 1 - gi if reverse else gi) * 8, 8)
            a8, b8 = a_ref[pl.ds(r0, 8), :], b_ref[pl.ds(r0, 8), :]
            rows = [None] * 8
            for k in order:
                if reverse:
                    rows[k] = b8[k:k + 1] + h
                    h = a8[k:k + 1] * rows[k]
                else:
                    h = a8[k:k + 1] * h + b8[k:k + 1]
                    rows[k] = h
            o_ref[pl.ds(r0, 8), :] = jnp.concatenate(rows, axis=0)
            return h

        carry[...] = lax.fori_loop(0, tb // 8, group, carry[...])

    spec = pl.BlockSpec((tb, ch), (lambda i: (nt - 1 - i, 0)) if reverse else (lambda i: (i, 0)))
    return pl.pallas_call(body, name=name, out_shape=SDS((s, ch), F32), grid=(nt,), in_specs=[spec, spec],
                          out_specs=spec, scratch_shapes=[pltpu.VMEM((1, ch), F32)],
                          compiler_params=_params(1))(a, b)


def _scan_tile(s):
    return min(512, s)


def _load_chunked(ref, t0, pt):
    ln = pt // 8
    return jnp.concatenate([ref[pl.ds(t0 + j, 8, stride=ln), :] for j in range(ln)], axis=0)


def _store_natural(ref, t0, pt, val):
    ln = pt // 8
    for j in range(ln):
        ref[pl.ds(t0 + j, 8, stride=ln), :] = val[j * 8:(j + 1) * 8]


def _scan_cplx(b, a_row, *, reverse, tb, name):
    s, ch2 = b.shape
    ch = ch2 // 2
    nt = s // tb
    ln = tb // 8

    def cmul(pr, pi, xr, xi):
        return pr * xr - pi * xi, pr * xi + pi * xr

    def body(a_ref, b_ref, o_ref, carry, pw):
        ar = a_ref[:, 0:ch]
        ai = -a_ref[:, ch:ch2] if reverse else a_ref[:, ch:ch2]

        @pl.when(pl.program_id(0) == 0)
        def _():
            carry[...] = jnp.zeros_like(carry)

            def fill(j, p):
                pw[pl.ds(j, 1), 0:ch] = p[0]
                pw[pl.ds(j, 1), ch:ch2] = p[1]
                return cmul(ar, ai, *p)

            lax.fori_loop(0, ln, fill, (ar, ai))

        def rows_of(j):
            return pl.ds(pl.multiple_of((ln - 1 - j if reverse else j) * 8, 8), 8)

        def local(j, x):
            rows = rows_of(j)
            nr, ni = cmul(ar, ai, *x)
            xr, xi = nr + b_ref[rows, 0:ch], ni + b_ref[rows, ch:ch2]
            o_ref[rows, 0:ch] = xr
            o_ref[rows, ch:ch2] = xi
            return xr, xi

        zero = jnp.zeros((8, ch), F32)
        er, ei = lax.fori_loop(0, ln, local, (zero, zero), unroll=2)
        apr, api = pw[ln - 1:ln, 0:ch], pw[ln - 1:ln, ch:ch2]
        cr, ci = carry[:, 0:ch], carry[:, ch:ch2]
        into_r, into_i = [None] * 8, [None] * 8
        for c in (range(7, -1, -1) if reverse else range(8)):
            into_r[c], into_i[c] = cr, ci
            pr, pi = cmul(apr, api, cr, ci)
            cr, ci = er[c:c + 1] + pr, ei[c:c + 1] + pi
        carry[:, 0:ch] = cr
        carry[:, ch:ch2] = ci
        into_r, into_i = jnp.concatenate(into_r, axis=0), jnp.concatenate(into_i, axis=0)

        def fix(j, carry_):
            rows = rows_of(j)
            dr, di = cmul(pw[pl.ds(j, 1), 0:ch], pw[pl.ds(j, 1), ch:ch2], into_r, into_i)
            o_ref[rows, 0:ch] += dr
            o_ref[rows, ch:ch2] += di
            return carry_

        lax.fori_loop(0, ln, fix, 0, unroll=2)

    spec = pl.BlockSpec((tb, ch2), (lambda i: (nt - 1 - i, 0)) if reverse else (lambda i: (i, 0)))
    return pl.pallas_call(body, name=name, out_shape=SDS((s, ch2), F32), grid=(nt,),
                          in_specs=[_const((1, ch2)), spec], out_specs=spec,
                          scratch_shapes=[pltpu.VMEM((1, ch2), F32), pltpu.VMEM((ln, ch2), F32)],
                          compiler_params=_params(1))(a_row, b)


def _neg_expm1(z):
    series = -z * (1.0 + z * (0.5 + z * (1.0 / 6 + z * (1.0 / 24 + z * (1.0 / 120)))))
    return jnp.where(z > -0.05, series, 1.0 - jnp.exp(z))


def _lru_gate(xc, pre_r, pre_i, lam):
    log_a = -LRU_C * jax.nn.sigmoid(pre_r) * jax.nn.softplus(-lam)
    return jnp.exp(log_a), jnp.sqrt(_neg_expm1(2.0 * log_a)) * jax.nn.sigmoid(pre_i) * xc


def _lru_gates_fwd(proj, conv_w, conv_b, w_cat, b_cat, lam, tb):
    s = proj.shape[0]

    def body(cx, cxp, w_ref, cb_ref, wc_ref, bc_ref, lam_ref, a_ref, b_ref):
        has_prev = (pl.program_id(0) > 0).astype(F32)
        xc = _conv_taps(cx[...], cxp[...] * has_prev, w_ref, 4) + cb_ref[...]
        pre = jnp.dot(xc.astype(MXU_DTYPE), wc_ref[...], preferred_element_type=F32) + bc_ref[...]
        a_ref[...], b_ref[...] = _lru_gate(xc, pre[:, 0:BR], pre[:, BR:2 * BR], lam_ref[...])

    big = SDS((s, BR), F32)
    return pl.pallas_call(
        body, name="lru_gates_fwd", out_shape=(big, big), grid=(s // tb,),
        in_specs=[_rows(tb, BR, CB_CX), _prev8(tb, BR, CB_CX), _const((8, BR)), _const((1, BR)),
                  _const((BR, 2 * BR)), _const((1, 2 * BR)), _const((1, BR))],
        out_specs=(_rows(tb, BR), _rows(tb, BR)), compiler_params=_params(1),
    )(proj, proj, conv_w, conv_b, w_cat, b_cat, lam)


def _gate_out(h, proj, cb, tb, name):
    s = proj.shape[0]

    def body(h_ref, g_ref, o_ref):
        o_ref[...] = (h_ref[...] * _silu(g_ref[...])).astype(MXU_DTYPE)

    return pl.pallas_call(body, name=name, out_shape=SDS((s, BR), MXU_DTYPE), grid=(s // tb,),
                          in_specs=[_rows(tb, BR), _rows(tb, BR, cb)], out_specs=_rows(tb, BR),
                          compiler_params=_params(1))(h, proj)


def _gate_out_bwd(dycat, dy_cb, h, proj, cb, tb, name):
    s = proj.shape[0]

    def body(dy, h_ref, g_ref, dh_ref, dg_ref):
        dh_ref[...] = dy[...] * _silu(g_ref[...])
        dg_ref[...] = dy[...] * h_ref[...] * _dsilu(g_ref[...])

    big = SDS((s, BR), F32)
    return pl.pallas_call(body, name=name, out_shape=(big, big), grid=(s // tb,),
                          in_specs=[_rows(tb, BR, dy_cb), _rows(tb, BR), _rows(tb, BR, cb)],
                          out_specs=(_rows(tb, BR), _rows(tb, BR)), compiler_params=_params(1))(dycat, h, proj)


def _lru_gates_bwd(proj, lmb, h, conv_w, conv_b, w_cat, b_cat, lam, tb):
    s = proj.shape[0]

    def body(cx, cxp, l_ref, h_ref, hp_ref, w_ref, cb_ref, wc_ref, bc_ref, lam_ref,
             dxc_ref, dpre_ref, xc_ref, dbc_ref, dlam_ref):
        _init_acc(dbc_ref, dlam_ref)
        has_prev = (pl.program_id(0) > 0).astype(F32)
        xc = _conv_taps(cx[...], cxp[...] * has_prev, w_ref, 4) + cb_ref[...]
        xcb = xc.astype(MXU_DTYPE)
        pre = jnp.dot(xcb, wc_ref[...], preferred_element_type=F32) + bc_ref[...]
        _, vjp = jax.vjp(_lru_gate, xc, pre[:, 0:BR], pre[:, BR:2 * BR], lam_ref[...])
        lm = l_ref[...]
        dxc, dpr, dpi, dlam = vjp((lm * _shift_down(h_ref[...], hp_ref[...] * has_prev, 1), lm))
        dpre = jnp.concatenate([dpr, dpi], axis=1)
        dpreb = dpre.astype(MXU_DTYPE)
        dxc_ref[...] = dxc + lax.dot_general(dpreb, wc_ref[...], (((1,), (1,)), ((), ())),
                                             preferred_element_type=F32)
        dpre_ref[...] = dpreb
        xc_ref[...] = xcb
        dbc_ref[...] += _colsum(dpre)
        dlam_ref[...] += dlam

    return pl.pallas_call(
        body, name="lru_gates_bwd",
        out_shape=(SDS((s, BR), F32), SDS((s, 2 * BR), MXU_DTYPE), SDS((s, BR), MXU_DTYPE),
                   SDS((1, 2 * BR), F32), SDS((1, BR), F32)),
        grid=(s // tb,),
        in_specs=[_rows(tb, BR, CB_CX), _prev8(tb, BR, CB_CX), _rows(tb, BR), _rows(tb, BR), _prev8(tb, BR),
                  _const((8, BR)), _const((1, BR)), _const((BR, 2 * BR)), _const((1, 2 * BR)), _const((1, BR))],
        out_specs=(_rows(tb, BR), _rows(tb, 2 * BR), _rows(tb, BR), _const((1, 2 * BR)), _const((1, BR))),
        compiler_params=_params(1))(proj, proj, lmb, h, h, conv_w, conv_b, w_cat, b_cat, lam)


def _conv_c_bwd(dxc, proj, conv_w, tb):
    s = proj.shape[0]

    def body(g, gn, cx, cxp, w_ref, dcx_ref, dw_ref, db_ref):
        _init_acc(dw_ref, db_ref)
        i = pl.program_id(0)
        has_prev = (i > 0).astype(F32)
        has_next = (i < pl.num_programs(0) - 1).astype(F32)
        gt = g[...]
        dcx_ref[...] = _conv_taps_t(gt, gn[...] * has_next, w_ref, 4)
        _conv_wgrad(dw_ref, gt, cx[...], cxp[...] * has_prev, 4)
        db_ref[...] += _colsum(gt)

    return pl.pallas_call(
        body, name="conv_c_bwd", out_shape=(SDS((s, BR), F32), SDS((8, BR), F32), SDS((1, BR), F32)),
        grid=(s // tb,),
        in_specs=[_rows(tb, BR), _next8(tb, BR, s), _rows(tb, BR, CB_CX), _prev8(tb, BR, CB_CX), _const((8, BR))],
        out_specs=(_rows(tb, BR), _const((8, BR)), _const((1, BR))), compiler_params=_params(1),
    )(dxc, dxc, proj, proj, conv_w)


def _s5_disc(lam_re, lam_im, log_dt):
    dt = jnp.exp(log_dt)
    mag = jnp.exp(lam_re * dt)
    ab_re = mag * jnp.cos(lam_im * dt)
    ab_im = mag * jnp.sin(lam_im * dt)
    den = lam_re * lam_re + lam_im * lam_im
    f_re = ((ab_re - 1.0) * lam_re + ab_im * lam_im) / den
    f_im = (ab_im * lam_re - (ab_re - 1.0) * lam_im) / den
    return ab_re, ab_im, f_re, f_im


def _s5_bbar(f_re, f_im, b_re, b_im):
    return f_re * b_re - f_im * b_im, f_re * b_im + f_im * b_re


def _s5_disc_fwd(lam_re, lam_im, log_dt):
    def body(lr, li, ld, o0, o1, o2, o3):
        o0[...], o1[...], o2[...], o3[...] = _s5_disc(lr[...], li[...], ld[...])
    return pl.pallas_call(body, name="s5_disc_fwd", out_shape=(SDS(lam_re.shape, F32),) * 4)(lam_re, lam_im, log_dt)


def _s5_disc_bwd(lam_re, lam_im, log_dt, cts):
    def body(lr, li, ld, c0, c1, c2, c3, o0, o1, o2):
        _, vjp = jax.vjp(_s5_disc, lr[...], li[...], ld[...])
        o0[...], o1[...], o2[...] = vjp((c0[...], c1[...], c2[...], c3[...]))
    return pl.pallas_call(body, name="s5_disc_bwd", out_shape=(SDS(lam_re.shape, F32), SDS(lam_re.shape, F32),
                                                                SDS(log_dt.shape, F32)))(lam_re, lam_im, log_dt, *cts)


def _s5_bbar_fwd(f_re, f_im, b_re, b_im):
    def body(fr, fi, br, bi, o0, o1):
        o0[...], o1[...] = _s5_bbar(fr[...], fi[...], br[...], bi[...])
    return pl.pallas_call(body, name="s5_bbar_fwd", out_shape=(SDS(b_re.shape, F32),) * 2)(f_re, f_im, b_re, b_im)


def _s5_bbar_bwd(f_re, f_im, b_re, b_im, d_re, d_im):
    def body(fr, fi, br, bi, dr, di, o0, o1, o2, o3):
        _, vjp = jax.vjp(_s5_bbar, fr[...], fi[...], br[...], bi[...])
        o0[...], o1[...], o2[...], o3[...] = vjp((dr[...], di[...]))
    col, mat = SDS(f_re.shape, F32), SDS(b_re.shape, F32)
    return pl.pallas_call(body, name="s5_bbar_bwd", out_shape=(col, col, mat, mat))(f_re, f_im, b_re, b_im, d_re, d_im)


def _s5_tail_fwd(ylin, proj, d_skip, w_glu, b_glu, tb):
    s = proj.shape[0]

    def body(yl, u, dg, dk, w_ref, b_ref, o_ref):
        g = jax.nn.gelu(yl[...] + dk[...] * u[...])
        t = jnp.dot(g.astype(MXU_DTYPE), w_ref[...], preferred_element_type=F32) + b_ref[...]
        o_ref[...] = (g * jax.nn.sigmoid(t) * _silu(dg[...])).astype(MXU_DTYPE)

    return pl.pallas_call(
        body, name="s5_tail_fwd", out_shape=SDS((s, BR), MXU_DTYPE), grid=(s // tb,),
        in_specs=[_rows(tb, BR), _rows(tb, BR, CB_DU), _rows(tb, BR, CB_DG), _const((1, BR)), _const((BR, BR)),
                  _const((1, BR))],
        out_specs=_rows(tb, BR), compiler_params=_params(1))(ylin, proj, proj, d_skip, w_glu, b_glu)


def _s5_tail_bwd(dycat, ylin, proj, d_skip, w_glu, b_glu, tb):
    s = proj.shape[0]

    def body(dy, yl, u, dg, dk, w_ref, b_ref, dyl_ref, dus_ref, ddg_ref, g_ref, dt_ref, ddk_ref, dbg_ref):
        _init_acc(ddk_ref, dbg_ref)
        g, gelu_vjp = jax.vjp(jax.nn.gelu, yl[...] + dk[...] * u[...])
        gb = g.astype(MXU_DTYPE)
        sg = jax.nn.sigmoid(jnp.dot(gb, w_ref[...], preferred_element_type=F32) + b_ref[...])
        dz = dy[...] * _silu(dg[...])
        ddg_ref[...] = dy[...] * g * sg * _dsilu(dg[...])
        dt = dz * g * sg * (1.0 - sg)
        dtb = dt.astype(MXU_DTYPE)
        dgel = dz * sg + lax.dot_general(dtb, w_ref[...], (((1,), (1,)), ((), ())), preferred_element_type=F32)
        dyv, = gelu_vjp(dgel)
        dyl_ref[...] = dyv
        dus_ref[...] = dyv * dk[...]
        g_ref[...] = gb
        dt_ref[...] = dtb
        ddk_ref[...] += _colsum(dyv * u[...])
        dbg_ref[...] += _colsum(dt)

    big, half, vec = SDS((s, BR), F32), SDS((s, BR), MXU_DTYPE), SDS((1, BR), F32)
    return pl.pallas_call(
        body, name="s5_tail_bwd", out_shape=(big, big, big, half, half, vec, vec), grid=(s // tb,),
        in_specs=[_rows(tb, BR, 3), _rows(tb, BR), _rows(tb, BR, CB_DU), _rows(tb, BR, CB_DG), _const((1, BR)),
                  _const((BR, BR)), _const((1, BR))],
        out_specs=(_rows(tb, BR),) * 5 + (_const((1, BR)), _const((1, BR))), compiler_params=_params(1),
    )(dycat, ylin, proj, proj, d_skip, w_glu, b_glu)


def _s5_da(lmb, x, tb):
    s, ch2 = x.shape
    ch = ch2 // 2
    assert tb == _scan_tile(s)

    def body(l_ref, x_ref, xp_ref, o_ref):
        _init_acc(o_ref)
        has_prev = (pl.program_id(0) > 0).astype(F32)
        row = lax.broadcasted_iota(jnp.int32, (8, ch2), 0)
        first = jnp.where(row == 0, pltpu.roll(xp_ref[...], 1, 0) * has_prev, pltpu.roll(x_ref[tb - 8:tb, :], 1, 0))
        xprev = jnp.concatenate([first, x_ref[0:tb - 8, :]], axis=0)
        lr, li, xr, xi = l_ref[:, 0:ch], l_ref[:, ch:ch2], xprev[:, 0:ch], xprev[:, ch:ch2]
        o_ref[:, 0:ch] += _colsum(lr * xr + li * xi)
        o_ref[:, ch:ch2] += _colsum(li * xr - lr * xi)

    return pl.pallas_call(body, name="s5_da", out_shape=SDS((1, ch2), F32), grid=(s // tb,),
                          in_specs=[_rows(tb, ch2), _rows(tb, ch2), _prev8(tb, ch2)], out_specs=_const((1, ch2)),
                          compiler_params=_params(1))(lmb, x, x)


def _assemble_dproj(da, dqkv, dbg, dcx, dcg, du, dus, ddg, tb):
    s = da.shape[0]

    def body(da_ref, q0, q1, q2, k0, k1, k2, v0, v1, v2, dbg_ref, dcx_ref, dcg_ref, du_ref, dus_ref, ddg_ref, o_ref):
        o_ref[:, 0:4 * BR] = da_ref[...]
        for j, parts in enumerate(((q0, q1, q2), (k0, k1, k2), (v0, v1, v2))):
            o_ref[:, (4 + j) * BR:(5 + j) * BR] = (parts[0][...] + parts[1][...] + parts[2][...]).astype(MXU_DTYPE)
        o_ref[:, 7 * BR:8 * BR] = dbg_ref[...].astype(MXU_DTYPE)
        o_ref[:, 8 * BR:9 * BR] = dcx_ref[...].astype(MXU_DTYPE)
        o_ref[:, 9 * BR:10 * BR] = dcg_ref[...].astype(MXU_DTYPE)
        o_ref[:, 10 * BR:11 * BR] = (du_ref[...] + dus_ref[...]).astype(MXU_DTYPE)
        o_ref[:, 11 * BR:12 * BR] = ddg_ref[...].astype(MXU_DTYPE)

    flat = [t for grp in dqkv for t in grp]
    return pl.pallas_call(
        body, name="assemble_dproj", out_shape=SDS((s, N_IN), MXU_DTYPE), grid=(s // tb,),
        in_specs=[_rows(tb, 4 * BR)] + [_rows(tb, BR)] * 15, out_specs=_rows(tb, N_IN),
        compiler_params=_params(1))(da, *flat, dbg, dcx, dcg, du, dus, ddg)


def _sum_leading(xs, tr, name):
    n, _, c = xs[0].shape
    nl = len(xs)
    tr = min([tr] + [x.shape[1] for x in xs])
    assert all(x.shape[1] % tr == 0 for x in xs), (name, tr)
    nrs = [x.shape[1] // tr for x in xs]
    starts = [sum(nrs[:l]) for l in range(nl)]

    def body(*refs):
        i = pl.program_id(0)
        for l in range(nl):
            @pl.when((i >= starts[l]) & (i < starts[l] + nrs[l]))
            def _():
                acc = refs[l * n][...].astype(F32)
                for ref in refs[l * n + 1:(l + 1) * n]:
                    acc = acc + ref[...].astype(F32)
                refs[nl * n][...] = acc

    specs = [pl.BlockSpec((None, tr, c), functools.partial(
        lambda i, k, l: (k, jnp.clip(i - starts[l], 0, nrs[l] - 1), 0), k=k, l=l)) for l in range(nl) for k in range(n)]
    return pl.pallas_call(body, name=name, out_shape=SDS((sum(nrs) * tr, c), F32), grid=(sum(nrs),), in_specs=specs,
                          out_specs=pl.BlockSpec((tr, c), lambda i: (i, 0)),
                          compiler_params=_params(1))(*[x for x in xs for _ in range(n)])


def _adamw(w, g_parts, m, v, tr, name):
    r, c = w.shape
    tr = min(tr, r)
    n = len(g_parts)
    assert r % tr == 0, (name, r, tr)

    def body(*refs):
        w_ref, m_ref, v_ref = refs[0], refs[1 + n], refs[2 + n]
        g_ref, d_ref, nm_ref, nv_ref = refs[3 + n:]
        g = refs[1][...]
        for ref in refs[2:1 + n]:
            g = g + ref[...]
        mm = ADAM_B1 * m_ref[...] + (1.0 - ADAM_B1) * g
        vv = ADAM_B2 * v_ref[...] + (1.0 - ADAM_B2) * jnp.square(g)
        m_hat = mm / (1.0 - ADAM_B1 ** ADAM_STEP)
        v_hat = vv / (1.0 - ADAM_B2 ** ADAM_STEP)
        g_ref[...] = g
        d_ref[...] = -ADAM_LR * (m_hat / (jnp.sqrt(v_hat) + ADAM_EPS) + ADAM_WD * w_ref[...])
        nm_ref[...] = mm
        nv_ref[...] = vv

    spec = pl.BlockSpec((tr, c), lambda i: (i, 0))
    return pl.pallas_call(body, name=name, out_shape=(SDS((r, c), F32),) * 4, grid=(r // tr,),
                          in_specs=[spec] * (3 + n), out_specs=(spec,) * 4,
                          compiler_params=_params(1))(w, *g_parts, m, v)


def _allgather8(block, name):
    m_per, n = block.shape

    def body(x_ref, out_ref, send_sems, recv_sems, local_sem):
        x, y, c = lax.axis_index("x"), lax.axis_index("y"), lax.axis_index("c")
        me, sibling = (x, y, c), (x, y, 1 - c)
        chips = [(1 - x, y), (x, 1 - y), (1 - x, 1 - y)]

        def rows(px, py, pc):
            return out_ref.at[pl.ds((4 * px + 2 * py + pc) * m_per, m_per), :]

        def copy(k, blk, to, src=None):
            return pltpu.make_async_remote_copy(
                src_ref=rows(*blk) if src is None else src, dst_ref=rows(*blk), send_sem=send_sems.at[k],
                recv_sem=recv_sems.at[k], device_id=to, device_id_type=MESH)

        mine = pltpu.make_async_copy(x_ref, rows(*me), local_sem)
        mine.start()
        first = [copy(0, me, sibling, src=x_ref)]
        first += [copy(1 + j, me, (*chip, c), src=x_ref) for j, chip in enumerate(chips)]
        for cp in first:
            cp.start()
        passed = [copy(4 + j, (*chip, c), sibling) for j, chip in enumerate(chips)]
        for j, chip in enumerate(chips):
            copy(1 + j, (*chip, c), me).wait_recv()
            passed[j].start()
        copy(0, sibling, me).wait_recv()
        for j, chip in enumerate(chips):
            copy(4 + j, (*chip, 1 - c), me).wait_recv()
        for cp in first + passed:
            cp.wait_send()
        mine.wait()

    return pl.pallas_call(
        body, name=name, out_shape=SDS((N_DEV * m_per, n), block.dtype),
        in_specs=[pl.BlockSpec(memory_space=pltpu.VMEM)], out_specs=pl.BlockSpec(memory_space=pltpu.VMEM),
        scratch_shapes=[pltpu.SemaphoreType.DMA((7,)), pltpu.SemaphoreType.DMA((7,)), pltpu.SemaphoreType.DMA],
        compiler_params=_params())(block)


class _Exchange:
    def __init__(self, items, out_shapes):
        self.items, self.out_shapes = list(items), tuple(out_shapes)
        self.arrays = [it[0] for it in self.items]
        n = len(self.items)
        self.n_in, self.n_out = n, len(self.out_shapes)
        self.scratch = [pltpu.SemaphoreType.DMA((n * N_CHIPS,)), pltpu.SemaphoreType.DMA((n * N_CHIPS,)),
                        pltpu.SemaphoreType.DMA((n,))]

    def _copies(self, ins, outs, sems, m):
        send_sems, recv_sems, local_sems = sems
        c = lax.axis_index("c")
        others = [j for j in range(N_CHIPS) if j != m]

        def remote(a, src, dst, to, from_):
            return pltpu.make_async_remote_copy(
                src_ref=src, dst_ref=dst, send_sem=send_sems.at[a * N_CHIPS + to],
                recv_sem=recv_sems.at[a * N_CHIPS + from_], device_id=(to // 2, to % 2, c), device_id_type=MESH)

        local, sends, recvs = [], [], []
        for a, (_, oi, src_of, dst_of) in enumerate(self.items):
            local.append(pltpu.make_async_copy(src_of(ins[a], m), dst_of(outs[oi], m), local_sems.at[a]))
            for j in others:
                sends.append(remote(a, src_of(ins[a], j), dst_of(outs[oi], m), j, m))
                recvs.append(remote(a, src_of(ins[a], m), dst_of(outs[oi], j), j, j))
        return local, sends, recvs

    def _on_my_chip(self, fn):
        chip = 2 * lax.axis_index("x") + lax.axis_index("y")
        for m in range(N_CHIPS):
            pl.when(chip == m)(functools.partial(fn, m))

    def start(self, ins, outs, sems):
        def go(m):
            local, sends, _ = self._copies(ins, outs, sems, m)
            for cp in local + sends:
                cp.start()
        self._on_my_chip(go)

    def wait(self, ins, outs, sems):
        def go(m):
            local, sends, recvs = self._copies(ins, outs, sems, m)
            for cp in recvs:
                cp.wait_recv()
            for cp in sends:
                cp.wait_send()
            for cp in local:
                cp.wait()
        self._on_my_chip(go)


def _half_rows(ref, cc):
    h = ref.shape[-2] // 2
    return ref.at[(slice(None),) * (len(ref.shape) - 2) + (pl.ds(cc * h, h), slice(None))]


class _Gather:
    def __init__(self, items, out_shapes):
        self.items, self.out_shapes = list(items), tuple(out_shapes)
        self.arrays = [it[0] for it in self.items]
        n = len(self.items)
        self.n_in, self.n_out = n, len(self.out_shapes)
        self.scratch = [pltpu.SemaphoreType.DMA((n * N_CHIPS,)) for _ in range(4)] + [pltpu.SemaphoreType.DMA((n,))]

    def _copies(self, ins, outs, sems, m, cc):
        ici_send, ici_recv, d2d_send, d2d_recv, local_sems = sems
        others = [j for j in range(N_CHIPS) if j != m]
        local, sends, arrivals, passed_on, from_sibling = [], [], [], [], []
        for a, (_, oi, src_of, dst_of) in enumerate(self.items):
            src, out = src_of(ins[a]), outs[oi]
            local.append(pltpu.make_async_copy(src, dst_of(out, m), local_sems.at[a]))
            for j in others:
                k = a * N_CHIPS + j
                mine_there = _half_rows(dst_of(out, m), cc)
                theirs_here = _half_rows(dst_of(out, j), cc)
                sends.append(pltpu.make_async_remote_copy(
                    src_ref=_half_rows(src, cc), dst_ref=mine_there, send_sem=ici_send.at[k],
                    recv_sem=ici_recv.at[a * N_CHIPS + m], device_id=(j // 2, j % 2, cc), device_id_type=MESH))
                arrivals.append(pltpu.make_async_remote_copy(
                    src_ref=_half_rows(src, cc), dst_ref=theirs_here, send_sem=ici_send.at[k], recv_sem=ici_recv.at[k],
                    device_id=(j // 2, j % 2, cc), device_id_type=MESH))
                passed_on.append(pltpu.make_async_remote_copy(
                    src_ref=theirs_here, dst_ref=theirs_here, send_sem=d2d_send.at[k], recv_sem=d2d_recv.at[k],
                    device_id=(m // 2, m % 2, 1 - cc), device_id_type=MESH))
                other_half = _half_rows(dst_of(out, j), 1 - cc)
                from_sibling.append(pltpu.make_async_remote_copy(
                    src_ref=other_half, dst_ref=other_half, send_sem=d2d_send.at[k], recv_sem=d2d_recv.at[k],
                    device_id=(m // 2, m % 2, 1 - cc), device_id_type=MESH))
        return local, sends, arrivals, passed_on, from_sibling

    def _on_my_core(self, fn):
        chip = 2 * lax.axis_index("x") + lax.axis_index("y")
        c = lax.axis_index("c")
        for m in range(N_CHIPS):
            for cc in range(2):
                pl.when((chip == m) & (c == cc))(functools.partial(fn, m, cc))

    def start(self, ins, outs, sems):
        def go(m, cc):
            local, sends, _, _, _ = self._copies(ins, outs, sems, m, cc)
            for cp in local + sends:
                cp.start()
        self._on_my_core(go)

    def wait(self, ins, outs, sems):
        def go(m, cc):
            local, sends, arrivals, passed_on, from_sibling = self._copies(ins, outs, sems, m, cc)
            for arrived, onward in zip(arrivals, passed_on):
                arrived.wait_recv()
                onward.start()
            for cp in from_sibling:
                cp.wait_recv()
            for cp in sends + passed_on:
                cp.wait_send()
            for cp in local:
                cp.wait()
        self._on_my_core(go)


def _run_exchange(ex, name):
    def body(*refs):
        ins, outs, sems = refs[:ex.n_in], refs[ex.n_in:ex.n_in + ex.n_out], refs[ex.n_in + ex.n_out:]
        ex.start(ins, outs, sems)
        ex.wait(ins, outs, sems)

    return pl.pallas_call(
        body, name=name, out_shape=ex.out_shapes, in_specs=[ANY] * ex.n_in, out_specs=(ANY,) * ex.n_out,
        scratch_shapes=ex.scratch, compiler_params=_params())(*ex.arrays)


def _sibling_swap(arrays, name):
    n = len(arrays)

    def body(*refs):
        ins, outs = refs[:n], refs[n:2 * n]
        send_sems, recv_sems = refs[2 * n:]
        peer = (lax.axis_index("x"), lax.axis_index("y"), 1 - lax.axis_index("c"))
        cps = [pltpu.make_async_remote_copy(src_ref=ins[a], dst_ref=outs[a], send_sem=send_sems.at[a],
                                            recv_sem=recv_sems.at[a], device_id=peer, device_id_type=MESH)
               for a in range(n)]
        for cp in cps:
            cp.start()
        for cp in cps:
            cp.wait()

    return pl.pallas_call(
        body, name=name, out_shape=tuple(SDS(a.shape, a.dtype) for a in arrays), in_specs=[ANY] * n,
        out_specs=(ANY,) * n, scratch_shapes=[pltpu.SemaphoreType.DMA((n,)), pltpu.SemaphoreType.DMA((n,))],
        compiler_params=_params())(*arrays)


def _block_diag(w):
    h, n, m = w.shape
    eye = jnp.eye(h, dtype=w.dtype)
    return (w[:, :, None, :] * eye[:, None, :, None]).reshape(h * n, h * m)


def _diag_blocks(d, h, col0=0, ncols=None, stacked=1):
    ncols = d.shape[1] - col0 if ncols is None else ncols
    n, m = d.shape[0] // (h * stacked), ncols // h
    lanes = 128
    assert m <= lanes and lanes % m == 0 and col0 % lanes == 0

    def body(d_ref, o_ref):
        for gi in range(h * stacked):
            c = col0 + (gi % h) * m
            chunk = d_ref[gi * n:(gi + 1) * n, c // lanes * lanes:c // lanes * lanes + lanes]
            o_ref[gi * n:(gi + 1) * n, :] = chunk[:, c % lanes:c % lanes + m]

    out = pl.pallas_call(body, name="diag_blocks", out_shape=SDS((stacked * h * n, m), d.dtype),
                         compiler_params=_params())(d)
    return out.reshape(stacked * h, n, m)


S5_CHUNKS = 4
S5_PER = S5_GROUPS // S5_CHUNKS
CH_W = S5_PER * S5_CH
ST_W = S5_PER * S5_STATE


def _bd_stack(mats):
    _, _, n, m = mats.shape
    eye = jnp.eye(S5_PER, dtype=mats.dtype)
    t = mats.reshape(2, S5_CHUNKS, S5_PER, n, m)
    bd = t[:, :, :, :, None, :] * eye[None, None, :, None, :, None]
    return bd.reshape(2 * S5_CHUNKS, S5_PER * n, S5_PER * m).astype(MXU_DTYPE)


def _bd_expand(a, a_col0, w8, name):
    s = a.shape[0]
    tm = min(1024, s)
    pt = _scan_tile(s)
    c0 = a_col0 // CH_W

    def body(a_ref, w_ref, o_ref):
        a_t = jnp.concatenate([_load_chunked(a_ref, t0, pt) for t0 in range(0, tm, pt)], axis=0)
        o_ref[...] = jnp.dot(a_t.astype(MXU_DTYPE), w_ref[...], preferred_element_type=F32)

    return pl.pallas_call(
        body, name=name, out_shape=SDS((s, 2 * S5_N), F32), grid=(s // tm, 2 * S5_CHUNKS),
        in_specs=[pl.BlockSpec((tm, CH_W), lambda i, b: (i, c0 + b % S5_CHUNKS)),
                  pl.BlockSpec((None, CH_W, ST_W), lambda i, b: (b, 0, 0))],
        out_specs=pl.BlockSpec((tm, ST_W), lambda i, b: (i, b)), compiler_params=_params(2))(a, w8)


def _bd_reduce(x, w8, name):
    s = x.shape[0]
    tm = min(1024, s)
    pt = _scan_tile(s)

    def body(x_ref, w_ref, o_ref, acc):
        p = pl.program_id(2)

        @pl.when(p == 0)
        def _():
            acc[...] = jnp.zeros_like(acc)

        acc[...] += jnp.dot(x_ref[...].astype(MXU_DTYPE), w_ref[...], preferred_element_type=F32)

        @pl.when(p == 1)
        def _():
            for t0 in range(0, tm, pt):
                _store_natural(o_ref, t0, pt, acc[t0:t0 + pt, :])

    return pl.pallas_call(
        body, name=name, out_shape=SDS((s, BR), F32), grid=(s // tm, S5_CHUNKS, 2),
        in_specs=[pl.BlockSpec((tm, ST_W), lambda i, q, p: (i, p * S5_CHUNKS + q)),
                  pl.BlockSpec((None, ST_W, CH_W), lambda i, q, p: (p * S5_CHUNKS + q, 0, 0))],
        out_specs=pl.BlockSpec((tm, CH_W), lambda i, q, p: (i, q)),
        scratch_shapes=[pltpu.VMEM((tm, CH_W), F32)], compiler_params=_params(3))(x, w8)


def _bd_wgrad(a, a_col0, x, name):
    s = a.shape[0]
    tk = min(1024, s)
    nk = s // tk
    pt = _scan_tile(s)
    c0 = a_col0 // CH_W

    def body(a_ref, x_ref, o_ref, acc):
        k = pl.program_id(1)

        @pl.when(k == 0)
        def _():
            acc[...] = jnp.zeros_like(acc)

        a_t = jnp.concatenate([_load_chunked(a_ref, t0, pt) for t0 in range(0, tk, pt)], axis=0)
        acc[...] += lax.dot_general(a_t.astype(MXU_DTYPE), x_ref[...].astype(MXU_DTYPE),
                                    (((0,), (0,)), ((), ())), preferred_element_type=F32)

        @pl.when(k == nk - 1)
        def _():
            o_ref[...] = acc[...]

    return pl.pallas_call(
        body, name=name, out_shape=SDS((2 * S5_CHUNKS * CH_W, ST_W), F32), grid=(2 * S5_CHUNKS, nk),
        in_specs=[pl.BlockSpec((tk, CH_W), lambda b, k: (k, c0 + b % S5_CHUNKS)),
                  pl.BlockSpec((tk, ST_W), lambda b, k: (k, b))],
        out_specs=pl.BlockSpec((CH_W, ST_W), lambda b, k: (b, 0)),
        scratch_shapes=[pltpu.VMEM((CH_W, ST_W), F32)], compiler_params=_params(2))(a, x)


def _tiles(s):
    return dict(tb=min(512, s), tln=min(256, s), tscan=min(512, s))


def _layer_weights(p, l):
    pad8 = lambda w: jnp.pad(w, ((0, 8 - w.shape[0]), (0, 0)))
    return dict(
        conv_a=pad8(p["conv_a"][l]), conv_c=pad8(p["conv_c"][l]), conv_c_b=p["conv_c_b"][l][None],
        w_cat=jnp.concatenate([_block_diag(p["lru_wa"][l]), _block_diag(p["lru_wx"][l])], axis=1).astype(MXU_DTYPE),
        b_cat=jnp.concatenate([p["lru_ba"][l], p["lru_bx"][l]])[None], lam=p["lru_lambda"][l][None],
        lam_re=p["s5_lam_re"][l], lam_im=p["s5_lam_im"][l], log_dt=p["s5_log_dt"][l][:, None],
        b_re=p["s5_b_re"][l].reshape(S5_N, S5_CH), b_im=p["s5_b_im"][l].reshape(S5_N, S5_CH),
        c_re=p["s5_c_re"][l], c_im=p["s5_c_im"][l], d_skip=p["s5_d"][l][None], b_glu=p["s5_b_glu"][l][None],
        ln_g=p["ln_g"][l][None], ln_b=p["ln_b"][l][None])


def _s5_matrices(lw):
    ab_re, ab_im, f_re, f_im = _s5_disc_fwd(lw["lam_re"], lw["lam_im"], lw["log_dt"])
    f_re, f_im = f_re.reshape(S5_N, 1), f_im.reshape(S5_N, 1)
    bb_re, bb_im = _s5_bbar_fwd(f_re, f_im, lw["b_re"], lw["b_im"])
    bb = jnp.stack([bb_re, bb_im]).reshape(2, S5_GROUPS, S5_STATE, S5_CH)
    cc = jnp.stack([lw["c_re"], -lw["c_im"]])
    a_row = jnp.concatenate([ab_re.reshape(1, S5_N), ab_im.reshape(1, S5_N)], axis=1)
    return dict(f_re=f_re, f_im=f_im, a_row=a_row, w_bu=_bd_stack(jnp.swapaxes(bb, 2, 3)), w_du=_bd_stack(bb),
                w_cx=_bd_stack(jnp.swapaxes(cc, 2, 3)), w_dx=_bd_stack(cc))


def _mm_hooked(hook, *args, **kw):
    if hook is None:
        return _mm(*args, **kw)
    out = _mm(*args, carry=hook[0], **kw)
    hook[1](out[1:])
    return out[0]


def _layer_fwd(x, ada, w_in, get_rest, lw, s5m, bias_tabs, hooks=None):
    s = x.shape[0]
    t = _tiles(s)
    tb = t["tb"]
    shift, scale, gate = ada
    hooks = hooks or {}
    h = _modulate(x, scale, shift, tb)
    proj = _mm_hooked(hooks.get("in_proj"), h, w_in, name="in_proj", tm=1024, tn=1024, tk=D_MODEL)
    w_out, w_glu = get_rest()
    y_a = _branch_a_fwd(proj, lw["conv_a"], tb)
    os_, lses = [], []
    for g, (_, dil) in enumerate(DILATIONS):
        o, lse = _attn_fwd(proj, bias_tabs[g], dil)
        os_.append(o)
        lses.append(lse)
    y_b = _attn_combine(os_, lses, proj, tb)
    lru_a, lru_b = _lru_gates_fwd(proj, lw["conv_c"], lw["conv_c_b"], lw["w_cat"], lw["b_cat"], lw["lam"], tb)
    lru_h = _scan_real(lru_a, lru_b, reverse=False, tb=tb, name="lru_scan")
    y_c = _gate_out(lru_h, proj, CB_CG, tb, "lru_out")
    bu = _bd_expand(proj, CB_DU * BR, s5m["w_bu"], "s5_bu")
    s5_x = _scan_cplx(bu, s5m["a_row"], reverse=False, tb=t["tscan"], name="s5_scan")
    ylin = _bd_reduce(s5_x, s5m["w_cx"], "s5_cx")
    y_d = _s5_tail_fwd(ylin, proj, lw["d_skip"], w_glu, lw["b_glu"], tb)
    ycat = jnp.concatenate([y_a, y_b, y_c, y_d], axis=1)
    x_next, xhat, y, rstd = _out_ln(ycat, w_out, x, gate, lw["ln_g"], lw["ln_b"], t["tln"])
    saved = dict(x=x, h=h, proj=proj, os=os_, lses=lses, lru_a=lru_a, lru_h=lru_h, s5_x=s5_x, ylin=ylin, ycat=ycat,
                 xhat=xhat, y=y, rstd=rstd)
    return x_next, saved


def _layer_bwd(dxn, sv, ada, w_in, w_out, w_glu, lw, s5m, bias_tabs, head_ones, hooks=None):
    s = dxn.shape[0]
    t = _tiles(s)
    tb = t["tb"]
    shift, scale, gate = ada
    proj = sv["proj"]
    g = {}
    hook = lambda name: hooks[name](g) if hooks and name in hooks else None
    dyb, dxa, g["ln_g"], g["ln_b"], dgate = _ln_bwd(dxn, sv["xhat"], sv["y"], sv["rstd"], lw["ln_g"], gate, t["tln"])
    g["w_out"] = _mm_hooked(hook("dw_out"), sv["ycat"], dyb, name="dw_out", ta=True, out_dtype=WIRE_DTYPE,
                            tm=1024, tn=1024, tk=1024)
    dycat = _mm(dyb, w_out, name="dycat", tb=True, tm=1024, tn=1024, tk=D_MODEL)
    da, dconv_a = _branch_a_bwd(dycat, proj, lw["conv_a"], tb)
    g["conv_a"] = dconv_a[0:3]
    pre = _attn_bwd_pre(dycat, sv["os"], sv["lses"], proj, head_ones, tb)
    dbg, dos, dms = pre[0], pre[1:4], pre[4:7]
    dqkv, dbias = [], []
    for gi, (_, dil) in enumerate(DILATIONS):
        hk = hook(f"attn_bwd_d{dil}")
        dq, dk, dv, dbi, *got = _attn_bwd(proj, dos[gi], sv["lses"][gi], dms[gi], bias_tabs[gi], dil,
                                          carry=hk and hk[0])
        if hk:
            hk[1](got)
        dqkv.append((dq, dk, dv))
        dbias.append(dbi)
    dqkv = list(zip(*dqkv))
    dh, dcg = _gate_out_bwd(dycat, 2, sv["lru_h"], proj, CB_CG, tb, "lru_out_bwd")
    lmb = _scan_real(sv["lru_a"], dh, reverse=True, tb=tb, name="lru_scan_bwd")
    dxc, dpre, xcb, dbcat, dlam = _lru_gates_bwd(proj, lmb, sv["lru_h"], lw["conv_c"], lw["conv_c_b"], lw["w_cat"],
                                                  lw["b_cat"], lw["lam"], tb)
    dwcat = _mm(xcb, dpre, name="dw_lru", ta=True, tn=1024)
    g["lru_wa"] = _diag_blocks(dwcat, LRU_HEADS, 0, BR)
    g["lru_wx"] = _diag_blocks(dwcat, LRU_HEADS, BR, BR)
    g["lru_ba"], g["lru_bx"], g["lru_lambda"] = dbcat[0, 0:BR], dbcat[0, BR:2 * BR], dlam[0]
    dcx, dconv_c, dccb = _conv_c_bwd(dxc, proj, lw["conv_c"], tb)
    g["conv_c"], g["conv_c_b"] = dconv_c[0:4], dccb[0]
    dyl, dus, ddg, gb, dtb, ddk, dbglu = _s5_tail_bwd(dycat, sv["ylin"], proj, lw["d_skip"], w_glu, lw["b_glu"], tb)
    g["s5_d"], g["s5_b_glu"] = ddk[0], dbglu[0]
    g["s5_w_glu"] = _mm(gb, dtb, name="dw_glu", ta=True, out_dtype=WIRE_DTYPE)
    dxd = _bd_expand(dyl, 0, s5m["w_dx"], "s5_dx")
    s5_l = _scan_cplx(dxd, s5m["a_row"], reverse=True, tb=t["tscan"], name="s5_scan_bwd")
    dab = _s5_da(s5_l, sv["s5_x"], t["tscan"])
    du = _bd_reduce(s5_l, s5m["w_du"], "s5_du")
    per_group = lambda d8: _diag_blocks(d8, S5_PER, stacked=2 * S5_CHUNKS).reshape(2, S5_GROUPS, S5_CH, S5_STATE)
    dbb = per_group(_bd_wgrad(proj, CB_DU * BR, s5_l, "dw_s5_b"))
    dcc = per_group(_bd_wgrad(dyl, 0, sv["s5_x"], "dw_s5_c"))
    from_bd = lambda half: jnp.swapaxes(dbb[half], 1, 2).reshape(S5_N, S5_CH)
    df_re, df_im, db_re, db_im = _s5_bbar_bwd(s5m["f_re"], s5m["f_im"], lw["b_re"], lw["b_im"],
                                              from_bd(0), from_bd(1))
    shp = (S5_GROUPS, S5_STATE)
    g["s5_lam_re"], g["s5_lam_im"], dlog_dt = _s5_disc_bwd(
        lw["lam_re"], lw["lam_im"], lw["log_dt"],
        (dab[:, 0:S5_N].reshape(shp), dab[:, S5_N:].reshape(shp), df_re.reshape(shp), df_im.reshape(shp)))
    g["s5_log_dt"] = dlog_dt[:, 0]
    g["s5_b_re"] = db_re.reshape(S5_GROUPS, S5_STATE, S5_CH)
    g["s5_b_im"] = db_im.reshape(S5_GROUPS, S5_STATE, S5_CH)
    g["s5_c_re"], g["s5_c_im"] = dcc[0], -dcc[1]
    dproj = _assemble_dproj(da, dqkv, dbg, dcx, dcg, du, dus, ddg, tb)
    g["w_in"] = _mm_hooked(hook("dw_in"), sv["h"], dproj, name="dw_in", ta=True, out_dtype=WIRE_DTYPE,
                           tm=1024, tn=1536, tk=1024)
    dhm = _mm_hooked(hook("dh"), dproj, w_in, name="dh", tb=True, tm=1024, tn=1024, tk=1536)
    dx, dshift, dscale = _mod_bwd(dhm, dxa, sv["x"], scale, tb)
    g["ada"] = jnp.concatenate([dshift[0], dscale[0], dgate[0]])
    return dx, g, dbias


SMALL = ("rel_bias", "conv_a", "conv_c", "conv_c_b", "lru_wa", "lru_ba", "lru_wx", "lru_bx", "lru_lambda",
         "s5_lam_re", "s5_lam_im", "s5_log_dt", "s5_b_re", "s5_b_im", "s5_c_re", "s5_c_im", "s5_d", "s5_b_glu",
         "ln_g", "ln_b")
PER_LAYER_SMALL = SMALL[1:]


def _local_step(x, target, ada_rows, w_in, w_out, w_glu, p, comm=None):
    if comm is None:
        get_w_in = lambda l: w_in[l]
        get_rest = lambda l: (w_out[l], w_glu[l])
        fwd_hooks = bwd_hooks = lambda *_: None
    else:
        get_w_in, get_rest, fwd_hooks, bwd_hooks = comm.w_in, comm.rest, comm.fwd_hooks, comm.bwd_hooks
    s = x.shape[0]
    buckets = _bucket_maps()
    bias_tabs = _bias_tables(p["rel_bias"], buckets)
    head_ones = _block_diag(jnp.ones((ATT_HEADS, HEAD_DIM, HEAD_DIM), MXU_DTYPE))
    lws = [_layer_weights(p, l) for l in range(DEPTH)]
    s5ms = [_s5_matrices(lw) for lw in lws]
    adas = [tuple(ada_rows[l, k * D_MODEL:(k + 1) * D_MODEL][None] for k in range(3)) for l in range(DEPTH)]
    saved = []
    for l in range(DEPTH):
        x, sv = _layer_fwd(x, adas[l], get_w_in(l), functools.partial(get_rest, l), lws[l], s5ms[l], bias_tabs,
                           fwd_hooks(l))
        saved.append(sv)
    loss, dx = _loss_head(x, target, _tiles(s)["tb"])
    grads = [None] * DEPTH
    dbias_sum = []
    for l in reversed(range(DEPTH)):
        dx, grads[l], dbias = _layer_bwd(dx, saved[l], adas[l], get_w_in(l), *get_rest(l), lws[l], s5ms[l],
                                         bias_tabs, head_ones, bwd_hooks(l, grads))
        dbias_sum.append(jnp.stack(dbias))
    drel = _rel_bias_grad(jnp.stack(dbias_sum), buckets)[:, 0:ATT_HEADS]
    small = {n: jnp.stack([grads[l][n] for l in range(DEPTH)]) for n in PER_LAYER_SMALL + ("ada",)}
    small["rel_bias"] = drel
    big = {n: [grads[l][n] for l in range(DEPTH)] for n in ("w_in", "w_out", "s5_w_glu")}
    return loss, dx, big, small


PACK_ROWS = 256


def _pack(parts):
    flat = jnp.concatenate([t.reshape(-1).astype(F32) for t in parts])
    n = flat.shape[0]
    rows = -(-n // (PACK_ROWS * 128)) * PACK_ROWS
    return jnp.pad(flat, (0, rows * 128 - n)).reshape(rows, 128)


def _unpack(packed, shapes):
    flat = packed.reshape(packed.shape[:-2] + (-1,))
    out, off = [], 0
    for shp in shapes:
        size = math.prod(shp)
        out.append(flat[..., off:off + size].reshape(flat.shape[:-1] + tuple(shp)))
        off += size
    return out


def _take_cols(t, chip, width):
    return lax.dynamic_slice_in_dim(t, chip * width, width, axis=t.ndim - 1)


class _Comm:
    IN_W, OUT_R, GLU_R = N_IN // N_CHIPS, D_MODEL // N_CHIPS, BR // N_CHIPS

    def __init__(self, w_in_b, w_out_b, w_glu_b):
        assert DEPTH == 2
        self.shards = (w_in_b, w_out_b, w_glu_b)
        in_w = self.IN_W
        self.w_in_full = {0: _run_exchange(_Gather(
            [(w_in_b, 0, lambda ref: ref.at[0], lambda ref, j: ref.at[:, pl.ds(j * in_w, in_w)])],
            [SDS((D_MODEL, N_IN), WIRE_DTYPE)]), "gather_w_in0")[0]}
        self.w_out_full = self.w_glu_full = None
        self.recv = {}

    def w_in(self, l):
        return self.w_in_full[l]

    def rest(self, l):
        return self.w_out_full[l], self.w_glu_full[l]

    def fwd_hooks(self, l):
        if l != 0:
            return None
        w_in_b, w_out_b, w_glu_b = self.shards
        in_w, out_r, glu_r = self.IN_W, self.OUT_R, self.GLU_R
        whole = lambda ref: ref
        items = [(w_out_b, 0, whole, lambda ref, j: ref.at[:, pl.ds(j * out_r, out_r), :]),
                 (w_glu_b, 1, whole, lambda ref, j: ref.at[:, pl.ds(j * glu_r, glu_r), :]),
                 (w_in_b, 2, lambda ref: ref.at[1], lambda ref, j: ref.at[:, pl.ds(j * in_w, in_w)])]
        shapes = [SDS((DEPTH, D_MODEL, D_MODEL), WIRE_DTYPE), SDS((DEPTH, BR, BR), WIRE_DTYPE),
                  SDS((D_MODEL, N_IN), WIRE_DTYPE)]

        def done(got):
            self.w_out_full, self.w_glu_full, self.w_in_full[1] = got

        return {"in_proj": (_Gather(items, shapes), done)}

    W_IN_ROWS = ((0, 1024), (1024, 512), (1536, 512))

    def _scatter(self, parts):
        in_w, out_r, glu_r = self.IN_W, self.OUT_R, self.GLU_R
        items, shapes, keys = [], [], []
        for oi, (name, l, arr, *rows) in enumerate(parts):
            if name == "w_in":
                r0, nr = rows[0] if rows else (0, D_MODEL)
                cut = functools.partial(lambda ref, j, r0, nr: ref.at[pl.ds(r0, nr), pl.ds(j * in_w, in_w)], r0=r0, nr=nr)
                shard = (nr, in_w)
            elif name == "w_out":
                cut, shard = (lambda ref, j: ref.at[pl.ds(j * out_r, out_r), :]), (out_r, D_MODEL)
            else:
                cut, shard = (lambda ref, j: ref.at[pl.ds(j * glu_r, glu_r), :]), (glu_r, BR)
            items.append((arr, oi, cut, lambda ref, j: ref.at[j]))
            shapes.append(SDS((N_CHIPS,) + shard, WIRE_DTYPE))
            keys.append((name, l) + ((rows[0][0],) if rows else ()))

        def done(got):
            self.recv.update(zip(keys, got))

        return _Exchange(items, shapes), done

    def received(self, name):
        return [self.recv[k] for k in sorted(k for k in self.recv if k[0] == name)]

    def bwd_hooks(self, l, grads):
        if l != 0:
            return None
        g1 = grads[1]
        w_in_part = lambda k: (lambda g: self._scatter([("w_in", 1, g1["w_in"], self.W_IN_ROWS[k])]))
        return {"dw_out": lambda g: self._scatter([("w_out", 1, g1["w_out"]), ("s5_w_glu", 1, g1["s5_w_glu"])]),
                "attn_bwd_d16": w_in_part(0), "attn_bwd_d4": w_in_part(1), "attn_bwd_d1": w_in_part(2),
                "dw_in": lambda g: self._scatter([("w_out", 0, g["w_out"]), ("s5_w_glu", 0, g["s5_w_glu"])]),
                "dh": lambda g: self._scatter([("w_in", 0, g["w_in"])])}


def kernel(x, c, rel_bias, w_ada, b_ada, w_in, conv_a, conv_c, conv_c_b, lru_wa, lru_ba, lru_wx, lru_bx, lru_lambda, s5_lam_re, s5_lam_im, s5_log_dt, s5_b_re, s5_b_im, s5_c_re, s5_c_im, s5_d, s5_w_glu, s5_b_glu, w_out, ln_g, ln_b, loss_target, m_rel_bias, m_w_ada, m_b_ada, m_w_in, m_conv_a, m_conv_c, m_conv_c_b, m_lru_wa, m_lru_ba, m_lru_wx, m_lru_bx, m_lru_lambda, m_s5_lam_re, m_s5_lam_im, m_s5_log_dt, m_s5_b_re, m_s5_b_im, m_s5_c_re, m_s5_c_im, m_s5_d, m_s5_w_glu, m_s5_b_glu, m_w_out, m_ln_g, m_ln_b, v_rel_bias, v_w_ada, v_b_ada, v_w_in, v_conv_a, v_conv_c, v_conv_c_b, v_lru_wa, v_lru_ba, v_lru_wx, v_lru_bx, v_lru_lambda, v_s5_lam_re, v_s5_lam_im, v_s5_log_dt, v_s5_b_re, v_s5_b_im, v_s5_c_re, v_s5_c_im, v_s5_d, v_s5_w_glu, v_s5_b_glu, v_w_out, v_ln_g, v_ln_b):
    args = dict(locals())
    names = ("rel_bias", "w_ada", "b_ada", "w_in", "conv_a", "conv_c", "conv_c_b", "lru_wa", "lru_ba", "lru_wx",
             "lru_bx", "lru_lambda", "s5_lam_re", "s5_lam_im", "s5_log_dt", "s5_b_re", "s5_b_im", "s5_c_re", "s5_c_im",
             "s5_d", "s5_w_glu", "s5_b_glu", "w_out", "ln_g", "ln_b")
    w = {n: args[n] for n in names}
    mom = {n: args["m_" + n] for n in names}
    var = {n: args["v_" + n] for n in names}
    chip = 2 * lax.axis_index("x") + lax.axis_index("y")
    me = 2 * chip + lax.axis_index("c")
    ada_w = 3 * D_MODEL // N_CHIPS
    in_w = N_IN // N_CHIPS
    out_r = D_MODEL // N_CHIPS
    glu_r = BR // N_CHIPS
    conv_w = BR // N_CHIPS

    comm = _Comm(w["w_in"].astype(WIRE_DTYPE), w["w_out"].astype(WIRE_DTYPE), w["s5_w_glu"].astype(WIRE_DTYPE))

    taps = jnp.concatenate([w["conv_a"].reshape(DEPTH * 3, conv_w), w["conv_c"].reshape(DEPTH * 4, conv_w)])
    first = jnp.concatenate([c, jnp.pad(taps, ((0, 1), (0, D_MODEL - conv_w)))])
    got = _allgather8(first, "gather_c_taps").reshape(N_CHIPS, 2, 16, D_MODEL)
    c_all = got[:, :, 0].reshape(N_DEV, D_MODEL)
    taps_all = jnp.transpose(got[:, 0, 1:1 + DEPTH * 7, 0:conv_w], (1, 0, 2)).reshape(DEPTH * 7, BR)
    conv_a_f = taps_all[0:DEPTH * 3].reshape(DEPTH, 3, BR)
    conv_c_f = taps_all[DEPTH * 3:].reshape(DEPTH, 4, BR)

    cond_all = _silu_rows(c_all)
    ada_part = jnp.stack([_mm(cond_all, w["w_ada"][l], name="ada_fwd", tk=D_MODEL, tn=512,
                              bias=_take_cols(w["b_ada"][l][None], chip, ada_w)) for l in range(DEPTH)])
    ada_all = _allgather8(ada_part.reshape(DEPTH * N_DEV, ada_w), "gather_ada")
    ada_all = ada_all.reshape(N_CHIPS, 2, DEPTH, N_DEV, ada_w)[:, 0]
    ada_rows = lax.dynamic_index_in_dim(ada_all, me, axis=2, keepdims=False)
    ada_rows = jnp.transpose(ada_rows, (1, 0, 2)).reshape(DEPTH, 3 * D_MODEL)

    p = dict(w)
    p["conv_a"], p["conv_c"] = conv_a_f, conv_c_f
    loss, dx, _, small = _local_step(x[0], loss_target[0], ada_rows, None, None, None, p, comm)

    sums = [_sum_leading(comm.received(name), 256, "sum_chips") for name in ("w_in", "w_out", "s5_w_glu")]
    others = _sibling_swap(sums, "swap_cores")
    out = {}
    for name, mine, other in zip(("w_in", "w_out", "s5_w_glu"), sums, others):
        shp = w[name].shape
        flat = lambda t: t.reshape(-1, shp[-1])
        res = _adamw(flat(w[name]), [mine, other], flat(mom[name]), flat(var[name]), 128, "adamw_big")
        out[name] = [t.reshape(shp) for t in res]

    small_names = SMALL + ("ada",)
    small["loss"] = loss
    order = small_names + ("loss",)
    shapes = [small[n].shape for n in order]
    gathered = _allgather8(_pack([small[n] for n in order]), "gather_small")
    gathered = gathered.reshape(N_DEV, -1, 128)
    total = dict(zip(order, _unpack(_sum_leading([gathered], PACK_ROWS, "sum_devices"), shapes)))
    d_ada_all = _unpack(gathered, shapes)[order.index("ada")]
    g_small = {n: total[n] for n in SMALL}
    g_small["conv_a"] = _take_cols(total["conv_a"], chip, conv_w)
    g_small["conv_c"] = _take_cols(total["conv_c"], chip, conv_w)
    g_small["b_ada"] = total["ada"]
    g_w_ada = jnp.stack([_mm(cond_all, _take_cols(d_ada_all[:, l], chip, ada_w), name="dw_ada", ta=True, tn=ada_w)
                         for l in range(DEPTH)])
    upd_names = SMALL + ("b_ada",)
    upd_shapes = [w[n].shape for n in upd_names]
    res = _adamw(_pack([w[n] for n in upd_names]), [_pack([g_small[n] for n in upd_names])],
                 _pack([mom[n] for n in upd_names]), _pack([var[n] for n in upd_names]), PACK_ROWS, "adamw_small")
    for k, t in enumerate(res):
        for n, val in zip(upd_names, _unpack(t, upd_shapes)):
            out.setdefault(n, [None] * 4)[k] = val
    shp = w["w_ada"].shape
    flat = lambda t: t.reshape(-1, shp[-1])
    out["w_ada"] = [t.reshape(shp) for t in _adamw(flat(w["w_ada"]), [flat(g_w_ada)], flat(mom["w_ada"]),
                                                  flat(var["w_ada"]), 128, "adamw_ada")]
    return (total["loss"].reshape(()), dx[None]) + tuple(out[n][k] for k in range(4) for n in names)
```

```python
import functools
import math

import jax
import jax.numpy as jnp
from jax import lax
from jax.experimental import pallas as pl
from jax.experimental.pallas import tpu as pltpu

F32 = jnp.float32
MXU_DTYPE = jnp.bfloat16
WIRE_DTYPE = jnp.bfloat16
SDS = jax.ShapeDtypeStruct
MESH = pl.DeviceIdType.MESH
ANY = pl.BlockSpec(memory_space=pl.ANY)
VMEM_LIMIT = 48 * 1024 * 1024

D_MODEL = 2048
DEPTH = 2
BR = 512
ATT_HEADS = 8
HEAD_DIM = 64
DILATIONS = ((128, 1), (512, 4), (2048, 16))
BLK = 128
REL_BUCKETS = 32
REL_MAX_DIST = 2048
LRU_HEADS = 8
LRU_C = 8.0
S5_CH = 16
S5_GROUPS = 32
S5_STATE = 64
S5_N = S5_GROUPS * S5_STATE
N_IN = 12 * BR
ALPHA = (2 * DEPTH) ** 0.25
LN_EPS = 1e-5
NEG = -1e30
ADAM_LR, ADAM_B1, ADAM_B2, ADAM_EPS, ADAM_WD, ADAM_STEP = 0.001, 0.9, 0.999, 1e-08, 0.01, 10
CB_AB, CB_AC, CB_AX, CB_AG, CB_Q, CB_K, CB_V, CB_BG, CB_CX, CB_CG, CB_DU, CB_DG = range(12)
N_CHIPS = 4
N_DEV = 8


def _params(n_axes=0):
    kw = {"dimension_semantics": ("arbitrary",) * n_axes} if n_axes else {}
    return pltpu.CompilerParams(vmem_limit_bytes=VMEM_LIMIT, **kw)


def _rows(tb, w, cb=0):
    return pl.BlockSpec((tb, w), lambda i: (i, cb))


def _prev8(tb, w, cb=0):
    return pl.BlockSpec((8, w), lambda i: (jnp.maximum(i * (tb // 8) - 1, 0), cb))


def _next8(tb, w, n_rows, cb=0):
    return pl.BlockSpec((8, w), lambda i: (jnp.minimum((i + 1) * (tb // 8), n_rows // 8 - 1), cb))


def _const(shape):
    return pl.BlockSpec(shape, lambda *_: (0,) * len(shape))


def _silu(x):
    return x * jax.nn.sigmoid(x)


def _dsilu(x):
    s = jax.nn.sigmoid(x)
    return s * (1.0 + x * (1.0 - s))


def _shift_down(cur, prev8, j):
    rolled = pltpu.roll(cur, j, 0)
    row = lax.broadcasted_iota(jnp.int32, (8, cur.shape[1]), 0)
    first = jnp.where(row < j, pltpu.roll(prev8, j, 0), rolled[0:8])
    return jnp.concatenate([first, rolled[8:]], axis=0)


def _shift_up(cur, next8, j):
    t = cur.shape[0]
    rolled = pltpu.roll(cur, t - j, 0)
    row = lax.broadcasted_iota(jnp.int32, (8, cur.shape[1]), 0)
    last = jnp.where(row >= 8 - j, pltpu.roll(next8, 8 - j, 0), rolled[t - 8:t])
    return jnp.concatenate([rolled[:t - 8], last], axis=0)


def _colsum(x):
    return jnp.sum(x, axis=0, keepdims=True)


def _init_acc(*refs):
    @pl.when(pl.program_id(0) == 0)
    def _():
        for r in refs:
            r[...] = jnp.zeros_like(r)


def _call(body, *, name, out_shape, grid, in_specs, out_specs, scratch_shapes, args, carry=None):
    out_shape, out_specs, in_specs = tuple(out_shape), tuple(out_specs), list(in_specs)
    scratch_shapes = list(scratch_shapes)
    if carry is None:
        return pl.pallas_call(body, name=name, out_shape=out_shape, grid=grid, in_specs=in_specs, out_specs=out_specs,
                              scratch_shapes=scratch_shapes, compiler_params=_params(len(grid)))(*args)
    n_in, n_out, n_scr = len(in_specs), len(out_shape), len(scratch_shapes)

    def wrapped(*refs):
        ins, refs = refs[:n_in], refs[n_in:]
        x_ins, refs = refs[:carry.n_in], refs[carry.n_in:]
        outs, refs = refs[:n_out], refs[n_out:]
        x_outs, refs = refs[:carry.n_out], refs[carry.n_out:]
        scr, x_sems = refs[:n_scr], refs[n_scr:]
        at = [pl.program_id(d) for d in range(len(grid))]
        first = functools.reduce(lambda p, q: p & q, [i == 0 for i in at])
        last = functools.reduce(lambda p, q: p & q, [i == g - 1 for i, g in zip(at, grid)])
        pl.when(first)(lambda: carry.start(x_ins, x_outs, x_sems))
        body(*ins, *outs, *scr)
        pl.when(last)(lambda: carry.wait(x_ins, x_outs, x_sems))

    return pl.pallas_call(
        wrapped, name=name, out_shape=out_shape + carry.out_shapes, grid=grid, in_specs=in_specs + [ANY] * carry.n_in,
        out_specs=out_specs + (ANY,) * carry.n_out, scratch_shapes=scratch_shapes + carry.scratch,
        compiler_params=_params(len(grid)))(*args, *carry.arrays)


def _mm(a, b, *, name, ta=False, tb=False, out_dtype=F32, tm=512, tn=512, tk=512, a_col0=0, a_ncols=None, bias=None,
        carry=None):
    a_ncols = a.shape[1] - a_col0 if a_ncols is None else a_ncols
    m, k = (a_ncols, a.shape[0]) if ta else (a.shape[0], a_ncols)
    n = b.shape[0] if tb else b.shape[1]
    assert k == (b.shape[1] if tb else b.shape[0]), (name, a.shape, b.shape)
    tm, tn, tk = min(tm, m), min(tn, n), min(tk, k)
    nk = k // tk
    a_off = a_col0 // (tm if ta else tk)
    assert m % tm == 0 and n % tn == 0 and k % tk == 0 and a_col0 % (tm if ta else tk) == 0, (name, m, n, k)

    def body(*refs):
        if bias is None:
            a_ref, b_ref, o_ref, acc = refs
        else:
            a_ref, b_ref, bias_ref, o_ref, acc = refs
        kk = pl.program_id(2)

        @pl.when(kk == 0)
        def _():
            acc[...] = jnp.zeros_like(acc)

        dims = (((0 if ta else 1,), (1 if tb else 0,)), ((), ()))
        acc[...] += lax.dot_general(a_ref[...].astype(MXU_DTYPE), b_ref[...].astype(MXU_DTYPE), dims,
                                    preferred_element_type=F32)

        @pl.when(kk == nk - 1)
        def _():
            r = acc[...]
            if bias is not None:
                r = r + bias_ref[...]
            o_ref[...] = r.astype(out_dtype)

    a_spec = (pl.BlockSpec((tk, tm), lambda i, j, kk: (kk, i + a_off)) if ta
              else pl.BlockSpec((tm, tk), lambda i, j, kk: (i, kk + a_off)))
    b_spec = (pl.BlockSpec((tn, tk), lambda i, j, kk: (j, kk)) if tb
              else pl.BlockSpec((tk, tn), lambda i, j, kk: (kk, j)))
    in_specs, args = [a_spec, b_spec], [a, b]
    if bias is not None:
        in_specs.append(pl.BlockSpec((1, tn), lambda i, j, kk: (0, j)))
        args.append(bias)
    out = _call(body, name=name, out_shape=[SDS((m, n), out_dtype)], grid=(m // tm, n // tn, nk), in_specs=in_specs,
                out_specs=[pl.BlockSpec((tm, tn), lambda i, j, kk: (i, j))],
                scratch_shapes=[pltpu.VMEM((tm, tn), F32)], args=args, carry=carry)
    return out[0] if carry is None else out


def _silu_rows(c_all):
    def body(c_ref, o_ref):
        o_ref[...] = _silu(c_ref[...])
    return pl.pallas_call(body, name="cond_silu", out_shape=SDS(c_all.shape, F32))(c_all)


def _modulate(x, scale, shift, tb):
    s, d = x.shape

    def body(x_ref, sc_ref, sh_ref, o_ref):
        o_ref[...] = (x_ref[...] * (1.0 + sc_ref[...]) + sh_ref[...]).astype(MXU_DTYPE)

    return pl.pallas_call(body, name="modulate", out_shape=SDS((s, d), MXU_DTYPE), grid=(s // tb,),
                          in_specs=[_rows(tb, d), _const((1, d)), _const((1, d))], out_specs=_rows(tb, d),
                          compiler_params=_params(1))(x, scale, shift)


def _out_ln(ycat, w_out, x, gate, ln_g, ln_b, tb):
    s, d = x.shape

    def body(yc_ref, w_ref, x_ref, gt_ref, g_ref, b_ref, xn_ref, xh_ref, y_ref, rs_ref):
        y = jnp.dot(yc_ref[...], w_ref[...], preferred_element_type=F32)
        res = ALPHA * x_ref[...] + (1.0 + gt_ref[...]) * y
        mu = jnp.mean(res, axis=-1, keepdims=True)
        cen = res - mu
        var = jnp.mean(cen * cen, axis=-1, keepdims=True)
        rstd = lax.rsqrt(var + LN_EPS)
        xhat = cen * rstd
        xn_ref[...] = xhat * g_ref[...] + b_ref[...]
        xh_ref[...] = xhat
        y_ref[...] = y
        rs_ref[...] = rstd

    big = SDS((s, d), F32)
    return pl.pallas_call(
        body, name="out_proj_ln", out_shape=(big, big, big, SDS((s, 1), F32)), grid=(s // tb,),
        in_specs=[_rows(tb, d), _const((d, d)), _rows(tb, d), _const((1, d)), _const((1, d)), _const((1, d))],
        out_specs=(_rows(tb, d), _rows(tb, d), _rows(tb, d), _rows(tb, 1)), compiler_params=_params(1),
    )(ycat, w_out, x, gate, ln_g, ln_b)


def _ln_bwd(dxn, xhat, y, rstd, ln_g, gate, tb):
    s, d = dxn.shape

    def body(dxn_ref, xh_ref, y_ref, rs_ref, g_ref, gt_ref, dy_ref, dxa_ref, dg_ref, db_ref, dgt_ref):
        _init_acc(dg_ref, db_ref, dgt_ref)
        dxn_t, xh = dxn_ref[...], xh_ref[...]
        dxh = dxn_t * g_ref[...]
        dres = rs_ref[...] * (dxh - jnp.mean(dxh, axis=-1, keepdims=True)
                              - xh * jnp.mean(dxh * xh, axis=-1, keepdims=True))
        dy_ref[...] = ((1.0 + gt_ref[...]) * dres).astype(MXU_DTYPE)
        dxa_ref[...] = ALPHA * dres
        dg_ref[...] += _colsum(dxn_t * xh)
        db_ref[...] += _colsum(dxn_t)
        dgt_ref[...] += _colsum(dres * y_ref[...])

    vec = SDS((1, d), F32)
    return pl.pallas_call(
        body, name="ln_bwd", out_shape=(SDS((s, d), MXU_DTYPE), SDS((s, d), F32), vec, vec, vec), grid=(s // tb,),
        in_specs=[_rows(tb, d), _rows(tb, d), _rows(tb, d), _rows(tb, 1), _const((1, d)), _const((1, d))],
        out_specs=(_rows(tb, d), _rows(tb, d), _const((1, d)), _const((1, d)), _const((1, d))),
        compiler_params=_params(1))(dxn, xhat, y, rstd, ln_g, gate)


def _mod_bwd(dh, dxa, x, scale, tb):
    s, d = dh.shape

    def body(dh_ref, dxa_ref, x_ref, sc_ref, dx_ref, dsh_ref, dsc_ref):
        _init_acc(dsh_ref, dsc_ref)
        dh_t = dh_ref[...]
        dx_ref[...] = dxa_ref[...] + dh_t * (1.0 + sc_ref[...])
        dsh_ref[...] += _colsum(dh_t)
        dsc_ref[...] += _colsum(dh_t * x_ref[...])

    vec = SDS((1, d), F32)
    return pl.pallas_call(
        body, name="mod_bwd", out_shape=(SDS((s, d), F32), vec, vec), grid=(s // tb,),
        in_specs=[_rows(tb, d), _rows(tb, d), _rows(tb, d), _const((1, d))],
        out_specs=(_rows(tb, d), _const((1, d)), _const((1, d))), compiler_params=_params(1))(dh, dxa, x, scale)


def _loss_head(y, target, tb):
    s, d = y.shape

    def body(y_ref, t_ref, l_ref, dy_ref):
        _init_acc(l_ref)
        err = y_ref[...] - t_ref[...]
        l_ref[...] += (0.5 / d) * jnp.sum(err * err, keepdims=True)
        dy_ref[...] = err * (1.0 / d)

    return pl.pallas_call(body, name="loss_head", out_shape=(SDS((1, 1), F32), SDS((s, d), F32)), grid=(s // tb,),
                          in_specs=[_rows(tb, d), _rows(tb, d)], out_specs=(_const((1, 1)), _rows(tb, d)),
                          compiler_params=_params(1))(y, target)


def _conv_taps(u, up, w_ref, width):
    out = w_ref[width - 1:width, :] * u
    for j in range(width - 2, -1, -1):
        out = out + w_ref[j:j + 1, :] * _shift_down(u, up, width - 1 - j)
    return out


def _conv_taps_t(g, gn, w_ref, width):
    out = w_ref[width - 1:width, :] * g
    for j in range(width - 2, -1, -1):
        out = out + w_ref[j:j + 1, :] * _shift_up(g, gn, width - 1 - j)
    return out


def _conv_wgrad(dw_ref, g, u, up, width):
    dw_ref[width - 1:width, :] += _colsum(g * u)
    for j in range(width - 1):
        dw_ref[j:j + 1, :] += _colsum(g * _shift_down(u, up, width - 1 - j))


def _branch_a_fwd(proj, conv_w, tb):
    s = proj.shape[0]

    def body(ab, ac, ax, ag, acp, axp, w_ref, o_ref):
        has_prev = (pl.program_id(0) > 0).astype(F32)
        u = ac[...] * ax[...]
        up = acp[...] * axp[...] * has_prev
        o_ref[...] = (ab[...] * _conv_taps(u, up, w_ref, 3) * _silu(ag[...])).astype(MXU_DTYPE)

    return pl.pallas_call(
        body, name="branch_a_fwd", out_shape=SDS((s, BR), MXU_DTYPE), grid=(s // tb,),
        in_specs=[_rows(tb, BR, CB_AB), _rows(tb, BR, CB_AC), _rows(tb, BR, CB_AX), _rows(tb, BR, CB_AG),
                  _prev8(tb, BR, CB_AC), _prev8(tb, BR, CB_AX), _const((8, BR))],
        out_specs=_rows(tb, BR), compiler_params=_params(1))(proj, proj, proj, proj, proj, proj, conv_w)


def _branch_a_bwd(dycat, proj, conv_w, tb):
    s = proj.shape[0]

    def body(dy, dyn, ab, abn, ag, agn, ac, acp, ax, axp, w_ref, o_ref, dw_ref):
        _init_acc(dw_ref)
        i = pl.program_id(0)
        has_prev = (i > 0).astype(F32)
        has_next = (i < pl.num_programs(0) - 1).astype(F32)
        u = ac[...] * ax[...]
        up = acp[...] * axp[...] * has_prev
        v = _conv_taps(u, up, w_ref, 3)
        sg = _silu(ag[...])
        dv = dy[...] * ab[...] * sg
        dvn = dyn[...] * abn[...] * _silu(agn[...]) * has_next
        du = _conv_taps_t(dv, dvn, w_ref, 3)
        o_ref[:, 0:BR] = (dy[...] * v * sg).astype(MXU_DTYPE)
        o_ref[:, BR:2 * BR] = (du * ax[...]).astype(MXU_DTYPE)
        o_ref[:, 2 * BR:3 * BR] = (du * ac[...]).astype(MXU_DTYPE)
        o_ref[:, 3 * BR:4 * BR] = (dy[...] * ab[...] * v * _dsilu(ag[...])).astype(MXU_DTYPE)
        _conv_wgrad(dw_ref, dv, u, up, 3)

    return pl.pallas_call(
        body, name="branch_a_bwd", out_shape=(SDS((s, 4 * BR), MXU_DTYPE), SDS((8, BR), F32)), grid=(s // tb,),
        in_specs=[_rows(tb, BR, 0), _next8(tb, BR, s, 0),
                  _rows(tb, BR, CB_AB), _next8(tb, BR, s, CB_AB), _rows(tb, BR, CB_AG), _next8(tb, BR, s, CB_AG),
                  _rows(tb, BR, CB_AC), _prev8(tb, BR, CB_AC), _rows(tb, BR, CB_AX), _prev8(tb, BR, CB_AX),
                  _const((8, BR))],
        out_specs=(_rows(tb, 4 * BR), _const((8, BR))), compiler_params=_params(1),
    )(dycat, dycat, proj, proj, proj, proj, proj, proj, proj, proj, conv_w)


def _t5_bucket(dist):
    max_exact = REL_BUCKETS // 2
    nf = jnp.maximum(dist, 1).astype(F32)
    large = max_exact + (jnp.log(nf / max_exact) / math.log(REL_MAX_DIST / max_exact)
                         * (REL_BUCKETS - max_exact)).astype(jnp.int32)
    large = jnp.minimum(large, REL_BUCKETS - 1)
    return jnp.where(dist < max_exact, dist, large)


def _bucket_maps():
    maps = []
    i = jnp.arange(BLK)[:, None]
    j = jnp.arange(2 * BLK)[None, :]
    delta = i + BLK - j
    for window, dil in DILATIONS:
        span = window // dil
        bucket = _t5_bucket(jnp.clip(delta, 0, span) * dil)
        maps.append(jnp.where((delta >= 0) & (delta <= span), bucket, -1))
    return jnp.stack(maps).astype(jnp.int32)


def _bias_tables(rel_bias, buckets):
    n_pat = len(DILATIONS)

    def body(rb_ref, bk_ref, o_ref):
        for g in range(n_pat):
            bk = bk_ref[g]
            for h in range(ATT_HEADS):
                def per_bucket(b, acc):
                    return jnp.where(bk == b, rb_ref[b, h], acc)
                o_ref[g, h] = lax.fori_loop(0, REL_BUCKETS, per_bucket, jnp.full((BLK, 2 * BLK), NEG, F32))

    return pl.pallas_call(
        body, name="bias_tables", out_shape=SDS((n_pat, ATT_HEADS, BLK, 2 * BLK), F32),
        in_specs=[pl.BlockSpec(memory_space=pltpu.SMEM), pl.BlockSpec(memory_space=pltpu.VMEM)],
        compiler_params=_params())(rel_bias, buckets)


def _head_masks():
    lane = lax.broadcasted_iota(jnp.int32, (1, 2 * HEAD_DIM), 1)
    return [(lane < HEAD_DIM).astype(F32), (lane >= HEAD_DIM).astype(F32)]


def _strided(base, size, dil):
    return pl.ds(base, size, stride=dil) if dil > 1 else pl.ds(pl.multiple_of(base, BLK), size)


def _attn_groups(s, dil):
    return max(1, min(1024, s) // (dil * BLK)) if dil == 1 else max(1, min(2048, s) // (dil * BLK))


def _attn_fwd(proj, bias, dil):
    s = proj.shape[0]
    grp = _attn_groups(s, dil)
    u1 = dil * BLK
    unit = grp * u1
    nb = s // unit
    w = 2 * HEAD_DIM
    q0, k0, v0 = (cb * (BR // w) for cb in (CB_Q, CB_K, CB_V))

    def body(q_ref, kc_ref, kp_ref, vc_ref, vp_ref, bias_ref, o_ref, lse_ref, kbuf, vbuf):
        n = pl.program_id(1)
        col = lax.broadcasted_iota(jnp.int32, (1, 2 * BLK), 1)
        masks = _head_masks()
        kbuf[0:u1, :] = kp_ref[...]
        kbuf[u1:, :] = kc_ref[...]
        vbuf[0:u1, :] = vp_ref[...]
        vbuf[u1:, :] = vc_ref[...]

        def per_r(t, carry):
            j = t // dil
            base = j * u1 + t % dil
            rows = _strided(base, BLK, dil)
            no_prev = jnp.where((n == 0) & (j == 0) & (col < BLK), NEG, 0.0)
            q = q_ref[rows, :] * (HEAD_DIM ** -0.5)
            k = kbuf[_strided(base, 2 * BLK, dil), :].astype(MXU_DTYPE)
            v = vbuf[_strided(base, 2 * BLK, dil), :].astype(MXU_DTYPE)
            q2 = jnp.concatenate([q * masks[0], q * masks[1]], axis=0).astype(MXU_DTYPE)
            sc = lax.dot_general(q2, k, (((1,), (1,)), ((), ())), preferred_element_type=F32)
            sc = sc + jnp.concatenate([bias_ref[0], bias_ref[1]], axis=0) + no_prev
            mx = jnp.max(sc, axis=-1, keepdims=True)
            p = jnp.exp(sc - mx)
            l = jnp.sum(p, axis=-1, keepdims=True)
            o2 = jnp.dot((p / l).astype(MXU_DTYPE), v, preferred_element_type=F32)
            lse2 = mx + jnp.log(l)
            o_ref[rows, :] = o2[0:BLK] * masks[0] + o2[BLK:2 * BLK] * masks[1]
            lse_ref[rows, :] = lse2[0:BLK] * masks[0] + lse2[BLK:2 * BLK] * masks[1]
            return carry

        lax.fori_loop(0, grp * dil, per_r, 0, unroll=8)

    cur = lambda c0: pl.BlockSpec((unit, w), lambda hp, n: (n, c0 + hp))
    prev = lambda c0: pl.BlockSpec((u1, w), lambda hp, n: (jnp.maximum(n * grp - 1, 0), c0 + hp))
    out = pl.BlockSpec((unit, w), lambda hp, n: (n, hp))
    return pl.pallas_call(
        body, name=f"attn_fwd_d{dil}", out_shape=(SDS((s, BR), F32), SDS((s, BR), F32)), grid=(BR // w, nb),
        in_specs=[cur(q0), cur(k0), prev(k0), cur(v0), prev(v0),
                  pl.BlockSpec((2, BLK, 2 * BLK), lambda hp, n: (hp, 0, 0))],
        out_specs=(out, out),
        scratch_shapes=[pltpu.VMEM((unit + u1, w), F32), pltpu.VMEM((unit + u1, w), F32)],
        compiler_params=_params(2))(proj, proj, proj, proj, proj, bias)


def _softmax3(l0, l1, l2):
    mx = jnp.maximum(jnp.maximum(l0, l1), l2)
    e0, e1, e2 = jnp.exp(l0 - mx), jnp.exp(l1 - mx), jnp.exp(l2 - mx)
    inv = 1.0 / (e0 + e1 + e2)
    return e0 * inv, e1 * inv, e2 * inv


def _attn_combine(os_, lses, proj, tb):
    s = proj.shape[0]

    def body(o0, o1, o2, l0, l1, l2, bg, y_ref):
        w0, w1, w2 = _softmax3(l0[...], l1[...], l2[...])
        attn = w0 * o0[...] + w1 * o1[...] + w2 * o2[...]
        y_ref[...] = (attn * _silu(bg[...])).astype(MXU_DTYPE)

    return pl.pallas_call(
        body, name="attn_combine", out_shape=SDS((s, BR), MXU_DTYPE), grid=(s // tb,),
        in_specs=[_rows(tb, BR)] * 6 + [_rows(tb, BR, CB_BG)], out_specs=_rows(tb, BR),
        compiler_params=_params(1))(*os_, *lses, proj)


def _attn_bwd_pre(dycat, os_, lses, proj, head_ones, tb):
    s = proj.shape[0]

    def body(dy, o0, o1, o2, l0, l1, l2, bg, e_ref, dbg_ref, do0, do1, do2, dm0, dm1, dm2):
        w0, w1, w2 = _softmax3(l0[...], l1[...], l2[...])
        attn = w0 * o0[...] + w1 * o1[...] + w2 * o2[...]
        dattn = dy[...] * _silu(bg[...])
        dbg_ref[...] = dy[...] * attn * _dsilu(bg[...])
        prod = dattn * attn
        hi = prod.astype(MXU_DTYPE)
        lo = (prod - hi.astype(F32)).astype(MXU_DTYPE)
        tot = (jnp.dot(hi, e_ref[...], preferred_element_type=F32)
               + jnp.dot(lo, e_ref[...], preferred_element_type=F32))
        for wg, do_ref, dm_ref in ((w0, do0, dm0), (w1, do1, dm1), (w2, do2, dm2)):
            do_ref[...] = wg * dattn
            dm_ref[...] = wg * tot

    big = SDS((s, BR), F32)
    return pl.pallas_call(
        body, name="attn_bwd_pre", out_shape=(big,) * 7, grid=(s // tb,),
        in_specs=[_rows(tb, BR, 1)] + [_rows(tb, BR)] * 6 + [_rows(tb, BR, CB_BG), _const((BR, BR))],
        out_specs=(_rows(tb, BR),) * 7, compiler_params=_params(1))(dycat, *os_, *lses, proj, head_ones)


def _attn_bwd(proj, do, lse, dm, bias, dil, carry=None):
    s = proj.shape[0]
    grp = _attn_groups(s, dil)
    u1 = dil * BLK
    unit = grp * u1
    nb = s // unit
    w = 2 * HEAD_DIM
    q0, k0, v0 = (cb * (BR // w) for cb in (CB_Q, CB_K, CB_V))

    def body(q_ref, kc_ref, kp_ref, vc_ref, vp_ref, do_ref, lse_ref, dm_ref, bias_ref,
             dq_ref, dk_ref, dv_ref, dbias_ref, kbuf, vbuf, stage_k, stage_v):
        n = pl.program_id(1)
        col = lax.broadcasted_iota(jnp.int32, (1, 2 * BLK), 1)
        masks = _head_masks()

        @pl.when(n == 0)
        def _():
            dbias_ref[...] = jnp.zeros_like(dbias_ref)
            stage_k[...] = jnp.zeros_like(stage_k)
            stage_v[...] = jnp.zeros_like(stage_v)

        for out_ref, stage in ((dk_ref, stage_k), (dv_ref, stage_v)):
            if grp > 1:
                out_ref[0:unit - u1, :] = stage[u1:unit, :]
            stage[0:u1, :] = stage[unit:unit + u1, :]

        @pl.when(n < nb)
        def _():
            kbuf[0:u1, :] = kp_ref[...]
            kbuf[u1:, :] = kc_ref[...]
            vbuf[0:u1, :] = vp_ref[...]
            vbuf[u1:, :] = vc_ref[...]

            def per_r(t, carry):
                j = t // dil
                base = j * u1 + t % dil
                rows = _strided(base, BLK, dil)
                rows_hi = _strided(base + u1, BLK, dil)
                no_prev = jnp.where((n == 0) & (j == 0) & (col < BLK), NEG, 0.0)
                q = q_ref[rows, :] * (HEAD_DIM ** -0.5)
                k = kbuf[_strided(base, 2 * BLK, dil), :].astype(MXU_DTYPE)
                v = vbuf[_strided(base, 2 * BLK, dil), :].astype(MXU_DTYPE)
                do_t, lse_t, dm_t = do_ref[rows, :], lse_ref[rows, :], dm_ref[rows, :]
                stack = lambda t: jnp.concatenate([t * masks[0], t * masks[1]], axis=0).astype(MXU_DTYPE)
                per_head = lambda t: jnp.concatenate([t[:, 0:1], t[:, HEAD_DIM:HEAD_DIM + 1]], axis=0)
                q2, do2 = stack(q), stack(do_t)
                sc = lax.dot_general(q2, k, (((1,), (1,)), ((), ())), preferred_element_type=F32)
                p = jnp.exp(sc + jnp.concatenate([bias_ref[0], bias_ref[1]], axis=0) + no_prev - per_head(lse_t))
                dp = lax.dot_general(do2, v, (((1,), (1,)), ((), ())), preferred_element_type=F32)
                ds = p * (dp - per_head(dm_t))
                dbias_ref[0] += ds[0:BLK]
                dbias_ref[1] += ds[BLK:2 * BLK]
                dsb, pb = ds.astype(MXU_DTYPE), p.astype(MXU_DTYPE)
                dq2 = jnp.dot(dsb, k, preferred_element_type=F32)
                dk_acc = lax.dot_general(dsb, q2, (((0,), (0,)), ((), ())), preferred_element_type=F32)
                dv_acc = lax.dot_general(pb, do2, (((0,), (0,)), ((), ())), preferred_element_type=F32)
                dq_ref[rows, :] = (dq2[0:BLK] * masks[0] + dq2[BLK:2 * BLK] * masks[1]) * (HEAD_DIM ** -0.5)
                stage_k[rows, :] = stage_k[rows, :] + dk_acc[0:BLK]
                stage_v[rows, :] = stage_v[rows, :] + dv_acc[0:BLK]
                stage_k[rows_hi, :] = dk_acc[BLK:2 * BLK]
                stage_v[rows_hi, :] = dv_acc[BLK:2 * BLK]
                return carry

            lax.fori_loop(0, grp * dil, per_r, 0, unroll=8)

        dk_ref[unit - u1:unit, :] = stage_k[0:u1, :]
        dv_ref[unit - u1:unit, :] = stage_v[0:u1, :]

    qn = lambda n: jnp.minimum(n, nb - 1)
    cur = lambda c0: pl.BlockSpec((unit, w), lambda hp, n: (qn(n), c0 + hp))
    prev = lambda c0: pl.BlockSpec((u1, w), lambda hp, n: (jnp.maximum(qn(n) * grp - 1, 0), c0 + hp))
    row = pl.BlockSpec((unit, w), lambda hp, n: (qn(n), hp))
    late = pl.BlockSpec((unit, w), lambda hp, n: (jnp.maximum(n - 1, 0), hp))
    tab = pl.BlockSpec((2, BLK, 2 * BLK), lambda hp, n: (hp, 0, 0))
    big = SDS((s, BR), F32)
    return _call(
        body, name=f"attn_bwd_d{dil}", out_shape=(big, big, big, SDS((ATT_HEADS, BLK, 2 * BLK), F32)),
        grid=(BR // w, nb + 1),
        in_specs=[cur(q0), cur(k0), prev(k0), cur(v0), prev(v0), row, row, row, tab],
        out_specs=(row, late, late, tab),
        scratch_shapes=[pltpu.VMEM((unit + u1, w), F32)] * 4,
        args=(proj, proj, proj, proj, proj, do, lse, dm, bias), carry=carry)


def _rel_bias_grad(dbias, buckets):
    def body(db_ref, bk_ref, o_ref):
        row = lax.broadcasted_iota(jnp.int32, (REL_BUCKETS, 128), 0)
        lane = lax.broadcasted_iota(jnp.int32, (REL_BUCKETS, 128), 1)

        def per_bucket(b, acc):
            for g in range(len(DILATIONS)):
                hit = bk_ref[g] == b
                for h in range(ATT_HEADS):
                    both = db_ref[0, g, h] + db_ref[1, g, h]
                    val = jnp.sum(jnp.where(hit, both, 0.0), keepdims=True)
                    acc = acc + jnp.where((row == b) & (lane == h), val, 0.0)
            return acc

        o_ref[...] = lax.fori_loop(0, REL_BUCKETS, per_bucket, jnp.zeros((REL_BUCKETS, 128), F32))

    assert dbias.shape[0] == DEPTH == 2
    return pl.pallas_call(body, name="rel_bias_grad", out_shape=SDS((REL_BUCKETS, 128), F32),
                          compiler_params=_params())(dbias, buckets)


def _scan_real(a, b, *, reverse, tb, name):
    s, ch = a.shape
    nt = s // tb
    order = range(7, -1, -1) if reverse else range(8)

    def body(a_ref, b_ref, o_ref, carry):
        @pl.when(pl.program_id(0) == 0)
        def _():
            carry[...] = jnp.zeros_like(carry)

        def group(gi, h):
            r0 = pl.multiple_of((tb // 8 - 1 - gi if reverse else gi) * 8, 8)
            a8, b8 = a_ref[pl.ds(r0, 8), :], b_ref[pl.ds(r0, 8), :]
            rows = [None] * 8
            for k in order:
                if reverse:
                    rows[k] = b8[k:k + 1] + h
                    h = a8[k:k + 1] * rows[k]
                else:
                    h = a8[k:k + 1] * h + b8[k:k + 1]
                    rows[k] = h
            o_ref[pl.ds(r0, 8), :] = jnp.concatenate(rows, axis=0)
            return h

        carry[...] = lax.fori_loop(0, tb // 8, group, carry[...])

    spec = pl.BlockSpec((tb, ch), (lambda i: (nt - 1 - i, 0)) if reverse else (lambda i: (i, 0)))
    return pl.pallas_call(body, name=name, out_shape=SDS((s, ch), F32), grid=(nt,), in_specs=[spec, spec],
                          out_specs=spec, scratch_shapes=[pltpu.VMEM((1, ch), F32)],
                          compiler_params=_params(1))(a, b)


def _scan_tile(s):
    return min(512, s)


def _load_chunked(ref, t0, pt):
    ln = pt // 8
    return jnp.concatenate([ref[pl.ds(t0 + j, 8, stride=ln), :] for j in range(ln)], axis=0)


def _store_natural(ref, t0, pt, val):
    ln = pt // 8
    for j in range(ln):
        ref[pl.ds(t0 + j, 8, stride=ln), :] = val[j * 8:(j + 1) * 8]


def _scan_tile_in_place(a_ref, x_ref, carry, pw, *, reverse):
    ch2 = x_ref.shape[1]
    ch = ch2 // 2
    ln = x_ref.shape[0] // 8
    ar = a_ref[:, 0:ch]
    ai = -a_ref[:, ch:ch2] if reverse else a_ref[:, ch:ch2]

    def cmul(pr, pi, xr, xi):
        return pr * xr - pi * xi, pr * xi + pi * xr

    @pl.when(pl.program_id(0) == 0)
    def _():
        carry[...] = jnp.zeros_like(carry)

        def fill(j, p):
            pw[pl.ds(j, 1), 0:ch] = p[0]
            pw[pl.ds(j, 1), ch:ch2] = p[1]
            return cmul(ar, ai, *p)

        lax.fori_loop(0, ln, fill, (ar, ai))

    def rows_of(j):
        return pl.ds(pl.multiple_of((ln - 1 - j if reverse else j) * 8, 8), 8)

    def local(j, x):
        rows = rows_of(j)
        nr, ni = cmul(ar, ai, *x)
        xr, xi = nr + x_ref[rows, 0:ch], ni + x_ref[rows, ch:ch2]
        x_ref[rows, 0:ch] = xr
        x_ref[rows, ch:ch2] = xi
        return xr, xi

    zero = jnp.zeros((8, ch), F32)
    er, ei = lax.fori_loop(0, ln, local, (zero, zero), unroll=2)
    apr, api = pw[ln - 1:ln, 0:ch], pw[ln - 1:ln, ch:ch2]
    cr, ci = carry[:, 0:ch], carry[:, ch:ch2]
    into_r, into_i = [None] * 8, [None] * 8
    for c in (range(7, -1, -1) if reverse else range(8)):
        into_r[c], into_i[c] = cr, ci
        pr, pi = cmul(apr, api, cr, ci)
        cr, ci = er[c:c + 1] + pr, ei[c:c + 1] + pi
    carry[:, 0:ch] = cr
    carry[:, ch:ch2] = ci
    into_r, into_i = jnp.concatenate(into_r, axis=0), jnp.concatenate(into_i, axis=0)

    def fix(j, carry_):
        rows = rows_of(j)
        dr, di = cmul(pw[pl.ds(j, 1), 0:ch], pw[pl.ds(j, 1), ch:ch2], into_r, into_i)
        x_ref[rows, 0:ch] += dr
        x_ref[rows, ch:ch2] += di
        return carry_

    lax.fori_loop(0, ln, fix, 0, unroll=2)


def _neg_expm1(z):
    series = -z * (1.0 + z * (0.5 + z * (1.0 / 6 + z * (1.0 / 24 + z * (1.0 / 120)))))
    return jnp.where(z > -0.05, series, 1.0 - jnp.exp(z))


def _lru_gate(xc, pre_r, pre_i, lam):
    log_a = -LRU_C * jax.nn.sigmoid(pre_r) * jax.nn.softplus(-lam)
    return jnp.exp(log_a), jnp.sqrt(_neg_expm1(2.0 * log_a)) * jax.nn.sigmoid(pre_i) * xc


def _lru_gates_fwd(proj, conv_w, conv_b, w_cat, b_cat, lam, tb):
    s = proj.shape[0]

    def body(cx, cxp, w_ref, cb_ref, wc_ref, bc_ref, lam_ref, a_ref, b_ref):
        has_prev = (pl.program_id(0) > 0).astype(F32)
        xc = _conv_taps(cx[...], cxp[...] * has_prev, w_ref, 4) + cb_ref[...]
        pre = jnp.dot(xc.astype(MXU_DTYPE), wc_ref[...], preferred_element_type=F32) + bc_ref[...]
        a_ref[...], b_ref[...] = _lru_gate(xc, pre[:, 0:BR], pre[:, BR:2 * BR], lam_ref[...])

    big = SDS((s, BR), F32)
    return pl.pallas_call(
        body, name="lru_gates_fwd", out_shape=(big, big), grid=(s // tb,),
        in_specs=[_rows(tb, BR, CB_CX), _prev8(tb, BR, CB_CX), _const((8, BR)), _const((1, BR)),
                  _const((BR, 2 * BR)), _const((1, 2 * BR)), _const((1, BR))],
        out_specs=(_rows(tb, BR), _rows(tb, BR)), compiler_params=_params(1),
    )(proj, proj, conv_w, conv_b, w_cat, b_cat, lam)


def _gate_out(h, proj, cb, tb, name):
    s = proj.shape[0]

    def body(h_ref, g_ref, o_ref):
        o_ref[...] = (h_ref[...] * _silu(g_ref[...])).astype(MXU_DTYPE)

    return pl.pallas_call(body, name=name, out_shape=SDS((s, BR), MXU_DTYPE), grid=(s // tb,),
                          in_specs=[_rows(tb, BR), _rows(tb, BR, cb)], out_specs=_rows(tb, BR),
                          compiler_params=_params(1))(h, proj)


def _gate_out_bwd(dycat, dy_cb, h, proj, cb, tb, name):
    s = proj.shape[0]

    def body(dy, h_ref, g_ref, dh_ref, dg_ref):
        dh_ref[...] = dy[...] * _silu(g_ref[...])
        dg_ref[...] = dy[...] * h_ref[...] * _dsilu(g_ref[...])

    big = SDS((s, BR), F32)
    return pl.pallas_call(body, name=name, out_shape=(big, big), grid=(s // tb,),
                          in_specs=[_rows(tb, BR, dy_cb), _rows(tb, BR), _rows(tb, BR, cb)],
                          out_specs=(_rows(tb, BR), _rows(tb, BR)), compiler_params=_params(1))(dycat, h, proj)


def _lru_gates_bwd(proj, lmb, h, conv_w, conv_b, w_cat, b_cat, lam, tb):
    s = proj.shape[0]

    def body(cx, cxp, l_ref, h_ref, hp_ref, w_ref, cb_ref, wc_ref, bc_ref, lam_ref,
             dxc_ref, dpre_ref, xc_ref, dbc_ref, dlam_ref):
        _init_acc(dbc_ref, dlam_ref)
        has_prev = (pl.program_id(0) > 0).astype(F32)
        xc = _conv_taps(cx[...], cxp[...] * has_prev, w_ref, 4) + cb_ref[...]
        xcb = xc.astype(MXU_DTYPE)
        pre = jnp.dot(xcb, wc_ref[...], preferred_element_type=F32) + bc_ref[...]
        _, vjp = jax.vjp(_lru_gate, xc, pre[:, 0:BR], pre[:, BR:2 * BR], lam_ref[...])
        lm = l_ref[...]
        dxc, dpr, dpi, dlam = vjp((lm * _shift_down(h_ref[...], hp_ref[...] * has_prev, 1), lm))
        dpre = jnp.concatenate([dpr, dpi], axis=1)
        dpreb = dpre.astype(MXU_DTYPE)
        dxc_ref[...] = dxc + lax.dot_general(dpreb, wc_ref[...], (((1,), (1,)), ((), ())),
                                             preferred_element_type=F32)
        dpre_ref[...] = dpreb
        xc_ref[...] = xcb
        dbc_ref[...] += _colsum(dpre)
        dlam_ref[...] += dlam

    return pl.pallas_call(
        body, name="lru_gates_bwd",
        out_shape=(SDS((s, BR), F32), SDS((s, 2 * BR), MXU_DTYPE), SDS((s, BR), MXU_DTYPE),
                   SDS((1, 2 * BR), F32), SDS((1, BR), F32)),
        grid=(s // tb,),
        in_specs=[_rows(tb, BR, CB_CX), _prev8(tb, BR, CB_CX), _rows(tb, BR), _rows(tb, BR), _prev8(tb, BR),
                  _const((8, BR)), _const((1, BR)), _const((BR, 2 * BR)), _const((1, 2 * BR)), _const((1, BR))],
        out_specs=(_rows(tb, BR), _rows(tb, 2 * BR), _rows(tb, BR), _const((1, 2 * BR)), _const((1, BR))),
        compiler_params=_params(1))(proj, proj, lmb, h, h, conv_w, conv_b, w_cat, b_cat, lam)


def _conv_c_bwd(dxc, proj, conv_w, tb):
    s = proj.shape[0]

    def body(g, gn, cx, cxp, w_ref, dcx_ref, dw_ref, db_ref):
        _init_acc(dw_ref, db_ref)
        i = pl.program_id(0)
        has_prev = (i > 0).astype(F32)
        has_next = (i < pl.num_programs(0) - 1).astype(F32)
        gt = g[...]
        dcx_ref[...] = _conv_taps_t(gt, gn[...] * has_next, w_ref, 4)
        _conv_wgrad(dw_ref, gt, cx[...], cxp[...] * has_prev, 4)
        db_ref[...] += _colsum(gt)

    return pl.pallas_call(
        body, name="conv_c_bwd", out_shape=(SDS((s, BR), F32), SDS((8, BR), F32), SDS((1, BR), F32)),
        grid=(s // tb,),
        in_specs=[_rows(tb, BR), _next8(tb, BR, s), _rows(tb, BR, CB_CX), _prev8(tb, BR, CB_CX), _const((8, BR))],
        out_specs=(_rows(tb, BR), _const((8, BR)), _const((1, BR))), compiler_params=_params(1),
    )(dxc, dxc, proj, proj, conv_w)


def _s5_disc(lam_re, lam_im, log_dt):
    dt = jnp.exp(log_dt)
    mag = jnp.exp(lam_re * dt)
    ab_re = mag * jnp.cos(lam_im * dt)
    ab_im = mag * jnp.sin(lam_im * dt)
    den = lam_re * lam_re + lam_im * lam_im
    f_re = ((ab_re - 1.0) * lam_re + ab_im * lam_im) / den
    f_im = (ab_im * lam_re - (ab_re - 1.0) * lam_im) / den
    return ab_re, ab_im, f_re, f_im


def _s5_bbar(f_re, f_im, b_re, b_im):
    return f_re * b_re - f_im * b_im, f_re * b_im + f_im * b_re


def _s5_disc_fwd(lam_re, lam_im, log_dt):
    def body(lr, li, ld, o0, o1, o2, o3):
        o0[...], o1[...], o2[...], o3[...] = _s5_disc(lr[...], li[...], ld[...])
    return pl.pallas_call(body, name="s5_disc_fwd", out_shape=(SDS(lam_re.shape, F32),) * 4)(lam_re, lam_im, log_dt)


def _s5_disc_bwd(lam_re, lam_im, log_dt, cts):
    def body(lr, li, ld, c0, c1, c2, c3, o0, o1, o2):
        _, vjp = jax.vjp(_s5_disc, lr[...], li[...], ld[...])
        o0[...], o1[...], o2[...] = vjp((c0[...], c1[...], c2[...], c3[...]))
    return pl.pallas_call(body, name="s5_disc_bwd", out_shape=(SDS(lam_re.shape, F32), SDS(lam_re.shape, F32),
                                                                SDS(log_dt.shape, F32)))(lam_re, lam_im, log_dt, *cts)


def _s5_bbar_fwd(f_re, f_im, b_re, b_im):
    def body(fr, fi, br, bi, o0, o1):
        o0[...], o1[...] = _s5_bbar(fr[...], fi[...], br[...], bi[...])
    return pl.pallas_call(body, name="s5_bbar_fwd", out_shape=(SDS(b_re.shape, F32),) * 2)(f_re, f_im, b_re, b_im)


def _s5_bbar_bwd(f_re, f_im, b_re, b_im, d_re, d_im):
    def body(fr, fi, br, bi, dr, di, o0, o1, o2, o3):
        _, vjp = jax.vjp(_s5_bbar, fr[...], fi[...], br[...], bi[...])
        o0[...], o1[...], o2[...], o3[...] = vjp((dr[...], di[...]))
    col, mat = SDS(f_re.shape, F32), SDS(b_re.shape, F32)
    return pl.pallas_call(body, name="s5_bbar_bwd", out_shape=(col, col, mat, mat))(f_re, f_im, b_re, b_im, d_re, d_im)


def _s5_tail_fwd(ylin, proj, d_skip, w_glu, b_glu, tb):
    s = proj.shape[0]

    def body(yl, u, dg, dk, w_ref, b_ref, o_ref):
        g = jax.nn.gelu(yl[...] + dk[...] * u[...])
        t = jnp.dot(g.astype(MXU_DTYPE), w_ref[...], preferred_element_type=F32) + b_ref[...]
        o_ref[...] = (g * jax.nn.sigmoid(t) * _silu(dg[...])).astype(MXU_DTYPE)

    return pl.pallas_call(
        body, name="s5_tail_fwd", out_shape=SDS((s, BR), MXU_DTYPE), grid=(s // tb,),
        in_specs=[_rows(tb, BR), _rows(tb, BR, CB_DU), _rows(tb, BR, CB_DG), _const((1, BR)), _const((BR, BR)),
                  _const((1, BR))],
        out_specs=_rows(tb, BR), compiler_params=_params(1))(ylin, proj, proj, d_skip, w_glu, b_glu)


def _s5_tail_bwd(dycat, ylin, proj, d_skip, w_glu, b_glu, tb):
    s = proj.shape[0]

    def body(dy, yl, u, dg, dk, w_ref, b_ref, dyl_ref, dus_ref, ddg_ref, g_ref, dt_ref, ddk_ref, dbg_ref):
        _init_acc(ddk_ref, dbg_ref)
        g, gelu_vjp = jax.vjp(jax.nn.gelu, yl[...] + dk[...] * u[...])
        gb = g.astype(MXU_DTYPE)
        sg = jax.nn.sigmoid(jnp.dot(gb, w_ref[...], preferred_element_type=F32) + b_ref[...])
        dz = dy[...] * _silu(dg[...])
        ddg_ref[...] = dy[...] * g * sg * _dsilu(dg[...])
        dt = dz * g * sg * (1.0 - sg)
        dtb = dt.astype(MXU_DTYPE)
        dgel = dz * sg + lax.dot_general(dtb, w_ref[...], (((1,), (1,)), ((), ())), preferred_element_type=F32)
        dyv, = gelu_vjp(dgel)
        dyl_ref[...] = dyv
        dus_ref[...] = dyv * dk[...]
        g_ref[...] = gb
        dt_ref[...] = dtb
        ddk_ref[...] += _colsum(dyv * u[...])
        dbg_ref[...] += _colsum(dt)

    big, half, vec = SDS((s, BR), F32), SDS((s, BR), MXU_DTYPE), SDS((1, BR), F32)
    return pl.pallas_call(
        body, name="s5_tail_bwd", out_shape=(big, big, big, half, half, vec, vec), grid=(s // tb,),
        in_specs=[_rows(tb, BR, 3), _rows(tb, BR), _rows(tb, BR, CB_DU), _rows(tb, BR, CB_DG), _const((1, BR)),
                  _const((BR, BR)), _const((1, BR))],
        out_specs=(_rows(tb, BR),) * 5 + (_const((1, BR)), _const((1, BR))), compiler_params=_params(1),
    )(dycat, ylin, proj, proj, d_skip, w_glu, b_glu)


def _assemble_dproj(da, dqkv, dbg, dcx, dcg, du, dus, ddg, tb):
    s = da.shape[0]

    def body(da_ref, q0, q1, q2, k0, k1, k2, v0, v1, v2, dbg_ref, dcx_ref, dcg_ref, du_ref, dus_ref, ddg_ref, o_ref):
        o_ref[:, 0:4 * BR] = da_ref[...]
        for j, parts in enumerate(((q0, q1, q2), (k0, k1, k2), (v0, v1, v2))):
            o_ref[:, (4 + j) * BR:(5 + j) * BR] = (parts[0][...] + parts[1][...] + parts[2][...]).astype(MXU_DTYPE)
        o_ref[:, 7 * BR:8 * BR] = dbg_ref[...].astype(MXU_DTYPE)
        o_ref[:, 8 * BR:9 * BR] = dcx_ref[...].astype(MXU_DTYPE)
        o_ref[:, 9 * BR:10 * BR] = dcg_ref[...].astype(MXU_DTYPE)
        o_ref[:, 10 * BR:11 * BR] = (du_ref[...] + dus_ref[...]).astype(MXU_DTYPE)
        o_ref[:, 11 * BR:12 * BR] = ddg_ref[...].astype(MXU_DTYPE)

    flat = [t for grp in dqkv for t in grp]
    return pl.pallas_call(
        body, name="assemble_dproj", out_shape=SDS((s, N_IN), MXU_DTYPE), grid=(s // tb,),
        in_specs=[_rows(tb, 4 * BR)] + [_rows(tb, BR)] * 15, out_specs=_rows(tb, N_IN),
        compiler_params=_params(1))(da, *flat, dbg, dcx, dcg, du, dus, ddg)


def _sum_leading(xs, tr, name):
    n, _, c = xs[0].shape
    nl = len(xs)
    tr = min([tr] + [x.shape[1] for x in xs])
    assert all(x.shape[1] % tr == 0 for x in xs), (name, tr)
    nrs = [x.shape[1] // tr for x in xs]
    starts = [sum(nrs[:l]) for l in range(nl)]

    def body(*refs):
        i = pl.program_id(0)
        for l in range(nl):
            @pl.when((i >= starts[l]) & (i < starts[l] + nrs[l]))
            def _():
                acc = refs[l * n][...].astype(F32)
                for ref in refs[l * n + 1:(l + 1) * n]:
                    acc = acc + ref[...].astype(F32)
                refs[nl * n][...] = acc

    specs = [pl.BlockSpec((None, tr, c), functools.partial(
        lambda i, k, l: (k, jnp.clip(i - starts[l], 0, nrs[l] - 1), 0), k=k, l=l)) for l in range(nl) for k in range(n)]
    return pl.pallas_call(body, name=name, out_shape=SDS((sum(nrs) * tr, c), F32), grid=(sum(nrs),), in_specs=specs,
                          out_specs=pl.BlockSpec((tr, c), lambda i: (i, 0)),
                          compiler_params=_params(1))(*[x for x in xs for _ in range(n)])


def _adamw(w, g_parts, m, v, tr, name):
    r, c = w.shape
    tr = min(tr, r)
    n = len(g_parts)
    assert r % tr == 0, (name, r, tr)

    def body(*refs):
        w_ref, m_ref, v_ref = refs[0], refs[1 + n], refs[2 + n]
        g_ref, d_ref, nm_ref, nv_ref = refs[3 + n:]
        g = refs[1][...]
        for ref in refs[2:1 + n]:
            g = g + ref[...]
        mm = ADAM_B1 * m_ref[...] + (1.0 - ADAM_B1) * g
        vv = ADAM_B2 * v_ref[...] + (1.0 - ADAM_B2) * jnp.square(g)
        m_hat = mm / (1.0 - ADAM_B1 ** ADAM_STEP)
        v_hat = vv / (1.0 - ADAM_B2 ** ADAM_STEP)
        g_ref[...] = g
        d_ref[...] = -ADAM_LR * (m_hat / (jnp.sqrt(v_hat) + ADAM_EPS) + ADAM_WD * w_ref[...])
        nm_ref[...] = mm
        nv_ref[...] = vv

    spec = pl.BlockSpec((tr, c), lambda i: (i, 0))
    return pl.pallas_call(body, name=name, out_shape=(SDS((r, c), F32),) * 4, grid=(r // tr,),
                          in_specs=[spec] * (3 + n), out_specs=(spec,) * 4,
                          compiler_params=_params(1))(w, *g_parts, m, v)


def _allgather8(block, name):
    m_per, n = block.shape

    def body(x_ref, out_ref, send_sems, recv_sems, local_sem):
        x, y, c = lax.axis_index("x"), lax.axis_index("y"), lax.axis_index("c")
        me, sibling = (x, y, c), (x, y, 1 - c)
        chips = [(1 - x, y), (x, 1 - y), (1 - x, 1 - y)]

        def rows(px, py, pc):
            return out_ref.at[pl.ds((4 * px + 2 * py + pc) * m_per, m_per), :]

        def copy(k, blk, to, src=None):
            return pltpu.make_async_remote_copy(
                src_ref=rows(*blk) if src is None else src, dst_ref=rows(*blk), send_sem=send_sems.at[k],
                recv_sem=recv_sems.at[k], device_id=to, device_id_type=MESH)

        mine = pltpu.make_async_copy(x_ref, rows(*me), local_sem)
        mine.start()
        first = [copy(0, me, sibling, src=x_ref)]
        first += [copy(1 + j, me, (*chip, c), src=x_ref) for j, chip in enumerate(chips)]
        for cp in first:
            cp.start()
        passed = [copy(4 + j, (*chip, c), sibling) for j, chip in enumerate(chips)]
        for j, chip in enumerate(chips):
            copy(1 + j, (*chip, c), me).wait_recv()
            passed[j].start()
        copy(0, sibling, me).wait_recv()
        for j, chip in enumerate(chips):
            copy(4 + j, (*chip, 1 - c), me).wait_recv()
        for cp in first + passed:
            cp.wait_send()
        mine.wait()

    return pl.pallas_call(
        body, name=name, out_shape=SDS((N_DEV * m_per, n), block.dtype),
        in_specs=[pl.BlockSpec(memory_space=pltpu.VMEM)], out_specs=pl.BlockSpec(memory_space=pltpu.VMEM),
        scratch_shapes=[pltpu.SemaphoreType.DMA((7,)), pltpu.SemaphoreType.DMA((7,)), pltpu.SemaphoreType.DMA],
        compiler_params=_params())(block)


class _Exchange:
    def __init__(self, items, out_shapes):
        self.items, self.out_shapes = list(items), tuple(out_shapes)
        self.arrays = [it[0] for it in self.items]
        n = len(self.items)
        self.n_in, self.n_out = n, len(self.out_shapes)
        self.scratch = [pltpu.SemaphoreType.DMA((n * N_CHIPS,)), pltpu.SemaphoreType.DMA((n * N_CHIPS,)),
                        pltpu.SemaphoreType.DMA((n,))]

    def _copies(self, ins, outs, sems, m):
        send_sems, recv_sems, local_sems = sems
        c = lax.axis_index("c")
        others = [j for j in range(N_CHIPS) if j != m]

        def remote(a, src, dst, to, from_):
            return pltpu.make_async_remote_copy(
                src_ref=src, dst_ref=dst, send_sem=send_sems.at[a * N_CHIPS + to],
                recv_sem=recv_sems.at[a * N_CHIPS + from_], device_id=(to // 2, to % 2, c), device_id_type=MESH)

        local, sends, recvs = [], [], []
        for a, (_, oi, src_of, dst_of) in enumerate(self.items):
            local.append(pltpu.make_async_copy(src_of(ins[a], m), dst_of(outs[oi], m), local_sems.at[a]))
            for j in others:
                sends.append(remote(a, src_of(ins[a], j), dst_of(outs[oi], m), j, m))
                recvs.append(remote(a, src_of(ins[a], m), dst_of(outs[oi], j), j, j))
        return local, sends, recvs

    def _on_my_chip(self, fn):
        chip = 2 * lax.axis_index("x") + lax.axis_index("y")
        for m in range(N_CHIPS):
            pl.when(chip == m)(functools.partial(fn, m))

    def start(self, ins, outs, sems):
        def go(m):
            local, sends, _ = self._copies(ins, outs, sems, m)
            for cp in local + sends:
                cp.start()
        self._on_my_chip(go)

    def wait(self, ins, outs, sems):
        def go(m):
            local, sends, recvs = self._copies(ins, outs, sems, m)
            for cp in recvs:
                cp.wait_recv()
            for cp in sends:
                cp.wait_send()
            for cp in local:
                cp.wait()
        self._on_my_chip(go)


def _half_rows(ref, cc):
    h = ref.shape[-2] // 2
    return ref.at[(slice(None),) * (len(ref.shape) - 2) + (pl.ds(cc * h, h), slice(None))]


class _Gather:
    def __init__(self, items, out_shapes):
        self.items, self.out_shapes = list(items), tuple(out_shapes)
        self.arrays = [it[0] for it in self.items]
        n = len(self.items)
        self.n_in, self.n_out = n, len(self.out_shapes)
        self.scratch = [pltpu.SemaphoreType.DMA((n * N_CHIPS,)) for _ in range(4)] + [pltpu.SemaphoreType.DMA((n,))]

    def _copies(self, ins, outs, sems, m, cc):
        ici_send, ici_recv, d2d_send, d2d_recv, local_sems = sems
        others = [j for j in range(N_CHIPS) if j != m]
        local, sends, arrivals, passed_on, from_sibling = [], [], [], [], []
        for a, (_, oi, src_of, dst_of) in enumerate(self.items):
            src, out = src_of(ins[a]), outs[oi]
            local.append(pltpu.make_async_copy(src, dst_of(out, m), local_sems.at[a]))
            for j in others:
                k = a * N_CHIPS + j
                mine_there = _half_rows(dst_of(out, m), cc)
                theirs_here = _half_rows(dst_of(out, j), cc)
                sends.append(pltpu.make_async_remote_copy(
                    src_ref=_half_rows(src, cc), dst_ref=mine_there, send_sem=ici_send.at[k],
                    recv_sem=ici_recv.at[a * N_CHIPS + m], device_id=(j // 2, j % 2, cc), device_id_type=MESH))
                arrivals.append(pltpu.make_async_remote_copy(
                    src_ref=_half_rows(src, cc), dst_ref=theirs_here, send_sem=ici_send.at[k], recv_sem=ici_recv.at[k],
                    device_id=(j // 2, j % 2, cc), device_id_type=MESH))
                passed_on.append(pltpu.make_async_remote_copy(
                    src_ref=theirs_here, dst_ref=theirs_here, send_sem=d2d_send.at[k], recv_sem=d2d_recv.at[k],
                    device_id=(m // 2, m % 2, 1 - cc), device_id_type=MESH))
                other_half = _half_rows(dst_of(out, j), 1 - cc)
                from_sibling.append(pltpu.make_async_remote_copy(
                    src_ref=other_half, dst_ref=other_half, send_sem=d2d_send.at[k], recv_sem=d2d_recv.at[k],
                    device_id=(m // 2, m % 2, 1 - cc), device_id_type=MESH))
        return local, sends, arrivals, passed_on, from_sibling

    def _on_my_core(self, fn):
        chip = 2 * lax.axis_index("x") + lax.axis_index("y")
        c = lax.axis_index("c")
        for m in range(N_CHIPS):
            for cc in range(2):
                pl.when((chip == m) & (c == cc))(functools.partial(fn, m, cc))

    def start(self, ins, outs, sems):
        def go(m, cc):
            local, sends, _, _, _ = self._copies(ins, outs, sems, m, cc)
            for cp in local + sends:
                cp.start()
        self._on_my_core(go)

    def wait(self, ins, outs, sems):
        def go(m, cc):
            local, sends, arrivals, passed_on, from_sibling = self._copies(ins, outs, sems, m, cc)
            for arrived, onward in zip(arrivals, passed_on):
                arrived.wait_recv()
                onward.start()
            for cp in from_sibling:
                cp.wait_recv()
            for cp in sends + passed_on:
                cp.wait_send()
            for cp in local:
                cp.wait()
        self._on_my_core(go)


def _run_exchange(ex, name):
    def body(*refs):
        ins, outs, sems = refs[:ex.n_in], refs[ex.n_in:ex.n_in + ex.n_out], refs[ex.n_in + ex.n_out:]
        ex.start(ins, outs, sems)
        ex.wait(ins, outs, sems)

    return pl.pallas_call(
        body, name=name, out_shape=ex.out_shapes, in_specs=[ANY] * ex.n_in, out_specs=(ANY,) * ex.n_out,
        scratch_shapes=ex.scratch, compiler_params=_params())(*ex.arrays)


def _sibling_swap(arrays, name):
    n = len(arrays)

    def body(*refs):
        ins, outs = refs[:n], refs[n:2 * n]
        send_sems, recv_sems = refs[2 * n:]
        peer = (lax.axis_index("x"), lax.axis_index("y"), 1 - lax.axis_index("c"))
        cps = [pltpu.make_async_remote_copy(src_ref=ins[a], dst_ref=outs[a], send_sem=send_sems.at[a],
                                            recv_sem=recv_sems.at[a], device_id=peer, device_id_type=MESH)
               for a in range(n)]
        for cp in cps:
            cp.start()
        for cp in cps:
            cp.wait()

    return pl.pallas_call(
        body, name=name, out_shape=tuple(SDS(a.shape, a.dtype) for a in arrays), in_specs=[ANY] * n,
        out_specs=(ANY,) * n, scratch_shapes=[pltpu.SemaphoreType.DMA((n,)), pltpu.SemaphoreType.DMA((n,))],
        compiler_params=_params())(*arrays)


def _block_diag(w):
    h, n, m = w.shape
    eye = jnp.eye(h, dtype=w.dtype)
    return (w[:, :, None, :] * eye[:, None, :, None]).reshape(h * n, h * m)


def _diag_blocks(d, h, col0=0, ncols=None, stacked=1):
    ncols = d.shape[1] - col0 if ncols is None else ncols
    n, m = d.shape[0] // (h * stacked), ncols // h
    lanes = 128
    assert m <= lanes and lanes % m == 0 and col0 % lanes == 0

    def body(d_ref, o_ref):
        for gi in range(h * stacked):
            c = col0 + (gi % h) * m
            chunk = d_ref[gi * n:(gi + 1) * n, c // lanes * lanes:c // lanes * lanes + lanes]
            o_ref[gi * n:(gi + 1) * n, :] = chunk[:, c % lanes:c % lanes + m]

    out = pl.pallas_call(body, name="diag_blocks", out_shape=SDS((stacked * h * n, m), d.dtype),
                         compiler_params=_params())(d)
    return out.reshape(stacked * h, n, m)


S5_CHUNKS = 4
S5_PER = S5_GROUPS // S5_CHUNKS
CH_W = S5_PER * S5_CH
ST_W = S5_PER * S5_STATE


def _bd_stack(mats):
    _, _, n, m = mats.shape
    eye = jnp.eye(S5_PER, dtype=mats.dtype)
    t = mats.reshape(2, S5_CHUNKS, S5_PER, n, m)
    bd = t[:, :, :, :, None, :] * eye[None, None, :, None, :, None]
    return bd.reshape(2 * S5_CHUNKS, S5_PER * n, S5_PER * m).astype(MXU_DTYPE)


def _chunks_chunked(src_ref, buf):
    pt = src_ref.shape[0]
    out = []
    for q in range(S5_CHUNKS):
        buf[q] = src_ref[:, q * CH_W:(q + 1) * CH_W]
        out.append(_load_chunked(buf.at[q], 0, pt).astype(MXU_DTYPE))
    return out


def _expand_into(dst_ref, chunks, w_ref):
    for b in range(2 * S5_CHUNKS):
        dst_ref[:, b * ST_W:(b + 1) * ST_W] = jnp.dot(chunks[b % S5_CHUNKS], w_ref[b], preferred_element_type=F32)


def _reduce_from(src_ref, w_ref, buf, dst_ref):
    pt = src_ref.shape[0]
    for q in range(S5_CHUNKS):
        y = jnp.dot(src_ref[:, q * ST_W:(q + 1) * ST_W].astype(MXU_DTYPE), w_ref[q], preferred_element_type=F32)
        p = S5_CHUNKS + q
        y = y + jnp.dot(src_ref[:, p * ST_W:(p + 1) * ST_W].astype(MXU_DTYPE), w_ref[p], preferred_element_type=F32)
        _store_natural(buf.at[q], 0, pt, y)
        dst_ref[:, q * CH_W:(q + 1) * CH_W] = buf[q]


def _s5_core_fwd(proj, w_bu, w_cx, a_row):
    s = proj.shape[0]
    pt = _scan_tile(s)
    ch2 = 2 * S5_N

    def body(u_ref, wb_ref, wc_ref, a_ref, x_ref, y_ref, carry, pw, buf):
        _expand_into(x_ref, _chunks_chunked(u_ref, buf), wb_ref)
        _scan_tile_in_place(a_ref, x_ref, carry, pw, reverse=False)
        _reduce_from(x_ref, wc_ref, buf, y_ref)

    return pl.pallas_call(
        body, name="s5_core_fwd", out_shape=(SDS((s, ch2), F32), SDS((s, BR), F32)), grid=(s // pt,),
        in_specs=[_rows(pt, BR, CB_DU), _const(w_bu.shape), _const(w_cx.shape), _const((1, ch2))],
        out_specs=(_rows(pt, ch2), _rows(pt, BR)),
        scratch_shapes=[pltpu.VMEM((1, ch2), F32), pltpu.VMEM((pt // 8, ch2), F32),
                        pltpu.VMEM((S5_CHUNKS, pt, CH_W), F32)],
        compiler_params=_params(1))(proj, w_bu, w_cx, a_row)


def _s5_core_bwd(dyl, proj, x, w_dx, w_du, a_row):
    s = proj.shape[0]
    pt = _scan_tile(s)
    nt = s // pt
    ch2 = 2 * S5_N
    ch = S5_N

    def body(dy_ref, u_ref, x_ref, xp_ref, wx_ref, wu_ref, a_ref, du_ref, da_ref, dwb_ref, dwc_ref,
             l_ref, carry, pw, buf, buf2):
        i = pl.program_id(0)
        _init_acc(da_ref, dwb_ref, dwc_ref)
        dy_c = _chunks_chunked(dy_ref, buf)
        u_c = _chunks_chunked(u_ref, buf2)
        _expand_into(l_ref, dy_c, wx_ref)
        _scan_tile_in_place(a_ref, l_ref, carry, pw, reverse=True)
        has_prev = (i < nt - 1).astype(F32)
        row = lax.broadcasted_iota(jnp.int32, (8, ch2), 0)
        first = jnp.where(row == 0, pltpu.roll(xp_ref[...], 1, 0) * has_prev, pltpu.roll(x_ref[pt - 8:pt, :], 1, 0))
        xprev = jnp.concatenate([first, x_ref[0:pt - 8, :]], axis=0)
        lr, li, xr, xi = l_ref[:, 0:ch], l_ref[:, ch:ch2], xprev[:, 0:ch], xprev[:, ch:ch2]
        da_ref[:, 0:ch] += _colsum(lr * xr + li * xi)
        da_ref[:, ch:ch2] += _colsum(li * xr - lr * xi)
        _reduce_from(l_ref, wu_ref, buf, du_ref)
        tn = (((0,), (0,)), ((), ()))
        for b in range(2 * S5_CHUNKS):
            cols, rows = slice(b * ST_W, (b + 1) * ST_W), slice(b * CH_W, (b + 1) * CH_W)
            dwb_ref[rows, :] += lax.dot_general(u_c[b % S5_CHUNKS], l_ref[:, cols].astype(MXU_DTYPE), tn,
                                                preferred_element_type=F32)
            dwc_ref[rows, :] += lax.dot_general(dy_c[b % S5_CHUNKS], x_ref[:, cols].astype(MXU_DTYPE), tn,
                                                preferred_element_type=F32)

    rev = lambda w, cb=0: pl.BlockSpec((pt, w), lambda i: (nt - 1 - i, cb))
    halo = pl.BlockSpec((8, ch2), lambda i: (jnp.maximum((nt - 1 - i) * (pt // 8) - 1, 0), 0))
    wshape = SDS((2 * S5_CHUNKS * CH_W, ST_W), F32)
    return pl.pallas_call(
        body, name="s5_core_bwd", out_shape=(SDS((s, BR), F32), SDS((1, ch2), F32), wshape, wshape), grid=(nt,),
        in_specs=[rev(BR, 0), rev(BR, CB_DU), rev(ch2), halo, _const(w_dx.shape), _const(w_du.shape),
                  _const((1, ch2))],
        out_specs=(rev(BR), _const((1, ch2)), _const(wshape.shape), _const(wshape.shape)),
        scratch_shapes=[pltpu.VMEM((pt, ch2), F32), pltpu.VMEM((1, ch2), F32), pltpu.VMEM((pt // 8, ch2), F32),
                        pltpu.VMEM((S5_CHUNKS, pt, CH_W), F32), pltpu.VMEM((S5_CHUNKS, pt, CH_W), F32)],
        compiler_params=_params(1))(dyl, proj, x, x, w_dx, w_du, a_row)


def _tiles(s):
    return dict(tb=min(512, s), tln=min(256, s))


def _layer_weights(p, l):
    pad8 = lambda w: jnp.pad(w, ((0, 8 - w.shape[0]), (0, 0)))
    return dict(
        conv_a=pad8(p["conv_a"][l]), conv_c=pad8(p["conv_c"][l]), conv_c_b=p["conv_c_b"][l][None],
        w_cat=jnp.concatenate([_block_diag(p["lru_wa"][l]), _block_diag(p["lru_wx"][l])], axis=1).astype(MXU_DTYPE),
        b_cat=jnp.concatenate([p["lru_ba"][l], p["lru_bx"][l]])[None], lam=p["lru_lambda"][l][None],
        lam_re=p["s5_lam_re"][l], lam_im=p["s5_lam_im"][l], log_dt=p["s5_log_dt"][l][:, None],
        b_re=p["s5_b_re"][l].reshape(S5_N, S5_CH), b_im=p["s5_b_im"][l].reshape(S5_N, S5_CH),
        c_re=p["s5_c_re"][l], c_im=p["s5_c_im"][l], d_skip=p["s5_d"][l][None], b_glu=p["s5_b_glu"][l][None],
        ln_g=p["ln_g"][l][None], ln_b=p["ln_b"][l][None])


def _s5_matrices(lw):
    ab_re, ab_im, f_re, f_im = _s5_disc_fwd(lw["lam_re"], lw["lam_im"], lw["log_dt"])
    f_re, f_im = f_re.reshape(S5_N, 1), f_im.reshape(S5_N, 1)
    bb_re, bb_im = _s5_bbar_fwd(f_re, f_im, lw["b_re"], lw["b_im"])
    bb = jnp.stack([bb_re, bb_im]).reshape(2, S5_GROUPS, S5_STATE, S5_CH)
    cc = jnp.stack([lw["c_re"], -lw["c_im"]])
    a_row = jnp.concatenate([ab_re.reshape(1, S5_N), ab_im.reshape(1, S5_N)], axis=1)
    return dict(f_re=f_re, f_im=f_im, a_row=a_row, w_bu=_bd_stack(jnp.swapaxes(bb, 2, 3)), w_du=_bd_stack(bb),
                w_cx=_bd_stack(jnp.swapaxes(cc, 2, 3)), w_dx=_bd_stack(cc))


def _mm_hooked(hook, *args, **kw):
    if hook is None:
        return _mm(*args, **kw)
    out = _mm(*args, carry=hook[0], **kw)
    hook[1](out[1:])
    return out[0]


def _layer_fwd(x, ada, w_in, get_rest, lw, s5m, bias_tabs, hooks=None):
    s = x.shape[0]
    t = _tiles(s)
    tb = t["tb"]
    shift, scale, gate = ada
    hooks = hooks or {}
    h = _modulate(x, scale, shift, tb)
    proj = _mm_hooked(hooks.get("in_proj"), h, w_in, name="in_proj", tm=1024, tn=1024, tk=D_MODEL)
    w_out, w_glu = get_rest()
    y_a = _branch_a_fwd(proj, lw["conv_a"], tb)
    os_, lses = [], []
    for g, (_, dil) in enumerate(DILATIONS):
        o, lse = _attn_fwd(proj, bias_tabs[g], dil)
        os_.append(o)
        lses.append(lse)
    y_b = _attn_combine(os_, lses, proj, tb)
    lru_a, lru_b = _lru_gates_fwd(proj, lw["conv_c"], lw["conv_c_b"], lw["w_cat"], lw["b_cat"], lw["lam"], tb)
    lru_h = _scan_real(lru_a, lru_b, reverse=False, tb=tb, name="lru_scan")
    y_c = _gate_out(lru_h, proj, CB_CG, tb, "lru_out")
    s5_x, ylin = _s5_core_fwd(proj, s5m["w_bu"], s5m["w_cx"], s5m["a_row"])
    y_d = _s5_tail_fwd(ylin, proj, lw["d_skip"], w_glu, lw["b_glu"], tb)
    ycat = jnp.concatenate([y_a, y_b, y_c, y_d], axis=1)
    x_next, xhat, y, rstd = _out_ln(ycat, w_out, x, gate, lw["ln_g"], lw["ln_b"], t["tln"])
    saved = dict(x=x, h=h, proj=proj, os=os_, lses=lses, lru_a=lru_a, lru_h=lru_h, s5_x=s5_x, ylin=ylin, ycat=ycat,
                 xhat=xhat, y=y, rstd=rstd)
    return x_next, saved


def _layer_bwd(dxn, sv, ada, w_in, w_out, w_glu, lw, s5m, bias_tabs, head_ones, hooks=None):
    s = dxn.shape[0]
    t = _tiles(s)
    tb = t["tb"]
    shift, scale, gate = ada
    proj = sv["proj"]
    g = {}
    hook = lambda name: hooks[name](g) if hooks and name in hooks else None
    dyb, dxa, g["ln_g"], g["ln_b"], dgate = _ln_bwd(dxn, sv["xhat"], sv["y"], sv["rstd"], lw["ln_g"], gate, t["tln"])
    g["w_out"] = _mm_hooked(hook("dw_out"), sv["ycat"], dyb, name="dw_out", ta=True, out_dtype=WIRE_DTYPE,
                            tm=1024, tn=1024, tk=1024)
    dycat = _mm(dyb, w_out, name="dycat", tb=True, tm=1024, tn=1024, tk=D_MODEL)
    da, dconv_a = _branch_a_bwd(dycat, proj, lw["conv_a"], tb)
    g["conv_a"] = dconv_a[0:3]
    pre = _attn_bwd_pre(dycat, sv["os"], sv["lses"], proj, head_ones, tb)
    dbg, dos, dms = pre[0], pre[1:4], pre[4:7]
    dqkv, dbias = [], []
    for gi, (_, dil) in enumerate(DILATIONS):
        hk = hook(f"attn_bwd_d{dil}")
        dq, dk, dv, dbi, *got = _attn_bwd(proj, dos[gi], sv["lses"][gi], dms[gi], bias_tabs[gi], dil,
                                          carry=hk and hk[0])
        if hk:
            hk[1](got)
        dqkv.append((dq, dk, dv))
        dbias.append(dbi)
    dqkv = list(zip(*dqkv))
    dh, dcg = _gate_out_bwd(dycat, 2, sv["lru_h"], proj, CB_CG, tb, "lru_out_bwd")
    lmb = _scan_real(sv["lru_a"], dh, reverse=True, tb=tb, name="lru_scan_bwd")
    dxc, dpre, xcb, dbcat, dlam = _lru_gates_bwd(proj, lmb, sv["lru_h"], lw["conv_c"], lw["conv_c_b"], lw["w_cat"],
                                                  lw["b_cat"], lw["lam"], tb)
    dwcat = _mm(xcb, dpre, name="dw_lru", ta=True, tn=1024)
    g["lru_wa"] = _diag_blocks(dwcat, LRU_HEADS, 0, BR)
    g["lru_wx"] = _diag_blocks(dwcat, LRU_HEADS, BR, BR)
    g["lru_ba"], g["lru_bx"], g["lru_lambda"] = dbcat[0, 0:BR], dbcat[0, BR:2 * BR], dlam[0]
    dcx, dconv_c, dccb = _conv_c_bwd(dxc, proj, lw["conv_c"], tb)
    g["conv_c"], g["conv_c_b"] = dconv_c[0:4], dccb[0]
    dyl, dus, ddg, gb, dtb, ddk, dbglu = _s5_tail_bwd(dycat, sv["ylin"], proj, lw["d_skip"], w_glu, lw["b_glu"], tb)
    g["s5_d"], g["s5_b_glu"] = ddk[0], dbglu[0]
    g["s5_w_glu"] = _mm(gb, dtb, name="dw_glu", ta=True, out_dtype=WIRE_DTYPE)
    du, dab, dwb8, dwc8 = _s5_core_bwd(dyl, proj, sv["s5_x"], s5m["w_dx"], s5m["w_du"], s5m["a_row"])
    per_group = lambda d8: _diag_blocks(d8, S5_PER, stacked=2 * S5_CHUNKS).reshape(2, S5_GROUPS, S5_CH, S5_STATE)
    dbb, dcc = per_group(dwb8), per_group(dwc8)
    from_bd = lambda half: jnp.swapaxes(dbb[half], 1, 2).reshape(S5_N, S5_CH)
    df_re, df_im, db_re, db_im = _s5_bbar_bwd(s5m["f_re"], s5m["f_im"], lw["b_re"], lw["b_im"],
                                              from_bd(0), from_bd(1))
    shp = (S5_GROUPS, S5_STATE)
    g["s5_lam_re"], g["s5_lam_im"], dlog_dt = _s5_disc_bwd(
        lw["lam_re"], lw["lam_im"], lw["log_dt"],
        (dab[:, 0:S5_N].reshape(shp), dab[:, S5_N:].reshape(shp), df_re.reshape(shp), df_im.reshape(shp)))
    g["s5_log_dt"] = dlog_dt[:, 0]
    g["s5_b_re"] = db_re.reshape(S5_GROUPS, S5_STATE, S5_CH)
    g["s5_b_im"] = db_im.reshape(S5_GROUPS, S5_STATE, S5_CH)
    g["s5_c_re"], g["s5_c_im"] = dcc[0], -dcc[1]
    dproj = _assemble_dproj(da, dqkv, dbg, dcx, dcg, du, dus, ddg, tb)
    g["w_in"] = _mm_hooked(hook("dw_in"), sv["h"], dproj, name="dw_in", ta=True, out_dtype=WIRE_DTYPE,
                           tm=1024, tn=1536, tk=1024)
    dhm = _mm_hooked(hook("dh"), dproj, w_in, name="dh", tb=True, tm=1024, tn=1024, tk=1536)
    dx, dshift, dscale = _mod_bwd(dhm, dxa, sv["x"], scale, tb)
    g["ada"] = jnp.concatenate([dshift[0], dscale[0], dgate[0]])
    return dx, g, dbias


SMALL = ("rel_bias", "conv_a", "conv_c", "conv_c_b", "lru_wa", "lru_ba", "lru_wx", "lru_bx", "lru_lambda",
         "s5_lam_re", "s5_lam_im", "s5_log_dt", "s5_b_re", "s5_b_im", "s5_c_re", "s5_c_im", "s5_d", "s5_b_glu",
         "ln_g", "ln_b")
PER_LAYER_SMALL = SMALL[1:]


def _local_step(x, target, ada_rows, w_in, w_out, w_glu, p, comm=None):
    if comm is None:
        get_w_in = lambda l: w_in[l]
        get_rest = lambda l: (w_out[l], w_glu[l])
        fwd_hooks = bwd_hooks = lambda *_: None
    else:
        get_w_in, get_rest, fwd_hooks, bwd_hooks = comm.w_in, comm.rest, comm.fwd_hooks, comm.bwd_hooks
    s = x.shape[0]
    buckets = _bucket_maps()
    bias_tabs = _bias_tables(p["rel_bias"], buckets)
    head_ones = _block_diag(jnp.ones((ATT_HEADS, HEAD_DIM, HEAD_DIM), MXU_DTYPE))
    lws = [_layer_weights(p, l) for l in range(DEPTH)]
    s5ms = [_s5_matrices(lw) for lw in lws]
    adas = [tuple(ada_rows[l, k * D_MODEL:(k + 1) * D_MODEL][None] for k in range(3)) for l in range(DEPTH)]
    saved = []
    for l in range(DEPTH):
        x, sv = _layer_fwd(x, adas[l], get_w_in(l), functools.partial(get_rest, l), lws[l], s5ms[l], bias_tabs,
                           fwd_hooks(l))
        saved.append(sv)
    loss, dx = _loss_head(x, target, _tiles(s)["tb"])
    grads = [None] * DEPTH
    dbias_sum = []
    for l in reversed(range(DEPTH)):
        dx, grads[l], dbias = _layer_bwd(dx, saved[l], adas[l], get_w_in(l), *get_rest(l), lws[l], s5ms[l],
                                         bias_tabs, head_ones, bwd_hooks(l, grads))
        dbias_sum.append(jnp.stack(dbias))
    drel = _rel_bias_grad(jnp.stack(dbias_sum), buckets)[:, 0:ATT_HEADS]
    small = {n: jnp.stack([grads[l][n] for l in range(DEPTH)]) for n in PER_LAYER_SMALL + ("ada",)}
    small["rel_bias"] = drel
    big = {n: [grads[l][n] for l in range(DEPTH)] for n in ("w_in", "w_out", "s5_w_glu")}
    return loss, dx, big, small


PACK_ROWS = 256


def _pack(parts):
    flat = jnp.concatenate([t.reshape(-1).astype(F32) for t in parts])
    n = flat.shape[0]
    rows = -(-n // (PACK_ROWS * 128)) * PACK_ROWS
    return jnp.pad(flat, (0, rows * 128 - n)).reshape(rows, 128)


def _unpack(packed, shapes):
    flat = packed.reshape(packed.shape[:-2] + (-1,))
    out, off = [], 0
    for shp in shapes:
        size = math.prod(shp)
        out.append(flat[..., off:off + size].reshape(flat.shape[:-1] + tuple(shp)))
        off += size
    return out


def _take_cols(t, chip, width):
    return lax.dynamic_slice_in_dim(t, chip * width, width, axis=t.ndim - 1)


class _Comm:
    IN_W, OUT_R, GLU_R = N_IN // N_CHIPS, D_MODEL // N_CHIPS, BR // N_CHIPS

    def __init__(self, w_in_b, w_out_b, w_glu_b):
        assert DEPTH == 2
        self.shards = (w_in_b, w_out_b, w_glu_b)
        in_w = self.IN_W
        self.w_in_full = {0: _run_exchange(_Gather(
            [(w_in_b, 0, lambda ref: ref.at[0], lambda ref, j: ref.at[:, pl.ds(j * in_w, in_w)])],
            [SDS((D_MODEL, N_IN), WIRE_DTYPE)]), "gather_w_in0")[0]}
        self.w_out_full = self.w_glu_full = None
        self.recv = {}

    def w_in(self, l):
        return self.w_in_full[l]

    def rest(self, l):
        return self.w_out_full[l], self.w_glu_full[l]

    def fwd_hooks(self, l):
        if l != 0:
            return None
        w_in_b, w_out_b, w_glu_b = self.shards
        in_w, out_r, glu_r = self.IN_W, self.OUT_R, self.GLU_R
        whole = lambda ref: ref
        items = [(w_out_b, 0, whole, lambda ref, j: ref.at[:, pl.ds(j * out_r, out_r), :]),
                 (w_glu_b, 1, whole, lambda ref, j: ref.at[:, pl.ds(j * glu_r, glu_r), :]),
                 (w_in_b, 2, lambda ref: ref.at[1], lambda ref, j: ref.at[:, pl.ds(j * in_w, in_w)])]
        shapes = [SDS((DEPTH, D_MODEL, D_MODEL), WIRE_DTYPE), SDS((DEPTH, BR, BR), WIRE_DTYPE),
                  SDS((D_MODEL, N_IN), WIRE_DTYPE)]

        def done(got):
            self.w_out_full, self.w_glu_full, self.w_in_full[1] = got

        return {"in_proj": (_Gather(items, shapes), done)}

    W_IN_ROWS = ((0, 1024), (1024, 512), (1536, 512))

    def _scatter(self, parts):
        in_w, out_r, glu_r = self.IN_W, self.OUT_R, self.GLU_R
        items, shapes, keys = [], [], []
        for oi, (name, l, arr, *rows) in enumerate(parts):
            if name == "w_in":
                r0, nr = rows[0] if rows else (0, D_MODEL)
                cut = functools.partial(lambda ref, j, r0, nr: ref.at[pl.ds(r0, nr), pl.ds(j * in_w, in_w)], r0=r0, nr=nr)
                shard = (nr, in_w)
            elif name == "w_out":
                cut, shard = (lambda ref, j: ref.at[pl.ds(j * out_r, out_r), :]), (out_r, D_MODEL)
            else:
                cut, shard = (lambda ref, j: ref.at[pl.ds(j * glu_r, glu_r), :]), (glu_r, BR)
            items.append((arr, oi, cut, lambda ref, j: ref.at[j]))
            shapes.append(SDS((N_CHIPS,) + shard, WIRE_DTYPE))
            keys.append((name, l) + ((rows[0][0],) if rows else ()))

        def done(got):
            self.recv.update(zip(keys, got))

        return _Exchange(items, shapes), done

    def received(self, name):
        return [self.recv[k] for k in sorted(k for k in self.recv if k[0] == name)]

    def bwd_hooks(self, l, grads):
        if l != 0:
            return None
        g1 = grads[1]
        w_in_part = lambda k: (lambda g: self._scatter([("w_in", 1, g1["w_in"], self.W_IN_ROWS[k])]))
        return {"dw_out": lambda g: self._scatter([("w_out", 1, g1["w_out"]), ("s5_w_glu", 1, g1["s5_w_glu"])]),
                "attn_bwd_d16": w_in_part(0), "attn_bwd_d4": w_in_part(1), "attn_bwd_d1": w_in_part(2),
                "dw_in": lambda g: self._scatter([("w_out", 0, g["w_out"]), ("s5_w_glu", 0, g["s5_w_glu"])]),
                "dh": lambda g: self._scatter([("w_in", 0, g["w_in"])])}


def kernel(x, c, rel_bias, w_ada, b_ada, w_in, conv_a, conv_c, conv_c_b, lru_wa, lru_ba, lru_wx, lru_bx, lru_lambda, s5_lam_re, s5_lam_im, s5_log_dt, s5_b_re, s5_b_im, s5_c_re, s5_c_im, s5_d, s5_w_glu, s5_b_glu, w_out, ln_g, ln_b, loss_target, m_rel_bias, m_w_ada, m_b_ada, m_w_in, m_conv_a, m_conv_c, m_conv_c_b, m_lru_wa, m_lru_ba, m_lru_wx, m_lru_bx, m_lru_lambda, m_s5_lam_re, m_s5_lam_im, m_s5_log_dt, m_s5_b_re, m_s5_b_im, m_s5_c_re, m_s5_c_im, m_s5_d, m_s5_w_glu, m_s5_b_glu, m_w_out, m_ln_g, m_ln_b, v_rel_bias, v_w_ada, v_b_ada, v_w_in, v_conv_a, v_conv_c, v_conv_c_b, v_lru_wa, v_lru_ba, v_lru_wx, v_lru_bx, v_lru_lambda, v_s5_lam_re, v_s5_lam_im, v_s5_log_dt, v_s5_b_re, v_s5_b_im, v_s5_c_re, v_s5_c_im, v_s5_d, v_s5_w_glu, v_s5_b_glu, v_w_out, v_ln_g, v_ln_b):
    args = dict(locals())
    names = ("rel_bias", "w_ada", "b_ada", "w_in", "conv_a", "conv_c", "conv_c_b", "lru_wa", "lru_ba", "lru_wx",
             "lru_bx", "lru_lambda", "s5_lam_re", "s5_lam_im", "s5_log_dt", "s5_b_re", "s5_b_im", "s5_c_re", "s5_c_im",
             "s5_d", "s5_w_glu", "s5_b_glu", "w_out", "ln_g", "ln_b")
    w = {n: args[n] for n in names}
    mom = {n: args["m_" + n] for n in names}
    var = {n: args["v_" + n] for n in names}
    chip = 2 * lax.axis_index("x") + lax.axis_index("y")
    me = 2 * chip + lax.axis_index("c")
    ada_w = 3 * D_MODEL // N_CHIPS
    in_w = N_IN // N_CHIPS
    out_r = D_MODEL // N_CHIPS
    glu_r = BR // N_CHIPS
    conv_w = BR // N_CHIPS

    comm = _Comm(w["w_in"].astype(WIRE_DTYPE), w["w_out"].astype(WIRE_DTYPE), w["s5_w_glu"].astype(WIRE_DTYPE))

    taps = jnp.concatenate([w["conv_a"].reshape(DEPTH * 3, conv_w), w["conv_c"].reshape(DEPTH * 4, conv_w)])
    first = jnp.concatenate([c, jnp.pad(taps, ((0, 1), (0, D_MODEL - conv_w)))])
    got = _allgather8(first, "gather_c_taps").reshape(N_CHIPS, 2, 16, D_MODEL)
    c_all = got[:, :, 0].reshape(N_DEV, D_MODEL)
    taps_all = jnp.transpose(got[:, 0, 1:1 + DEPTH * 7, 0:conv_w], (1, 0, 2)).reshape(DEPTH * 7, BR)
    conv_a_f = taps_all[0:DEPTH * 3].reshape(DEPTH, 3, BR)
    conv_c_f = taps_all[DEPTH * 3:].reshape(DEPTH, 4, BR)

    cond_all = _silu_rows(c_all)
    ada_part = jnp.stack([_mm(cond_all, w["w_ada"][l], name="ada_fwd", tk=D_MODEL, tn=512,
                              bias=_take_cols(w["b_ada"][l][None], chip, ada_w)) for l in range(DEPTH)])
    ada_all = _allgather8(ada_part.reshape(DEPTH * N_DEV, ada_w), "gather_ada")
    ada_all = ada_all.reshape(N_CHIPS, 2, DEPTH, N_DEV, ada_w)[:, 0]
    ada_rows = lax.dynamic_index_in_dim(ada_all, me, axis=2, keepdims=False)
    ada_rows = jnp.transpose(ada_rows, (1, 0, 2)).reshape(DEPTH, 3 * D_MODEL)

    p = dict(w)
    p["conv_a"], p["conv_c"] = conv_a_f, conv_c_f
    loss, dx, _, small = _local_step(x[0], loss_target[0], ada_rows, None, None, None, p, comm)

    sums = [_sum_leading(comm.received(name), 256, "sum_chips") for name in ("w_in", "w_out", "s5_w_glu")]
    others = _sibling_swap(sums, "swap_cores")
    out = {}
    for name, mine, other in zip(("w_in", "w_out", "s5_w_glu"), sums, others):
        shp = w[name].shape
        flat = lambda t: t.reshape(-1, shp[-1])
        res = _adamw(flat(w[name]), [mine, other], flat(mom[name]), flat(var[name]), 128, "adamw_big")
        out[name] = [t.reshape(shp) for t in res]

    small_names = SMALL + ("ada",)
    small["loss"] = loss
    order = small_names + ("loss",)
    shapes = [small[n].shape for n in order]
    gathered = _allgather8(_pack([small[n] for n in order]), "gather_small")
    gathered = gathered.reshape(N_DEV, -1, 128)
    total = dict(zip(order, _unpack(_sum_leading([gathered], PACK_ROWS, "sum_devices"), shapes)))
    d_ada_all = _unpack(gathered, shapes)[order.index("ada")]
    g_small = {n: total[n] for n in SMALL}
    g_small["conv_a"] = _take_cols(total["conv_a"], chip, conv_w)
    g_small["conv_c"] = _take_cols(total["conv_c"], chip, conv_w)
    g_small["b_ada"] = total["ada"]
    g_w_ada = jnp.stack([_mm(cond_all, _take_cols(d_ada_all[:, l], chip, ada_w), name="dw_ada", ta=True, tn=ada_w)
                         for l in range(DEPTH)])
    upd_names = SMALL + ("b_ada",)
    upd_shapes = [w[n].shape for n in upd_names]
    res = _adamw(_pack([w[n] for n in upd_names]), [_pack([g_small[n] for n in upd_names])],
                 _pack([mom[n] for n in upd_names]), _pack([var[n] for n in upd_names]), PACK_ROWS, "adamw_small")
    for k, t in enumerate(res):
        for n, val in zip(upd_names, _unpack(t, upd_shapes)):
            out.setdefault(n, [None] * 4)[k] = val
    shp = w["w_ada"].shape
    flat = lambda t: t.reshape(-1, shp[-1])
    out["w_ada"] = [t.reshape(shp) for t in _adamw(flat(w["w_ada"]), [flat(g_w_ada)], flat(mom["w_ada"]),
                                                  flat(var["w_ada"]), 128, "adamw_ada")]
    return (total["loss"].reshape(()), dx[None]) + tuple(out[n][k] for k in range(4) for n in names)
```

```python
import functools
import math

import jax
import jax.numpy as jnp
from jax import lax
from jax.experimental import pallas as pl
from jax.experimental.pallas import tpu as pltpu

F32 = jnp.float32
MXU_DTYPE = jnp.bfloat16
WIRE_DTYPE = jnp.bfloat16
SDS = jax.ShapeDtypeStruct
MESH = pl.DeviceIdType.MESH
ANY = pl.BlockSpec(memory_space=pl.ANY)
VMEM_LIMIT = 48 * 1024 * 1024

D_MODEL = 2048
DEPTH = 2
BR = 512
ATT_HEADS = 8
HEAD_DIM = 64
DILATIONS = ((128, 1), (512, 4), (2048, 16))
BLK = 128
REL_BUCKETS = 32
REL_MAX_DIST = 2048
LRU_HEADS = 8
LRU_C = 8.0
S5_CH = 16
S5_GROUPS = 32
S5_STATE = 64
S5_N = S5_GROUPS * S5_STATE
N_IN = 12 * BR
ALPHA = (2 * DEPTH) ** 0.25
LN_EPS = 1e-5
NEG = -1e30
ADAM_LR, ADAM_B1, ADAM_B2, ADAM_EPS, ADAM_WD, ADAM_STEP = 0.001, 0.9, 0.999, 1e-08, 0.01, 10
CB_AB, CB_AC, CB_AX, CB_AG, CB_Q, CB_K, CB_V, CB_BG, CB_CX, CB_CG, CB_DU, CB_DG = range(12)
N_CHIPS = 4
N_DEV = 8


def _params(n_axes=0):
    kw = {"dimension_semantics": ("arbitrary",) * n_axes} if n_axes else {}
    return pltpu.CompilerParams(vmem_limit_bytes=VMEM_LIMIT, **kw)


def _rows(tb, w, cb=0):
    return pl.BlockSpec((tb, w), lambda i: (i, cb))


def _prev8(tb, w, cb=0):
    return pl.BlockSpec((8, w), lambda i: (jnp.maximum(i * (tb // 8) - 1, 0), cb))


def _next8(tb, w, n_rows, cb=0):
    return pl.BlockSpec((8, w), lambda i: (jnp.minimum((i + 1) * (tb // 8), n_rows // 8 - 1), cb))


def _const(shape):
    return pl.BlockSpec(shape, lambda *_: (0,) * len(shape))


def _silu(x):
    return x * jax.nn.sigmoid(x)


def _dsilu(x):
    s = jax.nn.sigmoid(x)
    return s * (1.0 + x * (1.0 - s))


def _shift_down(cur, prev8, j):
    rolled = pltpu.roll(cur, j, 0)
    row = lax.broadcasted_iota(jnp.int32, (8, cur.shape[1]), 0)
    first = jnp.where(row < j, pltpu.roll(prev8, j, 0), rolled[0:8])
    return jnp.concatenate([first, rolled[8:]], axis=0)


def _shift_up(cur, next8, j):
    t = cur.shape[0]
    rolled = pltpu.roll(cur, t - j, 0)
    row = lax.broadcasted_iota(jnp.int32, (8, cur.shape[1]), 0)
    last = jnp.where(row >= 8 - j, pltpu.roll(next8, 8 - j, 0), rolled[t - 8:t])
    return jnp.concatenate([rolled[:t - 8], last], axis=0)


def _colsum(x):
    return jnp.sum(x, axis=0, keepdims=True)


def _init_acc(*refs):
    @pl.when(pl.program_id(0) == 0)
    def _():
        for r in refs:
            r[...] = jnp.zeros_like(r)


def _call(body, *, name, out_shape, grid, in_specs, out_specs, scratch_shapes, args, carry=None):
    out_shape, out_specs, in_specs = tuple(out_shape), tuple(out_specs), list(in_specs)
    scratch_shapes = list(scratch_shapes)
    if carry is None:
        return pl.pallas_call(body, name=name, out_shape=out_shape, grid=grid, in_specs=in_specs, out_specs=out_specs,
                              scratch_shapes=scratch_shapes, compiler_params=_params(len(grid)))(*args)
    n_in, n_out, n_scr = len(in_specs), len(out_shape), len(scratch_shapes)

    def wrapped(*refs):
        ins, refs = refs[:n_in], refs[n_in:]
        x_ins, refs = refs[:carry.n_in], refs[carry.n_in:]
        outs, refs = refs[:n_out], refs[n_out:]
        x_outs, refs = refs[:carry.n_out], refs[carry.n_out:]
        scr, x_sems = refs[:n_scr], refs[n_scr:]
        at = [pl.program_id(d) for d in range(len(grid))]
        first = functools.reduce(lambda p, q: p & q, [i == 0 for i in at])
        last = functools.reduce(lambda p, q: p & q, [i == g - 1 for i, g in zip(at, grid)])
        pl.when(first)(lambda: carry.start(x_ins, x_outs, x_sems))
        body(*ins, *outs, *scr)
        pl.when(last)(lambda: carry.wait(x_ins, x_outs, x_sems))

    return pl.pallas_call(
        wrapped, name=name, out_shape=out_shape + carry.out_shapes, grid=grid, in_specs=in_specs + [ANY] * carry.n_in,
        out_specs=out_specs + (ANY,) * carry.n_out, scratch_shapes=scratch_shapes + carry.scratch,
        compiler_params=_params(len(grid)))(*args, *carry.arrays)


def _mm(a, b, *, name, ta=False, tb=False, out_dtype=F32, tm=512, tn=512, tk=512, a_col0=0, a_ncols=None, bias=None,
        carry=None):
    a_ncols = a.shape[1] - a_col0 if a_ncols is None else a_ncols
    m, k = (a_ncols, a.shape[0]) if ta else (a.shape[0], a_ncols)
    n = b.shape[0] if tb else b.shape[1]
    assert k == (b.shape[1] if tb else b.shape[0]), (name, a.shape, b.shape)
    tm, tn, tk = min(tm, m), min(tn, n), min(tk, k)
    nk = k // tk
    a_off = a_col0 // (tm if ta else tk)
    assert m % tm == 0 and n % tn == 0 and k % tk == 0 and a_col0 % (tm if ta else tk) == 0, (name, m, n, k)

    def body(*refs):
        if bias is None:
            a_ref, b_ref, o_ref, acc = refs
        else:
            a_ref, b_ref, bias_ref, o_ref, acc = refs
        kk = pl.program_id(2)

        @pl.when(kk == 0)
        def _():
            acc[...] = jnp.zeros_like(acc)

        dims = (((0 if ta else 1,), (1 if tb else 0,)), ((), ()))
        acc[...] += lax.dot_general(a_ref[...].astype(MXU_DTYPE), b_ref[...].astype(MXU_DTYPE), dims,
                                    preferred_element_type=F32)

        @pl.when(kk == nk - 1)
        def _():
            r = acc[...]
            if bias is not None:
                r = r + bias_ref[...]
            o_ref[...] = r.astype(out_dtype)

    a_spec = (pl.BlockSpec((tk, tm), lambda i, j, kk: (kk, i + a_off)) if ta
              else pl.BlockSpec((tm, tk), lambda i, j, kk: (i, kk + a_off)))
    b_spec = (pl.BlockSpec((tn, tk), lambda i, j, kk: (j, kk)) if tb
              else pl.BlockSpec((tk, tn), lambda i, j, kk: (kk, j)))
    in_specs, args = [a_spec, b_spec], [a, b]
    if bias is not None:
        in_specs.append(pl.BlockSpec((1, tn), lambda i, j, kk: (0, j)))
        args.append(bias)
    out = _call(body, name=name, out_shape=[SDS((m, n), out_dtype)], grid=(m // tm, n // tn, nk), in_specs=in_specs,
                out_specs=[pl.BlockSpec((tm, tn), lambda i, j, kk: (i, j))],
                scratch_shapes=[pltpu.VMEM((tm, tn), F32)], args=args, carry=carry)
    return out[0] if carry is None else out


def _silu_rows(c_all):
    def body(c_ref, o_ref):
        o_ref[...] = _silu(c_ref[...])
    return pl.pallas_call(body, name="cond_silu", out_shape=SDS(c_all.shape, F32))(c_all)


def _modulate(x, scale, shift, tb):
    s, d = x.shape

    def body(x_ref, sc_ref, sh_ref, o_ref):
        o_ref[...] = (x_ref[...] * (1.0 + sc_ref[...]) + sh_ref[...]).astype(MXU_DTYPE)

    return pl.pallas_call(body, name="modulate", out_shape=SDS((s, d), MXU_DTYPE), grid=(s // tb,),
                          in_specs=[_rows(tb, d), _const((1, d)), _const((1, d))], out_specs=_rows(tb, d),
                          compiler_params=_params(1))(x, scale, shift)


def _out_ln(ycat, w_out, x, gate, ln_g, ln_b, tb):
    s, d = x.shape

    def body(yc_ref, w_ref, x_ref, gt_ref, g_ref, b_ref, xn_ref, xh_ref, y_ref, rs_ref):
        y = jnp.dot(yc_ref[...], w_ref[...], preferred_element_type=F32)
        res = ALPHA * x_ref[...] + (1.0 + gt_ref[...]) * y
        mu = jnp.mean(res, axis=-1, keepdims=True)
        cen = res - mu
        var = jnp.mean(cen * cen, axis=-1, keepdims=True)
        rstd = lax.rsqrt(var + LN_EPS)
        xhat = cen * rstd
        xn_ref[...] = xhat * g_ref[...] + b_ref[...]
        xh_ref[...] = xhat
        y_ref[...] = y
        rs_ref[...] = rstd

    big = SDS((s, d), F32)
    return pl.pallas_call(
        body, name="out_proj_ln", out_shape=(big, big, big, SDS((s, 1), F32)), grid=(s // tb,),
        in_specs=[_rows(tb, d), pl.BlockSpec((d, d), lambda i: (0, 0), pipeline_mode=pl.Buffered(1)), _rows(tb, d),
                  _const((1, d)), _const((1, d)), _const((1, d))],
        out_specs=(_rows(tb, d), _rows(tb, d), _rows(tb, d), _rows(tb, 1)), compiler_params=_params(1),
    )(ycat, w_out, x, gate, ln_g, ln_b)


def _ln_bwd(dxn, xhat, y, rstd, ln_g, gate, tb):
    s, d = dxn.shape

    def body(dxn_ref, xh_ref, y_ref, rs_ref, g_ref, gt_ref, dy_ref, dxa_ref, dg_ref, db_ref, dgt_ref):
        _init_acc(dg_ref, db_ref, dgt_ref)
        dxn_t, xh = dxn_ref[...], xh_ref[...]
        dxh = dxn_t * g_ref[...]
        dres = rs_ref[...] * (dxh - jnp.mean(dxh, axis=-1, keepdims=True)
                              - xh * jnp.mean(dxh * xh, axis=-1, keepdims=True))
        dy_ref[...] = ((1.0 + gt_ref[...]) * dres).astype(MXU_DTYPE)
        dxa_ref[...] = ALPHA * dres
        dg_ref[...] += _colsum(dxn_t * xh)
        db_ref[...] += _colsum(dxn_t)
        dgt_ref[...] += _colsum(dres * y_ref[...])

    vec = SDS((1, d), F32)
    return pl.pallas_call(
        body, name="ln_bwd", out_shape=(SDS((s, d), MXU_DTYPE), SDS((s, d), F32), vec, vec, vec), grid=(s // tb,),
        in_specs=[_rows(tb, d), _rows(tb, d), _rows(tb, d), _rows(tb, 1), _const((1, d)), _const((1, d))],
        out_specs=(_rows(tb, d), _rows(tb, d), _const((1, d)), _const((1, d)), _const((1, d))),
        compiler_params=_params(1))(dxn, xhat, y, rstd, ln_g, gate)


def _dh_mod_bwd(dproj, w_in, dxa, x, scale, carry=None):
    s, d = dxa.shape
    k = dproj.shape[1]
    tm, tn, tk = min(1024, s), 1024, 1536
    nk = k // tk
    assert s % tm == 0 and d % tn == 0 and k % tk == 0

    def body(a_ref, b_ref, dxa_ref, x_ref, sc_ref, dx_ref, dsh_ref, dsc_ref, acc):
        i, kk = pl.program_id(1), pl.program_id(2)

        @pl.when(kk == 0)
        def _():
            acc[...] = jnp.zeros_like(acc)

        @pl.when((kk == 0) & (i == 0))
        def _():
            dsh_ref[...] = jnp.zeros_like(dsh_ref)
            dsc_ref[...] = jnp.zeros_like(dsc_ref)

        acc[...] += lax.dot_general(a_ref[...], b_ref[...], (((1,), (1,)), ((), ())), preferred_element_type=F32)

        @pl.when(kk == nk - 1)
        def _():
            dh_t = acc[...]
            dx_ref[...] = dxa_ref[...] + dh_t * (1.0 + sc_ref[...])
            dsh_ref[...] += _colsum(dh_t)
            dsc_ref[...] += _colsum(dh_t * x_ref[...])

    tile = pl.BlockSpec((tm, tn), lambda j, i, kk: (i, j))
    vec = pl.BlockSpec((1, tn), lambda j, i, kk: (0, j))
    return _call(
        body, name="dh", out_shape=(SDS((s, d), F32), SDS((1, d), F32), SDS((1, d), F32)),
        grid=(d // tn, s // tm, nk),
        in_specs=[pl.BlockSpec((tm, tk), lambda j, i, kk: (i, kk)), pl.BlockSpec((tn, tk), lambda j, i, kk: (j, kk)),
                  tile, tile, vec],
        out_specs=(tile, vec, vec), scratch_shapes=[pltpu.VMEM((tm, tn), F32)],
        args=(dproj, w_in, dxa, x, scale), carry=carry)


def _loss_head(y, target, tb):
    s, d = y.shape

    def body(y_ref, t_ref, l_ref, dy_ref):
        _init_acc(l_ref)
        err = y_ref[...] - t_ref[...]
        l_ref[...] += (0.5 / d) * jnp.sum(err * err, keepdims=True)
        dy_ref[...] = err * (1.0 / d)

    return pl.pallas_call(body, name="loss_head", out_shape=(SDS((1, 1), F32), SDS((s, d), F32)), grid=(s // tb,),
                          in_specs=[_rows(tb, d), _rows(tb, d)], out_specs=(_const((1, 1)), _rows(tb, d)),
                          compiler_params=_params(1))(y, target)


def _conv_taps(u, up, w_ref, width):
    out = w_ref[width - 1:width, :] * u
    for j in range(width - 2, -1, -1):
        out = out + w_ref[j:j + 1, :] * _shift_down(u, up, width - 1 - j)
    return out


def _conv_taps_t(g, gn, w_ref, width):
    out = w_ref[width - 1:width, :] * g
    for j in range(width - 2, -1, -1):
        out = out + w_ref[j:j + 1, :] * _shift_up(g, gn, width - 1 - j)
    return out


def _conv_wgrad(dw_ref, g, u, up, width):
    dw_ref[width - 1:width, :] += _colsum(g * u)
    for j in range(width - 1):
        dw_ref[j:j + 1, :] += _colsum(g * _shift_down(u, up, width - 1 - j))


def _branch_a_fwd(proj, conv_w, tb):
    s = proj.shape[0]

    def body(ab, ac, ax, ag, acp, axp, w_ref, o_ref):
        has_prev = (pl.program_id(0) > 0).astype(F32)
        u = ac[...] * ax[...]
        up = acp[...] * axp[...] * has_prev
        o_ref[...] = (ab[...] * _conv_taps(u, up, w_ref, 3) * _silu(ag[...])).astype(MXU_DTYPE)

    return pl.pallas_call(
        body, name="branch_a_fwd", out_shape=SDS((s, BR), MXU_DTYPE), grid=(s // tb,),
        in_specs=[_rows(tb, BR, CB_AB), _rows(tb, BR, CB_AC), _rows(tb, BR, CB_AX), _rows(tb, BR, CB_AG),
                  _prev8(tb, BR, CB_AC), _prev8(tb, BR, CB_AX), _const((8, BR))],
        out_specs=_rows(tb, BR), compiler_params=_params(1))(proj, proj, proj, proj, proj, proj, conv_w)


def _branch_a_bwd(dycat, proj, conv_w, tb):
    s = proj.shape[0]

    def body(dy, dyn, ab, abn, ag, agn, ac, acp, ax, axp, w_ref, o_ref, dw_ref):
        _init_acc(dw_ref)
        i = pl.program_id(0)
        has_prev = (i > 0).astype(F32)
        has_next = (i < pl.num_programs(0) - 1).astype(F32)
        u = ac[...] * ax[...]
        up = acp[...] * axp[...] * has_prev
        v = _conv_taps(u, up, w_ref, 3)
        sg = _silu(ag[...])
        dv = dy[...] * ab[...] * sg
        dvn = dyn[...] * abn[...] * _silu(agn[...]) * has_next
        du = _conv_taps_t(dv, dvn, w_ref, 3)
        o_ref[:, 0:BR] = (dy[...] * v * sg).astype(MXU_DTYPE)
        o_ref[:, BR:2 * BR] = (du * ax[...]).astype(MXU_DTYPE)
        o_ref[:, 2 * BR:3 * BR] = (du * ac[...]).astype(MXU_DTYPE)
        o_ref[:, 3 * BR:4 * BR] = (dy[...] * ab[...] * v * _dsilu(ag[...])).astype(MXU_DTYPE)
        _conv_wgrad(dw_ref, dv, u, up, 3)

    return pl.pallas_call(
        body, name="branch_a_bwd", out_shape=(SDS((s, 4 * BR), MXU_DTYPE), SDS((8, BR), F32)), grid=(s // tb,),
        in_specs=[_rows(tb, BR, 0), _next8(tb, BR, s, 0),
                  _rows(tb, BR, CB_AB), _next8(tb, BR, s, CB_AB), _rows(tb, BR, CB_AG), _next8(tb, BR, s, CB_AG),
                  _rows(tb, BR, CB_AC), _prev8(tb, BR, CB_AC), _rows(tb, BR, CB_AX), _prev8(tb, BR, CB_AX),
                  _const((8, BR))],
        out_specs=(_rows(tb, 4 * BR), _const((8, BR))), compiler_params=_params(1),
    )(dycat, dycat, proj, proj, proj, proj, proj, proj, proj, proj, conv_w)


def _t5_bucket(dist):
    max_exact = REL_BUCKETS // 2
    nf = jnp.maximum(dist, 1).astype(F32)
    large = max_exact + (jnp.log(nf / max_exact) / math.log(REL_MAX_DIST / max_exact)
                         * (REL_BUCKETS - max_exact)).astype(jnp.int32)
    large = jnp.minimum(large, REL_BUCKETS - 1)
    return jnp.where(dist < max_exact, dist, large)


def _bucket_maps():
    maps = []
    i = jnp.arange(BLK)[:, None]
    j = jnp.arange(2 * BLK)[None, :]
    delta = i + BLK - j
    for window, dil in DILATIONS:
        span = window // dil
        bucket = _t5_bucket(jnp.clip(delta, 0, span) * dil)
        maps.append(jnp.where((delta >= 0) & (delta <= span), bucket, -1))
    return jnp.stack(maps).astype(jnp.int32)


def _bias_tables(rel_bias, buckets):
    n_pat = len(DILATIONS)

    def body(rb_ref, bk_ref, o_ref):
        for g in range(n_pat):
            bk = bk_ref[g]
            for h in range(ATT_HEADS):
                def per_bucket(b, acc):
                    return jnp.where(bk == b, rb_ref[b, h], acc)
                o_ref[g, h] = lax.fori_loop(0, REL_BUCKETS, per_bucket, jnp.full((BLK, 2 * BLK), NEG, F32))

    return pl.pallas_call(
        body, name="bias_tables", out_shape=SDS((n_pat, ATT_HEADS, BLK, 2 * BLK), F32),
        in_specs=[pl.BlockSpec(memory_space=pltpu.SMEM), pl.BlockSpec(memory_space=pltpu.VMEM)],
        compiler_params=_params())(rel_bias, buckets)


def _head_masks():
    lane = lax.broadcasted_iota(jnp.int32, (1, 2 * HEAD_DIM), 1)
    return [(lane < HEAD_DIM).astype(F32), (lane >= HEAD_DIM).astype(F32)]


def _strided(base, size, dil):
    return pl.ds(base, size, stride=dil) if dil > 1 else pl.ds(pl.multiple_of(base, BLK), size)


def _attn_groups(s, dil):
    return max(1, min(1024, s) // (dil * BLK)) if dil == 1 else max(1, min(2048, s) // (dil * BLK))


def _attn_fwd(proj, bias, dil):
    s = proj.shape[0]
    grp = _attn_groups(s, dil)
    u1 = dil * BLK
    unit = grp * u1
    nb = s // unit
    w = 2 * HEAD_DIM
    q0, k0, v0 = (cb * (BR // w) for cb in (CB_Q, CB_K, CB_V))

    def body(q_ref, kc_ref, kp_ref, vc_ref, vp_ref, bias_ref, o_ref, lse_ref, kbuf, vbuf):
        n = pl.program_id(1)
        col = lax.broadcasted_iota(jnp.int32, (1, 2 * BLK), 1)
        masks = _head_masks()
        kbuf[0:u1, :] = kp_ref[...]
        kbuf[u1:, :] = kc_ref[...]
        vbuf[0:u1, :] = vp_ref[...]
        vbuf[u1:, :] = vc_ref[...]

        def per_r(t, carry):
            j = t // dil
            base = j * u1 + t % dil
            rows = _strided(base, BLK, dil)
            no_prev = jnp.where((n == 0) & (j == 0) & (col < BLK), NEG, 0.0)
            q = q_ref[rows, :] * (HEAD_DIM ** -0.5)
            k = kbuf[_strided(base, 2 * BLK, dil), :].astype(MXU_DTYPE)
            v = vbuf[_strided(base, 2 * BLK, dil), :].astype(MXU_DTYPE)
            q2 = jnp.concatenate([q * masks[0], q * masks[1]], axis=0).astype(MXU_DTYPE)
            sc = lax.dot_general(q2, k, (((1,), (1,)), ((), ())), preferred_element_type=F32)
            sc = sc + jnp.concatenate([bias_ref[0], bias_ref[1]], axis=0) + no_prev
            mx = jnp.max(sc, axis=-1, keepdims=True)
            p = jnp.exp(sc - mx)
            l = jnp.sum(p, axis=-1, keepdims=True)
            o2 = jnp.dot((p / l).astype(MXU_DTYPE), v, preferred_element_type=F32)
            lse2 = mx + jnp.log(l)
            o_ref[rows, :] = o2[0:BLK] * masks[0] + o2[BLK:2 * BLK] * masks[1]
            lse_ref[rows, :] = lse2[0:BLK] * masks[0] + lse2[BLK:2 * BLK] * masks[1]
            return carry

        lax.fori_loop(0, grp * dil, per_r, 0, unroll=8)

    cur = lambda c0: pl.BlockSpec((unit, w), lambda hp, n: (n, c0 + hp))
    prev = lambda c0: pl.BlockSpec((u1, w), lambda hp, n: (jnp.maximum(n * grp - 1, 0), c0 + hp))
    out = pl.BlockSpec((unit, w), lambda hp, n: (n, hp))
    return pl.pallas_call(
        body, name=f"attn_fwd_d{dil}", out_shape=(SDS((s, BR), F32), SDS((s, BR), F32)), grid=(BR // w, nb),
        in_specs=[cur(q0), cur(k0), prev(k0), cur(v0), prev(v0),
                  pl.BlockSpec((2, BLK, 2 * BLK), lambda hp, n: (hp, 0, 0))],
        out_specs=(out, out),
        scratch_shapes=[pltpu.VMEM((unit + u1, w), F32), pltpu.VMEM((unit + u1, w), F32)],
        compiler_params=_params(2))(proj, proj, proj, proj, proj, bias)


def _softmax3(l0, l1, l2):
    mx = jnp.maximum(jnp.maximum(l0, l1), l2)
    e0, e1, e2 = jnp.exp(l0 - mx), jnp.exp(l1 - mx), jnp.exp(l2 - mx)
    inv = 1.0 / (e0 + e1 + e2)
    return e0 * inv, e1 * inv, e2 * inv


def _attn_combine(os_, lses, proj, tb):
    s = proj.shape[0]

    def body(o0, o1, o2, l0, l1, l2, bg, y_ref):
        w0, w1, w2 = _softmax3(l0[...], l1[...], l2[...])
        attn = w0 * o0[...] + w1 * o1[...] + w2 * o2[...]
        y_ref[...] = (attn * _silu(bg[...])).astype(MXU_DTYPE)

    return pl.pallas_call(
        body, name="attn_combine", out_shape=SDS((s, BR), MXU_DTYPE), grid=(s // tb,),
        in_specs=[_rows(tb, BR)] * 6 + [_rows(tb, BR, CB_BG)], out_specs=_rows(tb, BR),
        compiler_params=_params(1))(*os_, *lses, proj)


def _attn_bwd_pre(dycat, os_, lses, proj, head_ones, tb):
    s = proj.shape[0]

    def body(dy, o0, o1, o2, l0, l1, l2, bg, e_ref, dbg_ref, do0, do1, do2, dm0, dm1, dm2):
        w0, w1, w2 = _softmax3(l0[...], l1[...], l2[...])
        attn = w0 * o0[...] + w1 * o1[...] + w2 * o2[...]
        dattn = dy[...] * _silu(bg[...])
        dbg_ref[...] = (dy[...] * attn * _dsilu(bg[...])).astype(MXU_DTYPE)
        prod = dattn * attn
        hi = prod.astype(MXU_DTYPE)
        lo = (prod - hi.astype(F32)).astype(MXU_DTYPE)
        tot = (jnp.dot(hi, e_ref[...], preferred_element_type=F32)
               + jnp.dot(lo, e_ref[...], preferred_element_type=F32))
        for wg, do_ref, dm_ref in ((w0, do0, dm0), (w1, do1, dm1), (w2, do2, dm2)):
            do_ref[...] = wg * dattn
            dm_ref[...] = wg * tot

    big = SDS((s, BR), F32)
    return pl.pallas_call(
        body, name="attn_bwd_pre", out_shape=(SDS((s, BR), MXU_DTYPE),) + (big,) * 6, grid=(s // tb,),
        in_specs=[_rows(tb, BR, 1)] + [_rows(tb, BR)] * 6 + [_rows(tb, BR, CB_BG), _const((BR, BR))],
        out_specs=(_rows(tb, BR),) * 7, compiler_params=_params(1))(dycat, *os_, *lses, proj, head_ones)


def _attn_bwd(proj, do, lse, dm, bias, dil, carry=None):
    s = proj.shape[0]
    grp = _attn_groups(s, dil)
    u1 = dil * BLK
    unit = grp * u1
    nb = s // unit
    w = 2 * HEAD_DIM
    q0, k0, v0 = (cb * (BR // w) for cb in (CB_Q, CB_K, CB_V))

    def body(q_ref, kc_ref, kp_ref, vc_ref, vp_ref, do_ref, lse_ref, dm_ref, bias_ref,
             dq_ref, dk_ref, dv_ref, dbias_ref, kbuf, vbuf, stage_k, stage_v):
        n = pl.program_id(1)
        col = lax.broadcasted_iota(jnp.int32, (1, 2 * BLK), 1)
        masks = _head_masks()

        @pl.when(n == 0)
        def _():
            dbias_ref[...] = jnp.zeros_like(dbias_ref)
            stage_k[...] = jnp.zeros_like(stage_k)
            stage_v[...] = jnp.zeros_like(stage_v)

        for out_ref, stage in ((dk_ref, stage_k), (dv_ref, stage_v)):
            if grp > 1:
                out_ref[0:unit - u1, :] = stage[u1:unit, :]
            stage[0:u1, :] = stage[unit:unit + u1, :]

        @pl.when(n < nb)
        def _():
            kbuf[0:u1, :] = kp_ref[...]
            kbuf[u1:, :] = kc_ref[...]
            vbuf[0:u1, :] = vp_ref[...]
            vbuf[u1:, :] = vc_ref[...]

            def per_r(t, carry):
                j = t // dil
                base = j * u1 + t % dil
                rows = _strided(base, BLK, dil)
                rows_hi = _strided(base + u1, BLK, dil)
                no_prev = jnp.where((n == 0) & (j == 0) & (col < BLK), NEG, 0.0)
                q = q_ref[rows, :] * (HEAD_DIM ** -0.5)
                k = kbuf[_strided(base, 2 * BLK, dil), :].astype(MXU_DTYPE)
                v = vbuf[_strided(base, 2 * BLK, dil), :].astype(MXU_DTYPE)
                do_t, lse_t, dm_t = do_ref[rows, :], lse_ref[rows, :], dm_ref[rows, :]
                stack = lambda t: jnp.concatenate([t * masks[0], t * masks[1]], axis=0).astype(MXU_DTYPE)
                per_head = lambda t: jnp.concatenate([t[:, 0:1], t[:, HEAD_DIM:HEAD_DIM + 1]], axis=0)
                q2, do2 = stack(q), stack(do_t)
                sc = lax.dot_general(q2, k, (((1,), (1,)), ((), ())), preferred_element_type=F32)
                p = jnp.exp(sc + jnp.concatenate([bias_ref[0], bias_ref[1]], axis=0) + no_prev - per_head(lse_t))
                dp = lax.dot_general(do2, v, (((1,), (1,)), ((), ())), preferred_element_type=F32)
                ds = p * (dp - per_head(dm_t))
                dbias_ref[0] += ds[0:BLK]
                dbias_ref[1] += ds[BLK:2 * BLK]
                dsb, pb = ds.astype(MXU_DTYPE), p.astype(MXU_DTYPE)
                dq2 = jnp.dot(dsb, k, preferred_element_type=F32)
                dk_acc = lax.dot_general(dsb, q2, (((0,), (0,)), ((), ())), preferred_element_type=F32)
                dv_acc = lax.dot_general(pb, do2, (((0,), (0,)), ((), ())), preferred_element_type=F32)
                dq_ref[rows, :] = (dq2[0:BLK] * masks[0] + dq2[BLK:2 * BLK] * masks[1]) * (HEAD_DIM ** -0.5)
                stage_k[rows, :] = stage_k[rows, :] + dk_acc[0:BLK]
                stage_v[rows, :] = stage_v[rows, :] + dv_acc[0:BLK]
                stage_k[rows_hi, :] = dk_acc[BLK:2 * BLK]
                stage_v[rows_hi, :] = dv_acc[BLK:2 * BLK]
                return carry

            lax.fori_loop(0, grp * dil, per_r, 0, unroll=8)

        dk_ref[unit - u1:unit, :] = stage_k[0:u1, :]
        dv_ref[unit - u1:unit, :] = stage_v[0:u1, :]

    qn = lambda n: jnp.minimum(n, nb - 1)
    cur = lambda c0: pl.BlockSpec((unit, w), lambda hp, n: (qn(n), c0 + hp))
    prev = lambda c0: pl.BlockSpec((u1, w), lambda hp, n: (jnp.maximum(qn(n) * grp - 1, 0), c0 + hp))
    row = pl.BlockSpec((unit, w), lambda hp, n: (qn(n), hp))
    late = pl.BlockSpec((unit, w), lambda hp, n: (jnp.maximum(n - 1, 0), hp))
    tab = pl.BlockSpec((2, BLK, 2 * BLK), lambda hp, n: (hp, 0, 0))
    big = SDS((s, BR), F32)
    return _call(
        body, name=f"attn_bwd_d{dil}", out_shape=(big, big, big, SDS((ATT_HEADS, BLK, 2 * BLK), F32)),
        grid=(BR // w, nb + 1),
        in_specs=[cur(q0), cur(k0), prev(k0), cur(v0), prev(v0), row, row, row, tab],
        out_specs=(row, late, late, tab),
        scratch_shapes=[pltpu.VMEM((unit + u1, w), F32)] * 4,
        args=(proj, proj, proj, proj, proj, do, lse, dm, bias), carry=carry)


def _rel_bias_grad(dbias, buckets):
    def body(db_ref, bk_ref, o_ref):
        row = lax.broadcasted_iota(jnp.int32, (REL_BUCKETS, 128), 0)
        lane = lax.broadcasted_iota(jnp.int32, (REL_BUCKETS, 128), 1)

        def per_bucket(b, acc):
            for g in range(len(DILATIONS)):
                hit = bk_ref[g] == b
                for h in range(ATT_HEADS):
                    both = db_ref[0, g, h] + db_ref[1, g, h]
                    val = jnp.sum(jnp.where(hit, both, 0.0), keepdims=True)
                    acc = acc + jnp.where((row == b) & (lane == h), val, 0.0)
            return acc

        o_ref[...] = lax.fori_loop(0, REL_BUCKETS, per_bucket, jnp.zeros((REL_BUCKETS, 128), F32))

    assert dbias.shape[0] == DEPTH == 2
    return pl.pallas_call(body, name="rel_bias_grad", out_shape=SDS((REL_BUCKETS, 128), F32),
                          compiler_params=_params())(dbias, buckets)


def _scan_real(a, b, *, reverse, tb, name):
    s, ch = a.shape
    nt = s // tb
    order = range(7, -1, -1) if reverse else range(8)

    def body(a_ref, b_ref, o_ref, carry):
        @pl.when(pl.program_id(0) == 0)
        def _():
            carry[...] = jnp.zeros_like(carry)

        def group(gi, h):
            r0 = pl.multiple_of((tb // 8 - 1 - gi if reverse else gi) * 8, 8)
            a8, b8 = a_ref[pl.ds(r0, 8), :], b_ref[pl.ds(r0, 8), :]
            rows = [None] * 8
            for k in order:
                if reverse:
                    rows[k] = b8[k:k + 1] + h
                    h = a8[k:k + 1] * rows[k]
                else:
                    h = a8[k:k + 1] * h + b8[k:k + 1]
                    rows[k] = h
            o_ref[pl.ds(r0, 8), :] = jnp.concatenate(rows, axis=0)
            return h

        carry[...] = lax.fori_loop(0, tb // 8, group, carry[...])

    spec = pl.BlockSpec((tb, ch), (lambda i: (nt - 1 - i, 0)) if reverse else (lambda i: (i, 0)))
    return pl.pallas_call(body, name=name, out_shape=SDS((s, ch), F32), grid=(nt,), in_specs=[spec, spec],
                          out_specs=spec, scratch_shapes=[pltpu.VMEM((1, ch), F32)],
                          compiler_params=_params(1))(a, b)


def _scan_tile(s):
    return min(512, s)


def _load_chunked(ref, t0, pt):
    ln = pt // 8
    return jnp.concatenate([ref[pl.ds(t0 + j, 8, stride=ln), :] for j in range(ln)], axis=0)


def _store_natural(ref, t0, pt, val):
    ln = pt // 8
    for j in range(ln):
        ref[pl.ds(t0 + j, 8, stride=ln), :] = val[j * 8:(j + 1) * 8]


def _scan_tile_in_place(a_ref, x_ref, carry, pw, *, reverse):
    ch2 = x_ref.shape[1]
    ch = ch2 // 2
    ln = x_ref.shape[0] // 8
    ar = a_ref[:, 0:ch]
    ai = -a_ref[:, ch:ch2] if reverse else a_ref[:, ch:ch2]

    def cmul(pr, pi, xr, xi):
        return pr * xr - pi * xi, pr * xi + pi * xr

    @pl.when(pl.program_id(0) == 0)
    def _():
        carry[...] = jnp.zeros_like(carry)

        def fill(j, p):
            pw[pl.ds(j, 1), 0:ch] = p[0]
            pw[pl.ds(j, 1), ch:ch2] = p[1]
            return cmul(ar, ai, *p)

        lax.fori_loop(0, ln, fill, (ar, ai))

    def rows_of(j):
        return pl.ds(pl.multiple_of((ln - 1 - j if reverse else j) * 8, 8), 8)

    def local(j, x):
        rows = rows_of(j)
        nr, ni = cmul(ar, ai, *x)
        xr, xi = nr + x_ref[rows, 0:ch], ni + x_ref[rows, ch:ch2]
        x_ref[rows, 0:ch] = xr
        x_ref[rows, ch:ch2] = xi
        return xr, xi

    zero = jnp.zeros((8, ch), F32)
    er, ei = lax.fori_loop(0, ln, local, (zero, zero), unroll=2)
    apr, api = pw[ln - 1:ln, 0:ch], pw[ln - 1:ln, ch:ch2]
    cr, ci = carry[:, 0:ch], carry[:, ch:ch2]
    into_r, into_i = [None] * 8, [None] * 8
    for c in (range(7, -1, -1) if reverse else range(8)):
        into_r[c], into_i[c] = cr, ci
        pr, pi = cmul(apr, api, cr, ci)
        cr, ci = er[c:c + 1] + pr, ei[c:c + 1] + pi
    carry[:, 0:ch] = cr
    carry[:, ch:ch2] = ci
    into_r, into_i = jnp.concatenate(into_r, axis=0), jnp.concatenate(into_i, axis=0)

    def fix(j, carry_):
        rows = rows_of(j)
        dr, di = cmul(pw[pl.ds(j, 1), 0:ch], pw[pl.ds(j, 1), ch:ch2], into_r, into_i)
        x_ref[rows, 0:ch] += dr
        x_ref[rows, ch:ch2] += di
        return carry_

    lax.fori_loop(0, ln, fix, 0, unroll=2)


def _neg_expm1(z):
    series = -z * (1.0 + z * (0.5 + z * (1.0 / 6 + z * (1.0 / 24 + z * (1.0 / 120)))))
    return jnp.where(z > -0.05, series, 1.0 - jnp.exp(z))


def _lru_gate(xc, pre_r, pre_i, lam):
    log_a = -LRU_C * jax.nn.sigmoid(pre_r) * jax.nn.softplus(-lam)
    return jnp.exp(log_a), jnp.sqrt(_neg_expm1(2.0 * log_a)) * jax.nn.sigmoid(pre_i) * xc


def _lru_gates_fwd(proj, conv_w, conv_b, w_cat, b_cat, lam, tb):
    s = proj.shape[0]

    def body(cx, cxp, w_ref, cb_ref, wc_ref, bc_ref, lam_ref, a_ref, b_ref):
        has_prev = (pl.program_id(0) > 0).astype(F32)
        xc = _conv_taps(cx[...], cxp[...] * has_prev, w_ref, 4) + cb_ref[...]
        pre = jnp.dot(xc.astype(MXU_DTYPE), wc_ref[...], preferred_element_type=F32) + bc_ref[...]
        a_ref[...], b_ref[...] = _lru_gate(xc, pre[:, 0:BR], pre[:, BR:2 * BR], lam_ref[...])

    big = SDS((s, BR), F32)
    return pl.pallas_call(
        body, name="lru_gates_fwd", out_shape=(big, big), grid=(s // tb,),
        in_specs=[_rows(tb, BR, CB_CX), _prev8(tb, BR, CB_CX), _const((8, BR)), _const((1, BR)),
                  _const((BR, 2 * BR)), _const((1, 2 * BR)), _const((1, BR))],
        out_specs=(_rows(tb, BR), _rows(tb, BR)), compiler_params=_params(1),
    )(proj, proj, conv_w, conv_b, w_cat, b_cat, lam)


def _gate_out(h, proj, cb, tb, name):
    s = proj.shape[0]

    def body(h_ref, g_ref, o_ref):
        o_ref[...] = (h_ref[...] * _silu(g_ref[...])).astype(MXU_DTYPE)

    return pl.pallas_call(body, name=name, out_shape=SDS((s, BR), MXU_DTYPE), grid=(s // tb,),
                          in_specs=[_rows(tb, BR), _rows(tb, BR, cb)], out_specs=_rows(tb, BR),
                          compiler_params=_params(1))(h, proj)


def _gate_out_bwd(dycat, dy_cb, h, proj, cb, tb, name):
    s = proj.shape[0]

    def body(dy, h_ref, g_ref, dh_ref, dg_ref):
        dh_ref[...] = dy[...] * _silu(g_ref[...])
        dg_ref[...] = (dy[...] * h_ref[...] * _dsilu(g_ref[...])).astype(MXU_DTYPE)

    return pl.pallas_call(body, name=name, out_shape=(SDS((s, BR), F32), SDS((s, BR), MXU_DTYPE)), grid=(s // tb,),
                          in_specs=[_rows(tb, BR, dy_cb), _rows(tb, BR), _rows(tb, BR, cb)],
                          out_specs=(_rows(tb, BR), _rows(tb, BR)), compiler_params=_params(1))(dycat, h, proj)


def _lru_gates_bwd(proj, lmb, h, conv_w, conv_b, w_cat, b_cat, lam, tb):
    s = proj.shape[0]

    def body(cx, cxp, l_ref, h_ref, hp_ref, w_ref, cb_ref, wc_ref, bc_ref, lam_ref,
             dxc_ref, dpre_ref, xc_ref, dbc_ref, dlam_ref):
        _init_acc(dbc_ref, dlam_ref)
        has_prev = (pl.program_id(0) > 0).astype(F32)
        xc = _conv_taps(cx[...], cxp[...] * has_prev, w_ref, 4) + cb_ref[...]
        xcb = xc.astype(MXU_DTYPE)
        pre = jnp.dot(xcb, wc_ref[...], preferred_element_type=F32) + bc_ref[...]
        _, vjp = jax.vjp(_lru_gate, xc, pre[:, 0:BR], pre[:, BR:2 * BR], lam_ref[...])
        lm = l_ref[...]
        dxc, dpr, dpi, dlam = vjp((lm * _shift_down(h_ref[...], hp_ref[...] * has_prev, 1), lm))
        dpre = jnp.concatenate([dpr, dpi], axis=1)
        dpreb = dpre.astype(MXU_DTYPE)
        dxc_ref[...] = dxc + lax.dot_general(dpreb, wc_ref[...], (((1,), (1,)), ((), ())),
                                             preferred_element_type=F32)
        dpre_ref[...] = dpreb
        xc_ref[...] = xcb
        dbc_ref[...] += _colsum(dpre)
        dlam_ref[...] += dlam

    return pl.pallas_call(
        body, name="lru_gates_bwd",
        out_shape=(SDS((s, BR), F32), SDS((s, 2 * BR), MXU_DTYPE), SDS((s, BR), MXU_DTYPE),
                   SDS((1, 2 * BR), F32), SDS((1, BR), F32)),
        grid=(s // tb,),
        in_specs=[_rows(tb, BR, CB_CX), _prev8(tb, BR, CB_CX), _rows(tb, BR), _rows(tb, BR), _prev8(tb, BR),
                  _const((8, BR)), _const((1, BR)), _const((BR, 2 * BR)), _const((1, 2 * BR)), _const((1, BR))],
        out_specs=(_rows(tb, BR), _rows(tb, 2 * BR), _rows(tb, BR), _const((1, 2 * BR)), _const((1, BR))),
        compiler_params=_params(1))(proj, proj, lmb, h, h, conv_w, conv_b, w_cat, b_cat, lam)


def _conv_c_bwd(dxc, proj, conv_w, tb):
    s = proj.shape[0]

    def body(g, gn, cx, cxp, w_ref, dcx_ref, dw_ref, db_ref):
        _init_acc(dw_ref, db_ref)
        i = pl.program_id(0)
        has_prev = (i > 0).astype(F32)
        has_next = (i < pl.num_programs(0) - 1).astype(F32)
        gt = g[...]
        dcx_ref[...] = _conv_taps_t(gt, gn[...] * has_next, w_ref, 4).astype(MXU_DTYPE)
        _conv_wgrad(dw_ref, gt, cx[...], cxp[...] * has_prev, 4)
        db_ref[...] += _colsum(gt)

    return pl.pallas_call(
        body, name="conv_c_bwd", out_shape=(SDS((s, BR), MXU_DTYPE), SDS((8, BR), F32), SDS((1, BR), F32)),
        grid=(s // tb,),
        in_specs=[_rows(tb, BR), _next8(tb, BR, s), _rows(tb, BR, CB_CX), _prev8(tb, BR, CB_CX), _const((8, BR))],
        out_specs=(_rows(tb, BR), _const((8, BR)), _const((1, BR))), compiler_params=_params(1),
    )(dxc, dxc, proj, proj, conv_w)


def _s5_disc(lam_re, lam_im, log_dt):
    dt = jnp.exp(log_dt)
    mag = jnp.exp(lam_re * dt)
    ab_re = mag * jnp.cos(lam_im * dt)
    ab_im = mag * jnp.sin(lam_im * dt)
    den = lam_re * lam_re + lam_im * lam_im
    f_re = ((ab_re - 1.0) * lam_re + ab_im * lam_im) / den
    f_im = (ab_im * lam_re - (ab_re - 1.0) * lam_im) / den
    return ab_re, ab_im, f_re, f_im


def _s5_bbar(f_re, f_im, b_re, b_im):
    return f_re * b_re - f_im * b_im, f_re * b_im + f_im * b_re


def _s5_disc_fwd(lam_re, lam_im, log_dt):
    def body(lr, li, ld, o0, o1, o2, o3):
        o0[...], o1[...], o2[...], o3[...] = _s5_disc(lr[...], li[...], ld[...])
    return pl.pallas_call(body, name="s5_disc_fwd", out_shape=(SDS(lam_re.shape, F32),) * 4)(lam_re, lam_im, log_dt)


def _s5_disc_bwd(lam_re, lam_im, log_dt, cts):
    def body(lr, li, ld, c0, c1, c2, c3, o0, o1, o2):
        _, vjp = jax.vjp(_s5_disc, lr[...], li[...], ld[...])
        o0[...], o1[...], o2[...] = vjp((c0[...], c1[...], c2[...], c3[...]))
    return pl.pallas_call(body, name="s5_disc_bwd", out_shape=(SDS(lam_re.shape, F32), SDS(lam_re.shape, F32),
                                                                SDS(log_dt.shape, F32)))(lam_re, lam_im, log_dt, *cts)


def _s5_bbar_fwd(f_re, f_im, b_re, b_im):
    def body(fr, fi, br, bi, o0, o1):
        o0[...], o1[...] = _s5_bbar(fr[...], fi[...], br[...], bi[...])
    return pl.pallas_call(body, name="s5_bbar_fwd", out_shape=(SDS(b_re.shape, F32),) * 2)(f_re, f_im, b_re, b_im)


def _s5_bbar_bwd(f_re, f_im, b_re, b_im, d_re, d_im):
    def body(fr, fi, br, bi, dr, di, o0, o1, o2, o3):
        _, vjp = jax.vjp(_s5_bbar, fr[...], fi[...], br[...], bi[...])
        o0[...], o1[...], o2[...], o3[...] = vjp((dr[...], di[...]))
    col, mat = SDS(f_re.shape, F32), SDS(b_re.shape, F32)
    return pl.pallas_call(body, name="s5_bbar_bwd", out_shape=(col, col, mat, mat))(f_re, f_im, b_re, b_im, d_re, d_im)


def _s5_tail_fwd(ylin, proj, d_skip, w_glu, b_glu, tb):
    s = proj.shape[0]

    def body(yl, u, dg, dk, w_ref, b_ref, o_ref):
        g = jax.nn.gelu(yl[...] + dk[...] * u[...])
        t = jnp.dot(g.astype(MXU_DTYPE), w_ref[...], preferred_element_type=F32) + b_ref[...]
        o_ref[...] = (g * jax.nn.sigmoid(t) * _silu(dg[...])).astype(MXU_DTYPE)

    return pl.pallas_call(
        body, name="s5_tail_fwd", out_shape=SDS((s, BR), MXU_DTYPE), grid=(s // tb,),
        in_specs=[_rows(tb, BR), _rows(tb, BR, CB_DU), _rows(tb, BR, CB_DG), _const((1, BR)), _const((BR, BR)),
                  _const((1, BR))],
        out_specs=_rows(tb, BR), compiler_params=_params(1))(ylin, proj, proj, d_skip, w_glu, b_glu)


def _s5_tail_bwd(dycat, ylin, proj, d_skip, w_glu, b_glu, tb):
    s = proj.shape[0]

    def body(dy, yl, u, dg, dk, w_ref, b_ref, dyl_ref, dus_ref, ddg_ref, g_ref, dt_ref, ddk_ref, dbg_ref):
        _init_acc(ddk_ref, dbg_ref)
        g, gelu_vjp = jax.vjp(jax.nn.gelu, yl[...] + dk[...] * u[...])
        gb = g.astype(MXU_DTYPE)
        sg = jax.nn.sigmoid(jnp.dot(gb, w_ref[...], preferred_element_type=F32) + b_ref[...])
        dz = dy[...] * _silu(dg[...])
        ddg_ref[...] = (dy[...] * g * sg * _dsilu(dg[...])).astype(MXU_DTYPE)
        dt = dz * g * sg * (1.0 - sg)
        dtb = dt.astype(MXU_DTYPE)
        dgel = dz * sg + lax.dot_general(dtb, w_ref[...], (((1,), (1,)), ((), ())), preferred_element_type=F32)
        dyv, = gelu_vjp(dgel)
        dyl_ref[...] = dyv
        dus_ref[...] = dyv * dk[...]
        g_ref[...] = gb
        dt_ref[...] = dtb
        ddk_ref[...] += _colsum(dyv * u[...])
        dbg_ref[...] += _colsum(dt)

    big, half, vec = SDS((s, BR), F32), SDS((s, BR), MXU_DTYPE), SDS((1, BR), F32)
    return pl.pallas_call(
        body, name="s5_tail_bwd", out_shape=(big, big, half, half, half, vec, vec), grid=(s // tb,),
        in_specs=[_rows(tb, BR, 3), _rows(tb, BR), _rows(tb, BR, CB_DU), _rows(tb, BR, CB_DG), _const((1, BR)),
                  _const((BR, BR)), _const((1, BR))],
        out_specs=(_rows(tb, BR),) * 5 + (_const((1, BR)), _const((1, BR))), compiler_params=_params(1),
    )(dycat, ylin, proj, proj, d_skip, w_glu, b_glu)


def _assemble_dproj(da, dqkv, dbg, dcx, dcg, du, dus, ddg, tb):
    s = da.shape[0]

    def body(da_ref, q0, q1, q2, k0, k1, k2, v0, v1, v2, dbg_ref, dcx_ref, dcg_ref, du_ref, dus_ref, ddg_ref, o_ref):
        o_ref[:, 0:4 * BR] = da_ref[...]
        for j, parts in enumerate(((q0, q1, q2), (k0, k1, k2), (v0, v1, v2))):
            o_ref[:, (4 + j) * BR:(5 + j) * BR] = (parts[0][...] + parts[1][...] + parts[2][...]).astype(MXU_DTYPE)
        o_ref[:, 7 * BR:8 * BR] = dbg_ref[...].astype(MXU_DTYPE)
        o_ref[:, 8 * BR:9 * BR] = dcx_ref[...].astype(MXU_DTYPE)
        o_ref[:, 9 * BR:10 * BR] = dcg_ref[...].astype(MXU_DTYPE)
        o_ref[:, 10 * BR:11 * BR] = (du_ref[...] + dus_ref[...]).astype(MXU_DTYPE)
        o_ref[:, 11 * BR:12 * BR] = ddg_ref[...].astype(MXU_DTYPE)

    flat = [t for grp in dqkv for t in grp]
    return pl.pallas_call(
        body, name="assemble_dproj", out_shape=SDS((s, N_IN), MXU_DTYPE), grid=(s // tb,),
        in_specs=[_rows(tb, 4 * BR)] + [_rows(tb, BR)] * 15, out_specs=_rows(tb, N_IN),
        compiler_params=_params(1))(da, *flat, dbg, dcx, dcg, du, dus, ddg)


def _sum_leading(xs, tr, name):
    n, _, c = xs[0].shape
    nl = len(xs)
    tr = min([tr] + [x.shape[1] for x in xs])
    assert all(x.shape[1] % tr == 0 for x in xs), (name, tr)
    nrs = [x.shape[1] // tr for x in xs]
    starts = [sum(nrs[:l]) for l in range(nl)]

    def body(*refs):
        i = pl.program_id(0)
        for l in range(nl):
            @pl.when((i >= starts[l]) & (i < starts[l] + nrs[l]))
            def _():
                acc = refs[l * n][...].astype(F32)
                for ref in refs[l * n + 1:(l + 1) * n]:
                    acc = acc + ref[...].astype(F32)
                refs[nl * n][...] = acc

    specs = [pl.BlockSpec((None, tr, c), functools.partial(
        lambda i, k, l: (k, jnp.clip(i - starts[l], 0, nrs[l] - 1), 0), k=k, l=l)) for l in range(nl) for k in range(n)]
    return pl.pallas_call(body, name=name, out_shape=SDS((sum(nrs) * tr, c), F32), grid=(sum(nrs),), in_specs=specs,
                          out_specs=pl.BlockSpec((tr, c), lambda i: (i, 0)),
                          compiler_params=_params(1))(*[x for x in xs for _ in range(n)])


def _adamw(w, g_parts, m, v, tr, name):
    r, c = w.shape
    tr = min(tr, r)
    n = len(g_parts)
    assert r % tr == 0, (name, r, tr)

    def body(*refs):
        w_ref, m_ref, v_ref = refs[0], refs[1 + n], refs[2 + n]
        g_ref, d_ref, nm_ref, nv_ref = refs[3 + n:]
        g = refs[1][...]
        for ref in refs[2:1 + n]:
            g = g + ref[...]
        mm = ADAM_B1 * m_ref[...] + (1.0 - ADAM_B1) * g
        vv = ADAM_B2 * v_ref[...] + (1.0 - ADAM_B2) * jnp.square(g)
        m_hat = mm / (1.0 - ADAM_B1 ** ADAM_STEP)
        v_hat = vv / (1.0 - ADAM_B2 ** ADAM_STEP)
        g_ref[...] = g
        d_ref[...] = -ADAM_LR * (m_hat / (jnp.sqrt(v_hat) + ADAM_EPS) + ADAM_WD * w_ref[...])
        nm_ref[...] = mm
        nv_ref[...] = vv

    spec = pl.BlockSpec((tr, c), lambda i: (i, 0))
    return pl.pallas_call(body, name=name, out_shape=(SDS((r, c), F32),) * 4, grid=(r // tr,),
                          in_specs=[spec] * (3 + n), out_specs=(spec,) * 4,
                          compiler_params=_params(1))(w, *g_parts, m, v)


def _allgather8(block, name):
    m_per, n = block.shape

    def body(x_ref, out_ref, send_sems, recv_sems, local_sem):
        x, y, c = lax.axis_index("x"), lax.axis_index("y"), lax.axis_index("c")
        me, sibling = (x, y, c), (x, y, 1 - c)
        chips = [(1 - x, y), (x, 1 - y), (1 - x, 1 - y)]

        def rows(px, py, pc):
            return out_ref.at[pl.ds((4 * px + 2 * py + pc) * m_per, m_per), :]

        def copy(k, blk, to, src=None):
            return pltpu.make_async_remote_copy(
                src_ref=rows(*blk) if src is None else src, dst_ref=rows(*blk), send_sem=send_sems.at[k],
                recv_sem=recv_sems.at[k], device_id=to, device_id_type=MESH)

        mine = pltpu.make_async_copy(x_ref, rows(*me), local_sem)
        mine.start()
        first = [copy(0, me, sibling, src=x_ref)]
        first += [copy(1 + j, me, (*chip, c), src=x_ref) for j, chip in enumerate(chips)]
        for cp in first:
            cp.start()
        passed = [copy(4 + j, (*chip, c), sibling) for j, chip in enumerate(chips)]
        for j, chip in enumerate(chips):
            copy(1 + j, (*chip, c), me).wait_recv()
            passed[j].start()
        copy(0, sibling, me).wait_recv()
        for j, chip in enumerate(chips):
            copy(4 + j, (*chip, 1 - c), me).wait_recv()
        for cp in first + passed:
            cp.wait_send()
        mine.wait()

    return pl.pallas_call(
        body, name=name, out_shape=SDS((N_DEV * m_per, n), block.dtype),
        in_specs=[pl.BlockSpec(memory_space=pltpu.VMEM)], out_specs=pl.BlockSpec(memory_space=pltpu.VMEM),
        scratch_shapes=[pltpu.SemaphoreType.DMA((7,)), pltpu.SemaphoreType.DMA((7,)), pltpu.SemaphoreType.DMA],
        compiler_params=_params())(block)


class _Exchange:
    def __init__(self, items, out_shapes):
        self.items, self.out_shapes = list(items), tuple(out_shapes)
        self.arrays = [it[0] for it in self.items]
        n = len(self.items)
        self.n_in, self.n_out = n, len(self.out_shapes)
        self.scratch = [pltpu.SemaphoreType.DMA((n * N_CHIPS,)), pltpu.SemaphoreType.DMA((n * N_CHIPS,)),
                        pltpu.SemaphoreType.DMA((n,))]

    def _copies(self, ins, outs, sems, m):
        send_sems, recv_sems, local_sems = sems
        c = lax.axis_index("c")
        others = [j for j in range(N_CHIPS) if j != m]

        def remote(a, src, dst, to, from_):
            return pltpu.make_async_remote_copy(
                src_ref=src, dst_ref=dst, send_sem=send_sems.at[a * N_CHIPS + to],
                recv_sem=recv_sems.at[a * N_CHIPS + from_], device_id=(to // 2, to % 2, c), device_id_type=MESH)

        local, sends, recvs = [], [], []
        for a, (_, oi, src_of, dst_of) in enumerate(self.items):
            local.append(pltpu.make_async_copy(src_of(ins[a], m), dst_of(outs[oi], m), local_sems.at[a]))
            for j in others:
                sends.append(remote(a, src_of(ins[a], j), dst_of(outs[oi], m), j, m))
                recvs.append(remote(a, src_of(ins[a], m), dst_of(outs[oi], j), j, j))
        return local, sends, recvs

    def _on_my_chip(self, fn):
        chip = 2 * lax.axis_index("x") + lax.axis_index("y")
        for m in range(N_CHIPS):
            pl.when(chip == m)(functools.partial(fn, m))

    def start(self, ins, outs, sems):
        def go(m):
            local, sends, _ = self._copies(ins, outs, sems, m)
            for cp in local + sends:
                cp.start()
        self._on_my_chip(go)

    def wait(self, ins, outs, sems):
        def go(m):
            local, sends, recvs = self._copies(ins, outs, sems, m)
            for cp in recvs:
                cp.wait_recv()
            for cp in sends:
                cp.wait_send()
            for cp in local:
                cp.wait()
        self._on_my_chip(go)


def _half_rows(ref, cc):
    h = ref.shape[-2] // 2
    return ref.at[(slice(None),) * (len(ref.shape) - 2) + (pl.ds(cc * h, h), slice(None))]


class _Gather:
    def __init__(self, items, out_shapes):
        self.items, self.out_shapes = list(items), tuple(out_shapes)
        self.arrays = [it[0] for it in self.items]
        n = len(self.items)
        self.n_in, self.n_out = n, len(self.out_shapes)
        self.scratch = [pltpu.SemaphoreType.DMA((n * N_CHIPS,)) for _ in range(4)] + [pltpu.SemaphoreType.DMA((n,))]

    def _copies(self, ins, outs, sems, m, cc):
        ici_send, ici_recv, d2d_send, d2d_recv, local_sems = sems
        others = [j for j in range(N_CHIPS) if j != m]
        local, sends, arrivals, passed_on, from_sibling = [], [], [], [], []
        for a, (_, oi, src_of, dst_of) in enumerate(self.items):
            src, out = src_of(ins[a]), outs[oi]
            local.append(pltpu.make_async_copy(src, dst_of(out, m), local_sems.at[a]))
            for j in others:
                k = a * N_CHIPS + j
                mine_there = _half_rows(dst_of(out, m), cc)
                theirs_here = _half_rows(dst_of(out, j), cc)
                sends.append(pltpu.make_async_remote_copy(
                    src_ref=_half_rows(src, cc), dst_ref=mine_there, send_sem=ici_send.at[k],
                    recv_sem=ici_recv.at[a * N_CHIPS + m], device_id=(j // 2, j % 2, cc), device_id_type=MESH))
                arrivals.append(pltpu.make_async_remote_copy(
                    src_ref=_half_rows(src, cc), dst_ref=theirs_here, send_sem=ici_send.at[k], recv_sem=ici_recv.at[k],
                    device_id=(j // 2, j % 2, cc), device_id_type=MESH))
                passed_on.append(pltpu.make_async_remote_copy(
                    src_ref=theirs_here, dst_ref=theirs_here, send_sem=d2d_send.at[k], recv_sem=d2d_recv.at[k],
                    device_id=(m // 2, m % 2, 1 - cc), device_id_type=MESH))
                other_half = _half_rows(dst_of(out, j), 1 - cc)
                from_sibling.append(pltpu.make_async_remote_copy(
                    src_ref=other_half, dst_ref=other_half, send_sem=d2d_send.at[k], recv_sem=d2d_recv.at[k],
                    device_id=(m // 2, m % 2, 1 - cc), device_id_type=MESH))
        return local, sends, arrivals, passed_on, from_sibling

    def _on_my_core(self, fn):
        chip = 2 * lax.axis_index("x") + lax.axis_index("y")
        c = lax.axis_index("c")
        for m in range(N_CHIPS):
            for cc in range(2):
                pl.when((chip == m) & (c == cc))(functools.partial(fn, m, cc))

    def start(self, ins, outs, sems):
        def go(m, cc):
            local, sends, _, _, _ = self._copies(ins, outs, sems, m, cc)
            for cp in local + sends:
                cp.start()
        self._on_my_core(go)

    def wait(self, ins, outs, sems):
        def go(m, cc):
            local, sends, arrivals, passed_on, from_sibling = self._copies(ins, outs, sems, m, cc)
            for arrived, onward in zip(arrivals, passed_on):
                arrived.wait_recv()
                onward.start()
            for cp in from_sibling:
                cp.wait_recv()
            for cp in sends + passed_on:
                cp.wait_send()
            for cp in local:
                cp.wait()
        self._on_my_core(go)


def _run_exchange(ex, name):
    def body(*refs):
        ins, outs, sems = refs[:ex.n_in], refs[ex.n_in:ex.n_in + ex.n_out], refs[ex.n_in + ex.n_out:]
        ex.start(ins, outs, sems)
        ex.wait(ins, outs, sems)

    return pl.pallas_call(
        body, name=name, out_shape=ex.out_shapes, in_specs=[ANY] * ex.n_in, out_specs=(ANY,) * ex.n_out,
        scratch_shapes=ex.scratch, compiler_params=_params())(*ex.arrays)


def _sibling_swap(arrays, name):
    n = len(arrays)

    def body(*refs):
        ins, outs = refs[:n], refs[n:2 * n]
        send_sems, recv_sems = refs[2 * n:]
        peer = (lax.axis_index("x"), lax.axis_index("y"), 1 - lax.axis_index("c"))
        cps = [pltpu.make_async_remote_copy(src_ref=ins[a], dst_ref=outs[a], send_sem=send_sems.at[a],
                                            recv_sem=recv_sems.at[a], device_id=peer, device_id_type=MESH)
               for a in range(n)]
        for cp in cps:
            cp.start()
        for cp in cps:
            cp.wait()

    return pl.pallas_call(
        body, name=name, out_shape=tuple(SDS(a.shape, a.dtype) for a in arrays), in_specs=[ANY] * n,
        out_specs=(ANY,) * n, scratch_shapes=[pltpu.SemaphoreType.DMA((n,)), pltpu.SemaphoreType.DMA((n,))],
        compiler_params=_params())(*arrays)


def _block_diag(w):
    h, n, m = w.shape
    eye = jnp.eye(h, dtype=w.dtype)
    return (w[:, :, None, :] * eye[:, None, :, None]).reshape(h * n, h * m)


def _diag_blocks(d, h, col0=0, ncols=None, stacked=1):
    ncols = d.shape[1] - col0 if ncols is None else ncols
    n, m = d.shape[0] // (h * stacked), ncols // h
    lanes = 128
    assert m <= lanes and lanes % m == 0 and col0 % lanes == 0

    def body(d_ref, o_ref):
        for gi in range(h * stacked):
            c = col0 + (gi % h) * m
            chunk = d_ref[gi * n:(gi + 1) * n, c // lanes * lanes:c // lanes * lanes + lanes]
            o_ref[gi * n:(gi + 1) * n, :] = chunk[:, c % lanes:c % lanes + m]

    out = pl.pallas_call(body, name="diag_blocks", out_shape=SDS((stacked * h * n, m), d.dtype),
                         compiler_params=_params())(d)
    return out.reshape(stacked * h, n, m)


S5_CHUNKS = 4
S5_PER = S5_GROUPS // S5_CHUNKS
CH_W = S5_PER * S5_CH
ST_W = S5_PER * S5_STATE


def _bd_stack(mats):
    _, _, n, m = mats.shape
    eye = jnp.eye(S5_PER, dtype=mats.dtype)
    t = mats.reshape(2, S5_CHUNKS, S5_PER, n, m)
    bd = t[:, :, :, :, None, :] * eye[None, None, :, None, :, None]
    return bd.reshape(2 * S5_CHUNKS, S5_PER * n, S5_PER * m).astype(MXU_DTYPE)


def _chunks_chunked(src_ref, buf):
    pt = src_ref.shape[0]
    out = []
    for q in range(S5_CHUNKS):
        buf[q] = src_ref[:, q * CH_W:(q + 1) * CH_W]
        out.append(_load_chunked(buf.at[q], 0, pt).astype(MXU_DTYPE))
    return out


def _expand_into(dst_ref, chunks, w_ref):
    for b in range(2 * S5_CHUNKS):
        dst_ref[:, b * ST_W:(b + 1) * ST_W] = jnp.dot(chunks[b % S5_CHUNKS], w_ref[b], preferred_element_type=F32)


def _reduce_from(src_ref, w_ref, buf, dst_ref):
    pt = src_ref.shape[0]
    for q in range(S5_CHUNKS):
        y = jnp.dot(src_ref[:, q * ST_W:(q + 1) * ST_W].astype(MXU_DTYPE), w_ref[q], preferred_element_type=F32)
        p = S5_CHUNKS + q
        y = y + jnp.dot(src_ref[:, p * ST_W:(p + 1) * ST_W].astype(MXU_DTYPE), w_ref[p], preferred_element_type=F32)
        _store_natural(buf.at[q], 0, pt, y)
        dst_ref[:, q * CH_W:(q + 1) * CH_W] = buf[q]


def _s5_core_fwd(proj, w_bu, w_cx, a_row):
    s = proj.shape[0]
    pt = _scan_tile(s)
    ch2 = 2 * S5_N

    def body(u_ref, wb_ref, wc_ref, a_ref, x_ref, y_ref, carry, pw, buf):
        _expand_into(x_ref, _chunks_chunked(u_ref, buf), wb_ref)
        _scan_tile_in_place(a_ref, x_ref, carry, pw, reverse=False)
        _reduce_from(x_ref, wc_ref, buf, y_ref)

    return pl.pallas_call(
        body, name="s5_core_fwd", out_shape=(SDS((s, ch2), F32), SDS((s, BR), F32)), grid=(s // pt,),
        in_specs=[_rows(pt, BR, CB_DU), _const(w_bu.shape), _const(w_cx.shape), _const((1, ch2))],
        out_specs=(_rows(pt, ch2), _rows(pt, BR)),
        scratch_shapes=[pltpu.VMEM((1, ch2), F32), pltpu.VMEM((pt // 8, ch2), F32),
                        pltpu.VMEM((S5_CHUNKS, pt, CH_W), F32)],
        compiler_params=_params(1))(proj, w_bu, w_cx, a_row)


def _s5_core_bwd(dyl, proj, x, w_dx, w_du, a_row):
    s = proj.shape[0]
    pt = _scan_tile(s)
    nt = s // pt
    ch2 = 2 * S5_N
    ch = S5_N

    def body(dy_ref, u_ref, x_ref, xp_ref, wx_ref, wu_ref, a_ref, du_ref, da_ref, dwb_ref, dwc_ref,
             l_ref, carry, pw, buf, buf2):
        i = pl.program_id(0)
        _init_acc(da_ref, dwb_ref, dwc_ref)
        dy_c = _chunks_chunked(dy_ref, buf)
        u_c = _chunks_chunked(u_ref, buf2)
        _expand_into(l_ref, dy_c, wx_ref)
        _scan_tile_in_place(a_ref, l_ref, carry, pw, reverse=True)
        has_prev = (i < nt - 1).astype(F32)
        row = lax.broadcasted_iota(jnp.int32, (8, ch2), 0)
        first = jnp.where(row == 0, pltpu.roll(xp_ref[...], 1, 0) * has_prev, pltpu.roll(x_ref[pt - 8:pt, :], 1, 0))
        xprev = jnp.concatenate([first, x_ref[0:pt - 8, :]], axis=0)
        lr, li, xr, xi = l_ref[:, 0:ch], l_ref[:, ch:ch2], xprev[:, 0:ch], xprev[:, ch:ch2]
        da_ref[:, 0:ch] += _colsum(lr * xr + li * xi)
        da_ref[:, ch:ch2] += _colsum(li * xr - lr * xi)
        _reduce_from(l_ref, wu_ref, buf, du_ref)
        tn = (((0,), (0,)), ((), ()))
        for b in range(2 * S5_CHUNKS):
            cols, rows = slice(b * ST_W, (b + 1) * ST_W), slice(b * CH_W, (b + 1) * CH_W)
            dwb_ref[rows, :] += lax.dot_general(u_c[b % S5_CHUNKS], l_ref[:, cols].astype(MXU_DTYPE), tn,
                                                preferred_element_type=F32)
            dwc_ref[rows, :] += lax.dot_general(dy_c[b % S5_CHUNKS], x_ref[:, cols].astype(MXU_DTYPE), tn,
                                                preferred_element_type=F32)

    rev = lambda w, cb=0: pl.BlockSpec((pt, w), lambda i: (nt - 1 - i, cb))
    halo = pl.BlockSpec((8, ch2), lambda i: (jnp.maximum((nt - 1 - i) * (pt // 8) - 1, 0), 0))
    wshape = SDS((2 * S5_CHUNKS * CH_W, ST_W), F32)
    return pl.pallas_call(
        body, name="s5_core_bwd", out_shape=(SDS((s, BR), F32), SDS((1, ch2), F32), wshape, wshape), grid=(nt,),
        in_specs=[rev(BR, 0), rev(BR, CB_DU), rev(ch2), halo, _const(w_dx.shape), _const(w_du.shape),
                  _const((1, ch2))],
        out_specs=(rev(BR), _const((1, ch2)), _const(wshape.shape), _const(wshape.shape)),
        scratch_shapes=[pltpu.VMEM((pt, ch2), F32), pltpu.VMEM((1, ch2), F32), pltpu.VMEM((pt // 8, ch2), F32),
                        pltpu.VMEM((S5_CHUNKS, pt, CH_W), F32), pltpu.VMEM((S5_CHUNKS, pt, CH_W), F32)],
        compiler_params=_params(1))(dyl, proj, x, x, w_dx, w_du, a_row)


def _tiles(s):
    return dict(tb=min(512, s), tln=min(256, s))


def _layer_weights(p, l):
    pad8 = lambda w: jnp.pad(w, ((0, 8 - w.shape[0]), (0, 0)))
    return dict(
        conv_a=pad8(p["conv_a"][l]), conv_c=pad8(p["conv_c"][l]), conv_c_b=p["conv_c_b"][l][None],
        w_cat=jnp.concatenate([_block_diag(p["lru_wa"][l]), _block_diag(p["lru_wx"][l])], axis=1).astype(MXU_DTYPE),
        b_cat=jnp.concatenate([p["lru_ba"][l], p["lru_bx"][l]])[None], lam=p["lru_lambda"][l][None],
        lam_re=p["s5_lam_re"][l], lam_im=p["s5_lam_im"][l], log_dt=p["s5_log_dt"][l][:, None],
        b_re=p["s5_b_re"][l].reshape(S5_N, S5_CH), b_im=p["s5_b_im"][l].reshape(S5_N, S5_CH),
        c_re=p["s5_c_re"][l], c_im=p["s5_c_im"][l], d_skip=p["s5_d"][l][None], b_glu=p["s5_b_glu"][l][None],
        ln_g=p["ln_g"][l][None], ln_b=p["ln_b"][l][None])


def _s5_matrices(lw):
    ab_re, ab_im, f_re, f_im = _s5_disc_fwd(lw["lam_re"], lw["lam_im"], lw["log_dt"])
    f_re, f_im = f_re.reshape(S5_N, 1), f_im.reshape(S5_N, 1)
    bb_re, bb_im = _s5_bbar_fwd(f_re, f_im, lw["b_re"], lw["b_im"])
    bb = jnp.stack([bb_re, bb_im]).reshape(2, S5_GROUPS, S5_STATE, S5_CH)
    cc = jnp.stack([lw["c_re"], -lw["c_im"]])
    a_row = jnp.concatenate([ab_re.reshape(1, S5_N), ab_im.reshape(1, S5_N)], axis=1)
    return dict(f_re=f_re, f_im=f_im, a_row=a_row, w_bu=_bd_stack(jnp.swapaxes(bb, 2, 3)), w_du=_bd_stack(bb),
                w_cx=_bd_stack(jnp.swapaxes(cc, 2, 3)), w_dx=_bd_stack(cc))


def _mm_hooked(hook, *args, **kw):
    if hook is None:
        return _mm(*args, **kw)
    out = _mm(*args, carry=hook[0], **kw)
    hook[1](out[1:])
    return out[0]


def _layer_fwd(x, ada, w_in, get_rest, lw, s5m, bias_tabs, hooks=None):
    s = x.shape[0]
    t = _tiles(s)
    tb = t["tb"]
    shift, scale, gate = ada
    hooks = hooks or {}
    h = _modulate(x, scale, shift, tb)
    proj = _mm_hooked(hooks.get("in_proj"), h, w_in, name="in_proj", tm=1024, tn=1024, tk=D_MODEL)
    w_out, w_glu = get_rest()
    y_a = _branch_a_fwd(proj, lw["conv_a"], tb)
    os_, lses = [], []
    for g, (_, dil) in enumerate(DILATIONS):
        o, lse = _attn_fwd(proj, bias_tabs[g], dil)
        os_.append(o)
        lses.append(lse)
    y_b = _attn_combine(os_, lses, proj, tb)
    lru_a, lru_b = _lru_gates_fwd(proj, lw["conv_c"], lw["conv_c_b"], lw["w_cat"], lw["b_cat"], lw["lam"], tb)
    lru_h = _scan_real(lru_a, lru_b, reverse=False, tb=tb, name="lru_scan")
    y_c = _gate_out(lru_h, proj, CB_CG, tb, "lru_out")
    s5_x, ylin = _s5_core_fwd(proj, s5m["w_bu"], s5m["w_cx"], s5m["a_row"])
    y_d = _s5_tail_fwd(ylin, proj, lw["d_skip"], w_glu, lw["b_glu"], tb)
    ycat = jnp.concatenate([y_a, y_b, y_c, y_d], axis=1)
    x_next, xhat, y, rstd = _out_ln(ycat, w_out, x, gate, lw["ln_g"], lw["ln_b"], t["tln"])
    saved = dict(x=x, h=h, proj=proj, os=os_, lses=lses, lru_a=lru_a, lru_h=lru_h, s5_x=s5_x, ylin=ylin, ycat=ycat,
                 xhat=xhat, y=y, rstd=rstd)
    return x_next, saved


def _layer_bwd(dxn, sv, ada, w_in, w_out, w_glu, lw, s5m, bias_tabs, head_ones, hooks=None):
    s = dxn.shape[0]
    t = _tiles(s)
    tb = t["tb"]
    shift, scale, gate = ada
    proj = sv["proj"]
    g = {}
    hook = lambda name: hooks[name](g) if hooks and name in hooks else None
    dyb, dxa, g["ln_g"], g["ln_b"], dgate = _ln_bwd(dxn, sv["xhat"], sv["y"], sv["rstd"], lw["ln_g"], gate, t["tln"])
    g["w_out"] = _mm_hooked(hook("dw_out"), sv["ycat"], dyb, name="dw_out", ta=True, out_dtype=WIRE_DTYPE,
                            tm=1024, tn=1024, tk=1024)
    dycat = _mm(dyb, w_out, name="dycat", tb=True, tm=1024, tn=1024, tk=D_MODEL)
    da, dconv_a = _branch_a_bwd(dycat, proj, lw["conv_a"], tb)
    g["conv_a"] = dconv_a[0:3]
    pre = _attn_bwd_pre(dycat, sv["os"], sv["lses"], proj, head_ones, tb)
    dbg, dos, dms = pre[0], pre[1:4], pre[4:7]
    dqkv, dbias = [], []
    for gi, (_, dil) in enumerate(DILATIONS):
        hk = hook(f"attn_bwd_d{dil}")
        dq, dk, dv, dbi, *got = _attn_bwd(proj, dos[gi], sv["lses"][gi], dms[gi], bias_tabs[gi], dil,
                                          carry=hk and hk[0])
        if hk:
            hk[1](got)
        dqkv.append((dq, dk, dv))
        dbias.append(dbi)
    dqkv = list(zip(*dqkv))
    dh, dcg = _gate_out_bwd(dycat, 2, sv["lru_h"], proj, CB_CG, tb, "lru_out_bwd")
    lmb = _scan_real(sv["lru_a"], dh, reverse=True, tb=tb, name="lru_scan_bwd")
    dxc, dpre, xcb, dbcat, dlam = _lru_gates_bwd(proj, lmb, sv["lru_h"], lw["conv_c"], lw["conv_c_b"], lw["w_cat"],
                                                  lw["b_cat"], lw["lam"], tb)
    dwcat = _mm(xcb, dpre, name="dw_lru", ta=True, tn=1024)
    g["lru_wa"] = _diag_blocks(dwcat, LRU_HEADS, 0, BR)
    g["lru_wx"] = _diag_blocks(dwcat, LRU_HEADS, BR, BR)
    g["lru_ba"], g["lru_bx"], g["lru_lambda"] = dbcat[0, 0:BR], dbcat[0, BR:2 * BR], dlam[0]
    dcx, dconv_c, dccb = _conv_c_bwd(dxc, proj, lw["conv_c"], tb)
    g["conv_c"], g["conv_c_b"] = dconv_c[0:4], dccb[0]
    dyl, dus, ddg, gb, dtb, ddk, dbglu = _s5_tail_bwd(dycat, sv["ylin"], proj, lw["d_skip"], w_glu, lw["b_glu"], tb)
    g["s5_d"], g["s5_b_glu"] = ddk[0], dbglu[0]
    g["s5_w_glu"] = _mm(gb, dtb, name="dw_glu", ta=True, out_dtype=WIRE_DTYPE)
    du, dab, dwb8, dwc8 = _s5_core_bwd(dyl, proj, sv["s5_x"], s5m["w_dx"], s5m["w_du"], s5m["a_row"])
    per_group = lambda d8: _diag_blocks(d8, S5_PER, stacked=2 * S5_CHUNKS).reshape(2, S5_GROUPS, S5_CH, S5_STATE)
    dbb, dcc = per_group(dwb8), per_group(dwc8)
    from_bd = lambda half: jnp.swapaxes(dbb[half], 1, 2).reshape(S5_N, S5_CH)
    df_re, df_im, db_re, db_im = _s5_bbar_bwd(s5m["f_re"], s5m["f_im"], lw["b_re"], lw["b_im"],
                                              from_bd(0), from_bd(1))
    shp = (S5_GROUPS, S5_STATE)
    g["s5_lam_re"], g["s5_lam_im"], dlog_dt = _s5_disc_bwd(
        lw["lam_re"], lw["lam_im"], lw["log_dt"],
        (dab[:, 0:S5_N].reshape(shp), dab[:, S5_N:].reshape(shp), df_re.reshape(shp), df_im.reshape(shp)))
    g["s5_log_dt"] = dlog_dt[:, 0]
    g["s5_b_re"] = db_re.reshape(S5_GROUPS, S5_STATE, S5_CH)
    g["s5_b_im"] = db_im.reshape(S5_GROUPS, S5_STATE, S5_CH)
    g["s5_c_re"], g["s5_c_im"] = dcc[0], -dcc[1]
    dproj = _assemble_dproj(da, dqkv, dbg, dcx, dcg, du, dus, ddg, tb)
    g["w_in"] = _mm_hooked(hook("dw_in"), sv["h"], dproj, name="dw_in", ta=True, out_dtype=WIRE_DTYPE,
                           tm=1024, tn=1536, tk=1024)
    hk = hook("dh")
    dx, dshift, dscale, *got = _dh_mod_bwd(dproj, w_in, dxa, sv["x"], scale, carry=hk and hk[0])
    if hk:
        hk[1](got)
    g["ada"] = jnp.concatenate([dshift[0], dscale[0], dgate[0]])
    return dx, g, dbias


SMALL = ("rel_bias", "conv_a", "conv_c", "conv_c_b", "lru_wa", "lru_ba", "lru_wx", "lru_bx", "lru_lambda",
         "s5_lam_re", "s5_lam_im", "s5_log_dt", "s5_b_re", "s5_b_im", "s5_c_re", "s5_c_im", "s5_d", "s5_b_glu",
         "ln_g", "ln_b")
PER_LAYER_SMALL = SMALL[1:]


def _local_step(x, target, ada_rows, w_in, w_out, w_glu, p, comm=None):
    if comm is None:
        get_w_in = lambda l: w_in[l]
        get_rest = lambda l: (w_out[l], w_glu[l])
        fwd_hooks = bwd_hooks = lambda *_: None
    else:
        get_w_in, get_rest, fwd_hooks, bwd_hooks = comm.w_in, comm.rest, comm.fwd_hooks, comm.bwd_hooks
    s = x.shape[0]
    buckets = _bucket_maps()
    bias_tabs = _bias_tables(p["rel_bias"], buckets)
    head_ones = _block_diag(jnp.ones((ATT_HEADS, HEAD_DIM, HEAD_DIM), MXU_DTYPE))
    lws = [_layer_weights(p, l) for l in range(DEPTH)]
    s5ms = [_s5_matrices(lw) for lw in lws]
    adas = [tuple(ada_rows[l, k * D_MODEL:(k + 1) * D_MODEL][None] for k in range(3)) for l in range(DEPTH)]
    saved = []
    for l in range(DEPTH):
        x, sv = _layer_fwd(x, adas[l], get_w_in(l), functools.partial(get_rest, l), lws[l], s5ms[l], bias_tabs,
                           fwd_hooks(l))
        saved.append(sv)
    loss, dx = _loss_head(x, target, _tiles(s)["tb"])
    grads = [None] * DEPTH
    dbias_sum = []
    for l in reversed(range(DEPTH)):
        dx, grads[l], dbias = _layer_bwd(dx, saved[l], adas[l], get_w_in(l), *get_rest(l), lws[l], s5ms[l],
                                         bias_tabs, head_ones, bwd_hooks(l, grads))
        dbias_sum.append(jnp.stack(dbias))
    drel = _rel_bias_grad(jnp.stack(dbias_sum), buckets)[:, 0:ATT_HEADS]
    small = {n: jnp.stack([grads[l][n] for l in range(DEPTH)]) for n in PER_LAYER_SMALL + ("ada",)}
    small["rel_bias"] = drel
    big = {n: [grads[l][n] for l in range(DEPTH)] for n in ("w_in", "w_out", "s5_w_glu")}
    return loss, dx, big, small


PACK_ROWS = 256


def _pack(parts):
    flat = jnp.concatenate([t.reshape(-1).astype(F32) for t in parts])
    n = flat.shape[0]
    rows = -(-n // (PACK_ROWS * 128)) * PACK_ROWS
    return jnp.pad(flat, (0, rows * 128 - n)).reshape(rows, 128)


def _unpack(packed, shapes):
    flat = packed.reshape(packed.shape[:-2] + (-1,))
    out, off = [], 0
    for shp in shapes:
        size = math.prod(shp)
        out.append(flat[..., off:off + size].reshape(flat.shape[:-1] + tuple(shp)))
        off += size
    return out


def _take_cols(t, chip, width):
    return lax.dynamic_slice_in_dim(t, chip * width, width, axis=t.ndim - 1)


class _Comm:
    IN_W, OUT_R, GLU_R = N_IN // N_CHIPS, D_MODEL // N_CHIPS, BR // N_CHIPS

    def __init__(self, w_in_b, w_out_b, w_glu_b):
        assert DEPTH == 2
        self.shards = (w_in_b, w_out_b, w_glu_b)
        in_w = self.IN_W
        self.w_in_full = {0: _run_exchange(_Gather(
            [(w_in_b, 0, lambda ref: ref.at[0], lambda ref, j: ref.at[:, pl.ds(j * in_w, in_w)])],
            [SDS((D_MODEL, N_IN), WIRE_DTYPE)]), "gather_w_in0")[0]}
        self.w_out_full = self.w_glu_full = None
        self.recv = {}

    def w_in(self, l):
        return self.w_in_full[l]

    def rest(self, l):
        return self.w_out_full[l], self.w_glu_full[l]

    def fwd_hooks(self, l):
        if l != 0:
            return None
        w_in_b, w_out_b, w_glu_b = self.shards
        in_w, out_r, glu_r = self.IN_W, self.OUT_R, self.GLU_R
        whole = lambda ref: ref
        items = [(w_out_b, 0, whole, lambda ref, j: ref.at[:, pl.ds(j * out_r, out_r), :]),
                 (w_glu_b, 1, whole, lambda ref, j: ref.at[:, pl.ds(j * glu_r, glu_r), :]),
                 (w_in_b, 2, lambda ref: ref.at[1], lambda ref, j: ref.at[:, pl.ds(j * in_w, in_w)])]
        shapes = [SDS((DEPTH, D_MODEL, D_MODEL), WIRE_DTYPE), SDS((DEPTH, BR, BR), WIRE_DTYPE),
                  SDS((D_MODEL, N_IN), WIRE_DTYPE)]

        def done(got):
            self.w_out_full, self.w_glu_full, self.w_in_full[1] = got

        return {"in_proj": (_Gather(items, shapes), done)}

    W_IN_ROWS = ((0, 1024), (1024, 512), (1536, 512))

    def _scatter(self, parts):
        in_w, out_r, glu_r = self.IN_W, self.OUT_R, self.GLU_R
        items, shapes, keys = [], [], []
        for oi, (name, l, arr, *rows) in enumerate(parts):
            if name == "w_in":
                r0, nr = rows[0] if rows else (0, D_MODEL)
                cut = functools.partial(lambda ref, j, r0, nr: ref.at[pl.ds(r0, nr), pl.ds(j * in_w, in_w)], r0=r0, nr=nr)
                shard = (nr, in_w)
            elif name == "w_out":
                cut, shard = (lambda ref, j: ref.at[pl.ds(j * out_r, out_r), :]), (out_r, D_MODEL)
            else:
                cut, shard = (lambda ref, j: ref.at[pl.ds(j * glu_r, glu_r), :]), (glu_r, BR)
            items.append((arr, oi, cut, lambda ref, j: ref.at[j]))
            shapes.append(SDS((N_CHIPS,) + shard, WIRE_DTYPE))
            keys.append((name, l) + ((rows[0][0],) if rows else ()))

        def done(got):
            self.recv.update(zip(keys, got))

        return _Exchange(items, shapes), done

    def received(self, name):
        return [self.recv[k] for k in sorted(k for k in self.recv if k[0] == name)]

    def bwd_hooks(self, l, grads):
        if l != 0:
            return None
        g1 = grads[1]
        w_in_part = lambda k: (lambda g: self._scatter([("w_in", 1, g1["w_in"], self.W_IN_ROWS[k])]))
        return {"dw_out": lambda g: self._scatter([("w_out", 1, g1["w_out"]), ("s5_w_glu", 1, g1["s5_w_glu"])]),
                "attn_bwd_d16": w_in_part(0), "attn_bwd_d4": w_in_part(1), "attn_bwd_d1": w_in_part(2),
                "dw_in": lambda g: self._scatter([("w_out", 0, g["w_out"]), ("s5_w_glu", 0, g["s5_w_glu"])]),
                "dh": lambda g: self._scatter([("w_in", 0, g["w_in"])])}


def kernel(x, c, rel_bias, w_ada, b_ada, w_in, conv_a, conv_c, conv_c_b, lru_wa, lru_ba, lru_wx, lru_bx, lru_lambda, s5_lam_re, s5_lam_im, s5_log_dt, s5_b_re, s5_b_im, s5_c_re, s5_c_im, s5_d, s5_w_glu, s5_b_glu, w_out, ln_g, ln_b, loss_target, m_rel_bias, m_w_ada, m_b_ada, m_w_in, m_conv_a, m_conv_c, m_conv_c_b, m_lru_wa, m_lru_ba, m_lru_wx, m_lru_bx, m_lru_lambda, m_s5_lam_re, m_s5_lam_im, m_s5_log_dt, m_s5_b_re, m_s5_b_im, m_s5_c_re, m_s5_c_im, m_s5_d, m_s5_w_glu, m_s5_b_glu, m_w_out, m_ln_g, m_ln_b, v_rel_bias, v_w_ada, v_b_ada, v_w_in, v_conv_a, v_conv_c, v_conv_c_b, v_lru_wa, v_lru_ba, v_lru_wx, v_lru_bx, v_lru_lambda, v_s5_lam_re, v_s5_lam_im, v_s5_log_dt, v_s5_b_re, v_s5_b_im, v_s5_c_re, v_s5_c_im, v_s5_d, v_s5_w_glu, v_s5_b_glu, v_w_out, v_ln_g, v_ln_b):
    args = dict(locals())
    names = ("rel_bias", "w_ada", "b_ada", "w_in", "conv_a", "conv_c", "conv_c_b", "lru_wa", "lru_ba", "lru_wx",
             "lru_bx", "lru_lambda", "s5_lam_re", "s5_lam_im", "s5_log_dt", "s5_b_re", "s5_b_im", "s5_c_re", "s5_c_im",
             "s5_d", "s5_w_glu", "s5_b_glu", "w_out", "ln_g", "ln_b")
    w = {n: args[n] for n in names}
    mom = {n: args["m_" + n] for n in names}
    var = {n: args["v_" + n] for n in names}
    chip = 2 * lax.axis_index("x") + lax.axis_index("y")
    me = 2 * chip + lax.axis_index("c")
    ada_w = 3 * D_MODEL // N_CHIPS
    in_w = N_IN // N_CHIPS
    out_r = D_MODEL // N_CHIPS
    glu_r = BR // N_CHIPS
    conv_w = BR // N_CHIPS

    comm = _Comm(w["w_in"].astype(WIRE_DTYPE), w["w_out"].astype(WIRE_DTYPE), w["s5_w_glu"].astype(WIRE_DTYPE))

    taps = jnp.concatenate([w["conv_a"].reshape(DEPTH * 3, conv_w), w["conv_c"].reshape(DEPTH * 4, conv_w)])
    first = jnp.concatenate([c, jnp.pad(taps, ((0, 1), (0, D_MODEL - conv_w)))])
    got = _allgather8(first, "gather_c_taps").reshape(N_CHIPS, 2, 16, D_MODEL)
    c_all = got[:, :, 0].reshape(N_DEV, D_MODEL)
    taps_all = jnp.transpose(got[:, 0, 1:1 + DEPTH * 7, 0:conv_w], (1, 0, 2)).reshape(DEPTH * 7, BR)
    conv_a_f = taps_all[0:DEPTH * 3].reshape(DEPTH, 3, BR)
    conv_c_f = taps_all[DEPTH * 3:].reshape(DEPTH, 4, BR)

    cond_all = _silu_rows(c_all)
    ada_part = jnp.stack([_mm(cond_all, w["w_ada"][l], name="ada_fwd", tk=D_MODEL, tn=512,
                              bias=_take_cols(w["b_ada"][l][None], chip, ada_w)) for l in range(DEPTH)])
    ada_all = _allgather8(ada_part.reshape(DEPTH * N_DEV, ada_w), "gather_ada")
    ada_all = ada_all.reshape(N_CHIPS, 2, DEPTH, N_DEV, ada_w)[:, 0]
    ada_rows = lax.dynamic_index_in_dim(ada_all, me, axis=2, keepdims=False)
    ada_rows = jnp.transpose(ada_rows, (1, 0, 2)).reshape(DEPTH, 3 * D_MODEL)

    p = dict(w)
    p["conv_a"], p["conv_c"] = conv_a_f, conv_c_f
    loss, dx, _, small = _local_step(x[0], loss_target[0], ada_rows, None, None, None, p, comm)

    sums = [_sum_leading(comm.received(name), 256, "sum_chips") for name in ("w_in", "w_out", "s5_w_glu")]
    others = _sibling_swap(sums, "swap_cores")
    out = {}
    for name, mine, other in zip(("w_in", "w_out", "s5_w_glu"), sums, others):
        shp = w[name].shape
        flat = lambda t: t.reshape(-1, shp[-1])
        res = _adamw(flat(w[name]), [mine, other], flat(mom[name]), flat(var[name]), 128, "adamw_big")
        out[name] = [t.reshape(shp) for t in res]

    small_names = SMALL + ("ada",)
    small["loss"] = loss
    order = small_names + ("loss",)
    shapes = [small[n].shape for n in order]
    gathered = _allgather8(_pack([small[n] for n in order]), "gather_small")
    gathered = gathered.reshape(N_DEV, -1, 128)
    total = dict(zip(order, _unpack(_sum_leading([gathered], PACK_ROWS, "sum_devices"), shapes)))
    d_ada_all = _unpack(gathered, shapes)[order.index("ada")]
    g_small = {n: total[n] for n in SMALL}
    g_small["conv_a"] = _take_cols(total["conv_a"], chip, conv_w)
    g_small["conv_c"] = _take_cols(total["conv_c"], chip, conv_w)
    g_small["b_ada"] = total["ada"]
    g_w_ada = jnp.stack([_mm(cond_all, _take_cols(d_ada_all[:, l], chip, ada_w), name="dw_ada", ta=True, tn=ada_w)
                         for l in range(DEPTH)])
    upd_names = SMALL + ("b_ada",)
    upd_shapes = [w[n].shape for n in upd_names]
    res = _adamw(_pack([w[n] for n in upd_names]), [_pack([g_small[n] for n in upd_names])],
                 _pack([mom[n] for n in upd_names]), _pack([var[n] for n in upd_names]), PACK_ROWS, "adamw_small")
    for k, t in enumerate(res):
        for n, val in zip(upd_names, _unpack(t, upd_shapes)):
            out.setdefault(n, [None] * 4)[k] = val
    shp = w["w_ada"].shape
    flat = lambda t: t.reshape(-1, shp[-1])
    out["w_ada"] = [t.reshape(shp) for t in _adamw(flat(w["w_ada"]), [flat(g_w_ada)], flat(mom["w_ada"]),
                                                  flat(var["w_ada"]), 128, "adamw_ada")]
    return (total["loss"].reshape(()), dx[None]) + tuple(out[n][k] for k in range(4) for n in names)
```

```python
import functools
import math

import jax
import jax.numpy as jnp
from jax import lax
from jax.experimental import pallas as pl
from jax.experimental.pallas import tpu as pltpu

F32 = jnp.float32
MXU_DTYPE = jnp.bfloat16
WIRE_DTYPE = jnp.bfloat16
SDS = jax.ShapeDtypeStruct
MESH = pl.DeviceIdType.MESH
ANY = pl.BlockSpec(memory_space=pl.ANY)
VMEM_LIMIT = 48 * 1024 * 1024

D_MODEL = 2048
DEPTH = 2
BR = 512
ATT_HEADS = 8
HEAD_DIM = 64
DILATIONS = ((128, 1), (512, 4), (2048, 16))
BLK = 128
REL_BUCKETS = 32
REL_MAX_DIST = 2048
LRU_HEADS = 8
LRU_C = 8.0
S5_CH = 16
S5_GROUPS = 32
S5_STATE = 64
S5_N = S5_GROUPS * S5_STATE
N_IN = 12 * BR
ALPHA = (2 * DEPTH) ** 0.25
LN_EPS = 1e-5
NEG = -1e30
ADAM_LR, ADAM_B1, ADAM_B2, ADAM_EPS, ADAM_WD, ADAM_STEP = 0.001, 0.9, 0.999, 1e-08, 0.01, 10
CB_AB, CB_AC, CB_AX, CB_AG, CB_Q, CB_K, CB_V, CB_BG, CB_CX, CB_CG, CB_DU, CB_DG = range(12)
N_CHIPS = 4
N_DEV = 8


def _params(n_axes=0):
    kw = {"dimension_semantics": ("arbitrary",) * n_axes} if n_axes else {}
    return pltpu.CompilerParams(vmem_limit_bytes=VMEM_LIMIT, **kw)


def _rows(tb, w, cb=0):
    return pl.BlockSpec((tb, w), lambda i: (i, cb))


def _prev8(tb, w, cb=0):
    return pl.BlockSpec((8, w), lambda i: (jnp.maximum(i * (tb // 8) - 1, 0), cb))


def _next8(tb, w, n_rows, cb=0):
    return pl.BlockSpec((8, w), lambda i: (jnp.minimum((i + 1) * (tb // 8), n_rows // 8 - 1), cb))


def _const(shape):
    return pl.BlockSpec(shape, lambda *_: (0,) * len(shape))


def _silu(x):
    return x * jax.nn.sigmoid(x)


def _dsilu(x):
    s = jax.nn.sigmoid(x)
    return s * (1.0 + x * (1.0 - s))


def _shift_down(cur, prev8, j):
    rolled = pltpu.roll(cur, j, 0)
    row = lax.broadcasted_iota(jnp.int32, (8, cur.shape[1]), 0)
    first = jnp.where(row < j, pltpu.roll(prev8, j, 0), rolled[0:8])
    return jnp.concatenate([first, rolled[8:]], axis=0)


def _shift_up(cur, next8, j):
    t = cur.shape[0]
    rolled = pltpu.roll(cur, t - j, 0)
    row = lax.broadcasted_iota(jnp.int32, (8, cur.shape[1]), 0)
    last = jnp.where(row >= 8 - j, pltpu.roll(next8, 8 - j, 0), rolled[t - 8:t])
    return jnp.concatenate([rolled[:t - 8], last], axis=0)


def _colsum(x):
    return jnp.sum(x, axis=0, keepdims=True)


def _init_acc(*refs):
    @pl.when(pl.program_id(0) == 0)
    def _():
        for r in refs:
            r[...] = jnp.zeros_like(r)


def _call(body, *, name, out_shape, grid, in_specs, out_specs, scratch_shapes, args, carry=None):
    out_shape, out_specs, in_specs = tuple(out_shape), tuple(out_specs), list(in_specs)
    scratch_shapes = list(scratch_shapes)
    if carry is None:
        return pl.pallas_call(body, name=name, out_shape=out_shape, grid=grid, in_specs=in_specs, out_specs=out_specs,
                              scratch_shapes=scratch_shapes, compiler_params=_params(len(grid)))(*args)
    n_in, n_out, n_scr = len(in_specs), len(out_shape), len(scratch_shapes)

    def wrapped(*refs):
        ins, refs = refs[:n_in], refs[n_in:]
        x_ins, refs = refs[:carry.n_in], refs[carry.n_in:]
        outs, refs = refs[:n_out], refs[n_out:]
        x_outs, refs = refs[:carry.n_out], refs[carry.n_out:]
        scr, x_sems = refs[:n_scr], refs[n_scr:]
        at = [pl.program_id(d) for d in range(len(grid))]
        first = functools.reduce(lambda p, q: p & q, [i == 0 for i in at])
        last = functools.reduce(lambda p, q: p & q, [i == g - 1 for i, g in zip(at, grid)])
        pl.when(first)(lambda: carry.start(x_ins, x_outs, x_sems))
        body(*ins, *outs, *scr)
        pl.when(last)(lambda: carry.wait(x_ins, x_outs, x_sems))

    return pl.pallas_call(
        wrapped, name=name, out_shape=out_shape + carry.out_shapes, grid=grid, in_specs=in_specs + [ANY] * carry.n_in,
        out_specs=out_specs + (ANY,) * carry.n_out, scratch_shapes=scratch_shapes + carry.scratch,
        compiler_params=_params(len(grid)))(*args, *carry.arrays)


def _mm(a, b, *, name, ta=False, tb=False, out_dtype=F32, tm=512, tn=512, tk=512, a_col0=0, a_ncols=None, bias=None,
        carry=None):
    a_ncols = a.shape[1] - a_col0 if a_ncols is None else a_ncols
    m, k = (a_ncols, a.shape[0]) if ta else (a.shape[0], a_ncols)
    n = b.shape[0] if tb else b.shape[1]
    assert k == (b.shape[1] if tb else b.shape[0]), (name, a.shape, b.shape)
    tm, tn, tk = min(tm, m), min(tn, n), min(tk, k)
    nk = k // tk
    a_off = a_col0 // (tm if ta else tk)
    assert m % tm == 0 and n % tn == 0 and k % tk == 0 and a_col0 % (tm if ta else tk) == 0, (name, m, n, k)

    def body(*refs):
        if bias is None:
            a_ref, b_ref, o_ref, acc = refs
        else:
            a_ref, b_ref, bias_ref, o_ref, acc = refs
        kk = pl.program_id(2)

        @pl.when(kk == 0)
        def _():
            acc[...] = jnp.zeros_like(acc)

        dims = (((0 if ta else 1,), (1 if tb else 0,)), ((), ()))
        acc[...] += lax.dot_general(a_ref[...].astype(MXU_DTYPE), b_ref[...].astype(MXU_DTYPE), dims,
                                    preferred_element_type=F32)

        @pl.when(kk == nk - 1)
        def _():
            r = acc[...]
            if bias is not None:
                r = r + bias_ref[...]
            o_ref[...] = r.astype(out_dtype)

    a_spec = (pl.BlockSpec((tk, tm), lambda i, j, kk: (kk, i + a_off)) if ta
              else pl.BlockSpec((tm, tk), lambda i, j, kk: (i, kk + a_off)))
    b_spec = (pl.BlockSpec((tn, tk), lambda i, j, kk: (j, kk)) if tb
              else pl.BlockSpec((tk, tn), lambda i, j, kk: (kk, j)))
    in_specs, args = [a_spec, b_spec], [a, b]
    if bias is not None:
        in_specs.append(pl.BlockSpec((1, tn), lambda i, j, kk: (0, j)))
        args.append(bias)
    out = _call(body, name=name, out_shape=[SDS((m, n), out_dtype)], grid=(m // tm, n // tn, nk), in_specs=in_specs,
                out_specs=[pl.BlockSpec((tm, tn), lambda i, j, kk: (i, j))],
                scratch_shapes=[pltpu.VMEM((tm, tn), F32)], args=args, carry=carry)
    return out[0] if carry is None else out


def _silu_rows(c_all):
    def body(c_ref, o_ref):
        o_ref[...] = _silu(c_ref[...])
    return pl.pallas_call(body, name="cond_silu", out_shape=SDS(c_all.shape, F32))(c_all)


def _modulate(x, scale, shift, tb):
    s, d = x.shape

    def body(x_ref, sc_ref, sh_ref, o_ref):
        o_ref[...] = (x_ref[...] * (1.0 + sc_ref[...]) + sh_ref[...]).astype(MXU_DTYPE)

    return pl.pallas_call(body, name="modulate", out_shape=SDS((s, d), MXU_DTYPE), grid=(s // tb,),
                          in_specs=[_rows(tb, d), _const((1, d)), _const((1, d))], out_specs=_rows(tb, d),
                          compiler_params=_params(1))(x, scale, shift)


def _out_ln(ycat, w_out, x, gate, ln_g, ln_b, tb):
    s, d = x.shape

    def body(yc_ref, w_ref, x_ref, gt_ref, g_ref, b_ref, xn_ref, xh_ref, y_ref, rs_ref):
        y = jnp.dot(yc_ref[...], w_ref[...], preferred_element_type=F32)
        res = ALPHA * x_ref[...] + (1.0 + gt_ref[...]) * y
        mu = jnp.mean(res, axis=-1, keepdims=True)
        cen = res - mu
        var = jnp.mean(cen * cen, axis=-1, keepdims=True)
        rstd = lax.rsqrt(var + LN_EPS)
        xhat = cen * rstd
        xn_ref[...] = xhat * g_ref[...] + b_ref[...]
        xh_ref[...] = xhat
        y_ref[...] = y
        rs_ref[...] = rstd

    big = SDS((s, d), F32)
    return pl.pallas_call(
        body, name="out_proj_ln", out_shape=(big, big, big, SDS((s, 1), F32)), grid=(s // tb,),
        in_specs=[_rows(tb, d), pl.BlockSpec((d, d), lambda i: (0, 0), pipeline_mode=pl.Buffered(1)), _rows(tb, d),
                  _const((1, d)), _const((1, d)), _const((1, d))],
        out_specs=(_rows(tb, d), _rows(tb, d), _rows(tb, d), _rows(tb, 1)), compiler_params=_params(1),
    )(ycat, w_out, x, gate, ln_g, ln_b)


def _ln_bwd(dxn, xhat, y, rstd, ln_g, gate, tb):
    s, d = dxn.shape

    def body(dxn_ref, xh_ref, y_ref, rs_ref, g_ref, gt_ref, dy_ref, dxa_ref, dg_ref, db_ref, dgt_ref):
        _init_acc(dg_ref, db_ref, dgt_ref)
        dxn_t, xh = dxn_ref[...], xh_ref[...]
        dxh = dxn_t * g_ref[...]
        dres = rs_ref[...] * (dxh - jnp.mean(dxh, axis=-1, keepdims=True)
                              - xh * jnp.mean(dxh * xh, axis=-1, keepdims=True))
        dy_ref[...] = ((1.0 + gt_ref[...]) * dres).astype(MXU_DTYPE)
        dxa_ref[...] = ALPHA * dres
        dg_ref[...] += _colsum(dxn_t * xh)
        db_ref[...] += _colsum(dxn_t)
        dgt_ref[...] += _colsum(dres * y_ref[...])

    vec = SDS((1, d), F32)
    return pl.pallas_call(
        body, name="ln_bwd", out_shape=(SDS((s, d), MXU_DTYPE), SDS((s, d), F32), vec, vec, vec), grid=(s // tb,),
        in_specs=[_rows(tb, d), _rows(tb, d), _rows(tb, d), _rows(tb, 1), _const((1, d)), _const((1, d))],
        out_specs=(_rows(tb, d), _rows(tb, d), _const((1, d)), _const((1, d)), _const((1, d))),
        compiler_params=_params(1))(dxn, xhat, y, rstd, ln_g, gate)


def _dh_mod_bwd(dproj, w_in, dxa, x, scale, carry=None):
    s, d = dxa.shape
    k = dproj.shape[1]
    tm, tn, tk = min(1024, s), 1024, 1536
    nk = k // tk
    assert s % tm == 0 and d % tn == 0 and k % tk == 0

    def body(a_ref, b_ref, dxa_ref, x_ref, sc_ref, dx_ref, dsh_ref, dsc_ref, acc):
        i, kk = pl.program_id(1), pl.program_id(2)

        @pl.when(kk == 0)
        def _():
            acc[...] = jnp.zeros_like(acc)

        @pl.when((kk == 0) & (i == 0))
        def _():
            dsh_ref[...] = jnp.zeros_like(dsh_ref)
            dsc_ref[...] = jnp.zeros_like(dsc_ref)

        acc[...] += lax.dot_general(a_ref[...], b_ref[...], (((1,), (1,)), ((), ())), preferred_element_type=F32)

        @pl.when(kk == nk - 1)
        def _():
            dh_t = acc[...]
            dx_ref[...] = dxa_ref[...] + dh_t * (1.0 + sc_ref[...])
            dsh_ref[...] += _colsum(dh_t)
            dsc_ref[...] += _colsum(dh_t * x_ref[...])

    tile = pl.BlockSpec((tm, tn), lambda j, i, kk: (i, j))
    vec = pl.BlockSpec((1, tn), lambda j, i, kk: (0, j))
    return _call(
        body, name="dh", out_shape=(SDS((s, d), F32), SDS((1, d), F32), SDS((1, d), F32)),
        grid=(d // tn, s // tm, nk),
        in_specs=[pl.BlockSpec((tm, tk), lambda j, i, kk: (i, kk)), pl.BlockSpec((tn, tk), lambda j, i, kk: (j, kk)),
                  tile, tile, vec],
        out_specs=(tile, vec, vec), scratch_shapes=[pltpu.VMEM((tm, tn), F32)],
        args=(dproj, w_in, dxa, x, scale), carry=carry)


def _loss_head(y, target, tb):
    s, d = y.shape

    def body(y_ref, t_ref, l_ref, dy_ref):
        _init_acc(l_ref)
        err = y_ref[...] - t_ref[...]
        l_ref[...] += (0.5 / d) * jnp.sum(err * err, keepdims=True)
        dy_ref[...] = err * (1.0 / d)

    return pl.pallas_call(body, name="loss_head", out_shape=(SDS((1, 1), F32), SDS((s, d), F32)), grid=(s // tb,),
                          in_specs=[_rows(tb, d), _rows(tb, d)], out_specs=(_const((1, 1)), _rows(tb, d)),
                          compiler_params=_params(1))(y, target)


def _conv_taps(u, up, w_ref, width):
    out = w_ref[width - 1:width, :] * u
    for j in range(width - 2, -1, -1):
        out = out + w_ref[j:j + 1, :] * _shift_down(u, up, width - 1 - j)
    return out


def _conv_taps_t(g, gn, w_ref, width):
    out = w_ref[width - 1:width, :] * g
    for j in range(width - 2, -1, -1):
        out = out + w_ref[j:j + 1, :] * _shift_up(g, gn, width - 1 - j)
    return out


def _conv_wgrad(dw_ref, g, u, up, width):
    dw_ref[width - 1:width, :] += _colsum(g * u)
    for j in range(width - 1):
        dw_ref[j:j + 1, :] += _colsum(g * _shift_down(u, up, width - 1 - j))


def _branch_a_fwd(proj, conv_w, tb):
    s = proj.shape[0]

    def body(ab, ac, ax, ag, acp, axp, w_ref, o_ref):
        has_prev = (pl.program_id(0) > 0).astype(F32)
        u = ac[...] * ax[...]
        up = acp[...] * axp[...] * has_prev
        o_ref[...] = (ab[...] * _conv_taps(u, up, w_ref, 3) * _silu(ag[...])).astype(MXU_DTYPE)

    return pl.pallas_call(
        body, name="branch_a_fwd", out_shape=SDS((s, BR), MXU_DTYPE), grid=(s // tb,),
        in_specs=[_rows(tb, BR, CB_AB), _rows(tb, BR, CB_AC), _rows(tb, BR, CB_AX), _rows(tb, BR, CB_AG),
                  _prev8(tb, BR, CB_AC), _prev8(tb, BR, CB_AX), _const((8, BR))],
        out_specs=_rows(tb, BR), compiler_params=_params(1))(proj, proj, proj, proj, proj, proj, conv_w)


def _branch_a_bwd(dycat, proj, conv_w, tb):
    s = proj.shape[0]

    def body(dy, dyn, ab, abn, ag, agn, ac, acp, ax, axp, w_ref, o_ref, dw_ref):
        _init_acc(dw_ref)
        i = pl.program_id(0)
        has_prev = (i > 0).astype(F32)
        has_next = (i < pl.num_programs(0) - 1).astype(F32)
        u = ac[...] * ax[...]
        up = acp[...] * axp[...] * has_prev
        v = _conv_taps(u, up, w_ref, 3)
        sg = _silu(ag[...])
        dv = dy[...] * ab[...] * sg
        dvn = dyn[...] * abn[...] * _silu(agn[...]) * has_next
        du = _conv_taps_t(dv, dvn, w_ref, 3)
        o_ref[:, 0:BR] = (dy[...] * v * sg).astype(MXU_DTYPE)
        o_ref[:, BR:2 * BR] = (du * ax[...]).astype(MXU_DTYPE)
        o_ref[:, 2 * BR:3 * BR] = (du * ac[...]).astype(MXU_DTYPE)
        o_ref[:, 3 * BR:4 * BR] = (dy[...] * ab[...] * v * _dsilu(ag[...])).astype(MXU_DTYPE)
        _conv_wgrad(dw_ref, dv, u, up, 3)

    return pl.pallas_call(
        body, name="branch_a_bwd", out_shape=(SDS((s, 4 * BR), MXU_DTYPE), SDS((8, BR), F32)), grid=(s // tb,),
        in_specs=[_rows(tb, BR, 0), _next8(tb, BR, s, 0),
                  _rows(tb, BR, CB_AB), _next8(tb, BR, s, CB_AB), _rows(tb, BR, CB_AG), _next8(tb, BR, s, CB_AG),
                  _rows(tb, BR, CB_AC), _prev8(tb, BR, CB_AC), _rows(tb, BR, CB_AX), _prev8(tb, BR, CB_AX),
                  _const((8, BR))],
        out_specs=(_rows(tb, 4 * BR), _const((8, BR))), compiler_params=_params(1),
    )(dycat, dycat, proj, proj, proj, proj, proj, proj, proj, proj, conv_w)


def _t5_bucket(dist):
    max_exact = REL_BUCKETS // 2
    nf = jnp.maximum(dist, 1).astype(F32)
    large = max_exact + (jnp.log(nf / max_exact) / math.log(REL_MAX_DIST / max_exact)
                         * (REL_BUCKETS - max_exact)).astype(jnp.int32)
    large = jnp.minimum(large, REL_BUCKETS - 1)
    return jnp.where(dist < max_exact, dist, large)


def _bucket_maps():
    maps = []
    i = jnp.arange(BLK)[:, None]
    j = jnp.arange(2 * BLK)[None, :]
    delta = i + BLK - j
    for window, dil in DILATIONS:
        span = window // dil
        bucket = _t5_bucket(jnp.clip(delta, 0, span) * dil)
        maps.append(jnp.where((delta >= 0) & (delta <= span), bucket, -1))
    return jnp.stack(maps).astype(jnp.int32)


def _bias_tables(rel_bias, buckets):
    n_pat = len(DILATIONS)

    def body(rb_ref, bk_ref, o_ref):
        for g in range(n_pat):
            bk = bk_ref[g]
            for h in range(ATT_HEADS):
                def per_bucket(b, acc):
                    return jnp.where(bk == b, rb_ref[b, h], acc)
                o_ref[g, h] = lax.fori_loop(0, REL_BUCKETS, per_bucket, jnp.full((BLK, 2 * BLK), NEG, F32))

    return pl.pallas_call(
        body, name="bias_tables", out_shape=SDS((n_pat, ATT_HEADS, BLK, 2 * BLK), F32),
        in_specs=[pl.BlockSpec(memory_space=pltpu.SMEM), pl.BlockSpec(memory_space=pltpu.VMEM)],
        compiler_params=_params())(rel_bias, buckets)


def _head_masks():
    lane = lax.broadcasted_iota(jnp.int32, (1, 2 * HEAD_DIM), 1)
    return [(lane < HEAD_DIM).astype(F32), (lane >= HEAD_DIM).astype(F32)]


def _strided(base, size, dil):
    return pl.ds(base, size, stride=dil) if dil > 1 else pl.ds(pl.multiple_of(base, BLK), size)


def _attn_groups(s, dil):
    return max(1, min(1024, s) // (dil * BLK)) if dil == 1 else max(1, min(2048, s) // (dil * BLK))


def _attn_fwd(proj, bias, dil):
    s = proj.shape[0]
    grp = _attn_groups(s, dil)
    u1 = dil * BLK
    unit = grp * u1
    nb = s // unit
    w = 2 * HEAD_DIM
    q0, k0, v0 = (cb * (BR // w) for cb in (CB_Q, CB_K, CB_V))

    def body(q_ref, kc_ref, kp_ref, vc_ref, vp_ref, bias_ref, o_ref, lse_ref, kbuf, vbuf):
        n = pl.program_id(1)
        col = lax.broadcasted_iota(jnp.int32, (1, 2 * BLK), 1)
        masks = _head_masks()
        kbuf[0:u1, :] = kp_ref[...]
        kbuf[u1:, :] = kc_ref[...]
        vbuf[0:u1, :] = vp_ref[...]
        vbuf[u1:, :] = vc_ref[...]

        def per_r(t, carry):
            j = t // dil
            base = j * u1 + t % dil
            rows = _strided(base, BLK, dil)
            no_prev = jnp.where((n == 0) & (j == 0) & (col < BLK), NEG, 0.0)
            q = q_ref[rows, :] * (HEAD_DIM ** -0.5)
            k = kbuf[_strided(base, 2 * BLK, dil), :].astype(MXU_DTYPE)
            v = vbuf[_strided(base, 2 * BLK, dil), :].astype(MXU_DTYPE)
            q2 = jnp.concatenate([q * masks[0], q * masks[1]], axis=0).astype(MXU_DTYPE)
            sc = lax.dot_general(q2, k, (((1,), (1,)), ((), ())), preferred_element_type=F32)
            sc = sc + jnp.concatenate([bias_ref[0], bias_ref[1]], axis=0) + no_prev
            mx = jnp.max(sc, axis=-1, keepdims=True)
            p = jnp.exp(sc - mx)
            l = jnp.sum(p, axis=-1, keepdims=True)
            o2 = jnp.dot((p / l).astype(MXU_DTYPE), v, preferred_element_type=F32)
            lse2 = mx + jnp.log(l)
            o_ref[rows, :] = o2[0:BLK] * masks[0] + o2[BLK:2 * BLK] * masks[1]
            lse_ref[rows, :] = lse2[0:BLK] * masks[0] + lse2[BLK:2 * BLK] * masks[1]
            return carry

        lax.fori_loop(0, grp * dil, per_r, 0, unroll=8)

    cur = lambda c0: pl.BlockSpec((unit, w), lambda hp, n: (n, c0 + hp))
    prev = lambda c0: pl.BlockSpec((u1, w), lambda hp, n: (jnp.maximum(n * grp - 1, 0), c0 + hp))
    out = pl.BlockSpec((unit, w), lambda hp, n: (n, hp))
    return pl.pallas_call(
        body, name=f"attn_fwd_d{dil}", out_shape=(SDS((s, BR), F32), SDS((s, BR), F32)), grid=(BR // w, nb),
        in_specs=[cur(q0), cur(k0), prev(k0), cur(v0), prev(v0),
                  pl.BlockSpec((2, BLK, 2 * BLK), lambda hp, n: (hp, 0, 0))],
        out_specs=(out, out),
        scratch_shapes=[pltpu.VMEM((unit + u1, w), F32), pltpu.VMEM((unit + u1, w), F32)],
        compiler_params=_params(2))(proj, proj, proj, proj, proj, bias)


def _softmax3(l0, l1, l2):
    mx = jnp.maximum(jnp.maximum(l0, l1), l2)
    e0, e1, e2 = jnp.exp(l0 - mx), jnp.exp(l1 - mx), jnp.exp(l2 - mx)
    inv = 1.0 / (e0 + e1 + e2)
    return e0 * inv, e1 * inv, e2 * inv


def _attn_combine(os_, lses, proj, tb):
    s = proj.shape[0]

    def body(o0, o1, o2, l0, l1, l2, bg, y_ref):
        w0, w1, w2 = _softmax3(l0[...], l1[...], l2[...])
        attn = w0 * o0[...] + w1 * o1[...] + w2 * o2[...]
        y_ref[...] = (attn * _silu(bg[...])).astype(MXU_DTYPE)

    return pl.pallas_call(
        body, name="attn_combine", out_shape=SDS((s, BR), MXU_DTYPE), grid=(s // tb,),
        in_specs=[_rows(tb, BR)] * 6 + [_rows(tb, BR, CB_BG)], out_specs=_rows(tb, BR),
        compiler_params=_params(1))(*os_, *lses, proj)


def _attn_bwd_pre(dycat, os_, lses, proj, head_ones, tb):
    s = proj.shape[0]

    def body(dy, o0, o1, o2, l0, l1, l2, bg, e_ref, dbg_ref, do0, do1, do2, dm0, dm1, dm2):
        w0, w1, w2 = _softmax3(l0[...], l1[...], l2[...])
        attn = w0 * o0[...] + w1 * o1[...] + w2 * o2[...]
        dattn = dy[...] * _silu(bg[...])
        dbg_ref[...] = (dy[...] * attn * _dsilu(bg[...])).astype(MXU_DTYPE)
        prod = dattn * attn
        hi = prod.astype(MXU_DTYPE)
        lo = (prod - hi.astype(F32)).astype(MXU_DTYPE)
        tot = (jnp.dot(hi, e_ref[...], preferred_element_type=F32)
               + jnp.dot(lo, e_ref[...], preferred_element_type=F32))
        for wg, do_ref, dm_ref in ((w0, do0, dm0), (w1, do1, dm1), (w2, do2, dm2)):
            do_ref[...] = wg * dattn
            dm_ref[...] = wg * tot

    big = SDS((s, BR), F32)
    return pl.pallas_call(
        body, name="attn_bwd_pre", out_shape=(SDS((s, BR), MXU_DTYPE),) + (big,) * 6, grid=(s // tb,),
        in_specs=[_rows(tb, BR, 1)] + [_rows(tb, BR)] * 6 + [_rows(tb, BR, CB_BG), _const((BR, BR))],
        out_specs=(_rows(tb, BR),) * 7, compiler_params=_params(1))(dycat, *os_, *lses, proj, head_ones)


def _attn_bwd(proj, do, lse, dm, bias, dil, carry=None):
    s = proj.shape[0]
    grp = _attn_groups(s, dil)
    u1 = dil * BLK
    unit = grp * u1
    nb = s // unit
    w = 2 * HEAD_DIM
    q0, k0, v0 = (cb * (BR // w) for cb in (CB_Q, CB_K, CB_V))

    def body(q_ref, kc_ref, kp_ref, vc_ref, vp_ref, do_ref, lse_ref, dm_ref, bias_ref,
             dq_ref, dk_ref, dv_ref, dbias_ref, kbuf, vbuf, stage_k, stage_v):
        n = pl.program_id(1)
        col = lax.broadcasted_iota(jnp.int32, (1, 2 * BLK), 1)
        masks = _head_masks()

        @pl.when(n == 0)
        def _():
            dbias_ref[...] = jnp.zeros_like(dbias_ref)
            stage_k[...] = jnp.zeros_like(stage_k)
            stage_v[...] = jnp.zeros_like(stage_v)

        for out_ref, stage in ((dk_ref, stage_k), (dv_ref, stage_v)):
            if grp > 1:
                out_ref[0:unit - u1, :] = stage[u1:unit, :]
            stage[0:u1, :] = stage[unit:unit + u1, :]

        @pl.when(n < nb)
        def _():
            kbuf[0:u1, :] = kp_ref[...]
            kbuf[u1:, :] = kc_ref[...]
            vbuf[0:u1, :] = vp_ref[...]
            vbuf[u1:, :] = vc_ref[...]

            def per_r(t, carry):
                j = t // dil
                base = j * u1 + t % dil
                rows = _strided(base, BLK, dil)
                rows_hi = _strided(base + u1, BLK, dil)
                no_prev = jnp.where((n == 0) & (j == 0) & (col < BLK), NEG, 0.0)
                q = q_ref[rows, :] * (HEAD_DIM ** -0.5)
                k = kbuf[_strided(base, 2 * BLK, dil), :].astype(MXU_DTYPE)
                v = vbuf[_strided(base, 2 * BLK, dil), :].astype(MXU_DTYPE)
                do_t, lse_t, dm_t = do_ref[rows, :], lse_ref[rows, :], dm_ref[rows, :]
                stack = lambda t: jnp.concatenate([t * masks[0], t * masks[1]], axis=0).astype(MXU_DTYPE)
                per_head = lambda t: jnp.concatenate([t[:, 0:1], t[:, HEAD_DIM:HEAD_DIM + 1]], axis=0)
                q2, do2 = stack(q), stack(do_t)
                sc = lax.dot_general(q2, k, (((1,), (1,)), ((), ())), preferred_element_type=F32)
                p = jnp.exp(sc + jnp.concatenate([bias_ref[0], bias_ref[1]], axis=0) + no_prev - per_head(lse_t))
                dp = lax.dot_general(do2, v, (((1,), (1,)), ((), ())), preferred_element_type=F32)
                ds = p * (dp - per_head(dm_t))
                dbias_ref[0] += ds[0:BLK]
                dbias_ref[1] += ds[BLK:2 * BLK]
                dsb, pb = ds.astype(MXU_DTYPE), p.astype(MXU_DTYPE)
                dq2 = jnp.dot(dsb, k, preferred_element_type=F32)
                dk_acc = lax.dot_general(dsb, q2, (((0,), (0,)), ((), ())), preferred_element_type=F32)
                dv_acc = lax.dot_general(pb, do2, (((0,), (0,)), ((), ())), preferred_element_type=F32)
                dq_ref[rows, :] = (dq2[0:BLK] * masks[0] + dq2[BLK:2 * BLK] * masks[1]) * (HEAD_DIM ** -0.5)
                stage_k[rows, :] = stage_k[rows, :] + dk_acc[0:BLK]
                stage_v[rows, :] = stage_v[rows, :] + dv_acc[0:BLK]
                stage_k[rows_hi, :] = dk_acc[BLK:2 * BLK]
                stage_v[rows_hi, :] = dv_acc[BLK:2 * BLK]
                return carry

            lax.fori_loop(0, grp * dil, per_r, 0, unroll=8)

        dk_ref[unit - u1:unit, :] = stage_k[0:u1, :]
        dv_ref[unit - u1:unit, :] = stage_v[0:u1, :]

    qn = lambda n: jnp.minimum(n, nb - 1)
    cur = lambda c0: pl.BlockSpec((unit, w), lambda hp, n: (qn(n), c0 + hp))
    prev = lambda c0: pl.BlockSpec((u1, w), lambda hp, n: (jnp.maximum(qn(n) * grp - 1, 0), c0 + hp))
    row = pl.BlockSpec((unit, w), lambda hp, n: (qn(n), hp))
    late = pl.BlockSpec((unit, w), lambda hp, n: (jnp.maximum(n - 1, 0), hp))
    tab = pl.BlockSpec((2, BLK, 2 * BLK), lambda hp, n: (hp, 0, 0))
    big = SDS((s, BR), F32)
    return _call(
        body, name=f"attn_bwd_d{dil}", out_shape=(big, big, big, SDS((ATT_HEADS, BLK, 2 * BLK), F32)),
        grid=(BR // w, nb + 1),
        in_specs=[cur(q0), cur(k0), prev(k0), cur(v0), prev(v0), row, row, row, tab],
        out_specs=(row, late, late, tab),
        scratch_shapes=[pltpu.VMEM((unit + u1, w), F32)] * 4,
        args=(proj, proj, proj, proj, proj, do, lse, dm, bias), carry=carry)


def _rel_bias_grad(dbias, buckets):
    def body(db_ref, bk_ref, o_ref):
        row = lax.broadcasted_iota(jnp.int32, (REL_BUCKETS, 128), 0)
        lane = lax.broadcasted_iota(jnp.int32, (REL_BUCKETS, 128), 1)

        def per_bucket(b, acc):
            for g in range(len(DILATIONS)):
                hit = bk_ref[g] == b
                for h in range(ATT_HEADS):
                    both = db_ref[0, g, h] + db_ref[1, g, h]
                    val = jnp.sum(jnp.where(hit, both, 0.0), keepdims=True)
                    acc = acc + jnp.where((row == b) & (lane == h), val, 0.0)
            return acc

        o_ref[...] = lax.fori_loop(0, REL_BUCKETS, per_bucket, jnp.zeros((REL_BUCKETS, 128), F32))

    assert dbias.shape[0] == DEPTH == 2
    return pl.pallas_call(body, name="rel_bias_grad", out_shape=SDS((REL_BUCKETS, 128), F32),
                          compiler_params=_params())(dbias, buckets)


def _scan_real(a, b, *, reverse, tb, name):
    s, ch = a.shape
    nt = s // tb
    order = range(7, -1, -1) if reverse else range(8)

    def body(a_ref, b_ref, o_ref, carry):
        @pl.when(pl.program_id(0) == 0)
        def _():
            carry[...] = jnp.zeros_like(carry)

        def group(gi, h):
            r0 = pl.multiple_of((tb // 8 - 1 - gi if reverse else gi) * 8, 8)
            a8, b8 = a_ref[pl.ds(r0, 8), :], b_ref[pl.ds(r0, 8), :]
            rows = [None] * 8
            for k in order:
                if reverse:
                    rows[k] = b8[k:k + 1] + h
                    h = a8[k:k + 1] * rows[k]
                else:
                    h = a8[k:k + 1] * h + b8[k:k + 1]
                    rows[k] = h
            o_ref[pl.ds(r0, 8), :] = jnp.concatenate(rows, axis=0)
            return h

        carry[...] = lax.fori_loop(0, tb // 8, group, carry[...])

    spec = pl.BlockSpec((tb, ch), (lambda i: (nt - 1 - i, 0)) if reverse else (lambda i: (i, 0)))
    return pl.pallas_call(body, name=name, out_shape=SDS((s, ch), F32), grid=(nt,), in_specs=[spec, spec],
                          out_specs=spec, scratch_shapes=[pltpu.VMEM((1, ch), F32)],
                          compiler_params=_params(1))(a, b)


def _scan_tile(s):
    return min(512, s)


def _load_chunked(ref, t0, pt):
    ln = pt // 8
    return jnp.concatenate([ref[pl.ds(t0 + j, 8, stride=ln), :] for j in range(ln)], axis=0)


def _store_natural(ref, t0, pt, val):
    ln = pt // 8
    for j in range(ln):
        ref[pl.ds(t0 + j, 8, stride=ln), :] = val[j * 8:(j + 1) * 8]


def _scan_tile_in_place(a_ref, x_ref, carry, pw, *, reverse):
    ch2 = x_ref.shape[1]
    ch = ch2 // 2
    ln = x_ref.shape[0] // 8
    ar = a_ref[:, 0:ch]
    ai = -a_ref[:, ch:ch2] if reverse else a_ref[:, ch:ch2]

    def cmul(pr, pi, xr, xi):
        return pr * xr - pi * xi, pr * xi + pi * xr

    @pl.when(pl.program_id(0) == 0)
    def _():
        carry[...] = jnp.zeros_like(carry)

        def fill(j, p):
            pw[pl.ds(j, 1), 0:ch] = p[0]
            pw[pl.ds(j, 1), ch:ch2] = p[1]
            return cmul(ar, ai, *p)

        lax.fori_loop(0, ln, fill, (ar, ai))

    def rows_of(j):
        return pl.ds(pl.multiple_of((ln - 1 - j if reverse else j) * 8, 8), 8)

    def local(j, x):
        rows = rows_of(j)
        nr, ni = cmul(ar, ai, *x)
        xr, xi = nr + x_ref[rows, 0:ch], ni + x_ref[rows, ch:ch2]
        x_ref[rows, 0:ch] = xr
        x_ref[rows, ch:ch2] = xi
        return xr, xi

    zero = jnp.zeros((8, ch), F32)
    er, ei = lax.fori_loop(0, ln, local, (zero, zero), unroll=2)
    apr, api = pw[ln - 1:ln, 0:ch], pw[ln - 1:ln, ch:ch2]
    cr, ci = carry[:, 0:ch], carry[:, ch:ch2]
    into_r, into_i = [None] * 8, [None] * 8
    for c in (range(7, -1, -1) if reverse else range(8)):
        into_r[c], into_i[c] = cr, ci
        pr, pi = cmul(apr, api, cr, ci)
        cr, ci = er[c:c + 1] + pr, ei[c:c + 1] + pi
    carry[:, 0:ch] = cr
    carry[:, ch:ch2] = ci
    into_r, into_i = jnp.concatenate(into_r, axis=0), jnp.concatenate(into_i, axis=0)

    def fix(j, carry_):
        rows = rows_of(j)
        dr, di = cmul(pw[pl.ds(j, 1), 0:ch], pw[pl.ds(j, 1), ch:ch2], into_r, into_i)
        x_ref[rows, 0:ch] += dr
        x_ref[rows, ch:ch2] += di
        return carry_

    lax.fori_loop(0, ln, fix, 0, unroll=2)


def _neg_expm1(z):
    series = -z * (1.0 + z * (0.5 + z * (1.0 / 6 + z * (1.0 / 24 + z * (1.0 / 120)))))
    return jnp.where(z > -0.05, series, 1.0 - jnp.exp(z))


def _lru_gate(xc, pre_r, pre_i, lam):
    log_a = -LRU_C * jax.nn.sigmoid(pre_r) * jax.nn.softplus(-lam)
    return jnp.exp(log_a), jnp.sqrt(_neg_expm1(2.0 * log_a)) * jax.nn.sigmoid(pre_i) * xc


def _lru_gates_fwd(proj, conv_w, conv_b, w_cat, b_cat, lam, tb):
    s = proj.shape[0]

    def body(cx, cxp, w_ref, cb_ref, wc_ref, bc_ref, lam_ref, a_ref, b_ref):
        has_prev = (pl.program_id(0) > 0).astype(F32)
        xc = _conv_taps(cx[...], cxp[...] * has_prev, w_ref, 4) + cb_ref[...]
        pre = jnp.dot(xc.astype(MXU_DTYPE), wc_ref[...], preferred_element_type=F32) + bc_ref[...]
        a_ref[...], b_ref[...] = _lru_gate(xc, pre[:, 0:BR], pre[:, BR:2 * BR], lam_ref[...])

    big = SDS((s, BR), F32)
    return pl.pallas_call(
        body, name="lru_gates_fwd", out_shape=(big, big), grid=(s // tb,),
        in_specs=[_rows(tb, BR, CB_CX), _prev8(tb, BR, CB_CX), _const((8, BR)), _const((1, BR)),
                  _const((BR, 2 * BR)), _const((1, 2 * BR)), _const((1, BR))],
        out_specs=(_rows(tb, BR), _rows(tb, BR)), compiler_params=_params(1),
    )(proj, proj, conv_w, conv_b, w_cat, b_cat, lam)


def _gate_out(h, proj, cb, tb, name):
    s = proj.shape[0]

    def body(h_ref, g_ref, o_ref):
        o_ref[...] = (h_ref[...] * _silu(g_ref[...])).astype(MXU_DTYPE)

    return pl.pallas_call(body, name=name, out_shape=SDS((s, BR), MXU_DTYPE), grid=(s // tb,),
                          in_specs=[_rows(tb, BR), _rows(tb, BR, cb)], out_specs=_rows(tb, BR),
                          compiler_params=_params(1))(h, proj)


def _gate_out_bwd(dycat, dy_cb, h, proj, cb, tb, name):
    s = proj.shape[0]

    def body(dy, h_ref, g_ref, dh_ref, dg_ref):
        dh_ref[...] = dy[...] * _silu(g_ref[...])
        dg_ref[...] = (dy[...] * h_ref[...] * _dsilu(g_ref[...])).astype(MXU_DTYPE)

    return pl.pallas_call(body, name=name, out_shape=(SDS((s, BR), F32), SDS((s, BR), MXU_DTYPE)), grid=(s // tb,),
                          in_specs=[_rows(tb, BR, dy_cb), _rows(tb, BR), _rows(tb, BR, cb)],
                          out_specs=(_rows(tb, BR), _rows(tb, BR)), compiler_params=_params(1))(dycat, h, proj)


def _lru_gates_bwd(proj, lmb, h, conv_w, conv_b, w_cat, b_cat, lam, tb):
    s = proj.shape[0]

    def body(cx, cxp, l_ref, h_ref, hp_ref, w_ref, cb_ref, wc_ref, bc_ref, lam_ref,
             dxc_ref, dpre_ref, xc_ref, dbc_ref, dlam_ref):
        _init_acc(dbc_ref, dlam_ref)
        has_prev = (pl.program_id(0) > 0).astype(F32)
        xc = _conv_taps(cx[...], cxp[...] * has_prev, w_ref, 4) + cb_ref[...]
        xcb = xc.astype(MXU_DTYPE)
        pre = jnp.dot(xcb, wc_ref[...], preferred_element_type=F32) + bc_ref[...]
        _, vjp = jax.vjp(_lru_gate, xc, pre[:, 0:BR], pre[:, BR:2 * BR], lam_ref[...])
        lm = l_ref[...]
        dxc, dpr, dpi, dlam = vjp((lm * _shift_down(h_ref[...], hp_ref[...] * has_prev, 1), lm))
        dpre = jnp.concatenate([dpr, dpi], axis=1)
        dpreb = dpre.astype(MXU_DTYPE)
        dxc_ref[...] = dxc + lax.dot_general(dpreb, wc_ref[...], (((1,), (1,)), ((), ())),
                                             preferred_element_type=F32)
        dpre_ref[...] = dpreb
        xc_ref[...] = xcb
        dbc_ref[...] += _colsum(dpre)
        dlam_ref[...] += dlam

    return pl.pallas_call(
        body, name="lru_gates_bwd",
        out_shape=(SDS((s, BR), F32), SDS((s, 2 * BR), MXU_DTYPE), SDS((s, BR), MXU_DTYPE),
                   SDS((1, 2 * BR), F32), SDS((1, BR), F32)),
        grid=(s // tb,),
        in_specs=[_rows(tb, BR, CB_CX), _prev8(tb, BR, CB_CX), _rows(tb, BR), _rows(tb, BR), _prev8(tb, BR),
                  _const((8, BR)), _const((1, BR)), _const((BR, 2 * BR)), _const((1, 2 * BR)), _const((1, BR))],
        out_specs=(_rows(tb, BR), _rows(tb, 2 * BR), _rows(tb, BR), _const((1, 2 * BR)), _const((1, BR))),
        compiler_params=_params(1))(proj, proj, lmb, h, h, conv_w, conv_b, w_cat, b_cat, lam)


def _conv_c_bwd(dxc, proj, conv_w, tb):
    s = proj.shape[0]

    def body(g, gn, cx, cxp, w_ref, dcx_ref, dw_ref, db_ref):
        _init_acc(dw_ref, db_ref)
        i = pl.program_id(0)
        has_prev = (i > 0).astype(F32)
        has_next = (i < pl.num_programs(0) - 1).astype(F32)
        gt = g[...]
        dcx_ref[...] = _conv_taps_t(gt, gn[...] * has_next, w_ref, 4).astype(MXU_DTYPE)
        _conv_wgrad(dw_ref, gt, cx[...], cxp[...] * has_prev, 4)
        db_ref[...] += _colsum(gt)

    return pl.pallas_call(
        body, name="conv_c_bwd", out_shape=(SDS((s, BR), MXU_DTYPE), SDS((8, BR), F32), SDS((1, BR), F32)),
        grid=(s // tb,),
        in_specs=[_rows(tb, BR), _next8(tb, BR, s), _rows(tb, BR, CB_CX), _prev8(tb, BR, CB_CX), _const((8, BR))],
        out_specs=(_rows(tb, BR), _const((8, BR)), _const((1, BR))), compiler_params=_params(1),
    )(dxc, dxc, proj, proj, conv_w)


def _s5_disc(lam_re, lam_im, log_dt):
    dt = jnp.exp(log_dt)
    mag = jnp.exp(lam_re * dt)
    ab_re = mag * jnp.cos(lam_im * dt)
    ab_im = mag * jnp.sin(lam_im * dt)
    den = lam_re * lam_re + lam_im * lam_im
    f_re = ((ab_re - 1.0) * lam_re + ab_im * lam_im) / den
    f_im = (ab_im * lam_re - (ab_re - 1.0) * lam_im) / den
    return ab_re, ab_im, f_re, f_im


def _s5_bbar(f_re, f_im, b_re, b_im):
    return f_re * b_re - f_im * b_im, f_re * b_im + f_im * b_re


def _s5_disc_fwd(lam_re, lam_im, log_dt):
    def body(lr, li, ld, o0, o1, o2, o3):
        o0[...], o1[...], o2[...], o3[...] = _s5_disc(lr[...], li[...], ld[...])
    return pl.pallas_call(body, name="s5_disc_fwd", out_shape=(SDS(lam_re.shape, F32),) * 4)(lam_re, lam_im, log_dt)


def _s5_disc_bwd(lam_re, lam_im, log_dt, cts):
    def body(lr, li, ld, c0, c1, c2, c3, o0, o1, o2):
        _, vjp = jax.vjp(_s5_disc, lr[...], li[...], ld[...])
        o0[...], o1[...], o2[...] = vjp((c0[...], c1[...], c2[...], c3[...]))
    return pl.pallas_call(body, name="s5_disc_bwd", out_shape=(SDS(lam_re.shape, F32), SDS(lam_re.shape, F32),
                                                                SDS(log_dt.shape, F32)))(lam_re, lam_im, log_dt, *cts)


def _s5_bbar_fwd(f_re, f_im, b_re, b_im):
    def body(fr, fi, br, bi, o0, o1):
        o0[...], o1[...] = _s5_bbar(fr[...], fi[...], br[...], bi[...])
    return pl.pallas_call(body, name="s5_bbar_fwd", out_shape=(SDS(b_re.shape, F32),) * 2)(f_re, f_im, b_re, b_im)


def _s5_bbar_bwd(f_re, f_im, b_re, b_im, d_re, d_im):
    def body(fr, fi, br, bi, dr, di, o0, o1, o2, o3):
        _, vjp = jax.vjp(_s5_bbar, fr[...], fi[...], br[...], bi[...])
        o0[...], o1[...], o2[...], o3[...] = vjp((dr[...], di[...]))
    col, mat = SDS(f_re.shape, F32), SDS(b_re.shape, F32)
    return pl.pallas_call(body, name="s5_bbar_bwd", out_shape=(col, col, mat, mat))(f_re, f_im, b_re, b_im, d_re, d_im)


def _s5_tail_fwd(ylin, proj, d_skip, w_glu, b_glu, tb):
    s = proj.shape[0]

    def body(yl, u, dg, dk, w_ref, b_ref, o_ref):
        g = jax.nn.gelu(yl[...] + dk[...] * u[...])
        t = jnp.dot(g.astype(MXU_DTYPE), w_ref[...], preferred_element_type=F32) + b_ref[...]
        o_ref[...] = (g * jax.nn.sigmoid(t) * _silu(dg[...])).astype(MXU_DTYPE)

    return pl.pallas_call(
        body, name="s5_tail_fwd", out_shape=SDS((s, BR), MXU_DTYPE), grid=(s // tb,),
        in_specs=[_rows(tb, BR), _rows(tb, BR, CB_DU), _rows(tb, BR, CB_DG), _const((1, BR)), _const((BR, BR)),
                  _const((1, BR))],
        out_specs=_rows(tb, BR), compiler_params=_params(1))(ylin, proj, proj, d_skip, w_glu, b_glu)


def _s5_tail_bwd(dycat, ylin, proj, d_skip, w_glu, b_glu, tb):
    s = proj.shape[0]

    def body(dy, yl, u, dg, dk, w_ref, b_ref, dyl_ref, dus_ref, ddg_ref, g_ref, dt_ref, ddk_ref, dbg_ref):
        _init_acc(ddk_ref, dbg_ref)
        g, gelu_vjp = jax.vjp(jax.nn.gelu, yl[...] + dk[...] * u[...])
        gb = g.astype(MXU_DTYPE)
        sg = jax.nn.sigmoid(jnp.dot(gb, w_ref[...], preferred_element_type=F32) + b_ref[...])
        dz = dy[...] * _silu(dg[...])
        ddg_ref[...] = (dy[...] * g * sg * _dsilu(dg[...])).astype(MXU_DTYPE)
        dt = dz * g * sg * (1.0 - sg)
        dtb = dt.astype(MXU_DTYPE)
        dgel = dz * sg + lax.dot_general(dtb, w_ref[...], (((1,), (1,)), ((), ())), preferred_element_type=F32)
        dyv, = gelu_vjp(dgel)
        dyl_ref[...] = dyv
        dus_ref[...] = dyv * dk[...]
        g_ref[...] = gb
        dt_ref[...] = dtb
        ddk_ref[...] += _colsum(dyv * u[...])
        dbg_ref[...] += _colsum(dt)

    big, half, vec = SDS((s, BR), F32), SDS((s, BR), MXU_DTYPE), SDS((1, BR), F32)
    return pl.pallas_call(
        body, name="s5_tail_bwd", out_shape=(big, big, half, half, half, vec, vec), grid=(s // tb,),
        in_specs=[_rows(tb, BR, 3), _rows(tb, BR), _rows(tb, BR, CB_DU), _rows(tb, BR, CB_DG), _const((1, BR)),
                  _const((BR, BR)), _const((1, BR))],
        out_specs=(_rows(tb, BR),) * 5 + (_const((1, BR)), _const((1, BR))), compiler_params=_params(1),
    )(dycat, ylin, proj, proj, d_skip, w_glu, b_glu)


def _assemble_dproj(da, dqkv, dbg, dcx, dcg, du, dus, ddg, tb):
    s = da.shape[0]

    def body(da_ref, q0, q1, q2, k0, k1, k2, v0, v1, v2, dbg_ref, dcx_ref, dcg_ref, du_ref, dus_ref, ddg_ref, o_ref):
        o_ref[:, 0:4 * BR] = da_ref[...]
        for j, parts in enumerate(((q0, q1, q2), (k0, k1, k2), (v0, v1, v2))):
            o_ref[:, (4 + j) * BR:(5 + j) * BR] = (parts[0][...] + parts[1][...] + parts[2][...]).astype(MXU_DTYPE)
        o_ref[:, 7 * BR:8 * BR] = dbg_ref[...].astype(MXU_DTYPE)
        o_ref[:, 8 * BR:9 * BR] = dcx_ref[...].astype(MXU_DTYPE)
        o_ref[:, 9 * BR:10 * BR] = dcg_ref[...].astype(MXU_DTYPE)
        o_ref[:, 10 * BR:11 * BR] = (du_ref[...] + dus_ref[...]).astype(MXU_DTYPE)
        o_ref[:, 11 * BR:12 * BR] = ddg_ref[...].astype(MXU_DTYPE)

    flat = [t for grp in dqkv for t in grp]
    return pl.pallas_call(
        body, name="assemble_dproj", out_shape=SDS((s, N_IN), MXU_DTYPE), grid=(s // tb,),
        in_specs=[_rows(tb, 4 * BR)] + [_rows(tb, BR)] * 15, out_specs=_rows(tb, N_IN),
        compiler_params=_params(1))(da, *flat, dbg, dcx, dcg, du, dus, ddg)


def _sum_leading(xs, tr, name):
    n, _, c = xs[0].shape
    nl = len(xs)
    tr = min([tr] + [x.shape[1] for x in xs])
    assert all(x.shape[1] % tr == 0 for x in xs), (name, tr)
    nrs = [x.shape[1] // tr for x in xs]
    starts = [sum(nrs[:l]) for l in range(nl)]

    def body(*refs):
        i = pl.program_id(0)
        for l in range(nl):
            @pl.when((i >= starts[l]) & (i < starts[l] + nrs[l]))
            def _():
                acc = refs[l * n][...].astype(F32)
                for ref in refs[l * n + 1:(l + 1) * n]:
                    acc = acc + ref[...].astype(F32)
                refs[nl * n][...] = acc

    specs = [pl.BlockSpec((None, tr, c), functools.partial(
        lambda i, k, l: (k, jnp.clip(i - starts[l], 0, nrs[l] - 1), 0), k=k, l=l)) for l in range(nl) for k in range(n)]
    return pl.pallas_call(body, name=name, out_shape=SDS((sum(nrs) * tr, c), F32), grid=(sum(nrs),), in_specs=specs,
                          out_specs=pl.BlockSpec((tr, c), lambda i: (i, 0)),
                          compiler_params=_params(1))(*[x for x in xs for _ in range(n)])


def _adamw(w, g_parts, m, v, tr, name, carry=None):
    r, c = w.shape
    tr = min(tr, r)
    n = len(g_parts)
    assert r % tr == 0, (name, r, tr)

    def body(*refs):
        w_ref, m_ref, v_ref = refs[0], refs[1 + n], refs[2 + n]
        g_ref, d_ref, nm_ref, nv_ref = refs[3 + n:]
        g = refs[1][...]
        for ref in refs[2:1 + n]:
            g = g + ref[...]
        mm = ADAM_B1 * m_ref[...] + (1.0 - ADAM_B1) * g
        vv = ADAM_B2 * v_ref[...] + (1.0 - ADAM_B2) * jnp.square(g)
        m_hat = mm / (1.0 - ADAM_B1 ** ADAM_STEP)
        v_hat = vv / (1.0 - ADAM_B2 ** ADAM_STEP)
        g_ref[...] = g
        d_ref[...] = -ADAM_LR * (m_hat / (jnp.sqrt(v_hat) + ADAM_EPS) + ADAM_WD * w_ref[...])
        nm_ref[...] = mm
        nv_ref[...] = vv

    spec = pl.BlockSpec((tr, c), lambda i: (i, 0))
    return _call(body, name=name, out_shape=(SDS((r, c), F32),) * 4, grid=(r // tr,), in_specs=[spec] * (3 + n),
                 out_specs=(spec,) * 4, scratch_shapes=[], args=(w, *g_parts, m, v), carry=carry)


class _AllGather8:
    def __init__(self, block):
        self.m_per = block.shape[0]
        self.arrays, self.n_in, self.n_out = [block], 1, 1
        self.out_shapes = (SDS((N_DEV * self.m_per, block.shape[1]), block.dtype),)
        self.scratch = [pltpu.SemaphoreType.DMA((7,)), pltpu.SemaphoreType.DMA((7,)), pltpu.SemaphoreType.DMA]

    def _copies(self, ins, outs, sems):
        (x_ref,), (out_ref,), (send_sems, recv_sems, local_sem) = ins, outs, sems
        x, y, c = lax.axis_index("x"), lax.axis_index("y"), lax.axis_index("c")
        me, sibling = (x, y, c), (x, y, 1 - c)
        chips = [(1 - x, y), (x, 1 - y), (1 - x, 1 - y)]

        def rows(px, py, pc):
            return out_ref.at[pl.ds((4 * px + 2 * py + pc) * self.m_per, self.m_per), :]

        def copy(k, blk, to, src=None):
            return pltpu.make_async_remote_copy(
                src_ref=rows(*blk) if src is None else src, dst_ref=rows(*blk), send_sem=send_sems.at[k],
                recv_sem=recv_sems.at[k], device_id=to, device_id_type=MESH)

        mine = pltpu.make_async_copy(x_ref, rows(*me), local_sem)
        first = [copy(0, me, sibling, src=x_ref)]
        first += [copy(1 + j, me, (*chip, c), src=x_ref) for j, chip in enumerate(chips)]
        passed = [copy(4 + j, (*chip, c), sibling) for j, chip in enumerate(chips)]
        arrivals = [copy(1 + j, (*chip, c), me) for j, chip in enumerate(chips)]
        from_sibling = [copy(0, sibling, me)] + [copy(4 + j, (*chip, 1 - c), me) for j, chip in enumerate(chips)]
        return mine, first, passed, arrivals, from_sibling

    def start(self, ins, outs, sems):
        mine, first, _, _, _ = self._copies(ins, outs, sems)
        mine.start()
        for cp in first:
            cp.start()

    def wait(self, ins, outs, sems):
        mine, first, passed, arrivals, from_sibling = self._copies(ins, outs, sems)
        for arrived, onward in zip(arrivals, passed):
            arrived.wait_recv()
            onward.start()
        for cp in from_sibling:
            cp.wait_recv()
        for cp in first + passed:
            cp.wait_send()
        mine.wait()


def _allgather8(block, name):
    ex = _AllGather8(block)

    def body(x_ref, out_ref, *sems):
        ex.start((x_ref,), (out_ref,), sems)
        ex.wait((x_ref,), (out_ref,), sems)

    return pl.pallas_call(
        body, name=name, out_shape=ex.out_shapes[0], in_specs=[pl.BlockSpec(memory_space=pltpu.VMEM)],
        out_specs=pl.BlockSpec(memory_space=pltpu.VMEM), scratch_shapes=ex.scratch, compiler_params=_params())(block)


class _Exchange:
    def __init__(self, items, out_shapes):
        self.items, self.out_shapes = list(items), tuple(out_shapes)
        self.arrays = [it[0] for it in self.items]
        n = len(self.items)
        self.n_in, self.n_out = n, len(self.out_shapes)
        self.scratch = [pltpu.SemaphoreType.DMA((n * N_CHIPS,)), pltpu.SemaphoreType.DMA((n * N_CHIPS,)),
                        pltpu.SemaphoreType.DMA((n,))]

    def _copies(self, ins, outs, sems, m):
        send_sems, recv_sems, local_sems = sems
        c = lax.axis_index("c")
        others = [j for j in range(N_CHIPS) if j != m]

        def remote(a, src, dst, to, from_):
            return pltpu.make_async_remote_copy(
                src_ref=src, dst_ref=dst, send_sem=send_sems.at[a * N_CHIPS + to],
                recv_sem=recv_sems.at[a * N_CHIPS + from_], device_id=(to // 2, to % 2, c), device_id_type=MESH)

        local, sends, recvs = [], [], []
        for a, (_, oi, src_of, dst_of) in enumerate(self.items):
            local.append(pltpu.make_async_copy(src_of(ins[a], m), dst_of(outs[oi], m), local_sems.at[a]))
            for j in others:
                sends.append(remote(a, src_of(ins[a], j), dst_of(outs[oi], m), j, m))
                recvs.append(remote(a, src_of(ins[a], m), dst_of(outs[oi], j), j, j))
        return local, sends, recvs

    def _on_my_chip(self, fn):
        chip = 2 * lax.axis_index("x") + lax.axis_index("y")
        for m in range(N_CHIPS):
            pl.when(chip == m)(functools.partial(fn, m))

    def start(self, ins, outs, sems):
        def go(m):
            local, sends, _ = self._copies(ins, outs, sems, m)
            for cp in local + sends:
                cp.start()
        self._on_my_chip(go)

    def wait(self, ins, outs, sems):
        def go(m):
            local, sends, recvs = self._copies(ins, outs, sems, m)
            for cp in recvs:
                cp.wait_recv()
            for cp in sends:
                cp.wait_send()
            for cp in local:
                cp.wait()
        self._on_my_chip(go)


def _half_rows(ref, cc):
    h = ref.shape[-2] // 2
    return ref.at[(slice(None),) * (len(ref.shape) - 2) + (pl.ds(cc * h, h), slice(None))]


class _Gather:
    def __init__(self, items, out_shapes):
        self.items, self.out_shapes = list(items), tuple(out_shapes)
        self.arrays = [it[0] for it in self.items]
        n = len(self.items)
        self.n_in, self.n_out = n, len(self.out_shapes)
        self.scratch = [pltpu.SemaphoreType.DMA((n * N_CHIPS,)) for _ in range(4)] + [pltpu.SemaphoreType.DMA((n,))]

    def _copies(self, ins, outs, sems, m, cc):
        ici_send, ici_recv, d2d_send, d2d_recv, local_sems = sems
        others = [j for j in range(N_CHIPS) if j != m]
        local, sends, arrivals, passed_on, from_sibling = [], [], [], [], []
        for a, (_, oi, src_of, dst_of) in enumerate(self.items):
            src, out = src_of(ins[a]), outs[oi]
            local.append(pltpu.make_async_copy(src, dst_of(out, m), local_sems.at[a]))
            for j in others:
                k = a * N_CHIPS + j
                mine_there = _half_rows(dst_of(out, m), cc)
                theirs_here = _half_rows(dst_of(out, j), cc)
                sends.append(pltpu.make_async_remote_copy(
                    src_ref=_half_rows(src, cc), dst_ref=mine_there, send_sem=ici_send.at[k],
                    recv_sem=ici_recv.at[a * N_CHIPS + m], device_id=(j // 2, j % 2, cc), device_id_type=MESH))
                arrivals.append(pltpu.make_async_remote_copy(
                    src_ref=_half_rows(src, cc), dst_ref=theirs_here, send_sem=ici_send.at[k], recv_sem=ici_recv.at[k],
                    device_id=(j // 2, j % 2, cc), device_id_type=MESH))
                passed_on.append(pltpu.make_async_remote_copy(
                    src_ref=theirs_here, dst_ref=theirs_here, send_sem=d2d_send.at[k], recv_sem=d2d_recv.at[k],
                    device_id=(m // 2, m % 2, 1 - cc), device_id_type=MESH))
                other_half = _half_rows(dst_of(out, j), 1 - cc)
                from_sibling.append(pltpu.make_async_remote_copy(
                    src_ref=other_half, dst_ref=other_half, send_sem=d2d_send.at[k], recv_sem=d2d_recv.at[k],
                    device_id=(m // 2, m % 2, 1 - cc), device_id_type=MESH))
        return local, sends, arrivals, passed_on, from_sibling

    def _on_my_core(self, fn):
        chip = 2 * lax.axis_index("x") + lax.axis_index("y")
        c = lax.axis_index("c")
        for m in range(N_CHIPS):
            for cc in range(2):
                pl.when((chip == m) & (c == cc))(functools.partial(fn, m, cc))

    def start(self, ins, outs, sems):
        def go(m, cc):
            local, sends, _, _, _ = self._copies(ins, outs, sems, m, cc)
            for cp in local + sends:
                cp.start()
        self._on_my_core(go)

    def wait(self, ins, outs, sems):
        def go(m, cc):
            local, sends, arrivals, passed_on, from_sibling = self._copies(ins, outs, sems, m, cc)
            for arrived, onward in zip(arrivals, passed_on):
                arrived.wait_recv()
                onward.start()
            for cp in from_sibling:
                cp.wait_recv()
            for cp in sends + passed_on:
                cp.wait_send()
            for cp in local:
                cp.wait()
        self._on_my_core(go)


def _run_exchange(ex, name):
    def body(*refs):
        ins, outs, sems = refs[:ex.n_in], refs[ex.n_in:ex.n_in + ex.n_out], refs[ex.n_in + ex.n_out:]
        ex.start(ins, outs, sems)
        ex.wait(ins, outs, sems)

    return pl.pallas_call(
        body, name=name, out_shape=ex.out_shapes, in_specs=[ANY] * ex.n_in, out_specs=(ANY,) * ex.n_out,
        scratch_shapes=ex.scratch, compiler_params=_params())(*ex.arrays)


def _sibling_swap(arrays, name):
    n = len(arrays)

    def body(*refs):
        ins, outs = refs[:n], refs[n:2 * n]
        send_sems, recv_sems = refs[2 * n:]
        peer = (lax.axis_index("x"), lax.axis_index("y"), 1 - lax.axis_index("c"))
        cps = [pltpu.make_async_remote_copy(src_ref=ins[a], dst_ref=outs[a], send_sem=send_sems.at[a],
                                            recv_sem=recv_sems.at[a], device_id=peer, device_id_type=MESH)
               for a in range(n)]
        for cp in cps:
            cp.start()
        for cp in cps:
            cp.wait()

    return pl.pallas_call(
        body, name=name, out_shape=tuple(SDS(a.shape, a.dtype) for a in arrays), in_specs=[ANY] * n,
        out_specs=(ANY,) * n, scratch_shapes=[pltpu.SemaphoreType.DMA((n,)), pltpu.SemaphoreType.DMA((n,))],
        compiler_params=_params())(*arrays)


def _block_diag(w):
    h, n, m = w.shape
    eye = jnp.eye(h, dtype=w.dtype)
    return (w[:, :, None, :] * eye[:, None, :, None]).reshape(h * n, h * m)


def _diag_blocks(d, h, col0=0, ncols=None, stacked=1):
    ncols = d.shape[1] - col0 if ncols is None else ncols
    n, m = d.shape[0] // (h * stacked), ncols // h
    lanes = 128
    assert m <= lanes and lanes % m == 0 and col0 % lanes == 0

    def body(d_ref, o_ref):
        for gi in range(h * stacked):
            c = col0 + (gi % h) * m
            chunk = d_ref[gi * n:(gi + 1) * n, c // lanes * lanes:c // lanes * lanes + lanes]
            o_ref[gi * n:(gi + 1) * n, :] = chunk[:, c % lanes:c % lanes + m]

    out = pl.pallas_call(body, name="diag_blocks", out_shape=SDS((stacked * h * n, m), d.dtype),
                         compiler_params=_params())(d)
    return out.reshape(stacked * h, n, m)


S5_CHUNKS = 4
S5_PER = S5_GROUPS // S5_CHUNKS
CH_W = S5_PER * S5_CH
ST_W = S5_PER * S5_STATE


def _bd_stack(mats):
    _, _, n, m = mats.shape
    eye = jnp.eye(S5_PER, dtype=mats.dtype)
    t = mats.reshape(2, S5_CHUNKS, S5_PER, n, m)
    bd = t[:, :, :, :, None, :] * eye[None, None, :, None, :, None]
    return bd.reshape(2 * S5_CHUNKS, S5_PER * n, S5_PER * m).astype(MXU_DTYPE)


def _chunks_chunked(src_ref, buf):
    pt = src_ref.shape[0]
    out = []
    for q in range(S5_CHUNKS):
        buf[q] = src_ref[:, q * CH_W:(q + 1) * CH_W]
        out.append(_load_chunked(buf.at[q], 0, pt).astype(MXU_DTYPE))
    return out


def _expand_into(dst_ref, chunks, w_ref):
    for b in range(2 * S5_CHUNKS):
        dst_ref[:, b * ST_W:(b + 1) * ST_W] = jnp.dot(chunks[b % S5_CHUNKS], w_ref[b], preferred_element_type=F32)


def _reduce_from(src_ref, w_ref, buf, dst_ref):
    pt = src_ref.shape[0]
    for q in range(S5_CHUNKS):
        y = jnp.dot(src_ref[:, q * ST_W:(q + 1) * ST_W].astype(MXU_DTYPE), w_ref[q], preferred_element_type=F32)
        p = S5_CHUNKS + q
        y = y + jnp.dot(src_ref[:, p * ST_W:(p + 1) * ST_W].astype(MXU_DTYPE), w_ref[p], preferred_element_type=F32)
        _store_natural(buf.at[q], 0, pt, y)
        dst_ref[:, q * CH_W:(q + 1) * CH_W] = buf[q]


def _s5_core_fwd(proj, w_bu, w_cx, a_row):
    s = proj.shape[0]
    pt = _scan_tile(s)
    ch2 = 2 * S5_N

    def body(u_ref, wb_ref, wc_ref, a_ref, x_ref, y_ref, carry, pw, buf):
        _expand_into(x_ref, _chunks_chunked(u_ref, buf), wb_ref)
        _scan_tile_in_place(a_ref, x_ref, carry, pw, reverse=False)
        _reduce_from(x_ref, wc_ref, buf, y_ref)

    return pl.pallas_call(
        body, name="s5_core_fwd", out_shape=(SDS((s, ch2), F32), SDS((s, BR), F32)), grid=(s // pt,),
        in_specs=[_rows(pt, BR, CB_DU), _const(w_bu.shape), _const(w_cx.shape), _const((1, ch2))],
        out_specs=(_rows(pt, ch2), _rows(pt, BR)),
        scratch_shapes=[pltpu.VMEM((1, ch2), F32), pltpu.VMEM((pt // 8, ch2), F32),
                        pltpu.VMEM((S5_CHUNKS, pt, CH_W), F32)],
        compiler_params=_params(1))(proj, w_bu, w_cx, a_row)


def _s5_core_bwd(dyl, proj, x, w_dx, w_du, a_row):
    s = proj.shape[0]
    pt = _scan_tile(s)
    nt = s // pt
    ch2 = 2 * S5_N
    ch = S5_N

    def body(dy_ref, u_ref, x_ref, xp_ref, wx_ref, wu_ref, a_ref, du_ref, da_ref, dwb_ref, dwc_ref,
             l_ref, carry, pw, buf, buf2):
        i = pl.program_id(0)
        _init_acc(da_ref, dwb_ref, dwc_ref)
        dy_c = _chunks_chunked(dy_ref, buf)
        u_c = _chunks_chunked(u_ref, buf2)
        _expand_into(l_ref, dy_c, wx_ref)
        _scan_tile_in_place(a_ref, l_ref, carry, pw, reverse=True)
        has_prev = (i < nt - 1).astype(F32)
        row = lax.broadcasted_iota(jnp.int32, (8, ch2), 0)
        first = jnp.where(row == 0, pltpu.roll(xp_ref[...], 1, 0) * has_prev, pltpu.roll(x_ref[pt - 8:pt, :], 1, 0))
        xprev = jnp.concatenate([first, x_ref[0:pt - 8, :]], axis=0)
        lr, li, xr, xi = l_ref[:, 0:ch], l_ref[:, ch:ch2], xprev[:, 0:ch], xprev[:, ch:ch2]
        da_ref[:, 0:ch] += _colsum(lr * xr + li * xi)
        da_ref[:, ch:ch2] += _colsum(li * xr - lr * xi)
        _reduce_from(l_ref, wu_ref, buf, du_ref)
        tn = (((0,), (0,)), ((), ()))
        for b in range(2 * S5_CHUNKS):
            cols, rows = slice(b * ST_W, (b + 1) * ST_W), slice(b * CH_W, (b + 1) * CH_W)
            dwb_ref[rows, :] += lax.dot_general(u_c[b % S5_CHUNKS], l_ref[:, cols].astype(MXU_DTYPE), tn,
                                                preferred_element_type=F32)
            dwc_ref[rows, :] += lax.dot_general(dy_c[b % S5_CHUNKS], x_ref[:, cols].astype(MXU_DTYPE), tn,
                                                preferred_element_type=F32)

    rev = lambda w, cb=0: pl.BlockSpec((pt, w), lambda i: (nt - 1 - i, cb))
    halo = pl.BlockSpec((8, ch2), lambda i: (jnp.maximum((nt - 1 - i) * (pt // 8) - 1, 0), 0))
    wshape = SDS((2 * S5_CHUNKS * CH_W, ST_W), F32)
    return pl.pallas_call(
        body, name="s5_core_bwd", out_shape=(SDS((s, BR), F32), SDS((1, ch2), F32), wshape, wshape), grid=(nt,),
        in_specs=[rev(BR, 0), rev(BR, CB_DU), rev(ch2), halo, _const(w_dx.shape), _const(w_du.shape),
                  _const((1, ch2))],
        out_specs=(rev(BR), _const((1, ch2)), _const(wshape.shape), _const(wshape.shape)),
        scratch_shapes=[pltpu.VMEM((pt, ch2), F32), pltpu.VMEM((1, ch2), F32), pltpu.VMEM((pt // 8, ch2), F32),
                        pltpu.VMEM((S5_CHUNKS, pt, CH_W), F32), pltpu.VMEM((S5_CHUNKS, pt, CH_W), F32)],
        compiler_params=_params(1))(dyl, proj, x, x, w_dx, w_du, a_row)


def _tiles(s):
    return dict(tb=min(512, s), tln=min(256, s))


def _layer_weights(p, l):
    pad8 = lambda w: jnp.pad(w, ((0, 8 - w.shape[0]), (0, 0)))
    return dict(
        conv_a=pad8(p["conv_a"][l]), conv_c=pad8(p["conv_c"][l]), conv_c_b=p["conv_c_b"][l][None],
        w_cat=jnp.concatenate([_block_diag(p["lru_wa"][l]), _block_diag(p["lru_wx"][l])], axis=1).astype(MXU_DTYPE),
        b_cat=jnp.concatenate([p["lru_ba"][l], p["lru_bx"][l]])[None], lam=p["lru_lambda"][l][None],
        lam_re=p["s5_lam_re"][l], lam_im=p["s5_lam_im"][l], log_dt=p["s5_log_dt"][l][:, None],
        b_re=p["s5_b_re"][l].reshape(S5_N, S5_CH), b_im=p["s5_b_im"][l].reshape(S5_N, S5_CH),
        c_re=p["s5_c_re"][l], c_im=p["s5_c_im"][l], d_skip=p["s5_d"][l][None], b_glu=p["s5_b_glu"][l][None],
        ln_g=p["ln_g"][l][None], ln_b=p["ln_b"][l][None])


def _s5_matrices(lw):
    ab_re, ab_im, f_re, f_im = _s5_disc_fwd(lw["lam_re"], lw["lam_im"], lw["log_dt"])
    f_re, f_im = f_re.reshape(S5_N, 1), f_im.reshape(S5_N, 1)
    bb_re, bb_im = _s5_bbar_fwd(f_re, f_im, lw["b_re"], lw["b_im"])
    bb = jnp.stack([bb_re, bb_im]).reshape(2, S5_GROUPS, S5_STATE, S5_CH)
    cc = jnp.stack([lw["c_re"], -lw["c_im"]])
    a_row = jnp.concatenate([ab_re.reshape(1, S5_N), ab_im.reshape(1, S5_N)], axis=1)
    return dict(f_re=f_re, f_im=f_im, a_row=a_row, w_bu=_bd_stack(jnp.swapaxes(bb, 2, 3)), w_du=_bd_stack(bb),
                w_cx=_bd_stack(jnp.swapaxes(cc, 2, 3)), w_dx=_bd_stack(cc))


def _mm_hooked(hook, *args, **kw):
    if hook is None:
        return _mm(*args, **kw)
    out = _mm(*args, carry=hook[0], **kw)
    hook[1](out[1:])
    return out[0]


def _layer_fwd(x, ada, w_in, get_rest, lw, s5m, bias_tabs, hooks=None):
    s = x.shape[0]
    t = _tiles(s)
    tb = t["tb"]
    shift, scale, gate = ada
    hooks = hooks or {}
    h = _modulate(x, scale, shift, tb)
    proj = _mm_hooked(hooks.get("in_proj"), h, w_in, name="in_proj", tm=1024, tn=1536, tk=D_MODEL)
    w_out, w_glu = get_rest()
    y_a = _branch_a_fwd(proj, lw["conv_a"], tb)
    os_, lses = [], []
    for g, (_, dil) in enumerate(DILATIONS):
        o, lse = _attn_fwd(proj, bias_tabs[g], dil)
        os_.append(o)
        lses.append(lse)
    y_b = _attn_combine(os_, lses, proj, tb)
    lru_a, lru_b = _lru_gates_fwd(proj, lw["conv_c"], lw["conv_c_b"], lw["w_cat"], lw["b_cat"], lw["lam"], tb)
    lru_h = _scan_real(lru_a, lru_b, reverse=False, tb=tb, name="lru_scan")
    y_c = _gate_out(lru_h, proj, CB_CG, tb, "lru_out")
    s5_x, ylin = _s5_core_fwd(proj, s5m["w_bu"], s5m["w_cx"], s5m["a_row"])
    y_d = _s5_tail_fwd(ylin, proj, lw["d_skip"], w_glu, lw["b_glu"], tb)
    ycat = jnp.concatenate([y_a, y_b, y_c, y_d], axis=1)
    x_next, xhat, y, rstd = _out_ln(ycat, w_out, x, gate, lw["ln_g"], lw["ln_b"], t["tln"])
    saved = dict(x=x, h=h, proj=proj, os=os_, lses=lses, lru_a=lru_a, lru_h=lru_h, s5_x=s5_x, ylin=ylin, ycat=ycat,
                 xhat=xhat, y=y, rstd=rstd)
    return x_next, saved


def _layer_bwd(dxn, sv, ada, w_in, w_out, w_glu, lw, s5m, bias_tabs, head_ones, hooks=None):
    s = dxn.shape[0]
    t = _tiles(s)
    tb = t["tb"]
    shift, scale, gate = ada
    proj = sv["proj"]
    g = {}
    hook = lambda name: hooks[name](g) if hooks and name in hooks else None
    dyb, dxa, g["ln_g"], g["ln_b"], dgate = _ln_bwd(dxn, sv["xhat"], sv["y"], sv["rstd"], lw["ln_g"], gate, t["tln"])
    g["w_out"] = _mm_hooked(hook("dw_out"), sv["ycat"], dyb, name="dw_out", ta=True, out_dtype=WIRE_DTYPE,
                            tm=1024, tn=1024, tk=2048)
    dycat =_mm(dyb, w_out, name="dycat", tb=True, tm=1024, tn=1024, tk=D_MODEL)
    da, dconv_a = _branch_a_bwd(dycat, proj, lw["conv_a"], tb)
    g["conv_a"] = dconv_a[0:3]
    pre = _attn_bwd_pre(dycat, sv["os"], sv["lses"], proj, head_ones, tb)
    dbg, dos, dms = pre[0], pre[1:4], pre[4:7]
    dqkv, dbias = [], []
    for gi, (_, dil) in enumerate(DILATIONS):
        hk = hook(f"attn_bwd_d{dil}")
        dq, dk, dv, dbi, *got = _attn_bwd(proj, dos[gi], sv["lses"][gi], dms[gi], bias_tabs[gi], dil,
                                          carry=hk and hk[0])
        if hk:
            hk[1](got)
        dqkv.append((dq, dk, dv))
        dbias.append(dbi)
    dqkv = list(zip(*dqkv))
    dh, dcg = _gate_out_bwd(dycat, 2, sv["lru_h"], proj, CB_CG, tb, "lru_out_bwd")
    lmb = _scan_real(sv["lru_a"], dh, reverse=True, tb=tb, name="lru_scan_bwd")
    dxc, dpre, xcb, dbcat, dlam = _lru_gates_bwd(proj, lmb, sv["lru_h"], lw["conv_c"], lw["conv_c_b"], lw["w_cat"],
                                                  lw["b_cat"], lw["lam"], tb)
    dwcat = _mm(xcb, dpre, name="dw_lru", ta=True, tn=1024)
    g["lru_wa"] = _diag_blocks(dwcat, LRU_HEADS, 0, BR)
    g["lru_wx"] = _diag_blocks(dwcat, LRU_HEADS, BR, BR)
    g["lru_ba"], g["lru_bx"], g["lru_lambda"] = dbcat[0, 0:BR], dbcat[0, BR:2 * BR], dlam[0]
    dcx, dconv_c, dccb = _conv_c_bwd(dxc, proj, lw["conv_c"], tb)
    g["conv_c"], g["conv_c_b"] = dconv_c[0:4], dccb[0]
    dyl, dus, ddg, gb, dtb, ddk, dbglu = _s5_tail_bwd(dycat, sv["ylin"], proj, lw["d_skip"], w_glu, lw["b_glu"], tb)
    g["s5_d"], g["s5_b_glu"] = ddk[0], dbglu[0]
    g["s5_w_glu"] = _mm(gb, dtb, name="dw_glu", ta=True, out_dtype=WIRE_DTYPE)
    du, dab, dwb8, dwc8 = _s5_core_bwd(dyl, proj, sv["s5_x"], s5m["w_dx"], s5m["w_du"], s5m["a_row"])
    per_group = lambda d8: _diag_blocks(d8, S5_PER, stacked=2 * S5_CHUNKS).reshape(2, S5_GROUPS, S5_CH, S5_STATE)
    dbb, dcc = per_group(dwb8), per_group(dwc8)
    from_bd = lambda half: jnp.swapaxes(dbb[half], 1, 2).reshape(S5_N, S5_CH)
    df_re, df_im, db_re, db_im = _s5_bbar_bwd(s5m["f_re"], s5m["f_im"], lw["b_re"], lw["b_im"],
                                              from_bd(0), from_bd(1))
    shp = (S5_GROUPS, S5_STATE)
    g["s5_lam_re"], g["s5_lam_im"], dlog_dt = _s5_disc_bwd(
        lw["lam_re"], lw["lam_im"], lw["log_dt"],
        (dab[:, 0:S5_N].reshape(shp), dab[:, S5_N:].reshape(shp), df_re.reshape(shp), df_im.reshape(shp)))
    g["s5_log_dt"] = dlog_dt[:, 0]
    g["s5_b_re"] = db_re.reshape(S5_GROUPS, S5_STATE, S5_CH)
    g["s5_b_im"] = db_im.reshape(S5_GROUPS, S5_STATE, S5_CH)
    g["s5_c_re"], g["s5_c_im"] = dcc[0], -dcc[1]
    dproj = _assemble_dproj(da, dqkv, dbg, dcx, dcg, du, dus, ddg, tb)
    g["w_in"] = _mm_hooked(hook("dw_in"), sv["h"], dproj, name="dw_in", ta=True, out_dtype=WIRE_DTYPE,
                           tm=1024, tn=1536, tk=2048)
    hk = hook("dh")
    dx, dshift, dscale, *got = _dh_mod_bwd(dproj, w_in, dxa, sv["x"], scale, carry=hk and hk[0])
    if hk:
        hk[1](got)
    g["ada"] = jnp.concatenate([dshift[0], dscale[0], dgate[0]])
    return dx, g, dbias


SMALL = ("rel_bias", "conv_a", "conv_c", "conv_c_b", "lru_wa", "lru_ba", "lru_wx", "lru_bx", "lru_lambda",
         "s5_lam_re", "s5_lam_im", "s5_log_dt", "s5_b_re", "s5_b_im", "s5_c_re", "s5_c_im", "s5_d", "s5_b_glu",
         "ln_g", "ln_b")
PER_LAYER_SMALL = SMALL[1:]


def _local_step(x, target, ada_rows, w_in, w_out, w_glu, p, comm=None):
    if comm is None:
        get_w_in = lambda l: w_in[l]
        get_rest = lambda l: (w_out[l], w_glu[l])
        fwd_hooks = bwd_hooks = lambda *_: None
    else:
        get_w_in, get_rest, fwd_hooks, bwd_hooks = comm.w_in, comm.rest, comm.fwd_hooks, comm.bwd_hooks
    s = x.shape[0]
    buckets = _bucket_maps()
    bias_tabs = _bias_tables(p["rel_bias"], buckets)
    head_ones = _block_diag(jnp.ones((ATT_HEADS, HEAD_DIM, HEAD_DIM), MXU_DTYPE))
    lws = [_layer_weights(p, l) for l in range(DEPTH)]
    s5ms = [_s5_matrices(lw) for lw in lws]
    adas = [tuple(ada_rows[l, k * D_MODEL:(k + 1) * D_MODEL][None] for k in range(3)) for l in range(DEPTH)]
    saved = []
    for l in range(DEPTH):
        x, sv = _layer_fwd(x, adas[l], get_w_in(l), functools.partial(get_rest, l), lws[l], s5ms[l], bias_tabs,
                           fwd_hooks(l))
        saved.append(sv)
    loss, dx = _loss_head(x, target, _tiles(s)["tb"])
    grads = [None] * DEPTH
    dbias_sum = []
    for l in reversed(range(DEPTH)):
        dx, grads[l], dbias = _layer_bwd(dx, saved[l], adas[l], get_w_in(l), *get_rest(l), lws[l], s5ms[l],
                                         bias_tabs, head_ones, bwd_hooks(l, grads))
        dbias_sum.append(jnp.stack(dbias))
    drel = _rel_bias_grad(jnp.stack(dbias_sum), buckets)[:, 0:ATT_HEADS]
    small = {n: jnp.stack([grads[l][n] for l in range(DEPTH)]) for n in PER_LAYER_SMALL + ("ada",)}
    small["rel_bias"] = drel
    big = {n: [grads[l][n] for l in range(DEPTH)] for n in ("w_in", "w_out", "s5_w_glu")}
    return loss, dx, big, small


PACK_ROWS = 256


def _pack(parts):
    flat = jnp.concatenate([t.reshape(-1).astype(F32) for t in parts])
    n = flat.shape[0]
    rows = -(-n // (PACK_ROWS * 128)) * PACK_ROWS
    return jnp.pad(flat, (0, rows * 128 - n)).reshape(rows, 128)


def _unpack(packed, shapes):
    flat = packed.reshape(packed.shape[:-2] + (-1,))
    out, off = [], 0
    for shp in shapes:
        size = math.prod(shp)
        out.append(flat[..., off:off + size].reshape(flat.shape[:-1] + tuple(shp)))
        off += size
    return out


def _take_cols(t, chip, width):
    return lax.dynamic_slice_in_dim(t, chip * width, width, axis=t.ndim - 1)


class _Comm:
    IN_W, OUT_R, GLU_R = N_IN // N_CHIPS, D_MODEL // N_CHIPS, BR // N_CHIPS

    def __init__(self, w_in_b, w_out_b, w_glu_b):
        assert DEPTH == 2
        self.shards = (w_in_b, w_out_b, w_glu_b)
        in_w = self.IN_W
        self.w_in_full = {0: _run_exchange(_Gather(
            [(w_in_b, 0, lambda ref: ref.at[0], lambda ref, j: ref.at[:, pl.ds(j * in_w, in_w)])],
            [SDS((D_MODEL, N_IN), WIRE_DTYPE)]), "gather_w_in0")[0]}
        self.w_out_full = self.w_glu_full = None
        self.recv = {}

    def w_in(self, l):
        return self.w_in_full[l]

    def rest(self, l):
        return self.w_out_full[l], self.w_glu_full[l]

    def fwd_hooks(self, l):
        if l != 0:
            return None
        w_in_b, w_out_b, w_glu_b = self.shards
        in_w, out_r, glu_r = self.IN_W, self.OUT_R, self.GLU_R
        whole = lambda ref: ref
        items = [(w_out_b, 0, whole, lambda ref, j: ref.at[:, pl.ds(j * out_r, out_r), :]),
                 (w_glu_b, 1, whole, lambda ref, j: ref.at[:, pl.ds(j * glu_r, glu_r), :]),
                 (w_in_b, 2, lambda ref: ref.at[1], lambda ref, j: ref.at[:, pl.ds(j * in_w, in_w)])]
        shapes = [SDS((DEPTH, D_MODEL, D_MODEL), WIRE_DTYPE), SDS((DEPTH, BR, BR), WIRE_DTYPE),
                  SDS((D_MODEL, N_IN), WIRE_DTYPE)]

        def done(got):
            self.w_out_full, self.w_glu_full, self.w_in_full[1] = got

        return {"in_proj": (_Gather(items, shapes), done)}

    W_IN_ROWS = ((0, 1024), (1024, 512), (1536, 512))

    def _scatter(self, parts):
        in_w, out_r, glu_r = self.IN_W, self.OUT_R, self.GLU_R
        items, shapes, keys = [], [], []
        for oi, (name, l, arr, *rows) in enumerate(parts):
            if name == "w_in":
                r0, nr = rows[0] if rows else (0, D_MODEL)
                cut = functools.partial(lambda ref, j, r0, nr: ref.at[pl.ds(r0, nr), pl.ds(j * in_w, in_w)], r0=r0, nr=nr)
                shard = (nr, in_w)
            elif name == "w_out":
                cut, shard = (lambda ref, j: ref.at[pl.ds(j * out_r, out_r), :]), (out_r, D_MODEL)
            else:
                cut, shard = (lambda ref, j: ref.at[pl.ds(j * glu_r, glu_r), :]), (glu_r, BR)
            items.append((arr, oi, cut, lambda ref, j: ref.at[j]))
            shapes.append(SDS((N_CHIPS,) + shard, WIRE_DTYPE))
            keys.append((name, l) + ((rows[0][0],) if rows else ()))

        def done(got):
            self.recv.update(zip(keys, got))

        return _Exchange(items, shapes), done

    def received(self, name):
        return [self.recv[k] for k in sorted(k for k in self.recv if k[0] == name)]

    def bwd_hooks(self, l, grads):
        if l != 0:
            return None
        g1 = grads[1]
        w_in_part = lambda k: (lambda g: self._scatter([("w_in", 1, g1["w_in"], self.W_IN_ROWS[k])]))
        return {"dw_out": lambda g: self._scatter([("w_out", 1, g1["w_out"]), ("s5_w_glu", 1, g1["s5_w_glu"])]),
                "attn_bwd_d16": w_in_part(0), "attn_bwd_d4": w_in_part(1), "attn_bwd_d1": w_in_part(2),
                "dw_in": lambda g: self._scatter([("w_out", 0, g["w_out"]), ("s5_w_glu", 0, g["s5_w_glu"])]),
                "dh": lambda g: self._scatter([("w_in", 0, g["w_in"])])}


def kernel(x, c, rel_bias, w_ada, b_ada, w_in, conv_a, conv_c, conv_c_b, lru_wa, lru_ba, lru_wx, lru_bx, lru_lambda, s5_lam_re, s5_lam_im, s5_log_dt, s5_b_re, s5_b_im, s5_c_re, s5_c_im, s5_d, s5_w_glu, s5_b_glu, w_out, ln_g, ln_b, loss_target, m_rel_bias, m_w_ada, m_b_ada, m_w_in, m_conv_a, m_conv_c, m_conv_c_b, m_lru_wa, m_lru_ba, m_lru_wx, m_lru_bx, m_lru_lambda, m_s5_lam_re, m_s5_lam_im, m_s5_log_dt, m_s5_b_re, m_s5_b_im, m_s5_c_re, m_s5_c_im, m_s5_d, m_s5_w_glu, m_s5_b_glu, m_w_out, m_ln_g, m_ln_b, v_rel_bias, v_w_ada, v_b_ada, v_w_in, v_conv_a, v_conv_c, v_conv_c_b, v_lru_wa, v_lru_ba, v_lru_wx, v_lru_bx, v_lru_lambda, v_s5_lam_re, v_s5_lam_im, v_s5_log_dt, v_s5_b_re, v_s5_b_im, v_s5_c_re, v_s5_c_im, v_s5_d, v_s5_w_glu, v_s5_b_glu, v_w_out, v_ln_g, v_ln_b):
    args = dict(locals())
    names = ("rel_bias", "w_ada", "b_ada", "w_in", "conv_a", "conv_c", "conv_c_b", "lru_wa", "lru_ba", "lru_wx",
             "lru_bx", "lru_lambda", "s5_lam_re", "s5_lam_im", "s5_log_dt", "s5_b_re", "s5_b_im", "s5_c_re", "s5_c_im",
             "s5_d", "s5_w_glu", "s5_b_glu", "w_out", "ln_g", "ln_b")
    w = {n: args[n] for n in names}
    mom = {n: args["m_" + n] for n in names}
    var = {n: args["v_" + n] for n in names}
    chip = 2 * lax.axis_index("x") + lax.axis_index("y")
    me = 2 * chip + lax.axis_index("c")
    ada_w = 3 * D_MODEL // N_CHIPS
    in_w = N_IN // N_CHIPS
    out_r = D_MODEL // N_CHIPS
    glu_r = BR // N_CHIPS
    conv_w = BR // N_CHIPS

    comm = _Comm(w["w_in"].astype(WIRE_DTYPE), w["w_out"].astype(WIRE_DTYPE), w["s5_w_glu"].astype(WIRE_DTYPE))

    taps = jnp.concatenate([w["conv_a"].reshape(DEPTH * 3, conv_w), w["conv_c"].reshape(DEPTH * 4, conv_w)])
    first = jnp.concatenate([c, jnp.pad(taps, ((0, 1), (0, D_MODEL - conv_w)))])
    got = _allgather8(first, "gather_c_taps").reshape(N_CHIPS, 2, 16, D_MODEL)
    c_all = got[:, :, 0].reshape(N_DEV, D_MODEL)
    taps_all = jnp.transpose(got[:, 0, 1:1 + DEPTH * 7, 0:conv_w], (1, 0, 2)).reshape(DEPTH * 7, BR)
    conv_a_f = taps_all[0:DEPTH * 3].reshape(DEPTH, 3, BR)
    conv_c_f = taps_all[DEPTH * 3:].reshape(DEPTH, 4, BR)

    cond_all = _silu_rows(c_all)
    ada_part = jnp.stack([_mm(cond_all, w["w_ada"][l], name="ada_fwd", tk=D_MODEL, tn=512,
                              bias=_take_cols(w["b_ada"][l][None], chip, ada_w)) for l in range(DEPTH)])
    ada_all = _allgather8(ada_part.reshape(DEPTH * N_DEV, ada_w), "gather_ada")
    ada_all = ada_all.reshape(N_CHIPS, 2, DEPTH, N_DEV, ada_w)[:, 0]
    ada_rows = lax.dynamic_index_in_dim(ada_all, me, axis=2, keepdims=False)
    ada_rows = jnp.transpose(ada_rows, (1, 0, 2)).reshape(DEPTH, 3 * D_MODEL)

    p = dict(w)
    p["conv_a"], p["conv_c"] = conv_a_f, conv_c_f
    loss, dx, _, small = _local_step(x[0], loss_target[0], ada_rows, None, None, None, p, comm)

    sums = [_sum_leading(comm.received(name), 256, "sum_chips") for name in ("w_in", "w_out", "s5_w_glu")]
    others = _sibling_swap(sums, "swap_cores")
    small_names = SMALL + ("ada",)
    small["loss"] = loss
    order = small_names + ("loss",)
    shapes = [small[n].shape for n in order]
    gather_small = _AllGather8(_pack([small[n] for n in order]))
    out = {}
    for name, mine, other in zip(("w_in", "w_out", "s5_w_glu"), sums, others):
        shp = w[name].shape
        flat = lambda t: t.reshape(-1, shp[-1])
        res = _adamw(flat(w[name]), [mine, other], flat(mom[name]), flat(var[name]), 128, "adamw_big",
                     carry=gather_small if name == "w_in" else None)
        if name == "w_in":
            gathered = res[4]
        out[name] = [t.reshape(shp) for t in res[:4]]
    gathered = gathered.reshape(N_DEV, -1, 128)
    total = dict(zip(order, _unpack(_sum_leading([gathered], PACK_ROWS, "sum_devices"), shapes)))
    d_ada_all = _unpack(gathered, shapes)[order.index("ada")]
    g_small = {n: total[n] for n in SMALL}
    g_small["conv_a"] = _take_cols(total["conv_a"], chip, conv_w)
    g_small["conv_c"] = _take_cols(total["conv_c"], chip, conv_w)
    g_small["b_ada"] = total["ada"]
    g_w_ada = jnp.stack([_mm(cond_all, _take_cols(d_ada_all[:, l], chip, ada_w), name="dw_ada", ta=True, tn=ada_w)
                         for l in range(DEPTH)])
    upd_names = SMALL + ("b_ada",)
    upd_shapes = [w[n].shape for n in upd_names]
    res = _adamw(_pack([w[n] for n in upd_names]), [_pack([g_small[n] for n in upd_names])],
                 _pack([mom[n] for n in upd_names]), _pack([var[n] for n in upd_names]), PACK_ROWS, "adamw_small")
    for k, t in enumerate(res):
        for n, val in zip(upd_names, _unpack(t, upd_shapes)):
            out.setdefault(n, [None] * 4)[k] = val
    shp = w["w_ada"].shape
    flat = lambda t: t.reshape(-1, shp[-1])
    out["w_ada"] = [t.reshape(shp) for t in _adamw(flat(w["w_ada"]), [flat(g_w_ada)], flat(mom["w_ada"]),
                                                  flat(var["w_ada"]), 128, "adamw_ada")]
    return (total["loss"].reshape(()), dx[None]) + tuple(out[n][k] for k in range(4) for n in names)
```

```python
import functools
import math

import jax
import jax.numpy as jnp
from jax import lax
from jax.experimental import pallas as pl
from jax.experimental.pallas import tpu as pltpu

F32 = jnp.float32
MXU_DTYPE = jnp.bfloat16
WIRE_DTYPE = jnp.bfloat16
SDS = jax.ShapeDtypeStruct
MESH = pl.DeviceIdType.MESH
ANY = pl.BlockSpec(memory_space=pl.ANY)
VMEM_LIMIT = 48 * 1024 * 1024

D_MODEL = 2048
DEPTH = 2
BR = 512
ATT_HEADS = 8
HEAD_DIM = 64
DILATIONS = ((128, 1), (512, 4), (2048, 16))
BLK = 128
REL_BUCKETS = 32
REL_MAX_DIST = 2048
LRU_HEADS = 8
LRU_C = 8.0
S5_CH = 16
S5_GROUPS = 32
S5_STATE = 64
S5_N = S5_GROUPS * S5_STATE
N_IN = 12 * BR
ALPHA = (2 * DEPTH) ** 0.25
LN_EPS = 1e-5
NEG = -1e30
ADAM_LR, ADAM_B1, ADAM_B2, ADAM_EPS, ADAM_WD, ADAM_STEP = 0.001, 0.9, 0.999, 1e-08, 0.01, 10
CB_AB, CB_AC, CB_AX, CB_AG, CB_Q, CB_K, CB_V, CB_BG, CB_CX, CB_CG, CB_DU, CB_DG = range(12)
N_CHIPS = 4
N_DEV = 8


def _params(n_axes=0):
    kw = {"dimension_semantics": ("arbitrary",) * n_axes} if n_axes else {}
    return pltpu.CompilerParams(vmem_limit_bytes=VMEM_LIMIT, **kw)


def _rows(tb, w, cb=0):
    return pl.BlockSpec((tb, w), lambda i: (i, cb))


def _prev8(tb, w, cb=0):
    return pl.BlockSpec((8, w), lambda i: (jnp.maximum(i * (tb // 8) - 1, 0), cb))


def _next8(tb, w, n_rows, cb=0):
    return pl.BlockSpec((8, w), lambda i: (jnp.minimum((i + 1) * (tb // 8), n_rows // 8 - 1), cb))


def _const(shape):
    return pl.BlockSpec(shape, lambda *_: (0,) * len(shape))


def _silu(x):
    return x * jax.nn.sigmoid(x)


def _dsilu(x):
    s = jax.nn.sigmoid(x)
    return s * (1.0 + x * (1.0 - s))


def _shift_down(cur, prev8, j):
    rolled = pltpu.roll(cur, j, 0)
    row = lax.broadcasted_iota(jnp.int32, (8, cur.shape[1]), 0)
    first = jnp.where(row < j, pltpu.roll(prev8, j, 0), rolled[0:8])
    return jnp.concatenate([first, rolled[8:]], axis=0)


def _shift_up(cur, next8, j):
    t = cur.shape[0]
    rolled = pltpu.roll(cur, t - j, 0)
    row = lax.broadcasted_iota(jnp.int32, (8, cur.shape[1]), 0)
    last = jnp.where(row >= 8 - j, pltpu.roll(next8, 8 - j, 0), rolled[t - 8:t])
    return jnp.concatenate([rolled[:t - 8], last], axis=0)


def _colsum(x):
    return jnp.sum(x, axis=0, keepdims=True)


def _init_acc(*refs):
    @pl.when(pl.program_id(0) == 0)
    def _():
        for r in refs:
            r[...] = jnp.zeros_like(r)


def _call(body, *, name, out_shape, grid, in_specs, out_specs, scratch_shapes, args, carry=None):
    out_shape, out_specs, in_specs = tuple(out_shape), tuple(out_specs), list(in_specs)
    scratch_shapes = list(scratch_shapes)
    if carry is None:
        return pl.pallas_call(body, name=name, out_shape=out_shape, grid=grid, in_specs=in_specs, out_specs=out_specs,
                              scratch_shapes=scratch_shapes, compiler_params=_params(len(grid)))(*args)
    n_in, n_out, n_scr = len(in_specs), len(out_shape), len(scratch_shapes)

    def wrapped(*refs):
        ins, refs = refs[:n_in], refs[n_in:]
        x_ins, refs = refs[:carry.n_in], refs[carry.n_in:]
        outs, refs = refs[:n_out], refs[n_out:]
        x_outs, refs = refs[:carry.n_out], refs[carry.n_out:]
        scr, x_sems = refs[:n_scr], refs[n_scr:]
        at = [pl.program_id(d) for d in range(len(grid))]
        first = functools.reduce(lambda p, q: p & q, [i == 0 for i in at])
        last = functools.reduce(lambda p, q: p & q, [i == g - 1 for i, g in zip(at, grid)])
        pl.when(first)(lambda: carry.start(x_ins, x_outs, x_sems))
        body(*ins, *outs, *scr)
        pl.when(last)(lambda: carry.wait(x_ins, x_outs, x_sems))

    return pl.pallas_call(
        wrapped, name=name, out_shape=out_shape + carry.out_shapes, grid=grid, in_specs=in_specs + [ANY] * carry.n_in,
        out_specs=out_specs + (ANY,) * carry.n_out, scratch_shapes=scratch_shapes + carry.scratch,
        compiler_params=_params(len(grid)))(*args, *carry.arrays)


def _mm(a, b, *, name, ta=False, tb=False, out_dtype=F32, tm=512, tn=512, tk=512, bias=None, carry=None):
    m, k = (a.shape[1], a.shape[0]) if ta else a.shape
    n = b.shape[0] if tb else b.shape[1]
    assert k == (b.shape[1] if tb else b.shape[0]), (name, a.shape, b.shape)
    tm, tn, tk = min(tm, m), min(tn, n), min(tk, k)
    nk = k // tk
    assert m % tm == 0 and n % tn == 0 and k % tk == 0, (name, m, n, k)

    def body(*refs):
        if bias is None:
            a_ref, b_ref, o_ref, acc = refs
        else:
            a_ref, b_ref, bias_ref, o_ref, acc = refs
        kk = pl.program_id(2)

        @pl.when(kk == 0)
        def _():
            acc[...] = jnp.zeros_like(acc)

        dims = (((0 if ta else 1,), (1 if tb else 0,)), ((), ()))
        acc[...] += lax.dot_general(a_ref[...].astype(MXU_DTYPE), b_ref[...].astype(MXU_DTYPE), dims,
                                    preferred_element_type=F32)

        @pl.when(kk == nk - 1)
        def _():
            r = acc[...]
            if bias is not None:
                r = r + bias_ref[...]
            o_ref[...] = r.astype(out_dtype)

    a_spec = (pl.BlockSpec((tk, tm), lambda i, j, kk: (kk, i)) if ta
              else pl.BlockSpec((tm, tk), lambda i, j, kk: (i, kk)))
    b_spec = (pl.BlockSpec((tn, tk), lambda i, j, kk: (j, kk)) if tb
              else pl.BlockSpec((tk, tn), lambda i, j, kk: (kk, j)))
    in_specs, args = [a_spec, b_spec], [a, b]
    if bias is not None:
        in_specs.append(pl.BlockSpec((1, tn), lambda i, j, kk: (0, j)))
        args.append(bias)
    out = _call(body, name=name, out_shape=[SDS((m, n), out_dtype)], grid=(m // tm, n // tn, nk), in_specs=in_specs,
                out_specs=[pl.BlockSpec((tm, tn), lambda i, j, kk: (i, j))],
                scratch_shapes=[pltpu.VMEM((tm, tn), F32)], args=args, carry=carry)
    return out[0] if carry is None else out


def _silu_rows(c_all):
    def body(c_ref, o_ref):
        o_ref[...] = _silu(c_ref[...])
    return pl.pallas_call(body, name="cond_silu", out_shape=SDS(c_all.shape, F32))(c_all)


def _modulate(x, scale, shift, tb):
    s, d = x.shape

    def body(x_ref, sc_ref, sh_ref, o_ref):
        o_ref[...] = (x_ref[...] * (1.0 + sc_ref[...]) + sh_ref[...]).astype(MXU_DTYPE)

    return pl.pallas_call(body, name="modulate", out_shape=SDS((s, d), MXU_DTYPE), grid=(s // tb,),
                          in_specs=[_rows(tb, d), _const((1, d)), _const((1, d))], out_specs=_rows(tb, d),
                          compiler_params=_params(1))(x, scale, shift)


def _out_ln(ycat, w_out, x, gate, ln_g, ln_b, tb):
    s, d = x.shape

    def body(yc_ref, w_ref, x_ref, gt_ref, g_ref, b_ref, xn_ref, xh_ref, y_ref, rs_ref):
        y = jnp.dot(yc_ref[...], w_ref[...], preferred_element_type=F32)
        res = ALPHA * x_ref[...] + (1.0 + gt_ref[...]) * y
        mu = jnp.mean(res, axis=-1, keepdims=True)
        cen = res - mu
        var = jnp.mean(cen * cen, axis=-1, keepdims=True)
        rstd = lax.rsqrt(var + LN_EPS)
        xhat = cen * rstd
        xn_ref[...] = xhat * g_ref[...] + b_ref[...]
        xh_ref[...] = xhat
        y_ref[...] = y
        rs_ref[...] = rstd

    big = SDS((s, d), F32)
    return pl.pallas_call(
        body, name="out_proj_ln", out_shape=(big, big, big, SDS((s, 1), F32)), grid=(s // tb,),
        in_specs=[_rows(tb, d), pl.BlockSpec((d, d), lambda i: (0, 0), pipeline_mode=pl.Buffered(1)), _rows(tb, d),
                  _const((1, d)), _const((1, d)), _const((1, d))],
        out_specs=(_rows(tb, d), _rows(tb, d), _rows(tb, d), _rows(tb, 1)), compiler_params=_params(1),
    )(ycat, w_out, x, gate, ln_g, ln_b)


def _ln_bwd(dxn, xhat, y, rstd, ln_g, gate, tb):
    s, d = dxn.shape

    def body(dxn_ref, xh_ref, y_ref, rs_ref, g_ref, gt_ref, dy_ref, dxa_ref, dg_ref, db_ref, dgt_ref):
        _init_acc(dg_ref, db_ref, dgt_ref)
        dxn_t, xh = dxn_ref[...], xh_ref[...]
        dxh = dxn_t * g_ref[...]
        dres = rs_ref[...] * (dxh - jnp.mean(dxh, axis=-1, keepdims=True)
                              - xh * jnp.mean(dxh * xh, axis=-1, keepdims=True))
        dy_ref[...] = ((1.0 + gt_ref[...]) * dres).astype(MXU_DTYPE)
        dxa_ref[...] = ALPHA * dres
        dg_ref[...] += _colsum(dxn_t * xh)
        db_ref[...] += _colsum(dxn_t)
        dgt_ref[...] += _colsum(dres * y_ref[...])

    vec = SDS((1, d), F32)
    return pl.pallas_call(
        body, name="ln_bwd", out_shape=(SDS((s, d), MXU_DTYPE), SDS((s, d), F32), vec, vec, vec), grid=(s // tb,),
        in_specs=[_rows(tb, d), _rows(tb, d), _rows(tb, d), _rows(tb, 1), _const((1, d)), _const((1, d))],
        out_specs=(_rows(tb, d), _rows(tb, d), _const((1, d)), _const((1, d)), _const((1, d))),
        compiler_params=_params(1))(dxn, xhat, y, rstd, ln_g, gate)


def _dh_mod_bwd(dproj, w_in, dxa, x, scale, carry=None):
    s, d = dxa.shape
    k = dproj.shape[1]
    tm, tn, tk = min(1024, s), 1024, 1536
    nk = k // tk
    assert s % tm == 0 and d % tn == 0 and k % tk == 0

    def body(a_ref, b_ref, dxa_ref, x_ref, sc_ref, dx_ref, dsh_ref, dsc_ref, acc):
        i, kk = pl.program_id(1), pl.program_id(2)

        @pl.when(kk == 0)
        def _():
            acc[...] = jnp.zeros_like(acc)

        @pl.when((kk == 0) & (i == 0))
        def _():
            dsh_ref[...] = jnp.zeros_like(dsh_ref)
            dsc_ref[...] = jnp.zeros_like(dsc_ref)

        acc[...] += lax.dot_general(a_ref[...], b_ref[...], (((1,), (1,)), ((), ())), preferred_element_type=F32)

        @pl.when(kk == nk - 1)
        def _():
            dh_t = acc[...]
            dx_ref[...] = dxa_ref[...] + dh_t * (1.0 + sc_ref[...])
            dsh_ref[...] += _colsum(dh_t)
            dsc_ref[...] += _colsum(dh_t * x_ref[...])

    tile = pl.BlockSpec((tm, tn), lambda j, i, kk: (i, j))
    vec = pl.BlockSpec((1, tn), lambda j, i, kk: (0, j))
    return _call(
        body, name="dh", out_shape=(SDS((s, d), F32), SDS((1, d), F32), SDS((1, d), F32)),
        grid=(d // tn, s // tm, nk),
        in_specs=[pl.BlockSpec((tm, tk), lambda j, i, kk: (i, kk)), pl.BlockSpec((tn, tk), lambda j, i, kk: (j, kk)),
                  tile, tile, vec],
        out_specs=(tile, vec, vec), scratch_shapes=[pltpu.VMEM((tm, tn), F32)],
        args=(dproj, w_in, dxa, x, scale), carry=carry)


def _loss_head(y, target, tb):
    s, d = y.shape

    def body(y_ref, t_ref, l_ref, dy_ref):
        _init_acc(l_ref)
        err = y_ref[...] - t_ref[...]
        l_ref[...] += (0.5 / d) * jnp.sum(err * err, keepdims=True)
        dy_ref[...] = err * (1.0 / d)

    return pl.pallas_call(body, name="loss_head", out_shape=(SDS((1, 1), F32), SDS((s, d), F32)), grid=(s // tb,),
                          in_specs=[_rows(tb, d), _rows(tb, d)], out_specs=(_const((1, 1)), _rows(tb, d)),
                          compiler_params=_params(1))(y, target)


def _conv_taps(u, up, w_ref, width):
    out = w_ref[width - 1:width, :] * u
    for j in range(width - 2, -1, -1):
        out = out + w_ref[j:j + 1, :] * _shift_down(u, up, width - 1 - j)
    return out


def _conv_taps_t(g, gn, w_ref, width):
    out = w_ref[width - 1:width, :] * g
    for j in range(width - 2, -1, -1):
        out = out + w_ref[j:j + 1, :] * _shift_up(g, gn, width - 1 - j)
    return out


def _conv_wgrad(dw_ref, g, u, up, width):
    dw_ref[width - 1:width, :] += _colsum(g * u)
    for j in range(width - 1):
        dw_ref[j:j + 1, :] += _colsum(g * _shift_down(u, up, width - 1 - j))


def _branch_a_fwd(proj, conv_w, tb):
    s = proj.shape[0]

    def body(ab, ac, ax, ag, acp, axp, w_ref, o_ref):
        has_prev = (pl.program_id(0) > 0).astype(F32)
        u = ac[...] * ax[...]
        up = acp[...] * axp[...] * has_prev
        o_ref[...] = (ab[...] * _conv_taps(u, up, w_ref, 3) * _silu(ag[...])).astype(MXU_DTYPE)

    return pl.pallas_call(
        body, name="branch_a_fwd", out_shape=SDS((s, BR), MXU_DTYPE), grid=(s // tb,),
        in_specs=[_rows(tb, BR, CB_AB), _rows(tb, BR, CB_AC), _rows(tb, BR, CB_AX), _rows(tb, BR, CB_AG),
                  _prev8(tb, BR, CB_AC), _prev8(tb, BR, CB_AX), _const((8, BR))],
        out_specs=_rows(tb, BR), compiler_params=_params(1))(proj, proj, proj, proj, proj, proj, conv_w)


def _branch_a_bwd(dycat, proj, conv_w, tb):
    s = proj.shape[0]

    def body(dy, dyn, ab, abn, ag, agn, ac, acp, ax, axp, w_ref, o_ref, dw_ref):
        _init_acc(dw_ref)
        i = pl.program_id(0)
        has_prev = (i > 0).astype(F32)
        has_next = (i < pl.num_programs(0) - 1).astype(F32)
        u = ac[...] * ax[...]
        up = acp[...] * axp[...] * has_prev
        v = _conv_taps(u, up, w_ref, 3)
        sg = _silu(ag[...])
        dv = dy[...] * ab[...] * sg
        dvn = dyn[...] * abn[...] * _silu(agn[...]) * has_next
        du = _conv_taps_t(dv, dvn, w_ref, 3)
        o_ref[:, 0:BR] = (dy[...] * v * sg).astype(MXU_DTYPE)
        o_ref[:, BR:2 * BR] = (du * ax[...]).astype(MXU_DTYPE)
        o_ref[:, 2 * BR:3 * BR] = (du * ac[...]).astype(MXU_DTYPE)
        o_ref[:, 3 * BR:4 * BR] = (dy[...] * ab[...] * v * _dsilu(ag[...])).astype(MXU_DTYPE)
        _conv_wgrad(dw_ref, dv, u, up, 3)

    return pl.pallas_call(
        body, name="branch_a_bwd", out_shape=(SDS((s, 4 * BR), MXU_DTYPE), SDS((8, BR), F32)), grid=(s // tb,),
        in_specs=[_rows(tb, BR, 0), _next8(tb, BR, s, 0),
                  _rows(tb, BR, CB_AB), _next8(tb, BR, s, CB_AB), _rows(tb, BR, CB_AG), _next8(tb, BR, s, CB_AG),
                  _rows(tb, BR, CB_AC), _prev8(tb, BR, CB_AC), _rows(tb, BR, CB_AX), _prev8(tb, BR, CB_AX),
                  _const((8, BR))],
        out_specs=(_rows(tb, 4 * BR), _const((8, BR))), compiler_params=_params(1),
    )(dycat, dycat, proj, proj, proj, proj, proj, proj, proj, proj, conv_w)


def _t5_bucket(dist):
    max_exact = REL_BUCKETS // 2
    nf = jnp.maximum(dist, 1).astype(F32)
    large = max_exact + (jnp.log(nf / max_exact) / math.log(REL_MAX_DIST / max_exact)
                         * (REL_BUCKETS - max_exact)).astype(jnp.int32)
    large = jnp.minimum(large, REL_BUCKETS - 1)
    return jnp.where(dist < max_exact, dist, large)


def _bucket_maps():
    maps = []
    i = jnp.arange(BLK)[:, None]
    j = jnp.arange(2 * BLK)[None, :]
    delta = i + BLK - j
    for window, dil in DILATIONS:
        span = window // dil
        bucket = _t5_bucket(jnp.clip(delta, 0, span) * dil)
        maps.append(jnp.where((delta >= 0) & (delta <= span), bucket, -1))
    return jnp.stack(maps).astype(jnp.int32)


def _bias_tables(rel_bias, buckets):
    n_pat = len(DILATIONS)

    def body(rb_ref, bk_ref, o_ref):
        for g in range(n_pat):
            bk = bk_ref[g]
            for h in range(ATT_HEADS):
                def per_bucket(b, acc):
                    return jnp.where(bk == b, rb_ref[b, h], acc)
                o_ref[g, h] = lax.fori_loop(0, REL_BUCKETS, per_bucket, jnp.full((BLK, 2 * BLK), NEG, F32))

    return pl.pallas_call(
        body, name="bias_tables", out_shape=SDS((n_pat, ATT_HEADS, BLK, 2 * BLK), F32),
        in_specs=[pl.BlockSpec(memory_space=pltpu.SMEM), pl.BlockSpec(memory_space=pltpu.VMEM)],
        compiler_params=_params())(rel_bias, buckets)


def _head_masks():
    lane = lax.broadcasted_iota(jnp.int32, (1, 2 * HEAD_DIM), 1)
    return [(lane < HEAD_DIM).astype(F32), (lane >= HEAD_DIM).astype(F32)]


def _strided(base, size, dil):
    return pl.ds(base, size, stride=dil) if dil > 1 else pl.ds(pl.multiple_of(base, BLK), size)


def _attn_groups(s, dil):
    return max(1, min(1024, s) // (dil * BLK)) if dil == 1 else max(1, min(2048, s) // (dil * BLK))


def _attn_fwd(proj, bias, dil):
    s = proj.shape[0]
    grp = _attn_groups(s, dil)
    u1 = dil * BLK
    unit = grp * u1
    nb = s // unit
    w = 2 * HEAD_DIM
    q0, k0, v0 = (cb * (BR // w) for cb in (CB_Q, CB_K, CB_V))

    def body(q_ref, kc_ref, kp_ref, vc_ref, vp_ref, bias_ref, o_ref, lse_ref, kbuf, vbuf):
        n = pl.program_id(1)
        col = lax.broadcasted_iota(jnp.int32, (1, 2 * BLK), 1)
        masks = _head_masks()
        kbuf[0:u1, :] = kp_ref[...]
        kbuf[u1:, :] = kc_ref[...]
        vbuf[0:u1, :] = vp_ref[...]
        vbuf[u1:, :] = vc_ref[...]

        def per_r(t, carry):
            j = t // dil
            base = j * u1 + t % dil
            rows = _strided(base, BLK, dil)
            no_prev = jnp.where((n == 0) & (j == 0) & (col < BLK), NEG, 0.0)
            q = q_ref[rows, :] * (HEAD_DIM ** -0.5)
            k = kbuf[_strided(base, 2 * BLK, dil), :].astype(MXU_DTYPE)
            v = vbuf[_strided(base, 2 * BLK, dil), :].astype(MXU_DTYPE)
            q2 = jnp.concatenate([q * masks[0], q * masks[1]], axis=0).astype(MXU_DTYPE)
            sc = lax.dot_general(q2, k, (((1,), (1,)), ((), ())), preferred_element_type=F32)
            sc = sc + jnp.concatenate([bias_ref[0], bias_ref[1]], axis=0) + no_prev
            mx = jnp.max(sc, axis=-1, keepdims=True)
            p = jnp.exp(sc - mx)
            l = jnp.sum(p, axis=-1, keepdims=True)
            o2 = jnp.dot((p / l).astype(MXU_DTYPE), v, preferred_element_type=F32)
            lse2 = mx + jnp.log(l)
            o_ref[rows, :] = o2[0:BLK] * masks[0] + o2[BLK:2 * BLK] * masks[1]
            lse_ref[rows, :] = lse2[0:BLK] * masks[0] + lse2[BLK:2 * BLK] * masks[1]
            return carry

        lax.fori_loop(0, grp * dil, per_r, 0, unroll=8)

    cur = lambda c0: pl.BlockSpec((unit, w), lambda hp, n: (n, c0 + hp))
    prev = lambda c0: pl.BlockSpec((u1, w), lambda hp, n: (jnp.maximum(n * grp - 1, 0), c0 + hp))
    out = pl.BlockSpec((unit, w), lambda hp, n: (n, hp))
    return pl.pallas_call(
        body, name=f"attn_fwd_d{dil}", out_shape=(SDS((s, BR), F32), SDS((s, BR), F32)), grid=(BR // w, nb),
        in_specs=[cur(q0), cur(k0), prev(k0), cur(v0), prev(v0),
                  pl.BlockSpec((2, BLK, 2 * BLK), lambda hp, n: (hp, 0, 0))],
        out_specs=(out, out),
        scratch_shapes=[pltpu.VMEM((unit + u1, w), F32), pltpu.VMEM((unit + u1, w), F32)],
        compiler_params=_params(2))(proj, proj, proj, proj, proj, bias)


def _softmax3(l0, l1, l2):
    mx = jnp.maximum(jnp.maximum(l0, l1), l2)
    e0, e1, e2 = jnp.exp(l0 - mx), jnp.exp(l1 - mx), jnp.exp(l2 - mx)
    inv = 1.0 / (e0 + e1 + e2)
    return e0 * inv, e1 * inv, e2 * inv


def _attn_combine(os_, lses, proj, tb):
    s = proj.shape[0]

    def body(o0, o1, o2, l0, l1, l2, bg, y_ref):
        w0, w1, w2 = _softmax3(l0[...], l1[...], l2[...])
        attn = w0 * o0[...] + w1 * o1[...] + w2 * o2[...]
        y_ref[...] = (attn * _silu(bg[...])).astype(MXU_DTYPE)

    return pl.pallas_call(
        body, name="attn_combine", out_shape=SDS((s, BR), MXU_DTYPE), grid=(s // tb,),
        in_specs=[_rows(tb, BR)] * 6 + [_rows(tb, BR, CB_BG)], out_specs=_rows(tb, BR),
        compiler_params=_params(1))(*os_, *lses, proj)


def _attn_bwd_pre(dycat, os_, lses, proj, head_ones, tb):
    s = proj.shape[0]

    def body(dy, o0, o1, o2, l0, l1, l2, bg, e_ref, dbg_ref, do0, do1, do2, dm0, dm1, dm2):
        w0, w1, w2 = _softmax3(l0[...], l1[...], l2[...])
        attn = w0 * o0[...] + w1 * o1[...] + w2 * o2[...]
        dattn = dy[...] * _silu(bg[...])
        dbg_ref[...] = (dy[...] * attn * _dsilu(bg[...])).astype(MXU_DTYPE)
        prod = dattn * attn
        hi = prod.astype(MXU_DTYPE)
        lo = (prod - hi.astype(F32)).astype(MXU_DTYPE)
        tot = (jnp.dot(hi, e_ref[...], preferred_element_type=F32)
               + jnp.dot(lo, e_ref[...], preferred_element_type=F32))
        for wg, do_ref, dm_ref in ((w0, do0, dm0), (w1, do1, dm1), (w2, do2, dm2)):
            do_ref[...] = wg * dattn
            dm_ref[...] = wg * tot

    big = SDS((s, BR), F32)
    return pl.pallas_call(
        body, name="attn_bwd_pre", out_shape=(SDS((s, BR), MXU_DTYPE),) + (big,) * 6, grid=(s // tb,),
        in_specs=[_rows(tb, BR, 1)] + [_rows(tb, BR)] * 6 + [_rows(tb, BR, CB_BG), _const((BR, BR))],
        out_specs=(_rows(tb, BR),) * 7, compiler_params=_params(1))(dycat, *os_, *lses, proj, head_ones)


def _attn_bwd(proj, do, lse, dm, bias, dil, carry=None):
    s = proj.shape[0]
    grp = _attn_groups(s, dil)
    u1 = dil * BLK
    unit = grp * u1
    nb = s // unit
    w = 2 * HEAD_DIM
    q0, k0, v0 = (cb * (BR // w) for cb in (CB_Q, CB_K, CB_V))

    def body(q_ref, kc_ref, kp_ref, vc_ref, vp_ref, do_ref, lse_ref, dm_ref, bias_ref,
             dq_ref, dk_ref, dv_ref, dbias_ref, kbuf, vbuf, stage_k, stage_v):
        n = pl.program_id(1)
        col = lax.broadcasted_iota(jnp.int32, (1, 2 * BLK), 1)
        masks = _head_masks()

        @pl.when(n == 0)
        def _():
            dbias_ref[...] = jnp.zeros_like(dbias_ref)
            stage_k[...] = jnp.zeros_like(stage_k)
            stage_v[...] = jnp.zeros_like(stage_v)

        for out_ref, stage in ((dk_ref, stage_k), (dv_ref, stage_v)):
            if grp > 1:
                out_ref[0:unit - u1, :] = stage[u1:unit, :]
            stage[0:u1, :] = stage[unit:unit + u1, :]

        @pl.when(n < nb)
        def _():
            kbuf[0:u1, :] = kp_ref[...]
            kbuf[u1:, :] = kc_ref[...]
            vbuf[0:u1, :] = vp_ref[...]
            vbuf[u1:, :] = vc_ref[...]

            def per_r(t, carry):
                j = t // dil
                base = j * u1 + t % dil
                rows = _strided(base, BLK, dil)
                rows_hi = _strided(base + u1, BLK, dil)
                no_prev = jnp.where((n == 0) & (j == 0) & (col < BLK), NEG, 0.0)
                q = q_ref[rows, :] * (HEAD_DIM ** -0.5)
                k = kbuf[_strided(base, 2 * BLK, dil), :].astype(MXU_DTYPE)
                v = vbuf[_strided(base, 2 * BLK, dil), :].astype(MXU_DTYPE)
                do_t, lse_t, dm_t = do_ref[rows, :], lse_ref[rows, :], dm_ref[rows, :]
                stack = lambda t: jnp.concatenate([t * masks[0], t * masks[1]], axis=0).astype(MXU_DTYPE)
                per_head = lambda t: jnp.concatenate([t[:, 0:1], t[:, HEAD_DIM:HEAD_DIM + 1]], axis=0)
                q2, do2 = stack(q), stack(do_t)
                sc = lax.dot_general(q2, k, (((1,), (1,)), ((), ())), preferred_element_type=F32)
                p = jnp.exp(sc + jnp.concatenate([bias_ref[0], bias_ref[1]], axis=0) + no_prev - per_head(lse_t))
                dp = lax.dot_general(do2, v, (((1,), (1,)), ((), ())), preferred_element_type=F32)
                ds = p * (dp - per_head(dm_t))
                dbias_ref[0] += ds[0:BLK]
                dbias_ref[1] += ds[BLK:2 * BLK]
                dsb, pb = ds.astype(MXU_DTYPE), p.astype(MXU_DTYPE)
                dq2 = jnp.dot(dsb, k, preferred_element_type=F32)
                dk_acc = lax.dot_general(dsb, q2, (((0,), (0,)), ((), ())), preferred_element_type=F32)
                dv_acc = lax.dot_general(pb, do2, (((0,), (0,)), ((), ())), preferred_element_type=F32)
                dq_ref[rows, :] = (dq2[0:BLK] * masks[0] + dq2[BLK:2 * BLK] * masks[1]) * (HEAD_DIM ** -0.5)
                stage_k[rows, :] = stage_k[rows, :] + dk_acc[0:BLK]
                stage_v[rows, :] = stage_v[rows, :] + dv_acc[0:BLK]
                stage_k[rows_hi, :] = dk_acc[BLK:2 * BLK]
                stage_v[rows_hi, :] = dv_acc[BLK:2 * BLK]
                return carry

            lax.fori_loop(0, grp * dil, per_r, 0, unroll=8)

        dk_ref[unit - u1:unit, :] = stage_k[0:u1, :]
        dv_ref[unit - u1:unit, :] = stage_v[0:u1, :]

    qn = lambda n: jnp.minimum(n, nb - 1)
    cur = lambda c0: pl.BlockSpec((unit, w), lambda hp, n: (qn(n), c0 + hp))
    prev = lambda c0: pl.BlockSpec((u1, w), lambda hp, n: (jnp.maximum(qn(n) * grp - 1, 0), c0 + hp))
    row = pl.BlockSpec((unit, w), lambda hp, n: (qn(n), hp))
    late = pl.BlockSpec((unit, w), lambda hp, n: (jnp.maximum(n - 1, 0), hp))
    tab = pl.BlockSpec((2, BLK, 2 * BLK), lambda hp, n: (hp, 0, 0))
    big = SDS((s, BR), F32)
    return _call(
        body, name=f"attn_bwd_d{dil}", out_shape=(big, big, big, SDS((ATT_HEADS, BLK, 2 * BLK), F32)),
        grid=(BR // w, nb + 1),
        in_specs=[cur(q0), cur(k0), prev(k0), cur(v0), prev(v0), row, row, row, tab],
        out_specs=(row, late, late, tab),
        scratch_shapes=[pltpu.VMEM((unit + u1, w), F32)] * 4,
        args=(proj, proj, proj, proj, proj, do, lse, dm, bias), carry=carry)


def _rel_bias_grad(dbias, buckets):
    def body(db_ref, bk_ref, o_ref):
        row = lax.broadcasted_iota(jnp.int32, (REL_BUCKETS, 128), 0)
        lane = lax.broadcasted_iota(jnp.int32, (REL_BUCKETS, 128), 1)

        def per_bucket(b, acc):
            for g in range(len(DILATIONS)):
                hit = bk_ref[g] == b
                for h in range(ATT_HEADS):
                    both = db_ref[0, g, h] + db_ref[1, g, h]
                    val = jnp.sum(jnp.where(hit, both, 0.0), keepdims=True)
                    acc = acc + jnp.where((row == b) & (lane == h), val, 0.0)
            return acc

        o_ref[...] = lax.fori_loop(0, REL_BUCKETS, per_bucket, jnp.zeros((REL_BUCKETS, 128), F32))

    assert dbias.shape[0] == DEPTH == 2
    return pl.pallas_call(body, name="rel_bias_grad", out_shape=SDS((REL_BUCKETS, 128), F32),
                          compiler_params=_params())(dbias, buckets)


def _scan_real(a, b, *, reverse, tb, name):
    s, ch = a.shape
    nt = s // tb
    order = range(7, -1, -1) if reverse else range(8)

    def body(a_ref, b_ref, o_ref, carry):
        @pl.when(pl.program_id(0) == 0)
        def _():
            carry[...] = jnp.zeros_like(carry)

        def group(gi, h):
            r0 = pl.multiple_of((tb // 8 - 1 - gi if reverse else gi) * 8, 8)
            a8, b8 = a_ref[pl.ds(r0, 8), :], b_ref[pl.ds(r0, 8), :]
            rows = [None] * 8
            for k in order:
                if reverse:
                    rows[k] = b8[k:k + 1] + h
                    h = a8[k:k + 1] * rows[k]
                else:
                    h = a8[k:k + 1] * h + b8[k:k + 1]
                    rows[k] = h
            o_ref[pl.ds(r0, 8), :] = jnp.concatenate(rows, axis=0)
            return h

        carry[...] = lax.fori_loop(0, tb // 8, group, carry[...])

    spec = pl.BlockSpec((tb, ch), (lambda i: (nt - 1 - i, 0)) if reverse else (lambda i: (i, 0)))
    return pl.pallas_call(body, name=name, out_shape=SDS((s, ch), F32), grid=(nt,), in_specs=[spec, spec],
                          out_specs=spec, scratch_shapes=[pltpu.VMEM((1, ch), F32)],
                          compiler_params=_params(1))(a, b)


def _scan_tile(s):
    return min(512, s)


def _load_chunked(ref, t0, pt):
    ln = pt // 8
    return jnp.concatenate([ref[pl.ds(t0 + j, 8, stride=ln), :] for j in range(ln)], axis=0)


def _store_natural(ref, t0, pt, val):
    ln = pt // 8
    for j in range(ln):
        ref[pl.ds(t0 + j, 8, stride=ln), :] = val[j * 8:(j + 1) * 8]


def _scan_tile_in_place(a_ref, x_ref, carry, pw, *, reverse):
    ch2 = x_ref.shape[1]
    ch = ch2 // 2
    ln = x_ref.shape[0] // 8
    ar = a_ref[:, 0:ch]
    ai = -a_ref[:, ch:ch2] if reverse else a_ref[:, ch:ch2]

    def cmul(pr, pi, xr, xi):
        return pr * xr - pi * xi, pr * xi + pi * xr

    @pl.when(pl.program_id(0) == 0)
    def _():
        carry[...] = jnp.zeros_like(carry)

        def fill(j, p):
            pw[pl.ds(j, 1), 0:ch] = p[0]
            pw[pl.ds(j, 1), ch:ch2] = p[1]
            return cmul(ar, ai, *p)

        lax.fori_loop(0, ln, fill, (ar, ai))

    def rows_of(j):
        return pl.ds(pl.multiple_of((ln - 1 - j if reverse else j) * 8, 8), 8)

    def local(j, x):
        rows = rows_of(j)
        nr, ni = cmul(ar, ai, *x)
        xr, xi = nr + x_ref[rows, 0:ch], ni + x_ref[rows, ch:ch2]
        x_ref[rows, 0:ch] = xr
        x_ref[rows, ch:ch2] = xi
        return xr, xi

    zero = jnp.zeros((8, ch), F32)
    er, ei = lax.fori_loop(0, ln, local, (zero, zero), unroll=2)
    apr, api = pw[ln - 1:ln, 0:ch], pw[ln - 1:ln, ch:ch2]
    cr, ci = carry[:, 0:ch], carry[:, ch:ch2]
    into_r, into_i = [None] * 8, [None] * 8
    for c in (range(7, -1, -1) if reverse else range(8)):
        into_r[c], into_i[c] = cr, ci
        pr, pi = cmul(apr, api, cr, ci)
        cr, ci = er[c:c + 1] + pr, ei[c:c + 1] + pi
    carry[:, 0:ch] = cr
    carry[:, ch:ch2] = ci
    into_r, into_i = jnp.concatenate(into_r, axis=0), jnp.concatenate(into_i, axis=0)

    def fix(j, carry_):
        rows = rows_of(j)
        dr, di = cmul(pw[pl.ds(j, 1), 0:ch], pw[pl.ds(j, 1), ch:ch2], into_r, into_i)
        x_ref[rows, 0:ch] += dr
        x_ref[rows, ch:ch2] += di
        return carry_

    lax.fori_loop(0, ln, fix, 0, unroll=2)


def _neg_expm1(z):
    series = -z * (1.0 + z * (0.5 + z * (1.0 / 6 + z * (1.0 / 24 + z * (1.0 / 120)))))
    return jnp.where(z > -0.05, series, 1.0 - jnp.exp(z))


def _lru_gate(xc, pre_r, pre_i, lam):
    log_a = -LRU_C * jax.nn.sigmoid(pre_r) * jax.nn.softplus(-lam)
    return jnp.exp(log_a), jnp.sqrt(_neg_expm1(2.0 * log_a)) * jax.nn.sigmoid(pre_i) * xc


def _lru_gates_fwd(proj, conv_w, conv_b, w_cat, b_cat, lam, tb):
    s = proj.shape[0]

    def body(cx, cxp, w_ref, cb_ref, wc_ref, bc_ref, lam_ref, a_ref, b_ref):
        has_prev = (pl.program_id(0) > 0).astype(F32)
        xc = _conv_taps(cx[...], cxp[...] * has_prev, w_ref, 4) + cb_ref[...]
        pre = jnp.dot(xc.astype(MXU_DTYPE), wc_ref[...], preferred_element_type=F32) + bc_ref[...]
        a_ref[...], b_ref[...] = _lru_gate(xc, pre[:, 0:BR], pre[:, BR:2 * BR], lam_ref[...])

    big = SDS((s, BR), F32)
    return pl.pallas_call(
        body, name="lru_gates_fwd", out_shape=(big, big), grid=(s // tb,),
        in_specs=[_rows(tb, BR, CB_CX), _prev8(tb, BR, CB_CX), _const((8, BR)), _const((1, BR)),
                  _const((BR, 2 * BR)), _const((1, 2 * BR)), _const((1, BR))],
        out_specs=(_rows(tb, BR), _rows(tb, BR)), compiler_params=_params(1),
    )(proj, proj, conv_w, conv_b, w_cat, b_cat, lam)


def _gate_out(h, proj, cb, tb, name):
    s = proj.shape[0]

    def body(h_ref, g_ref, o_ref):
        o_ref[...] = (h_ref[...] * _silu(g_ref[...])).astype(MXU_DTYPE)

    return pl.pallas_call(body, name=name, out_shape=SDS((s, BR), MXU_DTYPE), grid=(s // tb,),
                          in_specs=[_rows(tb, BR), _rows(tb, BR, cb)], out_specs=_rows(tb, BR),
                          compiler_params=_params(1))(h, proj)


def _gate_out_bwd(dycat, dy_cb, h, proj, cb, tb, name):
    s = proj.shape[0]

    def body(dy, h_ref, g_ref, dh_ref, dg_ref):
        dh_ref[...] = dy[...] * _silu(g_ref[...])
        dg_ref[...] = (dy[...] * h_ref[...] * _dsilu(g_ref[...])).astype(MXU_DTYPE)

    return pl.pallas_call(body, name=name, out_shape=(SDS((s, BR), F32), SDS((s, BR), MXU_DTYPE)), grid=(s // tb,),
                          in_specs=[_rows(tb, BR, dy_cb), _rows(tb, BR), _rows(tb, BR, cb)],
                          out_specs=(_rows(tb, BR), _rows(tb, BR)), compiler_params=_params(1))(dycat, h, proj)


def _lru_gates_bwd(proj, lmb, h, conv_w, conv_b, w_cat, b_cat, lam, tb):
    s = proj.shape[0]

    def body(cx, cxp, l_ref, h_ref, hp_ref, w_ref, cb_ref, wc_ref, bc_ref, lam_ref,
             dxc_ref, dpre_ref, xc_ref, dbc_ref, dlam_ref):
        _init_acc(dbc_ref, dlam_ref)
        has_prev = (pl.program_id(0) > 0).astype(F32)
        xc = _conv_taps(cx[...], cxp[...] * has_prev, w_ref, 4) + cb_ref[...]
        xcb = xc.astype(MXU_DTYPE)
        pre = jnp.dot(xcb, wc_ref[...], preferred_element_type=F32) + bc_ref[...]
        _, vjp = jax.vjp(_lru_gate, xc, pre[:, 0:BR], pre[:, BR:2 * BR], lam_ref[...])
        lm = l_ref[...]
        dxc, dpr, dpi, dlam = vjp((lm * _shift_down(h_ref[...], hp_ref[...] * has_prev, 1), lm))
        dpre = jnp.concatenate([dpr, dpi], axis=1)
        dpreb = dpre.astype(MXU_DTYPE)
        dxc_ref[...] = dxc + lax.dot_general(dpreb, wc_ref[...], (((1,), (1,)), ((), ())),
                                             preferred_element_type=F32)
        dpre_ref[...] = dpreb
        xc_ref[...] = xcb
        dbc_ref[...] += _colsum(dpre)
        dlam_ref[...] += dlam

    return pl.pallas_call(
        body, name="lru_gates_bwd",
        out_shape=(SDS((s, BR), F32), SDS((s, 2 * BR), MXU_DTYPE), SDS((s, BR), MXU_DTYPE),
                   SDS((1, 2 * BR), F32), SDS((1, BR), F32)),
        grid=(s // tb,),
        in_specs=[_rows(tb, BR, CB_CX), _prev8(tb, BR, CB_CX), _rows(tb, BR), _rows(tb, BR), _prev8(tb, BR),
                  _const((8, BR)), _const((1, BR)), _const((BR, 2 * BR)), _const((1, 2 * BR)), _const((1, BR))],
        out_specs=(_rows(tb, BR), _rows(tb, 2 * BR), _rows(tb, BR), _const((1, 2 * BR)), _const((1, BR))),
        compiler_params=_params(1))(proj, proj, lmb, h, h, conv_w, conv_b, w_cat, b_cat, lam)


def _conv_c_bwd(dxc, proj, conv_w, tb):
    s = proj.shape[0]

    def body(g, gn, cx, cxp, w_ref, dcx_ref, dw_ref, db_ref):
        _init_acc(dw_ref, db_ref)
        i = pl.program_id(0)
        has_prev = (i > 0).astype(F32)
        has_next = (i < pl.num_programs(0) - 1).astype(F32)
        gt = g[...]
        dcx_ref[...] = _conv_taps_t(gt, gn[...] * has_next, w_ref, 4).astype(MXU_DTYPE)
        _conv_wgrad(dw_ref, gt, cx[...], cxp[...] * has_prev, 4)
        db_ref[...] += _colsum(gt)

    return pl.pallas_call(
        body, name="conv_c_bwd", out_shape=(SDS((s, BR), MXU_DTYPE), SDS((8, BR), F32), SDS((1, BR), F32)),
        grid=(s // tb,),
        in_specs=[_rows(tb, BR), _next8(tb, BR, s), _rows(tb, BR, CB_CX), _prev8(tb, BR, CB_CX), _const((8, BR))],
        out_specs=(_rows(tb, BR), _const((8, BR)), _const((1, BR))), compiler_params=_params(1),
    )(dxc, dxc, proj, proj, conv_w)


def _s5_disc(lam_re, lam_im, log_dt):
    dt = jnp.exp(log_dt)
    mag = jnp.exp(lam_re * dt)
    ab_re = mag * jnp.cos(lam_im * dt)
    ab_im = mag * jnp.sin(lam_im * dt)
    den = lam_re * lam_re + lam_im * lam_im
    f_re = ((ab_re - 1.0) * lam_re + ab_im * lam_im) / den
    f_im = (ab_im * lam_re - (ab_re - 1.0) * lam_im) / den
    return ab_re, ab_im, f_re, f_im


def _s5_bbar(f_re, f_im, b_re, b_im):
    return f_re * b_re - f_im * b_im, f_re * b_im + f_im * b_re


def _s5_disc_fwd(lam_re, lam_im, log_dt):
    def body(lr, li, ld, o0, o1, o2, o3):
        o0[...], o1[...], o2[...], o3[...] = _s5_disc(lr[...], li[...], ld[...])
    return pl.pallas_call(body, name="s5_disc_fwd", out_shape=(SDS(lam_re.shape, F32),) * 4)(lam_re, lam_im, log_dt)


def _s5_disc_bwd(lam_re, lam_im, log_dt, cts):
    def body(lr, li, ld, c0, c1, c2, c3, o0, o1, o2):
        _, vjp = jax.vjp(_s5_disc, lr[...], li[...], ld[...])
        o0[...], o1[...], o2[...] = vjp((c0[...], c1[...], c2[...], c3[...]))
    return pl.pallas_call(body, name="s5_disc_bwd", out_shape=(SDS(lam_re.shape, F32), SDS(lam_re.shape, F32),
                                                                SDS(log_dt.shape, F32)))(lam_re, lam_im, log_dt, *cts)


def _s5_bbar_fwd(f_re, f_im, b_re, b_im):
    def body(fr, fi, br, bi, o0, o1):
        o0[...], o1[...] = _s5_bbar(fr[...], fi[...], br[...], bi[...])
    return pl.pallas_call(body, name="s5_bbar_fwd", out_shape=(SDS(b_re.shape, F32),) * 2)(f_re, f_im, b_re, b_im)


def _s5_bbar_bwd(f_re, f_im, b_re, b_im, d_re, d_im):
    def body(fr, fi, br, bi, dr, di, o0, o1, o2, o3):
        _, vjp = jax.vjp(_s5_bbar, fr[...], fi[...], br[...], bi[...])
        o0[...], o1[...], o2[...], o3[...] = vjp((dr[...], di[...]))
    col, mat = SDS(f_re.shape, F32), SDS(b_re.shape, F32)
    return pl.pallas_call(body, name="s5_bbar_bwd", out_shape=(col, col, mat, mat))(f_re, f_im, b_re, b_im, d_re, d_im)


def _s5_tail_fwd(ylin, proj, d_skip, w_glu, b_glu, tb):
    s = proj.shape[0]

    def body(yl, u, dg, dk, w_ref, b_ref, o_ref):
        g = jax.nn.gelu(yl[...] + dk[...] * u[...])
        t = jnp.dot(g.astype(MXU_DTYPE), w_ref[...], preferred_element_type=F32) + b_ref[...]
        o_ref[...] = (g * jax.nn.sigmoid(t) * _silu(dg[...])).astype(MXU_DTYPE)

    return pl.pallas_call(
        body, name="s5_tail_fwd", out_shape=SDS((s, BR), MXU_DTYPE), grid=(s // tb,),
        in_specs=[_rows(tb, BR), _rows(tb, BR, CB_DU), _rows(tb, BR, CB_DG), _const((1, BR)), _const((BR, BR)),
                  _const((1, BR))],
        out_specs=_rows(tb, BR), compiler_params=_params(1))(ylin, proj, proj, d_skip, w_glu, b_glu)


def _s5_tail_bwd(dycat, ylin, proj, d_skip, w_glu, b_glu, tb):
    s = proj.shape[0]

    def body(dy, yl, u, dg, dk, w_ref, b_ref, dyl_ref, dus_ref, ddg_ref, g_ref, dt_ref, ddk_ref, dbg_ref):
        _init_acc(ddk_ref, dbg_ref)
        g, gelu_vjp = jax.vjp(jax.nn.gelu, yl[...] + dk[...] * u[...])
        gb = g.astype(MXU_DTYPE)
        sg = jax.nn.sigmoid(jnp.dot(gb, w_ref[...], preferred_element_type=F32) + b_ref[...])
        dz = dy[...] * _silu(dg[...])
        ddg_ref[...] = (dy[...] * g * sg * _dsilu(dg[...])).astype(MXU_DTYPE)
        dt = dz * g * sg * (1.0 - sg)
        dtb = dt.astype(MXU_DTYPE)
        dgel = dz * sg + lax.dot_general(dtb, w_ref[...], (((1,), (1,)), ((), ())), preferred_element_type=F32)
        dyv, = gelu_vjp(dgel)
        dyl_ref[...] = dyv
        dus_ref[...] = dyv * dk[...]
        g_ref[...] = gb
        dt_ref[...] = dtb
        ddk_ref[...] += _colsum(dyv * u[...])
        dbg_ref[...] += _colsum(dt)

    big, half, vec = SDS((s, BR), F32), SDS((s, BR), MXU_DTYPE), SDS((1, BR), F32)
    return pl.pallas_call(
        body, name="s5_tail_bwd", out_shape=(big, big, half, half, half, vec, vec), grid=(s // tb,),
        in_specs=[_rows(tb, BR, 3), _rows(tb, BR), _rows(tb, BR, CB_DU), _rows(tb, BR, CB_DG), _const((1, BR)),
                  _const((BR, BR)), _const((1, BR))],
        out_specs=(_rows(tb, BR),) * 5 + (_const((1, BR)), _const((1, BR))), compiler_params=_params(1),
    )(dycat, ylin, proj, proj, d_skip, w_glu, b_glu)


def _assemble_dproj(da, dqkv, dbg, dcx, dcg, du, dus, ddg, tb):
    s = da.shape[0]

    def body(da_ref, q0, q1, q2, k0, k1, k2, v0, v1, v2, dbg_ref, dcx_ref, dcg_ref, du_ref, dus_ref, ddg_ref, o_ref):
        o_ref[:, 0:4 * BR] = da_ref[...]
        for j, parts in enumerate(((q0, q1, q2), (k0, k1, k2), (v0, v1, v2))):
            o_ref[:, (4 + j) * BR:(5 + j) * BR] = (parts[0][...] + parts[1][...] + parts[2][...]).astype(MXU_DTYPE)
        o_ref[:, 7 * BR:8 * BR] = dbg_ref[...].astype(MXU_DTYPE)
        o_ref[:, 8 * BR:9 * BR] = dcx_ref[...].astype(MXU_DTYPE)
        o_ref[:, 9 * BR:10 * BR] = dcg_ref[...].astype(MXU_DTYPE)
        o_ref[:, 10 * BR:11 * BR] = (du_ref[...] + dus_ref[...]).astype(MXU_DTYPE)
        o_ref[:, 11 * BR:12 * BR] = ddg_ref[...].astype(MXU_DTYPE)

    flat = [t for grp in dqkv for t in grp]
    return pl.pallas_call(
        body, name="assemble_dproj", out_shape=SDS((s, N_IN), MXU_DTYPE), grid=(s // tb,),
        in_specs=[_rows(tb, 4 * BR)] + [_rows(tb, BR)] * 15, out_specs=_rows(tb, N_IN),
        compiler_params=_params(1))(da, *flat, dbg, dcx, dcg, du, dus, ddg)


def _sum_leading(xs, tr, name):
    n, _, c = xs[0].shape
    nl = len(xs)
    tr = min([tr] + [x.shape[1] for x in xs])
    assert all(x.shape[1] % tr == 0 for x in xs), (name, tr)
    nrs = [x.shape[1] // tr for x in xs]
    starts = [sum(nrs[:l]) for l in range(nl)]

    def body(*refs):
        i = pl.program_id(0)
        for l in range(nl):
            @pl.when((i >= starts[l]) & (i < starts[l] + nrs[l]))
            def _():
                acc = refs[l * n][...].astype(F32)
                for ref in refs[l * n + 1:(l + 1) * n]:
                    acc = acc + ref[...].astype(F32)
                refs[nl * n][...] = acc

    specs = [pl.BlockSpec((None, tr, c), functools.partial(
        lambda i, k, l: (k, jnp.clip(i - starts[l], 0, nrs[l] - 1), 0), k=k, l=l)) for l in range(nl) for k in range(n)]
    return pl.pallas_call(body, name=name, out_shape=SDS((sum(nrs) * tr, c), F32), grid=(sum(nrs),), in_specs=specs,
                          out_specs=pl.BlockSpec((tr, c), lambda i: (i, 0)),
                          compiler_params=_params(1))(*[x for x in xs for _ in range(n)])


def _adamw(w, g_parts, m, v, tr, name):
    r, c = w.shape
    tr = min(tr, r)
    n = len(g_parts)
    assert r % tr == 0, (name, r, tr)

    def body(*refs):
        w_ref, m_ref, v_ref = refs[0], refs[1 + n], refs[2 + n]
        g_ref, d_ref, nm_ref, nv_ref = refs[3 + n:]
        g = refs[1][...]
        for ref in refs[2:1 + n]:
            g = g + ref[...]
        mm = ADAM_B1 * m_ref[...] + (1.0 - ADAM_B1) * g
        vv = ADAM_B2 * v_ref[...] + (1.0 - ADAM_B2) * jnp.square(g)
        m_hat = mm / (1.0 - ADAM_B1 ** ADAM_STEP)
        v_hat = vv / (1.0 - ADAM_B2 ** ADAM_STEP)
        g_ref[...] = g
        d_ref[...] = -ADAM_LR * (m_hat / (jnp.sqrt(v_hat) + ADAM_EPS) + ADAM_WD * w_ref[...])
        nm_ref[...] = mm
        nv_ref[...] = vv

    spec = pl.BlockSpec((tr, c), lambda i: (i, 0))
    return _call(body, name=name, out_shape=(SDS((r, c), F32),) * 4, grid=(r // tr,), in_specs=[spec] * (3 + n),
                 out_specs=(spec,) * 4, scratch_shapes=[], args=(w, *g_parts, m, v))


class _AllGather8:
    def __init__(self, block):
        self.m_per = block.shape[0]
        self.arrays, self.n_in, self.n_out = [block], 1, 1
        self.out_shapes = (SDS((N_DEV * self.m_per, block.shape[1]), block.dtype),)
        self.scratch = [pltpu.SemaphoreType.DMA((7,)), pltpu.SemaphoreType.DMA((7,)), pltpu.SemaphoreType.DMA]

    def _copies(self, ins, outs, sems):
        (x_ref,), (out_ref,), (send_sems, recv_sems, local_sem) = ins, outs, sems
        x, y, c = lax.axis_index("x"), lax.axis_index("y"), lax.axis_index("c")
        me, sibling = (x, y, c), (x, y, 1 - c)
        chips = [(1 - x, y), (x, 1 - y), (1 - x, 1 - y)]

        def rows(px, py, pc):
            return out_ref.at[pl.ds((4 * px + 2 * py + pc) * self.m_per, self.m_per), :]

        def copy(k, blk, to, src=None):
            return pltpu.make_async_remote_copy(
                src_ref=rows(*blk) if src is None else src, dst_ref=rows(*blk), send_sem=send_sems.at[k],
                recv_sem=recv_sems.at[k], device_id=to, device_id_type=MESH)

        mine = pltpu.make_async_copy(x_ref, rows(*me), local_sem)
        first = [copy(0, me, sibling, src=x_ref)]
        first += [copy(1 + j, me, (*chip, c), src=x_ref) for j, chip in enumerate(chips)]
        passed = [copy(4 + j, (*chip, c), sibling) for j, chip in enumerate(chips)]
        arrivals = [copy(1 + j, (*chip, c), me) for j, chip in enumerate(chips)]
        from_sibling = [copy(0, sibling, me)] + [copy(4 + j, (*chip, 1 - c), me) for j, chip in enumerate(chips)]
        return mine, first, passed, arrivals, from_sibling

    def start(self, ins, outs, sems):
        mine, first, _, _, _ = self._copies(ins, outs, sems)
        mine.start()
        for cp in first:
            cp.start()

    def wait(self, ins, outs, sems):
        mine, first, passed, arrivals, from_sibling = self._copies(ins, outs, sems)
        for arrived, onward in zip(arrivals, passed):
            arrived.wait_recv()
            onward.start()
        for cp in from_sibling:
            cp.wait_recv()
        for cp in first + passed:
            cp.wait_send()
        mine.wait()


def _allgather8(block, name):
    ex = _AllGather8(block)

    def body(x_ref, out_ref, *sems):
        ex.start((x_ref,), (out_ref,), sems)
        ex.wait((x_ref,), (out_ref,), sems)

    return pl.pallas_call(
        body, name=name, out_shape=ex.out_shapes[0], in_specs=[pl.BlockSpec(memory_space=pltpu.VMEM)],
        out_specs=pl.BlockSpec(memory_space=pltpu.VMEM), scratch_shapes=ex.scratch, compiler_params=_params())(block)


class _Exchange:
    def __init__(self, items, out_shapes):
        self.items, self.out_shapes = list(items), tuple(out_shapes)
        self.arrays = [it[0] for it in self.items]
        n = len(self.items)
        self.n_in, self.n_out = n, len(self.out_shapes)
        self.scratch = [pltpu.SemaphoreType.DMA((n * N_CHIPS,)), pltpu.SemaphoreType.DMA((n * N_CHIPS,)),
                        pltpu.SemaphoreType.DMA((n,))]

    def _copies(self, ins, outs, sems, m):
        send_sems, recv_sems, local_sems = sems
        c = lax.axis_index("c")
        others = [j for j in range(N_CHIPS) if j != m]

        def remote(a, src, dst, to, from_):
            return pltpu.make_async_remote_copy(
                src_ref=src, dst_ref=dst, send_sem=send_sems.at[a * N_CHIPS + to],
                recv_sem=recv_sems.at[a * N_CHIPS + from_], device_id=(to // 2, to % 2, c), device_id_type=MESH)

        local, sends, recvs = [], [], []
        for a, (_, oi, src_of, dst_of) in enumerate(self.items):
            local.append(pltpu.make_async_copy(src_of(ins[a], m), dst_of(outs[oi], m), local_sems.at[a]))
            for j in others:
                sends.append(remote(a, src_of(ins[a], j), dst_of(outs[oi], m), j, m))
                recvs.append(remote(a, src_of(ins[a], m), dst_of(outs[oi], j), j, j))
        return local, sends, recvs

    def _on_my_chip(self, fn):
        chip = 2 * lax.axis_index("x") + lax.axis_index("y")
        for m in range(N_CHIPS):
            pl.when(chip == m)(functools.partial(fn, m))

    def start(self, ins, outs, sems):
        def go(m):
            local, sends, _ = self._copies(ins, outs, sems, m)
            for cp in local + sends:
                cp.start()
        self._on_my_chip(go)

    def wait(self, ins, outs, sems):
        def go(m):
            local, sends, recvs = self._copies(ins, outs, sems, m)
            for cp in recvs:
                cp.wait_recv()
            for cp in sends:
                cp.wait_send()
            for cp in local:
                cp.wait()
        self._on_my_chip(go)


def _half_rows(ref, cc):
    h = ref.shape[-2] // 2
    return ref.at[(slice(None),) * (len(ref.shape) - 2) + (pl.ds(cc * h, h), slice(None))]


class _Gather:
    def __init__(self, items, out_shapes):
        self.items, self.out_shapes = list(items), tuple(out_shapes)
        self.arrays = [it[0] for it in self.items]
        n = len(self.items)
        self.n_in, self.n_out = n, len(self.out_shapes)
        self.scratch = [pltpu.SemaphoreType.DMA((n * N_CHIPS,)) for _ in range(4)] + [pltpu.SemaphoreType.DMA((n,))]

    def _copies(self, ins, outs, sems, m, cc):
        ici_send, ici_recv, d2d_send, d2d_recv, local_sems = sems
        others = [j for j in range(N_CHIPS) if j != m]
        local, sends, arrivals, passed_on, from_sibling = [], [], [], [], []
        for a, (_, oi, src_of, dst_of) in enumerate(self.items):
            src, out = src_of(ins[a]), outs[oi]
            local.append(pltpu.make_async_copy(src, dst_of(out, m), local_sems.at[a]))
            for j in others:
                k = a * N_CHIPS + j
                mine_there = _half_rows(dst_of(out, m), cc)
                theirs_here = _half_rows(dst_of(out, j), cc)
                sends.append(pltpu.make_async_remote_copy(
                    src_ref=_half_rows(src, cc), dst_ref=mine_there, send_sem=ici_send.at[k],
                    recv_sem=ici_recv.at[a * N_CHIPS + m], device_id=(j // 2, j % 2, cc), device_id_type=MESH))
                arrivals.append(pltpu.make_async_remote_copy(
                    src_ref=_half_rows(src, cc), dst_ref=theirs_here, send_sem=ici_send.at[k], recv_sem=ici_recv.at[k],
                    device_id=(j // 2, j % 2, cc), device_id_type=MESH))
                passed_on.append(pltpu.make_async_remote_copy(
                    src_ref=theirs_here, dst_ref=theirs_here, send_sem=d2d_send.at[k], recv_sem=d2d_recv.at[k],
                    device_id=(m // 2, m % 2, 1 - cc), device_id_type=MESH))
                other_half = _half_rows(dst_of(out, j), 1 - cc)
                from_sibling.append(pltpu.make_async_remote_copy(
                    src_ref=other_half, dst_ref=other_half, send_sem=d2d_send.at[k], recv_sem=d2d_recv.at[k],
                    device_id=(m // 2, m % 2, 1 - cc), device_id_type=MESH))
        return local, sends, arrivals, passed_on, from_sibling

    def _on_my_core(self, fn):
        chip = 2 * lax.axis_index("x") + lax.axis_index("y")
        c = lax.axis_index("c")
        for m in range(N_CHIPS):
            for cc in range(2):
                pl.when((chip == m) & (c == cc))(functools.partial(fn, m, cc))

    def start(self, ins, outs, sems):
        def go(m, cc):
            local, sends, _, _, _ = self._copies(ins, outs, sems, m, cc)
            for cp in local + sends:
                cp.start()
        self._on_my_core(go)

    def wait(self, ins, outs, sems):
        def go(m, cc):
            local, sends, arrivals, passed_on, from_sibling = self._copies(ins, outs, sems, m, cc)
            for arrived, onward in zip(arrivals, passed_on):
                arrived.wait_recv()
                onward.start()
            for cp in from_sibling:
                cp.wait_recv()
            for cp in sends + passed_on:
                cp.wait_send()
            for cp in local:
                cp.wait()
        self._on_my_core(go)


def _run_exchange(ex, name):
    def body(*refs):
        ins, outs, sems = refs[:ex.n_in], refs[ex.n_in:ex.n_in + ex.n_out], refs[ex.n_in + ex.n_out:]
        ex.start(ins, outs, sems)
        ex.wait(ins, outs, sems)

    return pl.pallas_call(
        body, name=name, out_shape=ex.out_shapes, in_specs=[ANY] * ex.n_in, out_specs=(ANY,) * ex.n_out,
        scratch_shapes=ex.scratch, compiler_params=_params())(*ex.arrays)


def _sibling_swap(arrays, name, also):
    n = len(arrays)

    def body(*refs):
        ins, refs = refs[:n], refs[n:]
        x_ins, refs = refs[:also.n_in], refs[also.n_in:]
        outs, refs = refs[:n], refs[n:]
        x_outs, refs = refs[:also.n_out], refs[also.n_out:]
        send_sems, recv_sems, x_sems = refs[0], refs[1], refs[2:]
        peer = (lax.axis_index("x"), lax.axis_index("y"), 1 - lax.axis_index("c"))
        cps = [pltpu.make_async_remote_copy(src_ref=ins[a], dst_ref=outs[a], send_sem=send_sems.at[a],
                                            recv_sem=recv_sems.at[a], device_id=peer, device_id_type=MESH)
               for a in range(n)]
        also.start(x_ins, x_outs, x_sems)
        for cp in cps:
            cp.start()
        also.wait(x_ins, x_outs, x_sems)
        for cp in cps:
            cp.wait()

    return pl.pallas_call(
        body, name=name, out_shape=tuple(SDS(a.shape, a.dtype) for a in arrays) + also.out_shapes,
        in_specs=[ANY] * (n + also.n_in), out_specs=(ANY,) * (n + also.n_out),
        scratch_shapes=[pltpu.SemaphoreType.DMA((n,)), pltpu.SemaphoreType.DMA((n,))] + also.scratch,
        compiler_params=_params())(*arrays, *also.arrays)


def _block_diag(w):
    h, n, m = w.shape
    eye = jnp.eye(h, dtype=w.dtype)
    return (w[:, :, None, :] * eye[:, None, :, None]).reshape(h * n, h * m)


def _diag_blocks(d, h, col0=0, ncols=None, stacked=1):
    ncols = d.shape[1] - col0 if ncols is None else ncols
    n, m = d.shape[0] // (h * stacked), ncols // h
    lanes = 128
    assert m <= lanes and lanes % m == 0 and col0 % lanes == 0

    def body(d_ref, o_ref):
        for gi in range(h * stacked):
            c = col0 + (gi % h) * m
            chunk = d_ref[gi * n:(gi + 1) * n, c // lanes * lanes:c // lanes * lanes + lanes]
            o_ref[gi * n:(gi + 1) * n, :] = chunk[:, c % lanes:c % lanes + m]

    out = pl.pallas_call(body, name="diag_blocks", out_shape=SDS((stacked * h * n, m), d.dtype),
                         compiler_params=_params())(d)
    return out.reshape(stacked * h, n, m)


S5_CHUNKS = 4
S5_PER = S5_GROUPS // S5_CHUNKS
CH_W = S5_PER * S5_CH
ST_W = S5_PER * S5_STATE


def _bd_stack(mats):
    _, _, n, m = mats.shape
    eye = jnp.eye(S5_PER, dtype=mats.dtype)
    t = mats.reshape(2, S5_CHUNKS, S5_PER, n, m)
    bd = t[:, :, :, :, None, :] * eye[None, None, :, None, :, None]
    return bd.reshape(2 * S5_CHUNKS, S5_PER * n, S5_PER * m).astype(MXU_DTYPE)


def _chunks_chunked(src_ref, buf):
    pt = src_ref.shape[0]
    out = []
    for q in range(S5_CHUNKS):
        buf[q] = src_ref[:, q * CH_W:(q + 1) * CH_W]
        out.append(_load_chunked(buf.at[q], 0, pt).astype(MXU_DTYPE))
    return out


def _expand_into(dst_ref, chunks, w_ref):
    for b in range(2 * S5_CHUNKS):
        dst_ref[:, b * ST_W:(b + 1) * ST_W] = jnp.dot(chunks[b % S5_CHUNKS], w_ref[b], preferred_element_type=F32)


def _reduce_from(src_ref, w_ref, buf, dst_ref):
    pt = src_ref.shape[0]
    for q in range(S5_CHUNKS):
        y = jnp.dot(src_ref[:, q * ST_W:(q + 1) * ST_W].astype(MXU_DTYPE), w_ref[q], preferred_element_type=F32)
        p = S5_CHUNKS + q
        y = y + jnp.dot(src_ref[:, p * ST_W:(p + 1) * ST_W].astype(MXU_DTYPE), w_ref[p], preferred_element_type=F32)
        _store_natural(buf.at[q], 0, pt, y)
        dst_ref[:, q * CH_W:(q + 1) * CH_W] = buf[q]


def _s5_core_fwd(proj, w_bu, w_cx, a_row):
    s = proj.shape[0]
    pt = _scan_tile(s)
    ch2 = 2 * S5_N

    def body(u_ref, wb_ref, wc_ref, a_ref, x_ref, y_ref, carry, pw, buf):
        _expand_into(x_ref, _chunks_chunked(u_ref, buf), wb_ref)
        _scan_tile_in_place(a_ref, x_ref, carry, pw, reverse=False)
        _reduce_from(x_ref, wc_ref, buf, y_ref)

    return pl.pallas_call(
        body, name="s5_core_fwd", out_shape=(SDS((s, ch2), F32), SDS((s, BR), F32)), grid=(s // pt,),
        in_specs=[_rows(pt, BR, CB_DU), _const(w_bu.shape), _const(w_cx.shape), _const((1, ch2))],
        out_specs=(_rows(pt, ch2), _rows(pt, BR)),
        scratch_shapes=[pltpu.VMEM((1, ch2), F32), pltpu.VMEM((pt // 8, ch2), F32),
                        pltpu.VMEM((S5_CHUNKS, pt, CH_W), F32)],
        compiler_params=_params(1))(proj, w_bu, w_cx, a_row)


def _s5_core_bwd(dyl, proj, x, w_dx, w_du, a_row):
    s = proj.shape[0]
    pt = _scan_tile(s)
    nt = s // pt
    ch2 = 2 * S5_N
    ch = S5_N

    def body(dy_ref, u_ref, x_ref, xp_ref, wx_ref, wu_ref, a_ref, du_ref, da_ref, dwb_ref, dwc_ref,
             l_ref, carry, pw, buf, buf2):
        i = pl.program_id(0)
        _init_acc(da_ref, dwb_ref, dwc_ref)
        dy_c = _chunks_chunked(dy_ref, buf)
        u_c = _chunks_chunked(u_ref, buf2)
        _expand_into(l_ref, dy_c, wx_ref)
        _scan_tile_in_place(a_ref, l_ref, carry, pw, reverse=True)
        has_prev = (i < nt - 1).astype(F32)
        row = lax.broadcasted_iota(jnp.int32, (8, ch2), 0)
        first = jnp.where(row == 0, pltpu.roll(xp_ref[...], 1, 0) * has_prev, pltpu.roll(x_ref[pt - 8:pt, :], 1, 0))
        xprev = jnp.concatenate([first, x_ref[0:pt - 8, :]], axis=0)
        lr, li, xr, xi = l_ref[:, 0:ch], l_ref[:, ch:ch2], xprev[:, 0:ch], xprev[:, ch:ch2]
        da_ref[:, 0:ch] += _colsum(lr * xr + li * xi)
        da_ref[:, ch:ch2] += _colsum(li * xr - lr * xi)
        _reduce_from(l_ref, wu_ref, buf, du_ref)
        tn = (((0,), (0,)), ((), ()))
        for b in range(2 * S5_CHUNKS):
            cols, rows = slice(b * ST_W, (b + 1) * ST_W), slice(b * CH_W, (b + 1) * CH_W)
            dwb_ref[rows, :] += lax.dot_general(u_c[b % S5_CHUNKS], l_ref[:, cols].astype(MXU_DTYPE), tn,
                                                preferred_element_type=F32)
            dwc_ref[rows, :] += lax.dot_general(dy_c[b % S5_CHUNKS], x_ref[:, cols].astype(MXU_DTYPE), tn,
                                                preferred_element_type=F32)

    rev = lambda w, cb=0: pl.BlockSpec((pt, w), lambda i: (nt - 1 - i, cb))
    halo = pl.BlockSpec((8, ch2), lambda i: (jnp.maximum((nt - 1 - i) * (pt // 8) - 1, 0), 0))
    wshape = SDS((2 * S5_CHUNKS * CH_W, ST_W), F32)
    return pl.pallas_call(
        body, name="s5_core_bwd", out_shape=(SDS((s, BR), F32), SDS((1, ch2), F32), wshape, wshape), grid=(nt,),
        in_specs=[rev(BR, 0), rev(BR, CB_DU), rev(ch2), halo, _const(w_dx.shape), _const(w_du.shape),
                  _const((1, ch2))],
        out_specs=(rev(BR), _const((1, ch2)), _const(wshape.shape), _const(wshape.shape)),
        scratch_shapes=[pltpu.VMEM((pt, ch2), F32), pltpu.VMEM((1, ch2), F32), pltpu.VMEM((pt // 8, ch2), F32),
                        pltpu.VMEM((S5_CHUNKS, pt, CH_W), F32), pltpu.VMEM((S5_CHUNKS, pt, CH_W), F32)],
        compiler_params=_params(1))(dyl, proj, x, x, w_dx, w_du, a_row)


def _tiles(s):
    return dict(tb=min(512, s), tln=min(256, s))


def _layer_weights(p, l):
    pad8 = lambda w: jnp.pad(w, ((0, 8 - w.shape[0]), (0, 0)))
    return dict(
        conv_a=pad8(p["conv_a"][l]), conv_c=pad8(p["conv_c"][l]), conv_c_b=p["conv_c_b"][l][None],
        w_cat=jnp.concatenate([_block_diag(p["lru_wa"][l]), _block_diag(p["lru_wx"][l])], axis=1).astype(MXU_DTYPE),
        b_cat=jnp.concatenate([p["lru_ba"][l], p["lru_bx"][l]])[None], lam=p["lru_lambda"][l][None],
        lam_re=p["s5_lam_re"][l], lam_im=p["s5_lam_im"][l], log_dt=p["s5_log_dt"][l][:, None],
        b_re=p["s5_b_re"][l].reshape(S5_N, S5_CH), b_im=p["s5_b_im"][l].reshape(S5_N, S5_CH),
        c_re=p["s5_c_re"][l], c_im=p["s5_c_im"][l], d_skip=p["s5_d"][l][None], b_glu=p["s5_b_glu"][l][None],
        ln_g=p["ln_g"][l][None], ln_b=p["ln_b"][l][None])


def _s5_matrices(lw):
    ab_re, ab_im, f_re, f_im = _s5_disc_fwd(lw["lam_re"], lw["lam_im"], lw["log_dt"])
    f_re, f_im = f_re.reshape(S5_N, 1), f_im.reshape(S5_N, 1)
    bb_re, bb_im = _s5_bbar_fwd(f_re, f_im, lw["b_re"], lw["b_im"])
    bb = jnp.stack([bb_re, bb_im]).reshape(2, S5_GROUPS, S5_STATE, S5_CH)
    cc = jnp.stack([lw["c_re"], -lw["c_im"]])
    a_row = jnp.concatenate([ab_re.reshape(1, S5_N), ab_im.reshape(1, S5_N)], axis=1)
    return dict(f_re=f_re, f_im=f_im, a_row=a_row, w_bu=_bd_stack(jnp.swapaxes(bb, 2, 3)), w_du=_bd_stack(bb),
                w_cx=_bd_stack(jnp.swapaxes(cc, 2, 3)), w_dx=_bd_stack(cc))


def _mm_hooked(hook, *args, **kw):
    if hook is None:
        return _mm(*args, **kw)
    out = _mm(*args, carry=hook[0], **kw)
    hook[1](out[1:])
    return out[0]


def _layer_fwd(x, ada, w_in, get_rest, lw, s5m, bias_tabs, hooks=None):
    s = x.shape[0]
    t = _tiles(s)
    tb = t["tb"]
    shift, scale, gate = ada
    hooks = hooks or {}
    h = _modulate(x, scale, shift, tb)
    proj = _mm_hooked(hooks.get("in_proj"), h, w_in, name="in_proj", tm=1024, tn=1536, tk=D_MODEL)
    w_out, w_glu = get_rest()
    y_a = _branch_a_fwd(proj, lw["conv_a"], tb)
    os_, lses = [], []
    for g, (_, dil) in enumerate(DILATIONS):
        o, lse = _attn_fwd(proj, bias_tabs[g], dil)
        os_.append(o)
        lses.append(lse)
    y_b = _attn_combine(os_, lses, proj, tb)
    lru_a, lru_b = _lru_gates_fwd(proj, lw["conv_c"], lw["conv_c_b"], lw["w_cat"], lw["b_cat"], lw["lam"], tb)
    lru_h = _scan_real(lru_a, lru_b, reverse=False, tb=tb, name="lru_scan")
    y_c = _gate_out(lru_h, proj, CB_CG, tb, "lru_out")
    s5_x, ylin = _s5_core_fwd(proj, s5m["w_bu"], s5m["w_cx"], s5m["a_row"])
    y_d = _s5_tail_fwd(ylin, proj, lw["d_skip"], w_glu, lw["b_glu"], tb)
    ycat = jnp.concatenate([y_a, y_b, y_c, y_d], axis=1)
    x_next, xhat, y, rstd = _out_ln(ycat, w_out, x, gate, lw["ln_g"], lw["ln_b"], t["tln"])
    saved = dict(x=x, h=h, proj=proj, os=os_, lses=lses, lru_a=lru_a, lru_h=lru_h, s5_x=s5_x, ylin=ylin, ycat=ycat,
                 xhat=xhat, y=y, rstd=rstd)
    return x_next, saved


def _layer_bwd(dxn, sv, ada, w_in, w_out, w_glu, lw, s5m, bias_tabs, head_ones, hooks=None):
    s = dxn.shape[0]
    t = _tiles(s)
    tb = t["tb"]
    shift, scale, gate = ada
    proj = sv["proj"]
    g = {}
    hook = lambda name: hooks[name](g) if hooks and name in hooks else None
    dyb, dxa, g["ln_g"], g["ln_b"], dgate = _ln_bwd(dxn, sv["xhat"], sv["y"], sv["rstd"], lw["ln_g"], gate, t["tln"])
    g["w_out"] = _mm_hooked(hook("dw_out"), sv["ycat"], dyb, name="dw_out", ta=True, out_dtype=WIRE_DTYPE,
                            tm=1024, tn=1024, tk=2048)
    dycat =_mm(dyb, w_out, name="dycat", tb=True, tm=1024, tn=1024, tk=D_MODEL)
    da, dconv_a = _branch_a_bwd(dycat, proj, lw["conv_a"], tb)
    g["conv_a"] = dconv_a[0:3]
    pre = _attn_bwd_pre(dycat, sv["os"], sv["lses"], proj, head_ones, tb)
    dbg, dos, dms = pre[0], pre[1:4], pre[4:7]
    dqkv, dbias = [], []
    for gi, (_, dil) in enumerate(DILATIONS):
        hk = hook(f"attn_bwd_d{dil}")
        dq, dk, dv, dbi, *got = _attn_bwd(proj, dos[gi], sv["lses"][gi], dms[gi], bias_tabs[gi], dil,
                                          carry=hk and hk[0])
        if hk:
            hk[1](got)
        dqkv.append((dq, dk, dv))
        dbias.append(dbi)
    dqkv = list(zip(*dqkv))
    dh, dcg = _gate_out_bwd(dycat, 2, sv["lru_h"], proj, CB_CG, tb, "lru_out_bwd")
    lmb = _scan_real(sv["lru_a"], dh, reverse=True, tb=tb, name="lru_scan_bwd")
    dxc, dpre, xcb, dbcat, dlam = _lru_gates_bwd(proj, lmb, sv["lru_h"], lw["conv_c"], lw["conv_c_b"], lw["w_cat"],
                                                  lw["b_cat"], lw["lam"], tb)
    dwcat = _mm(xcb, dpre, name="dw_lru", ta=True, tn=1024)
    g["lru_wa"] = _diag_blocks(dwcat, LRU_HEADS, 0, BR)
    g["lru_wx"] = _diag_blocks(dwcat, LRU_HEADS, BR, BR)
    g["lru_ba"], g["lru_bx"], g["lru_lambda"] = dbcat[0, 0:BR], dbcat[0, BR:2 * BR], dlam[0]
    dcx, dconv_c, dccb = _conv_c_bwd(dxc, proj, lw["conv_c"], tb)
    g["conv_c"], g["conv_c_b"] = dconv_c[0:4], dccb[0]
    dyl, dus, ddg, gb, dtb, ddk, dbglu = _s5_tail_bwd(dycat, sv["ylin"], proj, lw["d_skip"], w_glu, lw["b_glu"], tb)
    g["s5_d"], g["s5_b_glu"] = ddk[0], dbglu[0]
    g["s5_w_glu"] = _mm(gb, dtb, name="dw_glu", ta=True, out_dtype=WIRE_DTYPE)
    du, dab, dwb8, dwc8 = _s5_core_bwd(dyl, proj, sv["s5_x"], s5m["w_dx"], s5m["w_du"], s5m["a_row"])
    per_group = lambda d8: _diag_blocks(d8, S5_PER, stacked=2 * S5_CHUNKS).reshape(2, S5_GROUPS, S5_CH, S5_STATE)
    dbb, dcc = per_group(dwb8), per_group(dwc8)
    from_bd = lambda half: jnp.swapaxes(dbb[half], 1, 2).reshape(S5_N, S5_CH)
    df_re, df_im, db_re, db_im = _s5_bbar_bwd(s5m["f_re"], s5m["f_im"], lw["b_re"], lw["b_im"],
                                              from_bd(0), from_bd(1))
    shp = (S5_GROUPS, S5_STATE)
    g["s5_lam_re"], g["s5_lam_im"], dlog_dt = _s5_disc_bwd(
        lw["lam_re"], lw["lam_im"], lw["log_dt"],
        (dab[:, 0:S5_N].reshape(shp), dab[:, S5_N:].reshape(shp), df_re.reshape(shp), df_im.reshape(shp)))
    g["s5_log_dt"] = dlog_dt[:, 0]
    g["s5_b_re"] = db_re.reshape(S5_GROUPS, S5_STATE, S5_CH)
    g["s5_b_im"] = db_im.reshape(S5_GROUPS, S5_STATE, S5_CH)
    g["s5_c_re"], g["s5_c_im"] = dcc[0], -dcc[1]
    dproj = _assemble_dproj(da, dqkv, dbg, dcx, dcg, du, dus, ddg, tb)
    g["w_in"] = _mm_hooked(hook("dw_in"), sv["h"], dproj, name="dw_in", ta=True, out_dtype=WIRE_DTYPE,
                           tm=1024, tn=1536, tk=2048)
    hk = hook("dh")
    dx, dshift, dscale, *got = _dh_mod_bwd(dproj, w_in, dxa, sv["x"], scale, carry=hk and hk[0])
    if hk:
        hk[1](got)
    g["ada"] = jnp.concatenate([dshift[0], dscale[0], dgate[0]])
    return dx, g, dbias


SMALL = ("rel_bias", "conv_a", "conv_c", "conv_c_b", "lru_wa", "lru_ba", "lru_wx", "lru_bx", "lru_lambda",
         "s5_lam_re", "s5_lam_im", "s5_log_dt", "s5_b_re", "s5_b_im", "s5_c_re", "s5_c_im", "s5_d", "s5_b_glu",
         "ln_g", "ln_b")
PER_LAYER_SMALL = SMALL[1:]


def _local_step(x, target, ada_rows, w_in, w_out, w_glu, p, comm=None):
    if comm is None:
        get_w_in = lambda l: w_in[l]
        get_rest = lambda l: (w_out[l], w_glu[l])
        fwd_hooks = bwd_hooks = lambda *_: None
    else:
        get_w_in, get_rest, fwd_hooks, bwd_hooks = comm.w_in, comm.rest, comm.fwd_hooks, comm.bwd_hooks
    s = x.shape[0]
    buckets = _bucket_maps()
    bias_tabs = _bias_tables(p["rel_bias"], buckets)
    head_ones = _block_diag(jnp.ones((ATT_HEADS, HEAD_DIM, HEAD_DIM), MXU_DTYPE))
    lws = [_layer_weights(p, l) for l in range(DEPTH)]
    s5ms = [_s5_matrices(lw) for lw in lws]
    adas = [tuple(ada_rows[l, k * D_MODEL:(k + 1) * D_MODEL][None] for k in range(3)) for l in range(DEPTH)]
    saved = []
    for l in range(DEPTH):
        x, sv = _layer_fwd(x, adas[l], get_w_in(l), functools.partial(get_rest, l), lws[l], s5ms[l], bias_tabs,
                           fwd_hooks(l))
        saved.append(sv)
    loss, dx = _loss_head(x, target, _tiles(s)["tb"])
    grads = [None] * DEPTH
    dbias_sum = []
    for l in reversed(range(DEPTH)):
        dx, grads[l], dbias = _layer_bwd(dx, saved[l], adas[l], get_w_in(l), *get_rest(l), lws[l], s5ms[l],
                                         bias_tabs, head_ones, bwd_hooks(l, grads))
        dbias_sum.append(jnp.stack(dbias))
    drel = _rel_bias_grad(jnp.stack(dbias_sum), buckets)[:, 0:ATT_HEADS]
    small = {n: jnp.stack([grads[l][n] for l in range(DEPTH)]) for n in PER_LAYER_SMALL + ("ada",)}
    small["rel_bias"] = drel
    big = {n: [grads[l][n] for l in range(DEPTH)] for n in ("w_in", "w_out", "s5_w_glu")}
    return loss, dx, big, small


PACK_ROWS = 256


def _pack(parts):
    flat = jnp.concatenate([t.reshape(-1).astype(F32) for t in parts])
    n = flat.shape[0]
    rows = -(-n // (PACK_ROWS * 128)) * PACK_ROWS
    return jnp.pad(flat, (0, rows * 128 - n)).reshape(rows, 128)


def _unpack(packed, shapes):
    flat = packed.reshape(packed.shape[:-2] + (-1,))
    out, off = [], 0
    for shp in shapes:
        size = math.prod(shp)
        out.append(flat[..., off:off + size].reshape(flat.shape[:-1] + tuple(shp)))
        off += size
    return out


def _take_cols(t, chip, width):
    return lax.dynamic_slice_in_dim(t, chip * width, width, axis=t.ndim - 1)


class _Comm:
    IN_W, OUT_R, GLU_R = N_IN // N_CHIPS, D_MODEL // N_CHIPS, BR // N_CHIPS

    def __init__(self, w_in_b, w_out_b, w_glu_b):
        assert DEPTH == 2
        self.shards = (w_in_b, w_out_b, w_glu_b)
        in_w = self.IN_W
        self.w_in_full = {0: _run_exchange(_Gather(
            [(w_in_b, 0, lambda ref: ref.at[0], lambda ref, j: ref.at[:, pl.ds(j * in_w, in_w)])],
            [SDS((D_MODEL, N_IN), WIRE_DTYPE)]), "gather_w_in0")[0]}
        self.w_out_full = self.w_glu_full = None
        self.recv = {}

    def w_in(self, l):
        return self.w_in_full[l]

    def rest(self, l):
        return self.w_out_full[l], self.w_glu_full[l]

    def fwd_hooks(self, l):
        if l != 0:
            return None
        w_in_b, w_out_b, w_glu_b = self.shards
        in_w, out_r, glu_r = self.IN_W, self.OUT_R, self.GLU_R
        whole = lambda ref: ref
        items = [(w_out_b, 0, whole, lambda ref, j: ref.at[:, pl.ds(j * out_r, out_r), :]),
                 (w_glu_b, 1, whole, lambda ref, j: ref.at[:, pl.ds(j * glu_r, glu_r), :]),
                 (w_in_b, 2, lambda ref: ref.at[1], lambda ref, j: ref.at[:, pl.ds(j * in_w, in_w)])]
        shapes = [SDS((DEPTH, D_MODEL, D_MODEL), WIRE_DTYPE), SDS((DEPTH, BR, BR), WIRE_DTYPE),
                  SDS((D_MODEL, N_IN), WIRE_DTYPE)]

        def done(got):
            self.w_out_full, self.w_glu_full, self.w_in_full[1] = got

        return {"in_proj": (_Gather(items, shapes), done)}

    W_IN_ROWS = ((0, 1024), (1024, 512), (1536, 512))

    def _scatter(self, parts):
        in_w, out_r, glu_r = self.IN_W, self.OUT_R, self.GLU_R
        items, shapes, keys = [], [], []
        for oi, (name, l, arr, *rows) in enumerate(parts):
            if name == "w_in":
                r0, nr = rows[0] if rows else (0, D_MODEL)
                cut = functools.partial(lambda ref, j, r0, nr: ref.at[pl.ds(r0, nr), pl.ds(j * in_w, in_w)], r0=r0, nr=nr)
                shard = (nr, in_w)
            elif name == "w_out":
                cut, shard = (lambda ref, j: ref.at[pl.ds(j * out_r, out_r), :]), (out_r, D_MODEL)
            else:
                cut, shard = (lambda ref, j: ref.at[pl.ds(j * glu_r, glu_r), :]), (glu_r, BR)
            items.append((arr, oi, cut, lambda ref, j: ref.at[j]))
            shapes.append(SDS((N_CHIPS,) + shard, WIRE_DTYPE))
            keys.append((name, l) + ((rows[0][0],) if rows else ()))

        def done(got):
            self.recv.update(zip(keys, got))

        return _Exchange(items, shapes), done

    def received(self, name):
        return [self.recv[k] for k in sorted(k for k in self.recv if k[0] == name)]

    def bwd_hooks(self, l, grads):
        if l != 0:
            return None
        g1 = grads[1]
        w_in_part = lambda k: (lambda g: self._scatter([("w_in", 1, g1["w_in"], self.W_IN_ROWS[k])]))
        return {"dw_out": lambda g: self._scatter([("w_out", 1, g1["w_out"]), ("s5_w_glu", 1, g1["s5_w_glu"])]),
                "attn_bwd_d16": w_in_part(0), "attn_bwd_d4": w_in_part(1), "attn_bwd_d1": w_in_part(2),
                "dw_in": lambda g: self._scatter([("w_out", 0, g["w_out"]), ("s5_w_glu", 0, g["s5_w_glu"])]),
                "dh": lambda g: self._scatter([("w_in", 0, g["w_in"])])}


def kernel(x, c, rel_bias, w_ada, b_ada, w_in, conv_a, conv_c, conv_c_b, lru_wa, lru_ba, lru_wx, lru_bx, lru_lambda, s5_lam_re, s5_lam_im, s5_log_dt, s5_b_re, s5_b_im, s5_c_re, s5_c_im, s5_d, s5_w_glu, s5_b_glu, w_out, ln_g, ln_b, loss_target, m_rel_bias, m_w_ada, m_b_ada, m_w_in, m_conv_a, m_conv_c, m_conv_c_b, m_lru_wa, m_lru_ba, m_lru_wx, m_lru_bx, m_lru_lambda, m_s5_lam_re, m_s5_lam_im, m_s5_log_dt, m_s5_b_re, m_s5_b_im, m_s5_c_re, m_s5_c_im, m_s5_d, m_s5_w_glu, m_s5_b_glu, m_w_out, m_ln_g, m_ln_b, v_rel_bias, v_w_ada, v_b_ada, v_w_in, v_conv_a, v_conv_c, v_conv_c_b, v_lru_wa, v_lru_ba, v_lru_wx, v_lru_bx, v_lru_lambda, v_s5_lam_re, v_s5_lam_im, v_s5_log_dt, v_s5_b_re, v_s5_b_im, v_s5_c_re, v_s5_c_im, v_s5_d, v_s5_w_glu, v_s5_b_glu, v_w_out, v_ln_g, v_ln_b):
    args = dict(locals())
    names = ("rel_bias", "w_ada", "b_ada", "w_in", "conv_a", "conv_c", "conv_c_b", "lru_wa", "lru_ba", "lru_wx",
             "lru_bx", "lru_lambda", "s5_lam_re", "s5_lam_im", "s5_log_dt", "s5_b_re", "s5_b_im", "s5_c_re", "s5_c_im",
             "s5_d", "s5_w_glu", "s5_b_glu", "w_out", "ln_g", "ln_b")
    w = {n: args[n] for n in names}
    mom = {n: args["m_" + n] for n in names}
    var = {n: args["v_" + n] for n in names}
    chip = 2 * lax.axis_index("x") + lax.axis_index("y")
    me = 2 * chip + lax.axis_index("c")
    ada_w = 3 * D_MODEL // N_CHIPS
    conv_w = BR // N_CHIPS

    comm = _Comm(w["w_in"].astype(WIRE_DTYPE), w["w_out"].astype(WIRE_DTYPE), w["s5_w_glu"].astype(WIRE_DTYPE))

    taps = jnp.concatenate([w["conv_a"].reshape(DEPTH * 3, conv_w), w["conv_c"].reshape(DEPTH * 4, conv_w)])
    first = jnp.concatenate([c, jnp.pad(taps, ((0, 1), (0, D_MODEL - conv_w)))])
    got = _allgather8(first, "gather_c_taps").reshape(N_CHIPS, 2, 16, D_MODEL)
    c_all = got[:, :, 0].reshape(N_DEV, D_MODEL)
    taps_all = jnp.transpose(got[:, 0, 1:1 + DEPTH * 7, 0:conv_w], (1, 0, 2)).reshape(DEPTH * 7, BR)
    conv_a_f = taps_all[0:DEPTH * 3].reshape(DEPTH, 3, BR)
    conv_c_f = taps_all[DEPTH * 3:].reshape(DEPTH, 4, BR)

    cond_all = _silu_rows(c_all)
    ada_part = jnp.stack([_mm(cond_all, w["w_ada"][l], name="ada_fwd", tk=D_MODEL, tn=512,
                              bias=_take_cols(w["b_ada"][l][None], chip, ada_w)) for l in range(DEPTH)])
    ada_all = _allgather8(ada_part.reshape(DEPTH * N_DEV, ada_w), "gather_ada")
    ada_all = ada_all.reshape(N_CHIPS, 2, DEPTH, N_DEV, ada_w)[:, 0]
    ada_rows = lax.dynamic_index_in_dim(ada_all, me, axis=2, keepdims=False)
    ada_rows = jnp.transpose(ada_rows, (1, 0, 2)).reshape(DEPTH, 3 * D_MODEL)

    p = dict(w)
    p["conv_a"], p["conv_c"] = conv_a_f, conv_c_f
    loss, dx, _, small = _local_step(x[0], loss_target[0], ada_rows, None, None, None, p, comm)

    sums = [_sum_leading(comm.received(name), 256, "sum_chips") for name in ("w_in", "w_out", "s5_w_glu")]
    small_names = SMALL + ("ada",)
    small["loss"] = loss
    order = small_names + ("loss",)
    shapes = [small[n].shape for n in order]
    *others, gathered = _sibling_swap(sums, "swap_cores", _AllGather8(_pack([small[n] for n in order])))
    out = {}
    for name, mine, other in zip(("w_in", "w_out", "s5_w_glu"), sums, others):
        shp = w[name].shape
        flat = lambda t: t.reshape(-1, shp[-1])
        res = _adamw(flat(w[name]), [mine, other], flat(mom[name]), flat(var[name]), 128, "adamw_big")
        out[name] = [t.reshape(shp) for t in res]
    gathered = gathered.reshape(N_DEV, -1, 128)
    total = dict(zip(order, _unpack(_sum_leading([gathered], PACK_ROWS, "sum_devices"), shapes)))
    d_ada_all = _unpack(gathered, shapes)[order.index("ada")]
    g_small = {n: total[n] for n in SMALL}
    g_small["conv_a"] = _take_cols(total["conv_a"], chip, conv_w)
    g_small["conv_c"] = _take_cols(total["conv_c"], chip, conv_w)
    g_small["b_ada"] = total["ada"]
    g_w_ada = jnp.stack([_mm(cond_all, _take_cols(d_ada_all[:, l], chip, ada_w), name="dw_ada", ta=True, tn=ada_w)
                         for l in range(DEPTH)])
    upd_names = SMALL + ("b_ada",)
    upd_shapes = [w[n].shape for n in upd_names]
    res = _adamw(_pack([w[n] for n in upd_names]), [_pack([g_small[n] for n in upd_names])],
                 _pack([mom[n] for n in upd_names]), _pack([var[n] for n in upd_names]), PACK_ROWS, "adamw_small")
    for k, t in enumerate(res):
        for n, val in zip(upd_names, _unpack(t, upd_shapes)):
            out.setdefault(n, [None] * 4)[k] = val
    shp = w["w_ada"].shape
    flat = lambda t: t.reshape(-1, shp[-1])
    out["w_ada"] = [t.reshape(shp) for t in _adamw(flat(w["w_ada"]), [flat(g_w_ada)], flat(mom["w_ada"]),
                                                  flat(var["w_ada"]), 128, "adamw_ada")]
    return (total["loss"].reshape(()), dx[None]) + tuple(out[n][k] for k in range(4) for n in names)
```

```python
import functools
import math

import jax
import jax.numpy as jnp
from jax import lax
from jax.experimental import pallas as pl
from jax.experimental.pallas import tpu as pltpu

F32 = jnp.float32
MXU_DTYPE = jnp.bfloat16
WIRE_DTYPE = jnp.bfloat16
SDS = jax.ShapeDtypeStruct
MESH = pl.DeviceIdType.MESH
ANY = pl.BlockSpec(memory_space=pl.ANY)
VMEM_LIMIT = 48 * 1024 * 1024

D_MODEL = 2048
DEPTH = 2
BR = 512
ATT_HEADS = 8
HEAD_DIM = 64
DILATIONS = ((128, 1), (512, 4), (2048, 16))
BLK = 128
REL_BUCKETS = 32
REL_MAX_DIST = 2048
LRU_HEADS = 8
LRU_C = 8.0
S5_CH = 16
S5_GROUPS = 32
S5_STATE = 64
S5_N = S5_GROUPS * S5_STATE
N_IN = 12 * BR
ALPHA = (2 * DEPTH) ** 0.25
LN_EPS = 1e-5
NEG = -1e30
ADAM_LR, ADAM_B1, ADAM_B2, ADAM_EPS, ADAM_WD, ADAM_STEP = 0.001, 0.9, 0.999, 1e-08, 0.01, 10
CB_AB, CB_AC, CB_AX, CB_AG, CB_Q, CB_K, CB_V, CB_BG, CB_CX, CB_CG, CB_DU, CB_DG = range(12)
N_CHIPS = 4
N_DEV = 8


def _params(n_axes=0):
    kw = {"dimension_semantics": ("arbitrary",) * n_axes} if n_axes else {}
    return pltpu.CompilerParams(vmem_limit_bytes=VMEM_LIMIT, **kw)


def _rows(tb, w, cb=0):
    return pl.BlockSpec((tb, w), lambda i: (i, cb))


def _prev8(tb, w, cb=0):
    return pl.BlockSpec((8, w), lambda i: (jnp.maximum(i * (tb // 8) - 1, 0), cb))


def _next8(tb, w, n_rows, cb=0):
    return pl.BlockSpec((8, w), lambda i: (jnp.minimum((i + 1) * (tb // 8), n_rows // 8 - 1), cb))


def _const(shape):
    return pl.BlockSpec(shape, lambda *_: (0,) * len(shape))


def _silu(x):
    return x * jax.nn.sigmoid(x)


def _dsilu(x):
    s = jax.nn.sigmoid(x)
    return s * (1.0 + x * (1.0 - s))


def _shift_down(cur, prev8, j):
    rolled = pltpu.roll(cur, j, 0)
    row = lax.broadcasted_iota(jnp.int32, (8, cur.shape[1]), 0)
    first = jnp.where(row < j, pltpu.roll(prev8, j, 0), rolled[0:8])
    return jnp.concatenate([first, rolled[8:]], axis=0)


def _shift_up(cur, next8, j):
    t = cur.shape[0]
    rolled = pltpu.roll(cur, t - j, 0)
    row = lax.broadcasted_iota(jnp.int32, (8, cur.shape[1]), 0)
    last = jnp.where(row >= 8 - j, pltpu.roll(next8, 8 - j, 0), rolled[t - 8:t])
    return jnp.concatenate([rolled[:t - 8], last], axis=0)


def _colsum(x):
    return jnp.sum(x, axis=0, keepdims=True)


def _init_acc(*refs):
    @pl.when(pl.program_id(0) == 0)
    def _():
        for r in refs:
            r[...] = jnp.zeros_like(r)


def _call(body, *, name, out_shape, grid, in_specs, out_specs, scratch_shapes, args, carry=None):
    out_shape, out_specs, in_specs = tuple(out_shape), tuple(out_specs), list(in_specs)
    scratch_shapes = list(scratch_shapes)
    if carry is None:
        return pl.pallas_call(body, name=name, out_shape=out_shape, grid=grid, in_specs=in_specs, out_specs=out_specs,
                              scratch_shapes=scratch_shapes, compiler_params=_params(len(grid)))(*args)
    n_in, n_out, n_scr = len(in_specs), len(out_shape), len(scratch_shapes)

    def wrapped(*refs):
        ins, refs = refs[:n_in], refs[n_in:]
        x_ins, refs = refs[:carry.n_in], refs[carry.n_in:]
        outs, refs = refs[:n_out], refs[n_out:]
        x_outs, refs = refs[:carry.n_out], refs[carry.n_out:]
        scr, x_sems = refs[:n_scr], refs[n_scr:]
        at = [pl.program_id(d) for d in range(len(grid))]
        first = functools.reduce(lambda p, q: p & q, [i == 0 for i in at])
        last = functools.reduce(lambda p, q: p & q, [i == g - 1 for i, g in zip(at, grid)])
        pl.when(first)(lambda: carry.start(x_ins, x_outs, x_sems))
        body(*ins, *outs, *scr)
        pl.when(last)(lambda: carry.wait(x_ins, x_outs, x_sems))

    return pl.pallas_call(
        wrapped, name=name, out_shape=out_shape + carry.out_shapes, grid=grid, in_specs=in_specs + [ANY] * carry.n_in,
        out_specs=out_specs + (ANY,) * carry.n_out, scratch_shapes=scratch_shapes + carry.scratch,
        compiler_params=_params(len(grid)))(*args, *carry.arrays)


def _mm(a, b, *, name, ta=False, tb=False, out_dtype=F32, tm=512, tn=512, tk=512, bias=None, carry=None):
    m, k = (a.shape[1], a.shape[0]) if ta else a.shape
    n = b.shape[0] if tb else b.shape[1]
    assert k == (b.shape[1] if tb else b.shape[0]), (name, a.shape, b.shape)
    tm, tn, tk = min(tm, m), min(tn, n), min(tk, k)
    nk = k // tk
    assert m % tm == 0 and n % tn == 0 and k % tk == 0, (name, m, n, k)

    def body(*refs):
        if bias is None:
            a_ref, b_ref, o_ref, acc = refs
        else:
            a_ref, b_ref, bias_ref, o_ref, acc = refs
        kk = pl.program_id(2)

        @pl.when(kk == 0)
        def _():
            acc[...] = jnp.zeros_like(acc)

        dims = (((0 if ta else 1,), (1 if tb else 0,)), ((), ()))
        acc[...] += lax.dot_general(a_ref[...].astype(MXU_DTYPE), b_ref[...].astype(MXU_DTYPE), dims,
                                    preferred_element_type=F32)

        @pl.when(kk == nk - 1)
        def _():
            r = acc[...]
            if bias is not None:
                r = r + bias_ref[...]
            o_ref[...] = r.astype(out_dtype)

    a_spec = (pl.BlockSpec((tk, tm), lambda i, j, kk: (kk, i)) if ta
              else pl.BlockSpec((tm, tk), lambda i, j, kk: (i, kk)))
    b_spec = (pl.BlockSpec((tn, tk), lambda i, j, kk: (j, kk)) if tb
              else pl.BlockSpec((tk, tn), lambda i, j, kk: (kk, j)))
    in_specs, args = [a_spec, b_spec], [a, b]
    if bias is not None:
        in_specs.append(pl.BlockSpec((1, tn), lambda i, j, kk: (0, j)))
        args.append(bias)
    out = _call(body, name=name, out_shape=[SDS((m, n), out_dtype)], grid=(m // tm, n // tn, nk), in_specs=in_specs,
                out_specs=[pl.BlockSpec((tm, tn), lambda i, j, kk: (i, j))],
                scratch_shapes=[pltpu.VMEM((tm, tn), F32)], args=args, carry=carry)
    return out[0] if carry is None else out


def _silu_rows(c_all):
    def body(c_ref, o_ref):
        o_ref[...] = _silu(c_ref[...])
    return pl.pallas_call(body, name="cond_silu", out_shape=SDS(c_all.shape, F32))(c_all)


def _modulate(x, scale, shift, tb):
    s, d = x.shape

    def body(x_ref, sc_ref, sh_ref, o_ref):
        o_ref[...] = (x_ref[...] * (1.0 + sc_ref[...]) + sh_ref[...]).astype(MXU_DTYPE)

    return pl.pallas_call(body, name="modulate", out_shape=SDS((s, d), MXU_DTYPE), grid=(s // tb,),
                          in_specs=[_rows(tb, d), _const((1, d)), _const((1, d))], out_specs=_rows(tb, d),
                          compiler_params=_params(1))(x, scale, shift)


def _out_ln(ycat, w_out, x, gate, ln_g, ln_b, tb):
    s, d = x.shape

    def body(yc_ref, w_ref, x_ref, gt_ref, g_ref, b_ref, xn_ref, xh_ref, y_ref, rs_ref):
        y = jnp.dot(yc_ref[...], w_ref[...], preferred_element_type=F32)
        res = ALPHA * x_ref[...] + (1.0 + gt_ref[...]) * y
        mu = jnp.mean(res, axis=-1, keepdims=True)
        cen = res - mu
        var = jnp.mean(cen * cen, axis=-1, keepdims=True)
        rstd = lax.rsqrt(var + LN_EPS)
        xhat = cen * rstd
        xn_ref[...] = xhat * g_ref[...] + b_ref[...]
        xh_ref[...] = xhat
        y_ref[...] = y
        rs_ref[...] = rstd

    big = SDS((s, d), F32)
    return pl.pallas_call(
        body, name="out_proj_ln", out_shape=(big, big, big, SDS((s, 1), F32)), grid=(s // tb,),
        in_specs=[_rows(tb, d), pl.BlockSpec((d, d), lambda i: (0, 0), pipeline_mode=pl.Buffered(1)), _rows(tb, d),
                  _const((1, d)), _const((1, d)), _const((1, d))],
        out_specs=(_rows(tb, d), _rows(tb, d), _rows(tb, d), _rows(tb, 1)), compiler_params=_params(1),
    )(ycat, w_out, x, gate, ln_g, ln_b)


def _ln_bwd(dxn, xhat, y, rstd, ln_g, gate, tb):
    s, d = dxn.shape

    def body(dxn_ref, xh_ref, y_ref, rs_ref, g_ref, gt_ref, dy_ref, dxa_ref, dg_ref, db_ref, dgt_ref):
        _init_acc(dg_ref, db_ref, dgt_ref)
        dxn_t, xh = dxn_ref[...], xh_ref[...]
        dxh = dxn_t * g_ref[...]
        dres = rs_ref[...] * (dxh - jnp.mean(dxh, axis=-1, keepdims=True)
                              - xh * jnp.mean(dxh * xh, axis=-1, keepdims=True))
        dy_ref[...] = ((1.0 + gt_ref[...]) * dres).astype(MXU_DTYPE)
        dxa_ref[...] = ALPHA * dres
        dg_ref[...] += _colsum(dxn_t * xh)
        db_ref[...] += _colsum(dxn_t)
        dgt_ref[...] += _colsum(dres * y_ref[...])

    vec = SDS((1, d), F32)
    return pl.pallas_call(
        body, name="ln_bwd", out_shape=(SDS((s, d), MXU_DTYPE), SDS((s, d), F32), vec, vec, vec), grid=(s // tb,),
        in_specs=[_rows(tb, d), _rows(tb, d), _rows(tb, d), _rows(tb, 1), _const((1, d)), _const((1, d))],
        out_specs=(_rows(tb, d), _rows(tb, d), _const((1, d)), _const((1, d)), _const((1, d))),
        compiler_params=_params(1))(dxn, xhat, y, rstd, ln_g, gate)


def _dh_mod_bwd(dproj, w_in, dxa, x, scale, carry=None):
    s, d = dxa.shape
    k = dproj.shape[1]
    tm, tn, tk = min(1024, s), 1024, 1536
    nk = k // tk
    assert s % tm == 0 and d % tn == 0 and k % tk == 0

    def body(a_ref, b_ref, dxa_ref, x_ref, sc_ref, dx_ref, dsh_ref, dsc_ref, acc):
        i, kk = pl.program_id(1), pl.program_id(2)

        @pl.when(kk == 0)
        def _():
            acc[...] = jnp.zeros_like(acc)

        @pl.when((kk == 0) & (i == 0))
        def _():
            dsh_ref[...] = jnp.zeros_like(dsh_ref)
            dsc_ref[...] = jnp.zeros_like(dsc_ref)

        acc[...] += lax.dot_general(a_ref[...], b_ref[...], (((1,), (1,)), ((), ())), preferred_element_type=F32)

        @pl.when(kk == nk - 1)
        def _():
            dh_t = acc[...]
            dx_ref[...] = dxa_ref[...] + dh_t * (1.0 + sc_ref[...])
            dsh_ref[...] += _colsum(dh_t)
            dsc_ref[...] += _colsum(dh_t * x_ref[...])

    tile = pl.BlockSpec((tm, tn), lambda j, i, kk: (i, j))
    vec = pl.BlockSpec((1, tn), lambda j, i, kk: (0, j))
    return _call(
        body, name="dh", out_shape=(SDS((s, d), F32), SDS((1, d), F32), SDS((1, d), F32)),
        grid=(d // tn, s // tm, nk),
        in_specs=[pl.BlockSpec((tm, tk), lambda j, i, kk: (i, kk)), pl.BlockSpec((tn, tk), lambda j, i, kk: (j, kk)),
                  tile, tile, vec],
        out_specs=(tile, vec, vec), scratch_shapes=[pltpu.VMEM((tm, tn), F32)],
        args=(dproj, w_in, dxa, x, scale), carry=carry)


def _out_ln_loss(ycat, w_out, x, gate, ln_g, ln_b, target, tb):
    s, d = x.shape

    def body(yc_ref, w_ref, x_ref, gt_ref, g_ref, b_ref, t_ref, l_ref, dy_ref, dxa_ref, dg_ref, db_ref, dgt_ref):
        _init_acc(l_ref, dg_ref, db_ref, dgt_ref)
        y = jnp.dot(yc_ref[...], w_ref[...], preferred_element_type=F32)
        res = ALPHA * x_ref[...] + (1.0 + gt_ref[...]) * y
        cen = res - jnp.mean(res, axis=-1, keepdims=True)
        rstd = lax.rsqrt(jnp.mean(cen * cen, axis=-1, keepdims=True) + LN_EPS)
        xh = cen * rstd
        err = xh * g_ref[...] + b_ref[...] - t_ref[...]
        l_ref[...] += (0.5 / d) * jnp.sum(err * err, keepdims=True)
        dxn_t = err * (1.0 / d)
        dxh = dxn_t * g_ref[...]
        dres = rstd * (dxh - jnp.mean(dxh, axis=-1, keepdims=True) - xh * jnp.mean(dxh * xh, axis=-1, keepdims=True))
        dy_ref[...] = ((1.0 + gt_ref[...]) * dres).astype(MXU_DTYPE)
        dxa_ref[...] = ALPHA * dres
        dg_ref[...] += _colsum(dxn_t * xh)
        db_ref[...] += _colsum(dxn_t)
        dgt_ref[...] += _colsum(dres * y)

    vec = SDS((1, d), F32)
    return pl.pallas_call(
        body, name="out_proj_ln_loss", out_shape=(SDS((1, 1), F32), SDS((s, d), MXU_DTYPE), SDS((s, d), F32), vec, vec, vec),
        grid=(s // tb,),
        in_specs=[_rows(tb, d), pl.BlockSpec((d, d), lambda i: (0, 0), pipeline_mode=pl.Buffered(1)), _rows(tb, d),
                  _const((1, d)), _const((1, d)), _const((1, d)), _rows(tb, d)],
        out_specs=(_const((1, 1)), _rows(tb, d), _rows(tb, d), _const((1, d)), _const((1, d)), _const((1, d))),
        compiler_params=_params(1))(ycat, w_out, x, gate, ln_g, ln_b, target)


def _conv_taps(u, up, w_ref, width):
    out = w_ref[width - 1:width, :] * u
    for j in range(width - 2, -1, -1):
        out = out + w_ref[j:j + 1, :] * _shift_down(u, up, width - 1 - j)
    return out


def _conv_taps_t(g, gn, w_ref, width):
    out = w_ref[width - 1:width, :] * g
    for j in range(width - 2, -1, -1):
        out = out + w_ref[j:j + 1, :] * _shift_up(g, gn, width - 1 - j)
    return out


def _conv_wgrad(dw_ref, g, u, up, width):
    dw_ref[width - 1:width, :] += _colsum(g * u)
    for j in range(width - 1):
        dw_ref[j:j + 1, :] += _colsum(g * _shift_down(u, up, width - 1 - j))


def _branch_a_fwd(proj, conv_w, tb):
    s = proj.shape[0]

    def body(ab, ac, ax, ag, acp, axp, w_ref, o_ref):
        has_prev = (pl.program_id(0) > 0).astype(F32)
        u = ac[...] * ax[...]
        up = acp[...] * axp[...] * has_prev
        o_ref[...] = (ab[...] * _conv_taps(u, up, w_ref, 3) * _silu(ag[...])).astype(MXU_DTYPE)

    return pl.pallas_call(
        body, name="branch_a_fwd", out_shape=SDS((s, BR), MXU_DTYPE), grid=(s // tb,),
        in_specs=[_rows(tb, BR, CB_AB), _rows(tb, BR, CB_AC), _rows(tb, BR, CB_AX), _rows(tb, BR, CB_AG),
                  _prev8(tb, BR, CB_AC), _prev8(tb, BR, CB_AX), _const((8, BR))],
        out_specs=_rows(tb, BR), compiler_params=_params(1))(proj, proj, proj, proj, proj, proj, conv_w)


def _branch_a_bwd(dycat, proj, conv_w, tb):
    s = proj.shape[0]

    def body(dy, dyn, ab, abn, ag, agn, ac, acp, ax, axp, w_ref, o_ref, dw_ref):
        _init_acc(dw_ref)
        i = pl.program_id(0)
        has_prev = (i > 0).astype(F32)
        has_next = (i < pl.num_programs(0) - 1).astype(F32)
        u = ac[...] * ax[...]
        up = acp[...] * axp[...] * has_prev
        v = _conv_taps(u, up, w_ref, 3)
        sg = _silu(ag[...])
        dv = dy[...] * ab[...] * sg
        dvn = dyn[...] * abn[...] * _silu(agn[...]) * has_next
        du = _conv_taps_t(dv, dvn, w_ref, 3)
        o_ref[:, 0:BR] = (dy[...] * v * sg).astype(MXU_DTYPE)
        o_ref[:, BR:2 * BR] = (du * ax[...]).astype(MXU_DTYPE)
        o_ref[:, 2 * BR:3 * BR] = (du * ac[...]).astype(MXU_DTYPE)
        o_ref[:, 3 * BR:4 * BR] = (dy[...] * ab[...] * v * _dsilu(ag[...])).astype(MXU_DTYPE)
        _conv_wgrad(dw_ref, dv, u, up, 3)

    return pl.pallas_call(
        body, name="branch_a_bwd", out_shape=(SDS((s, 4 * BR), MXU_DTYPE), SDS((8, BR), F32)), grid=(s // tb,),
        in_specs=[_rows(tb, BR, 0), _next8(tb, BR, s, 0),
                  _rows(tb, BR, CB_AB), _next8(tb, BR, s, CB_AB), _rows(tb, BR, CB_AG), _next8(tb, BR, s, CB_AG),
                  _rows(tb, BR, CB_AC), _prev8(tb, BR, CB_AC), _rows(tb, BR, CB_AX), _prev8(tb, BR, CB_AX),
                  _const((8, BR))],
        out_specs=(_rows(tb, 4 * BR), _const((8, BR))), compiler_params=_params(1),
    )(dycat, dycat, proj, proj, proj, proj, proj, proj, proj, proj, conv_w)


def _t5_bucket(dist):
    max_exact = REL_BUCKETS // 2
    nf = jnp.maximum(dist, 1).astype(F32)
    large = max_exact + (jnp.log(nf / max_exact) / math.log(REL_MAX_DIST / max_exact)
                         * (REL_BUCKETS - max_exact)).astype(jnp.int32)
    large = jnp.minimum(large, REL_BUCKETS - 1)
    return jnp.where(dist < max_exact, dist, large)


def _bucket_maps():
    maps = []
    i = jnp.arange(BLK)[:, None]
    j = jnp.arange(2 * BLK)[None, :]
    delta = i + BLK - j
    for window, dil in DILATIONS:
        span = window // dil
        bucket = _t5_bucket(jnp.clip(delta, 0, span) * dil)
        maps.append(jnp.where((delta >= 0) & (delta <= span), bucket, -1))
    return jnp.stack(maps).astype(jnp.int32)


def _bias_tables(rel_bias, buckets):
    n_pat = len(DILATIONS)

    def body(rb_ref, bk_ref, o_ref):
        for g in range(n_pat):
            bk = bk_ref[g]
            for h in range(ATT_HEADS):
                def per_bucket(b, acc):
                    return jnp.where(bk == b, rb_ref[b, h], acc)
                o_ref[g, h] = lax.fori_loop(0, REL_BUCKETS, per_bucket, jnp.full((BLK, 2 * BLK), NEG, F32))

    return pl.pallas_call(
        body, name="bias_tables", out_shape=SDS((n_pat, ATT_HEADS, BLK, 2 * BLK), F32),
        in_specs=[pl.BlockSpec(memory_space=pltpu.SMEM), pl.BlockSpec(memory_space=pltpu.VMEM)],
        compiler_params=_params())(rel_bias, buckets)


def _head_masks():
    lane = lax.broadcasted_iota(jnp.int32, (1, 2 * HEAD_DIM), 1)
    return [(lane < HEAD_DIM).astype(F32), (lane >= HEAD_DIM).astype(F32)]


def _strided(base, size, dil):
    return pl.ds(base, size, stride=dil) if dil > 1 else pl.ds(pl.multiple_of(base, BLK), size)


def _attn_groups(s, dil):
    return max(1, min(1024, s) // (dil * BLK)) if dil == 1 else max(1, min(2048, s) // (dil * BLK))


def _attn_fwd(proj, bias, dil):
    s = proj.shape[0]
    grp = _attn_groups(s, dil)
    u1 = dil * BLK
    unit = grp * u1
    nb = s // unit
    w = 2 * HEAD_DIM
    q0, k0, v0 = (cb * (BR // w) for cb in (CB_Q, CB_K, CB_V))

    def body(q_ref, kc_ref, kp_ref, vc_ref, vp_ref, bias_ref, o_ref, lse_ref, kbuf, vbuf):
        n = pl.program_id(1)
        col = lax.broadcasted_iota(jnp.int32, (1, 2 * BLK), 1)
        masks = _head_masks()
        kbuf[0:u1, :] = kp_ref[...]
        kbuf[u1:, :] = kc_ref[...]
        vbuf[0:u1, :] = vp_ref[...]
        vbuf[u1:, :] = vc_ref[...]

        def per_r(t, carry):
            j = t // dil
            base = j * u1 + t % dil
            rows = _strided(base, BLK, dil)
            no_prev = jnp.where((n == 0) & (j == 0) & (col < BLK), NEG, 0.0)
            q = q_ref[rows, :] * (HEAD_DIM ** -0.5)
            k = kbuf[_strided(base, 2 * BLK, dil), :].astype(MXU_DTYPE)
            v = vbuf[_strided(base, 2 * BLK, dil), :].astype(MXU_DTYPE)
            q2 = jnp.concatenate([q * masks[0], q * masks[1]], axis=0).astype(MXU_DTYPE)
            sc = lax.dot_general(q2, k, (((1,), (1,)), ((), ())), preferred_element_type=F32)
            sc = sc + jnp.concatenate([bias_ref[0], bias_ref[1]], axis=0) + no_prev
            mx = jnp.max(sc, axis=-1, keepdims=True)
            p = jnp.exp(sc - mx)
            l = jnp.sum(p, axis=-1, keepdims=True)
            o2 = jnp.dot((p / l).astype(MXU_DTYPE), v, preferred_element_type=F32)
            lse2 = mx + jnp.log(l)
            o_ref[rows, :] = o2[0:BLK] * masks[0] + o2[BLK:2 * BLK] * masks[1]
            lse_ref[rows, :] = lse2[0:BLK] * masks[0] + lse2[BLK:2 * BLK] * masks[1]
            return carry

        lax.fori_loop(0, grp * dil, per_r, 0, unroll=8)

    cur = lambda c0: pl.BlockSpec((unit, w), lambda hp, n: (n, c0 + hp))
    prev = lambda c0: pl.BlockSpec((u1, w), lambda hp, n: (jnp.maximum(n * grp - 1, 0), c0 + hp))
    out = pl.BlockSpec((unit, w), lambda hp, n: (n, hp))
    return pl.pallas_call(
        body, name=f"attn_fwd_d{dil}", out_shape=(SDS((s, BR), F32), SDS((s, BR), F32)), grid=(BR // w, nb),
        in_specs=[cur(q0), cur(k0), prev(k0), cur(v0), prev(v0),
                  pl.BlockSpec((2, BLK, 2 * BLK), lambda hp, n: (hp, 0, 0))],
        out_specs=(out, out),
        scratch_shapes=[pltpu.VMEM((unit + u1, w), F32), pltpu.VMEM((unit + u1, w), F32)],
        compiler_params=_params(2))(proj, proj, proj, proj, proj, bias)


def _softmax3(l0, l1, l2):
    mx = jnp.maximum(jnp.maximum(l0, l1), l2)
    e0, e1, e2 = jnp.exp(l0 - mx), jnp.exp(l1 - mx), jnp.exp(l2 - mx)
    inv = 1.0 / (e0 + e1 + e2)
    return e0 * inv, e1 * inv, e2 * inv


def _attn_combine(os_, lses, proj, tb):
    s = proj.shape[0]

    def body(o0, o1, o2, l0, l1, l2, bg, y_ref):
        w0, w1, w2 = _softmax3(l0[...], l1[...], l2[...])
        attn = w0 * o0[...] + w1 * o1[...] + w2 * o2[...]
        y_ref[...] = (attn * _silu(bg[...])).astype(MXU_DTYPE)

    return pl.pallas_call(
        body, name="attn_combine", out_shape=SDS((s, BR), MXU_DTYPE), grid=(s // tb,),
        in_specs=[_rows(tb, BR)] * 6 + [_rows(tb, BR, CB_BG)], out_specs=_rows(tb, BR),
        compiler_params=_params(1))(*os_, *lses, proj)


def _attn_bwd_pre(dycat, os_, lses, proj, head_ones, tb):
    s = proj.shape[0]

    def body(dy, o0, o1, o2, l0, l1, l2, bg, e_ref, dbg_ref, do0, do1, do2, dm0, dm1, dm2):
        w0, w1, w2 = _softmax3(l0[...], l1[...], l2[...])
        attn = w0 * o0[...] + w1 * o1[...] + w2 * o2[...]
        dattn = dy[...] * _silu(bg[...])
        dbg_ref[...] = (dy[...] * attn * _dsilu(bg[...])).astype(MXU_DTYPE)
        prod = dattn * attn
        hi = prod.astype(MXU_DTYPE)
        lo = (prod - hi.astype(F32)).astype(MXU_DTYPE)
        tot = (jnp.dot(hi, e_ref[...], preferred_element_type=F32)
               + jnp.dot(lo, e_ref[...], preferred_element_type=F32))
        for wg, do_ref, dm_ref in ((w0, do0, dm0), (w1, do1, dm1), (w2, do2, dm2)):
            do_ref[...] = wg * dattn
            dm_ref[...] = wg * tot

    big = SDS((s, BR), F32)
    return pl.pallas_call(
        body, name="attn_bwd_pre", out_shape=(SDS((s, BR), MXU_DTYPE),) + (big,) * 6, grid=(s // tb,),
        in_specs=[_rows(tb, BR, 1)] + [_rows(tb, BR)] * 6 + [_rows(tb, BR, CB_BG), _const((BR, BR))],
        out_specs=(_rows(tb, BR),) * 7, compiler_params=_params(1))(dycat, *os_, *lses, proj, head_ones)


def _attn_bwd(proj, do, lse, dm, bias, dil, carry=None):
    s = proj.shape[0]
    grp = _attn_groups(s, dil)
    u1 = dil * BLK
    unit = grp * u1
    nb = s // unit
    w = 2 * HEAD_DIM
    q0, k0, v0 = (cb * (BR // w) for cb in (CB_Q, CB_K, CB_V))

    def body(q_ref, kc_ref, kp_ref, vc_ref, vp_ref, do_ref, lse_ref, dm_ref, bias_ref,
             dq_ref, dk_ref, dv_ref, dbias_ref, kbuf, vbuf, stage_k, stage_v):
        n = pl.program_id(1)
        col = lax.broadcasted_iota(jnp.int32, (1, 2 * BLK), 1)
        masks = _head_masks()

        @pl.when(n == 0)
        def _():
            dbias_ref[...] = jnp.zeros_like(dbias_ref)
            stage_k[...] = jnp.zeros_like(stage_k)
            stage_v[...] = jnp.zeros_like(stage_v)

        for out_ref, stage in ((dk_ref, stage_k), (dv_ref, stage_v)):
            if grp > 1:
                out_ref[0:unit - u1, :] = stage[u1:unit, :]
            stage[0:u1, :] = stage[unit:unit + u1, :]

        @pl.when(n < nb)
        def _():
            kbuf[0:u1, :] = kp_ref[...]
            kbuf[u1:, :] = kc_ref[...]
            vbuf[0:u1, :] = vp_ref[...]
            vbuf[u1:, :] = vc_ref[...]

            def per_r(t, carry):
                j = t // dil
                base = j * u1 + t % dil
                rows = _strided(base, BLK, dil)
                rows_hi = _strided(base + u1, BLK, dil)
                no_prev = jnp.where((n == 0) & (j == 0) & (col < BLK), NEG, 0.0)
                q = q_ref[rows, :] * (HEAD_DIM ** -0.5)
                k = kbuf[_strided(base, 2 * BLK, dil), :].astype(MXU_DTYPE)
                v = vbuf[_strided(base, 2 * BLK, dil), :].astype(MXU_DTYPE)
                do_t, lse_t, dm_t = do_ref[rows, :], lse_ref[rows, :], dm_ref[rows, :]
                stack = lambda t: jnp.concatenate([t * masks[0], t * masks[1]], axis=0).astype(MXU_DTYPE)
                per_head = lambda t: jnp.concatenate([t[:, 0:1], t[:, HEAD_DIM:HEAD_DIM + 1]], axis=0)
                q2, do2 = stack(q), stack(do_t)
                sc = lax.dot_general(q2, k, (((1,), (1,)), ((), ())), preferred_element_type=F32)
                p = jnp.exp(sc + jnp.concatenate([bias_ref[0], bias_ref[1]], axis=0) + no_prev - per_head(lse_t))
                dp = lax.dot_general(do2, v, (((1,), (1,)), ((), ())), preferred_element_type=F32)
                ds = p * (dp - per_head(dm_t))
                dbias_ref[0] += ds[0:BLK]
                dbias_ref[1] += ds[BLK:2 * BLK]
                dsb, pb = ds.astype(MXU_DTYPE), p.astype(MXU_DTYPE)
                dq2 = jnp.dot(dsb, k, preferred_element_type=F32)
                dk_acc = lax.dot_general(dsb, q2, (((0,), (0,)), ((), ())), preferred_element_type=F32)
                dv_acc = lax.dot_general(pb, do2, (((0,), (0,)), ((), ())), preferred_element_type=F32)
                dq_ref[rows, :] = (dq2[0:BLK] * masks[0] + dq2[BLK:2 * BLK] * masks[1]) * (HEAD_DIM ** -0.5)
                stage_k[rows, :] = stage_k[rows, :] + dk_acc[0:BLK]
                stage_v[rows, :] = stage_v[rows, :] + dv_acc[0:BLK]
                stage_k[rows_hi, :] = dk_acc[BLK:2 * BLK]
                stage_v[rows_hi, :] = dv_acc[BLK:2 * BLK]
                return carry

            lax.fori_loop(0, grp * dil, per_r, 0, unroll=8)

        dk_ref[unit - u1:unit, :] = stage_k[0:u1, :]
        dv_ref[unit - u1:unit, :] = stage_v[0:u1, :]

    qn = lambda n: jnp.minimum(n, nb - 1)
    cur = lambda c0: pl.BlockSpec((unit, w), lambda hp, n: (qn(n), c0 + hp))
    prev = lambda c0: pl.BlockSpec((u1, w), lambda hp, n: (jnp.maximum(qn(n) * grp - 1, 0), c0 + hp))
    row = pl.BlockSpec((unit, w), lambda hp, n: (qn(n), hp))
    late = pl.BlockSpec((unit, w), lambda hp, n: (jnp.maximum(n - 1, 0), hp))
    tab = pl.BlockSpec((2, BLK, 2 * BLK), lambda hp, n: (hp, 0, 0))
    big = SDS((s, BR), F32)
    return _call(
        body, name=f"attn_bwd_d{dil}", out_shape=(big, big, big, SDS((ATT_HEADS, BLK, 2 * BLK), F32)),
        grid=(BR // w, nb + 1),
        in_specs=[cur(q0), cur(k0), prev(k0), cur(v0), prev(v0), row, row, row, tab],
        out_specs=(row, late, late, tab),
        scratch_shapes=[pltpu.VMEM((unit + u1, w), F32)] * 4,
        args=(proj, proj, proj, proj, proj, do, lse, dm, bias), carry=carry)


def _rel_bias_grad(dbias, buckets):
    def body(db_ref, bk_ref, o_ref):
        row = lax.broadcasted_iota(jnp.int32, (REL_BUCKETS, 128), 0)
        lane = lax.broadcasted_iota(jnp.int32, (REL_BUCKETS, 128), 1)

        def per_bucket(b, acc):
            for g in range(len(DILATIONS)):
                hit = bk_ref[g] == b
                for h in range(ATT_HEADS):
                    both = db_ref[0, g, h] + db_ref[1, g, h]
                    val = jnp.sum(jnp.where(hit, both, 0.0), keepdims=True)
                    acc = acc + jnp.where((row == b) & (lane == h), val, 0.0)
            return acc

        o_ref[...] = lax.fori_loop(0, REL_BUCKETS, per_bucket, jnp.zeros((REL_BUCKETS, 128), F32))

    assert dbias.shape[0] == DEPTH == 2
    return pl.pallas_call(body, name="rel_bias_grad", out_shape=SDS((REL_BUCKETS, 128), F32),
                          compiler_params=_params())(dbias, buckets)


def _scan_real(a, b, *, reverse, tb, name):
    s, ch = a.shape
    nt = s // tb
    order = range(7, -1, -1) if reverse else range(8)

    def body(a_ref, b_ref, o_ref, carry):
        @pl.when(pl.program_id(0) == 0)
        def _():
            carry[...] = jnp.zeros_like(carry)

        def group(gi, h):
            r0 = pl.multiple_of((tb // 8 - 1 - gi if reverse else gi) * 8, 8)
            a8, b8 = a_ref[pl.ds(r0, 8), :], b_ref[pl.ds(r0, 8), :]
            rows = [None] * 8
            for k in order:
                if reverse:
                    rows[k] = b8[k:k + 1] + h
                    h = a8[k:k + 1] * rows[k]
                else:
                    h = a8[k:k + 1] * h + b8[k:k + 1]
                    rows[k] = h
            o_ref[pl.ds(r0, 8), :] = jnp.concatenate(rows, axis=0)
            return h

        carry[...] = lax.fori_loop(0, tb // 8, group, carry[...])

    spec = pl.BlockSpec((tb, ch), (lambda i: (nt - 1 - i, 0)) if reverse else (lambda i: (i, 0)))
    return pl.pallas_call(body, name=name, out_shape=SDS((s, ch), F32), grid=(nt,), in_specs=[spec, spec],
                          out_specs=spec, scratch_shapes=[pltpu.VMEM((1, ch), F32)],
                          compiler_params=_params(1))(a, b)


def _scan_tile(s):
    return min(512, s)


def _load_chunked(ref, t0, pt):
    ln = pt // 8
    return jnp.concatenate([ref[pl.ds(t0 + j, 8, stride=ln), :] for j in range(ln)], axis=0)


def _store_natural(ref, t0, pt, val):
    ln = pt // 8
    for j in range(ln):
        ref[pl.ds(t0 + j, 8, stride=ln), :] = val[j * 8:(j + 1) * 8]


def _scan_tile_in_place(a_ref, x_ref, carry, pw, *, reverse):
    ch2 = x_ref.shape[1]
    ch = ch2 // 2
    ln = x_ref.shape[0] // 8
    ar = a_ref[:, 0:ch]
    ai = -a_ref[:, ch:ch2] if reverse else a_ref[:, ch:ch2]

    def cmul(pr, pi, xr, xi):
        return pr * xr - pi * xi, pr * xi + pi * xr

    @pl.when(pl.program_id(0) == 0)
    def _():
        carry[...] = jnp.zeros_like(carry)

        def fill(j, p):
            pw[pl.ds(j, 1), 0:ch] = p[0]
            pw[pl.ds(j, 1), ch:ch2] = p[1]
            return cmul(ar, ai, *p)

        lax.fori_loop(0, ln, fill, (ar, ai))

    def rows_of(j):
        return pl.ds(pl.multiple_of((ln - 1 - j if reverse else j) * 8, 8), 8)

    def local(j, x):
        rows = rows_of(j)
        nr, ni = cmul(ar, ai, *x)
        xr, xi = nr + x_ref[rows, 0:ch], ni + x_ref[rows, ch:ch2]
        x_ref[rows, 0:ch] = xr
        x_ref[rows, ch:ch2] = xi
        return xr, xi

    zero = jnp.zeros((8, ch), F32)
    er, ei = lax.fori_loop(0, ln, local, (zero, zero), unroll=2)
    apr, api = pw[ln - 1:ln, 0:ch], pw[ln - 1:ln, ch:ch2]
    cr, ci = carry[:, 0:ch], carry[:, ch:ch2]
    into_r, into_i = [None] * 8, [None] * 8
    for c in (range(7, -1, -1) if reverse else range(8)):
        into_r[c], into_i[c] = cr, ci
        pr, pi = cmul(apr, api, cr, ci)
        cr, ci = er[c:c + 1] + pr, ei[c:c + 1] + pi
    carry[:, 0:ch] = cr
    carry[:, ch:ch2] = ci
    into_r, into_i = jnp.concatenate(into_r, axis=0), jnp.concatenate(into_i, axis=0)

    def fix(j, carry_):
        rows = rows_of(j)
        dr, di = cmul(pw[pl.ds(j, 1), 0:ch], pw[pl.ds(j, 1), ch:ch2], into_r, into_i)
        x_ref[rows, 0:ch] += dr
        x_ref[rows, ch:ch2] += di
        return carry_

    lax.fori_loop(0, ln, fix, 0, unroll=2)


def _neg_expm1(z):
    series = -z * (1.0 + z * (0.5 + z * (1.0 / 6 + z * (1.0 / 24 + z * (1.0 / 120)))))
    return jnp.where(z > -0.05, series, 1.0 - jnp.exp(z))


def _lru_gate(xc, pre_r, pre_i, lam):
    log_a = -LRU_C * jax.nn.sigmoid(pre_r) * jax.nn.softplus(-lam)
    return jnp.exp(log_a), jnp.sqrt(_neg_expm1(2.0 * log_a)) * jax.nn.sigmoid(pre_i) * xc


def _lru_gates_fwd(proj, conv_w, conv_b, w_cat, b_cat, lam, tb):
    s = proj.shape[0]

    def body(cx, cxp, w_ref, cb_ref, wc_ref, bc_ref, lam_ref, a_ref, b_ref):
        has_prev = (pl.program_id(0) > 0).astype(F32)
        xc = _conv_taps(cx[...], cxp[...] * has_prev, w_ref, 4) + cb_ref[...]
        pre = jnp.dot(xc.astype(MXU_DTYPE), wc_ref[...], preferred_element_type=F32) + bc_ref[...]
        a_ref[...], b_ref[...] = _lru_gate(xc, pre[:, 0:BR], pre[:, BR:2 * BR], lam_ref[...])

    big = SDS((s, BR), F32)
    return pl.pallas_call(
        body, name="lru_gates_fwd", out_shape=(big, big), grid=(s // tb,),
        in_specs=[_rows(tb, BR, CB_CX), _prev8(tb, BR, CB_CX), _const((8, BR)), _const((1, BR)),
                  _const((BR, 2 * BR)), _const((1, 2 * BR)), _const((1, BR))],
        out_specs=(_rows(tb, BR), _rows(tb, BR)), compiler_params=_params(1),
    )(proj, proj, conv_w, conv_b, w_cat, b_cat, lam)


def _gate_out(h, proj, cb, tb, name):
    s = proj.shape[0]

    def body(h_ref, g_ref, o_ref):
        o_ref[...] = (h_ref[...] * _silu(g_ref[...])).astype(MXU_DTYPE)

    return pl.pallas_call(body, name=name, out_shape=SDS((s, BR), MXU_DTYPE), grid=(s // tb,),
                          in_specs=[_rows(tb, BR), _rows(tb, BR, cb)], out_specs=_rows(tb, BR),
                          compiler_params=_params(1))(h, proj)


def _gate_out_bwd(dycat, dy_cb, h, proj, cb, tb, name):
    s = proj.shape[0]

    def body(dy, h_ref, g_ref, dh_ref, dg_ref):
        dh_ref[...] = dy[...] * _silu(g_ref[...])
        dg_ref[...] = (dy[...] * h_ref[...] * _dsilu(g_ref[...])).astype(MXU_DTYPE)

    return pl.pallas_call(body, name=name, out_shape=(SDS((s, BR), F32), SDS((s, BR), MXU_DTYPE)), grid=(s // tb,),
                          in_specs=[_rows(tb, BR, dy_cb), _rows(tb, BR), _rows(tb, BR, cb)],
                          out_specs=(_rows(tb, BR), _rows(tb, BR)), compiler_params=_params(1))(dycat, h, proj)


def _lru_gates_bwd(proj, lmb, h, conv_w, conv_b, w_cat, b_cat, lam, tb):
    s = proj.shape[0]

    def body(cx, cxp, l_ref, h_ref, hp_ref, w_ref, cb_ref, wc_ref, bc_ref, lam_ref,
             dxc_ref, dpre_ref, xc_ref, dbc_ref, dlam_ref):
        _init_acc(dbc_ref, dlam_ref)
        has_prev = (pl.program_id(0) > 0).astype(F32)
        xc = _conv_taps(cx[...], cxp[...] * has_prev, w_ref, 4) + cb_ref[...]
        xcb = xc.astype(MXU_DTYPE)
        pre = jnp.dot(xcb, wc_ref[...], preferred_element_type=F32) + bc_ref[...]
        _, vjp = jax.vjp(_lru_gate, xc, pre[:, 0:BR], pre[:, BR:2 * BR], lam_ref[...])
        lm = l_ref[...]
        dxc, dpr, dpi, dlam = vjp((lm * _shift_down(h_ref[...], hp_ref[...] * has_prev, 1), lm))
        dpre = jnp.concatenate([dpr, dpi], axis=1)
        dpreb = dpre.astype(MXU_DTYPE)
        dxc_ref[...] = dxc + lax.dot_general(dpreb, wc_ref[...], (((1,), (1,)), ((), ())),
                                             preferred_element_type=F32)
        dpre_ref[...] = dpreb
        xc_ref[...] = xcb
        dbc_ref[...] += _colsum(dpre)
        dlam_ref[...] += dlam

    return pl.pallas_call(
        body, name="lru_gates_bwd",
        out_shape=(SDS((s, BR), F32), SDS((s, 2 * BR), MXU_DTYPE), SDS((s, BR), MXU_DTYPE),
                   SDS((1, 2 * BR), F32), SDS((1, BR), F32)),
        grid=(s // tb,),
        in_specs=[_rows(tb, BR, CB_CX), _prev8(tb, BR, CB_CX), _rows(tb, BR), _rows(tb, BR), _prev8(tb, BR),
                  _const((8, BR)), _const((1, BR)), _const((BR, 2 * BR)), _const((1, 2 * BR)), _const((1, BR))],
        out_specs=(_rows(tb, BR), _rows(tb, 2 * BR), _rows(tb, BR), _const((1, 2 * BR)), _const((1, BR))),
        compiler_params=_params(1))(proj, proj, lmb, h, h, conv_w, conv_b, w_cat, b_cat, lam)


def _conv_c_bwd(dxc, proj, conv_w, tb):
    s = proj.shape[0]

    def body(g, gn, cx, cxp, w_ref, dcx_ref, dw_ref, db_ref):
        _init_acc(dw_ref, db_ref)
        i = pl.program_id(0)
        has_prev = (i > 0).astype(F32)
        has_next = (i < pl.num_programs(0) - 1).astype(F32)
        gt = g[...]
        dcx_ref[...] = _conv_taps_t(gt, gn[...] * has_next, w_ref, 4).astype(MXU_DTYPE)
        _conv_wgrad(dw_ref, gt, cx[...], cxp[...] * has_prev, 4)
        db_ref[...] += _colsum(gt)

    return pl.pallas_call(
        body, name="conv_c_bwd", out_shape=(SDS((s, BR), MXU_DTYPE), SDS((8, BR), F32), SDS((1, BR), F32)),
        grid=(s // tb,),
        in_specs=[_rows(tb, BR), _next8(tb, BR, s), _rows(tb, BR, CB_CX), _prev8(tb, BR, CB_CX), _const((8, BR))],
        out_specs=(_rows(tb, BR), _const((8, BR)), _const((1, BR))), compiler_params=_params(1),
    )(dxc, dxc, proj, proj, conv_w)


def _s5_disc(lam_re, lam_im, log_dt):
    dt = jnp.exp(log_dt)
    mag = jnp.exp(lam_re * dt)
    ab_re = mag * jnp.cos(lam_im * dt)
    ab_im = mag * jnp.sin(lam_im * dt)
    den = lam_re * lam_re + lam_im * lam_im
    f_re = ((ab_re - 1.0) * lam_re + ab_im * lam_im) / den
    f_im = (ab_im * lam_re - (ab_re - 1.0) * lam_im) / den
    return ab_re, ab_im, f_re, f_im


def _s5_bbar(f_re, f_im, b_re, b_im):
    return f_re * b_re - f_im * b_im, f_re * b_im + f_im * b_re


def _s5_disc_fwd(lam_re, lam_im, log_dt):
    def body(lr, li, ld, o0, o1, o2, o3):
        o0[...], o1[...], o2[...], o3[...] = _s5_disc(lr[...], li[...], ld[...])
    return pl.pallas_call(body, name="s5_disc_fwd", out_shape=(SDS(lam_re.shape, F32),) * 4)(lam_re, lam_im, log_dt)


def _s5_disc_bwd(lam_re, lam_im, log_dt, cts):
    def body(lr, li, ld, c0, c1, c2, c3, o0, o1, o2):
        _, vjp = jax.vjp(_s5_disc, lr[...], li[...], ld[...])
        o0[...], o1[...], o2[...] = vjp((c0[...], c1[...], c2[...], c3[...]))
    return pl.pallas_call(body, name="s5_disc_bwd", out_shape=(SDS(lam_re.shape, F32), SDS(lam_re.shape, F32),
                                                                SDS(log_dt.shape, F32)))(lam_re, lam_im, log_dt, *cts)


def _s5_bbar_fwd(f_re, f_im, b_re, b_im):
    def body(fr, fi, br, bi, o0, o1):
        o0[...], o1[...] = _s5_bbar(fr[...], fi[...], br[...], bi[...])
    return pl.pallas_call(body, name="s5_bbar_fwd", out_shape=(SDS(b_re.shape, F32),) * 2)(f_re, f_im, b_re, b_im)


def _s5_bbar_bwd(f_re, f_im, b_re, b_im, d_re, d_im):
    def body(fr, fi, br, bi, dr, di, o0, o1, o2, o3):
        _, vjp = jax.vjp(_s5_bbar, fr[...], fi[...], br[...], bi[...])
        o0[...], o1[...], o2[...], o3[...] = vjp((dr[...], di[...]))
    col, mat = SDS(f_re.shape, F32), SDS(b_re.shape, F32)
    return pl.pallas_call(body, name="s5_bbar_bwd", out_shape=(col, col, mat, mat))(f_re, f_im, b_re, b_im, d_re, d_im)


def _s5_tail_fwd(ylin, proj, d_skip, w_glu, b_glu, tb):
    s = proj.shape[0]

    def body(yl, u, dg, dk, w_ref, b_ref, o_ref):
        g = jax.nn.gelu(yl[...] + dk[...] * u[...])
        t = jnp.dot(g.astype(MXU_DTYPE), w_ref[...], preferred_element_type=F32) + b_ref[...]
        o_ref[...] = (g * jax.nn.sigmoid(t) * _silu(dg[...])).astype(MXU_DTYPE)

    return pl.pallas_call(
        body, name="s5_tail_fwd", out_shape=SDS((s, BR), MXU_DTYPE), grid=(s // tb,),
        in_specs=[_rows(tb, BR), _rows(tb, BR, CB_DU), _rows(tb, BR, CB_DG), _const((1, BR)), _const((BR, BR)),
                  _const((1, BR))],
        out_specs=_rows(tb, BR), compiler_params=_params(1))(ylin, proj, proj, d_skip, w_glu, b_glu)


def _s5_tail_bwd(dycat, ylin, proj, d_skip, w_glu, b_glu, tb):
    s = proj.shape[0]

    def body(dy, yl, u, dg, dk, w_ref, b_ref, dyl_ref, dus_ref, ddg_ref, g_ref, dt_ref, ddk_ref, dbg_ref):
        _init_acc(ddk_ref, dbg_ref)
        g, gelu_vjp = jax.vjp(jax.nn.gelu, yl[...] + dk[...] * u[...])
        gb = g.astype(MXU_DTYPE)
        sg = jax.nn.sigmoid(jnp.dot(gb, w_ref[...], preferred_element_type=F32) + b_ref[...])
        dz = dy[...] * _silu(dg[...])
        ddg_ref[...] = (dy[...] * g * sg * _dsilu(dg[...])).astype(MXU_DTYPE)
        dt = dz * g * sg * (1.0 - sg)
        dtb = dt.astype(MXU_DTYPE)
        dgel = dz * sg + lax.dot_general(dtb, w_ref[...], (((1,), (1,)), ((), ())), preferred_element_type=F32)
        dyv, = gelu_vjp(dgel)
        dyl_ref[...] = dyv
        dus_ref[...] = dyv * dk[...]
        g_ref[...] = gb
        dt_ref[...] = dtb
        ddk_ref[...] += _colsum(dyv * u[...])
        dbg_ref[...] += _colsum(dt)

    big, half, vec = SDS((s, BR), F32), SDS((s, BR), MXU_DTYPE), SDS((1, BR), F32)
    return pl.pallas_call(
        body, name="s5_tail_bwd", out_shape=(big, big, half, half, half, vec, vec), grid=(s // tb,),
        in_specs=[_rows(tb, BR, 3), _rows(tb, BR), _rows(tb, BR, CB_DU), _rows(tb, BR, CB_DG), _const((1, BR)),
                  _const((BR, BR)), _const((1, BR))],
        out_specs=(_rows(tb, BR),) * 5 + (_const((1, BR)), _const((1, BR))), compiler_params=_params(1),
    )(dycat, ylin, proj, proj, d_skip, w_glu, b_glu)


def _assemble_dproj(da, dqkv, dbg, dcx, dcg, du, dus, ddg, tb):
    s = da.shape[0]

    def body(da_ref, q0, q1, q2, k0, k1, k2, v0, v1, v2, dbg_ref, dcx_ref, dcg_ref, du_ref, dus_ref, ddg_ref, o_ref):
        o_ref[:, 0:4 * BR] = da_ref[...]
        for j, parts in enumerate(((q0, q1, q2), (k0, k1, k2), (v0, v1, v2))):
            o_ref[:, (4 + j) * BR:(5 + j) * BR] = (parts[0][...] + parts[1][...] + parts[2][...]).astype(MXU_DTYPE)
        o_ref[:, 7 * BR:8 * BR] = dbg_ref[...].astype(MXU_DTYPE)
        o_ref[:, 8 * BR:9 * BR] = dcx_ref[...].astype(MXU_DTYPE)
        o_ref[:, 9 * BR:10 * BR] = dcg_ref[...].astype(MXU_DTYPE)
        o_ref[:, 10 * BR:11 * BR] = (du_ref[...] + dus_ref[...]).astype(MXU_DTYPE)
        o_ref[:, 11 * BR:12 * BR] = ddg_ref[...].astype(MXU_DTYPE)

    flat = [t for grp in dqkv for t in grp]
    return pl.pallas_call(
        body, name="assemble_dproj", out_shape=SDS((s, N_IN), MXU_DTYPE), grid=(s // tb,),
        in_specs=[_rows(tb, 4 * BR)] + [_rows(tb, BR)] * 15, out_specs=_rows(tb, N_IN),
        compiler_params=_params(1))(da, *flat, dbg, dcx, dcg, du, dus, ddg)


def _sum_leading(xs, tr, name):
    n, _, c = xs[0].shape
    nl = len(xs)
    tr = min([tr] + [x.shape[1] for x in xs])
    assert all(x.shape[1] % tr == 0 for x in xs), (name, tr)
    nrs = [x.shape[1] // tr for x in xs]
    starts = [sum(nrs[:l]) for l in range(nl)]

    def body(*refs):
        i = pl.program_id(0)
        for l in range(nl):
            @pl.when((i >= starts[l]) & (i < starts[l] + nrs[l]))
            def _():
                acc = refs[l * n][...].astype(F32)
                for ref in refs[l * n + 1:(l + 1) * n]:
                    acc = acc + ref[...].astype(F32)
                refs[nl * n][...] = acc

    specs = [pl.BlockSpec((None, tr, c), functools.partial(
        lambda i, k, l: (k, jnp.clip(i - starts[l], 0, nrs[l] - 1), 0), k=k, l=l)) for l in range(nl) for k in range(n)]
    return pl.pallas_call(body, name=name, out_shape=SDS((sum(nrs) * tr, c), F32), grid=(sum(nrs),), in_specs=specs,
                          out_specs=pl.BlockSpec((tr, c), lambda i: (i, 0)),
                          compiler_params=_params(1))(*[x for x in xs for _ in range(n)])


def _adamw(w, g_parts, m, v, tr, name):
    r, c = w.shape
    tr = min(tr, r)
    n = len(g_parts)
    assert r % tr == 0, (name, r, tr)

    def body(*refs):
        w_ref, m_ref, v_ref = refs[0], refs[1 + n], refs[2 + n]
        g_ref, d_ref, nm_ref, nv_ref = refs[3 + n:]
        g = refs[1][...]
        for ref in refs[2:1 + n]:
            g = g + ref[...]
        mm = ADAM_B1 * m_ref[...] + (1.0 - ADAM_B1) * g
        vv = ADAM_B2 * v_ref[...] + (1.0 - ADAM_B2) * jnp.square(g)
        m_hat = mm / (1.0 - ADAM_B1 ** ADAM_STEP)
        v_hat = vv / (1.0 - ADAM_B2 ** ADAM_STEP)
        g_ref[...] = g
        d_ref[...] = -ADAM_LR * (m_hat / (jnp.sqrt(v_hat) + ADAM_EPS) + ADAM_WD * w_ref[...])
        nm_ref[...] = mm
        nv_ref[...] = vv

    spec = pl.BlockSpec((tr, c), lambda i: (i, 0))
    return _call(body, name=name, out_shape=(SDS((r, c), F32),) * 4, grid=(r // tr,), in_specs=[spec] * (3 + n),
                 out_specs=(spec,) * 4, scratch_shapes=[], args=(w, *g_parts, m, v))


class _AllGather8:
    def __init__(self, block):
        self.m_per = block.shape[0]
        self.arrays, self.n_in, self.n_out = [block], 1, 1
        self.out_shapes = (SDS((N_DEV * self.m_per, block.shape[1]), block.dtype),)
        self.scratch = [pltpu.SemaphoreType.DMA((7,)), pltpu.SemaphoreType.DMA((7,)), pltpu.SemaphoreType.DMA]

    def _copies(self, ins, outs, sems):
        (x_ref,), (out_ref,), (send_sems, recv_sems, local_sem) = ins, outs, sems
        x, y, c = lax.axis_index("x"), lax.axis_index("y"), lax.axis_index("c")
        me, sibling = (x, y, c), (x, y, 1 - c)
        chips = [(1 - x, y), (x, 1 - y), (1 - x, 1 - y)]

        def rows(px, py, pc):
            return out_ref.at[pl.ds((4 * px + 2 * py + pc) * self.m_per, self.m_per), :]

        def copy(k, blk, to, src=None):
            return pltpu.make_async_remote_copy(
                src_ref=rows(*blk) if src is None else src, dst_ref=rows(*blk), send_sem=send_sems.at[k],
                recv_sem=recv_sems.at[k], device_id=to, device_id_type=MESH)

        mine = pltpu.make_async_copy(x_ref, rows(*me), local_sem)
        first = [copy(0, me, sibling, src=x_ref)]
        first += [copy(1 + j, me, (*chip, c), src=x_ref) for j, chip in enumerate(chips)]
        passed = [copy(4 + j, (*chip, c), sibling) for j, chip in enumerate(chips)]
        arrivals = [copy(1 + j, (*chip, c), me) for j, chip in enumerate(chips)]
        from_sibling = [copy(0, sibling, me)] + [copy(4 + j, (*chip, 1 - c), me) for j, chip in enumerate(chips)]
        return mine, first, passed, arrivals, from_sibling

    def start(self, ins, outs, sems):
        mine, first, _, _, _ = self._copies(ins, outs, sems)
        mine.start()
        for cp in first:
            cp.start()

    def wait(self, ins, outs, sems):
        mine, first, passed, arrivals, from_sibling = self._copies(ins, outs, sems)
        for arrived, onward in zip(arrivals, passed):
            arrived.wait_recv()
            onward.start()
        for cp in from_sibling:
            cp.wait_recv()
        for cp in first + passed:
            cp.wait_send()
        mine.wait()


def _allgather8(block, name):
    ex = _AllGather8(block)

    def body(x_ref, out_ref, *sems):
        ex.start((x_ref,), (out_ref,), sems)
        ex.wait((x_ref,), (out_ref,), sems)

    return pl.pallas_call(
        body, name=name, out_shape=ex.out_shapes[0], in_specs=[pl.BlockSpec(memory_space=pltpu.VMEM)],
        out_specs=pl.BlockSpec(memory_space=pltpu.VMEM), scratch_shapes=ex.scratch, compiler_params=_params())(block)


class _Exchange:
    def __init__(self, items, out_shapes):
        self.items, self.out_shapes = list(items), tuple(out_shapes)
        self.arrays = [it[0] for it in self.items]
        n = len(self.items)
        self.n_in, self.n_out = n, len(self.out_shapes)
        self.scratch = [pltpu.SemaphoreType.DMA((n * N_CHIPS,)), pltpu.SemaphoreType.DMA((n * N_CHIPS,)),
                        pltpu.SemaphoreType.DMA((n,))]

    def _copies(self, ins, outs, sems, m):
        send_sems, recv_sems, local_sems = sems
        c = lax.axis_index("c")
        others = [j for j in range(N_CHIPS) if j != m]

        def remote(a, src, dst, to, from_):
            return pltpu.make_async_remote_copy(
                src_ref=src, dst_ref=dst, send_sem=send_sems.at[a * N_CHIPS + to],
                recv_sem=recv_sems.at[a * N_CHIPS + from_], device_id=(to // 2, to % 2, c), device_id_type=MESH)

        local, sends, recvs = [], [], []
        for a, (_, oi, src_of, dst_of) in enumerate(self.items):
            local.append(pltpu.make_async_copy(src_of(ins[a], m), dst_of(outs[oi], m), local_sems.at[a]))
            for j in others:
                sends.append(remote(a, src_of(ins[a], j), dst_of(outs[oi], m), j, m))
                recvs.append(remote(a, src_of(ins[a], m), dst_of(outs[oi], j), j, j))
        return local, sends, recvs

    def _on_my_chip(self, fn):
        chip = 2 * lax.axis_index("x") + lax.axis_index("y")
        for m in range(N_CHIPS):
            pl.when(chip == m)(functools.partial(fn, m))

    def start(self, ins, outs, sems):
        def go(m):
            local, sends, _ = self._copies(ins, outs, sems, m)
            for cp in local + sends:
                cp.start()
        self._on_my_chip(go)

    def wait(self, ins, outs, sems):
        def go(m):
            local, sends, recvs = self._copies(ins, outs, sems, m)
            for cp in recvs:
                cp.wait_recv()
            for cp in sends:
                cp.wait_send()
            for cp in local:
                cp.wait()
        self._on_my_chip(go)


def _half_rows(ref, cc):
    h = ref.shape[-2] // 2
    return ref.at[(slice(None),) * (len(ref.shape) - 2) + (pl.ds(cc * h, h), slice(None))]


class _Gather:
    def __init__(self, items, out_shapes):
        self.items, self.out_shapes = list(items), tuple(out_shapes)
        self.arrays = [it[0] for it in self.items]
        n = len(self.items)
        self.n_in, self.n_out = n, len(self.out_shapes)
        self.scratch = [pltpu.SemaphoreType.DMA((n * N_CHIPS,)) for _ in range(4)] + [pltpu.SemaphoreType.DMA((n,))]

    def _copies(self, ins, outs, sems, m, cc):
        ici_send, ici_recv, d2d_send, d2d_recv, local_sems = sems
        others = [j for j in range(N_CHIPS) if j != m]
        local, sends, arrivals, passed_on, from_sibling = [], [], [], [], []
        for a, (_, oi, src_of, dst_of) in enumerate(self.items):
            src, out = src_of(ins[a]), outs[oi]
            local.append(pltpu.make_async_copy(src, dst_of(out, m), local_sems.at[a]))
            for j in others:
                k = a * N_CHIPS + j
                mine_there = _half_rows(dst_of(out, m), cc)
                theirs_here = _half_rows(dst_of(out, j), cc)
                sends.append(pltpu.make_async_remote_copy(
                    src_ref=_half_rows(src, cc), dst_ref=mine_there, send_sem=ici_send.at[k],
                    recv_sem=ici_recv.at[a * N_CHIPS + m], device_id=(j // 2, j % 2, cc), device_id_type=MESH))
                arrivals.append(pltpu.make_async_remote_copy(
                    src_ref=_half_rows(src, cc), dst_ref=theirs_here, send_sem=ici_send.at[k], recv_sem=ici_recv.at[k],
                    device_id=(j // 2, j % 2, cc), device_id_type=MESH))
                passed_on.append(pltpu.make_async_remote_copy(
                    src_ref=theirs_here, dst_ref=theirs_here, send_sem=d2d_send.at[k], recv_sem=d2d_recv.at[k],
                    device_id=(m // 2, m % 2, 1 - cc), device_id_type=MESH))
                other_half = _half_rows(dst_of(out, j), 1 - cc)
                from_sibling.append(pltpu.make_async_remote_copy(
                    src_ref=other_half, dst_ref=other_half, send_sem=d2d_send.at[k], recv_sem=d2d_recv.at[k],
                    device_id=(m // 2, m % 2, 1 - cc), device_id_type=MESH))
        return local, sends, arrivals, passed_on, from_sibling

    def _on_my_core(self, fn):
        chip = 2 * lax.axis_index("x") + lax.axis_index("y")
        c = lax.axis_index("c")
        for m in range(N_CHIPS):
            for cc in range(2):
                pl.when((chip == m) & (c == cc))(functools.partial(fn, m, cc))

    def start(self, ins, outs, sems):
        def go(m, cc):
            local, sends, _, _, _ = self._copies(ins, outs, sems, m, cc)
            for cp in local + sends:
                cp.start()
        self._on_my_core(go)

    def wait(self, ins, outs, sems):
        def go(m, cc):
            local, sends, arrivals, passed_on, from_sibling = self._copies(ins, outs, sems, m, cc)
            for arrived, onward in zip(arrivals, passed_on):
                arrived.wait_recv()
                onward.start()
            for cp in from_sibling:
                cp.wait_recv()
            for cp in sends + passed_on:
                cp.wait_send()
            for cp in local:
                cp.wait()
        self._on_my_core(go)


def _run_exchange(ex, name):
    def body(*refs):
        ins, outs, sems = refs[:ex.n_in], refs[ex.n_in:ex.n_in + ex.n_out], refs[ex.n_in + ex.n_out:]
        ex.start(ins, outs, sems)
        ex.wait(ins, outs, sems)

    return pl.pallas_call(
        body, name=name, out_shape=ex.out_shapes, in_specs=[ANY] * ex.n_in, out_specs=(ANY,) * ex.n_out,
        scratch_shapes=ex.scratch, compiler_params=_params())(*ex.arrays)


def _sibling_swap(arrays, name, also):
    n = len(arrays)

    def body(*refs):
        ins, refs = refs[:n], refs[n:]
        x_ins, refs = refs[:also.n_in], refs[also.n_in:]
        outs, refs = refs[:n], refs[n:]
        x_outs, refs = refs[:also.n_out], refs[also.n_out:]
        send_sems, recv_sems, x_sems = refs[0], refs[1], refs[2:]
        peer = (lax.axis_index("x"), lax.axis_index("y"), 1 - lax.axis_index("c"))
        cps = [pltpu.make_async_remote_copy(src_ref=ins[a], dst_ref=outs[a], send_sem=send_sems.at[a],
                                            recv_sem=recv_sems.at[a], device_id=peer, device_id_type=MESH)
               for a in range(n)]
        also.start(x_ins, x_outs, x_sems)
        for cp in cps:
            cp.start()
        also.wait(x_ins, x_outs, x_sems)
        for cp in cps:
            cp.wait()

    return pl.pallas_call(
        body, name=name, out_shape=tuple(SDS(a.shape, a.dtype) for a in arrays) + also.out_shapes,
        in_specs=[ANY] * (n + also.n_in), out_specs=(ANY,) * (n + also.n_out),
        scratch_shapes=[pltpu.SemaphoreType.DMA((n,)), pltpu.SemaphoreType.DMA((n,))] + also.scratch,
        compiler_params=_params())(*arrays, *also.arrays)


def _block_diag(w):
    h, n, m = w.shape
    eye = jnp.eye(h, dtype=w.dtype)
    return (w[:, :, None, :] * eye[:, None, :, None]).reshape(h * n, h * m)


def _diag_blocks(d, h, col0=0, ncols=None, stacked=1):
    ncols = d.shape[1] - col0 if ncols is None else ncols
    n, m = d.shape[0] // (h * stacked), ncols // h
    lanes = 128
    assert m <= lanes and lanes % m == 0 and col0 % lanes == 0

    def body(d_ref, o_ref):
        for gi in range(h * stacked):
            c = col0 + (gi % h) * m
            chunk = d_ref[gi * n:(gi + 1) * n, c // lanes * lanes:c // lanes * lanes + lanes]
            o_ref[gi * n:(gi + 1) * n, :] = chunk[:, c % lanes:c % lanes + m]

    out = pl.pallas_call(body, name="diag_blocks", out_shape=SDS((stacked * h * n, m), d.dtype),
                         compiler_params=_params())(d)
    return out.reshape(stacked * h, n, m)


S5_CHUNKS = 4
S5_PER = S5_GROUPS // S5_CHUNKS
CH_W = S5_PER * S5_CH
ST_W = S5_PER * S5_STATE


def _bd_stack(mats):
    _, _, n, m = mats.shape
    eye = jnp.eye(S5_PER, dtype=mats.dtype)
    t = mats.reshape(2, S5_CHUNKS, S5_PER, n, m)
    bd = t[:, :, :, :, None, :] * eye[None, None, :, None, :, None]
    return bd.reshape(2 * S5_CHUNKS, S5_PER * n, S5_PER * m).astype(MXU_DTYPE)


def _chunks_chunked(src_ref, buf):
    pt = src_ref.shape[0]
    out = []
    for q in range(S5_CHUNKS):
        buf[q] = src_ref[:, q * CH_W:(q + 1) * CH_W]
        out.append(_load_chunked(buf.at[q], 0, pt).astype(MXU_DTYPE))
    return out


def _expand_into(dst_ref, chunks, w_ref):
    for b in range(2 * S5_CHUNKS):
        dst_ref[:, b * ST_W:(b + 1) * ST_W] = jnp.dot(chunks[b % S5_CHUNKS], w_ref[b], preferred_element_type=F32)


def _reduce_from(src_ref, w_ref, buf, dst_ref):
    pt = src_ref.shape[0]
    for q in range(S5_CHUNKS):
        y = jnp.dot(src_ref[:, q * ST_W:(q + 1) * ST_W].astype(MXU_DTYPE), w_ref[q], preferred_element_type=F32)
        p = S5_CHUNKS + q
        y = y + jnp.dot(src_ref[:, p * ST_W:(p + 1) * ST_W].astype(MXU_DTYPE), w_ref[p], preferred_element_type=F32)
        _store_natural(buf.at[q], 0, pt, y)
        dst_ref[:, q * CH_W:(q + 1) * CH_W] = buf[q]


def _s5_core_fwd(proj, w_bu, w_cx, a_row):
    s = proj.shape[0]
    pt = _scan_tile(s)
    ch2 = 2 * S5_N

    def body(u_ref, wb_ref, wc_ref, a_ref, x_ref, y_ref, carry, pw, buf):
        _expand_into(x_ref, _chunks_chunked(u_ref, buf), wb_ref)
        _scan_tile_in_place(a_ref, x_ref, carry, pw, reverse=False)
        _reduce_from(x_ref, wc_ref, buf, y_ref)

    return pl.pallas_call(
        body, name="s5_core_fwd", out_shape=(SDS((s, ch2), F32), SDS((s, BR), F32)), grid=(s // pt,),
        in_specs=[_rows(pt, BR, CB_DU), _const(w_bu.shape), _const(w_cx.shape), _const((1, ch2))],
        out_specs=(_rows(pt, ch2), _rows(pt, BR)),
        scratch_shapes=[pltpu.VMEM((1, ch2), F32), pltpu.VMEM((pt // 8, ch2), F32),
                        pltpu.VMEM((S5_CHUNKS, pt, CH_W), F32)],
        compiler_params=_params(1))(proj, w_bu, w_cx, a_row)


def _s5_core_bwd(dyl, proj, x, w_dx, w_du, a_row):
    s = proj.shape[0]
    pt = _scan_tile(s)
    nt = s // pt
    ch2 = 2 * S5_N
    ch = S5_N

    def body(dy_ref, u_ref, x_ref, xp_ref, wx_ref, wu_ref, a_ref, du_ref, da_ref, dwb_ref, dwc_ref,
             l_ref, carry, pw, buf, buf2):
        i = pl.program_id(0)
        _init_acc(da_ref, dwb_ref, dwc_ref)
        dy_c = _chunks_chunked(dy_ref, buf)
        u_c = _chunks_chunked(u_ref, buf2)
        _expand_into(l_ref, dy_c, wx_ref)
        _scan_tile_in_place(a_ref, l_ref, carry, pw, reverse=True)
        has_prev = (i < nt - 1).astype(F32)
        row = lax.broadcasted_iota(jnp.int32, (8, ch2), 0)
        first = jnp.where(row == 0, pltpu.roll(xp_ref[...], 1, 0) * has_prev, pltpu.roll(x_ref[pt - 8:pt, :], 1, 0))
        xprev = jnp.concatenate([first, x_ref[0:pt - 8, :]], axis=0)
        lr, li, xr, xi = l_ref[:, 0:ch], l_ref[:, ch:ch2], xprev[:, 0:ch], xprev[:, ch:ch2]
        da_ref[:, 0:ch] += _colsum(lr * xr + li * xi)
        da_ref[:, ch:ch2] += _colsum(li * xr - lr * xi)
        _reduce_from(l_ref, wu_ref, buf, du_ref)
        tn = (((0,), (0,)), ((), ()))
        for b in range(2 * S5_CHUNKS):
            cols, rows = slice(b * ST_W, (b + 1) * ST_W), slice(b * CH_W, (b + 1) * CH_W)
            dwb_ref[rows, :] += lax.dot_general(u_c[b % S5_CHUNKS], l_ref[:, cols].astype(MXU_DTYPE), tn,
                                                preferred_element_type=F32)
            dwc_ref[rows, :] += lax.dot_general(dy_c[b % S5_CHUNKS], x_ref[:, cols].astype(MXU_DTYPE), tn,
                                                preferred_element_type=F32)

    rev = lambda w, cb=0: pl.BlockSpec((pt, w), lambda i: (nt - 1 - i, cb))
    halo = pl.BlockSpec((8, ch2), lambda i: (jnp.maximum((nt - 1 - i) * (pt // 8) - 1, 0), 0))
    wshape = SDS((2 * S5_CHUNKS * CH_W, ST_W), F32)
    return pl.pallas_call(
        body, name="s5_core_bwd", out_shape=(SDS((s, BR), F32), SDS((1, ch2), F32), wshape, wshape), grid=(nt,),
        in_specs=[rev(BR, 0), rev(BR, CB_DU), rev(ch2), halo, _const(w_dx.shape), _const(w_du.shape),
                  _const((1, ch2))],
        out_specs=(rev(BR), _const((1, ch2)), _const(wshape.shape), _const(wshape.shape)),
        scratch_shapes=[pltpu.VMEM((pt, ch2), F32), pltpu.VMEM((1, ch2), F32), pltpu.VMEM((pt // 8, ch2), F32),
                        pltpu.VMEM((S5_CHUNKS, pt, CH_W), F32), pltpu.VMEM((S5_CHUNKS, pt, CH_W), F32)],
        compiler_params=_params(1))(dyl, proj, x, x, w_dx, w_du, a_row)


def _tiles(s):
    return dict(tb=min(512, s), tln=min(256, s))


def _layer_weights(p, l):
    pad8 = lambda w: jnp.pad(w, ((0, 8 - w.shape[0]), (0, 0)))
    return dict(
        conv_a=pad8(p["conv_a"][l]), conv_c=pad8(p["conv_c"][l]), conv_c_b=p["conv_c_b"][l][None],
        w_cat=jnp.concatenate([_block_diag(p["lru_wa"][l]), _block_diag(p["lru_wx"][l])], axis=1).astype(MXU_DTYPE),
        b_cat=jnp.concatenate([p["lru_ba"][l], p["lru_bx"][l]])[None], lam=p["lru_lambda"][l][None],
        lam_re=p["s5_lam_re"][l], lam_im=p["s5_lam_im"][l], log_dt=p["s5_log_dt"][l][:, None],
        b_re=p["s5_b_re"][l].reshape(S5_N, S5_CH), b_im=p["s5_b_im"][l].reshape(S5_N, S5_CH),
        c_re=p["s5_c_re"][l], c_im=p["s5_c_im"][l], d_skip=p["s5_d"][l][None], b_glu=p["s5_b_glu"][l][None],
        ln_g=p["ln_g"][l][None], ln_b=p["ln_b"][l][None])


def _s5_matrices(lw):
    ab_re, ab_im, f_re, f_im = _s5_disc_fwd(lw["lam_re"], lw["lam_im"], lw["log_dt"])
    f_re, f_im = f_re.reshape(S5_N, 1), f_im.reshape(S5_N, 1)
    bb_re, bb_im = _s5_bbar_fwd(f_re, f_im, lw["b_re"], lw["b_im"])
    bb = jnp.stack([bb_re, bb_im]).reshape(2, S5_GROUPS, S5_STATE, S5_CH)
    cc = jnp.stack([lw["c_re"], -lw["c_im"]])
    a_row = jnp.concatenate([ab_re.reshape(1, S5_N), ab_im.reshape(1, S5_N)], axis=1)
    return dict(f_re=f_re, f_im=f_im, a_row=a_row, w_bu=_bd_stack(jnp.swapaxes(bb, 2, 3)), w_du=_bd_stack(bb),
                w_cx=_bd_stack(jnp.swapaxes(cc, 2, 3)), w_dx=_bd_stack(cc))


def _mm_hooked(hook, *args, **kw):
    if hook is None:
        return _mm(*args, **kw)
    out = _mm(*args, carry=hook[0], **kw)
    hook[1](out[1:])
    return out[0]


def _layer_fwd(x, ada, w_in, get_rest, lw, s5m, bias_tabs, hooks=None, target=None):
    s = x.shape[0]
    t = _tiles(s)
    tb = t["tb"]
    shift, scale, gate = ada
    hooks = hooks or {}
    h = _modulate(x, scale, shift, tb)
    proj = _mm_hooked(hooks.get("in_proj"), h, w_in, name="in_proj", tm=1024, tn=1536, tk=D_MODEL)
    w_out, w_glu = get_rest()
    y_a = _branch_a_fwd(proj, lw["conv_a"], tb)
    os_, lses = [], []
    for g, (_, dil) in enumerate(DILATIONS):
        o, lse = _attn_fwd(proj, bias_tabs[g], dil)
        os_.append(o)
        lses.append(lse)
    y_b = _attn_combine(os_, lses, proj, tb)
    lru_a, lru_b = _lru_gates_fwd(proj, lw["conv_c"], lw["conv_c_b"], lw["w_cat"], lw["b_cat"], lw["lam"], tb)
    lru_h = _scan_real(lru_a, lru_b, reverse=False, tb=tb, name="lru_scan")
    y_c = _gate_out(lru_h, proj, CB_CG, tb, "lru_out")
    s5_x, ylin = _s5_core_fwd(proj, s5m["w_bu"], s5m["w_cx"], s5m["a_row"])
    y_d = _s5_tail_fwd(ylin, proj, lw["d_skip"], w_glu, lw["b_glu"], tb)
    ycat = jnp.concatenate([y_a, y_b, y_c, y_d], axis=1)
    saved = dict(x=x, h=h, proj=proj, os=os_, lses=lses, lru_a=lru_a, lru_h=lru_h, s5_x=s5_x, ylin=ylin, ycat=ycat)
    if target is not None:
        loss, *saved["head"] = _out_ln_loss(ycat, w_out, x, gate, lw["ln_g"], lw["ln_b"], target, t["tln"])
        return loss, saved
    x_next, saved["xhat"], saved["y"], saved["rstd"] = _out_ln(ycat, w_out, x, gate, lw["ln_g"], lw["ln_b"], t["tln"])
    return x_next, saved


def _layer_bwd(dxn, sv, ada, w_in, w_out, w_glu, lw, s5m, bias_tabs, head_ones, hooks=None):
    proj = sv["proj"]
    s = proj.shape[0]
    t = _tiles(s)
    tb = t["tb"]
    shift, scale, gate = ada
    g = {}
    hook = lambda name: hooks[name](g) if hooks and name in hooks else None
    if "head" in sv:
        dyb, dxa, g["ln_g"], g["ln_b"], dgate = sv["head"]
    else:
        dyb, dxa, g["ln_g"], g["ln_b"], dgate = _ln_bwd(dxn, sv["xhat"], sv["y"], sv["rstd"], lw["ln_g"], gate,
                                                        t["tln"])
    g["w_out"] = _mm_hooked(hook("dw_out"), sv["ycat"], dyb, name="dw_out", ta=True, out_dtype=WIRE_DTYPE,
                            tm=1024, tn=1024, tk=2048)
    dycat =_mm(dyb, w_out, name="dycat", tb=True, tm=1024, tn=1024, tk=D_MODEL)
    da, dconv_a = _branch_a_bwd(dycat, proj, lw["conv_a"], tb)
    g["conv_a"] = dconv_a[0:3]
    pre = _attn_bwd_pre(dycat, sv["os"], sv["lses"], proj, head_ones, tb)
    dbg, dos, dms = pre[0], pre[1:4], pre[4:7]
    dqkv, dbias = [], []
    for gi, (_, dil) in enumerate(DILATIONS):
        hk = hook(f"attn_bwd_d{dil}")
        dq, dk, dv, dbi, *got = _attn_bwd(proj, dos[gi], sv["lses"][gi], dms[gi], bias_tabs[gi], dil,
                                          carry=hk and hk[0])
        if hk:
            hk[1](got)
        dqkv.append((dq, dk, dv))
        dbias.append(dbi)
    dqkv = list(zip(*dqkv))
    dh, dcg = _gate_out_bwd(dycat, 2, sv["lru_h"], proj, CB_CG, tb, "lru_out_bwd")
    lmb = _scan_real(sv["lru_a"], dh, reverse=True, tb=tb, name="lru_scan_bwd")
    dxc, dpre, xcb, dbcat, dlam = _lru_gates_bwd(proj, lmb, sv["lru_h"], lw["conv_c"], lw["conv_c_b"], lw["w_cat"],
                                                  lw["b_cat"], lw["lam"], tb)
    dwcat = _mm(xcb, dpre, name="dw_lru", ta=True, tn=1024)
    g["lru_wa"] = _diag_blocks(dwcat, LRU_HEADS, 0, BR)
    g["lru_wx"] = _diag_blocks(dwcat, LRU_HEADS, BR, BR)
    g["lru_ba"], g["lru_bx"], g["lru_lambda"] = dbcat[0, 0:BR], dbcat[0, BR:2 * BR], dlam[0]
    dcx, dconv_c, dccb = _conv_c_bwd(dxc, proj, lw["conv_c"], tb)
    g["conv_c"], g["conv_c_b"] = dconv_c[0:4], dccb[0]
    dyl, dus, ddg, gb, dtb, ddk, dbglu = _s5_tail_bwd(dycat, sv["ylin"], proj, lw["d_skip"], w_glu, lw["b_glu"], tb)
    g["s5_d"], g["s5_b_glu"] = ddk[0], dbglu[0]
    g["s5_w_glu"] = _mm(gb, dtb, name="dw_glu", ta=True, out_dtype=WIRE_DTYPE)
    du, dab, dwb8, dwc8 = _s5_core_bwd(dyl, proj, sv["s5_x"], s5m["w_dx"], s5m["w_du"], s5m["a_row"])
    per_group = lambda d8: _diag_blocks(d8, S5_PER, stacked=2 * S5_CHUNKS).reshape(2, S5_GROUPS, S5_CH, S5_STATE)
    dbb, dcc = per_group(dwb8), per_group(dwc8)
    from_bd = lambda half: jnp.swapaxes(dbb[half], 1, 2).reshape(S5_N, S5_CH)
    df_re, df_im, db_re, db_im = _s5_bbar_bwd(s5m["f_re"], s5m["f_im"], lw["b_re"], lw["b_im"],
                                              from_bd(0), from_bd(1))
    shp = (S5_GROUPS, S5_STATE)
    g["s5_lam_re"], g["s5_lam_im"], dlog_dt = _s5_disc_bwd(
        lw["lam_re"], lw["lam_im"], lw["log_dt"],
        (dab[:, 0:S5_N].reshape(shp), dab[:, S5_N:].reshape(shp), df_re.reshape(shp), df_im.reshape(shp)))
    g["s5_log_dt"] = dlog_dt[:, 0]
    g["s5_b_re"] = db_re.reshape(S5_GROUPS, S5_STATE, S5_CH)
    g["s5_b_im"] = db_im.reshape(S5_GROUPS, S5_STATE, S5_CH)
    g["s5_c_re"], g["s5_c_im"] = dcc[0], -dcc[1]
    dproj = _assemble_dproj(da, dqkv, dbg, dcx, dcg, du, dus, ddg, tb)
    g["w_in"] = _mm_hooked(hook("dw_in"), sv["h"], dproj, name="dw_in", ta=True, out_dtype=WIRE_DTYPE,
                           tm=1024, tn=1536, tk=2048)
    hk = hook("dh")
    dx, dshift, dscale, *got = _dh_mod_bwd(dproj, w_in, dxa, sv["x"], scale, carry=hk and hk[0])
    if hk:
        hk[1](got)
    g["ada"] = jnp.concatenate([dshift[0], dscale[0], dgate[0]])
    return dx, g, dbias


SMALL = ("rel_bias", "conv_a", "conv_c", "conv_c_b", "lru_wa", "lru_ba", "lru_wx", "lru_bx", "lru_lambda",
         "s5_lam_re", "s5_lam_im", "s5_log_dt", "s5_b_re", "s5_b_im", "s5_c_re", "s5_c_im", "s5_d", "s5_b_glu",
         "ln_g", "ln_b")
PER_LAYER_SMALL = SMALL[1:]


def _local_step(x, target, ada_rows, w_in, w_out, w_glu, p, comm=None):
    if comm is None:
        get_w_in = lambda l: w_in[l]
        get_rest = lambda l: (w_out[l], w_glu[l])
        fwd_hooks = bwd_hooks = lambda *_: None
    else:
        get_w_in, get_rest, fwd_hooks, bwd_hooks = comm.w_in, comm.rest, comm.fwd_hooks, comm.bwd_hooks
    s = x.shape[0]
    buckets = _bucket_maps()
    bias_tabs = _bias_tables(p["rel_bias"], buckets)
    head_ones = _block_diag(jnp.ones((ATT_HEADS, HEAD_DIM, HEAD_DIM), MXU_DTYPE))
    lws = [_layer_weights(p, l) for l in range(DEPTH)]
    s5ms = [_s5_matrices(lw) for lw in lws]
    adas = [tuple(ada_rows[l, k * D_MODEL:(k + 1) * D_MODEL][None] for k in range(3)) for l in range(DEPTH)]
    saved = []
    for l in range(DEPTH):
        x, sv = _layer_fwd(x, adas[l], get_w_in(l), functools.partial(get_rest, l), lws[l], s5ms[l], bias_tabs,
                           fwd_hooks(l), target if l == DEPTH - 1 else None)
        saved.append(sv)
    loss, dx = x, None
    grads = [None] * DEPTH
    dbias_sum = []
    for l in reversed(range(DEPTH)):
        dx, grads[l], dbias = _layer_bwd(dx, saved[l], adas[l], get_w_in(l), *get_rest(l), lws[l], s5ms[l],
                                         bias_tabs, head_ones, bwd_hooks(l, grads))
        dbias_sum.append(jnp.stack(dbias))
    drel = _rel_bias_grad(jnp.stack(dbias_sum), buckets)[:, 0:ATT_HEADS]
    small = {n: jnp.stack([grads[l][n] for l in range(DEPTH)]) for n in PER_LAYER_SMALL + ("ada",)}
    small["rel_bias"] = drel
    big = {n: [grads[l][n] for l in range(DEPTH)] for n in ("w_in", "w_out", "s5_w_glu")}
    return loss, dx, big, small


PACK_ROWS = 256


def _pack(parts):
    flat = jnp.concatenate([t.reshape(-1).astype(F32) for t in parts])
    n = flat.shape[0]
    rows = -(-n // (PACK_ROWS * 128)) * PACK_ROWS
    return jnp.pad(flat, (0, rows * 128 - n)).reshape(rows, 128)


def _unpack(packed, shapes):
    flat = packed.reshape(packed.shape[:-2] + (-1,))
    out, off = [], 0
    for shp in shapes:
        size = math.prod(shp)
        out.append(flat[..., off:off + size].reshape(flat.shape[:-1] + tuple(shp)))
        off += size
    return out


def _take_cols(t, chip, width):
    return lax.dynamic_slice_in_dim(t, chip * width, width, axis=t.ndim - 1)


class _Comm:
    IN_W, OUT_R, GLU_R = N_IN // N_CHIPS, D_MODEL // N_CHIPS, BR // N_CHIPS

    def __init__(self, w_in_b, w_out_b, w_glu_b):
        assert DEPTH == 2
        self.shards = (w_in_b, w_out_b, w_glu_b)
        in_w = self.IN_W
        self.w_in_full = {0: _run_exchange(_Gather(
            [(w_in_b, 0, lambda ref: ref.at[0], lambda ref, j: ref.at[:, pl.ds(j * in_w, in_w)])],
            [SDS((D_MODEL, N_IN), WIRE_DTYPE)]), "gather_w_in0")[0]}
        self.w_out_full = self.w_glu_full = None
        self.recv = {}

    def w_in(self, l):
        return self.w_in_full[l]

    def rest(self, l):
        return self.w_out_full[l], self.w_glu_full[l]

    def fwd_hooks(self, l):
        if l != 0:
            return None
        w_in_b, w_out_b, w_glu_b = self.shards
        in_w, out_r, glu_r = self.IN_W, self.OUT_R, self.GLU_R
        whole = lambda ref: ref
        items = [(w_out_b, 0, whole, lambda ref, j: ref.at[:, pl.ds(j * out_r, out_r), :]),
                 (w_glu_b, 1, whole, lambda ref, j: ref.at[:, pl.ds(j * glu_r, glu_r), :]),
                 (w_in_b, 2, lambda ref: ref.at[1], lambda ref, j: ref.at[:, pl.ds(j * in_w, in_w)])]
        shapes = [SDS((DEPTH, D_MODEL, D_MODEL), WIRE_DTYPE), SDS((DEPTH, BR, BR), WIRE_DTYPE),
                  SDS((D_MODEL, N_IN), WIRE_DTYPE)]

        def done(got):
            self.w_out_full, self.w_glu_full, self.w_in_full[1] = got

        return {"in_proj": (_Gather(items, shapes), done)}

    W_IN_ROWS = ((0, 1024), (1024, 512), (1536, 512))

    def _scatter(self, parts):
        in_w, out_r, glu_r = self.IN_W, self.OUT_R, self.GLU_R
        items, shapes, keys = [], [], []
        for oi, (name, l, arr, *rows) in enumerate(parts):
            if name == "w_in":
                r0, nr = rows[0] if rows else (0, D_MODEL)
                cut = functools.partial(lambda ref, j, r0, nr: ref.at[pl.ds(r0, nr), pl.ds(j * in_w, in_w)], r0=r0, nr=nr)
                shard = (nr, in_w)
            elif name == "w_out":
                cut, shard = (lambda ref, j: ref.at[pl.ds(j * out_r, out_r), :]), (out_r, D_MODEL)
            else:
                cut, shard = (lambda ref, j: ref.at[pl.ds(j * glu_r, glu_r), :]), (glu_r, BR)
            items.append((arr, oi, cut, lambda ref, j: ref.at[j]))
            shapes.append(SDS((N_CHIPS,) + shard, WIRE_DTYPE))
            keys.append((name, l) + ((rows[0][0],) if rows else ()))

        def done(got):
            self.recv.update(zip(keys, got))

        return _Exchange(items, shapes), done

    def received(self, name):
        return [self.recv[k] for k in sorted(k for k in self.recv if k[0] == name)]

    def bwd_hooks(self, l, grads):
        if l != 0:
            return None
        g1 = grads[1]
        w_in_part = lambda k: (lambda g: self._scatter([("w_in", 1, g1["w_in"], self.W_IN_ROWS[k])]))
        return {"dw_out": lambda g: self._scatter([("w_out", 1, g1["w_out"]), ("s5_w_glu", 1, g1["s5_w_glu"])]),
                "attn_bwd_d16": w_in_part(0), "attn_bwd_d4": w_in_part(1), "attn_bwd_d1": w_in_part(2),
                "dw_in": lambda g: self._scatter([("w_out", 0, g["w_out"]), ("s5_w_glu", 0, g["s5_w_glu"])]),
                "dh": lambda g: self._scatter([("w_in", 0, g["w_in"])])}


def kernel(x, c, rel_bias, w_ada, b_ada, w_in, conv_a, conv_c, conv_c_b, lru_wa, lru_ba, lru_wx, lru_bx, lru_lambda, s5_lam_re, s5_lam_im, s5_log_dt, s5_b_re, s5_b_im, s5_c_re, s5_c_im, s5_d, s5_w_glu, s5_b_glu, w_out, ln_g, ln_b, loss_target, m_rel_bias, m_w_ada, m_b_ada, m_w_in, m_conv_a, m_conv_c, m_conv_c_b, m_lru_wa, m_lru_ba, m_lru_wx, m_lru_bx, m_lru_lambda, m_s5_lam_re, m_s5_lam_im, m_s5_log_dt, m_s5_b_re, m_s5_b_im, m_s5_c_re, m_s5_c_im, m_s5_d, m_s5_w_glu, m_s5_b_glu, m_w_out, m_ln_g, m_ln_b, v_rel_bias, v_w_ada, v_b_ada, v_w_in, v_conv_a, v_conv_c, v_conv_c_b, v_lru_wa, v_lru_ba, v_lru_wx, v_lru_bx, v_lru_lambda, v_s5_lam_re, v_s5_lam_im, v_s5_log_dt, v_s5_b_re, v_s5_b_im, v_s5_c_re, v_s5_c_im, v_s5_d, v_s5_w_glu, v_s5_b_glu, v_w_out, v_ln_g, v_ln_b):
    args = dict(locals())
    names = ("rel_bias", "w_ada", "b_ada", "w_in", "conv_a", "conv_c", "conv_c_b", "lru_wa", "lru_ba", "lru_wx",
             "lru_bx", "lru_lambda", "s5_lam_re", "s5_lam_im", "s5_log_dt", "s5_b_re", "s5_b_im", "s5_c_re", "s5_c_im",
             "s5_d", "s5_w_glu", "s5_b_glu", "w_out", "ln_g", "ln_b")
    w = {n: args[n] for n in names}
    mom = {n: args["m_" + n] for n in names}
    var = {n: args["v_" + n] for n in names}
    chip = 2 * lax.axis_index("x") + lax.axis_index("y")
    me = 2 * chip + lax.axis_index("c")
    ada_w = 3 * D_MODEL // N_CHIPS
    conv_w = BR // N_CHIPS

    comm = _Comm(w["w_in"].astype(WIRE_DTYPE), w["w_out"].astype(WIRE_DTYPE), w["s5_w_glu"].astype(WIRE_DTYPE))

    taps = jnp.concatenate([w["conv_a"].reshape(DEPTH * 3, conv_w), w["conv_c"].reshape(DEPTH * 4, conv_w)])
    first = jnp.concatenate([c, jnp.pad(taps, ((0, 1), (0, D_MODEL - conv_w)))])
    got = _allgather8(first, "gather_c_taps").reshape(N_CHIPS, 2, 16, D_MODEL)
    c_all = got[:, :, 0].reshape(N_DEV, D_MODEL)
    taps_all = jnp.transpose(got[:, 0, 1:1 + DEPTH * 7, 0:conv_w], (1, 0, 2)).reshape(DEPTH * 7, BR)
    conv_a_f = taps_all[0:DEPTH * 3].reshape(DEPTH, 3, BR)
    conv_c_f = taps_all[DEPTH * 3:].reshape(DEPTH, 4, BR)

    cond_all = _silu_rows(c_all)
    ada_part = jnp.stack([_mm(cond_all, w["w_ada"][l], name="ada_fwd", tk=D_MODEL, tn=512,
                              bias=_take_cols(w["b_ada"][l][None], chip, ada_w)) for l in range(DEPTH)])
    ada_all = _allgather8(ada_part.reshape(DEPTH * N_DEV, ada_w), "gather_ada")
    ada_all = ada_all.reshape(N_CHIPS, 2, DEPTH, N_DEV, ada_w)[:, 0]
    ada_rows = lax.dynamic_index_in_dim(ada_all, me, axis=2, keepdims=False)
    ada_rows = jnp.transpose(ada_rows, (1, 0, 2)).reshape(DEPTH, 3 * D_MODEL)

    p = dict(w)
    p["conv_a"], p["conv_c"] = conv_a_f, conv_c_f
    loss, dx, _, small = _local_step(x[0], loss_target[0], ada_rows, None, None, None, p, comm)

    sums = [_sum_leading(comm.received(name), 256, "sum_chips") for name in ("w_in", "w_out", "s5_w_glu")]
    small_names = SMALL + ("ada",)
    small["loss"] = loss
    order = small_names + ("loss",)
    shapes = [small[n].shape for n in order]
    *others, gathered = _sibling_swap(sums, "swap_cores", _AllGather8(_pack([small[n] for n in order])))
    out = {}
    for name, mine, other in zip(("w_in", "w_out", "s5_w_glu"), sums, others):
        shp = w[name].shape
        flat = lambda t: t.reshape(-1, shp[-1])
        res = _adamw(flat(w[name]), [mine, other], flat(mom[name]), flat(var[name]), 128, "adamw_big")
        out[name] = [t.reshape(shp) for t in res]
    gathered = gathered.reshape(N_DEV, -1, 128)
    total = dict(zip(order, _unpack(_sum_leading([gathered], PACK_ROWS, "sum_devices"), shapes)))
    d_ada_all = _unpack(gathered, shapes)[order.index("ada")]
    g_small = {n: total[n] for n in SMALL}
    g_small["conv_a"] = _take_cols(total["conv_a"], chip, conv_w)
    g_small["conv_c"] = _take_cols(total["conv_c"], chip, conv_w)
    g_small["b_ada"] = total["ada"]
    g_w_ada = jnp.stack([_mm(cond_all, _take_cols(d_ada_all[:, l], chip, ada_w), name="dw_ada", ta=True, tn=ada_w)
                         for l in range(DEPTH)])
    upd_names = SMALL + ("b_ada",)
    upd_shapes = [w[n].shape for n in upd_names]
    res = _adamw(_pack([w[n] for n in upd_names]), [_pack([g_small[n] for n in upd_names])],
                 _pack([mom[n] for n in upd_names]), _pack([var[n] for n in upd_names]), PACK_ROWS, "adamw_small")
    for k, t in enumerate(res):
        for n, val in zip(upd_names, _unpack(t, upd_shapes)):
            out.setdefault(n, [None] * 4)[k] = val
    shp = w["w_ada"].shape
    flat = lambda t: t.reshape(-1, shp[-1])
    out["w_ada"] = [t.reshape(shp) for t in _adamw(flat(w["w_ada"]), [flat(g_w_ada)], flat(mom["w_ada"]),
                                                  flat(var["w_ada"]), 128, "adamw_ada")]
    return (total["loss"].reshape(()), dx[None]) + tuple(out[n][k] for k in range(4) for n in names)
```

```python
import functools
import math

import jax
import jax.numpy as jnp
from jax import lax
from jax.experimental import pallas as pl
from jax.experimental.pallas import tpu as pltpu

F32 = jnp.float32
MXU_DTYPE = jnp.bfloat16
WIRE_DTYPE = jnp.bfloat16
SDS = jax.ShapeDtypeStruct
MESH = pl.DeviceIdType.MESH
ANY = pl.BlockSpec(memory_space=pl.ANY)
VMEM_LIMIT = 48 * 1024 * 1024

D_MODEL = 2048
DEPTH = 2
BR = 512
ATT_HEADS = 8
HEAD_DIM = 64
DILATIONS = ((128, 1), (512, 4), (2048, 16))
BLK = 128
REL_BUCKETS = 32
REL_MAX_DIST = 2048
LRU_HEADS = 8
LRU_C = 8.0
S5_CH = 16
S5_GROUPS = 32
S5_STATE = 64
S5_N = S5_GROUPS * S5_STATE
N_IN = 12 * BR
ALPHA = (2 * DEPTH) ** 0.25
LN_EPS = 1e-5
NEG = -1e30
ADAM_LR, ADAM_B1, ADAM_B2, ADAM_EPS, ADAM_WD, ADAM_STEP = 0.001, 0.9, 0.999, 1e-08, 0.01, 10
CB_AB, CB_AC, CB_AX, CB_AG, CB_Q, CB_K, CB_V, CB_BG, CB_CX, CB_CG, CB_DU, CB_DG = range(12)
N_CHIPS = 4
N_DEV = 8


def _params(n_axes=0):
    kw = {"dimension_semantics": ("arbitrary",) * n_axes} if n_axes else {}
    return pltpu.CompilerParams(vmem_limit_bytes=VMEM_LIMIT, **kw)


def _rows(tb, w, cb=0):
    return pl.BlockSpec((tb, w), lambda i: (i, cb))


def _prev8(tb, w, cb=0):
    return pl.BlockSpec((8, w), lambda i: (jnp.maximum(i * (tb // 8) - 1, 0), cb))


def _next8(tb, w, n_rows, cb=0):
    return pl.BlockSpec((8, w), lambda i: (jnp.minimum((i + 1) * (tb // 8), n_rows // 8 - 1), cb))


def _const(shape):
    return pl.BlockSpec(shape, lambda *_: (0,) * len(shape))


def _silu(x):
    return x * jax.nn.sigmoid(x)


def _dsilu(x):
    s = jax.nn.sigmoid(x)
    return s * (1.0 + x * (1.0 - s))


def _shift_down(cur, prev8, j):
    rolled = pltpu.roll(cur, j, 0)
    row = lax.broadcasted_iota(jnp.int32, (8, cur.shape[1]), 0)
    first = jnp.where(row < j, pltpu.roll(prev8, j, 0), rolled[0:8])
    return jnp.concatenate([first, rolled[8:]], axis=0)


def _shift_up(cur, next8, j):
    t = cur.shape[0]
    rolled = pltpu.roll(cur, t - j, 0)
    row = lax.broadcasted_iota(jnp.int32, (8, cur.shape[1]), 0)
    last = jnp.where(row >= 8 - j, pltpu.roll(next8, 8 - j, 0), rolled[t - 8:t])
    return jnp.concatenate([rolled[:t - 8], last], axis=0)


def _colsum(x):
    return jnp.sum(x, axis=0, keepdims=True)


def _init_acc(*refs):
    @pl.when(pl.program_id(0) == 0)
    def _():
        for r in refs:
            r[...] = jnp.zeros_like(r)


def _call(body, *, name, out_shape, grid, in_specs, out_specs, scratch_shapes, args, carry=None):
    out_shape, out_specs, in_specs = tuple(out_shape), tuple(out_specs), list(in_specs)
    scratch_shapes = list(scratch_shapes)
    if carry is None:
        return pl.pallas_call(body, name=name, out_shape=out_shape, grid=grid, in_specs=in_specs, out_specs=out_specs,
                              scratch_shapes=scratch_shapes, compiler_params=_params(len(grid)))(*args)
    n_in, n_out, n_scr = len(in_specs), len(out_shape), len(scratch_shapes)

    def wrapped(*refs):
        ins, refs = refs[:n_in], refs[n_in:]
        x_ins, refs = refs[:carry.n_in], refs[carry.n_in:]
        outs, refs = refs[:n_out], refs[n_out:]
        x_outs, refs = refs[:carry.n_out], refs[carry.n_out:]
        scr, x_sems = refs[:n_scr], refs[n_scr:]
        at = [pl.program_id(d) for d in range(len(grid))]
        first = functools.reduce(lambda p, q: p & q, [i == 0 for i in at])
        last = functools.reduce(lambda p, q: p & q, [i == g - 1 for i, g in zip(at, grid)])
        pl.when(first)(lambda: carry.start(x_ins, x_outs, x_sems))
        body(*ins, *outs, *scr)
        pl.when(last)(lambda: carry.wait(x_ins, x_outs, x_sems))

    return pl.pallas_call(
        wrapped, name=name, out_shape=out_shape + carry.out_shapes, grid=grid, in_specs=in_specs + [ANY] * carry.n_in,
        out_specs=out_specs + (ANY,) * carry.n_out, scratch_shapes=scratch_shapes + carry.scratch,
        compiler_params=_params(len(grid)))(*args, *carry.arrays)


def _mm(a, b, *, name, ta=False, tb=False, out_dtype=F32, tm=512, tn=512, tk=512, bias=None, carry=None):
    m, k = (a.shape[1], a.shape[0]) if ta else a.shape
    n = b.shape[0] if tb else b.shape[1]
    assert k == (b.shape[1] if tb else b.shape[0]), (name, a.shape, b.shape)
    tm, tn, tk = min(tm, m), min(tn, n), min(tk, k)
    nk = k // tk
    assert m % tm == 0 and n % tn == 0 and k % tk == 0, (name, m, n, k)

    def body(*refs):
        if bias is None:
            a_ref, b_ref, o_ref, acc = refs
        else:
            a_ref, b_ref, bias_ref, o_ref, acc = refs
        kk = pl.program_id(2)

        @pl.when(kk == 0)
        def _():
            acc[...] = jnp.zeros_like(acc)

        dims = (((0 if ta else 1,), (1 if tb else 0,)), ((), ()))
        acc[...] += lax.dot_general(a_ref[...].astype(MXU_DTYPE), b_ref[...].astype(MXU_DTYPE), dims,
                                    preferred_element_type=F32)

        @pl.when(kk == nk - 1)
        def _():
            r = acc[...]
            if bias is not None:
                r = r + bias_ref[...]
            o_ref[...] = r.astype(out_dtype)

    a_spec = (pl.BlockSpec((tk, tm), lambda i, j, kk: (kk, i)) if ta
              else pl.BlockSpec((tm, tk), lambda i, j, kk: (i, kk)))
    b_spec = (pl.BlockSpec((tn, tk), lambda i, j, kk: (j, kk)) if tb
              else pl.BlockSpec((tk, tn), lambda i, j, kk: (kk, j)))
    in_specs, args = [a_spec, b_spec], [a, b]
    if bias is not None:
        in_specs.append(pl.BlockSpec((1, tn), lambda i, j, kk: (0, j)))
        args.append(bias)
    out = _call(body, name=name, out_shape=[SDS((m, n), out_dtype)], grid=(m // tm, n // tn, nk), in_specs=in_specs,
                out_specs=[pl.BlockSpec((tm, tn), lambda i, j, kk: (i, j))],
                scratch_shapes=[pltpu.VMEM((tm, tn), F32)], args=args, carry=carry)
    return out[0] if carry is None else out


def _silu_rows(c_all):
    def body(c_ref, o_ref):
        o_ref[...] = _silu(c_ref[...])
    return pl.pallas_call(body, name="cond_silu", out_shape=SDS(c_all.shape, F32))(c_all)


def _modulate(x, scale, shift, tb):
    s, d = x.shape

    def body(x_ref, sc_ref, sh_ref, o_ref):
        o_ref[...] = (x_ref[...] * (1.0 + sc_ref[...]) + sh_ref[...]).astype(MXU_DTYPE)

    return pl.pallas_call(body, name="modulate", out_shape=SDS((s, d), MXU_DTYPE), grid=(s // tb,),
                          in_specs=[_rows(tb, d), _const((1, d)), _const((1, d))], out_specs=_rows(tb, d),
                          compiler_params=_params(1))(x, scale, shift)


def _out_ln(ycat, w_out, x, gate, ln_g, ln_b, next_scale, next_shift, tb):
    s, d = x.shape

    def body(yc_ref, w_ref, x_ref, gt_ref, g_ref, b_ref, sc_ref, sh_ref, xn_ref, xh_ref, y_ref, rs_ref, hn_ref):
        y = jnp.dot(yc_ref[...], w_ref[...], preferred_element_type=F32)
        res = ALPHA * x_ref[...] + (1.0 + gt_ref[...]) * y
        mu = jnp.mean(res, axis=-1, keepdims=True)
        cen = res - mu
        var = jnp.mean(cen * cen, axis=-1, keepdims=True)
        rstd = lax.rsqrt(var + LN_EPS)
        xhat = cen * rstd
        xn = xhat * g_ref[...] + b_ref[...]
        xn_ref[...] = xn
        xh_ref[...] = xhat
        y_ref[...] = y
        rs_ref[...] = rstd
        hn_ref[...] = (xn * (1.0 + sc_ref[...]) + sh_ref[...]).astype(MXU_DTYPE)

    big = SDS((s, d), F32)
    return pl.pallas_call(
        body, name="out_proj_ln", out_shape=(big, big, big, SDS((s, 1), F32), SDS((s, d), MXU_DTYPE)), grid=(s // tb,),
        in_specs=[_rows(tb, d), pl.BlockSpec((d, d), lambda i: (0, 0), pipeline_mode=pl.Buffered(1)), _rows(tb, d)]
        + [_const((1, d))] * 5,
        out_specs=(_rows(tb, d), _rows(tb, d), _rows(tb, d), _rows(tb, 1), _rows(tb, d)), compiler_params=_params(1),
    )(ycat, w_out, x, gate, ln_g, ln_b, next_scale, next_shift)


def _ln_bwd(dxn, xhat, y, rstd, ln_g, gate, tb):
    s, d = dxn.shape

    def body(dxn_ref, xh_ref, y_ref, rs_ref, g_ref, gt_ref, dy_ref, dxa_ref, dg_ref, db_ref, dgt_ref):
        _init_acc(dg_ref, db_ref, dgt_ref)
        dxn_t, xh = dxn_ref[...], xh_ref[...]
        dxh = dxn_t * g_ref[...]
        dres = rs_ref[...] * (dxh - jnp.mean(dxh, axis=-1, keepdims=True)
                              - xh * jnp.mean(dxh * xh, axis=-1, keepdims=True))
        dy_ref[...] = ((1.0 + gt_ref[...]) * dres).astype(MXU_DTYPE)
        dxa_ref[...] = ALPHA * dres
        dg_ref[...] += _colsum(dxn_t * xh)
        db_ref[...] += _colsum(dxn_t)
        dgt_ref[...] += _colsum(dres * y_ref[...])

    vec = SDS((1, d), F32)
    return pl.pallas_call(
        body, name="ln_bwd", out_shape=(SDS((s, d), MXU_DTYPE), SDS((s, d), F32), vec, vec, vec), grid=(s // tb,),
        in_specs=[_rows(tb, d), _rows(tb, d), _rows(tb, d), _rows(tb, 1), _const((1, d)), _const((1, d))],
        out_specs=(_rows(tb, d), _rows(tb, d), _const((1, d)), _const((1, d)), _const((1, d))),
        compiler_params=_params(1))(dxn, xhat, y, rstd, ln_g, gate)


def _dh_mod_bwd(dproj, w_in, dxa, x, scale, carry=None):
    s, d = dxa.shape
    k = dproj.shape[1]
    tm, tn, tk = min(1024, s), 1024, 1536
    nk = k // tk
    assert s % tm == 0 and d % tn == 0 and k % tk == 0

    def body(a_ref, b_ref, dxa_ref, x_ref, sc_ref, dx_ref, dsh_ref, dsc_ref, acc):
        i, kk = pl.program_id(1), pl.program_id(2)

        @pl.when(kk == 0)
        def _():
            acc[...] = jnp.zeros_like(acc)

        @pl.when((kk == 0) & (i == 0))
        def _():
            dsh_ref[...] = jnp.zeros_like(dsh_ref)
            dsc_ref[...] = jnp.zeros_like(dsc_ref)

        acc[...] += lax.dot_general(a_ref[...], b_ref[...], (((1,), (1,)), ((), ())), preferred_element_type=F32)

        @pl.when(kk == nk - 1)
        def _():
            dh_t = acc[...]
            dx_ref[...] = dxa_ref[...] + dh_t * (1.0 + sc_ref[...])
            dsh_ref[...] += _colsum(dh_t)
            dsc_ref[...] += _colsum(dh_t * x_ref[...])

    tile = pl.BlockSpec((tm, tn), lambda j, i, kk: (i, j))
    vec = pl.BlockSpec((1, tn), lambda j, i, kk: (0, j))
    return _call(
        body, name="dh", out_shape=(SDS((s, d), F32), SDS((1, d), F32), SDS((1, d), F32)),
        grid=(d // tn, s // tm, nk),
        in_specs=[pl.BlockSpec((tm, tk), lambda j, i, kk: (i, kk)), pl.BlockSpec((tn, tk), lambda j, i, kk: (j, kk)),
                  tile, tile, vec],
        out_specs=(tile, vec, vec), scratch_shapes=[pltpu.VMEM((tm, tn), F32)],
        args=(dproj, w_in, dxa, x, scale), carry=carry)


def _out_ln_loss(ycat, w_out, x, gate, ln_g, ln_b, target, tb):
    s, d = x.shape

    def body(yc_ref, w_ref, x_ref, gt_ref, g_ref, b_ref, t_ref, l_ref, dy_ref, dxa_ref, dg_ref, db_ref, dgt_ref):
        _init_acc(l_ref, dg_ref, db_ref, dgt_ref)
        y = jnp.dot(yc_ref[...], w_ref[...], preferred_element_type=F32)
        res = ALPHA * x_ref[...] + (1.0 + gt_ref[...]) * y
        cen = res - jnp.mean(res, axis=-1, keepdims=True)
        rstd = lax.rsqrt(jnp.mean(cen * cen, axis=-1, keepdims=True) + LN_EPS)
        xh = cen * rstd
        err = xh * g_ref[...] + b_ref[...] - t_ref[...]
        l_ref[...] += (0.5 / d) * jnp.sum(err * err, keepdims=True)
        dxn_t = err * (1.0 / d)
        dxh = dxn_t * g_ref[...]
        dres = rstd * (dxh - jnp.mean(dxh, axis=-1, keepdims=True) - xh * jnp.mean(dxh * xh, axis=-1, keepdims=True))
        dy_ref[...] = ((1.0 + gt_ref[...]) * dres).astype(MXU_DTYPE)
        dxa_ref[...] = ALPHA * dres
        dg_ref[...] += _colsum(dxn_t * xh)
        db_ref[...] += _colsum(dxn_t)
        dgt_ref[...] += _colsum(dres * y)

    vec = SDS((1, d), F32)
    return pl.pallas_call(
        body, name="out_proj_ln_loss", out_shape=(SDS((1, 1), F32), SDS((s, d), MXU_DTYPE), SDS((s, d), F32), vec, vec, vec),
        grid=(s // tb,),
        in_specs=[_rows(tb, d), pl.BlockSpec((d, d), lambda i: (0, 0), pipeline_mode=pl.Buffered(1)), _rows(tb, d),
                  _const((1, d)), _const((1, d)), _const((1, d)), _rows(tb, d)],
        out_specs=(_const((1, 1)), _rows(tb, d), _rows(tb, d), _const((1, d)), _const((1, d)), _const((1, d))),
        compiler_params=_params(1))(ycat, w_out, x, gate, ln_g, ln_b, target)


def _conv_taps(u, up, w_ref, width):
    out = w_ref[width - 1:width, :] * u
    for j in range(width - 2, -1, -1):
        out = out + w_ref[j:j + 1, :] * _shift_down(u, up, width - 1 - j)
    return out


def _conv_taps_t(g, gn, w_ref, width):
    out = w_ref[width - 1:width, :] * g
    for j in range(width - 2, -1, -1):
        out = out + w_ref[j:j + 1, :] * _shift_up(g, gn, width - 1 - j)
    return out


def _conv_wgrad(dw_ref, g, u, up, width):
    dw_ref[width - 1:width, :] += _colsum(g * u)
    for j in range(width - 1):
        dw_ref[j:j + 1, :] += _colsum(g * _shift_down(u, up, width - 1 - j))


def _branch_a_fwd(proj, conv_w, tb):
    s = proj.shape[0]

    def body(ab, ac, ax, ag, acp, axp, w_ref, o_ref):
        has_prev = (pl.program_id(0) > 0).astype(F32)
        u = ac[...] * ax[...]
        up = acp[...] * axp[...] * has_prev
        o_ref[...] = (ab[...] * _conv_taps(u, up, w_ref, 3) * _silu(ag[...])).astype(MXU_DTYPE)

    return pl.pallas_call(
        body, name="branch_a_fwd", out_shape=SDS((s, BR), MXU_DTYPE), grid=(s // tb,),
        in_specs=[_rows(tb, BR, CB_AB), _rows(tb, BR, CB_AC), _rows(tb, BR, CB_AX), _rows(tb, BR, CB_AG),
                  _prev8(tb, BR, CB_AC), _prev8(tb, BR, CB_AX), _const((8, BR))],
        out_specs=_rows(tb, BR), compiler_params=_params(1))(proj, proj, proj, proj, proj, proj, conv_w)


def _branch_a_bwd(dycat, proj, conv_w, tb):
    s = proj.shape[0]

    def body(dy, dyn, ab, abn, ag, agn, ac, acp, ax, axp, w_ref, o_ref, dw_ref):
        _init_acc(dw_ref)
        i = pl.program_id(0)
        has_prev = (i > 0).astype(F32)
        has_next = (i < pl.num_programs(0) - 1).astype(F32)
        u = ac[...] * ax[...]
        up = acp[...] * axp[...] * has_prev
        v = _conv_taps(u, up, w_ref, 3)
        sg = _silu(ag[...])
        dv = dy[...] * ab[...] * sg
        dvn = dyn[...] * abn[...] * _silu(agn[...]) * has_next
        du = _conv_taps_t(dv, dvn, w_ref, 3)
        o_ref[:, 0:BR] = (dy[...] * v * sg).astype(MXU_DTYPE)
        o_ref[:, BR:2 * BR] = (du * ax[...]).astype(MXU_DTYPE)
        o_ref[:, 2 * BR:3 * BR] = (du * ac[...]).astype(MXU_DTYPE)
        o_ref[:, 3 * BR:4 * BR] = (dy[...] * ab[...] * v * _dsilu(ag[...])).astype(MXU_DTYPE)
        _conv_wgrad(dw_ref, dv, u, up, 3)

    return pl.pallas_call(
        body, name="branch_a_bwd", out_shape=(SDS((s, 4 * BR), MXU_DTYPE), SDS((8, BR), F32)), grid=(s // tb,),
        in_specs=[_rows(tb, BR, 0), _next8(tb, BR, s, 0),
                  _rows(tb, BR, CB_AB), _next8(tb, BR, s, CB_AB), _rows(tb, BR, CB_AG), _next8(tb, BR, s, CB_AG),
                  _rows(tb, BR, CB_AC), _prev8(tb, BR, CB_AC), _rows(tb, BR, CB_AX), _prev8(tb, BR, CB_AX),
                  _const((8, BR))],
        out_specs=(_rows(tb, 4 * BR), _const((8, BR))), compiler_params=_params(1),
    )(dycat, dycat, proj, proj, proj, proj, proj, proj, proj, proj, conv_w)


def _t5_bucket(dist):
    max_exact = REL_BUCKETS // 2
    nf = jnp.maximum(dist, 1).astype(F32)
    large = max_exact + (jnp.log(nf / max_exact) / math.log(REL_MAX_DIST / max_exact)
                         * (REL_BUCKETS - max_exact)).astype(jnp.int32)
    large = jnp.minimum(large, REL_BUCKETS - 1)
    return jnp.where(dist < max_exact, dist, large)


def _bucket_maps():
    maps = []
    i = jnp.arange(BLK)[:, None]
    j = jnp.arange(2 * BLK)[None, :]
    delta = i + BLK - j
    for window, dil in DILATIONS:
        span = window // dil
        bucket = _t5_bucket(jnp.clip(delta, 0, span) * dil)
        maps.append(jnp.where((delta >= 0) & (delta <= span), bucket, -1))
    return jnp.stack(maps).astype(jnp.int32)


def _bias_tables(rel_bias, buckets):
    n_pat = len(DILATIONS)

    def body(rb_ref, bk_ref, o_ref):
        for g in range(n_pat):
            bk = bk_ref[g]
            for h in range(ATT_HEADS):
                def per_bucket(b, acc):
                    return jnp.where(bk == b, rb_ref[b, h], acc)
                o_ref[g, h] = lax.fori_loop(0, REL_BUCKETS, per_bucket, jnp.full((BLK, 2 * BLK), NEG, F32))

    return pl.pallas_call(
        body, name="bias_tables", out_shape=SDS((n_pat, ATT_HEADS, BLK, 2 * BLK), F32),
        in_specs=[pl.BlockSpec(memory_space=pltpu.SMEM), pl.BlockSpec(memory_space=pltpu.VMEM)],
        compiler_params=_params())(rel_bias, buckets)


def _head_masks():
    lane = lax.broadcasted_iota(jnp.int32, (1, 2 * HEAD_DIM), 1)
    return [(lane < HEAD_DIM).astype(F32), (lane >= HEAD_DIM).astype(F32)]


def _strided(base, size, dil):
    return pl.ds(base, size, stride=dil) if dil > 1 else pl.ds(pl.multiple_of(base, BLK), size)


def _attn_groups(s, dil):
    return max(1, min(1024, s) // (dil * BLK)) if dil == 1 else max(1, min(2048, s) // (dil * BLK))


def _attn_fwd(proj, bias, dil):
    s = proj.shape[0]
    grp = _attn_groups(s, dil)
    u1 = dil * BLK
    unit = grp * u1
    nb = s // unit
    w = 2 * HEAD_DIM
    q0, k0, v0 = (cb * (BR // w) for cb in (CB_Q, CB_K, CB_V))

    def body(q_ref, kc_ref, kp_ref, vc_ref, vp_ref, bias_ref, o_ref, lse_ref, kbuf, vbuf):
        n = pl.program_id(1)
        col = lax.broadcasted_iota(jnp.int32, (1, 2 * BLK), 1)
        masks = _head_masks()
        kbuf[0:u1, :] = kp_ref[...]
        kbuf[u1:, :] = kc_ref[...]
        vbuf[0:u1, :] = vp_ref[...]
        vbuf[u1:, :] = vc_ref[...]

        def per_r(t, carry):
            j = t // dil
            base = j * u1 + t % dil
            rows = _strided(base, BLK, dil)
            no_prev = jnp.where((n == 0) & (j == 0) & (col < BLK), NEG, 0.0)
            q = q_ref[rows, :] * (HEAD_DIM ** -0.5)
            k = kbuf[_strided(base, 2 * BLK, dil), :].astype(MXU_DTYPE)
            v = vbuf[_strided(base, 2 * BLK, dil), :].astype(MXU_DTYPE)
            q2 = jnp.concatenate([q * masks[0], q * masks[1]], axis=0).astype(MXU_DTYPE)
            sc = lax.dot_general(q2, k, (((1,), (1,)), ((), ())), preferred_element_type=F32)
            sc = sc + jnp.concatenate([bias_ref[0], bias_ref[1]], axis=0) + no_prev
            mx = jnp.max(sc, axis=-1, keepdims=True)
            p = jnp.exp(sc - mx)
            l = jnp.sum(p, axis=-1, keepdims=True)
            o2 = jnp.dot((p / l).astype(MXU_DTYPE), v, preferred_element_type=F32)
            lse2 = mx + jnp.log(l)
            o_ref[rows, :] = o2[0:BLK] * masks[0] + o2[BLK:2 * BLK] * masks[1]
            lse_ref[rows, :] = lse2[0:BLK] * masks[0] + lse2[BLK:2 * BLK] * masks[1]
            return carry

        lax.fori_loop(0, grp * dil, per_r, 0, unroll=8)

    cur = lambda c0: pl.BlockSpec((unit, w), lambda hp, n: (n, c0 + hp))
    prev = lambda c0: pl.BlockSpec((u1, w), lambda hp, n: (jnp.maximum(n * grp - 1, 0), c0 + hp))
    out = pl.BlockSpec((unit, w), lambda hp, n: (n, hp))
    return pl.pallas_call(
        body, name=f"attn_fwd_d{dil}", out_shape=(SDS((s, BR), F32), SDS((s, BR), F32)), grid=(BR // w, nb),
        in_specs=[cur(q0), cur(k0), prev(k0), cur(v0), prev(v0),
                  pl.BlockSpec((2, BLK, 2 * BLK), lambda hp, n: (hp, 0, 0))],
        out_specs=(out, out),
        scratch_shapes=[pltpu.VMEM((unit + u1, w), F32), pltpu.VMEM((unit + u1, w), F32)],
        compiler_params=_params(2))(proj, proj, proj, proj, proj, bias)


def _softmax3(l0, l1, l2):
    mx = jnp.maximum(jnp.maximum(l0, l1), l2)
    e0, e1, e2 = jnp.exp(l0 - mx), jnp.exp(l1 - mx), jnp.exp(l2 - mx)
    inv = 1.0 / (e0 + e1 + e2)
    return e0 * inv, e1 * inv, e2 * inv


def _attn_combine(os_, lses, proj, tb):
    s = proj.shape[0]

    def body(o0, o1, o2, l0, l1, l2, bg, y_ref):
        w0, w1, w2 = _softmax3(l0[...], l1[...], l2[...])
        attn = w0 * o0[...] + w1 * o1[...] + w2 * o2[...]
        y_ref[...] = (attn * _silu(bg[...])).astype(MXU_DTYPE)

    return pl.pallas_call(
        body, name="attn_combine", out_shape=SDS((s, BR), MXU_DTYPE), grid=(s // tb,),
        in_specs=[_rows(tb, BR)] * 6 + [_rows(tb, BR, CB_BG)], out_specs=_rows(tb, BR),
        compiler_params=_params(1))(*os_, *lses, proj)


def _attn_bwd_pre(dycat, os_, lses, proj, head_ones, tb):
    s = proj.shape[0]

    def body(dy, o0, o1, o2, l0, l1, l2, bg, e_ref, dbg_ref, do0, do1, do2, dm0, dm1, dm2):
        w0, w1, w2 = _softmax3(l0[...], l1[...], l2[...])
        attn = w0 * o0[...] + w1 * o1[...] + w2 * o2[...]
        dattn = dy[...] * _silu(bg[...])
        dbg_ref[...] = (dy[...] * attn * _dsilu(bg[...])).astype(MXU_DTYPE)
        prod = dattn * attn
        hi = prod.astype(MXU_DTYPE)
        lo = (prod - hi.astype(F32)).astype(MXU_DTYPE)
        tot = (jnp.dot(hi, e_ref[...], preferred_element_type=F32)
               + jnp.dot(lo, e_ref[...], preferred_element_type=F32))
        for wg, do_ref, dm_ref in ((w0, do0, dm0), (w1, do1, dm1), (w2, do2, dm2)):
            do_ref[...] = wg * dattn
            dm_ref[...] = wg * tot

    big = SDS((s, BR), F32)
    return pl.pallas_call(
        body, name="attn_bwd_pre", out_shape=(SDS((s, BR), MXU_DTYPE),) + (big,) * 6, grid=(s // tb,),
        in_specs=[_rows(tb, BR, 1)] + [_rows(tb, BR)] * 6 + [_rows(tb, BR, CB_BG), _const((BR, BR))],
        out_specs=(_rows(tb, BR),) * 7, compiler_params=_params(1))(dycat, *os_, *lses, proj, head_ones)


def _attn_bwd(proj, do, lse, dm, bias, dil, carry=None):
    s = proj.shape[0]
    grp = _attn_groups(s, dil)
    u1 = dil * BLK
    unit = grp * u1
    nb = s // unit
    w = 2 * HEAD_DIM
    q0, k0, v0 = (cb * (BR // w) for cb in (CB_Q, CB_K, CB_V))

    def body(q_ref, kc_ref, kp_ref, vc_ref, vp_ref, do_ref, lse_ref, dm_ref, bias_ref,
             dq_ref, dk_ref, dv_ref, dbias_ref, kbuf, vbuf, stage_k, stage_v):
        n = pl.program_id(1)
        col = lax.broadcasted_iota(jnp.int32, (1, 2 * BLK), 1)
        masks = _head_masks()

        @pl.when(n == 0)
        def _():
            dbias_ref[...] = jnp.zeros_like(dbias_ref)
            stage_k[...] = jnp.zeros_like(stage_k)
            stage_v[...] = jnp.zeros_like(stage_v)

        for out_ref, stage in ((dk_ref, stage_k), (dv_ref, stage_v)):
            if grp > 1:
                out_ref[0:unit - u1, :] = stage[u1:unit, :]
            stage[0:u1, :] = stage[unit:unit + u1, :]

        @pl.when(n < nb)
        def _():
            kbuf[0:u1, :] = kp_ref[...]
            kbuf[u1:, :] = kc_ref[...]
            vbuf[0:u1, :] = vp_ref[...]
            vbuf[u1:, :] = vc_ref[...]

            def per_r(t, carry):
                j = t // dil
                base = j * u1 + t % dil
                rows = _strided(base, BLK, dil)
                rows_hi = _strided(base + u1, BLK, dil)
                no_prev = jnp.where((n == 0) & (j == 0) & (col < BLK), NEG, 0.0)
                q = q_ref[rows, :] * (HEAD_DIM ** -0.5)
                k = kbuf[_strided(base, 2 * BLK, dil), :].astype(MXU_DTYPE)
                v = vbuf[_strided(base, 2 * BLK, dil), :].astype(MXU_DTYPE)
                do_t, lse_t, dm_t = do_ref[rows, :], lse_ref[rows, :], dm_ref[rows, :]
                stack = lambda t: jnp.concatenate([t * masks[0], t * masks[1]], axis=0).astype(MXU_DTYPE)
                per_head = lambda t: jnp.concatenate([t[:, 0:1], t[:, HEAD_DIM:HEAD_DIM + 1]], axis=0)
                q2, do2 = stack(q), stack(do_t)
                sc = lax.dot_general(q2, k, (((1,), (1,)), ((), ())), preferred_element_type=F32)
                p = jnp.exp(sc + jnp.concatenate([bias_ref[0], bias_ref[1]], axis=0) + no_prev - per_head(lse_t))
                dp = lax.dot_general(do2, v, (((1,), (1,)), ((), ())), preferred_element_type=F32)
                ds = p * (dp - per_head(dm_t))
                dbias_ref[0] += ds[0:BLK]
                dbias_ref[1] += ds[BLK:2 * BLK]
                dsb, pb = ds.astype(MXU_DTYPE), p.astype(MXU_DTYPE)
                dq2 = jnp.dot(dsb, k, preferred_element_type=F32)
                dk_acc = lax.dot_general(dsb, q2, (((0,), (0,)), ((), ())), preferred_element_type=F32)
                dv_acc = lax.dot_general(pb, do2, (((0,), (0,)), ((), ())), preferred_element_type=F32)
                dq_ref[rows, :] = (dq2[0:BLK] * masks[0] + dq2[BLK:2 * BLK] * masks[1]) * (HEAD_DIM ** -0.5)
                stage_k[rows, :] = stage_k[rows, :] + dk_acc[0:BLK]
                stage_v[rows, :] = stage_v[rows, :] + dv_acc[0:BLK]
                stage_k[rows_hi, :] = dk_acc[BLK:2 * BLK]
                stage_v[rows_hi, :] = dv_acc[BLK:2 * BLK]
                return carry

            lax.fori_loop(0, grp * dil, per_r, 0, unroll=8)

        dk_ref[unit - u1:unit, :] = stage_k[0:u1, :]
        dv_ref[unit - u1:unit, :] = stage_v[0:u1, :]

    qn = lambda n: jnp.minimum(n, nb - 1)
    cur = lambda c0: pl.BlockSpec((unit, w), lambda hp, n: (qn(n), c0 + hp))
    prev = lambda c0: pl.BlockSpec((u1, w), lambda hp, n: (jnp.maximum(qn(n) * grp - 1, 0), c0 + hp))
    row = pl.BlockSpec((unit, w), lambda hp, n: (qn(n), hp))
    late = pl.BlockSpec((unit, w), lambda hp, n: (jnp.maximum(n - 1, 0), hp))
    tab = pl.BlockSpec((2, BLK, 2 * BLK), lambda hp, n: (hp, 0, 0))
    big = SDS((s, BR), F32)
    return _call(
        body, name=f"attn_bwd_d{dil}", out_shape=(big, big, big, SDS((ATT_HEADS, BLK, 2 * BLK), F32)),
        grid=(BR // w, nb + 1),
        in_specs=[cur(q0), cur(k0), prev(k0), cur(v0), prev(v0), row, row, row, tab],
        out_specs=(row, late, late, tab),
        scratch_shapes=[pltpu.VMEM((unit + u1, w), F32)] * 4,
        args=(proj, proj, proj, proj, proj, do, lse, dm, bias), carry=carry)


def _rel_bias_grad(dbias, buckets):
    def body(db_ref, bk_ref, o_ref):
        row = lax.broadcasted_iota(jnp.int32, (REL_BUCKETS, 128), 0)
        lane = lax.broadcasted_iota(jnp.int32, (REL_BUCKETS, 128), 1)

        def per_bucket(b, acc):
            for g in range(len(DILATIONS)):
                hit = bk_ref[g] == b
                for h in range(ATT_HEADS):
                    both = db_ref[0, g, h] + db_ref[1, g, h]
                    val = jnp.sum(jnp.where(hit, both, 0.0), keepdims=True)
                    acc = acc + jnp.where((row == b) & (lane == h), val, 0.0)
            return acc

        o_ref[...] = lax.fori_loop(0, REL_BUCKETS, per_bucket, jnp.zeros((REL_BUCKETS, 128), F32))

    assert dbias.shape[0] == DEPTH == 2
    return pl.pallas_call(body, name="rel_bias_grad", out_shape=SDS((REL_BUCKETS, 128), F32),
                          compiler_params=_params())(dbias, buckets)


def _scan_rows(a_ref, b_ref, o_ref, carry, *, reverse):
    tb = a_ref.shape[0]
    order = range(7, -1, -1) if reverse else range(8)

    @pl.when(pl.program_id(0) == 0)
    def _():
        carry[...] = jnp.zeros_like(carry)

    def group(gi, h):
        r0 = pl.multiple_of((tb // 8 - 1 - gi if reverse else gi) * 8, 8)
        a8, b8 = a_ref[pl.ds(r0, 8), :], b_ref[pl.ds(r0, 8), :]
        rows = [None] * 8
        for k in order:
            if reverse:
                rows[k] = b8[k:k + 1] + h
                h = a8[k:k + 1] * rows[k]
            else:
                h = a8[k:k + 1] * h + b8[k:k + 1]
                rows[k] = h
        o_ref[pl.ds(r0, 8), :] = jnp.concatenate(rows, axis=0)
        return h

    carry[...] = lax.fori_loop(0, tb // 8, group, carry[...])


def _lru_scan_fwd(a, b, proj, tb):
    s = a.shape[0]

    def body(a_ref, b_ref, g_ref, h_ref, y_ref, carry):
        _scan_rows(a_ref, b_ref, h_ref, carry, reverse=False)
        y_ref[...] = (h_ref[...] * _silu(g_ref[...])).astype(MXU_DTYPE)

    return pl.pallas_call(
        body, name="lru_scan", out_shape=(SDS((s, BR), F32), SDS((s, BR), MXU_DTYPE)), grid=(s // tb,),
        in_specs=[_rows(tb, BR), _rows(tb, BR), _rows(tb, BR, CB_CG)], out_specs=(_rows(tb, BR), _rows(tb, BR)),
        scratch_shapes=[pltpu.VMEM((1, BR), F32)], compiler_params=_params(1))(a, b, proj)


def _lru_scan_bwd(a, dycat, h, proj, tb):
    s = a.shape[0]
    nt = s // tb

    def body(a_ref, dy_ref, h_ref, g_ref, l_ref, dg_ref, carry, dh_buf):
        dh_buf[...] = dy_ref[...] * _silu(g_ref[...])
        dg_ref[...] = (dy_ref[...] * h_ref[...] * _dsilu(g_ref[...])).astype(MXU_DTYPE)
        _scan_rows(a_ref, dh_buf, l_ref, carry, reverse=True)

    rev = lambda cb=0: pl.BlockSpec((tb, BR), lambda i: (nt - 1 - i, cb))
    return pl.pallas_call(
        body, name="lru_scan_bwd", out_shape=(SDS((s, BR), F32), SDS((s, BR), MXU_DTYPE)), grid=(nt,),
        in_specs=[rev(), rev(2), rev(), rev(CB_CG)], out_specs=(rev(), rev()),
        scratch_shapes=[pltpu.VMEM((1, BR), F32), pltpu.VMEM((tb, BR), F32)],
        compiler_params=_params(1))(a, dycat, h, proj)


def _scan_tile(s):
    return min(512, s)


def _load_chunked(ref, t0, pt):
    ln = pt // 8
    return jnp.concatenate([ref[pl.ds(t0 + j, 8, stride=ln), :] for j in range(ln)], axis=0)


def _store_natural(ref, t0, pt, val):
    ln = pt // 8
    for j in range(ln):
        ref[pl.ds(t0 + j, 8, stride=ln), :] = val[j * 8:(j + 1) * 8]


def _scan_tile_in_place(a_ref, x_ref, carry, pw, *, reverse):
    ch2 = x_ref.shape[1]
    ch = ch2 // 2
    ln = x_ref.shape[0] // 8
    ar = a_ref[:, 0:ch]
    ai = -a_ref[:, ch:ch2] if reverse else a_ref[:, ch:ch2]

    def cmul(pr, pi, xr, xi):
        return pr * xr - pi * xi, pr * xi + pi * xr

    @pl.when(pl.program_id(0) == 0)
    def _():
        carry[...] = jnp.zeros_like(carry)

        def fill(j, p):
            pw[pl.ds(j, 1), 0:ch] = p[0]
            pw[pl.ds(j, 1), ch:ch2] = p[1]
            return cmul(ar, ai, *p)

        lax.fori_loop(0, ln, fill, (ar, ai))

    def rows_of(j):
        return pl.ds(pl.multiple_of((ln - 1 - j if reverse else j) * 8, 8), 8)

    def local(j, x):
        rows = rows_of(j)
        nr, ni = cmul(ar, ai, *x)
        xr, xi = nr + x_ref[rows, 0:ch], ni + x_ref[rows, ch:ch2]
        x_ref[rows, 0:ch] = xr
        x_ref[rows, ch:ch2] = xi
        return xr, xi

    zero = jnp.zeros((8, ch), F32)
    er, ei = lax.fori_loop(0, ln, local, (zero, zero), unroll=2)
    apr, api = pw[ln - 1:ln, 0:ch], pw[ln - 1:ln, ch:ch2]
    cr, ci = carry[:, 0:ch], carry[:, ch:ch2]
    into_r, into_i = [None] * 8, [None] * 8
    for c in (range(7, -1, -1) if reverse else range(8)):
        into_r[c], into_i[c] = cr, ci
        pr, pi = cmul(apr, api, cr, ci)
        cr, ci = er[c:c + 1] + pr, ei[c:c + 1] + pi
    carry[:, 0:ch] = cr
    carry[:, ch:ch2] = ci
    into_r, into_i = jnp.concatenate(into_r, axis=0), jnp.concatenate(into_i, axis=0)

    def fix(j, carry_):
        rows = rows_of(j)
        dr, di = cmul(pw[pl.ds(j, 1), 0:ch], pw[pl.ds(j, 1), ch:ch2], into_r, into_i)
        x_ref[rows, 0:ch] += dr
        x_ref[rows, ch:ch2] += di
        return carry_

    lax.fori_loop(0, ln, fix, 0, unroll=2)


def _neg_expm1(z):
    series = -z * (1.0 + z * (0.5 + z * (1.0 / 6 + z * (1.0 / 24 + z * (1.0 / 120)))))
    return jnp.where(z > -0.05, series, 1.0 - jnp.exp(z))


def _lru_gate(xc, pre_r, pre_i, lam):
    log_a = -LRU_C * jax.nn.sigmoid(pre_r) * jax.nn.softplus(-lam)
    return jnp.exp(log_a), jnp.sqrt(_neg_expm1(2.0 * log_a)) * jax.nn.sigmoid(pre_i) * xc


def _lru_gates_fwd(proj, conv_w, conv_b, w_cat, b_cat, lam, tb):
    s = proj.shape[0]

    def body(cx, cxp, w_ref, cb_ref, wc_ref, bc_ref, lam_ref, a_ref, b_ref):
        has_prev = (pl.program_id(0) > 0).astype(F32)
        xc = _conv_taps(cx[...], cxp[...] * has_prev, w_ref, 4) + cb_ref[...]
        pre = jnp.dot(xc.astype(MXU_DTYPE), wc_ref[...], preferred_element_type=F32) + bc_ref[...]
        a_ref[...], b_ref[...] = _lru_gate(xc, pre[:, 0:BR], pre[:, BR:2 * BR], lam_ref[...])

    big = SDS((s, BR), F32)
    return pl.pallas_call(
        body, name="lru_gates_fwd", out_shape=(big, big), grid=(s // tb,),
        in_specs=[_rows(tb, BR, CB_CX), _prev8(tb, BR, CB_CX), _const((8, BR)), _const((1, BR)),
                  _const((BR, 2 * BR)), _const((1, 2 * BR)), _const((1, BR))],
        out_specs=(_rows(tb, BR), _rows(tb, BR)), compiler_params=_params(1),
    )(proj, proj, conv_w, conv_b, w_cat, b_cat, lam)


def _lru_gates_bwd(proj, lmb, h, conv_w, conv_b, w_cat, b_cat, lam, tb):
    s = proj.shape[0]

    def body(cx, cxp, l_ref, h_ref, hp_ref, w_ref, cb_ref, wc_ref, bc_ref, lam_ref,
             dxc_ref, dpre_ref, xc_ref, dbc_ref, dlam_ref):
        _init_acc(dbc_ref, dlam_ref)
        has_prev = (pl.program_id(0) > 0).astype(F32)
        xc = _conv_taps(cx[...], cxp[...] * has_prev, w_ref, 4) + cb_ref[...]
        xcb = xc.astype(MXU_DTYPE)
        pre = jnp.dot(xcb, wc_ref[...], preferred_element_type=F32) + bc_ref[...]
        _, vjp = jax.vjp(_lru_gate, xc, pre[:, 0:BR], pre[:, BR:2 * BR], lam_ref[...])
        lm = l_ref[...]
        dxc, dpr, dpi, dlam = vjp((lm * _shift_down(h_ref[...], hp_ref[...] * has_prev, 1), lm))
        dpre = jnp.concatenate([dpr, dpi], axis=1)
        dpreb = dpre.astype(MXU_DTYPE)
        dxc_ref[...] = dxc + lax.dot_general(dpreb, wc_ref[...], (((1,), (1,)), ((), ())),
                                             preferred_element_type=F32)
        dpre_ref[...] = dpreb
        xc_ref[...] = xcb
        dbc_ref[...] += _colsum(dpre)
        dlam_ref[...] += dlam

    return pl.pallas_call(
        body, name="lru_gates_bwd",
        out_shape=(SDS((s, BR), F32), SDS((s, 2 * BR), MXU_DTYPE), SDS((s, BR), MXU_DTYPE),
                   SDS((1, 2 * BR), F32), SDS((1, BR), F32)),
        grid=(s // tb,),
        in_specs=[_rows(tb, BR, CB_CX), _prev8(tb, BR, CB_CX), _rows(tb, BR), _rows(tb, BR), _prev8(tb, BR),
                  _const((8, BR)), _const((1, BR)), _const((BR, 2 * BR)), _const((1, 2 * BR)), _const((1, BR))],
        out_specs=(_rows(tb, BR), _rows(tb, 2 * BR), _rows(tb, BR), _const((1, 2 * BR)), _const((1, BR))),
        compiler_params=_params(1))(proj, proj, lmb, h, h, conv_w, conv_b, w_cat, b_cat, lam)


def _conv_c_bwd(dxc, proj, conv_w, tb):
    s = proj.shape[0]

    def body(g, gn, cx, cxp, w_ref, dcx_ref, dw_ref, db_ref):
        _init_acc(dw_ref, db_ref)
        i = pl.program_id(0)
        has_prev = (i > 0).astype(F32)
        has_next = (i < pl.num_programs(0) - 1).astype(F32)
        gt = g[...]
        dcx_ref[...] = _conv_taps_t(gt, gn[...] * has_next, w_ref, 4).astype(MXU_DTYPE)
        _conv_wgrad(dw_ref, gt, cx[...], cxp[...] * has_prev, 4)
        db_ref[...] += _colsum(gt)

    return pl.pallas_call(
        body, name="conv_c_bwd", out_shape=(SDS((s, BR), MXU_DTYPE), SDS((8, BR), F32), SDS((1, BR), F32)),
        grid=(s // tb,),
        in_specs=[_rows(tb, BR), _next8(tb, BR, s), _rows(tb, BR, CB_CX), _prev8(tb, BR, CB_CX), _const((8, BR))],
        out_specs=(_rows(tb, BR), _const((8, BR)), _const((1, BR))), compiler_params=_params(1),
    )(dxc, dxc, proj, proj, conv_w)


def _s5_disc(lam_re, lam_im, log_dt):
    dt = jnp.exp(log_dt)
    mag = jnp.exp(lam_re * dt)
    ab_re = mag * jnp.cos(lam_im * dt)
    ab_im = mag * jnp.sin(lam_im * dt)
    den = lam_re * lam_re + lam_im * lam_im
    f_re = ((ab_re - 1.0) * lam_re + ab_im * lam_im) / den
    f_im = (ab_im * lam_re - (ab_re - 1.0) * lam_im) / den
    return ab_re, ab_im, f_re, f_im


def _s5_bbar(f_re, f_im, b_re, b_im):
    return f_re * b_re - f_im * b_im, f_re * b_im + f_im * b_re


def _s5_disc_fwd(lam_re, lam_im, log_dt):
    def body(lr, li, ld, o0, o1, o2, o3):
        o0[...], o1[...], o2[...], o3[...] = _s5_disc(lr[...], li[...], ld[...])
    return pl.pallas_call(body, name="s5_disc_fwd", out_shape=(SDS(lam_re.shape, F32),) * 4)(lam_re, lam_im, log_dt)


def _s5_disc_bwd(lam_re, lam_im, log_dt, cts):
    def body(lr, li, ld, c0, c1, c2, c3, o0, o1, o2):
        _, vjp = jax.vjp(_s5_disc, lr[...], li[...], ld[...])
        o0[...], o1[...], o2[...] = vjp((c0[...], c1[...], c2[...], c3[...]))
    return pl.pallas_call(body, name="s5_disc_bwd", out_shape=(SDS(lam_re.shape, F32), SDS(lam_re.shape, F32),
                                                                SDS(log_dt.shape, F32)))(lam_re, lam_im, log_dt, *cts)


def _s5_bbar_fwd(f_re, f_im, b_re, b_im):
    def body(fr, fi, br, bi, o0, o1):
        o0[...], o1[...] = _s5_bbar(fr[...], fi[...], br[...], bi[...])
    return pl.pallas_call(body, name="s5_bbar_fwd", out_shape=(SDS(b_re.shape, F32),) * 2)(f_re, f_im, b_re, b_im)


def _s5_bbar_bwd(f_re, f_im, b_re, b_im, d_re, d_im):
    def body(fr, fi, br, bi, dr, di, o0, o1, o2, o3):
        _, vjp = jax.vjp(_s5_bbar, fr[...], fi[...], br[...], bi[...])
        o0[...], o1[...], o2[...], o3[...] = vjp((dr[...], di[...]))
    col, mat = SDS(f_re.shape, F32), SDS(b_re.shape, F32)
    return pl.pallas_call(body, name="s5_bbar_bwd", out_shape=(col, col, mat, mat))(f_re, f_im, b_re, b_im, d_re, d_im)


def _s5_tail_fwd(ylin, proj, d_skip, w_glu, b_glu, tb):
    s = proj.shape[0]

    def body(yl, u, dg, dk, w_ref, b_ref, o_ref):
        g = jax.nn.gelu(yl[...] + dk[...] * u[...])
        t = jnp.dot(g.astype(MXU_DTYPE), w_ref[...], preferred_element_type=F32) + b_ref[...]
        o_ref[...] = (g * jax.nn.sigmoid(t) * _silu(dg[...])).astype(MXU_DTYPE)

    return pl.pallas_call(
        body, name="s5_tail_fwd", out_shape=SDS((s, BR), MXU_DTYPE), grid=(s // tb,),
        in_specs=[_rows(tb, BR), _rows(tb, BR, CB_DU), _rows(tb, BR, CB_DG), _const((1, BR)), _const((BR, BR)),
                  _const((1, BR))],
        out_specs=_rows(tb, BR), compiler_params=_params(1))(ylin, proj, proj, d_skip, w_glu, b_glu)


def _s5_tail_bwd(dycat, ylin, proj, d_skip, w_glu, b_glu, tb):
    s = proj.shape[0]

    def body(dy, yl, u, dg, dk, w_ref, b_ref, dyl_ref, dus_ref, ddg_ref, g_ref, dt_ref, ddk_ref, dbg_ref):
        _init_acc(ddk_ref, dbg_ref)
        g, gelu_vjp = jax.vjp(jax.nn.gelu, yl[...] + dk[...] * u[...])
        gb = g.astype(MXU_DTYPE)
        sg = jax.nn.sigmoid(jnp.dot(gb, w_ref[...], preferred_element_type=F32) + b_ref[...])
        dz = dy[...] * _silu(dg[...])
        ddg_ref[...] = (dy[...] * g * sg * _dsilu(dg[...])).astype(MXU_DTYPE)
        dt = dz * g * sg * (1.0 - sg)
        dtb = dt.astype(MXU_DTYPE)
        dgel = dz * sg + lax.dot_general(dtb, w_ref[...], (((1,), (1,)), ((), ())), preferred_element_type=F32)
        dyv, = gelu_vjp(dgel)
        dyl_ref[...] = dyv
        dus_ref[...] = dyv * dk[...]
        g_ref[...] = gb
        dt_ref[...] = dtb
        ddk_ref[...] += _colsum(dyv * u[...])
        dbg_ref[...] += _colsum(dt)

    big, half, vec = SDS((s, BR), F32), SDS((s, BR), MXU_DTYPE), SDS((1, BR), F32)
    return pl.pallas_call(
        body, name="s5_tail_bwd", out_shape=(big, big, half, half, half, vec, vec), grid=(s // tb,),
        in_specs=[_rows(tb, BR, 3), _rows(tb, BR), _rows(tb, BR, CB_DU), _rows(tb, BR, CB_DG), _const((1, BR)),
                  _const((BR, BR)), _const((1, BR))],
        out_specs=(_rows(tb, BR),) * 5 + (_const((1, BR)), _const((1, BR))), compiler_params=_params(1),
    )(dycat, ylin, proj, proj, d_skip, w_glu, b_glu)


def _assemble_dproj(da, dqkv, dbg, dcx, dcg, du, dus, ddg, tb):
    s = da.shape[0]

    def body(da_ref, q0, q1, q2, k0, k1, k2, v0, v1, v2, dbg_ref, dcx_ref, dcg_ref, du_ref, dus_ref, ddg_ref, o_ref):
        o_ref[:, 0:4 * BR] = da_ref[...]
        for j, parts in enumerate(((q0, q1, q2), (k0, k1, k2), (v0, v1, v2))):
            o_ref[:, (4 + j) * BR:(5 + j) * BR] = (parts[0][...] + parts[1][...] + parts[2][...]).astype(MXU_DTYPE)
        o_ref[:, 7 * BR:8 * BR] = dbg_ref[...].astype(MXU_DTYPE)
        o_ref[:, 8 * BR:9 * BR] = dcx_ref[...].astype(MXU_DTYPE)
        o_ref[:, 9 * BR:10 * BR] = dcg_ref[...].astype(MXU_DTYPE)
        o_ref[:, 10 * BR:11 * BR] = (du_ref[...] + dus_ref[...]).astype(MXU_DTYPE)
        o_ref[:, 11 * BR:12 * BR] = ddg_ref[...].astype(MXU_DTYPE)

    flat = [t for grp in dqkv for t in grp]
    return pl.pallas_call(
        body, name="assemble_dproj", out_shape=SDS((s, N_IN), MXU_DTYPE), grid=(s // tb,),
        in_specs=[_rows(tb, 4 * BR)] + [_rows(tb, BR)] * 15, out_specs=_rows(tb, N_IN),
        compiler_params=_params(1))(da, *flat, dbg, dcx, dcg, du, dus, ddg)


def _sum_leading(xs, tr, name):
    n, _, c = xs[0].shape
    nl = len(xs)
    tr = min([tr] + [x.shape[1] for x in xs])
    assert all(x.shape[1] % tr == 0 for x in xs), (name, tr)
    nrs = [x.shape[1] // tr for x in xs]
    starts = [sum(nrs[:l]) for l in range(nl)]

    def body(*refs):
        i = pl.program_id(0)
        for l in range(nl):
            @pl.when((i >= starts[l]) & (i < starts[l] + nrs[l]))
            def _():
                acc = refs[l * n][...].astype(F32)
                for ref in refs[l * n + 1:(l + 1) * n]:
                    acc = acc + ref[...].astype(F32)
                refs[nl * n][...] = acc

    specs = [pl.BlockSpec((None, tr, c), functools.partial(
        lambda i, k, l: (k, jnp.clip(i - starts[l], 0, nrs[l] - 1), 0), k=k, l=l)) for l in range(nl) for k in range(n)]
    return pl.pallas_call(body, name=name, out_shape=SDS((sum(nrs) * tr, c), F32), grid=(sum(nrs),), in_specs=specs,
                          out_specs=pl.BlockSpec((tr, c), lambda i: (i, 0)),
                          compiler_params=_params(1))(*[x for x in xs for _ in range(n)])


def _adamw(w, g_parts, m, v, tr, name):
    r, c = w.shape
    tr = min(tr, r)
    n = len(g_parts)
    assert r % tr == 0, (name, r, tr)

    def body(*refs):
        w_ref, m_ref, v_ref = refs[0], refs[1 + n], refs[2 + n]
        g_ref, d_ref, nm_ref, nv_ref = refs[3 + n:]
        g = refs[1][...]
        for ref in refs[2:1 + n]:
            g = g + ref[...]
        mm = ADAM_B1 * m_ref[...] + (1.0 - ADAM_B1) * g
        vv = ADAM_B2 * v_ref[...] + (1.0 - ADAM_B2) * jnp.square(g)
        m_hat = mm / (1.0 - ADAM_B1 ** ADAM_STEP)
        v_hat = vv / (1.0 - ADAM_B2 ** ADAM_STEP)
        g_ref[...] = g
        d_ref[...] = -ADAM_LR * (m_hat / (jnp.sqrt(v_hat) + ADAM_EPS) + ADAM_WD * w_ref[...])
        nm_ref[...] = mm
        nv_ref[...] = vv

    spec = pl.BlockSpec((tr, c), lambda i: (i, 0))
    return _call(body, name=name, out_shape=(SDS((r, c), F32),) * 4, grid=(r // tr,), in_specs=[spec] * (3 + n),
                 out_specs=(spec,) * 4, scratch_shapes=[], args=(w, *g_parts, m, v))


class _AllGather8:
    def __init__(self, block):
        self.m_per = block.shape[0]
        self.arrays, self.n_in, self.n_out = [block], 1, 1
        self.out_shapes = (SDS((N_DEV * self.m_per, block.shape[1]), block.dtype),)
        self.scratch = [pltpu.SemaphoreType.DMA((7,)), pltpu.SemaphoreType.DMA((7,)), pltpu.SemaphoreType.DMA]

    def _copies(self, ins, outs, sems):
        (x_ref,), (out_ref,), (send_sems, recv_sems, local_sem) = ins, outs, sems
        x, y, c = lax.axis_index("x"), lax.axis_index("y"), lax.axis_index("c")
        me, sibling = (x, y, c), (x, y, 1 - c)
        chips = [(1 - x, y), (x, 1 - y), (1 - x, 1 - y)]

        def rows(px, py, pc):
            return out_ref.at[pl.ds((4 * px + 2 * py + pc) * self.m_per, self.m_per), :]

        def copy(k, blk, to, src=None):
            return pltpu.make_async_remote_copy(
                src_ref=rows(*blk) if src is None else src, dst_ref=rows(*blk), send_sem=send_sems.at[k],
                recv_sem=recv_sems.at[k], device_id=to, device_id_type=MESH)

        mine = pltpu.make_async_copy(x_ref, rows(*me), local_sem)
        first = [copy(0, me, sibling, src=x_ref)]
        first += [copy(1 + j, me, (*chip, c), src=x_ref) for j, chip in enumerate(chips)]
        passed = [copy(4 + j, (*chip, c), sibling) for j, chip in enumerate(chips)]
        arrivals = [copy(1 + j, (*chip, c), me) for j, chip in enumerate(chips)]
        from_sibling = [copy(0, sibling, me)] + [copy(4 + j, (*chip, 1 - c), me) for j, chip in enumerate(chips)]
        return mine, first, passed, arrivals, from_sibling

    def start(self, ins, outs, sems):
        mine, first, _, _, _ = self._copies(ins, outs, sems)
        mine.start()
        for cp in first:
            cp.start()

    def wait(self, ins, outs, sems):
        mine, first, passed, arrivals, from_sibling = self._copies(ins, outs, sems)
        for arrived, onward in zip(arrivals, passed):
            arrived.wait_recv()
            onward.start()
        for cp in from_sibling:
            cp.wait_recv()
        for cp in first + passed:
            cp.wait_send()
        mine.wait()


def _allgather8(block, name):
    ex = _AllGather8(block)

    def body(x_ref, out_ref, *sems):
        ex.start((x_ref,), (out_ref,), sems)
        ex.wait((x_ref,), (out_ref,), sems)

    return pl.pallas_call(
        body, name=name, out_shape=ex.out_shapes[0], in_specs=[pl.BlockSpec(memory_space=pltpu.VMEM)],
        out_specs=pl.BlockSpec(memory_space=pltpu.VMEM), scratch_shapes=ex.scratch, compiler_params=_params())(block)


class _Exchange:
    def __init__(self, items, out_shapes):
        self.items, self.out_shapes = list(items), tuple(out_shapes)
        self.arrays = [it[0] for it in self.items]
        n = len(self.items)
        self.n_in, self.n_out = n, len(self.out_shapes)
        self.scratch = [pltpu.SemaphoreType.DMA((n * N_CHIPS,)), pltpu.SemaphoreType.DMA((n * N_CHIPS,)),
                        pltpu.SemaphoreType.DMA((n,))]

    def _copies(self, ins, outs, sems, m):
        send_sems, recv_sems, local_sems = sems
        c = lax.axis_index("c")
        others = [j for j in range(N_CHIPS) if j != m]

        def remote(a, src, dst, to, from_):
            return pltpu.make_async_remote_copy(
                src_ref=src, dst_ref=dst, send_sem=send_sems.at[a * N_CHIPS + to],
                recv_sem=recv_sems.at[a * N_CHIPS + from_], device_id=(to // 2, to % 2, c), device_id_type=MESH)

        local, sends, recvs = [], [], []
        for a, (_, oi, src_of, dst_of) in enumerate(self.items):
            local.append(pltpu.make_async_copy(src_of(ins[a], m), dst_of(outs[oi], m), local_sems.at[a]))
            for j in others:
                sends.append(remote(a, src_of(ins[a], j), dst_of(outs[oi], m), j, m))
                recvs.append(remote(a, src_of(ins[a], m), dst_of(outs[oi], j), j, j))
        return local, sends, recvs

    def _on_my_chip(self, fn):
        chip = 2 * lax.axis_index("x") + lax.axis_index("y")
        for m in range(N_CHIPS):
            pl.when(chip == m)(functools.partial(fn, m))

    def start(self, ins, outs, sems):
        def go(m):
            local, sends, _ = self._copies(ins, outs, sems, m)
            for cp in local + sends:
                cp.start()
        self._on_my_chip(go)

    def wait(self, ins, outs, sems):
        def go(m):
            local, sends, recvs = self._copies(ins, outs, sems, m)
            for cp in recvs:
                cp.wait_recv()
            for cp in sends:
                cp.wait_send()
            for cp in local:
                cp.wait()
        self._on_my_chip(go)


def _half_rows(ref, cc):
    h = ref.shape[-2] // 2
    return ref.at[(slice(None),) * (len(ref.shape) - 2) + (pl.ds(cc * h, h), slice(None))]


class _Gather:
    def __init__(self, items, out_shapes):
        self.items, self.out_shapes = list(items), tuple(out_shapes)
        self.arrays = [it[0] for it in self.items]
        n = len(self.items)
        self.n_in, self.n_out = n, len(self.out_shapes)
        self.scratch = [pltpu.SemaphoreType.DMA((n * N_CHIPS,)) for _ in range(4)] + [pltpu.SemaphoreType.DMA((n,))]

    def _copies(self, ins, outs, sems, m, cc):
        ici_send, ici_recv, d2d_send, d2d_recv, local_sems = sems
        others = [j for j in range(N_CHIPS) if j != m]
        local, sends, arrivals, passed_on, from_sibling = [], [], [], [], []
        for a, (_, oi, src_of, dst_of) in enumerate(self.items):
            src, out = src_of(ins[a]), outs[oi]
            local.append(pltpu.make_async_copy(src, dst_of(out, m), local_sems.at[a]))
            for j in others:
                k = a * N_CHIPS + j
                mine_there = _half_rows(dst_of(out, m), cc)
                theirs_here = _half_rows(dst_of(out, j), cc)
                sends.append(pltpu.make_async_remote_copy(
                    src_ref=_half_rows(src, cc), dst_ref=mine_there, send_sem=ici_send.at[k],
                    recv_sem=ici_recv.at[a * N_CHIPS + m], device_id=(j // 2, j % 2, cc), device_id_type=MESH))
                arrivals.append(pltpu.make_async_remote_copy(
                    src_ref=_half_rows(src, cc), dst_ref=theirs_here, send_sem=ici_send.at[k], recv_sem=ici_recv.at[k],
                    device_id=(j // 2, j % 2, cc), device_id_type=MESH))
                passed_on.append(pltpu.make_async_remote_copy(
                    src_ref=theirs_here, dst_ref=theirs_here, send_sem=d2d_send.at[k], recv_sem=d2d_recv.at[k],
                    device_id=(m // 2, m % 2, 1 - cc), device_id_type=MESH))
                other_half = _half_rows(dst_of(out, j), 1 - cc)
                from_sibling.append(pltpu.make_async_remote_copy(
                    src_ref=other_half, dst_ref=other_half, send_sem=d2d_send.at[k], recv_sem=d2d_recv.at[k],
                    device_id=(m // 2, m % 2, 1 - cc), device_id_type=MESH))
        return local, sends, arrivals, passed_on, from_sibling

    def _on_my_core(self, fn):
        chip = 2 * lax.axis_index("x") + lax.axis_index("y")
        c = lax.axis_index("c")
        for m in range(N_CHIPS):
            for cc in range(2):
                pl.when((chip == m) & (c == cc))(functools.partial(fn, m, cc))

    def start(self, ins, outs, sems):
        def go(m, cc):
            local, sends, _, _, _ = self._copies(ins, outs, sems, m, cc)
            for cp in local + sends:
                cp.start()
        self._on_my_core(go)

    def wait(self, ins, outs, sems):
        def go(m, cc):
            local, sends, arrivals, passed_on, from_sibling = self._copies(ins, outs, sems, m, cc)
            for arrived, onward in zip(arrivals, passed_on):
                arrived.wait_recv()
                onward.start()
            for cp in from_sibling:
                cp.wait_recv()
            for cp in sends + passed_on:
                cp.wait_send()
            for cp in local:
                cp.wait()
        self._on_my_core(go)


def _run_exchange(ex, name):
    def body(*refs):
        ins, outs, sems = refs[:ex.n_in], refs[ex.n_in:ex.n_in + ex.n_out], refs[ex.n_in + ex.n_out:]
        ex.start(ins, outs, sems)
        ex.wait(ins, outs, sems)

    return pl.pallas_call(
        body, name=name, out_shape=ex.out_shapes, in_specs=[ANY] * ex.n_in, out_specs=(ANY,) * ex.n_out,
        scratch_shapes=ex.scratch, compiler_params=_params())(*ex.arrays)


def _sibling_swap(arrays, name, also):
    n = len(arrays)

    def body(*refs):
        ins, refs = refs[:n], refs[n:]
        x_ins, refs = refs[:also.n_in], refs[also.n_in:]
        outs, refs = refs[:n], refs[n:]
        x_outs, refs = refs[:also.n_out], refs[also.n_out:]
        send_sems, recv_sems, x_sems = refs[0], refs[1], refs[2:]
        peer = (lax.axis_index("x"), lax.axis_index("y"), 1 - lax.axis_index("c"))
        cps = [pltpu.make_async_remote_copy(src_ref=ins[a], dst_ref=outs[a], send_sem=send_sems.at[a],
                                            recv_sem=recv_sems.at[a], device_id=peer, device_id_type=MESH)
               for a in range(n)]
        also.start(x_ins, x_outs, x_sems)
        for cp in cps:
            cp.start()
        also.wait(x_ins, x_outs, x_sems)
        for cp in cps:
            cp.wait()

    return pl.pallas_call(
        body, name=name, out_shape=tuple(SDS(a.shape, a.dtype) for a in arrays) + also.out_shapes,
        in_specs=[ANY] * (n + also.n_in), out_specs=(ANY,) * (n + also.n_out),
        scratch_shapes=[pltpu.SemaphoreType.DMA((n,)), pltpu.SemaphoreType.DMA((n,))] + also.scratch,
        compiler_params=_params())(*arrays, *also.arrays)


def _block_diag(w):
    h, n, m = w.shape
    eye = jnp.eye(h, dtype=w.dtype)
    return (w[:, :, None, :] * eye[:, None, :, None]).reshape(h * n, h * m)


def _diag_blocks(d, h, col0=0, ncols=None, stacked=1):
    ncols = d.shape[1] - col0 if ncols is None else ncols
    n, m = d.shape[0] // (h * stacked), ncols // h
    lanes = 128
    assert m <= lanes and lanes % m == 0 and col0 % lanes == 0

    def body(d_ref, o_ref):
        for gi in range(h * stacked):
            c = col0 + (gi % h) * m
            chunk = d_ref[gi * n:(gi + 1) * n, c // lanes * lanes:c // lanes * lanes + lanes]
            o_ref[gi * n:(gi + 1) * n, :] = chunk[:, c % lanes:c % lanes + m]

    out = pl.pallas_call(body, name="diag_blocks", out_shape=SDS((stacked * h * n, m), d.dtype),
                         compiler_params=_params())(d)
    return out.reshape(stacked * h, n, m)


S5_CHUNKS = 4
S5_PER = S5_GROUPS // S5_CHUNKS
CH_W = S5_PER * S5_CH
ST_W = S5_PER * S5_STATE


def _bd_stack(mats):
    _, _, n, m = mats.shape
    eye = jnp.eye(S5_PER, dtype=mats.dtype)
    t = mats.reshape(2, S5_CHUNKS, S5_PER, n, m)
    bd = t[:, :, :, :, None, :] * eye[None, None, :, None, :, None]
    return bd.reshape(2 * S5_CHUNKS, S5_PER * n, S5_PER * m).astype(MXU_DTYPE)


def _chunks_chunked(src_ref, buf):
    pt = src_ref.shape[0]
    out = []
    for q in range(S5_CHUNKS):
        buf[q] = src_ref[:, q * CH_W:(q + 1) * CH_W]
        out.append(_load_chunked(buf.at[q], 0, pt).astype(MXU_DTYPE))
    return out


def _expand_into(dst_ref, chunks, w_ref):
    for b in range(2 * S5_CHUNKS):
        dst_ref[:, b * ST_W:(b + 1) * ST_W] = jnp.dot(chunks[b % S5_CHUNKS], w_ref[b], preferred_element_type=F32)


def _reduce_from(src_ref, w_ref, buf, dst_ref):
    pt = src_ref.shape[0]
    for q in range(S5_CHUNKS):
        y = jnp.dot(src_ref[:, q * ST_W:(q + 1) * ST_W].astype(MXU_DTYPE), w_ref[q], preferred_element_type=F32)
        p = S5_CHUNKS + q
        y = y + jnp.dot(src_ref[:, p * ST_W:(p + 1) * ST_W].astype(MXU_DTYPE), w_ref[p], preferred_element_type=F32)
        _store_natural(buf.at[q], 0, pt, y)
        dst_ref[:, q * CH_W:(q + 1) * CH_W] = buf[q]


def _s5_core_fwd(proj, w_bu, w_cx, a_row):
    s = proj.shape[0]
    pt = _scan_tile(s)
    ch2 = 2 * S5_N

    def body(u_ref, wb_ref, wc_ref, a_ref, x_ref, y_ref, carry, pw, buf):
        _expand_into(x_ref, _chunks_chunked(u_ref, buf), wb_ref)
        _scan_tile_in_place(a_ref, x_ref, carry, pw, reverse=False)
        _reduce_from(x_ref, wc_ref, buf, y_ref)

    return pl.pallas_call(
        body, name="s5_core_fwd", out_shape=(SDS((s, ch2), F32), SDS((s, BR), F32)), grid=(s // pt,),
        in_specs=[_rows(pt, BR, CB_DU), _const(w_bu.shape), _const(w_cx.shape), _const((1, ch2))],
        out_specs=(_rows(pt, ch2), _rows(pt, BR)),
        scratch_shapes=[pltpu.VMEM((1, ch2), F32), pltpu.VMEM((pt // 8, ch2), F32),
                        pltpu.VMEM((S5_CHUNKS, pt, CH_W), F32)],
        compiler_params=_params(1))(proj, w_bu, w_cx, a_row)


def _s5_core_bwd(dyl, proj, x, w_dx, w_du, a_row):
    s = proj.shape[0]
    pt = _scan_tile(s)
    nt = s // pt
    ch2 = 2 * S5_N
    ch = S5_N

    def body(dy_ref, u_ref, x_ref, xp_ref, wx_ref, wu_ref, a_ref, du_ref, da_ref, dwb_ref, dwc_ref,
             l_ref, carry, pw, buf, buf2):
        i = pl.program_id(0)
        _init_acc(da_ref, dwb_ref, dwc_ref)
        dy_c = _chunks_chunked(dy_ref, buf)
        u_c = _chunks_chunked(u_ref, buf2)
        _expand_into(l_ref, dy_c, wx_ref)
        _scan_tile_in_place(a_ref, l_ref, carry, pw, reverse=True)
        has_prev = (i < nt - 1).astype(F32)
        row = lax.broadcasted_iota(jnp.int32, (8, ch2), 0)
        first = jnp.where(row == 0, pltpu.roll(xp_ref[...], 1, 0) * has_prev, pltpu.roll(x_ref[pt - 8:pt, :], 1, 0))
        xprev = jnp.concatenate([first, x_ref[0:pt - 8, :]], axis=0)
        lr, li, xr, xi = l_ref[:, 0:ch], l_ref[:, ch:ch2], xprev[:, 0:ch], xprev[:, ch:ch2]
        da_ref[:, 0:ch] += _colsum(lr * xr + li * xi)
        da_ref[:, ch:ch2] += _colsum(li * xr - lr * xi)
        _reduce_from(l_ref, wu_ref, buf, du_ref)
        tn = (((0,), (0,)), ((), ()))
        for b in range(2 * S5_CHUNKS):
            cols, rows = slice(b * ST_W, (b + 1) * ST_W), slice(b * CH_W, (b + 1) * CH_W)
            dwb_ref[rows, :] += lax.dot_general(u_c[b % S5_CHUNKS], l_ref[:, cols].astype(MXU_DTYPE), tn,
                                                preferred_element_type=F32)
            dwc_ref[rows, :] += lax.dot_general(dy_c[b % S5_CHUNKS], x_ref[:, cols].astype(MXU_DTYPE), tn,
                                                preferred_element_type=F32)

    rev = lambda w, cb=0: pl.BlockSpec((pt, w), lambda i: (nt - 1 - i, cb))
    halo = pl.BlockSpec((8, ch2), lambda i: (jnp.maximum((nt - 1 - i) * (pt // 8) - 1, 0), 0))
    wshape = SDS((2 * S5_CHUNKS * CH_W, ST_W), F32)
    return pl.pallas_call(
        body, name="s5_core_bwd", out_shape=(SDS((s, BR), F32), SDS((1, ch2), F32), wshape, wshape), grid=(nt,),
        in_specs=[rev(BR, 0), rev(BR, CB_DU), rev(ch2), halo, _const(w_dx.shape), _const(w_du.shape),
                  _const((1, ch2))],
        out_specs=(rev(BR), _const((1, ch2)), _const(wshape.shape), _const(wshape.shape)),
        scratch_shapes=[pltpu.VMEM((pt, ch2), F32), pltpu.VMEM((1, ch2), F32), pltpu.VMEM((pt // 8, ch2), F32),
                        pltpu.VMEM((S5_CHUNKS, pt, CH_W), F32), pltpu.VMEM((S5_CHUNKS, pt, CH_W), F32)],
        compiler_params=_params(1))(dyl, proj, x, x, w_dx, w_du, a_row)


def _tiles(s):
    return dict(tb=min(512, s), tln=min(256, s))


def _layer_weights(p, l):
    pad8 = lambda w: jnp.pad(w, ((0, 8 - w.shape[0]), (0, 0)))
    return dict(
        conv_a=pad8(p["conv_a"][l]), conv_c=pad8(p["conv_c"][l]), conv_c_b=p["conv_c_b"][l][None],
        w_cat=jnp.concatenate([_block_diag(p["lru_wa"][l]), _block_diag(p["lru_wx"][l])], axis=1).astype(MXU_DTYPE),
        b_cat=jnp.concatenate([p["lru_ba"][l], p["lru_bx"][l]])[None], lam=p["lru_lambda"][l][None],
        lam_re=p["s5_lam_re"][l], lam_im=p["s5_lam_im"][l], log_dt=p["s5_log_dt"][l][:, None],
        b_re=p["s5_b_re"][l].reshape(S5_N, S5_CH), b_im=p["s5_b_im"][l].reshape(S5_N, S5_CH),
        c_re=p["s5_c_re"][l], c_im=p["s5_c_im"][l], d_skip=p["s5_d"][l][None], b_glu=p["s5_b_glu"][l][None],
        ln_g=p["ln_g"][l][None], ln_b=p["ln_b"][l][None])


def _s5_matrices(lw):
    ab_re, ab_im, f_re, f_im = _s5_disc_fwd(lw["lam_re"], lw["lam_im"], lw["log_dt"])
    f_re, f_im = f_re.reshape(S5_N, 1), f_im.reshape(S5_N, 1)
    bb_re, bb_im = _s5_bbar_fwd(f_re, f_im, lw["b_re"], lw["b_im"])
    bb = jnp.stack([bb_re, bb_im]).reshape(2, S5_GROUPS, S5_STATE, S5_CH)
    cc = jnp.stack([lw["c_re"], -lw["c_im"]])
    a_row = jnp.concatenate([ab_re.reshape(1, S5_N), ab_im.reshape(1, S5_N)], axis=1)
    return dict(f_re=f_re, f_im=f_im, a_row=a_row, w_bu=_bd_stack(jnp.swapaxes(bb, 2, 3)), w_du=_bd_stack(bb),
                w_cx=_bd_stack(jnp.swapaxes(cc, 2, 3)), w_dx=_bd_stack(cc))


def _mm_hooked(hook, *args, **kw):
    if hook is None:
        return _mm(*args, **kw)
    out = _mm(*args, carry=hook[0], **kw)
    hook[1](out[1:])
    return out[0]


def _layer_fwd(x, h, ada, w_in, get_rest, lw, s5m, bias_tabs, hooks=None, target=None, next_ada=None):
    s = x.shape[0]
    t = _tiles(s)
    tb = t["tb"]
    shift, scale, gate = ada
    hooks = hooks or {}
    if h is None:
        h = _modulate(x, scale, shift, tb)
    proj = _mm_hooked(hooks.get("in_proj"), h, w_in, name="in_proj", tm=1024, tn=1536, tk=D_MODEL)
    w_out, w_glu = get_rest()
    y_a = _branch_a_fwd(proj, lw["conv_a"], tb)
    os_, lses = [], []
    for g, (_, dil) in enumerate(DILATIONS):
        o, lse = _attn_fwd(proj, bias_tabs[g], dil)
        os_.append(o)
        lses.append(lse)
    y_b = _attn_combine(os_, lses, proj, tb)
    lru_a, lru_b = _lru_gates_fwd(proj, lw["conv_c"], lw["conv_c_b"], lw["w_cat"], lw["b_cat"], lw["lam"], tb)
    lru_h, y_c = _lru_scan_fwd(lru_a, lru_b, proj, tb)
    s5_x, ylin = _s5_core_fwd(proj, s5m["w_bu"], s5m["w_cx"], s5m["a_row"])
    y_d = _s5_tail_fwd(ylin, proj, lw["d_skip"], w_glu, lw["b_glu"], tb)
    ycat = jnp.concatenate([y_a, y_b, y_c, y_d], axis=1)
    saved = dict(x=x, h=h, proj=proj, os=os_, lses=lses, lru_a=lru_a, lru_h=lru_h, s5_x=s5_x, ylin=ylin, ycat=ycat)
    if target is not None:
        loss, *saved["head"] = _out_ln_loss(ycat, w_out, x, gate, lw["ln_g"], lw["ln_b"], target, t["tln"])
        return loss, None, saved
    x_next, saved["xhat"], saved["y"], saved["rstd"], h_next = _out_ln(
        ycat, w_out, x, gate, lw["ln_g"], lw["ln_b"], next_ada[1], next_ada[0], t["tln"])
    return x_next, h_next, saved


def _layer_bwd(dxn, sv, ada, w_in, w_out, w_glu, lw, s5m, bias_tabs, head_ones, hooks=None):
    proj = sv["proj"]
    s = proj.shape[0]
    t = _tiles(s)
    tb = t["tb"]
    shift, scale, gate = ada
    g = {}
    hook = lambda name: hooks[name](g) if hooks and name in hooks else None
    if "head" in sv:
        dyb, dxa, g["ln_g"], g["ln_b"], dgate = sv["head"]
    else:
        dyb, dxa, g["ln_g"], g["ln_b"], dgate = _ln_bwd(dxn, sv["xhat"], sv["y"], sv["rstd"], lw["ln_g"], gate,
                                                        t["tln"])
    g["w_out"] = _mm_hooked(hook("dw_out"), sv["ycat"], dyb, name="dw_out", ta=True, out_dtype=WIRE_DTYPE,
                            tm=1024, tn=1024, tk=2048)
    dycat =_mm(dyb, w_out, name="dycat", tb=True, tm=1024, tn=1024, tk=D_MODEL)
    da, dconv_a = _branch_a_bwd(dycat, proj, lw["conv_a"], tb)
    g["conv_a"] = dconv_a[0:3]
    pre = _attn_bwd_pre(dycat, sv["os"], sv["lses"], proj, head_ones, tb)
    dbg, dos, dms = pre[0], pre[1:4], pre[4:7]
    dqkv, dbias = [], []
    for gi, (_, dil) in enumerate(DILATIONS):
        hk = hook(f"attn_bwd_d{dil}")
        dq, dk, dv, dbi, *got = _attn_bwd(proj, dos[gi], sv["lses"][gi], dms[gi], bias_tabs[gi], dil,
                                          carry=hk and hk[0])
        if hk:
            hk[1](got)
        dqkv.append((dq, dk, dv))
        dbias.append(dbi)
    dqkv = list(zip(*dqkv))
    lmb, dcg = _lru_scan_bwd(sv["lru_a"], dycat, sv["lru_h"], proj, tb)
    dxc, dpre, xcb, dbcat, dlam = _lru_gates_bwd(proj, lmb, sv["lru_h"], lw["conv_c"], lw["conv_c_b"], lw["w_cat"],
                                                  lw["b_cat"], lw["lam"], tb)
    dwcat = _mm(xcb, dpre, name="dw_lru", ta=True, tn=1024)
    g["lru_wa"] = _diag_blocks(dwcat, LRU_HEADS, 0, BR)
    g["lru_wx"] = _diag_blocks(dwcat, LRU_HEADS, BR, BR)
    g["lru_ba"], g["lru_bx"], g["lru_lambda"] = dbcat[0, 0:BR], dbcat[0, BR:2 * BR], dlam[0]
    dcx, dconv_c, dccb = _conv_c_bwd(dxc, proj, lw["conv_c"], tb)
    g["conv_c"], g["conv_c_b"] = dconv_c[0:4], dccb[0]
    dyl, dus, ddg, gb, dtb, ddk, dbglu = _s5_tail_bwd(dycat, sv["ylin"], proj, lw["d_skip"], w_glu, lw["b_glu"], tb)
    g["s5_d"], g["s5_b_glu"] = ddk[0], dbglu[0]
    g["s5_w_glu"] = _mm(gb, dtb, name="dw_glu", ta=True, out_dtype=WIRE_DTYPE)
    du, dab, dwb8, dwc8 = _s5_core_bwd(dyl, proj, sv["s5_x"], s5m["w_dx"], s5m["w_du"], s5m["a_row"])
    per_group = lambda d8: _diag_blocks(d8, S5_PER, stacked=2 * S5_CHUNKS).reshape(2, S5_GROUPS, S5_CH, S5_STATE)
    dbb, dcc = per_group(dwb8), per_group(dwc8)
    from_bd = lambda half: jnp.swapaxes(dbb[half], 1, 2).reshape(S5_N, S5_CH)
    df_re, df_im, db_re, db_im = _s5_bbar_bwd(s5m["f_re"], s5m["f_im"], lw["b_re"], lw["b_im"],
                                              from_bd(0), from_bd(1))
    shp = (S5_GROUPS, S5_STATE)
    g["s5_lam_re"], g["s5_lam_im"], dlog_dt = _s5_disc_bwd(
        lw["lam_re"], lw["lam_im"], lw["log_dt"],
        (dab[:, 0:S5_N].reshape(shp), dab[:, S5_N:].reshape(shp), df_re.reshape(shp), df_im.reshape(shp)))
    g["s5_log_dt"] = dlog_dt[:, 0]
    g["s5_b_re"] = db_re.reshape(S5_GROUPS, S5_STATE, S5_CH)
    g["s5_b_im"] = db_im.reshape(S5_GROUPS, S5_STATE, S5_CH)
    g["s5_c_re"], g["s5_c_im"] = dcc[0], -dcc[1]
    dproj = _assemble_dproj(da, dqkv, dbg, dcx, dcg, du, dus, ddg, tb)
    g["w_in"] = _mm_hooked(hook("dw_in"), sv["h"], dproj, name="dw_in", ta=True, out_dtype=WIRE_DTYPE,
                           tm=1024, tn=1536, tk=2048)
    hk = hook("dh")
    dx, dshift, dscale, *got = _dh_mod_bwd(dproj, w_in, dxa, sv["x"], scale, carry=hk and hk[0])
    if hk:
        hk[1](got)
    g["ada"] = jnp.concatenate([dshift[0], dscale[0], dgate[0]])
    return dx, g, dbias


SMALL = ("rel_bias", "conv_a", "conv_c", "conv_c_b", "lru_wa", "lru_ba", "lru_wx", "lru_bx", "lru_lambda",
         "s5_lam_re", "s5_lam_im", "s5_log_dt", "s5_b_re", "s5_b_im", "s5_c_re", "s5_c_im", "s5_d", "s5_b_glu",
         "ln_g", "ln_b")
PER_LAYER_SMALL = SMALL[1:]


def _local_step(x, target, ada_rows, w_in, w_out, w_glu, p, comm=None):
    if comm is None:
        get_w_in = lambda l: w_in[l]
        get_rest = lambda l: (w_out[l], w_glu[l])
        fwd_hooks = bwd_hooks = lambda *_: None
    else:
        get_w_in, get_rest, fwd_hooks, bwd_hooks = comm.w_in, comm.rest, comm.fwd_hooks, comm.bwd_hooks
    s = x.shape[0]
    buckets = _bucket_maps()
    bias_tabs = _bias_tables(p["rel_bias"], buckets)
    head_ones = _block_diag(jnp.ones((ATT_HEADS, HEAD_DIM, HEAD_DIM), MXU_DTYPE))
    lws = [_layer_weights(p, l) for l in range(DEPTH)]
    s5ms = [_s5_matrices(lw) for lw in lws]
    adas = [tuple(ada_rows[l, k * D_MODEL:(k + 1) * D_MODEL][None] for k in range(3)) for l in range(DEPTH)]
    saved, h = [], None
    for l in range(DEPTH):
        last = l == DEPTH - 1
        x, h, sv = _layer_fwd(x, h, adas[l], get_w_in(l), functools.partial(get_rest, l), lws[l], s5ms[l], bias_tabs,
                              fwd_hooks(l), target if last else None, None if last else adas[l + 1])
        saved.append(sv)
    loss, dx = x, None
    grads = [None] * DEPTH
    dbias_sum = []
    for l in reversed(range(DEPTH)):
        dx, grads[l], dbias = _layer_bwd(dx, saved[l], adas[l], get_w_in(l), *get_rest(l), lws[l], s5ms[l],
                                         bias_tabs, head_ones, bwd_hooks(l, grads))
        dbias_sum.append(jnp.stack(dbias))
    drel = _rel_bias_grad(jnp.stack(dbias_sum), buckets)[:, 0:ATT_HEADS]
    small = {n: jnp.stack([grads[l][n] for l in range(DEPTH)]) for n in PER_LAYER_SMALL + ("ada",)}
    small["rel_bias"] = drel
    big = {n: [grads[l][n] for l in range(DEPTH)] for n in ("w_in", "w_out", "s5_w_glu")}
    return loss, dx, big, small


PACK_ROWS = 256


def _pack(parts):
    flat = jnp.concatenate([t.reshape(-1).astype(F32) for t in parts])
    n = flat.shape[0]
    rows = -(-n // (PACK_ROWS * 128)) * PACK_ROWS
    return jnp.pad(flat, (0, rows * 128 - n)).reshape(rows, 128)


def _unpack(packed, shapes):
    flat = packed.reshape(packed.shape[:-2] + (-1,))
    out, off = [], 0
    for shp in shapes:
        size = math.prod(shp)
        out.append(flat[..., off:off + size].reshape(flat.shape[:-1] + tuple(shp)))
        off += size
    return out


def _take_cols(t, chip, width):
    return lax.dynamic_slice_in_dim(t, chip * width, width, axis=t.ndim - 1)


class _Comm:
    IN_W, OUT_R, GLU_R = N_IN // N_CHIPS, D_MODEL // N_CHIPS, BR // N_CHIPS

    def __init__(self, w_in_b, w_out_b, w_glu_b):
        assert DEPTH == 2
        self.shards = (w_in_b, w_out_b, w_glu_b)
        in_w = self.IN_W
        self.w_in_full = {0: _run_exchange(_Gather(
            [(w_in_b, 0, lambda ref: ref.at[0], lambda ref, j: ref.at[:, pl.ds(j * in_w, in_w)])],
            [SDS((D_MODEL, N_IN), WIRE_DTYPE)]), "gather_w_in0")[0]}
        self.w_out_full = self.w_glu_full = None
        self.recv = {}

    def w_in(self, l):
        return self.w_in_full[l]

    def rest(self, l):
        return self.w_out_full[l], self.w_glu_full[l]

    def fwd_hooks(self, l):
        if l != 0:
            return None
        w_in_b, w_out_b, w_glu_b = self.shards
        in_w, out_r, glu_r = self.IN_W, self.OUT_R, self.GLU_R
        whole = lambda ref: ref
        items = [(w_out_b, 0, whole, lambda ref, j: ref.at[:, pl.ds(j * out_r, out_r), :]),
                 (w_glu_b, 1, whole, lambda ref, j: ref.at[:, pl.ds(j * glu_r, glu_r), :]),
                 (w_in_b, 2, lambda ref: ref.at[1], lambda ref, j: ref.at[:, pl.ds(j * in_w, in_w)])]
        shapes = [SDS((DEPTH, D_MODEL, D_MODEL), WIRE_DTYPE), SDS((DEPTH, BR, BR), WIRE_DTYPE),
                  SDS((D_MODEL, N_IN), WIRE_DTYPE)]

        def done(got):
            self.w_out_full, self.w_glu_full, self.w_in_full[1] = got

        return {"in_proj": (_Gather(items, shapes), done)}

    W_IN_ROWS = ((0, 1024), (1024, 512), (1536, 512))

    def _scatter(self, parts):
        in_w, out_r, glu_r = self.IN_W, self.OUT_R, self.GLU_R
        items, shapes, keys = [], [], []
        for oi, (name, l, arr, *rows) in enumerate(parts):
            if name == "w_in":
                r0, nr = rows[0] if rows else (0, D_MODEL)
                cut = functools.partial(lambda ref, j, r0, nr: ref.at[pl.ds(r0, nr), pl.ds(j * in_w, in_w)], r0=r0, nr=nr)
                shard = (nr, in_w)
            elif name == "w_out":
                cut, shard = (lambda ref, j: ref.at[pl.ds(j * out_r, out_r), :]), (out_r, D_MODEL)
            else:
                cut, shard = (lambda ref, j: ref.at[pl.ds(j * glu_r, glu_r), :]), (glu_r, BR)
            items.append((arr, oi, cut, lambda ref, j: ref.at[j]))
            shapes.append(SDS((N_CHIPS,) + shard, WIRE_DTYPE))
            keys.append((name, l) + ((rows[0][0],) if rows else ()))

        def done(got):
            self.recv.update(zip(keys, got))

        return _Exchange(items, shapes), done

    def received(self, name):
        return [self.recv[k] for k in sorted(k for k in self.recv if k[0] == name)]

    def bwd_hooks(self, l, grads):
        if l != 0:
            return None
        g1 = grads[1]
        w_in_part = lambda k: (lambda g: self._scatter([("w_in", 1, g1["w_in"], self.W_IN_ROWS[k])]))
        return {"dw_out": lambda g: self._scatter([("w_out", 1, g1["w_out"]), ("s5_w_glu", 1, g1["s5_w_glu"])]),
                "attn_bwd_d16": w_in_part(0), "attn_bwd_d4": w_in_part(1), "attn_bwd_d1": w_in_part(2),
                "dw_in": lambda g: self._scatter([("w_out", 0, g["w_out"]), ("s5_w_glu", 0, g["s5_w_glu"])]),
                "dh": lambda g: self._scatter([("w_in", 0, g["w_in"])])}


def kernel(x, c, rel_bias, w_ada, b_ada, w_in, conv_a, conv_c, conv_c_b, lru_wa, lru_ba, lru_wx, lru_bx, lru_lambda, s5_lam_re, s5_lam_im, s5_log_dt, s5_b_re, s5_b_im, s5_c_re, s5_c_im, s5_d, s5_w_glu, s5_b_glu, w_out, ln_g, ln_b, loss_target, m_rel_bias, m_w_ada, m_b_ada, m_w_in, m_conv_a, m_conv_c, m_conv_c_b, m_lru_wa, m_lru_ba, m_lru_wx, m_lru_bx, m_lru_lambda, m_s5_lam_re, m_s5_lam_im, m_s5_log_dt, m_s5_b_re, m_s5_b_im, m_s5_c_re, m_s5_c_im, m_s5_d, m_s5_w_glu, m_s5_b_glu, m_w_out, m_ln_g, m_ln_b, v_rel_bias, v_w_ada, v_b_ada, v_w_in, v_conv_a, v_conv_c, v_conv_c_b, v_lru_wa, v_lru_ba, v_lru_wx, v_lru_bx, v_lru_lambda, v_s5_lam_re, v_s5_lam_im, v_s5_log_dt, v_s5_b_re, v_s5_b_im, v_s5_c_re, v_s5_c_im, v_s5_d, v_s5_w_glu, v_s5_b_glu, v_w_out, v_ln_g, v_ln_b):
    args = dict(locals())
    names = ("rel_bias", "w_ada", "b_ada", "w_in", "conv_a", "conv_c", "conv_c_b", "lru_wa", "lru_ba", "lru_wx",
             "lru_bx", "lru_lambda", "s5_lam_re", "s5_lam_im", "s5_log_dt", "s5_b_re", "s5_b_im", "s5_c_re", "s5_c_im",
             "s5_d", "s5_w_glu", "s5_b_glu", "w_out", "ln_g", "ln_b")
    w = {n: args[n] for n in names}
    mom = {n: args["m_" + n] for n in names}
    var = {n: args["v_" + n] for n in names}
    chip = 2 * lax.axis_index("x") + lax.axis_index("y")
    me = 2 * chip + lax.axis_index("c")
    ada_w = 3 * D_MODEL // N_CHIPS
    conv_w = BR // N_CHIPS

    comm = _Comm(w["w_in"].astype(WIRE_DTYPE), w["w_out"].astype(WIRE_DTYPE), w["s5_w_glu"].astype(WIRE_DTYPE))

    taps = jnp.concatenate([w["conv_a"].reshape(DEPTH * 3, conv_w), w["conv_c"].reshape(DEPTH * 4, conv_w)])
    first = jnp.concatenate([c, jnp.pad(taps, ((0, 1), (0, D_MODEL - conv_w)))])
    got = _allgather8(first, "gather_c_taps").reshape(N_CHIPS, 2, 16, D_MODEL)
    c_all = got[:, :, 0].reshape(N_DEV, D_MODEL)
    taps_all = jnp.transpose(got[:, 0, 1:1 + DEPTH * 7, 0:conv_w], (1, 0, 2)).reshape(DEPTH * 7, BR)
    conv_a_f = taps_all[0:DEPTH * 3].reshape(DEPTH, 3, BR)
    conv_c_f = taps_all[DEPTH * 3:].reshape(DEPTH, 4, BR)

    cond_all = _silu_rows(c_all)
    ada_part = jnp.stack([_mm(cond_all, w["w_ada"][l], name="ada_fwd", tk=D_MODEL, tn=512,
                              bias=_take_cols(w["b_ada"][l][None], chip, ada_w)) for l in range(DEPTH)])
    ada_all = _allgather8(ada_part.reshape(DEPTH * N_DEV, ada_w), "gather_ada")
    ada_all = ada_all.reshape(N_CHIPS, 2, DEPTH, N_DEV, ada_w)[:, 0]
    ada_rows = lax.dynamic_index_in_dim(ada_all, me, axis=2, keepdims=False)
    ada_rows = jnp.transpose(ada_rows, (1, 0, 2)).reshape(DEPTH, 3 * D_MODEL)

    p = dict(w)
    p["conv_a"], p["conv_c"] = conv_a_f, conv_c_f
    loss, dx, _, small = _local_step(x[0], loss_target[0], ada_rows, None, None, None, p, comm)

    sums = [_sum_leading(comm.received(name), 256, "sum_chips") for name in ("w_in", "w_out", "s5_w_glu")]
    small_names = SMALL + ("ada",)
    small["loss"] = loss
    order = small_names + ("loss",)
    shapes = [small[n].shape for n in order]
    *others, gathered = _sibling_swap(sums, "swap_cores", _AllGather8(_pack([small[n] for n in order])))
    out = {}
    for name, mine, other in zip(("w_in", "w_out", "s5_w_glu"), sums, others):
        shp = w[name].shape
        flat = lambda t: t.reshape(-1, shp[-1])
        res = _adamw(flat(w[name]), [mine, other], flat(mom[name]), flat(var[name]), 128, "adamw_big")
        out[name] = [t.reshape(shp) for t in res]
    gathered = gathered.reshape(N_DEV, -1, 128)
    total = dict(zip(order, _unpack(_sum_leading([gathered], PACK_ROWS, "sum_devices"), shapes)))
    d_ada_all = _unpack(gathered, shapes)[order.index("ada")]
    g_small = {n: total[n] for n in SMALL}
    g_small["conv_a"] = _take_cols(total["conv_a"], chip, conv_w)
    g_small["conv_c"] = _take_cols(total["conv_c"], chip, conv_w)
    g_small["b_ada"] = total["ada"]
    g_w_ada = jnp.stack([_mm(cond_all, _take_cols(d_ada_all[:, l], chip, ada_w), name="dw_ada", ta=True, tn=ada_w)
                         for l in range(DEPTH)])
    upd_names = SMALL + ("b_ada",)
    upd_shapes = [w[n].shape for n in upd_names]
    res = _adamw(_pack([w[n] for n in upd_names]), [_pack([g_small[n] for n in upd_names])],
                 _pack([mom[n] for n in upd_names]), _pack([var[n] for n in upd_names]), PACK_ROWS, "adamw_small")
    for k, t in enumerate(res):
        for n, val in zip(upd_names, _unpack(t, upd_shapes)):
            out.setdefault(n, [None] * 4)[k] = val
    shp = w["w_ada"].shape
    flat = lambda t: t.reshape(-1, shp[-1])
    out["w_ada"] = [t.reshape(shp) for t in _adamw(flat(w["w_ada"]), [flat(g_w_ada)], flat(mom["w_ada"]),
                                                  flat(var["w_ada"]), 128, "adamw_ada")]
    return (total["loss"].reshape(()), dx[None]) + tuple(out[n][k] for k in range(4) for n in names)
```

```python
import functools
import math

import jax
import jax.numpy as jnp
from jax import lax
from jax.experimental import pallas as pl
from jax.experimental.pallas import tpu as pltpu

F32 = jnp.float32
MXU_DTYPE = jnp.bfloat16
WIRE_DTYPE = jnp.bfloat16
SDS = jax.ShapeDtypeStruct
MESH = pl.DeviceIdType.MESH
ANY = pl.BlockSpec(memory_space=pl.ANY)
VMEM_LIMIT = 48 * 1024 * 1024

D_MODEL = 2048
DEPTH = 2
BR = 512
ATT_HEADS = 8
HEAD_DIM = 64
DILATIONS = ((128, 1), (512, 4), (2048, 16))
BLK = 128
REL_BUCKETS = 32
REL_MAX_DIST = 2048
LRU_HEADS = 8
LRU_C = 8.0
S5_CH = 16
S5_GROUPS = 32
S5_STATE = 64
S5_N = S5_GROUPS * S5_STATE
N_IN = 12 * BR
ALPHA = (2 * DEPTH) ** 0.25
LN_EPS = 1e-5
NEG = -1e30
ADAM_LR, ADAM_B1, ADAM_B2, ADAM_EPS, ADAM_WD, ADAM_STEP = 0.001, 0.9, 0.999, 1e-08, 0.01, 10
CB_AB, CB_AC, CB_AX, CB_AG, CB_Q, CB_K, CB_V, CB_BG, CB_CX, CB_CG, CB_DU, CB_DG = range(12)
N_CHIPS = 4
N_DEV = 8


def _params(n_axes=0):
    kw = {"dimension_semantics": ("arbitrary",) * n_axes} if n_axes else {}
    return pltpu.CompilerParams(vmem_limit_bytes=VMEM_LIMIT, **kw)


def _rows(tb, w, cb=0):
    return pl.BlockSpec((tb, w), lambda i: (i, cb))


def _prev8(tb, w, cb=0):
    return pl.BlockSpec((8, w), lambda i: (jnp.maximum(i * (tb // 8) - 1, 0), cb))


def _next8(tb, w, n_rows, cb=0):
    return pl.BlockSpec((8, w), lambda i: (jnp.minimum((i + 1) * (tb // 8), n_rows // 8 - 1), cb))


def _const(shape):
    return pl.BlockSpec(shape, lambda *_: (0,) * len(shape))


def _silu(x):
    return x * jax.nn.sigmoid(x)


def _dsilu(x):
    s = jax.nn.sigmoid(x)
    return s * (1.0 + x * (1.0 - s))


def _shift_down(cur, prev8, j):
    rolled = pltpu.roll(cur, j, 0)
    row = lax.broadcasted_iota(jnp.int32, (8, cur.shape[1]), 0)
    first = jnp.where(row < j, pltpu.roll(prev8, j, 0), rolled[0:8])
    return jnp.concatenate([first, rolled[8:]], axis=0)


def _shift_up(cur, next8, j):
    t = cur.shape[0]
    rolled = pltpu.roll(cur, t - j, 0)
    row = lax.broadcasted_iota(jnp.int32, (8, cur.shape[1]), 0)
    last = jnp.where(row >= 8 - j, pltpu.roll(next8, 8 - j, 0), rolled[t - 8:t])
    return jnp.concatenate([rolled[:t - 8], last], axis=0)


def _colsum(x):
    return jnp.sum(x, axis=0, keepdims=True)


def _init_acc(*refs):
    @pl.when(pl.program_id(0) == 0)
    def _():
        for r in refs:
            r[...] = jnp.zeros_like(r)


def _call(body, *, name, out_shape, grid, in_specs, out_specs, scratch_shapes, args, carry=None):
    out_shape, out_specs, in_specs = tuple(out_shape), tuple(out_specs), list(in_specs)
    scratch_shapes = list(scratch_shapes)
    if carry is None:
        return pl.pallas_call(body, name=name, out_shape=out_shape, grid=grid, in_specs=in_specs, out_specs=out_specs,
                              scratch_shapes=scratch_shapes, compiler_params=_params(len(grid)))(*args)
    n_in, n_out, n_scr = len(in_specs), len(out_shape), len(scratch_shapes)

    def wrapped(*refs):
        ins, refs = refs[:n_in], refs[n_in:]
        x_ins, refs = refs[:carry.n_in], refs[carry.n_in:]
        outs, refs = refs[:n_out], refs[n_out:]
        x_outs, refs = refs[:carry.n_out], refs[carry.n_out:]
        scr, x_sems = refs[:n_scr], refs[n_scr:]
        at = [pl.program_id(d) for d in range(len(grid))]
        first = functools.reduce(lambda p, q: p & q, [i == 0 for i in at])
        last = functools.reduce(lambda p, q: p & q, [i == g - 1 for i, g in zip(at, grid)])
        pl.when(first)(lambda: carry.start(x_ins, x_outs, x_sems))
        body(*ins, *outs, *scr)
        pl.when(last)(lambda: carry.wait(x_ins, x_outs, x_sems))

    return pl.pallas_call(
        wrapped, name=name, out_shape=out_shape + carry.out_shapes, grid=grid, in_specs=in_specs + [ANY] * carry.n_in,
        out_specs=out_specs + (ANY,) * carry.n_out, scratch_shapes=scratch_shapes + carry.scratch,
        compiler_params=_params(len(grid)))(*args, *carry.arrays)


def _mm(a, b, *, name, ta=False, tb=False, out_dtype=F32, tm=512, tn=512, tk=512, bias=None, carry=None):
    m, k = (a.shape[1], a.shape[0]) if ta else a.shape
    n = b.shape[0] if tb else b.shape[1]
    assert k == (b.shape[1] if tb else b.shape[0]), (name, a.shape, b.shape)
    tm, tn, tk = min(tm, m), min(tn, n), min(tk, k)
    nk = k // tk
    assert m % tm == 0 and n % tn == 0 and k % tk == 0, (name, m, n, k)

    def body(*refs):
        if bias is None:
            a_ref, b_ref, o_ref, acc = refs
        else:
            a_ref, b_ref, bias_ref, o_ref, acc = refs
        kk = pl.program_id(2)

        @pl.when(kk == 0)
        def _():
            acc[...] = jnp.zeros_like(acc)

        dims = (((0 if ta else 1,), (1 if tb else 0,)), ((), ()))
        acc[...] += lax.dot_general(a_ref[...].astype(MXU_DTYPE), b_ref[...].astype(MXU_DTYPE), dims,
                                    preferred_element_type=F32)

        @pl.when(kk == nk - 1)
        def _():
            r = acc[...]
            if bias is not None:
                r = r + bias_ref[...]
            o_ref[...] = r.astype(out_dtype)

    a_spec = (pl.BlockSpec((tk, tm), lambda i, j, kk: (kk, i)) if ta
              else pl.BlockSpec((tm, tk), lambda i, j, kk: (i, kk)))
    b_spec = (pl.BlockSpec((tn, tk), lambda i, j, kk: (j, kk)) if tb
              else pl.BlockSpec((tk, tn), lambda i, j, kk: (kk, j)))
    in_specs, args = [a_spec, b_spec], [a, b]
    if bias is not None:
        in_specs.append(pl.BlockSpec((1, tn), lambda i, j, kk: (0, j)))
        args.append(bias)
    out = _call(body, name=name, out_shape=[SDS((m, n), out_dtype)], grid=(m // tm, n // tn, nk), in_specs=in_specs,
                out_specs=[pl.BlockSpec((tm, tn), lambda i, j, kk: (i, j))],
                scratch_shapes=[pltpu.VMEM((tm, tn), F32)], args=args, carry=carry)
    return out[0] if carry is None else out


def _silu_rows(c_all):
    def body(c_ref, o_ref):
        o_ref[...] = _silu(c_ref[...])
    return pl.pallas_call(body, name="cond_silu", out_shape=SDS(c_all.shape, F32))(c_all)


def _modulate(x, scale, shift, tb):
    s, d = x.shape

    def body(x_ref, sc_ref, sh_ref, o_ref):
        o_ref[...] = (x_ref[...] * (1.0 + sc_ref[...]) + sh_ref[...]).astype(MXU_DTYPE)

    return pl.pallas_call(body, name="modulate", out_shape=SDS((s, d), MXU_DTYPE), grid=(s // tb,),
                          in_specs=[_rows(tb, d), _const((1, d)), _const((1, d))], out_specs=_rows(tb, d),
                          compiler_params=_params(1))(x, scale, shift)


def _out_ln(ycat, w_out, x, gate, ln_g, ln_b, next_scale, next_shift, tb):
    s, d = x.shape

    def body(yc_ref, w_ref, x_ref, gt_ref, g_ref, b_ref, sc_ref, sh_ref, xn_ref, xh_ref, y_ref, rs_ref, hn_ref):
        y = jnp.dot(yc_ref[...], w_ref[...], preferred_element_type=F32)
        res = ALPHA * x_ref[...] + (1.0 + gt_ref[...]) * y
        mu = jnp.mean(res, axis=-1, keepdims=True)
        cen = res - mu
        var = jnp.mean(cen * cen, axis=-1, keepdims=True)
        rstd = lax.rsqrt(var + LN_EPS)
        xhat = cen * rstd
        xn = xhat * g_ref[...] + b_ref[...]
        xn_ref[...] = xn
        xh_ref[...] = xhat
        y_ref[...] = y
        rs_ref[...] = rstd
        hn_ref[...] = (xn * (1.0 + sc_ref[...]) + sh_ref[...]).astype(MXU_DTYPE)

    big = SDS((s, d), F32)
    return pl.pallas_call(
        body, name="out_proj_ln", out_shape=(big, big, big, SDS((s, 1), F32), SDS((s, d), MXU_DTYPE)), grid=(s // tb,),
        in_specs=[_rows(tb, d), pl.BlockSpec((d, d), lambda i: (0, 0), pipeline_mode=pl.Buffered(1)), _rows(tb, d)]
        + [_const((1, d))] * 5,
        out_specs=(_rows(tb, d), _rows(tb, d), _rows(tb, d), _rows(tb, 1), _rows(tb, d)), compiler_params=_params(1),
    )(ycat, w_out, x, gate, ln_g, ln_b, next_scale, next_shift)


def _ln_bwd(dxn, xhat, y, rstd, ln_g, gate, tb):
    s, d = dxn.shape

    def body(dxn_ref, xh_ref, y_ref, rs_ref, g_ref, gt_ref, dy_ref, dxa_ref, dg_ref, db_ref, dgt_ref):
        _init_acc(dg_ref, db_ref, dgt_ref)
        dxn_t, xh = dxn_ref[...], xh_ref[...]
        dxh = dxn_t * g_ref[...]
        dres = rs_ref[...] * (dxh - jnp.mean(dxh, axis=-1, keepdims=True)
                              - xh * jnp.mean(dxh * xh, axis=-1, keepdims=True))
        dy_ref[...] = ((1.0 + gt_ref[...]) * dres).astype(MXU_DTYPE)
        dxa_ref[...] = ALPHA * dres
        dg_ref[...] += _colsum(dxn_t * xh)
        db_ref[...] += _colsum(dxn_t)
        dgt_ref[...] += _colsum(dres * y_ref[...])

    vec = SDS((1, d), F32)
    return pl.pallas_call(
        body, name="ln_bwd", out_shape=(SDS((s, d), MXU_DTYPE), SDS((s, d), F32), vec, vec, vec), grid=(s // tb,),
        in_specs=[_rows(tb, d), _rows(tb, d), _rows(tb, d), _rows(tb, 1), _const((1, d)), _const((1, d))],
        out_specs=(_rows(tb, d), _rows(tb, d), _const((1, d)), _const((1, d)), _const((1, d))),
        compiler_params=_params(1))(dxn, xhat, y, rstd, ln_g, gate)


def _dh_mod_bwd(dproj, w_in, dxa, x, scale, carry=None):
    s, d = dxa.shape
    k = dproj.shape[1]
    tm, tn, tk = min(1024, s), 1024, 1536
    nk = k // tk
    assert s % tm == 0 and d % tn == 0 and k % tk == 0

    def body(a_ref, b_ref, dxa_ref, x_ref, sc_ref, dx_ref, dsh_ref, dsc_ref, acc):
        i, kk = pl.program_id(1), pl.program_id(2)

        @pl.when(kk == 0)
        def _():
            acc[...] = jnp.zeros_like(acc)

        @pl.when((kk == 0) & (i == 0))
        def _():
            dsh_ref[...] = jnp.zeros_like(dsh_ref)
            dsc_ref[...] = jnp.zeros_like(dsc_ref)

        acc[...] += lax.dot_general(a_ref[...], b_ref[...], (((1,), (1,)), ((), ())), preferred_element_type=F32)

        @pl.when(kk == nk - 1)
        def _():
            dh_t = acc[...]
            dx_ref[...] = dxa_ref[...] + dh_t * (1.0 + sc_ref[...])
            dsh_ref[...] += _colsum(dh_t)
            dsc_ref[...] += _colsum(dh_t * x_ref[...])

    tile = pl.BlockSpec((tm, tn), lambda j, i, kk: (i, j))
    vec = pl.BlockSpec((1, tn), lambda j, i, kk: (0, j))
    return _call(
        body, name="dh", out_shape=(SDS((s, d), F32), SDS((1, d), F32), SDS((1, d), F32)),
        grid=(d // tn, s // tm, nk),
        in_specs=[pl.BlockSpec((tm, tk), lambda j, i, kk: (i, kk)), pl.BlockSpec((tn, tk), lambda j, i, kk: (j, kk)),
                  tile, tile, vec],
        out_specs=(tile, vec, vec), scratch_shapes=[pltpu.VMEM((tm, tn), F32)],
        args=(dproj, w_in, dxa, x, scale), carry=carry)


def _out_ln_loss(ycat, w_out, x, gate, ln_g, ln_b, target, tb):
    s, d = x.shape

    def body(yc_ref, w_ref, x_ref, gt_ref, g_ref, b_ref, t_ref, l_ref, dy_ref, dxa_ref, dg_ref, db_ref, dgt_ref):
        _init_acc(l_ref, dg_ref, db_ref, dgt_ref)
        y = jnp.dot(yc_ref[...], w_ref[...], preferred_element_type=F32)
        res = ALPHA * x_ref[...] + (1.0 + gt_ref[...]) * y
        cen = res - jnp.mean(res, axis=-1, keepdims=True)
        rstd = lax.rsqrt(jnp.mean(cen * cen, axis=-1, keepdims=True) + LN_EPS)
        xh = cen * rstd
        err = xh * g_ref[...] + b_ref[...] - t_ref[...]
        l_ref[...] += (0.5 / d) * jnp.sum(err * err, keepdims=True)
        dxn_t = err * (1.0 / d)
        dxh = dxn_t * g_ref[...]
        dres = rstd * (dxh - jnp.mean(dxh, axis=-1, keepdims=True) - xh * jnp.mean(dxh * xh, axis=-1, keepdims=True))
        dy_ref[...] = ((1.0 + gt_ref[...]) * dres).astype(MXU_DTYPE)
        dxa_ref[...] = ALPHA * dres
        dg_ref[...] += _colsum(dxn_t * xh)
        db_ref[...] += _colsum(dxn_t)
        dgt_ref[...] += _colsum(dres * y)

    vec = SDS((1, d), F32)
    return pl.pallas_call(
        body, name="out_proj_ln_loss", out_shape=(SDS((1, 1), F32), SDS((s, d), MXU_DTYPE), SDS((s, d), F32), vec, vec, vec),
        grid=(s // tb,),
        in_specs=[_rows(tb, d), pl.BlockSpec((d, d), lambda i: (0, 0), pipeline_mode=pl.Buffered(1)), _rows(tb, d),
                  _const((1, d)), _const((1, d)), _const((1, d)), _rows(tb, d)],
        out_specs=(_const((1, 1)), _rows(tb, d), _rows(tb, d), _const((1, d)), _const((1, d)), _const((1, d))),
        compiler_params=_params(1))(ycat, w_out, x, gate, ln_g, ln_b, target)


def _conv_taps(u, up, w_ref, width):
    out = w_ref[width - 1:width, :] * u
    for j in range(width - 2, -1, -1):
        out = out + w_ref[j:j + 1, :] * _shift_down(u, up, width - 1 - j)
    return out


def _conv_taps_t(g, gn, w_ref, width):
    out = w_ref[width - 1:width, :] * g
    for j in range(width - 2, -1, -1):
        out = out + w_ref[j:j + 1, :] * _shift_up(g, gn, width - 1 - j)
    return out


def _conv_wgrad(dw_ref, g, u, up, width):
    dw_ref[width - 1:width, :] += _colsum(g * u)
    for j in range(width - 1):
        dw_ref[j:j + 1, :] += _colsum(g * _shift_down(u, up, width - 1 - j))


def _branch_a_fwd(proj, conv_w, tb):
    s = proj.shape[0]

    def body(ab, ac, ax, ag, acp, axp, w_ref, o_ref):
        has_prev = (pl.program_id(0) > 0).astype(F32)
        u = ac[...] * ax[...]
        up = acp[...] * axp[...] * has_prev
        o_ref[...] = (ab[...] * _conv_taps(u, up, w_ref, 3) * _silu(ag[...])).astype(MXU_DTYPE)

    return pl.pallas_call(
        body, name="branch_a_fwd", out_shape=SDS((s, BR), MXU_DTYPE), grid=(s // tb,),
        in_specs=[_rows(tb, BR, CB_AB), _rows(tb, BR, CB_AC), _rows(tb, BR, CB_AX), _rows(tb, BR, CB_AG),
                  _prev8(tb, BR, CB_AC), _prev8(tb, BR, CB_AX), _const((8, BR))],
        out_specs=_rows(tb, BR), compiler_params=_params(1))(proj, proj, proj, proj, proj, proj, conv_w)


def _branch_a_bwd(dycat, proj, conv_w, tb):
    s = proj.shape[0]

    def body(dy, dyn, ab, abn, ag, agn, ac, acp, ax, axp, w_ref, o_ref, dw_ref):
        _init_acc(dw_ref)
        i = pl.program_id(0)
        has_prev = (i > 0).astype(F32)
        has_next = (i < pl.num_programs(0) - 1).astype(F32)
        u = ac[...] * ax[...]
        up = acp[...] * axp[...] * has_prev
        v = _conv_taps(u, up, w_ref, 3)
        sg = _silu(ag[...])
        dv = dy[...] * ab[...] * sg
        dvn = dyn[...] * abn[...] * _silu(agn[...]) * has_next
        du = _conv_taps_t(dv, dvn, w_ref, 3)
        o_ref[:, 0:BR] = (dy[...] * v * sg).astype(MXU_DTYPE)
        o_ref[:, BR:2 * BR] = (du * ax[...]).astype(MXU_DTYPE)
        o_ref[:, 2 * BR:3 * BR] = (du * ac[...]).astype(MXU_DTYPE)
        o_ref[:, 3 * BR:4 * BR] = (dy[...] * ab[...] * v * _dsilu(ag[...])).astype(MXU_DTYPE)
        _conv_wgrad(dw_ref, dv, u, up, 3)

    return pl.pallas_call(
        body, name="branch_a_bwd", out_shape=(SDS((s, 4 * BR), MXU_DTYPE), SDS((8, BR), F32)), grid=(s // tb,),
        in_specs=[_rows(tb, BR, 0), _next8(tb, BR, s, 0),
                  _rows(tb, BR, CB_AB), _next8(tb, BR, s, CB_AB), _rows(tb, BR, CB_AG), _next8(tb, BR, s, CB_AG),
                  _rows(tb, BR, CB_AC), _prev8(tb, BR, CB_AC), _rows(tb, BR, CB_AX), _prev8(tb, BR, CB_AX),
                  _const((8, BR))],
        out_specs=(_rows(tb, 4 * BR), _const((8, BR))), compiler_params=_params(1),
    )(dycat, dycat, proj, proj, proj, proj, proj, proj, proj, proj, conv_w)


def _t5_bucket(dist):
    max_exact = REL_BUCKETS // 2
    nf = jnp.maximum(dist, 1).astype(F32)
    large = max_exact + (jnp.log(nf / max_exact) / math.log(REL_MAX_DIST / max_exact)
                         * (REL_BUCKETS - max_exact)).astype(jnp.int32)
    large = jnp.minimum(large, REL_BUCKETS - 1)
    return jnp.where(dist < max_exact, dist, large)


def _bucket_maps():
    maps = []
    i = jnp.arange(BLK)[:, None]
    j = jnp.arange(2 * BLK)[None, :]
    delta = i + BLK - j
    for window, dil in DILATIONS:
        span = window // dil
        bucket = _t5_bucket(jnp.clip(delta, 0, span) * dil)
        maps.append(jnp.where((delta >= 0) & (delta <= span), bucket, -1))
    return jnp.stack(maps).astype(jnp.int32)


def _bias_tables(rel_bias, buckets):
    n_pat = len(DILATIONS)

    def body(rb_ref, bk_ref, o_ref):
        for g in range(n_pat):
            bk = bk_ref[g]
            for h in range(ATT_HEADS):
                def per_bucket(b, acc):
                    return jnp.where(bk == b, rb_ref[b, h], acc)
                o_ref[g, h] = lax.fori_loop(0, REL_BUCKETS, per_bucket, jnp.full((BLK, 2 * BLK), NEG, F32))

    return pl.pallas_call(
        body, name="bias_tables", out_shape=SDS((n_pat, ATT_HEADS, BLK, 2 * BLK), F32),
        in_specs=[pl.BlockSpec(memory_space=pltpu.SMEM), pl.BlockSpec(memory_space=pltpu.VMEM)],
        compiler_params=_params())(rel_bias, buckets)


def _head_masks():
    lane = lax.broadcasted_iota(jnp.int32, (1, 2 * HEAD_DIM), 1)
    return [(lane < HEAD_DIM).astype(F32), (lane >= HEAD_DIM).astype(F32)]


def _strided(base, size, dil):
    return pl.ds(base, size, stride=dil) if dil > 1 else pl.ds(pl.multiple_of(base, BLK), size)


def _attn_groups(s, dil):
    return max(1, min(1024, s) // (dil * BLK)) if dil == 1 else max(1, min(2048, s) // (dil * BLK))


def _attn_fwd(proj, bias, dil):
    s = proj.shape[0]
    grp = _attn_groups(s, dil)
    u1 = dil * BLK
    unit = grp * u1
    nb = s // unit
    w = 2 * HEAD_DIM
    q0, k0, v0 = (cb * (BR // w) for cb in (CB_Q, CB_K, CB_V))

    def body(q_ref, kc_ref, kp_ref, vc_ref, vp_ref, bias_ref, o_ref, lse_ref, kbuf, vbuf):
        n = pl.program_id(1)
        col = lax.broadcasted_iota(jnp.int32, (1, 2 * BLK), 1)
        masks = _head_masks()
        kbuf[0:u1, :] = kp_ref[...]
        kbuf[u1:, :] = kc_ref[...]
        vbuf[0:u1, :] = vp_ref[...]
        vbuf[u1:, :] = vc_ref[...]

        def per_r(t, carry):
            j = t // dil
            base = j * u1 + t % dil
            rows = _strided(base, BLK, dil)
            no_prev = jnp.where((n == 0) & (j == 0) & (col < BLK), NEG, 0.0)
            q = q_ref[rows, :] * (HEAD_DIM ** -0.5)
            k = kbuf[_strided(base, 2 * BLK, dil), :].astype(MXU_DTYPE)
            v = vbuf[_strided(base, 2 * BLK, dil), :].astype(MXU_DTYPE)
            q2 = jnp.concatenate([q * masks[0], q * masks[1]], axis=0).astype(MXU_DTYPE)
            sc = lax.dot_general(q2, k, (((1,), (1,)), ((), ())), preferred_element_type=F32)
            sc = sc + jnp.concatenate([bias_ref[0], bias_ref[1]], axis=0) + no_prev
            mx = jnp.max(sc, axis=-1, keepdims=True)
            p = jnp.exp(sc - mx)
            l = jnp.sum(p, axis=-1, keepdims=True)
            o2 = jnp.dot((p / l).astype(MXU_DTYPE), v, preferred_element_type=F32)
            lse2 = mx + jnp.log(l)
            o_ref[rows, :] = o2[0:BLK] * masks[0] + o2[BLK:2 * BLK] * masks[1]
            lse_ref[rows, :] = lse2[0:BLK] * masks[0] + lse2[BLK:2 * BLK] * masks[1]
            return carry

        lax.fori_loop(0, grp * dil, per_r, 0, unroll=8)

    cur = lambda c0: pl.BlockSpec((unit, w), lambda hp, n: (n, c0 + hp))
    prev = lambda c0: pl.BlockSpec((u1, w), lambda hp, n: (jnp.maximum(n * grp - 1, 0), c0 + hp))
    out = pl.BlockSpec((unit, w), lambda hp, n: (n, hp))
    return pl.pallas_call(
        body, name=f"attn_fwd_d{dil}", out_shape=(SDS((s, BR), F32), SDS((s, BR), F32)), grid=(BR // w, nb),
        in_specs=[cur(q0), cur(k0), prev(k0), cur(v0), prev(v0),
                  pl.BlockSpec((2, BLK, 2 * BLK), lambda hp, n: (hp, 0, 0))],
        out_specs=(out, out),
        scratch_shapes=[pltpu.VMEM((unit + u1, w), F32), pltpu.VMEM((unit + u1, w), F32)],
        compiler_params=_params(2))(proj, proj, proj, proj, proj, bias)


def _softmax3(l0, l1, l2):
    mx = jnp.maximum(jnp.maximum(l0, l1), l2)
    e0, e1, e2 = jnp.exp(l0 - mx), jnp.exp(l1 - mx), jnp.exp(l2 - mx)
    inv = 1.0 / (e0 + e1 + e2)
    return e0 * inv, e1 * inv, e2 * inv


def _attn_combine(os_, lses, proj, tb):
    s = proj.shape[0]

    def body(o0, o1, o2, l0, l1, l2, bg, y_ref):
        w0, w1, w2 = _softmax3(l0[...], l1[...], l2[...])
        attn = w0 * o0[...] + w1 * o1[...] + w2 * o2[...]
        y_ref[...] = (attn * _silu(bg[...])).astype(MXU_DTYPE)

    return pl.pallas_call(
        body, name="attn_combine", out_shape=SDS((s, BR), MXU_DTYPE), grid=(s // tb,),
        in_specs=[_rows(tb, BR)] * 6 + [_rows(tb, BR, CB_BG)], out_specs=_rows(tb, BR),
        compiler_params=_params(1))(*os_, *lses, proj)


def _attn_bwd_pre(dycat, os_, lses, proj, head_ones, tb):
    s = proj.shape[0]

    def body(dy, o0, o1, o2, l0, l1, l2, bg, e_ref, dbg_ref, do0, do1, do2, dm0, dm1, dm2):
        w0, w1, w2 = _softmax3(l0[...], l1[...], l2[...])
        attn = w0 * o0[...] + w1 * o1[...] + w2 * o2[...]
        dattn = dy[...] * _silu(bg[...])
        dbg_ref[...] = (dy[...] * attn * _dsilu(bg[...])).astype(MXU_DTYPE)
        prod = dattn * attn
        hi = prod.astype(MXU_DTYPE)
        lo = (prod - hi.astype(F32)).astype(MXU_DTYPE)
        tot = (jnp.dot(hi, e_ref[...], preferred_element_type=F32)
               + jnp.dot(lo, e_ref[...], preferred_element_type=F32))
        for wg, do_ref, dm_ref in ((w0, do0, dm0), (w1, do1, dm1), (w2, do2, dm2)):
            do_ref[...] = wg * dattn
            dm_ref[...] = wg * tot

    big = SDS((s, BR), F32)
    return pl.pallas_call(
        body, name="attn_bwd_pre", out_shape=(SDS((s, BR), MXU_DTYPE),) + (big,) * 6, grid=(s // tb,),
        in_specs=[_rows(tb, BR, 1)] + [_rows(tb, BR)] * 6 + [_rows(tb, BR, CB_BG), _const((BR, BR))],
        out_specs=(_rows(tb, BR),) * 7, compiler_params=_params(1))(dycat, *os_, *lses, proj, head_ones)


def _attn_bwd(proj, do, lse, dm, bias, dil, carry=None):
    s = proj.shape[0]
    grp = _attn_groups(s, dil)
    u1 = dil * BLK
    unit = grp * u1
    nb = s // unit
    w = 2 * HEAD_DIM
    q0, k0, v0 = (cb * (BR // w) for cb in (CB_Q, CB_K, CB_V))

    def body(q_ref, kc_ref, kp_ref, vc_ref, vp_ref, do_ref, lse_ref, dm_ref, bias_ref,
             dq_ref, dk_ref, dv_ref, dbias_ref, kbuf, vbuf, stage_k, stage_v):
        n = pl.program_id(1)
        col = lax.broadcasted_iota(jnp.int32, (1, 2 * BLK), 1)
        masks = _head_masks()

        @pl.when(n == 0)
        def _():
            dbias_ref[...] = jnp.zeros_like(dbias_ref)
            stage_k[...] = jnp.zeros_like(stage_k)
            stage_v[...] = jnp.zeros_like(stage_v)

        for out_ref, stage in ((dk_ref, stage_k), (dv_ref, stage_v)):
            if grp > 1:
                out_ref[0:unit - u1, :] = stage[u1:unit, :]
            stage[0:u1, :] = stage[unit:unit + u1, :]

        @pl.when(n < nb)
        def _():
            kbuf[0:u1, :] = kp_ref[...]
            kbuf[u1:, :] = kc_ref[...]
            vbuf[0:u1, :] = vp_ref[...]
            vbuf[u1:, :] = vc_ref[...]

            def per_r(t, carry):
                j = t // dil
                base = j * u1 + t % dil
                rows = _strided(base, BLK, dil)
                rows_hi = _strided(base + u1, BLK, dil)
                no_prev = jnp.where((n == 0) & (j == 0) & (col < BLK), NEG, 0.0)
                q = q_ref[rows, :] * (HEAD_DIM ** -0.5)
                k = kbuf[_strided(base, 2 * BLK, dil), :].astype(MXU_DTYPE)
                v = vbuf[_strided(base, 2 * BLK, dil), :].astype(MXU_DTYPE)
                do_t, lse_t, dm_t = do_ref[rows, :], lse_ref[rows, :], dm_ref[rows, :]
                stack = lambda t: jnp.concatenate([t * masks[0], t * masks[1]], axis=0).astype(MXU_DTYPE)
                per_head = lambda t: jnp.concatenate([t[:, 0:1], t[:, HEAD_DIM:HEAD_DIM + 1]], axis=0)
                q2, do2 = stack(q), stack(do_t)
                sc = lax.dot_general(q2, k, (((1,), (1,)), ((), ())), preferred_element_type=F32)
                p = jnp.exp(sc + jnp.concatenate([bias_ref[0], bias_ref[1]], axis=0) + no_prev - per_head(lse_t))
                dp = lax.dot_general(do2, v, (((1,), (1,)), ((), ())), preferred_element_type=F32)
                ds = p * (dp - per_head(dm_t))
                dbias_ref[0] += ds[0:BLK]
                dbias_ref[1] += ds[BLK:2 * BLK]
                dsb, pb = ds.astype(MXU_DTYPE), p.astype(MXU_DTYPE)
                dq2 = jnp.dot(dsb, k, preferred_element_type=F32)
                dk_acc = lax.dot_general(dsb, q2, (((0,), (0,)), ((), ())), preferred_element_type=F32)
                dv_acc = lax.dot_general(pb, do2, (((0,), (0,)), ((), ())), preferred_element_type=F32)
                dq_ref[rows, :] = (dq2[0:BLK] * masks[0] + dq2[BLK:2 * BLK] * masks[1]) * (HEAD_DIM ** -0.5)
                stage_k[rows, :] = stage_k[rows, :] + dk_acc[0:BLK]
                stage_v[rows, :] = stage_v[rows, :] + dv_acc[0:BLK]
                stage_k[rows_hi, :] = dk_acc[BLK:2 * BLK]
                stage_v[rows_hi, :] = dv_acc[BLK:2 * BLK]
                return carry

            lax.fori_loop(0, grp * dil, per_r, 0, unroll=8)

        dk_ref[unit - u1:unit, :] = stage_k[0:u1, :]
        dv_ref[unit - u1:unit, :] = stage_v[0:u1, :]

    qn = lambda n: jnp.minimum(n, nb - 1)
    cur = lambda c0: pl.BlockSpec((unit, w), lambda hp, n: (qn(n), c0 + hp))
    prev = lambda c0: pl.BlockSpec((u1, w), lambda hp, n: (jnp.maximum(qn(n) * grp - 1, 0), c0 + hp))
    row = pl.BlockSpec((unit, w), lambda hp, n: (qn(n), hp))
    late = pl.BlockSpec((unit, w), lambda hp, n: (jnp.maximum(n - 1, 0), hp))
    tab = pl.BlockSpec((2, BLK, 2 * BLK), lambda hp, n: (hp, 0, 0))
    big = SDS((s, BR), F32)
    return _call(
        body, name=f"attn_bwd_d{dil}", out_shape=(big, big, big, SDS((ATT_HEADS, BLK, 2 * BLK), F32)),
        grid=(BR // w, nb + 1),
        in_specs=[cur(q0), cur(k0), prev(k0), cur(v0), prev(v0), row, row, row, tab],
        out_specs=(row, late, late, tab),
        scratch_shapes=[pltpu.VMEM((unit + u1, w), F32)] * 4,
        args=(proj, proj, proj, proj, proj, do, lse, dm, bias), carry=carry)


def _rel_bias_grad(dbias, buckets):
    def body(db_ref, bk_ref, o_ref):
        row = lax.broadcasted_iota(jnp.int32, (REL_BUCKETS, 128), 0)
        lane = lax.broadcasted_iota(jnp.int32, (REL_BUCKETS, 128), 1)

        def per_bucket(b, acc):
            for g in range(len(DILATIONS)):
                hit = bk_ref[g] == b
                for h in range(ATT_HEADS):
                    both = db_ref[0, g, h] + db_ref[1, g, h]
                    val = jnp.sum(jnp.where(hit, both, 0.0), keepdims=True)
                    acc = acc + jnp.where((row == b) & (lane == h), val, 0.0)
            return acc

        o_ref[...] = lax.fori_loop(0, REL_BUCKETS, per_bucket, jnp.zeros((REL_BUCKETS, 128), F32))

    assert dbias.shape[0] == DEPTH == 2
    return pl.pallas_call(body, name="rel_bias_grad", out_shape=SDS((REL_BUCKETS, 128), F32),
                          compiler_params=_params())(dbias, buckets)


def _scan_rows(a_ref, b_ref, o_ref, carry, *, reverse):
    tb = a_ref.shape[0]
    order = range(7, -1, -1) if reverse else range(8)

    @pl.when(pl.program_id(0) == 0)
    def _():
        carry[...] = jnp.zeros_like(carry)

    def group(gi, h):
        r0 = pl.multiple_of((tb // 8 - 1 - gi if reverse else gi) * 8, 8)
        a8, b8 = a_ref[pl.ds(r0, 8), :], b_ref[pl.ds(r0, 8), :]
        rows = [None] * 8
        for k in order:
            if reverse:
                rows[k] = b8[k:k + 1] + h
                h = a8[k:k + 1] * rows[k]
            else:
                h = a8[k:k + 1] * h + b8[k:k + 1]
                rows[k] = h
        o_ref[pl.ds(r0, 8), :] = jnp.concatenate(rows, axis=0)
        return h

    carry[...] = lax.fori_loop(0, tb // 8, group, carry[...])


def _lru_scan_fwd(a, b, proj, tb):
    s = a.shape[0]

    def body(a_ref, b_ref, g_ref, h_ref, y_ref, carry):
        _scan_rows(a_ref, b_ref, h_ref, carry, reverse=False)
        y_ref[...] = (h_ref[...] * _silu(g_ref[...])).astype(MXU_DTYPE)

    return pl.pallas_call(
        body, name="lru_scan", out_shape=(SDS((s, BR), F32), SDS((s, BR), MXU_DTYPE)), grid=(s // tb,),
        in_specs=[_rows(tb, BR), _rows(tb, BR), _rows(tb, BR, CB_CG)], out_specs=(_rows(tb, BR), _rows(tb, BR)),
        scratch_shapes=[pltpu.VMEM((1, BR), F32)], compiler_params=_params(1))(a, b, proj)


def _lru_scan_bwd(a, dycat, h, proj, tb):
    s = a.shape[0]
    nt = s // tb

    def body(a_ref, dy_ref, h_ref, g_ref, l_ref, dg_ref, carry, dh_buf):
        dh_buf[...] = dy_ref[...] * _silu(g_ref[...])
        dg_ref[...] = (dy_ref[...] * h_ref[...] * _dsilu(g_ref[...])).astype(MXU_DTYPE)
        _scan_rows(a_ref, dh_buf, l_ref, carry, reverse=True)

    rev = lambda cb=0: pl.BlockSpec((tb, BR), lambda i: (nt - 1 - i, cb))
    return pl.pallas_call(
        body, name="lru_scan_bwd", out_shape=(SDS((s, BR), F32), SDS((s, BR), MXU_DTYPE)), grid=(nt,),
        in_specs=[rev(), rev(2), rev(), rev(CB_CG)], out_specs=(rev(), rev()),
        scratch_shapes=[pltpu.VMEM((1, BR), F32), pltpu.VMEM((tb, BR), F32)],
        compiler_params=_params(1))(a, dycat, h, proj)


def _scan_tile(s):
    return min(512, s)


def _load_chunked(ref, t0, pt):
    ln = pt // 8
    return jnp.concatenate([ref[pl.ds(t0 + j, 8, stride=ln), :] for j in range(ln)], axis=0)


def _store_natural(ref, t0, pt, val):
    ln = pt // 8
    for j in range(ln):
        ref[pl.ds(t0 + j, 8, stride=ln), :] = val[j * 8:(j + 1) * 8]


def _scan_tile_in_place(a_ref, x_ref, carry, pw, *, reverse):
    ch2 = x_ref.shape[1]
    ch = ch2 // 2
    ln = x_ref.shape[0] // 8
    ar = a_ref[:, 0:ch]
    ai = -a_ref[:, ch:ch2] if reverse else a_ref[:, ch:ch2]

    def cmul(pr, pi, xr, xi):
        return pr * xr - pi * xi, pr * xi + pi * xr

    @pl.when(pl.program_id(0) == 0)
    def _():
        carry[...] = jnp.zeros_like(carry)

        def fill(j, p):
            pw[pl.ds(j, 1), 0:ch] = p[0]
            pw[pl.ds(j, 1), ch:ch2] = p[1]
            return cmul(ar, ai, *p)

        lax.fori_loop(0, ln, fill, (ar, ai))

    def rows_of(j):
        return pl.ds(pl.multiple_of((ln - 1 - j if reverse else j) * 8, 8), 8)

    def local(j, x):
        rows = rows_of(j)
        nr, ni = cmul(ar, ai, *x)
        xr, xi = nr + x_ref[rows, 0:ch], ni + x_ref[rows, ch:ch2]
        x_ref[rows, 0:ch] = xr
        x_ref[rows, ch:ch2] = xi
        return xr, xi

    zero = jnp.zeros((8, ch), F32)
    er, ei = lax.fori_loop(0, ln, local, (zero, zero), unroll=2)
    apr, api = pw[ln - 1:ln, 0:ch], pw[ln - 1:ln, ch:ch2]
    cr, ci = carry[:, 0:ch], carry[:, ch:ch2]
    into_r, into_i = [None] * 8, [None] * 8
    for c in (range(7, -1, -1) if reverse else range(8)):
        into_r[c], into_i[c] = cr, ci
        pr, pi = cmul(apr, api, cr, ci)
        cr, ci = er[c:c + 1] + pr, ei[c:c + 1] + pi
    carry[:, 0:ch] = cr
    carry[:, ch:ch2] = ci
    into_r, into_i = jnp.concatenate(into_r, axis=0), jnp.concatenate(into_i, axis=0)

    def fix(j, carry_):
        rows = rows_of(j)
        dr, di = cmul(pw[pl.ds(j, 1), 0:ch], pw[pl.ds(j, 1), ch:ch2], into_r, into_i)
        x_ref[rows, 0:ch] += dr
        x_ref[rows, ch:ch2] += di
        return carry_

    lax.fori_loop(0, ln, fix, 0, unroll=2)


def _neg_expm1(z):
    series = -z * (1.0 + z * (0.5 + z * (1.0 / 6 + z * (1.0 / 24 + z * (1.0 / 120)))))
    return jnp.where(z > -0.05, series, 1.0 - jnp.exp(z))


def _lru_gate(xc, pre_r, pre_i, lam):
    log_a = -LRU_C * jax.nn.sigmoid(pre_r) * jax.nn.softplus(-lam)
    return jnp.exp(log_a), jnp.sqrt(_neg_expm1(2.0 * log_a)) * jax.nn.sigmoid(pre_i) * xc


def _lru_gates_fwd(proj, conv_w, conv_b, w_cat, b_cat, lam, tb):
    s = proj.shape[0]

    def body(cx, cxp, w_ref, cb_ref, wc_ref, bc_ref, lam_ref, a_ref, b_ref):
        has_prev = (pl.program_id(0) > 0).astype(F32)
        xc = _conv_taps(cx[...], cxp[...] * has_prev, w_ref, 4) + cb_ref[...]
        pre = jnp.dot(xc.astype(MXU_DTYPE), wc_ref[...], preferred_element_type=F32) + bc_ref[...]
        a_ref[...], b_ref[...] = _lru_gate(xc, pre[:, 0:BR], pre[:, BR:2 * BR], lam_ref[...])

    big = SDS((s, BR), F32)
    return pl.pallas_call(
        body, name="lru_gates_fwd", out_shape=(big, big), grid=(s // tb,),
        in_specs=[_rows(tb, BR, CB_CX), _prev8(tb, BR, CB_CX), _const((8, BR)), _const((1, BR)),
                  _const((BR, 2 * BR)), _const((1, 2 * BR)), _const((1, BR))],
        out_specs=(_rows(tb, BR), _rows(tb, BR)), compiler_params=_params(1),
    )(proj, proj, conv_w, conv_b, w_cat, b_cat, lam)


def _lru_gates_bwd(proj, lmb, h, conv_w, conv_b, w_cat, b_cat, lam, tb):
    s = proj.shape[0]

    def body(cx, cxp, l_ref, h_ref, hp_ref, w_ref, cb_ref, wc_ref, bc_ref, lam_ref,
             dxc_ref, dpre_ref, xc_ref, dbc_ref, dlam_ref):
        _init_acc(dbc_ref, dlam_ref)
        has_prev = (pl.program_id(0) > 0).astype(F32)
        xc = _conv_taps(cx[...], cxp[...] * has_prev, w_ref, 4) + cb_ref[...]
        xcb = xc.astype(MXU_DTYPE)
        pre = jnp.dot(xcb, wc_ref[...], preferred_element_type=F32) + bc_ref[...]
        _, vjp = jax.vjp(_lru_gate, xc, pre[:, 0:BR], pre[:, BR:2 * BR], lam_ref[...])
        lm = l_ref[...]
        dxc, dpr, dpi, dlam = vjp((lm * _shift_down(h_ref[...], hp_ref[...] * has_prev, 1), lm))
        dpre = jnp.concatenate([dpr, dpi], axis=1)
        dpreb = dpre.astype(MXU_DTYPE)
        dxc_ref[...] = dxc + lax.dot_general(dpreb, wc_ref[...], (((1,), (1,)), ((), ())),
                                             preferred_element_type=F32)
        dpre_ref[...] = dpreb
        xc_ref[...] = xcb
        dbc_ref[...] += _colsum(dpre)
        dlam_ref[...] += dlam

    return pl.pallas_call(
        body, name="lru_gates_bwd",
        out_shape=(SDS((s, BR), F32), SDS((s, 2 * BR), MXU_DTYPE), SDS((s, BR), MXU_DTYPE),
                   SDS((1, 2 * BR), F32), SDS((1, BR), F32)),
        grid=(s // tb,),
        in_specs=[_rows(tb, BR, CB_CX), _prev8(tb, BR, CB_CX), _rows(tb, BR), _rows(tb, BR), _prev8(tb, BR),
                  _const((8, BR)), _const((1, BR)), _const((BR, 2 * BR)), _const((1, 2 * BR)), _const((1, BR))],
        out_specs=(_rows(tb, BR), _rows(tb, 2 * BR), _rows(tb, BR), _const((1, 2 * BR)), _const((1, BR))),
        compiler_params=_params(1))(proj, proj, lmb, h, h, conv_w, conv_b, w_cat, b_cat, lam)


def _conv_c_bwd(dxc, proj, conv_w, tb):
    s = proj.shape[0]

    def body(g, gn, cx, cxp, w_ref, dcx_ref, dw_ref, db_ref):
        _init_acc(dw_ref, db_ref)
        i = pl.program_id(0)
        has_prev = (i > 0).astype(F32)
        has_next = (i < pl.num_programs(0) - 1).astype(F32)
        gt = g[...]
        dcx_ref[...] = _conv_taps_t(gt, gn[...] * has_next, w_ref, 4).astype(MXU_DTYPE)
        _conv_wgrad(dw_ref, gt, cx[...], cxp[...] * has_prev, 4)
        db_ref[...] += _colsum(gt)

    return pl.pallas_call(
        body, name="conv_c_bwd", out_shape=(SDS((s, BR), MXU_DTYPE), SDS((8, BR), F32), SDS((1, BR), F32)),
        grid=(s // tb,),
        in_specs=[_rows(tb, BR), _next8(tb, BR, s), _rows(tb, BR, CB_CX), _prev8(tb, BR, CB_CX), _const((8, BR))],
        out_specs=(_rows(tb, BR), _const((8, BR)), _const((1, BR))), compiler_params=_params(1),
    )(dxc, dxc, proj, proj, conv_w)


def _s5_disc(lam_re, lam_im, log_dt):
    dt = jnp.exp(log_dt)
    mag = jnp.exp(lam_re * dt)
    ab_re = mag * jnp.cos(lam_im * dt)
    ab_im = mag * jnp.sin(lam_im * dt)
    den = lam_re * lam_re + lam_im * lam_im
    f_re = ((ab_re - 1.0) * lam_re + ab_im * lam_im) / den
    f_im = (ab_im * lam_re - (ab_re - 1.0) * lam_im) / den
    return ab_re, ab_im, f_re, f_im


def _s5_bbar(f_re, f_im, b_re, b_im):
    return f_re * b_re - f_im * b_im, f_re * b_im + f_im * b_re


def _s5_disc_fwd(lam_re, lam_im, log_dt):
    def body(lr, li, ld, o0, o1, o2, o3):
        o0[...], o1[...], o2[...], o3[...] = _s5_disc(lr[...], li[...], ld[...])
    return pl.pallas_call(body, name="s5_disc_fwd", out_shape=(SDS(lam_re.shape, F32),) * 4)(lam_re, lam_im, log_dt)


def _s5_disc_bwd(lam_re, lam_im, log_dt, cts):
    def body(lr, li, ld, c0, c1, c2, c3, o0, o1, o2):
        _, vjp = jax.vjp(_s5_disc, lr[...], li[...], ld[...])
        o0[...], o1[...], o2[...] = vjp((c0[...], c1[...], c2[...], c3[...]))
    return pl.pallas_call(body, name="s5_disc_bwd", out_shape=(SDS(lam_re.shape, F32), SDS(lam_re.shape, F32),
                                                                SDS(log_dt.shape, F32)))(lam_re, lam_im, log_dt, *cts)


def _s5_bbar_fwd(f_re, f_im, b_re, b_im):
    def body(fr, fi, br, bi, o0, o1):
        o0[...], o1[...] = _s5_bbar(fr[...], fi[...], br[...], bi[...])
    return pl.pallas_call(body, name="s5_bbar_fwd", out_shape=(SDS(b_re.shape, F32),) * 2)(f_re, f_im, b_re, b_im)


def _s5_bbar_bwd(f_re, f_im, b_re, b_im, d_re, d_im):
    def body(fr, fi, br, bi, dr, di, o0, o1, o2, o3):
        _, vjp = jax.vjp(_s5_bbar, fr[...], fi[...], br[...], bi[...])
        o0[...], o1[...], o2[...], o3[...] = vjp((dr[...], di[...]))
    col, mat = SDS(f_re.shape, F32), SDS(b_re.shape, F32)
    return pl.pallas_call(body, name="s5_bbar_bwd", out_shape=(col, col, mat, mat))(f_re, f_im, b_re, b_im, d_re, d_im)


def _s5_tail_bwd(dycat, ylin, proj, d_skip, w_glu, b_glu, tb):
    s = proj.shape[0]

    def body(dy, yl, u, dg, dk, w_ref, b_ref, dyl_ref, dus_ref, ddg_ref, g_ref, dt_ref, ddk_ref, dbg_ref):
        _init_acc(ddk_ref, dbg_ref)
        g, gelu_vjp = jax.vjp(jax.nn.gelu, yl[...] + dk[...] * u[...])
        gb = g.astype(MXU_DTYPE)
        sg = jax.nn.sigmoid(jnp.dot(gb, w_ref[...], preferred_element_type=F32) + b_ref[...])
        dz = dy[...] * _silu(dg[...])
        ddg_ref[...] = (dy[...] * g * sg * _dsilu(dg[...])).astype(MXU_DTYPE)
        dt = dz * g * sg * (1.0 - sg)
        dtb = dt.astype(MXU_DTYPE)
        dgel = dz * sg + lax.dot_general(dtb, w_ref[...], (((1,), (1,)), ((), ())), preferred_element_type=F32)
        dyv, = gelu_vjp(dgel)
        dyl_ref[...] = dyv
        dus_ref[...] = dyv * dk[...]
        g_ref[...] = gb
        dt_ref[...] = dtb
        ddk_ref[...] += _colsum(dyv * u[...])
        dbg_ref[...] += _colsum(dt)

    big, half, vec = SDS((s, BR), F32), SDS((s, BR), MXU_DTYPE), SDS((1, BR), F32)
    return pl.pallas_call(
        body, name="s5_tail_bwd", out_shape=(big, big, half, half, half, vec, vec), grid=(s // tb,),
        in_specs=[_rows(tb, BR, 3), _rows(tb, BR), _rows(tb, BR, CB_DU), _rows(tb, BR, CB_DG), _const((1, BR)),
                  _const((BR, BR)), _const((1, BR))],
        out_specs=(_rows(tb, BR),) * 5 + (_const((1, BR)), _const((1, BR))), compiler_params=_params(1),
    )(dycat, ylin, proj, proj, d_skip, w_glu, b_glu)


def _assemble_dproj(da, dqkv, dbg, dcx, dcg, du, dus, ddg, tb):
    s = da.shape[0]

    def body(da_ref, q0, q1, q2, k0, k1, k2, v0, v1, v2, dbg_ref, dcx_ref, dcg_ref, du_ref, dus_ref, ddg_ref, o_ref):
        o_ref[:, 0:4 * BR] = da_ref[...]
        for j, parts in enumerate(((q0, q1, q2), (k0, k1, k2), (v0, v1, v2))):
            o_ref[:, (4 + j) * BR:(5 + j) * BR] = (parts[0][...] + parts[1][...] + parts[2][...]).astype(MXU_DTYPE)
        o_ref[:, 7 * BR:8 * BR] = dbg_ref[...].astype(MXU_DTYPE)
        o_ref[:, 8 * BR:9 * BR] = dcx_ref[...].astype(MXU_DTYPE)
        o_ref[:, 9 * BR:10 * BR] = dcg_ref[...].astype(MXU_DTYPE)
        o_ref[:, 10 * BR:11 * BR] = (du_ref[...] + dus_ref[...]).astype(MXU_DTYPE)
        o_ref[:, 11 * BR:12 * BR] = ddg_ref[...].astype(MXU_DTYPE)

    flat = [t for grp in dqkv for t in grp]
    return pl.pallas_call(
        body, name="assemble_dproj", out_shape=SDS((s, N_IN), MXU_DTYPE), grid=(s // tb,),
        in_specs=[_rows(tb, 4 * BR)] + [_rows(tb, BR)] * 15, out_specs=_rows(tb, N_IN),
        compiler_params=_params(1))(da, *flat, dbg, dcx, dcg, du, dus, ddg)


def _sum_leading(xs, tr, name):
    n, _, c = xs[0].shape
    nl = len(xs)
    tr = min([tr] + [x.shape[1] for x in xs])
    assert all(x.shape[1] % tr == 0 for x in xs), (name, tr)
    nrs = [x.shape[1] // tr for x in xs]
    starts = [sum(nrs[:l]) for l in range(nl)]

    def body(*refs):
        i = pl.program_id(0)
        for l in range(nl):
            @pl.when((i >= starts[l]) & (i < starts[l] + nrs[l]))
            def _():
                acc = refs[l * n][...].astype(F32)
                for ref in refs[l * n + 1:(l + 1) * n]:
                    acc = acc + ref[...].astype(F32)
                refs[nl * n][...] = acc

    specs = [pl.BlockSpec((None, tr, c), functools.partial(
        lambda i, k, l: (k, jnp.clip(i - starts[l], 0, nrs[l] - 1), 0), k=k, l=l)) for l in range(nl) for k in range(n)]
    return pl.pallas_call(body, name=name, out_shape=SDS((sum(nrs) * tr, c), F32), grid=(sum(nrs),), in_specs=specs,
                          out_specs=pl.BlockSpec((tr, c), lambda i: (i, 0)),
                          compiler_params=_params(1))(*[x for x in xs for _ in range(n)])


def _adamw(w, g_parts, m, v, tr, name):
    r, c = w.shape
    tr = min(tr, r)
    n = len(g_parts)
    assert r % tr == 0, (name, r, tr)

    def body(*refs):
        w_ref, m_ref, v_ref = refs[0], refs[1 + n], refs[2 + n]
        g_ref, d_ref, nm_ref, nv_ref = refs[3 + n:]
        g = refs[1][...]
        for ref in refs[2:1 + n]:
            g = g + ref[...]
        mm = ADAM_B1 * m_ref[...] + (1.0 - ADAM_B1) * g
        vv = ADAM_B2 * v_ref[...] + (1.0 - ADAM_B2) * jnp.square(g)
        m_hat = mm / (1.0 - ADAM_B1 ** ADAM_STEP)
        v_hat = vv / (1.0 - ADAM_B2 ** ADAM_STEP)
        g_ref[...] = g
        d_ref[...] = -ADAM_LR * (m_hat / (jnp.sqrt(v_hat) + ADAM_EPS) + ADAM_WD * w_ref[...])
        nm_ref[...] = mm
        nv_ref[...] = vv

    spec = pl.BlockSpec((tr, c), lambda i: (i, 0))
    return _call(body, name=name, out_shape=(SDS((r, c), F32),) * 4, grid=(r // tr,), in_specs=[spec] * (3 + n),
                 out_specs=(spec,) * 4, scratch_shapes=[], args=(w, *g_parts, m, v))


class _AllGather8:
    def __init__(self, block):
        self.m_per = block.shape[0]
        self.arrays, self.n_in, self.n_out = [block], 1, 1
        self.out_shapes = (SDS((N_DEV * self.m_per, block.shape[1]), block.dtype),)
        self.scratch = [pltpu.SemaphoreType.DMA((7,)), pltpu.SemaphoreType.DMA((7,)), pltpu.SemaphoreType.DMA]

    def _copies(self, ins, outs, sems):
        (x_ref,), (out_ref,), (send_sems, recv_sems, local_sem) = ins, outs, sems
        x, y, c = lax.axis_index("x"), lax.axis_index("y"), lax.axis_index("c")
        me, sibling = (x, y, c), (x, y, 1 - c)
        chips = [(1 - x, y), (x, 1 - y), (1 - x, 1 - y)]

        def rows(px, py, pc):
            return out_ref.at[pl.ds((4 * px + 2 * py + pc) * self.m_per, self.m_per), :]

        def copy(k, blk, to, src=None):
            return pltpu.make_async_remote_copy(
                src_ref=rows(*blk) if src is None else src, dst_ref=rows(*blk), send_sem=send_sems.at[k],
                recv_sem=recv_sems.at[k], device_id=to, device_id_type=MESH)

        mine = pltpu.make_async_copy(x_ref, rows(*me), local_sem)
        first = [copy(0, me, sibling, src=x_ref)]
        first += [copy(1 + j, me, (*chip, c), src=x_ref) for j, chip in enumerate(chips)]
        passed = [copy(4 + j, (*chip, c), sibling) for j, chip in enumerate(chips)]
        arrivals = [copy(1 + j, (*chip, c), me) for j, chip in enumerate(chips)]
        from_sibling = [copy(0, sibling, me)] + [copy(4 + j, (*chip, 1 - c), me) for j, chip in enumerate(chips)]
        return mine, first, passed, arrivals, from_sibling

    def start(self, ins, outs, sems):
        mine, first, _, _, _ = self._copies(ins, outs, sems)
        mine.start()
        for cp in first:
            cp.start()

    def wait(self, ins, outs, sems):
        mine, first, passed, arrivals, from_sibling = self._copies(ins, outs, sems)
        for arrived, onward in zip(arrivals, passed):
            arrived.wait_recv()
            onward.start()
        for cp in from_sibling:
            cp.wait_recv()
        for cp in first + passed:
            cp.wait_send()
        mine.wait()


def _allgather8(block, name):
    ex = _AllGather8(block)

    def body(x_ref, out_ref, *sems):
        ex.start((x_ref,), (out_ref,), sems)
        ex.wait((x_ref,), (out_ref,), sems)

    return pl.pallas_call(
        body, name=name, out_shape=ex.out_shapes[0], in_specs=[pl.BlockSpec(memory_space=pltpu.VMEM)],
        out_specs=pl.BlockSpec(memory_space=pltpu.VMEM), scratch_shapes=ex.scratch, compiler_params=_params())(block)


class _Exchange:
    def __init__(self, items, out_shapes):
        self.items, self.out_shapes = list(items), tuple(out_shapes)
        self.arrays = [it[0] for it in self.items]
        n = len(self.items)
        self.n_in, self.n_out = n, len(self.out_shapes)
        self.scratch = [pltpu.SemaphoreType.DMA((n * N_CHIPS,)), pltpu.SemaphoreType.DMA((n * N_CHIPS,)),
                        pltpu.SemaphoreType.DMA((n,))]

    def _copies(self, ins, outs, sems, m):
        send_sems, recv_sems, local_sems = sems
        c = lax.axis_index("c")
        others = [j for j in range(N_CHIPS) if j != m]

        def remote(a, src, dst, to, from_):
            return pltpu.make_async_remote_copy(
                src_ref=src, dst_ref=dst, send_sem=send_sems.at[a * N_CHIPS + to],
                recv_sem=recv_sems.at[a * N_CHIPS + from_], device_id=(to // 2, to % 2, c), device_id_type=MESH)

        local, sends, recvs = [], [], []
        for a, (_, oi, src_of, dst_of) in enumerate(self.items):
            local.append(pltpu.make_async_copy(src_of(ins[a], m), dst_of(outs[oi], m), local_sems.at[a]))
            for j in others:
                sends.append(remote(a, src_of(ins[a], j), dst_of(outs[oi], m), j, m))
                recvs.append(remote(a, src_of(ins[a], m), dst_of(outs[oi], j), j, j))
        return local, sends, recvs

    def _on_my_chip(self, fn):
        chip = 2 * lax.axis_index("x") + lax.axis_index("y")
        for m in range(N_CHIPS):
            pl.when(chip == m)(functools.partial(fn, m))

    def start(self, ins, outs, sems):
        def go(m):
            local, sends, _ = self._copies(ins, outs, sems, m)
            for cp in local + sends:
                cp.start()
        self._on_my_chip(go)

    def wait(self, ins, outs, sems):
        def go(m):
            local, sends, recvs = self._copies(ins, outs, sems, m)
            for cp in recvs:
                cp.wait_recv()
            for cp in sends:
                cp.wait_send()
            for cp in local:
                cp.wait()
        self._on_my_chip(go)


def _half_rows(ref, cc):
    h = ref.shape[-2] // 2
    return ref.at[(slice(None),) * (len(ref.shape) - 2) + (pl.ds(cc * h, h), slice(None))]


class _Gather:
    def __init__(self, items, out_shapes):
        self.items, self.out_shapes = list(items), tuple(out_shapes)
        self.arrays = [it[0] for it in self.items]
        n = len(self.items)
        self.n_in, self.n_out = n, len(self.out_shapes)
        self.scratch = [pltpu.SemaphoreType.DMA((n * N_CHIPS,)) for _ in range(4)] + [pltpu.SemaphoreType.DMA((n,))]

    def _copies(self, ins, outs, sems, m, cc):
        ici_send, ici_recv, d2d_send, d2d_recv, local_sems = sems
        others = [j for j in range(N_CHIPS) if j != m]
        local, sends, arrivals, passed_on, from_sibling = [], [], [], [], []
        for a, (_, oi, src_of, dst_of) in enumerate(self.items):
            src, out = src_of(ins[a]), outs[oi]
            local.append(pltpu.make_async_copy(src, dst_of(out, m), local_sems.at[a]))
            for j in others:
                k = a * N_CHIPS + j
                mine_there = _half_rows(dst_of(out, m), cc)
                theirs_here = _half_rows(dst_of(out, j), cc)
                sends.append(pltpu.make_async_remote_copy(
                    src_ref=_half_rows(src, cc), dst_ref=mine_there, send_sem=ici_send.at[k],
                    recv_sem=ici_recv.at[a * N_CHIPS + m], device_id=(j // 2, j % 2, cc), device_id_type=MESH))
                arrivals.append(pltpu.make_async_remote_copy(
                    src_ref=_half_rows(src, cc), dst_ref=theirs_here, send_sem=ici_send.at[k], recv_sem=ici_recv.at[k],
                    device_id=(j // 2, j % 2, cc), device_id_type=MESH))
                passed_on.append(pltpu.make_async_remote_copy(
                    src_ref=theirs_here, dst_ref=theirs_here, send_sem=d2d_send.at[k], recv_sem=d2d_recv.at[k],
                    device_id=(m // 2, m % 2, 1 - cc), device_id_type=MESH))
                other_half = _half_rows(dst_of(out, j), 1 - cc)
                from_sibling.append(pltpu.make_async_remote_copy(
                    src_ref=other_half, dst_ref=other_half, send_sem=d2d_send.at[k], recv_sem=d2d_recv.at[k],
                    device_id=(m // 2, m % 2, 1 - cc), device_id_type=MESH))
        return local, sends, arrivals, passed_on, from_sibling

    def _on_my_core(self, fn):
        chip = 2 * lax.axis_index("x") + lax.axis_index("y")
        c = lax.axis_index("c")
        for m in range(N_CHIPS):
            for cc in range(2):
                pl.when((chip == m) & (c == cc))(functools.partial(fn, m, cc))

    def start(self, ins, outs, sems):
        def go(m, cc):
            local, sends, _, _, _ = self._copies(ins, outs, sems, m, cc)
            for cp in local + sends:
                cp.start()
        self._on_my_core(go)

    def wait(self, ins, outs, sems):
        def go(m, cc):
            local, sends, arrivals, passed_on, from_sibling = self._copies(ins, outs, sems, m, cc)
            for arrived, onward in zip(arrivals, passed_on):
                arrived.wait_recv()
                onward.start()
            for cp in from_sibling:
                cp.wait_recv()
            for cp in sends + passed_on:
                cp.wait_send()
            for cp in local:
                cp.wait()
        self._on_my_core(go)


def _run_exchange(ex, name):
    def body(*refs):
        ins, outs, sems = refs[:ex.n_in], refs[ex.n_in:ex.n_in + ex.n_out], refs[ex.n_in + ex.n_out:]
        ex.start(ins, outs, sems)
        ex.wait(ins, outs, sems)

    return pl.pallas_call(
        body, name=name, out_shape=ex.out_shapes, in_specs=[ANY] * ex.n_in, out_specs=(ANY,) * ex.n_out,
        scratch_shapes=ex.scratch, compiler_params=_params())(*ex.arrays)


def _sibling_swap(arrays, name, also):
    n = len(arrays)

    def body(*refs):
        ins, refs = refs[:n], refs[n:]
        x_ins, refs = refs[:also.n_in], refs[also.n_in:]
        outs, refs = refs[:n], refs[n:]
        x_outs, refs = refs[:also.n_out], refs[also.n_out:]
        send_sems, recv_sems, x_sems = refs[0], refs[1], refs[2:]
        peer = (lax.axis_index("x"), lax.axis_index("y"), 1 - lax.axis_index("c"))
        cps = [pltpu.make_async_remote_copy(src_ref=ins[a], dst_ref=outs[a], send_sem=send_sems.at[a],
                                            recv_sem=recv_sems.at[a], device_id=peer, device_id_type=MESH)
               for a in range(n)]
        also.start(x_ins, x_outs, x_sems)
        for cp in cps:
            cp.start()
        also.wait(x_ins, x_outs, x_sems)
        for cp in cps:
            cp.wait()

    return pl.pallas_call(
        body, name=name, out_shape=tuple(SDS(a.shape, a.dtype) for a in arrays) + also.out_shapes,
        in_specs=[ANY] * (n + also.n_in), out_specs=(ANY,) * (n + also.n_out),
        scratch_shapes=[pltpu.SemaphoreType.DMA((n,)), pltpu.SemaphoreType.DMA((n,))] + also.scratch,
        compiler_params=_params())(*arrays, *also.arrays)


def _block_diag(w):
    h, n, m = w.shape
    eye = jnp.eye(h, dtype=w.dtype)
    return (w[:, :, None, :] * eye[:, None, :, None]).reshape(h * n, h * m)


def _diag_blocks(d, h, col0=0, ncols=None, stacked=1):
    ncols = d.shape[1] - col0 if ncols is None else ncols
    n, m = d.shape[0] // (h * stacked), ncols // h
    lanes = 128
    assert m <= lanes and lanes % m == 0 and col0 % lanes == 0

    def body(d_ref, o_ref):
        for gi in range(h * stacked):
            c = col0 + (gi % h) * m
            chunk = d_ref[gi * n:(gi + 1) * n, c // lanes * lanes:c // lanes * lanes + lanes]
            o_ref[gi * n:(gi + 1) * n, :] = chunk[:, c % lanes:c % lanes + m]

    out = pl.pallas_call(body, name="diag_blocks", out_shape=SDS((stacked * h * n, m), d.dtype),
                         compiler_params=_params())(d)
    return out.reshape(stacked * h, n, m)


S5_CHUNKS = 4
S5_PER = S5_GROUPS // S5_CHUNKS
CH_W = S5_PER * S5_CH
ST_W = S5_PER * S5_STATE


def _bd_stack(mats):
    _, _, n, m = mats.shape
    eye = jnp.eye(S5_PER, dtype=mats.dtype)
    t = mats.reshape(2, S5_CHUNKS, S5_PER, n, m)
    bd = t[:, :, :, :, None, :] * eye[None, None, :, None, :, None]
    return bd.reshape(2 * S5_CHUNKS, S5_PER * n, S5_PER * m).astype(MXU_DTYPE)


def _chunks_chunked(src_ref, buf):
    pt = src_ref.shape[0]
    out = []
    for q in range(S5_CHUNKS):
        buf[q] = src_ref[:, q * CH_W:(q + 1) * CH_W]
        out.append(_load_chunked(buf.at[q], 0, pt).astype(MXU_DTYPE))
    return out


def _expand_into(dst_ref, chunks, w_ref):
    for b in range(2 * S5_CHUNKS):
        dst_ref[:, b * ST_W:(b + 1) * ST_W] = jnp.dot(chunks[b % S5_CHUNKS], w_ref[b], preferred_element_type=F32)


def _reduce_from(src_ref, w_ref, buf, dst_ref):
    pt = src_ref.shape[0]
    for q in range(S5_CHUNKS):
        y = jnp.dot(src_ref[:, q * ST_W:(q + 1) * ST_W].astype(MXU_DTYPE), w_ref[q], preferred_element_type=F32)
        p = S5_CHUNKS + q
        y = y + jnp.dot(src_ref[:, p * ST_W:(p + 1) * ST_W].astype(MXU_DTYPE), w_ref[p], preferred_element_type=F32)
        _store_natural(buf.at[q], 0, pt, y)
        dst_ref[:, q * CH_W:(q + 1) * CH_W] = buf[q]


def _s5_fwd(proj, w_bu, w_cx, a_row, d_skip, w_glu, b_glu):
    s = proj.shape[0]
    pt = _scan_tile(s)
    ch2 = 2 * S5_N

    def body(u_ref, dg_ref, wb_ref, wc_ref, a_ref, dk_ref, wg_ref, bg_ref, x_ref, y_ref, o_ref, carry, pw, buf):
        _expand_into(x_ref, _chunks_chunked(u_ref, buf), wb_ref)
        _scan_tile_in_place(a_ref, x_ref, carry, pw, reverse=False)
        _reduce_from(x_ref, wc_ref, buf, y_ref)
        g = jax.nn.gelu(y_ref[...] + dk_ref[...] * u_ref[...])
        t = jnp.dot(g.astype(MXU_DTYPE), wg_ref[...], preferred_element_type=F32) + bg_ref[...]
        o_ref[...] = (g * jax.nn.sigmoid(t) * _silu(dg_ref[...])).astype(MXU_DTYPE)

    return pl.pallas_call(
        body, name="s5_fwd", out_shape=(SDS((s, ch2), F32), SDS((s, BR), F32), SDS((s, BR), MXU_DTYPE)),
        grid=(s // pt,),
        in_specs=[_rows(pt, BR, CB_DU), _rows(pt, BR, CB_DG), _const(w_bu.shape), _const(w_cx.shape),
                  _const((1, ch2)), _const((1, BR)), _const((BR, BR)), _const((1, BR))],
        out_specs=(_rows(pt, ch2), _rows(pt, BR), _rows(pt, BR)),
        scratch_shapes=[pltpu.VMEM((1, ch2), F32), pltpu.VMEM((pt // 8, ch2), F32),
                        pltpu.VMEM((S5_CHUNKS, pt, CH_W), F32)],
        compiler_params=_params(1))(proj, proj, w_bu, w_cx, a_row, d_skip, w_glu, b_glu)


def _s5_core_bwd(dyl, proj, x, w_dx, w_du, a_row):
    s = proj.shape[0]
    pt = _scan_tile(s)
    nt = s // pt
    ch2 = 2 * S5_N
    ch = S5_N

    def body(dy_ref, u_ref, x_ref, xp_ref, wx_ref, wu_ref, a_ref, du_ref, da_ref, dwb_ref, dwc_ref,
             l_ref, carry, pw, buf, buf2):
        i = pl.program_id(0)
        _init_acc(da_ref, dwb_ref, dwc_ref)
        dy_c = _chunks_chunked(dy_ref, buf)
        u_c = _chunks_chunked(u_ref, buf2)
        _expand_into(l_ref, dy_c, wx_ref)
        _scan_tile_in_place(a_ref, l_ref, carry, pw, reverse=True)
        has_prev = (i < nt - 1).astype(F32)
        row = lax.broadcasted_iota(jnp.int32, (8, ch2), 0)
        first = jnp.where(row == 0, pltpu.roll(xp_ref[...], 1, 0) * has_prev, pltpu.roll(x_ref[pt - 8:pt, :], 1, 0))
        xprev = jnp.concatenate([first, x_ref[0:pt - 8, :]], axis=0)
        lr, li, xr, xi = l_ref[:, 0:ch], l_ref[:, ch:ch2], xprev[:, 0:ch], xprev[:, ch:ch2]
        da_ref[:, 0:ch] += _colsum(lr * xr + li * xi)
        da_ref[:, ch:ch2] += _colsum(li * xr - lr * xi)
        _reduce_from(l_ref, wu_ref, buf, du_ref)
        tn = (((0,), (0,)), ((), ()))
        for b in range(2 * S5_CHUNKS):
            cols, rows = slice(b * ST_W, (b + 1) * ST_W), slice(b * CH_W, (b + 1) * CH_W)
            dwb_ref[rows, :] += lax.dot_general(u_c[b % S5_CHUNKS], l_ref[:, cols].astype(MXU_DTYPE), tn,
                                                preferred_element_type=F32)
            dwc_ref[rows, :] += lax.dot_general(dy_c[b % S5_CHUNKS], x_ref[:, cols].astype(MXU_DTYPE), tn,
                                                preferred_element_type=F32)

    rev = lambda w, cb=0: pl.BlockSpec((pt, w), lambda i: (nt - 1 - i, cb))
    halo = pl.BlockSpec((8, ch2), lambda i: (jnp.maximum((nt - 1 - i) * (pt // 8) - 1, 0), 0))
    wshape = SDS((2 * S5_CHUNKS * CH_W, ST_W), F32)
    return pl.pallas_call(
        body, name="s5_core_bwd", out_shape=(SDS((s, BR), F32), SDS((1, ch2), F32), wshape, wshape), grid=(nt,),
        in_specs=[rev(BR, 0), rev(BR, CB_DU), rev(ch2), halo, _const(w_dx.shape), _const(w_du.shape),
                  _const((1, ch2))],
        out_specs=(rev(BR), _const((1, ch2)), _const(wshape.shape), _const(wshape.shape)),
        scratch_shapes=[pltpu.VMEM((pt, ch2), F32), pltpu.VMEM((1, ch2), F32), pltpu.VMEM((pt // 8, ch2), F32),
                        pltpu.VMEM((S5_CHUNKS, pt, CH_W), F32), pltpu.VMEM((S5_CHUNKS, pt, CH_W), F32)],
        compiler_params=_params(1))(dyl, proj, x, x, w_dx, w_du, a_row)


def _tiles(s):
    return dict(tb=min(512, s), tln=min(256, s))


def _layer_weights(p, l):
    pad8 = lambda w: jnp.pad(w, ((0, 8 - w.shape[0]), (0, 0)))
    return dict(
        conv_a=pad8(p["conv_a"][l]), conv_c=pad8(p["conv_c"][l]), conv_c_b=p["conv_c_b"][l][None],
        w_cat=jnp.concatenate([_block_diag(p["lru_wa"][l]), _block_diag(p["lru_wx"][l])], axis=1).astype(MXU_DTYPE),
        b_cat=jnp.concatenate([p["lru_ba"][l], p["lru_bx"][l]])[None], lam=p["lru_lambda"][l][None],
        lam_re=p["s5_lam_re"][l], lam_im=p["s5_lam_im"][l], log_dt=p["s5_log_dt"][l][:, None],
        b_re=p["s5_b_re"][l].reshape(S5_N, S5_CH), b_im=p["s5_b_im"][l].reshape(S5_N, S5_CH),
        c_re=p["s5_c_re"][l], c_im=p["s5_c_im"][l], d_skip=p["s5_d"][l][None], b_glu=p["s5_b_glu"][l][None],
        ln_g=p["ln_g"][l][None], ln_b=p["ln_b"][l][None])


def _s5_matrices(lw):
    ab_re, ab_im, f_re, f_im = _s5_disc_fwd(lw["lam_re"], lw["lam_im"], lw["log_dt"])
    f_re, f_im = f_re.reshape(S5_N, 1), f_im.reshape(S5_N, 1)
    bb_re, bb_im = _s5_bbar_fwd(f_re, f_im, lw["b_re"], lw["b_im"])
    bb = jnp.stack([bb_re, bb_im]).reshape(2, S5_GROUPS, S5_STATE, S5_CH)
    cc = jnp.stack([lw["c_re"], -lw["c_im"]])
    a_row = jnp.concatenate([ab_re.reshape(1, S5_N), ab_im.reshape(1, S5_N)], axis=1)
    return dict(f_re=f_re, f_im=f_im, a_row=a_row, w_bu=_bd_stack(jnp.swapaxes(bb, 2, 3)), w_du=_bd_stack(bb),
                w_cx=_bd_stack(jnp.swapaxes(cc, 2, 3)), w_dx=_bd_stack(cc))


def _mm_hooked(hook, *args, **kw):
    if hook is None:
        return _mm(*args, **kw)
    out = _mm(*args, carry=hook[0], **kw)
    hook[1](out[1:])
    return out[0]


def _layer_fwd(x, h, ada, w_in, get_rest, lw, s5m, bias_tabs, hooks=None, target=None, next_ada=None):
    s = x.shape[0]
    t = _tiles(s)
    tb = t["tb"]
    shift, scale, gate = ada
    hooks = hooks or {}
    if h is None:
        h = _modulate(x, scale, shift, tb)
    proj = _mm_hooked(hooks.get("in_proj"), h, w_in, name="in_proj", tm=1024, tn=1536, tk=D_MODEL)
    w_out, w_glu = get_rest()
    y_a = _branch_a_fwd(proj, lw["conv_a"], tb)
    os_, lses = [], []
    for g, (_, dil) in enumerate(DILATIONS):
        o, lse = _attn_fwd(proj, bias_tabs[g], dil)
        os_.append(o)
        lses.append(lse)
    y_b = _attn_combine(os_, lses, proj, tb)
    lru_a, lru_b = _lru_gates_fwd(proj, lw["conv_c"], lw["conv_c_b"], lw["w_cat"], lw["b_cat"], lw["lam"], tb)
    lru_h, y_c = _lru_scan_fwd(lru_a, lru_b, proj, tb)
    s5_x, ylin, y_d = _s5_fwd(proj, s5m["w_bu"], s5m["w_cx"], s5m["a_row"], lw["d_skip"], w_glu, lw["b_glu"])
    ycat = jnp.concatenate([y_a, y_b, y_c, y_d], axis=1)
    saved = dict(x=x, h=h, proj=proj, os=os_, lses=lses, lru_a=lru_a, lru_h=lru_h, s5_x=s5_x, ylin=ylin, ycat=ycat)
    if target is not None:
        loss, *saved["head"] = _out_ln_loss(ycat, w_out, x, gate, lw["ln_g"], lw["ln_b"], target, t["tln"])
        return loss, None, saved
    x_next, saved["xhat"], saved["y"], saved["rstd"], h_next = _out_ln(
        ycat, w_out, x, gate, lw["ln_g"], lw["ln_b"], next_ada[1], next_ada[0], t["tln"])
    return x_next, h_next, saved


def _layer_bwd(dxn, sv, ada, w_in, w_out, w_glu, lw, s5m, bias_tabs, head_ones, hooks=None):
    proj = sv["proj"]
    s = proj.shape[0]
    t = _tiles(s)
    tb = t["tb"]
    shift, scale, gate = ada
    g = {}
    hook = lambda name: hooks[name](g) if hooks and name in hooks else None
    if "head" in sv:
        dyb, dxa, g["ln_g"], g["ln_b"], dgate = sv["head"]
    else:
        dyb, dxa, g["ln_g"], g["ln_b"], dgate = _ln_bwd(dxn, sv["xhat"], sv["y"], sv["rstd"], lw["ln_g"], gate,
                                                        t["tln"])
    g["w_out"] = _mm_hooked(hook("dw_out"), sv["ycat"], dyb, name="dw_out", ta=True, out_dtype=WIRE_DTYPE,
                            tm=1024, tn=1024, tk=2048)
    dycat =_mm(dyb, w_out, name="dycat", tb=True, tm=1024, tn=1024, tk=D_MODEL)
    da, dconv_a = _branch_a_bwd(dycat, proj, lw["conv_a"], tb)
    g["conv_a"] = dconv_a[0:3]
    pre = _attn_bwd_pre(dycat, sv["os"], sv["lses"], proj, head_ones, tb)
    dbg, dos, dms = pre[0], pre[1:4], pre[4:7]
    dqkv, dbias = [], []
    for gi, (_, dil) in enumerate(DILATIONS):
        hk = hook(f"attn_bwd_d{dil}")
        dq, dk, dv, dbi, *got = _attn_bwd(proj, dos[gi], sv["lses"][gi], dms[gi], bias_tabs[gi], dil,
                                          carry=hk and hk[0])
        if hk:
            hk[1](got)
        dqkv.append((dq, dk, dv))
        dbias.append(dbi)
    dqkv = list(zip(*dqkv))
    lmb, dcg = _lru_scan_bwd(sv["lru_a"], dycat, sv["lru_h"], proj, tb)
    dxc, dpre, xcb, dbcat, dlam = _lru_gates_bwd(proj, lmb, sv["lru_h"], lw["conv_c"], lw["conv_c_b"], lw["w_cat"],
                                                  lw["b_cat"], lw["lam"], tb)
    dwcat = _mm(xcb, dpre, name="dw_lru", ta=True, tn=1024)
    g["lru_wa"] = _diag_blocks(dwcat, LRU_HEADS, 0, BR)
    g["lru_wx"] = _diag_blocks(dwcat, LRU_HEADS, BR, BR)
    g["lru_ba"], g["lru_bx"], g["lru_lambda"] = dbcat[0, 0:BR], dbcat[0, BR:2 * BR], dlam[0]
    dcx, dconv_c, dccb = _conv_c_bwd(dxc, proj, lw["conv_c"], tb)
    g["conv_c"], g["conv_c_b"] = dconv_c[0:4], dccb[0]
    dyl, dus, ddg, gb, dtb, ddk, dbglu = _s5_tail_bwd(dycat, sv["ylin"], proj, lw["d_skip"], w_glu, lw["b_glu"], tb)
    g["s5_d"], g["s5_b_glu"] = ddk[0], dbglu[0]
    g["s5_w_glu"] = _mm(gb, dtb, name="dw_glu", ta=True, out_dtype=WIRE_DTYPE)
    du, dab, dwb8, dwc8 = _s5_core_bwd(dyl, proj, sv["s5_x"], s5m["w_dx"], s5m["w_du"], s5m["a_row"])
    per_group = lambda d8: _diag_blocks(d8, S5_PER, stacked=2 * S5_CHUNKS).reshape(2, S5_GROUPS, S5_CH, S5_STATE)
    dbb, dcc = per_group(dwb8), per_group(dwc8)
    from_bd = lambda half: jnp.swapaxes(dbb[half], 1, 2).reshape(S5_N, S5_CH)
    df_re, df_im, db_re, db_im = _s5_bbar_bwd(s5m["f_re"], s5m["f_im"], lw["b_re"], lw["b_im"],
                                              from_bd(0), from_bd(1))
    shp = (S5_GROUPS, S5_STATE)
    g["s5_lam_re"], g["s5_lam_im"], dlog_dt = _s5_disc_bwd(
        lw["lam_re"], lw["lam_im"], lw["log_dt"],
        (dab[:, 0:S5_N].reshape(shp), dab[:, S5_N:].reshape(shp), df_re.reshape(shp), df_im.reshape(shp)))
    g["s5_log_dt"] = dlog_dt[:, 0]
    g["s5_b_re"] = db_re.reshape(S5_GROUPS, S5_STATE, S5_CH)
    g["s5_b_im"] = db_im.reshape(S5_GROUPS, S5_STATE, S5_CH)
    g["s5_c_re"], g["s5_c_im"] = dcc[0], -dcc[1]
    dproj = _assemble_dproj(da, dqkv, dbg, dcx, dcg, du, dus, ddg, tb)
    g["w_in"] = _mm_hooked(hook("dw_in"), sv["h"], dproj, name="dw_in", ta=True, out_dtype=WIRE_DTYPE,
                           tm=1024, tn=1536, tk=2048)
    hk = hook("dh")
    dx, dshift, dscale, *got = _dh_mod_bwd(dproj, w_in, dxa, sv["x"], scale, carry=hk and hk[0])
    if hk:
        hk[1](got)
    g["ada"] = jnp.concatenate([dshift[0], dscale[0], dgate[0]])
    return dx, g, dbias


SMALL = ("rel_bias", "conv_a", "conv_c", "conv_c_b", "lru_wa", "lru_ba", "lru_wx", "lru_bx", "lru_lambda",
         "s5_lam_re", "s5_lam_im", "s5_log_dt", "s5_b_re", "s5_b_im", "s5_c_re", "s5_c_im", "s5_d", "s5_b_glu",
         "ln_g", "ln_b")
PER_LAYER_SMALL = SMALL[1:]


def _local_step(x, target, ada_rows, w_in, w_out, w_glu, p, comm=None):
    if comm is None:
        get_w_in = lambda l: w_in[l]
        get_rest = lambda l: (w_out[l], w_glu[l])
        fwd_hooks = bwd_hooks = lambda *_: None
    else:
        get_w_in, get_rest, fwd_hooks, bwd_hooks = comm.w_in, comm.rest, comm.fwd_hooks, comm.bwd_hooks
    s = x.shape[0]
    buckets = _bucket_maps()
    bias_tabs = _bias_tables(p["rel_bias"], buckets)
    head_ones = _block_diag(jnp.ones((ATT_HEADS, HEAD_DIM, HEAD_DIM), MXU_DTYPE))
    lws = [_layer_weights(p, l) for l in range(DEPTH)]
    s5ms = [_s5_matrices(lw) for lw in lws]
    adas = [tuple(ada_rows[l, k * D_MODEL:(k + 1) * D_MODEL][None] for k in range(3)) for l in range(DEPTH)]
    saved, h = [], None
    for l in range(DEPTH):
        last = l == DEPTH - 1
        x, h, sv = _layer_fwd(x, h, adas[l], get_w_in(l), functools.partial(get_rest, l), lws[l], s5ms[l], bias_tabs,
                              fwd_hooks(l), target if last else None, None if last else adas[l + 1])
        saved.append(sv)
    loss, dx = x, None
    grads = [None] * DEPTH
    dbias_sum = []
    for l in reversed(range(DEPTH)):
        dx, grads[l], dbias = _layer_bwd(dx, saved[l], adas[l], get_w_in(l), *get_rest(l), lws[l], s5ms[l],
                                         bias_tabs, head_ones, bwd_hooks(l, grads))
        dbias_sum.append(jnp.stack(dbias))
    drel = _rel_bias_grad(jnp.stack(dbias_sum), buckets)[:, 0:ATT_HEADS]
    small = {n: jnp.stack([grads[l][n] for l in range(DEPTH)]) for n in PER_LAYER_SMALL + ("ada",)}
    small["rel_bias"] = drel
    big = {n: [grads[l][n] for l in range(DEPTH)] for n in ("w_in", "w_out", "s5_w_glu")}
    return loss, dx, big, small


PACK_ROWS = 256


def _pack(parts):
    flat = jnp.concatenate([t.reshape(-1).astype(F32) for t in parts])
    n = flat.shape[0]
    rows = -(-n // (PACK_ROWS * 128)) * PACK_ROWS
    return jnp.pad(flat, (0, rows * 128 - n)).reshape(rows, 128)


def _unpack(packed, shapes):
    flat = packed.reshape(packed.shape[:-2] + (-1,))
    out, off = [], 0
    for shp in shapes:
        size = math.prod(shp)
        out.append(flat[..., off:off + size].reshape(flat.shape[:-1] + tuple(shp)))
        off += size
    return out


def _take_cols(t, chip, width):
    return lax.dynamic_slice_in_dim(t, chip * width, width, axis=t.ndim - 1)


class _Comm:
    IN_W, OUT_R, GLU_R = N_IN // N_CHIPS, D_MODEL // N_CHIPS, BR // N_CHIPS

    def __init__(self, w_in_b, w_out_b, w_glu_b):
        assert DEPTH == 2
        self.shards = (w_in_b, w_out_b, w_glu_b)
        in_w = self.IN_W
        self.w_in_full = {0: _run_exchange(_Gather(
            [(w_in_b, 0, lambda ref: ref.at[0], lambda ref, j: ref.at[:, pl.ds(j * in_w, in_w)])],
            [SDS((D_MODEL, N_IN), WIRE_DTYPE)]), "gather_w_in0")[0]}
        self.w_out_full = self.w_glu_full = None
        self.recv = {}

    def w_in(self, l):
        return self.w_in_full[l]

    def rest(self, l):
        return self.w_out_full[l], self.w_glu_full[l]

    def fwd_hooks(self, l):
        if l != 0:
            return None
        w_in_b, w_out_b, w_glu_b = self.shards
        in_w, out_r, glu_r = self.IN_W, self.OUT_R, self.GLU_R
        whole = lambda ref: ref
        items = [(w_out_b, 0, whole, lambda ref, j: ref.at[:, pl.ds(j * out_r, out_r), :]),
                 (w_glu_b, 1, whole, lambda ref, j: ref.at[:, pl.ds(j * glu_r, glu_r), :]),
                 (w_in_b, 2, lambda ref: ref.at[1], lambda ref, j: ref.at[:, pl.ds(j * in_w, in_w)])]
        shapes = [SDS((DEPTH, D_MODEL, D_MODEL), WIRE_DTYPE), SDS((DEPTH, BR, BR), WIRE_DTYPE),
                  SDS((D_MODEL, N_IN), WIRE_DTYPE)]

        def done(got):
            self.w_out_full, self.w_glu_full, self.w_in_full[1] = got

        return {"in_proj": (_Gather(items, shapes), done)}

    W_IN_ROWS = ((0, 1024), (1024, 512), (1536, 512))

    def _scatter(self, parts):
        in_w, out_r, glu_r = self.IN_W, self.OUT_R, self.GLU_R
        items, shapes, keys = [], [], []
        for oi, (name, l, arr, *rows) in enumerate(parts):
            if name == "w_in":
                r0, nr = rows[0] if rows else (0, D_MODEL)
                cut = functools.partial(lambda ref, j, r0, nr: ref.at[pl.ds(r0, nr), pl.ds(j * in_w, in_w)], r0=r0, nr=nr)
                shard = (nr, in_w)
            elif name == "w_out":
                cut, shard = (lambda ref, j: ref.at[pl.ds(j * out_r, out_r), :]), (out_r, D_MODEL)
            else:
                cut, shard = (lambda ref, j: ref.at[pl.ds(j * glu_r, glu_r), :]), (glu_r, BR)
            items.append((arr, oi, cut, lambda ref, j: ref.at[j]))
            shapes.append(SDS((N_CHIPS,) + shard, WIRE_DTYPE))
            keys.append((name, l) + ((rows[0][0],) if rows else ()))

        def done(got):
            self.recv.update(zip(keys, got))

        return _Exchange(items, shapes), done

    def received(self, name):
        return [self.recv[k] for k in sorted(k for k in self.recv if k[0] == name)]

    def bwd_hooks(self, l, grads):
        if l != 0:
            return None
        g1 = grads[1]
        w_in_part = lambda k: (lambda g: self._scatter([("w_in", 1, g1["w_in"], self.W_IN_ROWS[k])]))
        return {"dw_out": lambda g: self._scatter([("w_out", 1, g1["w_out"]), ("s5_w_glu", 1, g1["s5_w_glu"])]),
                "attn_bwd_d16": w_in_part(0), "attn_bwd_d4": w_in_part(1), "attn_bwd_d1": w_in_part(2),
                "dw_in": lambda g: self._scatter([("w_out", 0, g["w_out"]), ("s5_w_glu", 0, g["s5_w_glu"])]),
                "dh": lambda g: self._scatter([("w_in", 0, g["w_in"])])}


def kernel(x, c, rel_bias, w_ada, b_ada, w_in, conv_a, conv_c, conv_c_b, lru_wa, lru_ba, lru_wx, lru_bx, lru_lambda, s5_lam_re, s5_lam_im, s5_log_dt, s5_b_re, s5_b_im, s5_c_re, s5_c_im, s5_d, s5_w_glu, s5_b_glu, w_out, ln_g, ln_b, loss_target, m_rel_bias, m_w_ada, m_b_ada, m_w_in, m_conv_a, m_conv_c, m_conv_c_b, m_lru_wa, m_lru_ba, m_lru_wx, m_lru_bx, m_lru_lambda, m_s5_lam_re, m_s5_lam_im, m_s5_log_dt, m_s5_b_re, m_s5_b_im, m_s5_c_re, m_s5_c_im, m_s5_d, m_s5_w_glu, m_s5_b_glu, m_w_out, m_ln_g, m_ln_b, v_rel_bias, v_w_ada, v_b_ada, v_w_in, v_conv_a, v_conv_c, v_conv_c_b, v_lru_wa, v_lru_ba, v_lru_wx, v_lru_bx, v_lru_lambda, v_s5_lam_re, v_s5_lam_im, v_s5_log_dt, v_s5_b_re, v_s5_b_im, v_s5_c_re, v_s5_c_im, v_s5_d, v_s5_w_glu, v_s5_b_glu, v_w_out, v_ln_g, v_ln_b):
    args = dict(locals())
    names = ("rel_bias", "w_ada", "b_ada", "w_in", "conv_a", "conv_c", "conv_c_b", "lru_wa", "lru_ba", "lru_wx",
             "lru_bx", "lru_lambda", "s5_lam_re", "s5_lam_im", "s5_log_dt", "s5_b_re", "s5_b_im", "s5_c_re", "s5_c_im",
             "s5_d", "s5_w_glu", "s5_b_glu", "w_out", "ln_g", "ln_b")
    w = {n: args[n] for n in names}
    mom = {n: args["m_" + n] for n in names}
    var = {n: args["v_" + n] for n in names}
    chip = 2 * lax.axis_index("x") + lax.axis_index("y")
    me = 2 * chip + lax.axis_index("c")
    ada_w = 3 * D_MODEL // N_CHIPS
    conv_w = BR // N_CHIPS

    comm = _Comm(w["w_in"].astype(WIRE_DTYPE), w["w_out"].astype(WIRE_DTYPE), w["s5_w_glu"].astype(WIRE_DTYPE))

    taps = jnp.concatenate([w["conv_a"].reshape(DEPTH * 3, conv_w), w["conv_c"].reshape(DEPTH * 4, conv_w)])
    first = jnp.concatenate([c, jnp.pad(taps, ((0, 1), (0, D_MODEL - conv_w)))])
    got = _allgather8(first, "gather_c_taps").reshape(N_CHIPS, 2, 16, D_MODEL)
    c_all = got[:, :, 0].reshape(N_DEV, D_MODEL)
    taps_all = jnp.transpose(got[:, 0, 1:1 + DEPTH * 7, 0:conv_w], (1, 0, 2)).reshape(DEPTH * 7, BR)
    conv_a_f = taps_all[0:DEPTH * 3].reshape(DEPTH, 3, BR)
    conv_c_f = taps_all[DEPTH * 3:].reshape(DEPTH, 4, BR)

    cond_all = _silu_rows(c_all)
    ada_part = jnp.stack([_mm(cond_all, w["w_ada"][l], name="ada_fwd", tk=D_MODEL, tn=512,
                              bias=_take_cols(w["b_ada"][l][None], chip, ada_w)) for l in range(DEPTH)])
    ada_all = _allgather8(ada_part.reshape(DEPTH * N_DEV, ada_w), "gather_ada")
    ada_all = ada_all.reshape(N_CHIPS, 2, DEPTH, N_DEV, ada_w)[:, 0]
    ada_rows = lax.dynamic_index_in_dim(ada_all, me, axis=2, keepdims=False)
    ada_rows = jnp.transpose(ada_rows, (1, 0, 2)).reshape(DEPTH, 3 * D_MODEL)

    p = dict(w)
    p["conv_a"], p["conv_c"] = conv_a_f, conv_c_f
    loss, dx, _, small = _local_step(x[0], loss_target[0], ada_rows, None, None, None, p, comm)

    sums = [_sum_leading(comm.received(name), 256, "sum_chips") for name in ("w_in", "w_out", "s5_w_glu")]
    small_names = SMALL + ("ada",)
    small["loss"] = loss
    order = small_names + ("loss",)
    shapes = [small[n].shape for n in order]
    *others, gathered = _sibling_swap(sums, "swap_cores", _AllGather8(_pack([small[n] for n in order])))
    out = {}
    for name, mine, other in zip(("w_in", "w_out", "s5_w_glu"), sums, others):
        shp = w[name].shape
        flat = lambda t: t.reshape(-1, shp[-1])
        res = _adamw(flat(w[name]), [mine, other], flat(mom[name]), flat(var[name]), 128, "adamw_big")
        out[name] = [t.reshape(shp) for t in res]
    gathered = gathered.reshape(N_DEV, -1, 128)
    total = dict(zip(order, _unpack(_sum_leading([gathered], PACK_ROWS, "sum_devices"), shapes)))
    d_ada_all = _unpack(gathered, shapes)[order.index("ada")]
    g_small = {n: total[n] for n in SMALL}
    g_small["conv_a"] = _take_cols(total["conv_a"], chip, conv_w)
    g_small["conv_c"] = _take_cols(total["conv_c"], chip, conv_w)
    g_small["b_ada"] = total["ada"]
    g_w_ada = jnp.stack([_mm(cond_all, _take_cols(d_ada_all[:, l], chip, ada_w), name="dw_ada", ta=True, tn=ada_w)
                         for l in range(DEPTH)])
    upd_names = SMALL + ("b_ada",)
    upd_shapes = [w[n].shape for n in upd_names]
    res = _adamw(_pack([w[n] for n in upd_names]), [_pack([g_small[n] for n in upd_names])],
                 _pack([mom[n] for n in upd_names]), _pack([var[n] for n in upd_names]), PACK_ROWS, "adamw_small")
    for k, t in enumerate(res):
        for n, val in zip(upd_names, _unpack(t, upd_shapes)):
            out.setdefault(n, [None] * 4)[k] = val
    shp = w["w_ada"].shape
    flat = lambda t: t.reshape(-1, shp[-1])
    out["w_ada"] = [t.reshape(shp) for t in _adamw(flat(w["w_ada"]), [flat(g_w_ada)], flat(mom["w_ada"]),
                                                  flat(var["w_ada"]), 128, "adamw_ada")]
    return (total["loss"].reshape(()), dx[None]) + tuple(out[n][k] for k in range(4) for n in names)
```

```python
import functools
import math

import jax
import jax.numpy as jnp
from jax import lax
from jax.experimental import pallas as pl
from jax.experimental.pallas import tpu as pltpu

F32 = jnp.float32
MXU_DTYPE = jnp.bfloat16
WIRE_DTYPE = jnp.bfloat16
SDS = jax.ShapeDtypeStruct
MESH = pl.DeviceIdType.MESH
ANY = pl.BlockSpec(memory_space=pl.ANY)
VMEM_LIMIT = 48 * 1024 * 1024

D_MODEL = 2048
DEPTH = 2
BR = 512
ATT_HEADS = 8
HEAD_DIM = 64
DILATIONS = ((128, 1), (512, 4), (2048, 16))
BLK = 128
REL_BUCKETS = 32
REL_MAX_DIST = 2048
LRU_HEADS = 8
LRU_C = 8.0
S5_CH = 16
S5_GROUPS = 32
S5_STATE = 64
S5_N = S5_GROUPS * S5_STATE
N_IN = 12 * BR
ALPHA = (2 * DEPTH) ** 0.25
LN_EPS = 1e-5
NEG = -1e30
ADAM_LR, ADAM_B1, ADAM_B2, ADAM_EPS, ADAM_WD, ADAM_STEP = 0.001, 0.9, 0.999, 1e-08, 0.01, 10
CB_AB, CB_AC, CB_AX, CB_AG, CB_Q, CB_K, CB_V, CB_BG, CB_CX, CB_CG, CB_DU, CB_DG = range(12)
N_CHIPS = 4
N_DEV = 8


def _params(n_axes=0):
    kw = {"dimension_semantics": ("arbitrary",) * n_axes} if n_axes else {}
    return pltpu.CompilerParams(vmem_limit_bytes=VMEM_LIMIT, **kw)


def _rows(tb, w, cb=0):
    return pl.BlockSpec((tb, w), lambda i: (i, cb))


def _prev8(tb, w, cb=0):
    return pl.BlockSpec((8, w), lambda i: (jnp.maximum(i * (tb // 8) - 1, 0), cb))


def _next8(tb, w, n_rows, cb=0):
    return pl.BlockSpec((8, w), lambda i: (jnp.minimum((i + 1) * (tb // 8), n_rows // 8 - 1), cb))


def _const(shape):
    return pl.BlockSpec(shape, lambda *_: (0,) * len(shape))


def _silu(x):
    return x * jax.nn.sigmoid(x)


def _dsilu(x):
    s = jax.nn.sigmoid(x)
    return s * (1.0 + x * (1.0 - s))


def _shift_down(cur, prev8, j):
    rolled = pltpu.roll(cur, j, 0)
    row = lax.broadcasted_iota(jnp.int32, (8, cur.shape[1]), 0)
    first = jnp.where(row < j, pltpu.roll(prev8, j, 0), rolled[0:8])
    return jnp.concatenate([first, rolled[8:]], axis=0)


def _shift_up(cur, next8, j):
    t = cur.shape[0]
    rolled = pltpu.roll(cur, t - j, 0)
    row = lax.broadcasted_iota(jnp.int32, (8, cur.shape[1]), 0)
    last = jnp.where(row >= 8 - j, pltpu.roll(next8, 8 - j, 0), rolled[t - 8:t])
    return jnp.concatenate([rolled[:t - 8], last], axis=0)


def _colsum(x):
    return jnp.sum(x, axis=0, keepdims=True)


def _init_acc(*refs):
    @pl.when(pl.program_id(0) == 0)
    def _():
        for r in refs:
            r[...] = jnp.zeros_like(r)


def _call(body, *, name, out_shape, grid, in_specs, out_specs, scratch_shapes, args, carry=None):
    out_shape, out_specs, in_specs = tuple(out_shape), tuple(out_specs), list(in_specs)
    scratch_shapes = list(scratch_shapes)
    if carry is None:
        return pl.pallas_call(body, name=name, out_shape=out_shape, grid=grid, in_specs=in_specs, out_specs=out_specs,
                              scratch_shapes=scratch_shapes, compiler_params=_params(len(grid)))(*args)
    n_in, n_out, n_scr = len(in_specs), len(out_shape), len(scratch_shapes)

    def wrapped(*refs):
        ins, refs = refs[:n_in], refs[n_in:]
        x_ins, refs = refs[:carry.n_in], refs[carry.n_in:]
        outs, refs = refs[:n_out], refs[n_out:]
        x_outs, refs = refs[:carry.n_out], refs[carry.n_out:]
        scr, x_sems = refs[:n_scr], refs[n_scr:]
        at = [pl.program_id(d) for d in range(len(grid))]
        first = functools.reduce(lambda p, q: p & q, [i == 0 for i in at])
        last = functools.reduce(lambda p, q: p & q, [i == g - 1 for i, g in zip(at, grid)])
        pl.when(first)(lambda: carry.start(x_ins, x_outs, x_sems))
        body(*ins, *outs, *scr)
        pl.when(last)(lambda: carry.wait(x_ins, x_outs, x_sems))

    return pl.pallas_call(
        wrapped, name=name, out_shape=out_shape + carry.out_shapes, grid=grid, in_specs=in_specs + [ANY] * carry.n_in,
        out_specs=out_specs + (ANY,) * carry.n_out, scratch_shapes=scratch_shapes + carry.scratch,
        compiler_params=_params(len(grid)))(*args, *carry.arrays)


def _mm(a, b, *, name, ta=False, tb=False, out_dtype=F32, tm=512, tn=512, tk=512, bias=None, carry=None):
    m, k = (a.shape[1], a.shape[0]) if ta else a.shape
    n = b.shape[0] if tb else b.shape[1]
    assert k == (b.shape[1] if tb else b.shape[0]), (name, a.shape, b.shape)
    tm, tn, tk = min(tm, m), min(tn, n), min(tk, k)
    nk = k // tk
    assert m % tm == 0 and n % tn == 0 and k % tk == 0, (name, m, n, k)

    def body(*refs):
        if bias is None:
            a_ref, b_ref, o_ref, acc = refs
        else:
            a_ref, b_ref, bias_ref, o_ref, acc = refs
        kk = pl.program_id(2)

        @pl.when(kk == 0)
        def _():
            acc[...] = jnp.zeros_like(acc)

        dims = (((0 if ta else 1,), (1 if tb else 0,)), ((), ()))
        acc[...] += lax.dot_general(a_ref[...].astype(MXU_DTYPE), b_ref[...].astype(MXU_DTYPE), dims,
                                    preferred_element_type=F32)

        @pl.when(kk == nk - 1)
        def _():
            r = acc[...]
            if bias is not None:
                r = r + bias_ref[...]
            o_ref[...] = r.astype(out_dtype)

    a_spec = (pl.BlockSpec((tk, tm), lambda i, j, kk: (kk, i)) if ta
              else pl.BlockSpec((tm, tk), lambda i, j, kk: (i, kk)))
    b_spec = (pl.BlockSpec((tn, tk), lambda i, j, kk: (j, kk)) if tb
              else pl.BlockSpec((tk, tn), lambda i, j, kk: (kk, j)))
    in_specs, args = [a_spec, b_spec], [a, b]
    if bias is not None:
        in_specs.append(pl.BlockSpec((1, tn), lambda i, j, kk: (0, j)))
        args.append(bias)
    out = _call(body, name=name, out_shape=[SDS((m, n), out_dtype)], grid=(m // tm, n // tn, nk), in_specs=in_specs,
                out_specs=[pl.BlockSpec((tm, tn), lambda i, j, kk: (i, j))],
                scratch_shapes=[pltpu.VMEM((tm, tn), F32)], args=args, carry=carry)
    return out[0] if carry is None else out


def _silu_rows(c_all):
    def body(c_ref, o_ref):
        o_ref[...] = _silu(c_ref[...])
    return pl.pallas_call(body, name="cond_silu", out_shape=SDS(c_all.shape, F32))(c_all)


def _modulate(x, scale, shift, tb):
    s, d = x.shape

    def body(x_ref, sc_ref, sh_ref, o_ref):
        o_ref[...] = (x_ref[...] * (1.0 + sc_ref[...]) + sh_ref[...]).astype(MXU_DTYPE)

    return pl.pallas_call(body, name="modulate", out_shape=SDS((s, d), MXU_DTYPE), grid=(s // tb,),
                          in_specs=[_rows(tb, d), _const((1, d)), _const((1, d))], out_specs=_rows(tb, d),
                          compiler_params=_params(1))(x, scale, shift)


def _out_ln(ycat, w_out, x, gate, ln_g, ln_b, next_scale, next_shift, tb):
    s, d = x.shape

    def body(yc_ref, w_ref, x_ref, gt_ref, g_ref, b_ref, sc_ref, sh_ref, xn_ref, xh_ref, y_ref, rs_ref, hn_ref):
        y = jnp.dot(yc_ref[...], w_ref[...], preferred_element_type=F32)
        res = ALPHA * x_ref[...] + (1.0 + gt_ref[...]) * y
        mu = jnp.mean(res, axis=-1, keepdims=True)
        cen = res - mu
        var = jnp.mean(cen * cen, axis=-1, keepdims=True)
        rstd = lax.rsqrt(var + LN_EPS)
        xhat = cen * rstd
        xn = xhat * g_ref[...] + b_ref[...]
        xn_ref[...] = xn
        xh_ref[...] = xhat
        y_ref[...] = y
        rs_ref[...] = rstd
        hn_ref[...] = (xn * (1.0 + sc_ref[...]) + sh_ref[...]).astype(MXU_DTYPE)

    big = SDS((s, d), F32)
    return pl.pallas_call(
        body, name="out_proj_ln", out_shape=(big, big, big, SDS((s, 1), F32), SDS((s, d), MXU_DTYPE)), grid=(s // tb,),
        in_specs=[_rows(tb, d), pl.BlockSpec((d, d), lambda i: (0, 0), pipeline_mode=pl.Buffered(1)), _rows(tb, d)]
        + [_const((1, d))] * 5,
        out_specs=(_rows(tb, d), _rows(tb, d), _rows(tb, d), _rows(tb, 1), _rows(tb, d)), compiler_params=_params(1),
    )(ycat, w_out, x, gate, ln_g, ln_b, next_scale, next_shift)


def _ln_bwd(dxn, xhat, y, rstd, ln_g, gate, w_out, tb):
    s, d = dxn.shape

    def body(dxn_ref, xh_ref, y_ref, rs_ref, g_ref, gt_ref, w_ref, dy_ref, dxa_ref, dg_ref, db_ref, dgt_ref, dyc_ref):
        _init_acc(dg_ref, db_ref, dgt_ref)
        dxn_t, xh = dxn_ref[...], xh_ref[...]
        dxh = dxn_t * g_ref[...]
        dres = rs_ref[...] * (dxh - jnp.mean(dxh, axis=-1, keepdims=True)
                              - xh * jnp.mean(dxh * xh, axis=-1, keepdims=True))
        dyb = ((1.0 + gt_ref[...]) * dres).astype(MXU_DTYPE)
        dy_ref[...] = dyb
        dxa_ref[...] = ALPHA * dres
        dg_ref[...] += _colsum(dxn_t * xh)
        db_ref[...] += _colsum(dxn_t)
        dgt_ref[...] += _colsum(dres * y_ref[...])
        dyc_ref[...] = lax.dot_general(dyb, w_ref[...], (((1,), (1,)), ((), ())), preferred_element_type=F32)

    vec = SDS((1, d), F32)
    return pl.pallas_call(
        body, name="ln_bwd_dycat", out_shape=(SDS((s, d), MXU_DTYPE), SDS((s, d), F32), vec, vec, vec, SDS((s, d), F32)),
        grid=(s // tb,),
        in_specs=[_rows(tb, d), _rows(tb, d), _rows(tb, d), _rows(tb, 1), _const((1, d)), _const((1, d)),
                  pl.BlockSpec((d, d), lambda i: (0, 0), pipeline_mode=pl.Buffered(1))],
        out_specs=(_rows(tb, d), _rows(tb, d), _const((1, d)), _const((1, d)), _const((1, d)), _rows(tb, d)),
        compiler_params=_params(1))(dxn, xhat, y, rstd, ln_g, gate, w_out)


def _dh_mod_bwd(dproj, w_in, dxa, x, scale, carry=None):
    s, d = dxa.shape
    k = dproj.shape[1]
    tm, tn, tk = min(1024, s), 1024, 1536
    nk = k // tk
    assert s % tm == 0 and d % tn == 0 and k % tk == 0

    def body(a_ref, b_ref, dxa_ref, x_ref, sc_ref, dx_ref, dsh_ref, dsc_ref, acc):
        i, kk = pl.program_id(1), pl.program_id(2)

        @pl.when(kk == 0)
        def _():
            acc[...] = jnp.zeros_like(acc)

        @pl.when((kk == 0) & (i == 0))
        def _():
            dsh_ref[...] = jnp.zeros_like(dsh_ref)
            dsc_ref[...] = jnp.zeros_like(dsc_ref)

        acc[...] += lax.dot_general(a_ref[...], b_ref[...], (((1,), (1,)), ((), ())), preferred_element_type=F32)

        @pl.when(kk == nk - 1)
        def _():
            dh_t = acc[...]
            dx_ref[...] = dxa_ref[...] + dh_t * (1.0 + sc_ref[...])
            dsh_ref[...] += _colsum(dh_t)
            dsc_ref[...] += _colsum(dh_t * x_ref[...])

    tile = pl.BlockSpec((tm, tn), lambda j, i, kk: (i, j))
    vec = pl.BlockSpec((1, tn), lambda j, i, kk: (0, j))
    return _call(
        body, name="dh", out_shape=(SDS((s, d), F32), SDS((1, d), F32), SDS((1, d), F32)),
        grid=(d // tn, s // tm, nk),
        in_specs=[pl.BlockSpec((tm, tk), lambda j, i, kk: (i, kk)), pl.BlockSpec((tn, tk), lambda j, i, kk: (j, kk)),
                  tile, tile, vec],
        out_specs=(tile, vec, vec), scratch_shapes=[pltpu.VMEM((tm, tn), F32)],
        args=(dproj, w_in, dxa, x, scale), carry=carry)


def _out_ln_loss(ycat, w_out, x, gate, ln_g, ln_b, target, tb):
    s, d = x.shape

    def body(yc_ref, w_ref, x_ref, gt_ref, g_ref, b_ref, t_ref, l_ref, dy_ref, dxa_ref, dg_ref, db_ref, dgt_ref):
        _init_acc(l_ref, dg_ref, db_ref, dgt_ref)
        y = jnp.dot(yc_ref[...], w_ref[...], preferred_element_type=F32)
        res = ALPHA * x_ref[...] + (1.0 + gt_ref[...]) * y
        cen = res - jnp.mean(res, axis=-1, keepdims=True)
        rstd = lax.rsqrt(jnp.mean(cen * cen, axis=-1, keepdims=True) + LN_EPS)
        xh = cen * rstd
        err = xh * g_ref[...] + b_ref[...] - t_ref[...]
        l_ref[...] += (0.5 / d) * jnp.sum(err * err, keepdims=True)
        dxn_t = err * (1.0 / d)
        dxh = dxn_t * g_ref[...]
        dres = rstd * (dxh - jnp.mean(dxh, axis=-1, keepdims=True) - xh * jnp.mean(dxh * xh, axis=-1, keepdims=True))
        dy_ref[...] = ((1.0 + gt_ref[...]) * dres).astype(MXU_DTYPE)
        dxa_ref[...] = ALPHA * dres
        dg_ref[...] += _colsum(dxn_t * xh)
        db_ref[...] += _colsum(dxn_t)
        dgt_ref[...] += _colsum(dres * y)

    vec = SDS((1, d), F32)
    return pl.pallas_call(
        body, name="out_proj_ln_loss", out_shape=(SDS((1, 1), F32), SDS((s, d), MXU_DTYPE), SDS((s, d), F32), vec, vec, vec),
        grid=(s // tb,),
        in_specs=[_rows(tb, d), pl.BlockSpec((d, d), lambda i: (0, 0), pipeline_mode=pl.Buffered(1)), _rows(tb, d),
                  _const((1, d)), _const((1, d)), _const((1, d)), _rows(tb, d)],
        out_specs=(_const((1, 1)), _rows(tb, d), _rows(tb, d), _const((1, d)), _const((1, d)), _const((1, d))),
        compiler_params=_params(1))(ycat, w_out, x, gate, ln_g, ln_b, target)


def _conv_taps(u, up, w_ref, width):
    out = w_ref[width - 1:width, :] * u
    for j in range(width - 2, -1, -1):
        out = out + w_ref[j:j + 1, :] * _shift_down(u, up, width - 1 - j)
    return out


def _conv_taps_t(g, gn, w_ref, width):
    out = w_ref[width - 1:width, :] * g
    for j in range(width - 2, -1, -1):
        out = out + w_ref[j:j + 1, :] * _shift_up(g, gn, width - 1 - j)
    return out


def _conv_wgrad(dw_ref, g, u, up, width):
    dw_ref[width - 1:width, :] += _colsum(g * u)
    for j in range(width - 1):
        dw_ref[j:j + 1, :] += _colsum(g * _shift_down(u, up, width - 1 - j))


def _branch_a_fwd(proj, conv_w, tb):
    s = proj.shape[0]

    def body(ab, ac, ax, ag, acp, axp, w_ref, o_ref):
        has_prev = (pl.program_id(0) > 0).astype(F32)
        u = ac[...] * ax[...]
        up = acp[...] * axp[...] * has_prev
        o_ref[...] = (ab[...] * _conv_taps(u, up, w_ref, 3) * _silu(ag[...])).astype(MXU_DTYPE)

    return pl.pallas_call(
        body, name="branch_a_fwd", out_shape=SDS((s, BR), MXU_DTYPE), grid=(s // tb,),
        in_specs=[_rows(tb, BR, CB_AB), _rows(tb, BR, CB_AC), _rows(tb, BR, CB_AX), _rows(tb, BR, CB_AG),
                  _prev8(tb, BR, CB_AC), _prev8(tb, BR, CB_AX), _const((8, BR))],
        out_specs=_rows(tb, BR), compiler_params=_params(1))(proj, proj, proj, proj, proj, proj, conv_w)


def _branch_a_bwd(dycat, proj, conv_w, tb):
    s = proj.shape[0]

    def body(dy, dyn, ab, abn, ag, agn, ac, acp, ax, axp, w_ref, o_ref, dw_ref):
        _init_acc(dw_ref)
        i = pl.program_id(0)
        has_prev = (i > 0).astype(F32)
        has_next = (i < pl.num_programs(0) - 1).astype(F32)
        u = ac[...] * ax[...]
        up = acp[...] * axp[...] * has_prev
        v = _conv_taps(u, up, w_ref, 3)
        sg = _silu(ag[...])
        dv = dy[...] * ab[...] * sg
        dvn = dyn[...] * abn[...] * _silu(agn[...]) * has_next
        du = _conv_taps_t(dv, dvn, w_ref, 3)
        o_ref[:, 0:BR] = (dy[...] * v * sg).astype(MXU_DTYPE)
        o_ref[:, BR:2 * BR] = (du * ax[...]).astype(MXU_DTYPE)
        o_ref[:, 2 * BR:3 * BR] = (du * ac[...]).astype(MXU_DTYPE)
        o_ref[:, 3 * BR:4 * BR] = (dy[...] * ab[...] * v * _dsilu(ag[...])).astype(MXU_DTYPE)
        _conv_wgrad(dw_ref, dv, u, up, 3)

    return pl.pallas_call(
        body, name="branch_a_bwd", out_shape=(SDS((s, 4 * BR), MXU_DTYPE), SDS((8, BR), F32)), grid=(s // tb,),
        in_specs=[_rows(tb, BR, 0), _next8(tb, BR, s, 0),
                  _rows(tb, BR, CB_AB), _next8(tb, BR, s, CB_AB), _rows(tb, BR, CB_AG), _next8(tb, BR, s, CB_AG),
                  _rows(tb, BR, CB_AC), _prev8(tb, BR, CB_AC), _rows(tb, BR, CB_AX), _prev8(tb, BR, CB_AX),
                  _const((8, BR))],
        out_specs=(_rows(tb, 4 * BR), _const((8, BR))), compiler_params=_params(1),
    )(dycat, dycat, proj, proj, proj, proj, proj, proj, proj, proj, conv_w)


def _t5_bucket(dist):
    max_exact = REL_BUCKETS // 2
    nf = jnp.maximum(dist, 1).astype(F32)
    large = max_exact + (jnp.log(nf / max_exact) / math.log(REL_MAX_DIST / max_exact)
                         * (REL_BUCKETS - max_exact)).astype(jnp.int32)
    large = jnp.minimum(large, REL_BUCKETS - 1)
    return jnp.where(dist < max_exact, dist, large)


def _bucket_maps():
    maps = []
    i = jnp.arange(BLK)[:, None]
    j = jnp.arange(2 * BLK)[None, :]
    delta = i + BLK - j
    for window, dil in DILATIONS:
        span = window // dil
        bucket = _t5_bucket(jnp.clip(delta, 0, span) * dil)
        maps.append(jnp.where((delta >= 0) & (delta <= span), bucket, -1))
    return jnp.stack(maps).astype(jnp.int32)


def _bias_tables(rel_bias, buckets):
    n_pat = len(DILATIONS)

    def body(rb_ref, bk_ref, o_ref):
        for g in range(n_pat):
            bk = bk_ref[g]
            for h in range(ATT_HEADS):
                def per_bucket(b, acc):
                    return jnp.where(bk == b, rb_ref[b, h], acc)
                o_ref[g, h] = lax.fori_loop(0, REL_BUCKETS, per_bucket, jnp.full((BLK, 2 * BLK), NEG, F32))

    return pl.pallas_call(
        body, name="bias_tables", out_shape=SDS((n_pat, ATT_HEADS, BLK, 2 * BLK), F32),
        in_specs=[pl.BlockSpec(memory_space=pltpu.SMEM), pl.BlockSpec(memory_space=pltpu.VMEM)],
        compiler_params=_params())(rel_bias, buckets)


def _head_masks():
    lane = lax.broadcasted_iota(jnp.int32, (1, 2 * HEAD_DIM), 1)
    return [(lane < HEAD_DIM).astype(F32), (lane >= HEAD_DIM).astype(F32)]


def _strided(base, size, dil):
    return pl.ds(base, size, stride=dil) if dil > 1 else pl.ds(pl.multiple_of(base, BLK), size)


def _attn_groups(s, dil):
    return max(1, min(1024, s) // (dil * BLK)) if dil == 1 else max(1, min(2048, s) // (dil * BLK))


def _attn_fwd(proj, bias, dil):
    s = proj.shape[0]
    grp = _attn_groups(s, dil)
    u1 = dil * BLK
    unit = grp * u1
    nb = s // unit
    w = 2 * HEAD_DIM
    q0, k0, v0 = (cb * (BR // w) for cb in (CB_Q, CB_K, CB_V))

    def body(q_ref, kc_ref, kp_ref, vc_ref, vp_ref, bias_ref, o_ref, lse_ref, kbuf, vbuf):
        n = pl.program_id(1)
        col = lax.broadcasted_iota(jnp.int32, (1, 2 * BLK), 1)
        masks = _head_masks()
        kbuf[0:u1, :] = kp_ref[...]
        kbuf[u1:, :] = kc_ref[...]
        vbuf[0:u1, :] = vp_ref[...]
        vbuf[u1:, :] = vc_ref[...]

        def per_r(t, carry):
            j = t // dil
            base = j * u1 + t % dil
            rows = _strided(base, BLK, dil)
            no_prev = jnp.where((n == 0) & (j == 0) & (col < BLK), NEG, 0.0)
            q = q_ref[rows, :] * (HEAD_DIM ** -0.5)
            k = kbuf[_strided(base, 2 * BLK, dil), :].astype(MXU_DTYPE)
            v = vbuf[_strided(base, 2 * BLK, dil), :].astype(MXU_DTYPE)
            q2 = jnp.concatenate([q * masks[0], q * masks[1]], axis=0).astype(MXU_DTYPE)
            sc = lax.dot_general(q2, k, (((1,), (1,)), ((), ())), preferred_element_type=F32)
            sc = sc + jnp.concatenate([bias_ref[0], bias_ref[1]], axis=0) + no_prev
            mx = jnp.max(sc, axis=-1, keepdims=True)
            p = jnp.exp(sc - mx)
            l = jnp.sum(p, axis=-1, keepdims=True)
            o2 = jnp.dot((p / l).astype(MXU_DTYPE), v, preferred_element_type=F32)
            lse2 = mx + jnp.log(l)
            o_ref[rows, :] = o2[0:BLK] * masks[0] + o2[BLK:2 * BLK] * masks[1]
            lse_ref[rows, :] = lse2[0:BLK] * masks[0] + lse2[BLK:2 * BLK] * masks[1]
            return carry

        lax.fori_loop(0, grp * dil, per_r, 0, unroll=8)

    cur = lambda c0: pl.BlockSpec((unit, w), lambda hp, n: (n, c0 + hp))
    prev = lambda c0: pl.BlockSpec((u1, w), lambda hp, n: (jnp.maximum(n * grp - 1, 0), c0 + hp))
    out = pl.BlockSpec((unit, w), lambda hp, n: (n, hp))
    return pl.pallas_call(
        body, name=f"attn_fwd_d{dil}", out_shape=(SDS((s, BR), F32), SDS((s, BR), F32)), grid=(BR // w, nb),
        in_specs=[cur(q0), cur(k0), prev(k0), cur(v0), prev(v0),
                  pl.BlockSpec((2, BLK, 2 * BLK), lambda hp, n: (hp, 0, 0))],
        out_specs=(out, out),
        scratch_shapes=[pltpu.VMEM((unit + u1, w), F32), pltpu.VMEM((unit + u1, w), F32)],
        compiler_params=_params(2))(proj, proj, proj, proj, proj, bias)


def _softmax3(l0, l1, l2):
    mx = jnp.maximum(jnp.maximum(l0, l1), l2)
    e0, e1, e2 = jnp.exp(l0 - mx), jnp.exp(l1 - mx), jnp.exp(l2 - mx)
    inv = 1.0 / (e0 + e1 + e2)
    return e0 * inv, e1 * inv, e2 * inv


def _attn_combine(os_, lses, proj, tb):
    s = proj.shape[0]

    def body(o0, o1, o2, l0, l1, l2, bg, y_ref):
        w0, w1, w2 = _softmax3(l0[...], l1[...], l2[...])
        attn = w0 * o0[...] + w1 * o1[...] + w2 * o2[...]
        y_ref[...] = (attn * _silu(bg[...])).astype(MXU_DTYPE)

    return pl.pallas_call(
        body, name="attn_combine", out_shape=SDS((s, BR), MXU_DTYPE), grid=(s // tb,),
        in_specs=[_rows(tb, BR)] * 6 + [_rows(tb, BR, CB_BG)], out_specs=_rows(tb, BR),
        compiler_params=_params(1))(*os_, *lses, proj)


def _attn_bwd_pre(dycat, os_, lses, proj, head_ones, tb):
    s = proj.shape[0]

    def body(dy, o0, o1, o2, l0, l1, l2, bg, e_ref, dbg_ref, do0, do1, do2, dm0, dm1, dm2):
        w0, w1, w2 = _softmax3(l0[...], l1[...], l2[...])
        attn = w0 * o0[...] + w1 * o1[...] + w2 * o2[...]
        dattn = dy[...] * _silu(bg[...])
        dbg_ref[...] = (dy[...] * attn * _dsilu(bg[...])).astype(MXU_DTYPE)
        prod = dattn * attn
        hi = prod.astype(MXU_DTYPE)
        lo = (prod - hi.astype(F32)).astype(MXU_DTYPE)
        tot = (jnp.dot(hi, e_ref[...], preferred_element_type=F32)
               + jnp.dot(lo, e_ref[...], preferred_element_type=F32))
        for wg, do_ref, dm_ref in ((w0, do0, dm0), (w1, do1, dm1), (w2, do2, dm2)):
            do_ref[...] = wg * dattn
            dm_ref[...] = wg * tot

    big = SDS((s, BR), F32)
    return pl.pallas_call(
        body, name="attn_bwd_pre", out_shape=(SDS((s, BR), MXU_DTYPE),) + (big,) * 6, grid=(s // tb,),
        in_specs=[_rows(tb, BR, 1)] + [_rows(tb, BR)] * 6 + [_rows(tb, BR, CB_BG), _const((BR, BR))],
        out_specs=(_rows(tb, BR),) * 7, compiler_params=_params(1))(dycat, *os_, *lses, proj, head_ones)


def _attn_bwd(proj, do, lse, dm, bias, dil, carry=None):
    s = proj.shape[0]
    grp = _attn_groups(s, dil)
    u1 = dil * BLK
    unit = grp * u1
    nb = s // unit
    w = 2 * HEAD_DIM
    q0, k0, v0 = (cb * (BR // w) for cb in (CB_Q, CB_K, CB_V))

    def body(q_ref, kc_ref, kp_ref, vc_ref, vp_ref, do_ref, lse_ref, dm_ref, bias_ref,
             dq_ref, dk_ref, dv_ref, dbias_ref, kbuf, vbuf, stage_k, stage_v):
        n = pl.program_id(1)
        col = lax.broadcasted_iota(jnp.int32, (1, 2 * BLK), 1)
        masks = _head_masks()

        @pl.when(n == 0)
        def _():
            dbias_ref[...] = jnp.zeros_like(dbias_ref)
            stage_k[...] = jnp.zeros_like(stage_k)
            stage_v[...] = jnp.zeros_like(stage_v)

        for out_ref, stage in ((dk_ref, stage_k), (dv_ref, stage_v)):
            if grp > 1:
                out_ref[0:unit - u1, :] = stage[u1:unit, :]
            stage[0:u1, :] = stage[unit:unit + u1, :]

        @pl.when(n < nb)
        def _():
            kbuf[0:u1, :] = kp_ref[...]
            kbuf[u1:, :] = kc_ref[...]
            vbuf[0:u1, :] = vp_ref[...]
            vbuf[u1:, :] = vc_ref[...]

            def per_r(t, carry):
                j = t // dil
                base = j * u1 + t % dil
                rows = _strided(base, BLK, dil)
                rows_hi = _strided(base + u1, BLK, dil)
                no_prev = jnp.where((n == 0) & (j == 0) & (col < BLK), NEG, 0.0)
                q = q_ref[rows, :] * (HEAD_DIM ** -0.5)
                k = kbuf[_strided(base, 2 * BLK, dil), :].astype(MXU_DTYPE)
                v = vbuf[_strided(base, 2 * BLK, dil), :].astype(MXU_DTYPE)
                do_t, lse_t, dm_t = do_ref[rows, :], lse_ref[rows, :], dm_ref[rows, :]
                stack = lambda t: jnp.concatenate([t * masks[0], t * masks[1]], axis=0).astype(MXU_DTYPE)
                per_head = lambda t: jnp.concatenate([t[:, 0:1], t[:, HEAD_DIM:HEAD_DIM + 1]], axis=0)
                q2, do2 = stack(q), stack(do_t)
                sc = lax.dot_general(q2, k, (((1,), (1,)), ((), ())), preferred_element_type=F32)
                p = jnp.exp(sc + jnp.concatenate([bias_ref[0], bias_ref[1]], axis=0) + no_prev - per_head(lse_t))
                dp = lax.dot_general(do2, v, (((1,), (1,)), ((), ())), preferred_element_type=F32)
                ds = p * (dp - per_head(dm_t))
                dbias_ref[0] += ds[0:BLK]
                dbias_ref[1] += ds[BLK:2 * BLK]
                dsb, pb = ds.astype(MXU_DTYPE), p.astype(MXU_DTYPE)
                dq2 = jnp.dot(dsb, k, preferred_element_type=F32)
                dk_acc = lax.dot_general(dsb, q2, (((0,), (0,)), ((), ())), preferred_element_type=F32)
                dv_acc = lax.dot_general(pb, do2, (((0,), (0,)), ((), ())), preferred_element_type=F32)
                dq_ref[rows, :] = (dq2[0:BLK] * masks[0] + dq2[BLK:2 * BLK] * masks[1]) * (HEAD_DIM ** -0.5)
                stage_k[rows, :] = stage_k[rows, :] + dk_acc[0:BLK]
                stage_v[rows, :] = stage_v[rows, :] + dv_acc[0:BLK]
                stage_k[rows_hi, :] = dk_acc[BLK:2 * BLK]
                stage_v[rows_hi, :] = dv_acc[BLK:2 * BLK]
                return carry

            lax.fori_loop(0, grp * dil, per_r, 0, unroll=8)

        dk_ref[unit - u1:unit, :] = stage_k[0:u1, :]
        dv_ref[unit - u1:unit, :] = stage_v[0:u1, :]

    qn = lambda n: jnp.minimum(n, nb - 1)
    cur = lambda c0: pl.BlockSpec((unit, w), lambda hp, n: (qn(n), c0 + hp))
    prev = lambda c0: pl.BlockSpec((u1, w), lambda hp, n: (jnp.maximum(qn(n) * grp - 1, 0), c0 + hp))
    row = pl.BlockSpec((unit, w), lambda hp, n: (qn(n), hp))
    late = pl.BlockSpec((unit, w), lambda hp, n: (jnp.maximum(n - 1, 0), hp))
    tab = pl.BlockSpec((2, BLK, 2 * BLK), lambda hp, n: (hp, 0, 0))
    big = SDS((s, BR), F32)
    return _call(
        body, name=f"attn_bwd_d{dil}", out_shape=(big, big, big, SDS((ATT_HEADS, BLK, 2 * BLK), F32)),
        grid=(BR // w, nb + 1),
        in_specs=[cur(q0), cur(k0), prev(k0), cur(v0), prev(v0), row, row, row, tab],
        out_specs=(row, late, late, tab),
        scratch_shapes=[pltpu.VMEM((unit + u1, w), F32)] * 4,
        args=(proj, proj, proj, proj, proj, do, lse, dm, bias), carry=carry)


def _rel_bias_grad(dbias, buckets):
    def body(db_ref, bk_ref, o_ref):
        row = lax.broadcasted_iota(jnp.int32, (REL_BUCKETS, 128), 0)
        lane = lax.broadcasted_iota(jnp.int32, (REL_BUCKETS, 128), 1)

        def per_bucket(b, acc):
            for g in range(len(DILATIONS)):
                hit = bk_ref[g] == b
                for h in range(ATT_HEADS):
                    both = db_ref[0, g, h] + db_ref[1, g, h]
                    val = jnp.sum(jnp.where(hit, both, 0.0), keepdims=True)
                    acc = acc + jnp.where((row == b) & (lane == h), val, 0.0)
            return acc

        o_ref[...] = lax.fori_loop(0, REL_BUCKETS, per_bucket, jnp.zeros((REL_BUCKETS, 128), F32))

    assert dbias.shape[0] == DEPTH == 2
    return pl.pallas_call(body, name="rel_bias_grad", out_shape=SDS((REL_BUCKETS, 128), F32),
                          compiler_params=_params())(dbias, buckets)


def _scan_rows(a_ref, b_ref, o_ref, carry, *, reverse):
    tb = a_ref.shape[0]
    order = range(7, -1, -1) if reverse else range(8)

    @pl.when(pl.program_id(0) == 0)
    def _():
        carry[...] = jnp.zeros_like(carry)

    def group(gi, h):
        r0 = pl.multiple_of((tb // 8 - 1 - gi if reverse else gi) * 8, 8)
        a8, b8 = a_ref[pl.ds(r0, 8), :], b_ref[pl.ds(r0, 8), :]
        rows = [None] * 8
        for k in order:
            if reverse:
                rows[k] = b8[k:k + 1] + h
                h = a8[k:k + 1] * rows[k]
            else:
                h = a8[k:k + 1] * h + b8[k:k + 1]
                rows[k] = h
        o_ref[pl.ds(r0, 8), :] = jnp.concatenate(rows, axis=0)
        return h

    carry[...] = lax.fori_loop(0, tb // 8, group, carry[...])


def _lru_scan_fwd(a, b, proj, tb):
    s = a.shape[0]

    def body(a_ref, b_ref, g_ref, h_ref, y_ref, carry):
        _scan_rows(a_ref, b_ref, h_ref, carry, reverse=False)
        y_ref[...] = (h_ref[...] * _silu(g_ref[...])).astype(MXU_DTYPE)

    return pl.pallas_call(
        body, name="lru_scan", out_shape=(SDS((s, BR), F32), SDS((s, BR), MXU_DTYPE)), grid=(s // tb,),
        in_specs=[_rows(tb, BR), _rows(tb, BR), _rows(tb, BR, CB_CG)], out_specs=(_rows(tb, BR), _rows(tb, BR)),
        scratch_shapes=[pltpu.VMEM((1, BR), F32)], compiler_params=_params(1))(a, b, proj)


def _lru_scan_bwd(a, dycat, h, proj, tb):
    s = a.shape[0]
    nt = s // tb

    def body(a_ref, dy_ref, h_ref, g_ref, l_ref, dg_ref, carry, dh_buf):
        dh_buf[...] = dy_ref[...] * _silu(g_ref[...])
        dg_ref[...] = (dy_ref[...] * h_ref[...] * _dsilu(g_ref[...])).astype(MXU_DTYPE)
        _scan_rows(a_ref, dh_buf, l_ref, carry, reverse=True)

    rev = lambda cb=0: pl.BlockSpec((tb, BR), lambda i: (nt - 1 - i, cb))
    return pl.pallas_call(
        body, name="lru_scan_bwd", out_shape=(SDS((s, BR), F32), SDS((s, BR), MXU_DTYPE)), grid=(nt,),
        in_specs=[rev(), rev(2), rev(), rev(CB_CG)], out_specs=(rev(), rev()),
        scratch_shapes=[pltpu.VMEM((1, BR), F32), pltpu.VMEM((tb, BR), F32)],
        compiler_params=_params(1))(a, dycat, h, proj)


def _scan_tile(s):
    return min(512, s)


def _load_chunked(ref, t0, pt):
    ln = pt // 8
    return jnp.concatenate([ref[pl.ds(t0 + j, 8, stride=ln), :] for j in range(ln)], axis=0)


def _store_natural(ref, t0, pt, val):
    ln = pt // 8
    for j in range(ln):
        ref[pl.ds(t0 + j, 8, stride=ln), :] = val[j * 8:(j + 1) * 8]


def _scan_tile_in_place(a_ref, x_ref, carry, pw, *, reverse):
    ch2 = x_ref.shape[1]
    ch = ch2 // 2
    ln = x_ref.shape[0] // 8
    ar = a_ref[:, 0:ch]
    ai = -a_ref[:, ch:ch2] if reverse else a_ref[:, ch:ch2]

    def cmul(pr, pi, xr, xi):
        return pr * xr - pi * xi, pr * xi + pi * xr

    @pl.when(pl.program_id(0) == 0)
    def _():
        carry[...] = jnp.zeros_like(carry)

        def fill(j, p):
            pw[pl.ds(j, 1), 0:ch] = p[0]
            pw[pl.ds(j, 1), ch:ch2] = p[1]
            return cmul(ar, ai, *p)

        lax.fori_loop(0, ln, fill, (ar, ai))

    def rows_of(j):
        return pl.ds(pl.multiple_of((ln - 1 - j if reverse else j) * 8, 8), 8)

    def local(j, x):
        rows = rows_of(j)
        nr, ni = cmul(ar, ai, *x)
        xr, xi = nr + x_ref[rows, 0:ch], ni + x_ref[rows, ch:ch2]
        x_ref[rows, 0:ch] = xr
        x_ref[rows, ch:ch2] = xi
        return xr, xi

    zero = jnp.zeros((8, ch), F32)
    er, ei = lax.fori_loop(0, ln, local, (zero, zero), unroll=2)
    apr, api = pw[ln - 1:ln, 0:ch], pw[ln - 1:ln, ch:ch2]
    cr, ci = carry[:, 0:ch], carry[:, ch:ch2]
    into_r, into_i = [None] * 8, [None] * 8
    for c in (range(7, -1, -1) if reverse else range(8)):
        into_r[c], into_i[c] = cr, ci
        pr, pi = cmul(apr, api, cr, ci)
        cr, ci = er[c:c + 1] + pr, ei[c:c + 1] + pi
    carry[:, 0:ch] = cr
    carry[:, ch:ch2] = ci
    into_r, into_i = jnp.concatenate(into_r, axis=0), jnp.concatenate(into_i, axis=0)

    def fix(j, carry_):
        rows = rows_of(j)
        dr, di = cmul(pw[pl.ds(j, 1), 0:ch], pw[pl.ds(j, 1), ch:ch2], into_r, into_i)
        x_ref[rows, 0:ch] += dr
        x_ref[rows, ch:ch2] += di
        return carry_

    lax.fori_loop(0, ln, fix, 0, unroll=2)


def _neg_expm1(z):
    series = -z * (1.0 + z * (0.5 + z * (1.0 / 6 + z * (1.0 / 24 + z * (1.0 / 120)))))
    return jnp.where(z > -0.05, series, 1.0 - jnp.exp(z))


def _lru_gate(xc, pre_r, pre_i, lam):
    log_a = -LRU_C * jax.nn.sigmoid(pre_r) * jax.nn.softplus(-lam)
    return jnp.exp(log_a), jnp.sqrt(_neg_expm1(2.0 * log_a)) * jax.nn.sigmoid(pre_i) * xc


def _lru_gates_fwd(proj, conv_w, conv_b, w_cat, b_cat, lam, tb):
    s = proj.shape[0]

    def body(cx, cxp, w_ref, cb_ref, wc_ref, bc_ref, lam_ref, a_ref, b_ref):
        has_prev = (pl.program_id(0) > 0).astype(F32)
        xc = _conv_taps(cx[...], cxp[...] * has_prev, w_ref, 4) + cb_ref[...]
        pre = jnp.dot(xc.astype(MXU_DTYPE), wc_ref[...], preferred_element_type=F32) + bc_ref[...]
        a_ref[...], b_ref[...] = _lru_gate(xc, pre[:, 0:BR], pre[:, BR:2 * BR], lam_ref[...])

    big = SDS((s, BR), F32)
    return pl.pallas_call(
        body, name="lru_gates_fwd", out_shape=(big, big), grid=(s // tb,),
        in_specs=[_rows(tb, BR, CB_CX), _prev8(tb, BR, CB_CX), _const((8, BR)), _const((1, BR)),
                  _const((BR, 2 * BR)), _const((1, 2 * BR)), _const((1, BR))],
        out_specs=(_rows(tb, BR), _rows(tb, BR)), compiler_params=_params(1),
    )(proj, proj, conv_w, conv_b, w_cat, b_cat, lam)


def _lru_gates_bwd(proj, lmb, h, conv_w, conv_b, w_cat, b_cat, lam, tb):
    s = proj.shape[0]

    def body(cx, cxp, l_ref, h_ref, hp_ref, w_ref, cb_ref, wc_ref, bc_ref, lam_ref,
             dxc_ref, dpre_ref, xc_ref, dbc_ref, dlam_ref):
        _init_acc(dbc_ref, dlam_ref)
        has_prev = (pl.program_id(0) > 0).astype(F32)
        xc = _conv_taps(cx[...], cxp[...] * has_prev, w_ref, 4) + cb_ref[...]
        xcb = xc.astype(MXU_DTYPE)
        pre = jnp.dot(xcb, wc_ref[...], preferred_element_type=F32) + bc_ref[...]
        _, vjp = jax.vjp(_lru_gate, xc, pre[:, 0:BR], pre[:, BR:2 * BR], lam_ref[...])
        lm = l_ref[...]
        dxc, dpr, dpi, dlam = vjp((lm * _shift_down(h_ref[...], hp_ref[...] * has_prev, 1), lm))
        dpre = jnp.concatenate([dpr, dpi], axis=1)
        dpreb = dpre.astype(MXU_DTYPE)
        dxc_ref[...] = dxc + lax.dot_general(dpreb, wc_ref[...], (((1,), (1,)), ((), ())),
                                             preferred_element_type=F32)
        dpre_ref[...] = dpreb
        xc_ref[...] = xcb
        dbc_ref[...] += _colsum(dpre)
        dlam_ref[...] += dlam

    return pl.pallas_call(
        body, name="lru_gates_bwd",
        out_shape=(SDS((s, BR), F32), SDS((s, 2 * BR), MXU_DTYPE), SDS((s, BR), MXU_DTYPE),
                   SDS((1, 2 * BR), F32), SDS((1, BR), F32)),
        grid=(s // tb,),
        in_specs=[_rows(tb, BR, CB_CX), _prev8(tb, BR, CB_CX), _rows(tb, BR), _rows(tb, BR), _prev8(tb, BR),
                  _const((8, BR)), _const((1, BR)), _const((BR, 2 * BR)), _const((1, 2 * BR)), _const((1, BR))],
        out_specs=(_rows(tb, BR), _rows(tb, 2 * BR), _rows(tb, BR), _const((1, 2 * BR)), _const((1, BR))),
        compiler_params=_params(1))(proj, proj, lmb, h, h, conv_w, conv_b, w_cat, b_cat, lam)


def _conv_c_bwd(dxc, proj, conv_w, tb):
    s = proj.shape[0]

    def body(g, gn, cx, cxp, w_ref, dcx_ref, dw_ref, db_ref):
        _init_acc(dw_ref, db_ref)
        i = pl.program_id(0)
        has_prev = (i > 0).astype(F32)
        has_next = (i < pl.num_programs(0) - 1).astype(F32)
        gt = g[...]
        dcx_ref[...] = _conv_taps_t(gt, gn[...] * has_next, w_ref, 4).astype(MXU_DTYPE)
        _conv_wgrad(dw_ref, gt, cx[...], cxp[...] * has_prev, 4)
        db_ref[...] += _colsum(gt)

    return pl.pallas_call(
        body, name="conv_c_bwd", out_shape=(SDS((s, BR), MXU_DTYPE), SDS((8, BR), F32), SDS((1, BR), F32)),
        grid=(s // tb,),
        in_specs=[_rows(tb, BR), _next8(tb, BR, s), _rows(tb, BR, CB_CX), _prev8(tb, BR, CB_CX), _const((8, BR))],
        out_specs=(_rows(tb, BR), _const((8, BR)), _const((1, BR))), compiler_params=_params(1),
    )(dxc, dxc, proj, proj, conv_w)


def _s5_disc(lam_re, lam_im, log_dt):
    dt = jnp.exp(log_dt)
    mag = jnp.exp(lam_re * dt)
    ab_re = mag * jnp.cos(lam_im * dt)
    ab_im = mag * jnp.sin(lam_im * dt)
    den = lam_re * lam_re + lam_im * lam_im
    f_re = ((ab_re - 1.0) * lam_re + ab_im * lam_im) / den
    f_im = (ab_im * lam_re - (ab_re - 1.0) * lam_im) / den
    return ab_re, ab_im, f_re, f_im


def _s5_bbar(f_re, f_im, b_re, b_im):
    return f_re * b_re - f_im * b_im, f_re * b_im + f_im * b_re


def _s5_disc_fwd(lam_re, lam_im, log_dt):
    def body(lr, li, ld, o0, o1, o2, o3):
        o0[...], o1[...], o2[...], o3[...] = _s5_disc(lr[...], li[...], ld[...])
    return pl.pallas_call(body, name="s5_disc_fwd", out_shape=(SDS(lam_re.shape, F32),) * 4)(lam_re, lam_im, log_dt)


def _s5_disc_bwd(lam_re, lam_im, log_dt, cts):
    def body(lr, li, ld, c0, c1, c2, c3, o0, o1, o2):
        _, vjp = jax.vjp(_s5_disc, lr[...], li[...], ld[...])
        o0[...], o1[...], o2[...] = vjp((c0[...], c1[...], c2[...], c3[...]))
    return pl.pallas_call(body, name="s5_disc_bwd", out_shape=(SDS(lam_re.shape, F32), SDS(lam_re.shape, F32),
                                                                SDS(log_dt.shape, F32)))(lam_re, lam_im, log_dt, *cts)


def _s5_bbar_fwd(f_re, f_im, b_re, b_im):
    def body(fr, fi, br, bi, o0, o1):
        o0[...], o1[...] = _s5_bbar(fr[...], fi[...], br[...], bi[...])
    return pl.pallas_call(body, name="s5_bbar_fwd", out_shape=(SDS(b_re.shape, F32),) * 2)(f_re, f_im, b_re, b_im)


def _s5_bbar_bwd(f_re, f_im, b_re, b_im, d_re, d_im):
    def body(fr, fi, br, bi, dr, di, o0, o1, o2, o3):
        _, vjp = jax.vjp(_s5_bbar, fr[...], fi[...], br[...], bi[...])
        o0[...], o1[...], o2[...], o3[...] = vjp((dr[...], di[...]))
    col, mat = SDS(f_re.shape, F32), SDS(b_re.shape, F32)
    return pl.pallas_call(body, name="s5_bbar_bwd", out_shape=(col, col, mat, mat))(f_re, f_im, b_re, b_im, d_re, d_im)


def _s5_tail_bwd(dycat, ylin, proj, d_skip, w_glu, b_glu, tb):
    s = proj.shape[0]

    def body(dy, yl, u, dg, dk, w_ref, b_ref, dyl_ref, dus_ref, ddg_ref, g_ref, dt_ref, ddk_ref, dbg_ref):
        _init_acc(ddk_ref, dbg_ref)
        g, gelu_vjp = jax.vjp(jax.nn.gelu, yl[...] + dk[...] * u[...])
        gb = g.astype(MXU_DTYPE)
        sg = jax.nn.sigmoid(jnp.dot(gb, w_ref[...], preferred_element_type=F32) + b_ref[...])
        dz = dy[...] * _silu(dg[...])
        ddg_ref[...] = (dy[...] * g * sg * _dsilu(dg[...])).astype(MXU_DTYPE)
        dt = dz * g * sg * (1.0 - sg)
        dtb = dt.astype(MXU_DTYPE)
        dgel = dz * sg + lax.dot_general(dtb, w_ref[...], (((1,), (1,)), ((), ())), preferred_element_type=F32)
        dyv, = gelu_vjp(dgel)
        dyl_ref[...] = dyv
        dus_ref[...] = dyv * dk[...]
        g_ref[...] = gb
        dt_ref[...] = dtb
        ddk_ref[...] += _colsum(dyv * u[...])
        dbg_ref[...] += _colsum(dt)

    big, half, vec = SDS((s, BR), F32), SDS((s, BR), MXU_DTYPE), SDS((1, BR), F32)
    return pl.pallas_call(
        body, name="s5_tail_bwd", out_shape=(big, big, half, half, half, vec, vec), grid=(s // tb,),
        in_specs=[_rows(tb, BR, 3), _rows(tb, BR), _rows(tb, BR, CB_DU), _rows(tb, BR, CB_DG), _const((1, BR)),
                  _const((BR, BR)), _const((1, BR))],
        out_specs=(_rows(tb, BR),) * 5 + (_const((1, BR)), _const((1, BR))), compiler_params=_params(1),
    )(dycat, ylin, proj, proj, d_skip, w_glu, b_glu)


def _assemble_dproj(da, dqkv, dbg, dcx, dcg, du, dus, ddg, tb):
    s = da.shape[0]

    def body(da_ref, q0, q1, q2, k0, k1, k2, v0, v1, v2, dbg_ref, dcx_ref, dcg_ref, du_ref, dus_ref, ddg_ref, o_ref):
        o_ref[:, 0:4 * BR] = da_ref[...]
        for j, parts in enumerate(((q0, q1, q2), (k0, k1, k2), (v0, v1, v2))):
            o_ref[:, (4 + j) * BR:(5 + j) * BR] = (parts[0][...] + parts[1][...] + parts[2][...]).astype(MXU_DTYPE)
        o_ref[:, 7 * BR:8 * BR] = dbg_ref[...].astype(MXU_DTYPE)
        o_ref[:, 8 * BR:9 * BR] = dcx_ref[...].astype(MXU_DTYPE)
        o_ref[:, 9 * BR:10 * BR] = dcg_ref[...].astype(MXU_DTYPE)
        o_ref[:, 10 * BR:11 * BR] = (du_ref[...] + dus_ref[...]).astype(MXU_DTYPE)
        o_ref[:, 11 * BR:12 * BR] = ddg_ref[...].astype(MXU_DTYPE)

    flat = [t for grp in dqkv for t in grp]
    return pl.pallas_call(
        body, name="assemble_dproj", out_shape=SDS((s, N_IN), MXU_DTYPE), grid=(s // tb,),
        in_specs=[_rows(tb, 4 * BR)] + [_rows(tb, BR)] * 15, out_specs=_rows(tb, N_IN),
        compiler_params=_params(1))(da, *flat, dbg, dcx, dcg, du, dus, ddg)


def _sum_leading(xs, tr, name):
    n, _, c = xs[0].shape
    nl = len(xs)
    tr = min([tr] + [x.shape[1] for x in xs])
    assert all(x.shape[1] % tr == 0 for x in xs), (name, tr)
    nrs = [x.shape[1] // tr for x in xs]
    starts = [sum(nrs[:l]) for l in range(nl)]

    def body(*refs):
        i = pl.program_id(0)
        for l in range(nl):
            @pl.when((i >= starts[l]) & (i < starts[l] + nrs[l]))
            def _():
                acc = refs[l * n][...].astype(F32)
                for ref in refs[l * n + 1:(l + 1) * n]:
                    acc = acc + ref[...].astype(F32)
                refs[nl * n][...] = acc

    specs = [pl.BlockSpec((None, tr, c), functools.partial(
        lambda i, k, l: (k, jnp.clip(i - starts[l], 0, nrs[l] - 1), 0), k=k, l=l)) for l in range(nl) for k in range(n)]
    return pl.pallas_call(body, name=name, out_shape=SDS((sum(nrs) * tr, c), F32), grid=(sum(nrs),), in_specs=specs,
                          out_specs=pl.BlockSpec((tr, c), lambda i: (i, 0)),
                          compiler_params=_params(1))(*[x for x in xs for _ in range(n)])


def _adamw(w, g_parts, m, v, tr, name):
    r, c = w.shape
    tr = min(tr, r)
    n = len(g_parts)
    assert r % tr == 0, (name, r, tr)

    def body(*refs):
        w_ref, m_ref, v_ref = refs[0], refs[1 + n], refs[2 + n]
        g_ref, d_ref, nm_ref, nv_ref = refs[3 + n:]
        g = refs[1][...]
        for ref in refs[2:1 + n]:
            g = g + ref[...]
        mm = ADAM_B1 * m_ref[...] + (1.0 - ADAM_B1) * g
        vv = ADAM_B2 * v_ref[...] + (1.0 - ADAM_B2) * jnp.square(g)
        m_hat = mm / (1.0 - ADAM_B1 ** ADAM_STEP)
        v_hat = vv / (1.0 - ADAM_B2 ** ADAM_STEP)
        g_ref[...] = g
        d_ref[...] = -ADAM_LR * (m_hat / (jnp.sqrt(v_hat) + ADAM_EPS) + ADAM_WD * w_ref[...])
        nm_ref[...] = mm
        nv_ref[...] = vv

    spec = pl.BlockSpec((tr, c), lambda i: (i, 0))
    return _call(body, name=name, out_shape=(SDS((r, c), F32),) * 4, grid=(r // tr,), in_specs=[spec] * (3 + n),
                 out_specs=(spec,) * 4, scratch_shapes=[], args=(w, *g_parts, m, v))


class _AllGather8:
    def __init__(self, block):
        self.m_per = block.shape[0]
        self.arrays, self.n_in, self.n_out = [block], 1, 1
        self.out_shapes = (SDS((N_DEV * self.m_per, block.shape[1]), block.dtype),)
        self.scratch = [pltpu.SemaphoreType.DMA((7,)), pltpu.SemaphoreType.DMA((7,)), pltpu.SemaphoreType.DMA]

    def _copies(self, ins, outs, sems):
        (x_ref,), (out_ref,), (send_sems, recv_sems, local_sem) = ins, outs, sems
        x, y, c = lax.axis_index("x"), lax.axis_index("y"), lax.axis_index("c")
        me, sibling = (x, y, c), (x, y, 1 - c)
        chips = [(1 - x, y), (x, 1 - y), (1 - x, 1 - y)]

        def rows(px, py, pc):
            return out_ref.at[pl.ds((4 * px + 2 * py + pc) * self.m_per, self.m_per), :]

        def copy(k, blk, to, src=None):
            return pltpu.make_async_remote_copy(
                src_ref=rows(*blk) if src is None else src, dst_ref=rows(*blk), send_sem=send_sems.at[k],
                recv_sem=recv_sems.at[k], device_id=to, device_id_type=MESH)

        mine = pltpu.make_async_copy(x_ref, rows(*me), local_sem)
        first = [copy(0, me, sibling, src=x_ref)]
        first += [copy(1 + j, me, (*chip, c), src=x_ref) for j, chip in enumerate(chips)]
        passed = [copy(4 + j, (*chip, c), sibling) for j, chip in enumerate(chips)]
        arrivals = [copy(1 + j, (*chip, c), me) for j, chip in enumerate(chips)]
        from_sibling = [copy(0, sibling, me)] + [copy(4 + j, (*chip, 1 - c), me) for j, chip in enumerate(chips)]
        return mine, first, passed, arrivals, from_sibling

    def start(self, ins, outs, sems):
        mine, first, _, _, _ = self._copies(ins, outs, sems)
        mine.start()
        for cp in first:
            cp.start()

    def wait(self, ins, outs, sems):
        mine, first, passed, arrivals, from_sibling = self._copies(ins, outs, sems)
        for arrived, onward in zip(arrivals, passed):
            arrived.wait_recv()
            onward.start()
        for cp in from_sibling:
            cp.wait_recv()
        for cp in first + passed:
            cp.wait_send()
        mine.wait()


def _allgather8(block, name):
    ex = _AllGather8(block)

    def body(x_ref, out_ref, *sems):
        ex.start((x_ref,), (out_ref,), sems)
        ex.wait((x_ref,), (out_ref,), sems)

    return pl.pallas_call(
        body, name=name, out_shape=ex.out_shapes[0], in_specs=[pl.BlockSpec(memory_space=pltpu.VMEM)],
        out_specs=pl.BlockSpec(memory_space=pltpu.VMEM), scratch_shapes=ex.scratch, compiler_params=_params())(block)


class _Exchange:
    def __init__(self, items, out_shapes):
        self.items, self.out_shapes = list(items), tuple(out_shapes)
        self.arrays = [it[0] for it in self.items]
        n = len(self.items)
        self.n_in, self.n_out = n, len(self.out_shapes)
        self.scratch = [pltpu.SemaphoreType.DMA((n * N_CHIPS,)), pltpu.SemaphoreType.DMA((n * N_CHIPS,)),
                        pltpu.SemaphoreType.DMA((n,))]

    def _copies(self, ins, outs, sems, m):
        send_sems, recv_sems, local_sems = sems
        c = lax.axis_index("c")
        others = [j for j in range(N_CHIPS) if j != m]

        def remote(a, src, dst, to, from_):
            return pltpu.make_async_remote_copy(
                src_ref=src, dst_ref=dst, send_sem=send_sems.at[a * N_CHIPS + to],
                recv_sem=recv_sems.at[a * N_CHIPS + from_], device_id=(to // 2, to % 2, c), device_id_type=MESH)

        local, sends, recvs = [], [], []
        for a, (_, oi, src_of, dst_of) in enumerate(self.items):
            local.append(pltpu.make_async_copy(src_of(ins[a], m), dst_of(outs[oi], m), local_sems.at[a]))
            for j in others:
                sends.append(remote(a, src_of(ins[a], j), dst_of(outs[oi], m), j, m))
                recvs.append(remote(a, src_of(ins[a], m), dst_of(outs[oi], j), j, j))
        return local, sends, recvs

    def _on_my_chip(self, fn):
        chip = 2 * lax.axis_index("x") + lax.axis_index("y")
        for m in range(N_CHIPS):
            pl.when(chip == m)(functools.partial(fn, m))

    def start(self, ins, outs, sems):
        def go(m):
            local, sends, _ = self._copies(ins, outs, sems, m)
            for cp in local + sends:
                cp.start()
        self._on_my_chip(go)

    def wait(self, ins, outs, sems):
        def go(m):
            local, sends, recvs = self._copies(ins, outs, sems, m)
            for cp in recvs:
                cp.wait_recv()
            for cp in sends:
                cp.wait_send()
            for cp in local:
                cp.wait()
        self._on_my_chip(go)


def _half_rows(ref, cc):
    h = ref.shape[-2] // 2
    return ref.at[(slice(None),) * (len(ref.shape) - 2) + (pl.ds(cc * h, h), slice(None))]


class _Gather:
    def __init__(self, items, out_shapes):
        self.items, self.out_shapes = list(items), tuple(out_shapes)
        self.arrays = [it[0] for it in self.items]
        n = len(self.items)
        self.n_in, self.n_out = n, len(self.out_shapes)
        self.scratch = [pltpu.SemaphoreType.DMA((n * N_CHIPS,)) for _ in range(4)] + [pltpu.SemaphoreType.DMA((n,))]

    def _copies(self, ins, outs, sems, m, cc):
        ici_send, ici_recv, d2d_send, d2d_recv, local_sems = sems
        others = [j for j in range(N_CHIPS) if j != m]
        local, sends, arrivals, passed_on, from_sibling = [], [], [], [], []
        for a, (_, oi, src_of, dst_of) in enumerate(self.items):
            src, out = src_of(ins[a]), outs[oi]
            local.append(pltpu.make_async_copy(src, dst_of(out, m), local_sems.at[a]))
            for j in others:
                k = a * N_CHIPS + j
                mine_there = _half_rows(dst_of(out, m), cc)
                theirs_here = _half_rows(dst_of(out, j), cc)
                sends.append(pltpu.make_async_remote_copy(
                    src_ref=_half_rows(src, cc), dst_ref=mine_there, send_sem=ici_send.at[k],
                    recv_sem=ici_recv.at[a * N_CHIPS + m], device_id=(j // 2, j % 2, cc), device_id_type=MESH))
                arrivals.append(pltpu.make_async_remote_copy(
                    src_ref=_half_rows(src, cc), dst_ref=theirs_here, send_sem=ici_send.at[k], recv_sem=ici_recv.at[k],
                    device_id=(j // 2, j % 2, cc), device_id_type=MESH))
                passed_on.append(pltpu.make_async_remote_copy(
                    src_ref=theirs_here, dst_ref=theirs_here, send_sem=d2d_send.at[k], recv_sem=d2d_recv.at[k],
                    device_id=(m // 2, m % 2, 1 - cc), device_id_type=MESH))
                other_half = _half_rows(dst_of(out, j), 1 - cc)
                from_sibling.append(pltpu.make_async_remote_copy(
                    src_ref=other_half, dst_ref=other_half, send_sem=d2d_send.at[k], recv_sem=d2d_recv.at[k],
                    device_id=(m // 2, m % 2, 1 - cc), device_id_type=MESH))
        return local, sends, arrivals, passed_on, from_sibling

    def _on_my_core(self, fn):
        chip = 2 * lax.axis_index("x") + lax.axis_index("y")
        c = lax.axis_index("c")
        for m in range(N_CHIPS):
            for cc in range(2):
                pl.when((chip == m) & (c == cc))(functools.partial(fn, m, cc))

    def start(self, ins, outs, sems):
        def go(m, cc):
            local, sends, _, _, _ = self._copies(ins, outs, sems, m, cc)
            for cp in local + sends:
                cp.start()
        self._on_my_core(go)

    def wait(self, ins, outs, sems):
        def go(m, cc):
            local, sends, arrivals, passed_on, from_sibling = self._copies(ins, outs, sems, m, cc)
            for arrived, onward in zip(arrivals, passed_on):
                arrived.wait_recv()
                onward.start()
            for cp in from_sibling:
                cp.wait_recv()
            for cp in sends + passed_on:
                cp.wait_send()
            for cp in local:
                cp.wait()
        self._on_my_core(go)


def _run_exchange(ex, name):
    def body(*refs):
        ins, outs, sems = refs[:ex.n_in], refs[ex.n_in:ex.n_in + ex.n_out], refs[ex.n_in + ex.n_out:]
        ex.start(ins, outs, sems)
        ex.wait(ins, outs, sems)

    return pl.pallas_call(
        body, name=name, out_shape=ex.out_shapes, in_specs=[ANY] * ex.n_in, out_specs=(ANY,) * ex.n_out,
        scratch_shapes=ex.scratch, compiler_params=_params())(*ex.arrays)


def _sibling_swap(arrays, name, also):
    n = len(arrays)

    def body(*refs):
        ins, refs = refs[:n], refs[n:]
        x_ins, refs = refs[:also.n_in], refs[also.n_in:]
        outs, refs = refs[:n], refs[n:]
        x_outs, refs = refs[:also.n_out], refs[also.n_out:]
        send_sems, recv_sems, x_sems = refs[0], refs[1], refs[2:]
        peer = (lax.axis_index("x"), lax.axis_index("y"), 1 - lax.axis_index("c"))
        cps = [pltpu.make_async_remote_copy(src_ref=ins[a], dst_ref=outs[a], send_sem=send_sems.at[a],
                                            recv_sem=recv_sems.at[a], device_id=peer, device_id_type=MESH)
               for a in range(n)]
        also.start(x_ins, x_outs, x_sems)
        for cp in cps:
            cp.start()
        also.wait(x_ins, x_outs, x_sems)
        for cp in cps:
            cp.wait()

    return pl.pallas_call(
        body, name=name, out_shape=tuple(SDS(a.shape, a.dtype) for a in arrays) + also.out_shapes,
        in_specs=[ANY] * (n + also.n_in), out_specs=(ANY,) * (n + also.n_out),
        scratch_shapes=[pltpu.SemaphoreType.DMA((n,)), pltpu.SemaphoreType.DMA((n,))] + also.scratch,
        compiler_params=_params())(*arrays, *also.arrays)


def _block_diag(w):
    h, n, m = w.shape
    eye = jnp.eye(h, dtype=w.dtype)
    return (w[:, :, None, :] * eye[:, None, :, None]).reshape(h * n, h * m)


def _diag_blocks(d, h, col0=0, ncols=None, stacked=1):
    ncols = d.shape[1] - col0 if ncols is None else ncols
    n, m = d.shape[0] // (h * stacked), ncols // h
    lanes = 128
    assert m <= lanes and lanes % m == 0 and col0 % lanes == 0

    def body(d_ref, o_ref):
        for gi in range(h * stacked):
            c = col0 + (gi % h) * m
            chunk = d_ref[gi * n:(gi + 1) * n, c // lanes * lanes:c // lanes * lanes + lanes]
            o_ref[gi * n:(gi + 1) * n, :] = chunk[:, c % lanes:c % lanes + m]

    out = pl.pallas_call(body, name="diag_blocks", out_shape=SDS((stacked * h * n, m), d.dtype),
                         compiler_params=_params())(d)
    return out.reshape(stacked * h, n, m)


S5_CHUNKS = 4
S5_PER = S5_GROUPS // S5_CHUNKS
CH_W = S5_PER * S5_CH
ST_W = S5_PER * S5_STATE


def _bd_stack(mats):
    _, _, n, m = mats.shape
    eye = jnp.eye(S5_PER, dtype=mats.dtype)
    t = mats.reshape(2, S5_CHUNKS, S5_PER, n, m)
    bd = t[:, :, :, :, None, :] * eye[None, None, :, None, :, None]
    return bd.reshape(2 * S5_CHUNKS, S5_PER * n, S5_PER * m).astype(MXU_DTYPE)


def _chunks_chunked(src_ref, buf):
    pt = src_ref.shape[0]
    out = []
    for q in range(S5_CHUNKS):
        buf[q] = src_ref[:, q * CH_W:(q + 1) * CH_W]
        out.append(_load_chunked(buf.at[q], 0, pt).astype(MXU_DTYPE))
    return out


def _expand_into(dst_ref, chunks, w_ref):
    for b in range(2 * S5_CHUNKS):
        dst_ref[:, b * ST_W:(b + 1) * ST_W] = jnp.dot(chunks[b % S5_CHUNKS], w_ref[b], preferred_element_type=F32)


def _reduce_from(src_ref, w_ref, buf, dst_ref):
    pt = src_ref.shape[0]
    for q in range(S5_CHUNKS):
        y = jnp.dot(src_ref[:, q * ST_W:(q + 1) * ST_W].astype(MXU_DTYPE), w_ref[q], preferred_element_type=F32)
        p = S5_CHUNKS + q
        y = y + jnp.dot(src_ref[:, p * ST_W:(p + 1) * ST_W].astype(MXU_DTYPE), w_ref[p], preferred_element_type=F32)
        _store_natural(buf.at[q], 0, pt, y)
        dst_ref[:, q * CH_W:(q + 1) * CH_W] = buf[q]


def _s5_fwd(proj, w_bu, w_cx, a_row, d_skip, w_glu, b_glu):
    s = proj.shape[0]
    pt = _scan_tile(s)
    ch2 = 2 * S5_N

    def body(u_ref, dg_ref, wb_ref, wc_ref, a_ref, dk_ref, wg_ref, bg_ref, x_ref, y_ref, o_ref, carry, pw, buf):
        _expand_into(x_ref, _chunks_chunked(u_ref, buf), wb_ref)
        _scan_tile_in_place(a_ref, x_ref, carry, pw, reverse=False)
        _reduce_from(x_ref, wc_ref, buf, y_ref)
        g = jax.nn.gelu(y_ref[...] + dk_ref[...] * u_ref[...])
        t = jnp.dot(g.astype(MXU_DTYPE), wg_ref[...], preferred_element_type=F32) + bg_ref[...]
        o_ref[...] = (g * jax.nn.sigmoid(t) * _silu(dg_ref[...])).astype(MXU_DTYPE)

    return pl.pallas_call(
        body, name="s5_fwd", out_shape=(SDS((s, ch2), F32), SDS((s, BR), F32), SDS((s, BR), MXU_DTYPE)),
        grid=(s // pt,),
        in_specs=[_rows(pt, BR, CB_DU), _rows(pt, BR, CB_DG), _const(w_bu.shape), _const(w_cx.shape),
                  _const((1, ch2)), _const((1, BR)), _const((BR, BR)), _const((1, BR))],
        out_specs=(_rows(pt, ch2), _rows(pt, BR), _rows(pt, BR)),
        scratch_shapes=[pltpu.VMEM((1, ch2), F32), pltpu.VMEM((pt // 8, ch2), F32),
                        pltpu.VMEM((S5_CHUNKS, pt, CH_W), F32)],
        compiler_params=_params(1))(proj, proj, w_bu, w_cx, a_row, d_skip, w_glu, b_glu)


def _s5_core_bwd(dyl, proj, x, w_dx, w_du, a_row):
    s = proj.shape[0]
    pt = _scan_tile(s)
    nt = s // pt
    ch2 = 2 * S5_N
    ch = S5_N

    def body(dy_ref, u_ref, x_ref, xp_ref, wx_ref, wu_ref, a_ref, du_ref, da_ref, dwb_ref, dwc_ref,
             l_ref, carry, pw, buf, buf2):
        i = pl.program_id(0)
        _init_acc(da_ref, dwb_ref, dwc_ref)
        dy_c = _chunks_chunked(dy_ref, buf)
        u_c = _chunks_chunked(u_ref, buf2)
        _expand_into(l_ref, dy_c, wx_ref)
        _scan_tile_in_place(a_ref, l_ref, carry, pw, reverse=True)
        has_prev = (i < nt - 1).astype(F32)
        row = lax.broadcasted_iota(jnp.int32, (8, ch2), 0)
        first = jnp.where(row == 0, pltpu.roll(xp_ref[...], 1, 0) * has_prev, pltpu.roll(x_ref[pt - 8:pt, :], 1, 0))
        xprev = jnp.concatenate([first, x_ref[0:pt - 8, :]], axis=0)
        lr, li, xr, xi = l_ref[:, 0:ch], l_ref[:, ch:ch2], xprev[:, 0:ch], xprev[:, ch:ch2]
        da_ref[:, 0:ch] += _colsum(lr * xr + li * xi)
        da_ref[:, ch:ch2] += _colsum(li * xr - lr * xi)
        _reduce_from(l_ref, wu_ref, buf, du_ref)
        tn = (((0,), (0,)), ((), ()))
        for b in range(2 * S5_CHUNKS):
            cols, rows = slice(b * ST_W, (b + 1) * ST_W), slice(b * CH_W, (b + 1) * CH_W)
            dwb_ref[rows, :] += lax.dot_general(u_c[b % S5_CHUNKS], l_ref[:, cols].astype(MXU_DTYPE), tn,
                                                preferred_element_type=F32)
            dwc_ref[rows, :] += lax.dot_general(dy_c[b % S5_CHUNKS], x_ref[:, cols].astype(MXU_DTYPE), tn,
                                                preferred_element_type=F32)

    rev = lambda w, cb=0: pl.BlockSpec((pt, w), lambda i: (nt - 1 - i, cb))
    halo = pl.BlockSpec((8, ch2), lambda i: (jnp.maximum((nt - 1 - i) * (pt // 8) - 1, 0), 0))
    wshape = SDS((2 * S5_CHUNKS * CH_W, ST_W), F32)
    return pl.pallas_call(
        body, name="s5_core_bwd", out_shape=(SDS((s, BR), F32), SDS((1, ch2), F32), wshape, wshape), grid=(nt,),
        in_specs=[rev(BR, 0), rev(BR, CB_DU), rev(ch2), halo, _const(w_dx.shape), _const(w_du.shape),
                  _const((1, ch2))],
        out_specs=(rev(BR), _const((1, ch2)), _const(wshape.shape), _const(wshape.shape)),
        scratch_shapes=[pltpu.VMEM((pt, ch2), F32), pltpu.VMEM((1, ch2), F32), pltpu.VMEM((pt // 8, ch2), F32),
                        pltpu.VMEM((S5_CHUNKS, pt, CH_W), F32), pltpu.VMEM((S5_CHUNKS, pt, CH_W), F32)],
        compiler_params=_params(1))(dyl, proj, x, x, w_dx, w_du, a_row)


def _tiles(s):
    return dict(tb=min(512, s), tln=min(256, s))


def _layer_weights(p, l):
    pad8 = lambda w: jnp.pad(w, ((0, 8 - w.shape[0]), (0, 0)))
    return dict(
        conv_a=pad8(p["conv_a"][l]), conv_c=pad8(p["conv_c"][l]), conv_c_b=p["conv_c_b"][l][None],
        w_cat=jnp.concatenate([_block_diag(p["lru_wa"][l]), _block_diag(p["lru_wx"][l])], axis=1).astype(MXU_DTYPE),
        b_cat=jnp.concatenate([p["lru_ba"][l], p["lru_bx"][l]])[None], lam=p["lru_lambda"][l][None],
        lam_re=p["s5_lam_re"][l], lam_im=p["s5_lam_im"][l], log_dt=p["s5_log_dt"][l][:, None],
        b_re=p["s5_b_re"][l].reshape(S5_N, S5_CH), b_im=p["s5_b_im"][l].reshape(S5_N, S5_CH),
        c_re=p["s5_c_re"][l], c_im=p["s5_c_im"][l], d_skip=p["s5_d"][l][None], b_glu=p["s5_b_glu"][l][None],
        ln_g=p["ln_g"][l][None], ln_b=p["ln_b"][l][None])


def _s5_matrices(lw):
    ab_re, ab_im, f_re, f_im = _s5_disc_fwd(lw["lam_re"], lw["lam_im"], lw["log_dt"])
    f_re, f_im = f_re.reshape(S5_N, 1), f_im.reshape(S5_N, 1)
    bb_re, bb_im = _s5_bbar_fwd(f_re, f_im, lw["b_re"], lw["b_im"])
    bb = jnp.stack([bb_re, bb_im]).reshape(2, S5_GROUPS, S5_STATE, S5_CH)
    cc = jnp.stack([lw["c_re"], -lw["c_im"]])
    a_row = jnp.concatenate([ab_re.reshape(1, S5_N), ab_im.reshape(1, S5_N)], axis=1)
    return dict(f_re=f_re, f_im=f_im, a_row=a_row, w_bu=_bd_stack(jnp.swapaxes(bb, 2, 3)), w_du=_bd_stack(bb),
                w_cx=_bd_stack(jnp.swapaxes(cc, 2, 3)), w_dx=_bd_stack(cc))


def _mm_hooked(hook, *args, **kw):
    if hook is None:
        return _mm(*args, **kw)
    out = _mm(*args, carry=hook[0], **kw)
    hook[1](out[1:])
    return out[0]


def _layer_fwd(x, h, ada, w_in, get_rest, lw, s5m, bias_tabs, hooks=None, target=None, next_ada=None):
    s = x.shape[0]
    t = _tiles(s)
    tb = t["tb"]
    shift, scale, gate = ada
    hooks = hooks or {}
    if h is None:
        h = _modulate(x, scale, shift, tb)
    proj = _mm_hooked(hooks.get("in_proj"), h, w_in, name="in_proj", tm=1024, tn=1536, tk=D_MODEL)
    w_out, w_glu = get_rest()
    y_a = _branch_a_fwd(proj, lw["conv_a"], tb)
    os_, lses = [], []
    for g, (_, dil) in enumerate(DILATIONS):
        o, lse = _attn_fwd(proj, bias_tabs[g], dil)
        os_.append(o)
        lses.append(lse)
    y_b = _attn_combine(os_, lses, proj, tb)
    lru_a, lru_b = _lru_gates_fwd(proj, lw["conv_c"], lw["conv_c_b"], lw["w_cat"], lw["b_cat"], lw["lam"], tb)
    lru_h, y_c = _lru_scan_fwd(lru_a, lru_b, proj, tb)
    s5_x, ylin, y_d = _s5_fwd(proj, s5m["w_bu"], s5m["w_cx"], s5m["a_row"], lw["d_skip"], w_glu, lw["b_glu"])
    ycat = jnp.concatenate([y_a, y_b, y_c, y_d], axis=1)
    saved = dict(x=x, h=h, proj=proj, os=os_, lses=lses, lru_a=lru_a, lru_h=lru_h, s5_x=s5_x, ylin=ylin, ycat=ycat)
    if target is not None:
        loss, *saved["head"] = _out_ln_loss(ycat, w_out, x, gate, lw["ln_g"], lw["ln_b"], target, t["tln"])
        return loss, None, saved
    x_next, saved["xhat"], saved["y"], saved["rstd"], h_next = _out_ln(
        ycat, w_out, x, gate, lw["ln_g"], lw["ln_b"], next_ada[1], next_ada[0], t["tln"])
    return x_next, h_next, saved


def _layer_bwd(dxn, sv, ada, w_in, w_out, w_glu, lw, s5m, bias_tabs, head_ones, hooks=None):
    proj = sv["proj"]
    s = proj.shape[0]
    t = _tiles(s)
    tb = t["tb"]
    shift, scale, gate = ada
    g = {}
    hook = lambda name: hooks[name](g) if hooks and name in hooks else None
    if "head" in sv:
        dyb, dxa, g["ln_g"], g["ln_b"], dgate = sv["head"]
        dycat = _mm(dyb, w_out, name="dycat", tb=True, tm=1024, tn=1024, tk=D_MODEL)
    else:
        dyb, dxa, g["ln_g"], g["ln_b"], dgate, dycat = _ln_bwd(dxn, sv["xhat"], sv["y"], sv["rstd"], lw["ln_g"], gate,
                                                               w_out, t["tln"])
    g["w_out"] = _mm_hooked(hook("dw_out"), sv["ycat"], dyb, name="dw_out", ta=True, out_dtype=WIRE_DTYPE,
                            tm=1024, tn=1024, tk=2048)
    da, dconv_a = _branch_a_bwd(dycat, proj, lw["conv_a"], tb)
    g["conv_a"] = dconv_a[0:3]
    pre = _attn_bwd_pre(dycat, sv["os"], sv["lses"], proj, head_ones, tb)
    dbg, dos, dms = pre[0], pre[1:4], pre[4:7]
    dqkv, dbias = [], []
    for gi, (_, dil) in enumerate(DILATIONS):
        hk = hook(f"attn_bwd_d{dil}")
        dq, dk, dv, dbi, *got = _attn_bwd(proj, dos[gi], sv["lses"][gi], dms[gi], bias_tabs[gi], dil,
                                          carry=hk and hk[0])
        if hk:
            hk[1](got)
        dqkv.append((dq, dk, dv))
        dbias.append(dbi)
    dqkv = list(zip(*dqkv))
    lmb, dcg = _lru_scan_bwd(sv["lru_a"], dycat, sv["lru_h"], proj, tb)
    dxc, dpre, xcb, dbcat, dlam = _lru_gates_bwd(proj, lmb, sv["lru_h"], lw["conv_c"], lw["conv_c_b"], lw["w_cat"],
                                                  lw["b_cat"], lw["lam"], tb)
    dwcat = _mm(xcb, dpre, name="dw_lru", ta=True, tn=1024)
    g["lru_wa"] = _diag_blocks(dwcat, LRU_HEADS, 0, BR)
    g["lru_wx"] = _diag_blocks(dwcat, LRU_HEADS, BR, BR)
    g["lru_ba"], g["lru_bx"], g["lru_lambda"] = dbcat[0, 0:BR], dbcat[0, BR:2 * BR], dlam[0]
    dcx, dconv_c, dccb = _conv_c_bwd(dxc, proj, lw["conv_c"], tb)
    g["conv_c"], g["conv_c_b"] = dconv_c[0:4], dccb[0]
    dyl, dus, ddg, gb, dtb, ddk, dbglu = _s5_tail_bwd(dycat, sv["ylin"], proj, lw["d_skip"], w_glu, lw["b_glu"], tb)
    g["s5_d"], g["s5_b_glu"] = ddk[0], dbglu[0]
    g["s5_w_glu"] = _mm(gb, dtb, name="dw_glu", ta=True, out_dtype=WIRE_DTYPE)
    du, dab, dwb8, dwc8 = _s5_core_bwd(dyl, proj, sv["s5_x"], s5m["w_dx"], s5m["w_du"], s5m["a_row"])
    per_group = lambda d8: _diag_blocks(d8, S5_PER, stacked=2 * S5_CHUNKS).reshape(2, S5_GROUPS, S5_CH, S5_STATE)
    dbb, dcc = per_group(dwb8), per_group(dwc8)
    from_bd = lambda half: jnp.swapaxes(dbb[half], 1, 2).reshape(S5_N, S5_CH)
    df_re, df_im, db_re, db_im = _s5_bbar_bwd(s5m["f_re"], s5m["f_im"], lw["b_re"], lw["b_im"],
                                              from_bd(0), from_bd(1))
    shp = (S5_GROUPS, S5_STATE)
    g["s5_lam_re"], g["s5_lam_im"], dlog_dt = _s5_disc_bwd(
        lw["lam_re"], lw["lam_im"], lw["log_dt"],
        (dab[:, 0:S5_N].reshape(shp), dab[:, S5_N:].reshape(shp), df_re.reshape(shp), df_im.reshape(shp)))
    g["s5_log_dt"] = dlog_dt[:, 0]
    g["s5_b_re"] = db_re.reshape(S5_GROUPS, S5_STATE, S5_CH)
    g["s5_b_im"] = db_im.reshape(S5_GROUPS, S5_STATE, S5_CH)
    g["s5_c_re"], g["s5_c_im"] = dcc[0], -dcc[1]
    dproj = _assemble_dproj(da, dqkv, dbg, dcx, dcg, du, dus, ddg, tb)
    g["w_in"] = _mm_hooked(hook("dw_in"), sv["h"], dproj, name="dw_in", ta=True, out_dtype=WIRE_DTYPE,
                           tm=1024, tn=1536, tk=2048)
    hk = hook("dh")
    dx, dshift, dscale, *got = _dh_mod_bwd(dproj, w_in, dxa, sv["x"], scale, carry=hk and hk[0])
    if hk:
        hk[1](got)
    g["ada"] = jnp.concatenate([dshift[0], dscale[0], dgate[0]])
    return dx, g, dbias


SMALL = ("rel_bias", "conv_a", "conv_c", "conv_c_b", "lru_wa", "lru_ba", "lru_wx", "lru_bx", "lru_lambda",
         "s5_lam_re", "s5_lam_im", "s5_log_dt", "s5_b_re", "s5_b_im", "s5_c_re", "s5_c_im", "s5_d", "s5_b_glu",
         "ln_g", "ln_b")
PER_LAYER_SMALL = SMALL[1:]


def _local_step(x, target, ada_rows, w_in, w_out, w_glu, p, comm=None):
    if comm is None:
        get_w_in = lambda l: w_in[l]
        get_rest = lambda l: (w_out[l], w_glu[l])
        fwd_hooks = bwd_hooks = lambda *_: None
    else:
        get_w_in, get_rest, fwd_hooks, bwd_hooks = comm.w_in, comm.rest, comm.fwd_hooks, comm.bwd_hooks
    s = x.shape[0]
    buckets = _bucket_maps()
    bias_tabs = _bias_tables(p["rel_bias"], buckets)
    head_ones = _block_diag(jnp.ones((ATT_HEADS, HEAD_DIM, HEAD_DIM), MXU_DTYPE))
    lws = [_layer_weights(p, l) for l in range(DEPTH)]
    s5ms = [_s5_matrices(lw) for lw in lws]
    adas = [tuple(ada_rows[l, k * D_MODEL:(k + 1) * D_MODEL][None] for k in range(3)) for l in range(DEPTH)]
    saved, h = [], None
    for l in range(DEPTH):
        last = l == DEPTH - 1
        x, h, sv = _layer_fwd(x, h, adas[l], get_w_in(l), functools.partial(get_rest, l), lws[l], s5ms[l], bias_tabs,
                              fwd_hooks(l), target if last else None, None if last else adas[l + 1])
        saved.append(sv)
    loss, dx = x, None
    grads = [None] * DEPTH
    dbias_sum = []
    for l in reversed(range(DEPTH)):
        dx, grads[l], dbias = _layer_bwd(dx, saved[l], adas[l], get_w_in(l), *get_rest(l), lws[l], s5ms[l],
                                         bias_tabs, head_ones, bwd_hooks(l, grads))
        dbias_sum.append(jnp.stack(dbias))
    drel = _rel_bias_grad(jnp.stack(dbias_sum), buckets)[:, 0:ATT_HEADS]
    small = {n: jnp.stack([grads[l][n] for l in range(DEPTH)]) for n in PER_LAYER_SMALL + ("ada",)}
    small["rel_bias"] = drel
    big = {n: [grads[l][n] for l in range(DEPTH)] for n in ("w_in", "w_out", "s5_w_glu")}
    return loss, dx, big, small


PACK_ROWS = 256


def _pack(parts):
    flat = jnp.concatenate([t.reshape(-1).astype(F32) for t in parts])
    n = flat.shape[0]
    rows = -(-n // (PACK_ROWS * 128)) * PACK_ROWS
    return jnp.pad(flat, (0, rows * 128 - n)).reshape(rows, 128)


def _unpack(packed, shapes):
    flat = packed.reshape(packed.shape[:-2] + (-1,))
    out, off = [], 0
    for shp in shapes:
        size = math.prod(shp)
        out.append(flat[..., off:off + size].reshape(flat.shape[:-1] + tuple(shp)))
        off += size
    return out


def _take_cols(t, chip, width):
    return lax.dynamic_slice_in_dim(t, chip * width, width, axis=t.ndim - 1)


class _Comm:
    IN_W, OUT_R, GLU_R = N_IN // N_CHIPS, D_MODEL // N_CHIPS, BR // N_CHIPS

    def __init__(self, w_in_b, w_out_b, w_glu_b):
        assert DEPTH == 2
        self.shards = (w_in_b, w_out_b, w_glu_b)
        in_w = self.IN_W
        self.w_in_full = {0: _run_exchange(_Gather(
            [(w_in_b, 0, lambda ref: ref.at[0], lambda ref, j: ref.at[:, pl.ds(j * in_w, in_w)])],
            [SDS((D_MODEL, N_IN), WIRE_DTYPE)]), "gather_w_in0")[0]}
        self.w_out_full = self.w_glu_full = None
        self.recv = {}

    def w_in(self, l):
        return self.w_in_full[l]

    def rest(self, l):
        return self.w_out_full[l], self.w_glu_full[l]

    def fwd_hooks(self, l):
        if l != 0:
            return None
        w_in_b, w_out_b, w_glu_b = self.shards
        in_w, out_r, glu_r = self.IN_W, self.OUT_R, self.GLU_R
        whole = lambda ref: ref
        items = [(w_out_b, 0, whole, lambda ref, j: ref.at[:, pl.ds(j * out_r, out_r), :]),
                 (w_glu_b, 1, whole, lambda ref, j: ref.at[:, pl.ds(j * glu_r, glu_r), :]),
                 (w_in_b, 2, lambda ref: ref.at[1], lambda ref, j: ref.at[:, pl.ds(j * in_w, in_w)])]
        shapes = [SDS((DEPTH, D_MODEL, D_MODEL), WIRE_DTYPE), SDS((DEPTH, BR, BR), WIRE_DTYPE),
                  SDS((D_MODEL, N_IN), WIRE_DTYPE)]

        def done(got):
            self.w_out_full, self.w_glu_full, self.w_in_full[1] = got

        return {"in_proj": (_Gather(items, shapes), done)}

    W_IN_ROWS = ((0, 1024), (1024, 512), (1536, 512))

    def _scatter(self, parts):
        in_w, out_r, glu_r = self.IN_W, self.OUT_R, self.GLU_R
        items, shapes, keys = [], [], []
        for oi, (name, l, arr, *rows) in enumerate(parts):
            if name == "w_in":
                r0, nr = rows[0] if rows else (0, D_MODEL)
                cut = functools.partial(lambda ref, j, r0, nr: ref.at[pl.ds(r0, nr), pl.ds(j * in_w, in_w)], r0=r0, nr=nr)
                shard = (nr, in_w)
            elif name == "w_out":
                cut, shard = (lambda ref, j: ref.at[pl.ds(j * out_r, out_r), :]), (out_r, D_MODEL)
            else:
                cut, shard = (lambda ref, j: ref.at[pl.ds(j * glu_r, glu_r), :]), (glu_r, BR)
            items.append((arr, oi, cut, lambda ref, j: ref.at[j]))
            shapes.append(SDS((N_CHIPS,) + shard, WIRE_DTYPE))
            keys.append((name, l) + ((rows[0][0],) if rows else ()))

        def done(got):
            self.recv.update(zip(keys, got))

        return _Exchange(items, shapes), done

    def received(self, name):
        return [self.recv[k] for k in sorted(k for k in self.recv if k[0] == name)]

    def bwd_hooks(self, l, grads):
        if l != 0:
            return None
        g1 = grads[1]
        w_in_part = lambda k: (lambda g: self._scatter([("w_in", 1, g1["w_in"], self.W_IN_ROWS[k])]))
        return {"dw_out": lambda g: self._scatter([("w_out", 1, g1["w_out"]), ("s5_w_glu", 1, g1["s5_w_glu"])]),
                "attn_bwd_d16": w_in_part(0), "attn_bwd_d4": w_in_part(1), "attn_bwd_d1": w_in_part(2),
                "dw_in": lambda g: self._scatter([("w_out", 0, g["w_out"]), ("s5_w_glu", 0, g["s5_w_glu"])]),
                "dh": lambda g: self._scatter([("w_in", 0, g["w_in"])])}


def kernel(x, c, rel_bias, w_ada, b_ada, w_in, conv_a, conv_c, conv_c_b, lru_wa, lru_ba, lru_wx, lru_bx, lru_lambda, s5_lam_re, s5_lam_im, s5_log_dt, s5_b_re, s5_b_im, s5_c_re, s5_c_im, s5_d, s5_w_glu, s5_b_glu, w_out, ln_g, ln_b, loss_target, m_rel_bias, m_w_ada, m_b_ada, m_w_in, m_conv_a, m_conv_c, m_conv_c_b, m_lru_wa, m_lru_ba, m_lru_wx, m_lru_bx, m_lru_lambda, m_s5_lam_re, m_s5_lam_im, m_s5_log_dt, m_s5_b_re, m_s5_b_im, m_s5_c_re, m_s5_c_im, m_s5_d, m_s5_w_glu, m_s5_b_glu, m_w_out, m_ln_g, m_ln_b, v_rel_bias, v_w_ada, v_b_ada, v_w_in, v_conv_a, v_conv_c, v_conv_c_b, v_lru_wa, v_lru_ba, v_lru_wx, v_lru_bx, v_lru_lambda, v_s5_lam_re, v_s5_lam_im, v_s5_log_dt, v_s5_b_re, v_s5_b_im, v_s5_c_re, v_s5_c_im, v_s5_d, v_s5_w_glu, v_s5_b_glu, v_w_out, v_ln_g, v_ln_b):
    args = dict(locals())
    names = ("rel_bias", "w_ada", "b_ada", "w_in", "conv_a", "conv_c", "conv_c_b", "lru_wa", "lru_ba", "lru_wx",
             "lru_bx", "lru_lambda", "s5_lam_re", "s5_lam_im", "s5_log_dt", "s5_b_re", "s5_b_im", "s5_c_re", "s5_c_im",
             "s5_d", "s5_w_glu", "s5_b_glu", "w_out", "ln_g", "ln_b")
    w = {n: args[n] for n in names}
    mom = {n: args["m_" + n] for n in names}
    var = {n: args["v_" + n] for n in names}
    chip = 2 * lax.axis_index("x") + lax.axis_index("y")
    me = 2 * chip + lax.axis_index("c")
    ada_w = 3 * D_MODEL // N_CHIPS
    conv_w = BR // N_CHIPS

    comm = _Comm(w["w_in"].astype(WIRE_DTYPE), w["w_out"].astype(WIRE_DTYPE), w["s5_w_glu"].astype(WIRE_DTYPE))

    taps = jnp.concatenate([w["conv_a"].reshape(DEPTH * 3, conv_w), w["conv_c"].reshape(DEPTH * 4, conv_w)])
    first = jnp.concatenate([c, jnp.pad(taps, ((0, 1), (0, D_MODEL - conv_w)))])
    got = _allgather8(first, "gather_c_taps").reshape(N_CHIPS, 2, 16, D_MODEL)
    c_all = got[:, :, 0].reshape(N_DEV, D_MODEL)
    taps_all = jnp.transpose(got[:, 0, 1:1 + DEPTH * 7, 0:conv_w], (1, 0, 2)).reshape(DEPTH * 7, BR)
    conv_a_f = taps_all[0:DEPTH * 3].reshape(DEPTH, 3, BR)
    conv_c_f = taps_all[DEPTH * 3:].reshape(DEPTH, 4, BR)

    cond_all = _silu_rows(c_all)
    ada_part = jnp.stack([_mm(cond_all, w["w_ada"][l], name="ada_fwd", tk=D_MODEL, tn=512,
                              bias=_take_cols(w["b_ada"][l][None], chip, ada_w)) for l in range(DEPTH)])
    ada_all = _allgather8(ada_part.reshape(DEPTH * N_DEV, ada_w), "gather_ada")
    ada_all = ada_all.reshape(N_CHIPS, 2, DEPTH, N_DEV, ada_w)[:, 0]
    ada_rows = lax.dynamic_index_in_dim(ada_all, me, axis=2, keepdims=False)
    ada_rows = jnp.transpose(ada_rows, (1, 0, 2)).reshape(DEPTH, 3 * D_MODEL)

    p = dict(w)
    p["conv_a"], p["conv_c"] = conv_a_f, conv_c_f
    loss, dx, _, small = _local_step(x[0], loss_target[0], ada_rows, None, None, None, p, comm)

    sums = [_sum_leading(comm.received(name), 256, "sum_chips") for name in ("w_in", "w_out", "s5_w_glu")]
    small_names = SMALL + ("ada",)
    small["loss"] = loss
    order = small_names + ("loss",)
    shapes = [small[n].shape for n in order]
    *others, gathered = _sibling_swap(sums, "swap_cores", _AllGather8(_pack([small[n] for n in order])))
    out = {}
    for name, mine, other in zip(("w_in", "w_out", "s5_w_glu"), sums, others):
        shp = w[name].shape
        flat = lambda t: t.reshape(-1, shp[-1])
        res = _adamw(flat(w[name]), [mine, other], flat(mom[name]), flat(var[name]), 128, "adamw_big")
        out[name] = [t.reshape(shp) for t in res]
    gathered = gathered.reshape(N_DEV, -1, 128)
    total = dict(zip(order, _unpack(_sum_leading([gathered], PACK_ROWS, "sum_devices"), shapes)))
    d_ada_all = _unpack(gathered, shapes)[order.index("ada")]
    g_small = {n: total[n] for n in SMALL}
    g_small["conv_a"] = _take_cols(total["conv_a"], chip, conv_w)
    g_small["conv_c"] = _take_cols(total["conv_c"], chip, conv_w)
    g_small["b_ada"] = total["ada"]
    g_w_ada = jnp.stack([_mm(cond_all, _take_cols(d_ada_all[:, l], chip, ada_w), name="dw_ada", ta=True, tn=ada_w)
                         for l in range(DEPTH)])
    upd_names = SMALL + ("b_ada",)
    upd_shapes = [w[n].shape for n in upd_names]
    res = _adamw(_pack([w[n] for n in upd_names]), [_pack([g_small[n] for n in upd_names])],
                 _pack([mom[n] for n in upd_names]), _pack([var[n] for n in upd_names]), PACK_ROWS, "adamw_small")
    for k, t in enumerate(res):
        for n, val in zip(upd_names, _unpack(t, upd_shapes)):
            out.setdefault(n, [None] * 4)[k] = val
    shp = w["w_ada"].shape
    flat = lambda t: t.reshape(-1, shp[-1])
    out["w_ada"] = [t.reshape(shp) for t in _adamw(flat(w["w_ada"]), [flat(g_w_ada)], flat(mom["w_ada"]),
                                                  flat(var["w_ada"]), 128, "adamw_ada")]
    return (total["loss"].reshape(()), dx[None]) + tuple(out[n][k] for k in range(4) for n in names)
```

```python
import functools
import math

import jax
import jax.numpy as jnp
from jax import lax
from jax.experimental import pallas as pl
from jax.experimental.pallas import tpu as pltpu

F32 = jnp.float32
MXU_DTYPE = jnp.bfloat16
WIRE_DTYPE = jnp.bfloat16
SDS = jax.ShapeDtypeStruct
MESH = pl.DeviceIdType.MESH
ANY = pl.BlockSpec(memory_space=pl.ANY)
VMEM_LIMIT = 48 * 1024 * 1024

D_MODEL = 2048
DEPTH = 2
BR = 512
ATT_HEADS = 8
HEAD_DIM = 64
DILATIONS = ((128, 1), (512, 4), (2048, 16))
BLK = 128
REL_BUCKETS = 32
REL_MAX_DIST = 2048
LRU_HEADS = 8
LRU_C = 8.0
S5_CH = 16
S5_GROUPS = 32
S5_STATE = 64
S5_N = S5_GROUPS * S5_STATE
N_IN = 12 * BR
ALPHA = (2 * DEPTH) ** 0.25
LN_EPS = 1e-5
NEG = -1e30
ADAM_LR, ADAM_B1, ADAM_B2, ADAM_EPS, ADAM_WD, ADAM_STEP = 0.001, 0.9, 0.999, 1e-08, 0.01, 10
CB_AB, CB_AC, CB_AX, CB_AG, CB_Q, CB_K, CB_V, CB_BG, CB_CX, CB_CG, CB_DU, CB_DG = range(12)
N_CHIPS = 4
N_DEV = 8


def _params(n_axes=0):
    kw = {"dimension_semantics": ("arbitrary",) * n_axes} if n_axes else {}
    return pltpu.CompilerParams(vmem_limit_bytes=VMEM_LIMIT, **kw)


def _rows(tb, w, cb=0):
    return pl.BlockSpec((tb, w), lambda i: (i, cb))


def _prev8(tb, w, cb=0):
    return pl.BlockSpec((8, w), lambda i: (jnp.maximum(i * (tb // 8) - 1, 0), cb))


def _next8(tb, w, n_rows, cb=0):
    return pl.BlockSpec((8, w), lambda i: (jnp.minimum((i + 1) * (tb // 8), n_rows // 8 - 1), cb))


def _const(shape):
    return pl.BlockSpec(shape, lambda *_: (0,) * len(shape))


def _silu(x):
    return x * jax.nn.sigmoid(x)


def _dsilu(x):
    s = jax.nn.sigmoid(x)
    return s * (1.0 + x * (1.0 - s))


def _shift_down(cur, prev8, j):
    rolled = pltpu.roll(cur, j, 0)
    row = lax.broadcasted_iota(jnp.int32, (8, cur.shape[1]), 0)
    first = jnp.where(row < j, pltpu.roll(prev8, j, 0), rolled[0:8])
    return jnp.concatenate([first, rolled[8:]], axis=0)


def _shift_up(cur, next8, j):
    t = cur.shape[0]
    rolled = pltpu.roll(cur, t - j, 0)
    row = lax.broadcasted_iota(jnp.int32, (8, cur.shape[1]), 0)
    last = jnp.where(row >= 8 - j, pltpu.roll(next8, 8 - j, 0), rolled[t - 8:t])
    return jnp.concatenate([rolled[:t - 8], last], axis=0)


def _colsum(x):
    return jnp.sum(x, axis=0, keepdims=True)


def _init_acc(*refs):
    @pl.when(pl.program_id(0) == 0)
    def _():
        for r in refs:
            r[...] = jnp.zeros_like(r)


def _call(body, *, name, out_shape, grid, in_specs, out_specs, scratch_shapes, args, carry=None):
    out_shape, out_specs, in_specs = tuple(out_shape), tuple(out_specs), list(in_specs)
    scratch_shapes = list(scratch_shapes)
    if carry is None:
        return pl.pallas_call(body, name=name, out_shape=out_shape, grid=grid, in_specs=in_specs, out_specs=out_specs,
                              scratch_shapes=scratch_shapes, compiler_params=_params(len(grid)))(*args)
    n_in, n_out, n_scr = len(in_specs), len(out_shape), len(scratch_shapes)

    def wrapped(*refs):
        ins, refs = refs[:n_in], refs[n_in:]
        x_ins, refs = refs[:carry.n_in], refs[carry.n_in:]
        outs, refs = refs[:n_out], refs[n_out:]
        x_outs, refs = refs[:carry.n_out], refs[carry.n_out:]
        scr, x_sems = refs[:n_scr], refs[n_scr:]
        at = [pl.program_id(d) for d in range(len(grid))]
        first = functools.reduce(lambda p, q: p & q, [i == 0 for i in at])
        last = functools.reduce(lambda p, q: p & q, [i == g - 1 for i, g in zip(at, grid)])
        pl.when(first)(lambda: carry.start(x_ins, x_outs, x_sems))
        body(*ins, *outs, *scr)
        pl.when(last)(lambda: carry.wait(x_ins, x_outs, x_sems))

    return pl.pallas_call(
        wrapped, name=name, out_shape=out_shape + carry.out_shapes, grid=grid, in_specs=in_specs + [ANY] * carry.n_in,
        out_specs=out_specs + (ANY,) * carry.n_out, scratch_shapes=scratch_shapes + carry.scratch,
        compiler_params=_params(len(grid)))(*args, *carry.arrays)


def _mm(a, b, *, name, ta=False, tb=False, out_dtype=F32, tm=512, tn=512, tk=512, bias=None, carry=None):
    m, k = (a.shape[1], a.shape[0]) if ta else a.shape
    n = b.shape[0] if tb else b.shape[1]
    assert k == (b.shape[1] if tb else b.shape[0]), (name, a.shape, b.shape)
    tm, tn, tk = min(tm, m), min(tn, n), min(tk, k)
    nk = k // tk
    assert m % tm == 0 and n % tn == 0 and k % tk == 0, (name, m, n, k)

    def body(*refs):
        if bias is None:
            a_ref, b_ref, o_ref, acc = refs
        else:
            a_ref, b_ref, bias_ref, o_ref, acc = refs
        kk = pl.program_id(2)

        @pl.when(kk == 0)
        def _():
            acc[...] = jnp.zeros_like(acc)

        dims = (((0 if ta else 1,), (1 if tb else 0,)), ((), ()))
        acc[...] += lax.dot_general(a_ref[...].astype(MXU_DTYPE), b_ref[...].astype(MXU_DTYPE), dims,
                                    preferred_element_type=F32)

        @pl.when(kk == nk - 1)
        def _():
            r = acc[...]
            if bias is not None:
                r = r + bias_ref[...]
            o_ref[...] = r.astype(out_dtype)

    a_spec = (pl.BlockSpec((tk, tm), lambda i, j, kk: (kk, i)) if ta
              else pl.BlockSpec((tm, tk), lambda i, j, kk: (i, kk)))
    b_spec = (pl.BlockSpec((tn, tk), lambda i, j, kk: (j, kk)) if tb
              else pl.BlockSpec((tk, tn), lambda i, j, kk: (kk, j)))
    in_specs, args = [a_spec, b_spec], [a, b]
    if bias is not None:
        in_specs.append(pl.BlockSpec((1, tn), lambda i, j, kk: (0, j)))
        args.append(bias)
    out = _call(body, name=name, out_shape=[SDS((m, n), out_dtype)], grid=(m // tm, n // tn, nk), in_specs=in_specs,
                out_specs=[pl.BlockSpec((tm, tn), lambda i, j, kk: (i, j))],
                scratch_shapes=[pltpu.VMEM((tm, tn), F32)], args=args, carry=carry)
    return out[0] if carry is None else out


def _silu_rows(c_all):
    def body(c_ref, o_ref):
        o_ref[...] = _silu(c_ref[...])
    return pl.pallas_call(body, name="cond_silu", out_shape=SDS(c_all.shape, F32))(c_all)


def _modulate(x, scale, shift, tb):
    s, d = x.shape

    def body(x_ref, sc_ref, sh_ref, o_ref):
        o_ref[...] = (x_ref[...] * (1.0 + sc_ref[...]) + sh_ref[...]).astype(MXU_DTYPE)

    return pl.pallas_call(body, name="modulate", out_shape=SDS((s, d), MXU_DTYPE), grid=(s // tb,),
                          in_specs=[_rows(tb, d), _const((1, d)), _const((1, d))], out_specs=_rows(tb, d),
                          compiler_params=_params(1))(x, scale, shift)


def _out_ln(ycat, w_out, x, gate, ln_g, ln_b, next_scale, next_shift, tb):
    s, d = x.shape

    def body(yc_ref, w_ref, x_ref, gt_ref, g_ref, b_ref, sc_ref, sh_ref, xn_ref, xh_ref, y_ref, rs_ref, hn_ref):
        y = jnp.dot(yc_ref[...], w_ref[...], preferred_element_type=F32)
        res = ALPHA * x_ref[...] + (1.0 + gt_ref[...]) * y
        mu = jnp.mean(res, axis=-1, keepdims=True)
        cen = res - mu
        var = jnp.mean(cen * cen, axis=-1, keepdims=True)
        rstd = lax.rsqrt(var + LN_EPS)
        xhat = cen * rstd
        xn = xhat * g_ref[...] + b_ref[...]
        xn_ref[...] = xn
        xh_ref[...] = xhat
        y_ref[...] = y
        rs_ref[...] = rstd
        hn_ref[...] = (xn * (1.0 + sc_ref[...]) + sh_ref[...]).astype(MXU_DTYPE)

    big = SDS((s, d), F32)
    return pl.pallas_call(
        body, name="out_proj_ln", out_shape=(big, big, big, SDS((s, 1), F32), SDS((s, d), MXU_DTYPE)), grid=(s // tb,),
        in_specs=[_rows(tb, d), pl.BlockSpec((d, d), lambda i: (0, 0), pipeline_mode=pl.Buffered(1)), _rows(tb, d)]
        + [_const((1, d))] * 5,
        out_specs=(_rows(tb, d), _rows(tb, d), _rows(tb, d), _rows(tb, 1), _rows(tb, d)), compiler_params=_params(1),
    )(ycat, w_out, x, gate, ln_g, ln_b, next_scale, next_shift)


def _ln_bwd(dxn, xhat, y, rstd, ln_g, gate, w_out, tb):
    s, d = dxn.shape

    def body(dxn_ref, xh_ref, y_ref, rs_ref, g_ref, gt_ref, w_ref, dy_ref, dxa_ref, dg_ref, db_ref, dgt_ref, dyc_ref):
        _init_acc(dg_ref, db_ref, dgt_ref)
        dxn_t, xh = dxn_ref[...], xh_ref[...]
        dxh = dxn_t * g_ref[...]
        dres = rs_ref[...] * (dxh - jnp.mean(dxh, axis=-1, keepdims=True)
                              - xh * jnp.mean(dxh * xh, axis=-1, keepdims=True))
        dyb = ((1.0 + gt_ref[...]) * dres).astype(MXU_DTYPE)
        dy_ref[...] = dyb
        dxa_ref[...] = ALPHA * dres
        dg_ref[...] += _colsum(dxn_t * xh)
        db_ref[...] += _colsum(dxn_t)
        dgt_ref[...] += _colsum(dres * y_ref[...])
        dyc_ref[...] = lax.dot_general(dyb, w_ref[...], (((1,), (1,)), ((), ())), preferred_element_type=F32)

    vec = SDS((1, d), F32)
    return pl.pallas_call(
        body, name="ln_bwd_dycat", out_shape=(SDS((s, d), MXU_DTYPE), SDS((s, d), F32), vec, vec, vec, SDS((s, d), F32)),
        grid=(s // tb,),
        in_specs=[_rows(tb, d), _rows(tb, d), _rows(tb, d), _rows(tb, 1), _const((1, d)), _const((1, d)),
                  pl.BlockSpec((d, d), lambda i: (0, 0), pipeline_mode=pl.Buffered(1))],
        out_specs=(_rows(tb, d), _rows(tb, d), _const((1, d)), _const((1, d)), _const((1, d)), _rows(tb, d)),
        compiler_params=_params(1))(dxn, xhat, y, rstd, ln_g, gate, w_out)


def _dh_mod_bwd(dproj, w_in, dxa, x, scale, carry=None):
    s, d = dxa.shape
    k = dproj.shape[1]
    tm, tn, tk = min(1024, s), 1024, 1536
    nk = k // tk
    assert s % tm == 0 and d % tn == 0 and k % tk == 0

    def body(a_ref, b_ref, dxa_ref, x_ref, sc_ref, dx_ref, dsh_ref, dsc_ref, acc):
        i, kk = pl.program_id(1), pl.program_id(2)

        @pl.when(kk == 0)
        def _():
            acc[...] = jnp.zeros_like(acc)

        @pl.when((kk == 0) & (i == 0))
        def _():
            dsh_ref[...] = jnp.zeros_like(dsh_ref)
            dsc_ref[...] = jnp.zeros_like(dsc_ref)

        acc[...] += lax.dot_general(a_ref[...], b_ref[...], (((1,), (1,)), ((), ())), preferred_element_type=F32)

        @pl.when(kk == nk - 1)
        def _():
            dh_t = acc[...]
            dx_ref[...] = dxa_ref[...] + dh_t * (1.0 + sc_ref[...])
            dsh_ref[...] += _colsum(dh_t)
            dsc_ref[...] += _colsum(dh_t * x_ref[...])

    tile = pl.BlockSpec((tm, tn), lambda j, i, kk: (i, j))
    vec = pl.BlockSpec((1, tn), lambda j, i, kk: (0, j))
    return _call(
        body, name="dh", out_shape=(SDS((s, d), F32), SDS((1, d), F32), SDS((1, d), F32)),
        grid=(d // tn, s // tm, nk),
        in_specs=[pl.BlockSpec((tm, tk), lambda j, i, kk: (i, kk)), pl.BlockSpec((tn, tk), lambda j, i, kk: (j, kk)),
                  tile, tile, vec],
        out_specs=(tile, vec, vec), scratch_shapes=[pltpu.VMEM((tm, tn), F32)],
        args=(dproj, w_in, dxa, x, scale), carry=carry)


def _out_ln_loss(ycat, w_out, x, gate, ln_g, ln_b, target, tb):
    s, d = x.shape

    def body(yc_ref, w_ref, x_ref, gt_ref, g_ref, b_ref, t_ref, l_ref, dy_ref, dxa_ref, dg_ref, db_ref, dgt_ref):
        _init_acc(l_ref, dg_ref, db_ref, dgt_ref)
        y = jnp.dot(yc_ref[...], w_ref[...], preferred_element_type=F32)
        res = ALPHA * x_ref[...] + (1.0 + gt_ref[...]) * y
        cen = res - jnp.mean(res, axis=-1, keepdims=True)
        rstd = lax.rsqrt(jnp.mean(cen * cen, axis=-1, keepdims=True) + LN_EPS)
        xh = cen * rstd
        err = xh * g_ref[...] + b_ref[...] - t_ref[...]
        l_ref[...] += (0.5 / d) * jnp.sum(err * err, keepdims=True)
        dxn_t = err * (1.0 / d)
        dxh = dxn_t * g_ref[...]
        dres = rstd * (dxh - jnp.mean(dxh, axis=-1, keepdims=True) - xh * jnp.mean(dxh * xh, axis=-1, keepdims=True))
        dy_ref[...] = ((1.0 + gt_ref[...]) * dres).astype(MXU_DTYPE)
        dxa_ref[...] = ALPHA * dres
        dg_ref[...] += _colsum(dxn_t * xh)
        db_ref[...] += _colsum(dxn_t)
        dgt_ref[...] += _colsum(dres * y)

    vec = SDS((1, d), F32)
    return pl.pallas_call(
        body, name="out_proj_ln_loss", out_shape=(SDS((1, 1), F32), SDS((s, d), MXU_DTYPE), SDS((s, d), F32), vec, vec, vec),
        grid=(s // tb,),
        in_specs=[_rows(tb, d), pl.BlockSpec((d, d), lambda i: (0, 0), pipeline_mode=pl.Buffered(1)), _rows(tb, d),
                  _const((1, d)), _const((1, d)), _const((1, d)), _rows(tb, d)],
        out_specs=(_const((1, 1)), _rows(tb, d), _rows(tb, d), _const((1, d)), _const((1, d)), _const((1, d))),
        compiler_params=_params(1))(ycat, w_out, x, gate, ln_g, ln_b, target)


def _conv_taps(u, up, w_ref, width):
    out = w_ref[width - 1:width, :] * u
    for j in range(width - 2, -1, -1):
        out = out + w_ref[j:j + 1, :] * _shift_down(u, up, width - 1 - j)
    return out


def _conv_taps_t(g, gn, w_ref, width):
    out = w_ref[width - 1:width, :] * g
    for j in range(width - 2, -1, -1):
        out = out + w_ref[j:j + 1, :] * _shift_up(g, gn, width - 1 - j)
    return out


def _conv_wgrad(dw_ref, g, u, up, width):
    dw_ref[width - 1:width, :] += _colsum(g * u)
    for j in range(width - 1):
        dw_ref[j:j + 1, :] += _colsum(g * _shift_down(u, up, width - 1 - j))


def _branch_a_fwd(proj, conv_w, tb):
    s = proj.shape[0]

    def body(ab, ac, ax, ag, acp, axp, w_ref, o_ref):
        has_prev = (pl.program_id(0) > 0).astype(F32)
        u = ac[...] * ax[...]
        up = acp[...] * axp[...] * has_prev
        o_ref[...] = (ab[...] * _conv_taps(u, up, w_ref, 3) * _silu(ag[...])).astype(MXU_DTYPE)

    return pl.pallas_call(
        body, name="branch_a_fwd", out_shape=SDS((s, BR), MXU_DTYPE), grid=(s // tb,),
        in_specs=[_rows(tb, BR, CB_AB), _rows(tb, BR, CB_AC), _rows(tb, BR, CB_AX), _rows(tb, BR, CB_AG),
                  _prev8(tb, BR, CB_AC), _prev8(tb, BR, CB_AX), _const((8, BR))],
        out_specs=_rows(tb, BR), compiler_params=_params(1))(proj, proj, proj, proj, proj, proj, conv_w)


def _branch_a_bwd(dycat, proj, conv_w, tb):
    s = proj.shape[0]

    def body(dy, dyn, ab, abn, ag, agn, ac, acp, ax, axp, w_ref, o_ref, dw_ref):
        _init_acc(dw_ref)
        i = pl.program_id(0)
        has_prev = (i > 0).astype(F32)
        has_next = (i < pl.num_programs(0) - 1).astype(F32)
        u = ac[...] * ax[...]
        up = acp[...] * axp[...] * has_prev
        v = _conv_taps(u, up, w_ref, 3)
        sg = _silu(ag[...])
        dv = dy[...] * ab[...] * sg
        dvn = dyn[...] * abn[...] * _silu(agn[...]) * has_next
        du = _conv_taps_t(dv, dvn, w_ref, 3)
        o_ref[:, 0:BR] = (dy[...] * v * sg).astype(MXU_DTYPE)
        o_ref[:, BR:2 * BR] = (du * ax[...]).astype(MXU_DTYPE)
        o_ref[:, 2 * BR:3 * BR] = (du * ac[...]).astype(MXU_DTYPE)
        o_ref[:, 3 * BR:4 * BR] = (dy[...] * ab[...] * v * _dsilu(ag[...])).astype(MXU_DTYPE)
        _conv_wgrad(dw_ref, dv, u, up, 3)

    return pl.pallas_call(
        body, name="branch_a_bwd", out_shape=(SDS((s, 4 * BR), MXU_DTYPE), SDS((8, BR), F32)), grid=(s // tb,),
        in_specs=[_rows(tb, BR, 0), _next8(tb, BR, s, 0),
                  _rows(tb, BR, CB_AB), _next8(tb, BR, s, CB_AB), _rows(tb, BR, CB_AG), _next8(tb, BR, s, CB_AG),
                  _rows(tb, BR, CB_AC), _prev8(tb, BR, CB_AC), _rows(tb, BR, CB_AX), _prev8(tb, BR, CB_AX),
                  _const((8, BR))],
        out_specs=(_rows(tb, 4 * BR), _const((8, BR))), compiler_params=_params(1),
    )(dycat, dycat, proj, proj, proj, proj, proj, proj, proj, proj, conv_w)


def _t5_bucket(dist):
    max_exact = REL_BUCKETS // 2
    nf = jnp.maximum(dist, 1).astype(F32)
    large = max_exact + (jnp.log(nf / max_exact) / math.log(REL_MAX_DIST / max_exact)
                         * (REL_BUCKETS - max_exact)).astype(jnp.int32)
    large = jnp.minimum(large, REL_BUCKETS - 1)
    return jnp.where(dist < max_exact, dist, large)


def _bucket_maps():
    maps = []
    i = jnp.arange(BLK)[:, None]
    j = jnp.arange(2 * BLK)[None, :]
    delta = i + BLK - j
    for window, dil in DILATIONS:
        span = window // dil
        bucket = _t5_bucket(jnp.clip(delta, 0, span) * dil)
        maps.append(jnp.where((delta >= 0) & (delta <= span), bucket, -1))
    return jnp.stack(maps).astype(jnp.int32)


def _bias_tables(rel_bias, buckets):
    n_pat = len(DILATIONS)

    def body(rb_ref, bk_ref, o_ref):
        for g in range(n_pat):
            bk = bk_ref[g]
            for h in range(ATT_HEADS):
                def per_bucket(b, acc):
                    return jnp.where(bk == b, rb_ref[b, h], acc)
                o_ref[g, h] = lax.fori_loop(0, REL_BUCKETS, per_bucket, jnp.full((BLK, 2 * BLK), NEG, F32))

    return pl.pallas_call(
        body, name="bias_tables", out_shape=SDS((n_pat, ATT_HEADS, BLK, 2 * BLK), F32),
        in_specs=[pl.BlockSpec(memory_space=pltpu.SMEM), pl.BlockSpec(memory_space=pltpu.VMEM)],
        compiler_params=_params())(rel_bias, buckets)


def _head_masks():
    lane = lax.broadcasted_iota(jnp.int32, (1, 2 * HEAD_DIM), 1)
    return [(lane < HEAD_DIM).astype(F32), (lane >= HEAD_DIM).astype(F32)]


def _strided(base, size, dil):
    return pl.ds(base, size, stride=dil) if dil > 1 else pl.ds(pl.multiple_of(base, BLK), size)


def _attn_groups(s, dil):
    return max(1, min(1024, s) // (dil * BLK)) if dil == 1 else max(1, min(2048, s) // (dil * BLK))


def _attn_fwd(proj, bias, dil):
    s = proj.shape[0]
    grp = _attn_groups(s, dil)
    u1 = dil * BLK
    unit = grp * u1
    nb = s // unit
    w = 2 * HEAD_DIM
    q0, k0, v0 = (cb * (BR // w) for cb in (CB_Q, CB_K, CB_V))

    def body(q_ref, kc_ref, kp_ref, vc_ref, vp_ref, bias_ref, o_ref, lse_ref, kbuf, vbuf):
        n = pl.program_id(1)
        col = lax.broadcasted_iota(jnp.int32, (1, 2 * BLK), 1)
        masks = _head_masks()
        kbuf[0:u1, :] = kp_ref[...]
        kbuf[u1:, :] = kc_ref[...]
        vbuf[0:u1, :] = vp_ref[...]
        vbuf[u1:, :] = vc_ref[...]

        def per_r(t, carry):
            j = t // dil
            base = j * u1 + t % dil
            rows = _strided(base, BLK, dil)
            no_prev = jnp.where((n == 0) & (j == 0) & (col < BLK), NEG, 0.0)
            q = q_ref[rows, :] * (HEAD_DIM ** -0.5)
            k = kbuf[_strided(base, 2 * BLK, dil), :].astype(MXU_DTYPE)
            v = vbuf[_strided(base, 2 * BLK, dil), :].astype(MXU_DTYPE)
            q2 = jnp.concatenate([q * masks[0], q * masks[1]], axis=0).astype(MXU_DTYPE)
            sc = lax.dot_general(q2, k, (((1,), (1,)), ((), ())), preferred_element_type=F32)
            sc = sc + jnp.concatenate([bias_ref[0], bias_ref[1]], axis=0) + no_prev
            mx = jnp.max(sc, axis=-1, keepdims=True)
            p = jnp.exp(sc - mx)
            l = jnp.sum(p, axis=-1, keepdims=True)
            o2 = jnp.dot((p / l).astype(MXU_DTYPE), v, preferred_element_type=F32)
            lse2 = mx + jnp.log(l)
            o_ref[rows, :] = o2[0:BLK] * masks[0] + o2[BLK:2 * BLK] * masks[1]
            lse_ref[rows, :] = lse2[0:BLK] * masks[0] + lse2[BLK:2 * BLK] * masks[1]
            return carry

        lax.fori_loop(0, grp * dil, per_r, 0, unroll=8)

    cur = lambda c0: pl.BlockSpec((unit, w), lambda hp, n: (n, c0 + hp))
    prev = lambda c0: pl.BlockSpec((u1, w), lambda hp, n: (jnp.maximum(n * grp - 1, 0), c0 + hp))
    out = pl.BlockSpec((unit, w), lambda hp, n: (n, hp))
    return pl.pallas_call(
        body, name=f"attn_fwd_d{dil}", out_shape=(SDS((s, BR), F32), SDS((s, BR), F32)), grid=(BR // w, nb),
        in_specs=[cur(q0), cur(k0), prev(k0), cur(v0), prev(v0),
                  pl.BlockSpec((2, BLK, 2 * BLK), lambda hp, n: (hp, 0, 0))],
        out_specs=(out, out),
        scratch_shapes=[pltpu.VMEM((unit + u1, w), F32), pltpu.VMEM((unit + u1, w), F32)],
        compiler_params=_params(2))(proj, proj, proj, proj, proj, bias)


def _softmax3(l0, l1, l2):
    mx = jnp.maximum(jnp.maximum(l0, l1), l2)
    e0, e1, e2 = jnp.exp(l0 - mx), jnp.exp(l1 - mx), jnp.exp(l2 - mx)
    inv = 1.0 / (e0 + e1 + e2)
    return e0 * inv, e1 * inv, e2 * inv


def _attn_combine(os_, lses, proj, tb):
    s = proj.shape[0]

    def body(o0, o1, o2, l0, l1, l2, bg, y_ref):
        w0, w1, w2 = _softmax3(l0[...], l1[...], l2[...])
        attn = w0 * o0[...] + w1 * o1[...] + w2 * o2[...]
        y_ref[...] = (attn * _silu(bg[...])).astype(MXU_DTYPE)

    return pl.pallas_call(
        body, name="attn_combine", out_shape=SDS((s, BR), MXU_DTYPE), grid=(s // tb,),
        in_specs=[_rows(tb, BR)] * 6 + [_rows(tb, BR, CB_BG)], out_specs=_rows(tb, BR),
        compiler_params=_params(1))(*os_, *lses, proj)


def _attn_bwd_pre(dycat, os_, lses, proj, head_ones, tb):
    s = proj.shape[0]

    def body(dy, o0, o1, o2, l0, l1, l2, bg, e_ref, dbg_ref, do0, do1, do2, dm0, dm1, dm2):
        w0, w1, w2 = _softmax3(l0[...], l1[...], l2[...])
        attn = w0 * o0[...] + w1 * o1[...] + w2 * o2[...]
        dattn = dy[...] * _silu(bg[...])
        dbg_ref[...] = (dy[...] * attn * _dsilu(bg[...])).astype(MXU_DTYPE)
        prod = dattn * attn
        hi = prod.astype(MXU_DTYPE)
        lo = (prod - hi.astype(F32)).astype(MXU_DTYPE)
        tot = (jnp.dot(hi, e_ref[...], preferred_element_type=F32)
               + jnp.dot(lo, e_ref[...], preferred_element_type=F32))
        for wg, do_ref, dm_ref in ((w0, do0, dm0), (w1, do1, dm1), (w2, do2, dm2)):
            do_ref[...] = wg * dattn
            dm_ref[...] = wg * tot

    big = SDS((s, BR), F32)
    return pl.pallas_call(
        body, name="attn_bwd_pre", out_shape=(SDS((s, BR), MXU_DTYPE),) + (big,) * 6, grid=(s // tb,),
        in_specs=[_rows(tb, BR, 1)] + [_rows(tb, BR)] * 6 + [_rows(tb, BR, CB_BG), _const((BR, BR))],
        out_specs=(_rows(tb, BR),) * 7, compiler_params=_params(1))(dycat, *os_, *lses, proj, head_ones)


def _attn_bwd(proj, do, lse, dm, bias, dil, carry=None, add=()):
    s = proj.shape[0]
    grp = _attn_groups(s, dil)
    u1 = dil * BLK
    unit = grp * u1
    nb = s // unit
    w = 2 * HEAD_DIM
    q0, k0, v0 = (cb * (BR // w) for cb in (CB_Q, CB_K, CB_V))
    n_add = len(add)

    def body(q_ref, kc_ref, kp_ref, vc_ref, vp_ref, do_ref, lse_ref, dm_ref, bias_ref, *rest):
        more, (dq_ref, dk_ref, dv_ref, dbias_ref, kbuf, vbuf, stage_k, stage_v) = rest[:3 * n_add], rest[3 * n_add:]
        more_q, more_k, more_v = more[0::3], more[1::3], more[2::3]
        plus = lambda val, refs, rows: functools.reduce(lambda acc, r: acc + r[rows, :], refs, val)
        n = pl.program_id(1)
        col = lax.broadcasted_iota(jnp.int32, (1, 2 * BLK), 1)
        masks = _head_masks()

        @pl.when(n == 0)
        def _():
            dbias_ref[...] = jnp.zeros_like(dbias_ref)
            stage_k[...] = jnp.zeros_like(stage_k)
            stage_v[...] = jnp.zeros_like(stage_v)

        for out_ref, stage, more_ in ((dk_ref, stage_k, more_k), (dv_ref, stage_v, more_v)):
            if grp > 1:
                out_ref[0:unit - u1, :] = plus(stage[u1:unit, :], more_, slice(0, unit - u1))
            stage[0:u1, :] = stage[unit:unit + u1, :]

        @pl.when(n < nb)
        def _():
            kbuf[0:u1, :] = kp_ref[...]
            kbuf[u1:, :] = kc_ref[...]
            vbuf[0:u1, :] = vp_ref[...]
            vbuf[u1:, :] = vc_ref[...]

            def per_r(t, carry):
                j = t // dil
                base = j * u1 + t % dil
                rows = _strided(base, BLK, dil)
                rows_hi = _strided(base + u1, BLK, dil)
                no_prev = jnp.where((n == 0) & (j == 0) & (col < BLK), NEG, 0.0)
                q = q_ref[rows, :] * (HEAD_DIM ** -0.5)
                k = kbuf[_strided(base, 2 * BLK, dil), :].astype(MXU_DTYPE)
                v = vbuf[_strided(base, 2 * BLK, dil), :].astype(MXU_DTYPE)
                do_t, lse_t, dm_t = do_ref[rows, :], lse_ref[rows, :], dm_ref[rows, :]
                stack = lambda t: jnp.concatenate([t * masks[0], t * masks[1]], axis=0).astype(MXU_DTYPE)
                per_head = lambda t: jnp.concatenate([t[:, 0:1], t[:, HEAD_DIM:HEAD_DIM + 1]], axis=0)
                q2, do2 = stack(q), stack(do_t)
                sc = lax.dot_general(q2, k, (((1,), (1,)), ((), ())), preferred_element_type=F32)
                p = jnp.exp(sc + jnp.concatenate([bias_ref[0], bias_ref[1]], axis=0) + no_prev - per_head(lse_t))
                dp = lax.dot_general(do2, v, (((1,), (1,)), ((), ())), preferred_element_type=F32)
                ds = p * (dp - per_head(dm_t))
                dbias_ref[0] += ds[0:BLK]
                dbias_ref[1] += ds[BLK:2 * BLK]
                dsb, pb = ds.astype(MXU_DTYPE), p.astype(MXU_DTYPE)
                dq2 = jnp.dot(dsb, k, preferred_element_type=F32)
                dk_acc = lax.dot_general(dsb, q2, (((0,), (0,)), ((), ())), preferred_element_type=F32)
                dv_acc = lax.dot_general(pb, do2, (((0,), (0,)), ((), ())), preferred_element_type=F32)
                dq_ref[rows, :] = plus((dq2[0:BLK] * masks[0] + dq2[BLK:2 * BLK] * masks[1]) * (HEAD_DIM ** -0.5),
                                       more_q, rows)
                stage_k[rows, :] = stage_k[rows, :] + dk_acc[0:BLK]
                stage_v[rows, :] = stage_v[rows, :] + dv_acc[0:BLK]
                stage_k[rows_hi, :] = dk_acc[BLK:2 * BLK]
                stage_v[rows_hi, :] = dv_acc[BLK:2 * BLK]
                return carry

            lax.fori_loop(0, grp * dil, per_r, 0, unroll=8)

        dk_ref[unit - u1:unit, :] = plus(stage_k[0:u1, :], more_k, slice(unit - u1, unit))
        dv_ref[unit - u1:unit, :] = plus(stage_v[0:u1, :], more_v, slice(unit - u1, unit))

    qn = lambda n: jnp.minimum(n, nb - 1)
    cur = lambda c0: pl.BlockSpec((unit, w), lambda hp, n: (qn(n), c0 + hp))
    prev = lambda c0: pl.BlockSpec((u1, w), lambda hp, n: (jnp.maximum(qn(n) * grp - 1, 0), c0 + hp))
    row = pl.BlockSpec((unit, w), lambda hp, n: (qn(n), hp))
    late = pl.BlockSpec((unit, w), lambda hp, n: (jnp.maximum(n - 1, 0), hp))
    tab = pl.BlockSpec((2, BLK, 2 * BLK), lambda hp, n: (hp, 0, 0))
    big = SDS((s, BR), F32)
    return _call(
        body, name=f"attn_bwd_d{dil}", out_shape=(big, big, big, SDS((ATT_HEADS, BLK, 2 * BLK), F32)),
        grid=(BR // w, nb + 1),
        in_specs=[cur(q0), cur(k0), prev(k0), cur(v0), prev(v0), row, row, row, tab] + [row, late, late] * n_add,
        out_specs=(row, late, late, tab),
        scratch_shapes=[pltpu.VMEM((unit + u1, w), F32)] * 4,
        args=(proj, proj, proj, proj, proj, do, lse, dm, bias) + tuple(t for part in add for t in part), carry=carry)


def _rel_bias_grad(dbias, buckets):
    def body(db_ref, bk_ref, o_ref):
        row = lax.broadcasted_iota(jnp.int32, (REL_BUCKETS, 128), 0)
        lane = lax.broadcasted_iota(jnp.int32, (REL_BUCKETS, 128), 1)

        def per_bucket(b, acc):
            for g in range(len(DILATIONS)):
                hit = bk_ref[g] == b
                for h in range(ATT_HEADS):
                    both = db_ref[0, g, h] + db_ref[1, g, h]
                    val = jnp.sum(jnp.where(hit, both, 0.0), keepdims=True)
                    acc = acc + jnp.where((row == b) & (lane == h), val, 0.0)
            return acc

        o_ref[...] = lax.fori_loop(0, REL_BUCKETS, per_bucket, jnp.zeros((REL_BUCKETS, 128), F32))

    assert dbias.shape[0] == DEPTH == 2
    return pl.pallas_call(body, name="rel_bias_grad", out_shape=SDS((REL_BUCKETS, 128), F32),
                          compiler_params=_params())(dbias, buckets)


def _scan_rows(a_ref, b_ref, o_ref, carry, *, reverse):
    tb = a_ref.shape[0]
    order = range(7, -1, -1) if reverse else range(8)

    @pl.when(pl.program_id(0) == 0)
    def _():
        carry[...] = jnp.zeros_like(carry)

    def group(gi, h):
        r0 = pl.multiple_of((tb // 8 - 1 - gi if reverse else gi) * 8, 8)
        a8, b8 = a_ref[pl.ds(r0, 8), :], b_ref[pl.ds(r0, 8), :]
        rows = [None] * 8
        for k in order:
            if reverse:
                rows[k] = b8[k:k + 1] + h
                h = a8[k:k + 1] * rows[k]
            else:
                h = a8[k:k + 1] * h + b8[k:k + 1]
                rows[k] = h
        o_ref[pl.ds(r0, 8), :] = jnp.concatenate(rows, axis=0)
        return h

    carry[...] = lax.fori_loop(0, tb // 8, group, carry[...])


def _lru_scan_fwd(a, b, proj, tb):
    s = a.shape[0]

    def body(a_ref, b_ref, g_ref, h_ref, y_ref, carry):
        _scan_rows(a_ref, b_ref, h_ref, carry, reverse=False)
        y_ref[...] = (h_ref[...] * _silu(g_ref[...])).astype(MXU_DTYPE)

    return pl.pallas_call(
        body, name="lru_scan", out_shape=(SDS((s, BR), F32), SDS((s, BR), MXU_DTYPE)), grid=(s // tb,),
        in_specs=[_rows(tb, BR), _rows(tb, BR), _rows(tb, BR, CB_CG)], out_specs=(_rows(tb, BR), _rows(tb, BR)),
        scratch_shapes=[pltpu.VMEM((1, BR), F32)], compiler_params=_params(1))(a, b, proj)


def _lru_scan_bwd(a, dycat, h, proj, tb):
    s = a.shape[0]
    nt = s // tb

    def body(a_ref, dy_ref, h_ref, g_ref, l_ref, dg_ref, carry, dh_buf):
        dh_buf[...] = dy_ref[...] * _silu(g_ref[...])
        dg_ref[...] = (dy_ref[...] * h_ref[...] * _dsilu(g_ref[...])).astype(MXU_DTYPE)
        _scan_rows(a_ref, dh_buf, l_ref, carry, reverse=True)

    rev = lambda cb=0: pl.BlockSpec((tb, BR), lambda i: (nt - 1 - i, cb))
    return pl.pallas_call(
        body, name="lru_scan_bwd", out_shape=(SDS((s, BR), F32), SDS((s, BR), MXU_DTYPE)), grid=(nt,),
        in_specs=[rev(), rev(2), rev(), rev(CB_CG)], out_specs=(rev(), rev()),
        scratch_shapes=[pltpu.VMEM((1, BR), F32), pltpu.VMEM((tb, BR), F32)],
        compiler_params=_params(1))(a, dycat, h, proj)


def _scan_tile(s):
    return min(512, s)


def _load_chunked(ref, t0, pt):
    ln = pt // 8
    return jnp.concatenate([ref[pl.ds(t0 + j, 8, stride=ln), :] for j in range(ln)], axis=0)


def _store_natural(ref, t0, pt, val):
    ln = pt // 8
    for j in range(ln):
        ref[pl.ds(t0 + j, 8, stride=ln), :] = val[j * 8:(j + 1) * 8]


def _scan_tile_in_place(a_ref, x_ref, carry, pw, *, reverse):
    ch2 = x_ref.shape[1]
    ch = ch2 // 2
    ln = x_ref.shape[0] // 8
    ar = a_ref[:, 0:ch]
    ai = -a_ref[:, ch:ch2] if reverse else a_ref[:, ch:ch2]

    def cmul(pr, pi, xr, xi):
        return pr * xr - pi * xi, pr * xi + pi * xr

    @pl.when(pl.program_id(0) == 0)
    def _():
        carry[...] = jnp.zeros_like(carry)

        def fill(j, p):
            pw[pl.ds(j, 1), 0:ch] = p[0]
            pw[pl.ds(j, 1), ch:ch2] = p[1]
            return cmul(ar, ai, *p)

        lax.fori_loop(0, ln, fill, (ar, ai))

    def rows_of(j):
        return pl.ds(pl.multiple_of((ln - 1 - j if reverse else j) * 8, 8), 8)

    def local(j, x):
        rows = rows_of(j)
        nr, ni = cmul(ar, ai, *x)
        xr, xi = nr + x_ref[rows, 0:ch], ni + x_ref[rows, ch:ch2]
        x_ref[rows, 0:ch] = xr
        x_ref[rows, ch:ch2] = xi
        return xr, xi

    zero = jnp.zeros((8, ch), F32)
    er, ei = lax.fori_loop(0, ln, local, (zero, zero), unroll=2)
    apr, api = pw[ln - 1:ln, 0:ch], pw[ln - 1:ln, ch:ch2]
    cr, ci = carry[:, 0:ch], carry[:, ch:ch2]
    into_r, into_i = [None] * 8, [None] * 8
    for c in (range(7, -1, -1) if reverse else range(8)):
        into_r[c], into_i[c] = cr, ci
        pr, pi = cmul(apr, api, cr, ci)
        cr, ci = er[c:c + 1] + pr, ei[c:c + 1] + pi
    carry[:, 0:ch] = cr
    carry[:, ch:ch2] = ci
    into_r, into_i = jnp.concatenate(into_r, axis=0), jnp.concatenate(into_i, axis=0)

    def fix(j, carry_):
        rows = rows_of(j)
        dr, di = cmul(pw[pl.ds(j, 1), 0:ch], pw[pl.ds(j, 1), ch:ch2], into_r, into_i)
        x_ref[rows, 0:ch] += dr
        x_ref[rows, ch:ch2] += di
        return carry_

    lax.fori_loop(0, ln, fix, 0, unroll=2)


def _neg_expm1(z):
    series = -z * (1.0 + z * (0.5 + z * (1.0 / 6 + z * (1.0 / 24 + z * (1.0 / 120)))))
    return jnp.where(z > -0.05, series, 1.0 - jnp.exp(z))


def _lru_gate(xc, pre_r, pre_i, lam):
    log_a = -LRU_C * jax.nn.sigmoid(pre_r) * jax.nn.softplus(-lam)
    return jnp.exp(log_a), jnp.sqrt(_neg_expm1(2.0 * log_a)) * jax.nn.sigmoid(pre_i) * xc


def _lru_gates_fwd(proj, conv_w, conv_b, w_cat, b_cat, lam, tb):
    s = proj.shape[0]

    def body(cx, cxp, w_ref, cb_ref, wc_ref, bc_ref, lam_ref, a_ref, b_ref):
        has_prev = (pl.program_id(0) > 0).astype(F32)
        xc = _conv_taps(cx[...], cxp[...] * has_prev, w_ref, 4) + cb_ref[...]
        pre = jnp.dot(xc.astype(MXU_DTYPE), wc_ref[...], preferred_element_type=F32) + bc_ref[...]
        a_ref[...], b_ref[...] = _lru_gate(xc, pre[:, 0:BR], pre[:, BR:2 * BR], lam_ref[...])

    big = SDS((s, BR), F32)
    return pl.pallas_call(
        body, name="lru_gates_fwd", out_shape=(big, big), grid=(s // tb,),
        in_specs=[_rows(tb, BR, CB_CX), _prev8(tb, BR, CB_CX), _const((8, BR)), _const((1, BR)),
                  _const((BR, 2 * BR)), _const((1, 2 * BR)), _const((1, BR))],
        out_specs=(_rows(tb, BR), _rows(tb, BR)), compiler_params=_params(1),
    )(proj, proj, conv_w, conv_b, w_cat, b_cat, lam)


def _lru_gates_bwd(proj, lmb, h, conv_w, conv_b, w_cat, b_cat, lam, tb):
    s = proj.shape[0]

    def body(cx, cxp, l_ref, h_ref, hp_ref, w_ref, cb_ref, wc_ref, bc_ref, lam_ref,
             dxc_ref, dpre_ref, xc_ref, dbc_ref, dlam_ref):
        _init_acc(dbc_ref, dlam_ref)
        has_prev = (pl.program_id(0) > 0).astype(F32)
        xc = _conv_taps(cx[...], cxp[...] * has_prev, w_ref, 4) + cb_ref[...]
        xcb = xc.astype(MXU_DTYPE)
        pre = jnp.dot(xcb, wc_ref[...], preferred_element_type=F32) + bc_ref[...]
        _, vjp = jax.vjp(_lru_gate, xc, pre[:, 0:BR], pre[:, BR:2 * BR], lam_ref[...])
        lm = l_ref[...]
        dxc, dpr, dpi, dlam = vjp((lm * _shift_down(h_ref[...], hp_ref[...] * has_prev, 1), lm))
        dpre = jnp.concatenate([dpr, dpi], axis=1)
        dpreb = dpre.astype(MXU_DTYPE)
        dxc_ref[...] = dxc + lax.dot_general(dpreb, wc_ref[...], (((1,), (1,)), ((), ())),
                                             preferred_element_type=F32)
        dpre_ref[...] = dpreb
        xc_ref[...] = xcb
        dbc_ref[...] += _colsum(dpre)
        dlam_ref[...] += dlam

    return pl.pallas_call(
        body, name="lru_gates_bwd",
        out_shape=(SDS((s, BR), F32), SDS((s, 2 * BR), MXU_DTYPE), SDS((s, BR), MXU_DTYPE),
                   SDS((1, 2 * BR), F32), SDS((1, BR), F32)),
        grid=(s // tb,),
        in_specs=[_rows(tb, BR, CB_CX), _prev8(tb, BR, CB_CX), _rows(tb, BR), _rows(tb, BR), _prev8(tb, BR),
                  _const((8, BR)), _const((1, BR)), _const((BR, 2 * BR)), _const((1, 2 * BR)), _const((1, BR))],
        out_specs=(_rows(tb, BR), _rows(tb, 2 * BR), _rows(tb, BR), _const((1, 2 * BR)), _const((1, BR))),
        compiler_params=_params(1))(proj, proj, lmb, h, h, conv_w, conv_b, w_cat, b_cat, lam)


def _conv_c_bwd(dxc, proj, conv_w, tb):
    s = proj.shape[0]

    def body(g, gn, cx, cxp, w_ref, dcx_ref, dw_ref, db_ref):
        _init_acc(dw_ref, db_ref)
        i = pl.program_id(0)
        has_prev = (i > 0).astype(F32)
        has_next = (i < pl.num_programs(0) - 1).astype(F32)
        gt = g[...]
        dcx_ref[...] = _conv_taps_t(gt, gn[...] * has_next, w_ref, 4).astype(MXU_DTYPE)
        _conv_wgrad(dw_ref, gt, cx[...], cxp[...] * has_prev, 4)
        db_ref[...] += _colsum(gt)

    return pl.pallas_call(
        body, name="conv_c_bwd", out_shape=(SDS((s, BR), MXU_DTYPE), SDS((8, BR), F32), SDS((1, BR), F32)),
        grid=(s // tb,),
        in_specs=[_rows(tb, BR), _next8(tb, BR, s), _rows(tb, BR, CB_CX), _prev8(tb, BR, CB_CX), _const((8, BR))],
        out_specs=(_rows(tb, BR), _const((8, BR)), _const((1, BR))), compiler_params=_params(1),
    )(dxc, dxc, proj, proj, conv_w)


def _s5_disc(lam_re, lam_im, log_dt):
    dt = jnp.exp(log_dt)
    mag = jnp.exp(lam_re * dt)
    ab_re = mag * jnp.cos(lam_im * dt)
    ab_im = mag * jnp.sin(lam_im * dt)
    den = lam_re * lam_re + lam_im * lam_im
    f_re = ((ab_re - 1.0) * lam_re + ab_im * lam_im) / den
    f_im = (ab_im * lam_re - (ab_re - 1.0) * lam_im) / den
    return ab_re, ab_im, f_re, f_im


def _s5_bbar(f_re, f_im, b_re, b_im):
    return f_re * b_re - f_im * b_im, f_re * b_im + f_im * b_re


def _s5_disc_fwd(lam_re, lam_im, log_dt):
    def body(lr, li, ld, o0, o1, o2, o3):
        o0[...], o1[...], o2[...], o3[...] = _s5_disc(lr[...], li[...], ld[...])
    return pl.pallas_call(body, name="s5_disc_fwd", out_shape=(SDS(lam_re.shape, F32),) * 4)(lam_re, lam_im, log_dt)


def _s5_disc_bwd(lam_re, lam_im, log_dt, cts):
    def body(lr, li, ld, c0, c1, c2, c3, o0, o1, o2):
        _, vjp = jax.vjp(_s5_disc, lr[...], li[...], ld[...])
        o0[...], o1[...], o2[...] = vjp((c0[...], c1[...], c2[...], c3[...]))
    return pl.pallas_call(body, name="s5_disc_bwd", out_shape=(SDS(lam_re.shape, F32), SDS(lam_re.shape, F32),
                                                                SDS(log_dt.shape, F32)))(lam_re, lam_im, log_dt, *cts)


def _s5_bbar_fwd(f_re, f_im, b_re, b_im):
    def body(fr, fi, br, bi, o0, o1):
        o0[...], o1[...] = _s5_bbar(fr[...], fi[...], br[...], bi[...])
    return pl.pallas_call(body, name="s5_bbar_fwd", out_shape=(SDS(b_re.shape, F32),) * 2)(f_re, f_im, b_re, b_im)


def _s5_bbar_bwd(f_re, f_im, b_re, b_im, d_re, d_im):
    def body(fr, fi, br, bi, dr, di, o0, o1, o2, o3):
        _, vjp = jax.vjp(_s5_bbar, fr[...], fi[...], br[...], bi[...])
        o0[...], o1[...], o2[...], o3[...] = vjp((dr[...], di[...]))
    col, mat = SDS(f_re.shape, F32), SDS(b_re.shape, F32)
    return pl.pallas_call(body, name="s5_bbar_bwd", out_shape=(col, col, mat, mat))(f_re, f_im, b_re, b_im, d_re, d_im)


def _s5_tail_bwd(dycat, ylin, proj, d_skip, w_glu, b_glu, tb):
    s = proj.shape[0]

    def body(dy, yl, u, dg, dk, w_ref, b_ref, dyl_ref, dus_ref, ddg_ref, g_ref, dt_ref, ddk_ref, dbg_ref):
        _init_acc(ddk_ref, dbg_ref)
        g, gelu_vjp = jax.vjp(jax.nn.gelu, yl[...] + dk[...] * u[...])
        gb = g.astype(MXU_DTYPE)
        sg = jax.nn.sigmoid(jnp.dot(gb, w_ref[...], preferred_element_type=F32) + b_ref[...])
        dz = dy[...] * _silu(dg[...])
        ddg_ref[...] = (dy[...] * g * sg * _dsilu(dg[...])).astype(MXU_DTYPE)
        dt = dz * g * sg * (1.0 - sg)
        dtb = dt.astype(MXU_DTYPE)
        dgel = dz * sg + lax.dot_general(dtb, w_ref[...], (((1,), (1,)), ((), ())), preferred_element_type=F32)
        dyv, = gelu_vjp(dgel)
        dyl_ref[...] = dyv
        dus_ref[...] = dyv * dk[...]
        g_ref[...] = gb
        dt_ref[...] = dtb
        ddk_ref[...] += _colsum(dyv * u[...])
        dbg_ref[...] += _colsum(dt)

    big, half, vec = SDS((s, BR), F32), SDS((s, BR), MXU_DTYPE), SDS((1, BR), F32)
    return pl.pallas_call(
        body, name="s5_tail_bwd", out_shape=(big, big, half, half, half, vec, vec), grid=(s // tb,),
        in_specs=[_rows(tb, BR, 3), _rows(tb, BR), _rows(tb, BR, CB_DU), _rows(tb, BR, CB_DG), _const((1, BR)),
                  _const((BR, BR)), _const((1, BR))],
        out_specs=(_rows(tb, BR),) * 5 + (_const((1, BR)), _const((1, BR))), compiler_params=_params(1),
    )(dycat, ylin, proj, proj, d_skip, w_glu, b_glu)


def _assemble_dproj(da, dqkv, dbg, dcx, dcg, du, dus, ddg, tb):
    s = da.shape[0]

    def body(da_ref, dq_ref, dk_ref, dv_ref, dbg_ref, dcx_ref, dcg_ref, du_ref, dus_ref, ddg_ref, o_ref):
        o_ref[:, 0:4 * BR] = da_ref[...]
        for j, part in enumerate((dq_ref, dk_ref, dv_ref)):
            o_ref[:, (4 + j) * BR:(5 + j) * BR] = part[...].astype(MXU_DTYPE)
        o_ref[:, 7 * BR:8 * BR] = dbg_ref[...].astype(MXU_DTYPE)
        o_ref[:, 8 * BR:9 * BR] = dcx_ref[...].astype(MXU_DTYPE)
        o_ref[:, 9 * BR:10 * BR] = dcg_ref[...].astype(MXU_DTYPE)
        o_ref[:, 10 * BR:11 * BR] = (du_ref[...] + dus_ref[...]).astype(MXU_DTYPE)
        o_ref[:, 11 * BR:12 * BR] = ddg_ref[...].astype(MXU_DTYPE)

    return pl.pallas_call(
        body, name="assemble_dproj", out_shape=SDS((s, N_IN), MXU_DTYPE), grid=(s // tb,),
        in_specs=[_rows(tb, 4 * BR)] + [_rows(tb, BR)] * 9, out_specs=_rows(tb, N_IN),
        compiler_params=_params(1))(da, *dqkv, dbg, dcx, dcg, du, dus, ddg)


def _sum_leading(xs, tr, name):
    n, _, c = xs[0].shape
    nl = len(xs)
    tr = min([tr] + [x.shape[1] for x in xs])
    assert all(x.shape[1] % tr == 0 for x in xs), (name, tr)
    nrs = [x.shape[1] // tr for x in xs]
    starts = [sum(nrs[:l]) for l in range(nl)]

    def body(*refs):
        i = pl.program_id(0)
        for l in range(nl):
            @pl.when((i >= starts[l]) & (i < starts[l] + nrs[l]))
            def _():
                acc = refs[l * n][...].astype(F32)
                for ref in refs[l * n + 1:(l + 1) * n]:
                    acc = acc + ref[...].astype(F32)
                refs[nl * n][...] = acc

    specs = [pl.BlockSpec((None, tr, c), functools.partial(
        lambda i, k, l: (k, jnp.clip(i - starts[l], 0, nrs[l] - 1), 0), k=k, l=l)) for l in range(nl) for k in range(n)]
    return pl.pallas_call(body, name=name, out_shape=SDS((sum(nrs) * tr, c), F32), grid=(sum(nrs),), in_specs=specs,
                          out_specs=pl.BlockSpec((tr, c), lambda i: (i, 0)),
                          compiler_params=_params(1))(*[x for x in xs for _ in range(n)])


def _adamw(w, g_parts, m, v, tr, name):
    r, c = w.shape
    tr = min(tr, r)
    n = len(g_parts)
    assert r % tr == 0, (name, r, tr)

    def body(*refs):
        w_ref, m_ref, v_ref = refs[0], refs[1 + n], refs[2 + n]
        g_ref, d_ref, nm_ref, nv_ref = refs[3 + n:]
        g = refs[1][...]
        for ref in refs[2:1 + n]:
            g = g + ref[...]
        mm = ADAM_B1 * m_ref[...] + (1.0 - ADAM_B1) * g
        vv = ADAM_B2 * v_ref[...] + (1.0 - ADAM_B2) * jnp.square(g)
        m_hat = mm / (1.0 - ADAM_B1 ** ADAM_STEP)
        v_hat = vv / (1.0 - ADAM_B2 ** ADAM_STEP)
        g_ref[...] = g
        d_ref[...] = -ADAM_LR * (m_hat / (jnp.sqrt(v_hat) + ADAM_EPS) + ADAM_WD * w_ref[...])
        nm_ref[...] = mm
        nv_ref[...] = vv

    spec = pl.BlockSpec((tr, c), lambda i: (i, 0))
    return _call(body, name=name, out_shape=(SDS((r, c), F32),) * 4, grid=(r // tr,), in_specs=[spec] * (3 + n),
                 out_specs=(spec,) * 4, scratch_shapes=[], args=(w, *g_parts, m, v))


class _AllGather8:
    def __init__(self, block):
        self.m_per = block.shape[0]
        self.arrays, self.n_in, self.n_out = [block], 1, 1
        self.out_shapes = (SDS((N_DEV * self.m_per, block.shape[1]), block.dtype),)
        self.scratch = [pltpu.SemaphoreType.DMA((7,)), pltpu.SemaphoreType.DMA((7,)), pltpu.SemaphoreType.DMA]

    def _copies(self, ins, outs, sems):
        (x_ref,), (out_ref,), (send_sems, recv_sems, local_sem) = ins, outs, sems
        x, y, c = lax.axis_index("x"), lax.axis_index("y"), lax.axis_index("c")
        me, sibling = (x, y, c), (x, y, 1 - c)
        chips = [(1 - x, y), (x, 1 - y), (1 - x, 1 - y)]

        def rows(px, py, pc):
            return out_ref.at[pl.ds((4 * px + 2 * py + pc) * self.m_per, self.m_per), :]

        def copy(k, blk, to, src=None):
            return pltpu.make_async_remote_copy(
                src_ref=rows(*blk) if src is None else src, dst_ref=rows(*blk), send_sem=send_sems.at[k],
                recv_sem=recv_sems.at[k], device_id=to, device_id_type=MESH)

        mine = pltpu.make_async_copy(x_ref, rows(*me), local_sem)
        first = [copy(0, me, sibling, src=x_ref)]
        first += [copy(1 + j, me, (*chip, c), src=x_ref) for j, chip in enumerate(chips)]
        passed = [copy(4 + j, (*chip, c), sibling) for j, chip in enumerate(chips)]
        arrivals = [copy(1 + j, (*chip, c), me) for j, chip in enumerate(chips)]
        from_sibling = [copy(0, sibling, me)] + [copy(4 + j, (*chip, 1 - c), me) for j, chip in enumerate(chips)]
        return mine, first, passed, arrivals, from_sibling

    def start(self, ins, outs, sems):
        mine, first, _, _, _ = self._copies(ins, outs, sems)
        mine.start()
        for cp in first:
            cp.start()

    def wait(self, ins, outs, sems):
        mine, first, passed, arrivals, from_sibling = self._copies(ins, outs, sems)
        for arrived, onward in zip(arrivals, passed):
            arrived.wait_recv()
            onward.start()
        for cp in from_sibling:
            cp.wait_recv()
        for cp in first + passed:
            cp.wait_send()
        mine.wait()


def _allgather8(block, name):
    ex = _AllGather8(block)

    def body(x_ref, out_ref, *sems):
        ex.start((x_ref,), (out_ref,), sems)
        ex.wait((x_ref,), (out_ref,), sems)

    return pl.pallas_call(
        body, name=name, out_shape=ex.out_shapes[0], in_specs=[pl.BlockSpec(memory_space=pltpu.VMEM)],
        out_specs=pl.BlockSpec(memory_space=pltpu.VMEM), scratch_shapes=ex.scratch, compiler_params=_params())(block)


class _Exchange:
    def __init__(self, items, out_shapes):
        self.items, self.out_shapes = list(items), tuple(out_shapes)
        self.arrays = [it[0] for it in self.items]
        n = len(self.items)
        self.n_in, self.n_out = n, len(self.out_shapes)
        self.scratch = [pltpu.SemaphoreType.DMA((n * N_CHIPS,)), pltpu.SemaphoreType.DMA((n * N_CHIPS,)),
                        pltpu.SemaphoreType.DMA((n,))]

    def _copies(self, ins, outs, sems, m):
        send_sems, recv_sems, local_sems = sems
        c = lax.axis_index("c")
        others = [j for j in range(N_CHIPS) if j != m]

        def remote(a, src, dst, to, from_):
            return pltpu.make_async_remote_copy(
                src_ref=src, dst_ref=dst, send_sem=send_sems.at[a * N_CHIPS + to],
                recv_sem=recv_sems.at[a * N_CHIPS + from_], device_id=(to // 2, to % 2, c), device_id_type=MESH)

        local, sends, recvs = [], [], []
        for a, (_, oi, src_of, dst_of) in enumerate(self.items):
            local.append(pltpu.make_async_copy(src_of(ins[a], m), dst_of(outs[oi], m), local_sems.at[a]))
            for j in others:
                sends.append(remote(a, src_of(ins[a], j), dst_of(outs[oi], m), j, m))
                recvs.append(remote(a, src_of(ins[a], m), dst_of(outs[oi], j), j, j))
        return local, sends, recvs

    def _on_my_chip(self, fn):
        chip = 2 * lax.axis_index("x") + lax.axis_index("y")
        for m in range(N_CHIPS):
            pl.when(chip == m)(functools.partial(fn, m))

    def start(self, ins, outs, sems):
        def go(m):
            local, sends, _ = self._copies(ins, outs, sems, m)
            for cp in local + sends:
                cp.start()
        self._on_my_chip(go)

    def wait(self, ins, outs, sems):
        def go(m):
            local, sends, recvs = self._copies(ins, outs, sems, m)
            for cp in recvs:
                cp.wait_recv()
            for cp in sends:
                cp.wait_send()
            for cp in local:
                cp.wait()
        self._on_my_chip(go)


def _half_rows(ref, cc):
    h = ref.shape[-2] // 2
    return ref.at[(slice(None),) * (len(ref.shape) - 2) + (pl.ds(cc * h, h), slice(None))]


class _Gather:
    def __init__(self, items, out_shapes):
        self.items, self.out_shapes = list(items), tuple(out_shapes)
        self.arrays = [it[0] for it in self.items]
        n = len(self.items)
        self.n_in, self.n_out = n, len(self.out_shapes)
        self.scratch = [pltpu.SemaphoreType.DMA((n * N_CHIPS,)) for _ in range(4)] + [pltpu.SemaphoreType.DMA((n,))]

    def _copies(self, ins, outs, sems, m, cc):
        ici_send, ici_recv, d2d_send, d2d_recv, local_sems = sems
        others = [j for j in range(N_CHIPS) if j != m]
        local, sends, arrivals, passed_on, from_sibling = [], [], [], [], []
        for a, (_, oi, src_of, dst_of) in enumerate(self.items):
            src, out = src_of(ins[a]), outs[oi]
            local.append(pltpu.make_async_copy(src, dst_of(out, m), local_sems.at[a]))
            for j in others:
                k = a * N_CHIPS + j
                mine_there = _half_rows(dst_of(out, m), cc)
                theirs_here = _half_rows(dst_of(out, j), cc)
                sends.append(pltpu.make_async_remote_copy(
                    src_ref=_half_rows(src, cc), dst_ref=mine_there, send_sem=ici_send.at[k],
                    recv_sem=ici_recv.at[a * N_CHIPS + m], device_id=(j // 2, j % 2, cc), device_id_type=MESH))
                arrivals.append(pltpu.make_async_remote_copy(
                    src_ref=_half_rows(src, cc), dst_ref=theirs_here, send_sem=ici_send.at[k], recv_sem=ici_recv.at[k],
                    device_id=(j // 2, j % 2, cc), device_id_type=MESH))
                passed_on.append(pltpu.make_async_remote_copy(
                    src_ref=theirs_here, dst_ref=theirs_here, send_sem=d2d_send.at[k], recv_sem=d2d_recv.at[k],
                    device_id=(m // 2, m % 2, 1 - cc), device_id_type=MESH))
                other_half = _half_rows(dst_of(out, j), 1 - cc)
                from_sibling.append(pltpu.make_async_remote_copy(
                    src_ref=other_half, dst_ref=other_half, send_sem=d2d_send.at[k], recv_sem=d2d_recv.at[k],
                    device_id=(m // 2, m % 2, 1 - cc), device_id_type=MESH))
        return local, sends, arrivals, passed_on, from_sibling

    def _on_my_core(self, fn):
        chip = 2 * lax.axis_index("x") + lax.axis_index("y")
        c = lax.axis_index("c")
        for m in range(N_CHIPS):
            for cc in range(2):
                pl.when((chip == m) & (c == cc))(functools.partial(fn, m, cc))

    def start(self, ins, outs, sems):
        def go(m, cc):
            local, sends, _, _, _ = self._copies(ins, outs, sems, m, cc)
            for cp in local + sends:
                cp.start()
        self._on_my_core(go)

    def wait(self, ins, outs, sems):
        def go(m, cc):
            local, sends, arrivals, passed_on, from_sibling = self._copies(ins, outs, sems, m, cc)
            for arrived, onward in zip(arrivals, passed_on):
                arrived.wait_recv()
                onward.start()
            for cp in from_sibling:
                cp.wait_recv()
            for cp in sends + passed_on:
                cp.wait_send()
            for cp in local:
                cp.wait()
        self._on_my_core(go)


def _run_exchange(ex, name):
    def body(*refs):
        ins, outs, sems = refs[:ex.n_in], refs[ex.n_in:ex.n_in + ex.n_out], refs[ex.n_in + ex.n_out:]
        ex.start(ins, outs, sems)
        ex.wait(ins, outs, sems)

    return pl.pallas_call(
        body, name=name, out_shape=ex.out_shapes, in_specs=[ANY] * ex.n_in, out_specs=(ANY,) * ex.n_out,
        scratch_shapes=ex.scratch, compiler_params=_params())(*ex.arrays)


def _sibling_swap(arrays, name, also):
    n = len(arrays)

    def body(*refs):
        ins, refs = refs[:n], refs[n:]
        x_ins, refs = refs[:also.n_in], refs[also.n_in:]
        outs, refs = refs[:n], refs[n:]
        x_outs, refs = refs[:also.n_out], refs[also.n_out:]
        send_sems, recv_sems, x_sems = refs[0], refs[1], refs[2:]
        peer = (lax.axis_index("x"), lax.axis_index("y"), 1 - lax.axis_index("c"))
        cps = [pltpu.make_async_remote_copy(src_ref=ins[a], dst_ref=outs[a], send_sem=send_sems.at[a],
                                            recv_sem=recv_sems.at[a], device_id=peer, device_id_type=MESH)
               for a in range(n)]
        also.start(x_ins, x_outs, x_sems)
        for cp in cps:
            cp.start()
        also.wait(x_ins, x_outs, x_sems)
        for cp in cps:
            cp.wait()

    return pl.pallas_call(
        body, name=name, out_shape=tuple(SDS(a.shape, a.dtype) for a in arrays) + also.out_shapes,
        in_specs=[ANY] * (n + also.n_in), out_specs=(ANY,) * (n + also.n_out),
        scratch_shapes=[pltpu.SemaphoreType.DMA((n,)), pltpu.SemaphoreType.DMA((n,))] + also.scratch,
        compiler_params=_params())(*arrays, *also.arrays)


def _block_diag(w):
    h, n, m = w.shape
    eye = jnp.eye(h, dtype=w.dtype)
    return (w[:, :, None, :] * eye[:, None, :, None]).reshape(h * n, h * m)


def _diag_blocks(d, h, col0=0, ncols=None, stacked=1):
    ncols = d.shape[1] - col0 if ncols is None else ncols
    n, m = d.shape[0] // (h * stacked), ncols // h
    lanes = 128
    assert m <= lanes and lanes % m == 0 and col0 % lanes == 0

    def body(d_ref, o_ref):
        for gi in range(h * stacked):
            c = col0 + (gi % h) * m
            chunk = d_ref[gi * n:(gi + 1) * n, c // lanes * lanes:c // lanes * lanes + lanes]
            o_ref[gi * n:(gi + 1) * n, :] = chunk[:, c % lanes:c % lanes + m]

    out = pl.pallas_call(body, name="diag_blocks", out_shape=SDS((stacked * h * n, m), d.dtype),
                         compiler_params=_params())(d)
    return out.reshape(stacked * h, n, m)


S5_CHUNKS = 4
S5_PER = S5_GROUPS // S5_CHUNKS
CH_W = S5_PER * S5_CH
ST_W = S5_PER * S5_STATE


def _bd_stack(mats):
    _, _, n, m = mats.shape
    eye = jnp.eye(S5_PER, dtype=mats.dtype)
    t = mats.reshape(2, S5_CHUNKS, S5_PER, n, m)
    bd = t[:, :, :, :, None, :] * eye[None, None, :, None, :, None]
    return bd.reshape(2 * S5_CHUNKS, S5_PER * n, S5_PER * m).astype(MXU_DTYPE)


def _chunks_chunked(src_ref, buf):
    pt = src_ref.shape[0]
    out = []
    for q in range(S5_CHUNKS):
        buf[q] = src_ref[:, q * CH_W:(q + 1) * CH_W]
        out.append(_load_chunked(buf.at[q], 0, pt).astype(MXU_DTYPE))
    return out


def _expand_into(dst_ref, chunks, w_ref):
    for b in range(2 * S5_CHUNKS):
        dst_ref[:, b * ST_W:(b + 1) * ST_W] = jnp.dot(chunks[b % S5_CHUNKS], w_ref[b], preferred_element_type=F32)


def _reduce_from(src_ref, w_ref, buf, dst_ref):
    pt = src_ref.shape[0]
    for q in range(S5_CHUNKS):
        y = jnp.dot(src_ref[:, q * ST_W:(q + 1) * ST_W].astype(MXU_DTYPE), w_ref[q], preferred_element_type=F32)
        p = S5_CHUNKS + q
        y = y + jnp.dot(src_ref[:, p * ST_W:(p + 1) * ST_W].astype(MXU_DTYPE), w_ref[p], preferred_element_type=F32)
        _store_natural(buf.at[q], 0, pt, y)
        dst_ref[:, q * CH_W:(q + 1) * CH_W] = buf[q]


def _s5_fwd(proj, w_bu, w_cx, a_row, d_skip, w_glu, b_glu):
    s = proj.shape[0]
    pt = _scan_tile(s)
    ch2 = 2 * S5_N

    def body(u_ref, dg_ref, wb_ref, wc_ref, a_ref, dk_ref, wg_ref, bg_ref, x_ref, y_ref, o_ref, carry, pw, buf):
        _expand_into(x_ref, _chunks_chunked(u_ref, buf), wb_ref)
        _scan_tile_in_place(a_ref, x_ref, carry, pw, reverse=False)
        _reduce_from(x_ref, wc_ref, buf, y_ref)
        g = jax.nn.gelu(y_ref[...] + dk_ref[...] * u_ref[...])
        t = jnp.dot(g.astype(MXU_DTYPE), wg_ref[...], preferred_element_type=F32) + bg_ref[...]
        o_ref[...] = (g * jax.nn.sigmoid(t) * _silu(dg_ref[...])).astype(MXU_DTYPE)

    return pl.pallas_call(
        body, name="s5_fwd", out_shape=(SDS((s, ch2), F32), SDS((s, BR), F32), SDS((s, BR), MXU_DTYPE)),
        grid=(s // pt,),
        in_specs=[_rows(pt, BR, CB_DU), _rows(pt, BR, CB_DG), _const(w_bu.shape), _const(w_cx.shape),
                  _const((1, ch2)), _const((1, BR)), _const((BR, BR)), _const((1, BR))],
        out_specs=(_rows(pt, ch2), _rows(pt, BR), _rows(pt, BR)),
        scratch_shapes=[pltpu.VMEM((1, ch2), F32), pltpu.VMEM((pt // 8, ch2), F32),
                        pltpu.VMEM((S5_CHUNKS, pt, CH_W), F32)],
        compiler_params=_params(1))(proj, proj, w_bu, w_cx, a_row, d_skip, w_glu, b_glu)


def _s5_core_bwd(dyl, proj, x, w_dx, w_du, a_row):
    s = proj.shape[0]
    pt = _scan_tile(s)
    nt = s // pt
    ch2 = 2 * S5_N
    ch = S5_N

    def body(dy_ref, u_ref, x_ref, xp_ref, wx_ref, wu_ref, a_ref, du_ref, da_ref, dwb_ref, dwc_ref,
             l_ref, carry, pw, buf, buf2):
        i = pl.program_id(0)
        _init_acc(da_ref, dwb_ref, dwc_ref)
        dy_c = _chunks_chunked(dy_ref, buf)
        u_c = _chunks_chunked(u_ref, buf2)
        _expand_into(l_ref, dy_c, wx_ref)
        _scan_tile_in_place(a_ref, l_ref, carry, pw, reverse=True)
        has_prev = (i < nt - 1).astype(F32)
        row = lax.broadcasted_iota(jnp.int32, (8, ch2), 0)
        first = jnp.where(row == 0, pltpu.roll(xp_ref[...], 1, 0) * has_prev, pltpu.roll(x_ref[pt - 8:pt, :], 1, 0))
        xprev = jnp.concatenate([first, x_ref[0:pt - 8, :]], axis=0)
        lr, li, xr, xi = l_ref[:, 0:ch], l_ref[:, ch:ch2], xprev[:, 0:ch], xprev[:, ch:ch2]
        da_ref[:, 0:ch] += _colsum(lr * xr + li * xi)
        da_ref[:, ch:ch2] += _colsum(li * xr - lr * xi)
        _reduce_from(l_ref, wu_ref, buf, du_ref)
        tn = (((0,), (0,)), ((), ()))
        for b in range(2 * S5_CHUNKS):
            cols, rows = slice(b * ST_W, (b + 1) * ST_W), slice(b * CH_W, (b + 1) * CH_W)
            dwb_ref[rows, :] += lax.dot_general(u_c[b % S5_CHUNKS], l_ref[:, cols].astype(MXU_DTYPE), tn,
                                                preferred_element_type=F32)
            dwc_ref[rows, :] += lax.dot_general(dy_c[b % S5_CHUNKS], x_ref[:, cols].astype(MXU_DTYPE), tn,
                                                preferred_element_type=F32)

    rev = lambda w, cb=0: pl.BlockSpec((pt, w), lambda i: (nt - 1 - i, cb))
    halo = pl.BlockSpec((8, ch2), lambda i: (jnp.maximum((nt - 1 - i) * (pt // 8) - 1, 0), 0))
    wshape = SDS((2 * S5_CHUNKS * CH_W, ST_W), F32)
    return pl.pallas_call(
        body, name="s5_core_bwd", out_shape=(SDS((s, BR), F32), SDS((1, ch2), F32), wshape, wshape), grid=(nt,),
        in_specs=[rev(BR, 0), rev(BR, CB_DU), rev(ch2), halo, _const(w_dx.shape), _const(w_du.shape),
                  _const((1, ch2))],
        out_specs=(rev(BR), _const((1, ch2)), _const(wshape.shape), _const(wshape.shape)),
        scratch_shapes=[pltpu.VMEM((pt, ch2), F32), pltpu.VMEM((1, ch2), F32), pltpu.VMEM((pt // 8, ch2), F32),
                        pltpu.VMEM((S5_CHUNKS, pt, CH_W), F32), pltpu.VMEM((S5_CHUNKS, pt, CH_W), F32)],
        compiler_params=_params(1))(dyl, proj, x, x, w_dx, w_du, a_row)


def _tiles(s):
    return dict(tb=min(512, s), tln=min(256, s))


def _layer_weights(p, l):
    pad8 = lambda w: jnp.pad(w, ((0, 8 - w.shape[0]), (0, 0)))
    return dict(
        conv_a=pad8(p["conv_a"][l]), conv_c=pad8(p["conv_c"][l]), conv_c_b=p["conv_c_b"][l][None],
        w_cat=jnp.concatenate([_block_diag(p["lru_wa"][l]), _block_diag(p["lru_wx"][l])], axis=1).astype(MXU_DTYPE),
        b_cat=jnp.concatenate([p["lru_ba"][l], p["lru_bx"][l]])[None], lam=p["lru_lambda"][l][None],
        lam_re=p["s5_lam_re"][l], lam_im=p["s5_lam_im"][l], log_dt=p["s5_log_dt"][l][:, None],
        b_re=p["s5_b_re"][l].reshape(S5_N, S5_CH), b_im=p["s5_b_im"][l].reshape(S5_N, S5_CH),
        c_re=p["s5_c_re"][l], c_im=p["s5_c_im"][l], d_skip=p["s5_d"][l][None], b_glu=p["s5_b_glu"][l][None],
        ln_g=p["ln_g"][l][None], ln_b=p["ln_b"][l][None])


def _s5_matrices(lw):
    ab_re, ab_im, f_re, f_im = _s5_disc_fwd(lw["lam_re"], lw["lam_im"], lw["log_dt"])
    f_re, f_im = f_re.reshape(S5_N, 1), f_im.reshape(S5_N, 1)
    bb_re, bb_im = _s5_bbar_fwd(f_re, f_im, lw["b_re"], lw["b_im"])
    bb = jnp.stack([bb_re, bb_im]).reshape(2, S5_GROUPS, S5_STATE, S5_CH)
    cc = jnp.stack([lw["c_re"], -lw["c_im"]])
    a_row = jnp.concatenate([ab_re.reshape(1, S5_N), ab_im.reshape(1, S5_N)], axis=1)
    return dict(f_re=f_re, f_im=f_im, a_row=a_row, w_bu=_bd_stack(jnp.swapaxes(bb, 2, 3)), w_du=_bd_stack(bb),
                w_cx=_bd_stack(jnp.swapaxes(cc, 2, 3)), w_dx=_bd_stack(cc))


def _mm_hooked(hook, *args, **kw):
    if hook is None:
        return _mm(*args, **kw)
    out = _mm(*args, carry=hook[0], **kw)
    hook[1](out[1:])
    return out[0]


def _layer_fwd(x, h, ada, w_in, get_rest, lw, s5m, bias_tabs, hooks=None, target=None, next_ada=None):
    s = x.shape[0]
    t = _tiles(s)
    tb = t["tb"]
    shift, scale, gate = ada
    hooks = hooks or {}
    if h is None:
        h = _modulate(x, scale, shift, tb)
    proj = _mm_hooked(hooks.get("in_proj"), h, w_in, name="in_proj", tm=1024, tn=1536, tk=D_MODEL)
    w_out, w_glu = get_rest()
    y_a = _branch_a_fwd(proj, lw["conv_a"], tb)
    os_, lses = [], []
    for g, (_, dil) in enumerate(DILATIONS):
        o, lse = _attn_fwd(proj, bias_tabs[g], dil)
        os_.append(o)
        lses.append(lse)
    y_b = _attn_combine(os_, lses, proj, tb)
    lru_a, lru_b = _lru_gates_fwd(proj, lw["conv_c"], lw["conv_c_b"], lw["w_cat"], lw["b_cat"], lw["lam"], tb)
    lru_h, y_c = _lru_scan_fwd(lru_a, lru_b, proj, tb)
    s5_x, ylin, y_d = _s5_fwd(proj, s5m["w_bu"], s5m["w_cx"], s5m["a_row"], lw["d_skip"], w_glu, lw["b_glu"])
    ycat = jnp.concatenate([y_a, y_b, y_c, y_d], axis=1)
    saved = dict(x=x, h=h, proj=proj, os=os_, lses=lses, lru_a=lru_a, lru_h=lru_h, s5_x=s5_x, ylin=ylin, ycat=ycat)
    if target is not None:
        loss, *saved["head"] = _out_ln_loss(ycat, w_out, x, gate, lw["ln_g"], lw["ln_b"], target, t["tln"])
        return loss, None, saved
    x_next, saved["xhat"], saved["y"], saved["rstd"], h_next = _out_ln(
        ycat, w_out, x, gate, lw["ln_g"], lw["ln_b"], next_ada[1], next_ada[0], t["tln"])
    return x_next, h_next, saved


def _layer_bwd(dxn, sv, ada, w_in, w_out, w_glu, lw, s5m, bias_tabs, head_ones, hooks=None):
    proj = sv["proj"]
    s = proj.shape[0]
    t = _tiles(s)
    tb = t["tb"]
    shift, scale, gate = ada
    g = {}
    hook = lambda name: hooks[name](g) if hooks and name in hooks else None
    if "head" in sv:
        dyb, dxa, g["ln_g"], g["ln_b"], dgate = sv["head"]
        dycat = _mm(dyb, w_out, name="dycat", tb=True, tm=1024, tn=1024, tk=D_MODEL)
    else:
        dyb, dxa, g["ln_g"], g["ln_b"], dgate, dycat = _ln_bwd(dxn, sv["xhat"], sv["y"], sv["rstd"], lw["ln_g"], gate,
                                                               w_out, t["tln"])
    g["w_out"] = _mm_hooked(hook("dw_out"), sv["ycat"], dyb, name="dw_out", ta=True, out_dtype=WIRE_DTYPE,
                            tm=1024, tn=1024, tk=2048)
    da, dconv_a = _branch_a_bwd(dycat, proj, lw["conv_a"], tb)
    g["conv_a"] = dconv_a[0:3]
    pre = _attn_bwd_pre(dycat, sv["os"], sv["lses"], proj, head_ones, tb)
    dbg, dos, dms = pre[0], pre[1:4], pre[4:7]
    parts, dbias = [], []
    for gi, (_, dil) in enumerate(DILATIONS):
        hk = hook(f"attn_bwd_d{dil}")
        last = gi == len(DILATIONS) - 1
        dq, dk, dv, dbi, *got = _attn_bwd(proj, dos[gi], sv["lses"][gi], dms[gi], bias_tabs[gi], dil,
                                          carry=hk and hk[0], add=tuple(parts) if last else ())
        if hk:
            hk[1](got)
        parts.append((dq, dk, dv))
        dbias.append(dbi)
    dqkv = parts[-1]
    lmb, dcg = _lru_scan_bwd(sv["lru_a"], dycat, sv["lru_h"], proj, tb)
    dxc, dpre, xcb, dbcat, dlam = _lru_gates_bwd(proj, lmb, sv["lru_h"], lw["conv_c"], lw["conv_c_b"], lw["w_cat"],
                                                  lw["b_cat"], lw["lam"], tb)
    dwcat = _mm(xcb, dpre, name="dw_lru", ta=True, tn=1024)
    g["lru_wa"] = _diag_blocks(dwcat, LRU_HEADS, 0, BR)
    g["lru_wx"] = _diag_blocks(dwcat, LRU_HEADS, BR, BR)
    g["lru_ba"], g["lru_bx"], g["lru_lambda"] = dbcat[0, 0:BR], dbcat[0, BR:2 * BR], dlam[0]
    dcx, dconv_c, dccb = _conv_c_bwd(dxc, proj, lw["conv_c"], tb)
    g["conv_c"], g["conv_c_b"] = dconv_c[0:4], dccb[0]
    dyl, dus, ddg, gb, dtb, ddk, dbglu = _s5_tail_bwd(dycat, sv["ylin"], proj, lw["d_skip"], w_glu, lw["b_glu"], tb)
    g["s5_d"], g["s5_b_glu"] = ddk[0], dbglu[0]
    g["s5_w_glu"] = _mm(gb, dtb, name="dw_glu", ta=True, out_dtype=WIRE_DTYPE)
    du, dab, dwb8, dwc8 = _s5_core_bwd(dyl, proj, sv["s5_x"], s5m["w_dx"], s5m["w_du"], s5m["a_row"])
    per_group = lambda d8: _diag_blocks(d8, S5_PER, stacked=2 * S5_CHUNKS).reshape(2, S5_GROUPS, S5_CH, S5_STATE)
    dbb, dcc = per_group(dwb8), per_group(dwc8)
    from_bd = lambda half: jnp.swapaxes(dbb[half], 1, 2).reshape(S5_N, S5_CH)
    df_re, df_im, db_re, db_im = _s5_bbar_bwd(s5m["f_re"], s5m["f_im"], lw["b_re"], lw["b_im"],
                                              from_bd(0), from_bd(1))
    shp = (S5_GROUPS, S5_STATE)
    g["s5_lam_re"], g["s5_lam_im"], dlog_dt = _s5_disc_bwd(
        lw["lam_re"], lw["lam_im"], lw["log_dt"],
        (dab[:, 0:S5_N].reshape(shp), dab[:, S5_N:].reshape(shp), df_re.reshape(shp), df_im.reshape(shp)))
    g["s5_log_dt"] = dlog_dt[:, 0]
    g["s5_b_re"] = db_re.reshape(S5_GROUPS, S5_STATE, S5_CH)
    g["s5_b_im"] = db_im.reshape(S5_GROUPS, S5_STATE, S5_CH)
    g["s5_c_re"], g["s5_c_im"] = dcc[0], -dcc[1]
    dproj = _assemble_dproj(da, dqkv, dbg, dcx, dcg, du, dus, ddg, tb)
    g["w_in"] = _mm_hooked(hook("dw_in"), sv["h"], dproj, name="dw_in", ta=True, out_dtype=WIRE_DTYPE,
                           tm=1024, tn=1536, tk=2048)
    hk = hook("dh")
    dx, dshift, dscale, *got = _dh_mod_bwd(dproj, w_in, dxa, sv["x"], scale, carry=hk and hk[0])
    if hk:
        hk[1](got)
    g["ada"] = jnp.concatenate([dshift[0], dscale[0], dgate[0]])
    return dx, g, dbias


SMALL = ("rel_bias", "conv_a", "conv_c", "conv_c_b", "lru_wa", "lru_ba", "lru_wx", "lru_bx", "lru_lambda",
         "s5_lam_re", "s5_lam_im", "s5_log_dt", "s5_b_re", "s5_b_im", "s5_c_re", "s5_c_im", "s5_d", "s5_b_glu",
         "ln_g", "ln_b")
PER_LAYER_SMALL = SMALL[1:]


def _local_step(x, target, ada_rows, w_in, w_out, w_glu, p, comm=None):
    if comm is None:
        get_w_in = lambda l: w_in[l]
        get_rest = lambda l: (w_out[l], w_glu[l])
        fwd_hooks = bwd_hooks = lambda *_: None
    else:
        get_w_in, get_rest, fwd_hooks, bwd_hooks = comm.w_in, comm.rest, comm.fwd_hooks, comm.bwd_hooks
    s = x.shape[0]
    buckets = _bucket_maps()
    bias_tabs = _bias_tables(p["rel_bias"], buckets)
    head_ones = _block_diag(jnp.ones((ATT_HEADS, HEAD_DIM, HEAD_DIM), MXU_DTYPE))
    lws = [_layer_weights(p, l) for l in range(DEPTH)]
    s5ms = [_s5_matrices(lw) for lw in lws]
    adas = [tuple(ada_rows[l, k * D_MODEL:(k + 1) * D_MODEL][None] for k in range(3)) for l in range(DEPTH)]
    saved, h = [], None
    for l in range(DEPTH):
        last = l == DEPTH - 1
        x, h, sv = _layer_fwd(x, h, adas[l], get_w_in(l), functools.partial(get_rest, l), lws[l], s5ms[l], bias_tabs,
                              fwd_hooks(l), target if last else None, None if last else adas[l + 1])
        saved.append(sv)
    loss, dx = x, None
    grads = [None] * DEPTH
    dbias_sum = []
    for l in reversed(range(DEPTH)):
        dx, grads[l], dbias = _layer_bwd(dx, saved[l], adas[l], get_w_in(l), *get_rest(l), lws[l], s5ms[l],
                                         bias_tabs, head_ones, bwd_hooks(l, grads))
        dbias_sum.append(jnp.stack(dbias))
    drel = _rel_bias_grad(jnp.stack(dbias_sum), buckets)[:, 0:ATT_HEADS]
    small = {n: jnp.stack([grads[l][n] for l in range(DEPTH)]) for n in PER_LAYER_SMALL + ("ada",)}
    small["rel_bias"] = drel
    big = {n: [grads[l][n] for l in range(DEPTH)] for n in ("w_in", "w_out", "s5_w_glu")}
    return loss, dx, big, small


PACK_ROWS = 256


def _pack(parts):
    flat = jnp.concatenate([t.reshape(-1).astype(F32) for t in parts])
    n = flat.shape[0]
    rows = -(-n // (PACK_ROWS * 128)) * PACK_ROWS
    return jnp.pad(flat, (0, rows * 128 - n)).reshape(rows, 128)


def _unpack(packed, shapes):
    flat = packed.reshape(packed.shape[:-2] + (-1,))
    out, off = [], 0
    for shp in shapes:
        size = math.prod(shp)
        out.append(flat[..., off:off + size].reshape(flat.shape[:-1] + tuple(shp)))
        off += size
    return out


def _take_cols(t, chip, width):
    return lax.dynamic_slice_in_dim(t, chip * width, width, axis=t.ndim - 1)


class _Comm:
    IN_W, OUT_R, GLU_R = N_IN // N_CHIPS, D_MODEL // N_CHIPS, BR // N_CHIPS

    def __init__(self, w_in_b, w_out_b, w_glu_b):
        assert DEPTH == 2
        self.shards = (w_in_b, w_out_b, w_glu_b)
        in_w = self.IN_W
        self.w_in_full = {0: _run_exchange(_Gather(
            [(w_in_b, 0, lambda ref: ref.at[0], lambda ref, j: ref.at[:, pl.ds(j * in_w, in_w)])],
            [SDS((D_MODEL, N_IN), WIRE_DTYPE)]), "gather_w_in0")[0]}
        self.w_out_full = self.w_glu_full = None
        self.recv = {}

    def w_in(self, l):
        return self.w_in_full[l]

    def rest(self, l):
        return self.w_out_full[l], self.w_glu_full[l]

    def fwd_hooks(self, l):
        if l != 0:
            return None
        w_in_b, w_out_b, w_glu_b = self.shards
        in_w, out_r, glu_r = self.IN_W, self.OUT_R, self.GLU_R
        whole = lambda ref: ref
        items = [(w_out_b, 0, whole, lambda ref, j: ref.at[:, pl.ds(j * out_r, out_r), :]),
                 (w_glu_b, 1, whole, lambda ref, j: ref.at[:, pl.ds(j * glu_r, glu_r), :]),
                 (w_in_b, 2, lambda ref: ref.at[1], lambda ref, j: ref.at[:, pl.ds(j * in_w, in_w)])]
        shapes = [SDS((DEPTH, D_MODEL, D_MODEL), WIRE_DTYPE), SDS((DEPTH, BR, BR), WIRE_DTYPE),
                  SDS((D_MODEL, N_IN), WIRE_DTYPE)]

        def done(got):
            self.w_out_full, self.w_glu_full, self.w_in_full[1] = got

        return {"in_proj": (_Gather(items, shapes), done)}

    W_IN_ROWS = ((0, 1024), (1024, 512), (1536, 512))

    def _scatter(self, parts):
        in_w, out_r, glu_r = self.IN_W, self.OUT_R, self.GLU_R
        items, shapes, keys = [], [], []
        for oi, (name, l, arr, *rows) in enumerate(parts):
            if name == "w_in":
                r0, nr = rows[0] if rows else (0, D_MODEL)
                cut = functools.partial(lambda ref, j, r0, nr: ref.at[pl.ds(r0, nr), pl.ds(j * in_w, in_w)], r0=r0, nr=nr)
                shard = (nr, in_w)
            elif name == "w_out":
                cut, shard = (lambda ref, j: ref.at[pl.ds(j * out_r, out_r), :]), (out_r, D_MODEL)
            else:
                cut, shard = (lambda ref, j: ref.at[pl.ds(j * glu_r, glu_r), :]), (glu_r, BR)
            items.append((arr, oi, cut, lambda ref, j: ref.at[j]))
            shapes.append(SDS((N_CHIPS,) + shard, WIRE_DTYPE))
            keys.append((name, l) + ((rows[0][0],) if rows else ()))

        def done(got):
            self.recv.update(zip(keys, got))

        return _Exchange(items, shapes), done

    def received(self, name):
        return [self.recv[k] for k in sorted(k for k in self.recv if k[0] == name)]

    def bwd_hooks(self, l, grads):
        if l != 0:
            return None
        g1 = grads[1]
        w_in_part = lambda k: (lambda g: self._scatter([("w_in", 1, g1["w_in"], self.W_IN_ROWS[k])]))
        return {"dw_out": lambda g: self._scatter([("w_out", 1, g1["w_out"]), ("s5_w_glu", 1, g1["s5_w_glu"])]),
                "attn_bwd_d16": w_in_part(0), "attn_bwd_d4": w_in_part(1), "attn_bwd_d1": w_in_part(2),
                "dw_in": lambda g: self._scatter([("w_out", 0, g["w_out"]), ("s5_w_glu", 0, g["s5_w_glu"])]),
                "dh": lambda g: self._scatter([("w_in", 0, g["w_in"])])}


def kernel(x, c, rel_bias, w_ada, b_ada, w_in, conv_a, conv_c, conv_c_b, lru_wa, lru_ba, lru_wx, lru_bx, lru_lambda, s5_lam_re, s5_lam_im, s5_log_dt, s5_b_re, s5_b_im, s5_c_re, s5_c_im, s5_d, s5_w_glu, s5_b_glu, w_out, ln_g, ln_b, loss_target, m_rel_bias, m_w_ada, m_b_ada, m_w_in, m_conv_a, m_conv_c, m_conv_c_b, m_lru_wa, m_lru_ba, m_lru_wx, m_lru_bx, m_lru_lambda, m_s5_lam_re, m_s5_lam_im, m_s5_log_dt, m_s5_b_re, m_s5_b_im, m_s5_c_re, m_s5_c_im, m_s5_d, m_s5_w_glu, m_s5_b_glu, m_w_out, m_ln_g, m_ln_b, v_rel_bias, v_w_ada, v_b_ada, v_w_in, v_conv_a, v_conv_c, v_conv_c_b, v_lru_wa, v_lru_ba, v_lru_wx, v_lru_bx, v_lru_lambda, v_s5_lam_re, v_s5_lam_im, v_s5_log_dt, v_s5_b_re, v_s5_b_im, v_s5_c_re, v_s5_c_im, v_s5_d, v_s5_w_glu, v_s5_b_glu, v_w_out, v_ln_g, v_ln_b):
    args = dict(locals())
    names = ("rel_bias", "w_ada", "b_ada", "w_in", "conv_a", "conv_c", "conv_c_b", "lru_wa", "lru_ba", "lru_wx",
             "lru_bx", "lru_lambda", "s5_lam_re", "s5_lam_im", "s5_log_dt", "s5_b_re", "s5_b_im", "s5_c_re", "s5_c_im",
             "s5_d", "s5_w_glu", "s5_b_glu", "w_out", "ln_g", "ln_b")
    w = {n: args[n] for n in names}
    mom = {n: args["m_" + n] for n in names}
    var = {n: args["v_" + n] for n in names}
    chip = 2 * lax.axis_index("x") + lax.axis_index("y")
    me = 2 * chip + lax.axis_index("c")
    ada_w = 3 * D_MODEL // N_CHIPS
    conv_w = BR // N_CHIPS

    comm = _Comm(w["w_in"].astype(WIRE_DTYPE), w["w_out"].astype(WIRE_DTYPE), w["s5_w_glu"].astype(WIRE_DTYPE))

    taps = jnp.concatenate([w["conv_a"].reshape(DEPTH * 3, conv_w), w["conv_c"].reshape(DEPTH * 4, conv_w)])
    first = jnp.concatenate([c, jnp.pad(taps, ((0, 1), (0, D_MODEL - conv_w)))])
    got = _allgather8(first, "gather_c_taps").reshape(N_CHIPS, 2, 16, D_MODEL)
    c_all = got[:, :, 0].reshape(N_DEV, D_MODEL)
    taps_all = jnp.transpose(got[:, 0, 1:1 + DEPTH * 7, 0:conv_w], (1, 0, 2)).reshape(DEPTH * 7, BR)
    conv_a_f = taps_all[0:DEPTH * 3].reshape(DEPTH, 3, BR)
    conv_c_f = taps_all[DEPTH * 3:].reshape(DEPTH, 4, BR)

    cond_all = _silu_rows(c_all)
    ada_part = jnp.stack([_mm(cond_all, w["w_ada"][l], name="ada_fwd", tk=D_MODEL, tn=512,
                              bias=_take_cols(w["b_ada"][l][None], chip, ada_w)) for l in range(DEPTH)])
    ada_all = _allgather8(ada_part.reshape(DEPTH * N_DEV, ada_w), "gather_ada")
    ada_all = ada_all.reshape(N_CHIPS, 2, DEPTH, N_DEV, ada_w)[:, 0]
    ada_rows = lax.dynamic_index_in_dim(ada_all, me, axis=2, keepdims=False)
    ada_rows = jnp.transpose(ada_rows, (1, 0, 2)).reshape(DEPTH, 3 * D_MODEL)

    p = dict(w)
    p["conv_a"], p["conv_c"] = conv_a_f, conv_c_f
    loss, dx, _, small = _local_step(x[0], loss_target[0], ada_rows, None, None, None, p, comm)

    sums = [_sum_leading(comm.received(name), 256, "sum_chips") for name in ("w_in", "w_out", "s5_w_glu")]
    small_names = SMALL + ("ada",)
    small["loss"] = loss
    order = small_names + ("loss",)
    shapes = [small[n].shape for n in order]
    *others, gathered = _sibling_swap(sums, "swap_cores", _AllGather8(_pack([small[n] for n in order])))
    out = {}
    for name, mine, other in zip(("w_in", "w_out", "s5_w_glu"), sums, others):
        shp = w[name].shape
        flat = lambda t: t.reshape(-1, shp[-1])
        res = _adamw(flat(w[name]), [mine, other], flat(mom[name]), flat(var[name]), 128, "adamw_big")
        out[name] = [t.reshape(shp) for t in res]
    gathered = gathered.reshape(N_DEV, -1, 128)
    total = dict(zip(order, _unpack(_sum_leading([gathered], PACK_ROWS, "sum_devices"), shapes)))
    d_ada_all = _unpack(gathered, shapes)[order.index("ada")]
    g_small = {n: total[n] for n in SMALL}
    g_small["conv_a"] = _take_cols(total["conv_a"], chip, conv_w)
    g_small["conv_c"] = _take_cols(total["conv_c"], chip, conv_w)
    g_small["b_ada"] = total["ada"]
    g_w_ada = jnp.stack([_mm(cond_all, _take_cols(d_ada_all[:, l], chip, ada_w), name="dw_ada", ta=True, tn=ada_w)
                         for l in range(DEPTH)])
    upd_names = SMALL + ("b_ada",)
    upd_shapes = [w[n].shape for n in upd_names]
    res = _adamw(_pack([w[n] for n in upd_names]), [_pack([g_small[n] for n in upd_names])],
                 _pack([mom[n] for n in upd_names]), _pack([var[n] for n in upd_names]), PACK_ROWS, "adamw_small")
    for k, t in enumerate(res):
        for n, val in zip(upd_names, _unpack(t, upd_shapes)):
            out.setdefault(n, [None] * 4)[k] = val
    shp = w["w_ada"].shape
    flat = lambda t: t.reshape(-1, shp[-1])
    out["w_ada"] = [t.reshape(shp) for t in _adamw(flat(w["w_ada"]), [flat(g_w_ada)], flat(mom["w_ada"]),
                                                  flat(var["w_ada"]), 128, "adamw_ada")]
    return (total["loss"].reshape(()), dx[None]) + tuple(out[n][k] for k in range(4) for n in names)
```

```python
import functools
import math

import jax
import jax.numpy as jnp
from jax import lax
from jax.experimental import pallas as pl
from jax.experimental.pallas import tpu as pltpu

F32 = jnp.float32
MXU_DTYPE = jnp.bfloat16
WIRE_DTYPE = jnp.bfloat16
SDS = jax.ShapeDtypeStruct
MESH = pl.DeviceIdType.MESH
ANY = pl.BlockSpec(memory_space=pl.ANY)
VMEM_LIMIT = 48 * 1024 * 1024

D_MODEL = 2048
DEPTH = 2
BR = 512
ATT_HEADS = 8
HEAD_DIM = 64
DILATIONS = ((128, 1), (512, 4), (2048, 16))
BLK = 128
REL_BUCKETS = 32
REL_MAX_DIST = 2048
LRU_HEADS = 8
LRU_C = 8.0
S5_CH = 16
S5_GROUPS = 32
S5_STATE = 64
S5_N = S5_GROUPS * S5_STATE
N_IN = 12 * BR
ALPHA = (2 * DEPTH) ** 0.25
LN_EPS = 1e-5
NEG = -1e30
ADAM_LR, ADAM_B1, ADAM_B2, ADAM_EPS, ADAM_WD, ADAM_STEP = 0.001, 0.9, 0.999, 1e-08, 0.01, 10
CB_AB, CB_AC, CB_AX, CB_AG, CB_Q, CB_K, CB_V, CB_BG, CB_CX, CB_CG, CB_DU, CB_DG = range(12)
N_CHIPS = 4
N_DEV = 8


def _params(n_axes=0):
    kw = {"dimension_semantics": ("arbitrary",) * n_axes} if n_axes else {}
    return pltpu.CompilerParams(vmem_limit_bytes=VMEM_LIMIT, **kw)


def _rows(tb, w, cb=0):
    return pl.BlockSpec((tb, w), lambda i: (i, cb))


def _prev8(tb, w, cb=0):
    return pl.BlockSpec((8, w), lambda i: (jnp.maximum(i * (tb // 8) - 1, 0), cb))


def _next8(tb, w, n_rows, cb=0):
    return pl.BlockSpec((8, w), lambda i: (jnp.minimum((i + 1) * (tb // 8), n_rows // 8 - 1), cb))


def _const(shape):
    return pl.BlockSpec(shape, lambda *_: (0,) * len(shape))


def _silu(x):
    return x * jax.nn.sigmoid(x)


def _dsilu(x):
    s = jax.nn.sigmoid(x)
    return s * (1.0 + x * (1.0 - s))


def _shift_down(cur, prev8, j):
    rolled = pltpu.roll(cur, j, 0)
    row = lax.broadcasted_iota(jnp.int32, (8, cur.shape[1]), 0)
    first = jnp.where(row < j, pltpu.roll(prev8, j, 0), rolled[0:8])
    return jnp.concatenate([first, rolled[8:]], axis=0)


def _shift_up(cur, next8, j):
    t = cur.shape[0]
    rolled = pltpu.roll(cur, t - j, 0)
    row = lax.broadcasted_iota(jnp.int32, (8, cur.shape[1]), 0)
    last = jnp.where(row >= 8 - j, pltpu.roll(next8, 8 - j, 0), rolled[t - 8:t])
    return jnp.concatenate([rolled[:t - 8], last], axis=0)


def _colsum(x):
    return jnp.sum(x, axis=0, keepdims=True)


def _init_acc(*refs):
    @pl.when(pl.program_id(0) == 0)
    def _():
        for r in refs:
            r[...] = jnp.zeros_like(r)


def _call(body, *, name, out_shape, grid, in_specs, out_specs, scratch_shapes, args, carry=None):
    out_shape, out_specs, in_specs = tuple(out_shape), tuple(out_specs), list(in_specs)
    scratch_shapes = list(scratch_shapes)
    if carry is None:
        return pl.pallas_call(body, name=name, out_shape=out_shape, grid=grid, in_specs=in_specs, out_specs=out_specs,
                              scratch_shapes=scratch_shapes, compiler_params=_params(len(grid)))(*args)
    n_in, n_out, n_scr = len(in_specs), len(out_shape), len(scratch_shapes)

    def wrapped(*refs):
        ins, refs = refs[:n_in], refs[n_in:]
        x_ins, refs = refs[:carry.n_in], refs[carry.n_in:]
        outs, refs = refs[:n_out], refs[n_out:]
        x_outs, refs = refs[:carry.n_out], refs[carry.n_out:]
        scr, x_sems = refs[:n_scr], refs[n_scr:]
        at = [pl.program_id(d) for d in range(len(grid))]
        first = functools.reduce(lambda p, q: p & q, [i == 0 for i in at])
        last = functools.reduce(lambda p, q: p & q, [i == g - 1 for i, g in zip(at, grid)])
        pl.when(first)(lambda: carry.start(x_ins, x_outs, x_sems))
        body(*ins, *outs, *scr)
        pl.when(last)(lambda: carry.wait(x_ins, x_outs, x_sems))

    return pl.pallas_call(
        wrapped, name=name, out_shape=out_shape + carry.out_shapes, grid=grid, in_specs=in_specs + [ANY] * carry.n_in,
        out_specs=out_specs + (ANY,) * carry.n_out, scratch_shapes=scratch_shapes + carry.scratch,
        compiler_params=_params(len(grid)))(*args, *carry.arrays)


def _mm(a, b, *, name, ta=False, tb=False, out_dtype=F32, tm=512, tn=512, tk=512, bias=None, carry=None):
    m, k = (a.shape[1], a.shape[0]) if ta else a.shape
    n = b.shape[0] if tb else b.shape[1]
    assert k == (b.shape[1] if tb else b.shape[0]), (name, a.shape, b.shape)
    tm, tn, tk = min(tm, m), min(tn, n), min(tk, k)
    nk = k // tk
    assert m % tm == 0 and n % tn == 0 and k % tk == 0, (name, m, n, k)

    def body(*refs):
        if bias is None:
            a_ref, b_ref, o_ref, acc = refs
        else:
            a_ref, b_ref, bias_ref, o_ref, acc = refs
        kk = pl.program_id(2)

        @pl.when(kk == 0)
        def _():
            acc[...] = jnp.zeros_like(acc)

        dims = (((0 if ta else 1,), (1 if tb else 0,)), ((), ()))
        acc[...] += lax.dot_general(a_ref[...].astype(MXU_DTYPE), b_ref[...].astype(MXU_DTYPE), dims,
                                    preferred_element_type=F32)

        @pl.when(kk == nk - 1)
        def _():
            r = acc[...]
            if bias is not None:
                r = r + bias_ref[...]
            o_ref[...] = r.astype(out_dtype)

    a_spec = (pl.BlockSpec((tk, tm), lambda i, j, kk: (kk, i)) if ta
              else pl.BlockSpec((tm, tk), lambda i, j, kk: (i, kk)))
    b_spec = (pl.BlockSpec((tn, tk), lambda i, j, kk: (j, kk)) if tb
              else pl.BlockSpec((tk, tn), lambda i, j, kk: (kk, j)))
    in_specs, args = [a_spec, b_spec], [a, b]
    if bias is not None:
        in_specs.append(pl.BlockSpec((1, tn), lambda i, j, kk: (0, j)))
        args.append(bias)
    out = _call(body, name=name, out_shape=[SDS((m, n), out_dtype)], grid=(m // tm, n // tn, nk), in_specs=in_specs,
                out_specs=[pl.BlockSpec((tm, tn), lambda i, j, kk: (i, j))],
                scratch_shapes=[pltpu.VMEM((tm, tn), F32)], args=args, carry=carry)
    return out[0] if carry is None else out


def _silu_rows(c_all):
    def body(c_ref, o_ref):
        o_ref[...] = _silu(c_ref[...])
    return pl.pallas_call(body, name="cond_silu", out_shape=SDS(c_all.shape, F32))(c_all)


def _modulate(x, scale, shift, tb):
    s, d = x.shape

    def body(x_ref, sc_ref, sh_ref, o_ref):
        o_ref[...] = (x_ref[...] * (1.0 + sc_ref[...]) + sh_ref[...]).astype(MXU_DTYPE)

    return pl.pallas_call(body, name="modulate", out_shape=SDS((s, d), MXU_DTYPE), grid=(s // tb,),
                          in_specs=[_rows(tb, d), _const((1, d)), _const((1, d))], out_specs=_rows(tb, d),
                          compiler_params=_params(1))(x, scale, shift)


def _out_ln(ycat, w_out, x, gate, ln_g, ln_b, next_scale, next_shift, tb):
    s, d = x.shape

    def body(yc_ref, w_ref, x_ref, gt_ref, g_ref, b_ref, sc_ref, sh_ref, xn_ref, xh_ref, y_ref, rs_ref, hn_ref):
        y = jnp.dot(yc_ref[...], w_ref[...], preferred_element_type=F32)
        res = ALPHA * x_ref[...] + (1.0 + gt_ref[...]) * y
        mu = jnp.mean(res, axis=-1, keepdims=True)
        cen = res - mu
        var = jnp.mean(cen * cen, axis=-1, keepdims=True)
        rstd = lax.rsqrt(var + LN_EPS)
        xhat = cen * rstd
        xn = xhat * g_ref[...] + b_ref[...]
        xn_ref[...] = xn
        xh_ref[...] = xhat
        y_ref[...] = y
        rs_ref[...] = rstd
        hn_ref[...] = (xn * (1.0 + sc_ref[...]) + sh_ref[...]).astype(MXU_DTYPE)

    big = SDS((s, d), F32)
    return pl.pallas_call(
        body, name="out_proj_ln", out_shape=(big, big, big, SDS((s, 1), F32), SDS((s, d), MXU_DTYPE)), grid=(s // tb,),
        in_specs=[_rows(tb, d), pl.BlockSpec((d, d), lambda i: (0, 0), pipeline_mode=pl.Buffered(1)), _rows(tb, d)]
        + [_const((1, d))] * 5,
        out_specs=(_rows(tb, d), _rows(tb, d), _rows(tb, d), _rows(tb, 1), _rows(tb, d)), compiler_params=_params(1),
    )(ycat, w_out, x, gate, ln_g, ln_b, next_scale, next_shift)


def _ln_bwd(dxn, xhat, y, rstd, ln_g, gate, w_out, tb):
    s, d = dxn.shape

    def body(dxn_ref, xh_ref, y_ref, rs_ref, g_ref, gt_ref, w_ref, dy_ref, dxa_ref, dg_ref, db_ref, dgt_ref, dyc_ref):
        _init_acc(dg_ref, db_ref, dgt_ref)
        dxn_t, xh = dxn_ref[...], xh_ref[...]
        dxh = dxn_t * g_ref[...]
        dres = rs_ref[...] * (dxh - jnp.mean(dxh, axis=-1, keepdims=True)
                              - xh * jnp.mean(dxh * xh, axis=-1, keepdims=True))
        dyb = ((1.0 + gt_ref[...]) * dres).astype(MXU_DTYPE)
        dy_ref[...] = dyb
        dxa_ref[...] = ALPHA * dres
        dg_ref[...] += _colsum(dxn_t * xh)
        db_ref[...] += _colsum(dxn_t)
        dgt_ref[...] += _colsum(dres * y_ref[...])
        dyc_ref[...] = lax.dot_general(dyb, w_ref[...], (((1,), (1,)), ((), ())), preferred_element_type=F32)

    vec = SDS((1, d), F32)
    return pl.pallas_call(
        body, name="ln_bwd_dycat", out_shape=(SDS((s, d), MXU_DTYPE), SDS((s, d), F32), vec, vec, vec, SDS((s, d), F32)),
        grid=(s // tb,),
        in_specs=[_rows(tb, d), _rows(tb, d), _rows(tb, d), _rows(tb, 1), _const((1, d)), _const((1, d)),
                  pl.BlockSpec((d, d), lambda i: (0, 0), pipeline_mode=pl.Buffered(1))],
        out_specs=(_rows(tb, d), _rows(tb, d), _const((1, d)), _const((1, d)), _const((1, d)), _rows(tb, d)),
        compiler_params=_params(1))(dxn, xhat, y, rstd, ln_g, gate, w_out)


def _dh_mod_bwd(dproj, w_in, dxa, x, scale, carry=None):
    s, d = dxa.shape
    k = dproj.shape[1]
    tm, tn, tk = min(1024, s), 1024, 1536
    nk = k // tk
    assert s % tm == 0 and d % tn == 0 and k % tk == 0

    def body(a_ref, b_ref, dxa_ref, x_ref, sc_ref, dx_ref, dsh_ref, dsc_ref, acc):
        i, kk = pl.program_id(1), pl.program_id(2)

        @pl.when(kk == 0)
        def _():
            acc[...] = jnp.zeros_like(acc)

        @pl.when((kk == 0) & (i == 0))
        def _():
            dsh_ref[...] = jnp.zeros_like(dsh_ref)
            dsc_ref[...] = jnp.zeros_like(dsc_ref)

        acc[...] += lax.dot_general(a_ref[...], b_ref[...], (((1,), (1,)), ((), ())), preferred_element_type=F32)

        @pl.when(kk == nk - 1)
        def _():
            dh_t = acc[...]
            dx_ref[...] = dxa_ref[...] + dh_t * (1.0 + sc_ref[...])
            dsh_ref[...] += _colsum(dh_t)
            dsc_ref[...] += _colsum(dh_t * x_ref[...])

    tile = pl.BlockSpec((tm, tn), lambda j, i, kk: (i, j))
    vec = pl.BlockSpec((1, tn), lambda j, i, kk: (0, j))
    return _call(
        body, name="dh", out_shape=(SDS((s, d), F32), SDS((1, d), F32), SDS((1, d), F32)),
        grid=(d // tn, s // tm, nk),
        in_specs=[pl.BlockSpec((tm, tk), lambda j, i, kk: (i, kk)), pl.BlockSpec((tn, tk), lambda j, i, kk: (j, kk)),
                  tile, tile, vec],
        out_specs=(tile, vec, vec), scratch_shapes=[pltpu.VMEM((tm, tn), F32)],
        args=(dproj, w_in, dxa, x, scale), carry=carry)


def _out_ln_loss(ycat, w_out, x, gate, ln_g, ln_b, target, tb):
    s, d = x.shape

    def body(yc_ref, w_ref, x_ref, gt_ref, g_ref, b_ref, t_ref, l_ref, dy_ref, dxa_ref, dg_ref, db_ref, dgt_ref):
        _init_acc(l_ref, dg_ref, db_ref, dgt_ref)
        y = jnp.dot(yc_ref[...], w_ref[...], preferred_element_type=F32)
        res = ALPHA * x_ref[...] + (1.0 + gt_ref[...]) * y
        cen = res - jnp.mean(res, axis=-1, keepdims=True)
        rstd = lax.rsqrt(jnp.mean(cen * cen, axis=-1, keepdims=True) + LN_EPS)
        xh = cen * rstd
        err = xh * g_ref[...] + b_ref[...] - t_ref[...]
        l_ref[...] += (0.5 / d) * jnp.sum(err * err, keepdims=True)
        dxn_t = err * (1.0 / d)
        dxh = dxn_t * g_ref[...]
        dres = rstd * (dxh - jnp.mean(dxh, axis=-1, keepdims=True) - xh * jnp.mean(dxh * xh, axis=-1, keepdims=True))
        dy_ref[...] = ((1.0 + gt_ref[...]) * dres).astype(MXU_DTYPE)
        dxa_ref[...] = ALPHA * dres
        dg_ref[...] += _colsum(dxn_t * xh)
        db_ref[...] += _colsum(dxn_t)
        dgt_ref[...] += _colsum(dres * y)

    vec = SDS((1, d), F32)
    return pl.pallas_call(
        body, name="out_proj_ln_loss", out_shape=(SDS((1, 1), F32), SDS((s, d), MXU_DTYPE), SDS((s, d), F32), vec, vec, vec),
        grid=(s // tb,),
        in_specs=[_rows(tb, d), pl.BlockSpec((d, d), lambda i: (0, 0), pipeline_mode=pl.Buffered(1)), _rows(tb, d),
                  _const((1, d)), _const((1, d)), _const((1, d)), _rows(tb, d)],
        out_specs=(_const((1, 1)), _rows(tb, d), _rows(tb, d), _const((1, d)), _const((1, d)), _const((1, d))),
        compiler_params=_params(1))(ycat, w_out, x, gate, ln_g, ln_b, target)


def _conv_taps(u, up, w_ref, width):
    out = w_ref[width - 1:width, :] * u
    for j in range(width - 2, -1, -1):
        out = out + w_ref[j:j + 1, :] * _shift_down(u, up, width - 1 - j)
    return out


def _conv_taps_t(g, gn, w_ref, width):
    out = w_ref[width - 1:width, :] * g
    for j in range(width - 2, -1, -1):
        out = out + w_ref[j:j + 1, :] * _shift_up(g, gn, width - 1 - j)
    return out


def _conv_wgrad(dw_ref, g, u, up, width):
    dw_ref[width - 1:width, :] += _colsum(g * u)
    for j in range(width - 1):
        dw_ref[j:j + 1, :] += _colsum(g * _shift_down(u, up, width - 1 - j))


def _branch_a_fwd(proj, conv_w, tb):
    s = proj.shape[0]

    def body(ab, ac, ax, ag, acp, axp, w_ref, o_ref):
        has_prev = (pl.program_id(0) > 0).astype(F32)
        u = ac[...] * ax[...]
        up = acp[...] * axp[...] * has_prev
        o_ref[...] = (ab[...] * _conv_taps(u, up, w_ref, 3) * _silu(ag[...])).astype(MXU_DTYPE)

    return pl.pallas_call(
        body, name="branch_a_fwd", out_shape=SDS((s, BR), MXU_DTYPE), grid=(s // tb,),
        in_specs=[_rows(tb, BR, CB_AB), _rows(tb, BR, CB_AC), _rows(tb, BR, CB_AX), _rows(tb, BR, CB_AG),
                  _prev8(tb, BR, CB_AC), _prev8(tb, BR, CB_AX), _const((8, BR))],
        out_specs=_rows(tb, BR), compiler_params=_params(1))(proj, proj, proj, proj, proj, proj, conv_w)


def _branch_a_bwd(dycat, proj, conv_w, tb):
    s = proj.shape[0]

    def body(dy, dyn, ab, abn, ag, agn, ac, acp, ax, axp, w_ref, o_ref, dw_ref):
        _init_acc(dw_ref)
        i = pl.program_id(0)
        has_prev = (i > 0).astype(F32)
        has_next = (i < pl.num_programs(0) - 1).astype(F32)
        u = ac[...] * ax[...]
        up = acp[...] * axp[...] * has_prev
        v = _conv_taps(u, up, w_ref, 3)
        sg = _silu(ag[...])
        dv = dy[...] * ab[...] * sg
        dvn = dyn[...] * abn[...] * _silu(agn[...]) * has_next
        du = _conv_taps_t(dv, dvn, w_ref, 3)
        o_ref[:, 0:BR] = (dy[...] * v * sg).astype(MXU_DTYPE)
        o_ref[:, BR:2 * BR] = (du * ax[...]).astype(MXU_DTYPE)
        o_ref[:, 2 * BR:3 * BR] = (du * ac[...]).astype(MXU_DTYPE)
        o_ref[:, 3 * BR:4 * BR] = (dy[...] * ab[...] * v * _dsilu(ag[...])).astype(MXU_DTYPE)
        _conv_wgrad(dw_ref, dv, u, up, 3)

    return pl.pallas_call(
        body, name="branch_a_bwd", out_shape=(SDS((s, 4 * BR), MXU_DTYPE), SDS((8, BR), F32)), grid=(s // tb,),
        in_specs=[_rows(tb, BR, 0), _next8(tb, BR, s, 0),
                  _rows(tb, BR, CB_AB), _next8(tb, BR, s, CB_AB), _rows(tb, BR, CB_AG), _next8(tb, BR, s, CB_AG),
                  _rows(tb, BR, CB_AC), _prev8(tb, BR, CB_AC), _rows(tb, BR, CB_AX), _prev8(tb, BR, CB_AX),
                  _const((8, BR))],
        out_specs=(_rows(tb, 4 * BR), _const((8, BR))), compiler_params=_params(1),
    )(dycat, dycat, proj, proj, proj, proj, proj, proj, proj, proj, conv_w)


def _t5_bucket(dist):
    max_exact = REL_BUCKETS // 2
    nf = jnp.maximum(dist, 1).astype(F32)
    large = max_exact + (jnp.log(nf / max_exact) / math.log(REL_MAX_DIST / max_exact)
                         * (REL_BUCKETS - max_exact)).astype(jnp.int32)
    large = jnp.minimum(large, REL_BUCKETS - 1)
    return jnp.where(dist < max_exact, dist, large)


def _bucket_maps():
    maps = []
    i = jnp.arange(BLK)[:, None]
    j = jnp.arange(2 * BLK)[None, :]
    delta = i + BLK - j
    for window, dil in DILATIONS:
        span = window // dil
        bucket = _t5_bucket(jnp.clip(delta, 0, span) * dil)
        maps.append(jnp.where((delta >= 0) & (delta <= span), bucket, -1))
    return jnp.stack(maps).astype(jnp.int32)


def _bias_tables(rel_bias, buckets):
    n_pat = len(DILATIONS)

    def body(rb_ref, bk_ref, o_ref):
        for g in range(n_pat):
            bk = bk_ref[g]
            for h in range(ATT_HEADS):
                def per_bucket(b, acc):
                    return jnp.where(bk == b, rb_ref[b, h], acc)
                o_ref[g, h] = lax.fori_loop(0, REL_BUCKETS, per_bucket, jnp.full((BLK, 2 * BLK), NEG, F32))

    return pl.pallas_call(
        body, name="bias_tables", out_shape=SDS((n_pat, ATT_HEADS, BLK, 2 * BLK), F32),
        in_specs=[pl.BlockSpec(memory_space=pltpu.SMEM), pl.BlockSpec(memory_space=pltpu.VMEM)],
        compiler_params=_params())(rel_bias, buckets)


def _head_masks():
    lane = lax.broadcasted_iota(jnp.int32, (1, 2 * HEAD_DIM), 1)
    return [(lane < HEAD_DIM).astype(F32), (lane >= HEAD_DIM).astype(F32)]


def _strided(base, size, dil):
    return pl.ds(base, size, stride=dil) if dil > 1 else pl.ds(pl.multiple_of(base, BLK), size)


def _attn_groups(s, dil):
    return max(1, min(2048, s) // (dil * BLK))


def _attn_fwd(proj, bias, dil):
    s = proj.shape[0]
    grp = _attn_groups(s, dil)
    u1 = dil * BLK
    unit = grp * u1
    nb = s // unit
    w = 2 * HEAD_DIM
    q0, k0, v0 = (cb * (BR // w) for cb in (CB_Q, CB_K, CB_V))

    def body(q_ref, kc_ref, kp_ref, vc_ref, vp_ref, bias_ref, o_ref, lse_ref, kbuf, vbuf):
        n = pl.program_id(1)
        col = lax.broadcasted_iota(jnp.int32, (1, 2 * BLK), 1)
        masks = _head_masks()
        kbuf[0:u1, :] = kp_ref[...]
        kbuf[u1:, :] = kc_ref[...]
        vbuf[0:u1, :] = vp_ref[...]
        vbuf[u1:, :] = vc_ref[...]

        def per_r(t, carry):
            j = t // dil
            base = j * u1 + t % dil
            rows = _strided(base, BLK, dil)
            no_prev = jnp.where((n == 0) & (j == 0) & (col < BLK), NEG, 0.0)
            q = q_ref[rows, :] * (HEAD_DIM ** -0.5)
            k = kbuf[_strided(base, 2 * BLK, dil), :].astype(MXU_DTYPE)
            v = vbuf[_strided(base, 2 * BLK, dil), :].astype(MXU_DTYPE)
            q2 = jnp.concatenate([q * masks[0], q * masks[1]], axis=0).astype(MXU_DTYPE)
            sc = lax.dot_general(q2, k, (((1,), (1,)), ((), ())), preferred_element_type=F32)
            sc = sc + jnp.concatenate([bias_ref[0], bias_ref[1]], axis=0) + no_prev
            mx = jnp.max(sc, axis=-1, keepdims=True)
            p = jnp.exp(sc - mx)
            l = jnp.sum(p, axis=-1, keepdims=True)
            o2 = jnp.dot((p / l).astype(MXU_DTYPE), v, preferred_element_type=F32)
            lse2 = mx + jnp.log(l)
            o_ref[rows, :] = o2[0:BLK] * masks[0] + o2[BLK:2 * BLK] * masks[1]
            lse_ref[rows, :] = lse2[0:BLK] * masks[0] + lse2[BLK:2 * BLK] * masks[1]
            return carry

        lax.fori_loop(0, grp * dil, per_r, 0, unroll=8)

    cur = lambda c0: pl.BlockSpec((unit, w), lambda hp, n: (n, c0 + hp))
    prev = lambda c0: pl.BlockSpec((u1, w), lambda hp, n: (jnp.maximum(n * grp - 1, 0), c0 + hp))
    out = pl.BlockSpec((unit, w), lambda hp, n: (n, hp))
    return pl.pallas_call(
        body, name=f"attn_fwd_d{dil}", out_shape=(SDS((s, BR), F32), SDS((s, BR), F32)), grid=(BR // w, nb),
        in_specs=[cur(q0), cur(k0), prev(k0), cur(v0), prev(v0),
                  pl.BlockSpec((2, BLK, 2 * BLK), lambda hp, n: (hp, 0, 0))],
        out_specs=(out, out),
        scratch_shapes=[pltpu.VMEM((unit + u1, w), F32), pltpu.VMEM((unit + u1, w), F32)],
        compiler_params=_params(2))(proj, proj, proj, proj, proj, bias)


def _softmax3(l0, l1, l2):
    mx = jnp.maximum(jnp.maximum(l0, l1), l2)
    e0, e1, e2 = jnp.exp(l0 - mx), jnp.exp(l1 - mx), jnp.exp(l2 - mx)
    inv = 1.0 / (e0 + e1 + e2)
    return e0 * inv, e1 * inv, e2 * inv


def _attn_combine(os_, lses, proj, tb):
    s = proj.shape[0]

    def body(o0, o1, o2, l0, l1, l2, bg, y_ref):
        w0, w1, w2 = _softmax3(l0[...], l1[...], l2[...])
        attn = w0 * o0[...] + w1 * o1[...] + w2 * o2[...]
        y_ref[...] = (attn * _silu(bg[...])).astype(MXU_DTYPE)

    return pl.pallas_call(
        body, name="attn_combine", out_shape=SDS((s, BR), MXU_DTYPE), grid=(s // tb,),
        in_specs=[_rows(tb, BR)] * 6 + [_rows(tb, BR, CB_BG)], out_specs=_rows(tb, BR),
        compiler_params=_params(1))(*os_, *lses, proj)


def _attn_bwd_pre(dycat, os_, lses, proj, head_ones, tb):
    s = proj.shape[0]

    def body(dy, o0, o1, o2, l0, l1, l2, bg, e_ref, dbg_ref, do0, do1, do2, dm0, dm1, dm2):
        w0, w1, w2 = _softmax3(l0[...], l1[...], l2[...])
        attn = w0 * o0[...] + w1 * o1[...] + w2 * o2[...]
        dattn = dy[...] * _silu(bg[...])
        dbg_ref[...] = (dy[...] * attn * _dsilu(bg[...])).astype(MXU_DTYPE)
        prod = dattn * attn
        hi = prod.astype(MXU_DTYPE)
        lo = (prod - hi.astype(F32)).astype(MXU_DTYPE)
        tot = (jnp.dot(hi, e_ref[...], preferred_element_type=F32)
               + jnp.dot(lo, e_ref[...], preferred_element_type=F32))
        for wg, do_ref, dm_ref in ((w0, do0, dm0), (w1, do1, dm1), (w2, do2, dm2)):
            do_ref[...] = wg * dattn
            dm_ref[...] = wg * tot

    big = SDS((s, BR), F32)
    return pl.pallas_call(
        body, name="attn_bwd_pre", out_shape=(SDS((s, BR), MXU_DTYPE),) + (big,) * 6, grid=(s // tb,),
        in_specs=[_rows(tb, BR, 1)] + [_rows(tb, BR)] * 6 + [_rows(tb, BR, CB_BG), _const((BR, BR))],
        out_specs=(_rows(tb, BR),) * 7, compiler_params=_params(1))(dycat, *os_, *lses, proj, head_ones)


def _attn_bwd(proj, do, lse, dm, bias, dil, carry=None, add=()):
    s = proj.shape[0]
    grp = _attn_groups(s, dil)
    u1 = dil * BLK
    unit = grp * u1
    nb = s // unit
    w = 2 * HEAD_DIM
    q0, k0, v0 = (cb * (BR // w) for cb in (CB_Q, CB_K, CB_V))
    n_add = len(add)

    def body(q_ref, kc_ref, kp_ref, vc_ref, vp_ref, do_ref, lse_ref, dm_ref, bias_ref, *rest):
        more, (dq_ref, dk_ref, dv_ref, dbias_ref, kbuf, vbuf, stage_k, stage_v) = rest[:3 * n_add], rest[3 * n_add:]
        more_q, more_k, more_v = more[0::3], more[1::3], more[2::3]
        plus = lambda val, refs, rows: functools.reduce(lambda acc, r: acc + r[rows, :], refs, val)
        n = pl.program_id(1)
        col = lax.broadcasted_iota(jnp.int32, (1, 2 * BLK), 1)
        masks = _head_masks()

        @pl.when(n == 0)
        def _():
            dbias_ref[...] = jnp.zeros_like(dbias_ref)
            stage_k[...] = jnp.zeros_like(stage_k)
            stage_v[...] = jnp.zeros_like(stage_v)

        for out_ref, stage, more_ in ((dk_ref, stage_k, more_k), (dv_ref, stage_v, more_v)):
            if grp > 1:
                out_ref[0:unit - u1, :] = plus(stage[u1:unit, :], more_, slice(0, unit - u1))
            stage[0:u1, :] = stage[unit:unit + u1, :]

        @pl.when(n < nb)
        def _():
            kbuf[0:u1, :] = kp_ref[...]
            kbuf[u1:, :] = kc_ref[...]
            vbuf[0:u1, :] = vp_ref[...]
            vbuf[u1:, :] = vc_ref[...]

            def per_r(t, carry):
                j = t // dil
                base = j * u1 + t % dil
                rows = _strided(base, BLK, dil)
                rows_hi = _strided(base + u1, BLK, dil)
                no_prev = jnp.where((n == 0) & (j == 0) & (col < BLK), NEG, 0.0)
                q = q_ref[rows, :] * (HEAD_DIM ** -0.5)
                k = kbuf[_strided(base, 2 * BLK, dil), :].astype(MXU_DTYPE)
                v = vbuf[_strided(base, 2 * BLK, dil), :].astype(MXU_DTYPE)
                do_t, lse_t, dm_t = do_ref[rows, :], lse_ref[rows, :], dm_ref[rows, :]
                stack = lambda t: jnp.concatenate([t * masks[0], t * masks[1]], axis=0).astype(MXU_DTYPE)
                per_head = lambda t: jnp.concatenate([t[:, 0:1], t[:, HEAD_DIM:HEAD_DIM + 1]], axis=0)
                q2, do2 = stack(q), stack(do_t)
                sc = lax.dot_general(q2, k, (((1,), (1,)), ((), ())), preferred_element_type=F32)
                p = jnp.exp(sc + jnp.concatenate([bias_ref[0], bias_ref[1]], axis=0) + no_prev - per_head(lse_t))
                dp = lax.dot_general(do2, v, (((1,), (1,)), ((), ())), preferred_element_type=F32)
                ds = p * (dp - per_head(dm_t))
                dbias_ref[0] += ds[0:BLK]
                dbias_ref[1] += ds[BLK:2 * BLK]
                dsb, pb = ds.astype(MXU_DTYPE), p.astype(MXU_DTYPE)
                dq2 = jnp.dot(dsb, k, preferred_element_type=F32)
                dk_acc = lax.dot_general(dsb, q2, (((0,), (0,)), ((), ())), preferred_element_type=F32)
                dv_acc = lax.dot_general(pb, do2, (((0,), (0,)), ((), ())), preferred_element_type=F32)
                dq_ref[rows, :] = plus((dq2[0:BLK] * masks[0] + dq2[BLK:2 * BLK] * masks[1]) * (HEAD_DIM ** -0.5),
                                       more_q, rows)
                stage_k[rows, :] = stage_k[rows, :] + dk_acc[0:BLK]
                stage_v[rows, :] = stage_v[rows, :] + dv_acc[0:BLK]
                stage_k[rows_hi, :] = dk_acc[BLK:2 * BLK]
                stage_v[rows_hi, :] = dv_acc[BLK:2 * BLK]
                return carry

            lax.fori_loop(0, grp * dil, per_r, 0, unroll=8)

        dk_ref[unit - u1:unit, :] = plus(stage_k[0:u1, :], more_k, slice(unit - u1, unit))
        dv_ref[unit - u1:unit, :] = plus(stage_v[0:u1, :], more_v, slice(unit - u1, unit))

    qn = lambda n: jnp.minimum(n, nb - 1)
    cur = lambda c0: pl.BlockSpec((unit, w), lambda hp, n: (qn(n), c0 + hp))
    prev = lambda c0: pl.BlockSpec((u1, w), lambda hp, n: (jnp.maximum(qn(n) * grp - 1, 0), c0 + hp))
    row = pl.BlockSpec((unit, w), lambda hp, n: (qn(n), hp))
    late = pl.BlockSpec((unit, w), lambda hp, n: (jnp.maximum(n - 1, 0), hp))
    tab = pl.BlockSpec((2, BLK, 2 * BLK), lambda hp, n: (hp, 0, 0))
    big = SDS((s, BR), F32)
    return _call(
        body, name=f"attn_bwd_d{dil}", out_shape=(big, big, big, SDS((ATT_HEADS, BLK, 2 * BLK), F32)),
        grid=(BR // w, nb + 1),
        in_specs=[cur(q0), cur(k0), prev(k0), cur(v0), prev(v0), row, row, row, tab] + [row, late, late] * n_add,
        out_specs=(row, late, late, tab),
        scratch_shapes=[pltpu.VMEM((unit + u1, w), F32)] * 4,
        args=(proj, proj, proj, proj, proj, do, lse, dm, bias) + tuple(t for part in add for t in part), carry=carry)


def _rel_bias_grad(dbias, buckets):
    def body(db_ref, bk_ref, o_ref):
        row = lax.broadcasted_iota(jnp.int32, (REL_BUCKETS, 128), 0)
        lane = lax.broadcasted_iota(jnp.int32, (REL_BUCKETS, 128), 1)

        def per_bucket(b, acc):
            for g in range(len(DILATIONS)):
                hit = bk_ref[g] == b
                for h in range(ATT_HEADS):
                    both = db_ref[0, g, h] + db_ref[1, g, h]
                    val = jnp.sum(jnp.where(hit, both, 0.0), keepdims=True)
                    acc = acc + jnp.where((row == b) & (lane == h), val, 0.0)
            return acc

        o_ref[...] = lax.fori_loop(0, REL_BUCKETS, per_bucket, jnp.zeros((REL_BUCKETS, 128), F32))

    assert dbias.shape[0] == DEPTH == 2
    return pl.pallas_call(body, name="rel_bias_grad", out_shape=SDS((REL_BUCKETS, 128), F32),
                          compiler_params=_params())(dbias, buckets)


def _scan_rows(a_ref, b_ref, o_ref, carry, *, reverse):
    tb = a_ref.shape[0]
    order = range(7, -1, -1) if reverse else range(8)

    @pl.when(pl.program_id(0) == 0)
    def _():
        carry[...] = jnp.zeros_like(carry)

    def group(gi, h):
        r0 = pl.multiple_of((tb // 8 - 1 - gi if reverse else gi) * 8, 8)
        a8, b8 = a_ref[pl.ds(r0, 8), :], b_ref[pl.ds(r0, 8), :]
        rows = [None] * 8
        for k in order:
            if reverse:
                rows[k] = b8[k:k + 1] + h
                h = a8[k:k + 1] * rows[k]
            else:
                h = a8[k:k + 1] * h + b8[k:k + 1]
                rows[k] = h
        o_ref[pl.ds(r0, 8), :] = jnp.concatenate(rows, axis=0)
        return h

    carry[...] = lax.fori_loop(0, tb // 8, group, carry[...])


def _lru_scan_fwd(a, b, proj, tb):
    s = a.shape[0]

    def body(a_ref, b_ref, g_ref, h_ref, y_ref, carry):
        _scan_rows(a_ref, b_ref, h_ref, carry, reverse=False)
        y_ref[...] = (h_ref[...] * _silu(g_ref[...])).astype(MXU_DTYPE)

    return pl.pallas_call(
        body, name="lru_scan", out_shape=(SDS((s, BR), F32), SDS((s, BR), MXU_DTYPE)), grid=(s // tb,),
        in_specs=[_rows(tb, BR), _rows(tb, BR), _rows(tb, BR, CB_CG)], out_specs=(_rows(tb, BR), _rows(tb, BR)),
        scratch_shapes=[pltpu.VMEM((1, BR), F32)], compiler_params=_params(1))(a, b, proj)


def _lru_scan_bwd(a, dycat, h, proj, tb):
    s = a.shape[0]
    nt = s // tb

    def body(a_ref, dy_ref, h_ref, g_ref, l_ref, dg_ref, carry, dh_buf):
        dh_buf[...] = dy_ref[...] * _silu(g_ref[...])
        dg_ref[...] = (dy_ref[...] * h_ref[...] * _dsilu(g_ref[...])).astype(MXU_DTYPE)
        _scan_rows(a_ref, dh_buf, l_ref, carry, reverse=True)

    rev = lambda cb=0: pl.BlockSpec((tb, BR), lambda i: (nt - 1 - i, cb))
    return pl.pallas_call(
        body, name="lru_scan_bwd", out_shape=(SDS((s, BR), F32), SDS((s, BR), MXU_DTYPE)), grid=(nt,),
        in_specs=[rev(), rev(2), rev(), rev(CB_CG)], out_specs=(rev(), rev()),
        scratch_shapes=[pltpu.VMEM((1, BR), F32), pltpu.VMEM((tb, BR), F32)],
        compiler_params=_params(1))(a, dycat, h, proj)


def _scan_tile(s):
    return min(512, s)


def _load_chunked(ref, t0, pt):
    ln = pt // 8
    return jnp.concatenate([ref[pl.ds(t0 + j, 8, stride=ln), :] for j in range(ln)], axis=0)


def _store_natural(ref, t0, pt, val):
    ln = pt // 8
    for j in range(ln):
        ref[pl.ds(t0 + j, 8, stride=ln), :] = val[j * 8:(j + 1) * 8]


def _scan_tile_in_place(a_ref, x_ref, carry, pw, *, reverse):
    ch2 = x_ref.shape[1]
    ch = ch2 // 2
    ln = x_ref.shape[0] // 8
    ar = a_ref[:, 0:ch]
    ai = -a_ref[:, ch:ch2] if reverse else a_ref[:, ch:ch2]

    def cmul(pr, pi, xr, xi):
        return pr * xr - pi * xi, pr * xi + pi * xr

    @pl.when(pl.program_id(0) == 0)
    def _():
        carry[...] = jnp.zeros_like(carry)

        def fill(j, p):
            pw[pl.ds(j, 1), 0:ch] = p[0]
            pw[pl.ds(j, 1), ch:ch2] = p[1]
            return cmul(ar, ai, *p)

        lax.fori_loop(0, ln, fill, (ar, ai))

    def rows_of(j):
        return pl.ds(pl.multiple_of((ln - 1 - j if reverse else j) * 8, 8), 8)

    def local(j, x):
        rows = rows_of(j)
        nr, ni = cmul(ar, ai, *x)
        xr, xi = nr + x_ref[rows, 0:ch], ni + x_ref[rows, ch:ch2]
        x_ref[rows, 0:ch] = xr
        x_ref[rows, ch:ch2] = xi
        return xr, xi

    zero = jnp.zeros((8, ch), F32)
    er, ei = lax.fori_loop(0, ln, local, (zero, zero), unroll=2)
    apr, api = pw[ln - 1:ln, 0:ch], pw[ln - 1:ln, ch:ch2]
    cr, ci = carry[:, 0:ch], carry[:, ch:ch2]
    into_r, into_i = [None] * 8, [None] * 8
    for c in (range(7, -1, -1) if reverse else range(8)):
        into_r[c], into_i[c] = cr, ci
        pr, pi = cmul(apr, api, cr, ci)
        cr, ci = er[c:c + 1] + pr, ei[c:c + 1] + pi
    carry[:, 0:ch] = cr
    carry[:, ch:ch2] = ci
    into_r, into_i = jnp.concatenate(into_r, axis=0), jnp.concatenate(into_i, axis=0)

    def fix(j, carry_):
        rows = rows_of(j)
        dr, di = cmul(pw[pl.ds(j, 1), 0:ch], pw[pl.ds(j, 1), ch:ch2], into_r, into_i)
        x_ref[rows, 0:ch] += dr
        x_ref[rows, ch:ch2] += di
        return carry_

    lax.fori_loop(0, ln, fix, 0, unroll=2)


def _neg_expm1(z):
    series = -z * (1.0 + z * (0.5 + z * (1.0 / 6 + z * (1.0 / 24 + z * (1.0 / 120)))))
    return jnp.where(z > -0.05, series, 1.0 - jnp.exp(z))


def _lru_gate(xc, pre_r, pre_i, lam):
    log_a = -LRU_C * jax.nn.sigmoid(pre_r) * jax.nn.softplus(-lam)
    return jnp.exp(log_a), jnp.sqrt(_neg_expm1(2.0 * log_a)) * jax.nn.sigmoid(pre_i) * xc


def _lru_gates_fwd(proj, conv_w, conv_b, w_cat, b_cat, lam, tb):
    s = proj.shape[0]

    def body(cx, cxp, w_ref, cb_ref, wc_ref, bc_ref, lam_ref, a_ref, b_ref):
        has_prev = (pl.program_id(0) > 0).astype(F32)
        xc = _conv_taps(cx[...], cxp[...] * has_prev, w_ref, 4) + cb_ref[...]
        pre = jnp.dot(xc.astype(MXU_DTYPE), wc_ref[...], preferred_element_type=F32) + bc_ref[...]
        a_ref[...], b_ref[...] = _lru_gate(xc, pre[:, 0:BR], pre[:, BR:2 * BR], lam_ref[...])

    big = SDS((s, BR), F32)
    return pl.pallas_call(
        body, name="lru_gates_fwd", out_shape=(big, big), grid=(s // tb,),
        in_specs=[_rows(tb, BR, CB_CX), _prev8(tb, BR, CB_CX), _const((8, BR)), _const((1, BR)),
                  _const((BR, 2 * BR)), _const((1, 2 * BR)), _const((1, BR))],
        out_specs=(_rows(tb, BR), _rows(tb, BR)), compiler_params=_params(1),
    )(proj, proj, conv_w, conv_b, w_cat, b_cat, lam)


def _lru_gates_bwd(proj, lmb, h, conv_w, conv_b, w_cat, b_cat, lam, tb):
    s = proj.shape[0]

    def body(cx, cxp, l_ref, h_ref, hp_ref, w_ref, cb_ref, wc_ref, bc_ref, lam_ref,
             dxc_ref, dpre_ref, xc_ref, dbc_ref, dlam_ref):
        _init_acc(dbc_ref, dlam_ref)
        has_prev = (pl.program_id(0) > 0).astype(F32)
        xc = _conv_taps(cx[...], cxp[...] * has_prev, w_ref, 4) + cb_ref[...]
        xcb = xc.astype(MXU_DTYPE)
        pre = jnp.dot(xcb, wc_ref[...], preferred_element_type=F32) + bc_ref[...]
        _, vjp = jax.vjp(_lru_gate, xc, pre[:, 0:BR], pre[:, BR:2 * BR], lam_ref[...])
        lm = l_ref[...]
        dxc, dpr, dpi, dlam = vjp((lm * _shift_down(h_ref[...], hp_ref[...] * has_prev, 1), lm))
        dpre = jnp.concatenate([dpr, dpi], axis=1)
        dpreb = dpre.astype(MXU_DTYPE)
        dxc_ref[...] = dxc + lax.dot_general(dpreb, wc_ref[...], (((1,), (1,)), ((), ())),
                                             preferred_element_type=F32)
        dpre_ref[...] = dpreb
        xc_ref[...] = xcb
        dbc_ref[...] += _colsum(dpre)
        dlam_ref[...] += dlam

    return pl.pallas_call(
        body, name="lru_gates_bwd",
        out_shape=(SDS((s, BR), F32), SDS((s, 2 * BR), MXU_DTYPE), SDS((s, BR), MXU_DTYPE),
                   SDS((1, 2 * BR), F32), SDS((1, BR), F32)),
        grid=(s // tb,),
        in_specs=[_rows(tb, BR, CB_CX), _prev8(tb, BR, CB_CX), _rows(tb, BR), _rows(tb, BR), _prev8(tb, BR),
                  _const((8, BR)), _const((1, BR)), _const((BR, 2 * BR)), _const((1, 2 * BR)), _const((1, BR))],
        out_specs=(_rows(tb, BR), _rows(tb, 2 * BR), _rows(tb, BR), _const((1, 2 * BR)), _const((1, BR))),
        compiler_params=_params(1))(proj, proj, lmb, h, h, conv_w, conv_b, w_cat, b_cat, lam)


def _conv_c_bwd(dxc, proj, conv_w, tb):
    s = proj.shape[0]

    def body(g, gn, cx, cxp, w_ref, dcx_ref, dw_ref, db_ref):
        _init_acc(dw_ref, db_ref)
        i = pl.program_id(0)
        has_prev = (i > 0).astype(F32)
        has_next = (i < pl.num_programs(0) - 1).astype(F32)
        gt = g[...]
        dcx_ref[...] = _conv_taps_t(gt, gn[...] * has_next, w_ref, 4).astype(MXU_DTYPE)
        _conv_wgrad(dw_ref, gt, cx[...], cxp[...] * has_prev, 4)
        db_ref[...] += _colsum(gt)

    return pl.pallas_call(
        body, name="conv_c_bwd", out_shape=(SDS((s, BR), MXU_DTYPE), SDS((8, BR), F32), SDS((1, BR), F32)),
        grid=(s // tb,),
        in_specs=[_rows(tb, BR), _next8(tb, BR, s), _rows(tb, BR, CB_CX), _prev8(tb, BR, CB_CX), _const((8, BR))],
        out_specs=(_rows(tb, BR), _const((8, BR)), _const((1, BR))), compiler_params=_params(1),
    )(dxc, dxc, proj, proj, conv_w)


def _s5_disc(lam_re, lam_im, log_dt):
    dt = jnp.exp(log_dt)
    mag = jnp.exp(lam_re * dt)
    ab_re = mag * jnp.cos(lam_im * dt)
    ab_im = mag * jnp.sin(lam_im * dt)
    den = lam_re * lam_re + lam_im * lam_im
    f_re = ((ab_re - 1.0) * lam_re + ab_im * lam_im) / den
    f_im = (ab_im * lam_re - (ab_re - 1.0) * lam_im) / den
    return ab_re, ab_im, f_re, f_im


def _s5_bbar(f_re, f_im, b_re, b_im):
    return f_re * b_re - f_im * b_im, f_re * b_im + f_im * b_re


def _s5_disc_fwd(lam_re, lam_im, log_dt):
    def body(lr, li, ld, o0, o1, o2, o3):
        o0[...], o1[...], o2[...], o3[...] = _s5_disc(lr[...], li[...], ld[...])
    return pl.pallas_call(body, name="s5_disc_fwd", out_shape=(SDS(lam_re.shape, F32),) * 4)(lam_re, lam_im, log_dt)


def _s5_disc_bwd(lam_re, lam_im, log_dt, cts):
    def body(lr, li, ld, c0, c1, c2, c3, o0, o1, o2):
        _, vjp = jax.vjp(_s5_disc, lr[...], li[...], ld[...])
        o0[...], o1[...], o2[...] = vjp((c0[...], c1[...], c2[...], c3[...]))
    return pl.pallas_call(body, name="s5_disc_bwd", out_shape=(SDS(lam_re.shape, F32), SDS(lam_re.shape, F32),
                                                                SDS(log_dt.shape, F32)))(lam_re, lam_im, log_dt, *cts)


def _s5_bbar_fwd(f_re, f_im, b_re, b_im):
    def body(fr, fi, br, bi, o0, o1):
        o0[...], o1[...] = _s5_bbar(fr[...], fi[...], br[...], bi[...])
    return pl.pallas_call(body, name="s5_bbar_fwd", out_shape=(SDS(b_re.shape, F32),) * 2)(f_re, f_im, b_re, b_im)


def _s5_bbar_bwd(f_re, f_im, b_re, b_im, d_re, d_im):
    def body(fr, fi, br, bi, dr, di, o0, o1, o2, o3):
        _, vjp = jax.vjp(_s5_bbar, fr[...], fi[...], br[...], bi[...])
        o0[...], o1[...], o2[...], o3[...] = vjp((dr[...], di[...]))
    col, mat = SDS(f_re.shape, F32), SDS(b_re.shape, F32)
    return pl.pallas_call(body, name="s5_bbar_bwd", out_shape=(col, col, mat, mat))(f_re, f_im, b_re, b_im, d_re, d_im)


def _s5_tail_bwd(dycat, ylin, proj, d_skip, w_glu, b_glu, tb):
    s = proj.shape[0]

    def body(dy, yl, u, dg, dk, w_ref, b_ref, dyl_ref, dus_ref, ddg_ref, g_ref, dt_ref, ddk_ref, dbg_ref):
        _init_acc(ddk_ref, dbg_ref)
        g, gelu_vjp = jax.vjp(jax.nn.gelu, yl[...] + dk[...] * u[...])
        gb = g.astype(MXU_DTYPE)
        sg = jax.nn.sigmoid(jnp.dot(gb, w_ref[...], preferred_element_type=F32) + b_ref[...])
        dz = dy[...] * _silu(dg[...])
        ddg_ref[...] = (dy[...] * g * sg * _dsilu(dg[...])).astype(MXU_DTYPE)
        dt = dz * g * sg * (1.0 - sg)
        dtb = dt.astype(MXU_DTYPE)
        dgel = dz * sg + lax.dot_general(dtb, w_ref[...], (((1,), (1,)), ((), ())), preferred_element_type=F32)
        dyv, = gelu_vjp(dgel)
        dyl_ref[...] = dyv
        dus_ref[...] = dyv * dk[...]
        g_ref[...] = gb
        dt_ref[...] = dtb
        ddk_ref[...] += _colsum(dyv * u[...])
        dbg_ref[...] += _colsum(dt)

    big, half, vec = SDS((s, BR), F32), SDS((s, BR), MXU_DTYPE), SDS((1, BR), F32)
    return pl.pallas_call(
        body, name="s5_tail_bwd", out_shape=(big, big, half, half, half, vec, vec), grid=(s // tb,),
        in_specs=[_rows(tb, BR, 3), _rows(tb, BR), _rows(tb, BR, CB_DU), _rows(tb, BR, CB_DG), _const((1, BR)),
                  _const((BR, BR)), _const((1, BR))],
        out_specs=(_rows(tb, BR),) * 5 + (_const((1, BR)), _const((1, BR))), compiler_params=_params(1),
    )(dycat, ylin, proj, proj, d_skip, w_glu, b_glu)


def _assemble_dproj(da, dqkv, dbg, dcx, dcg, du, dus, ddg, tb):
    s = da.shape[0]

    def body(da_ref, dq_ref, dk_ref, dv_ref, dbg_ref, dcx_ref, dcg_ref, du_ref, dus_ref, ddg_ref, o_ref):
        o_ref[:, 0:4 * BR] = da_ref[...]
        for j, part in enumerate((dq_ref, dk_ref, dv_ref)):
            o_ref[:, (4 + j) * BR:(5 + j) * BR] = part[...].astype(MXU_DTYPE)
        o_ref[:, 7 * BR:8 * BR] = dbg_ref[...].astype(MXU_DTYPE)
        o_ref[:, 8 * BR:9 * BR] = dcx_ref[...].astype(MXU_DTYPE)
        o_ref[:, 9 * BR:10 * BR] = dcg_ref[...].astype(MXU_DTYPE)
        o_ref[:, 10 * BR:11 * BR] = (du_ref[...] + dus_ref[...]).astype(MXU_DTYPE)
        o_ref[:, 11 * BR:12 * BR] = ddg_ref[...].astype(MXU_DTYPE)

    return pl.pallas_call(
        body, name="assemble_dproj", out_shape=SDS((s, N_IN), MXU_DTYPE), grid=(s // tb,),
        in_specs=[_rows(tb, 4 * BR)] + [_rows(tb, BR)] * 9, out_specs=_rows(tb, N_IN),
        compiler_params=_params(1))(da, *dqkv, dbg, dcx, dcg, du, dus, ddg)


def _sum_leading(xs, tr, name):
    n, _, c = xs[0].shape
    nl = len(xs)
    tr = min([tr] + [x.shape[1] for x in xs])
    assert all(x.shape[1] % tr == 0 for x in xs), (name, tr)
    nrs = [x.shape[1] // tr for x in xs]
    starts = [sum(nrs[:l]) for l in range(nl)]

    def body(*refs):
        i = pl.program_id(0)
        for l in range(nl):
            @pl.when((i >= starts[l]) & (i < starts[l] + nrs[l]))
            def _():
                acc = refs[l * n][...].astype(F32)
                for ref in refs[l * n + 1:(l + 1) * n]:
                    acc = acc + ref[...].astype(F32)
                refs[nl * n][...] = acc

    specs = [pl.BlockSpec((None, tr, c), functools.partial(
        lambda i, k, l: (k, jnp.clip(i - starts[l], 0, nrs[l] - 1), 0), k=k, l=l)) for l in range(nl) for k in range(n)]
    return pl.pallas_call(body, name=name, out_shape=SDS((sum(nrs) * tr, c), F32), grid=(sum(nrs),), in_specs=specs,
                          out_specs=pl.BlockSpec((tr, c), lambda i: (i, 0)),
                          compiler_params=_params(1))(*[x for x in xs for _ in range(n)])


def _adamw(w, g_parts, m, v, tr, name):
    r, c = w.shape
    tr = min(tr, r)
    n = len(g_parts)
    assert r % tr == 0, (name, r, tr)

    def body(*refs):
        w_ref, m_ref, v_ref = refs[0], refs[1 + n], refs[2 + n]
        g_ref, d_ref, nm_ref, nv_ref = refs[3 + n:]
        g = refs[1][...]
        for ref in refs[2:1 + n]:
            g = g + ref[...]
        mm = ADAM_B1 * m_ref[...] + (1.0 - ADAM_B1) * g
        vv = ADAM_B2 * v_ref[...] + (1.0 - ADAM_B2) * jnp.square(g)
        m_hat = mm / (1.0 - ADAM_B1 ** ADAM_STEP)
        v_hat = vv / (1.0 - ADAM_B2 ** ADAM_STEP)
        g_ref[...] = g
        d_ref[...] = -ADAM_LR * (m_hat / (jnp.sqrt(v_hat) + ADAM_EPS) + ADAM_WD * w_ref[...])
        nm_ref[...] = mm
        nv_ref[...] = vv

    spec = pl.BlockSpec((tr, c), lambda i: (i, 0))
    return _call(body, name=name, out_shape=(SDS((r, c), F32),) * 4, grid=(r // tr,), in_specs=[spec] * (3 + n),
                 out_specs=(spec,) * 4, scratch_shapes=[], args=(w, *g_parts, m, v))


class _AllGather8:
    def __init__(self, block):
        self.m_per = block.shape[0]
        self.arrays, self.n_in, self.n_out = [block], 1, 1
        self.out_shapes = (SDS((N_DEV * self.m_per, block.shape[1]), block.dtype),)
        self.scratch = [pltpu.SemaphoreType.DMA((7,)), pltpu.SemaphoreType.DMA((7,)), pltpu.SemaphoreType.DMA]

    def _copies(self, ins, outs, sems):
        (x_ref,), (out_ref,), (send_sems, recv_sems, local_sem) = ins, outs, sems
        x, y, c = lax.axis_index("x"), lax.axis_index("y"), lax.axis_index("c")
        me, sibling = (x, y, c), (x, y, 1 - c)
        chips = [(1 - x, y), (x, 1 - y), (1 - x, 1 - y)]

        def rows(px, py, pc):
            return out_ref.at[pl.ds((4 * px + 2 * py + pc) * self.m_per, self.m_per), :]

        def copy(k, blk, to, src=None):
            return pltpu.make_async_remote_copy(
                src_ref=rows(*blk) if src is None else src, dst_ref=rows(*blk), send_sem=send_sems.at[k],
                recv_sem=recv_sems.at[k], device_id=to, device_id_type=MESH)

        mine = pltpu.make_async_copy(x_ref, rows(*me), local_sem)
        first = [copy(0, me, sibling, src=x_ref)]
        first += [copy(1 + j, me, (*chip, c), src=x_ref) for j, chip in enumerate(chips)]
        passed = [copy(4 + j, (*chip, c), sibling) for j, chip in enumerate(chips)]
        arrivals = [copy(1 + j, (*chip, c), me) for j, chip in enumerate(chips)]
        from_sibling = [copy(0, sibling, me)] + [copy(4 + j, (*chip, 1 - c), me) for j, chip in enumerate(chips)]
        return mine, first, passed, arrivals, from_sibling

    def start(self, ins, outs, sems):
        mine, first, _, _, _ = self._copies(ins, outs, sems)
        mine.start()
        for cp in first:
            cp.start()

    def wait(self, ins, outs, sems):
        mine, first, passed, arrivals, from_sibling = self._copies(ins, outs, sems)
        for arrived, onward in zip(arrivals, passed):
            arrived.wait_recv()
            onward.start()
        for cp in from_sibling:
            cp.wait_recv()
        for cp in first + passed:
            cp.wait_send()
        mine.wait()


def _allgather8(block, name):
    ex = _AllGather8(block)

    def body(x_ref, out_ref, *sems):
        ex.start((x_ref,), (out_ref,), sems)
        ex.wait((x_ref,), (out_ref,), sems)

    return pl.pallas_call(
        body, name=name, out_shape=ex.out_shapes[0], in_specs=[pl.BlockSpec(memory_space=pltpu.VMEM)],
        out_specs=pl.BlockSpec(memory_space=pltpu.VMEM), scratch_shapes=ex.scratch, compiler_params=_params())(block)


class _Exchange:
    def __init__(self, items, out_shapes):
        self.items, self.out_shapes = list(items), tuple(out_shapes)
        self.arrays = [it[0] for it in self.items]
        n = len(self.items)
        self.n_in, self.n_out = n, len(self.out_shapes)
        self.scratch = [pltpu.SemaphoreType.DMA((n * N_CHIPS,)), pltpu.SemaphoreType.DMA((n * N_CHIPS,)),
                        pltpu.SemaphoreType.DMA((n,))]

    def _copies(self, ins, outs, sems, m):
        send_sems, recv_sems, local_sems = sems
        c = lax.axis_index("c")
        others = [j for j in range(N_CHIPS) if j != m]

        def remote(a, src, dst, to, from_):
            return pltpu.make_async_remote_copy(
                src_ref=src, dst_ref=dst, send_sem=send_sems.at[a * N_CHIPS + to],
                recv_sem=recv_sems.at[a * N_CHIPS + from_], device_id=(to // 2, to % 2, c), device_id_type=MESH)

        local, sends, recvs = [], [], []
        for a, (_, oi, src_of, dst_of) in enumerate(self.items):
            local.append(pltpu.make_async_copy(src_of(ins[a], m), dst_of(outs[oi], m), local_sems.at[a]))
            for j in others:
                sends.append(remote(a, src_of(ins[a], j), dst_of(outs[oi], m), j, m))
                recvs.append(remote(a, src_of(ins[a], m), dst_of(outs[oi], j), j, j))
        return local, sends, recvs

    def _on_my_chip(self, fn):
        chip = 2 * lax.axis_index("x") + lax.axis_index("y")
        for m in range(N_CHIPS):
            pl.when(chip == m)(functools.partial(fn, m))

    def start(self, ins, outs, sems):
        def go(m):
            local, sends, _ = self._copies(ins, outs, sems, m)
            for cp in local + sends:
                cp.start()
        self._on_my_chip(go)

    def wait(self, ins, outs, sems):
        def go(m):
            local, sends, recvs = self._copies(ins, outs, sems, m)
            for cp in recvs:
                cp.wait_recv()
            for cp in sends:
                cp.wait_send()
            for cp in local:
                cp.wait()
        self._on_my_chip(go)


def _half_rows(ref, cc):
    h = ref.shape[-2] // 2
    return ref.at[(slice(None),) * (len(ref.shape) - 2) + (pl.ds(cc * h, h), slice(None))]


class _Gather:
    def __init__(self, items, out_shapes):
        self.items, self.out_shapes = list(items), tuple(out_shapes)
        self.arrays = [it[0] for it in self.items]
        n = len(self.items)
        self.n_in, self.n_out = n, len(self.out_shapes)
        self.scratch = [pltpu.SemaphoreType.DMA((n * N_CHIPS,)) for _ in range(4)] + [pltpu.SemaphoreType.DMA((n,))]

    def _copies(self, ins, outs, sems, m, cc):
        ici_send, ici_recv, d2d_send, d2d_recv, local_sems = sems
        others = [j for j in range(N_CHIPS) if j != m]
        local, sends, arrivals, passed_on, from_sibling = [], [], [], [], []
        for a, (_, oi, src_of, dst_of) in enumerate(self.items):
            src, out = src_of(ins[a]), outs[oi]
            local.append(pltpu.make_async_copy(src, dst_of(out, m), local_sems.at[a]))
            for j in others:
                k = a * N_CHIPS + j
                mine_there = _half_rows(dst_of(out, m), cc)
                theirs_here = _half_rows(dst_of(out, j), cc)
                sends.append(pltpu.make_async_remote_copy(
                    src_ref=_half_rows(src, cc), dst_ref=mine_there, send_sem=ici_send.at[k],
                    recv_sem=ici_recv.at[a * N_CHIPS + m], device_id=(j // 2, j % 2, cc), device_id_type=MESH))
                arrivals.append(pltpu.make_async_remote_copy(
                    src_ref=_half_rows(src, cc), dst_ref=theirs_here, send_sem=ici_send.at[k], recv_sem=ici_recv.at[k],
                    device_id=(j // 2, j % 2, cc), device_id_type=MESH))
                passed_on.append(pltpu.make_async_remote_copy(
                    src_ref=theirs_here, dst_ref=theirs_here, send_sem=d2d_send.at[k], recv_sem=d2d_recv.at[k],
                    device_id=(m // 2, m % 2, 1 - cc), device_id_type=MESH))
                other_half = _half_rows(dst_of(out, j), 1 - cc)
                from_sibling.append(pltpu.make_async_remote_copy(
                    src_ref=other_half, dst_ref=other_half, send_sem=d2d_send.at[k], recv_sem=d2d_recv.at[k],
                    device_id=(m // 2, m % 2, 1 - cc), device_id_type=MESH))
        return local, sends, arrivals, passed_on, from_sibling

    def _on_my_core(self, fn):
        chip = 2 * lax.axis_index("x") + lax.axis_index("y")
        c = lax.axis_index("c")
        for m in range(N_CHIPS):
            for cc in range(2):
                pl.when((chip == m) & (c == cc))(functools.partial(fn, m, cc))

    def start(self, ins, outs, sems):
        def go(m, cc):
            local, sends, _, _, _ = self._copies(ins, outs, sems, m, cc)
            for cp in local + sends:
                cp.start()
        self._on_my_core(go)

    def wait(self, ins, outs, sems):
        def go(m, cc):
            local, sends, arrivals, passed_on, from_sibling = self._copies(ins, outs, sems, m, cc)
            for arrived, onward in zip(arrivals, passed_on):
                arrived.wait_recv()
                onward.start()
            for cp in from_sibling:
                cp.wait_recv()
            for cp in sends + passed_on:
                cp.wait_send()
            for cp in local:
                cp.wait()
        self._on_my_core(go)


def _run_exchange(ex, name):
    def body(*refs):
        ins, outs, sems = refs[:ex.n_in], refs[ex.n_in:ex.n_in + ex.n_out], refs[ex.n_in + ex.n_out:]
        ex.start(ins, outs, sems)
        ex.wait(ins, outs, sems)

    return pl.pallas_call(
        body, name=name, out_shape=ex.out_shapes, in_specs=[ANY] * ex.n_in, out_specs=(ANY,) * ex.n_out,
        scratch_shapes=ex.scratch, compiler_params=_params())(*ex.arrays)


def _sibling_swap(arrays, name, also):
    n = len(arrays)

    def body(*refs):
        ins, refs = refs[:n], refs[n:]
        x_ins, refs = refs[:also.n_in], refs[also.n_in:]
        outs, refs = refs[:n], refs[n:]
        x_outs, refs = refs[:also.n_out], refs[also.n_out:]
        send_sems, recv_sems, x_sems = refs[0], refs[1], refs[2:]
        peer = (lax.axis_index("x"), lax.axis_index("y"), 1 - lax.axis_index("c"))
        cps = [pltpu.make_async_remote_copy(src_ref=ins[a], dst_ref=outs[a], send_sem=send_sems.at[a],
                                            recv_sem=recv_sems.at[a], device_id=peer, device_id_type=MESH)
               for a in range(n)]
        also.start(x_ins, x_outs, x_sems)
        for cp in cps:
            cp.start()
        also.wait(x_ins, x_outs, x_sems)
        for cp in cps:
            cp.wait()

    return pl.pallas_call(
        body, name=name, out_shape=tuple(SDS(a.shape, a.dtype) for a in arrays) + also.out_shapes,
        in_specs=[ANY] * (n + also.n_in), out_specs=(ANY,) * (n + also.n_out),
        scratch_shapes=[pltpu.SemaphoreType.DMA((n,)), pltpu.SemaphoreType.DMA((n,))] + also.scratch,
        compiler_params=_params())(*arrays, *also.arrays)


def _block_diag(w):
    h, n, m = w.shape
    eye = jnp.eye(h, dtype=w.dtype)
    return (w[:, :, None, :] * eye[:, None, :, None]).reshape(h * n, h * m)


def _diag_blocks(d, h, col0=0, ncols=None, stacked=1):
    ncols = d.shape[1] - col0 if ncols is None else ncols
    n, m = d.shape[0] // (h * stacked), ncols // h
    lanes = 128
    assert m <= lanes and lanes % m == 0 and col0 % lanes == 0

    def body(d_ref, o_ref):
        for gi in range(h * stacked):
            c = col0 + (gi % h) * m
            chunk = d_ref[gi * n:(gi + 1) * n, c // lanes * lanes:c // lanes * lanes + lanes]
            o_ref[gi * n:(gi + 1) * n, :] = chunk[:, c % lanes:c % lanes + m]

    out = pl.pallas_call(body, name="diag_blocks", out_shape=SDS((stacked * h * n, m), d.dtype),
                         compiler_params=_params())(d)
    return out.reshape(stacked * h, n, m)


S5_CHUNKS = 4
S5_PER = S5_GROUPS // S5_CHUNKS
CH_W = S5_PER * S5_CH
ST_W = S5_PER * S5_STATE


def _bd_stack(mats):
    _, _, n, m = mats.shape
    eye = jnp.eye(S5_PER, dtype=mats.dtype)
    t = mats.reshape(2, S5_CHUNKS, S5_PER, n, m)
    bd = t[:, :, :, :, None, :] * eye[None, None, :, None, :, None]
    return bd.reshape(2 * S5_CHUNKS, S5_PER * n, S5_PER * m).astype(MXU_DTYPE)


def _chunks_chunked(src_ref, buf):
    pt = src_ref.shape[0]
    out = []
    for q in range(S5_CHUNKS):
        buf[q] = src_ref[:, q * CH_W:(q + 1) * CH_W]
        out.append(_load_chunked(buf.at[q], 0, pt).astype(MXU_DTYPE))
    return out


def _expand_into(dst_ref, chunks, w_ref):
    for b in range(2 * S5_CHUNKS):
        dst_ref[:, b * ST_W:(b + 1) * ST_W] = jnp.dot(chunks[b % S5_CHUNKS], w_ref[b], preferred_element_type=F32)


def _reduce_from(src_ref, w_ref, buf, dst_ref):
    pt = src_ref.shape[0]
    for q in range(S5_CHUNKS):
        y = jnp.dot(src_ref[:, q * ST_W:(q + 1) * ST_W].astype(MXU_DTYPE), w_ref[q], preferred_element_type=F32)
        p = S5_CHUNKS + q
        y = y + jnp.dot(src_ref[:, p * ST_W:(p + 1) * ST_W].astype(MXU_DTYPE), w_ref[p], preferred_element_type=F32)
        _store_natural(buf.at[q], 0, pt, y)
        dst_ref[:, q * CH_W:(q + 1) * CH_W] = buf[q]


def _s5_fwd(proj, w_bu, w_cx, a_row, d_skip, w_glu, b_glu):
    s = proj.shape[0]
    pt = _scan_tile(s)
    ch2 = 2 * S5_N

    def body(u_ref, dg_ref, wb_ref, wc_ref, a_ref, dk_ref, wg_ref, bg_ref, x_ref, y_ref, o_ref, carry, pw, buf):
        _expand_into(x_ref, _chunks_chunked(u_ref, buf), wb_ref)
        _scan_tile_in_place(a_ref, x_ref, carry, pw, reverse=False)
        _reduce_from(x_ref, wc_ref, buf, y_ref)
        g = jax.nn.gelu(y_ref[...] + dk_ref[...] * u_ref[...])
        t = jnp.dot(g.astype(MXU_DTYPE), wg_ref[...], preferred_element_type=F32) + bg_ref[...]
        o_ref[...] = (g * jax.nn.sigmoid(t) * _silu(dg_ref[...])).astype(MXU_DTYPE)

    return pl.pallas_call(
        body, name="s5_fwd", out_shape=(SDS((s, ch2), F32), SDS((s, BR), F32), SDS((s, BR), MXU_DTYPE)),
        grid=(s // pt,),
        in_specs=[_rows(pt, BR, CB_DU), _rows(pt, BR, CB_DG), _const(w_bu.shape), _const(w_cx.shape),
                  _const((1, ch2)), _const((1, BR)), _const((BR, BR)), _const((1, BR))],
        out_specs=(_rows(pt, ch2), _rows(pt, BR), _rows(pt, BR)),
        scratch_shapes=[pltpu.VMEM((1, ch2), F32), pltpu.VMEM((pt // 8, ch2), F32),
                        pltpu.VMEM((S5_CHUNKS, pt, CH_W), F32)],
        compiler_params=_params(1))(proj, proj, w_bu, w_cx, a_row, d_skip, w_glu, b_glu)


def _s5_core_bwd(dyl, proj, x, w_dx, w_du, a_row):
    s = proj.shape[0]
    pt = _scan_tile(s)
    nt = s // pt
    ch2 = 2 * S5_N
    ch = S5_N

    def body(dy_ref, u_ref, x_ref, xp_ref, wx_ref, wu_ref, a_ref, du_ref, da_ref, dwb_ref, dwc_ref,
             l_ref, carry, pw, buf, buf2):
        i = pl.program_id(0)
        _init_acc(da_ref, dwb_ref, dwc_ref)
        dy_c = _chunks_chunked(dy_ref, buf)
        u_c = _chunks_chunked(u_ref, buf2)
        _expand_into(l_ref, dy_c, wx_ref)
        _scan_tile_in_place(a_ref, l_ref, carry, pw, reverse=True)
        has_prev = (i < nt - 1).astype(F32)
        row = lax.broadcasted_iota(jnp.int32, (8, ch2), 0)
        first = jnp.where(row == 0, pltpu.roll(xp_ref[...], 1, 0) * has_prev, pltpu.roll(x_ref[pt - 8:pt, :], 1, 0))
        xprev = jnp.concatenate([first, x_ref[0:pt - 8, :]], axis=0)
        lr, li, xr, xi = l_ref[:, 0:ch], l_ref[:, ch:ch2], xprev[:, 0:ch], xprev[:, ch:ch2]
        da_ref[:, 0:ch] += _colsum(lr * xr + li * xi)
        da_ref[:, ch:ch2] += _colsum(li * xr - lr * xi)
        _reduce_from(l_ref, wu_ref, buf, du_ref)
        tn = (((0,), (0,)), ((), ()))
        for b in range(2 * S5_CHUNKS):
            cols, rows = slice(b * ST_W, (b + 1) * ST_W), slice(b * CH_W, (b + 1) * CH_W)
            dwb_ref[rows, :] += lax.dot_general(u_c[b % S5_CHUNKS], l_ref[:, cols].astype(MXU_DTYPE), tn,
                                                preferred_element_type=F32)
            dwc_ref[rows, :] += lax.dot_general(dy_c[b % S5_CHUNKS], x_ref[:, cols].astype(MXU_DTYPE), tn,
                                                preferred_element_type=F32)

    rev = lambda w, cb=0: pl.BlockSpec((pt, w), lambda i: (nt - 1 - i, cb))
    halo = pl.BlockSpec((8, ch2), lambda i: (jnp.maximum((nt - 1 - i) * (pt // 8) - 1, 0), 0))
    wshape = SDS((2 * S5_CHUNKS * CH_W, ST_W), F32)
    return pl.pallas_call(
        body, name="s5_core_bwd", out_shape=(SDS((s, BR), F32), SDS((1, ch2), F32), wshape, wshape), grid=(nt,),
        in_specs=[rev(BR, 0), rev(BR, CB_DU), rev(ch2), halo, _const(w_dx.shape), _const(w_du.shape),
                  _const((1, ch2))],
        out_specs=(rev(BR), _const((1, ch2)), _const(wshape.shape), _const(wshape.shape)),
        scratch_shapes=[pltpu.VMEM((pt, ch2), F32), pltpu.VMEM((1, ch2), F32), pltpu.VMEM((pt // 8, ch2), F32),
                        pltpu.VMEM((S5_CHUNKS, pt, CH_W), F32), pltpu.VMEM((S5_CHUNKS, pt, CH_W), F32)],
        compiler_params=_params(1))(dyl, proj, x, x, w_dx, w_du, a_row)


def _tiles(s):
    return dict(tb=min(512, s), tln=min(256, s))


def _layer_weights(p, l):
    pad8 = lambda w: jnp.pad(w, ((0, 8 - w.shape[0]), (0, 0)))
    return dict(
        conv_a=pad8(p["conv_a"][l]), conv_c=pad8(p["conv_c"][l]), conv_c_b=p["conv_c_b"][l][None],
        w_cat=jnp.concatenate([_block_diag(p["lru_wa"][l]), _block_diag(p["lru_wx"][l])], axis=1).astype(MXU_DTYPE),
        b_cat=jnp.concatenate([p["lru_ba"][l], p["lru_bx"][l]])[None], lam=p["lru_lambda"][l][None],
        lam_re=p["s5_lam_re"][l], lam_im=p["s5_lam_im"][l], log_dt=p["s5_log_dt"][l][:, None],
        b_re=p["s5_b_re"][l].reshape(S5_N, S5_CH), b_im=p["s5_b_im"][l].reshape(S5_N, S5_CH),
        c_re=p["s5_c_re"][l], c_im=p["s5_c_im"][l], d_skip=p["s5_d"][l][None], b_glu=p["s5_b_glu"][l][None],
        ln_g=p["ln_g"][l][None], ln_b=p["ln_b"][l][None])


def _s5_matrices(lw):
    ab_re, ab_im, f_re, f_im = _s5_disc_fwd(lw["lam_re"], lw["lam_im"], lw["log_dt"])
    f_re, f_im = f_re.reshape(S5_N, 1), f_im.reshape(S5_N, 1)
    bb_re, bb_im = _s5_bbar_fwd(f_re, f_im, lw["b_re"], lw["b_im"])
    bb = jnp.stack([bb_re, bb_im]).reshape(2, S5_GROUPS, S5_STATE, S5_CH)
    cc = jnp.stack([lw["c_re"], -lw["c_im"]])
    a_row = jnp.concatenate([ab_re.reshape(1, S5_N), ab_im.reshape(1, S5_N)], axis=1)
    return dict(f_re=f_re, f_im=f_im, a_row=a_row, w_bu=_bd_stack(jnp.swapaxes(bb, 2, 3)), w_du=_bd_stack(bb),
                w_cx=_bd_stack(jnp.swapaxes(cc, 2, 3)), w_dx=_bd_stack(cc))


def _mm_hooked(hook, *args, **kw):
    if hook is None:
        return _mm(*args, **kw)
    out = _mm(*args, carry=hook[0], **kw)
    hook[1](out[1:])
    return out[0]


def _layer_fwd(x, h, ada, w_in, get_rest, lw, s5m, bias_tabs, hooks=None, target=None, next_ada=None):
    s = x.shape[0]
    t = _tiles(s)
    tb = t["tb"]
    shift, scale, gate = ada
    hooks = hooks or {}
    if h is None:
        h = _modulate(x, scale, shift, tb)
    proj = _mm_hooked(hooks.get("in_proj"), h, w_in, name="in_proj", tm=1024, tn=1536, tk=D_MODEL)
    w_out, w_glu = get_rest()
    y_a = _branch_a_fwd(proj, lw["conv_a"], tb)
    os_, lses = [], []
    for g, (_, dil) in enumerate(DILATIONS):
        o, lse = _attn_fwd(proj, bias_tabs[g], dil)
        os_.append(o)
        lses.append(lse)
    y_b = _attn_combine(os_, lses, proj, tb)
    lru_a, lru_b = _lru_gates_fwd(proj, lw["conv_c"], lw["conv_c_b"], lw["w_cat"], lw["b_cat"], lw["lam"], tb)
    lru_h, y_c = _lru_scan_fwd(lru_a, lru_b, proj, tb)
    s5_x, ylin, y_d = _s5_fwd(proj, s5m["w_bu"], s5m["w_cx"], s5m["a_row"], lw["d_skip"], w_glu, lw["b_glu"])
    ycat = jnp.concatenate([y_a, y_b, y_c, y_d], axis=1)
    saved = dict(x=x, h=h, proj=proj, os=os_, lses=lses, lru_a=lru_a, lru_h=lru_h, s5_x=s5_x, ylin=ylin, ycat=ycat)
    if target is not None:
        loss, *saved["head"] = _out_ln_loss(ycat, w_out, x, gate, lw["ln_g"], lw["ln_b"], target, t["tln"])
        return loss, None, saved
    x_next, saved["xhat"], saved["y"], saved["rstd"], h_next = _out_ln(
        ycat, w_out, x, gate, lw["ln_g"], lw["ln_b"], next_ada[1], next_ada[0], t["tln"])
    return x_next, h_next, saved


def _layer_bwd(dxn, sv, ada, w_in, w_out, w_glu, lw, s5m, bias_tabs, head_ones, hooks=None):
    proj = sv["proj"]
    s = proj.shape[0]
    t = _tiles(s)
    tb = t["tb"]
    shift, scale, gate = ada
    g = {}
    hook = lambda name: hooks[name](g) if hooks and name in hooks else None
    if "head" in sv:
        dyb, dxa, g["ln_g"], g["ln_b"], dgate = sv["head"]
        dycat = _mm(dyb, w_out, name="dycat", tb=True, tm=1024, tn=1024, tk=D_MODEL)
    else:
        dyb, dxa, g["ln_g"], g["ln_b"], dgate, dycat = _ln_bwd(dxn, sv["xhat"], sv["y"], sv["rstd"], lw["ln_g"], gate,
                                                               w_out, t["tln"])
    g["w_out"] = _mm_hooked(hook("dw_out"), sv["ycat"], dyb, name="dw_out", ta=True, out_dtype=WIRE_DTYPE,
                            tm=1024, tn=1024, tk=2048)
    da, dconv_a = _branch_a_bwd(dycat, proj, lw["conv_a"], tb)
    g["conv_a"] = dconv_a[0:3]
    pre = _attn_bwd_pre(dycat, sv["os"], sv["lses"], proj, head_ones, tb)
    dbg, dos, dms = pre[0], pre[1:4], pre[4:7]
    parts, dbias = [], []
    for gi, (_, dil) in enumerate(DILATIONS):
        hk = hook(f"attn_bwd_d{dil}")
        last = gi == len(DILATIONS) - 1
        dq, dk, dv, dbi, *got = _attn_bwd(proj, dos[gi], sv["lses"][gi], dms[gi], bias_tabs[gi], dil,
                                          carry=hk and hk[0], add=tuple(parts) if last else ())
        if hk:
            hk[1](got)
        parts.append((dq, dk, dv))
        dbias.append(dbi)
    dqkv = parts[-1]
    lmb, dcg = _lru_scan_bwd(sv["lru_a"], dycat, sv["lru_h"], proj, tb)
    dxc, dpre, xcb, dbcat, dlam = _lru_gates_bwd(proj, lmb, sv["lru_h"], lw["conv_c"], lw["conv_c_b"], lw["w_cat"],
                                                  lw["b_cat"], lw["lam"], tb)
    dwcat = _mm(xcb, dpre, name="dw_lru", ta=True, tn=1024)
    g["lru_wa"] = _diag_blocks(dwcat, LRU_HEADS, 0, BR)
    g["lru_wx"] = _diag_blocks(dwcat, LRU_HEADS, BR, BR)
    g["lru_ba"], g["lru_bx"], g["lru_lambda"] = dbcat[0, 0:BR], dbcat[0, BR:2 * BR], dlam[0]
    dcx, dconv_c, dccb = _conv_c_bwd(dxc, proj, lw["conv_c"], tb)
    g["conv_c"], g["conv_c_b"] = dconv_c[0:4], dccb[0]
    dyl, dus, ddg, gb, dtb, ddk, dbglu = _s5_tail_bwd(dycat, sv["ylin"], proj, lw["d_skip"], w_glu, lw["b_glu"], tb)
    g["s5_d"], g["s5_b_glu"] = ddk[0], dbglu[0]
    g["s5_w_glu"] = _mm(gb, dtb, name="dw_glu", ta=True, out_dtype=WIRE_DTYPE)
    du, dab, dwb8, dwc8 = _s5_core_bwd(dyl, proj, sv["s5_x"], s5m["w_dx"], s5m["w_du"], s5m["a_row"])
    per_group = lambda d8: _diag_blocks(d8, S5_PER, stacked=2 * S5_CHUNKS).reshape(2, S5_GROUPS, S5_CH, S5_STATE)
    dbb, dcc = per_group(dwb8), per_group(dwc8)
    from_bd = lambda half: jnp.swapaxes(dbb[half], 1, 2).reshape(S5_N, S5_CH)
    df_re, df_im, db_re, db_im = _s5_bbar_bwd(s5m["f_re"], s5m["f_im"], lw["b_re"], lw["b_im"],
                                              from_bd(0), from_bd(1))
    shp = (S5_GROUPS, S5_STATE)
    g["s5_lam_re"], g["s5_lam_im"], dlog_dt = _s5_disc_bwd(
        lw["lam_re"], lw["lam_im"], lw["log_dt"],
        (dab[:, 0:S5_N].reshape(shp), dab[:, S5_N:].reshape(shp), df_re.reshape(shp), df_im.reshape(shp)))
    g["s5_log_dt"] = dlog_dt[:, 0]
    g["s5_b_re"] = db_re.reshape(S5_GROUPS, S5_STATE, S5_CH)
    g["s5_b_im"] = db_im.reshape(S5_GROUPS, S5_STATE, S5_CH)
    g["s5_c_re"], g["s5_c_im"] = dcc[0], -dcc[1]
    dproj = _assemble_dproj(da, dqkv, dbg, dcx, dcg, du, dus, ddg, tb)
    g["w_in"] = _mm_hooked(hook("dw_in"), sv["h"], dproj, name="dw_in", ta=True, out_dtype=WIRE_DTYPE,
                           tm=1024, tn=1536, tk=2048)
    hk = hook("dh")
    dx, dshift, dscale, *got = _dh_mod_bwd(dproj, w_in, dxa, sv["x"], scale, carry=hk and hk[0])
    if hk:
        hk[1](got)
    g["ada"] = jnp.concatenate([dshift[0], dscale[0], dgate[0]])
    return dx, g, dbias


SMALL = ("rel_bias", "conv_a", "conv_c", "conv_c_b", "lru_wa", "lru_ba", "lru_wx", "lru_bx", "lru_lambda",
         "s5_lam_re", "s5_lam_im", "s5_log_dt", "s5_b_re", "s5_b_im", "s5_c_re", "s5_c_im", "s5_d", "s5_b_glu",
         "ln_g", "ln_b")
PER_LAYER_SMALL = SMALL[1:]


def _local_step(x, target, ada_rows, w_in, w_out, w_glu, p, comm=None):
    if comm is None:
        get_w_in = lambda l: w_in[l]
        get_rest = lambda l: (w_out[l], w_glu[l])
        fwd_hooks = bwd_hooks = lambda *_: None
    else:
        get_w_in, get_rest, fwd_hooks, bwd_hooks = comm.w_in, comm.rest, comm.fwd_hooks, comm.bwd_hooks
    s = x.shape[0]
    buckets = _bucket_maps()
    bias_tabs = _bias_tables(p["rel_bias"], buckets)
    head_ones = _block_diag(jnp.ones((ATT_HEADS, HEAD_DIM, HEAD_DIM), MXU_DTYPE))
    lws = [_layer_weights(p, l) for l in range(DEPTH)]
    s5ms = [_s5_matrices(lw) for lw in lws]
    adas = [tuple(ada_rows[l, k * D_MODEL:(k + 1) * D_MODEL][None] for k in range(3)) for l in range(DEPTH)]
    saved, h = [], None
    for l in range(DEPTH):
        last = l == DEPTH - 1
        x, h, sv = _layer_fwd(x, h, adas[l], get_w_in(l), functools.partial(get_rest, l), lws[l], s5ms[l], bias_tabs,
                              fwd_hooks(l), target if last else None, None if last else adas[l + 1])
        saved.append(sv)
    loss, dx = x, None
    grads = [None] * DEPTH
    dbias_sum = []
    for l in reversed(range(DEPTH)):
        dx, grads[l], dbias = _layer_bwd(dx, saved[l], adas[l], get_w_in(l), *get_rest(l), lws[l], s5ms[l],
                                         bias_tabs, head_ones, bwd_hooks(l, grads))
        dbias_sum.append(jnp.stack(dbias))
    drel = _rel_bias_grad(jnp.stack(dbias_sum), buckets)[:, 0:ATT_HEADS]
    small = {n: jnp.stack([grads[l][n] for l in range(DEPTH)]) for n in PER_LAYER_SMALL + ("ada",)}
    small["rel_bias"] = drel
    big = {n: [grads[l][n] for l in range(DEPTH)] for n in ("w_in", "w_out", "s5_w_glu")}
    return loss, dx, big, small


PACK_ROWS = 256


def _pack(parts):
    flat = jnp.concatenate([t.reshape(-1).astype(F32) for t in parts])
    n = flat.shape[0]
    rows = -(-n // (PACK_ROWS * 128)) * PACK_ROWS
    return jnp.pad(flat, (0, rows * 128 - n)).reshape(rows, 128)


def _unpack(packed, shapes):
    flat = packed.reshape(packed.shape[:-2] + (-1,))
    out, off = [], 0
    for shp in shapes:
        size = math.prod(shp)
        out.append(flat[..., off:off + size].reshape(flat.shape[:-1] + tuple(shp)))
        off += size
    return out


def _take_cols(t, chip, width):
    return lax.dynamic_slice_in_dim(t, chip * width, width, axis=t.ndim - 1)


class _Comm:
    IN_W, OUT_R, GLU_R = N_IN // N_CHIPS, D_MODEL // N_CHIPS, BR // N_CHIPS

    def __init__(self, w_in_b, w_out_b, w_glu_b):
        assert DEPTH == 2
        self.shards = (w_in_b, w_out_b, w_glu_b)
        in_w = self.IN_W
        self.w_in_full = {0: _run_exchange(_Gather(
            [(w_in_b, 0, lambda ref: ref.at[0], lambda ref, j: ref.at[:, pl.ds(j * in_w, in_w)])],
            [SDS((D_MODEL, N_IN), WIRE_DTYPE)]), "gather_w_in0")[0]}
        self.w_out_full = self.w_glu_full = None
        self.recv = {}

    def w_in(self, l):
        return self.w_in_full[l]

    def rest(self, l):
        return self.w_out_full[l], self.w_glu_full[l]

    def fwd_hooks(self, l):
        if l != 0:
            return None
        w_in_b, w_out_b, w_glu_b = self.shards
        in_w, out_r, glu_r = self.IN_W, self.OUT_R, self.GLU_R
        whole = lambda ref: ref
        items = [(w_out_b, 0, whole, lambda ref, j: ref.at[:, pl.ds(j * out_r, out_r), :]),
                 (w_glu_b, 1, whole, lambda ref, j: ref.at[:, pl.ds(j * glu_r, glu_r), :]),
                 (w_in_b, 2, lambda ref: ref.at[1], lambda ref, j: ref.at[:, pl.ds(j * in_w, in_w)])]
        shapes = [SDS((DEPTH, D_MODEL, D_MODEL), WIRE_DTYPE), SDS((DEPTH, BR, BR), WIRE_DTYPE),
                  SDS((D_MODEL, N_IN), WIRE_DTYPE)]

        def done(got):
            self.w_out_full, self.w_glu_full, self.w_in_full[1] = got

        return {"in_proj": (_Gather(items, shapes), done)}

    W_IN_ROWS = ((0, 1024), (1024, 512), (1536, 512))

    def _scatter(self, parts):
        in_w, out_r, glu_r = self.IN_W, self.OUT_R, self.GLU_R
        items, shapes, keys = [], [], []
        for oi, (name, l, arr, *rows) in enumerate(parts):
            if name == "w_in":
                r0, nr = rows[0] if rows else (0, D_MODEL)
                cut = functools.partial(lambda ref, j, r0, nr: ref.at[pl.ds(r0, nr), pl.ds(j * in_w, in_w)], r0=r0, nr=nr)
                shard = (nr, in_w)
            elif name == "w_out":
                cut, shard = (lambda ref, j: ref.at[pl.ds(j * out_r, out_r), :]), (out_r, D_MODEL)
            else:
                cut, shard = (lambda ref, j: ref.at[pl.ds(j * glu_r, glu_r), :]), (glu_r, BR)
            items.append((arr, oi, cut, lambda ref, j: ref.at[j]))
            shapes.append(SDS((N_CHIPS,) + shard, WIRE_DTYPE))
            keys.append((name, l) + ((rows[0][0],) if rows else ()))

        def done(got):
            self.recv.update(zip(keys, got))

        return _Exchange(items, shapes), done

    def received(self, name):
        return [self.recv[k] for k in sorted(k for k in self.recv if k[0] == name)]

    def bwd_hooks(self, l, grads):
        if l != 0:
            return None
        g1 = grads[1]
        w_in_part = lambda k: (lambda g: self._scatter([("w_in", 1, g1["w_in"], self.W_IN_ROWS[k])]))
        return {"dw_out": lambda g: self._scatter([("w_out", 1, g1["w_out"]), ("s5_w_glu", 1, g1["s5_w_glu"])]),
                "attn_bwd_d16": w_in_part(0), "attn_bwd_d4": w_in_part(1), "attn_bwd_d1": w_in_part(2),
                "dw_in": lambda g: self._scatter([("w_out", 0, g["w_out"]), ("s5_w_glu", 0, g["s5_w_glu"])]),
                "dh": lambda g: self._scatter([("w_in", 0, g["w_in"])])}


def kernel(x, c, rel_bias, w_ada, b_ada, w_in, conv_a, conv_c, conv_c_b, lru_wa, lru_ba, lru_wx, lru_bx, lru_lambda, s5_lam_re, s5_lam_im, s5_log_dt, s5_b_re, s5_b_im, s5_c_re, s5_c_im, s5_d, s5_w_glu, s5_b_glu, w_out, ln_g, ln_b, loss_target, m_rel_bias, m_w_ada, m_b_ada, m_w_in, m_conv_a, m_conv_c, m_conv_c_b, m_lru_wa, m_lru_ba, m_lru_wx, m_lru_bx, m_lru_lambda, m_s5_lam_re, m_s5_lam_im, m_s5_log_dt, m_s5_b_re, m_s5_b_im, m_s5_c_re, m_s5_c_im, m_s5_d, m_s5_w_glu, m_s5_b_glu, m_w_out, m_ln_g, m_ln_b, v_rel_bias, v_w_ada, v_b_ada, v_w_in, v_conv_a, v_conv_c, v_conv_c_b, v_lru_wa, v_lru_ba, v_lru_wx, v_lru_bx, v_lru_lambda, v_s5_lam_re, v_s5_lam_im, v_s5_log_dt, v_s5_b_re, v_s5_b_im, v_s5_c_re, v_s5_c_im, v_s5_d, v_s5_w_glu, v_s5_b_glu, v_w_out, v_ln_g, v_ln_b):
    args = dict(locals())
    names = ("rel_bias", "w_ada", "b_ada", "w_in", "conv_a", "conv_c", "conv_c_b", "lru_wa", "lru_ba", "lru_wx",
             "lru_bx", "lru_lambda", "s5_lam_re", "s5_lam_im", "s5_log_dt", "s5_b_re", "s5_b_im", "s5_c_re", "s5_c_im",
             "s5_d", "s5_w_glu", "s5_b_glu", "w_out", "ln_g", "ln_b")
    w = {n: args[n] for n in names}
    mom = {n: args["m_" + n] for n in names}
    var = {n: args["v_" + n] for n in names}
    chip = 2 * lax.axis_index("x") + lax.axis_index("y")
    me = 2 * chip + lax.axis_index("c")
    ada_w = 3 * D_MODEL // N_CHIPS
    conv_w = BR // N_CHIPS

    comm = _Comm(w["w_in"].astype(WIRE_DTYPE), w["w_out"].astype(WIRE_DTYPE), w["s5_w_glu"].astype(WIRE_DTYPE))

    taps = jnp.concatenate([w["conv_a"].reshape(DEPTH * 3, conv_w), w["conv_c"].reshape(DEPTH * 4, conv_w)])
    first = jnp.concatenate([c, jnp.pad(taps, ((0, 1), (0, D_MODEL - conv_w)))])
    got = _allgather8(first, "gather_c_taps").reshape(N_CHIPS, 2, 16, D_MODEL)
    c_all = got[:, :, 0].reshape(N_DEV, D_MODEL)
    taps_all = jnp.transpose(got[:, 0, 1:1 + DEPTH * 7, 0:conv_w], (1, 0, 2)).reshape(DEPTH * 7, BR)
    conv_a_f = taps_all[0:DEPTH * 3].reshape(DEPTH, 3, BR)
    conv_c_f = taps_all[DEPTH * 3:].reshape(DEPTH, 4, BR)

    cond_all = _silu_rows(c_all)
    ada_part = jnp.stack([_mm(cond_all, w["w_ada"][l], name="ada_fwd", tk=D_MODEL, tn=512,
                              bias=_take_cols(w["b_ada"][l][None], chip, ada_w)) for l in range(DEPTH)])
    ada_all = _allgather8(ada_part.reshape(DEPTH * N_DEV, ada_w), "gather_ada")
    ada_all = ada_all.reshape(N_CHIPS, 2, DEPTH, N_DEV, ada_w)[:, 0]
    ada_rows = lax.dynamic_index_in_dim(ada_all, me, axis=2, keepdims=False)
    ada_rows = jnp.transpose(ada_rows, (1, 0, 2)).reshape(DEPTH, 3 * D_MODEL)

    p = dict(w)
    p["conv_a"], p["conv_c"] = conv_a_f, conv_c_f
    loss, dx, _, small = _local_step(x[0], loss_target[0], ada_rows, None, None, None, p, comm)

    sums = [_sum_leading(comm.received(name), 256, "sum_chips") for name in ("w_in", "w_out", "s5_w_glu")]
    small_names = SMALL + ("ada",)
    small["loss"] = loss
    order = small_names + ("loss",)
    shapes = [small[n].shape for n in order]
    *others, gathered = _sibling_swap(sums, "swap_cores", _AllGather8(_pack([small[n] for n in order])))
    out = {}
    for name, mine, other in zip(("w_in", "w_out", "s5_w_glu"), sums, others):
        shp = w[name].shape
        flat = lambda t: t.reshape(-1, shp[-1])
        res = _adamw(flat(w[name]), [mine, other], flat(mom[name]), flat(var[name]), 128, "adamw_big")
        out[name] = [t.reshape(shp) for t in res]
    gathered = gathered.reshape(N_DEV, -1, 128)
    total = dict(zip(order, _unpack(_sum_leading([gathered], PACK_ROWS, "sum_devices"), shapes)))
    d_ada_all = _unpack(gathered, shapes)[order.index("ada")]
    g_small = {n: total[n] for n in SMALL}
    g_small["conv_a"] = _take_cols(total["conv_a"], chip, conv_w)
    g_small["conv_c"] = _take_cols(total["conv_c"], chip, conv_w)
    g_small["b_ada"] = total["ada"]
    g_w_ada = jnp.stack([_mm(cond_all, _take_cols(d_ada_all[:, l], chip, ada_w), name="dw_ada", ta=True, tn=ada_w)
                         for l in range(DEPTH)])
    upd_names = SMALL + ("b_ada",)
    upd_shapes = [w[n].shape for n in upd_names]
    res = _adamw(_pack([w[n] for n in upd_names]), [_pack([g_small[n] for n in upd_names])],
                 _pack([mom[n] for n in upd_names]), _pack([var[n] for n in upd_names]), PACK_ROWS, "adamw_small")
    for k, t in enumerate(res):
        for n, val in zip(upd_names, _unpack(t, upd_shapes)):
            out.setdefault(n, [None] * 4)[k] = val
    shp = w["w_ada"].shape
    flat = lambda t: t.reshape(-1, shp[-1])
    out["w_ada"] = [t.reshape(shp) for t in _adamw(flat(w["w_ada"]), [flat(g_w_ada)], flat(mom["w_ada"]),
                                                  flat(var["w_ada"]), 128, "adamw_ada")]
    return (total["loss"].reshape(()), dx[None]) + tuple(out[n][k] for k in range(4) for n in names)
```

```python
import functools
import math

import jax
import jax.numpy as jnp
from jax import lax
from jax.experimental import pallas as pl
from jax.experimental.pallas import tpu as pltpu
from jax.experimental.pallas import tpu_sc as plsc

F32 = jnp.float32
MXU_DTYPE = jnp.bfloat16
WIRE_DTYPE = jnp.bfloat16
SDS = jax.ShapeDtypeStruct
MESH = pl.DeviceIdType.MESH
ANY = pl.BlockSpec(memory_space=pl.ANY)
VMEM_LIMIT = 48 * 1024 * 1024

D_MODEL = 2048
DEPTH = 2
BR = 512
ATT_HEADS = 8
HEAD_DIM = 64
DILATIONS = ((128, 1), (512, 4), (2048, 16))
BLK = 128
REL_BUCKETS = 32
REL_MAX_DIST = 2048
LRU_HEADS = 8
LRU_C = 8.0
S5_CH = 16
S5_GROUPS = 32
S5_STATE = 64
S5_N = S5_GROUPS * S5_STATE
N_IN = 12 * BR
ALPHA = (2 * DEPTH) ** 0.25
LN_EPS = 1e-5
NEG = -1e30
ADAM_LR, ADAM_B1, ADAM_B2, ADAM_EPS, ADAM_WD, ADAM_STEP = 0.001, 0.9, 0.999, 1e-08, 0.01, 10
CB_AB, CB_AC, CB_AX, CB_AG, CB_Q, CB_K, CB_V, CB_BG, CB_CX, CB_CG, CB_DU, CB_DG = range(12)
N_CHIPS = 4
N_DEV = 8


def _params(n_axes=0):
    kw = {"dimension_semantics": ("arbitrary",) * n_axes} if n_axes else {}
    return pltpu.CompilerParams(vmem_limit_bytes=VMEM_LIMIT, **kw)


def _rows(tb, w, cb=0):
    return pl.BlockSpec((tb, w), lambda i: (i, cb))


def _prev8(tb, w, cb=0):
    return pl.BlockSpec((8, w), lambda i: (jnp.maximum(i * (tb // 8) - 1, 0), cb))


def _next8(tb, w, n_rows, cb=0):
    return pl.BlockSpec((8, w), lambda i: (jnp.minimum((i + 1) * (tb // 8), n_rows // 8 - 1), cb))


def _const(shape):
    return pl.BlockSpec(shape, lambda *_: (0,) * len(shape))


def _silu(x):
    return x * jax.nn.sigmoid(x)


def _dsilu(x):
    s = jax.nn.sigmoid(x)
    return s * (1.0 + x * (1.0 - s))


def _shift_down(cur, prev8, j):
    rolled = pltpu.roll(cur, j, 0)
    row = lax.broadcasted_iota(jnp.int32, (8, cur.shape[1]), 0)
    first = jnp.where(row < j, pltpu.roll(prev8, j, 0), rolled[0:8])
    return jnp.concatenate([first, rolled[8:]], axis=0)


def _shift_up(cur, next8, j):
    t = cur.shape[0]
    rolled = pltpu.roll(cur, t - j, 0)
    row = lax.broadcasted_iota(jnp.int32, (8, cur.shape[1]), 0)
    last = jnp.where(row >= 8 - j, pltpu.roll(next8, 8 - j, 0), rolled[t - 8:t])
    return jnp.concatenate([rolled[:t - 8], last], axis=0)


def _colsum(x):
    return jnp.sum(x, axis=0, keepdims=True)


def _init_acc(*refs):
    @pl.when(pl.program_id(0) == 0)
    def _():
        for r in refs:
            r[...] = jnp.zeros_like(r)


def _call(body, *, name, out_shape, grid, in_specs, out_specs, scratch_shapes, args, carry=None):
    out_shape, out_specs, in_specs = tuple(out_shape), tuple(out_specs), list(in_specs)
    scratch_shapes = list(scratch_shapes)
    if carry is None:
        return pl.pallas_call(body, name=name, out_shape=out_shape, grid=grid, in_specs=in_specs, out_specs=out_specs,
                              scratch_shapes=scratch_shapes, compiler_params=_params(len(grid)))(*args)
    n_in, n_out, n_scr = len(in_specs), len(out_shape), len(scratch_shapes)

    def wrapped(*refs):
        ins, refs = refs[:n_in], refs[n_in:]
        x_ins, refs = refs[:carry.n_in], refs[carry.n_in:]
        outs, refs = refs[:n_out], refs[n_out:]
        x_outs, refs = refs[:carry.n_out], refs[carry.n_out:]
        scr, x_sems = refs[:n_scr], refs[n_scr:]
        at = [pl.program_id(d) for d in range(len(grid))]
        first = functools.reduce(lambda p, q: p & q, [i == 0 for i in at])
        last = functools.reduce(lambda p, q: p & q, [i == g - 1 for i, g in zip(at, grid)])
        pl.when(first)(lambda: carry.start(x_ins, x_outs, x_sems))
        body(*ins, *outs, *scr)
        pl.when(last)(lambda: carry.wait(x_ins, x_outs, x_sems))

    return pl.pallas_call(
        wrapped, name=name, out_shape=out_shape + carry.out_shapes, grid=grid, in_specs=in_specs + [ANY] * carry.n_in,
        out_specs=out_specs + (ANY,) * carry.n_out, scratch_shapes=scratch_shapes + carry.scratch,
        compiler_params=_params(len(grid)))(*args, *carry.arrays)


def _mm(a, b, *, name, ta=False, tb=False, out_dtype=F32, tm=512, tn=512, tk=512, bias=None, carry=None):
    m, k = (a.shape[1], a.shape[0]) if ta else a.shape
    n = b.shape[0] if tb else b.shape[1]
    assert k == (b.shape[1] if tb else b.shape[0]), (name, a.shape, b.shape)
    tm, tn, tk = min(tm, m), min(tn, n), min(tk, k)
    nk = k // tk
    assert m % tm == 0 and n % tn == 0 and k % tk == 0, (name, m, n, k)

    def body(*refs):
        if bias is None:
            a_ref, b_ref, o_ref, acc = refs
        else:
            a_ref, b_ref, bias_ref, o_ref, acc = refs
        kk = pl.program_id(2)

        @pl.when(kk == 0)
        def _():
            acc[...] = jnp.zeros_like(acc)

        dims = (((0 if ta else 1,), (1 if tb else 0,)), ((), ()))
        acc[...] += lax.dot_general(a_ref[...].astype(MXU_DTYPE), b_ref[...].astype(MXU_DTYPE), dims,
                                    preferred_element_type=F32)

        @pl.when(kk == nk - 1)
        def _():
            r = acc[...]
            if bias is not None:
                r = r + bias_ref[...]
            o_ref[...] = r.astype(out_dtype)

    a_spec = (pl.BlockSpec((tk, tm), lambda i, j, kk: (kk, i)) if ta
              else pl.BlockSpec((tm, tk), lambda i, j, kk: (i, kk)))
    b_spec = (pl.BlockSpec((tn, tk), lambda i, j, kk: (j, kk)) if tb
              else pl.BlockSpec((tk, tn), lambda i, j, kk: (kk, j)))
    in_specs, args = [a_spec, b_spec], [a, b]
    if bias is not None:
        in_specs.append(pl.BlockSpec((1, tn), lambda i, j, kk: (0, j)))
        args.append(bias)
    out = _call(body, name=name, out_shape=[SDS((m, n), out_dtype)], grid=(m // tm, n // tn, nk), in_specs=in_specs,
                out_specs=[pl.BlockSpec((tm, tn), lambda i, j, kk: (i, j))],
                scratch_shapes=[pltpu.VMEM((tm, tn), F32)], args=args, carry=carry)
    return out[0] if carry is None else out


def _silu_rows(c_all):
    def body(c_ref, o_ref):
        o_ref[...] = _silu(c_ref[...])
    return pl.pallas_call(body, name="cond_silu", out_shape=SDS(c_all.shape, F32))(c_all)


def _modulate(x, scale, shift, tb):
    s, d = x.shape

    def body(x_ref, sc_ref, sh_ref, o_ref):
        o_ref[...] = (x_ref[...] * (1.0 + sc_ref[...]) + sh_ref[...]).astype(MXU_DTYPE)

    return pl.pallas_call(body, name="modulate", out_shape=SDS((s, d), MXU_DTYPE), grid=(s // tb,),
                          in_specs=[_rows(tb, d), _const((1, d)), _const((1, d))], out_specs=_rows(tb, d),
                          compiler_params=_params(1))(x, scale, shift)


def _out_ln(ycat, w_out, x, gate, ln_g, ln_b, next_scale, next_shift, tb):
    s, d = x.shape

    def body(yc_ref, w_ref, x_ref, gt_ref, g_ref, b_ref, sc_ref, sh_ref, xn_ref, xh_ref, y_ref, rs_ref, hn_ref):
        y = jnp.dot(yc_ref[...], w_ref[...], preferred_element_type=F32)
        res = ALPHA * x_ref[...] + (1.0 + gt_ref[...]) * y
        mu = jnp.mean(res, axis=-1, keepdims=True)
        cen = res - mu
        var = jnp.mean(cen * cen, axis=-1, keepdims=True)
        rstd = lax.rsqrt(var + LN_EPS)
        xhat = cen * rstd
        xn = xhat * g_ref[...] + b_ref[...]
        xn_ref[...] = xn
        xh_ref[...] = xhat
        y_ref[...] = y
        rs_ref[...] = rstd
        hn_ref[...] = (xn * (1.0 + sc_ref[...]) + sh_ref[...]).astype(MXU_DTYPE)

    big = SDS((s, d), F32)
    return pl.pallas_call(
        body, name="out_proj_ln", out_shape=(big, big, big, SDS((s, 1), F32), SDS((s, d), MXU_DTYPE)), grid=(s // tb,),
        in_specs=[_rows(tb, d), pl.BlockSpec((d, d), lambda i: (0, 0), pipeline_mode=pl.Buffered(1)), _rows(tb, d)]
        + [_const((1, d))] * 5,
        out_specs=(_rows(tb, d), _rows(tb, d), _rows(tb, d), _rows(tb, 1), _rows(tb, d)), compiler_params=_params(1),
    )(ycat, w_out, x, gate, ln_g, ln_b, next_scale, next_shift)


def _ln_bwd(dxn, xhat, y, rstd, ln_g, gate, w_out, tb):
    s, d = dxn.shape

    def body(dxn_ref, xh_ref, y_ref, rs_ref, g_ref, gt_ref, w_ref, dy_ref, dxa_ref, dg_ref, db_ref, dgt_ref, dyc_ref):
        _init_acc(dg_ref, db_ref, dgt_ref)
        dxn_t, xh = dxn_ref[...], xh_ref[...]
        dxh = dxn_t * g_ref[...]
        dres = rs_ref[...] * (dxh - jnp.mean(dxh, axis=-1, keepdims=True)
                              - xh * jnp.mean(dxh * xh, axis=-1, keepdims=True))
        dyb = ((1.0 + gt_ref[...]) * dres).astype(MXU_DTYPE)
        dy_ref[...] = dyb
        dxa_ref[...] = ALPHA * dres
        dg_ref[...] += _colsum(dxn_t * xh)
        db_ref[...] += _colsum(dxn_t)
        dgt_ref[...] += _colsum(dres * y_ref[...])
        dyc_ref[...] = lax.dot_general(dyb, w_ref[...], (((1,), (1,)), ((), ())), preferred_element_type=F32)

    vec = SDS((1, d), F32)
    return pl.pallas_call(
        body, name="ln_bwd_dycat", out_shape=(SDS((s, d), MXU_DTYPE), SDS((s, d), F32), vec, vec, vec, SDS((s, d), F32)),
        grid=(s // tb,),
        in_specs=[_rows(tb, d), _rows(tb, d), _rows(tb, d), _rows(tb, 1), _const((1, d)), _const((1, d)),
                  pl.BlockSpec((d, d), lambda i: (0, 0), pipeline_mode=pl.Buffered(1))],
        out_specs=(_rows(tb, d), _rows(tb, d), _const((1, d)), _const((1, d)), _const((1, d)), _rows(tb, d)),
        compiler_params=_params(1))(dxn, xhat, y, rstd, ln_g, gate, w_out)


def _dh_mod_bwd(dproj, w_in, dxa, x, scale, carry=None):
    s, d = dxa.shape
    k = dproj.shape[1]
    tm, tn, tk = min(1024, s), 1024, 1536
    nk = k // tk
    assert s % tm == 0 and d % tn == 0 and k % tk == 0

    def body(a_ref, b_ref, dxa_ref, x_ref, sc_ref, dx_ref, dsh_ref, dsc_ref, acc):
        i, kk = pl.program_id(1), pl.program_id(2)

        @pl.when(kk == 0)
        def _():
            acc[...] = jnp.zeros_like(acc)

        @pl.when((kk == 0) & (i == 0))
        def _():
            dsh_ref[...] = jnp.zeros_like(dsh_ref)
            dsc_ref[...] = jnp.zeros_like(dsc_ref)

        acc[...] += lax.dot_general(a_ref[...], b_ref[...], (((1,), (1,)), ((), ())), preferred_element_type=F32)

        @pl.when(kk == nk - 1)
        def _():
            dh_t = acc[...]
            dx_ref[...] = dxa_ref[...] + dh_t * (1.0 + sc_ref[...])
            dsh_ref[...] += _colsum(dh_t)
            dsc_ref[...] += _colsum(dh_t * x_ref[...])

    tile = pl.BlockSpec((tm, tn), lambda j, i, kk: (i, j))
    vec = pl.BlockSpec((1, tn), lambda j, i, kk: (0, j))
    return _call(
        body, name="dh", out_shape=(SDS((s, d), F32), SDS((1, d), F32), SDS((1, d), F32)),
        grid=(d // tn, s // tm, nk),
        in_specs=[pl.BlockSpec((tm, tk), lambda j, i, kk: (i, kk)), pl.BlockSpec((tn, tk), lambda j, i, kk: (j, kk)),
                  tile, tile, vec],
        out_specs=(tile, vec, vec), scratch_shapes=[pltpu.VMEM((tm, tn), F32)],
        args=(dproj, w_in, dxa, x, scale), carry=carry)


def _out_ln_loss(ycat, w_out, x, gate, ln_g, ln_b, target, tb):
    s, d = x.shape

    def body(yc_ref, w_ref, x_ref, gt_ref, g_ref, b_ref, t_ref, l_ref, dy_ref, dxa_ref, dg_ref, db_ref, dgt_ref):
        _init_acc(l_ref, dg_ref, db_ref, dgt_ref)
        y = jnp.dot(yc_ref[...], w_ref[...], preferred_element_type=F32)
        res = ALPHA * x_ref[...] + (1.0 + gt_ref[...]) * y
        cen = res - jnp.mean(res, axis=-1, keepdims=True)
        rstd = lax.rsqrt(jnp.mean(cen * cen, axis=-1, keepdims=True) + LN_EPS)
        xh = cen * rstd
        err = xh * g_ref[...] + b_ref[...] - t_ref[...]
        l_ref[...] += (0.5 / d) * jnp.sum(err * err, keepdims=True)
        dxn_t = err * (1.0 / d)
        dxh = dxn_t * g_ref[...]
        dres = rstd * (dxh - jnp.mean(dxh, axis=-1, keepdims=True) - xh * jnp.mean(dxh * xh, axis=-1, keepdims=True))
        dy_ref[...] = ((1.0 + gt_ref[...]) * dres).astype(MXU_DTYPE)
        dxa_ref[...] = ALPHA * dres
        dg_ref[...] += _colsum(dxn_t * xh)
        db_ref[...] += _colsum(dxn_t)
        dgt_ref[...] += _colsum(dres * y)

    vec = SDS((1, d), F32)
    return pl.pallas_call(
        body, name="out_proj_ln_loss", out_shape=(SDS((1, 1), F32), SDS((s, d), MXU_DTYPE), SDS((s, d), F32), vec, vec, vec),
        grid=(s // tb,),
        in_specs=[_rows(tb, d), pl.BlockSpec((d, d), lambda i: (0, 0), pipeline_mode=pl.Buffered(1)), _rows(tb, d),
                  _const((1, d)), _const((1, d)), _const((1, d)), _rows(tb, d)],
        out_specs=(_const((1, 1)), _rows(tb, d), _rows(tb, d), _const((1, d)), _const((1, d)), _const((1, d))),
        compiler_params=_params(1))(ycat, w_out, x, gate, ln_g, ln_b, target)


def _conv_taps(u, up, w_ref, width):
    out = w_ref[width - 1:width, :] * u
    for j in range(width - 2, -1, -1):
        out = out + w_ref[j:j + 1, :] * _shift_down(u, up, width - 1 - j)
    return out


def _conv_taps_t(g, gn, w_ref, width):
    out = w_ref[width - 1:width, :] * g
    for j in range(width - 2, -1, -1):
        out = out + w_ref[j:j + 1, :] * _shift_up(g, gn, width - 1 - j)
    return out


def _conv_wgrad(dw_ref, g, u, up, width):
    dw_ref[width - 1:width, :] += _colsum(g * u)
    for j in range(width - 1):
        dw_ref[j:j + 1, :] += _colsum(g * _shift_down(u, up, width - 1 - j))


def _branch_a_fwd(proj, conv_w, tb):
    s = proj.shape[0]

    def body(ab, ac, ax, ag, acp, axp, w_ref, o_ref):
        has_prev = (pl.program_id(0) > 0).astype(F32)
        u = ac[...] * ax[...]
        up = acp[...] * axp[...] * has_prev
        o_ref[...] = (ab[...] * _conv_taps(u, up, w_ref, 3) * _silu(ag[...])).astype(MXU_DTYPE)

    return pl.pallas_call(
        body, name="branch_a_fwd", out_shape=SDS((s, BR), MXU_DTYPE), grid=(s // tb,),
        in_specs=[_rows(tb, BR, CB_AB), _rows(tb, BR, CB_AC), _rows(tb, BR, CB_AX), _rows(tb, BR, CB_AG),
                  _prev8(tb, BR, CB_AC), _prev8(tb, BR, CB_AX), _const((8, BR))],
        out_specs=_rows(tb, BR), compiler_params=_params(1))(proj, proj, proj, proj, proj, proj, conv_w)


def _branch_a_bwd(dycat, proj, conv_w, tb):
    s = proj.shape[0]

    def body(dy, dyn, ab, abn, ag, agn, ac, acp, ax, axp, w_ref, o_ref, dw_ref):
        _init_acc(dw_ref)
        i = pl.program_id(0)
        has_prev = (i > 0).astype(F32)
        has_next = (i < pl.num_programs(0) - 1).astype(F32)
        u = ac[...] * ax[...]
        up = acp[...] * axp[...] * has_prev
        v = _conv_taps(u, up, w_ref, 3)
        sg = _silu(ag[...])
        dv = dy[...] * ab[...] * sg
        dvn = dyn[...] * abn[...] * _silu(agn[...]) * has_next
        du = _conv_taps_t(dv, dvn, w_ref, 3)
        o_ref[:, 0:BR] = (dy[...] * v * sg).astype(MXU_DTYPE)
        o_ref[:, BR:2 * BR] = (du * ax[...]).astype(MXU_DTYPE)
        o_ref[:, 2 * BR:3 * BR] = (du * ac[...]).astype(MXU_DTYPE)
        o_ref[:, 3 * BR:4 * BR] = (dy[...] * ab[...] * v * _dsilu(ag[...])).astype(MXU_DTYPE)
        _conv_wgrad(dw_ref, dv, u, up, 3)

    return pl.pallas_call(
        body, name="branch_a_bwd", out_shape=(SDS((s, 4 * BR), MXU_DTYPE), SDS((8, BR), F32)), grid=(s // tb,),
        in_specs=[_rows(tb, BR, 0), _next8(tb, BR, s, 0),
                  _rows(tb, BR, CB_AB), _next8(tb, BR, s, CB_AB), _rows(tb, BR, CB_AG), _next8(tb, BR, s, CB_AG),
                  _rows(tb, BR, CB_AC), _prev8(tb, BR, CB_AC), _rows(tb, BR, CB_AX), _prev8(tb, BR, CB_AX),
                  _const((8, BR))],
        out_specs=(_rows(tb, 4 * BR), _const((8, BR))), compiler_params=_params(1),
    )(dycat, dycat, proj, proj, proj, proj, proj, proj, proj, proj, conv_w)


def _t5_bucket(dist):
    max_exact = REL_BUCKETS // 2
    nf = jnp.maximum(dist, 1).astype(F32)
    large = max_exact + (jnp.log(nf / max_exact) / math.log(REL_MAX_DIST / max_exact)
                         * (REL_BUCKETS - max_exact)).astype(jnp.int32)
    large = jnp.minimum(large, REL_BUCKETS - 1)
    return jnp.where(dist < max_exact, dist, large)


def _bucket_maps():
    maps = []
    i = jnp.arange(BLK)[:, None]
    j = jnp.arange(2 * BLK)[None, :]
    delta = i + BLK - j
    for window, dil in DILATIONS:
        span = window // dil
        bucket = _t5_bucket(jnp.clip(delta, 0, span) * dil)
        maps.append(jnp.where((delta >= 0) & (delta <= span), bucket, -1))
    return jnp.stack(maps).astype(jnp.int32)


def _bias_tables(rel_bias, buckets):
    n_pat = len(DILATIONS)

    def body(rb_ref, bk_ref, o_ref):
        for g in range(n_pat):
            bk = bk_ref[g]
            for h in range(ATT_HEADS):
                def per_bucket(b, acc):
                    return jnp.where(bk == b, rb_ref[b, h], acc)
                o_ref[g, h] = lax.fori_loop(0, REL_BUCKETS, per_bucket, jnp.full((BLK, 2 * BLK), NEG, F32))

    return pl.pallas_call(
        body, name="bias_tables", out_shape=SDS((n_pat, ATT_HEADS, BLK, 2 * BLK), F32),
        in_specs=[pl.BlockSpec(memory_space=pltpu.SMEM), pl.BlockSpec(memory_space=pltpu.VMEM)],
        compiler_params=_params())(rel_bias, buckets)


def _head_masks():
    lane = lax.broadcasted_iota(jnp.int32, (1, 2 * HEAD_DIM), 1)
    return [(lane < HEAD_DIM).astype(F32), (lane >= HEAD_DIM).astype(F32)]


def _strided(base, size, dil):
    return pl.ds(base, size, stride=dil) if dil > 1 else pl.ds(pl.multiple_of(base, BLK), size)


def _attn_groups(s, dil):
    return max(1, min(2048, s) // (dil * BLK))


def _attn_fwd(proj, bias, dil):
    s = proj.shape[0]
    grp = _attn_groups(s, dil)
    u1 = dil * BLK
    unit = grp * u1
    nb = s // unit
    w = 2 * HEAD_DIM
    q0, k0, v0 = (cb * (BR // w) for cb in (CB_Q, CB_K, CB_V))

    def body(q_ref, kc_ref, kp_ref, vc_ref, vp_ref, bias_ref, o_ref, lse_ref, kbuf, vbuf):
        n = pl.program_id(1)
        col = lax.broadcasted_iota(jnp.int32, (1, 2 * BLK), 1)
        masks = _head_masks()
        kbuf[0:u1, :] = kp_ref[...]
        kbuf[u1:, :] = kc_ref[...]
        vbuf[0:u1, :] = vp_ref[...]
        vbuf[u1:, :] = vc_ref[...]

        def per_r(t, carry):
            j = t // dil
            base = j * u1 + t % dil
            rows = _strided(base, BLK, dil)
            no_prev = jnp.where((n == 0) & (j == 0) & (col < BLK), NEG, 0.0)
            q = q_ref[rows, :] * (HEAD_DIM ** -0.5)
            k = kbuf[_strided(base, 2 * BLK, dil), :].astype(MXU_DTYPE)
            v = vbuf[_strided(base, 2 * BLK, dil), :].astype(MXU_DTYPE)
            q2 = jnp.concatenate([q * masks[0], q * masks[1]], axis=0).astype(MXU_DTYPE)
            sc = lax.dot_general(q2, k, (((1,), (1,)), ((), ())), preferred_element_type=F32)
            sc = sc + jnp.concatenate([bias_ref[0], bias_ref[1]], axis=0) + no_prev
            mx = jnp.max(sc, axis=-1, keepdims=True)
            p = jnp.exp(sc - mx)
            l = jnp.sum(p, axis=-1, keepdims=True)
            o2 = jnp.dot((p / l).astype(MXU_DTYPE), v, preferred_element_type=F32)
            lse2 = mx + jnp.log(l)
            o_ref[rows, :] = o2[0:BLK] * masks[0] + o2[BLK:2 * BLK] * masks[1]
            lse_ref[rows, :] = lse2[0:BLK] * masks[0] + lse2[BLK:2 * BLK] * masks[1]
            return carry

        lax.fori_loop(0, grp * dil, per_r, 0, unroll=8)

    cur = lambda c0: pl.BlockSpec((unit, w), lambda hp, n: (n, c0 + hp))
    prev = lambda c0: pl.BlockSpec((u1, w), lambda hp, n: (jnp.maximum(n * grp - 1, 0), c0 + hp))
    out = pl.BlockSpec((unit, w), lambda hp, n: (n, hp))
    return pl.pallas_call(
        body, name=f"attn_fwd_d{dil}", out_shape=(SDS((s, BR), F32), SDS((s, BR), F32)), grid=(BR // w, nb),
        in_specs=[cur(q0), cur(k0), prev(k0), cur(v0), prev(v0),
                  pl.BlockSpec((2, BLK, 2 * BLK), lambda hp, n: (hp, 0, 0))],
        out_specs=(out, out),
        scratch_shapes=[pltpu.VMEM((unit + u1, w), F32), pltpu.VMEM((unit + u1, w), F32)],
        compiler_params=_params(2))(proj, proj, proj, proj, proj, bias)


def _softmax3(l0, l1, l2):
    mx = jnp.maximum(jnp.maximum(l0, l1), l2)
    e0, e1, e2 = jnp.exp(l0 - mx), jnp.exp(l1 - mx), jnp.exp(l2 - mx)
    inv = 1.0 / (e0 + e1 + e2)
    return e0 * inv, e1 * inv, e2 * inv


def _attn_combine(os_, lses, proj, tb):
    s = proj.shape[0]

    def body(o0, o1, o2, l0, l1, l2, bg, y_ref):
        w0, w1, w2 = _softmax3(l0[...], l1[...], l2[...])
        attn = w0 * o0[...] + w1 * o1[...] + w2 * o2[...]
        y_ref[...] = (attn * _silu(bg[...])).astype(MXU_DTYPE)

    return pl.pallas_call(
        body, name="attn_combine", out_shape=SDS((s, BR), MXU_DTYPE), grid=(s // tb,),
        in_specs=[_rows(tb, BR)] * 6 + [_rows(tb, BR, CB_BG)], out_specs=_rows(tb, BR),
        compiler_params=_params(1))(*os_, *lses, proj)


def _attn_bwd_pre(dycat, os_, lses, proj, head_ones, tb):
    s = proj.shape[0]

    def body(dy, o0, o1, o2, l0, l1, l2, bg, e_ref, dbg_ref, do0, do1, do2, dm0, dm1, dm2):
        w0, w1, w2 = _softmax3(l0[...], l1[...], l2[...])
        attn = w0 * o0[...] + w1 * o1[...] + w2 * o2[...]
        dattn = dy[...] * _silu(bg[...])
        dbg_ref[...] = (dy[...] * attn * _dsilu(bg[...])).astype(MXU_DTYPE)
        prod = dattn * attn
        hi = prod.astype(MXU_DTYPE)
        lo = (prod - hi.astype(F32)).astype(MXU_DTYPE)
        tot = (jnp.dot(hi, e_ref[...], preferred_element_type=F32)
               + jnp.dot(lo, e_ref[...], preferred_element_type=F32))
        for wg, do_ref, dm_ref in ((w0, do0, dm0), (w1, do1, dm1), (w2, do2, dm2)):
            do_ref[...] = wg * dattn
            dm_ref[...] = wg * tot

    big = SDS((s, BR), F32)
    return pl.pallas_call(
        body, name="attn_bwd_pre", out_shape=(SDS((s, BR), MXU_DTYPE),) + (big,) * 6, grid=(s // tb,),
        in_specs=[_rows(tb, BR, 1)] + [_rows(tb, BR)] * 6 + [_rows(tb, BR, CB_BG), _const((BR, BR))],
        out_specs=(_rows(tb, BR),) * 7, compiler_params=_params(1))(dycat, *os_, *lses, proj, head_ones)


def _attn_bwd(proj, do, lse, dm, bias, dil, carry=None, add=()):
    s = proj.shape[0]
    grp = _attn_groups(s, dil)
    u1 = dil * BLK
    unit = grp * u1
    nb = s // unit
    w = 2 * HEAD_DIM
    q0, k0, v0 = (cb * (BR // w) for cb in (CB_Q, CB_K, CB_V))
    n_add = len(add)

    def body(q_ref, kc_ref, kp_ref, vc_ref, vp_ref, do_ref, lse_ref, dm_ref, bias_ref, *rest):
        more, (dq_ref, dk_ref, dv_ref, dbias_ref, kbuf, vbuf, stage_k, stage_v) = rest[:3 * n_add], rest[3 * n_add:]
        more_q, more_k, more_v = more[0::3], more[1::3], more[2::3]
        plus = lambda val, refs, rows: functools.reduce(lambda acc, r: acc + r[rows, :], refs, val)
        n = pl.program_id(1)
        col = lax.broadcasted_iota(jnp.int32, (1, 2 * BLK), 1)
        masks = _head_masks()

        @pl.when(n == 0)
        def _():
            dbias_ref[...] = jnp.zeros_like(dbias_ref)
            stage_k[...] = jnp.zeros_like(stage_k)
            stage_v[...] = jnp.zeros_like(stage_v)

        for out_ref, stage, more_ in ((dk_ref, stage_k, more_k), (dv_ref, stage_v, more_v)):
            if grp > 1:
                out_ref[0:unit - u1, :] = plus(stage[u1:unit, :], more_, slice(0, unit - u1))
            stage[0:u1, :] = stage[unit:unit + u1, :]

        @pl.when(n < nb)
        def _():
            kbuf[0:u1, :] = kp_ref[...]
            kbuf[u1:, :] = kc_ref[...]
            vbuf[0:u1, :] = vp_ref[...]
            vbuf[u1:, :] = vc_ref[...]

            def per_r(t, carry):
                j = t // dil
                base = j * u1 + t % dil
                rows = _strided(base, BLK, dil)
                rows_hi = _strided(base + u1, BLK, dil)
                no_prev = jnp.where((n == 0) & (j == 0) & (col < BLK), NEG, 0.0)
                q = q_ref[rows, :] * (HEAD_DIM ** -0.5)
                k = kbuf[_strided(base, 2 * BLK, dil), :].astype(MXU_DTYPE)
                v = vbuf[_strided(base, 2 * BLK, dil), :].astype(MXU_DTYPE)
                do_t, lse_t, dm_t = do_ref[rows, :], lse_ref[rows, :], dm_ref[rows, :]
                stack = lambda t: jnp.concatenate([t * masks[0], t * masks[1]], axis=0).astype(MXU_DTYPE)
                per_head = lambda t: jnp.concatenate([t[:, 0:1], t[:, HEAD_DIM:HEAD_DIM + 1]], axis=0)
                q2, do2 = stack(q), stack(do_t)
                sc = lax.dot_general(q2, k, (((1,), (1,)), ((), ())), preferred_element_type=F32)
                p = jnp.exp(sc + jnp.concatenate([bias_ref[0], bias_ref[1]], axis=0) + no_prev - per_head(lse_t))
                dp = lax.dot_general(do2, v, (((1,), (1,)), ((), ())), preferred_element_type=F32)
                ds = p * (dp - per_head(dm_t))
                dbias_ref[0] += ds[0:BLK]
                dbias_ref[1] += ds[BLK:2 * BLK]
                dsb, pb = ds.astype(MXU_DTYPE), p.astype(MXU_DTYPE)
                dq2 = jnp.dot(dsb, k, preferred_element_type=F32)
                dk_acc = lax.dot_general(dsb, q2, (((0,), (0,)), ((), ())), preferred_element_type=F32)
                dv_acc = lax.dot_general(pb, do2, (((0,), (0,)), ((), ())), preferred_element_type=F32)
                dq_ref[rows, :] = plus((dq2[0:BLK] * masks[0] + dq2[BLK:2 * BLK] * masks[1]) * (HEAD_DIM ** -0.5),
                                       more_q, rows)
                stage_k[rows, :] = stage_k[rows, :] + dk_acc[0:BLK]
                stage_v[rows, :] = stage_v[rows, :] + dv_acc[0:BLK]
                stage_k[rows_hi, :] = dk_acc[BLK:2 * BLK]
                stage_v[rows_hi, :] = dv_acc[BLK:2 * BLK]
                return carry

            lax.fori_loop(0, grp * dil, per_r, 0, unroll=8)

        dk_ref[unit - u1:unit, :] = plus(stage_k[0:u1, :], more_k, slice(unit - u1, unit))
        dv_ref[unit - u1:unit, :] = plus(stage_v[0:u1, :], more_v, slice(unit - u1, unit))

    qn = lambda n: jnp.minimum(n, nb - 1)
    cur = lambda c0: pl.BlockSpec((unit, w), lambda hp, n: (qn(n), c0 + hp))
    prev = lambda c0: pl.BlockSpec((u1, w), lambda hp, n: (jnp.maximum(qn(n) * grp - 1, 0), c0 + hp))
    row = pl.BlockSpec((unit, w), lambda hp, n: (qn(n), hp))
    late = pl.BlockSpec((unit, w), lambda hp, n: (jnp.maximum(n - 1, 0), hp))
    tab = pl.BlockSpec((2, BLK, 2 * BLK), lambda hp, n: (hp, 0, 0))
    big = SDS((s, BR), F32)
    return _call(
        body, name=f"attn_bwd_d{dil}", out_shape=(big, big, big, SDS((ATT_HEADS, BLK, 2 * BLK), F32)),
        grid=(BR // w, nb + 1),
        in_specs=[cur(q0), cur(k0), prev(k0), cur(v0), prev(v0), row, row, row, tab] + [row, late, late] * n_add,
        out_specs=(row, late, late, tab),
        scratch_shapes=[pltpu.VMEM((unit + u1, w), F32)] * 4,
        args=(proj, proj, proj, proj, proj, do, lse, dm, bias) + tuple(t for part in add for t in part), carry=carry)


def _rel_bias_grad(dbias, buckets):
    def body(db_ref, bk_ref, o_ref):
        row = lax.broadcasted_iota(jnp.int32, (REL_BUCKETS, 128), 0)
        lane = lax.broadcasted_iota(jnp.int32, (REL_BUCKETS, 128), 1)

        def per_bucket(b, acc):
            for g in range(len(DILATIONS)):
                hit = bk_ref[g] == b
                for h in range(ATT_HEADS):
                    both = db_ref[0, g, h] + db_ref[1, g, h]
                    val = jnp.sum(jnp.where(hit, both, 0.0), keepdims=True)
                    acc = acc + jnp.where((row == b) & (lane == h), val, 0.0)
            return acc

        o_ref[...] = lax.fori_loop(0, REL_BUCKETS, per_bucket, jnp.zeros((REL_BUCKETS, 128), F32))

    assert dbias.shape[0] == DEPTH == 2
    return pl.pallas_call(body, name="rel_bias_grad", out_shape=SDS((REL_BUCKETS, 128), F32),
                          compiler_params=_params())(dbias, buckets)


def _scan_rows(a_ref, b_ref, o_ref, carry, *, reverse):
    tb = a_ref.shape[0]
    order = range(7, -1, -1) if reverse else range(8)

    @pl.when(pl.program_id(0) == 0)
    def _():
        carry[...] = jnp.zeros_like(carry)

    def group(gi, h):
        r0 = pl.multiple_of((tb // 8 - 1 - gi if reverse else gi) * 8, 8)
        a8, b8 = a_ref[pl.ds(r0, 8), :], b_ref[pl.ds(r0, 8), :]
        rows = [None] * 8
        for k in order:
            if reverse:
                rows[k] = b8[k:k + 1] + h
                h = a8[k:k + 1] * rows[k]
            else:
                h = a8[k:k + 1] * h + b8[k:k + 1]
                rows[k] = h
        o_ref[pl.ds(r0, 8), :] = jnp.concatenate(rows, axis=0)
        return h

    carry[...] = lax.fori_loop(0, tb // 8, group, carry[...])


def _lru_scan_fwd(a, b, proj, tb):
    s = a.shape[0]

    def body(a_ref, b_ref, g_ref, h_ref, y_ref, carry):
        _scan_rows(a_ref, b_ref, h_ref, carry, reverse=False)
        y_ref[...] = (h_ref[...] * _silu(g_ref[...])).astype(MXU_DTYPE)

    return pl.pallas_call(
        body, name="lru_scan", out_shape=(SDS((s, BR), F32), SDS((s, BR), MXU_DTYPE)), grid=(s // tb,),
        in_specs=[_rows(tb, BR), _rows(tb, BR), _rows(tb, BR, CB_CG)], out_specs=(_rows(tb, BR), _rows(tb, BR)),
        scratch_shapes=[pltpu.VMEM((1, BR), F32)], compiler_params=_params(1))(a, b, proj)


def _lru_scan_bwd(a, dycat, h, proj, tb):
    s = a.shape[0]
    nt = s // tb

    def body(a_ref, dy_ref, h_ref, g_ref, l_ref, dg_ref, carry, dh_buf):
        dh_buf[...] = dy_ref[...] * _silu(g_ref[...])
        dg_ref[...] = (dy_ref[...] * h_ref[...] * _dsilu(g_ref[...])).astype(MXU_DTYPE)
        _scan_rows(a_ref, dh_buf, l_ref, carry, reverse=True)

    rev = lambda cb=0: pl.BlockSpec((tb, BR), lambda i: (nt - 1 - i, cb))
    return pl.pallas_call(
        body, name="lru_scan_bwd", out_shape=(SDS((s, BR), F32), SDS((s, BR), MXU_DTYPE)), grid=(nt,),
        in_specs=[rev(), rev(2), rev(), rev(CB_CG)], out_specs=(rev(), rev()),
        scratch_shapes=[pltpu.VMEM((1, BR), F32), pltpu.VMEM((tb, BR), F32)],
        compiler_params=_params(1))(a, dycat, h, proj)


def _scan_tile(s):
    return min(512, s)


def _load_chunked(ref, t0, pt):
    ln = pt // 8
    return jnp.concatenate([ref[pl.ds(t0 + j, 8, stride=ln), :] for j in range(ln)], axis=0)


def _store_natural(ref, t0, pt, val):
    ln = pt // 8
    for j in range(ln):
        ref[pl.ds(t0 + j, 8, stride=ln), :] = val[j * 8:(j + 1) * 8]


def _scan_tile_in_place(a_ref, x_ref, carry, pw, *, reverse):
    ch2 = x_ref.shape[1]
    ch = ch2 // 2
    ln = x_ref.shape[0] // 8
    ar = a_ref[:, 0:ch]
    ai = -a_ref[:, ch:ch2] if reverse else a_ref[:, ch:ch2]

    def cmul(pr, pi, xr, xi):
        return pr * xr - pi * xi, pr * xi + pi * xr

    @pl.when(pl.program_id(0) == 0)
    def _():
        carry[...] = jnp.zeros_like(carry)

        def fill(j, p):
            pw[pl.ds(j, 1), 0:ch] = p[0]
            pw[pl.ds(j, 1), ch:ch2] = p[1]
            return cmul(ar, ai, *p)

        lax.fori_loop(0, ln, fill, (ar, ai))

    def rows_of(j):
        return pl.ds(pl.multiple_of((ln - 1 - j if reverse else j) * 8, 8), 8)

    def local(j, x):
        rows = rows_of(j)
        nr, ni = cmul(ar, ai, *x)
        xr, xi = nr + x_ref[rows, 0:ch], ni + x_ref[rows, ch:ch2]
        x_ref[rows, 0:ch] = xr
        x_ref[rows, ch:ch2] = xi
        return xr, xi

    zero = jnp.zeros((8, ch), F32)
    er, ei = lax.fori_loop(0, ln, local, (zero, zero), unroll=2)
    apr, api = pw[ln - 1:ln, 0:ch], pw[ln - 1:ln, ch:ch2]
    cr, ci = carry[:, 0:ch], carry[:, ch:ch2]
    into_r, into_i = [None] * 8, [None] * 8
    for c in (range(7, -1, -1) if reverse else range(8)):
        into_r[c], into_i[c] = cr, ci
        pr, pi = cmul(apr, api, cr, ci)
        cr, ci = er[c:c + 1] + pr, ei[c:c + 1] + pi
    carry[:, 0:ch] = cr
    carry[:, ch:ch2] = ci
    into_r, into_i = jnp.concatenate(into_r, axis=0), jnp.concatenate(into_i, axis=0)

    def fix(j, carry_):
        rows = rows_of(j)
        dr, di = cmul(pw[pl.ds(j, 1), 0:ch], pw[pl.ds(j, 1), ch:ch2], into_r, into_i)
        x_ref[rows, 0:ch] += dr
        x_ref[rows, ch:ch2] += di
        return carry_

    lax.fori_loop(0, ln, fix, 0, unroll=2)


def _neg_expm1(z):
    series = -z * (1.0 + z * (0.5 + z * (1.0 / 6 + z * (1.0 / 24 + z * (1.0 / 120)))))
    return jnp.where(z > -0.05, series, 1.0 - jnp.exp(z))


def _lru_gate(xc, pre_r, pre_i, lam):
    log_a = -LRU_C * jax.nn.sigmoid(pre_r) * jax.nn.softplus(-lam)
    return jnp.exp(log_a), jnp.sqrt(_neg_expm1(2.0 * log_a)) * jax.nn.sigmoid(pre_i) * xc


def _lru_gates_fwd(proj, conv_w, conv_b, w_cat, b_cat, lam, tb):
    s = proj.shape[0]

    def body(cx, cxp, w_ref, cb_ref, wc_ref, bc_ref, lam_ref, a_ref, b_ref):
        has_prev = (pl.program_id(0) > 0).astype(F32)
        xc = _conv_taps(cx[...], cxp[...] * has_prev, w_ref, 4) + cb_ref[...]
        pre = jnp.dot(xc.astype(MXU_DTYPE), wc_ref[...], preferred_element_type=F32) + bc_ref[...]
        a_ref[...], b_ref[...] = _lru_gate(xc, pre[:, 0:BR], pre[:, BR:2 * BR], lam_ref[...])

    big = SDS((s, BR), F32)
    return pl.pallas_call(
        body, name="lru_gates_fwd", out_shape=(big, big), grid=(s // tb,),
        in_specs=[_rows(tb, BR, CB_CX), _prev8(tb, BR, CB_CX), _const((8, BR)), _const((1, BR)),
                  _const((BR, 2 * BR)), _const((1, 2 * BR)), _const((1, BR))],
        out_specs=(_rows(tb, BR), _rows(tb, BR)), compiler_params=_params(1),
    )(proj, proj, conv_w, conv_b, w_cat, b_cat, lam)


def _lru_gates_bwd(proj, lmb, h, conv_w, conv_b, w_cat, b_cat, lam, tb):
    s = proj.shape[0]

    def body(cx, cxp, l_ref, h_ref, hp_ref, w_ref, cb_ref, wc_ref, bc_ref, lam_ref,
             dxc_ref, dpre_ref, xc_ref, dbc_ref, dlam_ref):
        _init_acc(dbc_ref, dlam_ref)
        has_prev = (pl.program_id(0) > 0).astype(F32)
        xc = _conv_taps(cx[...], cxp[...] * has_prev, w_ref, 4) + cb_ref[...]
        xcb = xc.astype(MXU_DTYPE)
        pre = jnp.dot(xcb, wc_ref[...], preferred_element_type=F32) + bc_ref[...]
        _, vjp = jax.vjp(_lru_gate, xc, pre[:, 0:BR], pre[:, BR:2 * BR], lam_ref[...])
        lm = l_ref[...]
        dxc, dpr, dpi, dlam = vjp((lm * _shift_down(h_ref[...], hp_ref[...] * has_prev, 1), lm))
        dpre = jnp.concatenate([dpr, dpi], axis=1)
        dpreb = dpre.astype(MXU_DTYPE)
        dxc_ref[...] = dxc + lax.dot_general(dpreb, wc_ref[...], (((1,), (1,)), ((), ())),
                                             preferred_element_type=F32)
        dpre_ref[...] = dpreb
        xc_ref[...] = xcb
        dbc_ref[...] += _colsum(dpre)
        dlam_ref[...] += dlam

    return pl.pallas_call(
        body, name="lru_gates_bwd",
        out_shape=(SDS((s, BR), F32), SDS((s, 2 * BR), MXU_DTYPE), SDS((s, BR), MXU_DTYPE),
                   SDS((1, 2 * BR), F32), SDS((1, BR), F32)),
        grid=(s // tb,),
        in_specs=[_rows(tb, BR, CB_CX), _prev8(tb, BR, CB_CX), _rows(tb, BR), _rows(tb, BR), _prev8(tb, BR),
                  _const((8, BR)), _const((1, BR)), _const((BR, 2 * BR)), _const((1, 2 * BR)), _const((1, BR))],
        out_specs=(_rows(tb, BR), _rows(tb, 2 * BR), _rows(tb, BR), _const((1, 2 * BR)), _const((1, BR))),
        compiler_params=_params(1))(proj, proj, lmb, h, h, conv_w, conv_b, w_cat, b_cat, lam)


def _conv_c_bwd(dxc, proj, conv_w, tb):
    s = proj.shape[0]

    def body(g, gn, cx, cxp, w_ref, dcx_ref, dw_ref, db_ref):
        _init_acc(dw_ref, db_ref)
        i = pl.program_id(0)
        has_prev = (i > 0).astype(F32)
        has_next = (i < pl.num_programs(0) - 1).astype(F32)
        gt = g[...]
        dcx_ref[...] = _conv_taps_t(gt, gn[...] * has_next, w_ref, 4).astype(MXU_DTYPE)
        _conv_wgrad(dw_ref, gt, cx[...], cxp[...] * has_prev, 4)
        db_ref[...] += _colsum(gt)

    return pl.pallas_call(
        body, name="conv_c_bwd", out_shape=(SDS((s, BR), MXU_DTYPE), SDS((8, BR), F32), SDS((1, BR), F32)),
        grid=(s // tb,),
        in_specs=[_rows(tb, BR), _next8(tb, BR, s), _rows(tb, BR, CB_CX), _prev8(tb, BR, CB_CX), _const((8, BR))],
        out_specs=(_rows(tb, BR), _const((8, BR)), _const((1, BR))), compiler_params=_params(1),
    )(dxc, dxc, proj, proj, conv_w)


def _s5_disc(lam_re, lam_im, log_dt):
    dt = jnp.exp(log_dt)
    mag = jnp.exp(lam_re * dt)
    ab_re = mag * jnp.cos(lam_im * dt)
    ab_im = mag * jnp.sin(lam_im * dt)
    den = lam_re * lam_re + lam_im * lam_im
    f_re = ((ab_re - 1.0) * lam_re + ab_im * lam_im) / den
    f_im = (ab_im * lam_re - (ab_re - 1.0) * lam_im) / den
    return ab_re, ab_im, f_re, f_im


def _s5_bbar(f_re, f_im, b_re, b_im):
    return f_re * b_re - f_im * b_im, f_re * b_im + f_im * b_re


def _s5_disc_fwd(lam_re, lam_im, log_dt):
    def body(lr, li, ld, o0, o1, o2, o3):
        o0[...], o1[...], o2[...], o3[...] = _s5_disc(lr[...], li[...], ld[...])
    return pl.pallas_call(body, name="s5_disc_fwd", out_shape=(SDS(lam_re.shape, F32),) * 4)(lam_re, lam_im, log_dt)


def _s5_disc_bwd(lam_re, lam_im, log_dt, cts):
    def body(lr, li, ld, c0, c1, c2, c3, o0, o1, o2):
        _, vjp = jax.vjp(_s5_disc, lr[...], li[...], ld[...])
        o0[...], o1[...], o2[...] = vjp((c0[...], c1[...], c2[...], c3[...]))
    return pl.pallas_call(body, name="s5_disc_bwd", out_shape=(SDS(lam_re.shape, F32), SDS(lam_re.shape, F32),
                                                                SDS(log_dt.shape, F32)))(lam_re, lam_im, log_dt, *cts)


def _s5_bbar_fwd(f_re, f_im, b_re, b_im):
    def body(fr, fi, br, bi, o0, o1):
        o0[...], o1[...] = _s5_bbar(fr[...], fi[...], br[...], bi[...])
    return pl.pallas_call(body, name="s5_bbar_fwd", out_shape=(SDS(b_re.shape, F32),) * 2)(f_re, f_im, b_re, b_im)


def _s5_bbar_bwd(f_re, f_im, b_re, b_im, d_re, d_im):
    def body(fr, fi, br, bi, dr, di, o0, o1, o2, o3):
        _, vjp = jax.vjp(_s5_bbar, fr[...], fi[...], br[...], bi[...])
        o0[...], o1[...], o2[...], o3[...] = vjp((dr[...], di[...]))
    col, mat = SDS(f_re.shape, F32), SDS(b_re.shape, F32)
    return pl.pallas_call(body, name="s5_bbar_bwd", out_shape=(col, col, mat, mat))(f_re, f_im, b_re, b_im, d_re, d_im)


def _s5_tail_bwd(dycat, ylin, proj, d_skip, w_glu, b_glu, tb):
    s = proj.shape[0]

    def body(dy, yl, u, dg, dk, w_ref, b_ref, dyl_ref, dus_ref, ddg_ref, g_ref, dt_ref, ddk_ref, dbg_ref):
        _init_acc(ddk_ref, dbg_ref)
        g, gelu_vjp = jax.vjp(jax.nn.gelu, yl[...] + dk[...] * u[...])
        gb = g.astype(MXU_DTYPE)
        sg = jax.nn.sigmoid(jnp.dot(gb, w_ref[...], preferred_element_type=F32) + b_ref[...])
        dz = dy[...] * _silu(dg[...])
        ddg_ref[...] = (dy[...] * g * sg * _dsilu(dg[...])).astype(MXU_DTYPE)
        dt = dz * g * sg * (1.0 - sg)
        dtb = dt.astype(MXU_DTYPE)
        dgel = dz * sg + lax.dot_general(dtb, w_ref[...], (((1,), (1,)), ((), ())), preferred_element_type=F32)
        dyv, = gelu_vjp(dgel)
        dyl_ref[...] = dyv
        dus_ref[...] = dyv * dk[...]
        g_ref[...] = gb
        dt_ref[...] = dtb
        ddk_ref[...] += _colsum(dyv * u[...])
        dbg_ref[...] += _colsum(dt)

    big, half, vec = SDS((s, BR), F32), SDS((s, BR), MXU_DTYPE), SDS((1, BR), F32)
    return pl.pallas_call(
        body, name="s5_tail_bwd", out_shape=(big, big, half, half, half, vec, vec), grid=(s // tb,),
        in_specs=[_rows(tb, BR, 3), _rows(tb, BR), _rows(tb, BR, CB_DU), _rows(tb, BR, CB_DG), _const((1, BR)),
                  _const((BR, BR)), _const((1, BR))],
        out_specs=(_rows(tb, BR),) * 5 + (_const((1, BR)), _const((1, BR))), compiler_params=_params(1),
    )(dycat, ylin, proj, proj, d_skip, w_glu, b_glu)


def _assemble_dproj(da, dqkv, dbg, dcx, dcg, du, dus, ddg, tb):
    s = da.shape[0]

    def body(da_ref, dq_ref, dk_ref, dv_ref, dbg_ref, dcx_ref, dcg_ref, du_ref, dus_ref, ddg_ref, o_ref):
        o_ref[:, 0:4 * BR] = da_ref[...]
        for j, part in enumerate((dq_ref, dk_ref, dv_ref)):
            o_ref[:, (4 + j) * BR:(5 + j) * BR] = part[...].astype(MXU_DTYPE)
        o_ref[:, 7 * BR:8 * BR] = dbg_ref[...].astype(MXU_DTYPE)
        o_ref[:, 8 * BR:9 * BR] = dcx_ref[...].astype(MXU_DTYPE)
        o_ref[:, 9 * BR:10 * BR] = dcg_ref[...].astype(MXU_DTYPE)
        o_ref[:, 10 * BR:11 * BR] = (du_ref[...] + dus_ref[...]).astype(MXU_DTYPE)
        o_ref[:, 11 * BR:12 * BR] = ddg_ref[...].astype(MXU_DTYPE)

    return pl.pallas_call(
        body, name="assemble_dproj", out_shape=SDS((s, N_IN), MXU_DTYPE), grid=(s // tb,),
        in_specs=[_rows(tb, 4 * BR)] + [_rows(tb, BR)] * 9, out_specs=_rows(tb, N_IN),
        compiler_params=_params(1))(da, *dqkv, dbg, dcx, dcg, du, dus, ddg)


def _sum_leading(xs, tr, name):
    n, _, c = xs[0].shape
    nl = len(xs)
    tr = min([tr] + [x.shape[1] for x in xs])
    assert all(x.shape[1] % tr == 0 for x in xs), (name, tr)
    nrs = [x.shape[1] // tr for x in xs]
    starts = [sum(nrs[:l]) for l in range(nl)]

    def body(*refs):
        i = pl.program_id(0)
        for l in range(nl):
            @pl.when((i >= starts[l]) & (i < starts[l] + nrs[l]))
            def _():
                acc = refs[l * n][...].astype(F32)
                for ref in refs[l * n + 1:(l + 1) * n]:
                    acc = acc + ref[...].astype(F32)
                refs[nl * n][...] = acc

    specs = [pl.BlockSpec((None, tr, c), functools.partial(
        lambda i, k, l: (k, jnp.clip(i - starts[l], 0, nrs[l] - 1), 0), k=k, l=l)) for l in range(nl) for k in range(n)]
    return pl.pallas_call(body, name=name, out_shape=SDS((sum(nrs) * tr, c), F32), grid=(sum(nrs),), in_specs=specs,
                          out_specs=pl.BlockSpec((tr, c), lambda i: (i, 0)),
                          compiler_params=_params(1))(*[x for x in xs for _ in range(n)])


def _adamw(w, g_parts, m, v, tr, name):
    r, c = w.shape
    tr = min(tr, r)
    n = len(g_parts)
    assert r % tr == 0, (name, r, tr)

    def body(*refs):
        w_ref, m_ref, v_ref = refs[0], refs[1 + n], refs[2 + n]
        g_ref, d_ref, nm_ref, nv_ref = refs[3 + n:]
        g = refs[1][...]
        for ref in refs[2:1 + n]:
            g = g + ref[...]
        mm = ADAM_B1 * m_ref[...] + (1.0 - ADAM_B1) * g
        vv = ADAM_B2 * v_ref[...] + (1.0 - ADAM_B2) * jnp.square(g)
        m_hat = mm / (1.0 - ADAM_B1 ** ADAM_STEP)
        v_hat = vv / (1.0 - ADAM_B2 ** ADAM_STEP)
        g_ref[...] = g
        d_ref[...] = -ADAM_LR * (m_hat / (jnp.sqrt(v_hat) + ADAM_EPS) + ADAM_WD * w_ref[...])
        nm_ref[...] = mm
        nv_ref[...] = vv

    spec = pl.BlockSpec((tr, c), lambda i: (i, 0))
    return _call(body, name=name, out_shape=(SDS((r, c), F32),) * 4, grid=(r // tr,), in_specs=[spec] * (3 + n),
                 out_specs=(spec,) * 4, scratch_shapes=[], args=(w, *g_parts, m, v))


class _AllGather8:
    def __init__(self, block):
        self.m_per = block.shape[0]
        self.arrays, self.n_in, self.n_out = [block], 1, 1
        self.out_shapes = (SDS((N_DEV * self.m_per, block.shape[1]), block.dtype),)
        self.scratch = [pltpu.SemaphoreType.DMA((7,)), pltpu.SemaphoreType.DMA((7,)), pltpu.SemaphoreType.DMA]

    def _copies(self, ins, outs, sems):
        (x_ref,), (out_ref,), (send_sems, recv_sems, local_sem) = ins, outs, sems
        x, y, c = lax.axis_index("x"), lax.axis_index("y"), lax.axis_index("c")
        me, sibling = (x, y, c), (x, y, 1 - c)
        chips = [(1 - x, y), (x, 1 - y), (1 - x, 1 - y)]

        def rows(px, py, pc):
            return out_ref.at[pl.ds((4 * px + 2 * py + pc) * self.m_per, self.m_per), :]

        def copy(k, blk, to, src=None):
            return pltpu.make_async_remote_copy(
                src_ref=rows(*blk) if src is None else src, dst_ref=rows(*blk), send_sem=send_sems.at[k],
                recv_sem=recv_sems.at[k], device_id=to, device_id_type=MESH)

        mine = pltpu.make_async_copy(x_ref, rows(*me), local_sem)
        first = [copy(0, me, sibling, src=x_ref)]
        first += [copy(1 + j, me, (*chip, c), src=x_ref) for j, chip in enumerate(chips)]
        passed = [copy(4 + j, (*chip, c), sibling) for j, chip in enumerate(chips)]
        arrivals = [copy(1 + j, (*chip, c), me) for j, chip in enumerate(chips)]
        from_sibling = [copy(0, sibling, me)] + [copy(4 + j, (*chip, 1 - c), me) for j, chip in enumerate(chips)]
        return mine, first, passed, arrivals, from_sibling

    def start(self, ins, outs, sems):
        mine, first, _, _, _ = self._copies(ins, outs, sems)
        mine.start()
        for cp in first:
            cp.start()

    def wait(self, ins, outs, sems):
        mine, first, passed, arrivals, from_sibling = self._copies(ins, outs, sems)
        for arrived, onward in zip(arrivals, passed):
            arrived.wait_recv()
            onward.start()
        for cp in from_sibling:
            cp.wait_recv()
        for cp in first + passed:
            cp.wait_send()
        mine.wait()


def _allgather8(block, name):
    ex = _AllGather8(block)

    def body(x_ref, out_ref, *sems):
        ex.start((x_ref,), (out_ref,), sems)
        ex.wait((x_ref,), (out_ref,), sems)

    return pl.pallas_call(
        body, name=name, out_shape=ex.out_shapes[0], in_specs=[pl.BlockSpec(memory_space=pltpu.VMEM)],
        out_specs=pl.BlockSpec(memory_space=pltpu.VMEM), scratch_shapes=ex.scratch, compiler_params=_params())(block)


class _Exchange:
    def __init__(self, items, out_shapes):
        self.items, self.out_shapes = list(items), tuple(out_shapes)
        self.arrays = [it[0] for it in self.items]
        n = len(self.items)
        self.n_in, self.n_out = n, len(self.out_shapes)
        self.scratch = [pltpu.SemaphoreType.DMA((n * N_CHIPS,)), pltpu.SemaphoreType.DMA((n * N_CHIPS,)),
                        pltpu.SemaphoreType.DMA((n,))]

    def _copies(self, ins, outs, sems, m):
        send_sems, recv_sems, local_sems = sems
        c = lax.axis_index("c")
        others = [j for j in range(N_CHIPS) if j != m]

        def remote(a, src, dst, to, from_):
            return pltpu.make_async_remote_copy(
                src_ref=src, dst_ref=dst, send_sem=send_sems.at[a * N_CHIPS + to],
                recv_sem=recv_sems.at[a * N_CHIPS + from_], device_id=(to // 2, to % 2, c), device_id_type=MESH)

        local, sends, recvs = [], [], []
        for a, (_, oi, src_of, dst_of) in enumerate(self.items):
            local.append(pltpu.make_async_copy(src_of(ins[a], m), dst_of(outs[oi], m), local_sems.at[a]))
            for j in others:
                sends.append(remote(a, src_of(ins[a], j), dst_of(outs[oi], m), j, m))
                recvs.append(remote(a, src_of(ins[a], m), dst_of(outs[oi], j), j, j))
        return local, sends, recvs

    def _on_my_chip(self, fn):
        chip = 2 * lax.axis_index("x") + lax.axis_index("y")
        for m in range(N_CHIPS):
            pl.when(chip == m)(functools.partial(fn, m))

    def start(self, ins, outs, sems):
        def go(m):
            local, sends, _ = self._copies(ins, outs, sems, m)
            for cp in local + sends:
                cp.start()
        self._on_my_chip(go)

    def wait(self, ins, outs, sems):
        def go(m):
            local, sends, recvs = self._copies(ins, outs, sems, m)
            for cp in recvs:
                cp.wait_recv()
            for cp in sends:
                cp.wait_send()
            for cp in local:
                cp.wait()
        self._on_my_chip(go)


def _half_rows(ref, cc):
    h = ref.shape[-2] // 2
    return ref.at[(slice(None),) * (len(ref.shape) - 2) + (pl.ds(cc * h, h), slice(None))]


class _Gather:
    def __init__(self, items, out_shapes):
        self.items, self.out_shapes = list(items), tuple(out_shapes)
        self.arrays = [it[0] for it in self.items]
        n = len(self.items)
        self.n_in, self.n_out = n, len(self.out_shapes)
        self.scratch = [pltpu.SemaphoreType.DMA((n * N_CHIPS,)) for _ in range(4)] + [pltpu.SemaphoreType.DMA((n,))]

    def _copies(self, ins, outs, sems, m, cc):
        ici_send, ici_recv, d2d_send, d2d_recv, local_sems = sems
        others = [j for j in range(N_CHIPS) if j != m]
        local, sends, arrivals, passed_on, from_sibling = [], [], [], [], []
        for a, (_, oi, src_of, dst_of) in enumerate(self.items):
            src, out = src_of(ins[a]), outs[oi]
            local.append(pltpu.make_async_copy(src, dst_of(out, m), local_sems.at[a]))
            for j in others:
                k = a * N_CHIPS + j
                mine_there = _half_rows(dst_of(out, m), cc)
                theirs_here = _half_rows(dst_of(out, j), cc)
                sends.append(pltpu.make_async_remote_copy(
                    src_ref=_half_rows(src, cc), dst_ref=mine_there, send_sem=ici_send.at[k],
                    recv_sem=ici_recv.at[a * N_CHIPS + m], device_id=(j // 2, j % 2, cc), device_id_type=MESH))
                arrivals.append(pltpu.make_async_remote_copy(
                    src_ref=_half_rows(src, cc), dst_ref=theirs_here, send_sem=ici_send.at[k], recv_sem=ici_recv.at[k],
                    device_id=(j // 2, j % 2, cc), device_id_type=MESH))
                passed_on.append(pltpu.make_async_remote_copy(
                    src_ref=theirs_here, dst_ref=theirs_here, send_sem=d2d_send.at[k], recv_sem=d2d_recv.at[k],
                    device_id=(m // 2, m % 2, 1 - cc), device_id_type=MESH))
                other_half = _half_rows(dst_of(out, j), 1 - cc)
                from_sibling.append(pltpu.make_async_remote_copy(
                    src_ref=other_half, dst_ref=other_half, send_sem=d2d_send.at[k], recv_sem=d2d_recv.at[k],
                    device_id=(m // 2, m % 2, 1 - cc), device_id_type=MESH))
        return local, sends, arrivals, passed_on, from_sibling

    def _on_my_core(self, fn):
        chip = 2 * lax.axis_index("x") + lax.axis_index("y")
        c = lax.axis_index("c")
        for m in range(N_CHIPS):
            for cc in range(2):
                pl.when((chip == m) & (c == cc))(functools.partial(fn, m, cc))

    def start(self, ins, outs, sems):
        def go(m, cc):
            local, sends, _, _, _ = self._copies(ins, outs, sems, m, cc)
            for cp in local + sends:
                cp.start()
        self._on_my_core(go)

    def wait(self, ins, outs, sems):
        def go(m, cc):
            local, sends, arrivals, passed_on, from_sibling = self._copies(ins, outs, sems, m, cc)
            for arrived, onward in zip(arrivals, passed_on):
                arrived.wait_recv()
                onward.start()
            for cp in from_sibling:
                cp.wait_recv()
            for cp in sends + passed_on:
                cp.wait_send()
            for cp in local:
                cp.wait()
        self._on_my_core(go)


def _run_exchange(ex, name):
    def body(*refs):
        ins, outs, sems = refs[:ex.n_in], refs[ex.n_in:ex.n_in + ex.n_out], refs[ex.n_in + ex.n_out:]
        ex.start(ins, outs, sems)
        ex.wait(ins, outs, sems)

    return pl.pallas_call(
        body, name=name, out_shape=ex.out_shapes, in_specs=[ANY] * ex.n_in, out_specs=(ANY,) * ex.n_out,
        scratch_shapes=ex.scratch, compiler_params=_params())(*ex.arrays)


def _sequencer_exchange(ex, name, collective_id):
    in_refs = [jax.new_ref(a, memory_space=pltpu.MemorySpace.HBM) for a in ex.arrays]
    out_refs = [jax.empty_ref(s, memory_space=pltpu.MemorySpace.HBM) for s in ex.out_shapes]

    @pl.kernel(mesh=plsc.ScalarSubcoreMesh(axis_name="sequencer", num_cores=1), name=name,
               scratch_types=tuple(ex.scratch), compiler_params=pltpu.CompilerParams(collective_id=collective_id))
    def launch(*sems):
        x, y, c = lax.axis_index("x"), lax.axis_index("y"), lax.axis_index("c")
        barrier = pltpu.get_barrier_semaphore()
        peers = [(x, y, 1 - c), (1 - x, y, c), (x, 1 - y, c), (1 - x, 1 - y, c)]
        for peer in peers:
            pl.semaphore_signal(barrier, inc=1, device_id=peer, device_id_type=MESH)
        pl.semaphore_wait(barrier, len(peers))
        ex.start(in_refs, out_refs, sems)
        ex.wait(in_refs, out_refs, sems)

    launch()
    return [r[...] for r in out_refs]


def _sibling_swap(arrays, name, also):
    n = len(arrays)

    def body(*refs):
        ins, refs = refs[:n], refs[n:]
        x_ins, refs = refs[:also.n_in], refs[also.n_in:]
        outs, refs = refs[:n], refs[n:]
        x_outs, refs = refs[:also.n_out], refs[also.n_out:]
        send_sems, recv_sems, x_sems = refs[0], refs[1], refs[2:]
        peer = (lax.axis_index("x"), lax.axis_index("y"), 1 - lax.axis_index("c"))
        cps = [pltpu.make_async_remote_copy(src_ref=ins[a], dst_ref=outs[a], send_sem=send_sems.at[a],
                                            recv_sem=recv_sems.at[a], device_id=peer, device_id_type=MESH)
               for a in range(n)]
        also.start(x_ins, x_outs, x_sems)
        for cp in cps:
            cp.start()
        also.wait(x_ins, x_outs, x_sems)
        for cp in cps:
            cp.wait()

    return pl.pallas_call(
        body, name=name, out_shape=tuple(SDS(a.shape, a.dtype) for a in arrays) + also.out_shapes,
        in_specs=[ANY] * (n + also.n_in), out_specs=(ANY,) * (n + also.n_out),
        scratch_shapes=[pltpu.SemaphoreType.DMA((n,)), pltpu.SemaphoreType.DMA((n,))] + also.scratch,
        compiler_params=_params())(*arrays, *also.arrays)


def _block_diag(w):
    h, n, m = w.shape
    eye = jnp.eye(h, dtype=w.dtype)
    return (w[:, :, None, :] * eye[:, None, :, None]).reshape(h * n, h * m)


def _diag_blocks(d, h, col0=0, ncols=None, stacked=1):
    ncols = d.shape[1] - col0 if ncols is None else ncols
    n, m = d.shape[0] // (h * stacked), ncols // h
    lanes = 128
    assert m <= lanes and lanes % m == 0 and col0 % lanes == 0

    def body(d_ref, o_ref):
        for gi in range(h * stacked):
            c = col0 + (gi % h) * m
            chunk = d_ref[gi * n:(gi + 1) * n, c // lanes * lanes:c // lanes * lanes + lanes]
            o_ref[gi * n:(gi + 1) * n, :] = chunk[:, c % lanes:c % lanes + m]

    out = pl.pallas_call(body, name="diag_blocks", out_shape=SDS((stacked * h * n, m), d.dtype),
                         compiler_params=_params())(d)
    return out.reshape(stacked * h, n, m)


S5_CHUNKS = 4
S5_PER = S5_GROUPS // S5_CHUNKS
CH_W = S5_PER * S5_CH
ST_W = S5_PER * S5_STATE


def _bd_stack(mats):
    _, _, n, m = mats.shape
    eye = jnp.eye(S5_PER, dtype=mats.dtype)
    t = mats.reshape(2, S5_CHUNKS, S5_PER, n, m)
    bd = t[:, :, :, :, None, :] * eye[None, None, :, None, :, None]
    return bd.reshape(2 * S5_CHUNKS, S5_PER * n, S5_PER * m).astype(MXU_DTYPE)


def _chunks_chunked(src_ref, buf):
    pt = src_ref.shape[0]
    out = []
    for q in range(S5_CHUNKS):
        buf[q] = src_ref[:, q * CH_W:(q + 1) * CH_W]
        out.append(_load_chunked(buf.at[q], 0, pt).astype(MXU_DTYPE))
    return out


def _expand_into(dst_ref, chunks, w_ref):
    for b in range(2 * S5_CHUNKS):
        dst_ref[:, b * ST_W:(b + 1) * ST_W] = jnp.dot(chunks[b % S5_CHUNKS], w_ref[b], preferred_element_type=F32)


def _reduce_from(src_ref, w_ref, buf, dst_ref):
    pt = src_ref.shape[0]
    for q in range(S5_CHUNKS):
        y = jnp.dot(src_ref[:, q * ST_W:(q + 1) * ST_W].astype(MXU_DTYPE), w_ref[q], preferred_element_type=F32)
        p = S5_CHUNKS + q
        y = y + jnp.dot(src_ref[:, p * ST_W:(p + 1) * ST_W].astype(MXU_DTYPE), w_ref[p], preferred_element_type=F32)
        _store_natural(buf.at[q], 0, pt, y)
        dst_ref[:, q * CH_W:(q + 1) * CH_W] = buf[q]


def _s5_fwd(proj, w_bu, w_cx, a_row, d_skip, w_glu, b_glu):
    s = proj.shape[0]
    pt = _scan_tile(s)
    ch2 = 2 * S5_N

    def body(u_ref, dg_ref, wb_ref, wc_ref, a_ref, dk_ref, wg_ref, bg_ref, x_ref, y_ref, o_ref, carry, pw, buf):
        _expand_into(x_ref, _chunks_chunked(u_ref, buf), wb_ref)
        _scan_tile_in_place(a_ref, x_ref, carry, pw, reverse=False)
        _reduce_from(x_ref, wc_ref, buf, y_ref)
        g = jax.nn.gelu(y_ref[...] + dk_ref[...] * u_ref[...])
        t = jnp.dot(g.astype(MXU_DTYPE), wg_ref[...], preferred_element_type=F32) + bg_ref[...]
        o_ref[...] = (g * jax.nn.sigmoid(t) * _silu(dg_ref[...])).astype(MXU_DTYPE)

    return pl.pallas_call(
        body, name="s5_fwd", out_shape=(SDS((s, ch2), F32), SDS((s, BR), F32), SDS((s, BR), MXU_DTYPE)),
        grid=(s // pt,),
        in_specs=[_rows(pt, BR, CB_DU), _rows(pt, BR, CB_DG), _const(w_bu.shape), _const(w_cx.shape),
                  _const((1, ch2)), _const((1, BR)), _const((BR, BR)), _const((1, BR))],
        out_specs=(_rows(pt, ch2), _rows(pt, BR), _rows(pt, BR)),
        scratch_shapes=[pltpu.VMEM((1, ch2), F32), pltpu.VMEM((pt // 8, ch2), F32),
                        pltpu.VMEM((S5_CHUNKS, pt, CH_W), F32)],
        compiler_params=_params(1))(proj, proj, w_bu, w_cx, a_row, d_skip, w_glu, b_glu)


def _s5_core_bwd(dyl, proj, x, w_dx, w_du, a_row):
    s = proj.shape[0]
    pt = _scan_tile(s)
    nt = s // pt
    ch2 = 2 * S5_N
    ch = S5_N

    def body(dy_ref, u_ref, x_ref, xp_ref, wx_ref, wu_ref, a_ref, du_ref, da_ref, dwb_ref, dwc_ref,
             l_ref, carry, pw, buf, buf2):
        i = pl.program_id(0)
        _init_acc(da_ref, dwb_ref, dwc_ref)
        dy_c = _chunks_chunked(dy_ref, buf)
        u_c = _chunks_chunked(u_ref, buf2)
        _expand_into(l_ref, dy_c, wx_ref)
        _scan_tile_in_place(a_ref, l_ref, carry, pw, reverse=True)
        has_prev = (i < nt - 1).astype(F32)
        row = lax.broadcasted_iota(jnp.int32, (8, ch2), 0)
        first = jnp.where(row == 0, pltpu.roll(xp_ref[...], 1, 0) * has_prev, pltpu.roll(x_ref[pt - 8:pt, :], 1, 0))
        xprev = jnp.concatenate([first, x_ref[0:pt - 8, :]], axis=0)
        lr, li, xr, xi = l_ref[:, 0:ch], l_ref[:, ch:ch2], xprev[:, 0:ch], xprev[:, ch:ch2]
        da_ref[:, 0:ch] += _colsum(lr * xr + li * xi)
        da_ref[:, ch:ch2] += _colsum(li * xr - lr * xi)
        _reduce_from(l_ref, wu_ref, buf, du_ref)
        tn = (((0,), (0,)), ((), ()))
        for b in range(2 * S5_CHUNKS):
            cols, rows = slice(b * ST_W, (b + 1) * ST_W), slice(b * CH_W, (b + 1) * CH_W)
            dwb_ref[rows, :] += lax.dot_general(u_c[b % S5_CHUNKS], l_ref[:, cols].astype(MXU_DTYPE), tn,
                                                preferred_element_type=F32)
            dwc_ref[rows, :] += lax.dot_general(dy_c[b % S5_CHUNKS], x_ref[:, cols].astype(MXU_DTYPE), tn,
                                                preferred_element_type=F32)

    rev = lambda w, cb=0: pl.BlockSpec((pt, w), lambda i: (nt - 1 - i, cb))
    halo = pl.BlockSpec((8, ch2), lambda i: (jnp.maximum((nt - 1 - i) * (pt // 8) - 1, 0), 0))
    wshape = SDS((2 * S5_CHUNKS * CH_W, ST_W), F32)
    return pl.pallas_call(
        body, name="s5_core_bwd", out_shape=(SDS((s, BR), F32), SDS((1, ch2), F32), wshape, wshape), grid=(nt,),
        in_specs=[rev(BR, 0), rev(BR, CB_DU), rev(ch2), halo, _const(w_dx.shape), _const(w_du.shape),
                  _const((1, ch2))],
        out_specs=(rev(BR), _const((1, ch2)), _const(wshape.shape), _const(wshape.shape)),
        scratch_shapes=[pltpu.VMEM((pt, ch2), F32), pltpu.VMEM((1, ch2), F32), pltpu.VMEM((pt // 8, ch2), F32),
                        pltpu.VMEM((S5_CHUNKS, pt, CH_W), F32), pltpu.VMEM((S5_CHUNKS, pt, CH_W), F32)],
        compiler_params=_params(1))(dyl, proj, x, x, w_dx, w_du, a_row)


def _tiles(s):
    return dict(tb=min(512, s), tln=min(256, s))


def _layer_weights(p, l):
    pad8 = lambda w: jnp.pad(w, ((0, 8 - w.shape[0]), (0, 0)))
    return dict(
        conv_a=pad8(p["conv_a"][l]), conv_c=pad8(p["conv_c"][l]), conv_c_b=p["conv_c_b"][l][None],
        w_cat=jnp.concatenate([_block_diag(p["lru_wa"][l]), _block_diag(p["lru_wx"][l])], axis=1).astype(MXU_DTYPE),
        b_cat=jnp.concatenate([p["lru_ba"][l], p["lru_bx"][l]])[None], lam=p["lru_lambda"][l][None],
        lam_re=p["s5_lam_re"][l], lam_im=p["s5_lam_im"][l], log_dt=p["s5_log_dt"][l][:, None],
        b_re=p["s5_b_re"][l].reshape(S5_N, S5_CH), b_im=p["s5_b_im"][l].reshape(S5_N, S5_CH),
        c_re=p["s5_c_re"][l], c_im=p["s5_c_im"][l], d_skip=p["s5_d"][l][None], b_glu=p["s5_b_glu"][l][None],
        ln_g=p["ln_g"][l][None], ln_b=p["ln_b"][l][None])


def _s5_matrices(lw):
    ab_re, ab_im, f_re, f_im = _s5_disc_fwd(lw["lam_re"], lw["lam_im"], lw["log_dt"])
    f_re, f_im = f_re.reshape(S5_N, 1), f_im.reshape(S5_N, 1)
    bb_re, bb_im = _s5_bbar_fwd(f_re, f_im, lw["b_re"], lw["b_im"])
    bb = jnp.stack([bb_re, bb_im]).reshape(2, S5_GROUPS, S5_STATE, S5_CH)
    cc = jnp.stack([lw["c_re"], -lw["c_im"]])
    a_row = jnp.concatenate([ab_re.reshape(1, S5_N), ab_im.reshape(1, S5_N)], axis=1)
    return dict(f_re=f_re, f_im=f_im, a_row=a_row, w_bu=_bd_stack(jnp.swapaxes(bb, 2, 3)), w_du=_bd_stack(bb),
                w_cx=_bd_stack(jnp.swapaxes(cc, 2, 3)), w_dx=_bd_stack(cc))


def _mm_hooked(hook, *args, **kw):
    if hook is None:
        return _mm(*args, **kw)
    out = _mm(*args, carry=hook[0], **kw)
    hook[1](out[1:])
    return out[0]


def _layer_fwd(x, h, ada, w_in, get_rest, lw, s5m, bias_tabs, hooks=None, target=None, next_ada=None):
    s = x.shape[0]
    t = _tiles(s)
    tb = t["tb"]
    shift, scale, gate = ada
    hooks = hooks or {}
    if h is None:
        h = _modulate(x, scale, shift, tb)
    proj = _mm_hooked(hooks.get("in_proj"), h, w_in, name="in_proj", tm=1024, tn=1536, tk=D_MODEL)
    w_out, w_glu = get_rest()
    y_a = _branch_a_fwd(proj, lw["conv_a"], tb)
    os_, lses = [], []
    for g, (_, dil) in enumerate(DILATIONS):
        o, lse = _attn_fwd(proj, bias_tabs[g], dil)
        os_.append(o)
        lses.append(lse)
    y_b = _attn_combine(os_, lses, proj, tb)
    lru_a, lru_b = _lru_gates_fwd(proj, lw["conv_c"], lw["conv_c_b"], lw["w_cat"], lw["b_cat"], lw["lam"], tb)
    lru_h, y_c = _lru_scan_fwd(lru_a, lru_b, proj, tb)
    s5_x, ylin, y_d = _s5_fwd(proj, s5m["w_bu"], s5m["w_cx"], s5m["a_row"], lw["d_skip"], w_glu, lw["b_glu"])
    ycat = jnp.concatenate([y_a, y_b, y_c, y_d], axis=1)
    saved = dict(x=x, h=h, proj=proj, os=os_, lses=lses, lru_a=lru_a, lru_h=lru_h, s5_x=s5_x, ylin=ylin, ycat=ycat)
    if target is not None:
        loss, *saved["head"] = _out_ln_loss(ycat, w_out, x, gate, lw["ln_g"], lw["ln_b"], target, t["tln"])
        return loss, None, saved
    x_next, saved["xhat"], saved["y"], saved["rstd"], h_next = _out_ln(
        ycat, w_out, x, gate, lw["ln_g"], lw["ln_b"], next_ada[1], next_ada[0], t["tln"])
    return x_next, h_next, saved


def _layer_bwd(dxn, sv, ada, w_in, w_out, w_glu, lw, s5m, bias_tabs, head_ones, hooks=None):
    proj = sv["proj"]
    s = proj.shape[0]
    t = _tiles(s)
    tb = t["tb"]
    shift, scale, gate = ada
    g = {}
    hook = lambda name: hooks[name](g) if hooks and name in hooks else None
    if "head" in sv:
        dyb, dxa, g["ln_g"], g["ln_b"], dgate = sv["head"]
        dycat = _mm(dyb, w_out, name="dycat", tb=True, tm=1024, tn=1024, tk=D_MODEL)
    else:
        dyb, dxa, g["ln_g"], g["ln_b"], dgate, dycat = _ln_bwd(dxn, sv["xhat"], sv["y"], sv["rstd"], lw["ln_g"], gate,
                                                               w_out, t["tln"])
    g["w_out"] = _mm_hooked(hook("dw_out"), sv["ycat"], dyb, name="dw_out", ta=True, out_dtype=WIRE_DTYPE,
                            tm=1024, tn=1024, tk=2048)
    da, dconv_a = _branch_a_bwd(dycat, proj, lw["conv_a"], tb)
    g["conv_a"] = dconv_a[0:3]
    pre = _attn_bwd_pre(dycat, sv["os"], sv["lses"], proj, head_ones, tb)
    dbg, dos, dms = pre[0], pre[1:4], pre[4:7]
    parts, dbias = [], []
    for gi, (_, dil) in enumerate(DILATIONS):
        hk = hook(f"attn_bwd_d{dil}")
        last = gi == len(DILATIONS) - 1
        dq, dk, dv, dbi, *got = _attn_bwd(proj, dos[gi], sv["lses"][gi], dms[gi], bias_tabs[gi], dil,
                                          carry=hk and hk[0], add=tuple(parts) if last else ())
        if hk:
            hk[1](got)
        parts.append((dq, dk, dv))
        dbias.append(dbi)
    dqkv = parts[-1]
    lmb, dcg = _lru_scan_bwd(sv["lru_a"], dycat, sv["lru_h"], proj, tb)
    dxc, dpre, xcb, dbcat, dlam = _lru_gates_bwd(proj, lmb, sv["lru_h"], lw["conv_c"], lw["conv_c_b"], lw["w_cat"],
                                                  lw["b_cat"], lw["lam"], tb)
    dwcat = _mm(xcb, dpre, name="dw_lru", ta=True, tn=1024)
    g["lru_wa"] = _diag_blocks(dwcat, LRU_HEADS, 0, BR)
    g["lru_wx"] = _diag_blocks(dwcat, LRU_HEADS, BR, BR)
    g["lru_ba"], g["lru_bx"], g["lru_lambda"] = dbcat[0, 0:BR], dbcat[0, BR:2 * BR], dlam[0]
    dcx, dconv_c, dccb = _conv_c_bwd(dxc, proj, lw["conv_c"], tb)
    g["conv_c"], g["conv_c_b"] = dconv_c[0:4], dccb[0]
    dyl, dus, ddg, gb, dtb, ddk, dbglu = _s5_tail_bwd(dycat, sv["ylin"], proj, lw["d_skip"], w_glu, lw["b_glu"], tb)
    g["s5_d"], g["s5_b_glu"] = ddk[0], dbglu[0]
    g["s5_w_glu"] = _mm(gb, dtb, name="dw_glu", ta=True, out_dtype=WIRE_DTYPE)
    du, dab, dwb8, dwc8 = _s5_core_bwd(dyl, proj, sv["s5_x"], s5m["w_dx"], s5m["w_du"], s5m["a_row"])
    per_group = lambda d8: _diag_blocks(d8, S5_PER, stacked=2 * S5_CHUNKS).reshape(2, S5_GROUPS, S5_CH, S5_STATE)
    dbb, dcc = per_group(dwb8), per_group(dwc8)
    from_bd = lambda half: jnp.swapaxes(dbb[half], 1, 2).reshape(S5_N, S5_CH)
    df_re, df_im, db_re, db_im = _s5_bbar_bwd(s5m["f_re"], s5m["f_im"], lw["b_re"], lw["b_im"],
                                              from_bd(0), from_bd(1))
    shp = (S5_GROUPS, S5_STATE)
    g["s5_lam_re"], g["s5_lam_im"], dlog_dt = _s5_disc_bwd(
        lw["lam_re"], lw["lam_im"], lw["log_dt"],
        (dab[:, 0:S5_N].reshape(shp), dab[:, S5_N:].reshape(shp), df_re.reshape(shp), df_im.reshape(shp)))
    g["s5_log_dt"] = dlog_dt[:, 0]
    g["s5_b_re"] = db_re.reshape(S5_GROUPS, S5_STATE, S5_CH)
    g["s5_b_im"] = db_im.reshape(S5_GROUPS, S5_STATE, S5_CH)
    g["s5_c_re"], g["s5_c_im"] = dcc[0], -dcc[1]
    dproj = _assemble_dproj(da, dqkv, dbg, dcx, dcg, du, dus, ddg, tb)
    g["w_in"] = _mm_hooked(hook("dw_in"), sv["h"], dproj, name="dw_in", ta=True, out_dtype=WIRE_DTYPE,
                           tm=1024, tn=1536, tk=2048)
    hk = hook("dh")
    dx, dshift, dscale, *got = _dh_mod_bwd(dproj, w_in, dxa, sv["x"], scale, carry=hk and hk[0])
    if hk:
        hk[1](got)
    g["ada"] = jnp.concatenate([dshift[0], dscale[0], dgate[0]])
    return dx, g, dbias


SMALL = ("rel_bias", "conv_a", "conv_c", "conv_c_b", "lru_wa", "lru_ba", "lru_wx", "lru_bx", "lru_lambda",
         "s5_lam_re", "s5_lam_im", "s5_log_dt", "s5_b_re", "s5_b_im", "s5_c_re", "s5_c_im", "s5_d", "s5_b_glu",
         "ln_g", "ln_b")
PER_LAYER_SMALL = SMALL[1:]


def _local_step(x, target, ada_rows, w_in, w_out, w_glu, p, comm=None):
    if comm is None:
        get_w_in = lambda l: w_in[l]
        get_rest = lambda l: (w_out[l], w_glu[l])
        fwd_hooks = bwd_hooks = lambda *_: None
    else:
        get_w_in, get_rest, fwd_hooks, bwd_hooks = comm.w_in, comm.rest, comm.fwd_hooks, comm.bwd_hooks
    s = x.shape[0]
    buckets = _bucket_maps()
    bias_tabs = _bias_tables(p["rel_bias"], buckets)
    head_ones = _block_diag(jnp.ones((ATT_HEADS, HEAD_DIM, HEAD_DIM), MXU_DTYPE))
    lws = [_layer_weights(p, l) for l in range(DEPTH)]
    s5ms = [_s5_matrices(lw) for lw in lws]
    adas = [tuple(ada_rows[l, k * D_MODEL:(k + 1) * D_MODEL][None] for k in range(3)) for l in range(DEPTH)]
    saved, h = [], None
    for l in range(DEPTH):
        last = l == DEPTH - 1
        x, h, sv = _layer_fwd(x, h, adas[l], get_w_in(l), functools.partial(get_rest, l), lws[l], s5ms[l], bias_tabs,
                              fwd_hooks(l), target if last else None, None if last else adas[l + 1])
        saved.append(sv)
    loss, dx = x, None
    grads = [None] * DEPTH
    dbias_sum = []
    for l in reversed(range(DEPTH)):
        dx, grads[l], dbias = _layer_bwd(dx, saved[l], adas[l], get_w_in(l), *get_rest(l), lws[l], s5ms[l],
                                         bias_tabs, head_ones, bwd_hooks(l, grads))
        dbias_sum.append(jnp.stack(dbias))
    drel = _rel_bias_grad(jnp.stack(dbias_sum), buckets)[:, 0:ATT_HEADS]
    small = {n: jnp.stack([grads[l][n] for l in range(DEPTH)]) for n in PER_LAYER_SMALL + ("ada",)}
    small["rel_bias"] = drel
    big = {n: [grads[l][n] for l in range(DEPTH)] for n in ("w_in", "w_out", "s5_w_glu")}
    return loss, dx, big, small


PACK_ROWS = 256


def _pack(parts):
    flat = jnp.concatenate([t.reshape(-1).astype(F32) for t in parts])
    n = flat.shape[0]
    rows = -(-n // (PACK_ROWS * 128)) * PACK_ROWS
    return jnp.pad(flat, (0, rows * 128 - n)).reshape(rows, 128)


def _unpack(packed, shapes):
    flat = packed.reshape(packed.shape[:-2] + (-1,))
    out, off = [], 0
    for shp in shapes:
        size = math.prod(shp)
        out.append(flat[..., off:off + size].reshape(flat.shape[:-1] + tuple(shp)))
        off += size
    return out


def _take_cols(t, chip, width):
    return lax.dynamic_slice_in_dim(t, chip * width, width, axis=t.ndim - 1)


class _Comm:
    IN_W, OUT_R, GLU_R = N_IN // N_CHIPS, D_MODEL // N_CHIPS, BR // N_CHIPS

    def __init__(self, w_in_b, w_out_b, w_glu_b):
        assert DEPTH == 2
        self.shards = (w_in_b, w_out_b, w_glu_b)
        in_w = self.IN_W
        self.w_in_full = {0: _sequencer_exchange(_Gather(
            [(w_in_b, 0, lambda ref: ref.at[0], lambda ref, j: ref.at[:, pl.ds(j * in_w, in_w)])],
            [SDS((D_MODEL, N_IN), WIRE_DTYPE)]), "gather_w_in0", collective_id=0)[0]}
        self.w_out_full = self.w_glu_full = None
        self.recv = {}

    def w_in(self, l):
        return self.w_in_full[l]

    def rest(self, l):
        return self.w_out_full[l], self.w_glu_full[l]

    def fwd_hooks(self, l):
        if l != 0:
            return None
        w_in_b, w_out_b, w_glu_b = self.shards
        in_w, out_r, glu_r = self.IN_W, self.OUT_R, self.GLU_R
        whole = lambda ref: ref
        items = [(w_out_b, 0, whole, lambda ref, j: ref.at[:, pl.ds(j * out_r, out_r), :]),
                 (w_glu_b, 1, whole, lambda ref, j: ref.at[:, pl.ds(j * glu_r, glu_r), :]),
                 (w_in_b, 2, lambda ref: ref.at[1], lambda ref, j: ref.at[:, pl.ds(j * in_w, in_w)])]
        shapes = [SDS((DEPTH, D_MODEL, D_MODEL), WIRE_DTYPE), SDS((DEPTH, BR, BR), WIRE_DTYPE),
                  SDS((D_MODEL, N_IN), WIRE_DTYPE)]

        def done(got):
            self.w_out_full, self.w_glu_full, self.w_in_full[1] = got

        return {"in_proj": (_Gather(items, shapes), done)}

    W_IN_ROWS = ((0, 1024), (1024, 512), (1536, 512))

    def _scatter(self, parts):
        in_w, out_r, glu_r = self.IN_W, self.OUT_R, self.GLU_R
        items, shapes, keys = [], [], []
        for oi, (name, l, arr, *rows) in enumerate(parts):
            if name == "w_in":
                r0, nr = rows[0] if rows else (0, D_MODEL)
                cut = functools.partial(lambda ref, j, r0, nr: ref.at[pl.ds(r0, nr), pl.ds(j * in_w, in_w)], r0=r0, nr=nr)
                shard = (nr, in_w)
            elif name == "w_out":
                cut, shard = (lambda ref, j: ref.at[pl.ds(j * out_r, out_r), :]), (out_r, D_MODEL)
            else:
                cut, shard = (lambda ref, j: ref.at[pl.ds(j * glu_r, glu_r), :]), (glu_r, BR)
            items.append((arr, oi, cut, lambda ref, j: ref.at[j]))
            shapes.append(SDS((N_CHIPS,) + shard, WIRE_DTYPE))
            keys.append((name, l) + ((rows[0][0],) if rows else ()))

        def done(got):
            self.recv.update(zip(keys, got))

        return _Exchange(items, shapes), done

    def received(self, name):
        return [self.recv[k] for k in sorted(k for k in self.recv if k[0] == name)]

    def bwd_hooks(self, l, grads):
        if l != 0:
            return None
        g1 = grads[1]
        w_in_part = lambda k: (lambda g: self._scatter([("w_in", 1, g1["w_in"], self.W_IN_ROWS[k])]))
        return {"dw_out": lambda g: self._scatter([("w_out", 1, g1["w_out"]), ("s5_w_glu", 1, g1["s5_w_glu"])]),
                "attn_bwd_d16": w_in_part(0), "attn_bwd_d4": w_in_part(1), "attn_bwd_d1": w_in_part(2),
                "dw_in": lambda g: self._scatter([("w_out", 0, g["w_out"]), ("s5_w_glu", 0, g["s5_w_glu"])]),
                "dh": lambda g: self._scatter([("w_in", 0, g["w_in"])])}


def kernel(x, c, rel_bias, w_ada, b_ada, w_in, conv_a, conv_c, conv_c_b, lru_wa, lru_ba, lru_wx, lru_bx, lru_lambda, s5_lam_re, s5_lam_im, s5_log_dt, s5_b_re, s5_b_im, s5_c_re, s5_c_im, s5_d, s5_w_glu, s5_b_glu, w_out, ln_g, ln_b, loss_target, m_rel_bias, m_w_ada, m_b_ada, m_w_in, m_conv_a, m_conv_c, m_conv_c_b, m_lru_wa, m_lru_ba, m_lru_wx, m_lru_bx, m_lru_lambda, m_s5_lam_re, m_s5_lam_im, m_s5_log_dt, m_s5_b_re, m_s5_b_im, m_s5_c_re, m_s5_c_im, m_s5_d, m_s5_w_glu, m_s5_b_glu, m_w_out, m_ln_g, m_ln_b, v_rel_bias, v_w_ada, v_b_ada, v_w_in, v_conv_a, v_conv_c, v_conv_c_b, v_lru_wa, v_lru_ba, v_lru_wx, v_lru_bx, v_lru_lambda, v_s5_lam_re, v_s5_lam_im, v_s5_log_dt, v_s5_b_re, v_s5_b_im, v_s5_c_re, v_s5_c_im, v_s5_d, v_s5_w_glu, v_s5_b_glu, v_w_out, v_ln_g, v_ln_b):
    args = dict(locals())
    names = ("rel_bias", "w_ada", "b_ada", "w_in", "conv_a", "conv_c", "conv_c_b", "lru_wa", "lru_ba", "lru_wx",
             "lru_bx", "lru_lambda", "s5_lam_re", "s5_lam_im", "s5_log_dt", "s5_b_re", "s5_b_im", "s5_c_re", "s5_c_im",
             "s5_d", "s5_w_glu", "s5_b_glu", "w_out", "ln_g", "ln_b")
    w = {n: args[n] for n in names}
    mom = {n: args["m_" + n] for n in names}
    var = {n: args["v_" + n] for n in names}
    chip = 2 * lax.axis_index("x") + lax.axis_index("y")
    me = 2 * chip + lax.axis_index("c")
    ada_w = 3 * D_MODEL // N_CHIPS
    conv_w = BR // N_CHIPS

    comm = _Comm(w["w_in"].astype(WIRE_DTYPE), w["w_out"].astype(WIRE_DTYPE), w["s5_w_glu"].astype(WIRE_DTYPE))

    taps = jnp.concatenate([w["conv_a"].reshape(DEPTH * 3, conv_w), w["conv_c"].reshape(DEPTH * 4, conv_w)])
    first = jnp.concatenate([c, jnp.pad(taps, ((0, 1), (0, D_MODEL - conv_w)))])
    got = _allgather8(first, "gather_c_taps").reshape(N_CHIPS, 2, 16, D_MODEL)
    c_all = got[:, :, 0].reshape(N_DEV, D_MODEL)
    taps_all = jnp.transpose(got[:, 0, 1:1 + DEPTH * 7, 0:conv_w], (1, 0, 2)).reshape(DEPTH * 7, BR)
    conv_a_f = taps_all[0:DEPTH * 3].reshape(DEPTH, 3, BR)
    conv_c_f = taps_all[DEPTH * 3:].reshape(DEPTH, 4, BR)

    cond_all = _silu_rows(c_all)
    ada_part = jnp.stack([_mm(cond_all, w["w_ada"][l], name="ada_fwd", tk=D_MODEL, tn=512,
                              bias=_take_cols(w["b_ada"][l][None], chip, ada_w)) for l in range(DEPTH)])
    ada_all = _allgather8(ada_part.reshape(DEPTH * N_DEV, ada_w), "gather_ada")
    ada_all = ada_all.reshape(N_CHIPS, 2, DEPTH, N_DEV, ada_w)[:, 0]
    ada_rows = lax.dynamic_index_in_dim(ada_all, me, axis=2, keepdims=False)
    ada_rows = jnp.transpose(ada_rows, (1, 0, 2)).reshape(DEPTH, 3 * D_MODEL)

    p = dict(w)
    p["conv_a"], p["conv_c"] = conv_a_f, conv_c_f
    loss, dx, _, small = _local_step(x[0], loss_target[0], ada_rows, None, None, None, p, comm)

    sums = [_sum_leading(comm.received(name), 256, "sum_chips") for name in ("w_in", "w_out", "s5_w_glu")]
    small_names = SMALL + ("ada",)
    small["loss"] = loss
    order = small_names + ("loss",)
    shapes = [small[n].shape for n in order]
    *others, gathered = _sibling_swap(sums, "swap_cores", _AllGather8(_pack([small[n] for n in order])))
    out = {}
    for name, mine, other in zip(("w_in", "w_out", "s5_w_glu"), sums, others):
        shp = w[name].shape
        flat = lambda t: t.reshape(-1, shp[-1])
        res = _adamw(flat(w[name]), [mine, other], flat(mom[name]), flat(var[name]), 128, "adamw_big")
        out[name] = [t.reshape(shp) for t in res]
    gathered = gathered.reshape(N_DEV, -1, 128)
    total = dict(zip(order, _unpack(_sum_leading([gathered], PACK_ROWS, "sum_devices"), shapes)))
    d_ada_all = _unpack(gathered, shapes)[order.index("ada")]
    g_small = {n: total[n] for n in SMALL}
    g_small["conv_a"] = _take_cols(total["conv_a"], chip, conv_w)
    g_small["conv_c"] = _take_cols(total["conv_c"], chip, conv_w)
    g_small["b_ada"] = total["ada"]
    g_w_ada = jnp.stack([_mm(cond_all, _take_cols(d_ada_all[:, l], chip, ada_w), name="dw_ada", ta=True, tn=ada_w)
                         for l in range(DEPTH)])
    upd_names = SMALL + ("b_ada",)
    upd_shapes = [w[n].shape for n in upd_names]
    res = _adamw(_pack([w[n] for n in upd_names]), [_pack([g_small[n] for n in upd_names])],
                 _pack([mom[n] for n in upd_names]), _pack([var[n] for n in upd_names]), PACK_ROWS, "adamw_small")
    for k, t in enumerate(res):
        for n, val in zip(upd_names, _unpack(t, upd_shapes)):
            out.setdefault(n, [None] * 4)[k] = val
    shp = w["w_ada"].shape
    flat = lambda t: t.reshape(-1, shp[-1])
    out["w_ada"] = [t.reshape(shp) for t in _adamw(flat(w["w_ada"]), [flat(g_w_ada)], flat(mom["w_ada"]),
                                                  flat(var["w_ada"]), 128, "adamw_ada")]
    return (total["loss"].reshape(()), dx[None]) + tuple(out[n][k] for k in range(4) for n in names)
```

```python
import functools
import math

import jax
import jax.numpy as jnp
from jax import lax
from jax.experimental import pallas as pl
from jax.experimental.pallas import tpu as pltpu

F32 = jnp.float32
MXU_DTYPE = jnp.bfloat16
WIRE_DTYPE = jnp.bfloat16
SDS = jax.ShapeDtypeStruct
MESH = pl.DeviceIdType.MESH
ANY = pl.BlockSpec(memory_space=pl.ANY)
VMEM_LIMIT = 48 * 1024 * 1024

D_MODEL = 2048
DEPTH = 2
BR = 512
ATT_HEADS = 8
HEAD_DIM = 64
DILATIONS = ((128, 1), (512, 4), (2048, 16))
BLK = 128
REL_BUCKETS = 32
REL_MAX_DIST = 2048
LRU_HEADS = 8
LRU_C = 8.0
S5_CH = 16
S5_GROUPS = 32
S5_STATE = 64
S5_N = S5_GROUPS * S5_STATE
N_IN = 12 * BR
ALPHA = (2 * DEPTH) ** 0.25
LN_EPS = 1e-5
NEG = -1e30
ADAM_LR, ADAM_B1, ADAM_B2, ADAM_EPS, ADAM_WD, ADAM_STEP = 0.001, 0.9, 0.999, 1e-08, 0.01, 10
CB_AB, CB_AC, CB_AX, CB_AG, CB_Q, CB_K, CB_V, CB_BG, CB_CX, CB_CG, CB_DU, CB_DG = range(12)
N_CHIPS = 4
N_DEV = 8


def _params(n_axes=0):
    kw = {"dimension_semantics": ("arbitrary",) * n_axes} if n_axes else {}
    return pltpu.CompilerParams(vmem_limit_bytes=VMEM_LIMIT, **kw)


def _rows(tb, w, cb=0):
    return pl.BlockSpec((tb, w), lambda i: (i, cb))


def _prev8(tb, w, cb=0):
    return pl.BlockSpec((8, w), lambda i: (jnp.maximum(i * (tb // 8) - 1, 0), cb))


def _next8(tb, w, n_rows, cb=0):
    return pl.BlockSpec((8, w), lambda i: (jnp.minimum((i + 1) * (tb // 8), n_rows // 8 - 1), cb))


def _const(shape):
    return pl.BlockSpec(shape, lambda *_: (0,) * len(shape))


def _silu(x):
    return x * jax.nn.sigmoid(x)


def _dsilu(x):
    s = jax.nn.sigmoid(x)
    return s * (1.0 + x * (1.0 - s))


def _shift_down(cur, prev8, j):
    rolled = pltpu.roll(cur, j, 0)
    row = lax.broadcasted_iota(jnp.int32, (8, cur.shape[1]), 0)
    first = jnp.where(row < j, pltpu.roll(prev8, j, 0), rolled[0:8])
    return jnp.concatenate([first, rolled[8:]], axis=0)


def _shift_up(cur, next8, j):
    t = cur.shape[0]
    rolled = pltpu.roll(cur, t - j, 0)
    row = lax.broadcasted_iota(jnp.int32, (8, cur.shape[1]), 0)
    last = jnp.where(row >= 8 - j, pltpu.roll(next8, 8 - j, 0), rolled[t - 8:t])
    return jnp.concatenate([rolled[:t - 8], last], axis=0)


def _colsum(x):
    return jnp.sum(x, axis=0, keepdims=True)


def _init_acc(*refs):
    @pl.when(pl.program_id(0) == 0)
    def _():
        for r in refs:
            r[...] = jnp.zeros_like(r)


def _call(body, *, name, out_shape, grid, in_specs, out_specs, scratch_shapes, args, carry=None):
    out_shape, out_specs, in_specs = tuple(out_shape), tuple(out_specs), list(in_specs)
    scratch_shapes = list(scratch_shapes)
    if carry is None:
        return pl.pallas_call(body, name=name, out_shape=out_shape, grid=grid, in_specs=in_specs, out_specs=out_specs,
                              scratch_shapes=scratch_shapes, compiler_params=_params(len(grid)))(*args)
    n_in, n_out, n_scr = len(in_specs), len(out_shape), len(scratch_shapes)

    def wrapped(*refs):
        ins, refs = refs[:n_in], refs[n_in:]
        x_ins, refs = refs[:carry.n_in], refs[carry.n_in:]
        outs, refs = refs[:n_out], refs[n_out:]
        x_outs, refs = refs[:carry.n_out], refs[carry.n_out:]
        scr, x_sems = refs[:n_scr], refs[n_scr:]
        at = [pl.program_id(d) for d in range(len(grid))]
        first = functools.reduce(lambda p, q: p & q, [i == 0 for i in at])
        last = functools.reduce(lambda p, q: p & q, [i == g - 1 for i, g in zip(at, grid)])
        pl.when(first)(lambda: carry.start(x_ins, x_outs, x_sems))
        body(*ins, *outs, *scr)
        pl.when(last)(lambda: carry.wait(x_ins, x_outs, x_sems))

    return pl.pallas_call(
        wrapped, name=name, out_shape=out_shape + carry.out_shapes, grid=grid, in_specs=in_specs + [ANY] * carry.n_in,
        out_specs=out_specs + (ANY,) * carry.n_out, scratch_shapes=scratch_shapes + carry.scratch,
        compiler_params=_params(len(grid)))(*args, *carry.arrays)


def _mm(a, b, *, name, ta=False, tb=False, out_dtype=F32, tm=512, tn=512, tk=512, bias=None, carry=None):
    m, k = (a.shape[1], a.shape[0]) if ta else a.shape
    n = b.shape[0] if tb else b.shape[1]
    assert k == (b.shape[1] if tb else b.shape[0]), (name, a.shape, b.shape)
    tm, tn, tk = min(tm, m), min(tn, n), min(tk, k)
    nk = k // tk
    assert m % tm == 0 and n % tn == 0 and k % tk == 0, (name, m, n, k)

    def body(*refs):
        if bias is None:
            a_ref, b_ref, o_ref, acc = refs
        else:
            a_ref, b_ref, bias_ref, o_ref, acc = refs
        kk = pl.program_id(2)

        @pl.when(kk == 0)
        def _():
            acc[...] = jnp.zeros_like(acc)

        dims = (((0 if ta else 1,), (1 if tb else 0,)), ((), ()))
        acc[...] += lax.dot_general(a_ref[...].astype(MXU_DTYPE), b_ref[...].astype(MXU_DTYPE), dims,
                                    preferred_element_type=F32)

        @pl.when(kk == nk - 1)
        def _():
            r = acc[...]
            if bias is not None:
                r = r + bias_ref[...]
            o_ref[...] = r.astype(out_dtype)

    a_spec = (pl.BlockSpec((tk, tm), lambda i, j, kk: (kk, i)) if ta
              else pl.BlockSpec((tm, tk), lambda i, j, kk: (i, kk)))
    b_spec = (pl.BlockSpec((tn, tk), lambda i, j, kk: (j, kk)) if tb
              else pl.BlockSpec((tk, tn), lambda i, j, kk: (kk, j)))
    in_specs, args = [a_spec, b_spec], [a, b]
    if bias is not None:
        in_specs.append(pl.BlockSpec((1, tn), lambda i, j, kk: (0, j)))
        args.append(bias)
    out = _call(body, name=name, out_shape=[SDS((m, n), out_dtype)], grid=(m // tm, n // tn, nk), in_specs=in_specs,
                out_specs=[pl.BlockSpec((tm, tn), lambda i, j, kk: (i, j))],
                scratch_shapes=[pltpu.VMEM((tm, tn), F32)], args=args, carry=carry)
    return out[0] if carry is None else out


def _silu_rows(c_all):
    def body(c_ref, o_ref):
        o_ref[...] = _silu(c_ref[...])
    return pl.pallas_call(body, name="cond_silu", out_shape=SDS(c_all.shape, F32))(c_all)


def _modulate(x, scale, shift, tb):
    s, d = x.shape

    def body(x_ref, sc_ref, sh_ref, o_ref):
        o_ref[...] = (x_ref[...] * (1.0 + sc_ref[...]) + sh_ref[...]).astype(MXU_DTYPE)

    return pl.pallas_call(body, name="modulate", out_shape=SDS((s, d), MXU_DTYPE), grid=(s // tb,),
                          in_specs=[_rows(tb, d), _const((1, d)), _const((1, d))], out_specs=_rows(tb, d),
                          compiler_params=_params(1))(x, scale, shift)


def _out_ln(ycat, w_out, x, gate, ln_g, ln_b, next_scale, next_shift, tb):
    s, d = x.shape

    def body(yc_ref, w_ref, x_ref, gt_ref, g_ref, b_ref, sc_ref, sh_ref, xn_ref, xh_ref, y_ref, rs_ref, hn_ref):
        y = jnp.dot(yc_ref[...], w_ref[...], preferred_element_type=F32)
        res = ALPHA * x_ref[...] + (1.0 + gt_ref[...]) * y
        mu = jnp.mean(res, axis=-1, keepdims=True)
        cen = res - mu
        var = jnp.mean(cen * cen, axis=-1, keepdims=True)
        rstd = lax.rsqrt(var + LN_EPS)
        xhat = cen * rstd
        xn = xhat * g_ref[...] + b_ref[...]
        xn_ref[...] = xn
        xh_ref[...] = xhat
        y_ref[...] = y
        rs_ref[...] = rstd
        hn_ref[...] = (xn * (1.0 + sc_ref[...]) + sh_ref[...]).astype(MXU_DTYPE)

    big = SDS((s, d), F32)
    return pl.pallas_call(
        body, name="out_proj_ln", out_shape=(big, big, big, SDS((s, 1), F32), SDS((s, d), MXU_DTYPE)), grid=(s // tb,),
        in_specs=[_rows(tb, d), pl.BlockSpec((d, d), lambda i: (0, 0), pipeline_mode=pl.Buffered(1)), _rows(tb, d)]
        + [_const((1, d))] * 5,
        out_specs=(_rows(tb, d), _rows(tb, d), _rows(tb, d), _rows(tb, 1), _rows(tb, d)), compiler_params=_params(1),
    )(ycat, w_out, x, gate, ln_g, ln_b, next_scale, next_shift)


def _ln_bwd(dxn, xhat, y, rstd, ln_g, gate, w_out, tb):
    s, d = dxn.shape

    def body(dxn_ref, xh_ref, y_ref, rs_ref, g_ref, gt_ref, w_ref, dy_ref, dxa_ref, dg_ref, db_ref, dgt_ref, dyc_ref):
        _init_acc(dg_ref, db_ref, dgt_ref)
        dxn_t, xh = dxn_ref[...], xh_ref[...]
        dxh = dxn_t * g_ref[...]
        dres = rs_ref[...] * (dxh - jnp.mean(dxh, axis=-1, keepdims=True)
                              - xh * jnp.mean(dxh * xh, axis=-1, keepdims=True))
        dyb = ((1.0 + gt_ref[...]) * dres).astype(MXU_DTYPE)
        dy_ref[...] = dyb
        dxa_ref[...] = ALPHA * dres
        dg_ref[...] += _colsum(dxn_t * xh)
        db_ref[...] += _colsum(dxn_t)
        dgt_ref[...] += _colsum(dres * y_ref[...])
        dyc_ref[...] = lax.dot_general(dyb, w_ref[...], (((1,), (1,)), ((), ())), preferred_element_type=F32)

    vec = SDS((1, d), F32)
    return pl.pallas_call(
        body, name="ln_bwd_dycat", out_shape=(SDS((s, d), MXU_DTYPE), SDS((s, d), F32), vec, vec, vec, SDS((s, d), F32)),
        grid=(s // tb,),
        in_specs=[_rows(tb, d), _rows(tb, d), _rows(tb, d), _rows(tb, 1), _const((1, d)), _const((1, d)),
                  pl.BlockSpec((d, d), lambda i: (0, 0), pipeline_mode=pl.Buffered(1))],
        out_specs=(_rows(tb, d), _rows(tb, d), _const((1, d)), _const((1, d)), _const((1, d)), _rows(tb, d)),
        compiler_params=_params(1))(dxn, xhat, y, rstd, ln_g, gate, w_out)


def _dh_mod_bwd(dproj, w_in, dxa, x, scale, carry=None):
    s, d = dxa.shape
    k = dproj.shape[1]
    tm, tn, tk = min(1024, s), 1024, 1536
    nk = k // tk
    assert s % tm == 0 and d % tn == 0 and k % tk == 0

    def body(a_ref, b_ref, dxa_ref, x_ref, sc_ref, dx_ref, dsh_ref, dsc_ref, acc):
        i, kk = pl.program_id(1), pl.program_id(2)

        @pl.when(kk == 0)
        def _():
            acc[...] = jnp.zeros_like(acc)

        @pl.when((kk == 0) & (i == 0))
        def _():
            dsh_ref[...] = jnp.zeros_like(dsh_ref)
            dsc_ref[...] = jnp.zeros_like(dsc_ref)

        acc[...] += lax.dot_general(a_ref[...], b_ref[...], (((1,), (1,)), ((), ())), preferred_element_type=F32)

        @pl.when(kk == nk - 1)
        def _():
            dh_t = acc[...]
            dx_ref[...] = dxa_ref[...] + dh_t * (1.0 + sc_ref[...])
            dsh_ref[...] += _colsum(dh_t)
            dsc_ref[...] += _colsum(dh_t * x_ref[...])

    tile = pl.BlockSpec((tm, tn), lambda j, i, kk: (i, j))
    vec = pl.BlockSpec((1, tn), lambda j, i, kk: (0, j))
    return _call(
        body, name="dh", out_shape=(SDS((s, d), F32), SDS((1, d), F32), SDS((1, d), F32)),
        grid=(d // tn, s // tm, nk),
        in_specs=[pl.BlockSpec((tm, tk), lambda j, i, kk: (i, kk)), pl.BlockSpec((tn, tk), lambda j, i, kk: (j, kk)),
                  tile, tile, vec],
        out_specs=(tile, vec, vec), scratch_shapes=[pltpu.VMEM((tm, tn), F32)],
        args=(dproj, w_in, dxa, x, scale), carry=carry)


def _out_ln_loss(ycat, w_out, x, gate, ln_g, ln_b, target, tb):
    s, d = x.shape

    def body(yc_ref, w_ref, x_ref, gt_ref, g_ref, b_ref, t_ref, l_ref, dy_ref, dxa_ref, dg_ref, db_ref, dgt_ref):
        _init_acc(l_ref, dg_ref, db_ref, dgt_ref)
        y = jnp.dot(yc_ref[...], w_ref[...], preferred_element_type=F32)
        res = ALPHA * x_ref[...] + (1.0 + gt_ref[...]) * y
        cen = res - jnp.mean(res, axis=-1, keepdims=True)
        rstd = lax.rsqrt(jnp.mean(cen * cen, axis=-1, keepdims=True) + LN_EPS)
        xh = cen * rstd
        err = xh * g_ref[...] + b_ref[...] - t_ref[...]
        l_ref[...] += (0.5 / d) * jnp.sum(err * err, keepdims=True)
        dxn_t = err * (1.0 / d)
        dxh = dxn_t * g_ref[...]
        dres = rstd * (dxh - jnp.mean(dxh, axis=-1, keepdims=True) - xh * jnp.mean(dxh * xh, axis=-1, keepdims=True))
        dy_ref[...] = ((1.0 + gt_ref[...]) * dres).astype(MXU_DTYPE)
        dxa_ref[...] = ALPHA * dres
        dg_ref[...] += _colsum(dxn_t * xh)
        db_ref[...] += _colsum(dxn_t)
        dgt_ref[...] += _colsum(dres * y)

    vec = SDS((1, d), F32)
    return pl.pallas_call(
        body, name="out_proj_ln_loss", out_shape=(SDS((1, 1), F32), SDS((s, d), MXU_DTYPE), SDS((s, d), F32), vec, vec, vec),
        grid=(s // tb,),
        in_specs=[_rows(tb, d), pl.BlockSpec((d, d), lambda i: (0, 0), pipeline_mode=pl.Buffered(1)), _rows(tb, d),
                  _const((1, d)), _const((1, d)), _const((1, d)), _rows(tb, d)],
        out_specs=(_const((1, 1)), _rows(tb, d), _rows(tb, d), _const((1, d)), _const((1, d)), _const((1, d))),
        compiler_params=_params(1))(ycat, w_out, x, gate, ln_g, ln_b, target)


def _conv_taps(u, up, w_ref, width):
    out = w_ref[width - 1:width, :] * u
    for j in range(width - 2, -1, -1):
        out = out + w_ref[j:j + 1, :] * _shift_down(u, up, width - 1 - j)
    return out


def _conv_taps_t(g, gn, w_ref, width):
    out = w_ref[width - 1:width, :] * g
    for j in range(width - 2, -1, -1):
        out = out + w_ref[j:j + 1, :] * _shift_up(g, gn, width - 1 - j)
    return out


def _conv_wgrad(dw_ref, g, u, up, width):
    dw_ref[width - 1:width, :] += _colsum(g * u)
    for j in range(width - 1):
        dw_ref[j:j + 1, :] += _colsum(g * _shift_down(u, up, width - 1 - j))


def _branch_a_fwd(proj, conv_w, tb):
    s = proj.shape[0]

    def body(ab, ac, ax, ag, acp, axp, w_ref, o_ref):
        has_prev = (pl.program_id(0) > 0).astype(F32)
        u = ac[...] * ax[...]
        up = acp[...] * axp[...] * has_prev
        o_ref[...] = (ab[...] * _conv_taps(u, up, w_ref, 3) * _silu(ag[...])).astype(MXU_DTYPE)

    return pl.pallas_call(
        body, name="branch_a_fwd", out_shape=SDS((s, BR), MXU_DTYPE), grid=(s // tb,),
        in_specs=[_rows(tb, BR, CB_AB), _rows(tb, BR, CB_AC), _rows(tb, BR, CB_AX), _rows(tb, BR, CB_AG),
                  _prev8(tb, BR, CB_AC), _prev8(tb, BR, CB_AX), _const((8, BR))],
        out_specs=_rows(tb, BR), compiler_params=_params(1))(proj, proj, proj, proj, proj, proj, conv_w)


def _branch_a_bwd(dycat, proj, conv_w, tb):
    s = proj.shape[0]

    def body(dy, dyn, ab, abn, ag, agn, ac, acp, ax, axp, w_ref, o_ref, dw_ref):
        _init_acc(dw_ref)
        i = pl.program_id(0)
        has_prev = (i > 0).astype(F32)
        has_next = (i < pl.num_programs(0) - 1).astype(F32)
        u = ac[...] * ax[...]
        up = acp[...] * axp[...] * has_prev
        v = _conv_taps(u, up, w_ref, 3)
        sg = _silu(ag[...])
        dv = dy[...] * ab[...] * sg
        dvn = dyn[...] * abn[...] * _silu(agn[...]) * has_next
        du = _conv_taps_t(dv, dvn, w_ref, 3)
        o_ref[:, 0:BR] = (dy[...] * v * sg).astype(MXU_DTYPE)
        o_ref[:, BR:2 * BR] = (du * ax[...]).astype(MXU_DTYPE)
        o_ref[:, 2 * BR:3 * BR] = (du * ac[...]).astype(MXU_DTYPE)
        o_ref[:, 3 * BR:4 * BR] = (dy[...] * ab[...] * v * _dsilu(ag[...])).astype(MXU_DTYPE)
        _conv_wgrad(dw_ref, dv, u, up, 3)

    return pl.pallas_call(
        body, name="branch_a_bwd", out_shape=(SDS((s, 4 * BR), MXU_DTYPE), SDS((8, BR), F32)), grid=(s // tb,),
        in_specs=[_rows(tb, BR, 0), _next8(tb, BR, s, 0),
                  _rows(tb, BR, CB_AB), _next8(tb, BR, s, CB_AB), _rows(tb, BR, CB_AG), _next8(tb, BR, s, CB_AG),
                  _rows(tb, BR, CB_AC), _prev8(tb, BR, CB_AC), _rows(tb, BR, CB_AX), _prev8(tb, BR, CB_AX),
                  _const((8, BR))],
        out_specs=(_rows(tb, 4 * BR), _const((8, BR))), compiler_params=_params(1),
    )(dycat, dycat, proj, proj, proj, proj, proj, proj, proj, proj, conv_w)


def _t5_bucket(dist):
    max_exact = REL_BUCKETS // 2
    nf = jnp.maximum(dist, 1).astype(F32)
    large = max_exact + (jnp.log(nf / max_exact) / math.log(REL_MAX_DIST / max_exact)
                         * (REL_BUCKETS - max_exact)).astype(jnp.int32)
    large = jnp.minimum(large, REL_BUCKETS - 1)
    return jnp.where(dist < max_exact, dist, large)


def _bucket_maps():
    maps = []
    i = jnp.arange(BLK)[:, None]
    j = jnp.arange(2 * BLK)[None, :]
    delta = i + BLK - j
    for window, dil in DILATIONS:
        span = window // dil
        bucket = _t5_bucket(jnp.clip(delta, 0, span) * dil)
        maps.append(jnp.where((delta >= 0) & (delta <= span), bucket, -1))
    return jnp.stack(maps).astype(jnp.int32)


def _bias_tables(rel_bias, buckets):
    n_pat = len(DILATIONS)

    def body(rb_ref, bk_ref, o_ref):
        for g in range(n_pat):
            bk = bk_ref[g]
            for h in range(ATT_HEADS):
                def per_bucket(b, acc):
                    return jnp.where(bk == b, rb_ref[b, h], acc)
                o_ref[g, h] = lax.fori_loop(0, REL_BUCKETS, per_bucket, jnp.full((BLK, 2 * BLK), NEG, F32))

    return pl.pallas_call(
        body, name="bias_tables", out_shape=SDS((n_pat, ATT_HEADS, BLK, 2 * BLK), F32),
        in_specs=[pl.BlockSpec(memory_space=pltpu.SMEM), pl.BlockSpec(memory_space=pltpu.VMEM)],
        compiler_params=_params())(rel_bias, buckets)


def _head_masks():
    lane = lax.broadcasted_iota(jnp.int32, (1, 2 * HEAD_DIM), 1)
    return [(lane < HEAD_DIM).astype(F32), (lane >= HEAD_DIM).astype(F32)]


def _strided(base, size, dil):
    return pl.ds(base, size, stride=dil) if dil > 1 else pl.ds(pl.multiple_of(base, BLK), size)


def _attn_groups(s, dil):
    return max(1, min(2048, s) // (dil * BLK))


def _attn_fwd(proj, bias, dil):
    s = proj.shape[0]
    grp = _attn_groups(s, dil)
    u1 = dil * BLK
    unit = grp * u1
    nb = s // unit
    w = 2 * HEAD_DIM
    q0, k0, v0 = (cb * (BR // w) for cb in (CB_Q, CB_K, CB_V))

    def body(q_ref, kc_ref, kp_ref, vc_ref, vp_ref, bias_ref, o_ref, lse_ref, kbuf, vbuf):
        n = pl.program_id(1)
        col = lax.broadcasted_iota(jnp.int32, (1, 2 * BLK), 1)
        masks = _head_masks()
        kbuf[0:u1, :] = kp_ref[...]
        kbuf[u1:, :] = kc_ref[...]
        vbuf[0:u1, :] = vp_ref[...]
        vbuf[u1:, :] = vc_ref[...]

        def per_r(t, carry):
            j = t // dil
            base = j * u1 + t % dil
            rows = _strided(base, BLK, dil)
            no_prev = jnp.where((n == 0) & (j == 0) & (col < BLK), NEG, 0.0)
            q = q_ref[rows, :] * (HEAD_DIM ** -0.5)
            k = kbuf[_strided(base, 2 * BLK, dil), :].astype(MXU_DTYPE)
            v = vbuf[_strided(base, 2 * BLK, dil), :].astype(MXU_DTYPE)
            q2 = jnp.concatenate([q * masks[0], q * masks[1]], axis=0).astype(MXU_DTYPE)
            sc = lax.dot_general(q2, k, (((1,), (1,)), ((), ())), preferred_element_type=F32)
            sc = sc + jnp.concatenate([bias_ref[0], bias_ref[1]], axis=0) + no_prev
            mx = jnp.max(sc, axis=-1, keepdims=True)
            p = jnp.exp(sc - mx)
            l = jnp.sum(p, axis=-1, keepdims=True)
            o2 = jnp.dot((p / l).astype(MXU_DTYPE), v, preferred_element_type=F32)
            lse2 = mx + jnp.log(l)
            o_ref[rows, :] = o2[0:BLK] * masks[0] + o2[BLK:2 * BLK] * masks[1]
            lse_ref[rows, :] = lse2[0:BLK] * masks[0] + lse2[BLK:2 * BLK] * masks[1]
            return carry

        lax.fori_loop(0, grp * dil, per_r, 0, unroll=8)

    cur = lambda c0: pl.BlockSpec((unit, w), lambda hp, n: (n, c0 + hp))
    prev = lambda c0: pl.BlockSpec((u1, w), lambda hp, n: (jnp.maximum(n * grp - 1, 0), c0 + hp))
    out = pl.BlockSpec((unit, w), lambda hp, n: (n, hp))
    return pl.pallas_call(
        body, name=f"attn_fwd_d{dil}", out_shape=(SDS((s, BR), F32), SDS((s, BR), F32)), grid=(BR // w, nb),
        in_specs=[cur(q0), cur(k0), prev(k0), cur(v0), prev(v0),
                  pl.BlockSpec((2, BLK, 2 * BLK), lambda hp, n: (hp, 0, 0))],
        out_specs=(out, out),
        scratch_shapes=[pltpu.VMEM((unit + u1, w), F32), pltpu.VMEM((unit + u1, w), F32)],
        compiler_params=_params(2))(proj, proj, proj, proj, proj, bias)


def _softmax3(l0, l1, l2):
    mx = jnp.maximum(jnp.maximum(l0, l1), l2)
    e0, e1, e2 = jnp.exp(l0 - mx), jnp.exp(l1 - mx), jnp.exp(l2 - mx)
    inv = 1.0 / (e0 + e1 + e2)
    return e0 * inv, e1 * inv, e2 * inv


def _attn_combine(os_, lses, proj, tb):
    s = proj.shape[0]

    def body(o0, o1, o2, l0, l1, l2, bg, y_ref):
        w0, w1, w2 = _softmax3(l0[...], l1[...], l2[...])
        attn = w0 * o0[...] + w1 * o1[...] + w2 * o2[...]
        y_ref[...] = (attn * _silu(bg[...])).astype(MXU_DTYPE)

    return pl.pallas_call(
        body, name="attn_combine", out_shape=SDS((s, BR), MXU_DTYPE), grid=(s // tb,),
        in_specs=[_rows(tb, BR)] * 6 + [_rows(tb, BR, CB_BG)], out_specs=_rows(tb, BR),
        compiler_params=_params(1))(*os_, *lses, proj)


def _attn_bwd_pre(dycat, os_, lses, proj, head_ones, tb):
    s = proj.shape[0]

    def body(dy, o0, o1, o2, l0, l1, l2, bg, e_ref, dbg_ref, do0, do1, do2, dm0, dm1, dm2):
        w0, w1, w2 = _softmax3(l0[...], l1[...], l2[...])
        attn = w0 * o0[...] + w1 * o1[...] + w2 * o2[...]
        dattn = dy[...] * _silu(bg[...])
        dbg_ref[...] = (dy[...] * attn * _dsilu(bg[...])).astype(MXU_DTYPE)
        prod = dattn * attn
        hi = prod.astype(MXU_DTYPE)
        lo = (prod - hi.astype(F32)).astype(MXU_DTYPE)
        tot = (jnp.dot(hi, e_ref[...], preferred_element_type=F32)
               + jnp.dot(lo, e_ref[...], preferred_element_type=F32))
        for wg, do_ref, dm_ref in ((w0, do0, dm0), (w1, do1, dm1), (w2, do2, dm2)):
            do_ref[...] = wg * dattn
            dm_ref[...] = wg * tot

    big = SDS((s, BR), F32)
    return pl.pallas_call(
        body, name="attn_bwd_pre", out_shape=(SDS((s, BR), MXU_DTYPE),) + (big,) * 6, grid=(s // tb,),
        in_specs=[_rows(tb, BR, 1)] + [_rows(tb, BR)] * 6 + [_rows(tb, BR, CB_BG), _const((BR, BR))],
        out_specs=(_rows(tb, BR),) * 7, compiler_params=_params(1))(dycat, *os_, *lses, proj, head_ones)


def _attn_bwd(proj, do, lse, dm, bias, dil, carry=None, add=()):
    s = proj.shape[0]
    grp = _attn_groups(s, dil)
    u1 = dil * BLK
    unit = grp * u1
    nb = s // unit
    w = 2 * HEAD_DIM
    q0, k0, v0 = (cb * (BR // w) for cb in (CB_Q, CB_K, CB_V))
    n_add = len(add)

    def body(q_ref, kc_ref, kp_ref, vc_ref, vp_ref, do_ref, lse_ref, dm_ref, bias_ref, *rest):
        more, (dq_ref, dk_ref, dv_ref, dbias_ref, kbuf, vbuf, stage_k, stage_v) = rest[:3 * n_add], rest[3 * n_add:]
        more_q, more_k, more_v = more[0::3], more[1::3], more[2::3]
        plus = lambda val, refs, rows: functools.reduce(lambda acc, r: acc + r[rows, :], refs, val)
        n = pl.program_id(1)
        col = lax.broadcasted_iota(jnp.int32, (1, 2 * BLK), 1)
        masks = _head_masks()

        @pl.when(n == 0)
        def _():
            dbias_ref[...] = jnp.zeros_like(dbias_ref)
            stage_k[...] = jnp.zeros_like(stage_k)
            stage_v[...] = jnp.zeros_like(stage_v)

        for out_ref, stage, more_ in ((dk_ref, stage_k, more_k), (dv_ref, stage_v, more_v)):
            if grp > 1:
                out_ref[0:unit - u1, :] = plus(stage[u1:unit, :], more_, slice(0, unit - u1))
            stage[0:u1, :] = stage[unit:unit + u1, :]

        @pl.when(n < nb)
        def _():
            kbuf[0:u1, :] = kp_ref[...]
            kbuf[u1:, :] = kc_ref[...]
            vbuf[0:u1, :] = vp_ref[...]
            vbuf[u1:, :] = vc_ref[...]

            def per_r(t, carry):
                j = t // dil
                base = j * u1 + t % dil
                rows = _strided(base, BLK, dil)
                rows_hi = _strided(base + u1, BLK, dil)
                no_prev = jnp.where((n == 0) & (j == 0) & (col < BLK), NEG, 0.0)
                q = q_ref[rows, :] * (HEAD_DIM ** -0.5)
                k = kbuf[_strided(base, 2 * BLK, dil), :].astype(MXU_DTYPE)
                v = vbuf[_strided(base, 2 * BLK, dil), :].astype(MXU_DTYPE)
                do_t, lse_t, dm_t = do_ref[rows, :], lse_ref[rows, :], dm_ref[rows, :]
                stack = lambda t: jnp.concatenate([t * masks[0], t * masks[1]], axis=0).astype(MXU_DTYPE)
                per_head = lambda t: jnp.concatenate([t[:, 0:1], t[:, HEAD_DIM:HEAD_DIM + 1]], axis=0)
                q2, do2 = stack(q), stack(do_t)
                sc = lax.dot_general(q2, k, (((1,), (1,)), ((), ())), preferred_element_type=F32)
                p = jnp.exp(sc + jnp.concatenate([bias_ref[0], bias_ref[1]], axis=0) + no_prev - per_head(lse_t))
                dp = lax.dot_general(do2, v, (((1,), (1,)), ((), ())), preferred_element_type=F32)
                ds = p * (dp - per_head(dm_t))
                dbias_ref[0] += ds[0:BLK]
                dbias_ref[1] += ds[BLK:2 * BLK]
                dsb, pb = ds.astype(MXU_DTYPE), p.astype(MXU_DTYPE)
                dq2 = jnp.dot(dsb, k, preferred_element_type=F32)
                dk_acc = lax.dot_general(dsb, q2, (((0,), (0,)), ((), ())), preferred_element_type=F32)
                dv_acc = lax.dot_general(pb, do2, (((0,), (0,)), ((), ())), preferred_element_type=F32)
                dq_ref[rows, :] = plus((dq2[0:BLK] * masks[0] + dq2[BLK:2 * BLK] * masks[1]) * (HEAD_DIM ** -0.5),
                                       more_q, rows)
                stage_k[rows, :] = stage_k[rows, :] + dk_acc[0:BLK]
                stage_v[rows, :] = stage_v[rows, :] + dv_acc[0:BLK]
                stage_k[rows_hi, :] = dk_acc[BLK:2 * BLK]
                stage_v[rows_hi, :] = dv_acc[BLK:2 * BLK]
                return carry

            lax.fori_loop(0, grp * dil, per_r, 0, unroll=8)

        dk_ref[unit - u1:unit, :] = plus(stage_k[0:u1, :], more_k, slice(unit - u1, unit))
        dv_ref[unit - u1:unit, :] = plus(stage_v[0:u1, :], more_v, slice(unit - u1, unit))

    qn = lambda n: jnp.minimum(n, nb - 1)
    cur = lambda c0: pl.BlockSpec((unit, w), lambda hp, n: (qn(n), c0 + hp))
    prev = lambda c0: pl.BlockSpec((u1, w), lambda hp, n: (jnp.maximum(qn(n) * grp - 1, 0), c0 + hp))
    row = pl.BlockSpec((unit, w), lambda hp, n: (qn(n), hp))
    late = pl.BlockSpec((unit, w), lambda hp, n: (jnp.maximum(n - 1, 0), hp))
    tab = pl.BlockSpec((2, BLK, 2 * BLK), lambda hp, n: (hp, 0, 0))
    big = SDS((s, BR), F32)
    return _call(
        body, name=f"attn_bwd_d{dil}", out_shape=(big, big, big, SDS((ATT_HEADS, BLK, 2 * BLK), F32)),
        grid=(BR // w, nb + 1),
        in_specs=[cur(q0), cur(k0), prev(k0), cur(v0), prev(v0), row, row, row, tab] + [row, late, late] * n_add,
        out_specs=(row, late, late, tab),
        scratch_shapes=[pltpu.VMEM((unit + u1, w), F32)] * 4,
        args=(proj, proj, proj, proj, proj, do, lse, dm, bias) + tuple(t for part in add for t in part), carry=carry)


def _rel_bias_grad(dbias, buckets):
    def body(db_ref, bk_ref, o_ref):
        row = lax.broadcasted_iota(jnp.int32, (REL_BUCKETS, 128), 0)
        lane = lax.broadcasted_iota(jnp.int32, (REL_BUCKETS, 128), 1)

        def per_bucket(b, acc):
            for g in range(len(DILATIONS)):
                hit = bk_ref[g] == b
                for h in range(ATT_HEADS):
                    both = db_ref[0, g, h] + db_ref[1, g, h]
                    val = jnp.sum(jnp.where(hit, both, 0.0), keepdims=True)
                    acc = acc + jnp.where((row == b) & (lane == h), val, 0.0)
            return acc

        o_ref[...] = lax.fori_loop(0, REL_BUCKETS, per_bucket, jnp.zeros((REL_BUCKETS, 128), F32))

    assert dbias.shape[0] == DEPTH == 2
    return pl.pallas_call(body, name="rel_bias_grad", out_shape=SDS((REL_BUCKETS, 128), F32),
                          compiler_params=_params())(dbias, buckets)


def _scan_rows(a_ref, b_ref, o_ref, carry, *, reverse):
    tb = a_ref.shape[0]
    order = range(7, -1, -1) if reverse else range(8)

    @pl.when(pl.program_id(0) == 0)
    def _():
        carry[...] = jnp.zeros_like(carry)

    def group(gi, h):
        r0 = pl.multiple_of((tb // 8 - 1 - gi if reverse else gi) * 8, 8)
        a8, b8 = a_ref[pl.ds(r0, 8), :], b_ref[pl.ds(r0, 8), :]
        rows = [None] * 8
        for k in order:
            if reverse:
                rows[k] = b8[k:k + 1] + h
                h = a8[k:k + 1] * rows[k]
            else:
                h = a8[k:k + 1] * h + b8[k:k + 1]
                rows[k] = h
        o_ref[pl.ds(r0, 8), :] = jnp.concatenate(rows, axis=0)
        return h

    carry[...] = lax.fori_loop(0, tb // 8, group, carry[...], unroll=4)


def _lru_scan_fwd(a, b, proj, tb):
    s = a.shape[0]

    def body(a_ref, b_ref, g_ref, h_ref, y_ref, carry):
        _scan_rows(a_ref, b_ref, h_ref, carry, reverse=False)
        y_ref[...] = (h_ref[...] * _silu(g_ref[...])).astype(MXU_DTYPE)

    return pl.pallas_call(
        body, name="lru_scan", out_shape=(SDS((s, BR), F32), SDS((s, BR), MXU_DTYPE)), grid=(s // tb,),
        in_specs=[_rows(tb, BR), _rows(tb, BR), _rows(tb, BR, CB_CG)], out_specs=(_rows(tb, BR), _rows(tb, BR)),
        scratch_shapes=[pltpu.VMEM((1, BR), F32)], compiler_params=_params(1))(a, b, proj)


def _lru_scan_bwd(a, dycat, h, proj, tb):
    s = a.shape[0]
    nt = s // tb

    def body(a_ref, dy_ref, h_ref, g_ref, l_ref, dg_ref, carry, dh_buf):
        dh_buf[...] = dy_ref[...] * _silu(g_ref[...])
        dg_ref[...] = (dy_ref[...] * h_ref[...] * _dsilu(g_ref[...])).astype(MXU_DTYPE)
        _scan_rows(a_ref, dh_buf, l_ref, carry, reverse=True)

    rev = lambda cb=0: pl.BlockSpec((tb, BR), lambda i: (nt - 1 - i, cb))
    return pl.pallas_call(
        body, name="lru_scan_bwd", out_shape=(SDS((s, BR), F32), SDS((s, BR), MXU_DTYPE)), grid=(nt,),
        in_specs=[rev(), rev(2), rev(), rev(CB_CG)], out_specs=(rev(), rev()),
        scratch_shapes=[pltpu.VMEM((1, BR), F32), pltpu.VMEM((tb, BR), F32)],
        compiler_params=_params(1))(a, dycat, h, proj)


def _scan_tile(s):
    return min(512, s)


def _load_chunked(ref, t0, pt):
    ln = pt // 8
    return jnp.concatenate([ref[pl.ds(t0 + j, 8, stride=ln), :] for j in range(ln)], axis=0)


def _store_natural(ref, t0, pt, val):
    ln = pt // 8
    for j in range(ln):
        ref[pl.ds(t0 + j, 8, stride=ln), :] = val[j * 8:(j + 1) * 8]


def _scan_tile_in_place(a_ref, x_ref, carry, pw, *, reverse):
    ch2 = x_ref.shape[1]
    ch = ch2 // 2
    ln = x_ref.shape[0] // 8
    ar = a_ref[:, 0:ch]
    ai = -a_ref[:, ch:ch2] if reverse else a_ref[:, ch:ch2]

    def cmul(pr, pi, xr, xi):
        return pr * xr - pi * xi, pr * xi + pi * xr

    @pl.when(pl.program_id(0) == 0)
    def _():
        carry[...] = jnp.zeros_like(carry)

        def fill(j, p):
            pw[pl.ds(j, 1), 0:ch] = p[0]
            pw[pl.ds(j, 1), ch:ch2] = p[1]
            return cmul(ar, ai, *p)

        lax.fori_loop(0, ln, fill, (ar, ai))

    def rows_of(j):
        return pl.ds(pl.multiple_of((ln - 1 - j if reverse else j) * 8, 8), 8)

    def local(j, x):
        rows = rows_of(j)
        nr, ni = cmul(ar, ai, *x)
        xr, xi = nr + x_ref[rows, 0:ch], ni + x_ref[rows, ch:ch2]
        x_ref[rows, 0:ch] = xr
        x_ref[rows, ch:ch2] = xi
        return xr, xi

    zero = jnp.zeros((8, ch), F32)
    er, ei = lax.fori_loop(0, ln, local, (zero, zero), unroll=2)
    apr, api = pw[ln - 1:ln, 0:ch], pw[ln - 1:ln, ch:ch2]
    cr, ci = carry[:, 0:ch], carry[:, ch:ch2]
    into_r, into_i = [None] * 8, [None] * 8
    for c in (range(7, -1, -1) if reverse else range(8)):
        into_r[c], into_i[c] = cr, ci
        pr, pi = cmul(apr, api, cr, ci)
        cr, ci = er[c:c + 1] + pr, ei[c:c + 1] + pi
    carry[:, 0:ch] = cr
    carry[:, ch:ch2] = ci
    into_r, into_i = jnp.concatenate(into_r, axis=0), jnp.concatenate(into_i, axis=0)

    def fix(j, carry_):
        rows = rows_of(j)
        dr, di = cmul(pw[pl.ds(j, 1), 0:ch], pw[pl.ds(j, 1), ch:ch2], into_r, into_i)
        x_ref[rows, 0:ch] += dr
        x_ref[rows, ch:ch2] += di
        return carry_

    lax.fori_loop(0, ln, fix, 0, unroll=2)


def _neg_expm1(z):
    series = -z * (1.0 + z * (0.5 + z * (1.0 / 6 + z * (1.0 / 24 + z * (1.0 / 120)))))
    return jnp.where(z > -0.05, series, 1.0 - jnp.exp(z))


def _lru_gate(xc, pre_r, pre_i, lam):
    log_a = -LRU_C * jax.nn.sigmoid(pre_r) * jax.nn.softplus(-lam)
    return jnp.exp(log_a), jnp.sqrt(_neg_expm1(2.0 * log_a)) * jax.nn.sigmoid(pre_i) * xc


def _lru_gates_fwd(proj, conv_w, conv_b, w_cat, b_cat, lam, tb):
    s = proj.shape[0]

    def body(cx, cxp, w_ref, cb_ref, wc_ref, bc_ref, lam_ref, a_ref, b_ref):
        has_prev = (pl.program_id(0) > 0).astype(F32)
        xc = _conv_taps(cx[...], cxp[...] * has_prev, w_ref, 4) + cb_ref[...]
        pre = jnp.dot(xc.astype(MXU_DTYPE), wc_ref[...], preferred_element_type=F32) + bc_ref[...]
        a_ref[...], b_ref[...] = _lru_gate(xc, pre[:, 0:BR], pre[:, BR:2 * BR], lam_ref[...])

    big = SDS((s, BR), F32)
    return pl.pallas_call(
        body, name="lru_gates_fwd", out_shape=(big, big), grid=(s // tb,),
        in_specs=[_rows(tb, BR, CB_CX), _prev8(tb, BR, CB_CX), _const((8, BR)), _const((1, BR)),
                  _const((BR, 2 * BR)), _const((1, 2 * BR)), _const((1, BR))],
        out_specs=(_rows(tb, BR), _rows(tb, BR)), compiler_params=_params(1),
    )(proj, proj, conv_w, conv_b, w_cat, b_cat, lam)


def _lru_gates_bwd(proj, lmb, h, conv_w, conv_b, w_cat, b_cat, lam, tb):
    s = proj.shape[0]

    def body(cx, cxp, l_ref, h_ref, hp_ref, w_ref, cb_ref, wc_ref, bc_ref, lam_ref,
             dxc_ref, dpre_ref, xc_ref, dbc_ref, dlam_ref):
        _init_acc(dbc_ref, dlam_ref)
        has_prev = (pl.program_id(0) > 0).astype(F32)
        xc = _conv_taps(cx[...], cxp[...] * has_prev, w_ref, 4) + cb_ref[...]
        xcb = xc.astype(MXU_DTYPE)
        pre = jnp.dot(xcb, wc_ref[...], preferred_element_type=F32) + bc_ref[...]
        _, vjp = jax.vjp(_lru_gate, xc, pre[:, 0:BR], pre[:, BR:2 * BR], lam_ref[...])
        lm = l_ref[...]
        dxc, dpr, dpi, dlam = vjp((lm * _shift_down(h_ref[...], hp_ref[...] * has_prev, 1), lm))
        dpre = jnp.concatenate([dpr, dpi], axis=1)
        dpreb = dpre.astype(MXU_DTYPE)
        dxc_ref[...] = dxc + lax.dot_general(dpreb, wc_ref[...], (((1,), (1,)), ((), ())),
                                             preferred_element_type=F32)
        dpre_ref[...] = dpreb
        xc_ref[...] = xcb
        dbc_ref[...] += _colsum(dpre)
        dlam_ref[...] += dlam

    return pl.pallas_call(
        body, name="lru_gates_bwd",
        out_shape=(SDS((s, BR), F32), SDS((s, 2 * BR), MXU_DTYPE), SDS((s, BR), MXU_DTYPE),
                   SDS((1, 2 * BR), F32), SDS((1, BR), F32)),
        grid=(s // tb,),
        in_specs=[_rows(tb, BR, CB_CX), _prev8(tb, BR, CB_CX), _rows(tb, BR), _rows(tb, BR), _prev8(tb, BR),
                  _const((8, BR)), _const((1, BR)), _const((BR, 2 * BR)), _const((1, 2 * BR)), _const((1, BR))],
        out_specs=(_rows(tb, BR), _rows(tb, 2 * BR), _rows(tb, BR), _const((1, 2 * BR)), _const((1, BR))),
        compiler_params=_params(1))(proj, proj, lmb, h, h, conv_w, conv_b, w_cat, b_cat, lam)


def _conv_c_bwd(dxc, proj, conv_w, tb):
    s = proj.shape[0]

    def body(g, gn, cx, cxp, w_ref, dcx_ref, dw_ref, db_ref):
        _init_acc(dw_ref, db_ref)
        i = pl.program_id(0)
        has_prev = (i > 0).astype(F32)
        has_next = (i < pl.num_programs(0) - 1).astype(F32)
        gt = g[...]
        dcx_ref[...] = _conv_taps_t(gt, gn[...] * has_next, w_ref, 4).astype(MXU_DTYPE)
        _conv_wgrad(dw_ref, gt, cx[...], cxp[...] * has_prev, 4)
        db_ref[...] += _colsum(gt)

    return pl.pallas_call(
        body, name="conv_c_bwd", out_shape=(SDS((s, BR), MXU_DTYPE), SDS((8, BR), F32), SDS((1, BR), F32)),
        grid=(s // tb,),
        in_specs=[_rows(tb, BR), _next8(tb, BR, s), _rows(tb, BR, CB_CX), _prev8(tb, BR, CB_CX), _const((8, BR))],
        out_specs=(_rows(tb, BR), _const((8, BR)), _const((1, BR))), compiler_params=_params(1),
    )(dxc, dxc, proj, proj, conv_w)


def _s5_disc(lam_re, lam_im, log_dt):
    dt = jnp.exp(log_dt)
    mag = jnp.exp(lam_re * dt)
    ab_re = mag * jnp.cos(lam_im * dt)
    ab_im = mag * jnp.sin(lam_im * dt)
    den = lam_re * lam_re + lam_im * lam_im
    f_re = ((ab_re - 1.0) * lam_re + ab_im * lam_im) / den
    f_im = (ab_im * lam_re - (ab_re - 1.0) * lam_im) / den
    return ab_re, ab_im, f_re, f_im


def _s5_bbar(f_re, f_im, b_re, b_im):
    return f_re * b_re - f_im * b_im, f_re * b_im + f_im * b_re


def _s5_disc_fwd(lam_re, lam_im, log_dt):
    def body(lr, li, ld, o0, o1, o2, o3):
        o0[...], o1[...], o2[...], o3[...] = _s5_disc(lr[...], li[...], ld[...])
    return pl.pallas_call(body, name="s5_disc_fwd", out_shape=(SDS(lam_re.shape, F32),) * 4)(lam_re, lam_im, log_dt)


def _s5_disc_bwd(lam_re, lam_im, log_dt, cts):
    def body(lr, li, ld, c0, c1, c2, c3, o0, o1, o2):
        _, vjp = jax.vjp(_s5_disc, lr[...], li[...], ld[...])
        o0[...], o1[...], o2[...] = vjp((c0[...], c1[...], c2[...], c3[...]))
    return pl.pallas_call(body, name="s5_disc_bwd", out_shape=(SDS(lam_re.shape, F32), SDS(lam_re.shape, F32),
                                                                SDS(log_dt.shape, F32)))(lam_re, lam_im, log_dt, *cts)


def _s5_bbar_fwd(f_re, f_im, b_re, b_im):
    def body(fr, fi, br, bi, o0, o1):
        o0[...], o1[...] = _s5_bbar(fr[...], fi[...], br[...], bi[...])
    return pl.pallas_call(body, name="s5_bbar_fwd", out_shape=(SDS(b_re.shape, F32),) * 2)(f_re, f_im, b_re, b_im)


def _s5_bbar_bwd(f_re, f_im, b_re, b_im, d_re, d_im):
    def body(fr, fi, br, bi, dr, di, o0, o1, o2, o3):
        _, vjp = jax.vjp(_s5_bbar, fr[...], fi[...], br[...], bi[...])
        o0[...], o1[...], o2[...], o3[...] = vjp((dr[...], di[...]))
    col, mat = SDS(f_re.shape, F32), SDS(b_re.shape, F32)
    return pl.pallas_call(body, name="s5_bbar_bwd", out_shape=(col, col, mat, mat))(f_re, f_im, b_re, b_im, d_re, d_im)


def _s5_tail_bwd(dycat, ylin, proj, d_skip, w_glu, b_glu, tb):
    s = proj.shape[0]

    def body(dy, yl, u, dg, dk, w_ref, b_ref, dyl_ref, dus_ref, ddg_ref, g_ref, dt_ref, ddk_ref, dbg_ref):
        _init_acc(ddk_ref, dbg_ref)
        g, gelu_vjp = jax.vjp(jax.nn.gelu, yl[...] + dk[...] * u[...])
        gb = g.astype(MXU_DTYPE)
        sg = jax.nn.sigmoid(jnp.dot(gb, w_ref[...], preferred_element_type=F32) + b_ref[...])
        dz = dy[...] * _silu(dg[...])
        ddg_ref[...] = (dy[...] * g * sg * _dsilu(dg[...])).astype(MXU_DTYPE)
        dt = dz * g * sg * (1.0 - sg)
        dtb = dt.astype(MXU_DTYPE)
        dgel = dz * sg + lax.dot_general(dtb, w_ref[...], (((1,), (1,)), ((), ())), preferred_element_type=F32)
        dyv, = gelu_vjp(dgel)
        dyl_ref[...] = dyv
        dus_ref[...] = dyv * dk[...]
        g_ref[...] = gb
        dt_ref[...] = dtb
        ddk_ref[...] += _colsum(dyv * u[...])
        dbg_ref[...] += _colsum(dt)

    big, half, vec = SDS((s, BR), F32), SDS((s, BR), MXU_DTYPE), SDS((1, BR), F32)
    return pl.pallas_call(
        body, name="s5_tail_bwd", out_shape=(big, big, half, half, half, vec, vec), grid=(s // tb,),
        in_specs=[_rows(tb, BR, 3), _rows(tb, BR), _rows(tb, BR, CB_DU), _rows(tb, BR, CB_DG), _const((1, BR)),
                  _const((BR, BR)), _const((1, BR))],
        out_specs=(_rows(tb, BR),) * 5 + (_const((1, BR)), _const((1, BR))), compiler_params=_params(1),
    )(dycat, ylin, proj, proj, d_skip, w_glu, b_glu)


def _assemble_dproj(da, dqkv, dbg, dcx, dcg, du, dus, ddg, tb):
    s = da.shape[0]

    def body(da_ref, dq_ref, dk_ref, dv_ref, dbg_ref, dcx_ref, dcg_ref, du_ref, dus_ref, ddg_ref, o_ref):
        o_ref[:, 0:4 * BR] = da_ref[...]
        for j, part in enumerate((dq_ref, dk_ref, dv_ref)):
            o_ref[:, (4 + j) * BR:(5 + j) * BR] = part[...].astype(MXU_DTYPE)
        o_ref[:, 7 * BR:8 * BR] = dbg_ref[...].astype(MXU_DTYPE)
        o_ref[:, 8 * BR:9 * BR] = dcx_ref[...].astype(MXU_DTYPE)
        o_ref[:, 9 * BR:10 * BR] = dcg_ref[...].astype(MXU_DTYPE)
        o_ref[:, 10 * BR:11 * BR] = (du_ref[...] + dus_ref[...]).astype(MXU_DTYPE)
        o_ref[:, 11 * BR:12 * BR] = ddg_ref[...].astype(MXU_DTYPE)

    return pl.pallas_call(
        body, name="assemble_dproj", out_shape=SDS((s, N_IN), MXU_DTYPE), grid=(s // tb,),
        in_specs=[_rows(tb, 4 * BR)] + [_rows(tb, BR)] * 9, out_specs=_rows(tb, N_IN),
        compiler_params=_params(1))(da, *dqkv, dbg, dcx, dcg, du, dus, ddg)


def _sum_leading(xs, tr, name):
    n, _, c = xs[0].shape
    nl = len(xs)
    tr = min([tr] + [x.shape[1] for x in xs])
    assert all(x.shape[1] % tr == 0 for x in xs), (name, tr)
    nrs = [x.shape[1] // tr for x in xs]
    starts = [sum(nrs[:l]) for l in range(nl)]

    def body(*refs):
        i = pl.program_id(0)
        for l in range(nl):
            @pl.when((i >= starts[l]) & (i < starts[l] + nrs[l]))
            def _():
                acc = refs[l * n][...].astype(F32)
                for ref in refs[l * n + 1:(l + 1) * n]:
                    acc = acc + ref[...].astype(F32)
                refs[nl * n][...] = acc

    specs = [pl.BlockSpec((None, tr, c), functools.partial(
        lambda i, k, l: (k, jnp.clip(i - starts[l], 0, nrs[l] - 1), 0), k=k, l=l)) for l in range(nl) for k in range(n)]
    return pl.pallas_call(body, name=name, out_shape=SDS((sum(nrs) * tr, c), F32), grid=(sum(nrs),), in_specs=specs,
                          out_specs=pl.BlockSpec((tr, c), lambda i: (i, 0)),
                          compiler_params=_params(1))(*[x for x in xs for _ in range(n)])


def _adamw(w, g_parts, m, v, tr, name):
    r, c = w.shape
    tr = min(tr, r)
    n = len(g_parts)
    assert r % tr == 0, (name, r, tr)

    def body(*refs):
        w_ref, m_ref, v_ref = refs[0], refs[1 + n], refs[2 + n]
        g_ref, d_ref, nm_ref, nv_ref = refs[3 + n:]
        g = refs[1][...]
        for ref in refs[2:1 + n]:
            g = g + ref[...]
        mm = ADAM_B1 * m_ref[...] + (1.0 - ADAM_B1) * g
        vv = ADAM_B2 * v_ref[...] + (1.0 - ADAM_B2) * jnp.square(g)
        m_hat = mm / (1.0 - ADAM_B1 ** ADAM_STEP)
        v_hat = vv / (1.0 - ADAM_B2 ** ADAM_STEP)
        g_ref[...] = g
        d_ref[...] = -ADAM_LR * (m_hat / (jnp.sqrt(v_hat) + ADAM_EPS) + ADAM_WD * w_ref[...])
        nm_ref[...] = mm
        nv_ref[...] = vv

    spec = pl.BlockSpec((tr, c), lambda i: (i, 0))
    return _call(body, name=name, out_shape=(SDS((r, c), F32),) * 4, grid=(r // tr,), in_specs=[spec] * (3 + n),
                 out_specs=(spec,) * 4, scratch_shapes=[], args=(w, *g_parts, m, v))


class _AllGather8:
    def __init__(self, block):
        self.m_per = block.shape[0]
        self.arrays, self.n_in, self.n_out = [block], 1, 1
        self.out_shapes = (SDS((N_DEV * self.m_per, block.shape[1]), block.dtype),)
        self.scratch = [pltpu.SemaphoreType.DMA((7,)), pltpu.SemaphoreType.DMA((7,)), pltpu.SemaphoreType.DMA]

    def _copies(self, ins, outs, sems):
        (x_ref,), (out_ref,), (send_sems, recv_sems, local_sem) = ins, outs, sems
        x, y, c = lax.axis_index("x"), lax.axis_index("y"), lax.axis_index("c")
        me, sibling = (x, y, c), (x, y, 1 - c)
        chips = [(1 - x, y), (x, 1 - y), (1 - x, 1 - y)]

        def rows(px, py, pc):
            return out_ref.at[pl.ds((4 * px + 2 * py + pc) * self.m_per, self.m_per), :]

        def copy(k, blk, to, src=None):
            return pltpu.make_async_remote_copy(
                src_ref=rows(*blk) if src is None else src, dst_ref=rows(*blk), send_sem=send_sems.at[k],
                recv_sem=recv_sems.at[k], device_id=to, device_id_type=MESH)

        mine = pltpu.make_async_copy(x_ref, rows(*me), local_sem)
        first = [copy(0, me, sibling, src=x_ref)]
        first += [copy(1 + j, me, (*chip, c), src=x_ref) for j, chip in enumerate(chips)]
        passed = [copy(4 + j, (*chip, c), sibling) for j, chip in enumerate(chips)]
        arrivals = [copy(1 + j, (*chip, c), me) for j, chip in enumerate(chips)]
        from_sibling = [copy(0, sibling, me)] + [copy(4 + j, (*chip, 1 - c), me) for j, chip in enumerate(chips)]
        return mine, first, passed, arrivals, from_sibling

    def start(self, ins, outs, sems):
        mine, first, _, _, _ = self._copies(ins, outs, sems)
        mine.start()
        for cp in first:
            cp.start()

    def wait(self, ins, outs, sems):
        mine, first, passed, arrivals, from_sibling = self._copies(ins, outs, sems)
        for arrived, onward in zip(arrivals, passed):
            arrived.wait_recv()
            onward.start()
        for cp in from_sibling:
            cp.wait_recv()
        for cp in first + passed:
            cp.wait_send()
        mine.wait()


def _allgather8(block, name):
    ex = _AllGather8(block)

    def body(x_ref, out_ref, *sems):
        ex.start((x_ref,), (out_ref,), sems)
        ex.wait((x_ref,), (out_ref,), sems)

    return pl.pallas_call(
        body, name=name, out_shape=ex.out_shapes[0], in_specs=[pl.BlockSpec(memory_space=pltpu.VMEM)],
        out_specs=pl.BlockSpec(memory_space=pltpu.VMEM), scratch_shapes=ex.scratch, compiler_params=_params())(block)


class _Exchange:
    def __init__(self, items, out_shapes):
        self.items, self.out_shapes = list(items), tuple(out_shapes)
        self.arrays = [it[0] for it in self.items]
        n = len(self.items)
        self.n_in, self.n_out = n, len(self.out_shapes)
        self.scratch = [pltpu.SemaphoreType.DMA((n * N_CHIPS,)), pltpu.SemaphoreType.DMA((n * N_CHIPS,)),
                        pltpu.SemaphoreType.DMA((n,))]

    def _copies(self, ins, outs, sems, m):
        send_sems, recv_sems, local_sems = sems
        c = lax.axis_index("c")
        others = [j for j in range(N_CHIPS) if j != m]

        def remote(a, src, dst, to, from_):
            return pltpu.make_async_remote_copy(
                src_ref=src, dst_ref=dst, send_sem=send_sems.at[a * N_CHIPS + to],
                recv_sem=recv_sems.at[a * N_CHIPS + from_], device_id=(to // 2, to % 2, c), device_id_type=MESH)

        local, sends, recvs = [], [], []
        for a, (_, oi, src_of, dst_of) in enumerate(self.items):
            local.append(pltpu.make_async_copy(src_of(ins[a], m), dst_of(outs[oi], m), local_sems.at[a]))
            for j in others:
                sends.append(remote(a, src_of(ins[a], j), dst_of(outs[oi], m), j, m))
                recvs.append(remote(a, src_of(ins[a], m), dst_of(outs[oi], j), j, j))
        return local, sends, recvs

    def _on_my_chip(self, fn):
        chip = 2 * lax.axis_index("x") + lax.axis_index("y")
        for m in range(N_CHIPS):
            pl.when(chip == m)(functools.partial(fn, m))

    def start(self, ins, outs, sems):
        def go(m):
            local, sends, _ = self._copies(ins, outs, sems, m)
            for cp in local + sends:
                cp.start()
        self._on_my_chip(go)

    def wait(self, ins, outs, sems):
        def go(m):
            local, sends, recvs = self._copies(ins, outs, sems, m)
            for cp in recvs:
                cp.wait_recv()
            for cp in sends:
                cp.wait_send()
            for cp in local:
                cp.wait()
        self._on_my_chip(go)


def _half_rows(ref, cc):
    h = ref.shape[-2] // 2
    return ref.at[(slice(None),) * (len(ref.shape) - 2) + (pl.ds(cc * h, h), slice(None))]


class _Gather:
    def __init__(self, items, out_shapes):
        self.items, self.out_shapes = list(items), tuple(out_shapes)
        self.arrays = [it[0] for it in self.items]
        n = len(self.items)
        self.n_in, self.n_out = n, len(self.out_shapes)
        self.scratch = [pltpu.SemaphoreType.DMA((n * N_CHIPS,)) for _ in range(4)] + [pltpu.SemaphoreType.DMA((n,))]

    def _copies(self, ins, outs, sems, m, cc):
        ici_send, ici_recv, d2d_send, d2d_recv, local_sems = sems
        others = [j for j in range(N_CHIPS) if j != m]
        local, sends, arrivals, passed_on, from_sibling = [], [], [], [], []
        for a, (_, oi, src_of, dst_of) in enumerate(self.items):
            src, out = src_of(ins[a]), outs[oi]
            local.append(pltpu.make_async_copy(src, dst_of(out, m), local_sems.at[a]))
            for j in others:
                k = a * N_CHIPS + j
                mine_there = _half_rows(dst_of(out, m), cc)
                theirs_here = _half_rows(dst_of(out, j), cc)
                sends.append(pltpu.make_async_remote_copy(
                    src_ref=_half_rows(src, cc), dst_ref=mine_there, send_sem=ici_send.at[k],
                    recv_sem=ici_recv.at[a * N_CHIPS + m], device_id=(j // 2, j % 2, cc), device_id_type=MESH))
                arrivals.append(pltpu.make_async_remote_copy(
                    src_ref=_half_rows(src, cc), dst_ref=theirs_here, send_sem=ici_send.at[k], recv_sem=ici_recv.at[k],
                    device_id=(j // 2, j % 2, cc), device_id_type=MESH))
                passed_on.append(pltpu.make_async_remote_copy(
                    src_ref=theirs_here, dst_ref=theirs_here, send_sem=d2d_send.at[k], recv_sem=d2d_recv.at[k],
                    device_id=(m // 2, m % 2, 1 - cc), device_id_type=MESH))
                other_half = _half_rows(dst_of(out, j), 1 - cc)
                from_sibling.append(pltpu.make_async_remote_copy(
                    src_ref=other_half, dst_ref=other_half, send_sem=d2d_send.at[k], recv_sem=d2d_recv.at[k],
                    device_id=(m // 2, m % 2, 1 - cc), device_id_type=MESH))
        return local, sends, arrivals, passed_on, from_sibling

    def _on_my_core(self, fn):
        chip = 2 * lax.axis_index("x") + lax.axis_index("y")
        c = lax.axis_index("c")
        for m in range(N_CHIPS):
            for cc in range(2):
                pl.when((chip == m) & (c == cc))(functools.partial(fn, m, cc))

    def start(self, ins, outs, sems):
        def go(m, cc):
            local, sends, _, _, _ = self._copies(ins, outs, sems, m, cc)
            for cp in local + sends:
                cp.start()
        self._on_my_core(go)

    def wait(self, ins, outs, sems):
        def go(m, cc):
            local, sends, arrivals, passed_on, from_sibling = self._copies(ins, outs, sems, m, cc)
            for arrived, onward in zip(arrivals, passed_on):
                arrived.wait_recv()
                onward.start()
            for cp in from_sibling:
                cp.wait_recv()
            for cp in sends + passed_on:
                cp.wait_send()
            for cp in local:
                cp.wait()
        self._on_my_core(go)


def _run_exchange(ex, name):
    def body(*refs):
        ins, outs, sems = refs[:ex.n_in], refs[ex.n_in:ex.n_in + ex.n_out], refs[ex.n_in + ex.n_out:]
        ex.start(ins, outs, sems)
        ex.wait(ins, outs, sems)

    return pl.pallas_call(
        body, name=name, out_shape=ex.out_shapes, in_specs=[ANY] * ex.n_in, out_specs=(ANY,) * ex.n_out,
        scratch_shapes=ex.scratch, compiler_params=_params())(*ex.arrays)


def _sibling_swap(arrays, name, also):
    n = len(arrays)

    def body(*refs):
        ins, refs = refs[:n], refs[n:]
        x_ins, refs = refs[:also.n_in], refs[also.n_in:]
        outs, refs = refs[:n], refs[n:]
        x_outs, refs = refs[:also.n_out], refs[also.n_out:]
        send_sems, recv_sems, x_sems = refs[0], refs[1], refs[2:]
        peer = (lax.axis_index("x"), lax.axis_index("y"), 1 - lax.axis_index("c"))
        cps = [pltpu.make_async_remote_copy(src_ref=ins[a], dst_ref=outs[a], send_sem=send_sems.at[a],
                                            recv_sem=recv_sems.at[a], device_id=peer, device_id_type=MESH)
               for a in range(n)]
        also.start(x_ins, x_outs, x_sems)
        for cp in cps:
            cp.start()
        also.wait(x_ins, x_outs, x_sems)
        for cp in cps:
            cp.wait()

    return pl.pallas_call(
        body, name=name, out_shape=tuple(SDS(a.shape, a.dtype) for a in arrays) + also.out_shapes,
        in_specs=[ANY] * (n + also.n_in), out_specs=(ANY,) * (n + also.n_out),
        scratch_shapes=[pltpu.SemaphoreType.DMA((n,)), pltpu.SemaphoreType.DMA((n,))] + also.scratch,
        compiler_params=_params())(*arrays, *also.arrays)


def _block_diag(w):
    h, n, m = w.shape
    eye = jnp.eye(h, dtype=w.dtype)
    return (w[:, :, None, :] * eye[:, None, :, None]).reshape(h * n, h * m)


def _diag_blocks(d, h, col0=0, ncols=None, stacked=1):
    ncols = d.shape[1] - col0 if ncols is None else ncols
    n, m = d.shape[0] // (h * stacked), ncols // h
    lanes = 128
    assert m <= lanes and lanes % m == 0 and col0 % lanes == 0

    def body(d_ref, o_ref):
        for gi in range(h * stacked):
            c = col0 + (gi % h) * m
            chunk = d_ref[gi * n:(gi + 1) * n, c // lanes * lanes:c // lanes * lanes + lanes]
            o_ref[gi * n:(gi + 1) * n, :] = chunk[:, c % lanes:c % lanes + m]

    out = pl.pallas_call(body, name="diag_blocks", out_shape=SDS((stacked * h * n, m), d.dtype),
                         compiler_params=_params())(d)
    return out.reshape(stacked * h, n, m)


S5_CHUNKS = 4
S5_PER = S5_GROUPS // S5_CHUNKS
CH_W = S5_PER * S5_CH
ST_W = S5_PER * S5_STATE


def _bd_stack(mats):
    _, _, n, m = mats.shape
    eye = jnp.eye(S5_PER, dtype=mats.dtype)
    t = mats.reshape(2, S5_CHUNKS, S5_PER, n, m)
    bd = t[:, :, :, :, None, :] * eye[None, None, :, None, :, None]
    return bd.reshape(2 * S5_CHUNKS, S5_PER * n, S5_PER * m).astype(MXU_DTYPE)


def _chunks_chunked(src_ref, buf):
    pt = src_ref.shape[0]
    out = []
    for q in range(S5_CHUNKS):
        buf[q] = src_ref[:, q * CH_W:(q + 1) * CH_W]
        out.append(_load_chunked(buf.at[q], 0, pt).astype(MXU_DTYPE))
    return out


def _expand_into(dst_ref, chunks, w_ref):
    for b in range(2 * S5_CHUNKS):
        dst_ref[:, b * ST_W:(b + 1) * ST_W] = jnp.dot(chunks[b % S5_CHUNKS], w_ref[b], preferred_element_type=F32)


def _reduce_from(src_ref, w_ref, buf, dst_ref):
    pt = src_ref.shape[0]
    for q in range(S5_CHUNKS):
        y = jnp.dot(src_ref[:, q * ST_W:(q + 1) * ST_W].astype(MXU_DTYPE), w_ref[q], preferred_element_type=F32)
        p = S5_CHUNKS + q
        y = y + jnp.dot(src_ref[:, p * ST_W:(p + 1) * ST_W].astype(MXU_DTYPE), w_ref[p], preferred_element_type=F32)
        _store_natural(buf.at[q], 0, pt, y)
        dst_ref[:, q * CH_W:(q + 1) * CH_W] = buf[q]


def _s5_fwd(proj, w_bu, w_cx, a_row, d_skip, w_glu, b_glu):
    s = proj.shape[0]
    pt = _scan_tile(s)
    ch2 = 2 * S5_N

    def body(u_ref, dg_ref, wb_ref, wc_ref, a_ref, dk_ref, wg_ref, bg_ref, x_ref, y_ref, o_ref, carry, pw, buf):
        _expand_into(x_ref, _chunks_chunked(u_ref, buf), wb_ref)
        _scan_tile_in_place(a_ref, x_ref, carry, pw, reverse=False)
        _reduce_from(x_ref, wc_ref, buf, y_ref)
        g = jax.nn.gelu(y_ref[...] + dk_ref[...] * u_ref[...])
        t = jnp.dot(g.astype(MXU_DTYPE), wg_ref[...], preferred_element_type=F32) + bg_ref[...]
        o_ref[...] = (g * jax.nn.sigmoid(t) * _silu(dg_ref[...])).astype(MXU_DTYPE)

    return pl.pallas_call(
        body, name="s5_fwd", out_shape=(SDS((s, ch2), F32), SDS((s, BR), F32), SDS((s, BR), MXU_DTYPE)),
        grid=(s // pt,),
        in_specs=[_rows(pt, BR, CB_DU), _rows(pt, BR, CB_DG), _const(w_bu.shape), _const(w_cx.shape),
                  _const((1, ch2)), _const((1, BR)), _const((BR, BR)), _const((1, BR))],
        out_specs=(_rows(pt, ch2), _rows(pt, BR), _rows(pt, BR)),
        scratch_shapes=[pltpu.VMEM((1, ch2), F32), pltpu.VMEM((pt // 8, ch2), F32),
                        pltpu.VMEM((S5_CHUNKS, pt, CH_W), F32)],
        compiler_params=_params(1))(proj, proj, w_bu, w_cx, a_row, d_skip, w_glu, b_glu)


def _s5_core_bwd(dyl, proj, x, w_dx, w_du, a_row):
    s = proj.shape[0]
    pt = _scan_tile(s)
    nt = s // pt
    ch2 = 2 * S5_N
    ch = S5_N

    def body(dy_ref, u_ref, x_ref, xp_ref, wx_ref, wu_ref, a_ref, du_ref, da_ref, dwb_ref, dwc_ref,
             l_ref, carry, pw, buf, buf2):
        i = pl.program_id(0)
        _init_acc(da_ref, dwb_ref, dwc_ref)
        dy_c = _chunks_chunked(dy_ref, buf)
        u_c = _chunks_chunked(u_ref, buf2)
        _expand_into(l_ref, dy_c, wx_ref)
        _scan_tile_in_place(a_ref, l_ref, carry, pw, reverse=True)
        has_prev = (i < nt - 1).astype(F32)
        row = lax.broadcasted_iota(jnp.int32, (8, ch2), 0)
        first = jnp.where(row == 0, pltpu.roll(xp_ref[...], 1, 0) * has_prev, pltpu.roll(x_ref[pt - 8:pt, :], 1, 0))
        xprev = jnp.concatenate([first, x_ref[0:pt - 8, :]], axis=0)
        lr, li, xr, xi = l_ref[:, 0:ch], l_ref[:, ch:ch2], xprev[:, 0:ch], xprev[:, ch:ch2]
        da_ref[:, 0:ch] += _colsum(lr * xr + li * xi)
        da_ref[:, ch:ch2] += _colsum(li * xr - lr * xi)
        _reduce_from(l_ref, wu_ref, buf, du_ref)
        tn = (((0,), (0,)), ((), ()))
        for b in range(2 * S5_CHUNKS):
            cols, rows = slice(b * ST_W, (b + 1) * ST_W), slice(b * CH_W, (b + 1) * CH_W)
            dwb_ref[rows, :] += lax.dot_general(u_c[b % S5_CHUNKS], l_ref[:, cols].astype(MXU_DTYPE), tn,
                                                preferred_element_type=F32)
            dwc_ref[rows, :] += lax.dot_general(dy_c[b % S5_CHUNKS], x_ref[:, cols].astype(MXU_DTYPE), tn,
                                                preferred_element_type=F32)

    rev = lambda w, cb=0: pl.BlockSpec((pt, w), lambda i: (nt - 1 - i, cb))
    halo = pl.BlockSpec((8, ch2), lambda i: (jnp.maximum((nt - 1 - i) * (pt // 8) - 1, 0), 0))
    wshape = SDS((2 * S5_CHUNKS * CH_W, ST_W), F32)
    return pl.pallas_call(
        body, name="s5_core_bwd", out_shape=(SDS((s, BR), F32), SDS((1, ch2), F32), wshape, wshape), grid=(nt,),
        in_specs=[rev(BR, 0), rev(BR, CB_DU), rev(ch2), halo, _const(w_dx.shape), _const(w_du.shape),
                  _const((1, ch2))],
        out_specs=(rev(BR), _const((1, ch2)), _const(wshape.shape), _const(wshape.shape)),
        scratch_shapes=[pltpu.VMEM((pt, ch2), F32), pltpu.VMEM((1, ch2), F32), pltpu.VMEM((pt // 8, ch2), F32),
                        pltpu.VMEM((S5_CHUNKS, pt, CH_W), F32), pltpu.VMEM((S5_CHUNKS, pt, CH_W), F32)],
        compiler_params=_params(1))(dyl, proj, x, x, w_dx, w_du, a_row)


def _tiles(s):
    return dict(tb=min(512, s), tln=min(256, s))


def _layer_weights(p, l):
    pad8 = lambda w: jnp.pad(w, ((0, 8 - w.shape[0]), (0, 0)))
    return dict(
        conv_a=pad8(p["conv_a"][l]), conv_c=pad8(p["conv_c"][l]), conv_c_b=p["conv_c_b"][l][None],
        w_cat=jnp.concatenate([_block_diag(p["lru_wa"][l]), _block_diag(p["lru_wx"][l])], axis=1).astype(MXU_DTYPE),
        b_cat=jnp.concatenate([p["lru_ba"][l], p["lru_bx"][l]])[None], lam=p["lru_lambda"][l][None],
        lam_re=p["s5_lam_re"][l], lam_im=p["s5_lam_im"][l], log_dt=p["s5_log_dt"][l][:, None],
        b_re=p["s5_b_re"][l].reshape(S5_N, S5_CH), b_im=p["s5_b_im"][l].reshape(S5_N, S5_CH),
        c_re=p["s5_c_re"][l], c_im=p["s5_c_im"][l], d_skip=p["s5_d"][l][None], b_glu=p["s5_b_glu"][l][None],
        ln_g=p["ln_g"][l][None], ln_b=p["ln_b"][l][None])


def _s5_matrices(lw):
    ab_re, ab_im, f_re, f_im = _s5_disc_fwd(lw["lam_re"], lw["lam_im"], lw["log_dt"])
    f_re, f_im = f_re.reshape(S5_N, 1), f_im.reshape(S5_N, 1)
    bb_re, bb_im = _s5_bbar_fwd(f_re, f_im, lw["b_re"], lw["b_im"])
    bb = jnp.stack([bb_re, bb_im]).reshape(2, S5_GROUPS, S5_STATE, S5_CH)
    cc = jnp.stack([lw["c_re"], -lw["c_im"]])
    a_row = jnp.concatenate([ab_re.reshape(1, S5_N), ab_im.reshape(1, S5_N)], axis=1)
    return dict(f_re=f_re, f_im=f_im, a_row=a_row, w_bu=_bd_stack(jnp.swapaxes(bb, 2, 3)), w_du=_bd_stack(bb),
                w_cx=_bd_stack(jnp.swapaxes(cc, 2, 3)), w_dx=_bd_stack(cc))


def _mm_hooked(hook, *args, **kw):
    if hook is None:
        return _mm(*args, **kw)
    out = _mm(*args, carry=hook[0], **kw)
    hook[1](out[1:])
    return out[0]


def _layer_fwd(x, h, ada, w_in, get_rest, lw, s5m, bias_tabs, hooks=None, target=None, next_ada=None):
    s = x.shape[0]
    t = _tiles(s)
    tb = t["tb"]
    shift, scale, gate = ada
    hooks = hooks or {}
    if h is None:
        h = _modulate(x, scale, shift, tb)
    proj = _mm_hooked(hooks.get("in_proj"), h, w_in, name="in_proj", tm=1024, tn=1536, tk=D_MODEL)
    w_out, w_glu = get_rest()
    y_a = _branch_a_fwd(proj, lw["conv_a"], tb)
    os_, lses = [], []
    for g, (_, dil) in enumerate(DILATIONS):
        o, lse = _attn_fwd(proj, bias_tabs[g], dil)
        os_.append(o)
        lses.append(lse)
    y_b = _attn_combine(os_, lses, proj, tb)
    lru_a, lru_b = _lru_gates_fwd(proj, lw["conv_c"], lw["conv_c_b"], lw["w_cat"], lw["b_cat"], lw["lam"], tb)
    lru_h, y_c = _lru_scan_fwd(lru_a, lru_b, proj, tb)
    s5_x, ylin, y_d = _s5_fwd(proj, s5m["w_bu"], s5m["w_cx"], s5m["a_row"], lw["d_skip"], w_glu, lw["b_glu"])
    ycat = jnp.concatenate([y_a, y_b, y_c, y_d], axis=1)
    saved = dict(x=x, h=h, proj=proj, os=os_, lses=lses, lru_a=lru_a, lru_h=lru_h, s5_x=s5_x, ylin=ylin, ycat=ycat)
    if target is not None:
        loss, *saved["head"] = _out_ln_loss(ycat, w_out, x, gate, lw["ln_g"], lw["ln_b"], target, t["tln"])
        return loss, None, saved
    x_next, saved["xhat"], saved["y"], saved["rstd"], h_next = _out_ln(
        ycat, w_out, x, gate, lw["ln_g"], lw["ln_b"], next_ada[1], next_ada[0], t["tln"])
    return x_next, h_next, saved


def _layer_bwd(dxn, sv, ada, w_in, w_out, w_glu, lw, s5m, bias_tabs, head_ones, hooks=None):
    proj = sv["proj"]
    s = proj.shape[0]
    t = _tiles(s)
    tb = t["tb"]
    shift, scale, gate = ada
    g = {}
    hook = lambda name: hooks[name](g) if hooks and name in hooks else None
    if "head" in sv:
        dyb, dxa, g["ln_g"], g["ln_b"], dgate = sv["head"]
        dycat = _mm(dyb, w_out, name="dycat", tb=True, tm=1024, tn=1024, tk=D_MODEL)
    else:
        dyb, dxa, g["ln_g"], g["ln_b"], dgate, dycat = _ln_bwd(dxn, sv["xhat"], sv["y"], sv["rstd"], lw["ln_g"], gate,
                                                               w_out, t["tln"])
    g["w_out"] = _mm_hooked(hook("dw_out"), sv["ycat"], dyb, name="dw_out", ta=True, out_dtype=WIRE_DTYPE,
                            tm=1024, tn=1024, tk=2048)
    da, dconv_a = _branch_a_bwd(dycat, proj, lw["conv_a"], tb)
    g["conv_a"] = dconv_a[0:3]
    pre = _attn_bwd_pre(dycat, sv["os"], sv["lses"], proj, head_ones, tb)
    dbg, dos, dms = pre[0], pre[1:4], pre[4:7]
    parts, dbias = [], []
    for gi, (_, dil) in enumerate(DILATIONS):
        hk = hook(f"attn_bwd_d{dil}")
        last = gi == len(DILATIONS) - 1
        dq, dk, dv, dbi, *got = _attn_bwd(proj, dos[gi], sv["lses"][gi], dms[gi], bias_tabs[gi], dil,
                                          carry=hk and hk[0], add=tuple(parts) if last else ())
        if hk:
            hk[1](got)
        parts.append((dq, dk, dv))
        dbias.append(dbi)
    dqkv = parts[-1]
    lmb, dcg = _lru_scan_bwd(sv["lru_a"], dycat, sv["lru_h"], proj, tb)
    dxc, dpre, xcb, dbcat, dlam = _lru_gates_bwd(proj, lmb, sv["lru_h"], lw["conv_c"], lw["conv_c_b"], lw["w_cat"],
                                                  lw["b_cat"], lw["lam"], tb)
    dwcat = _mm(xcb, dpre, name="dw_lru", ta=True, tn=1024)
    g["lru_wa"] = _diag_blocks(dwcat, LRU_HEADS, 0, BR)
    g["lru_wx"] = _diag_blocks(dwcat, LRU_HEADS, BR, BR)
    g["lru_ba"], g["lru_bx"], g["lru_lambda"] = dbcat[0, 0:BR], dbcat[0, BR:2 * BR], dlam[0]
    dcx, dconv_c, dccb = _conv_c_bwd(dxc, proj, lw["conv_c"], tb)
    g["conv_c"], g["conv_c_b"] = dconv_c[0:4], dccb[0]
    dyl, dus, ddg, gb, dtb, ddk, dbglu = _s5_tail_bwd(dycat, sv["ylin"], proj, lw["d_skip"], w_glu, lw["b_glu"], tb)
    g["s5_d"], g["s5_b_glu"] = ddk[0], dbglu[0]
    g["s5_w_glu"] = _mm(gb, dtb, name="dw_glu", ta=True, out_dtype=WIRE_DTYPE)
    du, dab, dwb8, dwc8 = _s5_core_bwd(dyl, proj, sv["s5_x"], s5m["w_dx"], s5m["w_du"], s5m["a_row"])
    per_group = lambda d8: _diag_blocks(d8, S5_PER, stacked=2 * S5_CHUNKS).reshape(2, S5_GROUPS, S5_CH, S5_STATE)
    dbb, dcc = per_group(dwb8), per_group(dwc8)
    from_bd = lambda half: jnp.swapaxes(dbb[half], 1, 2).reshape(S5_N, S5_CH)
    df_re, df_im, db_re, db_im = _s5_bbar_bwd(s5m["f_re"], s5m["f_im"], lw["b_re"], lw["b_im"],
                                              from_bd(0), from_bd(1))
    shp = (S5_GROUPS, S5_STATE)
    g["s5_lam_re"], g["s5_lam_im"], dlog_dt = _s5_disc_bwd(
        lw["lam_re"], lw["lam_im"], lw["log_dt"],
        (dab[:, 0:S5_N].reshape(shp), dab[:, S5_N:].reshape(shp), df_re.reshape(shp), df_im.reshape(shp)))
    g["s5_log_dt"] = dlog_dt[:, 0]
    g["s5_b_re"] = db_re.reshape(S5_GROUPS, S5_STATE, S5_CH)
    g["s5_b_im"] = db_im.reshape(S5_GROUPS, S5_STATE, S5_CH)
    g["s5_c_re"], g["s5_c_im"] = dcc[0], -dcc[1]
    dproj = _assemble_dproj(da, dqkv, dbg, dcx, dcg, du, dus, ddg, tb)
    g["w_in"] = _mm_hooked(hook("dw_in"), sv["h"], dproj, name="dw_in", ta=True, out_dtype=WIRE_DTYPE,
                           tm=1024, tn=1536, tk=2048)
    hk = hook("dh")
    dx, dshift, dscale, *got = _dh_mod_bwd(dproj, w_in, dxa, sv["x"], scale, carry=hk and hk[0])
    if hk:
        hk[1](got)
    g["ada"] = jnp.concatenate([dshift[0], dscale[0], dgate[0]])
    return dx, g, dbias


SMALL = ("rel_bias", "conv_a", "conv_c", "conv_c_b", "lru_wa", "lru_ba", "lru_wx", "lru_bx", "lru_lambda",
         "s5_lam_re", "s5_lam_im", "s5_log_dt", "s5_b_re", "s5_b_im", "s5_c_re", "s5_c_im", "s5_d", "s5_b_glu",
         "ln_g", "ln_b")
PER_LAYER_SMALL = SMALL[1:]


def _local_step(x, target, ada_rows, w_in, w_out, w_glu, p, comm=None):
    if comm is None:
        get_w_in = lambda l: w_in[l]
        get_rest = lambda l: (w_out[l], w_glu[l])
        fwd_hooks = bwd_hooks = lambda *_: None
    else:
        get_w_in, get_rest, fwd_hooks, bwd_hooks = comm.w_in, comm.rest, comm.fwd_hooks, comm.bwd_hooks
    s = x.shape[0]
    buckets = _bucket_maps()
    bias_tabs = _bias_tables(p["rel_bias"], buckets)
    head_ones = _block_diag(jnp.ones((ATT_HEADS, HEAD_DIM, HEAD_DIM), MXU_DTYPE))
    lws = [_layer_weights(p, l) for l in range(DEPTH)]
    s5ms = [_s5_matrices(lw) for lw in lws]
    adas = [tuple(ada_rows[l, k * D_MODEL:(k + 1) * D_MODEL][None] for k in range(3)) for l in range(DEPTH)]
    saved, h = [], None
    for l in range(DEPTH):
        last = l == DEPTH - 1
        x, h, sv = _layer_fwd(x, h, adas[l], get_w_in(l), functools.partial(get_rest, l), lws[l], s5ms[l], bias_tabs,
                              fwd_hooks(l), target if last else None, None if last else adas[l + 1])
        saved.append(sv)
    loss, dx = x, None
    grads = [None] * DEPTH
    dbias_sum = []
    for l in reversed(range(DEPTH)):
        dx, grads[l], dbias = _layer_bwd(dx, saved[l], adas[l], get_w_in(l), *get_rest(l), lws[l], s5ms[l],
                                         bias_tabs, head_ones, bwd_hooks(l, grads))
        dbias_sum.append(jnp.stack(dbias))
    drel = _rel_bias_grad(jnp.stack(dbias_sum), buckets)[:, 0:ATT_HEADS]
    small = {n: jnp.stack([grads[l][n] for l in range(DEPTH)]) for n in PER_LAYER_SMALL + ("ada",)}
    small["rel_bias"] = drel
    big = {n: [grads[l][n] for l in range(DEPTH)] for n in ("w_in", "w_out", "s5_w_glu")}
    return loss, dx, big, small


PACK_ROWS = 256


def _pack(parts):
    flat = jnp.concatenate([t.reshape(-1).astype(F32) for t in parts])
    n = flat.shape[0]
    rows = -(-n // (PACK_ROWS * 128)) * PACK_ROWS
    return jnp.pad(flat, (0, rows * 128 - n)).reshape(rows, 128)


def _unpack(packed, shapes):
    flat = packed.reshape(packed.shape[:-2] + (-1,))
    out, off = [], 0
    for shp in shapes:
        size = math.prod(shp)
        out.append(flat[..., off:off + size].reshape(flat.shape[:-1] + tuple(shp)))
        off += size
    return out


def _take_cols(t, chip, width):
    return lax.dynamic_slice_in_dim(t, chip * width, width, axis=t.ndim - 1)


class _Comm:
    IN_W, OUT_R, GLU_R = N_IN // N_CHIPS, D_MODEL // N_CHIPS, BR // N_CHIPS

    def __init__(self, w_in_b, w_out_b, w_glu_b):
        assert DEPTH == 2
        self.shards = (w_in_b, w_out_b, w_glu_b)
        in_w = self.IN_W
        self.w_in_full = {0: _run_exchange(_Gather(
            [(w_in_b, 0, lambda ref: ref.at[0], lambda ref, j: ref.at[:, pl.ds(j * in_w, in_w)])],
            [SDS((D_MODEL, N_IN), WIRE_DTYPE)]), "gather_w_in0")[0]}
        self.w_out_full = self.w_glu_full = None
        self.recv = {}

    def w_in(self, l):
        return self.w_in_full[l]

    def rest(self, l):
        return self.w_out_full[l], self.w_glu_full[l]

    def fwd_hooks(self, l):
        if l != 0:
            return None
        w_in_b, w_out_b, w_glu_b = self.shards
        in_w, out_r, glu_r = self.IN_W, self.OUT_R, self.GLU_R
        whole = lambda ref: ref
        items = [(w_out_b, 0, whole, lambda ref, j: ref.at[:, pl.ds(j * out_r, out_r), :]),
                 (w_glu_b, 1, whole, lambda ref, j: ref.at[:, pl.ds(j * glu_r, glu_r), :]),
                 (w_in_b, 2, lambda ref: ref.at[1], lambda ref, j: ref.at[:, pl.ds(j * in_w, in_w)])]
        shapes = [SDS((DEPTH, D_MODEL, D_MODEL), WIRE_DTYPE), SDS((DEPTH, BR, BR), WIRE_DTYPE),
                  SDS((D_MODEL, N_IN), WIRE_DTYPE)]

        def done(got):
            self.w_out_full, self.w_glu_full, self.w_in_full[1] = got

        return {"in_proj": (_Gather(items, shapes), done)}

    W_IN_ROWS = ((0, 1024), (1024, 512), (1536, 512))

    def _scatter(self, parts):
        in_w, out_r, glu_r = self.IN_W, self.OUT_R, self.GLU_R
        items, shapes, keys = [], [], []
        for oi, (name, l, arr, *rows) in enumerate(parts):
            if name == "w_in":
                r0, nr = rows[0] if rows else (0, D_MODEL)
                cut = functools.partial(lambda ref, j, r0, nr: ref.at[pl.ds(r0, nr), pl.ds(j * in_w, in_w)], r0=r0, nr=nr)
                shard = (nr, in_w)
            elif name == "w_out":
                cut, shard = (lambda ref, j: ref.at[pl.ds(j * out_r, out_r), :]), (out_r, D_MODEL)
            else:
                cut, shard = (lambda ref, j: ref.at[pl.ds(j * glu_r, glu_r), :]), (glu_r, BR)
            items.append((arr, oi, cut, lambda ref, j: ref.at[j]))
            shapes.append(SDS((N_CHIPS,) + shard, WIRE_DTYPE))
            keys.append((name, l) + ((rows[0][0],) if rows else ()))

        def done(got):
            self.recv.update(zip(keys, got))

        return _Exchange(items, shapes), done

    def received(self, name):
        return [self.recv[k] for k in sorted(k for k in self.recv if k[0] == name)]

    def bwd_hooks(self, l, grads):
        if l != 0:
            return None
        g1 = grads[1]
        w_in_part = lambda k: (lambda g: self._scatter([("w_in", 1, g1["w_in"], self.W_IN_ROWS[k])]))
        return {"dw_out": lambda g: self._scatter([("w_out", 1, g1["w_out"]), ("s5_w_glu", 1, g1["s5_w_glu"])]),
                "attn_bwd_d16": w_in_part(0), "attn_bwd_d4": w_in_part(1), "attn_bwd_d1": w_in_part(2),
                "dw_in": lambda g: self._scatter([("w_out", 0, g["w_out"]), ("s5_w_glu", 0, g["s5_w_glu"])]),
                "dh": lambda g: self._scatter([("w_in", 0, g["w_in"])])}


def kernel(x, c, rel_bias, w_ada, b_ada, w_in, conv_a, conv_c, conv_c_b, lru_wa, lru_ba, lru_wx, lru_bx, lru_lambda, s5_lam_re, s5_lam_im, s5_log_dt, s5_b_re, s5_b_im, s5_c_re, s5_c_im, s5_d, s5_w_glu, s5_b_glu, w_out, ln_g, ln_b, loss_target, m_rel_bias, m_w_ada, m_b_ada, m_w_in, m_conv_a, m_conv_c, m_conv_c_b, m_lru_wa, m_lru_ba, m_lru_wx, m_lru_bx, m_lru_lambda, m_s5_lam_re, m_s5_lam_im, m_s5_log_dt, m_s5_b_re, m_s5_b_im, m_s5_c_re, m_s5_c_im, m_s5_d, m_s5_w_glu, m_s5_b_glu, m_w_out, m_ln_g, m_ln_b, v_rel_bias, v_w_ada, v_b_ada, v_w_in, v_conv_a, v_conv_c, v_conv_c_b, v_lru_wa, v_lru_ba, v_lru_wx, v_lru_bx, v_lru_lambda, v_s5_lam_re, v_s5_lam_im, v_s5_log_dt, v_s5_b_re, v_s5_b_im, v_s5_c_re, v_s5_c_im, v_s5_d, v_s5_w_glu, v_s5_b_glu, v_w_out, v_ln_g, v_ln_b):
    args = dict(locals())
    names = ("rel_bias", "w_ada", "b_ada", "w_in", "conv_a", "conv_c", "conv_c_b", "lru_wa", "lru_ba", "lru_wx",
             "lru_bx", "lru_lambda", "s5_lam_re", "s5_lam_im", "s5_log_dt", "s5_b_re", "s5_b_im", "s5_c_re", "s5_c_im",
             "s5_d", "s5_w_glu", "s5_b_glu", "w_out", "ln_g", "ln_b")
    w = {n: args[n] for n in names}
    mom = {n: args["m_" + n] for n in names}
    var = {n: args["v_" + n] for n in names}
    chip = 2 * lax.axis_index("x") + lax.axis_index("y")
    me = 2 * chip + lax.axis_index("c")
    ada_w = 3 * D_MODEL // N_CHIPS
    conv_w = BR // N_CHIPS

    comm = _Comm(w["w_in"].astype(WIRE_DTYPE), w["w_out"].astype(WIRE_DTYPE), w["s5_w_glu"].astype(WIRE_DTYPE))

    taps = jnp.concatenate([w["conv_a"].reshape(DEPTH * 3, conv_w), w["conv_c"].reshape(DEPTH * 4, conv_w)])
    first = jnp.concatenate([c, jnp.pad(taps, ((0, 1), (0, D_MODEL - conv_w)))])
    got = _allgather8(first, "gather_c_taps").reshape(N_CHIPS, 2, 16, D_MODEL)
    c_all = got[:, :, 0].reshape(N_DEV, D_MODEL)
    taps_all = jnp.transpose(got[:, 0, 1:1 + DEPTH * 7, 0:conv_w], (1, 0, 2)).reshape(DEPTH * 7, BR)
    conv_a_f = taps_all[0:DEPTH * 3].reshape(DEPTH, 3, BR)
    conv_c_f = taps_all[DEPTH * 3:].reshape(DEPTH, 4, BR)

    cond_all = _silu_rows(c_all)
    ada_part = jnp.stack([_mm(cond_all, w["w_ada"][l], name="ada_fwd", tk=D_MODEL, tn=512,
                              bias=_take_cols(w["b_ada"][l][None], chip, ada_w)) for l in range(DEPTH)])
    ada_all = _allgather8(ada_part.reshape(DEPTH * N_DEV, ada_w), "gather_ada")
    ada_all = ada_all.reshape(N_CHIPS, 2, DEPTH, N_DEV, ada_w)[:, 0]
    ada_rows = lax.dynamic_index_in_dim(ada_all, me, axis=2, keepdims=False)
    ada_rows = jnp.transpose(ada_rows, (1, 0, 2)).reshape(DEPTH, 3 * D_MODEL)

    p = dict(w)
    p["conv_a"], p["conv_c"] = conv_a_f, conv_c_f
    loss, dx, _, small = _local_step(x[0], loss_target[0], ada_rows, None, None, None, p, comm)

    sums = [_sum_leading(comm.received(name), 256, "sum_chips") for name in ("w_in", "w_out", "s5_w_glu")]
    small_names = SMALL + ("ada",)
    small["loss"] = loss
    order = small_names + ("loss",)
    shapes = [small[n].shape for n in order]
    *others, gathered = _sibling_swap(sums, "swap_cores", _AllGather8(_pack([small[n] for n in order])))
    out = {}
    for name, mine, other in zip(("w_in", "w_out", "s5_w_glu"), sums, others):
        shp = w[name].shape
        flat = lambda t: t.reshape(-1, shp[-1])
        res = _adamw(flat(w[name]), [mine, other], flat(mom[name]), flat(var[name]), 128, "adamw_big")
        out[name] = [t.reshape(shp) for t in res]
    gathered = gathered.reshape(N_DEV, -1, 128)
    total = dict(zip(order, _unpack(_sum_leading([gathered], PACK_ROWS, "sum_devices"), shapes)))
    d_ada_all = _unpack(gathered, shapes)[order.index("ada")]
    g_small = {n: total[n] for n in SMALL}
    g_small["conv_a"] = _take_cols(total["conv_a"], chip, conv_w)
    g_small["conv_c"] = _take_cols(total["conv_c"], chip, conv_w)
    g_small["b_ada"] = total["ada"]
    g_w_ada = jnp.stack([_mm(cond_all, _take_cols(d_ada_all[:, l], chip, ada_w), name="dw_ada", ta=True, tn=ada_w)
                         for l in range(DEPTH)])
    upd_names = SMALL + ("b_ada",)
    upd_shapes = [w[n].shape for n in upd_names]
    res = _adamw(_pack([w[n] for n in upd_names]), [_pack([g_small[n] for n in upd_names])],
                 _pack([mom[n] for n in upd_names]), _pack([var[n] for n in upd_names]), PACK_ROWS, "adamw_small")
    for k, t in enumerate(res):
        for n, val in zip(upd_names, _unpack(t, upd_shapes)):
            out.setdefault(n, [None] * 4)[k] = val
    shp = w["w_ada"].shape
    flat = lambda t: t.reshape(-1, shp[-1])
    out["w_ada"] = [t.reshape(shp) for t in _adamw(flat(w["w_ada"]), [flat(g_w_ada)], flat(mom["w_ada"]),
                                                  flat(var["w_ada"]), 128, "adamw_ada")]
    return (total["loss"].reshape(()), dx[None]) + tuple(out[n][k] for k in range(4) for n in names)
```
